```python
import jax, jax.numpy as jnp
from jax import lax
import numpy as np

D_MODEL = 1024
BATCH = 8
SEQ = 8192
DEPTH = 1

RET_HEADS = 4
RET_QK_DIM = 128
RET_V_DIM = 256
RET_CHUNK = 128
ATTN_Q_HEADS = 16
ATTN_KV_HEADS = 2
ATTN_HEAD_DIM = 64
WINDOW = 128
ATTN_BLOCK = 128
D_FF = -(-8 * D_MODEL // (3 * 256)) * 256
ROPE_THETA = 10000.0
EPS = 1e-6

RET_QK = RET_HEADS * RET_QK_DIM
RET_V = RET_HEADS * RET_V_DIM
ATTN_Q = ATTN_Q_HEADS * ATTN_HEAD_DIM
ATTN_KV = ATTN_KV_HEADS * ATTN_HEAD_DIM
SPLITS = [RET_QK, RET_QK, RET_V, RET_V, ATTN_Q, ATTN_KV, ATTN_KV, D_MODEL, D_MODEL]
D_IN = sum(SPLITS)
SPLIT_IDX = [int(v) for v in np.cumsum(SPLITS)[:-1]]

kernel_name = "hybrid_retention_swa_sink_gated_block"


def rms_norm(x, g):
    xf = x.astype(jnp.float32)
    y = xf * lax.rsqrt(jnp.mean(xf * xf, axis=-1, keepdims=True) + EPS)
    return (y * g.astype(jnp.float32)).astype(x.dtype)


def rotary(x, pos):
    d = x.shape[-1]
    half = d // 2
    inv_freq = ROPE_THETA ** (-jnp.arange(half, dtype=jnp.float32) / half)
    ang = pos.astype(jnp.float32)[:, None] * inv_freq[None, :]
    cos = jnp.cos(ang)[None, :, None, :]
    sin = jnp.sin(ang)[None, :, None, :]
    xf = x.astype(jnp.float32)
    x1, x2 = xf[..., :half], xf[..., half:]
    out = jnp.concatenate([x1 * cos - x2 * sin, x2 * cos + x1 * sin], axis=-1)
    return out.astype(x.dtype)


def retention_chunkwise(q, k, v):
    B, S, H, dk = q.shape
    dv = v.shape[-1]
    C = RET_CHUNK
    N = S // C
    log_gamma = jnp.log1p(-jnp.exp2(-5.0 - jnp.arange(H, dtype=jnp.float32)))
    idx = jnp.arange(C, dtype=jnp.float32)
    rel = idx[:, None] - idx[None, :]
    intra_decay = jnp.where(rel[None] >= 0,
                            jnp.exp(log_gamma[:, None, None] * jnp.maximum(rel, 0.0)[None]), 0.0)
    q_decay = jnp.exp(log_gamma[:, None] * (idx + 1.0))[None, :, :, None]
    k_decay = jnp.exp(log_gamma[:, None] * (C - 1.0 - idx))[None, :, :, None]
    chunk_decay = jnp.exp(log_gamma * C)[None, :, None, None]

    qf = q.astype(jnp.float32) * (dk ** -0.5)
    kf = k.astype(jnp.float32)
    vf = v.astype(jnp.float32)
    to_chunks = lambda t: t.reshape(B, N, C, H, t.shape[-1]).transpose(1, 0, 3, 2, 4)
    qc, kc, vc = to_chunks(qf), to_chunks(kf), to_chunks(vf)

    def step(state, inp):
        qn, kn, vn = inp
        scores = jnp.einsum('bhcd,bhsd->bhcs', qn, kn) * intra_decay
        inner = jnp.einsum('bhcs,bhse->bhce', scores, vn)
        cross = jnp.einsum('bhcd,bhde->bhce', qn, state) * q_decay
        new_state = state * chunk_decay + jnp.einsum('bhsd,bhse->bhde', kn * k_decay, vn)
        return new_state, inner + cross

    state0 = jnp.zeros((B, H, dk, dv), jnp.float32)
    _, out = lax.scan(step, state0, (qc, kc, vc))
    return out.transpose(1, 0, 3, 2, 4).reshape(B, S, H, dv)


def head_group_norm(y, g):
    B, S, H, dv = y.shape
    mu = jnp.mean(y, axis=-1, keepdims=True)
    yc = y - mu
    var = jnp.mean(yc * yc, axis=-1, keepdims=True)
    yn = (yc * lax.rsqrt(var + EPS)).reshape(B, S, H * dv)
    return yn * g.astype(jnp.float32)


def sliding_window_sink_attention(q, k, v, sinks):
    B, S, Hq, d = q.shape
    Hkv = k.shape[2]
    G = Hq // Hkv
    C = ATTN_BLOCK
    N = S // C
    qb = q.reshape(B, N, C, Hkv, G, d)
    pad = ((0, 0), (C, 0), (0, 0), (0, 0))
    kp = jnp.pad(k, pad).reshape(B, N + 1, C, Hkv, d)
    vp = jnp.pad(v, pad).reshape(B, N + 1, C, Hkv, d)
    kb = jnp.concatenate([kp[:, :-1], kp[:, 1:]], axis=2)
    vb = jnp.concatenate([vp[:, :-1], vp[:, 1:]], axis=2)
    scores = jnp.einsum('bnqhgd,bnkhd->bnhgqk', qb, kb).astype(jnp.float32) * (d ** -0.5)
    qi = jnp.arange(C)[:, None]
    kj = jnp.arange(2 * C)[None, :]
    rel = C + qi - kj
    key_pos = jnp.arange(N)[:, None, None] * C + kj[None] - C
    mask = (rel[None] >= 0) & (rel[None] < WINDOW) & (key_pos >= 0)
    scores = jnp.where(mask[None, :, None, None], scores, -1e30)
    sink = sinks.astype(jnp.float32).reshape(1, 1, Hkv, G, 1, 1)
    m = jnp.maximum(jnp.max(scores, axis=-1, keepdims=True), sink)
    e = jnp.exp(scores - m)
    probs = e / (jnp.sum(e, axis=-1, keepdims=True) + jnp.exp(sink - m))
    out = jnp.einsum('bnhgqk,bnkhd->bnqhgd', probs.astype(v.dtype), vb)
    return out.reshape(B, S, Hq, d)


def _fwd_setup_inputs(seed: int = 0) -> dict:
    key = jax.random.key(seed)
    ks = jax.random.split(key, 16)
    nrm = lambda k, shape, fan_in: jax.random.normal(k, shape, jnp.float32) * (fan_in ** -0.5)
    gain = lambda k, shape: 1.0 + 0.02 * jax.random.normal(k, shape, jnp.float32)
    return {
        "x": jax.random.normal(ks[0], (BATCH, SEQ, D_MODEL), jnp.float32),
        "ln1_g": gain(ks[1], (DEPTH, D_MODEL)),
        "w_in": nrm(ks[2], (DEPTH, D_MODEL, D_IN), D_MODEL),
        "b_in": 0.02 * jax.random.normal(ks[3], (DEPTH, D_IN), jnp.float32),
        "ret_norm_g": gain(ks[4], (DEPTH, RET_V)),
        "w_ret_out": nrm(ks[5], (DEPTH, RET_V, D_MODEL), RET_V),
        "attn_sinks": 0.5 * jax.random.normal(ks[6], (DEPTH, ATTN_Q_HEADS), jnp.float32),
        "w_attn_out": nrm(ks[7], (DEPTH, ATTN_Q, D_MODEL), ATTN_Q),
        "w_out": nrm(ks[8], (DEPTH, D_MODEL, D_MODEL), D_MODEL),
        "ln2_g": gain(ks[9], (DEPTH, D_MODEL)),
        "w_ffn_gate": nrm(ks[10], (DEPTH, D_MODEL, D_FF), D_MODEL),
        "w_ffn_up": nrm(ks[11], (DEPTH, D_MODEL, D_FF), D_MODEL),
        "w_ffn_down": nrm(ks[12], (DEPTH, D_FF, D_MODEL), D_FF),
        "lnf_g": gain(ks[13], (D_MODEL,)),
    }


def _fwd_reference(x, ln1_g, w_in, b_in, ret_norm_g, w_ret_out, attn_sinks, w_attn_out, w_out,
              ln2_g, w_ffn_gate, w_ffn_up, w_ffn_down, lnf_g):
    B, S, _ = x.shape
    pos = jnp.arange(S, dtype=jnp.int32)
    for l in range(DEPTH):
        h = rms_norm(x, ln1_g[l])
        proj = h @ w_in[l] + b_in[l]
        rq, rk, rv, rg, aq, ak, av, gate_a, gate_b = jnp.split(proj, SPLIT_IDX, axis=-1)

        rq = rotary(rq.reshape(B, S, RET_HEADS, RET_QK_DIM), pos)
        rk = rotary(rk.reshape(B, S, RET_HEADS, RET_QK_DIM), pos)
        ry = retention_chunkwise(rq, rk, rv.reshape(B, S, RET_HEADS, RET_V_DIM))
        ry = head_group_norm(ry, ret_norm_g[l]).astype(x.dtype)
        branch_a = (jax.nn.silu(rg) * ry) @ w_ret_out[l]

        aq = rotary(aq.reshape(B, S, ATTN_Q_HEADS, ATTN_HEAD_DIM), pos)
        ak = rotary(ak.reshape(B, S, ATTN_KV_HEADS, ATTN_HEAD_DIM), pos)
        ay = sliding_window_sink_attention(aq, ak, av.reshape(B, S, ATTN_KV_HEADS, ATTN_HEAD_DIM),
                                           attn_sinks[l])
        branch_b = ay.reshape(B, S, ATTN_Q) @ w_attn_out[l]

        merged = jax.nn.sigmoid(gate_a) * branch_a + jax.nn.sigmoid(gate_b) * branch_b
        x = x + merged @ w_out[l]

        h2 = rms_norm(x, ln2_g[l])
        x = x + (jax.nn.silu(h2 @ w_ffn_gate[l]) * (h2 @ w_ffn_up[l])) @ w_ffn_down[l]
    return rms_norm(x, lnf_g)


import jax as _jax
import jax.numpy as _jnp

TWIN_FORMAT = 'train_step'
FWD_PARAMS = ['x', 'ln1_g', 'w_in', 'b_in', 'ret_norm_g', 'w_ret_out', 'attn_sinks', 'w_attn_out', 'w_out', 'ln2_g', 'w_ffn_gate', 'w_ffn_up', 'w_ffn_down', 'lnf_g']
TWIN_WEIGHTS = ['ln1_g', 'w_in', 'b_in', 'ret_norm_g', 'w_ret_out', 'attn_sinks', 'w_attn_out', 'w_out', 'ln2_g', 'w_ffn_gate', 'w_ffn_up', 'w_ffn_down', 'lnf_g']
TWIN_DIFF_INPUT = 'x'
TWIN_INPUTS = ['x', 'ln1_g', 'w_in', 'b_in', 'ret_norm_g', 'w_ret_out', 'attn_sinks', 'w_attn_out', 'w_out', 'ln2_g', 'w_ffn_gate', 'w_ffn_up', 'w_ffn_down', 'lnf_g', 'loss_target', 'm_ln1_g', 'm_w_in', 'm_b_in', 'm_ret_norm_g', 'm_w_ret_out', 'm_attn_sinks', 'm_w_attn_out', 'm_w_out', 'm_ln2_g', 'm_w_ffn_gate', 'm_w_ffn_up', 'm_w_ffn_down', 'm_lnf_g', 'v_ln1_g', 'v_w_in', 'v_b_in', 'v_ret_norm_g', 'v_w_ret_out', 'v_attn_sinks', 'v_w_attn_out', 'v_w_out', 'v_ln2_g', 'v_w_ffn_gate', 'v_w_ffn_up', 'v_w_ffn_down', 'v_lnf_g']
TWIN_OUTPUTS = ['loss', 'grad_x', 'grad_ln1_g', 'grad_w_in', 'grad_b_in', 'grad_ret_norm_g', 'grad_w_ret_out', 'grad_attn_sinks', 'grad_w_attn_out', 'grad_w_out', 'grad_ln2_g', 'grad_w_ffn_gate', 'grad_w_ffn_up', 'grad_w_ffn_down', 'grad_lnf_g', 'delta_ln1_g', 'delta_w_in', 'delta_b_in', 'delta_ret_norm_g', 'delta_w_ret_out', 'delta_attn_sinks', 'delta_w_attn_out', 'delta_w_out', 'delta_ln2_g', 'delta_w_ffn_gate', 'delta_w_ffn_up', 'delta_w_ffn_down', 'delta_lnf_g', 'new_m_ln1_g', 'new_m_w_in', 'new_m_b_in', 'new_m_ret_norm_g', 'new_m_w_ret_out', 'new_m_attn_sinks', 'new_m_w_attn_out', 'new_m_w_out', 'new_m_ln2_g', 'new_m_w_ffn_gate', 'new_m_w_ffn_up', 'new_m_w_ffn_down', 'new_m_lnf_g', 'new_v_ln1_g', 'new_v_w_in', 'new_v_b_in', 'new_v_ret_norm_g', 'new_v_w_ret_out', 'new_v_attn_sinks', 'new_v_w_attn_out', 'new_v_w_out', 'new_v_ln2_g', 'new_v_w_ffn_gate', 'new_v_w_ffn_up', 'new_v_w_ffn_down', 'new_v_lnf_g']
TWIN_LEAF_KINDS = {'loss': 'loss', 'grad_x': 'grad_x', 'grad_ln1_g': 'grad_w', 'grad_w_in': 'grad_w', 'grad_b_in': 'grad_w', 'grad_ret_norm_g': 'grad_w', 'grad_w_ret_out': 'grad_w', 'grad_attn_sinks': 'grad_w', 'grad_w_attn_out': 'grad_w', 'grad_w_out': 'grad_w', 'grad_ln2_g': 'grad_w', 'grad_w_ffn_gate': 'grad_w', 'grad_w_ffn_up': 'grad_w', 'grad_w_ffn_down': 'grad_w', 'grad_lnf_g': 'grad_w', 'delta_ln1_g': 'delta_w', 'delta_w_in': 'delta_w', 'delta_b_in': 'delta_w', 'delta_ret_norm_g': 'delta_w', 'delta_w_ret_out': 'delta_w', 'delta_attn_sinks': 'delta_w', 'delta_w_attn_out': 'delta_w', 'delta_w_out': 'delta_w', 'delta_ln2_g': 'delta_w', 'delta_w_ffn_gate': 'delta_w', 'delta_w_ffn_up': 'delta_w', 'delta_w_ffn_down': 'delta_w', 'delta_lnf_g': 'delta_w', 'new_m_ln1_g': 'new_m', 'new_m_w_in': 'new_m', 'new_m_b_in': 'new_m', 'new_m_ret_norm_g': 'new_m', 'new_m_w_ret_out': 'new_m', 'new_m_attn_sinks': 'new_m', 'new_m_w_attn_out': 'new_m', 'new_m_w_out': 'new_m', 'new_m_ln2_g': 'new_m', 'new_m_w_ffn_gate': 'new_m', 'new_m_w_ffn_up': 'new_m', 'new_m_w_ffn_down': 'new_m', 'new_m_lnf_g': 'new_m', 'new_v_ln1_g': 'new_v', 'new_v_w_in': 'new_v', 'new_v_b_in': 'new_v', 'new_v_ret_norm_g': 'new_v', 'new_v_w_ret_out': 'new_v', 'new_v_attn_sinks': 'new_v', 'new_v_w_attn_out': 'new_v', 'new_v_w_out': 'new_v', 'new_v_ln2_g': 'new_v', 'new_v_w_ffn_gate': 'new_v', 'new_v_w_ffn_up': 'new_v', 'new_v_w_ffn_down': 'new_v', 'new_v_lnf_g': 'new_v'}


def _forward(args):
    return _fwd_reference(*[args[k] for k in FWD_PARAMS])


def _output_shape():
    def fwd():
        inp = _fwd_setup_inputs(0)
        return _fwd_reference(*[inp[k] for k in FWD_PARAMS])
    out = _jax.eval_shape(fwd)
    return out.shape, out.dtype

N_MICROBATCH = 1
ADAM_LR = 0.001
ADAM_B1 = 0.9
ADAM_B2 = 0.999
ADAM_EPS = 1e-08
ADAM_WD = 0.01
ADAM_STEP = 10
PER_EXAMPLE_BATCH_AXIS = {'x': 0, 'loss_target': 0}
SHARED_INPUTS = []
_WEIGHT_DTYPES = {'ln1_g': _jnp.float32, 'w_in': _jnp.float32, 'b_in': _jnp.float32, 'ret_norm_g': _jnp.float32, 'w_ret_out': _jnp.float32, 'attn_sinks': _jnp.float32, 'w_attn_out': _jnp.float32, 'w_out': _jnp.float32, 'ln2_g': _jnp.float32, 'w_ffn_gate': _jnp.float32, 'w_ffn_up': _jnp.float32, 'w_ffn_down': _jnp.float32, 'lnf_g': _jnp.float32}
MOMENT_SCALE = {'ln1_g': 1.958495e-01, 'w_in': 7.368138e-02, 'b_in': 9.119064e-02, 'ret_norm_g': 8.925343e-02, 'w_ret_out': 8.601460e-02, 'attn_sinks': 2.231383e-02, 'w_attn_out': 2.908379e-02, 'w_out': 9.031282e-02, 'ln2_g': 1.815520e-01, 'w_ffn_gate': 7.675091e-02, 'w_ffn_up': 7.440169e-02, 'w_ffn_down': 1.235107e-01, 'lnf_g': 6.401020e+01}


def _to_microbatches(a, axis):
    t = _jnp.moveaxis(a, axis, 0)
    t = t.reshape((N_MICROBATCH, t.shape[0] // N_MICROBATCH) + t.shape[1:])
    return _jnp.moveaxis(t, 1, axis + 1)


def setup_inputs(seed: int = 0) -> dict:
    inp = _fwd_setup_inputs(seed)
    key = _jax.random.fold_in(_jax.random.key(seed), 7919)
    shape, _ = _output_shape()
    out = dict(inp)
    out["loss_target"] = _jax.random.normal(_jax.random.fold_in(key, 0), shape, _jnp.float32)
    for i, name in enumerate(TWIN_WEIGHTS):
        w = inp[name].astype(_jnp.float32)
        if MOMENT_SCALE is None:
            s = _jnp.sqrt(_jnp.mean(_jnp.square(w)) + 1e-30)
        else:
            s = MOMENT_SCALE[name]
        km, kv = _jax.random.split(_jax.random.fold_in(key, i + 1))
        out[name] = w
        out["m_" + name] = s * _jax.random.normal(km, w.shape, _jnp.float32)
        out["v_" + name] = (s * s) * _jax.random.uniform(kv, w.shape, _jnp.float32, 0.5, 1.5)
    if N_MICROBATCH > 1:
        for name, axis in PER_EXAMPLE_BATCH_AXIS.items():
            out[name] = _to_microbatches(out[name], axis)
    return {'x': out['x'], 'ln1_g': out['ln1_g'], 'w_in': out['w_in'], 'b_in': out['b_in'], 'ret_norm_g': out['ret_norm_g'], 'w_ret_out': out['w_ret_out'], 'attn_sinks': out['attn_sinks'], 'w_attn_out': out['w_attn_out'], 'w_out': out['w_out'], 'ln2_g': out['ln2_g'], 'w_ffn_gate': out['w_ffn_gate'], 'w_ffn_up': out['w_ffn_up'], 'w_ffn_down': out['w_ffn_down'], 'lnf_g': out['lnf_g'], 'loss_target': out['loss_target'], 'm_ln1_g': out['m_ln1_g'], 'm_w_in': out['m_w_in'], 'm_b_in': out['m_b_in'], 'm_ret_norm_g': out['m_ret_norm_g'], 'm_w_ret_out': out['m_w_ret_out'], 'm_attn_sinks': out['m_attn_sinks'], 'm_w_attn_out': out['m_w_attn_out'], 'm_w_out': out['m_w_out'], 'm_ln2_g': out['m_ln2_g'], 'm_w_ffn_gate': out['m_w_ffn_gate'], 'm_w_ffn_up': out['m_w_ffn_up'], 'm_w_ffn_down': out['m_w_ffn_down'], 'm_lnf_g': out['m_lnf_g'], 'v_ln1_g': out['v_ln1_g'], 'v_w_in': out['v_w_in'], 'v_b_in': out['v_b_in'], 'v_ret_norm_g': out['v_ret_norm_g'], 'v_w_ret_out': out['v_w_ret_out'], 'v_attn_sinks': out['v_attn_sinks'], 'v_w_attn_out': out['v_w_attn_out'], 'v_w_out': out['v_w_out'], 'v_ln2_g': out['v_ln2_g'], 'v_w_ffn_gate': out['v_w_ffn_gate'], 'v_w_ffn_up': out['v_w_ffn_up'], 'v_w_ffn_down': out['v_w_ffn_down'], 'v_lnf_g': out['v_lnf_g']}


def _loss(weights, diff, rest, loss_target):
    with _jax.named_scope("forward"):
        args = {**rest, TWIN_DIFF_INPUT: diff, **{k: w.astype(_WEIGHT_DTYPES[k]) for k, w in weights.items()}}
        y = _forward(args)
    with _jax.named_scope("loss_head"):
        err = _jnp.square(y.astype(_jnp.float32) - loss_target)
        return 0.5 * _jnp.sum(_jnp.mean(err, axis=-1)) if err.ndim else 0.5 * err


def _adamw(w, g, m, v):
    m = ADAM_B1 * m + (1.0 - ADAM_B1) * g
    v = ADAM_B2 * v + (1.0 - ADAM_B2) * _jnp.square(g)
    m_hat = m / (1.0 - ADAM_B1 ** ADAM_STEP)
    v_hat = v / (1.0 - ADAM_B2 ** ADAM_STEP)
    delta = -ADAM_LR * (m_hat / (_jnp.sqrt(v_hat) + ADAM_EPS) + ADAM_WD * w)
    return delta, m, v


def reference(x, ln1_g, w_in, b_in, ret_norm_g, w_ret_out, attn_sinks, w_attn_out, w_out, ln2_g, w_ffn_gate, w_ffn_up, w_ffn_down, lnf_g, loss_target, m_ln1_g, m_w_in, m_b_in, m_ret_norm_g, m_w_ret_out, m_attn_sinks, m_w_attn_out, m_w_out, m_ln2_g, m_w_ffn_gate, m_w_ffn_up, m_w_ffn_down, m_lnf_g, v_ln1_g, v_w_in, v_b_in, v_ret_norm_g, v_w_ret_out, v_attn_sinks, v_w_attn_out, v_w_out, v_ln2_g, v_w_ffn_gate, v_w_ffn_up, v_w_ffn_down, v_lnf_g):
    given = dict(x=x, ln1_g=ln1_g, w_in=w_in, b_in=b_in, ret_norm_g=ret_norm_g, w_ret_out=w_ret_out, attn_sinks=attn_sinks, w_attn_out=w_attn_out, w_out=w_out, ln2_g=ln2_g, w_ffn_gate=w_ffn_gate, w_ffn_up=w_ffn_up, w_ffn_down=w_ffn_down, lnf_g=lnf_g, loss_target=loss_target, m_ln1_g=m_ln1_g, m_w_in=m_w_in, m_b_in=m_b_in, m_ret_norm_g=m_ret_norm_g, m_w_ret_out=m_w_ret_out, m_attn_sinks=m_attn_sinks, m_w_attn_out=m_w_attn_out, m_w_out=m_w_out, m_ln2_g=m_ln2_g, m_w_ffn_gate=m_w_ffn_gate, m_w_ffn_up=m_w_ffn_up, m_w_ffn_down=m_w_ffn_down, m_lnf_g=m_lnf_g, v_ln1_g=v_ln1_g, v_w_in=v_w_in, v_b_in=v_b_in, v_ret_norm_g=v_ret_norm_g, v_w_ret_out=v_w_ret_out, v_attn_sinks=v_attn_sinks, v_w_attn_out=v_w_attn_out, v_w_out=v_w_out, v_ln2_g=v_ln2_g, v_w_ffn_gate=v_w_ffn_gate, v_w_ffn_up=v_w_ffn_up, v_w_ffn_down=v_w_ffn_down, v_lnf_g=v_lnf_g)
    weights = {n: given[n] for n in TWIN_WEIGHTS}
    shared = {n: given[n] for n in SHARED_INPUTS}
    per_example = {n: given[n] for n in ['x']}
    grad_fn = _jax.value_and_grad(_loss, argnums=(0, 1))

    def one_microbatch(ex, loss_target):
        ex = dict(ex)
        diff = ex.pop(TWIN_DIFF_INPUT)
        return grad_fn(weights, diff, {**shared, **ex}, loss_target)

    if N_MICROBATCH == 1:
        loss, (grad_w, grad_x) = one_microbatch(per_example, given["loss_target"])
    else:
        def body(carry, xs):
            loss_sum, grad_sum = carry
            l_k, (gw_k, gx_k) = one_microbatch(xs[0], xs[1])
            with _jax.named_scope("update"):
                return (loss_sum + l_k, _jax.tree.map(_jnp.add, grad_sum, gw_k)), gx_k

        init = (_jnp.zeros((), _jnp.float32), _jax.tree.map(_jnp.zeros_like, weights))
        (loss, grad_w), grad_x = _jax.lax.scan(body, init, (per_example, given["loss_target"]))
    with _jax.named_scope("update"):
        delta_w, new_m, new_v = {}, {}, {}
        for n in TWIN_WEIGHTS:
            delta_w[n], new_m[n], new_v[n] = _adamw(weights[n], grad_w[n], given["m_" + n], given["v_" + n])
    return (loss, grad_x, *[grad_w[n] for n in TWIN_WEIGHTS], *[delta_w[n] for n in TWIN_WEIGHTS],
            *[new_m[n] for n in TWIN_WEIGHTS], *[new_v[n] for n in TWIN_WEIGHTS])
```

```python
import functools

import jax
import jax.numpy as jnp
from jax import lax
from jax.experimental import pallas as pl
from jax.experimental.pallas import tpu as pltpu

D_MODEL = 1024
SEQ_CHUNK = 128
RET_HEADS = 4
RET_QK_DIM = 128
RET_V_DIM = 256
ATTN_Q_HEADS = 16
ATTN_HEAD_DIM = 64
D_FF = 2816
D_IN = 6400
ROPE_THETA = 10000.0
EPS = 1e-6
N_CHIPS = 4

ADAM_LR = 0.001
ADAM_B1 = 0.9
ADAM_B2 = 0.999
ADAM_EPS = 1e-08
ADAM_WD = 0.01
ADAM_STEP = 10

OFF_RQ, OFF_RK, OFF_RV, OFF_RG = 0, 512, 1024, 2048
OFF_AQ, OFF_AK, OFF_AV, OFF_GA, OFF_GB = 3072, 4096, 4224, 4352, 5376

V7X_VMEM_LIMIT = 56 * 1024 * 1024
LANES = 128

BF = jnp.bfloat16
F32 = jnp.float32
MESH = pl.DeviceIdType.MESH

_VMEM = pl.BlockSpec(memory_space=pltpu.VMEM)
_SMEM = pl.BlockSpec(memory_space=pltpu.SMEM)
_ANY = pl.BlockSpec(memory_space=pl.ANY)


def _dot(a, b):
    return jnp.dot(a, b, preferred_element_type=F32)


def _dot_nt(a, b):
    return lax.dot_general(a, b, (((1,), (1,)), ((), ())), preferred_element_type=F32)


def _dot_tn(a, b):
    return lax.dot_general(a, b, (((0,), (0,)), ((), ())), preferred_element_type=F32)


def _sigmoid(x):
    return 1.0 / (1.0 + jnp.exp(-x))


def _rows(tm, n):
    return pl.BlockSpec((tm, n), lambda i: (i, 0))


def _fixed(shape):
    nd = len(shape)
    return pl.BlockSpec(shape, lambda i: (0,) * nd)


def _params(sem, **kw):
    return pltpu.CompilerParams(dimension_semantics=sem, vmem_limit_bytes=V7X_VMEM_LIMIT, **kw)


def _rot_ret(p, cr, sr):
    return p * cr + pltpu.roll(p, 64, 1) * sr


def _rot_ret_bwd(d, cr, sr):
    return d * cr - pltpu.roll(d, 64, 1) * sr


def _attn_partner(p):
    lane = lax.broadcasted_iota(jnp.int32, p.shape, 1)
    first = (lane % 64) < 32
    return jnp.where(first, pltpu.roll(p, 96, 1), pltpu.roll(p, 32, 1))


def _rot_attn(p, ca, sa):
    return p * ca + _attn_partner(p) * sa


def _rot_attn_bwd(d, ca, sa):
    return d * ca - _attn_partner(d) * sa


def _proj_fwd(x, g1, w_in, b_in, cr, sr, ca, sa):
    S = x.shape[0]
    tm = 512
    scale = RET_QK_DIM ** -0.5

    def body(x_ref, g_ref, w_ref, b_ref, cr_ref, sr_ref, ca_ref, sa_ref,
             h_ref, rq_ref, rk_ref, rv_ref, rg_ref, aq_ref, ak_ref, av_ref, ga_ref, gb_ref):
        xv = x_ref[...]
        r = lax.rsqrt(jnp.mean(xv * xv, axis=-1, keepdims=True) + EPS)
        h = (xv * r * g_ref[...]).astype(BF)
        h_ref[...] = h

        def piece(lo, n):
            return _dot(h, w_ref[:, lo:lo + n]) + b_ref[:, lo:lo + n]

        crv, srv = cr_ref[...], sr_ref[...]
        pq = piece(OFF_RQ, 512)
        pk = piece(OFF_RK, 512)
        for hd in range(RET_HEADS):
            sl = slice(hd * 128, (hd + 1) * 128)
            rq_ref[:, sl] = (_rot_ret(pq[:, sl], crv, srv) * scale).astype(BF)
            rk_ref[:, sl] = _rot_ret(pk[:, sl], crv, srv)
        rv_ref[...] = piece(OFF_RV, 1024).astype(BF)
        rg_ref[...] = piece(OFF_RG, 1024)
        cav, sav = ca_ref[...], sa_ref[...]
        pa = piece(OFF_AQ, 1024)
        for blk in range(8):
            sl = slice(blk * 128, (blk + 1) * 128)
            aq_ref[:, sl] = _rot_attn(pa[:, sl], cav, sav).astype(BF)
        pkv = piece(OFF_AK, 256)
        ak_ref[...] = _rot_attn(pkv[:, :128], cav, sav).astype(BF)
        av_ref[...] = pkv[:, 128:].astype(BF)
        ga_ref[...] = piece(OFF_GA, 1024)
        gb_ref[...] = piece(OFF_GB, 1024)

    outs = [(D_MODEL, BF), (512, BF), (512, F32), (1024, BF), (1024, F32),
            (1024, BF), (128, BF), (128, BF), (1024, F32), (1024, F32)]
    return pl.pallas_call(
        body, name="proj_fwd", grid=(S // tm,),
        in_specs=[_rows(tm, D_MODEL), _fixed((1, D_MODEL)), _VMEM, _VMEM,
                  _rows(tm, 128), _rows(tm, 128), _rows(tm, 128), _rows(tm, 128)],
        out_specs=[_rows(tm, n) for n, _ in outs],
        out_shape=[jax.ShapeDtypeStruct((S, n), dt) for n, dt in outs],
        compiler_params=_params(("parallel",)),
    )(x, g1, w_in, b_in, cr, sr, ca, sa)


def _ret_fwd(rq, rk, rv, dmat, qd, kd, cd):
    S = rq.shape[0]
    N = S // SEQ_CHUNK

    def body(q_ref, k_ref, v_ref, dm_ref, qd_ref, kd_ref, cd_ref, y_ref, st_ref, state):
        @pl.when(pl.program_id(0) == 0)
        def _():
            state[...] = jnp.zeros_like(state)

        for hd in range(RET_HEADS):
            qs = slice(hd * 128, (hd + 1) * 128)
            vs = slice(hd * 256, (hd + 1) * 256)
            q = q_ref[:, qs]
            kf = k_ref[:, qs]
            k = kf.astype(BF)
            v = v_ref[:, vs]
            s = state[hd]
            sb = s.astype(BF)
            st_ref[0, hd] = sb
            a = _dot_nt(q, k) * dm_ref[hd]
            inner = _dot(a.astype(BF), v)
            cross = _dot(q, sb) * qd_ref[hd]
            y_ref[:, vs] = inner + cross
            kdv = (kf * kd_ref[hd]).astype(BF)
            state[hd] = s * cd_ref[hd] + _dot_tn(kdv, v)

    return pl.pallas_call(
        body, name="ret_fwd", grid=(N,),
        in_specs=[_rows(128, 512), _rows(128, 512), _rows(128, 1024), _VMEM, _VMEM, _VMEM, _VMEM],
        out_specs=[_rows(128, 1024), pl.BlockSpec((1, RET_HEADS, 128, 256), lambda i: (i, 0, 0, 0))],
        out_shape=[jax.ShapeDtypeStruct((S, 1024), F32),
                   jax.ShapeDtypeStruct((N, RET_HEADS, 128, 256), BF)],
        scratch_shapes=[pltpu.VMEM((RET_HEADS, 128, 256), F32)],
        compiler_params=_params(("arbitrary",)),
    )(rq, rk, rv, dmat, qd, kd, cd)


def _kv_halves(prev_ref, cur_ref):
    kk = jnp.concatenate([prev_ref[...], cur_ref[...]], axis=0).astype(F32)
    return kk, pltpu.roll(kk, 64, 1)


def _attn_mask(n):
    qi = lax.broadcasted_iota(jnp.int32, (128, 256), 0)
    kj = lax.broadcasted_iota(jnp.int32, (128, 256), 1)
    first_prev = jnp.where(n > 0, 0, 128)
    prev_ok = jnp.logical_and(jnp.logical_and(kj < 128, kj >= first_prev), qi < kj)
    cur_ok = jnp.logical_and(kj >= 128, qi >= kj - 128)
    return jnp.logical_or(prev_ok, cur_ok)


def _attn_probs(qp, kx, allowed, sink):
    s = _dot_nt(qp, kx) * (ATTN_HEAD_DIM ** -0.5)
    s = jnp.where(allowed, s, -1e30)
    m = jnp.maximum(jnp.max(s, axis=-1, keepdims=True), sink)
    e = jnp.exp(s - m)
    es = jnp.exp(sink - m)
    den = jnp.sum(e, axis=-1, keepdims=True) + es
    return e / den, es / den


def _attn_fwd(aq, ak, av, sinks):
    S = aq.shape[0]
    N = S // SEQ_CHUNK

    def body(sk_ref, q_ref, kp_ref, kc_ref, vp_ref, vc_ref, o_ref):
        n = pl.program_id(0)
        kk, kk_sw = _kv_halves(kp_ref, kc_ref)
        vv, vv_sw = _kv_halves(vp_ref, vc_ref)
        lo = lax.broadcasted_iota(jnp.int32, (256, 128), 1) < 64
        allowed = _attn_mask(n)
        for g in range(2):
            src_lo, src_hi = (kk, kk_sw) if g == 0 else (kk_sw, kk)
            k_lo = jnp.where(lo, src_lo, 0.0).astype(BF)
            k_hi = jnp.where(lo, 0.0, src_hi).astype(BF)
            src_lo, src_hi = (vv, vv_sw) if g == 0 else (vv_sw, vv)
            v_lo = jnp.where(lo, src_lo, 0.0).astype(BF)
            v_hi = jnp.where(lo, 0.0, src_hi).astype(BF)
            for p in range(4):
                sl = slice((4 * g + p) * 128, (4 * g + p + 1) * 128)
                qp = q_ref[:, sl]
                head = 8 * g + 2 * p
                p0, _ = _attn_probs(qp, k_lo, allowed, sk_ref[0, head])
                p1, _ = _attn_probs(qp, k_hi, allowed, sk_ref[0, head + 1])
                o = _dot(p0.astype(BF), v_lo) + _dot(p1.astype(BF), v_hi)
                o_ref[:, sl] = o.astype(BF)

    prev = lambda i: (jnp.maximum(i - 1, 0), 0)
    return pl.pallas_call(
        body, name="attn_fwd", grid=(N,),
        in_specs=[_SMEM, _rows(128, 1024), pl.BlockSpec((128, 128), prev), _rows(128, 128),
                  pl.BlockSpec((128, 128), prev), _rows(128, 128)],
        out_specs=_rows(128, 1024),
        out_shape=jax.ShapeDtypeStruct((S, 1024), BF),
        compiler_params=_params(("parallel",)),
    )(sinks, aq, ak, ak, av, av)


def _group_norm(y):
    mu = jnp.mean(y, axis=-1, keepdims=True)
    yc = y - mu
    rstd = lax.rsqrt(jnp.mean(yc * yc, axis=-1, keepdims=True) + EPS)
    return yc * rstd, rstd


def _merge_fwd(x, ry, rg, ay, ga, gb, g_ret, w_ret, w_attn, w_out):
    S = x.shape[0]
    tm = 256

    def body(x_ref, ry_ref, rg_ref, ay_ref, ga_ref, gb_ref, gr_ref, wr_ref, wa_ref, wo_ref,
             u_ref, ba_ref, bb_ref, mg_ref, x1_ref):
        for hd in range(RET_HEADS):
            vs = slice(hd * 256, (hd + 1) * 256)
            yn, _ = _group_norm(ry_ref[:, vs])
            rgv = rg_ref[:, vs]
            u_ref[:, vs] = (rgv * _sigmoid(rgv) * (yn * gr_ref[:, vs])).astype(BF)
        ba = _dot(u_ref[...], wr_ref[...])
        bb = _dot(ay_ref[...], wa_ref[...])
        ba_ref[...] = ba
        bb_ref[...] = bb
        merged = (_sigmoid(ga_ref[...]) * ba + _sigmoid(gb_ref[...]) * bb).astype(BF)
        mg_ref[...] = merged
        x1_ref[...] = x_ref[...] + _dot(merged, wo_ref[...])

    outs = [BF, F32, F32, BF, F32]
    return pl.pallas_call(
        body, name="merge_fwd", grid=(S // tm,),
        in_specs=[_rows(tm, 1024)] * 6 + [_fixed((1, 1024)), _VMEM, _VMEM, _VMEM],
        out_specs=[_rows(tm, 1024)] * 5,
        out_shape=[jax.ShapeDtypeStruct((S, 1024), dt) for dt in outs],
        compiler_params=_params(("parallel",)),
    )(x, ry, rg, ay, ga, gb, g_ret, w_ret, w_attn, w_out)


def _ffn_fwd(x1, tgt, g2, g3, w_gate, w_up, w_down):
    S = x1.shape[0]
    tm = 256

    def body(x1_ref, t_ref, g2_ref, g3_ref, wg_ref, wu_ref, wd_ref,
             gate_ref, up_ref, h2_ref, act_ref, dx2_ref, loss_ref, dg3_ref):
        @pl.when(pl.program_id(0) == 0)
        def _():
            loss_ref[...] = jnp.zeros_like(loss_ref)
            dg3_ref[...] = jnp.zeros_like(dg3_ref)

        x1v = x1_ref[...]
        r2 = lax.rsqrt(jnp.mean(x1v * x1v, axis=-1, keepdims=True) + EPS)
        h2 = (x1v * r2 * g2_ref[...]).astype(BF)
        h2_ref[...] = h2
        gate = _dot(h2, wg_ref[...])
        up = _dot(h2, wu_ref[...])
        gate_ref[...] = gate
        up_ref[...] = up
        act = (gate * _sigmoid(gate) * up).astype(BF)
        act_ref[...] = act
        x2 = x1v + _dot(act, wd_ref[...])
        r3 = lax.rsqrt(jnp.mean(x2 * x2, axis=-1, keepdims=True) + EPS)
        xh = x2 * r3
        g3v = g3_ref[...]
        err = xh * g3v - t_ref[...]
        loss_ref[...] += 0.5 * jnp.sum(jnp.mean(err * err, axis=-1, keepdims=True), axis=0, keepdims=True)
        dy = err * (1.0 / D_MODEL)
        dg3_ref[...] += jnp.sum(dy * xh, axis=0, keepdims=True)
        dxh = dy * g3v
        dx2_ref[...] = r3 * (dxh - xh * jnp.mean(dxh * xh, axis=-1, keepdims=True))

    return pl.pallas_call(
        body, name="ffn_fwd", grid=(S // tm,),
        in_specs=[_rows(tm, 1024), _rows(tm, 1024), _fixed((1, 1024)), _fixed((1, 1024)), _VMEM, _VMEM, _VMEM],
        out_specs=[_rows(tm, D_FF), _rows(tm, D_FF), _rows(tm, 1024), _rows(tm, D_FF), _rows(tm, 1024),
                   _fixed((1, LANES)), _fixed((1, 1024))],
        out_shape=[jax.ShapeDtypeStruct((S, D_FF), F32), jax.ShapeDtypeStruct((S, D_FF), F32),
                   jax.ShapeDtypeStruct((S, 1024), BF), jax.ShapeDtypeStruct((S, D_FF), BF),
                   jax.ShapeDtypeStruct((S, 1024), F32),
                   jax.ShapeDtypeStruct((1, LANES), F32), jax.ShapeDtypeStruct((1, 1024), F32)],
        compiler_params=_params(("arbitrary",)),
    )(x1, tgt, g2, g3, w_gate, w_up, w_down)


def _ffn_bwd(dx2, gate, up, x1, g2, w_gate, w_up, w_down):
    S = x1.shape[0]
    tm = 256

    def body(dx2_ref, gate_ref, up_ref, x1_ref, g2_ref, wg_ref, wu_ref, wd_ref,
             dgate_ref, dup_ref, dx1_ref, dg2_ref):
        @pl.when(pl.program_id(0) == 0)
        def _():
            dg2_ref[...] = jnp.zeros_like(dg2_ref)

        dx2v = dx2_ref[...]
        dact = _dot_nt(dx2v.astype(BF), wd_ref[...])
        gate = gate_ref[...]
        sg = _sigmoid(gate)
        dgate = (dact * up_ref[...] * (sg * (1.0 + gate * (1.0 - sg)))).astype(BF)
        dup = (dact * (gate * sg)).astype(BF)
        dgate_ref[...] = dgate
        dup_ref[...] = dup
        dh2 = _dot_nt(dgate, wg_ref[...]) + _dot_nt(dup, wu_ref[...])
        x1v = x1_ref[...]
        r2 = lax.rsqrt(jnp.mean(x1v * x1v, axis=-1, keepdims=True) + EPS)
        xh = x1v * r2
        dg2_ref[...] += jnp.sum(dh2 * xh, axis=0, keepdims=True)
        dxh = dh2 * g2_ref[...]
        dx1_ref[...] = dx2v + r2 * (dxh - xh * jnp.mean(dxh * xh, axis=-1, keepdims=True))

    return pl.pallas_call(
        body, name="ffn_bwd", grid=(S // tm,),
        in_specs=[_rows(tm, 1024), _rows(tm, D_FF), _rows(tm, D_FF), _rows(tm, 1024), _fixed((1, 1024)),
                  _VMEM, _VMEM, _VMEM],
        out_specs=[_rows(tm, D_FF), _rows(tm, D_FF), _rows(tm, 1024), _fixed((1, 1024))],
        out_shape=[jax.ShapeDtypeStruct((S, D_FF), BF), jax.ShapeDtypeStruct((S, D_FF), BF),
                   jax.ShapeDtypeStruct((S, 1024), F32), jax.ShapeDtypeStruct((1, 1024), F32)],
        compiler_params=_params(("arbitrary",)),
    )(dx2, gate, up, x1, g2, w_gate, w_up, w_down)


def _merge_bwd(dx1, ga, gb, ba, bb, rg, ry, g_ret, w_ret, w_attn, w_out):
    S = dx1.shape[0]
    tm = 256

    def body(dx1_ref, ga_ref, gb_ref, ba_ref, bb_ref, rg_ref, ry_ref, gr_ref, wr_ref, wa_ref, wo_ref,
             dgates_ref, drg_ref, dba_ref, dbb_ref, day_ref, dry_ref, dgr_ref, db_ref):
        @pl.when(pl.program_id(0) == 0)
        def _():
            dgr_ref[...] = jnp.zeros_like(dgr_ref)
            db_ref[...] = jnp.zeros_like(db_ref)

        dm = _dot_nt(dx1_ref[...].astype(BF), wo_ref[...])
        sa = _sigmoid(ga_ref[...])
        sb = _sigmoid(gb_ref[...])
        dga = (dm * ba_ref[...] * (sa * (1.0 - sa))).astype(BF)
        dgb = (dm * bb_ref[...] * (sb * (1.0 - sb))).astype(BF)
        dgates_ref[:, :1024] = dga
        dgates_ref[:, 1024:] = dgb
        dba = (dm * sa).astype(BF)
        dbb = (dm * sb).astype(BF)
        dba_ref[...] = dba
        dbb_ref[...] = dbb
        day_ref[...] = _dot_nt(dbb, wa_ref[...]).astype(BF)
        du = _dot_nt(dba, wr_ref[...])
        for hd in range(RET_HEADS):
            vs = slice(hd * 256, (hd + 1) * 256)
            yn, rstd = _group_norm(ry_ref[:, vs])
            rgv = rg_ref[:, vs]
            sr = _sigmoid(rgv)
            silu = rgv * sr
            duv = du[:, vs]
            grv = gr_ref[:, vs]
            drg = (duv * (yn * grv) * (sr * (1.0 + rgv * (1.0 - sr)))).astype(BF)
            drg_ref[:, vs] = drg
            dyg = duv * silu
            dgr_ref[:, vs] += jnp.sum(dyg * yn, axis=0, keepdims=True)
            dyn = dyg * grv
            dry_ref[:, vs] = rstd * (dyn - jnp.mean(dyn, axis=-1, keepdims=True)
                                     - yn * jnp.mean(dyn * yn, axis=-1, keepdims=True))
            db_ref[:, vs] += jnp.sum(drg.astype(F32), axis=0, keepdims=True)
        db_ref[:, 1024:2048] += jnp.sum(dga.astype(F32), axis=0, keepdims=True)
        db_ref[:, 2048:] += jnp.sum(dgb.astype(F32), axis=0, keepdims=True)

    return pl.pallas_call(
        body, name="merge_bwd", grid=(S // tm,),
        in_specs=[_rows(tm, 1024)] * 7 + [_fixed((1, 1024)), _VMEM, _VMEM, _VMEM],
        out_specs=[_rows(tm, 2048), _rows(tm, 1024), _rows(tm, 1024), _rows(tm, 1024), _rows(tm, 1024),
                   _rows(tm, 1024), _fixed((1, 1024)), _fixed((1, 3072))],
        out_shape=[jax.ShapeDtypeStruct((S, 2048), BF), jax.ShapeDtypeStruct((S, 1024), BF),
                   jax.ShapeDtypeStruct((S, 1024), BF), jax.ShapeDtypeStruct((S, 1024), BF),
                   jax.ShapeDtypeStruct((S, 1024), BF), jax.ShapeDtypeStruct((S, 1024), F32),
                   jax.ShapeDtypeStruct((1, 1024), F32), jax.ShapeDtypeStruct((1, 3072), F32)],
        compiler_params=_params(("arbitrary",)),
    )(dx1, ga, gb, ba, bb, rg, ry, g_ret, w_ret, w_attn, w_out)


def _attn_bwd(aq, ak, av, day, sinks, ca, sa):
    S = aq.shape[0]
    N = S // SEQ_CHUNK

    def body(sk_ref, q_ref, kp_ref, kc_ref, vp_ref, vc_ref, do_ref, cq_ref, sq_ref, cp_ref, sp_ref,
             dq_ref, dk_ref, dv_ref, dsk_ref, carry_k, carry_v):
        n = pl.program_id(0)

        @pl.when(n == 0)
        def _():
            carry_k[...] = jnp.zeros_like(carry_k)
            carry_v[...] = jnp.zeros_like(carry_v)
            dsk_ref[...] = jnp.zeros_like(dsk_ref)

        @pl.when(n < N)
        def _():
            kk, kk_sw = _kv_halves(kp_ref, kc_ref)
            vv, vv_sw = _kv_halves(vp_ref, vc_ref)
            lo = lax.broadcasted_iota(jnp.int32, (256, 128), 1) < 64
            allowed = _attn_mask(n)
            cq, sq = cq_ref[...], sq_ref[...]
            dk_groups, dv_groups = [], []
            for g in range(2):
                src_lo, src_hi = (kk, kk_sw) if g == 0 else (kk_sw, kk)
                k_lo = jnp.where(lo, src_lo, 0.0).astype(BF)
                k_hi = jnp.where(lo, 0.0, src_hi).astype(BF)
                src_lo, src_hi = (vv, vv_sw) if g == 0 else (vv_sw, vv)
                v_lo = jnp.where(lo, src_lo, 0.0).astype(BF)
                v_hi = jnp.where(lo, 0.0, src_hi).astype(BF)
                dk_acc = [jnp.zeros((256, 128), F32), jnp.zeros((256, 128), F32)]
                dv_acc = [jnp.zeros((256, 128), F32), jnp.zeros((256, 128), F32)]
                for p in range(4):
                    sl = slice((4 * g + p) * 128, (4 * g + p + 1) * 128)
                    qp = q_ref[:, sl]
                    dop = do_ref[:, sl]
                    dq = jnp.zeros((128, 128), F32)
                    for which, (kx, vx) in enumerate(((k_lo, v_lo), (k_hi, v_hi))):
                        head = 8 * g + 2 * p + which
                        pr, ps = _attn_probs(qp, kx, allowed, sk_ref[0, head])
                        dp = _dot_nt(dop, vx)
                        dr = jnp.sum(pr * dp, axis=-1, keepdims=True)
                        ds = (pr * (dp - dr) * (ATTN_HEAD_DIM ** -0.5)).astype(BF)
                        dsk_ref[head:head + 1, :] += jnp.sum(-ps * dr, axis=0, keepdims=True)
                        dq = dq + _dot(ds, kx)
                        dk_acc[which] = dk_acc[which] + _dot_tn(ds, qp)
                        dv_acc[which] = dv_acc[which] + _dot_tn(pr.astype(BF), dop)
                    dq_ref[:, sl] = _rot_attn_bwd(dq, cq, sq).astype(BF)
                tk = jnp.where(lo, dk_acc[0], dk_acc[1])
                tv = jnp.where(lo, dv_acc[0], dv_acc[1])
                dk_groups.append(tk + pltpu.roll(tk, 64, 1))
                dv_groups.append(tv + pltpu.roll(tv, 64, 1))
            dk_full = jnp.where(lo, dk_groups[0], dk_groups[1])
            dv_full = jnp.where(lo, dv_groups[0], dv_groups[1])
            dk_ref[...] = _rot_attn_bwd(carry_k[...] + dk_full[:128], cp_ref[...], sp_ref[...]).astype(BF)
            dv_ref[...] = (carry_v[...] + dv_full[:128]).astype(BF)
            carry_k[...] = dk_full[128:]
            carry_v[...] = dv_full[128:]

        @pl.when(n == N)
        def _():
            dk_ref[...] = _rot_attn_bwd(carry_k[...], cp_ref[...], sp_ref[...]).astype(BF)
            dv_ref[...] = carry_v[...].astype(BF)

    cur = lambda i: (jnp.minimum(i, N - 1), 0)
    prev = lambda i: (jnp.maximum(i - 1, 0), 0)
    blk = lambda w, im: pl.BlockSpec((128, w), im)
    return pl.pallas_call(
        body, name="attn_bwd", grid=(N + 1,),
        in_specs=[_SMEM, blk(1024, cur), blk(128, prev), blk(128, cur), blk(128, prev), blk(128, cur),
                  blk(1024, cur), blk(128, cur), blk(128, cur), blk(128, prev), blk(128, prev)],
        out_specs=[blk(1024, cur), blk(128, prev), blk(128, prev), _fixed((ATTN_Q_HEADS, LANES))],
        out_shape=[jax.ShapeDtypeStruct((S, 1024), BF), jax.ShapeDtypeStruct((S, 128), BF),
                   jax.ShapeDtypeStruct((S, 128), BF), jax.ShapeDtypeStruct((ATTN_Q_HEADS, LANES), F32)],
        scratch_shapes=[pltpu.VMEM((128, 128), F32), pltpu.VMEM((128, 128), F32)],
        compiler_params=_params(("arbitrary",)),
    )(sinks, aq, ak, ak, av, av, day, ca, sa, ca, sa)


def _ret_bwd(rq, rk, rv, dry, states, dmat, qd, kd, cd, cr, sr):
    S = rq.shape[0]
    N = S // SEQ_CHUNK
    scale = RET_QK_DIM ** -0.5

    def body(q_ref, k_ref, v_ref, dy_ref, st_ref, dm_ref, qd_ref, kd_ref, cd_ref, cr_ref, sr_ref,
             out_ref, dstate):
        @pl.when(pl.program_id(0) == 0)
        def _():
            dstate[...] = jnp.zeros_like(dstate)

        crv, srv = cr_ref[...], sr_ref[...]
        for hd in range(RET_HEADS):
            qs = slice(hd * 128, (hd + 1) * 128)
            vs = slice(hd * 256, (hd + 1) * 256)
            q = q_ref[:, qs]
            kf = k_ref[:, qs]
            k = kf.astype(BF)
            kdv = (kf * kd_ref[hd]).astype(BF)
            v = v_ref[:, vs]
            do = dy_ref[:, vs]
            dob = do.astype(BF)
            doq = (do * qd_ref[hd]).astype(BF)
            sn = st_ref[0, hd]
            ds = dstate[hd]
            dsb = ds.astype(BF)
            dmv = dm_ref[hd]
            a = (_dot_nt(q, k) * dmv).astype(BF)
            dv = _dot_tn(a, dob) + _dot(kdv, dsb)
            da = (_dot_nt(dob, v) * dmv).astype(BF)
            dq = _dot(da, k) + _dot_nt(doq, sn)
            dk = _dot_tn(da, q) + _dot_nt(v, dsb) * kd_ref[hd]
            dstate[hd] = ds * cd_ref[hd] + _dot_tn(q, doq)
            out_ref[:, qs] = _rot_ret_bwd(dq * scale, crv, srv).astype(BF)
            out_ref[:, 512 + hd * 128:512 + (hd + 1) * 128] = _rot_ret_bwd(dk, crv, srv).astype(BF)
            out_ref[:, 1024 + hd * 256:1024 + (hd + 1) * 256] = dv.astype(BF)

    rev = lambda w: pl.BlockSpec((128, w), lambda i: (N - 1 - i, 0))
    return pl.pallas_call(
        body, name="ret_bwd", grid=(N,),
        in_specs=[rev(512), rev(512), rev(1024), rev(1024),
                  pl.BlockSpec((1, RET_HEADS, 128, 256), lambda i: (N - 1 - i, 0, 0, 0)),
                  _VMEM, _VMEM, _VMEM, _VMEM, rev(128), rev(128)],
        out_specs=rev(2048),
        out_shape=jax.ShapeDtypeStruct((S, 2048), BF),
        scratch_shapes=[pltpu.VMEM((RET_HEADS, 128, 256), F32)],
        compiler_params=_params(("arbitrary",)),
    )(rq, rk, rv, dry, states, dmat, qd, kd, cd, cr, sr)


def _proj_bwd(d_ret, d_rg, d_att, d_gates, x, dx1, g1, w_in):
    S = x.shape[0]
    tm = 256
    pieces = ((0, 2048), (2048, 1024), (3072, 1280), (4352, 2048))

    def body(a_ref, b_ref, c_ref, d_ref, x_ref, dx1_ref, g_ref, w_ref, gx_ref, dg_ref, db_ref):
        @pl.when(pl.program_id(0) == 0)
        def _():
            dg_ref[...] = jnp.zeros_like(dg_ref)
            db_ref[...] = jnp.zeros_like(db_ref)

        dh = jnp.zeros((tm, D_MODEL), F32)
        for ref, (lo, n) in zip((a_ref, b_ref, c_ref, d_ref), pieces):
            v = ref[...]
            dh = dh + _dot_nt(v, w_ref[:, lo:lo + n])
            db_ref[:, lo:lo + n] += jnp.sum(v.astype(F32), axis=0, keepdims=True)
        xv = x_ref[...]
        r = lax.rsqrt(jnp.mean(xv * xv, axis=-1, keepdims=True) + EPS)
        xh = xv * r
        dg_ref[...] += jnp.sum(dh * xh, axis=0, keepdims=True)
        dxh = dh * g_ref[...]
        gx_ref[...] = dx1_ref[...] + r * (dxh - xh * jnp.mean(dxh * xh, axis=-1, keepdims=True))

    return pl.pallas_call(
        body, name="proj_bwd", grid=(S // tm,),
        in_specs=[_rows(tm, 2048), _rows(tm, 1024), _rows(tm, 1280), _rows(tm, 2048),
                  _rows(tm, 1024), _rows(tm, 1024), _fixed((1, 1024)), _VMEM],
        out_specs=[_rows(tm, 1024), _fixed((1, 1024)), _fixed((1, D_IN))],
        out_shape=[jax.ShapeDtypeStruct((S, 1024), F32), jax.ShapeDtypeStruct((1, 1024), F32),
                   jax.ShapeDtypeStruct((1, D_IN), F32)],
        compiler_params=_params(("arbitrary",)),
    )(d_ret, d_rg, d_att, d_gates, x, dx1, g1, w_in)


def _wgrad(a, b, name):
    S, M = a.shape
    N = b.shape[1]
    ts = 512

    def body(a_ref, b_ref, o_ref):
        @pl.when(pl.program_id(0) == 0)
        def _():
            o_ref[...] = jnp.zeros_like(o_ref)

        o_ref[...] += _dot_tn(a_ref[...].astype(BF), b_ref[...].astype(BF))

    return pl.pallas_call(
        body, name=name, grid=(S // ts,),
        in_specs=[_rows(ts, M), _rows(ts, N)],
        out_specs=_fixed((M, N)),
        out_shape=jax.ShapeDtypeStruct((M, N), F32),
        compiler_params=_params(("arbitrary",)),
    )(a, b)


def _add_n(terms, out_dtype, name):
    R, W = terms[0].shape
    tr = R // 8 if R % 64 == 0 else R
    while tr > 640 and tr % 16 == 0:
        tr //= 2

    def body(*refs):
        acc = refs[0][...].astype(F32)
        for r in refs[1:-1]:
            acc = acc + r[...].astype(F32)
        refs[-1][...] = acc.astype(out_dtype)

    return pl.pallas_call(
        body, name=name, grid=(R // tr,),
        in_specs=[_rows(tr, W)] * len(terms),
        out_specs=_rows(tr, W),
        out_shape=jax.ShapeDtypeStruct((R, W), out_dtype),
        compiler_params=_params(("parallel",)),
    )(*terms)


def _adamw(w, g, m, v, name):
    R, W = w.shape
    tr = R
    while tr * W > 256 * 1024 and tr % 16 == 0:
        tr //= 2

    def body(w_ref, g_ref, m_ref, v_ref, d_ref, mo_ref, vo_ref):
        gv = g_ref[...]
        mn = ADAM_B1 * m_ref[...] + (1.0 - ADAM_B1) * gv
        vn = ADAM_B2 * v_ref[...] + (1.0 - ADAM_B2) * (gv * gv)
        m_hat = mn / (1.0 - ADAM_B1 ** ADAM_STEP)
        v_hat = vn / (1.0 - ADAM_B2 ** ADAM_STEP)
        d_ref[...] = -ADAM_LR * (m_hat / (jnp.sqrt(v_hat) + ADAM_EPS) + ADAM_WD * w_ref[...])
        mo_ref[...] = mn
        vo_ref[...] = vn

    return pl.pallas_call(
        body, name=name, grid=(R // tr,),
        in_specs=[_rows(tr, W)] * 4,
        out_specs=[_rows(tr, W)] * 3,
        out_shape=[jax.ShapeDtypeStruct((R, W), F32)] * 3,
        compiler_params=_params(("parallel",)),
    )(w, g, m, v)


def _place():
    x, y, c = lax.axis_index("x"), lax.axis_index("y"), lax.axis_index("c")
    chips = [(1 - x, y), (x, 1 - y), (1 - x, 1 - y)]
    return x, y, c, chips


def _gather_weights(wb):
    R, W = wb.shape
    H = R // 2

    def body(w_ref, out_ref, send_sems, recv_sems, local_sem):
        x, y, c, chips = _place()
        j = 2 * x + y
        sibling = (x, y, 1 - c)

        def rows(pj, half):
            return out_ref.at[pl.ds((2 * pj + half) * H, H), :]

        def copy(k, pj, half, to, src=None):
            return pltpu.make_async_remote_copy(
                src_ref=rows(pj, half) if src is None else src, dst_ref=rows(pj, half),
                send_sem=send_sems.at[k], recv_sem=recv_sems.at[k], device_id=to, device_id_type=MESH)

        mine = pltpu.make_async_copy(w_ref, out_ref.at[pl.ds(j * R, R), :], local_sem)
        mine.start()
        first = [copy(k, j, c, (*chip, c), src=w_ref.at[pl.ds(c * H, H), :]) for k, chip in enumerate(chips)]
        for cp in first:
            cp.start()
        theirs = [2 * cx + cy for cx, cy in chips]
        passed = [copy(3 + k, theirs[k], c, sibling) for k in range(3)]
        for k in range(3):
            copy(k, theirs[k], c, (x, y, c)).wait_recv()
            passed[k].start()
        for k in range(3):
            copy(3 + k, theirs[k], 1 - c, (x, y, c)).wait_recv()
        for cp in first + passed:
            cp.wait_send()
        mine.wait()

    return pl.pallas_call(
        body, name="gather_weights",
        in_specs=[_ANY], out_specs=_ANY,
        out_shape=jax.ShapeDtypeStruct((N_CHIPS * R, W), wb.dtype),
        scratch_shapes=[pltpu.SemaphoreType.DMA((6,)), pltpu.SemaphoreType.DMA((6,)), pltpu.SemaphoreType.DMA],
        compiler_params=pltpu.CompilerParams(has_side_effects=True),
    )(wb)


def _to_sibling(a, name):
    def body(a_ref, out_ref, send_sem, recv_sem):
        x, y, c, _ = _place()
        cp = pltpu.make_async_remote_copy(src_ref=a_ref, dst_ref=out_ref, send_sem=send_sem, recv_sem=recv_sem,
                                          device_id=(x, y, 1 - c), device_id_type=MESH)
        cp.start()
        cp.wait()

    return pl.pallas_call(
        body, name=name, in_specs=[_ANY], out_specs=_ANY,
        out_shape=jax.ShapeDtypeStruct(a.shape, a.dtype),
        scratch_shapes=[pltpu.SemaphoreType.DMA, pltpu.SemaphoreType.DMA],
        compiler_params=pltpu.CompilerParams(has_side_effects=True),
    )(a)


def _to_chips(a):
    H = a.shape[0] // 3

    def body(a_ref, out_ref, send_sems, recv_sems):
        x, y, c, chips = _place()
        cps = [pltpu.make_async_remote_copy(
            src_ref=a_ref.at[pl.ds(k * H, H), :], dst_ref=out_ref.at[pl.ds(k * H, H), :],
            send_sem=send_sems.at[k], recv_sem=recv_sems.at[k], device_id=(*chip, c), device_id_type=MESH)
            for k, chip in enumerate(chips)]
        for cp in cps:
            cp.start()
        for cp in cps:
            cp.wait()

    return pl.pallas_call(
        body, name="grads_to_chips", in_specs=[_ANY], out_specs=_ANY,
        out_shape=jax.ShapeDtypeStruct(a.shape, a.dtype),
        scratch_shapes=[pltpu.SemaphoreType.DMA((3,)), pltpu.SemaphoreType.DMA((3,))],
        compiler_params=pltpu.CompilerParams(has_side_effects=True),
    )(a)


def _share_halves(t):
    H, W = t.shape

    def body(t_ref, out_ref, send_sem, recv_sem, local_sem):
        x, y, c, _ = _place()
        mine = pltpu.make_async_copy(t_ref, out_ref.at[pl.ds(c * H, H), :], local_sem)
        mine.start()
        cp = pltpu.make_async_remote_copy(
            src_ref=t_ref, dst_ref=out_ref.at[pl.ds(c * H, H), :], send_sem=send_sem, recv_sem=recv_sem,
            device_id=(x, y, 1 - c), device_id_type=MESH)
        cp.start()
        cp.wait_send()
        pltpu.make_async_remote_copy(
            src_ref=t_ref, dst_ref=out_ref.at[pl.ds((1 - c) * H, H), :], send_sem=send_sem, recv_sem=recv_sem,
            device_id=(x, y, 1 - c), device_id_type=MESH).wait_recv()
        mine.wait()

    return pl.pallas_call(
        body, name="share_halves", in_specs=[_ANY], out_specs=_ANY,
        out_shape=jax.ShapeDtypeStruct((2 * H, W), t.dtype),
        scratch_shapes=[pltpu.SemaphoreType.DMA, pltpu.SemaphoreType.DMA, pltpu.SemaphoreType.DMA],
        compiler_params=pltpu.CompilerParams(has_side_effects=True),
    )(t)


def _all_reduce_small(v):
    m_per, n = v.shape
    n_dev = 8

    def body(x_ref, out_ref, all_ref, send_sems, recv_sems, local_sem):
        x, y, c, chips = _place()
        me, sibling = (x, y, c), (x, y, 1 - c)

        def rows(px, py, pc):
            return all_ref.at[pl.ds((4 * px + 2 * py + pc) * m_per, m_per), :]

        def copy(k, block, to, src=None):
            return pltpu.make_async_remote_copy(
                src_ref=rows(*block) if src is None else src, dst_ref=rows(*block),
                send_sem=send_sems.at[k], recv_sem=recv_sems.at[k], device_id=to, device_id_type=MESH)

        mine = pltpu.make_async_copy(x_ref, rows(*me), local_sem)
        mine.start()
        first = [copy(0, me, sibling, src=x_ref)]
        first += [copy(1 + k, me, (*chip, c), src=x_ref) for k, chip in enumerate(chips)]
        for cp in first:
            cp.start()
        passed = [copy(4 + k, (*chip, c), sibling) for k, chip in enumerate(chips)]
        for k, chip in enumerate(chips):
            copy(1 + k, (*chip, c), me).wait_recv()
            passed[k].start()
        copy(0, sibling, me).wait_recv()
        for k, chip in enumerate(chips):
            copy(4 + k, (*chip, 1 - c), me).wait_recv()
        for cp in first + passed:
            cp.wait_send()
        mine.wait()
        acc = all_ref[pl.ds(0, m_per), :]
        for d in range(1, n_dev):
            acc = acc + all_ref[pl.ds(d * m_per, m_per), :]
        out_ref[...] = acc

    return pl.pallas_call(
        body, name="all_reduce_small",
        in_specs=[_VMEM], out_specs=_VMEM,
        out_shape=jax.ShapeDtypeStruct((m_per, n), F32),
        scratch_shapes=[pltpu.VMEM((n_dev * m_per, n), F32),
                        pltpu.SemaphoreType.DMA((7,)), pltpu.SemaphoreType.DMA((7,)), pltpu.SemaphoreType.DMA],
        compiler_params=pltpu.CompilerParams(has_side_effects=True),
    )(v)


SHARD_ROWS = 1600 + 3 * 256 + 3 * 704


def _pack_shard(w_in, w_ret, w_attn, w_out, w_gate, w_up, w_down):
    return jnp.concatenate([w_in.reshape(1600, 1024), w_ret, w_attn, w_out,
                            w_gate.reshape(704, 1024), w_up.reshape(704, 1024), w_down], axis=0)


def _unpack_shard(p):
    o = 0
    out = []
    for rows, shape in ((1600, (1024, 1600)), (256, (256, 1024)), (256, (256, 1024)), (256, (256, 1024)),
                        (704, (1024, 704)), (704, (1024, 704)), (704, (704, 1024))):
        out.append(p[o:o + rows].reshape(shape))
        o += rows
    return out


def _unpack_full(wall):
    def cols(lo, rows, n):
        return wall[:, lo:lo + rows].reshape(N_CHIPS, 1024, n).transpose(1, 0, 2).reshape(1024, N_CHIPS * n)

    def rws(lo, rows):
        return wall[:, lo:lo + rows].reshape(N_CHIPS * rows, 1024)

    return (cols(0, 1600, 1600), rws(1600, 256), rws(1856, 256), rws(2112, 256),
            cols(2368, 704, 704), cols(3072, 704, 704), rws(3776, 704))


def _pack_full(g_in, g_ret, g_attn, g_out, g_gate, g_up, g_down):
    def cols(g, n):
        return g.reshape(1024, N_CHIPS, n).transpose(1, 0, 2).reshape(N_CHIPS, n, 1024)

    def rws(g, rows):
        return g.reshape(N_CHIPS, rows, 1024)

    return jnp.concatenate([cols(g_in, 1600), rws(g_ret, 256), rws(g_attn, 256), rws(g_out, 256),
                            cols(g_gate, 704), cols(g_up, 704), rws(g_down, 704)], axis=1)


def _tables(S):
    pos = jnp.arange(S, dtype=jnp.int32).astype(F32)

    def cs(half):
        inv_freq = ROPE_THETA ** (-jnp.arange(half, dtype=F32) / half)
        ang = pos[:, None] * inv_freq[None, :]
        return jnp.cos(ang), jnp.sin(ang)

    c64, s64 = cs(64)
    c32, s32 = cs(32)
    cr = jnp.concatenate([c64, c64], axis=1)
    sr = jnp.concatenate([-s64, s64], axis=1)
    ca = jnp.concatenate([c32, c32, c32, c32], axis=1)
    sa = jnp.concatenate([-s32, s32, -s32, s32], axis=1)
    return cr, sr, ca, sa


def _decays():
    H, C = RET_HEADS, SEQ_CHUNK
    log_gamma = jnp.log1p(-jnp.exp2(-5.0 - jnp.arange(H, dtype=F32)))
    idx = jnp.arange(C, dtype=F32)
    rel = idx[:, None] - idx[None, :]
    dmat = jnp.where(rel[None] >= 0, jnp.exp(log_gamma[:, None, None] * jnp.maximum(rel, 0.0)[None]), 0.0)
    qd = jnp.exp(log_gamma[:, None] * (idx + 1.0))[:, :, None]
    kd = jnp.exp(log_gamma[:, None] * (C - 1.0 - idx))[:, :, None]
    cd = jnp.exp(log_gamma * C)[:, None, None]
    return dmat, qd, kd, cd


def _local_step(x, tgt, ln1_g, b_in, ret_norm_g, attn_sinks, ln2_g, lnf_g, full_w):
    w_in, w_ret, w_attn, w_out, w_gate, w_up, w_down = full_w
    S = x.shape[0]
    cr, sr, ca, sa = _tables(S)
    dmat, qd, kd, cd = _decays()

    h, rq, rk, rv, rg, aq, ak, av, ga, gb = _proj_fwd(x, ln1_g, w_in, b_in, cr, sr, ca, sa)
    ry, states = _ret_fwd(rq, rk, rv, dmat, qd, kd, cd)
    ay = _attn_fwd(aq, ak, av, attn_sinks)
    u, ba, bb, merged, x1 = _merge_fwd(x, ry, rg, ay, ga, gb, ret_norm_g, w_ret, w_attn, w_out)
    gate, up, h2, act, dx2, loss, dg3 = _ffn_fwd(x1, tgt, ln2_g, lnf_g, w_gate, w_up, w_down)

    dgate, dup, dx1, dg2 = _ffn_bwd(dx2, gate, up, x1, ln2_g, w_gate, w_up, w_down)
    g_down = _wgrad(act, dx2, "wgrad_down")
    g_gate = _wgrad(h2, dgate, "wgrad_gate")
    g_up = _wgrad(h2, dup, "wgrad_up")
    dgates, drg, dba, dbb, day, dry, dgr, db_mid = _merge_bwd(dx1, ga, gb, ba, bb, rg, ry, ret_norm_g,
                                                                w_ret, w_attn, w_out)
    g_out = _wgrad(merged, dx1, "wgrad_out")
    g_ret = _wgrad(u, dba, "wgrad_ret")
    g_attn = _wgrad(ay, dbb, "wgrad_attn")
    daq, dak, dav, dsk = _attn_bwd(aq, ak, av, day, attn_sinks, ca, sa)
    d_att = jnp.concatenate([daq, dak, dav], axis=1)
    d_ret = _ret_bwd(rq, rk, rv, dry, states, dmat, qd, kd, cd, cr, sr)
    grad_x, dg1, db = _proj_bwd(d_ret, drg, d_att, dgates, x, dx1, ln1_g, w_in)
    g_in = jnp.concatenate([_wgrad(h, d_ret, "wgrad_in_ret"), _wgrad(h, drg, "wgrad_in_rg"),
                            _wgrad(h, d_att, "wgrad_in_att"), _wgrad(h, dgates, "wgrad_in_gates")], axis=1)
    small = dict(loss=loss[0, 0], ln1_g=dg1, b_in=db, ret_norm_g=dgr, attn_sinks=dsk[:, 0][None, :],
                 ln2_g=dg2, lnf_g=dg3)
    return grad_x, (g_in, g_ret, g_attn, g_out, g_gate, g_up, g_down), small


SMALL_ROWS = 16


def _pack_small(ln1, ret, ln2, lnf, b_in, sinks, loss):
    b = jnp.pad(b_in.reshape(-1), (0, 7 * 1024 - D_IN)).reshape(7, 1024)
    s = jnp.pad(sinks.reshape(-1), (0, 1024 - ATTN_Q_HEADS)).reshape(1, 1024)
    l = jnp.pad(loss.reshape(1), (0, 1023)).reshape(1, 1024)
    z = jnp.zeros((SMALL_ROWS - 13, 1024), F32)
    return jnp.concatenate([ln1.reshape(1, 1024), ret.reshape(1, 1024), ln2.reshape(1, 1024),
                            lnf.reshape(1, 1024), b, s, l, z], axis=0)


def _unpack_small(p):
    return dict(ln1_g=p[0:1], ret_norm_g=p[1:2], ln2_g=p[2:3], lnf_g=p[3], b_in=p[4:11].reshape(1, -1)[:, :D_IN],
                attn_sinks=p[11:12, :ATTN_Q_HEADS], loss=p[12, 0])


def kernel(x, ln1_g, w_in, b_in, ret_norm_g, w_ret_out, attn_sinks, w_attn_out, w_out, ln2_g, w_ffn_gate, w_ffn_up, w_ffn_down, lnf_g, loss_target, m_ln1_g, m_w_in, m_b_in, m_ret_norm_g, m_w_ret_out, m_attn_sinks, m_w_attn_out, m_w_out, m_ln2_g, m_w_ffn_gate, m_w_ffn_up, m_w_ffn_down, m_lnf_g, v_ln1_g, v_w_in, v_b_in, v_ret_norm_g, v_w_ret_out, v_attn_sinks, v_w_attn_out, v_w_out, v_ln2_g, v_w_ffn_gate, v_w_ffn_up, v_w_ffn_down, v_lnf_g):
    xi = lax.axis_index("x")
    yi = lax.axis_index("y")
    ci = lax.axis_index("c")
    j = 2 * xi + yi

    big_w = [w_in[0], w_ret_out[0], w_attn_out[0], w_out[0], w_ffn_gate[0], w_ffn_up[0], w_ffn_down[0]]
    big_m = [m_w_in[0], m_w_ret_out[0], m_w_attn_out[0], m_w_out[0], m_w_ffn_gate[0], m_w_ffn_up[0], m_w_ffn_down[0]]
    big_v = [v_w_in[0], v_w_ret_out[0], v_w_attn_out[0], v_w_out[0], v_w_ffn_gate[0], v_w_ffn_up[0], v_w_ffn_down[0]]

    wb = _pack_shard(*[w.astype(BF) for w in big_w])
    wall = _gather_weights(wb).reshape(N_CHIPS, SHARD_ROWS, 1024)
    full_w = _unpack_full(wall)

    grad_x, full_g, small = _local_step(x[0], loss_target[0], ln1_g, b_in, ret_norm_g, attn_sinks, ln2_g,
                                        lnf_g.reshape(1, -1), full_w)

    H = SHARD_ROWS // 2
    g4 = _pack_full(*full_g).reshape(N_CHIPS, 2, H, 1024)
    order = jnp.stack([j, j ^ 2, j ^ 1, j ^ 3])
    g4 = jnp.take(g4, order, axis=0)
    keep = lax.dynamic_index_in_dim(g4, ci, axis=1, keepdims=False).reshape(N_CHIPS * H, 1024)
    give = lax.dynamic_index_in_dim(g4, 1 - ci, axis=1, keepdims=False).astype(BF).reshape(N_CHIPS * H, 1024)
    got = _to_sibling(give, "pair_exchange")
    own_pair = _add_n([keep[:H], got[:H]], F32, "pair_sum_own")
    out_pair = _add_n([keep[H:], got[H:]], BF, "pair_sum_out")
    others = _to_chips(out_pair)
    half = _add_n([own_pair, others[:H], others[H:2 * H], others[2 * H:]], F32, "chip_sum")
    g_shard = _unpack_shard(_share_halves(half))

    sm = _unpack_small(_all_reduce_small(_pack_small(
        small["ln1_g"], small["ret_norm_g"], small["ln2_g"], small["lnf_g"], small["b_in"],
        small["attn_sinks"], small["loss"])))

    names = ["w_in", "w_ret_out", "w_attn_out", "w_out", "w_ffn_gate", "w_ffn_up", "w_ffn_down"]
    upd = {}
    for n, w, g, m, v in zip(names, big_w, g_shard, big_m, big_v):
        d, mn, vn = _adamw(w, g, m, v, "adamw_" + n)
        upd[n] = (g[None], d[None], mn[None], vn[None])

    z = jnp.zeros((), F32)
    pw = _pack_small(ln1_g, ret_norm_g, ln2_g, lnf_g, b_in, attn_sinks, z)
    pm = _pack_small(m_ln1_g, m_ret_norm_g, m_ln2_g, m_lnf_g, m_b_in, m_attn_sinks, z)
    pv = _pack_small(v_ln1_g, v_ret_norm_g, v_ln2_g, v_lnf_g, v_b_in, v_attn_sinks, z)
    pg = _pack_small(sm["ln1_g"], sm["ret_norm_g"], sm["ln2_g"], sm["lnf_g"], sm["b_in"], sm["attn_sinks"], z)
    sd, smn, svn = (_unpack_small(t) for t in _adamw(pw, pg, pm, pv, "adamw_small"))
    for n in ["ln1_g", "b_in", "ret_norm_g", "attn_sinks", "ln2_g", "lnf_g"]:
        upd[n] = (sm[n], sd[n], smn[n], svn[n])

    order_w = ["ln1_g", "w_in", "b_in", "ret_norm_g", "w_ret_out", "attn_sinks", "w_attn_out", "w_out",
               "ln2_g", "w_ffn_gate", "w_ffn_up", "w_ffn_down", "lnf_g"]
    outs = [sm["loss"], grad_x[None]]
    for k in range(4):
        outs += [upd[n][k] for n in order_w]
    return tuple(outs)
```

```python
import functools

import jax
import jax.numpy as jnp
from jax import lax
from jax.experimental import pallas as pl
from jax.experimental.pallas import tpu as pltpu

D_MODEL = 1024
SEQ_CHUNK = 128
RET_HEADS = 4
RET_QK_DIM = 128
RET_V_DIM = 256
ATTN_Q_HEADS = 16
ATTN_HEAD_DIM = 64
D_FF = 2816
D_IN = 6400
ROPE_THETA = 10000.0
EPS = 1e-6
N_CHIPS = 4

ADAM_LR = 0.001
ADAM_B1 = 0.9
ADAM_B2 = 0.999
ADAM_EPS = 1e-08
ADAM_WD = 0.01
ADAM_STEP = 10

OFF_RQ, OFF_RK, OFF_RV, OFF_RG = 0, 512, 1024, 2048
OFF_AQ, OFF_AK, OFF_AV, OFF_GA, OFF_GB = 3072, 4096, 4224, 4352, 5376

V7X_VMEM_LIMIT = 56 * 1024 * 1024
LANES = 128

BF = jnp.bfloat16
F32 = jnp.float32
MESH = pl.DeviceIdType.MESH

_VMEM = pl.BlockSpec(memory_space=pltpu.VMEM)
_SMEM = pl.BlockSpec(memory_space=pltpu.SMEM)
_ANY = pl.BlockSpec(memory_space=pl.ANY)


def _dot(a, b):
    return jnp.dot(a, b, preferred_element_type=F32)


def _dot_nt(a, b):
    return lax.dot_general(a, b, (((1,), (1,)), ((), ())), preferred_element_type=F32)


def _dot_tn(a, b):
    return lax.dot_general(a, b, (((0,), (0,)), ((), ())), preferred_element_type=F32)


def _sigmoid(x):
    return 1.0 / (1.0 + jnp.exp(-x))


def _rows(tm, n):
    return pl.BlockSpec((tm, n), lambda i: (i, 0))


def _fixed(shape):
    nd = len(shape)
    return pl.BlockSpec(shape, lambda i: (0,) * nd)


def _params(sem, **kw):
    return pltpu.CompilerParams(dimension_semantics=sem, vmem_limit_bytes=V7X_VMEM_LIMIT, **kw)


def _rot_ret(p, cr, sr):
    return p * cr + pltpu.roll(p, 64, 1) * sr


def _rot_ret_bwd(d, cr, sr):
    return d * cr - pltpu.roll(d, 64, 1) * sr


def _attn_partner(p):
    lane = lax.broadcasted_iota(jnp.int32, p.shape, 1)
    first = (lane % 64) < 32
    return jnp.where(first, pltpu.roll(p, 96, 1), pltpu.roll(p, 32, 1))


def _rot_attn(p, ca, sa):
    return p * ca + _attn_partner(p) * sa


def _rot_attn_bwd(d, ca, sa):
    return d * ca - _attn_partner(d) * sa


def _proj_fwd(x, g1, w_in, b_in, cr, sr, ca, sa):
    S = x.shape[0]
    tm = 512
    scale = RET_QK_DIM ** -0.5

    def body(x_ref, g_ref, w_ref, b_ref, cr_ref, sr_ref, ca_ref, sa_ref,
             h_ref, rq_ref, rk_ref, rv_ref, rg_ref, aq_ref, ak_ref, av_ref, ga_ref, gb_ref):
        xv = x_ref[...]
        r = lax.rsqrt(jnp.mean(xv * xv, axis=-1, keepdims=True) + EPS)
        h = (xv * r * g_ref[...]).astype(BF)
        h_ref[...] = h

        def piece(lo, n):
            return _dot(h, w_ref[:, lo:lo + n]) + b_ref[:, lo:lo + n]

        crv, srv = cr_ref[...], sr_ref[...]
        pq = piece(OFF_RQ, 512)
        pk = piece(OFF_RK, 512)
        for hd in range(RET_HEADS):
            sl = slice(hd * 128, (hd + 1) * 128)
            rq_ref[:, sl] = (_rot_ret(pq[:, sl], crv, srv) * scale).astype(BF)
            rk_ref[:, sl] = _rot_ret(pk[:, sl], crv, srv)
        rv_ref[...] = piece(OFF_RV, 1024).astype(BF)
        rg_ref[...] = piece(OFF_RG, 1024)
        cav, sav = ca_ref[...], sa_ref[...]
        pa = piece(OFF_AQ, 1024)
        for blk in range(8):
            sl = slice(blk * 128, (blk + 1) * 128)
            aq_ref[:, sl] = _rot_attn(pa[:, sl], cav, sav).astype(BF)
        pkv = piece(OFF_AK, 256)
        ak_ref[...] = _rot_attn(pkv[:, :128], cav, sav).astype(BF)
        av_ref[...] = pkv[:, 128:].astype(BF)
        ga_ref[...] = piece(OFF_GA, 1024)
        gb_ref[...] = piece(OFF_GB, 1024)

    outs = [(D_MODEL, BF), (512, BF), (512, F32), (1024, BF), (1024, F32),
            (1024, BF), (128, BF), (128, BF), (1024, F32), (1024, F32)]
    return pl.pallas_call(
        body, name="proj_fwd", grid=(S // tm,),
        in_specs=[_rows(tm, D_MODEL), _fixed((1, D_MODEL)), _VMEM, _VMEM,
                  _rows(tm, 128), _rows(tm, 128), _rows(tm, 128), _rows(tm, 128)],
        out_specs=[_rows(tm, n) for n, _ in outs],
        out_shape=[jax.ShapeDtypeStruct((S, n), dt) for n, dt in outs],
        compiler_params=_params(("parallel",)),
    )(x, g1, w_in, b_in, cr, sr, ca, sa)


def _ret_fwd(rq, rk, rv, dmat, qd, kd, cd):
    S = rq.shape[0]
    N = S // SEQ_CHUNK

    def body(q_ref, k_ref, v_ref, dm_ref, qd_ref, kd_ref, cd_ref, y_ref, st_ref, state):
        @pl.when(pl.program_id(0) == 0)
        def _():
            state[...] = jnp.zeros_like(state)

        for hd in range(RET_HEADS):
            qs = slice(hd * 128, (hd + 1) * 128)
            vs = slice(hd * 256, (hd + 1) * 256)
            q = q_ref[:, qs]
            kf = k_ref[:, qs]
            k = kf.astype(BF)
            v = v_ref[:, vs]
            s = state[hd]
            sb = s.astype(BF)
            st_ref[0, hd] = sb
            a = _dot_nt(q, k) * dm_ref[hd]
            inner = _dot(a.astype(BF), v)
            cross = _dot(q, sb) * qd_ref[hd]
            y_ref[:, vs] = inner + cross
            kdv = (kf * kd_ref[hd]).astype(BF)
            state[hd] = s * cd_ref[hd] + _dot_tn(kdv, v)

    return pl.pallas_call(
        body, name="ret_fwd", grid=(N,),
        in_specs=[_rows(128, 512), _rows(128, 512), _rows(128, 1024), _VMEM, _VMEM, _VMEM, _VMEM],
        out_specs=[_rows(128, 1024), pl.BlockSpec((1, RET_HEADS, 128, 256), lambda i: (i, 0, 0, 0))],
        out_shape=[jax.ShapeDtypeStruct((S, 1024), F32),
                   jax.ShapeDtypeStruct((N, RET_HEADS, 128, 256), BF)],
        scratch_shapes=[pltpu.VMEM((RET_HEADS, 128, 256), F32)],
        compiler_params=_params(("arbitrary",)),
    )(rq, rk, rv, dmat, qd, kd, cd)


def _kv_halves(prev_ref, cur_ref):
    kk = jnp.concatenate([prev_ref[...], cur_ref[...]], axis=0).astype(F32)
    return kk, pltpu.roll(kk, 64, 1)


PAIRS = 4


def _attn_mask(n):
    kj = lax.broadcasted_iota(jnp.int32, (256, PAIRS * 128), 0)
    qi = lax.broadcasted_iota(jnp.int32, (256, PAIRS * 128), 1) & 127
    first_prev = jnp.where(n > 0, 0, 128)
    prev_ok = jnp.logical_and(jnp.logical_and(kj < 128, kj >= first_prev), qi < kj)
    cur_ok = jnp.logical_and(kj >= 128, qi >= kj - 128)
    return jnp.logical_or(prev_ok, cur_ok)


def _group_operands(g, kk, kk_sw, vv, vv_sw):
    lo = lax.broadcasted_iota(jnp.int32, (256, 128), 1) < 64
    k_a, k_b = (kk, kk_sw) if g == 0 else (kk_sw, kk)
    v_a, v_b = (vv, vv_sw) if g == 0 else (vv_sw, vv)
    return (jnp.where(lo, k_a, 0.0).astype(BF), jnp.where(lo, 0.0, k_b).astype(BF),
            jnp.where(lo, v_a, 0.0).astype(BF), jnp.where(lo, 0.0, v_b).astype(BF))


def _stack_pairs(ref, g):
    return jnp.concatenate([ref[:, (4 * g + p) * 128:(4 * g + p + 1) * 128] for p in range(PAIRS)], axis=0)


def _sink_row(sk_ref, g, which):
    return jnp.concatenate([jnp.full((1, 128), sk_ref[0, 8 * g + 2 * p + which], F32) for p in range(PAIRS)], axis=1)


def _attn_probs(qs, kx, allowed, sink):
    s = _dot_nt(kx, qs) * (ATTN_HEAD_DIM ** -0.5)
    s = jnp.where(allowed, s, -1e30)
    m = jnp.maximum(jnp.max(s, axis=0, keepdims=True), sink)
    e = jnp.exp(s - m)
    es = jnp.exp(sink - m)
    inv = 1.0 / (jnp.sum(e, axis=0, keepdims=True) + es)
    return e * inv, es * inv


def _attn_fwd(aq, ak, av, sinks):
    S = aq.shape[0]
    N = S // SEQ_CHUNK

    def body(sk_ref, q_ref, kp_ref, kc_ref, vp_ref, vc_ref, o_ref):
        kk, kk_sw = _kv_halves(kp_ref, kc_ref)
        vv, vv_sw = _kv_halves(vp_ref, vc_ref)
        allowed = _attn_mask(pl.program_id(0))
        for g in range(2):
            k_lo, k_hi, v_lo, v_hi = _group_operands(g, kk, kk_sw, vv, vv_sw)
            qs = _stack_pairs(q_ref, g)
            p0, _ = _attn_probs(qs, k_lo, allowed, _sink_row(sk_ref, g, 0))
            p1, _ = _attn_probs(qs, k_hi, allowed, _sink_row(sk_ref, g, 1))
            o = _dot_tn(p0.astype(BF), v_lo) + _dot_tn(p1.astype(BF), v_hi)
            for p in range(PAIRS):
                o_ref[:, (4 * g + p) * 128:(4 * g + p + 1) * 128] = o[p * 128:(p + 1) * 128].astype(BF)

    prev = lambda i: (jnp.maximum(i - 1, 0), 0)
    return pl.pallas_call(
        body, name="attn_fwd", grid=(N,),
        in_specs=[_SMEM, _rows(128, 1024), pl.BlockSpec((128, 128), prev), _rows(128, 128),
                  pl.BlockSpec((128, 128), prev), _rows(128, 128)],
        out_specs=_rows(128, 1024),
        out_shape=jax.ShapeDtypeStruct((S, 1024), BF),
        compiler_params=_params(("parallel",)),
    )(sinks, aq, ak, ak, av, av)


def _group_norm(y):
    mu = jnp.mean(y, axis=-1, keepdims=True)
    yc = y - mu
    rstd = lax.rsqrt(jnp.mean(yc * yc, axis=-1, keepdims=True) + EPS)
    return yc * rstd, rstd


def _merge_fwd(x, ry, rg, ay, ga, gb, g_ret, w_ret, w_attn, w_out):
    S = x.shape[0]
    tm = 256

    def body(x_ref, ry_ref, rg_ref, ay_ref, ga_ref, gb_ref, gr_ref, wr_ref, wa_ref, wo_ref,
             u_ref, ba_ref, bb_ref, mg_ref, x1_ref):
        for hd in range(RET_HEADS):
            vs = slice(hd * 256, (hd + 1) * 256)
            yn, _ = _group_norm(ry_ref[:, vs])
            rgv = rg_ref[:, vs]
            u_ref[:, vs] = (rgv * _sigmoid(rgv) * (yn * gr_ref[:, vs])).astype(BF)
        ba = _dot(u_ref[...], wr_ref[...])
        bb = _dot(ay_ref[...], wa_ref[...])
        ba_ref[...] = ba
        bb_ref[...] = bb
        merged = (_sigmoid(ga_ref[...]) * ba + _sigmoid(gb_ref[...]) * bb).astype(BF)
        mg_ref[...] = merged
        x1_ref[...] = x_ref[...] + _dot(merged, wo_ref[...])

    outs = [BF, F32, F32, BF, F32]
    return pl.pallas_call(
        body, name="merge_fwd", grid=(S // tm,),
        in_specs=[_rows(tm, 1024)] * 6 + [_fixed((1, 1024)), _VMEM, _VMEM, _VMEM],
        out_specs=[_rows(tm, 1024)] * 5,
        out_shape=[jax.ShapeDtypeStruct((S, 1024), dt) for dt in outs],
        compiler_params=_params(("parallel",)),
    )(x, ry, rg, ay, ga, gb, g_ret, w_ret, w_attn, w_out)


def _ffn_fwd(x1, tgt, g2, g3, w_gate, w_up, w_down):
    S = x1.shape[0]
    tm = 256

    def body(x1_ref, t_ref, g2_ref, g3_ref, wg_ref, wu_ref, wd_ref,
             gate_ref, up_ref, h2_ref, act_ref, dx2_ref, loss_ref, dg3_ref):
        @pl.when(pl.program_id(0) == 0)
        def _():
            loss_ref[...] = jnp.zeros_like(loss_ref)
            dg3_ref[...] = jnp.zeros_like(dg3_ref)

        x1v = x1_ref[...]
        r2 = lax.rsqrt(jnp.mean(x1v * x1v, axis=-1, keepdims=True) + EPS)
        h2 = (x1v * r2 * g2_ref[...]).astype(BF)
        h2_ref[...] = h2
        gate = _dot(h2, wg_ref[...])
        up = _dot(h2, wu_ref[...])
        gate_ref[...] = gate
        up_ref[...] = up
        act = (gate * _sigmoid(gate) * up).astype(BF)
        act_ref[...] = act
        x2 = x1v + _dot(act, wd_ref[...])
        r3 = lax.rsqrt(jnp.mean(x2 * x2, axis=-1, keepdims=True) + EPS)
        xh = x2 * r3
        g3v = g3_ref[...]
        err = xh * g3v - t_ref[...]
        loss_ref[...] += 0.5 * jnp.sum(jnp.mean(err * err, axis=-1, keepdims=True), axis=0, keepdims=True)
        dy = err * (1.0 / D_MODEL)
        dg3_ref[...] += jnp.sum(dy * xh, axis=0, keepdims=True)
        dxh = dy * g3v
        dx2_ref[...] = r3 * (dxh - xh * jnp.mean(dxh * xh, axis=-1, keepdims=True))

    return pl.pallas_call(
        body, name="ffn_fwd", grid=(S // tm,),
        in_specs=[_rows(tm, 1024), _rows(tm, 1024), _fixed((1, 1024)), _fixed((1, 1024)), _VMEM, _VMEM, _VMEM],
        out_specs=[_rows(tm, D_FF), _rows(tm, D_FF), _rows(tm, 1024), _rows(tm, D_FF), _rows(tm, 1024),
                   _fixed((1, LANES)), _fixed((1, 1024))],
        out_shape=[jax.ShapeDtypeStruct((S, D_FF), F32), jax.ShapeDtypeStruct((S, D_FF), F32),
                   jax.ShapeDtypeStruct((S, 1024), BF), jax.ShapeDtypeStruct((S, D_FF), BF),
                   jax.ShapeDtypeStruct((S, 1024), F32),
                   jax.ShapeDtypeStruct((1, LANES), F32), jax.ShapeDtypeStruct((1, 1024), F32)],
        compiler_params=_params(("arbitrary",)),
    )(x1, tgt, g2, g3, w_gate, w_up, w_down)


def _ffn_bwd(dx2, gate, up, x1, g2, w_gate, w_up, w_down):
    S = x1.shape[0]
    tm = 256

    def body(dx2_ref, gate_ref, up_ref, x1_ref, g2_ref, wg_ref, wu_ref, wd_ref,
             dgate_ref, dup_ref, dx1_ref, dg2_ref):
        @pl.when(pl.program_id(0) == 0)
        def _():
            dg2_ref[...] = jnp.zeros_like(dg2_ref)

        dx2v = dx2_ref[...]
        dact = _dot_nt(dx2v.astype(BF), wd_ref[...])
        gate = gate_ref[...]
        sg = _sigmoid(gate)
        dgate = (dact * up_ref[...] * (sg * (1.0 + gate * (1.0 - sg)))).astype(BF)
        dup = (dact * (gate * sg)).astype(BF)
        dgate_ref[...] = dgate
        dup_ref[...] = dup
        dh2 = _dot_nt(dgate, wg_ref[...]) + _dot_nt(dup, wu_ref[...])
        x1v = x1_ref[...]
        r2 = lax.rsqrt(jnp.mean(x1v * x1v, axis=-1, keepdims=True) + EPS)
        xh = x1v * r2
        dg2_ref[...] += jnp.sum(dh2 * xh, axis=0, keepdims=True)
        dxh = dh2 * g2_ref[...]
        dx1_ref[...] = dx2v + r2 * (dxh - xh * jnp.mean(dxh * xh, axis=-1, keepdims=True))

    return pl.pallas_call(
        body, name="ffn_bwd", grid=(S // tm,),
        in_specs=[_rows(tm, 1024), _rows(tm, D_FF), _rows(tm, D_FF), _rows(tm, 1024), _fixed((1, 1024)),
                  _VMEM, _VMEM, _VMEM],
        out_specs=[_rows(tm, D_FF), _rows(tm, D_FF), _rows(tm, 1024), _fixed((1, 1024))],
        out_shape=[jax.ShapeDtypeStruct((S, D_FF), BF), jax.ShapeDtypeStruct((S, D_FF), BF),
                   jax.ShapeDtypeStruct((S, 1024), F32), jax.ShapeDtypeStruct((1, 1024), F32)],
        compiler_params=_params(("arbitrary",)),
    )(dx2, gate, up, x1, g2, w_gate, w_up, w_down)


def _merge_bwd(dx1, ga, gb, ba, bb, rg, ry, g_ret, w_ret, w_attn, w_out):
    S = dx1.shape[0]
    tm = 256

    def body(dx1_ref, ga_ref, gb_ref, ba_ref, bb_ref, rg_ref, ry_ref, gr_ref, wr_ref, wa_ref, wo_ref,
             dgates_ref, drg_ref, dba_ref, dbb_ref, day_ref, dry_ref, dgr_ref, db_ref):
        @pl.when(pl.program_id(0) == 0)
        def _():
            dgr_ref[...] = jnp.zeros_like(dgr_ref)
            db_ref[...] = jnp.zeros_like(db_ref)

        dm = _dot_nt(dx1_ref[...].astype(BF), wo_ref[...])
        sa = _sigmoid(ga_ref[...])
        sb = _sigmoid(gb_ref[...])
        dga = (dm * ba_ref[...] * (sa * (1.0 - sa))).astype(BF)
        dgb = (dm * bb_ref[...] * (sb * (1.0 - sb))).astype(BF)
        dgates_ref[:, :1024] = dga
        dgates_ref[:, 1024:] = dgb
        dba = (dm * sa).astype(BF)
        dbb = (dm * sb).astype(BF)
        dba_ref[...] = dba
        dbb_ref[...] = dbb
        day_ref[...] = _dot_nt(dbb, wa_ref[...]).astype(BF)
        du = _dot_nt(dba, wr_ref[...])
        for hd in range(RET_HEADS):
            vs = slice(hd * 256, (hd + 1) * 256)
            yn, rstd = _group_norm(ry_ref[:, vs])
            rgv = rg_ref[:, vs]
            sr = _sigmoid(rgv)
            silu = rgv * sr
            duv = du[:, vs]
            grv = gr_ref[:, vs]
            drg = (duv * (yn * grv) * (sr * (1.0 + rgv * (1.0 - sr)))).astype(BF)
            drg_ref[:, vs] = drg
            dyg = duv * silu
            dgr_ref[:, vs] += jnp.sum(dyg * yn, axis=0, keepdims=True)
            dyn = dyg * grv
            dry_ref[:, vs] = rstd * (dyn - jnp.mean(dyn, axis=-1, keepdims=True)
                                     - yn * jnp.mean(dyn * yn, axis=-1, keepdims=True))
            db_ref[:, vs] += jnp.sum(drg.astype(F32), axis=0, keepdims=True)
        db_ref[:, 1024:2048] += jnp.sum(dga.astype(F32), axis=0, keepdims=True)
        db_ref[:, 2048:] += jnp.sum(dgb.astype(F32), axis=0, keepdims=True)

    return pl.pallas_call(
        body, name="merge_bwd", grid=(S // tm,),
        in_specs=[_rows(tm, 1024)] * 7 + [_fixed((1, 1024)), _VMEM, _VMEM, _VMEM],
        out_specs=[_rows(tm, 2048), _rows(tm, 1024), _rows(tm, 1024), _rows(tm, 1024), _rows(tm, 1024),
                   _rows(tm, 1024), _fixed((1, 1024)), _fixed((1, 3072))],
        out_shape=[jax.ShapeDtypeStruct((S, 2048), BF), jax.ShapeDtypeStruct((S, 1024), BF),
                   jax.ShapeDtypeStruct((S, 1024), BF), jax.ShapeDtypeStruct((S, 1024), BF),
                   jax.ShapeDtypeStruct((S, 1024), BF), jax.ShapeDtypeStruct((S, 1024), F32),
                   jax.ShapeDtypeStruct((1, 1024), F32), jax.ShapeDtypeStruct((1, 3072), F32)],
        compiler_params=_params(("arbitrary",)),
    )(dx1, ga, gb, ba, bb, rg, ry, g_ret, w_ret, w_attn, w_out)


def _attn_bwd(aq, ak, av, day, sinks, ca, sa):
    S = aq.shape[0]
    N = S // SEQ_CHUNK

    def body(sk_ref, q_ref, kp_ref, kc_ref, vp_ref, vc_ref, do_ref, cq_ref, sq_ref, cp_ref, sp_ref,
             dq_ref, dk_ref, dv_ref, dsk_ref, carry_k, carry_v):
        n = pl.program_id(0)

        @pl.when(n == 0)
        def _():
            carry_k[...] = jnp.zeros_like(carry_k)
            carry_v[...] = jnp.zeros_like(carry_v)
            dsk_ref[...] = jnp.zeros_like(dsk_ref)

        @pl.when(n < N)
        def _():
            kk, kk_sw = _kv_halves(kp_ref, kc_ref)
            vv, vv_sw = _kv_halves(vp_ref, vc_ref)
            lo = lax.broadcasted_iota(jnp.int32, (256, 128), 1) < 64
            allowed = _attn_mask(n)
            cq, sq = cq_ref[...], sq_ref[...]
            dk_groups, dv_groups = [], []
            for g in range(2):
                k_lo, k_hi, v_lo, v_hi = _group_operands(g, kk, kk_sw, vv, vv_sw)
                qs = _stack_pairs(q_ref, g)
                dos = _stack_pairs(do_ref, g)
                dq = jnp.zeros((PAIRS * 128, 128), F32)
                dk_half, dv_half = [], []
                for which, (kx, vx) in enumerate(((k_lo, v_lo), (k_hi, v_hi))):
                    pr, ps = _attn_probs(qs, kx, allowed, _sink_row(sk_ref, g, which))
                    dp = _dot_nt(vx, dos)
                    dr = jnp.sum(pr * dp, axis=0, keepdims=True)
                    ds = (pr * (dp - dr) * (ATTN_HEAD_DIM ** -0.5)).astype(BF)
                    dsk_ref[2 * g + which:2 * g + which + 1, :] -= ps * dr
                    dq = dq + _dot_tn(ds, kx)
                    dk_half.append(_dot(ds, qs))
                    dv_half.append(_dot(pr.astype(BF), dos))
                for p in range(PAIRS):
                    sl = slice((4 * g + p) * 128, (4 * g + p + 1) * 128)
                    dq_ref[:, sl] = _rot_attn_bwd(dq[p * 128:(p + 1) * 128], cq, sq).astype(BF)
                tk = jnp.where(lo, dk_half[0], dk_half[1])
                tv = jnp.where(lo, dv_half[0], dv_half[1])
                dk_groups.append(tk + pltpu.roll(tk, 64, 1))
                dv_groups.append(tv + pltpu.roll(tv, 64, 1))
            dk_full = jnp.where(lo, dk_groups[0], dk_groups[1])
            dv_full = jnp.where(lo, dv_groups[0], dv_groups[1])
            dk_ref[...] = _rot_attn_bwd(carry_k[...] + dk_full[:128], cp_ref[...], sp_ref[...]).astype(BF)
            dv_ref[...] = (carry_v[...] + dv_full[:128]).astype(BF)
            carry_k[...] = dk_full[128:]
            carry_v[...] = dv_full[128:]

        @pl.when(n == N)
        def _():
            dk_ref[...] = _rot_attn_bwd(carry_k[...], cp_ref[...], sp_ref[...]).astype(BF)
            dv_ref[...] = carry_v[...].astype(BF)

    cur = lambda i: (jnp.minimum(i, N - 1), 0)
    prev = lambda i: (jnp.maximum(i - 1, 0), 0)
    blk = lambda w, im: pl.BlockSpec((128, w), im)
    return pl.pallas_call(
        body, name="attn_bwd", grid=(N + 1,),
        in_specs=[_SMEM, blk(1024, cur), blk(128, prev), blk(128, cur), blk(128, prev), blk(128, cur),
                  blk(1024, cur), blk(128, cur), blk(128, cur), blk(128, prev), blk(128, prev)],
        out_specs=[blk(1024, cur), blk(128, prev), blk(128, prev), _fixed((4, PAIRS * 128))],
        out_shape=[jax.ShapeDtypeStruct((S, 1024), BF), jax.ShapeDtypeStruct((S, 128), BF),
                   jax.ShapeDtypeStruct((S, 128), BF), jax.ShapeDtypeStruct((4, PAIRS * 128), F32)],
        scratch_shapes=[pltpu.VMEM((128, 128), F32), pltpu.VMEM((128, 128), F32)],
        compiler_params=_params(("arbitrary",)),
    )(sinks, aq, ak, ak, av, av, day, ca, sa, ca, sa)


def _ret_bwd(rq, rk, rv, dry, states, dmat, qd, kd, cd, cr, sr):
    S = rq.shape[0]
    N = S // SEQ_CHUNK
    scale = RET_QK_DIM ** -0.5

    def body(q_ref, k_ref, v_ref, dy_ref, st_ref, dm_ref, qd_ref, kd_ref, cd_ref, cr_ref, sr_ref,
             out_ref, dstate):
        @pl.when(pl.program_id(0) == 0)
        def _():
            dstate[...] = jnp.zeros_like(dstate)

        crv, srv = cr_ref[...], sr_ref[...]
        for hd in range(RET_HEADS):
            qs = slice(hd * 128, (hd + 1) * 128)
            vs = slice(hd * 256, (hd + 1) * 256)
            q = q_ref[:, qs]
            kf = k_ref[:, qs]
            k = kf.astype(BF)
            kdv = (kf * kd_ref[hd]).astype(BF)
            v = v_ref[:, vs]
            do = dy_ref[:, vs]
            dob = do.astype(BF)
            doq = (do * qd_ref[hd]).astype(BF)
            sn = st_ref[0, hd]
            ds = dstate[hd]
            dsb = ds.astype(BF)
            dmv = dm_ref[hd]
            a = (_dot_nt(q, k) * dmv).astype(BF)
            dv = _dot_tn(a, dob) + _dot(kdv, dsb)
            da = (_dot_nt(dob, v) * dmv).astype(BF)
            dq = _dot(da, k) + _dot_nt(doq, sn)
            dk = _dot_tn(da, q) + _dot_nt(v, dsb) * kd_ref[hd]
            dstate[hd] = ds * cd_ref[hd] + _dot_tn(q, doq)
            out_ref[:, qs] = _rot_ret_bwd(dq * scale, crv, srv).astype(BF)
            out_ref[:, 512 + hd * 128:512 + (hd + 1) * 128] = _rot_ret_bwd(dk, crv, srv).astype(BF)
            out_ref[:, 1024 + hd * 256:1024 + (hd + 1) * 256] = dv.astype(BF)

    rev = lambda w: pl.BlockSpec((128, w), lambda i: (N - 1 - i, 0))
    return pl.pallas_call(
        body, name="ret_bwd", grid=(N,),
        in_specs=[rev(512), rev(512), rev(1024), rev(1024),
                  pl.BlockSpec((1, RET_HEADS, 128, 256), lambda i: (N - 1 - i, 0, 0, 0)),
                  _VMEM, _VMEM, _VMEM, _VMEM, rev(128), rev(128)],
        out_specs=rev(2048),
        out_shape=jax.ShapeDtypeStruct((S, 2048), BF),
        scratch_shapes=[pltpu.VMEM((RET_HEADS, 128, 256), F32)],
        compiler_params=_params(("arbitrary",)),
    )(rq, rk, rv, dry, states, dmat, qd, kd, cd, cr, sr)


def _proj_bwd(d_ret, d_rg, d_att, d_gates, x, dx1, g1, w_in):
    S = x.shape[0]
    tm = 256
    pieces = ((0, 2048), (2048, 1024), (3072, 1280), (4352, 2048))

    def body(a_ref, b_ref, c_ref, d_ref, x_ref, dx1_ref, g_ref, w_ref, gx_ref, dg_ref, db_ref):
        @pl.when(pl.program_id(0) == 0)
        def _():
            dg_ref[...] = jnp.zeros_like(dg_ref)
            db_ref[...] = jnp.zeros_like(db_ref)

        dh = jnp.zeros((tm, D_MODEL), F32)
        for ref, (lo, n) in zip((a_ref, b_ref, c_ref, d_ref), pieces):
            v = ref[...]
            dh = dh + _dot_nt(v, w_ref[:, lo:lo + n])
            db_ref[:, lo:lo + n] += jnp.sum(v.astype(F32), axis=0, keepdims=True)
        xv = x_ref[...]
        r = lax.rsqrt(jnp.mean(xv * xv, axis=-1, keepdims=True) + EPS)
        xh = xv * r
        dg_ref[...] += jnp.sum(dh * xh, axis=0, keepdims=True)
        dxh = dh * g_ref[...]
        gx_ref[...] = dx1_ref[...] + r * (dxh - xh * jnp.mean(dxh * xh, axis=-1, keepdims=True))

    return pl.pallas_call(
        body, name="proj_bwd", grid=(S // tm,),
        in_specs=[_rows(tm, 2048), _rows(tm, 1024), _rows(tm, 1280), _rows(tm, 2048),
                  _rows(tm, 1024), _rows(tm, 1024), _fixed((1, 1024)), _VMEM],
        out_specs=[_rows(tm, 1024), _fixed((1, 1024)), _fixed((1, D_IN))],
        out_shape=[jax.ShapeDtypeStruct((S, 1024), F32), jax.ShapeDtypeStruct((1, 1024), F32),
                   jax.ShapeDtypeStruct((1, D_IN), F32)],
        compiler_params=_params(("arbitrary",)),
    )(d_ret, d_rg, d_att, d_gates, x, dx1, g1, w_in)


def _wgrad(a, b, name):
    S, M = a.shape
    N = b.shape[1]
    ts = 512

    def body(a_ref, b_ref, o_ref):
        @pl.when(pl.program_id(0) == 0)
        def _():
            o_ref[...] = jnp.zeros_like(o_ref)

        o_ref[...] += _dot_tn(a_ref[...].astype(BF), b_ref[...].astype(BF))

    return pl.pallas_call(
        body, name=name, grid=(S // ts,),
        in_specs=[_rows(ts, M), _rows(ts, N)],
        out_specs=_fixed((M, N)),
        out_shape=jax.ShapeDtypeStruct((M, N), F32),
        compiler_params=_params(("arbitrary",)),
    )(a, b)


def _add_n(terms, out_dtype, name):
    R, W = terms[0].shape
    tr = R // 8 if R % 64 == 0 else R
    while tr > 640 and tr % 16 == 0:
        tr //= 2

    def body(*refs):
        acc = refs[0][...].astype(F32)
        for r in refs[1:-1]:
            acc = acc + r[...].astype(F32)
        refs[-1][...] = acc.astype(out_dtype)

    return pl.pallas_call(
        body, name=name, grid=(R // tr,),
        in_specs=[_rows(tr, W)] * len(terms),
        out_specs=_rows(tr, W),
        out_shape=jax.ShapeDtypeStruct((R, W), out_dtype),
        compiler_params=_params(("parallel",)),
    )(*terms)


def _adamw(w, g, m, v, name):
    R, W = w.shape
    tr = R
    while tr * W > 256 * 1024 and tr % 16 == 0:
        tr //= 2

    def body(w_ref, g_ref, m_ref, v_ref, d_ref, mo_ref, vo_ref):
        gv = g_ref[...]
        mn = ADAM_B1 * m_ref[...] + (1.0 - ADAM_B1) * gv
        vn = ADAM_B2 * v_ref[...] + (1.0 - ADAM_B2) * (gv * gv)
        m_hat = mn / (1.0 - ADAM_B1 ** ADAM_STEP)
        v_hat = vn / (1.0 - ADAM_B2 ** ADAM_STEP)
        d_ref[...] = -ADAM_LR * (m_hat / (jnp.sqrt(v_hat) + ADAM_EPS) + ADAM_WD * w_ref[...])
        mo_ref[...] = mn
        vo_ref[...] = vn

    return pl.pallas_call(
        body, name=name, grid=(R // tr,),
        in_specs=[_rows(tr, W)] * 4,
        out_specs=[_rows(tr, W)] * 3,
        out_shape=[jax.ShapeDtypeStruct((R, W), F32)] * 3,
        compiler_params=_params(("parallel",)),
    )(w, g, m, v)


def _place():
    x, y, c = lax.axis_index("x"), lax.axis_index("y"), lax.axis_index("c")
    chips = [(1 - x, y), (x, 1 - y), (1 - x, 1 - y)]
    return x, y, c, chips


def _gather_weights(wb):
    R, W = wb.shape
    H = R // 2

    def body(w_ref, out_ref, send_sems, recv_sems):
        x, y, c, chips = _place()
        j = 2 * x + y
        sibling = (x, y, 1 - c)

        def rows(pj, half):
            return out_ref.at[pl.ds((2 * pj + half) * H, H), :]

        def copy(k, pj, half, to, src=None):
            return pltpu.make_async_remote_copy(
                src_ref=rows(pj, half) if src is None else src, dst_ref=rows(pj, half),
                send_sem=send_sems.at[k], recv_sem=recv_sems.at[k], device_id=to, device_id_type=MESH)

        first = [copy(k, j, c, (*chip, c), src=w_ref.at[pl.ds(c * H, H), :]) for k, chip in enumerate(chips)]
        for cp in first:
            cp.start()
        theirs = [2 * cx + cy for cx, cy in chips]
        passed = [copy(3 + k, theirs[k], c, sibling) for k in range(3)]
        for k in range(3):
            copy(k, theirs[k], c, (x, y, c)).wait_recv()
            passed[k].start()
        for k in range(3):
            copy(3 + k, theirs[k], 1 - c, (x, y, c)).wait_recv()
        for cp in first + passed:
            cp.wait_send()

    return pl.pallas_call(
        body, name="gather_weights",
        in_specs=[_ANY], out_specs=_ANY,
        out_shape=jax.ShapeDtypeStruct((N_CHIPS * R, W), wb.dtype),
        scratch_shapes=[pltpu.SemaphoreType.DMA((6,)), pltpu.SemaphoreType.DMA((6,))],
        compiler_params=pltpu.CompilerParams(has_side_effects=True),
    )(wb)


def _to_sibling(a, name):
    def body(a_ref, out_ref, send_sem, recv_sem):
        x, y, c, _ = _place()
        cp = pltpu.make_async_remote_copy(src_ref=a_ref, dst_ref=out_ref, send_sem=send_sem, recv_sem=recv_sem,
                                          device_id=(x, y, 1 - c), device_id_type=MESH)
        cp.start()
        cp.wait()

    return pl.pallas_call(
        body, name=name, in_specs=[_ANY], out_specs=_ANY,
        out_shape=jax.ShapeDtypeStruct(a.shape, a.dtype),
        scratch_shapes=[pltpu.SemaphoreType.DMA, pltpu.SemaphoreType.DMA],
        compiler_params=pltpu.CompilerParams(has_side_effects=True),
    )(a)


def _to_chips(a):
    H = a.shape[0] // 3

    def body(a_ref, out_ref, send_sems, recv_sems):
        x, y, c, chips = _place()
        cps = [pltpu.make_async_remote_copy(
            src_ref=a_ref.at[pl.ds(k * H, H), :], dst_ref=out_ref.at[pl.ds(k * H, H), :],
            send_sem=send_sems.at[k], recv_sem=recv_sems.at[k], device_id=(*chip, c), device_id_type=MESH)
            for k, chip in enumerate(chips)]
        for cp in cps:
            cp.start()
        for cp in cps:
            cp.wait()

    return pl.pallas_call(
        body, name="grads_to_chips", in_specs=[_ANY], out_specs=_ANY,
        out_shape=jax.ShapeDtypeStruct(a.shape, a.dtype),
        scratch_shapes=[pltpu.SemaphoreType.DMA((3,)), pltpu.SemaphoreType.DMA((3,))],
        compiler_params=pltpu.CompilerParams(has_side_effects=True),
    )(a)


def _all_reduce_small(v):
    m_per, n = v.shape
    n_dev = 8

    def body(x_ref, out_ref, all_ref, send_sems, recv_sems, local_sem):
        x, y, c, chips = _place()
        me, sibling = (x, y, c), (x, y, 1 - c)

        def rows(px, py, pc):
            return all_ref.at[pl.ds((4 * px + 2 * py + pc) * m_per, m_per), :]

        def copy(k, block, to, src=None):
            return pltpu.make_async_remote_copy(
                src_ref=rows(*block) if src is None else src, dst_ref=rows(*block),
                send_sem=send_sems.at[k], recv_sem=recv_sems.at[k], device_id=to, device_id_type=MESH)

        mine = pltpu.make_async_copy(x_ref, rows(*me), local_sem)
        mine.start()
        first = [copy(0, me, sibling, src=x_ref)]
        first += [copy(1 + k, me, (*chip, c), src=x_ref) for k, chip in enumerate(chips)]
        for cp in first:
            cp.start()
        passed = [copy(4 + k, (*chip, c), sibling) for k, chip in enumerate(chips)]
        for k, chip in enumerate(chips):
            copy(1 + k, (*chip, c), me).wait_recv()
            passed[k].start()
        copy(0, sibling, me).wait_recv()
        for k, chip in enumerate(chips):
            copy(4 + k, (*chip, 1 - c), me).wait_recv()
        for cp in first + passed:
            cp.wait_send()
        mine.wait()
        acc = all_ref[pl.ds(0, m_per), :]
        for d in range(1, n_dev):
            acc = acc + all_ref[pl.ds(d * m_per, m_per), :]
        out_ref[...] = acc

    return pl.pallas_call(
        body, name="all_reduce_small",
        in_specs=[_VMEM], out_specs=_VMEM,
        out_shape=jax.ShapeDtypeStruct((m_per, n), F32),
        scratch_shapes=[pltpu.VMEM((n_dev * m_per, n), F32),
                        pltpu.SemaphoreType.DMA((7,)), pltpu.SemaphoreType.DMA((7,)), pltpu.SemaphoreType.DMA],
        compiler_params=pltpu.CompilerParams(has_side_effects=True),
    )(v)


SHARD_ROWS = 1600 + 3 * 256 + 3 * 704


def _pack_shard(w_in, w_ret, w_attn, w_out, w_gate, w_up, w_down):
    return jnp.concatenate([w_in.reshape(1600, 1024), w_ret, w_attn, w_out,
                            w_gate.reshape(704, 1024), w_up.reshape(704, 1024), w_down], axis=0)


def _unpack_shard(p):
    o = 0
    out = []
    for rows, shape in ((1600, (1024, 1600)), (256, (256, 1024)), (256, (256, 1024)), (256, (256, 1024)),
                        (704, (1024, 704)), (704, (1024, 704)), (704, (704, 1024))):
        out.append(p[o:o + rows].reshape(shape))
        o += rows
    return out


def _unpack_full(wall):
    def cols(lo, rows, n):
        return wall[:, lo:lo + rows].reshape(N_CHIPS, 1024, n).transpose(1, 0, 2).reshape(1024, N_CHIPS * n)

    def rws(lo, rows):
        return wall[:, lo:lo + rows].reshape(N_CHIPS * rows, 1024)

    return (cols(0, 1600, 1600), rws(1600, 256), rws(1856, 256), rws(2112, 256),
            cols(2368, 704, 704), cols(3072, 704, 704), rws(3776, 704))


def _pack_full(g_in, g_ret, g_attn, g_out, g_gate, g_up, g_down):
    def cols(g, n):
        return g.reshape(1024, N_CHIPS, n).transpose(1, 0, 2).reshape(N_CHIPS, n, 1024)

    def rws(g, rows):
        return g.reshape(N_CHIPS, rows, 1024)

    return jnp.concatenate([cols(g_in, 1600), rws(g_ret, 256), rws(g_attn, 256), rws(g_out, 256),
                            cols(g_gate, 704), cols(g_up, 704), rws(g_down, 704)], axis=1)


def _tables(S):
    pos = jnp.arange(S, dtype=jnp.int32).astype(F32)

    def cs(half):
        inv_freq = ROPE_THETA ** (-jnp.arange(half, dtype=F32) / half)
        ang = pos[:, None] * inv_freq[None, :]
        return jnp.cos(ang), jnp.sin(ang)

    c64, s64 = cs(64)
    c32, s32 = cs(32)
    cr = jnp.concatenate([c64, c64], axis=1)
    sr = jnp.concatenate([-s64, s64], axis=1)
    ca = jnp.concatenate([c32, c32, c32, c32], axis=1)
    sa = jnp.concatenate([-s32, s32, -s32, s32], axis=1)
    return cr, sr, ca, sa


def _decays():
    H, C = RET_HEADS, SEQ_CHUNK
    log_gamma = jnp.log1p(-jnp.exp2(-5.0 - jnp.arange(H, dtype=F32)))
    idx = jnp.arange(C, dtype=F32)
    rel = idx[:, None] - idx[None, :]
    dmat = jnp.where(rel[None] >= 0, jnp.exp(log_gamma[:, None, None] * jnp.maximum(rel, 0.0)[None]), 0.0)
    qd = jnp.exp(log_gamma[:, None] * (idx + 1.0))[:, :, None]
    kd = jnp.exp(log_gamma[:, None] * (C - 1.0 - idx))[:, :, None]
    cd = jnp.exp(log_gamma * C)[:, None, None]
    return dmat, qd, kd, cd


def _local_step(x, tgt, ln1_g, b_in, ret_norm_g, attn_sinks, ln2_g, lnf_g, full_w):
    w_in, w_ret, w_attn, w_out, w_gate, w_up, w_down = full_w
    S = x.shape[0]
    cr, sr, ca, sa = _tables(S)
    dmat, qd, kd, cd = _decays()

    h, rq, rk, rv, rg, aq, ak, av, ga, gb = _proj_fwd(x, ln1_g, w_in, b_in, cr, sr, ca, sa)
    ry, states = _ret_fwd(rq, rk, rv, dmat, qd, kd, cd)
    ay = _attn_fwd(aq, ak, av, attn_sinks)
    u, ba, bb, merged, x1 = _merge_fwd(x, ry, rg, ay, ga, gb, ret_norm_g, w_ret, w_attn, w_out)
    gate, up, h2, act, dx2, loss, dg3 = _ffn_fwd(x1, tgt, ln2_g, lnf_g, w_gate, w_up, w_down)

    dgate, dup, dx1, dg2 = _ffn_bwd(dx2, gate, up, x1, ln2_g, w_gate, w_up, w_down)
    g_down = _wgrad(act, dx2, "wgrad_down")
    g_gate = _wgrad(h2, dgate, "wgrad_gate")
    g_up = _wgrad(h2, dup, "wgrad_up")
    dgates, drg, dba, dbb, day, dry, dgr, db_mid = _merge_bwd(dx1, ga, gb, ba, bb, rg, ry, ret_norm_g,
                                                                w_ret, w_attn, w_out)
    g_out = _wgrad(merged, dx1, "wgrad_out")
    g_ret = _wgrad(u, dba, "wgrad_ret")
    g_attn = _wgrad(ay, dbb, "wgrad_attn")
    daq, dak, dav, dsk = _attn_bwd(aq, ak, av, day, attn_sinks, ca, sa)
    d_att = jnp.concatenate([daq, dak, dav], axis=1)
    d_ret = _ret_bwd(rq, rk, rv, dry, states, dmat, qd, kd, cd, cr, sr)
    grad_x, dg1, db = _proj_bwd(d_ret, drg, d_att, dgates, x, dx1, ln1_g, w_in)
    g_in = jnp.concatenate([_wgrad(h, d_ret, "wgrad_in_ret"), _wgrad(h, drg, "wgrad_in_rg"),
                            _wgrad(h, d_att, "wgrad_in_att"), _wgrad(h, dgates, "wgrad_in_gates")], axis=1)
    d_sinks = dsk.reshape(2, 2, PAIRS, 128).sum(-1).transpose(0, 2, 1).reshape(1, ATTN_Q_HEADS)
    small = dict(loss=loss[0, 0], ln1_g=dg1, b_in=db, ret_norm_g=dgr, attn_sinks=d_sinks,
                 ln2_g=dg2, lnf_g=dg3)
    return grad_x, (g_in, g_ret, g_attn, g_out, g_gate, g_up, g_down), small


SMALL_ROWS = 16


def _pack_small(ln1, ret, ln2, lnf, b_in, sinks, loss):
    b = jnp.pad(b_in.reshape(-1), (0, 7 * 1024 - D_IN)).reshape(7, 1024)
    s = jnp.pad(sinks.reshape(-1), (0, 1024 - ATTN_Q_HEADS)).reshape(1, 1024)
    l = jnp.pad(loss.reshape(1), (0, 1023)).reshape(1, 1024)
    z = jnp.zeros((SMALL_ROWS - 13, 1024), F32)
    return jnp.concatenate([ln1.reshape(1, 1024), ret.reshape(1, 1024), ln2.reshape(1, 1024),
                            lnf.reshape(1, 1024), b, s, l, z], axis=0)


def _unpack_small(p):
    return dict(ln1_g=p[0:1], ret_norm_g=p[1:2], ln2_g=p[2:3], lnf_g=p[3], b_in=p[4:11].reshape(1, -1)[:, :D_IN],
                attn_sinks=p[11:12, :ATTN_Q_HEADS], loss=p[12, 0])


def kernel(x, ln1_g, w_in, b_in, ret_norm_g, w_ret_out, attn_sinks, w_attn_out, w_out, ln2_g, w_ffn_gate, w_ffn_up, w_ffn_down, lnf_g, loss_target, m_ln1_g, m_w_in, m_b_in, m_ret_norm_g, m_w_ret_out, m_attn_sinks, m_w_attn_out, m_w_out, m_ln2_g, m_w_ffn_gate, m_w_ffn_up, m_w_ffn_down, m_lnf_g, v_ln1_g, v_w_in, v_b_in, v_ret_norm_g, v_w_ret_out, v_attn_sinks, v_w_attn_out, v_w_out, v_ln2_g, v_w_ffn_gate, v_w_ffn_up, v_w_ffn_down, v_lnf_g):
    xi = lax.axis_index("x")
    yi = lax.axis_index("y")
    ci = lax.axis_index("c")
    j = 2 * xi + yi

    big_w = [w_in[0], w_ret_out[0], w_attn_out[0], w_out[0], w_ffn_gate[0], w_ffn_up[0], w_ffn_down[0]]
    big_m = [m_w_in[0], m_w_ret_out[0], m_w_attn_out[0], m_w_out[0], m_w_ffn_gate[0], m_w_ffn_up[0], m_w_ffn_down[0]]
    big_v = [v_w_in[0], v_w_ret_out[0], v_w_attn_out[0], v_w_out[0], v_w_ffn_gate[0], v_w_ffn_up[0], v_w_ffn_down[0]]

    wb = _pack_shard(*[w.astype(BF) for w in big_w])
    wall = lax.dynamic_update_slice(_gather_weights(wb), wb, (j * SHARD_ROWS, 0))
    wall = wall.reshape(N_CHIPS, SHARD_ROWS, 1024)
    full_w = _unpack_full(wall)

    grad_x, full_g, small = _local_step(x[0], loss_target[0], ln1_g, b_in, ret_norm_g, attn_sinks, ln2_g,
                                        lnf_g.reshape(1, -1), full_w)

    H = SHARD_ROWS // 2
    g4 = _pack_full(*full_g).reshape(N_CHIPS, 2, H, 1024)
    order = jnp.stack([j, j ^ 2, j ^ 1, j ^ 3])
    g4 = jnp.take(g4, order, axis=0)
    keep = lax.dynamic_index_in_dim(g4, ci, axis=1, keepdims=False).reshape(N_CHIPS * H, 1024)
    give = lax.dynamic_index_in_dim(g4, 1 - ci, axis=1, keepdims=False).astype(BF).reshape(N_CHIPS * H, 1024)
    got = _to_sibling(give, "pair_exchange")
    own_pair = _add_n([keep[:H], got[:H]], F32, "pair_sum_own")
    out_pair = _add_n([keep[H:], got[H:]], BF, "pair_sum_out")
    others = _to_chips(out_pair)
    half = _add_n([own_pair, others[:H], others[H:2 * H], others[2 * H:]], F32, "chip_sum")
    other = _to_sibling(half, "share_halves")
    g_shard = _unpack_shard(jnp.where(ci == 0, jnp.concatenate([half, other], axis=0),
                                      jnp.concatenate([other, half], axis=0)))

    sm = _unpack_small(_all_reduce_small(_pack_small(
        small["ln1_g"], small["ret_norm_g"], small["ln2_g"], small["lnf_g"], small["b_in"],
        small["attn_sinks"], small["loss"])))

    names = ["w_in", "w_ret_out", "w_attn_out", "w_out", "w_ffn_gate", "w_ffn_up", "w_ffn_down"]
    upd = {}
    for n, w, g, m, v in zip(names, big_w, g_shard, big_m, big_v):
        d, mn, vn = _adamw(w, g, m, v, "adamw_" + n)
        upd[n] = (g[None], d[None], mn[None], vn[None])

    z = jnp.zeros((), F32)
    pw = _pack_small(ln1_g, ret_norm_g, ln2_g, lnf_g, b_in, attn_sinks, z)
    pm = _pack_small(m_ln1_g, m_ret_norm_g, m_ln2_g, m_lnf_g, m_b_in, m_attn_sinks, z)
    pv = _pack_small(v_ln1_g, v_ret_norm_g, v_ln2_g, v_lnf_g, v_b_in, v_attn_sinks, z)
    pg = _pack_small(sm["ln1_g"], sm["ret_norm_g"], sm["ln2_g"], sm["lnf_g"], sm["b_in"], sm["attn_sinks"], z)
    sd, smn, svn = (_unpack_small(t) for t in _adamw(pw, pg, pm, pv, "adamw_small"))
    for n in ["ln1_g", "b_in", "ret_norm_g", "attn_sinks", "ln2_g", "lnf_g"]:
        upd[n] = (sm[n], sd[n], smn[n], svn[n])

    order_w = ["ln1_g", "w_in", "b_in", "ret_norm_g", "w_ret_out", "attn_sinks", "w_attn_out", "w_out",
               "ln2_g", "w_ffn_gate", "w_ffn_up", "w_ffn_down", "lnf_g"]
    outs = [sm["loss"], grad_x[None]]
    for k in range(4):
        outs += [upd[n][k] for n in order_w]
    return tuple(outs)
```

```python
import functools

import jax
import jax.numpy as jnp
from jax import lax
from jax.experimental import pallas as pl
from jax.experimental.pallas import tpu as pltpu

D_MODEL = 1024
SEQ_CHUNK = 128
RET_HEADS = 4
RET_QK_DIM = 128
RET_V_DIM = 256
ATTN_Q_HEADS = 16
ATTN_HEAD_DIM = 64
D_FF = 2816
D_IN = 6400
ROPE_THETA = 10000.0
EPS = 1e-6
N_CHIPS = 4

ADAM_LR = 0.001
ADAM_B1 = 0.9
ADAM_B2 = 0.999
ADAM_EPS = 1e-08
ADAM_WD = 0.01
ADAM_STEP = 10

OFF_RQ, OFF_RK, OFF_RV, OFF_RG = 0, 512, 1024, 2048
OFF_AQ, OFF_AK, OFF_AV, OFF_GA, OFF_GB = 3072, 4096, 4224, 4352, 5376

V7X_VMEM_LIMIT = 56 * 1024 * 1024
LANES = 128

BF = jnp.bfloat16
F32 = jnp.float32
MESH = pl.DeviceIdType.MESH

_VMEM = pl.BlockSpec(memory_space=pltpu.VMEM)
_SMEM = pl.BlockSpec(memory_space=pltpu.SMEM)
_ANY = pl.BlockSpec(memory_space=pl.ANY)


def _dot(a, b):
    return jnp.dot(a, b, preferred_element_type=F32)


def _dot_nt(a, b):
    return lax.dot_general(a, b, (((1,), (1,)), ((), ())), preferred_element_type=F32)


def _dot_tn(a, b):
    return lax.dot_general(a, b, (((0,), (0,)), ((), ())), preferred_element_type=F32)


def _sigmoid(x):
    return 1.0 / (1.0 + jnp.exp(-x))


def _rows(tm, n):
    return pl.BlockSpec((tm, n), lambda i: (i, 0))


def _fixed(shape):
    nd = len(shape)
    return pl.BlockSpec(shape, lambda i: (0,) * nd)


def _params(sem, **kw):
    return pltpu.CompilerParams(dimension_semantics=sem, vmem_limit_bytes=V7X_VMEM_LIMIT, **kw)


def _rot_ret(p, cr, sr):
    return p * cr + pltpu.roll(p, 64, 1) * sr


def _rot_ret_bwd(d, cr, sr):
    return d * cr - pltpu.roll(d, 64, 1) * sr


def _attn_partner(p):
    lane = lax.broadcasted_iota(jnp.int32, p.shape, 1)
    first = (lane % 64) < 32
    return jnp.where(first, pltpu.roll(p, 96, 1), pltpu.roll(p, 32, 1))


def _rot_attn(p, ca, sa):
    return p * ca + _attn_partner(p) * sa


def _rot_attn_bwd(d, ca, sa):
    return d * ca - _attn_partner(d) * sa


def _proj_fwd(x, g1, w_in, b_in, cr, sr, ca, sa):
    S = x.shape[0]
    tm = 512
    scale = RET_QK_DIM ** -0.5

    def body(x_ref, g_ref, w_ref, b_ref, cr_ref, sr_ref, ca_ref, sa_ref,
             h_ref, rq_ref, rk_ref, rv_ref, rg_ref, aq_ref, ak_ref, av_ref, ga_ref, gb_ref):
        xv = x_ref[...]
        r = lax.rsqrt(jnp.mean(xv * xv, axis=-1, keepdims=True) + EPS)
        h = (xv * r * g_ref[...]).astype(BF)
        h_ref[...] = h

        def piece(lo, n):
            return _dot_nt(h, w_ref[lo:lo + n, :]) + b_ref[:, lo:lo + n]

        crv, srv = cr_ref[...], sr_ref[...]
        pq = piece(OFF_RQ, 512)
        pk = piece(OFF_RK, 512)
        for hd in range(RET_HEADS):
            sl = slice(hd * 128, (hd + 1) * 128)
            rq_ref[:, sl] = (_rot_ret(pq[:, sl], crv, srv) * scale).astype(BF)
            rk_ref[:, sl] = _rot_ret(pk[:, sl], crv, srv)
        rv_ref[...] = piece(OFF_RV, 1024).astype(BF)
        rg_ref[...] = piece(OFF_RG, 1024)
        cav, sav = ca_ref[...], sa_ref[...]
        pa = piece(OFF_AQ, 1024)
        for blk in range(8):
            sl = slice(blk * 128, (blk + 1) * 128)
            aq_ref[:, sl] = _rot_attn(pa[:, sl], cav, sav).astype(BF)
        pkv = piece(OFF_AK, 256)
        ak_ref[...] = _rot_attn(pkv[:, :128], cav, sav).astype(BF)
        av_ref[...] = pkv[:, 128:].astype(BF)
        ga_ref[...] = piece(OFF_GA, 1024)
        gb_ref[...] = piece(OFF_GB, 1024)

    outs = [(D_MODEL, BF), (512, BF), (512, F32), (1024, BF), (1024, F32),
            (1024, BF), (128, BF), (128, BF), (1024, F32), (1024, F32)]
    return pl.pallas_call(
        body, name="proj_fwd", grid=(S // tm,),
        in_specs=[_rows(tm, D_MODEL), _fixed((1, D_MODEL)), _VMEM, _VMEM,
                  _rows(tm, 128), _rows(tm, 128), _rows(tm, 128), _rows(tm, 128)],
        out_specs=[_rows(tm, n) for n, _ in outs],
        out_shape=[jax.ShapeDtypeStruct((S, n), dt) for n, dt in outs],
        compiler_params=_params(("parallel",)),
    )(x, g1, w_in, b_in, cr, sr, ca, sa)


def _ret_fwd(rq, rk, rv, dmat, qd, kd, cd):
    S = rq.shape[0]
    N = S // SEQ_CHUNK

    def body(q_ref, k_ref, v_ref, dm_ref, qd_ref, kd_ref, cd_ref, y_ref, st_ref, state):
        @pl.when(pl.program_id(0) == 0)
        def _():
            state[...] = jnp.zeros_like(state)

        for hd in range(RET_HEADS):
            qs = slice(hd * 128, (hd + 1) * 128)
            vs = slice(hd * 256, (hd + 1) * 256)
            q = q_ref[:, qs]
            kf = k_ref[:, qs]
            k = kf.astype(BF)
            v = v_ref[:, vs]
            s = state[hd]
            sb = s.astype(BF)
            st_ref[0, hd] = sb
            a = _dot_nt(q, k) * dm_ref[hd]
            inner = _dot(a.astype(BF), v)
            cross = _dot(q, sb) * qd_ref[hd]
            y_ref[:, vs] = inner + cross
            kdv = (kf * kd_ref[hd]).astype(BF)
            state[hd] = s * cd_ref[hd] + _dot_tn(kdv, v)

    return pl.pallas_call(
        body, name="ret_fwd", grid=(N,),
        in_specs=[_rows(128, 512), _rows(128, 512), _rows(128, 1024), _VMEM, _VMEM, _VMEM, _VMEM],
        out_specs=[_rows(128, 1024), pl.BlockSpec((1, RET_HEADS, 128, 256), lambda i: (i, 0, 0, 0))],
        out_shape=[jax.ShapeDtypeStruct((S, 1024), F32),
                   jax.ShapeDtypeStruct((N, RET_HEADS, 128, 256), BF)],
        scratch_shapes=[pltpu.VMEM((RET_HEADS, 128, 256), F32)],
        compiler_params=_params(("arbitrary",)),
    )(rq, rk, rv, dmat, qd, kd, cd)


def _kv_halves(prev_ref, cur_ref):
    kk = jnp.concatenate([prev_ref[...], cur_ref[...]], axis=0).astype(F32)
    return kk, pltpu.roll(kk, 64, 1)


PAIRS = 4


def _attn_mask(n):
    kj = lax.broadcasted_iota(jnp.int32, (256, PAIRS * 128), 0)
    qi = lax.broadcasted_iota(jnp.int32, (256, PAIRS * 128), 1) & 127
    first_prev = jnp.where(n > 0, 0, 128)
    prev_ok = jnp.logical_and(jnp.logical_and(kj < 128, kj >= first_prev), qi < kj)
    cur_ok = jnp.logical_and(kj >= 128, qi >= kj - 128)
    return jnp.logical_or(prev_ok, cur_ok)


def _group_operands(g, kk, kk_sw, vv, vv_sw):
    lo = lax.broadcasted_iota(jnp.int32, (256, 128), 1) < 64
    k_a, k_b = (kk, kk_sw) if g == 0 else (kk_sw, kk)
    v_a, v_b = (vv, vv_sw) if g == 0 else (vv_sw, vv)
    return (jnp.where(lo, k_a, 0.0).astype(BF), jnp.where(lo, 0.0, k_b).astype(BF),
            jnp.where(lo, v_a, 0.0).astype(BF), jnp.where(lo, 0.0, v_b).astype(BF))


def _stack_pairs(ref, g):
    return jnp.concatenate([ref[:, (4 * g + p) * 128:(4 * g + p + 1) * 128] for p in range(PAIRS)], axis=0)


def _sink_row(sk_ref, g, which):
    return jnp.concatenate([jnp.full((1, 128), sk_ref[0, 8 * g + 2 * p + which], F32) for p in range(PAIRS)], axis=1)


def _attn_probs(qs, kx, allowed, sink):
    s = _dot_nt(kx, qs) * (ATTN_HEAD_DIM ** -0.5)
    s = jnp.where(allowed, s, -1e30)
    m = jnp.maximum(jnp.max(s, axis=0, keepdims=True), sink)
    e = jnp.exp(s - m)
    es = jnp.exp(sink - m)
    inv = 1.0 / (jnp.sum(e, axis=0, keepdims=True) + es)
    return e * inv, es * inv


def _attn_fwd(aq, ak, av, sinks):
    S = aq.shape[0]
    N = S // SEQ_CHUNK

    def body(sk_ref, q_ref, kp_ref, kc_ref, vp_ref, vc_ref, o_ref):
        kk, kk_sw = _kv_halves(kp_ref, kc_ref)
        vv, vv_sw = _kv_halves(vp_ref, vc_ref)
        allowed = _attn_mask(pl.program_id(0))
        for g in range(2):
            k_lo, k_hi, v_lo, v_hi = _group_operands(g, kk, kk_sw, vv, vv_sw)
            qs = _stack_pairs(q_ref, g)
            p0, _ = _attn_probs(qs, k_lo, allowed, _sink_row(sk_ref, g, 0))
            p1, _ = _attn_probs(qs, k_hi, allowed, _sink_row(sk_ref, g, 1))
            o = _dot_tn(p0.astype(BF), v_lo) + _dot_tn(p1.astype(BF), v_hi)
            for p in range(PAIRS):
                o_ref[:, (4 * g + p) * 128:(4 * g + p + 1) * 128] = o[p * 128:(p + 1) * 128].astype(BF)

    prev = lambda i: (jnp.maximum(i - 1, 0), 0)
    return pl.pallas_call(
        body, name="attn_fwd", grid=(N,),
        in_specs=[_SMEM, _rows(128, 1024), pl.BlockSpec((128, 128), prev), _rows(128, 128),
                  pl.BlockSpec((128, 128), prev), _rows(128, 128)],
        out_specs=_rows(128, 1024),
        out_shape=jax.ShapeDtypeStruct((S, 1024), BF),
        compiler_params=_params(("parallel",)),
    )(sinks, aq, ak, ak, av, av)


def _group_norm(y):
    mu = jnp.mean(y, axis=-1, keepdims=True)
    yc = y - mu
    rstd = lax.rsqrt(jnp.mean(yc * yc, axis=-1, keepdims=True) + EPS)
    return yc * rstd, rstd


def _merge_fwd(x, ry, rg, ay, ga, gb, g_ret, w_ret, w_attn, w_out):
    S = x.shape[0]
    tm = 256

    def body(x_ref, ry_ref, rg_ref, ay_ref, ga_ref, gb_ref, gr_ref, wr_ref, wa_ref, wo_ref,
             u_ref, ba_ref, bb_ref, mg_ref, x1_ref):
        for hd in range(RET_HEADS):
            vs = slice(hd * 256, (hd + 1) * 256)
            yn, _ = _group_norm(ry_ref[:, vs])
            rgv = rg_ref[:, vs]
            u_ref[:, vs] = (rgv * _sigmoid(rgv) * (yn * gr_ref[:, vs])).astype(BF)
        ba = _dot(u_ref[...], wr_ref[...])
        bb = _dot(ay_ref[...], wa_ref[...])
        ba_ref[...] = ba
        bb_ref[...] = bb
        merged = (_sigmoid(ga_ref[...]) * ba + _sigmoid(gb_ref[...]) * bb).astype(BF)
        mg_ref[...] = merged
        x1_ref[...] = x_ref[...] + _dot(merged, wo_ref[...])

    outs = [BF, F32, F32, BF, F32]
    return pl.pallas_call(
        body, name="merge_fwd", grid=(S // tm,),
        in_specs=[_rows(tm, 1024)] * 6 + [_fixed((1, 1024)), _VMEM, _VMEM, _VMEM],
        out_specs=[_rows(tm, 1024)] * 5,
        out_shape=[jax.ShapeDtypeStruct((S, 1024), dt) for dt in outs],
        compiler_params=_params(("parallel",)),
    )(x, ry, rg, ay, ga, gb, g_ret, w_ret, w_attn, w_out)


def _ffn_fwd(x1, tgt, g2, g3, w_gate, w_up, w_down):
    S = x1.shape[0]
    tm = 256

    def body(x1_ref, t_ref, g2_ref, g3_ref, wg_ref, wu_ref, wd_ref,
             gate_ref, up_ref, h2_ref, act_ref, dx2_ref, loss_ref, dg3_ref):
        @pl.when(pl.program_id(0) == 0)
        def _():
            loss_ref[...] = jnp.zeros_like(loss_ref)
            dg3_ref[...] = jnp.zeros_like(dg3_ref)

        x1v = x1_ref[...]
        r2 = lax.rsqrt(jnp.mean(x1v * x1v, axis=-1, keepdims=True) + EPS)
        h2 = (x1v * r2 * g2_ref[...]).astype(BF)
        h2_ref[...] = h2
        gate = _dot_nt(h2, wg_ref[...])
        up = _dot_nt(h2, wu_ref[...])
        gate_ref[...] = gate
        up_ref[...] = up
        act = (gate * _sigmoid(gate) * up).astype(BF)
        act_ref[...] = act
        x2 = x1v + _dot(act, wd_ref[...])
        r3 = lax.rsqrt(jnp.mean(x2 * x2, axis=-1, keepdims=True) + EPS)
        xh = x2 * r3
        g3v = g3_ref[...]
        err = xh * g3v - t_ref[...]
        loss_ref[...] += 0.5 * jnp.sum(jnp.mean(err * err, axis=-1, keepdims=True), axis=0, keepdims=True)
        dy = err * (1.0 / D_MODEL)
        dg3_ref[...] += jnp.sum(dy * xh, axis=0, keepdims=True)
        dxh = dy * g3v
        dx2_ref[...] = r3 * (dxh - xh * jnp.mean(dxh * xh, axis=-1, keepdims=True))

    return pl.pallas_call(
        body, name="ffn_fwd", grid=(S // tm,),
        in_specs=[_rows(tm, 1024), _rows(tm, 1024), _fixed((1, 1024)), _fixed((1, 1024)), _VMEM, _VMEM, _VMEM],
        out_specs=[_rows(tm, D_FF), _rows(tm, D_FF), _rows(tm, 1024), _rows(tm, D_FF), _rows(tm, 1024),
                   _fixed((1, LANES)), _fixed((1, 1024))],
        out_shape=[jax.ShapeDtypeStruct((S, D_FF), F32), jax.ShapeDtypeStruct((S, D_FF), F32),
                   jax.ShapeDtypeStruct((S, 1024), BF), jax.ShapeDtypeStruct((S, D_FF), BF),
                   jax.ShapeDtypeStruct((S, 1024), F32),
                   jax.ShapeDtypeStruct((1, LANES), F32), jax.ShapeDtypeStruct((1, 1024), F32)],
        compiler_params=_params(("arbitrary",)),
    )(x1, tgt, g2, g3, w_gate, w_up, w_down)


def _ffn_bwd(dx2, gate, up, x1, g2, w_gate, w_up, w_down):
    S = x1.shape[0]
    tm = 256

    def body(dx2_ref, gate_ref, up_ref, x1_ref, g2_ref, wg_ref, wu_ref, wd_ref,
             dgate_ref, dup_ref, dx1_ref, dg2_ref):
        @pl.when(pl.program_id(0) == 0)
        def _():
            dg2_ref[...] = jnp.zeros_like(dg2_ref)

        dx2v = dx2_ref[...]
        dact = _dot_nt(dx2v.astype(BF), wd_ref[...])
        gate = gate_ref[...]
        sg = _sigmoid(gate)
        dgate = (dact * up_ref[...] * (sg * (1.0 + gate * (1.0 - sg)))).astype(BF)
        dup = (dact * (gate * sg)).astype(BF)
        dgate_ref[...] = dgate
        dup_ref[...] = dup
        dh2 = _dot(dgate, wg_ref[...]) + _dot(dup, wu_ref[...])
        x1v = x1_ref[...]
        r2 = lax.rsqrt(jnp.mean(x1v * x1v, axis=-1, keepdims=True) + EPS)
        xh = x1v * r2
        dg2_ref[...] += jnp.sum(dh2 * xh, axis=0, keepdims=True)
        dxh = dh2 * g2_ref[...]
        dx1_ref[...] = dx2v + r2 * (dxh - xh * jnp.mean(dxh * xh, axis=-1, keepdims=True))

    return pl.pallas_call(
        body, name="ffn_bwd", grid=(S // tm,),
        in_specs=[_rows(tm, 1024), _rows(tm, D_FF), _rows(tm, D_FF), _rows(tm, 1024), _fixed((1, 1024)),
                  _VMEM, _VMEM, _VMEM],
        out_specs=[_rows(tm, D_FF), _rows(tm, D_FF), _rows(tm, 1024), _fixed((1, 1024))],
        out_shape=[jax.ShapeDtypeStruct((S, D_FF), BF), jax.ShapeDtypeStruct((S, D_FF), BF),
                   jax.ShapeDtypeStruct((S, 1024), F32), jax.ShapeDtypeStruct((1, 1024), F32)],
        compiler_params=_params(("arbitrary",)),
    )(dx2, gate, up, x1, g2, w_gate, w_up, w_down)


def _merge_bwd(dx1, ga, gb, ba, bb, rg, ry, g_ret, w_ret, w_attn, w_out):
    S = dx1.shape[0]
    tm = 256

    def body(dx1_ref, ga_ref, gb_ref, ba_ref, bb_ref, rg_ref, ry_ref, gr_ref, wr_ref, wa_ref, wo_ref,
             dgates_ref, drg_ref, dba_ref, dbb_ref, day_ref, dry_ref, dgr_ref, db_ref):
        @pl.when(pl.program_id(0) == 0)
        def _():
            dgr_ref[...] = jnp.zeros_like(dgr_ref)
            db_ref[...] = jnp.zeros_like(db_ref)

        dm = _dot_nt(dx1_ref[...].astype(BF), wo_ref[...])
        sa = _sigmoid(ga_ref[...])
        sb = _sigmoid(gb_ref[...])
        dga = (dm * ba_ref[...] * (sa * (1.0 - sa))).astype(BF)
        dgb = (dm * bb_ref[...] * (sb * (1.0 - sb))).astype(BF)
        dgates_ref[:, :1024] = dga
        dgates_ref[:, 1024:] = dgb
        dba = (dm * sa).astype(BF)
        dbb = (dm * sb).astype(BF)
        dba_ref[...] = dba
        dbb_ref[...] = dbb
        day_ref[...] = _dot_nt(dbb, wa_ref[...]).astype(BF)
        du = _dot_nt(dba, wr_ref[...])
        for hd in range(RET_HEADS):
            vs = slice(hd * 256, (hd + 1) * 256)
            yn, rstd = _group_norm(ry_ref[:, vs])
            rgv = rg_ref[:, vs]
            sr = _sigmoid(rgv)
            silu = rgv * sr
            duv = du[:, vs]
            grv = gr_ref[:, vs]
            drg = (duv * (yn * grv) * (sr * (1.0 + rgv * (1.0 - sr)))).astype(BF)
            drg_ref[:, vs] = drg
            dyg = duv * silu
            dgr_ref[:, vs] += jnp.sum(dyg * yn, axis=0, keepdims=True)
            dyn = dyg * grv
            dry_ref[:, vs] = rstd * (dyn - jnp.mean(dyn, axis=-1, keepdims=True)
                                     - yn * jnp.mean(dyn * yn, axis=-1, keepdims=True))
            db_ref[:, vs] += jnp.sum(drg.astype(F32), axis=0, keepdims=True)
        db_ref[:, 1024:2048] += jnp.sum(dga.astype(F32), axis=0, keepdims=True)
        db_ref[:, 2048:] += jnp.sum(dgb.astype(F32), axis=0, keepdims=True)

    return pl.pallas_call(
        body, name="merge_bwd", grid=(S // tm,),
        in_specs=[_rows(tm, 1024)] * 7 + [_fixed((1, 1024)), _VMEM, _VMEM, _VMEM],
        out_specs=[_rows(tm, 2048), _rows(tm, 1024), _rows(tm, 1024), _rows(tm, 1024), _rows(tm, 1024),
                   _rows(tm, 1024), _fixed((1, 1024)), _fixed((1, 3072))],
        out_shape=[jax.ShapeDtypeStruct((S, 2048), BF), jax.ShapeDtypeStruct((S, 1024), BF),
                   jax.ShapeDtypeStruct((S, 1024), BF), jax.ShapeDtypeStruct((S, 1024), BF),
                   jax.ShapeDtypeStruct((S, 1024), BF), jax.ShapeDtypeStruct((S, 1024), F32),
                   jax.ShapeDtypeStruct((1, 1024), F32), jax.ShapeDtypeStruct((1, 3072), F32)],
        compiler_params=_params(("arbitrary",)),
    )(dx1, ga, gb, ba, bb, rg, ry, g_ret, w_ret, w_attn, w_out)


def _attn_bwd(aq, ak, av, day, sinks, ca, sa):
    S = aq.shape[0]
    N = S // SEQ_CHUNK

    def body(sk_ref, q_ref, kp_ref, kc_ref, vp_ref, vc_ref, do_ref, cq_ref, sq_ref, cp_ref, sp_ref,
             dq_ref, dk_ref, dv_ref, dsk_ref, carry_k, carry_v):
        n = pl.program_id(0)

        @pl.when(n == 0)
        def _():
            carry_k[...] = jnp.zeros_like(carry_k)
            carry_v[...] = jnp.zeros_like(carry_v)
            dsk_ref[...] = jnp.zeros_like(dsk_ref)

        @pl.when(n < N)
        def _():
            kk, kk_sw = _kv_halves(kp_ref, kc_ref)
            vv, vv_sw = _kv_halves(vp_ref, vc_ref)
            lo = lax.broadcasted_iota(jnp.int32, (256, 128), 1) < 64
            allowed = _attn_mask(n)
            cq, sq = cq_ref[...], sq_ref[...]
            dk_groups, dv_groups = [], []
            for g in range(2):
                k_lo, k_hi, v_lo, v_hi = _group_operands(g, kk, kk_sw, vv, vv_sw)
                qs = _stack_pairs(q_ref, g)
                dos = _stack_pairs(do_ref, g)
                dq = jnp.zeros((PAIRS * 128, 128), F32)
                dk_half, dv_half = [], []
                for which, (kx, vx) in enumerate(((k_lo, v_lo), (k_hi, v_hi))):
                    pr, ps = _attn_probs(qs, kx, allowed, _sink_row(sk_ref, g, which))
                    dp = _dot_nt(vx, dos)
                    dr = jnp.sum(pr * dp, axis=0, keepdims=True)
                    ds = (pr * (dp - dr) * (ATTN_HEAD_DIM ** -0.5)).astype(BF)
                    dsk_ref[2 * g + which:2 * g + which + 1, :] -= ps * dr
                    dq = dq + _dot_tn(ds, kx)
                    dk_half.append(_dot(ds, qs))
                    dv_half.append(_dot(pr.astype(BF), dos))
                for p in range(PAIRS):
                    sl = slice((4 * g + p) * 128, (4 * g + p + 1) * 128)
                    dq_ref[:, sl] = _rot_attn_bwd(dq[p * 128:(p + 1) * 128], cq, sq).astype(BF)
                tk = jnp.where(lo, dk_half[0], dk_half[1])
                tv = jnp.where(lo, dv_half[0], dv_half[1])
                dk_groups.append(tk + pltpu.roll(tk, 64, 1))
                dv_groups.append(tv + pltpu.roll(tv, 64, 1))
            dk_full = jnp.where(lo, dk_groups[0], dk_groups[1])
            dv_full = jnp.where(lo, dv_groups[0], dv_groups[1])
            dk_ref[...] = _rot_attn_bwd(carry_k[...] + dk_full[:128], cp_ref[...], sp_ref[...]).astype(BF)
            dv_ref[...] = (carry_v[...] + dv_full[:128]).astype(BF)
            carry_k[...] = dk_full[128:]
            carry_v[...] = dv_full[128:]

        @pl.when(n == N)
        def _():
            dk_ref[...] = _rot_attn_bwd(carry_k[...], cp_ref[...], sp_ref[...]).astype(BF)
            dv_ref[...] = carry_v[...].astype(BF)

    cur = lambda i: (jnp.minimum(i, N - 1), 0)
    prev = lambda i: (jnp.maximum(i - 1, 0), 0)
    blk = lambda w, im: pl.BlockSpec((128, w), im)
    return pl.pallas_call(
        body, name="attn_bwd", grid=(N + 1,),
        in_specs=[_SMEM, blk(1024, cur), blk(128, prev), blk(128, cur), blk(128, prev), blk(128, cur),
                  blk(1024, cur), blk(128, cur), blk(128, cur), blk(128, prev), blk(128, prev)],
        out_specs=[blk(1024, cur), blk(128, prev), blk(128, prev), _fixed((4, PAIRS * 128))],
        out_shape=[jax.ShapeDtypeStruct((S, 1024), BF), jax.ShapeDtypeStruct((S, 128), BF),
                   jax.ShapeDtypeStruct((S, 128), BF), jax.ShapeDtypeStruct((4, PAIRS * 128), F32)],
        scratch_shapes=[pltpu.VMEM((128, 128), F32), pltpu.VMEM((128, 128), F32)],
        compiler_params=_params(("arbitrary",)),
    )(sinks, aq, ak, ak, av, av, day, ca, sa, ca, sa)


def _ret_bwd(rq, rk, rv, dry, states, dmat, qd, kd, cd, cr, sr):
    S = rq.shape[0]
    N = S // SEQ_CHUNK
    scale = RET_QK_DIM ** -0.5

    def body(q_ref, k_ref, v_ref, dy_ref, st_ref, dm_ref, qd_ref, kd_ref, cd_ref, cr_ref, sr_ref,
             out_ref, dstate):
        @pl.when(pl.program_id(0) == 0)
        def _():
            dstate[...] = jnp.zeros_like(dstate)

        crv, srv = cr_ref[...], sr_ref[...]
        for hd in range(RET_HEADS):
            qs = slice(hd * 128, (hd + 1) * 128)
            vs = slice(hd * 256, (hd + 1) * 256)
            q = q_ref[:, qs]
            kf = k_ref[:, qs]
            k = kf.astype(BF)
            kdv = (kf * kd_ref[hd]).astype(BF)
            v = v_ref[:, vs]
            do = dy_ref[:, vs]
            dob = do.astype(BF)
            doq = (do * qd_ref[hd]).astype(BF)
            sn = st_ref[0, hd]
            ds = dstate[hd]
            dsb = ds.astype(BF)
            dmv = dm_ref[hd]
            a = (_dot_nt(q, k) * dmv).astype(BF)
            dv = _dot_tn(a, dob) + _dot(kdv, dsb)
            da = (_dot_nt(dob, v) * dmv).astype(BF)
            dq = _dot(da, k) + _dot_nt(doq, sn)
            dk = _dot_tn(da, q) + _dot_nt(v, dsb) * kd_ref[hd]
            dstate[hd] = ds * cd_ref[hd] + _dot_tn(q, doq)
            out_ref[:, qs] = _rot_ret_bwd(dq * scale, crv, srv).astype(BF)
            out_ref[:, 512 + hd * 128:512 + (hd + 1) * 128] = _rot_ret_bwd(dk, crv, srv).astype(BF)
            out_ref[:, 1024 + hd * 256:1024 + (hd + 1) * 256] = dv.astype(BF)

    rev = lambda w: pl.BlockSpec((128, w), lambda i: (N - 1 - i, 0))
    return pl.pallas_call(
        body, name="ret_bwd", grid=(N,),
        in_specs=[rev(512), rev(512), rev(1024), rev(1024),
                  pl.BlockSpec((1, RET_HEADS, 128, 256), lambda i: (N - 1 - i, 0, 0, 0)),
                  _VMEM, _VMEM, _VMEM, _VMEM, rev(128), rev(128)],
        out_specs=rev(2048),
        out_shape=jax.ShapeDtypeStruct((S, 2048), BF),
        scratch_shapes=[pltpu.VMEM((RET_HEADS, 128, 256), F32)],
        compiler_params=_params(("arbitrary",)),
    )(rq, rk, rv, dry, states, dmat, qd, kd, cd, cr, sr)


def _proj_bwd(d_pieces, x, dx1, g1, w_in_t):
    S = x.shape[0]
    tm = 256
    widths = [d.shape[1] for d in d_pieces]
    assert sum(widths) == D_IN
    nd = len(d_pieces)

    def body(*refs):
        d_refs = refs[:nd]
        x_ref, dx1_ref, g_ref, w_ref, gx_ref, dg_ref, db_ref = refs[nd:]

        @pl.when(pl.program_id(0) == 0)
        def _():
            dg_ref[...] = jnp.zeros_like(dg_ref)
            db_ref[...] = jnp.zeros_like(db_ref)

        dh = jnp.zeros((tm, D_MODEL), F32)
        lo = 0
        for ref, n in zip(d_refs, widths):
            v = ref[...]
            dh = dh + _dot(v, w_ref[lo:lo + n, :])
            db_ref[:, lo:lo + n] += jnp.sum(v.astype(F32), axis=0, keepdims=True)
            lo += n
        xv = x_ref[...]
        r = lax.rsqrt(jnp.mean(xv * xv, axis=-1, keepdims=True) + EPS)
        xh = xv * r
        dg_ref[...] += jnp.sum(dh * xh, axis=0, keepdims=True)
        dxh = dh * g_ref[...]
        gx_ref[...] = dx1_ref[...] + r * (dxh - xh * jnp.mean(dxh * xh, axis=-1, keepdims=True))

    return pl.pallas_call(
        body, name="proj_bwd", grid=(S // tm,),
        in_specs=[_rows(tm, n) for n in widths] + [_rows(tm, 1024), _rows(tm, 1024), _fixed((1, 1024)), _VMEM],
        out_specs=[_rows(tm, 1024), _fixed((1, 1024)), _fixed((1, D_IN))],
        out_shape=[jax.ShapeDtypeStruct((S, 1024), F32), jax.ShapeDtypeStruct((1, 1024), F32),
                   jax.ShapeDtypeStruct((1, D_IN), F32)],
        compiler_params=_params(("arbitrary",)),
    )(*d_pieces, x, dx1, g1, w_in_t)


SHARD_ROWS = (1600, 256, 256, 256, 704, 704, 704)
SHARD_OFFS = (0, 1600, 1856, 2112, 2368, 3072, 3776)
BLOCK_ROWS = 4480
HALF_ROWS = 2240
SUM_TILE = 320


def _wgrad(a_list, b, t, packed, name):
    S = b.shape[0]
    ts = 512
    steps = S // ts
    widths = [a.shape[1] for a in a_list]
    na = len(a_list)
    r, off = SHARD_ROWS[t], SHARD_OFFS[t]
    assert sum(widths) == N_CHIPS * r
    n_in = na + 1 + (2 if packed is not None else 0)

    def body(*refs):
        a_refs, b_ref = refs[:na], refs[na]
        g32_ref, gb_ref, acc, stage, sems = refs[n_in:]
        i = pl.program_id(0)

        @pl.when(i == 0)
        def _():
            acc[...] = jnp.zeros_like(acc)

        bv = b_ref[...].astype(BF)
        lo = 0
        for a_ref, n in zip(a_refs, widths):
            acc[lo:lo + n, :] += _dot_tn(a_ref[...].astype(BF), bv)
            lo += n

        @pl.when(i == steps - 1)
        def _():
            for chip in range(N_CHIPS):
                src = acc.at[chip * r:(chip + 1) * r, :]
                dst = pl.ds(chip * BLOCK_ROWS + off, r)
                c32 = pltpu.make_async_copy(src, g32_ref.at[dst, :], sems.at[0])
                c32.start()
                stage[...] = acc[chip * r:(chip + 1) * r, :].astype(BF)
                c16 = pltpu.make_async_copy(stage, gb_ref.at[dst, :], sems.at[1])
                c16.start()
                c32.wait()
                c16.wait()

    shapes = [jax.ShapeDtypeStruct((N_CHIPS * BLOCK_ROWS, 1024), F32),
              jax.ShapeDtypeStruct((N_CHIPS * BLOCK_ROWS, 1024), BF)]
    operands = list(a_list) + [b]
    in_specs = [_rows(ts, n) for n in widths] + [_rows(ts, 1024)]
    aliases = {}
    if packed is not None:
        operands += list(packed)
        in_specs += [_ANY, _ANY]
        aliases = {na + 1: 0, na + 2: 1}
    return pl.pallas_call(
        body, name=name, grid=(steps,),
        in_specs=in_specs, out_specs=[_ANY, _ANY], out_shape=shapes,
        scratch_shapes=[pltpu.VMEM((N_CHIPS * r, 1024), F32), pltpu.VMEM((r, 1024), BF),
                        pltpu.SemaphoreType.DMA((2,))],
        input_output_aliases=aliases,
        compiler_params=_params(("arbitrary",)),
    )(*operands)


def _stage_weight(place, w, name):
    r = w.shape[0]
    tr = 800 if r == 1600 else r
    nb = r // tr

    def body(p_ref, w_ref, o_ref):
        o_ref[...] = w_ref[...].astype(BF)

    return pl.pallas_call(
        body, name=name,
        grid_spec=pltpu.PrefetchScalarGridSpec(
            num_scalar_prefetch=1, grid=(nb,),
            in_specs=[pl.BlockSpec((tr, 1024), lambda i, p: (i, 0))],
            out_specs=pl.BlockSpec((tr, 1024), lambda i, p: (p[0] * nb + i, 0))),
        out_shape=jax.ShapeDtypeStruct((N_CHIPS * r, 1024), BF),
        compiler_params=_params(("arbitrary",)),
    )(place, w)


def _pair_sum(place, g32, got):
    nb = HALF_ROWS // SUM_TILE

    def body(p_ref, g_ref, r_ref, s32_ref, s16_ref):
        s = g_ref[...] + r_ref[...].astype(F32)
        s32_ref[...] = s
        s16_ref[...] = s.astype(BF)

    flat = pl.BlockSpec((SUM_TILE, 1024), lambda i, p: (i, 0))
    return pl.pallas_call(
        body, name="pair_sum",
        grid_spec=pltpu.PrefetchScalarGridSpec(
            num_scalar_prefetch=1, grid=(N_CHIPS * nb,),
            in_specs=[pl.BlockSpec((SUM_TILE, 1024), lambda i, p: ((2 * (i // nb) + p[1]) * nb + i % nb, 0)), flat],
            out_specs=[flat, flat]),
        out_shape=[jax.ShapeDtypeStruct((N_CHIPS * HALF_ROWS, 1024), F32),
                   jax.ShapeDtypeStruct((N_CHIPS * HALF_ROWS, 1024), BF)],
        compiler_params=_params(("arbitrary",)),
    )(place, g32, got)


def _chip_sum(place, s32, others):
    nb = HALF_ROWS // SUM_TILE

    def body(p_ref, a_ref, r0_ref, r1_ref, r2_ref, o_ref):
        o_ref[...] = ((a_ref[...] + r0_ref[...].astype(F32)) + r1_ref[...].astype(F32)) + r2_ref[...].astype(F32)

    blk = lambda f: pl.BlockSpec((SUM_TILE, 1024), f)
    return pl.pallas_call(
        body, name="chip_sum",
        grid_spec=pltpu.PrefetchScalarGridSpec(
            num_scalar_prefetch=1, grid=(nb,),
            in_specs=[blk(lambda i, p: (p[0] * nb + i, 0)), blk(lambda i, p: (i, 0)),
                      blk(lambda i, p: (nb + i, 0)), blk(lambda i, p: (2 * nb + i, 0))],
            out_specs=blk(lambda i, p: (p[1] * nb + i, 0))),
        out_shape=jax.ShapeDtypeStruct((BLOCK_ROWS, 1024), F32),
        compiler_params=_params(("arbitrary",)),
    )(place, s32, others, others, others)


def _adamw(w, g, m, v, name, g_row0=0):
    r = w.shape[0]
    tr = 320 if r % 320 == 0 and g_row0 % 320 == 0 else (64 if r % 64 == 0 else r)
    assert g_row0 % tr == 0
    g0 = g_row0 // tr

    def body(w_ref, g_ref, m_ref, v_ref, go_ref, d_ref, mo_ref, vo_ref):
        gv = g_ref[...]
        mn = ADAM_B1 * m_ref[...] + (1.0 - ADAM_B1) * gv
        vn = ADAM_B2 * v_ref[...] + (1.0 - ADAM_B2) * (gv * gv)
        m_hat = mn / (1.0 - ADAM_B1 ** ADAM_STEP)
        v_hat = vn / (1.0 - ADAM_B2 ** ADAM_STEP)
        go_ref[...] = gv
        d_ref[...] = -ADAM_LR * (m_hat / (jnp.sqrt(v_hat) + ADAM_EPS) + ADAM_WD * w_ref[...])
        mo_ref[...] = mn
        vo_ref[...] = vn

    return pl.pallas_call(
        body, name=name, grid=(r // tr,),
        in_specs=[_rows(tr, 1024), pl.BlockSpec((tr, 1024), lambda i: (g0 + i, 0)), _rows(tr, 1024), _rows(tr, 1024)],
        out_specs=[_rows(tr, 1024)] * 4,
        out_shape=[jax.ShapeDtypeStruct((r, 1024), F32)] * 4,
        compiler_params=_params(("parallel",)),
    )(w, g, m, v)


def _place():
    x, y, c = lax.axis_index("x"), lax.axis_index("y"), lax.axis_index("c")
    chips = [(1 - x, y), (x, 1 - y), (1 - x, 1 - y)]
    return x, y, c, chips


def _gather_weights(staged):
    nt = len(staged)

    def body(*refs):
        out_refs = refs[nt:2 * nt]
        send_sems, recv_sems = refs[2 * nt:]
        x, y, c, chips = _place()
        j = 2 * x + y
        me, sibling = (x, y, c), (x, y, 1 - c)
        theirs = [2 * cx + cy for cx, cy in chips]

        def copy(t, k, pj, half, to):
            h = SHARD_ROWS[t] // 2
            rows = out_refs[t].at[pl.ds((2 * pj + half) * h, h), :]
            return pltpu.make_async_remote_copy(
                src_ref=rows, dst_ref=rows, send_sem=send_sems.at[6 * t + k], recv_sem=recv_sems.at[6 * t + k],
                device_id=to, device_id_type=MESH)

        first = [copy(t, k, j, c, (*chip, c)) for t in range(nt) for k, chip in enumerate(chips)]
        for cp in first:
            cp.start()
        passed = []
        for t in range(nt):
            for k in range(3):
                copy(t, k, theirs[k], c, me).wait_recv()
                fwd = copy(t, 3 + k, theirs[k], c, sibling)
                fwd.start()
                passed.append(fwd)
        for t in range(nt):
            for k in range(3):
                copy(t, 3 + k, theirs[k], 1 - c, me).wait_recv()
        for cp in first + passed:
            cp.wait_send()

    return pl.pallas_call(
        body, name="gather_weights",
        in_specs=[_ANY] * nt, out_specs=[_ANY] * nt,
        out_shape=[jax.ShapeDtypeStruct(a.shape, a.dtype) for a in staged],
        scratch_shapes=[pltpu.SemaphoreType.DMA((6 * nt,)), pltpu.SemaphoreType.DMA((6 * nt,))],
        input_output_aliases={t: t for t in range(nt)},
    )(*staged)


def _pair_exchange(gb):
    def body(g_ref, out_ref, send_sems, recv_sems):
        x, y, c, _ = _place()
        cps = [pltpu.make_async_remote_copy(
            src_ref=g_ref.at[pl.ds((2 * pj + 1 - c) * HALF_ROWS, HALF_ROWS), :],
            dst_ref=out_ref.at[pl.ds(pj * HALF_ROWS, HALF_ROWS), :],
            send_sem=send_sems.at[pj], recv_sem=recv_sems.at[pj], device_id=(x, y, 1 - c), device_id_type=MESH)
            for pj in range(N_CHIPS)]
        for cp in cps:
            cp.start()
        for cp in cps:
            cp.wait()

    return pl.pallas_call(
        body, name="pair_exchange", in_specs=[_ANY], out_specs=_ANY,
        out_shape=jax.ShapeDtypeStruct((N_CHIPS * HALF_ROWS, 1024), gb.dtype),
        scratch_shapes=[pltpu.SemaphoreType.DMA((N_CHIPS,)), pltpu.SemaphoreType.DMA((N_CHIPS,))],
    )(gb)


def _to_chips(s16):
    def body(a_ref, out_ref, send_sems, recv_sems):
        x, y, c, chips = _place()
        cps = [pltpu.make_async_remote_copy(
            src_ref=a_ref.at[pl.ds((2 * cx + cy) * HALF_ROWS, HALF_ROWS), :],
            dst_ref=out_ref.at[pl.ds(k * HALF_ROWS, HALF_ROWS), :],
            send_sem=send_sems.at[k], recv_sem=recv_sems.at[k], device_id=(cx, cy, c), device_id_type=MESH)
            for k, (cx, cy) in enumerate(chips)]
        for cp in cps:
            cp.start()
        for cp in cps:
            cp.wait()

    return pl.pallas_call(
        body, name="grads_to_chips", in_specs=[_ANY], out_specs=_ANY,
        out_shape=jax.ShapeDtypeStruct((3 * HALF_ROWS, 1024), s16.dtype),
        scratch_shapes=[pltpu.SemaphoreType.DMA((3,)), pltpu.SemaphoreType.DMA((3,))],
    )(s16)


def _share_halves(shard):
    def body(in_ref, out_ref, send_sem, recv_sem):
        x, y, c, _ = _place()

        def copy(half):
            rows = out_ref.at[pl.ds(half * HALF_ROWS, HALF_ROWS), :]
            return pltpu.make_async_remote_copy(src_ref=rows, dst_ref=rows, send_sem=send_sem, recv_sem=recv_sem,
                                                device_id=(x, y, 1 - c), device_id_type=MESH)

        mine = copy(c)
        mine.start()
        mine.wait_send()
        copy(1 - c).wait_recv()

    return pl.pallas_call(
        body, name="share_halves", in_specs=[_ANY], out_specs=_ANY,
        out_shape=jax.ShapeDtypeStruct(shard.shape, shard.dtype),
        scratch_shapes=[pltpu.SemaphoreType.DMA, pltpu.SemaphoreType.DMA],
        input_output_aliases={0: 0},
    )(shard)


def _all_reduce_small(v):
    m_per, n = v.shape
    n_dev = 8

    def body(x_ref, out_ref, all_ref, send_sems, recv_sems, local_sem):
        x, y, c, chips = _place()
        me, sibling = (x, y, c), (x, y, 1 - c)

        def rows(px, py, pc):
            return all_ref.at[pl.ds((4 * px + 2 * py + pc) * m_per, m_per), :]

        def copy(k, block, to, src=None):
            return pltpu.make_async_remote_copy(
                src_ref=rows(*block) if src is None else src, dst_ref=rows(*block),
                send_sem=send_sems.at[k], recv_sem=recv_sems.at[k], device_id=to, device_id_type=MESH)

        mine = pltpu.make_async_copy(x_ref, rows(*me), local_sem)
        mine.start()
        first = [copy(0, me, sibling, src=x_ref)]
        first += [copy(1 + k, me, (*chip, c), src=x_ref) for k, chip in enumerate(chips)]
        for cp in first:
            cp.start()
        passed = [copy(4 + k, (*chip, c), sibling) for k, chip in enumerate(chips)]
        for k, chip in enumerate(chips):
            copy(1 + k, (*chip, c), me).wait_recv()
            passed[k].start()
        copy(0, sibling, me).wait_recv()
        for k, chip in enumerate(chips):
            copy(4 + k, (*chip, 1 - c), me).wait_recv()
        for cp in first + passed:
            cp.wait_send()
        mine.wait()
        acc = all_ref[pl.ds(0, m_per), :]
        for d in range(1, n_dev):
            acc = acc + all_ref[pl.ds(d * m_per, m_per), :]
        out_ref[...] = acc

    return pl.pallas_call(
        body, name="all_reduce_small",
        in_specs=[_VMEM], out_specs=_VMEM,
        out_shape=jax.ShapeDtypeStruct((m_per, n), F32),
        scratch_shapes=[pltpu.VMEM((n_dev * m_per, n), F32),
                        pltpu.SemaphoreType.DMA((7,)), pltpu.SemaphoreType.DMA((7,)), pltpu.SemaphoreType.DMA],
        compiler_params=pltpu.CompilerParams(has_side_effects=True),
    )(v)


def _tables(S):
    pos = jnp.arange(S, dtype=jnp.int32).astype(F32)

    def cs(half):
        inv_freq = ROPE_THETA ** (-jnp.arange(half, dtype=F32) / half)
        ang = pos[:, None] * inv_freq[None, :]
        return jnp.cos(ang), jnp.sin(ang)

    c64, s64 = cs(64)
    c32, s32 = cs(32)
    cr = jnp.concatenate([c64, c64], axis=1)
    sr = jnp.concatenate([-s64, s64], axis=1)
    ca = jnp.concatenate([c32, c32, c32, c32], axis=1)
    sa = jnp.concatenate([-s32, s32, -s32, s32], axis=1)
    return cr, sr, ca, sa


def _decays():
    H, C = RET_HEADS, SEQ_CHUNK
    log_gamma = jnp.log1p(-jnp.exp2(-5.0 - jnp.arange(H, dtype=F32)))
    idx = jnp.arange(C, dtype=F32)
    rel = idx[:, None] - idx[None, :]
    dmat = jnp.where(rel[None] >= 0, jnp.exp(log_gamma[:, None, None] * jnp.maximum(rel, 0.0)[None]), 0.0)
    qd = jnp.exp(log_gamma[:, None] * (idx + 1.0))[:, :, None]
    kd = jnp.exp(log_gamma[:, None] * (C - 1.0 - idx))[:, :, None]
    cd = jnp.exp(log_gamma * C)[:, None, None]
    return dmat, qd, kd, cd


def _local_step(x, tgt, ln1_g, b_in, ret_norm_g, attn_sinks, ln2_g, lnf_g, full_w):
    w_in, w_ret, w_attn, w_out, w_gate, w_up, w_down = full_w
    S = x.shape[0]
    cr, sr, ca, sa = _tables(S)
    dmat, qd, kd, cd = _decays()

    h, rq, rk, rv, rg, aq, ak, av, ga, gb = _proj_fwd(x, ln1_g, w_in, b_in, cr, sr, ca, sa)
    ry, states = _ret_fwd(rq, rk, rv, dmat, qd, kd, cd)
    ay = _attn_fwd(aq, ak, av, attn_sinks)
    u, ba, bb, merged, x1 = _merge_fwd(x, ry, rg, ay, ga, gb, ret_norm_g, w_ret, w_attn, w_out)
    gate, up, h2, act, dx2, loss, dg3 = _ffn_fwd(x1, tgt, ln2_g, lnf_g, w_gate, w_up, w_down)

    dgate, dup, dx1, dg2 = _ffn_bwd(dx2, gate, up, x1, ln2_g, w_gate, w_up, w_down)
    packed = _wgrad([act], dx2, 6, None, "wgrad_down")
    packed = _wgrad([dgate], h2, 4, packed, "wgrad_gate")
    packed = _wgrad([dup], h2, 5, packed, "wgrad_up")
    dgates, drg, dba, dbb, day, dry, dgr, db_mid = _merge_bwd(dx1, ga, gb, ba, bb, rg, ry, ret_norm_g,
                                                                w_ret, w_attn, w_out)
    packed = _wgrad([merged], dx1, 3, packed, "wgrad_out")
    packed = _wgrad([u], dba, 1, packed, "wgrad_ret")
    packed = _wgrad([ay], dbb, 2, packed, "wgrad_attn")
    daq, dak, dav, dsk = _attn_bwd(aq, ak, av, day, attn_sinks, ca, sa)
    d_ret = _ret_bwd(rq, rk, rv, dry, states, dmat, qd, kd, cd, cr, sr)
    d_proj = [d_ret, drg, daq, dak, dav, dgates]
    grad_x, dg1, db = _proj_bwd(d_proj, x, dx1, ln1_g, w_in)
    packed = _wgrad(d_proj, h, 0, packed, "wgrad_in")
    d_sinks = dsk.reshape(2, 2, PAIRS, 128).sum(-1).transpose(0, 2, 1).reshape(1, ATTN_Q_HEADS)
    small = dict(loss=loss[0, 0], ln1_g=dg1, b_in=db, ret_norm_g=dgr, attn_sinks=d_sinks,
                 ln2_g=dg2, lnf_g=dg3)
    return grad_x, packed, small


SMALL_ROWS = 16


def _pack_small(ln1, ret, ln2, lnf, b_in, sinks, loss):
    b = jnp.pad(b_in.reshape(-1), (0, 7 * 1024 - D_IN)).reshape(7, 1024)
    s = jnp.pad(sinks.reshape(-1), (0, 1024 - ATTN_Q_HEADS)).reshape(1, 1024)
    l = jnp.pad(loss.reshape(1), (0, 1023)).reshape(1, 1024)
    z = jnp.zeros((SMALL_ROWS - 13, 1024), F32)
    return jnp.concatenate([ln1.reshape(1, 1024), ret.reshape(1, 1024), ln2.reshape(1, 1024),
                            lnf.reshape(1, 1024), b, s, l, z], axis=0)


def _unpack_small(p):
    return dict(ln1_g=p[0:1], ret_norm_g=p[1:2], ln2_g=p[2:3], lnf_g=p[3], b_in=p[4:11].reshape(1, -1)[:, :D_IN],
                attn_sinks=p[11:12, :ATTN_Q_HEADS], loss=p[12, 0])


def kernel(x, ln1_g, w_in, b_in, ret_norm_g, w_ret_out, attn_sinks, w_attn_out, w_out, ln2_g, w_ffn_gate, w_ffn_up, w_ffn_down, lnf_g, loss_target, m_ln1_g, m_w_in, m_b_in, m_ret_norm_g, m_w_ret_out, m_attn_sinks, m_w_attn_out, m_w_out, m_ln2_g, m_w_ffn_gate, m_w_ffn_up, m_w_ffn_down, m_lnf_g, v_ln1_g, v_w_in, v_b_in, v_ret_norm_g, v_w_ret_out, v_attn_sinks, v_w_attn_out, v_w_out, v_ln2_g, v_w_ffn_gate, v_w_ffn_up, v_w_ffn_down, v_lnf_g):
    xi = lax.axis_index("x")
    yi = lax.axis_index("y")
    ci = lax.axis_index("c")
    j = 2 * xi + yi

    place = jnp.stack([j, ci]).astype(jnp.int32)

    names = ["w_in", "w_ret_out", "w_attn_out", "w_out", "w_ffn_gate", "w_ffn_up", "w_ffn_down"]
    transposed = [True, False, False, False, True, True, False]
    view = lambda a, t: a[0].T if t else a[0]
    big_w = [view(a, t) for a, t in zip([w_in, w_ret_out, w_attn_out, w_out, w_ffn_gate, w_ffn_up, w_ffn_down], transposed)]
    big_m = [view(a, t) for a, t in zip([m_w_in, m_w_ret_out, m_w_attn_out, m_w_out, m_w_ffn_gate, m_w_ffn_up, m_w_ffn_down], transposed)]
    big_v = [view(a, t) for a, t in zip([v_w_in, v_w_ret_out, v_w_attn_out, v_w_out, v_w_ffn_gate, v_w_ffn_up, v_w_ffn_down], transposed)]

    staged = [_stage_weight(place, w, "stage_" + n) for n, w in zip(names, big_w)]
    full_w = _gather_weights(staged)

    grad_x, (g32, g16), small = _local_step(x[0], loss_target[0], ln1_g, b_in, ret_norm_g, attn_sinks, ln2_g,
                                            lnf_g.reshape(1, -1), full_w)

    got = _pair_exchange(g16)
    s32, s16 = _pair_sum(place, g32, got)
    others = _to_chips(s16)
    g_shard = _share_halves(_chip_sum(place, s32, others))

    sm = _unpack_small(_all_reduce_small(_pack_small(
        small["ln1_g"], small["ret_norm_g"], small["ln2_g"], small["lnf_g"], small["b_in"],
        small["attn_sinks"], small["loss"])))

    upd = {}
    for t, (n, w, m, v) in enumerate(zip(names, big_w, big_m, big_v)):
        res = _adamw(w, g_shard, m, v, "adamw_" + n, g_row0=SHARD_OFFS[t])
        upd[n] = tuple((r.T if transposed[t] else r)[None] for r in res)

    z = jnp.zeros((), F32)
    pw = _pack_small(ln1_g, ret_norm_g, ln2_g, lnf_g, b_in, attn_sinks, z)
    pm = _pack_small(m_ln1_g, m_ret_norm_g, m_ln2_g, m_lnf_g, m_b_in, m_attn_sinks, z)
    pv = _pack_small(v_ln1_g, v_ret_norm_g, v_ln2_g, v_lnf_g, v_b_in, v_attn_sinks, z)
    pg = _pack_small(sm["ln1_g"], sm["ret_norm_g"], sm["ln2_g"], sm["lnf_g"], sm["b_in"], sm["attn_sinks"], z)
    sd, smn, svn = (_unpack_small(t) for t in _adamw(pw, pg, pm, pv, "adamw_small")[1:])
    for n in ["ln1_g", "b_in", "ret_norm_g", "attn_sinks", "ln2_g", "lnf_g"]:
        upd[n] = (sm[n], sd[n], smn[n], svn[n])

    order_w = ["ln1_g", "w_in", "b_in", "ret_norm_g", "w_ret_out", "attn_sinks", "w_attn_out", "w_out",
               "ln2_g", "w_ffn_gate", "w_ffn_up", "w_ffn_down", "lnf_g"]
    outs = [sm["loss"], grad_x[None]]
    for k in range(4):
        outs += [upd[n][k] for n in order_w]
    return tuple(outs)
```

```python
import functools

import jax
import jax.numpy as jnp
from jax import lax
from jax.experimental import pallas as pl
from jax.experimental.pallas import tpu as pltpu

D_MODEL = 1024
SEQ_CHUNK = 128
RET_HEADS = 4
RET_QK_DIM = 128
RET_V_DIM = 256
ATTN_Q_HEADS = 16
ATTN_HEAD_DIM = 64
D_FF = 2816
D_IN = 6400
ROPE_THETA = 10000.0
EPS = 1e-6
N_CHIPS = 4

ADAM_LR = 0.001
ADAM_B1 = 0.9
ADAM_B2 = 0.999
ADAM_EPS = 1e-08
ADAM_WD = 0.01
ADAM_STEP = 10

OFF_RQ, OFF_RK, OFF_RV, OFF_RG = 0, 512, 1024, 2048
OFF_AQ, OFF_AK, OFF_AV, OFF_GA, OFF_GB = 3072, 4096, 4224, 4352, 5376

V7X_VMEM_LIMIT = 56 * 1024 * 1024
LANES = 128

BF = jnp.bfloat16
F32 = jnp.float32
MESH = pl.DeviceIdType.MESH

_VMEM = pl.BlockSpec(memory_space=pltpu.VMEM)
_SMEM = pl.BlockSpec(memory_space=pltpu.SMEM)
_ANY = pl.BlockSpec(memory_space=pl.ANY)


def _dot(a, b):
    return jnp.dot(a, b, preferred_element_type=F32)


def _dot_nt(a, b):
    return lax.dot_general(a, b, (((1,), (1,)), ((), ())), preferred_element_type=F32)


def _dot_tn(a, b):
    return lax.dot_general(a, b, (((0,), (0,)), ((), ())), preferred_element_type=F32)


def _sigmoid(x):
    return 1.0 / (1.0 + jnp.exp(-x))


def _rows(tm, n):
    return pl.BlockSpec((tm, n), lambda i: (i, 0))


def _fixed(shape):
    nd = len(shape)
    return pl.BlockSpec(shape, lambda i: (0,) * nd)


def _params(sem, **kw):
    return pltpu.CompilerParams(dimension_semantics=sem, vmem_limit_bytes=V7X_VMEM_LIMIT, **kw)


def _rot_ret(p, cr, sr):
    return p * cr + pltpu.roll(p, 64, 1) * sr


def _rot_ret_bwd(d, cr, sr):
    return d * cr - pltpu.roll(d, 64, 1) * sr


def _attn_partner(p):
    lane = lax.broadcasted_iota(jnp.int32, p.shape, 1)
    first = (lane % 64) < 32
    return jnp.where(first, pltpu.roll(p, 96, 1), pltpu.roll(p, 32, 1))


def _rot_attn(p, ca, sa):
    return p * ca + _attn_partner(p) * sa


def _rot_attn_bwd(d, ca, sa):
    return d * ca - _attn_partner(d) * sa


def _proj_fwd(x, g1, w_in, b_in, cr, sr, ca, sa):
    S = x.shape[0]
    tm = 512
    scale = RET_QK_DIM ** -0.5

    def body(x_ref, g_ref, w_ref, b_ref, cr_ref, sr_ref, ca_ref, sa_ref,
             h_ref, rq_ref, rk_ref, rv_ref, rg_ref, aq_ref, ak_ref, av_ref, ga_ref, gb_ref):
        xv = x_ref[...]
        r = lax.rsqrt(jnp.mean(xv * xv, axis=-1, keepdims=True) + EPS)
        h = (xv * r * g_ref[...]).astype(BF)
        h_ref[...] = h

        def piece(lo, n):
            return _dot_nt(h, w_ref[lo:lo + n, :]) + b_ref[:, lo:lo + n]

        crv, srv = cr_ref[...], sr_ref[...]
        pq = piece(OFF_RQ, 512)
        pk = piece(OFF_RK, 512)
        for hd in range(RET_HEADS):
            sl = slice(hd * 128, (hd + 1) * 128)
            rq_ref[:, sl] = (_rot_ret(pq[:, sl], crv, srv) * scale).astype(BF)
            rk_ref[:, sl] = _rot_ret(pk[:, sl], crv, srv)
        rv_ref[...] = piece(OFF_RV, 1024).astype(BF)
        rg_ref[...] = piece(OFF_RG, 1024)
        cav, sav = ca_ref[...], sa_ref[...]
        pa = piece(OFF_AQ, 1024)
        for blk in range(8):
            sl = slice(blk * 128, (blk + 1) * 128)
            aq_ref[:, sl] = _rot_attn(pa[:, sl], cav, sav).astype(BF)
        pkv = piece(OFF_AK, 256)
        ak_ref[...] = _rot_attn(pkv[:, :128], cav, sav).astype(BF)
        av_ref[...] = pkv[:, 128:].astype(BF)
        ga_ref[...] = piece(OFF_GA, 1024)
        gb_ref[...] = piece(OFF_GB, 1024)

    outs = [(D_MODEL, BF), (512, BF), (512, F32), (1024, BF), (1024, F32),
            (1024, BF), (128, BF), (128, BF), (1024, F32), (1024, F32)]
    return pl.pallas_call(
        body, name="proj_fwd", grid=(S // tm,),
        in_specs=[_rows(tm, D_MODEL), _fixed((1, D_MODEL)), _VMEM, _VMEM,
                  _rows(tm, 128), _rows(tm, 128), _rows(tm, 128), _rows(tm, 128)],
        out_specs=[_rows(tm, n) for n, _ in outs],
        out_shape=[jax.ShapeDtypeStruct((S, n), dt) for n, dt in outs],
        compiler_params=_params(("parallel",)),
    )(x, g1, w_in, b_in, cr, sr, ca, sa)


def _ret_fwd(rq, rk, rv, dmat, qd, kd, cd):
    S = rq.shape[0]
    N = S // SEQ_CHUNK

    def body(q_ref, k_ref, v_ref, dm_ref, qd_ref, kd_ref, cd_ref, y_ref, st_ref, state):
        @pl.when(pl.program_id(0) == 0)
        def _():
            state[...] = jnp.zeros_like(state)

        for hd in range(RET_HEADS):
            qs = slice(hd * 128, (hd + 1) * 128)
            vs = slice(hd * 256, (hd + 1) * 256)
            q = q_ref[:, qs]
            kf = k_ref[:, qs]
            k = kf.astype(BF)
            v = v_ref[:, vs]
            s = state[hd]
            sb = s.astype(BF)
            st_ref[0, hd] = sb
            a = _dot_nt(q, k) * dm_ref[hd]
            inner = _dot(a.astype(BF), v)
            cross = _dot(q, sb) * qd_ref[hd]
            y_ref[:, vs] = inner + cross
            kdv = (kf * kd_ref[hd]).astype(BF)
            state[hd] = s * cd_ref[hd] + _dot_tn(kdv, v)

    return pl.pallas_call(
        body, name="ret_fwd", grid=(N,),
        in_specs=[_rows(128, 512), _rows(128, 512), _rows(128, 1024), _VMEM, _VMEM, _VMEM, _VMEM],
        out_specs=[_rows(128, 1024), pl.BlockSpec((1, RET_HEADS, 128, 256), lambda i: (i, 0, 0, 0))],
        out_shape=[jax.ShapeDtypeStruct((S, 1024), F32),
                   jax.ShapeDtypeStruct((N, RET_HEADS, 128, 256), BF)],
        scratch_shapes=[pltpu.VMEM((RET_HEADS, 128, 256), F32)],
        compiler_params=_params(("arbitrary",)),
    )(rq, rk, rv, dmat, qd, kd, cd)


def _kv_halves(prev_ref, cur_ref):
    kk = jnp.concatenate([prev_ref[...], cur_ref[...]], axis=0).astype(F32)
    return kk, pltpu.roll(kk, 64, 1)


PAIRS = 4


def _attn_mask(n):
    kj = lax.broadcasted_iota(jnp.int32, (256, PAIRS * 128), 0)
    qi = lax.broadcasted_iota(jnp.int32, (256, PAIRS * 128), 1) & 127
    first_prev = jnp.where(n > 0, 0, 128)
    prev_ok = jnp.logical_and(jnp.logical_and(kj < 128, kj >= first_prev), qi < kj)
    cur_ok = jnp.logical_and(kj >= 128, qi >= kj - 128)
    return jnp.logical_or(prev_ok, cur_ok)


def _group_operands(g, kk, kk_sw, vv, vv_sw):
    lo = lax.broadcasted_iota(jnp.int32, (256, 128), 1) < 64
    k_a, k_b = (kk, kk_sw) if g == 0 else (kk_sw, kk)
    v_a, v_b = (vv, vv_sw) if g == 0 else (vv_sw, vv)
    return (jnp.where(lo, k_a, 0.0).astype(BF), jnp.where(lo, 0.0, k_b).astype(BF),
            jnp.where(lo, v_a, 0.0).astype(BF), jnp.where(lo, 0.0, v_b).astype(BF))


def _stack_pairs(ref, g):
    return jnp.concatenate([ref[:, (4 * g + p) * 128:(4 * g + p + 1) * 128] for p in range(PAIRS)], axis=0)


def _sink_row(sk_ref, g, which):
    return jnp.concatenate([jnp.full((1, 128), sk_ref[0, 8 * g + 2 * p + which], F32) for p in range(PAIRS)], axis=1)


def _attn_probs(qs, kx, allowed, sink):
    s = _dot_nt(kx, qs) * (ATTN_HEAD_DIM ** -0.5)
    s = jnp.where(allowed, s, -1e30)
    m = jnp.maximum(jnp.max(s, axis=0, keepdims=True), sink)
    e = jnp.exp(s - m)
    es = jnp.exp(sink - m)
    inv = 1.0 / (jnp.sum(e, axis=0, keepdims=True) + es)
    return e * inv, es * inv


def _attn_fwd(aq, ak, av, sinks):
    S = aq.shape[0]
    N = S // SEQ_CHUNK

    def body(sk_ref, q_ref, kp_ref, kc_ref, vp_ref, vc_ref, o_ref):
        kk, kk_sw = _kv_halves(kp_ref, kc_ref)
        vv, vv_sw = _kv_halves(vp_ref, vc_ref)
        allowed = _attn_mask(pl.program_id(0))
        for g in range(2):
            k_lo, k_hi, v_lo, v_hi = _group_operands(g, kk, kk_sw, vv, vv_sw)
            qs = _stack_pairs(q_ref, g)
            p0, _ = _attn_probs(qs, k_lo, allowed, _sink_row(sk_ref, g, 0))
            p1, _ = _attn_probs(qs, k_hi, allowed, _sink_row(sk_ref, g, 1))
            o = _dot_tn(p0.astype(BF), v_lo) + _dot_tn(p1.astype(BF), v_hi)
            for p in range(PAIRS):
                o_ref[:, (4 * g + p) * 128:(4 * g + p + 1) * 128] = o[p * 128:(p + 1) * 128].astype(BF)

    prev = lambda i: (jnp.maximum(i - 1, 0), 0)
    return pl.pallas_call(
        body, name="attn_fwd", grid=(N,),
        in_specs=[_SMEM, _rows(128, 1024), pl.BlockSpec((128, 128), prev), _rows(128, 128),
                  pl.BlockSpec((128, 128), prev), _rows(128, 128)],
        out_specs=_rows(128, 1024),
        out_shape=jax.ShapeDtypeStruct((S, 1024), BF),
        compiler_params=_params(("parallel",)),
    )(sinks, aq, ak, ak, av, av)


def _group_norm(y):
    mu = jnp.mean(y, axis=-1, keepdims=True)
    yc = y - mu
    rstd = lax.rsqrt(jnp.mean(yc * yc, axis=-1, keepdims=True) + EPS)
    return yc * rstd, rstd


def _merge_fwd(x, ry, rg, ay, ga, gb, g_ret, w_ret, w_attn, w_out):
    S = x.shape[0]
    tm = 256

    def body(x_ref, ry_ref, rg_ref, ay_ref, ga_ref, gb_ref, gr_ref, wr_ref, wa_ref, wo_ref,
             u_ref, ba_ref, bb_ref, mg_ref, x1_ref):
        for hd in range(RET_HEADS):
            vs = slice(hd * 256, (hd + 1) * 256)
            yn, _ = _group_norm(ry_ref[:, vs])
            rgv = rg_ref[:, vs]
            u_ref[:, vs] = (rgv * _sigmoid(rgv) * (yn * gr_ref[:, vs])).astype(BF)
        ba = _dot(u_ref[...], wr_ref[...])
        bb = _dot(ay_ref[...], wa_ref[...])
        ba_ref[...] = ba
        bb_ref[...] = bb
        merged = (_sigmoid(ga_ref[...]) * ba + _sigmoid(gb_ref[...]) * bb).astype(BF)
        mg_ref[...] = merged
        x1_ref[...] = x_ref[...] + _dot(merged, wo_ref[...])

    outs = [BF, F32, F32, BF, F32]
    return pl.pallas_call(
        body, name="merge_fwd", grid=(S // tm,),
        in_specs=[_rows(tm, 1024)] * 6 + [_fixed((1, 1024)), _VMEM, _VMEM, _VMEM],
        out_specs=[_rows(tm, 1024)] * 5,
        out_shape=[jax.ShapeDtypeStruct((S, 1024), dt) for dt in outs],
        compiler_params=_params(("parallel",)),
    )(x, ry, rg, ay, ga, gb, g_ret, w_ret, w_attn, w_out)


def _ffn_fwd(x1, tgt, g2, g3, w_gate, w_up, w_down):
    S = x1.shape[0]
    tm = 256

    def body(x1_ref, t_ref, g2_ref, g3_ref, wg_ref, wu_ref, wd_ref,
             gate_ref, up_ref, h2_ref, act_ref, dx2_ref, loss_ref, dg3_ref):
        @pl.when(pl.program_id(0) == 0)
        def _():
            loss_ref[...] = jnp.zeros_like(loss_ref)
            dg3_ref[...] = jnp.zeros_like(dg3_ref)

        x1v = x1_ref[...]
        r2 = lax.rsqrt(jnp.mean(x1v * x1v, axis=-1, keepdims=True) + EPS)
        h2 = (x1v * r2 * g2_ref[...]).astype(BF)
        h2_ref[...] = h2
        gate = _dot_nt(h2, wg_ref[...])
        up = _dot_nt(h2, wu_ref[...])
        gate_ref[...] = gate
        up_ref[...] = up
        act = (gate * _sigmoid(gate) * up).astype(BF)
        act_ref[...] = act
        x2 = x1v + _dot(act, wd_ref[...])
        r3 = lax.rsqrt(jnp.mean(x2 * x2, axis=-1, keepdims=True) + EPS)
        xh = x2 * r3
        g3v = g3_ref[...]
        err = xh * g3v - t_ref[...]
        loss_ref[...] += 0.5 * jnp.sum(jnp.mean(err * err, axis=-1, keepdims=True), axis=0, keepdims=True)
        dy = err * (1.0 / D_MODEL)
        dg3_ref[...] += jnp.sum(dy * xh, axis=0, keepdims=True)
        dxh = dy * g3v
        dx2_ref[...] = r3 * (dxh - xh * jnp.mean(dxh * xh, axis=-1, keepdims=True))

    return pl.pallas_call(
        body, name="ffn_fwd", grid=(S // tm,),
        in_specs=[_rows(tm, 1024), _rows(tm, 1024), _fixed((1, 1024)), _fixed((1, 1024)), _VMEM, _VMEM, _VMEM],
        out_specs=[_rows(tm, D_FF), _rows(tm, D_FF), _rows(tm, 1024), _rows(tm, D_FF), _rows(tm, 1024),
                   _fixed((1, LANES)), _fixed((1, 1024))],
        out_shape=[jax.ShapeDtypeStruct((S, D_FF), F32), jax.ShapeDtypeStruct((S, D_FF), F32),
                   jax.ShapeDtypeStruct((S, 1024), BF), jax.ShapeDtypeStruct((S, D_FF), BF),
                   jax.ShapeDtypeStruct((S, 1024), F32),
                   jax.ShapeDtypeStruct((1, LANES), F32), jax.ShapeDtypeStruct((1, 1024), F32)],
        compiler_params=_params(("arbitrary",)),
    )(x1, tgt, g2, g3, w_gate, w_up, w_down)


def _ffn_bwd(dx2, gate, up, x1, g2, w_gate, w_up, w_down):
    S = x1.shape[0]
    tm = 256

    def body(dx2_ref, gate_ref, up_ref, x1_ref, g2_ref, wg_ref, wu_ref, wd_ref,
             dgate_ref, dup_ref, dx1_ref, dg2_ref):
        @pl.when(pl.program_id(0) == 0)
        def _():
            dg2_ref[...] = jnp.zeros_like(dg2_ref)

        dx2v = dx2_ref[...]
        dact = _dot_nt(dx2v.astype(BF), wd_ref[...])
        gate = gate_ref[...]
        sg = _sigmoid(gate)
        dgate = (dact * up_ref[...] * (sg * (1.0 + gate * (1.0 - sg)))).astype(BF)
        dup = (dact * (gate * sg)).astype(BF)
        dgate_ref[...] = dgate
        dup_ref[...] = dup
        dh2 = _dot(dgate, wg_ref[...]) + _dot(dup, wu_ref[...])
        x1v = x1_ref[...]
        r2 = lax.rsqrt(jnp.mean(x1v * x1v, axis=-1, keepdims=True) + EPS)
        xh = x1v * r2
        dg2_ref[...] += jnp.sum(dh2 * xh, axis=0, keepdims=True)
        dxh = dh2 * g2_ref[...]
        dx1_ref[...] = dx2v + r2 * (dxh - xh * jnp.mean(dxh * xh, axis=-1, keepdims=True))

    return pl.pallas_call(
        body, name="ffn_bwd", grid=(S // tm,),
        in_specs=[_rows(tm, 1024), _rows(tm, D_FF), _rows(tm, D_FF), _rows(tm, 1024), _fixed((1, 1024)),
                  _VMEM, _VMEM, _VMEM],
        out_specs=[_rows(tm, D_FF), _rows(tm, D_FF), _rows(tm, 1024), _fixed((1, 1024))],
        out_shape=[jax.ShapeDtypeStruct((S, D_FF), BF), jax.ShapeDtypeStruct((S, D_FF), BF),
                   jax.ShapeDtypeStruct((S, 1024), F32), jax.ShapeDtypeStruct((1, 1024), F32)],
        compiler_params=_params(("arbitrary",)),
    )(dx2, gate, up, x1, g2, w_gate, w_up, w_down)


def _merge_bwd(dx1, ga, gb, ba, bb, rg, ry, g_ret, w_ret, w_attn, w_out):
    S = dx1.shape[0]
    tm = 256

    def body(dx1_ref, ga_ref, gb_ref, ba_ref, bb_ref, rg_ref, ry_ref, gr_ref, wr_ref, wa_ref, wo_ref,
             dgates_ref, drg_ref, dba_ref, dbb_ref, day_ref, dry_ref, dgr_ref, db_ref):
        @pl.when(pl.program_id(0) == 0)
        def _():
            dgr_ref[...] = jnp.zeros_like(dgr_ref)
            db_ref[...] = jnp.zeros_like(db_ref)

        dm = _dot_nt(dx1_ref[...].astype(BF), wo_ref[...])
        sa = _sigmoid(ga_ref[...])
        sb = _sigmoid(gb_ref[...])
        dga = (dm * ba_ref[...] * (sa * (1.0 - sa))).astype(BF)
        dgb = (dm * bb_ref[...] * (sb * (1.0 - sb))).astype(BF)
        dgates_ref[:, :1024] = dga
        dgates_ref[:, 1024:] = dgb
        dba = (dm * sa).astype(BF)
        dbb = (dm * sb).astype(BF)
        dba_ref[...] = dba
        dbb_ref[...] = dbb
        day_ref[...] = _dot_nt(dbb, wa_ref[...]).astype(BF)
        du = _dot_nt(dba, wr_ref[...])
        for hd in range(RET_HEADS):
            vs = slice(hd * 256, (hd + 1) * 256)
            yn, rstd = _group_norm(ry_ref[:, vs])
            rgv = rg_ref[:, vs]
            sr = _sigmoid(rgv)
            silu = rgv * sr
            duv = du[:, vs]
            grv = gr_ref[:, vs]
            drg = (duv * (yn * grv) * (sr * (1.0 + rgv * (1.0 - sr)))).astype(BF)
            drg_ref[:, vs] = drg
            dyg = duv * silu
            dgr_ref[:, vs] += jnp.sum(dyg * yn, axis=0, keepdims=True)
            dyn = dyg * grv
            dry_ref[:, vs] = rstd * (dyn - jnp.mean(dyn, axis=-1, keepdims=True)
                                     - yn * jnp.mean(dyn * yn, axis=-1, keepdims=True))
            db_ref[:, vs] += jnp.sum(drg.astype(F32), axis=0, keepdims=True)
        db_ref[:, 1024:2048] += jnp.sum(dga.astype(F32), axis=0, keepdims=True)
        db_ref[:, 2048:] += jnp.sum(dgb.astype(F32), axis=0, keepdims=True)

    return pl.pallas_call(
        body, name="merge_bwd", grid=(S // tm,),
        in_specs=[_rows(tm, 1024)] * 7 + [_fixed((1, 1024)), _VMEM, _VMEM, _VMEM],
        out_specs=[_rows(tm, 2048), _rows(tm, 1024), _rows(tm, 1024), _rows(tm, 1024), _rows(tm, 1024),
                   _rows(tm, 1024), _fixed((1, 1024)), _fixed((1, 3072))],
        out_shape=[jax.ShapeDtypeStruct((S, 2048), BF), jax.ShapeDtypeStruct((S, 1024), BF),
                   jax.ShapeDtypeStruct((S, 1024), BF), jax.ShapeDtypeStruct((S, 1024), BF),
                   jax.ShapeDtypeStruct((S, 1024), BF), jax.ShapeDtypeStruct((S, 1024), F32),
                   jax.ShapeDtypeStruct((1, 1024), F32), jax.ShapeDtypeStruct((1, 3072), F32)],
        compiler_params=_params(("arbitrary",)),
    )(dx1, ga, gb, ba, bb, rg, ry, g_ret, w_ret, w_attn, w_out)


def _attn_bwd(aq, ak, av, day, sinks, ca, sa):
    S = aq.shape[0]
    N = S // SEQ_CHUNK

    def body(sk_ref, q_ref, kp_ref, kc_ref, vp_ref, vc_ref, do_ref, cq_ref, sq_ref, cp_ref, sp_ref,
             dq_ref, dk_ref, dv_ref, dsk_ref, carry_k, carry_v):
        n = pl.program_id(0)

        @pl.when(n == 0)
        def _():
            carry_k[...] = jnp.zeros_like(carry_k)
            carry_v[...] = jnp.zeros_like(carry_v)
            dsk_ref[...] = jnp.zeros_like(dsk_ref)

        @pl.when(n < N)
        def _():
            kk, kk_sw = _kv_halves(kp_ref, kc_ref)
            vv, vv_sw = _kv_halves(vp_ref, vc_ref)
            lo = lax.broadcasted_iota(jnp.int32, (256, 128), 1) < 64
            allowed = _attn_mask(n)
            cq, sq = cq_ref[...], sq_ref[...]
            dk_groups, dv_groups = [], []
            for g in range(2):
                k_lo, k_hi, v_lo, v_hi = _group_operands(g, kk, kk_sw, vv, vv_sw)
                qs = _stack_pairs(q_ref, g)
                dos = _stack_pairs(do_ref, g)
                dq = jnp.zeros((PAIRS * 128, 128), F32)
                dk_half, dv_half = [], []
                for which, (kx, vx) in enumerate(((k_lo, v_lo), (k_hi, v_hi))):
                    pr, ps = _attn_probs(qs, kx, allowed, _sink_row(sk_ref, g, which))
                    dp = _dot_nt(vx, dos)
                    dr = jnp.sum(pr * dp, axis=0, keepdims=True)
                    ds = (pr * (dp - dr) * (ATTN_HEAD_DIM ** -0.5)).astype(BF)
                    dsk_ref[2 * g + which:2 * g + which + 1, :] -= ps * dr
                    dq = dq + _dot_tn(ds, kx)
                    dk_half.append(_dot(ds, qs))
                    dv_half.append(_dot(pr.astype(BF), dos))
                for p in range(PAIRS):
                    sl = slice((4 * g + p) * 128, (4 * g + p + 1) * 128)
                    dq_ref[:, sl] = _rot_attn_bwd(dq[p * 128:(p + 1) * 128], cq, sq).astype(BF)
                tk = jnp.where(lo, dk_half[0], dk_half[1])
                tv = jnp.where(lo, dv_half[0], dv_half[1])
                dk_groups.append(tk + pltpu.roll(tk, 64, 1))
                dv_groups.append(tv + pltpu.roll(tv, 64, 1))
            dk_full = jnp.where(lo, dk_groups[0], dk_groups[1])
            dv_full = jnp.where(lo, dv_groups[0], dv_groups[1])
            dk_ref[...] = _rot_attn_bwd(carry_k[...] + dk_full[:128], cp_ref[...], sp_ref[...]).astype(BF)
            dv_ref[...] = (carry_v[...] + dv_full[:128]).astype(BF)
            carry_k[...] = dk_full[128:]
            carry_v[...] = dv_full[128:]

        @pl.when(n == N)
        def _():
            dk_ref[...] = _rot_attn_bwd(carry_k[...], cp_ref[...], sp_ref[...]).astype(BF)
            dv_ref[...] = carry_v[...].astype(BF)

    cur = lambda i: (jnp.minimum(i, N - 1), 0)
    prev = lambda i: (jnp.maximum(i - 1, 0), 0)
    blk = lambda w, im: pl.BlockSpec((128, w), im)
    return pl.pallas_call(
        body, name="attn_bwd", grid=(N + 1,),
        in_specs=[_SMEM, blk(1024, cur), blk(128, prev), blk(128, cur), blk(128, prev), blk(128, cur),
                  blk(1024, cur), blk(128, cur), blk(128, cur), blk(128, prev), blk(128, prev)],
        out_specs=[blk(1024, cur), blk(128, prev), blk(128, prev), _fixed((4, PAIRS * 128))],
        out_shape=[jax.ShapeDtypeStruct((S, 1024), BF), jax.ShapeDtypeStruct((S, 128), BF),
                   jax.ShapeDtypeStruct((S, 128), BF), jax.ShapeDtypeStruct((4, PAIRS * 128), F32)],
        scratch_shapes=[pltpu.VMEM((128, 128), F32), pltpu.VMEM((128, 128), F32)],
        compiler_params=_params(("arbitrary",)),
    )(sinks, aq, ak, ak, av, av, day, ca, sa, ca, sa)


def _ret_bwd(rq, rk, rv, dry, states, dmat, qd, kd, cd, cr, sr):
    S = rq.shape[0]
    N = S // SEQ_CHUNK
    scale = RET_QK_DIM ** -0.5

    def body(q_ref, k_ref, v_ref, dy_ref, st_ref, dm_ref, qd_ref, kd_ref, cd_ref, cr_ref, sr_ref,
             out_ref, dstate):
        @pl.when(pl.program_id(0) == 0)
        def _():
            dstate[...] = jnp.zeros_like(dstate)

        crv, srv = cr_ref[...], sr_ref[...]
        for hd in range(RET_HEADS):
            qs = slice(hd * 128, (hd + 1) * 128)
            vs = slice(hd * 256, (hd + 1) * 256)
            q = q_ref[:, qs]
            kf = k_ref[:, qs]
            k = kf.astype(BF)
            kdv = (kf * kd_ref[hd]).astype(BF)
            v = v_ref[:, vs]
            do = dy_ref[:, vs]
            dob = do.astype(BF)
            doq = (do * qd_ref[hd]).astype(BF)
            sn = st_ref[0, hd]
            ds = dstate[hd]
            dsb = ds.astype(BF)
            dmv = dm_ref[hd]
            a = (_dot_nt(q, k) * dmv).astype(BF)
            dv = _dot_tn(a, dob) + _dot(kdv, dsb)
            da = (_dot_nt(dob, v) * dmv).astype(BF)
            dq = _dot(da, k) + _dot_nt(doq, sn)
            dk = _dot_tn(da, q) + _dot_nt(v, dsb) * kd_ref[hd]
            dstate[hd] = ds * cd_ref[hd] + _dot_tn(q, doq)
            out_ref[:, qs] = _rot_ret_bwd(dq * scale, crv, srv).astype(BF)
            out_ref[:, 512 + hd * 128:512 + (hd + 1) * 128] = _rot_ret_bwd(dk, crv, srv).astype(BF)
            out_ref[:, 1024 + hd * 256:1024 + (hd + 1) * 256] = dv.astype(BF)

    rev = lambda w: pl.BlockSpec((128, w), lambda i: (N - 1 - i, 0))
    return pl.pallas_call(
        body, name="ret_bwd", grid=(N,),
        in_specs=[rev(512), rev(512), rev(1024), rev(1024),
                  pl.BlockSpec((1, RET_HEADS, 128, 256), lambda i: (N - 1 - i, 0, 0, 0)),
                  _VMEM, _VMEM, _VMEM, _VMEM, rev(128), rev(128)],
        out_specs=rev(2048),
        out_shape=jax.ShapeDtypeStruct((S, 2048), BF),
        scratch_shapes=[pltpu.VMEM((RET_HEADS, 128, 256), F32)],
        compiler_params=_params(("arbitrary",)),
    )(rq, rk, rv, dry, states, dmat, qd, kd, cd, cr, sr)


def _proj_bwd(d_pieces, x, dx1, g1, w_in_t):
    S = x.shape[0]
    tm = 256
    widths = [d.shape[1] for d in d_pieces]
    assert sum(widths) == D_IN
    nd = len(d_pieces)

    def body(*refs):
        d_refs = refs[:nd]
        x_ref, dx1_ref, g_ref, w_ref, gx_ref, dg_ref, db_ref = refs[nd:]

        @pl.when(pl.program_id(0) == 0)
        def _():
            dg_ref[...] = jnp.zeros_like(dg_ref)
            db_ref[...] = jnp.zeros_like(db_ref)

        dh = jnp.zeros((tm, D_MODEL), F32)
        lo = 0
        for ref, n in zip(d_refs, widths):
            v = ref[...]
            dh = dh + _dot(v, w_ref[lo:lo + n, :])
            db_ref[:, lo:lo + n] += jnp.sum(v.astype(F32), axis=0, keepdims=True)
            lo += n
        xv = x_ref[...]
        r = lax.rsqrt(jnp.mean(xv * xv, axis=-1, keepdims=True) + EPS)
        xh = xv * r
        dg_ref[...] += jnp.sum(dh * xh, axis=0, keepdims=True)
        dxh = dh * g_ref[...]
        gx_ref[...] = dx1_ref[...] + r * (dxh - xh * jnp.mean(dxh * xh, axis=-1, keepdims=True))

    return pl.pallas_call(
        body, name="proj_bwd", grid=(S // tm,),
        in_specs=[_rows(tm, n) for n in widths] + [_rows(tm, 1024), _rows(tm, 1024), _fixed((1, 1024)), _VMEM],
        out_specs=[_rows(tm, 1024), _fixed((1, 1024)), _fixed((1, D_IN))],
        out_shape=[jax.ShapeDtypeStruct((S, 1024), F32), jax.ShapeDtypeStruct((1, 1024), F32),
                   jax.ShapeDtypeStruct((1, D_IN), F32)],
        compiler_params=_params(("arbitrary",)),
    )(*d_pieces, x, dx1, g1, w_in_t)


SHARD_ROWS = (1600, 256, 256, 256, 704, 704, 704)
SHARD_OFFS = (0, 1600, 1856, 2112, 2368, 3072, 3776)
BLOCK_ROWS = 4480
HALF_ROWS = 2240
SUM_TILE = 320


def _wgrad(a_list, b, t, packed, name):
    S = b.shape[0]
    ts = 512
    steps = S // ts
    widths = [a.shape[1] for a in a_list]
    na = len(a_list)
    r, off = SHARD_ROWS[t], SHARD_OFFS[t]
    assert sum(widths) == N_CHIPS * r
    n_in = na + 1 + (2 if packed is not None else 0)

    def body(*refs):
        a_refs, b_ref = refs[:na], refs[na]
        g32_ref, gb_ref, acc, stage, sems = refs[n_in:]
        i = pl.program_id(0)

        @pl.when(i == 0)
        def _():
            acc[...] = jnp.zeros_like(acc)

        bv = b_ref[...].astype(BF)
        lo = 0
        for a_ref, n in zip(a_refs, widths):
            acc[lo:lo + n, :] += _dot_tn(a_ref[...].astype(BF), bv)
            lo += n

        @pl.when(i == steps - 1)
        def _():
            for chip in range(N_CHIPS):
                src = acc.at[chip * r:(chip + 1) * r, :]
                dst = pl.ds(chip * BLOCK_ROWS + off, r)
                c32 = pltpu.make_async_copy(src, g32_ref.at[dst, :], sems.at[0])
                c32.start()
                stage[...] = acc[chip * r:(chip + 1) * r, :].astype(BF)
                c16 = pltpu.make_async_copy(stage, gb_ref.at[dst, :], sems.at[1])
                c16.start()
                c32.wait()
                c16.wait()

    shapes = [jax.ShapeDtypeStruct((N_CHIPS * BLOCK_ROWS, 1024), F32),
              jax.ShapeDtypeStruct((N_CHIPS * BLOCK_ROWS, 1024), BF)]
    operands = list(a_list) + [b]
    in_specs = [_rows(ts, n) for n in widths] + [_rows(ts, 1024)]
    aliases = {}
    if packed is not None:
        operands += list(packed)
        in_specs += [_ANY, _ANY]
        aliases = {na + 1: 0, na + 2: 1}
    return pl.pallas_call(
        body, name=name, grid=(steps,),
        in_specs=in_specs, out_specs=[_ANY, _ANY], out_shape=shapes,
        scratch_shapes=[pltpu.VMEM((N_CHIPS * r, 1024), F32), pltpu.VMEM((r, 1024), BF),
                        pltpu.SemaphoreType.DMA((2,))],
        input_output_aliases=aliases,
        compiler_params=_params(("arbitrary",)),
    )(*operands)


def _stage_weight(place, w, name):
    r = w.shape[0]
    tr = 800 if r == 1600 else r
    nb = r // tr

    def body(p_ref, w_ref, o_ref):
        o_ref[...] = w_ref[...].astype(BF)

    return pl.pallas_call(
        body, name=name,
        grid_spec=pltpu.PrefetchScalarGridSpec(
            num_scalar_prefetch=1, grid=(nb,),
            in_specs=[pl.BlockSpec((tr, 1024), lambda i, p: (i, 0))],
            out_specs=pl.BlockSpec((tr, 1024), lambda i, p: (p[0] * nb + i, 0))),
        out_shape=jax.ShapeDtypeStruct((N_CHIPS * r, 1024), BF),
        compiler_params=_params(("arbitrary",)),
    )(place, w)


def _pair_sum(place, g32, got):
    nb = HALF_ROWS // SUM_TILE

    def body(p_ref, g_ref, r_ref, s32_ref, s16_ref):
        s = g_ref[...] + r_ref[...].astype(F32)
        s32_ref[...] = s
        s16_ref[...] = s.astype(BF)

    flat = pl.BlockSpec((SUM_TILE, 1024), lambda i, p: (i, 0))
    return pl.pallas_call(
        body, name="pair_sum",
        grid_spec=pltpu.PrefetchScalarGridSpec(
            num_scalar_prefetch=1, grid=(N_CHIPS * nb,),
            in_specs=[pl.BlockSpec((SUM_TILE, 1024), lambda i, p: ((2 * (i // nb) + p[1]) * nb + i % nb, 0)), flat],
            out_specs=[flat, flat]),
        out_shape=[jax.ShapeDtypeStruct((N_CHIPS * HALF_ROWS, 1024), F32),
                   jax.ShapeDtypeStruct((N_CHIPS * HALF_ROWS, 1024), BF)],
        compiler_params=_params(("arbitrary",)),
    )(place, g32, got)


def _chip_sum(place, s32, others):
    nb = HALF_ROWS // SUM_TILE

    def body(p_ref, a_ref, r0_ref, r1_ref, r2_ref, o_ref):
        o_ref[...] = ((a_ref[...] + r0_ref[...].astype(F32)) + r1_ref[...].astype(F32)) + r2_ref[...].astype(F32)

    blk = lambda f: pl.BlockSpec((SUM_TILE, 1024), f)
    return pl.pallas_call(
        body, name="chip_sum",
        grid_spec=pltpu.PrefetchScalarGridSpec(
            num_scalar_prefetch=1, grid=(nb,),
            in_specs=[blk(lambda i, p: (p[0] * nb + i, 0)), blk(lambda i, p: (i, 0)),
                      blk(lambda i, p: (nb + i, 0)), blk(lambda i, p: (2 * nb + i, 0))],
            out_specs=blk(lambda i, p: (p[1] * nb + i, 0))),
        out_shape=jax.ShapeDtypeStruct((BLOCK_ROWS, 1024), F32),
        compiler_params=_params(("arbitrary",)),
    )(place, s32, others, others, others)


def _adamw(w, g, m, v, name, g_row0=0):
    r = w.shape[0]
    tr = 320 if r % 320 == 0 and g_row0 % 320 == 0 else (64 if r % 64 == 0 else r)
    assert g_row0 % tr == 0
    g0 = g_row0 // tr

    def body(w_ref, g_ref, m_ref, v_ref, go_ref, d_ref, mo_ref, vo_ref):
        gv = g_ref[...]
        mn = ADAM_B1 * m_ref[...] + (1.0 - ADAM_B1) * gv
        vn = ADAM_B2 * v_ref[...] + (1.0 - ADAM_B2) * (gv * gv)
        m_hat = mn / (1.0 - ADAM_B1 ** ADAM_STEP)
        v_hat = vn / (1.0 - ADAM_B2 ** ADAM_STEP)
        go_ref[...] = gv
        d_ref[...] = -ADAM_LR * (m_hat / (jnp.sqrt(v_hat) + ADAM_EPS) + ADAM_WD * w_ref[...])
        mo_ref[...] = mn
        vo_ref[...] = vn

    return pl.pallas_call(
        body, name=name, grid=(r // tr,),
        in_specs=[_rows(tr, 1024), pl.BlockSpec((tr, 1024), lambda i: (g0 + i, 0)), _rows(tr, 1024), _rows(tr, 1024)],
        out_specs=[_rows(tr, 1024)] * 4,
        out_shape=[jax.ShapeDtypeStruct((r, 1024), F32)] * 4,
        compiler_params=_params(("parallel",)),
    )(w, g, m, v)


def _place():
    x, y, c = lax.axis_index("x"), lax.axis_index("y"), lax.axis_index("c")
    chips = [(1 - x, y), (x, 1 - y), (1 - x, 1 - y)]
    return x, y, c, chips


def _gather_weights(staged, tids):
    nt = len(staged)

    def body(*refs):
        out_refs = refs[nt:2 * nt]
        send_sems, recv_sems = refs[2 * nt:]
        x, y, c, chips = _place()
        j = 2 * x + y
        me, sibling = (x, y, c), (x, y, 1 - c)
        theirs = [2 * cx + cy for cx, cy in chips]

        def copy(t, k, pj, half, to):
            h = SHARD_ROWS[tids[t]] // 2
            rows = out_refs[t].at[pl.ds((2 * pj + half) * h, h), :]
            return pltpu.make_async_remote_copy(
                src_ref=rows, dst_ref=rows, send_sem=send_sems.at[6 * t + k], recv_sem=recv_sems.at[6 * t + k],
                device_id=to, device_id_type=MESH)

        first = [copy(t, k, j, c, (*chip, c)) for t in range(nt) for k, chip in enumerate(chips)]
        for cp in first:
            cp.start()
        passed = []
        for t in range(nt):
            for k in range(3):
                copy(t, k, theirs[k], c, me).wait_recv()
                fwd = copy(t, 3 + k, theirs[k], c, sibling)
                fwd.start()
                passed.append(fwd)
        for t in range(nt):
            for k in range(3):
                copy(t, 3 + k, theirs[k], 1 - c, me).wait_recv()
        for cp in first + passed:
            cp.wait_send()

    return pl.pallas_call(
        body, name="gather_weights",
        in_specs=[_ANY] * nt, out_specs=[_ANY] * nt,
        out_shape=[jax.ShapeDtypeStruct(a.shape, a.dtype) for a in staged],
        scratch_shapes=[pltpu.SemaphoreType.DMA((6 * nt,)), pltpu.SemaphoreType.DMA((6 * nt,))],
        input_output_aliases={t: t for t in range(nt)},
    )(*staged)


_HBM = pl.BlockSpec(memory_space=pltpu.HBM)
_SEM = pl.BlockSpec(memory_space=pltpu.SEMAPHORE)
_DATAFLOW = pltpu.SideEffectType.DATAFLOW_SIDE_EFFECTING


def _rest_copy(refs, tids, t, k, pj, to, send_sems, recv_sems):
    r = SHARD_ROWS[tids[t]]
    rows = refs[t].at[pl.ds(pj * r, r), :]
    return pltpu.make_async_remote_copy(src_ref=rows, dst_ref=rows, send_sem=send_sems.at[3 * t + k],
                                        recv_sem=recv_sems.at[3 * t + k], device_id=to, device_id_type=MESH)


def _gather_start(staged, tids, after):
    nt = len(staged)

    def body(*refs):
        in_refs = refs[:nt]
        send_sems, recv_sems = refs[nt + 1], refs[nt + 2]
        token = refs[-1]
        x, y, c, chips = _place()
        j = 2 * x + y
        for t in range(nt):
            for k, chip in enumerate(chips):
                _rest_copy(in_refs, tids, t, k, j, (*chip, c), send_sems, recv_sems).start()
        token[...] = jnp.zeros_like(token)

    out = pl.pallas_call(
        body, name="gather_start",
        out_shape=(pltpu.SemaphoreType.DMA((3 * nt,)), pltpu.SemaphoreType.DMA((3 * nt,)),
                   *[pltpu.HBM(a.shape, a.dtype) for a in staged], jax.ShapeDtypeStruct((8, LANES), F32)),
        in_specs=[_HBM] * nt + [_ANY], out_specs=(_SEM, _SEM, *[_HBM] * nt, _VMEM),
        input_output_aliases={t: 2 + t for t in range(nt)},
        compiler_params=pltpu.CompilerParams(has_side_effects=_DATAFLOW),
    )(*[pltpu.with_memory_space_constraint(a, pltpu.HBM) for a in staged], after)
    return out[0], out[1], list(out[2:2 + nt]), out[-1]


def _gather_wait(send_sems, recv_sems, in_flight, tids, after):
    nt = len(in_flight)

    def body(*refs):
        in_refs = refs[:nt]
        send_sems, recv_sems = refs[nt], refs[nt + 1]
        x, y, c, chips = _place()
        j = 2 * x + y
        for t in range(nt):
            for k, (cx, cy) in enumerate(chips):
                _rest_copy(in_refs, tids, t, k, j, (cx, cy, c), send_sems, recv_sems).wait_send()
                _rest_copy(in_refs, tids, t, k, 2 * cx + cy, (x, y, c), send_sems, recv_sems).wait_recv()

    out = pl.pallas_call(
        body, name="gather_wait",
        out_shape=tuple(pltpu.HBM(a.shape, a.dtype) for a in in_flight),
        in_specs=[_HBM] * nt + [_SEM, _SEM, _ANY], out_specs=tuple([_HBM] * nt),
        input_output_aliases={t: t for t in range(nt)},
        compiler_params=pltpu.CompilerParams(has_side_effects=_DATAFLOW),
    )(*in_flight, send_sems, recv_sems, after)
    return list(out)


def _pair_exchange(gb):
    def body(g_ref, out_ref, send_sems, recv_sems):
        x, y, c, _ = _place()
        cps = [pltpu.make_async_remote_copy(
            src_ref=g_ref.at[pl.ds((2 * pj + 1 - c) * HALF_ROWS, HALF_ROWS), :],
            dst_ref=out_ref.at[pl.ds(pj * HALF_ROWS, HALF_ROWS), :],
            send_sem=send_sems.at[pj], recv_sem=recv_sems.at[pj], device_id=(x, y, 1 - c), device_id_type=MESH)
            for pj in range(N_CHIPS)]
        for cp in cps:
            cp.start()
        for cp in cps:
            cp.wait()

    return pl.pallas_call(
        body, name="pair_exchange", in_specs=[_ANY], out_specs=_ANY,
        out_shape=jax.ShapeDtypeStruct((N_CHIPS * HALF_ROWS, 1024), gb.dtype),
        scratch_shapes=[pltpu.SemaphoreType.DMA((N_CHIPS,)), pltpu.SemaphoreType.DMA((N_CHIPS,))],
    )(gb)


def _to_chips(s16):
    def body(a_ref, out_ref, send_sems, recv_sems):
        x, y, c, chips = _place()
        cps = [pltpu.make_async_remote_copy(
            src_ref=a_ref.at[pl.ds((2 * cx + cy) * HALF_ROWS, HALF_ROWS), :],
            dst_ref=out_ref.at[pl.ds(k * HALF_ROWS, HALF_ROWS), :],
            send_sem=send_sems.at[k], recv_sem=recv_sems.at[k], device_id=(cx, cy, c), device_id_type=MESH)
            for k, (cx, cy) in enumerate(chips)]
        for cp in cps:
            cp.start()
        for cp in cps:
            cp.wait()

    return pl.pallas_call(
        body, name="grads_to_chips", in_specs=[_ANY], out_specs=_ANY,
        out_shape=jax.ShapeDtypeStruct((3 * HALF_ROWS, 1024), s16.dtype),
        scratch_shapes=[pltpu.SemaphoreType.DMA((3,)), pltpu.SemaphoreType.DMA((3,))],
    )(s16)


def _share_halves(shard):
    def body(in_ref, out_ref, send_sem, recv_sem):
        x, y, c, _ = _place()

        def copy(half):
            rows = out_ref.at[pl.ds(half * HALF_ROWS, HALF_ROWS), :]
            return pltpu.make_async_remote_copy(src_ref=rows, dst_ref=rows, send_sem=send_sem, recv_sem=recv_sem,
                                                device_id=(x, y, 1 - c), device_id_type=MESH)

        mine = copy(c)
        mine.start()
        mine.wait_send()
        copy(1 - c).wait_recv()

    return pl.pallas_call(
        body, name="share_halves", in_specs=[_ANY], out_specs=_ANY,
        out_shape=jax.ShapeDtypeStruct(shard.shape, shard.dtype),
        scratch_shapes=[pltpu.SemaphoreType.DMA, pltpu.SemaphoreType.DMA],
        input_output_aliases={0: 0},
    )(shard)


def _all_reduce_small(v):
    m_per, n = v.shape
    n_dev = 8

    def body(x_ref, out_ref, all_ref, send_sems, recv_sems, local_sem):
        x, y, c, chips = _place()
        me, sibling = (x, y, c), (x, y, 1 - c)

        def rows(px, py, pc):
            return all_ref.at[pl.ds((4 * px + 2 * py + pc) * m_per, m_per), :]

        def copy(k, block, to, src=None):
            return pltpu.make_async_remote_copy(
                src_ref=rows(*block) if src is None else src, dst_ref=rows(*block),
                send_sem=send_sems.at[k], recv_sem=recv_sems.at[k], device_id=to, device_id_type=MESH)

        mine = pltpu.make_async_copy(x_ref, rows(*me), local_sem)
        mine.start()
        first = [copy(0, me, sibling, src=x_ref)]
        first += [copy(1 + k, me, (*chip, c), src=x_ref) for k, chip in enumerate(chips)]
        for cp in first:
            cp.start()
        passed = [copy(4 + k, (*chip, c), sibling) for k, chip in enumerate(chips)]
        for k, chip in enumerate(chips):
            copy(1 + k, (*chip, c), me).wait_recv()
            passed[k].start()
        copy(0, sibling, me).wait_recv()
        for k, chip in enumerate(chips):
            copy(4 + k, (*chip, 1 - c), me).wait_recv()
        for cp in first + passed:
            cp.wait_send()
        mine.wait()
        acc = all_ref[pl.ds(0, m_per), :]
        for d in range(1, n_dev):
            acc = acc + all_ref[pl.ds(d * m_per, m_per), :]
        out_ref[...] = acc

    return pl.pallas_call(
        body, name="all_reduce_small",
        in_specs=[_VMEM], out_specs=_VMEM,
        out_shape=jax.ShapeDtypeStruct((m_per, n), F32),
        scratch_shapes=[pltpu.VMEM((n_dev * m_per, n), F32),
                        pltpu.SemaphoreType.DMA((7,)), pltpu.SemaphoreType.DMA((7,)), pltpu.SemaphoreType.DMA],
        compiler_params=pltpu.CompilerParams(has_side_effects=True),
    )(v)


def _tables(S):
    pos = jnp.arange(S, dtype=jnp.int32).astype(F32)

    def cs(half):
        inv_freq = ROPE_THETA ** (-jnp.arange(half, dtype=F32) / half)
        ang = pos[:, None] * inv_freq[None, :]
        return jnp.cos(ang), jnp.sin(ang)

    c64, s64 = cs(64)
    c32, s32 = cs(32)
    cr = jnp.concatenate([c64, c64], axis=1)
    sr = jnp.concatenate([-s64, s64], axis=1)
    ca = jnp.concatenate([c32, c32, c32, c32], axis=1)
    sa = jnp.concatenate([-s32, s32, -s32, s32], axis=1)
    return cr, sr, ca, sa


def _decays():
    H, C = RET_HEADS, SEQ_CHUNK
    log_gamma = jnp.log1p(-jnp.exp2(-5.0 - jnp.arange(H, dtype=F32)))
    idx = jnp.arange(C, dtype=F32)
    rel = idx[:, None] - idx[None, :]
    dmat = jnp.where(rel[None] >= 0, jnp.exp(log_gamma[:, None, None] * jnp.maximum(rel, 0.0)[None]), 0.0)
    qd = jnp.exp(log_gamma[:, None] * (idx + 1.0))[:, :, None]
    kd = jnp.exp(log_gamma[:, None] * (C - 1.0 - idx))[:, :, None]
    cd = jnp.exp(log_gamma * C)[:, None, None]
    return dmat, qd, kd, cd


def _local_step(x, tgt, ln1_g, b_in, ret_norm_g, attn_sinks, ln2_g, lnf_g, w_in, other_weights):
    S = x.shape[0]
    cr, sr, ca, sa = _tables(S)
    dmat, qd, kd, cd = _decays()

    h, rq, rk, rv, rg, aq, ak, av, ga, gb = _proj_fwd(x, ln1_g, w_in, b_in, cr, sr, ca, sa)
    ry, states = _ret_fwd(rq, rk, rv, dmat, qd, kd, cd)
    ay = _attn_fwd(aq, ak, av, attn_sinks)
    w_ret, w_attn, w_out, w_gate, w_up, w_down = other_weights(ay)
    u, ba, bb, merged, x1 = _merge_fwd(x, ry, rg, ay, ga, gb, ret_norm_g, w_ret, w_attn, w_out)
    gate, up, h2, act, dx2, loss, dg3 = _ffn_fwd(x1, tgt, ln2_g, lnf_g, w_gate, w_up, w_down)

    dgate, dup, dx1, dg2 = _ffn_bwd(dx2, gate, up, x1, ln2_g, w_gate, w_up, w_down)
    packed = _wgrad([act], dx2, 6, None, "wgrad_down")
    packed = _wgrad([dgate], h2, 4, packed, "wgrad_gate")
    packed = _wgrad([dup], h2, 5, packed, "wgrad_up")
    dgates, drg, dba, dbb, day, dry, dgr, db_mid = _merge_bwd(dx1, ga, gb, ba, bb, rg, ry, ret_norm_g,
                                                                w_ret, w_attn, w_out)
    packed = _wgrad([merged], dx1, 3, packed, "wgrad_out")
    packed = _wgrad([u], dba, 1, packed, "wgrad_ret")
    packed = _wgrad([ay], dbb, 2, packed, "wgrad_attn")
    daq, dak, dav, dsk = _attn_bwd(aq, ak, av, day, attn_sinks, ca, sa)
    d_ret = _ret_bwd(rq, rk, rv, dry, states, dmat, qd, kd, cd, cr, sr)
    d_proj = [d_ret, drg, daq, dak, dav, dgates]
    grad_x, dg1, db = _proj_bwd(d_proj, x, dx1, ln1_g, w_in)
    packed = _wgrad(d_proj, h, 0, packed, "wgrad_in")
    d_sinks = dsk.reshape(2, 2, PAIRS, 128).sum(-1).transpose(0, 2, 1).reshape(1, ATTN_Q_HEADS)
    small = dict(loss=loss, ln1_g=dg1, b_in=db, ret_norm_g=dgr, attn_sinks=d_sinks,
                 ln2_g=dg2, lnf_g=dg3)
    return grad_x, packed, small


SMALL_ROWS = 16


def _pack_small(ln1, ret, ln2, lnf, b_in, sinks, loss, name):
    def body(a_ref, b_ref, c_ref, d_ref, bi_ref, sk_ref, ls_ref, o_ref):
        o_ref[...] = jnp.zeros_like(o_ref)
        for row, ref in enumerate((a_ref, b_ref, c_ref, d_ref)):
            o_ref[row:row + 1, :] = ref[...]
        for k in range(6):
            o_ref[4 + k:5 + k, :] = bi_ref[:, k * 1024:(k + 1) * 1024]
        o_ref[10:11, 0:D_IN - 6144] = bi_ref[:, 6144:D_IN]
        o_ref[11:12, 0:ATTN_Q_HEADS] = sk_ref[...]
        o_ref[12:13, 0:LANES] = ls_ref[...]

    return pl.pallas_call(
        body, name=name, in_specs=[_VMEM] * 7, out_specs=_VMEM,
        out_shape=jax.ShapeDtypeStruct((SMALL_ROWS, 1024), F32),
    )(ln1.reshape(1, 1024), ret.reshape(1, 1024), ln2.reshape(1, 1024), lnf.reshape(1, 1024),
      b_in.reshape(1, D_IN), sinks.reshape(1, ATTN_Q_HEADS), loss.reshape(1, LANES))


def _unpack_small(p):
    return dict(ln1_g=p[0:1], ret_norm_g=p[1:2], ln2_g=p[2:3], lnf_g=p[3], b_in=p[4:11].reshape(1, -1)[:, :D_IN],
                attn_sinks=p[11:12, :ATTN_Q_HEADS], loss=p[12, 0])


def kernel(x, ln1_g, w_in, b_in, ret_norm_g, w_ret_out, attn_sinks, w_attn_out, w_out, ln2_g, w_ffn_gate, w_ffn_up, w_ffn_down, lnf_g, loss_target, m_ln1_g, m_w_in, m_b_in, m_ret_norm_g, m_w_ret_out, m_attn_sinks, m_w_attn_out, m_w_out, m_ln2_g, m_w_ffn_gate, m_w_ffn_up, m_w_ffn_down, m_lnf_g, v_ln1_g, v_w_in, v_b_in, v_ret_norm_g, v_w_ret_out, v_attn_sinks, v_w_attn_out, v_w_out, v_ln2_g, v_w_ffn_gate, v_w_ffn_up, v_w_ffn_down, v_lnf_g):
    xi = lax.axis_index("x")
    yi = lax.axis_index("y")
    ci = lax.axis_index("c")
    j = 2 * xi + yi

    place = jnp.stack([j, ci]).astype(jnp.int32)

    names = ["w_in", "w_ret_out", "w_attn_out", "w_out", "w_ffn_gate", "w_ffn_up", "w_ffn_down"]
    transposed = [True, False, False, False, True, True, False]
    view = lambda a, t: a[0].T if t else a[0]
    big_w = [view(a, t) for a, t in zip([w_in, w_ret_out, w_attn_out, w_out, w_ffn_gate, w_ffn_up, w_ffn_down], transposed)]
    big_m = [view(a, t) for a, t in zip([m_w_in, m_w_ret_out, m_w_attn_out, m_w_out, m_w_ffn_gate, m_w_ffn_up, m_w_ffn_down], transposed)]
    big_v = [view(a, t) for a, t in zip([v_w_in, v_w_ret_out, v_w_attn_out, v_w_out, v_w_ffn_gate, v_w_ffn_up, v_w_ffn_down], transposed)]

    staged = [_stage_weight(place, w, "stage_" + n) for n, w in zip(names, big_w)]
    w_in_full = _gather_weights(staged[:1], [0])[0]
    rest = list(range(1, 7))
    send_sems, recv_sems, in_flight, token = _gather_start(staged[1:], rest, w_in_full)
    other_weights = lambda after: _gather_wait(send_sems, recv_sems, in_flight, rest, after)

    grad_x, (g32, g16), small = _local_step(x[0], loss_target[0], ln1_g, b_in + token[0, 0], ret_norm_g,
                                            attn_sinks, ln2_g, lnf_g.reshape(1, -1), w_in_full, other_weights)

    got = _pair_exchange(g16)
    s32, s16 = _pair_sum(place, g32, got)
    others = _to_chips(s16)
    g_shard = _share_halves(_chip_sum(place, s32, others))

    pg = _all_reduce_small(_pack_small(
        small["ln1_g"], small["ret_norm_g"], small["ln2_g"], small["lnf_g"], small["b_in"],
        small["attn_sinks"], small["loss"], "pack_small_grads"))
    sm = _unpack_small(pg)

    upd = {}
    for t, (n, w, m, v) in enumerate(zip(names, big_w, big_m, big_v)):
        res = _adamw(w, g_shard, m, v, "adamw_" + n, g_row0=SHARD_OFFS[t])
        upd[n] = tuple((r.T if transposed[t] else r)[None] for r in res)

    z = jnp.zeros((1, LANES), F32)
    pw = _pack_small(ln1_g, ret_norm_g, ln2_g, lnf_g, b_in, attn_sinks, z, "pack_small_w")
    pm = _pack_small(m_ln1_g, m_ret_norm_g, m_ln2_g, m_lnf_g, m_b_in, m_attn_sinks, z, "pack_small_m")
    pv = _pack_small(v_ln1_g, v_ret_norm_g, v_ln2_g, v_lnf_g, v_b_in, v_attn_sinks, z, "pack_small_v")
    sd, smn, svn = (_unpack_small(t) for t in _adamw(pw, pg, pm, pv, "adamw_small")[1:])
    for n in ["ln1_g", "b_in", "ret_norm_g", "attn_sinks", "ln2_g", "lnf_g"]:
        upd[n] = (sm[n], sd[n], smn[n], svn[n])

    order_w = ["ln1_g", "w_in", "b_in", "ret_norm_g", "w_ret_out", "attn_sinks", "w_attn_out", "w_out",
               "ln2_g", "w_ffn_gate", "w_ffn_up", "w_ffn_down", "lnf_g"]
    outs = [sm["loss"], grad_x[None]]
    for k in range(4):
        outs += [upd[n][k] for n in order_w]
    return tuple(outs)
```

```python
from typing import NamedTuple

import jax
import jax.numpy as jnp
from jax import lax
from jax.experimental import pallas as pl
from jax.experimental.pallas import tpu as pltpu

D_MODEL = 1024
SEQ_CHUNK = 128
RET_HEADS = 4
RET_QK_DIM = 128
RET_V_DIM = 256
ATTN_Q_HEADS = 16
ATTN_HEAD_DIM = 64
D_FF = 2816
D_IN = 6400
ROPE_THETA = 10000.0
EPS = 1e-6
N_CHIPS = 4

ADAM_LR = 0.001
ADAM_B1 = 0.9
ADAM_B2 = 0.999
ADAM_EPS = 1e-08
ADAM_WD = 0.01
ADAM_STEP = 10

OFF_RQ, OFF_RK, OFF_RV, OFF_RG = 0, 512, 1024, 2048
OFF_AQ, OFF_AK, OFF_AV, OFF_GA, OFF_GB = 3072, 4096, 4224, 4352, 5376

V7X_VMEM_LIMIT = 56 * 1024 * 1024
LANES = 128

BF = jnp.bfloat16
F32 = jnp.float32
MESH = pl.DeviceIdType.MESH

_VMEM = pl.BlockSpec(memory_space=pltpu.VMEM)
_SMEM = pl.BlockSpec(memory_space=pltpu.SMEM)
_ANY = pl.BlockSpec(memory_space=pl.ANY)


def _dot(a, b):
    return jnp.dot(a, b, preferred_element_type=F32)


def _dot_nt(a, b):
    return lax.dot_general(a, b, (((1,), (1,)), ((), ())), preferred_element_type=F32)


def _dot_tn(a, b):
    return lax.dot_general(a, b, (((0,), (0,)), ((), ())), preferred_element_type=F32)


def _sigmoid(x):
    return 1.0 / (1.0 + jnp.exp(-x))


def _rows(tm, n):
    return pl.BlockSpec((tm, n), lambda i: (i, 0))


def _fixed(shape):
    nd = len(shape)
    return pl.BlockSpec(shape, lambda i: (0,) * nd)


def _params(sem, **kw):
    return pltpu.CompilerParams(dimension_semantics=sem, vmem_limit_bytes=V7X_VMEM_LIMIT, **kw)


def _rot_ret(p, cr, sr):
    return p * cr + pltpu.roll(p, 64, 1) * sr


def _rot_ret_bwd(d, cr, sr):
    return d * cr - pltpu.roll(d, 64, 1) * sr


def _attn_partner(p):
    lane = lax.broadcasted_iota(jnp.int32, p.shape, 1)
    first = (lane % 64) < 32
    return jnp.where(first, pltpu.roll(p, 96, 1), pltpu.roll(p, 32, 1))


def _rot_attn(p, ca, sa):
    return p * ca + _attn_partner(p) * sa


def _rot_attn_bwd(d, ca, sa):
    return d * ca - _attn_partner(d) * sa


def _proj_fwd(x, g1, w_in, b_in, cr, sr, ca, sa):
    S = x.shape[0]
    tm = 512
    scale = RET_QK_DIM ** -0.5

    def body(x_ref, g_ref, w_ref, b_ref, cr_ref, sr_ref, ca_ref, sa_ref,
             h_ref, rq_ref, rk_ref, rv_ref, rg_ref, aq_ref, ak_ref, av_ref, ga_ref, gb_ref):
        xv = x_ref[...]
        r = lax.rsqrt(jnp.mean(xv * xv, axis=-1, keepdims=True) + EPS)
        h = (xv * r * g_ref[...]).astype(BF)
        h_ref[...] = h

        def piece(lo, n):
            return _dot_nt(h, w_ref[lo:lo + n, :]) + b_ref[:, lo:lo + n]

        crv, srv = cr_ref[...], sr_ref[...]
        pq = piece(OFF_RQ, 512)
        pk = piece(OFF_RK, 512)
        for hd in range(RET_HEADS):
            sl = slice(hd * 128, (hd + 1) * 128)
            rq_ref[:, sl] = (_rot_ret(pq[:, sl], crv, srv) * scale).astype(BF)
            rk_ref[:, sl] = _rot_ret(pk[:, sl], crv, srv)
        rv_ref[...] = piece(OFF_RV, 1024).astype(BF)
        rg_ref[...] = piece(OFF_RG, 1024)
        cav, sav = ca_ref[...], sa_ref[...]
        pa = piece(OFF_AQ, 1024)
        for blk in range(8):
            sl = slice(blk * 128, (blk + 1) * 128)
            aq_ref[:, sl] = _rot_attn(pa[:, sl], cav, sav).astype(BF)
        pkv = piece(OFF_AK, 256)
        ak_ref[...] = _rot_attn(pkv[:, :128], cav, sav).astype(BF)
        av_ref[...] = pkv[:, 128:].astype(BF)
        ga_ref[...] = piece(OFF_GA, 1024)
        gb_ref[...] = piece(OFF_GB, 1024)

    outs = [(D_MODEL, BF), (512, BF), (512, F32), (1024, BF), (1024, F32),
            (1024, BF), (128, BF), (128, BF), (1024, F32), (1024, F32)]
    return pl.pallas_call(
        body, name="proj_fwd", grid=(S // tm,),
        in_specs=[_rows(tm, D_MODEL), _fixed((1, D_MODEL)), _VMEM, _VMEM,
                  _rows(tm, 128), _rows(tm, 128), _rows(tm, 128), _rows(tm, 128)],
        out_specs=[_rows(tm, n) for n, _ in outs],
        out_shape=[jax.ShapeDtypeStruct((S, n), dt) for n, dt in outs],
        compiler_params=_params(("parallel",)),
    )(x, g1, w_in, b_in, cr, sr, ca, sa)


def _ret_fwd(rq, rk, rv, dmat, qd, kd, cd):
    S = rq.shape[0]
    N = S // SEQ_CHUNK

    def body(q_ref, k_ref, v_ref, dm_ref, qd_ref, kd_ref, cd_ref, y_ref, st_ref, state):
        @pl.when(pl.program_id(0) == 0)
        def _():
            state[...] = jnp.zeros_like(state)

        for hd in range(RET_HEADS):
            qs = slice(hd * 128, (hd + 1) * 128)
            vs = slice(hd * 256, (hd + 1) * 256)
            q = q_ref[:, qs]
            kf = k_ref[:, qs]
            k = kf.astype(BF)
            v = v_ref[:, vs]
            s = state[hd]
            sb = s.astype(BF)
            st_ref[0, hd] = sb
            a = _dot_nt(q, k) * dm_ref[hd]
            inner = _dot(a.astype(BF), v)
            cross = _dot(q, sb) * qd_ref[hd]
            y_ref[:, vs] = inner + cross
            kdv = (kf * kd_ref[hd]).astype(BF)
            state[hd] = s * cd_ref[hd] + _dot_tn(kdv, v)

    return pl.pallas_call(
        body, name="ret_fwd", grid=(N,),
        in_specs=[_rows(128, 512), _rows(128, 512), _rows(128, 1024), _VMEM, _VMEM, _VMEM, _VMEM],
        out_specs=[_rows(128, 1024), pl.BlockSpec((1, RET_HEADS, 128, 256), lambda i: (i, 0, 0, 0))],
        out_shape=[jax.ShapeDtypeStruct((S, 1024), F32),
                   jax.ShapeDtypeStruct((N, RET_HEADS, 128, 256), BF)],
        scratch_shapes=[pltpu.VMEM((RET_HEADS, 128, 256), F32)],
        compiler_params=_params(("arbitrary",)),
    )(rq, rk, rv, dmat, qd, kd, cd)


def _kv_halves(prev_ref, cur_ref):
    kk = jnp.concatenate([prev_ref[...], cur_ref[...]], axis=0).astype(F32)
    return kk, pltpu.roll(kk, 64, 1)


PAIRS = 4


def _attn_mask(n):
    kj = lax.broadcasted_iota(jnp.int32, (256, PAIRS * 128), 0)
    qi = lax.broadcasted_iota(jnp.int32, (256, PAIRS * 128), 1) & 127
    first_prev = jnp.where(n > 0, 0, 128)
    prev_ok = jnp.logical_and(jnp.logical_and(kj < 128, kj >= first_prev), qi < kj)
    cur_ok = jnp.logical_and(kj >= 128, qi >= kj - 128)
    return jnp.logical_or(prev_ok, cur_ok)


def _group_operands(g, kk, kk_sw, vv, vv_sw):
    lo = lax.broadcasted_iota(jnp.int32, (256, 128), 1) < 64
    k_a, k_b = (kk, kk_sw) if g == 0 else (kk_sw, kk)
    v_a, v_b = (vv, vv_sw) if g == 0 else (vv_sw, vv)
    return (jnp.where(lo, k_a, 0.0).astype(BF), jnp.where(lo, 0.0, k_b).astype(BF),
            jnp.where(lo, v_a, 0.0).astype(BF), jnp.where(lo, 0.0, v_b).astype(BF))


def _stack_pairs(ref, g):
    return jnp.concatenate([ref[:, (4 * g + p) * 128:(4 * g + p + 1) * 128] for p in range(PAIRS)], axis=0)


def _sink_row(sk_ref, g, which):
    return jnp.concatenate([jnp.full((1, 128), sk_ref[0, 8 * g + 2 * p + which], F32) for p in range(PAIRS)], axis=1)


def _attn_probs(qs, kx, allowed, sink):
    s = _dot_nt(kx, qs) * (ATTN_HEAD_DIM ** -0.5)
    s = jnp.where(allowed, s, -1e30)
    m = jnp.maximum(jnp.max(s, axis=0, keepdims=True), sink)
    e = jnp.exp(s - m)
    es = jnp.exp(sink - m)
    inv = 1.0 / (jnp.sum(e, axis=0, keepdims=True) + es)
    return e * inv, es * inv


def _attn_fwd(aq, ak, av, sinks):
    S = aq.shape[0]
    N = S // SEQ_CHUNK

    def body(sk_ref, q_ref, kp_ref, kc_ref, vp_ref, vc_ref, o_ref):
        kk, kk_sw = _kv_halves(kp_ref, kc_ref)
        vv, vv_sw = _kv_halves(vp_ref, vc_ref)
        allowed = _attn_mask(pl.program_id(0))
        for g in range(2):
            k_lo, k_hi, v_lo, v_hi = _group_operands(g, kk, kk_sw, vv, vv_sw)
            qs = _stack_pairs(q_ref, g)
            p0, _ = _attn_probs(qs, k_lo, allowed, _sink_row(sk_ref, g, 0))
            p1, _ = _attn_probs(qs, k_hi, allowed, _sink_row(sk_ref, g, 1))
            o = _dot_tn(p0.astype(BF), v_lo) + _dot_tn(p1.astype(BF), v_hi)
            for p in range(PAIRS):
                o_ref[:, (4 * g + p) * 128:(4 * g + p + 1) * 128] = o[p * 128:(p + 1) * 128].astype(BF)

    prev = lambda i: (jnp.maximum(i - 1, 0), 0)
    return pl.pallas_call(
        body, name="attn_fwd", grid=(N,),
        in_specs=[_SMEM, _rows(128, 1024), pl.BlockSpec((128, 128), prev), _rows(128, 128),
                  pl.BlockSpec((128, 128), prev), _rows(128, 128)],
        out_specs=_rows(128, 1024),
        out_shape=jax.ShapeDtypeStruct((S, 1024), BF),
        compiler_params=_params(("parallel",)),
    )(sinks, aq, ak, ak, av, av)


def _group_norm(y):
    mu = jnp.mean(y, axis=-1, keepdims=True)
    yc = y - mu
    rstd = lax.rsqrt(jnp.mean(yc * yc, axis=-1, keepdims=True) + EPS)
    return yc * rstd, rstd


def _merge_fwd(x, ry, rg, ay, ga, gb, g_ret, w_ret, w_attn, w_out):
    S = x.shape[0]
    tm = 256

    def body(x_ref, ry_ref, rg_ref, ay_ref, ga_ref, gb_ref, gr_ref, wr_ref, wa_ref, wo_ref,
             u_ref, ba_ref, bb_ref, mg_ref, x1_ref):
        for hd in range(RET_HEADS):
            vs = slice(hd * 256, (hd + 1) * 256)
            yn, _ = _group_norm(ry_ref[:, vs])
            rgv = rg_ref[:, vs]
            u_ref[:, vs] = (rgv * _sigmoid(rgv) * (yn * gr_ref[:, vs])).astype(BF)
        ba = _dot(u_ref[...], wr_ref[...])
        bb = _dot(ay_ref[...], wa_ref[...])
        ba_ref[...] = ba
        bb_ref[...] = bb
        merged = (_sigmoid(ga_ref[...]) * ba + _sigmoid(gb_ref[...]) * bb).astype(BF)
        mg_ref[...] = merged
        x1_ref[...] = x_ref[...] + _dot(merged, wo_ref[...])

    outs = [BF, F32, F32, BF, F32]
    return pl.pallas_call(
        body, name="merge_fwd", grid=(S // tm,),
        in_specs=[_rows(tm, 1024)] * 6 + [_fixed((1, 1024)), _VMEM, _VMEM, _VMEM],
        out_specs=[_rows(tm, 1024)] * 5,
        out_shape=[jax.ShapeDtypeStruct((S, 1024), dt) for dt in outs],
        compiler_params=_params(("parallel",)),
    )(x, ry, rg, ay, ga, gb, g_ret, w_ret, w_attn, w_out)


def _ffn_fwd(x1, tgt, g2, g3, w_gate, w_up, w_down):
    S = x1.shape[0]
    tm = 256

    def body(x1_ref, t_ref, g2_ref, g3_ref, wg_ref, wu_ref, wd_ref,
             gate_ref, up_ref, h2_ref, act_ref, dx2_ref, loss_ref, dg3_ref):
        @pl.when(pl.program_id(0) == 0)
        def _():
            loss_ref[...] = jnp.zeros_like(loss_ref)
            dg3_ref[...] = jnp.zeros_like(dg3_ref)

        x1v = x1_ref[...]
        r2 = lax.rsqrt(jnp.mean(x1v * x1v, axis=-1, keepdims=True) + EPS)
        h2 = (x1v * r2 * g2_ref[...]).astype(BF)
        h2_ref[...] = h2
        gate = _dot_nt(h2, wg_ref[...])
        up = _dot_nt(h2, wu_ref[...])
        gate_ref[...] = gate
        up_ref[...] = up
        act = (gate * _sigmoid(gate) * up).astype(BF)
        act_ref[...] = act
        x2 = x1v + _dot(act, wd_ref[...])
        r3 = lax.rsqrt(jnp.mean(x2 * x2, axis=-1, keepdims=True) + EPS)
        xh = x2 * r3
        g3v = g3_ref[...]
        err = xh * g3v - t_ref[...]
        loss_ref[...] += 0.5 * jnp.sum(jnp.mean(err * err, axis=-1, keepdims=True), axis=0, keepdims=True)
        dy = err * (1.0 / D_MODEL)
        dg3_ref[...] += jnp.sum(dy * xh, axis=0, keepdims=True)
        dxh = dy * g3v
        dx2_ref[...] = r3 * (dxh - xh * jnp.mean(dxh * xh, axis=-1, keepdims=True))

    return pl.pallas_call(
        body, name="ffn_fwd", grid=(S // tm,),
        in_specs=[_rows(tm, 1024), _rows(tm, 1024), _fixed((1, 1024)), _fixed((1, 1024)), _VMEM, _VMEM, _VMEM],
        out_specs=[_rows(tm, D_FF), _rows(tm, D_FF), _rows(tm, 1024), _rows(tm, D_FF), _rows(tm, 1024),
                   _fixed((1, LANES)), _fixed((1, 1024))],
        out_shape=[jax.ShapeDtypeStruct((S, D_FF), F32), jax.ShapeDtypeStruct((S, D_FF), F32),
                   jax.ShapeDtypeStruct((S, 1024), BF), jax.ShapeDtypeStruct((S, D_FF), BF),
                   jax.ShapeDtypeStruct((S, 1024), F32),
                   jax.ShapeDtypeStruct((1, LANES), F32), jax.ShapeDtypeStruct((1, 1024), F32)],
        compiler_params=_params(("arbitrary",)),
    )(x1, tgt, g2, g3, w_gate, w_up, w_down)


def _ffn_bwd(dx2, gate, up, x1, g2, w_gate, w_up, w_down):
    S = x1.shape[0]
    tm = 256

    def body(dx2_ref, gate_ref, up_ref, x1_ref, g2_ref, wg_ref, wu_ref, wd_ref,
             dgate_ref, dup_ref, dx1_ref, dg2_ref):
        @pl.when(pl.program_id(0) == 0)
        def _():
            dg2_ref[...] = jnp.zeros_like(dg2_ref)

        dx2v = dx2_ref[...]
        dact = _dot_nt(dx2v.astype(BF), wd_ref[...])
        gate = gate_ref[...]
        sg = _sigmoid(gate)
        dgate = (dact * up_ref[...] * (sg * (1.0 + gate * (1.0 - sg)))).astype(BF)
        dup = (dact * (gate * sg)).astype(BF)
        dgate_ref[...] = dgate
        dup_ref[...] = dup
        dh2 = _dot(dgate, wg_ref[...]) + _dot(dup, wu_ref[...])
        x1v = x1_ref[...]
        r2 = lax.rsqrt(jnp.mean(x1v * x1v, axis=-1, keepdims=True) + EPS)
        xh = x1v * r2
        dg2_ref[...] += jnp.sum(dh2 * xh, axis=0, keepdims=True)
        dxh = dh2 * g2_ref[...]
        dx1_ref[...] = dx2v + r2 * (dxh - xh * jnp.mean(dxh * xh, axis=-1, keepdims=True))

    return pl.pallas_call(
        body, name="ffn_bwd", grid=(S // tm,),
        in_specs=[_rows(tm, 1024), _rows(tm, D_FF), _rows(tm, D_FF), _rows(tm, 1024), _fixed((1, 1024)),
                  _VMEM, _VMEM, _VMEM],
        out_specs=[_rows(tm, D_FF), _rows(tm, D_FF), _rows(tm, 1024), _fixed((1, 1024))],
        out_shape=[jax.ShapeDtypeStruct((S, D_FF), BF), jax.ShapeDtypeStruct((S, D_FF), BF),
                   jax.ShapeDtypeStruct((S, 1024), F32), jax.ShapeDtypeStruct((1, 1024), F32)],
        compiler_params=_params(("arbitrary",)),
    )(dx2, gate, up, x1, g2, w_gate, w_up, w_down)


def _merge_bwd(dx1, ga, gb, ba, bb, rg, ry, g_ret, w_ret, w_attn, w_out):
    S = dx1.shape[0]
    tm = 256

    def body(dx1_ref, ga_ref, gb_ref, ba_ref, bb_ref, rg_ref, ry_ref, gr_ref, wr_ref, wa_ref, wo_ref,
             dgates_ref, drg_ref, dba_ref, dbb_ref, day_ref, dry_ref, dgr_ref, db_ref):
        @pl.when(pl.program_id(0) == 0)
        def _():
            dgr_ref[...] = jnp.zeros_like(dgr_ref)
            db_ref[...] = jnp.zeros_like(db_ref)

        dm = _dot_nt(dx1_ref[...].astype(BF), wo_ref[...])
        sa = _sigmoid(ga_ref[...])
        sb = _sigmoid(gb_ref[...])
        dga = (dm * ba_ref[...] * (sa * (1.0 - sa))).astype(BF)
        dgb = (dm * bb_ref[...] * (sb * (1.0 - sb))).astype(BF)
        dgates_ref[:, :1024] = dga
        dgates_ref[:, 1024:] = dgb
        dba = (dm * sa).astype(BF)
        dbb = (dm * sb).astype(BF)
        dba_ref[...] = dba
        dbb_ref[...] = dbb
        day_ref[...] = _dot_nt(dbb, wa_ref[...]).astype(BF)
        du = _dot_nt(dba, wr_ref[...])
        for hd in range(RET_HEADS):
            vs = slice(hd * 256, (hd + 1) * 256)
            yn, rstd = _group_norm(ry_ref[:, vs])
            rgv = rg_ref[:, vs]
            sr = _sigmoid(rgv)
            silu = rgv * sr
            duv = du[:, vs]
            grv = gr_ref[:, vs]
            drg = (duv * (yn * grv) * (sr * (1.0 + rgv * (1.0 - sr)))).astype(BF)
            drg_ref[:, vs] = drg
            dyg = duv * silu
            dgr_ref[:, vs] += jnp.sum(dyg * yn, axis=0, keepdims=True)
            dyn = dyg * grv
            dry_ref[:, vs] = rstd * (dyn - jnp.mean(dyn, axis=-1, keepdims=True)
                                     - yn * jnp.mean(dyn * yn, axis=-1, keepdims=True))
            db_ref[:, vs] += jnp.sum(drg.astype(F32), axis=0, keepdims=True)
        db_ref[:, 1024:2048] += jnp.sum(dga.astype(F32), axis=0, keepdims=True)
        db_ref[:, 2048:] += jnp.sum(dgb.astype(F32), axis=0, keepdims=True)

    return pl.pallas_call(
        body, name="merge_bwd", grid=(S // tm,),
        in_specs=[_rows(tm, 1024)] * 7 + [_fixed((1, 1024)), _VMEM, _VMEM, _VMEM],
        out_specs=[_rows(tm, 2048), _rows(tm, 1024), _rows(tm, 1024), _rows(tm, 1024), _rows(tm, 1024),
                   _rows(tm, 1024), _fixed((1, 1024)), _fixed((1, 3072))],
        out_shape=[jax.ShapeDtypeStruct((S, 2048), BF), jax.ShapeDtypeStruct((S, 1024), BF),
                   jax.ShapeDtypeStruct((S, 1024), BF), jax.ShapeDtypeStruct((S, 1024), BF),
                   jax.ShapeDtypeStruct((S, 1024), BF), jax.ShapeDtypeStruct((S, 1024), F32),
                   jax.ShapeDtypeStruct((1, 1024), F32), jax.ShapeDtypeStruct((1, 3072), F32)],
        compiler_params=_params(("arbitrary",)),
    )(dx1, ga, gb, ba, bb, rg, ry, g_ret, w_ret, w_attn, w_out)


def _attn_bwd(aq, ak, av, day, sinks, ca, sa, after):
    S = aq.shape[0]
    N = S // SEQ_CHUNK

    def body(sk_ref, q_ref, kp_ref, kc_ref, vp_ref, vc_ref, do_ref, cq_ref, sq_ref, cp_ref, sp_ref, after_ref,
             dq_ref, dk_ref, dv_ref, dsk_ref, carry_k, carry_v):
        n = pl.program_id(0)

        @pl.when(n == 0)
        def _():
            carry_k[...] = jnp.zeros_like(carry_k)
            carry_v[...] = jnp.zeros_like(carry_v)
            dsk_ref[...] = jnp.zeros_like(dsk_ref)

        @pl.when(n < N)
        def _():
            kk, kk_sw = _kv_halves(kp_ref, kc_ref)
            vv, vv_sw = _kv_halves(vp_ref, vc_ref)
            lo = lax.broadcasted_iota(jnp.int32, (256, 128), 1) < 64
            allowed = _attn_mask(n)
            cq, sq = cq_ref[...], sq_ref[...]
            dk_groups, dv_groups = [], []
            for g in range(2):
                k_lo, k_hi, v_lo, v_hi = _group_operands(g, kk, kk_sw, vv, vv_sw)
                qs = _stack_pairs(q_ref, g)
                dos = _stack_pairs(do_ref, g)
                dq = jnp.zeros((PAIRS * 128, 128), F32)
                dk_half, dv_half = [], []
                for which, (kx, vx) in enumerate(((k_lo, v_lo), (k_hi, v_hi))):
                    pr, ps = _attn_probs(qs, kx, allowed, _sink_row(sk_ref, g, which))
                    dp = _dot_nt(vx, dos)
                    dr = jnp.sum(pr * dp, axis=0, keepdims=True)
                    ds = (pr * (dp - dr) * (ATTN_HEAD_DIM ** -0.5)).astype(BF)
                    dsk_ref[2 * g + which:2 * g + which + 1, :] -= ps * dr
                    dq = dq + _dot_tn(ds, kx)
                    dk_half.append(_dot(ds, qs))
                    dv_half.append(_dot(pr.astype(BF), dos))
                for p in range(PAIRS):
                    sl = slice((4 * g + p) * 128, (4 * g + p + 1) * 128)
                    dq_ref[:, sl] = _rot_attn_bwd(dq[p * 128:(p + 1) * 128], cq, sq).astype(BF)
                tk = jnp.where(lo, dk_half[0], dk_half[1])
                tv = jnp.where(lo, dv_half[0], dv_half[1])
                dk_groups.append(tk + pltpu.roll(tk, 64, 1))
                dv_groups.append(tv + pltpu.roll(tv, 64, 1))
            dk_full = jnp.where(lo, dk_groups[0], dk_groups[1])
            dv_full = jnp.where(lo, dv_groups[0], dv_groups[1])
            dk_ref[...] = _rot_attn_bwd(carry_k[...] + dk_full[:128], cp_ref[...], sp_ref[...]).astype(BF)
            dv_ref[...] = (carry_v[...] + dv_full[:128]).astype(BF)
            carry_k[...] = dk_full[128:]
            carry_v[...] = dv_full[128:]

        @pl.when(n == N)
        def _():
            dk_ref[...] = _rot_attn_bwd(carry_k[...], cp_ref[...], sp_ref[...]).astype(BF)
            dv_ref[...] = carry_v[...].astype(BF)

    cur = lambda i: (jnp.minimum(i, N - 1), 0)
    prev = lambda i: (jnp.maximum(i - 1, 0), 0)
    blk = lambda w, im: pl.BlockSpec((128, w), im)
    return pl.pallas_call(
        body, name="attn_bwd", grid=(N + 1,),
        in_specs=[_SMEM, blk(1024, cur), blk(128, prev), blk(128, cur), blk(128, prev), blk(128, cur),
                  blk(1024, cur), blk(128, cur), blk(128, cur), blk(128, prev), blk(128, prev), _ANY],
        out_specs=[blk(1024, cur), blk(128, prev), blk(128, prev), _fixed((4, PAIRS * 128))],
        out_shape=[jax.ShapeDtypeStruct((S, 1024), BF), jax.ShapeDtypeStruct((S, 128), BF),
                   jax.ShapeDtypeStruct((S, 128), BF), jax.ShapeDtypeStruct((4, PAIRS * 128), F32)],
        scratch_shapes=[pltpu.VMEM((128, 128), F32), pltpu.VMEM((128, 128), F32)],
        compiler_params=_params(("arbitrary",)),
    )(sinks, aq, ak, ak, av, av, day, ca, sa, ca, sa, after)


def _ret_bwd(rq, rk, rv, dry, states, dmat, qd, kd, cd, cr, sr, after):
    S = rq.shape[0]
    N = S // SEQ_CHUNK
    scale = RET_QK_DIM ** -0.5

    def body(q_ref, k_ref, v_ref, dy_ref, st_ref, dm_ref, qd_ref, kd_ref, cd_ref, cr_ref, sr_ref, after_ref,
             out_ref, dstate):
        @pl.when(pl.program_id(0) == 0)
        def _():
            dstate[...] = jnp.zeros_like(dstate)

        crv, srv = cr_ref[...], sr_ref[...]
        for hd in range(RET_HEADS):
            qs = slice(hd * 128, (hd + 1) * 128)
            vs = slice(hd * 256, (hd + 1) * 256)
            q = q_ref[:, qs]
            kf = k_ref[:, qs]
            k = kf.astype(BF)
            kdv = (kf * kd_ref[hd]).astype(BF)
            v = v_ref[:, vs]
            do = dy_ref[:, vs]
            dob = do.astype(BF)
            doq = (do * qd_ref[hd]).astype(BF)
            sn = st_ref[0, hd]
            ds = dstate[hd]
            dsb = ds.astype(BF)
            dmv = dm_ref[hd]
            a = (_dot_nt(q, k) * dmv).astype(BF)
            dv = _dot_tn(a, dob) + _dot(kdv, dsb)
            da = (_dot_nt(dob, v) * dmv).astype(BF)
            dq = _dot(da, k) + _dot_nt(doq, sn)
            dk = _dot_tn(da, q) + _dot_nt(v, dsb) * kd_ref[hd]
            dstate[hd] = ds * cd_ref[hd] + _dot_tn(q, doq)
            out_ref[:, qs] = _rot_ret_bwd(dq * scale, crv, srv).astype(BF)
            out_ref[:, 512 + hd * 128:512 + (hd + 1) * 128] = _rot_ret_bwd(dk, crv, srv).astype(BF)
            out_ref[:, 1024 + hd * 256:1024 + (hd + 1) * 256] = dv.astype(BF)

    rev = lambda w: pl.BlockSpec((128, w), lambda i: (N - 1 - i, 0))
    return pl.pallas_call(
        body, name="ret_bwd", grid=(N,),
        in_specs=[rev(512), rev(512), rev(1024), rev(1024),
                  pl.BlockSpec((1, RET_HEADS, 128, 256), lambda i: (N - 1 - i, 0, 0, 0)),
                  _VMEM, _VMEM, _VMEM, _VMEM, rev(128), rev(128), _ANY],
        out_specs=rev(2048),
        out_shape=jax.ShapeDtypeStruct((S, 2048), BF),
        scratch_shapes=[pltpu.VMEM((RET_HEADS, 128, 256), F32)],
        compiler_params=_params(("arbitrary",)),
    )(rq, rk, rv, dry, states, dmat, qd, kd, cd, cr, sr, after)


def _proj_bwd(d_pieces, x, dx1, g1, w_in_t):
    S = x.shape[0]
    tm = 256
    widths = [d.shape[1] for d in d_pieces]
    assert sum(widths) == D_IN
    nd = len(d_pieces)

    def body(*refs):
        d_refs = refs[:nd]
        x_ref, dx1_ref, g_ref, w_ref, gx_ref, dg_ref, db_ref = refs[nd:]

        @pl.when(pl.program_id(0) == 0)
        def _():
            dg_ref[...] = jnp.zeros_like(dg_ref)
            db_ref[...] = jnp.zeros_like(db_ref)

        dh = jnp.zeros((tm, D_MODEL), F32)
        lo = 0
        for ref, n in zip(d_refs, widths):
            v = ref[...]
            dh = dh + _dot(v, w_ref[lo:lo + n, :])
            db_ref[:, lo:lo + n] += jnp.sum(v.astype(F32), axis=0, keepdims=True)
            lo += n
        xv = x_ref[...]
        r = lax.rsqrt(jnp.mean(xv * xv, axis=-1, keepdims=True) + EPS)
        xh = xv * r
        dg_ref[...] += jnp.sum(dh * xh, axis=0, keepdims=True)
        dxh = dh * g_ref[...]
        gx_ref[...] = dx1_ref[...] + r * (dxh - xh * jnp.mean(dxh * xh, axis=-1, keepdims=True))

    return pl.pallas_call(
        body, name="proj_bwd", grid=(S // tm,),
        in_specs=[_rows(tm, n) for n in widths] + [_rows(tm, 1024), _rows(tm, 1024), _fixed((1, 1024)), _VMEM],
        out_specs=[_rows(tm, 1024), _fixed((1, 1024)), _fixed((1, D_IN))],
        out_shape=[jax.ShapeDtypeStruct((S, 1024), F32), jax.ShapeDtypeStruct((1, 1024), F32),
                   jax.ShapeDtypeStruct((1, D_IN), F32)],
        compiler_params=_params(("arbitrary",)),
    )(*d_pieces, x, dx1, g1, w_in_t)


SHARD_ROWS = (1600, 256, 256, 256, 704, 704, 704)


class Group(NamedTuple):
    name: str
    tensors: tuple
    offs: tuple
    block: int
    tile: int

    @property
    def half(self):
        return self.block // 2

    def off(self, t):
        return self.offs[self.tensors.index(t)]


LATE = Group("late", (0,), (0,), 1600, 800)
EARLY = Group("early", (1, 2, 3, 4, 5, 6), (0, 256, 512, 768, 1472, 2176), 2880, 480)
GROUP_OF = {t: g for g in (LATE, EARLY) for t in g.tensors}


def _wgrad(a_list, b, t, packed, name):
    S = b.shape[0]
    ts = 512
    steps = S // ts
    widths = [a.shape[1] for a in a_list]
    na = len(a_list)
    grp = GROUP_OF[t]
    r, off = SHARD_ROWS[t], grp.off(t)
    assert sum(widths) == N_CHIPS * r
    n_in = na + 1 + (2 if packed is not None else 0)

    def body(*refs):
        a_refs, b_ref = refs[:na], refs[na]
        g32_ref, gb_ref, acc, stage, sems = refs[n_in:]
        i = pl.program_id(0)

        @pl.when(i == 0)
        def _():
            acc[...] = jnp.zeros_like(acc)

        bv = b_ref[...].astype(BF)
        lo = 0
        for a_ref, n in zip(a_refs, widths):
            acc[lo:lo + n, :] += _dot_tn(a_ref[...].astype(BF), bv)
            lo += n

        @pl.when(i == steps - 1)
        def _():
            for chip in range(N_CHIPS):
                src = acc.at[chip * r:(chip + 1) * r, :]
                dst = pl.ds(chip * grp.block + off, r)
                c32 = pltpu.make_async_copy(src, g32_ref.at[dst, :], sems.at[0])
                c32.start()
                stage[...] = acc[chip * r:(chip + 1) * r, :].astype(BF)
                c16 = pltpu.make_async_copy(stage, gb_ref.at[dst, :], sems.at[1])
                c16.start()
                c32.wait()
                c16.wait()

    shapes = [jax.ShapeDtypeStruct((N_CHIPS * grp.block, 1024), F32),
              jax.ShapeDtypeStruct((N_CHIPS * grp.block, 1024), BF)]
    operands = list(a_list) + [b]
    in_specs = [_rows(ts, n) for n in widths] + [_rows(ts, 1024)]
    aliases = {}
    if packed is not None:
        operands += list(packed)
        in_specs += [_ANY, _ANY]
        aliases = {na + 1: 0, na + 2: 1}
    return pl.pallas_call(
        body, name=name, grid=(steps,),
        in_specs=in_specs, out_specs=[_ANY, _ANY], out_shape=shapes,
        scratch_shapes=[pltpu.VMEM((N_CHIPS * r, 1024), F32), pltpu.VMEM((r, 1024), BF),
                        pltpu.SemaphoreType.DMA((2,))],
        input_output_aliases=aliases,
        compiler_params=_params(("arbitrary",)),
    )(*operands)


def _stage_weight(place, w, name):
    r = w.shape[0]
    tr = 800 if r == 1600 else r
    nb = r // tr

    def body(p_ref, w_ref, o_ref):
        o_ref[...] = w_ref[...].astype(BF)

    return pl.pallas_call(
        body, name=name,
        grid_spec=pltpu.PrefetchScalarGridSpec(
            num_scalar_prefetch=1, grid=(nb,),
            in_specs=[pl.BlockSpec((tr, 1024), lambda i, p: (i, 0))],
            out_specs=pl.BlockSpec((tr, 1024), lambda i, p: (p[0] * nb + i, 0))),
        out_shape=jax.ShapeDtypeStruct((N_CHIPS * r, 1024), BF),
        compiler_params=_params(("arbitrary",)),
    )(place, w)


def _pair_sum(place, g32, got, grp):
    nb = grp.half // grp.tile

    def body(p_ref, g_ref, r_ref, s32_ref, s16_ref):
        s = g_ref[...] + r_ref[...].astype(F32)
        s32_ref[...] = s
        s16_ref[...] = s.astype(BF)

    flat = pl.BlockSpec((grp.tile, 1024), lambda i, p: (i, 0))
    return pl.pallas_call(
        body, name="pair_sum_" + grp.name,
        grid_spec=pltpu.PrefetchScalarGridSpec(
            num_scalar_prefetch=1, grid=(N_CHIPS * nb,),
            in_specs=[pl.BlockSpec((grp.tile, 1024), lambda i, p: ((2 * (i // nb) + p[1]) * nb + i % nb, 0)), flat],
            out_specs=[flat, flat]),
        out_shape=[jax.ShapeDtypeStruct((N_CHIPS * grp.half, 1024), F32),
                   jax.ShapeDtypeStruct((N_CHIPS * grp.half, 1024), BF)],
        compiler_params=_params(("arbitrary",)),
    )(place, g32, got)


def _chip_sum(place, s32, others, grp):
    nb = grp.half // grp.tile

    def body(p_ref, a_ref, r0_ref, r1_ref, r2_ref, o_ref):
        o_ref[...] = ((a_ref[...] + r0_ref[...].astype(F32)) + r1_ref[...].astype(F32)) + r2_ref[...].astype(F32)

    blk = lambda f: pl.BlockSpec((grp.tile, 1024), f)
    return pl.pallas_call(
        body, name="chip_sum_" + grp.name,
        grid_spec=pltpu.PrefetchScalarGridSpec(
            num_scalar_prefetch=1, grid=(nb,),
            in_specs=[blk(lambda i, p: (p[0] * nb + i, 0)), blk(lambda i, p: (i, 0)),
                      blk(lambda i, p: (nb + i, 0)), blk(lambda i, p: (2 * nb + i, 0))],
            out_specs=blk(lambda i, p: (p[1] * nb + i, 0))),
        out_shape=jax.ShapeDtypeStruct((grp.block, 1024), F32),
        compiler_params=_params(("arbitrary",)),
    )(place, s32, others, others, others)


def _adamw(w, g, m, v, name, g_row0=0, after=None):
    r = w.shape[0]
    tr = 320 if r % 320 == 0 and g_row0 % 320 == 0 else (64 if r % 64 == 0 else r)
    assert g_row0 % tr == 0
    g0 = g_row0 // tr
    extra = [] if after is None else [after]

    def body(w_ref, g_ref, m_ref, v_ref, *rest):
        go_ref, d_ref, mo_ref, vo_ref = rest[len(extra):]
        gv = g_ref[...]
        mn = ADAM_B1 * m_ref[...] + (1.0 - ADAM_B1) * gv
        vn = ADAM_B2 * v_ref[...] + (1.0 - ADAM_B2) * (gv * gv)
        m_hat = mn / (1.0 - ADAM_B1 ** ADAM_STEP)
        v_hat = vn / (1.0 - ADAM_B2 ** ADAM_STEP)
        go_ref[...] = gv
        d_ref[...] = -ADAM_LR * (m_hat / (jnp.sqrt(v_hat) + ADAM_EPS) + ADAM_WD * w_ref[...])
        mo_ref[...] = mn
        vo_ref[...] = vn

    return pl.pallas_call(
        body, name=name, grid=(r // tr,),
        in_specs=[_rows(tr, 1024), pl.BlockSpec((tr, 1024), lambda i: (g0 + i, 0)), _rows(tr, 1024), _rows(tr, 1024)]
        + [_ANY] * len(extra),
        out_specs=[_rows(tr, 1024)] * 4,
        out_shape=[jax.ShapeDtypeStruct((r, 1024), F32)] * 4,
        compiler_params=_params(("parallel",)),
    )(w, g, m, v, *extra)


def _place():
    x, y, c = lax.axis_index("x"), lax.axis_index("y"), lax.axis_index("c")
    chips = [(1 - x, y), (x, 1 - y), (1 - x, 1 - y)]
    return x, y, c, chips


def _gather_weights(staged, tids):
    nt = len(staged)

    def body(*refs):
        out_refs = refs[nt:2 * nt]
        send_sems, recv_sems = refs[2 * nt:]
        x, y, c, chips = _place()
        j = 2 * x + y
        me, sibling = (x, y, c), (x, y, 1 - c)
        theirs = [2 * cx + cy for cx, cy in chips]

        def copy(t, k, pj, half, to):
            h = SHARD_ROWS[tids[t]] // 2
            rows = out_refs[t].at[pl.ds((2 * pj + half) * h, h), :]
            return pltpu.make_async_remote_copy(
                src_ref=rows, dst_ref=rows, send_sem=send_sems.at[6 * t + k], recv_sem=recv_sems.at[6 * t + k],
                device_id=to, device_id_type=MESH)

        first = [copy(t, k, j, c, (*chip, c)) for t in range(nt) for k, chip in enumerate(chips)]
        for cp in first:
            cp.start()
        passed = []
        for t in range(nt):
            for k in range(3):
                copy(t, k, theirs[k], c, me).wait_recv()
                fwd = copy(t, 3 + k, theirs[k], c, sibling)
                fwd.start()
                passed.append(fwd)
        for t in range(nt):
            for k in range(3):
                copy(t, 3 + k, theirs[k], 1 - c, me).wait_recv()
        for cp in first + passed:
            cp.wait_send()

    return pl.pallas_call(
        body, name="gather_weights",
        in_specs=[_ANY] * nt, out_specs=[_ANY] * nt,
        out_shape=[jax.ShapeDtypeStruct(a.shape, a.dtype) for a in staged],
        scratch_shapes=[pltpu.SemaphoreType.DMA((6 * nt,)), pltpu.SemaphoreType.DMA((6 * nt,))],
        input_output_aliases={t: t for t in range(nt)},
    )(*staged)


_HBM = pl.BlockSpec(memory_space=pltpu.HBM)
_SEM = pl.BlockSpec(memory_space=pltpu.SEMAPHORE)
_DATAFLOW = pltpu.SideEffectType.DATAFLOW_SIDE_EFFECTING


def _exchange(name, arrays, n, plan):
    na = len(arrays)

    def body(*refs):
        pairs = plan(refs[na:2 * na], refs[2 * na], refs[2 * na + 1])
        for send, _ in pairs:
            send.start()
        for send, recv in pairs:
            send.wait_send()
            recv.wait_recv()

    return pl.pallas_call(
        body, name=name, in_specs=[_ANY] * na, out_specs=[_ANY] * na,
        out_shape=[jax.ShapeDtypeStruct(a.shape, a.dtype) for a in arrays],
        scratch_shapes=[pltpu.SemaphoreType.DMA((n,)), pltpu.SemaphoreType.DMA((n,))],
        input_output_aliases={i: i for i in range(na)},
    )(*arrays)


def _exchange_start(name, arrays, n, plan, after=()):
    na = len(arrays)

    def body(*refs):
        for send, _ in plan(refs[:na], refs[na + len(after)], refs[na + len(after) + 1]):
            send.start()
        refs[-1][...] = jnp.zeros_like(refs[-1])

    out = pl.pallas_call(
        body, name=name,
        out_shape=(pltpu.SemaphoreType.DMA((n,)), pltpu.SemaphoreType.DMA((n,)),
                   *[pltpu.HBM(a.shape, a.dtype) for a in arrays], jax.ShapeDtypeStruct((8, LANES), F32)),
        in_specs=[_HBM] * na + [_ANY] * len(after), out_specs=(_SEM, _SEM, *[_HBM] * na, _VMEM),
        input_output_aliases={i: 2 + i for i in range(na)},
        compiler_params=pltpu.CompilerParams(has_side_effects=_DATAFLOW),
    )(*[pltpu.with_memory_space_constraint(a, pltpu.HBM) for a in arrays], *after)
    return (out[0], out[1]), list(out[2:2 + na]), out[-1]


def _exchange_wait(name, sems, in_flight, plan, after=()):
    na = len(in_flight)

    def body(*refs):
        for send, recv in plan(refs[:na], refs[na], refs[na + 1]):
            send.wait_send()
            recv.wait_recv()

    out = pl.pallas_call(
        body, name=name,
        out_shape=tuple(pltpu.HBM(a.shape, a.dtype) for a in in_flight),
        in_specs=[_HBM] * na + [_SEM, _SEM] + [_ANY] * len(after), out_specs=tuple([_HBM] * na),
        input_output_aliases={i: i for i in range(na)},
        compiler_params=pltpu.CompilerParams(has_side_effects=_DATAFLOW),
    )(*in_flight, *sems, *after)
    return list(out)


def _remote(src, dst, send_sems, recv_sems, k, to):
    return pltpu.make_async_remote_copy(src_ref=src, dst_ref=dst, send_sem=send_sems.at[k], recv_sem=recv_sems.at[k],
                                        device_id=to, device_id_type=MESH)


def _plan_gather_direct(tids):
    def plan(refs, ss, rs):
        x, y, c, chips = _place()
        j = 2 * x + y
        pairs = []
        for t, tid in enumerate(tids):
            r = SHARD_ROWS[tid]
            for k, (cx, cy) in enumerate(chips):
                mine = refs[t].at[pl.ds(j * r, r), :]
                theirs = refs[t].at[pl.ds((2 * cx + cy) * r, r), :]
                pairs.append((_remote(mine, mine, ss, rs, 3 * t + k, (cx, cy, c)),
                              _remote(theirs, theirs, ss, rs, 3 * t + k, (x, y, c))))
        return pairs
    return plan


def _plan_pair_exchange(grp):
    def plan(refs, ss, rs):
        x, y, c, _ = _place()
        cps = [_remote(refs[0].at[pl.ds(pj * grp.block + (1 - c) * grp.half, grp.half), :],
                       refs[1].at[pl.ds(pj * grp.half, grp.half), :], ss, rs, pj, (x, y, 1 - c))
               for pj in range(N_CHIPS)]
        return [(cp, cp) for cp in cps]
    return plan


def _plan_to_chips(grp):
    def plan(refs, ss, rs):
        x, y, c, chips = _place()
        cps = [_remote(refs[0].at[pl.ds((2 * cx + cy) * grp.half, grp.half), :],
                       refs[1].at[pl.ds(k * grp.half, grp.half), :], ss, rs, k, (cx, cy, c))
               for k, (cx, cy) in enumerate(chips)]
        return [(cp, cp) for cp in cps]
    return plan


def _plan_share_halves(grp):
    def plan(refs, ss, rs):
        x, y, c, _ = _place()
        mine = refs[0].at[pl.ds(c * grp.half, grp.half), :]
        other = refs[0].at[pl.ds((1 - c) * grp.half, grp.half), :]
        return [(_remote(mine, mine, ss, rs, 0, (x, y, 1 - c)), _remote(other, other, ss, rs, 0, (x, y, 1 - c)))]
    return plan


def _empty(rows, dtype):
    return lax.empty((rows, 1024), dtype)


def _all_reduce_small(v, after):
    m_per, n = v.shape
    n_dev = 8

    def body(x_ref, after_ref, out_ref, all_ref, send_sems, recv_sems, local_sem):
        x, y, c, chips = _place()
        me, sibling = (x, y, c), (x, y, 1 - c)

        def rows(px, py, pc):
            return all_ref.at[pl.ds((4 * px + 2 * py + pc) * m_per, m_per), :]

        def copy(k, block, to, src=None):
            return pltpu.make_async_remote_copy(
                src_ref=rows(*block) if src is None else src, dst_ref=rows(*block),
                send_sem=send_sems.at[k], recv_sem=recv_sems.at[k], device_id=to, device_id_type=MESH)

        mine = pltpu.make_async_copy(x_ref, rows(*me), local_sem)
        mine.start()
        first = [copy(0, me, sibling, src=x_ref)]
        first += [copy(1 + k, me, (*chip, c), src=x_ref) for k, chip in enumerate(chips)]
        for cp in first:
            cp.start()
        passed = [copy(4 + k, (*chip, c), sibling) for k, chip in enumerate(chips)]
        for k, chip in enumerate(chips):
            copy(1 + k, (*chip, c), me).wait_recv()
            passed[k].start()
        copy(0, sibling, me).wait_recv()
        for k, chip in enumerate(chips):
            copy(4 + k, (*chip, 1 - c), me).wait_recv()
        for cp in first + passed:
            cp.wait_send()
        mine.wait()
        acc = all_ref[pl.ds(0, m_per), :]
        for d in range(1, n_dev):
            acc = acc + all_ref[pl.ds(d * m_per, m_per), :]
        out_ref[...] = acc

    return pl.pallas_call(
        body, name="all_reduce_small",
        in_specs=[_VMEM, _ANY], out_specs=_VMEM,
        out_shape=jax.ShapeDtypeStruct((m_per, n), F32),
        scratch_shapes=[pltpu.VMEM((n_dev * m_per, n), F32),
                        pltpu.SemaphoreType.DMA((7,)), pltpu.SemaphoreType.DMA((7,)), pltpu.SemaphoreType.DMA],
        compiler_params=pltpu.CompilerParams(has_side_effects=True),
    )(v, after)


def _tables(S):
    pos = jnp.arange(S, dtype=jnp.int32).astype(F32)

    def cs(half):
        inv_freq = ROPE_THETA ** (-jnp.arange(half, dtype=F32) / half)
        ang = pos[:, None] * inv_freq[None, :]
        return jnp.cos(ang), jnp.sin(ang)

    c64, s64 = cs(64)
    c32, s32 = cs(32)
    cr = jnp.concatenate([c64, c64], axis=1)
    sr = jnp.concatenate([-s64, s64], axis=1)
    ca = jnp.concatenate([c32, c32, c32, c32], axis=1)
    sa = jnp.concatenate([-s32, s32, -s32, s32], axis=1)
    return cr, sr, ca, sa


def _decays():
    H, C = RET_HEADS, SEQ_CHUNK
    log_gamma = jnp.log1p(-jnp.exp2(-5.0 - jnp.arange(H, dtype=F32)))
    idx = jnp.arange(C, dtype=F32)
    rel = idx[:, None] - idx[None, :]
    dmat = jnp.where(rel[None] >= 0, jnp.exp(log_gamma[:, None, None] * jnp.maximum(rel, 0.0)[None]), 0.0)
    qd = jnp.exp(log_gamma[:, None] * (idx + 1.0))[:, :, None]
    kd = jnp.exp(log_gamma[:, None] * (C - 1.0 - idx))[:, :, None]
    cd = jnp.exp(log_gamma * C)[:, None, None]
    return dmat, qd, kd, cd


def _local_step(place, x, tgt, ln1_g, b_in, ret_norm_g, attn_sinks, ln2_g, lnf_g, w_in, other_weights):
    S = x.shape[0]
    cr, sr, ca, sa = _tables(S)
    dmat, qd, kd, cd = _decays()

    h, rq, rk, rv, rg, aq, ak, av, ga, gb = _proj_fwd(x, ln1_g, w_in, b_in, cr, sr, ca, sa)
    ry, states = _ret_fwd(rq, rk, rv, dmat, qd, kd, cd)
    ay = _attn_fwd(aq, ak, av, attn_sinks)
    w_ret, w_attn, w_out, w_gate, w_up, w_down = other_weights(ay)
    u, ba, bb, merged, x1 = _merge_fwd(x, ry, rg, ay, ga, gb, ret_norm_g, w_ret, w_attn, w_out)
    gate, up, h2, act, dx2, loss, dg3 = _ffn_fwd(x1, tgt, ln2_g, lnf_g, w_gate, w_up, w_down)

    dgate, dup, dx1, dg2 = _ffn_bwd(dx2, gate, up, x1, ln2_g, w_gate, w_up, w_down)
    early = _wgrad([act], dx2, 6, None, "wgrad_down")
    early = _wgrad([dgate], h2, 4, early, "wgrad_gate")
    early = _wgrad([dup], h2, 5, early, "wgrad_up")
    dgates, drg, dba, dbb, day, dry, dgr, db_mid = _merge_bwd(dx1, ga, gb, ba, bb, rg, ry, ret_norm_g,
                                                                w_ret, w_attn, w_out)
    early = _wgrad([merged], dx1, 3, early, "wgrad_out")
    early = _wgrad([u], dba, 1, early, "wgrad_ret")
    e32, e16 = _wgrad([ay], dbb, 2, early, "wgrad_attn")

    grp = EARLY
    sems, flight, token = _exchange_start("pair_start_early", [e16, _empty(N_CHIPS * grp.half, BF)], N_CHIPS,
                                          _plan_pair_exchange(grp))
    daq, dak, dav, dsk = _attn_bwd(aq, ak, av, day, attn_sinks, ca, sa, token)
    _, got = _exchange_wait("pair_wait_early", sems, flight, _plan_pair_exchange(grp), after=(dsk,))
    s32, s16 = _pair_sum(place, e32, got, grp)
    sems, flight, token = _exchange_start("chips_start_early", [s16, _empty(3 * grp.half, BF)], 3, _plan_to_chips(grp))
    d_ret = _ret_bwd(rq, rk, rv, dry, states, dmat, qd, kd, cd, cr, sr, token)
    d_proj = [d_ret, drg, daq, dak, dav, dgates]
    grad_x, dg1, db = _proj_bwd(d_proj, x, dx1, ln1_g, w_in)
    late = _wgrad(d_proj, h, 0, None, "wgrad_in")
    _, others = _exchange_wait("chips_wait_early", sems, flight, _plan_to_chips(grp), after=(late[0],))
    early_shard = _chip_sum(place, s32, others, grp)

    d_sinks = dsk.reshape(2, 2, PAIRS, 128).sum(-1).transpose(0, 2, 1).reshape(1, ATTN_Q_HEADS)
    small = dict(loss=loss, ln1_g=dg1, b_in=db, ret_norm_g=dgr, attn_sinks=d_sinks,
                 ln2_g=dg2, lnf_g=dg3)
    return grad_x, early_shard, late, small


SMALL_ROWS = 16


def _pack_small(ln1, ret, ln2, lnf, b_in, sinks, loss, name):
    def body(a_ref, b_ref, c_ref, d_ref, bi_ref, sk_ref, ls_ref, o_ref):
        o_ref[...] = jnp.zeros_like(o_ref)
        for row, ref in enumerate((a_ref, b_ref, c_ref, d_ref)):
            o_ref[row:row + 1, :] = ref[...]
        for k in range(6):
            o_ref[4 + k:5 + k, :] = bi_ref[:, k * 1024:(k + 1) * 1024]
        o_ref[10:11, 0:D_IN - 6144] = bi_ref[:, 6144:D_IN]
        o_ref[11:12, 0:ATTN_Q_HEADS] = sk_ref[...]
        o_ref[12:13, 0:LANES] = ls_ref[...]

    return pl.pallas_call(
        body, name=name, in_specs=[_VMEM] * 7, out_specs=_VMEM,
        out_shape=jax.ShapeDtypeStruct((SMALL_ROWS, 1024), F32),
    )(ln1.reshape(1, 1024), ret.reshape(1, 1024), ln2.reshape(1, 1024), lnf.reshape(1, 1024),
      b_in.reshape(1, D_IN), sinks.reshape(1, ATTN_Q_HEADS), loss.reshape(1, LANES))


def _unpack_small(p):
    return dict(ln1_g=p[0:1], ret_norm_g=p[1:2], ln2_g=p[2:3], lnf_g=p[3], b_in=p[4:11].reshape(1, -1)[:, :D_IN],
                attn_sinks=p[11:12, :ATTN_Q_HEADS], loss=p[12, 0])


def kernel(x, ln1_g, w_in, b_in, ret_norm_g, w_ret_out, attn_sinks, w_attn_out, w_out, ln2_g, w_ffn_gate, w_ffn_up, w_ffn_down, lnf_g, loss_target, m_ln1_g, m_w_in, m_b_in, m_ret_norm_g, m_w_ret_out, m_attn_sinks, m_w_attn_out, m_w_out, m_ln2_g, m_w_ffn_gate, m_w_ffn_up, m_w_ffn_down, m_lnf_g, v_ln1_g, v_w_in, v_b_in, v_ret_norm_g, v_w_ret_out, v_attn_sinks, v_w_attn_out, v_w_out, v_ln2_g, v_w_ffn_gate, v_w_ffn_up, v_w_ffn_down, v_lnf_g):
    xi = lax.axis_index("x")
    yi = lax.axis_index("y")
    ci = lax.axis_index("c")
    j = 2 * xi + yi

    place = jnp.stack([j, ci]).astype(jnp.int32)

    names = ["w_in", "w_ret_out", "w_attn_out", "w_out", "w_ffn_gate", "w_ffn_up", "w_ffn_down"]
    transposed = [True, False, False, False, True, True, False]
    view = lambda a, t: a[0].T if t else a[0]
    big_w = [view(a, t) for a, t in zip([w_in, w_ret_out, w_attn_out, w_out, w_ffn_gate, w_ffn_up, w_ffn_down], transposed)]
    big_m = [view(a, t) for a, t in zip([m_w_in, m_w_ret_out, m_w_attn_out, m_w_out, m_w_ffn_gate, m_w_ffn_up, m_w_ffn_down], transposed)]
    big_v = [view(a, t) for a, t in zip([v_w_in, v_w_ret_out, v_w_attn_out, v_w_out, v_w_ffn_gate, v_w_ffn_up, v_w_ffn_down], transposed)]

    staged = [_stage_weight(place, w, "stage_" + n) for n, w in zip(names, big_w)]
    w_in_full = _gather_weights(staged[:1], [0])[0]
    rest = list(range(1, 7))
    sems, in_flight, token = _exchange_start("gather_start", staged[1:], 3 * len(rest), _plan_gather_direct(rest),
                                             after=(w_in_full,))
    other_weights = lambda after: _exchange_wait("gather_wait", sems, in_flight, _plan_gather_direct(rest),
                                                 after=(after,))

    grad_x, early_shard, (l32, l16), small = _local_step(
        place, x[0], loss_target[0], ln1_g, b_in + token[0, 0], ret_norm_g, attn_sinks, ln2_g, lnf_g.reshape(1, -1),
        w_in_full, other_weights)
    early_shard = _exchange("share_early", [early_shard], 1, _plan_share_halves(EARLY))[0]
    _, got = _exchange("pair_exchange_late", [l16, _empty(N_CHIPS * LATE.half, BF)], N_CHIPS, _plan_pair_exchange(LATE))
    s32, s16 = _pair_sum(place, l32, got, LATE)
    sems, flight, token = _exchange_start("chips_start_late", [s16, _empty(3 * LATE.half, BF)], 3, _plan_to_chips(LATE))

    upd = {}

    def update(t, shard, after):
        res = _adamw(big_w[t], shard, big_m[t], big_v[t], "adamw_" + names[t], g_row0=GROUP_OF[t].off(t), after=after)
        upd[names[t]] = tuple((r.T if transposed[t] else r)[None] for r in res)
        return res[0]

    done = [update(t, early_shard, token) for t in EARLY.tensors]

    pg = _all_reduce_small(_pack_small(
        small["ln1_g"], small["ret_norm_g"], small["ln2_g"], small["lnf_g"], small["b_in"],
        small["attn_sinks"], small["loss"], "pack_small_grads"), token)
    sm = _unpack_small(pg)
    z = jnp.zeros((1, LANES), F32)
    pw = _pack_small(ln1_g, ret_norm_g, ln2_g, lnf_g, b_in, attn_sinks, z, "pack_small_w")
    pm = _pack_small(m_ln1_g, m_ret_norm_g, m_ln2_g, m_lnf_g, m_b_in, m_attn_sinks, z, "pack_small_m")
    pv = _pack_small(v_ln1_g, v_ret_norm_g, v_ln2_g, v_lnf_g, v_b_in, v_attn_sinks, z, "pack_small_v")
    small_res = _adamw(pw, pg, pm, pv, "adamw_small")
    sd, smn, svn = (_unpack_small(t) for t in small_res[1:])
    for n in ["ln1_g", "b_in", "ret_norm_g", "attn_sinks", "ln2_g", "lnf_g"]:
        upd[n] = (sm[n], sd[n], smn[n], svn[n])

    _, others = _exchange_wait("chips_wait_late", sems, flight, _plan_to_chips(LATE), after=(*done, small_res[1]))
    late_shard = _exchange("share_late", [_chip_sum(place, s32, others, LATE)], 1, _plan_share_halves(LATE))[0]
    update(0, late_shard, None)

    order_w = ["ln1_g", "w_in", "b_in", "ret_norm_g", "w_ret_out", "attn_sinks", "w_attn_out", "w_out",
               "ln2_g", "w_ffn_gate", "w_ffn_up", "w_ffn_down", "lnf_g"]
    outs = [sm["loss"], grad_x[None]]
    for k in range(4):
        outs += [upd[n][k] for n in order_w]
    return tuple(outs)
```

```python
from typing import NamedTuple

import jax
import jax.numpy as jnp
from jax import lax
from jax.experimental import pallas as pl
from jax.experimental.pallas import tpu as pltpu

D_MODEL = 1024
SEQ_CHUNK = 128
RET_CHUNKS_PER_STEP = 2
RET_HEADS = 4
RET_QK_DIM = 128
RET_V_DIM = 256
ATTN_Q_HEADS = 16
ATTN_HEAD_DIM = 64
D_FF = 2816
D_IN = 6400
ROPE_THETA = 10000.0
EPS = 1e-6
N_CHIPS = 4

ADAM_LR = 0.001
ADAM_B1 = 0.9
ADAM_B2 = 0.999
ADAM_EPS = 1e-08
ADAM_WD = 0.01
ADAM_STEP = 10

OFF_RQ, OFF_RK, OFF_RV, OFF_RG = 0, 512, 1024, 2048
OFF_AQ, OFF_AK, OFF_AV, OFF_GA, OFF_GB = 3072, 4096, 4224, 4352, 5376

V7X_VMEM_LIMIT = 56 * 1024 * 1024
LANES = 128

BF = jnp.bfloat16
F32 = jnp.float32
MESH = pl.DeviceIdType.MESH

_VMEM = pl.BlockSpec(memory_space=pltpu.VMEM)
_SMEM = pl.BlockSpec(memory_space=pltpu.SMEM)
_ANY = pl.BlockSpec(memory_space=pl.ANY)


def _dot(a, b):
    return jnp.dot(a, b, preferred_element_type=F32)


def _dot_nt(a, b):
    return lax.dot_general(a, b, (((1,), (1,)), ((), ())), preferred_element_type=F32)


def _dot_tn(a, b):
    return lax.dot_general(a, b, (((0,), (0,)), ((), ())), preferred_element_type=F32)


def _sigmoid(x):
    return 1.0 / (1.0 + jnp.exp(-x))


def _rows(tm, n):
    return pl.BlockSpec((tm, n), lambda i: (i, 0))


def _fixed(shape):
    nd = len(shape)
    return pl.BlockSpec(shape, lambda i: (0,) * nd)


def _params(sem, **kw):
    return pltpu.CompilerParams(dimension_semantics=sem, vmem_limit_bytes=V7X_VMEM_LIMIT, **kw)


def _rot_ret(p, cr, sr):
    return p * cr + pltpu.roll(p, 64, 1) * sr


def _rot_ret_bwd(d, cr, sr):
    return d * cr - pltpu.roll(d, 64, 1) * sr


def _attn_partner(p):
    lane = lax.broadcasted_iota(jnp.int32, p.shape, 1)
    first = (lane % 64) < 32
    return jnp.where(first, pltpu.roll(p, 96, 1), pltpu.roll(p, 32, 1))


def _rot_attn(p, ca, sa):
    return p * ca + _attn_partner(p) * sa


def _rot_attn_bwd(d, ca, sa):
    return d * ca - _attn_partner(d) * sa


def _proj_fwd(x, g1, w_in, b_in, cr, sr, ca, sa):
    S = x.shape[0]
    tm = 512
    scale = RET_QK_DIM ** -0.5

    def body(x_ref, g_ref, w_ref, b_ref, cr_ref, sr_ref, ca_ref, sa_ref,
             h_ref, rq_ref, rk_ref, rv_ref, rg_ref, aq_ref, ak_ref, av_ref, ga_ref, gb_ref):
        xv = x_ref[...]
        r = lax.rsqrt(jnp.mean(xv * xv, axis=-1, keepdims=True) + EPS)
        h = (xv * r * g_ref[...]).astype(BF)
        h_ref[...] = h

        def piece(lo, n):
            return _dot_nt(h, w_ref[lo:lo + n, :]) + b_ref[:, lo:lo + n]

        crv, srv = cr_ref[...], sr_ref[...]
        pq = piece(OFF_RQ, 512)
        pk = piece(OFF_RK, 512)
        for hd in range(RET_HEADS):
            sl = slice(hd * 128, (hd + 1) * 128)
            rq_ref[:, sl] = (_rot_ret(pq[:, sl], crv, srv) * scale).astype(BF)
            rk_ref[:, sl] = _rot_ret(pk[:, sl], crv, srv)
        rv_ref[...] = piece(OFF_RV, 1024).astype(BF)
        rg_ref[...] = piece(OFF_RG, 1024)
        cav, sav = ca_ref[...], sa_ref[...]
        pa = piece(OFF_AQ, 1024)
        for blk in range(8):
            sl = slice(blk * 128, (blk + 1) * 128)
            aq_ref[:, sl] = _rot_attn(pa[:, sl], cav, sav).astype(BF)
        pkv = piece(OFF_AK, 256)
        ak_ref[...] = _rot_attn(pkv[:, :128], cav, sav).astype(BF)
        av_ref[...] = pkv[:, 128:].astype(BF)
        ga_ref[...] = piece(OFF_GA, 1024)
        gb_ref[...] = piece(OFF_GB, 1024)

    outs = [(D_MODEL, BF), (512, BF), (512, F32), (1024, BF), (1024, F32),
            (1024, BF), (128, BF), (128, BF), (1024, F32), (1024, F32)]
    return pl.pallas_call(
        body, name="proj_fwd", grid=(S // tm,),
        in_specs=[_rows(tm, D_MODEL), _fixed((1, D_MODEL)), _VMEM, _VMEM,
                  _rows(tm, 128), _rows(tm, 128), _rows(tm, 128), _rows(tm, 128)],
        out_specs=[_rows(tm, n) for n, _ in outs],
        out_shape=[jax.ShapeDtypeStruct((S, n), dt) for n, dt in outs],
        compiler_params=_params(("parallel",)),
    )(x, g1, w_in, b_in, cr, sr, ca, sa)


def _ret_fwd(rq, rk, rv, dmat, qd, kd, cd):
    S = rq.shape[0]
    N = S // SEQ_CHUNK
    cps = RET_CHUNKS_PER_STEP
    rows = cps * SEQ_CHUNK

    def body(q_ref, k_ref, v_ref, dm_ref, qd_ref, kd_ref, cd_ref, y_ref, st_ref, state):
        @pl.when(pl.program_id(0) == 0)
        def _():
            state[...] = jnp.zeros_like(state)

        for hd in range(RET_HEADS):
            qs = slice(hd * 128, (hd + 1) * 128)
            vs = slice(hd * 256, (hd + 1) * 256)
            s = state[hd]
            for ch in range(cps):
                rs = slice(ch * SEQ_CHUNK, (ch + 1) * SEQ_CHUNK)
                q = q_ref[rs, qs]
                kf = k_ref[rs, qs]
                k = kf.astype(BF)
                v = v_ref[rs, vs]
                sb = s.astype(BF)
                st_ref[ch, hd] = sb
                a = _dot_nt(q, k) * dm_ref[hd]
                inner = _dot(a.astype(BF), v)
                cross = _dot(q, sb) * qd_ref[hd]
                y_ref[rs, vs] = inner + cross
                kdv = (kf * kd_ref[hd]).astype(BF)
                s = s * cd_ref[hd] + _dot_tn(kdv, v)
            state[hd] = s

    return pl.pallas_call(
        body, name="ret_fwd", grid=(N // cps,),
        in_specs=[_rows(rows, 512), _rows(rows, 512), _rows(rows, 1024), _VMEM, _VMEM, _VMEM, _VMEM],
        out_specs=[_rows(rows, 1024), pl.BlockSpec((cps, RET_HEADS, 128, 256), lambda i: (i, 0, 0, 0))],
        out_shape=[jax.ShapeDtypeStruct((S, 1024), F32),
                   jax.ShapeDtypeStruct((N, RET_HEADS, 128, 256), BF)],
        scratch_shapes=[pltpu.VMEM((RET_HEADS, 128, 256), F32)],
        compiler_params=_params(("arbitrary",)),
    )(rq, rk, rv, dmat, qd, kd, cd)


def _kv_halves(prev_ref, cur_ref):
    kk = jnp.concatenate([prev_ref[...], cur_ref[...]], axis=0).astype(F32)
    return kk, pltpu.roll(kk, 64, 1)


PAIRS = 4


def _attn_mask(n):
    kj = lax.broadcasted_iota(jnp.int32, (256, PAIRS * 128), 0)
    qi = lax.broadcasted_iota(jnp.int32, (256, PAIRS * 128), 1) & 127
    first_prev = jnp.where(n > 0, 0, 128)
    prev_ok = jnp.logical_and(jnp.logical_and(kj < 128, kj >= first_prev), qi < kj)
    cur_ok = jnp.logical_and(kj >= 128, qi >= kj - 128)
    return jnp.logical_or(prev_ok, cur_ok)


def _group_operands(g, kk, kk_sw, vv, vv_sw):
    lo = lax.broadcasted_iota(jnp.int32, (256, 128), 1) < 64
    k_a, k_b = (kk, kk_sw) if g == 0 else (kk_sw, kk)
    v_a, v_b = (vv, vv_sw) if g == 0 else (vv_sw, vv)
    return (jnp.where(lo, k_a, 0.0).astype(BF), jnp.where(lo, 0.0, k_b).astype(BF),
            jnp.where(lo, v_a, 0.0).astype(BF), jnp.where(lo, 0.0, v_b).astype(BF))


def _stack_pairs(ref, g):
    return jnp.concatenate([ref[:, (4 * g + p) * 128:(4 * g + p + 1) * 128] for p in range(PAIRS)], axis=0)


def _sink_row(sk_ref, g, which):
    return jnp.concatenate([jnp.full((1, 128), sk_ref[0, 8 * g + 2 * p + which], F32) for p in range(PAIRS)], axis=1)


def _attn_probs(qs, kx, allowed, sink):
    s = _dot_nt(kx, qs) * (ATTN_HEAD_DIM ** -0.5)
    s = jnp.where(allowed, s, -1e30)
    m = jnp.maximum(jnp.max(s, axis=0, keepdims=True), sink)
    e = jnp.exp(s - m)
    es = jnp.exp(sink - m)
    inv = 1.0 / (jnp.sum(e, axis=0, keepdims=True) + es)
    return e * inv, es * inv


def _attn_fwd(aq, ak, av, sinks):
    S = aq.shape[0]
    N = S // SEQ_CHUNK

    def body(sk_ref, q_ref, kp_ref, kc_ref, vp_ref, vc_ref, o_ref):
        kk, kk_sw = _kv_halves(kp_ref, kc_ref)
        vv, vv_sw = _kv_halves(vp_ref, vc_ref)
        allowed = _attn_mask(pl.program_id(0))
        for g in range(2):
            k_lo, k_hi, v_lo, v_hi = _group_operands(g, kk, kk_sw, vv, vv_sw)
            qs = _stack_pairs(q_ref, g)
            p0, _ = _attn_probs(qs, k_lo, allowed, _sink_row(sk_ref, g, 0))
            p1, _ = _attn_probs(qs, k_hi, allowed, _sink_row(sk_ref, g, 1))
            o = _dot_tn(p0.astype(BF), v_lo) + _dot_tn(p1.astype(BF), v_hi)
            for p in range(PAIRS):
                o_ref[:, (4 * g + p) * 128:(4 * g + p + 1) * 128] = o[p * 128:(p + 1) * 128].astype(BF)

    prev = lambda i: (jnp.maximum(i - 1, 0), 0)
    return pl.pallas_call(
        body, name="attn_fwd", grid=(N,),
        in_specs=[_SMEM, _rows(128, 1024), pl.BlockSpec((128, 128), prev), _rows(128, 128),
                  pl.BlockSpec((128, 128), prev), _rows(128, 128)],
        out_specs=_rows(128, 1024),
        out_shape=jax.ShapeDtypeStruct((S, 1024), BF),
        compiler_params=_params(("parallel",)),
    )(sinks, aq, ak, ak, av, av)


def _group_norm(y):
    mu = jnp.mean(y, axis=-1, keepdims=True)
    yc = y - mu
    rstd = lax.rsqrt(jnp.mean(yc * yc, axis=-1, keepdims=True) + EPS)
    return yc * rstd, rstd


def _merge_fwd(x, ry, rg, ay, ga, gb, g_ret, w_ret, w_attn, w_out):
    S = x.shape[0]
    tm = 256

    def body(x_ref, ry_ref, rg_ref, ay_ref, ga_ref, gb_ref, gr_ref, wr_ref, wa_ref, wo_ref,
             u_ref, ba_ref, bb_ref, mg_ref, x1_ref):
        for hd in range(RET_HEADS):
            vs = slice(hd * 256, (hd + 1) * 256)
            yn, _ = _group_norm(ry_ref[:, vs])
            rgv = rg_ref[:, vs]
            u_ref[:, vs] = (rgv * _sigmoid(rgv) * (yn * gr_ref[:, vs])).astype(BF)
        ba = _dot(u_ref[...], wr_ref[...])
        bb = _dot(ay_ref[...], wa_ref[...])
        ba_ref[...] = ba
        bb_ref[...] = bb
        merged = (_sigmoid(ga_ref[...]) * ba + _sigmoid(gb_ref[...]) * bb).astype(BF)
        mg_ref[...] = merged
        x1_ref[...] = x_ref[...] + _dot(merged, wo_ref[...])

    outs = [BF, F32, F32, BF, F32]
    return pl.pallas_call(
        body, name="merge_fwd", grid=(S // tm,),
        in_specs=[_rows(tm, 1024)] * 6 + [_fixed((1, 1024)), _VMEM, _VMEM, _VMEM],
        out_specs=[_rows(tm, 1024)] * 5,
        out_shape=[jax.ShapeDtypeStruct((S, 1024), dt) for dt in outs],
        compiler_params=_params(("parallel",)),
    )(x, ry, rg, ay, ga, gb, g_ret, w_ret, w_attn, w_out)


def _ffn_fwd(x1, tgt, g2, g3, w_gate, w_up, w_down):
    S = x1.shape[0]
    tm = 256

    def body(x1_ref, t_ref, g2_ref, g3_ref, wg_ref, wu_ref, wd_ref,
             gate_ref, up_ref, h2_ref, act_ref, dx2_ref, loss_ref, dg3_ref):
        @pl.when(pl.program_id(0) == 0)
        def _():
            loss_ref[...] = jnp.zeros_like(loss_ref)
            dg3_ref[...] = jnp.zeros_like(dg3_ref)

        x1v = x1_ref[...]
        r2 = lax.rsqrt(jnp.mean(x1v * x1v, axis=-1, keepdims=True) + EPS)
        h2 = (x1v * r2 * g2_ref[...]).astype(BF)
        h2_ref[...] = h2
        gate = _dot_nt(h2, wg_ref[...])
        up = _dot_nt(h2, wu_ref[...])
        gate_ref[...] = gate
        up_ref[...] = up
        act = (gate * _sigmoid(gate) * up).astype(BF)
        act_ref[...] = act
        x2 = x1v + _dot(act, wd_ref[...])
        r3 = lax.rsqrt(jnp.mean(x2 * x2, axis=-1, keepdims=True) + EPS)
        xh = x2 * r3
        g3v = g3_ref[...]
        err = xh * g3v - t_ref[...]
        loss_ref[...] += 0.5 * jnp.sum(jnp.mean(err * err, axis=-1, keepdims=True), axis=0, keepdims=True)
        dy = err * (1.0 / D_MODEL)
        dg3_ref[...] += jnp.sum(dy * xh, axis=0, keepdims=True)
        dxh = dy * g3v
        dx2_ref[...] = r3 * (dxh - xh * jnp.mean(dxh * xh, axis=-1, keepdims=True))

    return pl.pallas_call(
        body, name="ffn_fwd", grid=(S // tm,),
        in_specs=[_rows(tm, 1024), _rows(tm, 1024), _fixed((1, 1024)), _fixed((1, 1024)), _VMEM, _VMEM, _VMEM],
        out_specs=[_rows(tm, D_FF), _rows(tm, D_FF), _rows(tm, 1024), _rows(tm, D_FF), _rows(tm, 1024),
                   _fixed((1, LANES)), _fixed((1, 1024))],
        out_shape=[jax.ShapeDtypeStruct((S, D_FF), F32), jax.ShapeDtypeStruct((S, D_FF), F32),
                   jax.ShapeDtypeStruct((S, 1024), BF), jax.ShapeDtypeStruct((S, D_FF), BF),
                   jax.ShapeDtypeStruct((S, 1024), F32),
                   jax.ShapeDtypeStruct((1, LANES), F32), jax.ShapeDtypeStruct((1, 1024), F32)],
        compiler_params=_params(("arbitrary",)),
    )(x1, tgt, g2, g3, w_gate, w_up, w_down)


def _ffn_bwd(dx2, gate, up, x1, g2, w_gate, w_up, w_down):
    S = x1.shape[0]
    tm = 256

    def body(dx2_ref, gate_ref, up_ref, x1_ref, g2_ref, wg_ref, wu_ref, wd_ref,
             dgate_ref, dup_ref, dx1_ref, dg2_ref):
        @pl.when(pl.program_id(0) == 0)
        def _():
            dg2_ref[...] = jnp.zeros_like(dg2_ref)

        dx2v = dx2_ref[...]
        dact = _dot_nt(dx2v.astype(BF), wd_ref[...])
        gate = gate_ref[...]
        sg = _sigmoid(gate)
        dgate = (dact * up_ref[...] * (sg * (1.0 + gate * (1.0 - sg)))).astype(BF)
        dup = (dact * (gate * sg)).astype(BF)
        dgate_ref[...] = dgate
        dup_ref[...] = dup
        dh2 = _dot(dgate, wg_ref[...]) + _dot(dup, wu_ref[...])
        x1v = x1_ref[...]
        r2 = lax.rsqrt(jnp.mean(x1v * x1v, axis=-1, keepdims=True) + EPS)
        xh = x1v * r2
        dg2_ref[...] += jnp.sum(dh2 * xh, axis=0, keepdims=True)
        dxh = dh2 * g2_ref[...]
        dx1_ref[...] = dx2v + r2 * (dxh - xh * jnp.mean(dxh * xh, axis=-1, keepdims=True))

    return pl.pallas_call(
        body, name="ffn_bwd", grid=(S // tm,),
        in_specs=[_rows(tm, 1024), _rows(tm, D_FF), _rows(tm, D_FF), _rows(tm, 1024), _fixed((1, 1024)),
                  _VMEM, _VMEM, _VMEM],
        out_specs=[_rows(tm, D_FF), _rows(tm, D_FF), _rows(tm, 1024), _fixed((1, 1024))],
        out_shape=[jax.ShapeDtypeStruct((S, D_FF), BF), jax.ShapeDtypeStruct((S, D_FF), BF),
                   jax.ShapeDtypeStruct((S, 1024), F32), jax.ShapeDtypeStruct((1, 1024), F32)],
        compiler_params=_params(("arbitrary",)),
    )(dx2, gate, up, x1, g2, w_gate, w_up, w_down)


def _merge_bwd(dx1, ga, gb, ba, bb, rg, ry, g_ret, w_ret, w_attn, w_out):
    S = dx1.shape[0]
    tm = 256

    def body(dx1_ref, ga_ref, gb_ref, ba_ref, bb_ref, rg_ref, ry_ref, gr_ref, wr_ref, wa_ref, wo_ref,
             dgates_ref, drg_ref, dba_ref, dbb_ref, day_ref, dry_ref, dgr_ref, db_ref):
        @pl.when(pl.program_id(0) == 0)
        def _():
            dgr_ref[...] = jnp.zeros_like(dgr_ref)
            db_ref[...] = jnp.zeros_like(db_ref)

        dm = _dot_nt(dx1_ref[...].astype(BF), wo_ref[...])
        sa = _sigmoid(ga_ref[...])
        sb = _sigmoid(gb_ref[...])
        dga = (dm * ba_ref[...] * (sa * (1.0 - sa))).astype(BF)
        dgb = (dm * bb_ref[...] * (sb * (1.0 - sb))).astype(BF)
        dgates_ref[:, :1024] = dga
        dgates_ref[:, 1024:] = dgb
        dba = (dm * sa).astype(BF)
        dbb = (dm * sb).astype(BF)
        dba_ref[...] = dba
        dbb_ref[...] = dbb
        day_ref[...] = _dot_nt(dbb, wa_ref[...]).astype(BF)
        du = _dot_nt(dba, wr_ref[...])
        for hd in range(RET_HEADS):
            vs = slice(hd * 256, (hd + 1) * 256)
            yn, rstd = _group_norm(ry_ref[:, vs])
            rgv = rg_ref[:, vs]
            sr = _sigmoid(rgv)
            silu = rgv * sr
            duv = du[:, vs]
            grv = gr_ref[:, vs]
            drg = (duv * (yn * grv) * (sr * (1.0 + rgv * (1.0 - sr)))).astype(BF)
            drg_ref[:, vs] = drg
            dyg = duv * silu
            dgr_ref[:, vs] += jnp.sum(dyg * yn, axis=0, keepdims=True)
            dyn = dyg * grv
            dry_ref[:, vs] = rstd * (dyn - jnp.mean(dyn, axis=-1, keepdims=True)
                                     - yn * jnp.mean(dyn * yn, axis=-1, keepdims=True))
            db_ref[:, vs] += jnp.sum(drg.astype(F32), axis=0, keepdims=True)
        db_ref[:, 1024:2048] += jnp.sum(dga.astype(F32), axis=0, keepdims=True)
        db_ref[:, 2048:] += jnp.sum(dgb.astype(F32), axis=0, keepdims=True)

    return pl.pallas_call(
        body, name="merge_bwd", grid=(S // tm,),
        in_specs=[_rows(tm, 1024)] * 7 + [_fixed((1, 1024)), _VMEM, _VMEM, _VMEM],
        out_specs=[_rows(tm, 2048), _rows(tm, 1024), _rows(tm, 1024), _rows(tm, 1024), _rows(tm, 1024),
                   _rows(tm, 1024), _fixed((1, 1024)), _fixed((1, 3072))],
        out_shape=[jax.ShapeDtypeStruct((S, 2048), BF), jax.ShapeDtypeStruct((S, 1024), BF),
                   jax.ShapeDtypeStruct((S, 1024), BF), jax.ShapeDtypeStruct((S, 1024), BF),
                   jax.ShapeDtypeStruct((S, 1024), BF), jax.ShapeDtypeStruct((S, 1024), F32),
                   jax.ShapeDtypeStruct((1, 1024), F32), jax.ShapeDtypeStruct((1, 3072), F32)],
        compiler_params=_params(("arbitrary",)),
    )(dx1, ga, gb, ba, bb, rg, ry, g_ret, w_ret, w_attn, w_out)


def _attn_bwd(aq, ak, av, day, sinks, ca, sa, after):
    S = aq.shape[0]
    N = S // SEQ_CHUNK

    def body(sk_ref, q_ref, kp_ref, kc_ref, vp_ref, vc_ref, do_ref, cq_ref, sq_ref, cp_ref, sp_ref, after_ref,
             dq_ref, dk_ref, dv_ref, dsk_ref, carry_k, carry_v):
        n = pl.program_id(0)

        @pl.when(n == 0)
        def _():
            carry_k[...] = jnp.zeros_like(carry_k)
            carry_v[...] = jnp.zeros_like(carry_v)
            dsk_ref[...] = jnp.zeros_like(dsk_ref)

        @pl.when(n < N)
        def _():
            kk, kk_sw = _kv_halves(kp_ref, kc_ref)
            vv, vv_sw = _kv_halves(vp_ref, vc_ref)
            lo = lax.broadcasted_iota(jnp.int32, (256, 128), 1) < 64
            allowed = _attn_mask(n)
            cq, sq = cq_ref[...], sq_ref[...]
            dk_groups, dv_groups = [], []
            for g in range(2):
                k_lo, k_hi, v_lo, v_hi = _group_operands(g, kk, kk_sw, vv, vv_sw)
                qs = _stack_pairs(q_ref, g)
                dos = _stack_pairs(do_ref, g)
                dq = jnp.zeros((PAIRS * 128, 128), F32)
                dk_half, dv_half = [], []
                for which, (kx, vx) in enumerate(((k_lo, v_lo), (k_hi, v_hi))):
                    pr, ps = _attn_probs(qs, kx, allowed, _sink_row(sk_ref, g, which))
                    dp = _dot_nt(vx, dos)
                    dr = jnp.sum(pr * dp, axis=0, keepdims=True)
                    ds = (pr * (dp - dr) * (ATTN_HEAD_DIM ** -0.5)).astype(BF)
                    dsk_ref[2 * g + which:2 * g + which + 1, :] -= ps * dr
                    dq = dq + _dot_tn(ds, kx)
                    dk_half.append(_dot(ds, qs))
                    dv_half.append(_dot(pr.astype(BF), dos))
                for p in range(PAIRS):
                    sl = slice((4 * g + p) * 128, (4 * g + p + 1) * 128)
                    dq_ref[:, sl] = _rot_attn_bwd(dq[p * 128:(p + 1) * 128], cq, sq).astype(BF)
                tk = jnp.where(lo, dk_half[0], dk_half[1])
                tv = jnp.where(lo, dv_half[0], dv_half[1])
                dk_groups.append(tk + pltpu.roll(tk, 64, 1))
                dv_groups.append(tv + pltpu.roll(tv, 64, 1))
            dk_full = jnp.where(lo, dk_groups[0], dk_groups[1])
            dv_full = jnp.where(lo, dv_groups[0], dv_groups[1])
            dk_ref[...] = _rot_attn_bwd(carry_k[...] + dk_full[:128], cp_ref[...], sp_ref[...]).astype(BF)
            dv_ref[...] = (carry_v[...] + dv_full[:128]).astype(BF)
            carry_k[...] = dk_full[128:]
            carry_v[...] = dv_full[128:]

        @pl.when(n == N)
        def _():
            dk_ref[...] = _rot_attn_bwd(carry_k[...], cp_ref[...], sp_ref[...]).astype(BF)
            dv_ref[...] = carry_v[...].astype(BF)

    cur = lambda i: (jnp.minimum(i, N - 1), 0)
    prev = lambda i: (jnp.maximum(i - 1, 0), 0)
    blk = lambda w, im: pl.BlockSpec((128, w), im)
    return pl.pallas_call(
        body, name="attn_bwd", grid=(N + 1,),
        in_specs=[_SMEM, blk(1024, cur), blk(128, prev), blk(128, cur), blk(128, prev), blk(128, cur),
                  blk(1024, cur), blk(128, cur), blk(128, cur), blk(128, prev), blk(128, prev), _ANY],
        out_specs=[blk(1024, cur), blk(128, prev), blk(128, prev), _fixed((4, PAIRS * 128))],
        out_shape=[jax.ShapeDtypeStruct((S, 1024), BF), jax.ShapeDtypeStruct((S, 128), BF),
                   jax.ShapeDtypeStruct((S, 128), BF), jax.ShapeDtypeStruct((4, PAIRS * 128), F32)],
        scratch_shapes=[pltpu.VMEM((128, 128), F32), pltpu.VMEM((128, 128), F32)],
        compiler_params=_params(("arbitrary",)),
    )(sinks, aq, ak, ak, av, av, day, ca, sa, ca, sa, after)


def _ret_bwd(rq, rk, rv, dry, states, dmat, qd, kd, cd, cr, sr, after):
    S = rq.shape[0]
    N = S // SEQ_CHUNK
    scale = RET_QK_DIM ** -0.5
    cps = RET_CHUNKS_PER_STEP
    steps = N // cps

    def body(q_ref, k_ref, v_ref, dy_ref, st_ref, dm_ref, qd_ref, kd_ref, cd_ref, cr_ref, sr_ref, after_ref,
             out_ref, dstate):
        @pl.when(pl.program_id(0) == 0)
        def _():
            dstate[...] = jnp.zeros_like(dstate)

        for hd in range(RET_HEADS):
            qs = slice(hd * 128, (hd + 1) * 128)
            vs = slice(hd * 256, (hd + 1) * 256)
            ds = dstate[hd]
            dmv = dm_ref[hd]
            for ch in reversed(range(cps)):
                rs = slice(ch * SEQ_CHUNK, (ch + 1) * SEQ_CHUNK)
                crv, srv = cr_ref[rs, :], sr_ref[rs, :]
                q = q_ref[rs, qs]
                kf = k_ref[rs, qs]
                k = kf.astype(BF)
                kdv = (kf * kd_ref[hd]).astype(BF)
                v = v_ref[rs, vs]
                do = dy_ref[rs, vs]
                dob = do.astype(BF)
                doq = (do * qd_ref[hd]).astype(BF)
                sn = st_ref[ch, hd]
                dsb = ds.astype(BF)
                a = (_dot_nt(q, k) * dmv).astype(BF)
                dv = _dot_tn(a, dob) + _dot(kdv, dsb)
                da = (_dot_nt(dob, v) * dmv).astype(BF)
                dq = _dot(da, k) + _dot_nt(doq, sn)
                dk = _dot_tn(da, q) + _dot_nt(v, dsb) * kd_ref[hd]
                ds = ds * cd_ref[hd] + _dot_tn(q, doq)
                out_ref[rs, qs] = _rot_ret_bwd(dq * scale, crv, srv).astype(BF)
                out_ref[rs, 512 + hd * 128:512 + (hd + 1) * 128] = _rot_ret_bwd(dk, crv, srv).astype(BF)
                out_ref[rs, 1024 + hd * 256:1024 + (hd + 1) * 256] = dv.astype(BF)
            dstate[hd] = ds

    rev = lambda w: pl.BlockSpec((cps * SEQ_CHUNK, w), lambda i: (steps - 1 - i, 0))
    return pl.pallas_call(
        body, name="ret_bwd", grid=(steps,),
        in_specs=[rev(512), rev(512), rev(1024), rev(1024),
                  pl.BlockSpec((cps, RET_HEADS, 128, 256), lambda i: (steps - 1 - i, 0, 0, 0)),
                  _VMEM, _VMEM, _VMEM, _VMEM, rev(128), rev(128), _ANY],
        out_specs=rev(2048),
        out_shape=jax.ShapeDtypeStruct((S, 2048), BF),
        scratch_shapes=[pltpu.VMEM((RET_HEADS, 128, 256), F32)],
        compiler_params=_params(("arbitrary",)),
    )(rq, rk, rv, dry, states, dmat, qd, kd, cd, cr, sr, after)


def _proj_bwd(d_pieces, x, dx1, g1, w_in_t):
    S = x.shape[0]
    tm = 256
    widths = [d.shape[1] for d in d_pieces]
    assert sum(widths) == D_IN
    nd = len(d_pieces)

    def body(*refs):
        d_refs = refs[:nd]
        x_ref, dx1_ref, g_ref, w_ref, gx_ref, dg_ref, db_ref = refs[nd:]

        @pl.when(pl.program_id(0) == 0)
        def _():
            dg_ref[...] = jnp.zeros_like(dg_ref)
            db_ref[...] = jnp.zeros_like(db_ref)

        dh = jnp.zeros((tm, D_MODEL), F32)
        lo = 0
        for ref, n in zip(d_refs, widths):
            v = ref[...]
            dh = dh + _dot(v, w_ref[lo:lo + n, :])
            db_ref[:, lo:lo + n] += jnp.sum(v.astype(F32), axis=0, keepdims=True)
            lo += n
        xv = x_ref[...]
        r = lax.rsqrt(jnp.mean(xv * xv, axis=-1, keepdims=True) + EPS)
        xh = xv * r
        dg_ref[...] += jnp.sum(dh * xh, axis=0, keepdims=True)
        dxh = dh * g_ref[...]
        gx_ref[...] = dx1_ref[...] + r * (dxh - xh * jnp.mean(dxh * xh, axis=-1, keepdims=True))

    return pl.pallas_call(
        body, name="proj_bwd", grid=(S // tm,),
        in_specs=[_rows(tm, n) for n in widths] + [_rows(tm, 1024), _rows(tm, 1024), _fixed((1, 1024)), _VMEM],
        out_specs=[_rows(tm, 1024), _fixed((1, 1024)), _fixed((1, D_IN))],
        out_shape=[jax.ShapeDtypeStruct((S, 1024), F32), jax.ShapeDtypeStruct((1, 1024), F32),
                   jax.ShapeDtypeStruct((1, D_IN), F32)],
        compiler_params=_params(("arbitrary",)),
    )(*d_pieces, x, dx1, g1, w_in_t)


SHARD_ROWS = (1600, 256, 256, 256, 704, 704, 704)


class Group(NamedTuple):
    name: str
    tensors: tuple
    offs: tuple
    block: int
    tile: int

    @property
    def half(self):
        return self.block // 2

    def off(self, t):
        return self.offs[self.tensors.index(t)]


LATE = Group("late", (0,), (0,), 1600, 800)
EARLY = Group("early", (1, 2, 3, 4, 5, 6), (0, 256, 512, 768, 1472, 2176), 2880, 480)
GROUP_OF = {t: g for g in (LATE, EARLY) for t in g.tensors}


def _wgrad(a_list, b, t, packed, name):
    S = b.shape[0]
    widths = [a.shape[1] for a in a_list]
    ts = 512 if sum(widths) > 4096 else 1024
    steps = S // ts
    na = len(a_list)
    grp = GROUP_OF[t]
    r, off = SHARD_ROWS[t], grp.off(t)
    assert sum(widths) == N_CHIPS * r
    n_in = na + 1 + (2 if packed is not None else 0)

    def body(*refs):
        a_refs, b_ref = refs[:na], refs[na]
        g32_ref, gb_ref, acc, stage, sems = refs[n_in:]
        i = pl.program_id(0)

        @pl.when(i == 0)
        def _():
            acc[...] = jnp.zeros_like(acc)

        bv = b_ref[...].astype(BF)
        lo = 0
        for a_ref, n in zip(a_refs, widths):
            acc[lo:lo + n, :] += _dot_tn(a_ref[...].astype(BF), bv)
            lo += n

        @pl.when(i == steps - 1)
        def _():
            for chip in range(N_CHIPS):
                src = acc.at[chip * r:(chip + 1) * r, :]
                dst = pl.ds(chip * grp.block + off, r)
                c32 = pltpu.make_async_copy(src, g32_ref.at[dst, :], sems.at[0])
                c32.start()
                stage[...] = acc[chip * r:(chip + 1) * r, :].astype(BF)
                c16 = pltpu.make_async_copy(stage, gb_ref.at[dst, :], sems.at[1])
                c16.start()
                c32.wait()
                c16.wait()

    shapes = [jax.ShapeDtypeStruct((N_CHIPS * grp.block, 1024), F32),
              jax.ShapeDtypeStruct((N_CHIPS * grp.block, 1024), BF)]
    operands = list(a_list) + [b]
    in_specs = [_rows(ts, n) for n in widths] + [_rows(ts, 1024)]
    aliases = {}
    if packed is not None:
        operands += list(packed)
        in_specs += [_ANY, _ANY]
        aliases = {na + 1: 0, na + 2: 1}
    return pl.pallas_call(
        body, name=name, grid=(steps,),
        in_specs=in_specs, out_specs=[_ANY, _ANY], out_shape=shapes,
        scratch_shapes=[pltpu.VMEM((N_CHIPS * r, 1024), F32), pltpu.VMEM((r, 1024), BF),
                        pltpu.SemaphoreType.DMA((2,))],
        input_output_aliases=aliases,
        compiler_params=_params(("arbitrary",)),
    )(*operands)


def _stage_weight(place, w, name):
    r = w.shape[0]
    tr = 800 if r == 1600 else r
    nb = r // tr

    def body(p_ref, w_ref, o_ref):
        o_ref[...] = w_ref[...].astype(BF)

    return pl.pallas_call(
        body, name=name,
        grid_spec=pltpu.PrefetchScalarGridSpec(
            num_scalar_prefetch=1, grid=(nb,),
            in_specs=[pl.BlockSpec((tr, 1024), lambda i, p: (i, 0))],
            out_specs=pl.BlockSpec((tr, 1024), lambda i, p: (p[0] * nb + i, 0))),
        out_shape=jax.ShapeDtypeStruct((N_CHIPS * r, 1024), BF),
        compiler_params=_params(("arbitrary",)),
    )(place, w)


def _pair_sum(place, g32, got, grp):
    nb = grp.half // grp.tile

    def body(p_ref, g_ref, r_ref, s32_ref, s16_ref):
        s = g_ref[...] + r_ref[...].astype(F32)
        s32_ref[...] = s
        s16_ref[...] = s.astype(BF)

    flat = pl.BlockSpec((grp.tile, 1024), lambda i, p: (i, 0))
    return pl.pallas_call(
        body, name="pair_sum_" + grp.name,
        grid_spec=pltpu.PrefetchScalarGridSpec(
            num_scalar_prefetch=1, grid=(N_CHIPS * nb,),
            in_specs=[pl.BlockSpec((grp.tile, 1024), lambda i, p: ((2 * (i // nb) + p[1]) * nb + i % nb, 0)), flat],
            out_specs=[flat, flat]),
        out_shape=[jax.ShapeDtypeStruct((N_CHIPS * grp.half, 1024), F32),
                   jax.ShapeDtypeStruct((N_CHIPS * grp.half, 1024), BF)],
        compiler_params=_params(("arbitrary",)),
    )(place, g32, got)


def _chip_sum(place, s32, others, grp):
    nb = grp.half // grp.tile

    def body(p_ref, a_ref, r0_ref, r1_ref, r2_ref, o_ref):
        o_ref[...] = ((a_ref[...] + r0_ref[...].astype(F32)) + r1_ref[...].astype(F32)) + r2_ref[...].astype(F32)

    blk = lambda f: pl.BlockSpec((grp.tile, 1024), f)
    return pl.pallas_call(
        body, name="chip_sum_" + grp.name,
        grid_spec=pltpu.PrefetchScalarGridSpec(
            num_scalar_prefetch=1, grid=(nb,),
            in_specs=[blk(lambda i, p: (p[0] * nb + i, 0)), blk(lambda i, p: (i, 0)),
                      blk(lambda i, p: (nb + i, 0)), blk(lambda i, p: (2 * nb + i, 0))],
            out_specs=blk(lambda i, p: (p[1] * nb + i, 0))),
        out_shape=jax.ShapeDtypeStruct((grp.block, 1024), F32),
        compiler_params=_params(("arbitrary",)),
    )(place, s32, others, others, others)


def _adamw(w, g, m, v, name, g_row0=0, after=None):
    r = w.shape[0]
    tr = 320 if r % 320 == 0 and g_row0 % 320 == 0 else (64 if r % 64 == 0 else r)
    assert g_row0 % tr == 0
    g0 = g_row0 // tr
    extra = [] if after is None else [after]

    def body(w_ref, g_ref, m_ref, v_ref, *rest):
        go_ref, d_ref, mo_ref, vo_ref = rest[len(extra):]
        gv = g_ref[...]
        mn = ADAM_B1 * m_ref[...] + (1.0 - ADAM_B1) * gv
        vn = ADAM_B2 * v_ref[...] + (1.0 - ADAM_B2) * (gv * gv)
        m_hat = mn / (1.0 - ADAM_B1 ** ADAM_STEP)
        v_hat = vn / (1.0 - ADAM_B2 ** ADAM_STEP)
        go_ref[...] = gv
        d_ref[...] = -ADAM_LR * (m_hat / (jnp.sqrt(v_hat) + ADAM_EPS) + ADAM_WD * w_ref[...])
        mo_ref[...] = mn
        vo_ref[...] = vn

    return pl.pallas_call(
        body, name=name, grid=(r // tr,),
        in_specs=[_rows(tr, 1024), pl.BlockSpec((tr, 1024), lambda i: (g0 + i, 0)), _rows(tr, 1024), _rows(tr, 1024)]
        + [_ANY] * len(extra),
        out_specs=[_rows(tr, 1024)] * 4,
        out_shape=[jax.ShapeDtypeStruct((r, 1024), F32)] * 4,
        compiler_params=_params(("parallel",)),
    )(w, g, m, v, *extra)


def _place():
    x, y, c = lax.axis_index("x"), lax.axis_index("y"), lax.axis_index("c")
    chips = [(1 - x, y), (x, 1 - y), (1 - x, 1 - y)]
    return x, y, c, chips


def _gather_weights(staged, tids):
    nt = len(staged)

    def body(*refs):
        out_refs = refs[nt:2 * nt]
        send_sems, recv_sems = refs[2 * nt:]
        x, y, c, chips = _place()
        j = 2 * x + y
        me, sibling = (x, y, c), (x, y, 1 - c)
        theirs = [2 * cx + cy for cx, cy in chips]

        def copy(t, k, pj, half, to):
            h = SHARD_ROWS[tids[t]] // 2
            rows = out_refs[t].at[pl.ds((2 * pj + half) * h, h), :]
            return pltpu.make_async_remote_copy(
                src_ref=rows, dst_ref=rows, send_sem=send_sems.at[6 * t + k], recv_sem=recv_sems.at[6 * t + k],
                device_id=to, device_id_type=MESH)

        first = [copy(t, k, j, c, (*chip, c)) for t in range(nt) for k, chip in enumerate(chips)]
        for cp in first:
            cp.start()
        passed = []
        for t in range(nt):
            for k in range(3):
                copy(t, k, theirs[k], c, me).wait_recv()
                fwd = copy(t, 3 + k, theirs[k], c, sibling)
                fwd.start()
                passed.append(fwd)
        for t in range(nt):
            for k in range(3):
                copy(t, 3 + k, theirs[k], 1 - c, me).wait_recv()
        for cp in first + passed:
            cp.wait_send()

    return pl.pallas_call(
        body, name="gather_weights",
        in_specs=[_ANY] * nt, out_specs=[_ANY] * nt,
        out_shape=[jax.ShapeDtypeStruct(a.shape, a.dtype) for a in staged],
        scratch_shapes=[pltpu.SemaphoreType.DMA((6 * nt,)), pltpu.SemaphoreType.DMA((6 * nt,))],
        input_output_aliases={t: t for t in range(nt)},
    )(*staged)


_HBM = pl.BlockSpec(memory_space=pltpu.HBM)
_SEM = pl.BlockSpec(memory_space=pltpu.SEMAPHORE)
_DATAFLOW = pltpu.SideEffectType.DATAFLOW_SIDE_EFFECTING


def _exchange(name, arrays, n, plan):
    na = len(arrays)

    def body(*refs):
        pairs = plan(refs[na:2 * na], refs[2 * na], refs[2 * na + 1])
        for send, _ in pairs:
            send.start()
        for send, recv in pairs:
            send.wait_send()
            recv.wait_recv()

    return pl.pallas_call(
        body, name=name, in_specs=[_ANY] * na, out_specs=[_ANY] * na,
        out_shape=[jax.ShapeDtypeStruct(a.shape, a.dtype) for a in arrays],
        scratch_shapes=[pltpu.SemaphoreType.DMA((n,)), pltpu.SemaphoreType.DMA((n,))],
        input_output_aliases={i: i for i in range(na)},
    )(*arrays)


def _exchange_start(name, arrays, n, plan, after=()):
    na = len(arrays)

    def body(*refs):
        for send, _ in plan(refs[:na], refs[na + len(after)], refs[na + len(after) + 1]):
            send.start()
        refs[-1][...] = jnp.zeros_like(refs[-1])

    out = pl.pallas_call(
        body, name=name,
        out_shape=(pltpu.SemaphoreType.DMA((n,)), pltpu.SemaphoreType.DMA((n,)),
                   *[pltpu.HBM(a.shape, a.dtype) for a in arrays], jax.ShapeDtypeStruct((8, LANES), F32)),
        in_specs=[_HBM] * na + [_ANY] * len(after), out_specs=(_SEM, _SEM, *[_HBM] * na, _VMEM),
        input_output_aliases={i: 2 + i for i in range(na)},
        compiler_params=pltpu.CompilerParams(has_side_effects=_DATAFLOW),
    )(*[pltpu.with_memory_space_constraint(a, pltpu.HBM) for a in arrays], *after)
    return (out[0], out[1]), list(out[2:2 + na]), out[-1]


def _exchange_wait(name, sems, in_flight, plan, after=()):
    na = len(in_flight)

    def body(*refs):
        for send, recv in plan(refs[:na], refs[na], refs[na + 1]):
            send.wait_send()
            recv.wait_recv()

    out = pl.pallas_call(
        body, name=name,
        out_shape=tuple(pltpu.HBM(a.shape, a.dtype) for a in in_flight),
        in_specs=[_HBM] * na + [_SEM, _SEM] + [_ANY] * len(after), out_specs=tuple([_HBM] * na),
        input_output_aliases={i: i for i in range(na)},
        compiler_params=pltpu.CompilerParams(has_side_effects=_DATAFLOW),
    )(*in_flight, *sems, *after)
    return list(out)


def _remote(src, dst, send_sems, recv_sems, k, to):
    return pltpu.make_async_remote_copy(src_ref=src, dst_ref=dst, send_sem=send_sems.at[k], recv_sem=recv_sems.at[k],
                                        device_id=to, device_id_type=MESH)


def _plan_gather_direct(tids):
    def plan(refs, ss, rs):
        x, y, c, chips = _place()
        j = 2 * x + y
        pairs = []
        for t, tid in enumerate(tids):
            r = SHARD_ROWS[tid]
            for k, (cx, cy) in enumerate(chips):
                mine = refs[t].at[pl.ds(j * r, r), :]
                theirs = refs[t].at[pl.ds((2 * cx + cy) * r, r), :]
                pairs.append((_remote(mine, mine, ss, rs, 3 * t + k, (cx, cy, c)),
                              _remote(theirs, theirs, ss, rs, 3 * t + k, (x, y, c))))
        return pairs
    return plan


def _plan_pair_exchange(grp):
    def plan(refs, ss, rs):
        x, y, c, _ = _place()
        cps = [_remote(refs[0].at[pl.ds(pj * grp.block + (1 - c) * grp.half, grp.half), :],
                       refs[1].at[pl.ds(pj * grp.half, grp.half), :], ss, rs, pj, (x, y, 1 - c))
               for pj in range(N_CHIPS)]
        return [(cp, cp) for cp in cps]
    return plan


def _plan_to_chips(grp):
    def plan(refs, ss, rs):
        x, y, c, chips = _place()
        cps = [_remote(refs[0].at[pl.ds((2 * cx + cy) * grp.half, grp.half), :],
                       refs[1].at[pl.ds(k * grp.half, grp.half), :], ss, rs, k, (cx, cy, c))
               for k, (cx, cy) in enumerate(chips)]
        return [(cp, cp) for cp in cps]
    return plan


def _plan_share_halves(grp):
    def plan(refs, ss, rs):
        x, y, c, _ = _place()
        mine = refs[0].at[pl.ds(c * grp.half, grp.half), :]
        other = refs[0].at[pl.ds((1 - c) * grp.half, grp.half), :]
        return [(_remote(mine, mine, ss, rs, 0, (x, y, 1 - c)), _remote(other, other, ss, rs, 0, (x, y, 1 - c)))]
    return plan


def _empty(rows, dtype):
    return lax.empty((rows, 1024), dtype)


def _all_reduce_small(v, after):
    m_per, n = v.shape
    n_dev = 8

    def body(x_ref, after_ref, out_ref, all_ref, send_sems, recv_sems, local_sem):
        x, y, c, chips = _place()
        me, sibling = (x, y, c), (x, y, 1 - c)

        def rows(px, py, pc):
            return all_ref.at[pl.ds((4 * px + 2 * py + pc) * m_per, m_per), :]

        def copy(k, block, to, src=None):
            return pltpu.make_async_remote_copy(
                src_ref=rows(*block) if src is None else src, dst_ref=rows(*block),
                send_sem=send_sems.at[k], recv_sem=recv_sems.at[k], device_id=to, device_id_type=MESH)

        mine = pltpu.make_async_copy(x_ref, rows(*me), local_sem)
        mine.start()
        first = [copy(0, me, sibling, src=x_ref)]
        first += [copy(1 + k, me, (*chip, c), src=x_ref) for k, chip in enumerate(chips)]
        for cp in first:
            cp.start()
        passed = [copy(4 + k, (*chip, c), sibling) for k, chip in enumerate(chips)]
        for k, chip in enumerate(chips):
            copy(1 + k, (*chip, c), me).wait_recv()
            passed[k].start()
        copy(0, sibling, me).wait_recv()
        for k, chip in enumerate(chips):
            copy(4 + k, (*chip, 1 - c), me).wait_recv()
        for cp in first + passed:
            cp.wait_send()
        mine.wait()
        acc = all_ref[pl.ds(0, m_per), :]
        for d in range(1, n_dev):
            acc = acc + all_ref[pl.ds(d * m_per, m_per), :]
        out_ref[...] = acc

    return pl.pallas_call(
        body, name="all_reduce_small",
        in_specs=[_VMEM, _ANY], out_specs=_VMEM,
        out_shape=jax.ShapeDtypeStruct((m_per, n), F32),
        scratch_shapes=[pltpu.VMEM((n_dev * m_per, n), F32),
                        pltpu.SemaphoreType.DMA((7,)), pltpu.SemaphoreType.DMA((7,)), pltpu.SemaphoreType.DMA],
        compiler_params=pltpu.CompilerParams(has_side_effects=True),
    )(v, after)


def _tables(S):
    pos = jnp.arange(S, dtype=jnp.int32).astype(F32)

    def cs(half):
        inv_freq = ROPE_THETA ** (-jnp.arange(half, dtype=F32) / half)
        ang = pos[:, None] * inv_freq[None, :]
        return jnp.cos(ang), jnp.sin(ang)

    c64, s64 = cs(64)
    c32, s32 = cs(32)
    cr = jnp.concatenate([c64, c64], axis=1)
    sr = jnp.concatenate([-s64, s64], axis=1)
    ca = jnp.concatenate([c32, c32, c32, c32], axis=1)
    sa = jnp.concatenate([-s32, s32, -s32, s32], axis=1)
    return cr, sr, ca, sa


def _decays():
    H, C = RET_HEADS, SEQ_CHUNK
    log_gamma = jnp.log1p(-jnp.exp2(-5.0 - jnp.arange(H, dtype=F32)))
    idx = jnp.arange(C, dtype=F32)
    rel = idx[:, None] - idx[None, :]
    dmat = jnp.where(rel[None] >= 0, jnp.exp(log_gamma[:, None, None] * jnp.maximum(rel, 0.0)[None]), 0.0)
    qd = jnp.exp(log_gamma[:, None] * (idx + 1.0))[:, :, None]
    kd = jnp.exp(log_gamma[:, None] * (C - 1.0 - idx))[:, :, None]
    cd = jnp.exp(log_gamma * C)[:, None, None]
    return dmat, qd, kd, cd


def _local_step(place, x, tgt, ln1_g, b_in, ret_norm_g, attn_sinks, ln2_g, lnf_g, w_in, other_weights):
    S = x.shape[0]
    cr, sr, ca, sa = _tables(S)
    dmat, qd, kd, cd = _decays()

    h, rq, rk, rv, rg, aq, ak, av, ga, gb = _proj_fwd(x, ln1_g, w_in, b_in, cr, sr, ca, sa)
    ry, states = _ret_fwd(rq, rk, rv, dmat, qd, kd, cd)
    ay = _attn_fwd(aq, ak, av, attn_sinks)
    w_ret, w_attn, w_out, w_gate, w_up, w_down = other_weights(ay)
    u, ba, bb, merged, x1 = _merge_fwd(x, ry, rg, ay, ga, gb, ret_norm_g, w_ret, w_attn, w_out)
    gate, up, h2, act, dx2, loss, dg3 = _ffn_fwd(x1, tgt, ln2_g, lnf_g, w_gate, w_up, w_down)

    dgate, dup, dx1, dg2 = _ffn_bwd(dx2, gate, up, x1, ln2_g, w_gate, w_up, w_down)
    early = _wgrad([act], dx2, 6, None, "wgrad_down")
    early = _wgrad([dgate], h2, 4, early, "wgrad_gate")
    early = _wgrad([dup], h2, 5, early, "wgrad_up")
    dgates, drg, dba, dbb, day, dry, dgr, db_mid = _merge_bwd(dx1, ga, gb, ba, bb, rg, ry, ret_norm_g,
                                                                w_ret, w_attn, w_out)
    early = _wgrad([merged], dx1, 3, early, "wgrad_out")
    early = _wgrad([u], dba, 1, early, "wgrad_ret")
    e32, e16 = _wgrad([ay], dbb, 2, early, "wgrad_attn")

    grp = EARLY
    sems, flight, token = _exchange_start("pair_start_early", [e16, _empty(N_CHIPS * grp.half, BF)], N_CHIPS,
                                          _plan_pair_exchange(grp))
    daq, dak, dav, dsk = _attn_bwd(aq, ak, av, day, attn_sinks, ca, sa, token)
    _, got = _exchange_wait("pair_wait_early", sems, flight, _plan_pair_exchange(grp), after=(dsk,))
    s32, s16 = _pair_sum(place, e32, got, grp)
    sems, flight, token = _exchange_start("chips_start_early", [s16, _empty(3 * grp.half, BF)], 3, _plan_to_chips(grp))
    d_ret = _ret_bwd(rq, rk, rv, dry, states, dmat, qd, kd, cd, cr, sr, token)
    d_proj = [d_ret, drg, daq, dak, dav, dgates]
    grad_x, dg1, db = _proj_bwd(d_proj, x, dx1, ln1_g, w_in)
    late = _wgrad(d_proj, h, 0, None, "wgrad_in")
    _, others = _exchange_wait("chips_wait_early", sems, flight, _plan_to_chips(grp), after=(late[0],))
    early_shard = _chip_sum(place, s32, others, grp)

    d_sinks = dsk.reshape(2, 2, PAIRS, 128).sum(-1).transpose(0, 2, 1).reshape(1, ATTN_Q_HEADS)
    small = dict(loss=loss, ln1_g=dg1, b_in=db, ret_norm_g=dgr, attn_sinks=d_sinks,
                 ln2_g=dg2, lnf_g=dg3)
    return grad_x, early_shard, late, small


SMALL_ROWS = 16


def _pack_small(ln1, ret, ln2, lnf, b_in, sinks, loss, name):
    def body(a_ref, b_ref, c_ref, d_ref, bi_ref, sk_ref, ls_ref, o_ref):
        o_ref[...] = jnp.zeros_like(o_ref)
        for row, ref in enumerate((a_ref, b_ref, c_ref, d_ref)):
            o_ref[row:row + 1, :] = ref[...]
        for k in range(6):
            o_ref[4 + k:5 + k, :] = bi_ref[:, k * 1024:(k + 1) * 1024]
        o_ref[10:11, 0:D_IN - 6144] = bi_ref[:, 6144:D_IN]
        o_ref[11:12, 0:ATTN_Q_HEADS] = sk_ref[...]
        o_ref[12:13, 0:LANES] = ls_ref[...]

    return pl.pallas_call(
        body, name=name, in_specs=[_VMEM] * 7, out_specs=_VMEM,
        out_shape=jax.ShapeDtypeStruct((SMALL_ROWS, 1024), F32),
    )(ln1.reshape(1, 1024), ret.reshape(1, 1024), ln2.reshape(1, 1024), lnf.reshape(1, 1024),
      b_in.reshape(1, D_IN), sinks.reshape(1, ATTN_Q_HEADS), loss.reshape(1, LANES))


def _unpack_small(p):
    return dict(ln1_g=p[0:1], ret_norm_g=p[1:2], ln2_g=p[2:3], lnf_g=p[3], b_in=p[4:11].reshape(1, -1)[:, :D_IN],
                attn_sinks=p[11:12, :ATTN_Q_HEADS], loss=p[12, 0])


def kernel(x, ln1_g, w_in, b_in, ret_norm_g, w_ret_out, attn_sinks, w_attn_out, w_out, ln2_g, w_ffn_gate, w_ffn_up, w_ffn_down, lnf_g, loss_target, m_ln1_g, m_w_in, m_b_in, m_ret_norm_g, m_w_ret_out, m_attn_sinks, m_w_attn_out, m_w_out, m_ln2_g, m_w_ffn_gate, m_w_ffn_up, m_w_ffn_down, m_lnf_g, v_ln1_g, v_w_in, v_b_in, v_ret_norm_g, v_w_ret_out, v_attn_sinks, v_w_attn_out, v_w_out, v_ln2_g, v_w_ffn_gate, v_w_ffn_up, v_w_ffn_down, v_lnf_g):
    xi = lax.axis_index("x")
    yi = lax.axis_index("y")
    ci = lax.axis_index("c")
    j = 2 * xi + yi

    place = jnp.stack([j, ci]).astype(jnp.int32)

    names = ["w_in", "w_ret_out", "w_attn_out", "w_out", "w_ffn_gate", "w_ffn_up", "w_ffn_down"]
    transposed = [True, False, False, False, True, True, False]
    view = lambda a, t: a[0].T if t else a[0]
    big_w = [view(a, t) for a, t in zip([w_in, w_ret_out, w_attn_out, w_out, w_ffn_gate, w_ffn_up, w_ffn_down], transposed)]
    big_m = [view(a, t) for a, t in zip([m_w_in, m_w_ret_out, m_w_attn_out, m_w_out, m_w_ffn_gate, m_w_ffn_up, m_w_ffn_down], transposed)]
    big_v = [view(a, t) for a, t in zip([v_w_in, v_w_ret_out, v_w_attn_out, v_w_out, v_w_ffn_gate, v_w_ffn_up, v_w_ffn_down], transposed)]

    staged = [_stage_weight(place, w, "stage_" + n) for n, w in zip(names, big_w)]
    w_in_full = _gather_weights(staged[:1], [0])[0]
    rest = list(range(1, 7))
    sems, in_flight, token = _exchange_start("gather_start", staged[1:], 3 * len(rest), _plan_gather_direct(rest),
                                             after=(w_in_full,))
    other_weights = lambda after: _exchange_wait("gather_wait", sems, in_flight, _plan_gather_direct(rest),
                                                 after=(after,))

    grad_x, early_shard, (l32, l16), small = _local_step(
        place, x[0], loss_target[0], ln1_g, b_in + token[0, 0], ret_norm_g, attn_sinks, ln2_g, lnf_g.reshape(1, -1),
        w_in_full, other_weights)
    pg = _all_reduce_small(_pack_small(
        small["ln1_g"], small["ret_norm_g"], small["ln2_g"], small["lnf_g"], small["b_in"],
        small["attn_sinks"], small["loss"], "pack_small_grads"), l32)
    early_shard = _exchange("share_early", [early_shard], 1, _plan_share_halves(EARLY))[0]
    _, got = _exchange("pair_exchange_late", [l16, _empty(N_CHIPS * LATE.half, BF)], N_CHIPS, _plan_pair_exchange(LATE))
    s32, s16 = _pair_sum(place, l32, got, LATE)
    sems, flight, token = _exchange_start("chips_start_late", [s16, _empty(3 * LATE.half, BF)], 3, _plan_to_chips(LATE),
                                          after=(pg,))

    upd = {}

    def update(t, shard, after):
        res = _adamw(big_w[t], shard, big_m[t], big_v[t], "adamw_" + names[t], g_row0=GROUP_OF[t].off(t), after=after)
        upd[names[t]] = tuple((r.T if transposed[t] else r)[None] for r in res)
        return res[0]

    done = [update(t, early_shard, token) for t in EARLY.tensors]

    sm = _unpack_small(pg)
    z = jnp.zeros((1, LANES), F32)
    pw = _pack_small(ln1_g, ret_norm_g, ln2_g, lnf_g, b_in, attn_sinks, z, "pack_small_w")
    pm = _pack_small(m_ln1_g, m_ret_norm_g, m_ln2_g, m_lnf_g, m_b_in, m_attn_sinks, z, "pack_small_m")
    pv = _pack_small(v_ln1_g, v_ret_norm_g, v_ln2_g, v_lnf_g, v_b_in, v_attn_sinks, z, "pack_small_v")
    small_res = _adamw(pw, pg, pm, pv, "adamw_small", after=token)
    sd, smn, svn = (_unpack_small(t) for t in small_res[1:])
    for n in ["ln1_g", "b_in", "ret_norm_g", "attn_sinks", "ln2_g", "lnf_g"]:
        upd[n] = (sm[n], sd[n], smn[n], svn[n])

    _, others = _exchange_wait("chips_wait_late", sems, flight, _plan_to_chips(LATE), after=(*done, small_res[1]))
    late_shard = _exchange("share_late", [_chip_sum(place, s32, others, LATE)], 1, _plan_share_halves(LATE))[0]
    update(0, late_shard, None)

    order_w = ["ln1_g", "w_in", "b_in", "ret_norm_g", "w_ret_out", "attn_sinks", "w_attn_out", "w_out",
               "ln2_g", "w_ffn_gate", "w_ffn_up", "w_ffn_down", "lnf_g"]
    outs = [sm["loss"], grad_x[None]]
    for k in range(4):
        outs += [upd[n][k] for n in order_w]
    return tuple(outs)
```

```python
from typing import NamedTuple

import jax
import jax.numpy as jnp
from jax import lax
from jax.experimental import pallas as pl
from jax.experimental.pallas import tpu as pltpu

D_MODEL = 1024
SEQ_CHUNK = 128
RET_CHUNKS_PER_STEP = 2
RET_HEADS = 4
RET_QK_DIM = 128
RET_V_DIM = 256
ATTN_Q_HEADS = 16
ATTN_HEAD_DIM = 64
D_FF = 2816
D_IN = 6400
ROPE_THETA = 10000.0
EPS = 1e-6
N_CHIPS = 4

ADAM_LR = 0.001
ADAM_B1 = 0.9
ADAM_B2 = 0.999
ADAM_EPS = 1e-08
ADAM_WD = 0.01
ADAM_STEP = 10

OFF_RQ, OFF_RK, OFF_RV, OFF_RG = 0, 512, 1024, 2048
OFF_AQ, OFF_AK, OFF_AV, OFF_GA, OFF_GB = 3072, 4096, 4224, 4352, 5376

V7X_VMEM_LIMIT = 56 * 1024 * 1024
LANES = 128

BF = jnp.bfloat16
F32 = jnp.float32
MESH = pl.DeviceIdType.MESH

_VMEM = pl.BlockSpec(memory_space=pltpu.VMEM)
_SMEM = pl.BlockSpec(memory_space=pltpu.SMEM)
_ANY = pl.BlockSpec(memory_space=pl.ANY)


def _dot(a, b):
    return jnp.dot(a, b, preferred_element_type=F32)


def _dot_nt(a, b):
    return lax.dot_general(a, b, (((1,), (1,)), ((), ())), preferred_element_type=F32)


def _dot_tn(a, b):
    return lax.dot_general(a, b, (((0,), (0,)), ((), ())), preferred_element_type=F32)


def _sigmoid(x):
    return 1.0 / (1.0 + jnp.exp(-x))


def _rows(tm, n):
    return pl.BlockSpec((tm, n), lambda i: (i, 0))


def _fixed(shape):
    nd = len(shape)
    return pl.BlockSpec(shape, lambda i: (0,) * nd)


def _params(sem, **kw):
    return pltpu.CompilerParams(dimension_semantics=sem, vmem_limit_bytes=V7X_VMEM_LIMIT, **kw)


def _rot_ret(p, cr, sr):
    return p * cr + pltpu.roll(p, 64, 1) * sr


def _rot_ret_bwd(d, cr, sr):
    return d * cr - pltpu.roll(d, 64, 1) * sr


def _attn_partner(p):
    lane = lax.broadcasted_iota(jnp.int32, p.shape, 1)
    first = (lane % 64) < 32
    return jnp.where(first, pltpu.roll(p, 96, 1), pltpu.roll(p, 32, 1))


def _rot_attn(p, ca, sa):
    return p * ca + _attn_partner(p) * sa


def _rot_attn_bwd(d, ca, sa):
    return d * ca - _attn_partner(d) * sa


def _proj_fwd(x, g1, w_in, b_in, cr, sr, ca, sa):
    S = x.shape[0]
    tm = 512
    scale = RET_QK_DIM ** -0.5

    def body(x_ref, g_ref, w_ref, b_ref, cr_ref, sr_ref, ca_ref, sa_ref,
             h_ref, rq_ref, rk_ref, rv_ref, rg_ref, aq_ref, ak_ref, av_ref, ga_ref, gb_ref):
        xv = x_ref[...]
        r = lax.rsqrt(jnp.mean(xv * xv, axis=-1, keepdims=True) + EPS)
        h = (xv * r * g_ref[...]).astype(BF)
        h_ref[...] = h

        def piece(lo, n):
            return _dot_nt(h, w_ref[lo:lo + n, :]) + b_ref[:, lo:lo + n]

        crv, srv = cr_ref[...], sr_ref[...]
        pq = piece(OFF_RQ, 512)
        pk = piece(OFF_RK, 512)
        for hd in range(RET_HEADS):
            sl = slice(hd * 128, (hd + 1) * 128)
            rq_ref[:, sl] = (_rot_ret(pq[:, sl], crv, srv) * scale).astype(BF)
            rk_ref[:, sl] = _rot_ret(pk[:, sl], crv, srv)
        rv_ref[...] = piece(OFF_RV, 1024).astype(BF)
        rg_ref[...] = piece(OFF_RG, 1024)
        cav, sav = ca_ref[...], sa_ref[...]
        pa = piece(OFF_AQ, 1024)
        for blk in range(8):
            sl = slice(blk * 128, (blk + 1) * 128)
            aq_ref[:, sl] = _rot_attn(pa[:, sl], cav, sav).astype(BF)
        pkv = piece(OFF_AK, 256)
        ak_ref[...] = _rot_attn(pkv[:, :128], cav, sav).astype(BF)
        av_ref[...] = pkv[:, 128:].astype(BF)
        ga_ref[...] = piece(OFF_GA, 1024)
        gb_ref[...] = piece(OFF_GB, 1024)

    outs = [(D_MODEL, BF), (512, BF), (512, F32), (1024, BF), (1024, F32),
            (1024, BF), (128, BF), (128, BF), (1024, F32), (1024, F32)]
    return pl.pallas_call(
        body, name="proj_fwd", grid=(S // tm,),
        in_specs=[_rows(tm, D_MODEL), _fixed((1, D_MODEL)), _VMEM, _VMEM,
                  _rows(tm, 128), _rows(tm, 128), _rows(tm, 128), _rows(tm, 128)],
        out_specs=[_rows(tm, n) for n, _ in outs],
        out_shape=[jax.ShapeDtypeStruct((S, n), dt) for n, dt in outs],
        compiler_params=_params(("parallel",)),
    )(x, g1, w_in, b_in, cr, sr, ca, sa)


def _ret_fwd(rq, rk, rv, dmat, qd, kd, cd):
    S = rq.shape[0]
    N = S // SEQ_CHUNK
    cps = RET_CHUNKS_PER_STEP
    rows = cps * SEQ_CHUNK

    def body(q_ref, k_ref, v_ref, dm_ref, qd_ref, kd_ref, cd_ref, y_ref, st_ref, state):
        @pl.when(pl.program_id(0) == 0)
        def _():
            state[...] = jnp.zeros_like(state)

        for hd in range(RET_HEADS):
            qs = slice(hd * 128, (hd + 1) * 128)
            vs = slice(hd * 256, (hd + 1) * 256)
            s = state[hd]
            for ch in range(cps):
                rs = slice(ch * SEQ_CHUNK, (ch + 1) * SEQ_CHUNK)
                q = q_ref[rs, qs]
                kf = k_ref[rs, qs]
                k = kf.astype(BF)
                v = v_ref[rs, vs]
                sb = s.astype(BF)
                st_ref[ch, hd] = sb
                a = _dot_nt(q, k) * dm_ref[hd]
                inner = _dot(a.astype(BF), v)
                cross = _dot(q, sb) * qd_ref[hd]
                y_ref[rs, vs] = inner + cross
                kdv = (kf * kd_ref[hd]).astype(BF)
                s = s * cd_ref[hd] + _dot_tn(kdv, v)
            state[hd] = s

    return pl.pallas_call(
        body, name="ret_fwd", grid=(N // cps,),
        in_specs=[_rows(rows, 512), _rows(rows, 512), _rows(rows, 1024), _VMEM, _VMEM, _VMEM, _VMEM],
        out_specs=[_rows(rows, 1024), pl.BlockSpec((cps, RET_HEADS, 128, 256), lambda i: (i, 0, 0, 0))],
        out_shape=[jax.ShapeDtypeStruct((S, 1024), F32),
                   jax.ShapeDtypeStruct((N, RET_HEADS, 128, 256), BF)],
        scratch_shapes=[pltpu.VMEM((RET_HEADS, 128, 256), F32)],
        compiler_params=_params(("arbitrary",)),
    )(rq, rk, rv, dmat, qd, kd, cd)


def _kv_halves(prev_ref, cur_ref):
    kk = jnp.concatenate([prev_ref[...], cur_ref[...]], axis=0).astype(F32)
    return kk, pltpu.roll(kk, 64, 1)


PAIRS = 4


def _attn_fold(n):
    t = lax.broadcasted_iota(jnp.int32, (128, PAIRS * 128), 0)
    qi = lax.broadcasted_iota(jnp.int32, (128, PAIRS * 128), 1) & 127
    cur = t <= qi
    return cur, jnp.logical_or(cur, n > 0)


def _group_operands(g, kk, kk_sw, vv, vv_sw):
    lo = lax.broadcasted_iota(jnp.int32, (256, 128), 1) < 64
    k_a, k_b = (kk, kk_sw) if g == 0 else (kk_sw, kk)
    v_a, v_b = (vv, vv_sw) if g == 0 else (vv_sw, vv)
    return (jnp.where(lo, k_a, 0.0).astype(BF), jnp.where(lo, 0.0, k_b).astype(BF),
            jnp.where(lo, v_a, 0.0).astype(BF), jnp.where(lo, 0.0, v_b).astype(BF))


def _stack_pairs(ref, g):
    return jnp.concatenate([ref[:, (4 * g + p) * 128:(4 * g + p + 1) * 128] for p in range(PAIRS)], axis=0)


def _sink_row(sk_ref, g, which):
    return jnp.concatenate([jnp.full((1, 128), sk_ref[0, 8 * g + 2 * p + which], F32) for p in range(PAIRS)], axis=1)


def _attn_probs(qs, kx, fold, sink):
    cur, valid = fold
    s = jnp.where(cur, _dot_nt(kx[128:], qs), _dot_nt(kx[:128], qs)) * (ATTN_HEAD_DIM ** -0.5)
    s = jnp.where(valid, s, -1e30)
    m = jnp.maximum(jnp.max(s, axis=0, keepdims=True), sink)
    e = jnp.exp(s - m)
    es = jnp.exp(sink - m)
    inv = 1.0 / (jnp.sum(e, axis=0, keepdims=True) + es)
    return e * inv, es * inv


def _unfold(p, cur):
    return jnp.where(cur, 0.0, p).astype(BF), jnp.where(cur, p, 0.0).astype(BF)


def _attn_fwd(aq, ak, av, sinks):
    S = aq.shape[0]
    N = S // SEQ_CHUNK

    def body(sk_ref, q_ref, kp_ref, kc_ref, vp_ref, vc_ref, o_ref):
        kk, kk_sw = _kv_halves(kp_ref, kc_ref)
        vv, vv_sw = _kv_halves(vp_ref, vc_ref)
        fold = _attn_fold(pl.program_id(0))
        for g in range(2):
            k_lo, k_hi, v_lo, v_hi = _group_operands(g, kk, kk_sw, vv, vv_sw)
            qs = _stack_pairs(q_ref, g)
            o = jnp.zeros((PAIRS * 128, 128), F32)
            for which, (kx, vx) in enumerate(((k_lo, v_lo), (k_hi, v_hi))):
                pr, _ = _attn_probs(qs, kx, fold, _sink_row(sk_ref, g, which))
                p_prev, p_cur = _unfold(pr, fold[0])
                o = o + _dot_tn(p_prev, vx[:128]) + _dot_tn(p_cur, vx[128:])
            for p in range(PAIRS):
                o_ref[:, (4 * g + p) * 128:(4 * g + p + 1) * 128] = o[p * 128:(p + 1) * 128].astype(BF)

    prev = lambda i: (jnp.maximum(i - 1, 0), 0)
    return pl.pallas_call(
        body, name="attn_fwd", grid=(N,),
        in_specs=[_SMEM, _rows(128, 1024), pl.BlockSpec((128, 128), prev), _rows(128, 128),
                  pl.BlockSpec((128, 128), prev), _rows(128, 128)],
        out_specs=_rows(128, 1024),
        out_shape=jax.ShapeDtypeStruct((S, 1024), BF),
        compiler_params=_params(("parallel",)),
    )(sinks, aq, ak, ak, av, av)


def _group_norm(y):
    mu = jnp.mean(y, axis=-1, keepdims=True)
    yc = y - mu
    rstd = lax.rsqrt(jnp.mean(yc * yc, axis=-1, keepdims=True) + EPS)
    return yc * rstd, rstd


def _merge_fwd(x, ry, rg, ay, ga, gb, g_ret, w_ret, w_attn, w_out):
    S = x.shape[0]
    tm = 256

    def body(x_ref, ry_ref, rg_ref, ay_ref, ga_ref, gb_ref, gr_ref, wr_ref, wa_ref, wo_ref,
             u_ref, ba_ref, bb_ref, mg_ref, x1_ref):
        for hd in range(RET_HEADS):
            vs = slice(hd * 256, (hd + 1) * 256)
            yn, _ = _group_norm(ry_ref[:, vs])
            rgv = rg_ref[:, vs]
            u_ref[:, vs] = (rgv * _sigmoid(rgv) * (yn * gr_ref[:, vs])).astype(BF)
        ba = _dot(u_ref[...], wr_ref[...])
        bb = _dot(ay_ref[...], wa_ref[...])
        ba_ref[...] = ba
        bb_ref[...] = bb
        merged = (_sigmoid(ga_ref[...]) * ba + _sigmoid(gb_ref[...]) * bb).astype(BF)
        mg_ref[...] = merged
        x1_ref[...] = x_ref[...] + _dot(merged, wo_ref[...])

    outs = [BF, F32, F32, BF, F32]
    return pl.pallas_call(
        body, name="merge_fwd", grid=(S // tm,),
        in_specs=[_rows(tm, 1024)] * 6 + [_fixed((1, 1024)), _VMEM, _VMEM, _VMEM],
        out_specs=[_rows(tm, 1024)] * 5,
        out_shape=[jax.ShapeDtypeStruct((S, 1024), dt) for dt in outs],
        compiler_params=_params(("parallel",)),
    )(x, ry, rg, ay, ga, gb, g_ret, w_ret, w_attn, w_out)


def _ffn_fwd(x1, tgt, g2, g3, w_gate, w_up, w_down):
    S = x1.shape[0]
    tm = 256

    def body(x1_ref, t_ref, g2_ref, g3_ref, wg_ref, wu_ref, wd_ref,
             gate_ref, up_ref, h2_ref, act_ref, dx2_ref, loss_ref, dg3_ref):
        @pl.when(pl.program_id(0) == 0)
        def _():
            loss_ref[...] = jnp.zeros_like(loss_ref)
            dg3_ref[...] = jnp.zeros_like(dg3_ref)

        x1v = x1_ref[...]
        r2 = lax.rsqrt(jnp.mean(x1v * x1v, axis=-1, keepdims=True) + EPS)
        h2 = (x1v * r2 * g2_ref[...]).astype(BF)
        h2_ref[...] = h2
        gate = _dot_nt(h2, wg_ref[...])
        up = _dot_nt(h2, wu_ref[...])
        gate_ref[...] = gate
        up_ref[...] = up
        act = (gate * _sigmoid(gate) * up).astype(BF)
        act_ref[...] = act
        x2 = x1v + _dot(act, wd_ref[...])
        r3 = lax.rsqrt(jnp.mean(x2 * x2, axis=-1, keepdims=True) + EPS)
        xh = x2 * r3
        g3v = g3_ref[...]
        err = xh * g3v - t_ref[...]
        loss_ref[...] += 0.5 * jnp.sum(jnp.mean(err * err, axis=-1, keepdims=True), axis=0, keepdims=True)
        dy = err * (1.0 / D_MODEL)
        dg3_ref[...] += jnp.sum(dy * xh, axis=0, keepdims=True)
        dxh = dy * g3v
        dx2_ref[...] = r3 * (dxh - xh * jnp.mean(dxh * xh, axis=-1, keepdims=True))

    return pl.pallas_call(
        body, name="ffn_fwd", grid=(S // tm,),
        in_specs=[_rows(tm, 1024), _rows(tm, 1024), _fixed((1, 1024)), _fixed((1, 1024)), _VMEM, _VMEM, _VMEM],
        out_specs=[_rows(tm, D_FF), _rows(tm, D_FF), _rows(tm, 1024), _rows(tm, D_FF), _rows(tm, 1024),
                   _fixed((1, LANES)), _fixed((1, 1024))],
        out_shape=[jax.ShapeDtypeStruct((S, D_FF), F32), jax.ShapeDtypeStruct((S, D_FF), F32),
                   jax.ShapeDtypeStruct((S, 1024), BF), jax.ShapeDtypeStruct((S, D_FF), BF),
                   jax.ShapeDtypeStruct((S, 1024), F32),
                   jax.ShapeDtypeStruct((1, LANES), F32), jax.ShapeDtypeStruct((1, 1024), F32)],
        compiler_params=_params(("arbitrary",)),
    )(x1, tgt, g2, g3, w_gate, w_up, w_down)


def _ffn_bwd(dx2, gate, up, x1, g2, w_gate, w_up, w_down):
    S = x1.shape[0]
    tm = 256

    def body(dx2_ref, gate_ref, up_ref, x1_ref, g2_ref, wg_ref, wu_ref, wd_ref,
             dgate_ref, dup_ref, dx1_ref, dg2_ref):
        @pl.when(pl.program_id(0) == 0)
        def _():
            dg2_ref[...] = jnp.zeros_like(dg2_ref)

        dx2v = dx2_ref[...]
        dact = _dot_nt(dx2v.astype(BF), wd_ref[...])
        gate = gate_ref[...]
        sg = _sigmoid(gate)
        dgate = (dact * up_ref[...] * (sg * (1.0 + gate * (1.0 - sg)))).astype(BF)
        dup = (dact * (gate * sg)).astype(BF)
        dgate_ref[...] = dgate
        dup_ref[...] = dup
        dh2 = _dot(dgate, wg_ref[...]) + _dot(dup, wu_ref[...])
        x1v = x1_ref[...]
        r2 = lax.rsqrt(jnp.mean(x1v * x1v, axis=-1, keepdims=True) + EPS)
        xh = x1v * r2
        dg2_ref[...] += jnp.sum(dh2 * xh, axis=0, keepdims=True)
        dxh = dh2 * g2_ref[...]
        dx1_ref[...] = dx2v + r2 * (dxh - xh * jnp.mean(dxh * xh, axis=-1, keepdims=True))

    return pl.pallas_call(
        body, name="ffn_bwd", grid=(S // tm,),
        in_specs=[_rows(tm, 1024), _rows(tm, D_FF), _rows(tm, D_FF), _rows(tm, 1024), _fixed((1, 1024)),
                  _VMEM, _VMEM, _VMEM],
        out_specs=[_rows(tm, D_FF), _rows(tm, D_FF), _rows(tm, 1024), _fixed((1, 1024))],
        out_shape=[jax.ShapeDtypeStruct((S, D_FF), BF), jax.ShapeDtypeStruct((S, D_FF), BF),
                   jax.ShapeDtypeStruct((S, 1024), F32), jax.ShapeDtypeStruct((1, 1024), F32)],
        compiler_params=_params(("arbitrary",)),
    )(dx2, gate, up, x1, g2, w_gate, w_up, w_down)


def _merge_bwd(dx1, ga, gb, ba, bb, rg, ry, g_ret, w_ret, w_attn, w_out):
    S = dx1.shape[0]
    tm = 256

    def body(dx1_ref, ga_ref, gb_ref, ba_ref, bb_ref, rg_ref, ry_ref, gr_ref, wr_ref, wa_ref, wo_ref,
             dgates_ref, drg_ref, dba_ref, dbb_ref, day_ref, dry_ref, dgr_ref, db_ref):
        @pl.when(pl.program_id(0) == 0)
        def _():
            dgr_ref[...] = jnp.zeros_like(dgr_ref)
            db_ref[...] = jnp.zeros_like(db_ref)

        dm = _dot_nt(dx1_ref[...].astype(BF), wo_ref[...])
        sa = _sigmoid(ga_ref[...])
        sb = _sigmoid(gb_ref[...])
        dga = (dm * ba_ref[...] * (sa * (1.0 - sa))).astype(BF)
        dgb = (dm * bb_ref[...] * (sb * (1.0 - sb))).astype(BF)
        dgates_ref[:, :1024] = dga
        dgates_ref[:, 1024:] = dgb
        dba = (dm * sa).astype(BF)
        dbb = (dm * sb).astype(BF)
        dba_ref[...] = dba
        dbb_ref[...] = dbb
        day_ref[...] = _dot_nt(dbb, wa_ref[...]).astype(BF)
        du = _dot_nt(dba, wr_ref[...])
        for hd in range(RET_HEADS):
            vs = slice(hd * 256, (hd + 1) * 256)
            yn, rstd = _group_norm(ry_ref[:, vs])
            rgv = rg_ref[:, vs]
            sr = _sigmoid(rgv)
            silu = rgv * sr
            duv = du[:, vs]
            grv = gr_ref[:, vs]
            drg = (duv * (yn * grv) * (sr * (1.0 + rgv * (1.0 - sr)))).astype(BF)
            drg_ref[:, vs] = drg
            dyg = duv * silu
            dgr_ref[:, vs] += jnp.sum(dyg * yn, axis=0, keepdims=True)
            dyn = dyg * grv
            dry_ref[:, vs] = rstd * (dyn - jnp.mean(dyn, axis=-1, keepdims=True)
                                     - yn * jnp.mean(dyn * yn, axis=-1, keepdims=True))
            db_ref[:, vs] += jnp.sum(drg.astype(F32), axis=0, keepdims=True)
        db_ref[:, 1024:2048] += jnp.sum(dga.astype(F32), axis=0, keepdims=True)
        db_ref[:, 2048:] += jnp.sum(dgb.astype(F32), axis=0, keepdims=True)

    return pl.pallas_call(
        body, name="merge_bwd", grid=(S // tm,),
        in_specs=[_rows(tm, 1024)] * 7 + [_fixed((1, 1024)), _VMEM, _VMEM, _VMEM],
        out_specs=[_rows(tm, 2048), _rows(tm, 1024), _rows(tm, 1024), _rows(tm, 1024), _rows(tm, 1024),
                   _rows(tm, 1024), _fixed((1, 1024)), _fixed((1, 3072))],
        out_shape=[jax.ShapeDtypeStruct((S, 2048), BF), jax.ShapeDtypeStruct((S, 1024), BF),
                   jax.ShapeDtypeStruct((S, 1024), BF), jax.ShapeDtypeStruct((S, 1024), BF),
                   jax.ShapeDtypeStruct((S, 1024), BF), jax.ShapeDtypeStruct((S, 1024), F32),
                   jax.ShapeDtypeStruct((1, 1024), F32), jax.ShapeDtypeStruct((1, 3072), F32)],
        compiler_params=_params(("arbitrary",)),
    )(dx1, ga, gb, ba, bb, rg, ry, g_ret, w_ret, w_attn, w_out)


def _attn_bwd(aq, ak, av, day, sinks, ca, sa, after):
    S = aq.shape[0]
    N = S // SEQ_CHUNK

    def body(sk_ref, q_ref, kp_ref, kc_ref, vp_ref, vc_ref, do_ref, cq_ref, sq_ref, cp_ref, sp_ref, after_ref,
             dq_ref, dk_ref, dv_ref, dsk_ref, carry_k, carry_v):
        n = pl.program_id(0)

        @pl.when(n == 0)
        def _():
            carry_k[...] = jnp.zeros_like(carry_k)
            carry_v[...] = jnp.zeros_like(carry_v)
            dsk_ref[...] = jnp.zeros_like(dsk_ref)

        @pl.when(n < N)
        def _():
            kk, kk_sw = _kv_halves(kp_ref, kc_ref)
            vv, vv_sw = _kv_halves(vp_ref, vc_ref)
            lo = lax.broadcasted_iota(jnp.int32, (256, 128), 1) < 64
            fold = _attn_fold(n)
            cur = fold[0]
            cq, sq = cq_ref[...], sq_ref[...]
            dk_groups, dv_groups = [], []
            for g in range(2):
                k_lo, k_hi, v_lo, v_hi = _group_operands(g, kk, kk_sw, vv, vv_sw)
                qs = _stack_pairs(q_ref, g)
                dos = _stack_pairs(do_ref, g)
                dq = jnp.zeros((PAIRS * 128, 128), F32)
                dk_half, dv_half = [], []
                for which, (kx, vx) in enumerate(((k_lo, v_lo), (k_hi, v_hi))):
                    pr, ps = _attn_probs(qs, kx, fold, _sink_row(sk_ref, g, which))
                    dp = jnp.where(cur, _dot_nt(vx[128:], dos), _dot_nt(vx[:128], dos))
                    dr = jnp.sum(pr * dp, axis=0, keepdims=True)
                    ds_prev, ds_cur = _unfold(pr * (dp - dr) * (ATTN_HEAD_DIM ** -0.5), cur)
                    p_prev, p_cur = _unfold(pr, cur)
                    dsk_ref[2 * g + which:2 * g + which + 1, :] -= ps * dr
                    dq = dq + _dot_tn(ds_prev, kx[:128]) + _dot_tn(ds_cur, kx[128:])
                    dk_half.append(jnp.concatenate([_dot(ds_prev, qs), _dot(ds_cur, qs)], axis=0))
                    dv_half.append(jnp.concatenate([_dot(p_prev, dos), _dot(p_cur, dos)], axis=0))
                for p in range(PAIRS):
                    sl = slice((4 * g + p) * 128, (4 * g + p + 1) * 128)
                    dq_ref[:, sl] = _rot_attn_bwd(dq[p * 128:(p + 1) * 128], cq, sq).astype(BF)
                tk = jnp.where(lo, dk_half[0], dk_half[1])
                tv = jnp.where(lo, dv_half[0], dv_half[1])
                dk_groups.append(tk + pltpu.roll(tk, 64, 1))
                dv_groups.append(tv + pltpu.roll(tv, 64, 1))
            dk_full = jnp.where(lo, dk_groups[0], dk_groups[1])
            dv_full = jnp.where(lo, dv_groups[0], dv_groups[1])
            dk_ref[...] = _rot_attn_bwd(carry_k[...] + dk_full[:128], cp_ref[...], sp_ref[...]).astype(BF)
            dv_ref[...] = (carry_v[...] + dv_full[:128]).astype(BF)
            carry_k[...] = dk_full[128:]
            carry_v[...] = dv_full[128:]

        @pl.when(n == N)
        def _():
            dk_ref[...] = _rot_attn_bwd(carry_k[...], cp_ref[...], sp_ref[...]).astype(BF)
            dv_ref[...] = carry_v[...].astype(BF)

    cur = lambda i: (jnp.minimum(i, N - 1), 0)
    prev = lambda i: (jnp.maximum(i - 1, 0), 0)
    blk = lambda w, im: pl.BlockSpec((128, w), im)
    return pl.pallas_call(
        body, name="attn_bwd", grid=(N + 1,),
        in_specs=[_SMEM, blk(1024, cur), blk(128, prev), blk(128, cur), blk(128, prev), blk(128, cur),
                  blk(1024, cur), blk(128, cur), blk(128, cur), blk(128, prev), blk(128, prev), _ANY],
        out_specs=[blk(1024, cur), blk(128, prev), blk(128, prev), _fixed((4, PAIRS * 128))],
        out_shape=[jax.ShapeDtypeStruct((S, 1024), BF), jax.ShapeDtypeStruct((S, 128), BF),
                   jax.ShapeDtypeStruct((S, 128), BF), jax.ShapeDtypeStruct((4, PAIRS * 128), F32)],
        scratch_shapes=[pltpu.VMEM((128, 128), F32), pltpu.VMEM((128, 128), F32)],
        compiler_params=_params(("arbitrary",)),
    )(sinks, aq, ak, ak, av, av, day, ca, sa, ca, sa, after)


def _ret_bwd(rq, rk, rv, dry, states, dmat, qd, kd, cd, cr, sr, after):
    S = rq.shape[0]
    N = S // SEQ_CHUNK
    scale = RET_QK_DIM ** -0.5
    cps = RET_CHUNKS_PER_STEP
    steps = N // cps

    def body(q_ref, k_ref, v_ref, dy_ref, st_ref, dm_ref, qd_ref, kd_ref, cd_ref, cr_ref, sr_ref, after_ref,
             out_ref, dstate):
        @pl.when(pl.program_id(0) == 0)
        def _():
            dstate[...] = jnp.zeros_like(dstate)

        for hd in range(RET_HEADS):
            qs = slice(hd * 128, (hd + 1) * 128)
            vs = slice(hd * 256, (hd + 1) * 256)
            ds = dstate[hd]
            dmv = dm_ref[hd]
            for ch in reversed(range(cps)):
                rs = slice(ch * SEQ_CHUNK, (ch + 1) * SEQ_CHUNK)
                crv, srv = cr_ref[rs, :], sr_ref[rs, :]
                q = q_ref[rs, qs]
                kf = k_ref[rs, qs]
                k = kf.astype(BF)
                kdv = (kf * kd_ref[hd]).astype(BF)
                v = v_ref[rs, vs]
                do = dy_ref[rs, vs]
                dob = do.astype(BF)
                doq = (do * qd_ref[hd]).astype(BF)
                sn = st_ref[ch, hd]
                dsb = ds.astype(BF)
                a = (_dot_nt(q, k) * dmv).astype(BF)
                dv = _dot_tn(a, dob) + _dot(kdv, dsb)
                da = (_dot_nt(dob, v) * dmv).astype(BF)
                dq = _dot(da, k) + _dot_nt(doq, sn)
                dk = _dot_tn(da, q) + _dot_nt(v, dsb) * kd_ref[hd]
                ds = ds * cd_ref[hd] + _dot_tn(q, doq)
                out_ref[rs, qs] = _rot_ret_bwd(dq * scale, crv, srv).astype(BF)
                out_ref[rs, 512 + hd * 128:512 + (hd + 1) * 128] = _rot_ret_bwd(dk, crv, srv).astype(BF)
                out_ref[rs, 1024 + hd * 256:1024 + (hd + 1) * 256] = dv.astype(BF)
            dstate[hd] = ds

    rev = lambda w: pl.BlockSpec((cps * SEQ_CHUNK, w), lambda i: (steps - 1 - i, 0))
    return pl.pallas_call(
        body, name="ret_bwd", grid=(steps,),
        in_specs=[rev(512), rev(512), rev(1024), rev(1024),
                  pl.BlockSpec((cps, RET_HEADS, 128, 256), lambda i: (steps - 1 - i, 0, 0, 0)),
                  _VMEM, _VMEM, _VMEM, _VMEM, rev(128), rev(128), _ANY],
        out_specs=rev(2048),
        out_shape=jax.ShapeDtypeStruct((S, 2048), BF),
        scratch_shapes=[pltpu.VMEM((RET_HEADS, 128, 256), F32)],
        compiler_params=_params(("arbitrary",)),
    )(rq, rk, rv, dry, states, dmat, qd, kd, cd, cr, sr, after)


def _proj_bwd(d_pieces, x, dx1, g1, w_in_t):
    S = x.shape[0]
    tm = 256
    widths = [d.shape[1] for d in d_pieces]
    assert sum(widths) == D_IN
    nd = len(d_pieces)

    def body(*refs):
        d_refs = refs[:nd]
        x_ref, dx1_ref, g_ref, w_ref, gx_ref, dg_ref, db_ref = refs[nd:]

        @pl.when(pl.program_id(0) == 0)
        def _():
            dg_ref[...] = jnp.zeros_like(dg_ref)
            db_ref[...] = jnp.zeros_like(db_ref)

        dh = jnp.zeros((tm, D_MODEL), F32)
        lo = 0
        for ref, n in zip(d_refs, widths):
            v = ref[...]
            dh = dh + _dot(v, w_ref[lo:lo + n, :])
            db_ref[:, lo:lo + n] += jnp.sum(v.astype(F32), axis=0, keepdims=True)
            lo += n
        xv = x_ref[...]
        r = lax.rsqrt(jnp.mean(xv * xv, axis=-1, keepdims=True) + EPS)
        xh = xv * r
        dg_ref[...] += jnp.sum(dh * xh, axis=0, keepdims=True)
        dxh = dh * g_ref[...]
        gx_ref[...] = dx1_ref[...] + r * (dxh - xh * jnp.mean(dxh * xh, axis=-1, keepdims=True))

    return pl.pallas_call(
        body, name="proj_bwd", grid=(S // tm,),
        in_specs=[_rows(tm, n) for n in widths] + [_rows(tm, 1024), _rows(tm, 1024), _fixed((1, 1024)), _VMEM],
        out_specs=[_rows(tm, 1024), _fixed((1, 1024)), _fixed((1, D_IN))],
        out_shape=[jax.ShapeDtypeStruct((S, 1024), F32), jax.ShapeDtypeStruct((1, 1024), F32),
                   jax.ShapeDtypeStruct((1, D_IN), F32)],
        compiler_params=_params(("arbitrary",)),
    )(*d_pieces, x, dx1, g1, w_in_t)


SHARD_ROWS = (1600, 256, 256, 256, 704, 704, 704)


class Group(NamedTuple):
    name: str
    tensors: tuple
    offs: tuple
    block: int
    tile: int

    @property
    def half(self):
        return self.block // 2

    def off(self, t):
        return self.offs[self.tensors.index(t)]


LATE = Group("late", (0,), (0,), 1600, 800)
EARLY = Group("early", (1, 2, 3, 4, 5, 6), (0, 256, 512, 768, 1472, 2176), 2880, 480)
GROUP_OF = {t: g for g in (LATE, EARLY) for t in g.tensors}


def _wgrad(a_list, b, t, packed, name):
    S = b.shape[0]
    widths = [a.shape[1] for a in a_list]
    ts = 512 if sum(widths) > 4096 else 1024
    steps = S // ts
    na = len(a_list)
    grp = GROUP_OF[t]
    r, off = SHARD_ROWS[t], grp.off(t)
    assert sum(widths) == N_CHIPS * r
    n_in = na + 1 + (2 if packed is not None else 0)

    def body(*refs):
        a_refs, b_ref = refs[:na], refs[na]
        g32_ref, gb_ref, acc, stage, sems = refs[n_in:]
        i = pl.program_id(0)

        @pl.when(i == 0)
        def _():
            acc[...] = jnp.zeros_like(acc)

        bv = b_ref[...].astype(BF)
        lo = 0
        for a_ref, n in zip(a_refs, widths):
            acc[lo:lo + n, :] += _dot_tn(a_ref[...].astype(BF), bv)
            lo += n

        @pl.when(i == steps - 1)
        def _():
            for chip in range(N_CHIPS):
                src = acc.at[chip * r:(chip + 1) * r, :]
                dst = pl.ds(chip * grp.block + off, r)
                c32 = pltpu.make_async_copy(src, g32_ref.at[dst, :], sems.at[0])
                c32.start()
                stage[...] = acc[chip * r:(chip + 1) * r, :].astype(BF)
                c16 = pltpu.make_async_copy(stage, gb_ref.at[dst, :], sems.at[1])
                c16.start()
                c32.wait()
                c16.wait()

    shapes = [jax.ShapeDtypeStruct((N_CHIPS * grp.block, 1024), F32),
              jax.ShapeDtypeStruct((N_CHIPS * grp.block, 1024), BF)]
    operands = list(a_list) + [b]
    in_specs = [_rows(ts, n) for n in widths] + [_rows(ts, 1024)]
    aliases = {}
    if packed is not None:
        operands += list(packed)
        in_specs += [_ANY, _ANY]
        aliases = {na + 1: 0, na + 2: 1}
    return pl.pallas_call(
        body, name=name, grid=(steps,),
        in_specs=in_specs, out_specs=[_ANY, _ANY], out_shape=shapes,
        scratch_shapes=[pltpu.VMEM((N_CHIPS * r, 1024), F32), pltpu.VMEM((r, 1024), BF),
                        pltpu.SemaphoreType.DMA((2,))],
        input_output_aliases=aliases,
        compiler_params=_params(("arbitrary",)),
    )(*operands)


def _stage_weight(place, w, name, after=()):
    r = w.shape[0]
    tr = 800 if r == 1600 else r
    nb = r // tr

    def body(p_ref, w_ref, *rest):
        rest[-1][...] = w_ref[...].astype(BF)

    return pl.pallas_call(
        body, name=name,
        grid_spec=pltpu.PrefetchScalarGridSpec(
            num_scalar_prefetch=1, grid=(nb,),
            in_specs=[pl.BlockSpec((tr, 1024), lambda i, p: (i, 0))] + [_ANY] * len(after),
            out_specs=pl.BlockSpec((tr, 1024), lambda i, p: (p[0] * nb + i, 0))),
        out_shape=jax.ShapeDtypeStruct((N_CHIPS * r, 1024), BF),
        compiler_params=_params(("arbitrary",)),
    )(place, w, *after)


def _pair_sum(place, g32, got, grp):
    nb = grp.half // grp.tile

    def body(p_ref, g_ref, r_ref, s32_ref, s16_ref):
        s = g_ref[...] + r_ref[...].astype(F32)
        s32_ref[...] = s
        s16_ref[...] = s.astype(BF)

    flat = pl.BlockSpec((grp.tile, 1024), lambda i, p: (i, 0))
    return pl.pallas_call(
        body, name="pair_sum_" + grp.name,
        grid_spec=pltpu.PrefetchScalarGridSpec(
            num_scalar_prefetch=1, grid=(N_CHIPS * nb,),
            in_specs=[pl.BlockSpec((grp.tile, 1024), lambda i, p: ((2 * (i // nb) + p[1]) * nb + i % nb, 0)), flat],
            out_specs=[flat, flat]),
        out_shape=[jax.ShapeDtypeStruct((N_CHIPS * grp.half, 1024), F32),
                   jax.ShapeDtypeStruct((N_CHIPS * grp.half, 1024), BF)],
        compiler_params=_params(("arbitrary",)),
    )(place, g32, got)


def _chip_sum(place, s32, others, grp):
    nb = grp.half // grp.tile

    def body(p_ref, a_ref, r0_ref, r1_ref, r2_ref, o_ref):
        o_ref[...] = ((a_ref[...] + r0_ref[...].astype(F32)) + r1_ref[...].astype(F32)) + r2_ref[...].astype(F32)

    blk = lambda f: pl.BlockSpec((grp.tile, 1024), f)
    return pl.pallas_call(
        body, name="chip_sum_" + grp.name,
        grid_spec=pltpu.PrefetchScalarGridSpec(
            num_scalar_prefetch=1, grid=(nb,),
            in_specs=[blk(lambda i, p: (p[0] * nb + i, 0)), blk(lambda i, p: (i, 0)),
                      blk(lambda i, p: (nb + i, 0)), blk(lambda i, p: (2 * nb + i, 0))],
            out_specs=blk(lambda i, p: (p[1] * nb + i, 0))),
        out_shape=jax.ShapeDtypeStruct((grp.block, 1024), F32),
        compiler_params=_params(("arbitrary",)),
    )(place, s32, others, others, others)


def _adamw(w, g, m, v, name, g_row0=0, after=None):
    r = w.shape[0]
    tr = 320 if r % 320 == 0 and g_row0 % 320 == 0 else (64 if r % 64 == 0 else r)
    assert g_row0 % tr == 0
    g0 = g_row0 // tr
    extra = [] if after is None else [after]

    def body(w_ref, g_ref, m_ref, v_ref, *rest):
        go_ref, d_ref, mo_ref, vo_ref = rest[len(extra):]
        gv = g_ref[...]
        mn = ADAM_B1 * m_ref[...] + (1.0 - ADAM_B1) * gv
        vn = ADAM_B2 * v_ref[...] + (1.0 - ADAM_B2) * (gv * gv)
        m_hat = mn / (1.0 - ADAM_B1 ** ADAM_STEP)
        v_hat = vn / (1.0 - ADAM_B2 ** ADAM_STEP)
        go_ref[...] = gv
        d_ref[...] = -ADAM_LR * (m_hat / (jnp.sqrt(v_hat) + ADAM_EPS) + ADAM_WD * w_ref[...])
        mo_ref[...] = mn
        vo_ref[...] = vn

    return pl.pallas_call(
        body, name=name, grid=(r // tr,),
        in_specs=[_rows(tr, 1024), pl.BlockSpec((tr, 1024), lambda i: (g0 + i, 0)), _rows(tr, 1024), _rows(tr, 1024)]
        + [_ANY] * len(extra),
        out_specs=[_rows(tr, 1024)] * 4,
        out_shape=[jax.ShapeDtypeStruct((r, 1024), F32)] * 4,
        compiler_params=_params(("parallel",)),
    )(w, g, m, v, *extra)


def _place():
    x, y, c = lax.axis_index("x"), lax.axis_index("y"), lax.axis_index("c")
    chips = [(1 - x, y), (x, 1 - y), (1 - x, 1 - y)]
    return x, y, c, chips


_HBM =pl.BlockSpec(memory_space=pltpu.HBM)
_SEM = pl.BlockSpec(memory_space=pltpu.SEMAPHORE)
_DATAFLOW = pltpu.SideEffectType.DATAFLOW_SIDE_EFFECTING


def _exchange(name, arrays, n, plan):
    na = len(arrays)

    def body(*refs):
        pairs = plan(refs[na:2 * na], refs[2 * na], refs[2 * na + 1])
        for send, _ in pairs:
            send.start()
        for send, recv in pairs:
            send.wait_send()
            recv.wait_recv()

    return pl.pallas_call(
        body, name=name, in_specs=[_ANY] * na, out_specs=[_ANY] * na,
        out_shape=[jax.ShapeDtypeStruct(a.shape, a.dtype) for a in arrays],
        scratch_shapes=[pltpu.SemaphoreType.DMA((n,)), pltpu.SemaphoreType.DMA((n,))],
        input_output_aliases={i: i for i in range(na)},
    )(*arrays)


def _exchange_start(name, arrays, n, plan, after=()):
    na = len(arrays)

    def body(*refs):
        for send, _ in plan(refs[:na], refs[na + len(after)], refs[na + len(after) + 1]):
            send.start()
        refs[-1][...] = jnp.zeros_like(refs[-1])

    out = pl.pallas_call(
        body, name=name,
        out_shape=(pltpu.SemaphoreType.DMA((n,)), pltpu.SemaphoreType.DMA((n,)),
                   *[pltpu.HBM(a.shape, a.dtype) for a in arrays], jax.ShapeDtypeStruct((8, LANES), F32)),
        in_specs=[_HBM] * na + [_ANY] * len(after), out_specs=(_SEM, _SEM, *[_HBM] * na, _VMEM),
        input_output_aliases={i: 2 + i for i in range(na)},
        compiler_params=pltpu.CompilerParams(has_side_effects=_DATAFLOW),
    )(*[pltpu.with_memory_space_constraint(a, pltpu.HBM) for a in arrays], *after)
    return (out[0], out[1]), list(out[2:2 + na]), out[-1]


def _exchange_wait(name, sems, in_flight, plan, after=()):
    na = len(in_flight)

    def body(*refs):
        for send, recv in plan(refs[:na], refs[na], refs[na + 1]):
            send.wait_send()
            recv.wait_recv()

    out = pl.pallas_call(
        body, name=name,
        out_shape=tuple(pltpu.HBM(a.shape, a.dtype) for a in in_flight),
        in_specs=[_HBM] * na + [_SEM, _SEM] + [_ANY] * len(after), out_specs=tuple([_HBM] * na),
        input_output_aliases={i: i for i in range(na)},
        compiler_params=pltpu.CompilerParams(has_side_effects=_DATAFLOW),
    )(*in_flight, *sems, *after)
    return list(out)


def _remote(src, dst, send_sems, recv_sems, k, to):
    return pltpu.make_async_remote_copy(src_ref=src, dst_ref=dst, send_sem=send_sems.at[k], recv_sem=recv_sems.at[k],
                                        device_id=to, device_id_type=MESH)


def _plan_gather_direct(tids):
    def plan(refs, ss, rs):
        x, y, c, chips = _place()
        j = 2 * x + y
        pairs = []
        for t, tid in enumerate(tids):
            r = SHARD_ROWS[tid]
            for k, (cx, cy) in enumerate(chips):
                mine = refs[t].at[pl.ds(j * r, r), :]
                theirs = refs[t].at[pl.ds((2 * cx + cy) * r, r), :]
                pairs.append((_remote(mine, mine, ss, rs, 3 * t + k, (cx, cy, c)),
                              _remote(theirs, theirs, ss, rs, 3 * t + k, (x, y, c))))
        return pairs
    return plan


def _half_rows(ref, tid, pj, half):
    h = SHARD_ROWS[tid] // 2
    return ref.at[pl.ds((2 * pj + half) * h, h), :]


def _plan_gather_halves(tid):
    def plan(refs, ss, rs):
        x, y, c, chips = _place()
        mine = _half_rows(refs[0], tid, 2 * x + y, c)
        return [(_remote(mine, mine, ss, rs, k, (cx, cy, c)),
                 _remote(_half_rows(refs[0], tid, 2 * cx + cy, c), _half_rows(refs[0], tid, 2 * cx + cy, c),
                         ss, rs, k, (x, y, c))) for k, (cx, cy) in enumerate(chips)]
    return plan


def _plan_forward_halves(tid):
    def plan(refs, ss, rs):
        x, y, c, chips = _place()
        pairs = []
        for k, (cx, cy) in enumerate(chips):
            got = _half_rows(refs[0], tid, 2 * cx + cy, c)
            missing = _half_rows(refs[0], tid, 2 * cx + cy, 1 - c)
            pairs.append((_remote(got, got, ss, rs, k, (x, y, 1 - c)), _remote(missing, missing, ss, rs, k, (x, y, 1 - c))))
        return pairs
    return plan


def _plan_pair_exchange(grp):
    def plan(refs, ss, rs):
        x, y, c, _ = _place()
        cps = [_remote(refs[0].at[pl.ds(pj * grp.block + (1 - c) * grp.half, grp.half), :],
                       refs[1].at[pl.ds(pj * grp.half, grp.half), :], ss, rs, pj, (x, y, 1 - c))
               for pj in range(N_CHIPS)]
        return [(cp, cp) for cp in cps]
    return plan


def _plan_to_chips(grp):
    def plan(refs, ss, rs):
        x, y, c, chips = _place()
        cps = [_remote(refs[0].at[pl.ds((2 * cx + cy) * grp.half, grp.half), :],
                       refs[1].at[pl.ds(k * grp.half, grp.half), :], ss, rs, k, (cx, cy, c))
               for k, (cx, cy) in enumerate(chips)]
        return [(cp, cp) for cp in cps]
    return plan


def _plan_share_halves(grp):
    def plan(refs, ss, rs):
        x, y, c, _ = _place()
        mine = refs[0].at[pl.ds(c * grp.half, grp.half), :]
        other = refs[0].at[pl.ds((1 - c) * grp.half, grp.half), :]
        return [(_remote(mine, mine, ss, rs, 0, (x, y, 1 - c)), _remote(other, other, ss, rs, 0, (x, y, 1 - c)))]
    return plan


def _empty(rows, dtype):
    return lax.empty((rows, 1024), dtype)


def _all_reduce_small(v, after):
    m_per, n = v.shape
    n_dev = 8

    def body(x_ref, after_ref, out_ref, all_ref, send_sems, recv_sems, local_sem):
        x, y, c, chips = _place()
        me, sibling = (x, y, c), (x, y, 1 - c)

        def rows(px, py, pc):
            return all_ref.at[pl.ds((4 * px + 2 * py + pc) * m_per, m_per), :]

        def copy(k, block, to, src=None):
            return pltpu.make_async_remote_copy(
                src_ref=rows(*block) if src is None else src, dst_ref=rows(*block),
                send_sem=send_sems.at[k], recv_sem=recv_sems.at[k], device_id=to, device_id_type=MESH)

        mine = pltpu.make_async_copy(x_ref, rows(*me), local_sem)
        mine.start()
        first = [copy(0, me, sibling, src=x_ref)]
        first += [copy(1 + k, me, (*chip, c), src=x_ref) for k, chip in enumerate(chips)]
        for cp in first:
            cp.start()
        passed = [copy(4 + k, (*chip, c), sibling) for k, chip in enumerate(chips)]
        for k, chip in enumerate(chips):
            copy(1 + k, (*chip, c), me).wait_recv()
            passed[k].start()
        copy(0, sibling, me).wait_recv()
        for k, chip in enumerate(chips):
            copy(4 + k, (*chip, 1 - c), me).wait_recv()
        for cp in first + passed:
            cp.wait_send()
        mine.wait()
        acc = all_ref[pl.ds(0, m_per), :]
        for d in range(1, n_dev):
            acc = acc + all_ref[pl.ds(d * m_per, m_per), :]
        out_ref[...] = acc

    return pl.pallas_call(
        body, name="all_reduce_small",
        in_specs=[_VMEM, _ANY], out_specs=_VMEM,
        out_shape=jax.ShapeDtypeStruct((m_per, n), F32),
        scratch_shapes=[pltpu.VMEM((n_dev * m_per, n), F32),
                        pltpu.SemaphoreType.DMA((7,)), pltpu.SemaphoreType.DMA((7,)), pltpu.SemaphoreType.DMA],
        compiler_params=pltpu.CompilerParams(has_side_effects=True),
    )(v, after)


def _tables(S, token):
    pos = jnp.arange(S, dtype=jnp.int32).astype(F32) + token[0, 0]

    def cs(half):
        inv_freq = ROPE_THETA ** (-jnp.arange(half, dtype=F32) / half)
        ang = pos[:, None] * inv_freq[None, :]
        return jnp.cos(ang), jnp.sin(ang)

    c64, s64 = cs(64)
    c32, s32 = cs(32)
    cr = jnp.concatenate([c64, c64], axis=1)
    sr = jnp.concatenate([-s64, s64], axis=1)
    ca = jnp.concatenate([c32, c32, c32, c32], axis=1)
    sa = jnp.concatenate([-s32, s32, -s32, s32], axis=1)
    return cr, sr, ca, sa


def _decays():
    H, C = RET_HEADS, SEQ_CHUNK
    log_gamma = jnp.log1p(-jnp.exp2(-5.0 - jnp.arange(H, dtype=F32)))
    idx = jnp.arange(C, dtype=F32)
    rel = idx[:, None] - idx[None, :]
    dmat = jnp.where(rel[None] >= 0, jnp.exp(log_gamma[:, None, None] * jnp.maximum(rel, 0.0)[None]), 0.0)
    qd = jnp.exp(log_gamma[:, None] * (idx + 1.0))[:, :, None]
    kd = jnp.exp(log_gamma[:, None] * (C - 1.0 - idx))[:, :, None]
    cd = jnp.exp(log_gamma * C)[:, None, None]
    return dmat, qd, kd, cd


def _local_step(place, x, tgt, ln1_g, b_in, ret_norm_g, attn_sinks, ln2_g, lnf_g, w_in, other_weights, tables, decays):
    S = x.shape[0]
    cr, sr, ca, sa = tables
    dmat, qd, kd, cd = decays

    h, rq, rk, rv, rg, aq, ak, av, ga, gb = _proj_fwd(x, ln1_g, w_in, b_in, cr, sr, ca, sa)
    ry, states = _ret_fwd(rq, rk, rv, dmat, qd, kd, cd)
    ay = _attn_fwd(aq, ak, av, attn_sinks)
    w_ret, w_attn, w_out, w_gate, w_up, w_down = other_weights(ay)
    u, ba, bb, merged, x1 = _merge_fwd(x, ry, rg, ay, ga, gb, ret_norm_g, w_ret, w_attn, w_out)
    gate, up, h2, act, dx2, loss, dg3 = _ffn_fwd(x1, tgt, ln2_g, lnf_g, w_gate, w_up, w_down)

    dgate, dup, dx1, dg2 = _ffn_bwd(dx2, gate, up, x1, ln2_g, w_gate, w_up, w_down)
    early = _wgrad([act], dx2, 6, None, "wgrad_down")
    early = _wgrad([dgate], h2, 4, early, "wgrad_gate")
    early = _wgrad([dup], h2, 5, early, "wgrad_up")
    dgates, drg, dba, dbb, day, dry, dgr, db_mid = _merge_bwd(dx1, ga, gb, ba, bb, rg, ry, ret_norm_g,
                                                                w_ret, w_attn, w_out)
    early = _wgrad([merged], dx1, 3, early, "wgrad_out")
    early = _wgrad([u], dba, 1, early, "wgrad_ret")
    e32, e16 = _wgrad([ay], dbb, 2, early, "wgrad_attn")

    grp = EARLY
    sems, flight, token = _exchange_start("pair_start_early", [e16, _empty(N_CHIPS * grp.half, BF)], N_CHIPS,
                                          _plan_pair_exchange(grp))
    daq, dak, dav, dsk = _attn_bwd(aq, ak, av, day, attn_sinks, ca, sa, token)
    _, got = _exchange_wait("pair_wait_early", sems, flight, _plan_pair_exchange(grp), after=(dsk,))
    s32, s16 = _pair_sum(place, e32, got, grp)
    sems, flight, token = _exchange_start("chips_start_early", [s16, _empty(3 * grp.half, BF)], 3, _plan_to_chips(grp))
    d_ret = _ret_bwd(rq, rk, rv, dry, states, dmat, qd, kd, cd, cr, sr, token)
    d_proj = [d_ret, drg, daq, dak, dav, dgates]
    grad_x, dg1, db = _proj_bwd(d_proj, x, dx1, ln1_g, w_in)
    late = _wgrad(d_proj, h, 0, None, "wgrad_in")
    _, others = _exchange_wait("chips_wait_early", sems, flight, _plan_to_chips(grp), after=(late[0],))
    early_shard = _chip_sum(place, s32, others, grp)

    d_sinks = dsk.reshape(2, 2, PAIRS, 128).sum(-1).transpose(0, 2, 1).reshape(1, ATTN_Q_HEADS)
    small = dict(loss=loss, ln1_g=dg1, b_in=db, ret_norm_g=dgr, attn_sinks=d_sinks,
                 ln2_g=dg2, lnf_g=dg3)
    return grad_x, early_shard, late, small


SMALL_ROWS = 16


def _pack_small(ln1, ret, ln2, lnf, b_in, sinks, loss, name):
    def body(a_ref, b_ref, c_ref, d_ref, bi_ref, sk_ref, ls_ref, o_ref):
        o_ref[...] = jnp.zeros_like(o_ref)
        for row, ref in enumerate((a_ref, b_ref, c_ref, d_ref)):
            o_ref[row:row + 1, :] = ref[...]
        for k in range(6):
            o_ref[4 + k:5 + k, :] = bi_ref[:, k * 1024:(k + 1) * 1024]
        o_ref[10:11, 0:D_IN - 6144] = bi_ref[:, 6144:D_IN]
        o_ref[11:12, 0:ATTN_Q_HEADS] = sk_ref[...]
        o_ref[12:13, 0:LANES] = ls_ref[...]

    return pl.pallas_call(
        body, name=name, in_specs=[_VMEM] * 7, out_specs=_VMEM,
        out_shape=jax.ShapeDtypeStruct((SMALL_ROWS, 1024), F32),
    )(ln1.reshape(1, 1024), ret.reshape(1, 1024), ln2.reshape(1, 1024), lnf.reshape(1, 1024),
      b_in.reshape(1, D_IN), sinks.reshape(1, ATTN_Q_HEADS), loss.reshape(1, LANES))


def _unpack_small(p):
    return dict(ln1_g=p[0:1], ret_norm_g=p[1:2], ln2_g=p[2:3], lnf_g=p[3], b_in=p[4:11].reshape(1, -1)[:, :D_IN],
                attn_sinks=p[11:12, :ATTN_Q_HEADS], loss=p[12, 0])


def kernel(x, ln1_g, w_in, b_in, ret_norm_g, w_ret_out, attn_sinks, w_attn_out, w_out, ln2_g, w_ffn_gate, w_ffn_up, w_ffn_down, lnf_g, loss_target, m_ln1_g, m_w_in, m_b_in, m_ret_norm_g, m_w_ret_out, m_attn_sinks, m_w_attn_out, m_w_out, m_ln2_g, m_w_ffn_gate, m_w_ffn_up, m_w_ffn_down, m_lnf_g, v_ln1_g, v_w_in, v_b_in, v_ret_norm_g, v_w_ret_out, v_attn_sinks, v_w_attn_out, v_w_out, v_ln2_g, v_w_ffn_gate, v_w_ffn_up, v_w_ffn_down, v_lnf_g):
    xi = lax.axis_index("x")
    yi = lax.axis_index("y")
    ci = lax.axis_index("c")
    j = 2 * xi + yi

    place = jnp.stack([j, ci]).astype(jnp.int32)

    names = ["w_in", "w_ret_out", "w_attn_out", "w_out", "w_ffn_gate", "w_ffn_up", "w_ffn_down"]
    transposed = [True, False, False, False, True, True, False]
    view = lambda a, t: a[0].T if t else a[0]
    big_w = [view(a, t) for a, t in zip([w_in, w_ret_out, w_attn_out, w_out, w_ffn_gate, w_ffn_up, w_ffn_down], transposed)]
    big_m = [view(a, t) for a, t in zip([m_w_in, m_w_ret_out, m_w_attn_out, m_w_out, m_w_ffn_gate, m_w_ffn_up, m_w_ffn_down], transposed)]
    big_v = [view(a, t) for a, t in zip([v_w_in, v_w_ret_out, v_w_attn_out, v_w_out, v_w_ffn_gate, v_w_ffn_up, v_w_ffn_down], transposed)]

    staged_in = _stage_weight(place, big_w[0], "stage_" + names[0])
    sems, in_flight, token = _exchange_start("gather_in_start", [staged_in], 3, _plan_gather_halves(0))
    staged = [_stage_weight(place, w, "stage_" + n, after=(token,)) for n, w in zip(names[1:], big_w[1:])]
    tables = _tables(x.shape[1], token)
    decays = _decays()
    w_in_full = _exchange_wait("gather_in_wait", sems, in_flight, _plan_gather_halves(0),
                               after=(*staged, *tables, *decays))[0]
    w_in_full = _exchange("gather_in_forward", [w_in_full], 3, _plan_forward_halves(0))[0]
    rest = list(range(1, 7))
    sems, in_flight, token = _exchange_start("gather_start", staged, 3 * len(rest), _plan_gather_direct(rest),
                                             after=(w_in_full,))
    other_weights = lambda after: _exchange_wait("gather_wait", sems, in_flight, _plan_gather_direct(rest),
                                                 after=(after,))

    grad_x, early_shard, (l32, l16), small = _local_step(
        place, x[0], loss_target[0], ln1_g, b_in + token[0, 0], ret_norm_g, attn_sinks, ln2_g, lnf_g.reshape(1, -1),
        w_in_full, other_weights, tables, decays)
    pg = _all_reduce_small(_pack_small(
        small["ln1_g"], small["ret_norm_g"], small["ln2_g"], small["lnf_g"], small["b_in"],
        small["attn_sinks"], small["loss"], "pack_small_grads"), l32)
    early_shard = _exchange("share_early", [early_shard], 1, _plan_share_halves(EARLY))[0]
    _, got = _exchange("pair_exchange_late", [l16, _empty(N_CHIPS * LATE.half, BF)], N_CHIPS, _plan_pair_exchange(LATE))
    s32, s16 = _pair_sum(place, l32, got, LATE)
    sems, flight, token = _exchange_start("chips_start_late", [s16, _empty(3 * LATE.half, BF)], 3, _plan_to_chips(LATE),
                                          after=(pg,))

    upd = {}

    def update(t, shard, after):
        res = _adamw(big_w[t], shard, big_m[t], big_v[t], "adamw_" + names[t], g_row0=GROUP_OF[t].off(t), after=after)
        upd[names[t]] = tuple((r.T if transposed[t] else r)[None] for r in res)
        return res[0]

    done = [update(t, early_shard, token) for t in EARLY.tensors]

    sm = _unpack_small(pg)
    z = jnp.zeros((1, LANES), F32)
    pw = _pack_small(ln1_g, ret_norm_g, ln2_g, lnf_g, b_in, attn_sinks, z, "pack_small_w")
    pm = _pack_small(m_ln1_g, m_ret_norm_g, m_ln2_g, m_lnf_g, m_b_in, m_attn_sinks, z, "pack_small_m")
    pv = _pack_small(v_ln1_g, v_ret_norm_g, v_ln2_g, v_lnf_g, v_b_in, v_attn_sinks, z, "pack_small_v")
    small_res = _adamw(pw, pg, pm, pv, "adamw_small", after=token)
    sd, smn, svn = (_unpack_small(t) for t in small_res[1:])
    for n in ["ln1_g", "b_in", "ret_norm_g", "attn_sinks", "ln2_g", "lnf_g"]:
        upd[n] = (sm[n], sd[n], smn[n], svn[n])

    _, others = _exchange_wait("chips_wait_late", sems, flight, _plan_to_chips(LATE), after=(*done, small_res[1]))
    late_shard = _exchange("share_late", [_chip_sum(place, s32, others, LATE)], 1, _plan_share_halves(LATE))[0]
    update(0, late_shard, None)

    order_w = ["ln1_g", "w_in", "b_in", "ret_norm_g", "w_ret_out", "attn_sinks", "w_attn_out", "w_out",
               "ln2_g", "w_ffn_gate", "w_ffn_up", "w_ffn_down", "lnf_g"]
    outs = [sm["loss"], grad_x[None]]
    for k in range(4):
        outs += [upd[n][k] for n in order_w]
    return tuple(outs)
```

```python
from typing import NamedTuple

import jax
import jax.numpy as jnp
from jax import lax
from jax.experimental import pallas as pl
from jax.experimental.pallas import tpu as pltpu

D_MODEL = 1024
SEQ_CHUNK = 128
RET_CHUNK = 256
RET_CHUNKS_PER_STEP = 2
RET_HEADS = 4
RET_QK_DIM = 128
RET_V_DIM = 256
ATTN_Q_HEADS = 16
ATTN_HEAD_DIM = 64
D_FF = 2816
D_IN = 6400
ROPE_THETA = 10000.0
EPS = 1e-6
N_CHIPS = 4

ADAM_LR = 0.001
ADAM_B1 = 0.9
ADAM_B2 = 0.999
ADAM_EPS = 1e-08
ADAM_WD = 0.01
ADAM_STEP = 10

OFF_RQ, OFF_RK, OFF_RV, OFF_RG = 0, 512, 1024, 2048
OFF_AQ, OFF_AK, OFF_AV, OFF_GA, OFF_GB = 3072, 4096, 4224, 4352, 5376

V7X_VMEM_LIMIT = 56 * 1024 * 1024
LANES = 128

BF = jnp.bfloat16
F32 = jnp.float32
MESH = pl.DeviceIdType.MESH

_VMEM = pl.BlockSpec(memory_space=pltpu.VMEM)
_SMEM = pl.BlockSpec(memory_space=pltpu.SMEM)
_ANY = pl.BlockSpec(memory_space=pl.ANY)


def _dot(a, b):
    return jnp.dot(a, b, preferred_element_type=F32)


def _dot_nt(a, b):
    return lax.dot_general(a, b, (((1,), (1,)), ((), ())), preferred_element_type=F32)


def _dot_tn(a, b):
    return lax.dot_general(a, b, (((0,), (0,)), ((), ())), preferred_element_type=F32)


def _sigmoid(x):
    return 1.0 / (1.0 + jnp.exp(-x))


def _rows(tm, n):
    return pl.BlockSpec((tm, n), lambda i: (i, 0))


def _fixed(shape):
    nd = len(shape)
    return pl.BlockSpec(shape, lambda i: (0,) * nd)


def _params(sem, **kw):
    return pltpu.CompilerParams(dimension_semantics=sem, vmem_limit_bytes=V7X_VMEM_LIMIT, **kw)


def _rot_ret(p, cr, sr):
    return p * cr + pltpu.roll(p, 64, 1) * sr


def _rot_ret_bwd(d, cr, sr):
    return d * cr - pltpu.roll(d, 64, 1) * sr


def _attn_partner(p):
    lane = lax.broadcasted_iota(jnp.int32, p.shape, 1)
    first = (lane % 64) < 32
    return jnp.where(first, pltpu.roll(p, 96, 1), pltpu.roll(p, 32, 1))


def _rot_attn(p, ca, sa):
    return p * ca + _attn_partner(p) * sa


def _rot_attn_bwd(d, ca, sa):
    return d * ca - _attn_partner(d) * sa


def _proj_fwd(x, g1, w_in, b_in, cr, sr, ca, sa):
    S = x.shape[0]
    tm = 512
    scale = RET_QK_DIM ** -0.5

    def body(x_ref, g_ref, w_ref, b_ref, cr_ref, sr_ref, ca_ref, sa_ref,
             h_ref, rq_ref, rk_ref, rv_ref, rg_ref, aq_ref, ak_ref, av_ref, ga_ref, gb_ref):
        xv = x_ref[...]
        r = lax.rsqrt(jnp.mean(xv * xv, axis=-1, keepdims=True) + EPS)
        h = (xv * r * g_ref[...]).astype(BF)
        h_ref[...] = h

        def piece(lo, n):
            return _dot_nt(h, w_ref[lo:lo + n, :]) + b_ref[:, lo:lo + n]

        crv, srv = cr_ref[...], sr_ref[...]
        pq = piece(OFF_RQ, 512)
        pk = piece(OFF_RK, 512)
        for hd in range(RET_HEADS):
            sl = slice(hd * 128, (hd + 1) * 128)
            rq_ref[:, sl] = (_rot_ret(pq[:, sl], crv, srv) * scale).astype(BF)
            rk_ref[:, sl] = _rot_ret(pk[:, sl], crv, srv)
        rv_ref[...] = piece(OFF_RV, 1024).astype(BF)
        rg_ref[...] = piece(OFF_RG, 1024)
        cav, sav = ca_ref[...], sa_ref[...]
        pa = piece(OFF_AQ, 1024)
        for blk in range(8):
            sl = slice(blk * 128, (blk + 1) * 128)
            aq_ref[:, sl] = _rot_attn(pa[:, sl], cav, sav).astype(BF)
        pkv = piece(OFF_AK, 256)
        ak_ref[...] = _rot_attn(pkv[:, :128], cav, sav).astype(BF)
        av_ref[...] = pkv[:, 128:].astype(BF)
        ga_ref[...] = piece(OFF_GA, 1024)
        gb_ref[...] = piece(OFF_GB, 1024)

    outs = [(D_MODEL, BF), (512, BF), (512, F32), (1024, BF), (1024, F32),
            (1024, BF), (128, BF), (128, BF), (1024, F32), (1024, F32)]
    return pl.pallas_call(
        body, name="proj_fwd", grid=(S // tm,),
        in_specs=[_rows(tm, D_MODEL), _fixed((1, D_MODEL)), _VMEM, _VMEM,
                  _rows(tm, 128), _rows(tm, 128), _rows(tm, 128), _rows(tm, 128)],
        out_specs=[_rows(tm, n) for n, _ in outs],
        out_shape=[jax.ShapeDtypeStruct((S, n), dt) for n, dt in outs],
        compiler_params=_params(("parallel",)),
    )(x, g1, w_in, b_in, cr, sr, ca, sa)


def _ret_fwd(rq, rk, rv, dmat, qd, kd, cd):
    S = rq.shape[0]
    N = S // RET_CHUNK
    cps = RET_CHUNKS_PER_STEP
    rows = cps * RET_CHUNK

    def body(q_ref, k_ref, v_ref, dm_ref, qd_ref, kd_ref, cd_ref, y_ref, st_ref, state):
        @pl.when(pl.program_id(0) == 0)
        def _():
            state[...] = jnp.zeros_like(state)

        for hd in range(RET_HEADS):
            qs = slice(hd * 128, (hd + 1) * 128)
            vs = slice(hd * 256, (hd + 1) * 256)
            s = state[hd]
            for ch in range(cps):
                rs = slice(ch * RET_CHUNK, (ch + 1) * RET_CHUNK)
                q = q_ref[rs, qs]
                kf = k_ref[rs, qs]
                k = kf.astype(BF)
                v = v_ref[rs, vs]
                sb = s.astype(BF)
                st_ref[ch, hd] = sb
                a = _dot_nt(q, k) * dm_ref[hd]
                inner = _dot(a.astype(BF), v)
                cross = _dot(q, sb) * qd_ref[hd]
                y_ref[rs, vs] = inner + cross
                kdv = (kf * kd_ref[hd]).astype(BF)
                s = s * cd_ref[hd] + _dot_tn(kdv, v)
            state[hd] = s

    return pl.pallas_call(
        body, name="ret_fwd", grid=(N // cps,),
        in_specs=[_rows(rows, 512), _rows(rows, 512), _rows(rows, 1024), _VMEM, _VMEM, _VMEM, _VMEM],
        out_specs=[_rows(rows, 1024), pl.BlockSpec((cps, RET_HEADS, 128, 256), lambda i: (i, 0, 0, 0))],
        out_shape=[jax.ShapeDtypeStruct((S, 1024), F32),
                   jax.ShapeDtypeStruct((N, RET_HEADS, 128, 256), BF)],
        scratch_shapes=[pltpu.VMEM((RET_HEADS, 128, 256), F32)],
        compiler_params=_params(("arbitrary",)),
    )(rq, rk, rv, dmat, qd, kd, cd)


def _kv_halves(prev_ref, cur_ref):
    kk = jnp.concatenate([prev_ref[...], cur_ref[...]], axis=0).astype(F32)
    return kk, pltpu.roll(kk, 64, 1)


PAIRS = 4


def _attn_fold(n):
    t = lax.broadcasted_iota(jnp.int32, (128, PAIRS * 128), 0)
    qi = lax.broadcasted_iota(jnp.int32, (128, PAIRS * 128), 1) & 127
    cur = t <= qi
    return cur, jnp.logical_or(cur, n > 0)


def _group_operands(g, kk, kk_sw, vv, vv_sw):
    lo = lax.broadcasted_iota(jnp.int32, (256, 128), 1) < 64
    k_a, k_b = (kk, kk_sw) if g == 0 else (kk_sw, kk)
    v_a, v_b = (vv, vv_sw) if g == 0 else (vv_sw, vv)
    return (jnp.where(lo, k_a, 0.0).astype(BF), jnp.where(lo, 0.0, k_b).astype(BF),
            jnp.where(lo, v_a, 0.0).astype(BF), jnp.where(lo, 0.0, v_b).astype(BF))


def _stack_pairs(ref, g):
    return jnp.concatenate([ref[:, (4 * g + p) * 128:(4 * g + p + 1) * 128] for p in range(PAIRS)], axis=0)


def _sink_row(sk_ref, g, which):
    return jnp.concatenate([jnp.full((1, 128), sk_ref[0, 8 * g + 2 * p + which], F32) for p in range(PAIRS)], axis=1)


def _fold(full, cur):
    return jnp.where(cur, full[128:], full[:128])


def _attn_probs(qs, kx, fold, sink):
    cur, valid = fold
    s = _fold(_dot_nt(kx, qs), cur) * (ATTN_HEAD_DIM ** -0.5)
    s = jnp.where(valid, s, -1e30)
    m = jnp.maximum(jnp.max(s, axis=0, keepdims=True), sink)
    e = jnp.exp(s - m)
    es = jnp.exp(sink - m)
    inv = 1.0 / (jnp.sum(e, axis=0, keepdims=True) + es)
    return e * inv, es * inv


def _unfold(p, cur):
    return jnp.concatenate([jnp.where(cur, 0.0, p), jnp.where(cur, p, 0.0)], axis=0).astype(BF)


def _attn_fwd(aq, ak, av, sinks):
    S = aq.shape[0]
    N = S // SEQ_CHUNK

    def body(sk_ref, q_ref, kp_ref, kc_ref, vp_ref, vc_ref, o_ref):
        kk, kk_sw = _kv_halves(kp_ref, kc_ref)
        vv, vv_sw = _kv_halves(vp_ref, vc_ref)
        fold = _attn_fold(pl.program_id(0))
        for g in range(2):
            k_lo, k_hi, v_lo, v_hi = _group_operands(g, kk, kk_sw, vv, vv_sw)
            qs = _stack_pairs(q_ref, g)
            o = jnp.zeros((PAIRS * 128, 128), F32)
            for which, (kx, vx) in enumerate(((k_lo, v_lo), (k_hi, v_hi))):
                pr, _ = _attn_probs(qs, kx, fold, _sink_row(sk_ref, g, which))
                o = o + _dot_tn(_unfold(pr, fold[0]), vx)
            for p in range(PAIRS):
                o_ref[:, (4 * g + p) * 128:(4 * g + p + 1) * 128] = o[p * 128:(p + 1) * 128].astype(BF)

    prev = lambda i: (jnp.maximum(i - 1, 0), 0)
    return pl.pallas_call(
        body, name="attn_fwd", grid=(N,),
        in_specs=[_SMEM, _rows(128, 1024), pl.BlockSpec((128, 128), prev), _rows(128, 128),
                  pl.BlockSpec((128, 128), prev), _rows(128, 128)],
        out_specs=_rows(128, 1024),
        out_shape=jax.ShapeDtypeStruct((S, 1024), BF),
        compiler_params=_params(("parallel",)),
    )(sinks, aq, ak, ak, av, av)


def _group_norm(y):
    mu = jnp.mean(y, axis=-1, keepdims=True)
    yc = y - mu
    rstd = lax.rsqrt(jnp.mean(yc * yc, axis=-1, keepdims=True) + EPS)
    return yc * rstd, rstd


def _merge_fwd(x, ry, rg, ay, ga, gb, g_ret, w_ret, w_attn, w_out):
    S = x.shape[0]
    tm = 256

    def body(x_ref, ry_ref, rg_ref, ay_ref, ga_ref, gb_ref, gr_ref, wr_ref, wa_ref, wo_ref,
             u_ref, ba_ref, bb_ref, mg_ref, x1_ref):
        for hd in range(RET_HEADS):
            vs = slice(hd * 256, (hd + 1) * 256)
            yn, _ = _group_norm(ry_ref[:, vs])
            rgv = rg_ref[:, vs]
            u_ref[:, vs] = (rgv * _sigmoid(rgv) * (yn * gr_ref[:, vs])).astype(BF)
        ba = _dot(u_ref[...], wr_ref[...])
        bb = _dot(ay_ref[...], wa_ref[...])
        ba_ref[...] = ba
        bb_ref[...] = bb
        merged = (_sigmoid(ga_ref[...]) * ba + _sigmoid(gb_ref[...]) * bb).astype(BF)
        mg_ref[...] = merged
        x1_ref[...] = x_ref[...] + _dot(merged, wo_ref[...])

    outs = [BF, F32, F32, BF, F32]
    return pl.pallas_call(
        body, name="merge_fwd", grid=(S // tm,),
        in_specs=[_rows(tm, 1024)] * 6 + [_fixed((1, 1024)), _VMEM, _VMEM, _VMEM],
        out_specs=[_rows(tm, 1024)] * 5,
        out_shape=[jax.ShapeDtypeStruct((S, 1024), dt) for dt in outs],
        compiler_params=_params(("parallel",)),
    )(x, ry, rg, ay, ga, gb, g_ret, w_ret, w_attn, w_out)


def _ffn_fwd(x1, tgt, g2, g3, w_gate, w_up, w_down):
    S = x1.shape[0]
    tm = 256

    def body(x1_ref, t_ref, g2_ref, g3_ref, wg_ref, wu_ref, wd_ref,
             gate_ref, up_ref, h2_ref, act_ref, dx2_ref, loss_ref, dg3_ref):
        @pl.when(pl.program_id(0) == 0)
        def _():
            loss_ref[...] = jnp.zeros_like(loss_ref)
            dg3_ref[...] = jnp.zeros_like(dg3_ref)

        x1v = x1_ref[...]
        r2 = lax.rsqrt(jnp.mean(x1v * x1v, axis=-1, keepdims=True) + EPS)
        h2 = (x1v * r2 * g2_ref[...]).astype(BF)
        h2_ref[...] = h2
        gate = _dot_nt(h2, wg_ref[...])
        up = _dot_nt(h2, wu_ref[...])
        gate_ref[...] = gate
        up_ref[...] = up
        act = (gate * _sigmoid(gate) * up).astype(BF)
        act_ref[...] = act
        x2 = x1v + _dot(act, wd_ref[...])
        r3 = lax.rsqrt(jnp.mean(x2 * x2, axis=-1, keepdims=True) + EPS)
        xh = x2 * r3
        g3v = g3_ref[...]
        err = xh * g3v - t_ref[...]
        loss_ref[...] += 0.5 * jnp.sum(jnp.mean(err * err, axis=-1, keepdims=True), axis=0, keepdims=True)
        dy = err * (1.0 / D_MODEL)
        dg3_ref[...] += jnp.sum(dy * xh, axis=0, keepdims=True)
        dxh = dy * g3v
        dx2_ref[...] = r3 * (dxh - xh * jnp.mean(dxh * xh, axis=-1, keepdims=True))

    return pl.pallas_call(
        body, name="ffn_fwd", grid=(S // tm,),
        in_specs=[_rows(tm, 1024), _rows(tm, 1024), _fixed((1, 1024)), _fixed((1, 1024)), _VMEM, _VMEM, _VMEM],
        out_specs=[_rows(tm, D_FF), _rows(tm, D_FF), _rows(tm, 1024), _rows(tm, D_FF), _rows(tm, 1024),
                   _fixed((1, LANES)), _fixed((1, 1024))],
        out_shape=[jax.ShapeDtypeStruct((S, D_FF), F32), jax.ShapeDtypeStruct((S, D_FF), F32),
                   jax.ShapeDtypeStruct((S, 1024), BF), jax.ShapeDtypeStruct((S, D_FF), BF),
                   jax.ShapeDtypeStruct((S, 1024), F32),
                   jax.ShapeDtypeStruct((1, LANES), F32), jax.ShapeDtypeStruct((1, 1024), F32)],
        compiler_params=_params(("arbitrary",)),
    )(x1, tgt, g2, g3, w_gate, w_up, w_down)


def _ffn_bwd(dx2, gate, up, x1, g2, w_gate, w_up, w_down):
    S = x1.shape[0]
    tm = 256

    def body(dx2_ref, gate_ref, up_ref, x1_ref, g2_ref, wg_ref, wu_ref, wd_ref,
             dgate_ref, dup_ref, dx1_ref, dg2_ref):
        @pl.when(pl.program_id(0) == 0)
        def _():
            dg2_ref[...] = jnp.zeros_like(dg2_ref)

        dx2v = dx2_ref[...]
        dact = _dot_nt(dx2v.astype(BF), wd_ref[...])
        gate = gate_ref[...]
        sg = _sigmoid(gate)
        dgate = (dact * up_ref[...] * (sg * (1.0 + gate * (1.0 - sg)))).astype(BF)
        dup = (dact * (gate * sg)).astype(BF)
        dgate_ref[...] = dgate
        dup_ref[...] = dup
        dh2 = _dot(dgate, wg_ref[...]) + _dot(dup, wu_ref[...])
        x1v = x1_ref[...]
        r2 = lax.rsqrt(jnp.mean(x1v * x1v, axis=-1, keepdims=True) + EPS)
        xh = x1v * r2
        dg2_ref[...] += jnp.sum(dh2 * xh, axis=0, keepdims=True)
        dxh = dh2 * g2_ref[...]
        dx1_ref[...] = dx2v + r2 * (dxh - xh * jnp.mean(dxh * xh, axis=-1, keepdims=True))

    return pl.pallas_call(
        body, name="ffn_bwd", grid=(S // tm,),
        in_specs=[_rows(tm, 1024), _rows(tm, D_FF), _rows(tm, D_FF), _rows(tm, 1024), _fixed((1, 1024)),
                  _VMEM, _VMEM, _VMEM],
        out_specs=[_rows(tm, D_FF), _rows(tm, D_FF), _rows(tm, 1024), _fixed((1, 1024))],
        out_shape=[jax.ShapeDtypeStruct((S, D_FF), BF), jax.ShapeDtypeStruct((S, D_FF), BF),
                   jax.ShapeDtypeStruct((S, 1024), F32), jax.ShapeDtypeStruct((1, 1024), F32)],
        compiler_params=_params(("arbitrary",)),
    )(dx2, gate, up, x1, g2, w_gate, w_up, w_down)


def _merge_bwd(dx1, ga, gb, ba, bb, rg, ry, g_ret, w_ret, w_attn, w_out):
    S = dx1.shape[0]
    tm = 256

    def body(dx1_ref, ga_ref, gb_ref, ba_ref, bb_ref, rg_ref, ry_ref, gr_ref, wr_ref, wa_ref, wo_ref,
             dgates_ref, drg_ref, dba_ref, dbb_ref, day_ref, dry_ref, dgr_ref, db_ref):
        @pl.when(pl.program_id(0) == 0)
        def _():
            dgr_ref[...] = jnp.zeros_like(dgr_ref)
            db_ref[...] = jnp.zeros_like(db_ref)

        dm = _dot_nt(dx1_ref[...].astype(BF), wo_ref[...])
        sa = _sigmoid(ga_ref[...])
        sb = _sigmoid(gb_ref[...])
        dga = (dm * ba_ref[...] * (sa * (1.0 - sa))).astype(BF)
        dgb = (dm * bb_ref[...] * (sb * (1.0 - sb))).astype(BF)
        dgates_ref[:, :1024] = dga
        dgates_ref[:, 1024:] = dgb
        dba = (dm * sa).astype(BF)
        dbb = (dm * sb).astype(BF)
        dba_ref[...] = dba
        dbb_ref[...] = dbb
        day_ref[...] = _dot_nt(dbb, wa_ref[...]).astype(BF)
        du = _dot_nt(dba, wr_ref[...])
        for hd in range(RET_HEADS):
            vs = slice(hd * 256, (hd + 1) * 256)
            yn, rstd = _group_norm(ry_ref[:, vs])
            rgv = rg_ref[:, vs]
            sr = _sigmoid(rgv)
            silu = rgv * sr
            duv = du[:, vs]
            grv = gr_ref[:, vs]
            drg = (duv * (yn * grv) * (sr * (1.0 + rgv * (1.0 - sr)))).astype(BF)
            drg_ref[:, vs] = drg
            dyg = duv * silu
            dgr_ref[:, vs] += jnp.sum(dyg * yn, axis=0, keepdims=True)
            dyn = dyg * grv
            dry_ref[:, vs] = rstd * (dyn - jnp.mean(dyn, axis=-1, keepdims=True)
                                     - yn * jnp.mean(dyn * yn, axis=-1, keepdims=True))
            db_ref[:, vs] += jnp.sum(drg.astype(F32), axis=0, keepdims=True)
        db_ref[:, 1024:2048] += jnp.sum(dga.astype(F32), axis=0, keepdims=True)
        db_ref[:, 2048:] += jnp.sum(dgb.astype(F32), axis=0, keepdims=True)

    return pl.pallas_call(
        body, name="merge_bwd", grid=(S // tm,),
        in_specs=[_rows(tm, 1024)] * 7 + [_fixed((1, 1024)), _VMEM, _VMEM, _VMEM],
        out_specs=[_rows(tm, 2048), _rows(tm, 1024), _rows(tm, 1024), _rows(tm, 1024), _rows(tm, 1024),
                   _rows(tm, 1024), _fixed((1, 1024)), _fixed((1, 3072))],
        out_shape=[jax.ShapeDtypeStruct((S, 2048), BF), jax.ShapeDtypeStruct((S, 1024), BF),
                   jax.ShapeDtypeStruct((S, 1024), BF), jax.ShapeDtypeStruct((S, 1024), BF),
                   jax.ShapeDtypeStruct((S, 1024), BF), jax.ShapeDtypeStruct((S, 1024), F32),
                   jax.ShapeDtypeStruct((1, 1024), F32), jax.ShapeDtypeStruct((1, 3072), F32)],
        compiler_params=_params(("arbitrary",)),
    )(dx1, ga, gb, ba, bb, rg, ry, g_ret, w_ret, w_attn, w_out)


def _attn_bwd(aq, ak, av, day, sinks, ca, sa, after):
    S = aq.shape[0]
    N = S // SEQ_CHUNK

    def body(sk_ref, q_ref, kp_ref, kc_ref, vp_ref, vc_ref, do_ref, cq_ref, sq_ref, cp_ref, sp_ref, after_ref,
             dq_ref, dk_ref, dv_ref, dsk_ref, carry_k, carry_v):
        n = pl.program_id(0)

        @pl.when(n == 0)
        def _():
            carry_k[...] = jnp.zeros_like(carry_k)
            carry_v[...] = jnp.zeros_like(carry_v)
            dsk_ref[...] = jnp.zeros_like(dsk_ref)

        @pl.when(n < N)
        def _():
            kk, kk_sw = _kv_halves(kp_ref, kc_ref)
            vv, vv_sw = _kv_halves(vp_ref, vc_ref)
            lo = lax.broadcasted_iota(jnp.int32, (256, 128), 1) < 64
            fold = _attn_fold(n)
            cur = fold[0]
            cq, sq = cq_ref[...], sq_ref[...]
            dk_groups, dv_groups = [], []
            for g in range(2):
                k_lo, k_hi, v_lo, v_hi = _group_operands(g, kk, kk_sw, vv, vv_sw)
                qs = _stack_pairs(q_ref, g)
                dos = _stack_pairs(do_ref, g)
                dq = jnp.zeros((PAIRS * 128, 128), F32)
                dk_half, dv_half = [], []
                for which, (kx, vx) in enumerate(((k_lo, v_lo), (k_hi, v_hi))):
                    pr, ps = _attn_probs(qs, kx, fold, _sink_row(sk_ref, g, which))
                    dp = _fold(_dot_nt(vx, dos), cur)
                    dr = jnp.sum(pr * dp, axis=0, keepdims=True)
                    ds = _unfold(pr * (dp - dr) * (ATTN_HEAD_DIM ** -0.5), cur)
                    dsk_ref[2 * g + which:2 * g + which + 1, :] -= ps * dr
                    dq = dq + _dot_tn(ds, kx)
                    dk_half.append(_dot(ds, qs))
                    dv_half.append(_dot(_unfold(pr, cur), dos))
                for p in range(PAIRS):
                    sl = slice((4 * g + p) * 128, (4 * g + p + 1) * 128)
                    dq_ref[:, sl] = _rot_attn_bwd(dq[p * 128:(p + 1) * 128], cq, sq).astype(BF)
                tk = jnp.where(lo, dk_half[0], dk_half[1])
                tv = jnp.where(lo, dv_half[0], dv_half[1])
                dk_groups.append(tk + pltpu.roll(tk, 64, 1))
                dv_groups.append(tv + pltpu.roll(tv, 64, 1))
            dk_full = jnp.where(lo, dk_groups[0], dk_groups[1])
            dv_full = jnp.where(lo, dv_groups[0], dv_groups[1])
            dk_ref[...] = _rot_attn_bwd(carry_k[...] + dk_full[:128], cp_ref[...], sp_ref[...]).astype(BF)
            dv_ref[...] = (carry_v[...] + dv_full[:128]).astype(BF)
            carry_k[...] = dk_full[128:]
            carry_v[...] = dv_full[128:]

        @pl.when(n == N)
        def _():
            dk_ref[...] = _rot_attn_bwd(carry_k[...], cp_ref[...], sp_ref[...]).astype(BF)
            dv_ref[...] = carry_v[...].astype(BF)

    cur = lambda i: (jnp.minimum(i, N - 1), 0)
    prev = lambda i: (jnp.maximum(i - 1, 0), 0)
    blk = lambda w, im: pl.BlockSpec((128, w), im)
    return pl.pallas_call(
        body, name="attn_bwd", grid=(N + 1,),
        in_specs=[_SMEM, blk(1024, cur), blk(128, prev), blk(128, cur), blk(128, prev), blk(128, cur),
                  blk(1024, cur), blk(128, cur), blk(128, cur), blk(128, prev), blk(128, prev), _ANY],
        out_specs=[blk(1024, cur), blk(128, prev), blk(128, prev), _fixed((4, PAIRS * 128))],
        out_shape=[jax.ShapeDtypeStruct((S, 1024), BF), jax.ShapeDtypeStruct((S, 128), BF),
                   jax.ShapeDtypeStruct((S, 128), BF), jax.ShapeDtypeStruct((4, PAIRS * 128), F32)],
        scratch_shapes=[pltpu.VMEM((128, 128), F32), pltpu.VMEM((128, 128), F32)],
        compiler_params=_params(("arbitrary",)),
    )(sinks, aq, ak, ak, av, av, day, ca, sa, ca, sa, after)


def _ret_bwd(rq, rk, rv, dry, states, dmat, qd, kd, cd, cr, sr, after):
    S = rq.shape[0]
    N = S // RET_CHUNK
    scale = RET_QK_DIM ** -0.5
    cps = RET_CHUNKS_PER_STEP
    steps = N // cps

    def body(q_ref, k_ref, v_ref, dy_ref, st_ref, dm_ref, qd_ref, kd_ref, cd_ref, cr_ref, sr_ref, after_ref,
             out_ref, dstate):
        @pl.when(pl.program_id(0) == 0)
        def _():
            dstate[...] = jnp.zeros_like(dstate)

        for hd in range(RET_HEADS):
            qs = slice(hd * 128, (hd + 1) * 128)
            vs = slice(hd * 256, (hd + 1) * 256)
            ds = dstate[hd]
            dmv = dm_ref[hd]
            for ch in reversed(range(cps)):
                rs = slice(ch * RET_CHUNK, (ch + 1) * RET_CHUNK)
                crv, srv = cr_ref[rs, :], sr_ref[rs, :]
                q = q_ref[rs, qs]
                kf = k_ref[rs, qs]
                k = kf.astype(BF)
                kdv = (kf * kd_ref[hd]).astype(BF)
                v = v_ref[rs, vs]
                do = dy_ref[rs, vs]
                dob = do.astype(BF)
                doq = (do * qd_ref[hd]).astype(BF)
                sn = st_ref[ch, hd]
                dsb = ds.astype(BF)
                a = (_dot_nt(q, k) * dmv).astype(BF)
                dv = _dot_tn(a, dob) + _dot(kdv, dsb)
                da = (_dot_nt(dob, v) * dmv).astype(BF)
                dq = _dot(da, k) + _dot_nt(doq, sn)
                dk = _dot_tn(da, q) + _dot_nt(v, dsb) * kd_ref[hd]
                ds = ds * cd_ref[hd] + _dot_tn(q, doq)
                out_ref[rs, qs] = _rot_ret_bwd(dq * scale, crv, srv).astype(BF)
                out_ref[rs, 512 + hd * 128:512 + (hd + 1) * 128] = _rot_ret_bwd(dk, crv, srv).astype(BF)
                out_ref[rs, 1024 + hd * 256:1024 + (hd + 1) * 256] = dv.astype(BF)
            dstate[hd] = ds

    rev = lambda w: pl.BlockSpec((cps * RET_CHUNK, w), lambda i: (steps - 1 - i, 0))
    return pl.pallas_call(
        body, name="ret_bwd", grid=(steps,),
        in_specs=[rev(512), rev(512), rev(1024), rev(1024),
                  pl.BlockSpec((cps, RET_HEADS, 128, 256), lambda i: (steps - 1 - i, 0, 0, 0)),
                  _VMEM, _VMEM, _VMEM, _VMEM, rev(128), rev(128), _ANY],
        out_specs=rev(2048),
        out_shape=jax.ShapeDtypeStruct((S, 2048), BF),
        scratch_shapes=[pltpu.VMEM((RET_HEADS, 128, 256), F32)],
        compiler_params=_params(("arbitrary",)),
    )(rq, rk, rv, dry, states, dmat, qd, kd, cd, cr, sr, after)


def _proj_bwd(d_pieces, x, dx1, g1, w_in_t):
    S = x.shape[0]
    tm = 256
    widths = [d.shape[1] for d in d_pieces]
    assert sum(widths) == D_IN
    nd = len(d_pieces)

    def body(*refs):
        d_refs = refs[:nd]
        x_ref, dx1_ref, g_ref, w_ref, gx_ref, dg_ref, db_ref = refs[nd:]

        @pl.when(pl.program_id(0) == 0)
        def _():
            dg_ref[...] = jnp.zeros_like(dg_ref)
            db_ref[...] = jnp.zeros_like(db_ref)

        dh = jnp.zeros((tm, D_MODEL), F32)
        lo = 0
        for ref, n in zip(d_refs, widths):
            v = ref[...]
            dh = dh + _dot(v, w_ref[lo:lo + n, :])
            db_ref[:, lo:lo + n] += jnp.sum(v.astype(F32), axis=0, keepdims=True)
            lo += n
        xv = x_ref[...]
        r = lax.rsqrt(jnp.mean(xv * xv, axis=-1, keepdims=True) + EPS)
        xh = xv * r
        dg_ref[...] += jnp.sum(dh * xh, axis=0, keepdims=True)
        dxh = dh * g_ref[...]
        gx_ref[...] = dx1_ref[...] + r * (dxh - xh * jnp.mean(dxh * xh, axis=-1, keepdims=True))

    return pl.pallas_call(
        body, name="proj_bwd", grid=(S // tm,),
        in_specs=[_rows(tm, n) for n in widths] + [_rows(tm, 1024), _rows(tm, 1024), _fixed((1, 1024)), _VMEM],
        out_specs=[_rows(tm, 1024), _fixed((1, 1024)), _fixed((1, D_IN))],
        out_shape=[jax.ShapeDtypeStruct((S, 1024), F32), jax.ShapeDtypeStruct((1, 1024), F32),
                   jax.ShapeDtypeStruct((1, D_IN), F32)],
        compiler_params=_params(("arbitrary",)),
    )(*d_pieces, x, dx1, g1, w_in_t)


SHARD_ROWS = (1600, 256, 256, 256, 704, 704, 704)


class Group(NamedTuple):
    name: str
    tensors: tuple
    offs: tuple
    block: int
    tile: int

    @property
    def half(self):
        return self.block // 2

    def off(self, t):
        return self.offs[self.tensors.index(t)]


LATE = Group("late", (0,), (0,), 1600, 800)
EARLY = Group("early", (1, 2, 3, 4, 5, 6), (0, 256, 512, 768, 1472, 2176), 2880, 480)
GROUP_OF = {t: g for g in (LATE, EARLY) for t in g.tensors}


def _wgrad(a_list, b, t, packed, name):
    S = b.shape[0]
    widths = [a.shape[1] for a in a_list]
    ts = 512 if sum(widths) > 4096 else 1024
    steps = S // ts
    na = len(a_list)
    grp = GROUP_OF[t]
    r, off = SHARD_ROWS[t], grp.off(t)
    assert sum(widths) == N_CHIPS * r
    n_in = na + 1 + (2 if packed is not None else 0)

    def body(*refs):
        a_refs, b_ref = refs[:na], refs[na]
        g32_ref, gb_ref, acc, stage, sems = refs[n_in:]
        i = pl.program_id(0)

        @pl.when(i == 0)
        def _():
            acc[...] = jnp.zeros_like(acc)

        bv = b_ref[...].astype(BF)
        lo = 0
        for a_ref, n in zip(a_refs, widths):
            acc[lo:lo + n, :] += _dot_tn(a_ref[...].astype(BF), bv)
            lo += n

        @pl.when(i == steps - 1)
        def _():
            for chip in range(N_CHIPS):
                src = acc.at[chip * r:(chip + 1) * r, :]
                dst = pl.ds(chip * grp.block + off, r)
                c32 = pltpu.make_async_copy(src, g32_ref.at[dst, :], sems.at[0])
                c32.start()
                stage[...] = acc[chip * r:(chip + 1) * r, :].astype(BF)
                c16 = pltpu.make_async_copy(stage, gb_ref.at[dst, :], sems.at[1])
                c16.start()
                c32.wait()
                c16.wait()

    shapes = [jax.ShapeDtypeStruct((N_CHIPS * grp.block, 1024), F32),
              jax.ShapeDtypeStruct((N_CHIPS * grp.block, 1024), BF)]
    operands = list(a_list) + [b]
    in_specs = [_rows(ts, n) for n in widths] + [_rows(ts, 1024)]
    aliases = {}
    if packed is not None:
        operands += list(packed)
        in_specs += [_ANY, _ANY]
        aliases = {na + 1: 0, na + 2: 1}
    return pl.pallas_call(
        body, name=name, grid=(steps,),
        in_specs=in_specs, out_specs=[_ANY, _ANY], out_shape=shapes,
        scratch_shapes=[pltpu.VMEM((N_CHIPS * r, 1024), F32), pltpu.VMEM((r, 1024), BF),
                        pltpu.SemaphoreType.DMA((2,))],
        input_output_aliases=aliases,
        compiler_params=_params(("arbitrary",)),
    )(*operands)


def _stage_weight(place, w, name, after=()):
    r = w.shape[0]
    tr = 800 if r == 1600 else r
    nb = r // tr

    def body(p_ref, w_ref, *rest):
        rest[-1][...] = w_ref[...].astype(BF)

    return pl.pallas_call(
        body, name=name,
        grid_spec=pltpu.PrefetchScalarGridSpec(
            num_scalar_prefetch=1, grid=(nb,),
            in_specs=[pl.BlockSpec((tr, 1024), lambda i, p: (i, 0))] + [_ANY] * len(after),
            out_specs=pl.BlockSpec((tr, 1024), lambda i, p: (p[0] * nb + i, 0))),
        out_shape=jax.ShapeDtypeStruct((N_CHIPS * r, 1024), BF),
        compiler_params=_params(("arbitrary",)),
    )(place, w, *after)


def _pair_sum(place, g32, got, grp):
    nb = grp.half // grp.tile

    def body(p_ref, g_ref, r_ref, s32_ref, s16_ref):
        s = g_ref[...] + r_ref[...].astype(F32)
        s32_ref[...] = s
        s16_ref[...] = s.astype(BF)

    flat = pl.BlockSpec((grp.tile, 1024), lambda i, p: (i, 0))
    return pl.pallas_call(
        body, name="pair_sum_" + grp.name,
        grid_spec=pltpu.PrefetchScalarGridSpec(
            num_scalar_prefetch=1, grid=(N_CHIPS * nb,),
            in_specs=[pl.BlockSpec((grp.tile, 1024), lambda i, p: ((2 * (i // nb) + p[1]) * nb + i % nb, 0)), flat],
            out_specs=[flat, flat]),
        out_shape=[jax.ShapeDtypeStruct((N_CHIPS * grp.half, 1024), F32),
                   jax.ShapeDtypeStruct((N_CHIPS * grp.half, 1024), BF)],
        compiler_params=_params(("arbitrary",)),
    )(place, g32, got)


def _chip_sum(place, s32, others, grp):
    nb = grp.half // grp.tile

    def body(p_ref, a_ref, r0_ref, r1_ref, r2_ref, o_ref):
        o_ref[...] = ((a_ref[...] + r0_ref[...].astype(F32)) + r1_ref[...].astype(F32)) + r2_ref[...].astype(F32)

    blk = lambda f: pl.BlockSpec((grp.tile, 1024), f)
    return pl.pallas_call(
        body, name="chip_sum_" + grp.name,
        grid_spec=pltpu.PrefetchScalarGridSpec(
            num_scalar_prefetch=1, grid=(nb,),
            in_specs=[blk(lambda i, p: (p[0] * nb + i, 0)), blk(lambda i, p: (i, 0)),
                      blk(lambda i, p: (nb + i, 0)), blk(lambda i, p: (2 * nb + i, 0))],
            out_specs=blk(lambda i, p: (p[1] * nb + i, 0))),
        out_shape=jax.ShapeDtypeStruct((grp.block, 1024), F32),
        compiler_params=_params(("arbitrary",)),
    )(place, s32, others, others, others)


def _adamw(w, g, m, v, name, g_row0=0, after=None):
    r = w.shape[0]
    tr = 320 if r % 320 == 0 and g_row0 % 320 == 0 else (64 if r % 64 == 0 else r)
    assert g_row0 % tr == 0
    g0 = g_row0 // tr
    extra = [] if after is None else [after]

    def body(w_ref, g_ref, m_ref, v_ref, *rest):
        go_ref, d_ref, mo_ref, vo_ref = rest[len(extra):]
        gv = g_ref[...]
        mn = ADAM_B1 * m_ref[...] + (1.0 - ADAM_B1) * gv
        vn = ADAM_B2 * v_ref[...] + (1.0 - ADAM_B2) * (gv * gv)
        m_hat = mn / (1.0 - ADAM_B1 ** ADAM_STEP)
        v_hat = vn / (1.0 - ADAM_B2 ** ADAM_STEP)
        go_ref[...] = gv
        d_ref[...] = -ADAM_LR * (m_hat / (jnp.sqrt(v_hat) + ADAM_EPS) + ADAM_WD * w_ref[...])
        mo_ref[...] = mn
        vo_ref[...] = vn

    return pl.pallas_call(
        body, name=name, grid=(r // tr,),
        in_specs=[_rows(tr, 1024), pl.BlockSpec((tr, 1024), lambda i: (g0 + i, 0)), _rows(tr, 1024), _rows(tr, 1024)]
        + [_ANY] * len(extra),
        out_specs=[_rows(tr, 1024)] * 4,
        out_shape=[jax.ShapeDtypeStruct((r, 1024), F32)] * 4,
        compiler_params=_params(("parallel",)),
    )(w, g, m, v, *extra)


def _place():
    x, y, c = lax.axis_index("x"), lax.axis_index("y"), lax.axis_index("c")
    chips = [(1 - x, y), (x, 1 - y), (1 - x, 1 - y)]
    return x, y, c, chips


_HBM =pl.BlockSpec(memory_space=pltpu.HBM)
_SEM = pl.BlockSpec(memory_space=pltpu.SEMAPHORE)
_DATAFLOW = pltpu.SideEffectType.DATAFLOW_SIDE_EFFECTING


def _exchange(name, arrays, n, plan):
    na = len(arrays)

    def body(*refs):
        pairs = plan(refs[na:2 * na], refs[2 * na], refs[2 * na + 1])
        for send, _ in pairs:
            send.start()
        for send, recv in pairs:
            send.wait_send()
            recv.wait_recv()

    return pl.pallas_call(
        body, name=name, in_specs=[_ANY] * na, out_specs=[_ANY] * na,
        out_shape=[jax.ShapeDtypeStruct(a.shape, a.dtype) for a in arrays],
        scratch_shapes=[pltpu.SemaphoreType.DMA((n,)), pltpu.SemaphoreType.DMA((n,))],
        input_output_aliases={i: i for i in range(na)},
    )(*arrays)


def _exchange_start(name, arrays, n, plan, after=()):
    na = len(arrays)

    def body(*refs):
        for send, _ in plan(refs[:na], refs[na + len(after)], refs[na + len(after) + 1]):
            send.start()
        refs[-1][...] = jnp.zeros_like(refs[-1])

    out = pl.pallas_call(
        body, name=name,
        out_shape=(pltpu.SemaphoreType.DMA((n,)), pltpu.SemaphoreType.DMA((n,)),
                   *[pltpu.HBM(a.shape, a.dtype) for a in arrays], jax.ShapeDtypeStruct((8, LANES), F32)),
        in_specs=[_HBM] * na + [_ANY] * len(after), out_specs=(_SEM, _SEM, *[_HBM] * na, _VMEM),
        input_output_aliases={i: 2 + i for i in range(na)},
        compiler_params=pltpu.CompilerParams(has_side_effects=_DATAFLOW),
    )(*[pltpu.with_memory_space_constraint(a, pltpu.HBM) for a in arrays], *after)
    return (out[0], out[1]), list(out[2:2 + na]), out[-1]


def _exchange_wait(name, sems, in_flight, plan, after=()):
    na = len(in_flight)

    def body(*refs):
        for send, recv in plan(refs[:na], refs[na], refs[na + 1]):
            send.wait_send()
            recv.wait_recv()

    out = pl.pallas_call(
        body, name=name,
        out_shape=tuple(pltpu.HBM(a.shape, a.dtype) for a in in_flight),
        in_specs=[_HBM] * na + [_SEM, _SEM] + [_ANY] * len(after), out_specs=tuple([_HBM] * na),
        input_output_aliases={i: i for i in range(na)},
        compiler_params=pltpu.CompilerParams(has_side_effects=_DATAFLOW),
    )(*in_flight, *sems, *after)
    return list(out)


def _remote(src, dst, send_sems, recv_sems, k, to):
    return pltpu.make_async_remote_copy(src_ref=src, dst_ref=dst, send_sem=send_sems.at[k], recv_sem=recv_sems.at[k],
                                        device_id=to, device_id_type=MESH)


def _plan_gather_direct(tids):
    def plan(refs, ss, rs):
        x, y, c, chips = _place()
        j = 2 * x + y
        pairs = []
        for t, tid in enumerate(tids):
            r = SHARD_ROWS[tid]
            for k, (cx, cy) in enumerate(chips):
                mine = refs[t].at[pl.ds(j * r, r), :]
                theirs = refs[t].at[pl.ds((2 * cx + cy) * r, r), :]
                pairs.append((_remote(mine, mine, ss, rs, 3 * t + k, (cx, cy, c)),
                              _remote(theirs, theirs, ss, rs, 3 * t + k, (x, y, c))))
        return pairs
    return plan


def _half_rows(ref, tid, pj, half):
    h = SHARD_ROWS[tid] // 2
    return ref.at[pl.ds((2 * pj + half) * h, h), :]


def _plan_gather_halves(tid):
    def plan(refs, ss, rs):
        x, y, c, chips = _place()
        mine = _half_rows(refs[0], tid, 2 * x + y, c)
        return [(_remote(mine, mine, ss, rs, k, (cx, cy, c)),
                 _remote(_half_rows(refs[0], tid, 2 * cx + cy, c), _half_rows(refs[0], tid, 2 * cx + cy, c),
                         ss, rs, k, (x, y, c))) for k, (cx, cy) in enumerate(chips)]
    return plan


def _plan_forward_halves(tid):
    def plan(refs, ss, rs):
        x, y, c, chips = _place()
        pairs = []
        for k, (cx, cy) in enumerate(chips):
            got = _half_rows(refs[0], tid, 2 * cx + cy, c)
            missing = _half_rows(refs[0], tid, 2 * cx + cy, 1 - c)
            pairs.append((_remote(got, got, ss, rs, k, (x, y, 1 - c)), _remote(missing, missing, ss, rs, k, (x, y, 1 - c))))
        return pairs
    return plan


def _plan_pair_exchange(grp):
    def plan(refs, ss, rs):
        x, y, c, _ = _place()
        cps = [_remote(refs[0].at[pl.ds(pj * grp.block + (1 - c) * grp.half, grp.half), :],
                       refs[1].at[pl.ds(pj * grp.half, grp.half), :], ss, rs, pj, (x, y, 1 - c))
               for pj in range(N_CHIPS)]
        return [(cp, cp) for cp in cps]
    return plan


def _plan_to_chips(grp):
    def plan(refs, ss, rs):
        x, y, c, chips = _place()
        cps = [_remote(refs[0].at[pl.ds((2 * cx + cy) * grp.half, grp.half), :],
                       refs[1].at[pl.ds(k * grp.half, grp.half), :], ss, rs, k, (cx, cy, c))
               for k, (cx, cy) in enumerate(chips)]
        return [(cp, cp) for cp in cps]
    return plan


def _plan_share_halves(grp):
    def plan(refs, ss, rs):
        x, y, c, _ = _place()
        mine = refs[0].at[pl.ds(c * grp.half, grp.half), :]
        other = refs[0].at[pl.ds((1 - c) * grp.half, grp.half), :]
        return [(_remote(mine, mine, ss, rs, 0, (x, y, 1 - c)), _remote(other, other, ss, rs, 0, (x, y, 1 - c)))]
    return plan


def _empty(rows, dtype):
    return lax.empty((rows, 1024), dtype)


def _all_reduce_small(v, after):
    m_per, n = v.shape
    n_dev = 8

    def body(x_ref, after_ref, out_ref, all_ref, send_sems, recv_sems, local_sem):
        x, y, c, chips = _place()
        me, sibling = (x, y, c), (x, y, 1 - c)

        def rows(px, py, pc):
            return all_ref.at[pl.ds((4 * px + 2 * py + pc) * m_per, m_per), :]

        def copy(k, block, to, src=None):
            return pltpu.make_async_remote_copy(
                src_ref=rows(*block) if src is None else src, dst_ref=rows(*block),
                send_sem=send_sems.at[k], recv_sem=recv_sems.at[k], device_id=to, device_id_type=MESH)

        mine = pltpu.make_async_copy(x_ref, rows(*me), local_sem)
        mine.start()
        first = [copy(0, me, sibling, src=x_ref)]
        first += [copy(1 + k, me, (*chip, c), src=x_ref) for k, chip in enumerate(chips)]
        for cp in first:
            cp.start()
        passed = [copy(4 + k, (*chip, c), sibling) for k, chip in enumerate(chips)]
        for k, chip in enumerate(chips):
            copy(1 + k, (*chip, c), me).wait_recv()
            passed[k].start()
        copy(0, sibling, me).wait_recv()
        for k, chip in enumerate(chips):
            copy(4 + k, (*chip, 1 - c), me).wait_recv()
        for cp in first + passed:
            cp.wait_send()
        mine.wait()
        acc = all_ref[pl.ds(0, m_per), :]
        for d in range(1, n_dev):
            acc = acc + all_ref[pl.ds(d * m_per, m_per), :]
        out_ref[...] = acc

    return pl.pallas_call(
        body, name="all_reduce_small",
        in_specs=[_VMEM, _ANY], out_specs=_VMEM,
        out_shape=jax.ShapeDtypeStruct((m_per, n), F32),
        scratch_shapes=[pltpu.VMEM((n_dev * m_per, n), F32),
                        pltpu.SemaphoreType.DMA((7,)), pltpu.SemaphoreType.DMA((7,)), pltpu.SemaphoreType.DMA],
        compiler_params=pltpu.CompilerParams(has_side_effects=True),
    )(v, after)


def _tables(S, token):
    pos = jnp.arange(S, dtype=jnp.int32).astype(F32) + token[0, 0]

    def cs(half):
        inv_freq = ROPE_THETA ** (-jnp.arange(half, dtype=F32) / half)
        ang = pos[:, None] * inv_freq[None, :]
        return jnp.cos(ang), jnp.sin(ang)

    c64, s64 = cs(64)
    c32, s32 = cs(32)
    cr = jnp.concatenate([c64, c64], axis=1)
    sr = jnp.concatenate([-s64, s64], axis=1)
    ca = jnp.concatenate([c32, c32, c32, c32], axis=1)
    sa = jnp.concatenate([-s32, s32, -s32, s32], axis=1)
    return cr, sr, ca, sa


def _decays():
    H, C = RET_HEADS, RET_CHUNK
    log_gamma = jnp.log1p(-jnp.exp2(-5.0 - jnp.arange(H, dtype=F32)))
    idx = jnp.arange(C, dtype=F32)
    rel = idx[:, None] - idx[None, :]
    dmat = jnp.where(rel[None] >= 0, jnp.exp(log_gamma[:, None, None] * jnp.maximum(rel, 0.0)[None]), 0.0)
    qd = jnp.exp(log_gamma[:, None] * (idx + 1.0))[:, :, None]
    kd = jnp.exp(log_gamma[:, None] * (C - 1.0 - idx))[:, :, None]
    cd = jnp.exp(log_gamma * C)[:, None, None]
    return dmat, qd, kd, cd


def _local_step(place, x, tgt, ln1_g, b_in, ret_norm_g, attn_sinks, ln2_g, lnf_g, w_in, other_weights, tables, decays):
    S = x.shape[0]
    cr, sr, ca, sa = tables
    dmat, qd, kd, cd = decays

    h, rq, rk, rv, rg, aq, ak, av, ga, gb = _proj_fwd(x, ln1_g, w_in, b_in, cr, sr, ca, sa)
    ry, states = _ret_fwd(rq, rk, rv, dmat, qd, kd, cd)
    ay = _attn_fwd(aq, ak, av, attn_sinks)
    w_ret, w_attn, w_out, w_gate, w_up, w_down = other_weights(ay)
    u, ba, bb, merged, x1 = _merge_fwd(x, ry, rg, ay, ga, gb, ret_norm_g, w_ret, w_attn, w_out)
    gate, up, h2, act, dx2, loss, dg3 = _ffn_fwd(x1, tgt, ln2_g, lnf_g, w_gate, w_up, w_down)

    dgate, dup, dx1, dg2 = _ffn_bwd(dx2, gate, up, x1, ln2_g, w_gate, w_up, w_down)
    early = _wgrad([act], dx2, 6, None, "wgrad_down")
    early = _wgrad([dgate], h2, 4, early, "wgrad_gate")
    early = _wgrad([dup], h2, 5, early, "wgrad_up")
    dgates, drg, dba, dbb, day, dry, dgr, db_mid = _merge_bwd(dx1, ga, gb, ba, bb, rg, ry, ret_norm_g,
                                                                w_ret, w_attn, w_out)
    early = _wgrad([merged], dx1, 3, early, "wgrad_out")
    early = _wgrad([u], dba, 1, early, "wgrad_ret")
    e32, e16 = _wgrad([ay], dbb, 2, early, "wgrad_attn")

    grp = EARLY
    sems, flight, token = _exchange_start("pair_start_early", [e16, _empty(N_CHIPS * grp.half, BF)], N_CHIPS,
                                          _plan_pair_exchange(grp))
    daq, dak, dav, dsk = _attn_bwd(aq, ak, av, day, attn_sinks, ca, sa, token)
    _, got = _exchange_wait("pair_wait_early", sems, flight, _plan_pair_exchange(grp), after=(dsk,))
    s32, s16 = _pair_sum(place, e32, got, grp)
    sems, flight, token = _exchange_start("chips_start_early", [s16, _empty(3 * grp.half, BF)], 3, _plan_to_chips(grp))
    d_ret = _ret_bwd(rq, rk, rv, dry, states, dmat, qd, kd, cd, cr, sr, token)
    d_proj = [d_ret, drg, daq, dak, dav, dgates]
    grad_x, dg1, db = _proj_bwd(d_proj, x, dx1, ln1_g, w_in)
    late = _wgrad(d_proj, h, 0, None, "wgrad_in")
    _, others = _exchange_wait("chips_wait_early", sems, flight, _plan_to_chips(grp), after=(late[0],))
    early_shard = _chip_sum(place, s32, others, grp)

    d_sinks = dsk.reshape(2, 2, PAIRS, 128).sum(-1).transpose(0, 2, 1).reshape(1, ATTN_Q_HEADS)
    small = dict(loss=loss, ln1_g=dg1, b_in=db, ret_norm_g=dgr, attn_sinks=d_sinks,
                 ln2_g=dg2, lnf_g=dg3)
    return grad_x, early_shard, late, small


SMALL_ROWS = 16


def _pack_small(ln1, ret, ln2, lnf, b_in, sinks, loss, name):
    def body(a_ref, b_ref, c_ref, d_ref, bi_ref, sk_ref, ls_ref, o_ref):
        o_ref[...] = jnp.zeros_like(o_ref)
        for row, ref in enumerate((a_ref, b_ref, c_ref, d_ref)):
            o_ref[row:row + 1, :] = ref[...]
        for k in range(6):
            o_ref[4 + k:5 + k, :] = bi_ref[:, k * 1024:(k + 1) * 1024]
        o_ref[10:11, 0:D_IN - 6144] = bi_ref[:, 6144:D_IN]
        o_ref[11:12, 0:ATTN_Q_HEADS] = sk_ref[...]
        o_ref[12:13, 0:LANES] = ls_ref[...]

    return pl.pallas_call(
        body, name=name, in_specs=[_VMEM] * 7, out_specs=_VMEM,
        out_shape=jax.ShapeDtypeStruct((SMALL_ROWS, 1024), F32),
    )(ln1.reshape(1, 1024), ret.reshape(1, 1024), ln2.reshape(1, 1024), lnf.reshape(1, 1024),
      b_in.reshape(1, D_IN), sinks.reshape(1, ATTN_Q_HEADS), loss.reshape(1, LANES))


def _unpack_small(p):
    return dict(ln1_g=p[0:1], ret_norm_g=p[1:2], ln2_g=p[2:3], lnf_g=p[3], b_in=p[4:11].reshape(1, -1)[:, :D_IN],
                attn_sinks=p[11:12, :ATTN_Q_HEADS], loss=p[12, 0])


def kernel(x, ln1_g, w_in, b_in, ret_norm_g, w_ret_out, attn_sinks, w_attn_out, w_out, ln2_g, w_ffn_gate, w_ffn_up, w_ffn_down, lnf_g, loss_target, m_ln1_g, m_w_in, m_b_in, m_ret_norm_g, m_w_ret_out, m_attn_sinks, m_w_attn_out, m_w_out, m_ln2_g, m_w_ffn_gate, m_w_ffn_up, m_w_ffn_down, m_lnf_g, v_ln1_g, v_w_in, v_b_in, v_ret_norm_g, v_w_ret_out, v_attn_sinks, v_w_attn_out, v_w_out, v_ln2_g, v_w_ffn_gate, v_w_ffn_up, v_w_ffn_down, v_lnf_g):
    xi = lax.axis_index("x")
    yi = lax.axis_index("y")
    ci = lax.axis_index("c")
    j = 2 * xi + yi

    place = jnp.stack([j, ci]).astype(jnp.int32)

    names = ["w_in", "w_ret_out", "w_attn_out", "w_out", "w_ffn_gate", "w_ffn_up", "w_ffn_down"]
    transposed = [True, False, False, False, True, True, False]
    view = lambda a, t: a[0].T if t else a[0]
    big_w = [view(a, t) for a, t in zip([w_in, w_ret_out, w_attn_out, w_out, w_ffn_gate, w_ffn_up, w_ffn_down], transposed)]
    big_m = [view(a, t) for a, t in zip([m_w_in, m_w_ret_out, m_w_attn_out, m_w_out, m_w_ffn_gate, m_w_ffn_up, m_w_ffn_down], transposed)]
    big_v = [view(a, t) for a, t in zip([v_w_in, v_w_ret_out, v_w_attn_out, v_w_out, v_w_ffn_gate, v_w_ffn_up, v_w_ffn_down], transposed)]

    staged_in = _stage_weight(place, big_w[0], "stage_" + names[0])
    sems, in_flight, token = _exchange_start("gather_in_start", [staged_in], 3, _plan_gather_halves(0))
    staged = [_stage_weight(place, w, "stage_" + n, after=(token,)) for n, w in zip(names[1:], big_w[1:])]
    tables = _tables(x.shape[1], token)
    decays = _decays()
    w_in_full = _exchange_wait("gather_in_wait", sems, in_flight, _plan_gather_halves(0),
                               after=(*staged, *tables, *decays))[0]
    w_in_full = _exchange("gather_in_forward", [w_in_full], 3, _plan_forward_halves(0))[0]
    rest = list(range(1, 7))
    sems, in_flight, token = _exchange_start("gather_start", staged, 3 * len(rest), _plan_gather_direct(rest),
                                             after=(w_in_full,))
    other_weights = lambda after: _exchange_wait("gather_wait", sems, in_flight, _plan_gather_direct(rest),
                                                 after=(after,))

    grad_x, early_shard, (l32, l16), small = _local_step(
        place, x[0], loss_target[0], ln1_g, b_in + token[0, 0], ret_norm_g, attn_sinks, ln2_g, lnf_g.reshape(1, -1),
        w_in_full, other_weights, tables, decays)
    pg = _all_reduce_small(_pack_small(
        small["ln1_g"], small["ret_norm_g"], small["ln2_g"], small["lnf_g"], small["b_in"],
        small["attn_sinks"], small["loss"], "pack_small_grads"), l32)
    early_shard = _exchange("share_early", [early_shard], 1, _plan_share_halves(EARLY))[0]
    _, got = _exchange("pair_exchange_late", [l16, _empty(N_CHIPS * LATE.half, BF)], N_CHIPS, _plan_pair_exchange(LATE))
    s32, s16 = _pair_sum(place, l32, got, LATE)
    sems, flight, token = _exchange_start("chips_start_late", [s16, _empty(3 * LATE.half, BF)], 3, _plan_to_chips(LATE),
                                          after=(pg,))

    upd = {}

    def update(t, shard, after):
        res = _adamw(big_w[t], shard, big_m[t], big_v[t], "adamw_" + names[t], g_row0=GROUP_OF[t].off(t), after=after)
        upd[names[t]] = tuple((r.T if transposed[t] else r)[None] for r in res)
        return res[0]

    done = [update(t, early_shard, token) for t in EARLY.tensors]

    sm = _unpack_small(pg)
    z = jnp.zeros((1, LANES), F32)
    pw = _pack_small(ln1_g, ret_norm_g, ln2_g, lnf_g, b_in, attn_sinks, z, "pack_small_w")
    pm = _pack_small(m_ln1_g, m_ret_norm_g, m_ln2_g, m_lnf_g, m_b_in, m_attn_sinks, z, "pack_small_m")
    pv = _pack_small(v_ln1_g, v_ret_norm_g, v_ln2_g, v_lnf_g, v_b_in, v_attn_sinks, z, "pack_small_v")
    small_res = _adamw(pw, pg, pm, pv, "adamw_small", after=token)
    sd, smn, svn = (_unpack_small(t) for t in small_res[1:])
    for n in ["ln1_g", "b_in", "ret_norm_g", "attn_sinks", "ln2_g", "lnf_g"]:
        upd[n] = (sm[n], sd[n], smn[n], svn[n])

    _, others = _exchange_wait("chips_wait_late", sems, flight, _plan_to_chips(LATE), after=(*done, small_res[1]))
    late_shard = _exchange("share_late", [_chip_sum(place, s32, others, LATE)], 1, _plan_share_halves(LATE))[0]
    update(0, late_shard, None)

    order_w = ["ln1_g", "w_in", "b_in", "ret_norm_g", "w_ret_out", "attn_sinks", "w_attn_out", "w_out",
               "ln2_g", "w_ffn_gate", "w_ffn_up", "w_ffn_down", "lnf_g"]
    outs = [sm["loss"], grad_x[None]]
    for k in range(4):
        outs += [upd[n][k] for n in order_w]
    return tuple(outs)
```

```python
from typing import NamedTuple

import jax
import jax.numpy as jnp
from jax import lax
from jax.experimental import pallas as pl
from jax.experimental.pallas import tpu as pltpu

D_MODEL = 1024
SEQ_CHUNK = 128
RET_CHUNK = 256
RET_CHUNKS_PER_STEP = 2
RET_HEADS = 4
RET_QK_DIM = 128
RET_V_DIM = 256
ATTN_Q_HEADS = 16
ATTN_HEAD_DIM = 64
D_FF = 2816
D_IN = 6400
ROPE_THETA = 10000.0
EPS = 1e-6
N_CHIPS = 4

ADAM_LR = 0.001
ADAM_B1 = 0.9
ADAM_B2 = 0.999
ADAM_EPS = 1e-08
ADAM_WD = 0.01
ADAM_STEP = 10

OFF_RQ, OFF_RK, OFF_RV, OFF_RG = 0, 512, 1024, 2048
OFF_AQ, OFF_AK, OFF_AV, OFF_GA, OFF_GB = 3072, 4096, 4224, 4352, 5376

V7X_VMEM_LIMIT = 56 * 1024 * 1024
LANES = 128

BF = jnp.bfloat16
F32 = jnp.float32
MESH = pl.DeviceIdType.MESH

_VMEM = pl.BlockSpec(memory_space=pltpu.VMEM)
_SMEM = pl.BlockSpec(memory_space=pltpu.SMEM)
_ANY = pl.BlockSpec(memory_space=pl.ANY)


def _dot(a, b):
    return jnp.dot(a, b, preferred_element_type=F32)


def _dot_nt(a, b):
    return lax.dot_general(a, b, (((1,), (1,)), ((), ())), preferred_element_type=F32)


def _dot_tn(a, b):
    return lax.dot_general(a, b, (((0,), (0,)), ((), ())), preferred_element_type=F32)


def _sigmoid(x):
    return 0.5 * jnp.tanh(0.5 * x) + 0.5


def _rows(tm, n):
    return pl.BlockSpec((tm, n), lambda i: (i, 0))


def _fixed(shape):
    nd = len(shape)
    return pl.BlockSpec(shape, lambda i: (0,) * nd)


def _params(sem, **kw):
    return pltpu.CompilerParams(dimension_semantics=sem, vmem_limit_bytes=V7X_VMEM_LIMIT, **kw)


def _rot_ret(p, cr, sr):
    return p * cr + pltpu.roll(p, 64, 1) * sr


def _rot_ret_bwd(d, cr, sr):
    return d * cr - pltpu.roll(d, 64, 1) * sr


def _attn_partner(p):
    lane = lax.broadcasted_iota(jnp.int32, p.shape, 1)
    first = (lane % 64) < 32
    return jnp.where(first, pltpu.roll(p, 96, 1), pltpu.roll(p, 32, 1))


def _rot_attn(p, ca, sa):
    return p * ca + _attn_partner(p) * sa


def _rot_attn_bwd(d, ca, sa):
    return d * ca - _attn_partner(d) * sa


def _proj_fwd(x, g1, w_in, b_in, cr, sr, ca, sa):
    S = x.shape[0]
    tm = 512
    scale = RET_QK_DIM ** -0.5

    def body(x_ref, g_ref, w_ref, b_ref, cr_ref, sr_ref, ca_ref, sa_ref,
             h_ref, rq_ref, rk_ref, rv_ref, rg_ref, aq_ref, ak_ref, av_ref, ga_ref, gb_ref):
        xv = x_ref[...]
        r = lax.rsqrt(jnp.mean(xv * xv, axis=-1, keepdims=True) + EPS)
        h = (xv * r * g_ref[...]).astype(BF)
        h_ref[...] = h

        def piece(lo, n):
            return _dot_nt(h, w_ref[lo:lo + n, :]) + b_ref[:, lo:lo + n]

        crv, srv = cr_ref[...], sr_ref[...]
        pq = piece(OFF_RQ, 512)
        pk = piece(OFF_RK, 512)
        for hd in range(RET_HEADS):
            sl = slice(hd * 128, (hd + 1) * 128)
            rq_ref[:, sl] = (_rot_ret(pq[:, sl], crv, srv) * scale).astype(BF)
            rk_ref[:, sl] = _rot_ret(pk[:, sl], crv, srv)
        rv_ref[...] = piece(OFF_RV, 1024).astype(BF)
        rg_ref[...] = piece(OFF_RG, 1024)
        cav, sav = ca_ref[...], sa_ref[...]
        pa = piece(OFF_AQ, 1024)
        for blk in range(8):
            sl = slice(blk * 128, (blk + 1) * 128)
            aq_ref[:, sl] = _rot_attn(pa[:, sl], cav, sav).astype(BF)
        pkv = piece(OFF_AK, 256)
        ak_ref[...] = _rot_attn(pkv[:, :128], cav, sav).astype(BF)
        av_ref[...] = pkv[:, 128:].astype(BF)
        ga_ref[...] = piece(OFF_GA, 1024)
        gb_ref[...] = piece(OFF_GB, 1024)

    outs = [(D_MODEL, BF), (512, BF), (512, F32), (1024, BF), (1024, F32),
            (1024, BF), (128, BF), (128, BF), (1024, F32), (1024, F32)]
    return pl.pallas_call(
        body, name="proj_fwd", grid=(S // tm,),
        in_specs=[_rows(tm, D_MODEL), _fixed((1, D_MODEL)), _VMEM, _VMEM,
                  _rows(tm, 128), _rows(tm, 128), _rows(tm, 128), _rows(tm, 128)],
        out_specs=[_rows(tm, n) for n, _ in outs],
        out_shape=[jax.ShapeDtypeStruct((S, n), dt) for n, dt in outs],
        compiler_params=_params(("parallel",)),
    )(x, g1, w_in, b_in, cr, sr, ca, sa)


def _ret_fwd(rq, rk, rv, dmat, qd, kd, cd):
    S = rq.shape[0]
    N = S // RET_CHUNK
    cps = RET_CHUNKS_PER_STEP
    rows = cps * RET_CHUNK

    def body(q_ref, k_ref, v_ref, dm_ref, qd_ref, kd_ref, cd_ref, y_ref, st_ref, state):
        @pl.when(pl.program_id(0) == 0)
        def _():
            state[...] = jnp.zeros_like(state)

        for hd in range(RET_HEADS):
            qs = slice(hd * 128, (hd + 1) * 128)
            vs = slice(hd * 256, (hd + 1) * 256)
            s = state[hd]
            for ch in range(cps):
                rs = slice(ch * RET_CHUNK, (ch + 1) * RET_CHUNK)
                q = q_ref[rs, qs]
                kf = k_ref[rs, qs]
                k = kf.astype(BF)
                v = v_ref[rs, vs]
                sb = s.astype(BF)
                st_ref[ch, hd] = sb
                a = _dot_nt(q, k) * dm_ref[hd]
                inner = _dot(a.astype(BF), v)
                cross = _dot(q, sb) * qd_ref[hd]
                y_ref[rs, vs] = inner + cross
                kdv = (kf * kd_ref[hd]).astype(BF)
                s = s * cd_ref[hd] + _dot_tn(kdv, v)
            state[hd] = s

    return pl.pallas_call(
        body, name="ret_fwd", grid=(N // cps,),
        in_specs=[_rows(rows, 512), _rows(rows, 512), _rows(rows, 1024), _VMEM, _VMEM, _VMEM, _VMEM],
        out_specs=[_rows(rows, 1024), pl.BlockSpec((cps, RET_HEADS, 128, 256), lambda i: (i, 0, 0, 0))],
        out_shape=[jax.ShapeDtypeStruct((S, 1024), F32),
                   jax.ShapeDtypeStruct((N, RET_HEADS, 128, 256), BF)],
        scratch_shapes=[pltpu.VMEM((RET_HEADS, 128, 256), F32)],
        compiler_params=_params(("arbitrary",)),
    )(rq, rk, rv, dmat, qd, kd, cd)


def _kv_halves(prev_ref, cur_ref):
    kk = jnp.concatenate([prev_ref[...], cur_ref[...]], axis=0).astype(F32)
    return kk, pltpu.roll(kk, 64, 1)


PAIRS = 4


def _attn_fold(n):
    t = lax.broadcasted_iota(jnp.int32, (128, PAIRS * 128), 0)
    qi = lax.broadcasted_iota(jnp.int32, (128, PAIRS * 128), 1) & 127
    cur = t <= qi
    return cur, jnp.logical_or(cur, n > 0)


def _group_operands(g, kk, kk_sw, vv, vv_sw):
    lo = lax.broadcasted_iota(jnp.int32, (256, 128), 1) < 64
    k_a, k_b = (kk, kk_sw) if g == 0 else (kk_sw, kk)
    v_a, v_b = (vv, vv_sw) if g == 0 else (vv_sw, vv)
    return (jnp.where(lo, k_a, 0.0).astype(BF), jnp.where(lo, 0.0, k_b).astype(BF),
            jnp.where(lo, v_a, 0.0).astype(BF), jnp.where(lo, 0.0, v_b).astype(BF))


def _stack_pairs(ref, g):
    return jnp.concatenate([ref[:, (4 * g + p) * 128:(4 * g + p + 1) * 128] for p in range(PAIRS)], axis=0)


def _sink_row(sk_ref, g, which):
    return jnp.concatenate([jnp.full((1, 128), sk_ref[0, 8 * g + 2 * p + which], F32) for p in range(PAIRS)], axis=1)


def _fold(full, cur):
    return jnp.where(cur, full[128:], full[:128])


def _attn_probs(qs, kx, fold, sink):
    cur, valid = fold
    s = _fold(_dot_nt(kx, qs), cur) * (ATTN_HEAD_DIM ** -0.5)
    s = jnp.where(valid, s, -1e30)
    m = jnp.maximum(jnp.max(s, axis=0, keepdims=True), sink)
    e = jnp.exp(s - m)
    es = jnp.exp(sink - m)
    inv = 1.0 / (jnp.sum(e, axis=0, keepdims=True) + es)
    return e * inv, es * inv


def _unfold(p, cur):
    return jnp.concatenate([jnp.where(cur, 0.0, p), jnp.where(cur, p, 0.0)], axis=0).astype(BF)


def _attn_fwd(aq, ak, av, sinks):
    S = aq.shape[0]
    N = S // SEQ_CHUNK

    def body(sk_ref, q_ref, kp_ref, kc_ref, vp_ref, vc_ref, o_ref):
        kk, kk_sw = _kv_halves(kp_ref, kc_ref)
        vv, vv_sw = _kv_halves(vp_ref, vc_ref)
        fold = _attn_fold(pl.program_id(0))
        for g in range(2):
            k_lo, k_hi, v_lo, v_hi = _group_operands(g, kk, kk_sw, vv, vv_sw)
            qs = _stack_pairs(q_ref, g)
            o = jnp.zeros((PAIRS * 128, 128), F32)
            for which, (kx, vx) in enumerate(((k_lo, v_lo), (k_hi, v_hi))):
                pr, _ = _attn_probs(qs, kx, fold, _sink_row(sk_ref, g, which))
                o = o + _dot_tn(_unfold(pr, fold[0]), vx)
            for p in range(PAIRS):
                o_ref[:, (4 * g + p) * 128:(4 * g + p + 1) * 128] = o[p * 128:(p + 1) * 128].astype(BF)

    prev = lambda i: (jnp.maximum(i - 1, 0), 0)
    return pl.pallas_call(
        body, name="attn_fwd", grid=(N,),
        in_specs=[_SMEM, _rows(128, 1024), pl.BlockSpec((128, 128), prev), _rows(128, 128),
                  pl.BlockSpec((128, 128), prev), _rows(128, 128)],
        out_specs=_rows(128, 1024),
        out_shape=jax.ShapeDtypeStruct((S, 1024), BF),
        compiler_params=_params(("parallel",)),
    )(sinks, aq, ak, ak, av, av)


def _group_norm(y):
    mu = jnp.mean(y, axis=-1, keepdims=True)
    yc = y - mu
    rstd = lax.rsqrt(jnp.mean(yc * yc, axis=-1, keepdims=True) + EPS)
    return yc * rstd, rstd


def _merge_fwd(x, ry, rg, ay, ga, gb, g_ret, w_ret, w_attn, w_out):
    S = x.shape[0]
    tm = 256

    def body(x_ref, ry_ref, rg_ref, ay_ref, ga_ref, gb_ref, gr_ref, wr_ref, wa_ref, wo_ref,
             u_ref, ba_ref, bb_ref, mg_ref, x1_ref):
        for hd in range(RET_HEADS):
            vs = slice(hd * 256, (hd + 1) * 256)
            yn, _ = _group_norm(ry_ref[:, vs])
            rgv = rg_ref[:, vs]
            u_ref[:, vs] = (rgv * _sigmoid(rgv) * (yn * gr_ref[:, vs])).astype(BF)
        ba = _dot(u_ref[...], wr_ref[...])
        bb = _dot(ay_ref[...], wa_ref[...])
        ba_ref[...] = ba
        bb_ref[...] = bb
        merged = (_sigmoid(ga_ref[...]) * ba + _sigmoid(gb_ref[...]) * bb).astype(BF)
        mg_ref[...] = merged
        x1_ref[...] = x_ref[...] + _dot(merged, wo_ref[...])

    outs = [BF, F32, F32, BF, F32]
    return pl.pallas_call(
        body, name="merge_fwd", grid=(S // tm,),
        in_specs=[_rows(tm, 1024)] * 6 + [_fixed((1, 1024)), _VMEM, _VMEM, _VMEM],
        out_specs=[_rows(tm, 1024)] * 5,
        out_shape=[jax.ShapeDtypeStruct((S, 1024), dt) for dt in outs],
        compiler_params=_params(("parallel",)),
    )(x, ry, rg, ay, ga, gb, g_ret, w_ret, w_attn, w_out)


def _ffn_fwd(x1, tgt, g2, g3, w_gate, w_up, w_down):
    S = x1.shape[0]
    tm = 256

    def body(x1_ref, t_ref, g2_ref, g3_ref, wg_ref, wu_ref, wd_ref,
             gate_ref, up_ref, h2_ref, act_ref, dx2_ref, loss_ref, dg3_ref):
        @pl.when(pl.program_id(0) == 0)
        def _():
            loss_ref[...] = jnp.zeros_like(loss_ref)
            dg3_ref[...] = jnp.zeros_like(dg3_ref)

        x1v = x1_ref[...]
        r2 = lax.rsqrt(jnp.mean(x1v * x1v, axis=-1, keepdims=True) + EPS)
        h2 = (x1v * r2 * g2_ref[...]).astype(BF)
        h2_ref[...] = h2
        gate = _dot_nt(h2, wg_ref[...])
        up = _dot_nt(h2, wu_ref[...])
        gate_ref[...] = gate
        up_ref[...] = up
        act = (gate * _sigmoid(gate) * up).astype(BF)
        act_ref[...] = act
        x2 = x1v + _dot(act, wd_ref[...])
        r3 = lax.rsqrt(jnp.mean(x2 * x2, axis=-1, keepdims=True) + EPS)
        xh = x2 * r3
        g3v = g3_ref[...]
        err = xh * g3v - t_ref[...]
        loss_ref[...] += 0.5 * jnp.sum(jnp.mean(err * err, axis=-1, keepdims=True), axis=0, keepdims=True)
        dy = err * (1.0 / D_MODEL)
        dg3_ref[...] += jnp.sum(dy * xh, axis=0, keepdims=True)
        dxh = dy * g3v
        dx2_ref[...] = r3 * (dxh - xh * jnp.mean(dxh * xh, axis=-1, keepdims=True))

    return pl.pallas_call(
        body, name="ffn_fwd", grid=(S // tm,),
        in_specs=[_rows(tm, 1024), _rows(tm, 1024), _fixed((1, 1024)), _fixed((1, 1024)), _VMEM, _VMEM, _VMEM],
        out_specs=[_rows(tm, D_FF), _rows(tm, D_FF), _rows(tm, 1024), _rows(tm, D_FF), _rows(tm, 1024),
                   _fixed((1, LANES)), _fixed((1, 1024))],
        out_shape=[jax.ShapeDtypeStruct((S, D_FF), F32), jax.ShapeDtypeStruct((S, D_FF), F32),
                   jax.ShapeDtypeStruct((S, 1024), BF), jax.ShapeDtypeStruct((S, D_FF), BF),
                   jax.ShapeDtypeStruct((S, 1024), F32),
                   jax.ShapeDtypeStruct((1, LANES), F32), jax.ShapeDtypeStruct((1, 1024), F32)],
        compiler_params=_params(("arbitrary",)),
    )(x1, tgt, g2, g3, w_gate, w_up, w_down)


def _ffn_bwd(dx2, gate, up, x1, g2, w_gate, w_up, w_down):
    S = x1.shape[0]
    tm = 256

    def body(dx2_ref, gate_ref, up_ref, x1_ref, g2_ref, wg_ref, wu_ref, wd_ref,
             dgate_ref, dup_ref, dx1_ref, dg2_ref):
        @pl.when(pl.program_id(0) == 0)
        def _():
            dg2_ref[...] = jnp.zeros_like(dg2_ref)

        dx2v = dx2_ref[...]
        dact = _dot_nt(dx2v.astype(BF), wd_ref[...])
        gate = gate_ref[...]
        sg = _sigmoid(gate)
        dgate = (dact * up_ref[...] * (sg * (1.0 + gate * (1.0 - sg)))).astype(BF)
        dup = (dact * (gate * sg)).astype(BF)
        dgate_ref[...] = dgate
        dup_ref[...] = dup
        dh2 = _dot(dgate, wg_ref[...]) + _dot(dup, wu_ref[...])
        x1v = x1_ref[...]
        r2 = lax.rsqrt(jnp.mean(x1v * x1v, axis=-1, keepdims=True) + EPS)
        xh = x1v * r2
        dg2_ref[...] += jnp.sum(dh2 * xh, axis=0, keepdims=True)
        dxh = dh2 * g2_ref[...]
        dx1_ref[...] = dx2v + r2 * (dxh - xh * jnp.mean(dxh * xh, axis=-1, keepdims=True))

    return pl.pallas_call(
        body, name="ffn_bwd", grid=(S // tm,),
        in_specs=[_rows(tm, 1024), _rows(tm, D_FF), _rows(tm, D_FF), _rows(tm, 1024), _fixed((1, 1024)),
                  _VMEM, _VMEM, _VMEM],
        out_specs=[_rows(tm, D_FF), _rows(tm, D_FF), _rows(tm, 1024), _fixed((1, 1024))],
        out_shape=[jax.ShapeDtypeStruct((S, D_FF), BF), jax.ShapeDtypeStruct((S, D_FF), BF),
                   jax.ShapeDtypeStruct((S, 1024), F32), jax.ShapeDtypeStruct((1, 1024), F32)],
        compiler_params=_params(("arbitrary",)),
    )(dx2, gate, up, x1, g2, w_gate, w_up, w_down)


def _merge_bwd(dx1, ga, gb, ba, bb, rg, ry, g_ret, w_ret, w_attn, w_out):
    S = dx1.shape[0]
    tm = 256

    def body(dx1_ref, ga_ref, gb_ref, ba_ref, bb_ref, rg_ref, ry_ref, gr_ref, wr_ref, wa_ref, wo_ref,
             dgates_ref, drg_ref, dba_ref, dbb_ref, day_ref, dry_ref, dgr_ref, db_ref):
        @pl.when(pl.program_id(0) == 0)
        def _():
            dgr_ref[...] = jnp.zeros_like(dgr_ref)
            db_ref[...] = jnp.zeros_like(db_ref)

        dm = _dot_nt(dx1_ref[...].astype(BF), wo_ref[...])
        sa = _sigmoid(ga_ref[...])
        sb = _sigmoid(gb_ref[...])
        dga = (dm * ba_ref[...] * (sa * (1.0 - sa))).astype(BF)
        dgb = (dm * bb_ref[...] * (sb * (1.0 - sb))).astype(BF)
        dgates_ref[:, :1024] = dga
        dgates_ref[:, 1024:] = dgb
        dba = (dm * sa).astype(BF)
        dbb = (dm * sb).astype(BF)
        dba_ref[...] = dba
        dbb_ref[...] = dbb
        day_ref[...] = _dot_nt(dbb, wa_ref[...]).astype(BF)
        du = _dot_nt(dba, wr_ref[...])
        for hd in range(RET_HEADS):
            vs = slice(hd * 256, (hd + 1) * 256)
            yn, rstd = _group_norm(ry_ref[:, vs])
            rgv = rg_ref[:, vs]
            sr = _sigmoid(rgv)
            silu = rgv * sr
            duv = du[:, vs]
            grv = gr_ref[:, vs]
            drg = (duv * (yn * grv) * (sr * (1.0 + rgv * (1.0 - sr)))).astype(BF)
            drg_ref[:, vs] = drg
            dyg = duv * silu
            dgr_ref[:, vs] += jnp.sum(dyg * yn, axis=0, keepdims=True)
            dyn = dyg * grv
            dry_ref[:, vs] = rstd * (dyn - jnp.mean(dyn, axis=-1, keepdims=True)
                                     - yn * jnp.mean(dyn * yn, axis=-1, keepdims=True))
            db_ref[:, vs] += jnp.sum(drg.astype(F32), axis=0, keepdims=True)
        db_ref[:, 1024:2048] += jnp.sum(dga.astype(F32), axis=0, keepdims=True)
        db_ref[:, 2048:] += jnp.sum(dgb.astype(F32), axis=0, keepdims=True)

    return pl.pallas_call(
        body, name="merge_bwd", grid=(S // tm,),
        in_specs=[_rows(tm, 1024)] * 7 + [_fixed((1, 1024)), _VMEM, _VMEM, _VMEM],
        out_specs=[_rows(tm, 2048), _rows(tm, 1024), _rows(tm, 1024), _rows(tm, 1024), _rows(tm, 1024),
                   _rows(tm, 1024), _fixed((1, 1024)), _fixed((1, 3072))],
        out_shape=[jax.ShapeDtypeStruct((S, 2048), BF), jax.ShapeDtypeStruct((S, 1024), BF),
                   jax.ShapeDtypeStruct((S, 1024), BF), jax.ShapeDtypeStruct((S, 1024), BF),
                   jax.ShapeDtypeStruct((S, 1024), BF), jax.ShapeDtypeStruct((S, 1024), F32),
                   jax.ShapeDtypeStruct((1, 1024), F32), jax.ShapeDtypeStruct((1, 3072), F32)],
        compiler_params=_params(("arbitrary",)),
    )(dx1, ga, gb, ba, bb, rg, ry, g_ret, w_ret, w_attn, w_out)


def _attn_bwd(aq, ak, av, day, sinks, ca, sa, after):
    S = aq.shape[0]
    N = S // SEQ_CHUNK

    def body(sk_ref, q_ref, kp_ref, kc_ref, vp_ref, vc_ref, do_ref, cq_ref, sq_ref, cp_ref, sp_ref, after_ref,
             dq_ref, dk_ref, dv_ref, dsk_ref, carry_k, carry_v):
        n = pl.program_id(0)

        @pl.when(n == 0)
        def _():
            carry_k[...] = jnp.zeros_like(carry_k)
            carry_v[...] = jnp.zeros_like(carry_v)
            dsk_ref[...] = jnp.zeros_like(dsk_ref)

        @pl.when(n < N)
        def _():
            kk, kk_sw = _kv_halves(kp_ref, kc_ref)
            vv, vv_sw = _kv_halves(vp_ref, vc_ref)
            lo = lax.broadcasted_iota(jnp.int32, (256, 128), 1) < 64
            fold = _attn_fold(n)
            cur = fold[0]
            cq, sq = cq_ref[...], sq_ref[...]
            dk_groups, dv_groups = [], []
            for g in range(2):
                k_lo, k_hi, v_lo, v_hi = _group_operands(g, kk, kk_sw, vv, vv_sw)
                qs = _stack_pairs(q_ref, g)
                dos = _stack_pairs(do_ref, g)
                dq = jnp.zeros((PAIRS * 128, 128), F32)
                dk_half, dv_half = [], []
                for which, (kx, vx) in enumerate(((k_lo, v_lo), (k_hi, v_hi))):
                    pr, ps = _attn_probs(qs, kx, fold, _sink_row(sk_ref, g, which))
                    dp = _fold(_dot_nt(vx, dos), cur)
                    dr = jnp.sum(pr * dp, axis=0, keepdims=True)
                    ds = _unfold(pr * (dp - dr) * (ATTN_HEAD_DIM ** -0.5), cur)
                    dsk_ref[2 * g + which:2 * g + which + 1, :] -= ps * dr
                    dq = dq + _dot_tn(ds, kx)
                    dk_half.append(_dot(ds, qs))
                    dv_half.append(_dot(_unfold(pr, cur), dos))
                for p in range(PAIRS):
                    sl = slice((4 * g + p) * 128, (4 * g + p + 1) * 128)
                    dq_ref[:, sl] = _rot_attn_bwd(dq[p * 128:(p + 1) * 128], cq, sq).astype(BF)
                tk = jnp.where(lo, dk_half[0], dk_half[1])
                tv = jnp.where(lo, dv_half[0], dv_half[1])
                dk_groups.append(tk + pltpu.roll(tk, 64, 1))
                dv_groups.append(tv + pltpu.roll(tv, 64, 1))
            dk_full = jnp.where(lo, dk_groups[0], dk_groups[1])
            dv_full = jnp.where(lo, dv_groups[0], dv_groups[1])
            dk_ref[...] = _rot_attn_bwd(carry_k[...] + dk_full[:128], cp_ref[...], sp_ref[...]).astype(BF)
            dv_ref[...] = (carry_v[...] + dv_full[:128]).astype(BF)
            carry_k[...] = dk_full[128:]
            carry_v[...] = dv_full[128:]

        @pl.when(n == N)
        def _():
            dk_ref[...] = _rot_attn_bwd(carry_k[...], cp_ref[...], sp_ref[...]).astype(BF)
            dv_ref[...] = carry_v[...].astype(BF)

    cur = lambda i: (jnp.minimum(i, N - 1), 0)
    prev = lambda i: (jnp.maximum(i - 1, 0), 0)
    blk = lambda w, im: pl.BlockSpec((128, w), im)
    return pl.pallas_call(
        body, name="attn_bwd", grid=(N + 1,),
        in_specs=[_SMEM, blk(1024, cur), blk(128, prev), blk(128, cur), blk(128, prev), blk(128, cur),
                  blk(1024, cur), blk(128, cur), blk(128, cur), blk(128, prev), blk(128, prev), _ANY],
        out_specs=[blk(1024, cur), blk(128, prev), blk(128, prev), _fixed((4, PAIRS * 128))],
        out_shape=[jax.ShapeDtypeStruct((S, 1024), BF), jax.ShapeDtypeStruct((S, 128), BF),
                   jax.ShapeDtypeStruct((S, 128), BF), jax.ShapeDtypeStruct((4, PAIRS * 128), F32)],
        scratch_shapes=[pltpu.VMEM((128, 128), F32), pltpu.VMEM((128, 128), F32)],
        compiler_params=_params(("arbitrary",)),
    )(sinks, aq, ak, ak, av, av, day, ca, sa, ca, sa, after)


def _ret_bwd(rq, rk, rv, dry, states, dmat, qd, kd, cd, cr, sr, after):
    S = rq.shape[0]
    N = S // RET_CHUNK
    scale = RET_QK_DIM ** -0.5
    cps = RET_CHUNKS_PER_STEP
    steps = N // cps

    def body(q_ref, k_ref, v_ref, dy_ref, st_ref, dm_ref, qd_ref, kd_ref, cd_ref, cr_ref, sr_ref, after_ref,
             out_ref, dstate):
        @pl.when(pl.program_id(0) == 0)
        def _():
            dstate[...] = jnp.zeros_like(dstate)

        for hd in range(RET_HEADS):
            qs = slice(hd * 128, (hd + 1) * 128)
            vs = slice(hd * 256, (hd + 1) * 256)
            ds = dstate[hd]
            dmv = dm_ref[hd]
            for ch in reversed(range(cps)):
                rs = slice(ch * RET_CHUNK, (ch + 1) * RET_CHUNK)
                crv, srv = cr_ref[rs, :], sr_ref[rs, :]
                q = q_ref[rs, qs]
                kf = k_ref[rs, qs]
                k = kf.astype(BF)
                kdv = (kf * kd_ref[hd]).astype(BF)
                v = v_ref[rs, vs]
                do = dy_ref[rs, vs]
                dob = do.astype(BF)
                doq = (do * qd_ref[hd]).astype(BF)
                sn = st_ref[ch, hd]
                dsb = ds.astype(BF)
                a = (_dot_nt(q, k) * dmv).astype(BF)
                dv = _dot_tn(a, dob) + _dot(kdv, dsb)
                da = (_dot_nt(dob, v) * dmv).astype(BF)
                dq = _dot(da, k) + _dot_nt(doq, sn)
                dk = _dot_tn(da, q) + _dot_nt(v, dsb) * kd_ref[hd]
                ds = ds * cd_ref[hd] + _dot_tn(q, doq)
                out_ref[rs, qs] = _rot_ret_bwd(dq * scale, crv, srv).astype(BF)
                out_ref[rs, 512 + hd * 128:512 + (hd + 1) * 128] = _rot_ret_bwd(dk, crv, srv).astype(BF)
                out_ref[rs, 1024 + hd * 256:1024 + (hd + 1) * 256] = dv.astype(BF)
            dstate[hd] = ds

    rev = lambda w: pl.BlockSpec((cps * RET_CHUNK, w), lambda i: (steps - 1 - i, 0))
    return pl.pallas_call(
        body, name="ret_bwd", grid=(steps,),
        in_specs=[rev(512), rev(512), rev(1024), rev(1024),
                  pl.BlockSpec((cps, RET_HEADS, 128, 256), lambda i: (steps - 1 - i, 0, 0, 0)),
                  _VMEM, _VMEM, _VMEM, _VMEM, rev(128), rev(128), _ANY],
        out_specs=rev(2048),
        out_shape=jax.ShapeDtypeStruct((S, 2048), BF),
        scratch_shapes=[pltpu.VMEM((RET_HEADS, 128, 256), F32)],
        compiler_params=_params(("arbitrary",)),
    )(rq, rk, rv, dry, states, dmat, qd, kd, cd, cr, sr, after)


def _proj_bwd(d_pieces, x, dx1, g1, w_in_t):
    S = x.shape[0]
    tm = 512
    widths = [d.shape[1] for d in d_pieces]
    assert sum(widths) == D_IN
    nd = len(d_pieces)

    def body(*refs):
        d_refs = refs[:nd]
        x_ref, dx1_ref, g_ref, w_ref, gx_ref, dg_ref, db_ref = refs[nd:]

        @pl.when(pl.program_id(0) == 0)
        def _():
            dg_ref[...] = jnp.zeros_like(dg_ref)
            db_ref[...] = jnp.zeros_like(db_ref)

        dh = jnp.zeros((tm, D_MODEL), F32)
        lo = 0
        for ref, n in zip(d_refs, widths):
            v = ref[...]
            dh = dh + _dot(v, w_ref[lo:lo + n, :])
            db_ref[:, lo:lo + n] += jnp.sum(v.astype(F32), axis=0, keepdims=True)
            lo += n
        xv = x_ref[...]
        r = lax.rsqrt(jnp.mean(xv * xv, axis=-1, keepdims=True) + EPS)
        xh = xv * r
        dg_ref[...] += jnp.sum(dh * xh, axis=0, keepdims=True)
        dxh = dh * g_ref[...]
        gx_ref[...] = dx1_ref[...] + r * (dxh - xh * jnp.mean(dxh * xh, axis=-1, keepdims=True))

    return pl.pallas_call(
        body, name="proj_bwd", grid=(S // tm,),
        in_specs=[_rows(tm, n) for n in widths] + [_rows(tm, 1024), _rows(tm, 1024), _fixed((1, 1024)), _VMEM],
        out_specs=[_rows(tm, 1024), _fixed((1, 1024)), _fixed((1, D_IN))],
        out_shape=[jax.ShapeDtypeStruct((S, 1024), F32), jax.ShapeDtypeStruct((1, 1024), F32),
                   jax.ShapeDtypeStruct((1, D_IN), F32)],
        compiler_params=_params(("arbitrary",)),
    )(*d_pieces, x, dx1, g1, w_in_t)


SHARD_ROWS = (1600, 256, 256, 256, 704, 704, 704)


class Group(NamedTuple):
    name: str
    tensors: tuple
    offs: tuple
    block: int
    tile: int

    @property
    def half(self):
        return self.block // 2

    def off(self, t):
        return self.offs[self.tensors.index(t)]


LATE = Group("late", (0,), (0,), 1600, 800)
EARLY = Group("early", (1, 2, 3, 4, 5, 6), (0, 256, 512, 768, 1472, 2176), 2880, 480)
GROUP_OF = {t: g for g in (LATE, EARLY) for t in g.tensors}


def _wgrad(a_list, b, t, packed, name, row0=0):
    S = b.shape[0]
    widths = [a.shape[1] for a in a_list]
    M = sum(widths)
    ts = 1024
    steps = S // ts
    na = len(a_list)
    grp = GROUP_OF[t]
    r, off = SHARD_ROWS[t], grp.off(t)
    segments = []
    for chip in range(N_CHIPS):
        lo, hi = max(chip * r, row0), min((chip + 1) * r, row0 + M)
        if lo < hi:
            segments.append((lo - row0, hi - lo, chip * grp.block + off + lo - chip * r))
    assert sum(n for _, n, _ in segments) == M
    stage_rows = max(n for _, n, _ in segments)
    n_in = na + 1 + (2 if packed is not None else 0)

    def body(*refs):
        a_refs, b_ref = refs[:na], refs[na]
        g32_ref, gb_ref, acc, stage, sems = refs[n_in:]
        i = pl.program_id(0)

        @pl.when(i == 0)
        def _():
            acc[...] = jnp.zeros_like(acc)

        bv = b_ref[...].astype(BF)
        lo = 0
        for a_ref, n in zip(a_refs, widths):
            acc[lo:lo + n, :] += _dot_tn(a_ref[...].astype(BF), bv)
            lo += n

        @pl.when(i == steps - 1)
        def _():
            for a0, n, p0 in segments:
                c32 = pltpu.make_async_copy(acc.at[a0:a0 + n, :], g32_ref.at[pl.ds(p0, n), :], sems.at[0])
                c32.start()
                stage[0:n, :] = acc[a0:a0 + n, :].astype(BF)
                c16 = pltpu.make_async_copy(stage.at[0:n, :], gb_ref.at[pl.ds(p0, n), :], sems.at[1])
                c16.start()
                c32.wait()
                c16.wait()

    shapes = [jax.ShapeDtypeStruct((N_CHIPS * grp.block, 1024), F32),
              jax.ShapeDtypeStruct((N_CHIPS * grp.block, 1024), BF)]
    operands = list(a_list) + [b]
    in_specs = [_rows(ts, n) for n in widths] + [_rows(ts, 1024)]
    aliases = {}
    if packed is not None:
        operands += list(packed)
        in_specs += [_ANY, _ANY]
        aliases = {na + 1: 0, na + 2: 1}
    return pl.pallas_call(
        body, name=name, grid=(steps,),
        in_specs=in_specs, out_specs=[_ANY, _ANY], out_shape=shapes,
        scratch_shapes=[pltpu.VMEM((M, 1024), F32), pltpu.VMEM((stage_rows, 1024), BF),
                        pltpu.SemaphoreType.DMA((2,))],
        input_output_aliases=aliases,
        compiler_params=_params(("arbitrary",)),
    )(*operands)


def _stage_weight(place, w, name, after=()):
    r = w.shape[0]
    tr = 800 if r == 1600 else r
    nb = r // tr

    def body(p_ref, w_ref, *rest):
        rest[-1][...] = w_ref[...].astype(BF)

    return pl.pallas_call(
        body, name=name,
        grid_spec=pltpu.PrefetchScalarGridSpec(
            num_scalar_prefetch=1, grid=(nb,),
            in_specs=[pl.BlockSpec((tr, 1024), lambda i, p: (i, 0))] + [_ANY] * len(after),
            out_specs=pl.BlockSpec((tr, 1024), lambda i, p: (p[0] * nb + i, 0))),
        out_shape=jax.ShapeDtypeStruct((N_CHIPS * r, 1024), BF),
        compiler_params=_params(("arbitrary",)),
    )(place, w, *after)


def _kept_block(grp, nb):
    return lambda i, p: ((2 * (i // nb) + p[1]) * nb + i % nb, 0)


def _pair_sum(place, g32, got, grp):
    nb = grp.half // grp.tile

    def body(p_ref, g_ref, r_ref, s16_ref):
        s16_ref[...] = (g_ref[...] + r_ref[...].astype(F32)).astype(BF)

    flat = pl.BlockSpec((grp.tile, 1024), lambda i, p: (i, 0))
    return pl.pallas_call(
        body, name="pair_sum_" + grp.name,
        grid_spec=pltpu.PrefetchScalarGridSpec(
            num_scalar_prefetch=1, grid=(N_CHIPS * nb,),
            in_specs=[pl.BlockSpec((grp.tile, 1024), _kept_block(grp, nb)), flat], out_specs=flat),
        out_shape=jax.ShapeDtypeStruct((N_CHIPS * grp.half, 1024), BF),
        compiler_params=_params(("arbitrary",)),
    )(place, g32, got)


def _chip_sum(place, g32, got, others, grp):
    nb = grp.half // grp.tile

    def body(p_ref, g_ref, r_ref, r0_ref, r1_ref, r2_ref, o_ref):
        own = g_ref[...] + r_ref[...].astype(F32)
        o_ref[...] = ((own + r0_ref[...].astype(F32)) + r1_ref[...].astype(F32)) + r2_ref[...].astype(F32)

    blk = lambda f: pl.BlockSpec((grp.tile, 1024), f)
    return pl.pallas_call(
        body, name="chip_sum_" + grp.name,
        grid_spec=pltpu.PrefetchScalarGridSpec(
            num_scalar_prefetch=1, grid=(nb,),
            in_specs=[blk(lambda i, p: ((2 * p[0] + p[1]) * nb + i, 0)), blk(lambda i, p: (p[0] * nb + i, 0)),
                      blk(lambda i, p: (i, 0)), blk(lambda i, p: (nb + i, 0)), blk(lambda i, p: (2 * nb + i, 0))],
            out_specs=blk(lambda i, p: (p[1] * nb + i, 0))),
        out_shape=jax.ShapeDtypeStruct((grp.block, 1024), F32),
        compiler_params=_params(("arbitrary",)),
    )(place, g32, got, others, others, others)


def _adamw(w, g, m, v, name, g_row0=0, after=None):
    r = w.shape[0]
    tr = 320 if r % 320 == 0 and g_row0 % 320 == 0 else (64 if r % 64 == 0 else r)
    assert g_row0 % tr == 0
    g0 = g_row0 // tr
    extra = [] if after is None else [after]

    def body(w_ref, g_ref, m_ref, v_ref, *rest):
        go_ref, d_ref, mo_ref, vo_ref = rest[len(extra):]
        gv = g_ref[...]
        mn = ADAM_B1 * m_ref[...] + (1.0 - ADAM_B1) * gv
        vn = ADAM_B2 * v_ref[...] + (1.0 - ADAM_B2) * (gv * gv)
        m_hat = mn / (1.0 - ADAM_B1 ** ADAM_STEP)
        v_hat = vn / (1.0 - ADAM_B2 ** ADAM_STEP)
        go_ref[...] = gv
        d_ref[...] = -ADAM_LR * (m_hat / (jnp.sqrt(v_hat) + ADAM_EPS) + ADAM_WD * w_ref[...])
        mo_ref[...] = mn
        vo_ref[...] = vn

    return pl.pallas_call(
        body, name=name, grid=(r // tr,),
        in_specs=[_rows(tr, 1024), pl.BlockSpec((tr, 1024), lambda i: (g0 + i, 0)), _rows(tr, 1024), _rows(tr, 1024)]
        + [_ANY] * len(extra),
        out_specs=[_rows(tr, 1024)] * 4,
        out_shape=[jax.ShapeDtypeStruct((r, 1024), F32)] * 4,
        compiler_params=_params(("parallel",)),
    )(w, g, m, v, *extra)


def _place():
    x, y, c = lax.axis_index("x"), lax.axis_index("y"), lax.axis_index("c")
    chips = [(1 - x, y), (x, 1 - y), (1 - x, 1 - y)]
    return x, y, c, chips


_HBM =pl.BlockSpec(memory_space=pltpu.HBM)
_SEM = pl.BlockSpec(memory_space=pltpu.SEMAPHORE)
_DATAFLOW = pltpu.SideEffectType.DATAFLOW_SIDE_EFFECTING


def _exchange(name, arrays, n, plan):
    na = len(arrays)

    def body(*refs):
        pairs = plan(refs[na:2 * na], refs[2 * na], refs[2 * na + 1])
        for send, _ in pairs:
            send.start()
        for send, recv in pairs:
            send.wait_send()
            recv.wait_recv()

    return pl.pallas_call(
        body, name=name, in_specs=[_ANY] * na, out_specs=[_ANY] * na,
        out_shape=[jax.ShapeDtypeStruct(a.shape, a.dtype) for a in arrays],
        scratch_shapes=[pltpu.SemaphoreType.DMA((n,)), pltpu.SemaphoreType.DMA((n,))],
        input_output_aliases={i: i for i in range(na)},
    )(*arrays)


def _exchange_start(name, arrays, n, plan, after=()):
    na = len(arrays)

    def body(*refs):
        for send, _ in plan(refs[:na], refs[na + len(after)], refs[na + len(after) + 1]):
            send.start()
        refs[-1][...] = jnp.zeros_like(refs[-1])

    out = pl.pallas_call(
        body, name=name,
        out_shape=(pltpu.SemaphoreType.DMA((n,)), pltpu.SemaphoreType.DMA((n,)),
                   *[pltpu.HBM(a.shape, a.dtype) for a in arrays], jax.ShapeDtypeStruct((8, LANES), F32)),
        in_specs=[_HBM] * na + [_ANY] * len(after), out_specs=(_SEM, _SEM, *[_HBM] * na, _VMEM),
        input_output_aliases={i: 2 + i for i in range(na)},
        compiler_params=pltpu.CompilerParams(has_side_effects=_DATAFLOW),
    )(*[pltpu.with_memory_space_constraint(a, pltpu.HBM) for a in arrays], *after)
    return (out[0], out[1]), list(out[2:2 + na]), out[-1]


def _exchange_wait(name, sems, in_flight, plan, after=()):
    na = len(in_flight)

    def body(*refs):
        for send, recv in plan(refs[:na], refs[na], refs[na + 1]):
            send.wait_send()
            recv.wait_recv()

    out = pl.pallas_call(
        body, name=name,
        out_shape=tuple(pltpu.HBM(a.shape, a.dtype) for a in in_flight),
        in_specs=[_HBM] * na + [_SEM, _SEM] + [_ANY] * len(after), out_specs=tuple([_HBM] * na),
        input_output_aliases={i: i for i in range(na)},
        compiler_params=pltpu.CompilerParams(has_side_effects=_DATAFLOW),
    )(*in_flight, *sems, *after)
    return list(out)


def _remote(src, dst, send_sems, recv_sems, k, to):
    return pltpu.make_async_remote_copy(src_ref=src, dst_ref=dst, send_sem=send_sems.at[k], recv_sem=recv_sems.at[k],
                                        device_id=to, device_id_type=MESH)


def _plan_gather_direct(tids):
    def plan(refs, ss, rs):
        x, y, c, chips = _place()
        j = 2 * x + y
        pairs = []
        for t, tid in enumerate(tids):
            r = SHARD_ROWS[tid]
            for k, (cx, cy) in enumerate(chips):
                mine = refs[t].at[pl.ds(j * r, r), :]
                theirs = refs[t].at[pl.ds((2 * cx + cy) * r, r), :]
                pairs.append((_remote(mine, mine, ss, rs, 3 * t + k, (cx, cy, c)),
                              _remote(theirs, theirs, ss, rs, 3 * t + k, (x, y, c))))
        return pairs
    return plan


def _half_rows(ref, tid, pj, half):
    h = SHARD_ROWS[tid] // 2
    return ref.at[pl.ds((2 * pj + half) * h, h), :]


def _plan_gather_halves(tid):
    def plan(refs, ss, rs):
        x, y, c, chips = _place()
        mine = _half_rows(refs[0], tid, 2 * x + y, c)
        return [(_remote(mine, mine, ss, rs, k, (cx, cy, c)),
                 _remote(_half_rows(refs[0], tid, 2 * cx + cy, c), _half_rows(refs[0], tid, 2 * cx + cy, c),
                         ss, rs, k, (x, y, c))) for k, (cx, cy) in enumerate(chips)]
    return plan


def _plan_forward_halves(tid):
    def plan(refs, ss, rs):
        x, y, c, chips = _place()
        pairs = []
        for k, (cx, cy) in enumerate(chips):
            got = _half_rows(refs[0], tid, 2 * cx + cy, c)
            missing = _half_rows(refs[0], tid, 2 * cx + cy, 1 - c)
            pairs.append((_remote(got, got, ss, rs, k, (x, y, 1 - c)), _remote(missing, missing, ss, rs, k, (x, y, 1 - c))))
        return pairs
    return plan


def _plan_pair_exchange(grp):
    def plan(refs, ss, rs):
        x, y, c, _ = _place()
        cps = [_remote(refs[0].at[pl.ds(pj * grp.block + (1 - c) * grp.half, grp.half), :],
                       refs[1].at[pl.ds(pj * grp.half, grp.half), :], ss, rs, pj, (x, y, 1 - c))
               for pj in range(N_CHIPS)]
        return [(cp, cp) for cp in cps]
    return plan


def _plan_to_chips(grp):
    def plan(refs, ss, rs):
        x, y, c, chips = _place()
        cps = [_remote(refs[0].at[pl.ds((2 * cx + cy) * grp.half, grp.half), :],
                       refs[1].at[pl.ds(k * grp.half, grp.half), :], ss, rs, k, (cx, cy, c))
               for k, (cx, cy) in enumerate(chips)]
        return [(cp, cp) for cp in cps]
    return plan


def _plan_share_halves(grp):
    def plan(refs, ss, rs):
        x, y, c, _ = _place()
        mine = refs[0].at[pl.ds(c * grp.half, grp.half), :]
        other = refs[0].at[pl.ds((1 - c) * grp.half, grp.half), :]
        return [(_remote(mine, mine, ss, rs, 0, (x, y, 1 - c)), _remote(other, other, ss, rs, 0, (x, y, 1 - c)))]
    return plan


def _empty(rows, dtype):
    return lax.empty((rows, 1024), dtype)


def _all_reduce_small(v, after):
    m_per, n = v.shape
    n_dev = 8

    def body(x_ref, after_ref, out_ref, all_ref, send_sems, recv_sems, local_sem):
        x, y, c, chips = _place()
        me, sibling = (x, y, c), (x, y, 1 - c)

        def rows(px, py, pc):
            return all_ref.at[pl.ds((4 * px + 2 * py + pc) * m_per, m_per), :]

        def copy(k, block, to, src=None):
            return pltpu.make_async_remote_copy(
                src_ref=rows(*block) if src is None else src, dst_ref=rows(*block),
                send_sem=send_sems.at[k], recv_sem=recv_sems.at[k], device_id=to, device_id_type=MESH)

        mine = pltpu.make_async_copy(x_ref, rows(*me), local_sem)
        mine.start()
        first = [copy(0, me, sibling, src=x_ref)]
        first += [copy(1 + k, me, (*chip, c), src=x_ref) for k, chip in enumerate(chips)]
        for cp in first:
            cp.start()
        passed = [copy(4 + k, (*chip, c), sibling) for k, chip in enumerate(chips)]
        for k, chip in enumerate(chips):
            copy(1 + k, (*chip, c), me).wait_recv()
            passed[k].start()
        copy(0, sibling, me).wait_recv()
        for k, chip in enumerate(chips):
            copy(4 + k, (*chip, 1 - c), me).wait_recv()
        for cp in first + passed:
            cp.wait_send()
        mine.wait()
        acc = all_ref[pl.ds(0, m_per), :]
        for d in range(1, n_dev):
            acc = acc + all_ref[pl.ds(d * m_per, m_per), :]
        out_ref[...] = acc

    return pl.pallas_call(
        body, name="all_reduce_small",
        in_specs=[_VMEM, _ANY], out_specs=_VMEM,
        out_shape=jax.ShapeDtypeStruct((m_per, n), F32),
        scratch_shapes=[pltpu.VMEM((n_dev * m_per, n), F32),
                        pltpu.SemaphoreType.DMA((7,)), pltpu.SemaphoreType.DMA((7,)), pltpu.SemaphoreType.DMA],
        compiler_params=pltpu.CompilerParams(has_side_effects=True),
    )(v, after)


def _tables(S, token):
    pos = jnp.arange(S, dtype=jnp.int32).astype(F32) + token[0, 0]

    def cs(half):
        inv_freq = ROPE_THETA ** (-jnp.arange(half, dtype=F32) / half)
        ang = pos[:, None] * inv_freq[None, :]
        return jnp.cos(ang), jnp.sin(ang)

    c64, s64 = cs(64)
    c32, s32 = cs(32)
    cr = jnp.concatenate([c64, c64], axis=1)
    sr = jnp.concatenate([-s64, s64], axis=1)
    ca = jnp.concatenate([c32, c32, c32, c32], axis=1)
    sa = jnp.concatenate([-s32, s32, -s32, s32], axis=1)
    return cr, sr, ca, sa


def _decays():
    H, C = RET_HEADS, RET_CHUNK
    log_gamma = jnp.log1p(-jnp.exp2(-5.0 - jnp.arange(H, dtype=F32)))
    idx = jnp.arange(C, dtype=F32)
    rel = idx[:, None] - idx[None, :]
    dmat = jnp.where(rel[None] >= 0, jnp.exp(log_gamma[:, None, None] * jnp.maximum(rel, 0.0)[None]), 0.0)
    qd = jnp.exp(log_gamma[:, None] * (idx + 1.0))[:, :, None]
    kd = jnp.exp(log_gamma[:, None] * (C - 1.0 - idx))[:, :, None]
    cd = jnp.exp(log_gamma * C)[:, None, None]
    return dmat, qd, kd, cd


def _local_step(place, x, tgt, ln1_g, b_in, ret_norm_g, attn_sinks, ln2_g, lnf_g, w_in, other_weights, tables, decays):
    S = x.shape[0]
    cr, sr, ca, sa = tables
    dmat, qd, kd, cd = decays

    h, rq, rk, rv, rg, aq, ak, av, ga, gb = _proj_fwd(x, ln1_g, w_in, b_in, cr, sr, ca, sa)
    ry, states = _ret_fwd(rq, rk, rv, dmat, qd, kd, cd)
    ay = _attn_fwd(aq, ak, av, attn_sinks)
    w_ret, w_attn, w_out, w_gate, w_up, w_down = other_weights(ay)
    u, ba, bb, merged, x1 = _merge_fwd(x, ry, rg, ay, ga, gb, ret_norm_g, w_ret, w_attn, w_out)
    gate, up, h2, act, dx2, loss, dg3 = _ffn_fwd(x1, tgt, ln2_g, lnf_g, w_gate, w_up, w_down)

    dgate, dup, dx1, dg2 = _ffn_bwd(dx2, gate, up, x1, ln2_g, w_gate, w_up, w_down)
    early = _wgrad([act], dx2, 6, None, "wgrad_down")
    early = _wgrad([dgate], h2, 4, early, "wgrad_gate")
    early = _wgrad([dup], h2, 5, early, "wgrad_up")
    dgates, drg, dba, dbb, day, dry, dgr, db_mid = _merge_bwd(dx1, ga, gb, ba, bb, rg, ry, ret_norm_g,
                                                                w_ret, w_attn, w_out)
    early = _wgrad([merged], dx1, 3, early, "wgrad_out")
    early = _wgrad([u], dba, 1, early, "wgrad_ret")
    e32, e16 = _wgrad([ay], dbb, 2, early, "wgrad_attn")

    grp = EARLY
    sems, flight, token = _exchange_start("pair_start_early", [e16, _empty(N_CHIPS * grp.half, BF)], N_CHIPS,
                                          _plan_pair_exchange(grp))
    daq, dak, dav, dsk = _attn_bwd(aq, ak, av, day, attn_sinks, ca, sa, token)
    _, got = _exchange_wait("pair_wait_early", sems, flight, _plan_pair_exchange(grp), after=(dsk,))
    s16 = _pair_sum(place, e32, got, grp)
    sems, flight, token = _exchange_start("chips_start_early", [s16, _empty(3 * grp.half, BF)], 3, _plan_to_chips(grp))
    d_ret = _ret_bwd(rq, rk, rv, dry, states, dmat, qd, kd, cd, cr, sr, token)
    d_proj = [d_ret, drg, daq, dak, dav, dgates]
    grad_x, dg1, db = _proj_bwd(d_proj, x, dx1, ln1_g, w_in)
    late = _wgrad(d_proj[:2], h, 0, None, "wgrad_in_a")
    late = _wgrad(d_proj[2:], h, 0, late, "wgrad_in_b", row0=d_ret.shape[1] + drg.shape[1])
    _, others = _exchange_wait("chips_wait_early", sems, flight, _plan_to_chips(grp), after=(late[0],))
    early_shard = _chip_sum(place, e32, got, others, grp)

    d_sinks = dsk.reshape(2, 2, PAIRS, 128).sum(-1).transpose(0, 2, 1).reshape(1, ATTN_Q_HEADS)
    small = dict(loss=loss, ln1_g=dg1, b_in=db, ret_norm_g=dgr, attn_sinks=d_sinks,
                 ln2_g=dg2, lnf_g=dg3)
    return grad_x, early_shard, late, small


SMALL_ROWS = 16


def _pack_small(ln1, ret, ln2, lnf, b_in, sinks, loss, name):
    def body(a_ref, b_ref, c_ref, d_ref, bi_ref, sk_ref, ls_ref, o_ref):
        o_ref[...] = jnp.zeros_like(o_ref)
        for row, ref in enumerate((a_ref, b_ref, c_ref, d_ref)):
            o_ref[row:row + 1, :] = ref[...]
        for k in range(6):
            o_ref[4 + k:5 + k, :] = bi_ref[:, k * 1024:(k + 1) * 1024]
        o_ref[10:11, 0:D_IN - 6144] = bi_ref[:, 6144:D_IN]
        o_ref[11:12, 0:ATTN_Q_HEADS] = sk_ref[...]
        o_ref[12:13, 0:LANES] = ls_ref[...]

    return pl.pallas_call(
        body, name=name, in_specs=[_VMEM] * 7, out_specs=_VMEM,
        out_shape=jax.ShapeDtypeStruct((SMALL_ROWS, 1024), F32),
    )(ln1.reshape(1, 1024), ret.reshape(1, 1024), ln2.reshape(1, 1024), lnf.reshape(1, 1024),
      b_in.reshape(1, D_IN), sinks.reshape(1, ATTN_Q_HEADS), loss.reshape(1, LANES))


def _unpack_small(p):
    return dict(ln1_g=p[0:1], ret_norm_g=p[1:2], ln2_g=p[2:3], lnf_g=p[3], b_in=p[4:11].reshape(1, -1)[:, :D_IN],
                attn_sinks=p[11:12, :ATTN_Q_HEADS], loss=p[12, 0])


def kernel(x, ln1_g, w_in, b_in, ret_norm_g, w_ret_out, attn_sinks, w_attn_out, w_out, ln2_g, w_ffn_gate, w_ffn_up, w_ffn_down, lnf_g, loss_target, m_ln1_g, m_w_in, m_b_in, m_ret_norm_g, m_w_ret_out, m_attn_sinks, m_w_attn_out, m_w_out, m_ln2_g, m_w_ffn_gate, m_w_ffn_up, m_w_ffn_down, m_lnf_g, v_ln1_g, v_w_in, v_b_in, v_ret_norm_g, v_w_ret_out, v_attn_sinks, v_w_attn_out, v_w_out, v_ln2_g, v_w_ffn_gate, v_w_ffn_up, v_w_ffn_down, v_lnf_g):
    xi = lax.axis_index("x")
    yi = lax.axis_index("y")
    ci = lax.axis_index("c")
    j = 2 * xi + yi

    place = jnp.stack([j, ci]).astype(jnp.int32)

    names = ["w_in", "w_ret_out", "w_attn_out", "w_out", "w_ffn_gate", "w_ffn_up", "w_ffn_down"]
    transposed = [True, False, False, False, True, True, False]
    view = lambda a, t: a[0].T if t else a[0]
    big_w = [view(a, t) for a, t in zip([w_in, w_ret_out, w_attn_out, w_out, w_ffn_gate, w_ffn_up, w_ffn_down], transposed)]
    big_m = [view(a, t) for a, t in zip([m_w_in, m_w_ret_out, m_w_attn_out, m_w_out, m_w_ffn_gate, m_w_ffn_up, m_w_ffn_down], transposed)]
    big_v = [view(a, t) for a, t in zip([v_w_in, v_w_ret_out, v_w_attn_out, v_w_out, v_w_ffn_gate, v_w_ffn_up, v_w_ffn_down], transposed)]

    staged_in = _stage_weight(place, big_w[0], "stage_" + names[0])
    sems, in_flight, token = _exchange_start("gather_in_start", [staged_in], 3, _plan_gather_halves(0))
    staged = [_stage_weight(place, w, "stage_" + n, after=(token,)) for n, w in zip(names[1:], big_w[1:])]
    tables = _tables(x.shape[1], token)
    decays = _decays()
    w_in_full = _exchange_wait("gather_in_wait", sems, in_flight, _plan_gather_halves(0),
                               after=(*staged, *tables, *decays))[0]
    w_in_full = _exchange("gather_in_forward", [w_in_full], 3, _plan_forward_halves(0))[0]
    rest = list(range(1, 7))
    sems, in_flight, token = _exchange_start("gather_start", staged, 3 * len(rest), _plan_gather_direct(rest),
                                             after=(w_in_full,))
    other_weights = lambda after: _exchange_wait("gather_wait", sems, in_flight, _plan_gather_direct(rest),
                                                 after=(after,))

    grad_x, early_shard, (l32, l16), small = _local_step(
        place, x[0], loss_target[0], ln1_g, b_in + token[0, 0], ret_norm_g, attn_sinks, ln2_g, lnf_g.reshape(1, -1),
        w_in_full, other_weights, tables, decays)
    pg = _all_reduce_small(_pack_small(
        small["ln1_g"], small["ret_norm_g"], small["ln2_g"], small["lnf_g"], small["b_in"],
        small["attn_sinks"], small["loss"], "pack_small_grads"), l32)
    early_shard = _exchange("share_early", [early_shard], 1, _plan_share_halves(EARLY))[0]
    _, got = _exchange("pair_exchange_late", [l16, _empty(N_CHIPS * LATE.half, BF)], N_CHIPS, _plan_pair_exchange(LATE))
    s16 = _pair_sum(place, l32, got, LATE)
    sems, flight, token = _exchange_start("chips_start_late", [s16, _empty(3 * LATE.half, BF)], 3, _plan_to_chips(LATE),
                                          after=(pg,))

    upd = {}

    def update(t, shard, after):
        res = _adamw(big_w[t], shard, big_m[t], big_v[t], "adamw_" + names[t], g_row0=GROUP_OF[t].off(t), after=after)
        upd[names[t]] = tuple((r.T if transposed[t] else r)[None] for r in res)
        return res[0]

    done = [update(t, early_shard, token) for t in EARLY.tensors]

    sm = _unpack_small(pg)
    z = jnp.zeros((1, LANES), F32)
    pw = _pack_small(ln1_g, ret_norm_g, ln2_g, lnf_g, b_in, attn_sinks, z, "pack_small_w")
    pm = _pack_small(m_ln1_g, m_ret_norm_g, m_ln2_g, m_lnf_g, m_b_in, m_attn_sinks, z, "pack_small_m")
    pv = _pack_small(v_ln1_g, v_ret_norm_g, v_ln2_g, v_lnf_g, v_b_in, v_attn_sinks, z, "pack_small_v")
    small_res = _adamw(pw, pg, pm, pv, "adamw_small", after=token)
    sd, smn, svn = (_unpack_small(t) for t in small_res[1:])
    for n in ["ln1_g", "b_in", "ret_norm_g", "attn_sinks", "ln2_g", "lnf_g"]:
        upd[n] = (sm[n], sd[n], smn[n], svn[n])

    _, others = _exchange_wait("chips_wait_late", sems, flight, _plan_to_chips(LATE), after=(*done, small_res[1]))
    late_shard = _exchange("share_late", [_chip_sum(place, l32, got, others, LATE)], 1, _plan_share_halves(LATE))[0]
    update(0, late_shard, None)

    order_w = ["ln1_g", "w_in", "b_in", "ret_norm_g", "w_ret_out", "attn_sinks", "w_attn_out", "w_out",
               "ln2_g", "w_ffn_gate", "w_ffn_up", "w_ffn_down", "lnf_g"]
    outs = [sm["loss"], grad_x[None]]
    for k in range(4):
        outs += [upd[n][k] for n in order_w]
    return tuple(outs)
```

```python
from typing import NamedTuple

import jax
import jax.numpy as jnp
from jax import lax
from jax.experimental import pallas as pl
from jax.experimental.pallas import tpu as pltpu

D_MODEL = 1024
SEQ_CHUNK = 128
RET_CHUNK = 256
RET_CHUNKS_PER_STEP = 2
RET_HEADS = 4
RET_QK_DIM = 128
RET_V_DIM = 256
ATTN_Q_HEADS = 16
ATTN_HEAD_DIM = 64
D_FF = 2816
D_IN = 6400
ROPE_THETA = 10000.0
EPS = 1e-6
N_CHIPS = 4

ADAM_LR = 0.001
ADAM_B1 = 0.9
ADAM_B2 = 0.999
ADAM_EPS = 1e-08
ADAM_WD = 0.01
ADAM_STEP = 10

OFF_RQ, OFF_RK, OFF_RV, OFF_RG = 0, 512, 1024, 2048
OFF_AQ, OFF_AK, OFF_AV, OFF_GA, OFF_GB = 3072, 4096, 4224, 4352, 5376

V7X_VMEM_LIMIT = 56 * 1024 * 1024
LANES = 128

BF = jnp.bfloat16
F32 = jnp.float32
MESH = pl.DeviceIdType.MESH

_VMEM = pl.BlockSpec(memory_space=pltpu.VMEM)
_SMEM = pl.BlockSpec(memory_space=pltpu.SMEM)
_ANY = pl.BlockSpec(memory_space=pl.ANY)


def _dot(a, b):
    return jnp.dot(a, b, preferred_element_type=F32)


def _dot_nt(a, b):
    return lax.dot_general(a, b, (((1,), (1,)), ((), ())), preferred_element_type=F32)


def _dot_tn(a, b):
    return lax.dot_general(a, b, (((0,), (0,)), ((), ())), preferred_element_type=F32)


def _sigmoid(x):
    return 0.5 * jnp.tanh(0.5 * x) + 0.5


def _rows(tm, n):
    return pl.BlockSpec((tm, n), lambda i: (i, 0))


def _fixed(shape):
    nd = len(shape)
    return pl.BlockSpec(shape, lambda i: (0,) * nd)


def _params(sem, **kw):
    return pltpu.CompilerParams(dimension_semantics=sem, vmem_limit_bytes=V7X_VMEM_LIMIT, **kw)


def _rot_ret(p, cr, sr):
    return p * cr + pltpu.roll(p, 64, 1) * sr


def _rot_ret_bwd(d, cr, sr):
    return d * cr - pltpu.roll(d, 64, 1) * sr


def _attn_partner(p):
    lane = lax.broadcasted_iota(jnp.int32, p.shape, 1)
    first = (lane % 64) < 32
    return jnp.where(first, pltpu.roll(p, 96, 1), pltpu.roll(p, 32, 1))


def _rot_attn(p, ca, sa):
    return p * ca + _attn_partner(p) * sa


def _rot_attn_bwd(d, ca, sa):
    return d * ca - _attn_partner(d) * sa


def _proj_fwd(x, g1, w_in, b_in, cr, sr, ca, sa):
    S = x.shape[0]
    tm = 512
    scale = RET_QK_DIM ** -0.5

    def body(x_ref, g_ref, w_ref, b_ref, cr_ref, sr_ref, ca_ref, sa_ref,
             h_ref, rq_ref, rk_ref, rv_ref, rg_ref, aq_ref, ak_ref, av_ref, ga_ref, gb_ref):
        xv = x_ref[...]
        r = lax.rsqrt(jnp.mean(xv * xv, axis=-1, keepdims=True) + EPS)
        h = (xv * r * g_ref[...]).astype(BF)
        h_ref[...] = h

        def piece(lo, n):
            return _dot_nt(h, w_ref[lo:lo + n, :]) + b_ref[:, lo:lo + n]

        crv, srv = cr_ref[...], sr_ref[...]
        pq = piece(OFF_RQ, 512)
        pk = piece(OFF_RK, 512)
        for hd in range(RET_HEADS):
            sl = slice(hd * 128, (hd + 1) * 128)
            rq_ref[:, sl] = (_rot_ret(pq[:, sl], crv, srv) * scale).astype(BF)
            rk_ref[:, sl] = _rot_ret(pk[:, sl], crv, srv)
        rv_ref[...] = piece(OFF_RV, 1024).astype(BF)
        rg_ref[...] = piece(OFF_RG, 1024)
        cav, sav = ca_ref[...], sa_ref[...]
        pa = piece(OFF_AQ, 1024)
        for blk in range(8):
            sl = slice(blk * 128, (blk + 1) * 128)
            aq_ref[:, sl] = _rot_attn(pa[:, sl], cav, sav).astype(BF)
        pkv = piece(OFF_AK, 256)
        ak_ref[...] = _rot_attn(pkv[:, :128], cav, sav).astype(BF)
        av_ref[...] = pkv[:, 128:].astype(BF)
        ga_ref[...] = piece(OFF_GA, 1024)
        gb_ref[...] = piece(OFF_GB, 1024)

    outs = [(D_MODEL, BF), (512, BF), (512, F32), (1024, BF), (1024, F32),
            (1024, BF), (128, BF), (128, BF), (1024, F32), (1024, F32)]
    return pl.pallas_call(
        body, name="proj_fwd", grid=(S // tm,),
        in_specs=[_rows(tm, D_MODEL), _fixed((1, D_MODEL)), _VMEM, _VMEM,
                  _rows(tm, 128), _rows(tm, 128), _rows(tm, 128), _rows(tm, 128)],
        out_specs=[_rows(tm, n) for n, _ in outs],
        out_shape=[jax.ShapeDtypeStruct((S, n), dt) for n, dt in outs],
        compiler_params=_params(("parallel",)),
    )(x, g1, w_in, b_in, cr, sr, ca, sa)


def _ret_fwd(rq, rk, rv, dmat, qd, kd, cd):
    S = rq.shape[0]
    N = S // RET_CHUNK
    cps = RET_CHUNKS_PER_STEP
    rows = cps * RET_CHUNK

    def body(q_ref, k_ref, v_ref, dm_ref, qd_ref, kd_ref, cd_ref, y_ref, st_ref, state):
        @pl.when(pl.program_id(0) == 0)
        def _():
            state[...] = jnp.zeros_like(state)

        for hd in range(RET_HEADS):
            qs = slice(hd * 128, (hd + 1) * 128)
            vs = slice(hd * 256, (hd + 1) * 256)
            free = []
            for ch in range(cps):
                rs = slice(ch * RET_CHUNK, (ch + 1) * RET_CHUNK)
                q = q_ref[rs, qs]
                kf = k_ref[rs, qs]
                v = v_ref[rs, vs]
                a = _dot_nt(q, kf.astype(BF)) * dm_ref[hd]
                kdv = (kf * kd_ref[hd]).astype(BF)
                free.append((rs, q, _dot(a.astype(BF), v), _dot_tn(kdv, v)))
            s = state[hd]
            for ch, (rs, q, inner, update) in enumerate(free):
                sb = s.astype(BF)
                st_ref[ch, hd] = sb
                y_ref[rs, vs] = inner + _dot(q, sb) * qd_ref[hd]
                s = s * cd_ref[hd] + update
            state[hd] = s

    return pl.pallas_call(
        body, name="ret_fwd", grid=(N // cps,),
        in_specs=[_rows(rows, 512), _rows(rows, 512), _rows(rows, 1024), _VMEM, _VMEM, _VMEM, _VMEM],
        out_specs=[_rows(rows, 1024), pl.BlockSpec((cps, RET_HEADS, 128, 256), lambda i: (i, 0, 0, 0))],
        out_shape=[jax.ShapeDtypeStruct((S, 1024), F32),
                   jax.ShapeDtypeStruct((N, RET_HEADS, 128, 256), BF)],
        scratch_shapes=[pltpu.VMEM((RET_HEADS, 128, 256), F32)],
        compiler_params=_params(("arbitrary",)),
    )(rq, rk, rv, dmat, qd, kd, cd)


def _kv_halves(prev_ref, cur_ref):
    kk = jnp.concatenate([prev_ref[...], cur_ref[...]], axis=0).astype(F32)
    return kk, pltpu.roll(kk, 64, 1)


PAIRS = 4


def _attn_fold(n):
    t = lax.broadcasted_iota(jnp.int32, (128, PAIRS * 128), 0)
    qi = lax.broadcasted_iota(jnp.int32, (128, PAIRS * 128), 1) & 127
    cur = t <= qi
    return cur, jnp.logical_or(cur, n > 0)


def _group_operands(g, kk, kk_sw, vv, vv_sw):
    lo = lax.broadcasted_iota(jnp.int32, (256, 128), 1) < 64
    k_a, k_b = (kk, kk_sw) if g == 0 else (kk_sw, kk)
    v_a, v_b = (vv, vv_sw) if g == 0 else (vv_sw, vv)
    return (jnp.where(lo, k_a, 0.0).astype(BF), jnp.where(lo, 0.0, k_b).astype(BF),
            jnp.where(lo, v_a, 0.0).astype(BF), jnp.where(lo, 0.0, v_b).astype(BF))


def _stack_pairs(ref, g):
    return jnp.concatenate([ref[:, (4 * g + p) * 128:(4 * g + p + 1) * 128] for p in range(PAIRS)], axis=0)


def _sink_row(sk_ref, g, which):
    return jnp.concatenate([jnp.full((1, 128), sk_ref[0, 8 * g + 2 * p + which], F32) for p in range(PAIRS)], axis=1)


def _fold(full, cur):
    return jnp.where(cur, full[128:], full[:128])


def _attn_probs(raw, fold, sink):
    cur, valid = fold
    s = _fold(raw, cur) * (ATTN_HEAD_DIM ** -0.5)
    s = jnp.where(valid, s, -1e30)
    m = jnp.maximum(jnp.max(s, axis=0, keepdims=True), sink)
    e = jnp.exp(s - m)
    es = jnp.exp(sink - m)
    inv = 1.0 / (jnp.sum(e, axis=0, keepdims=True) + es)
    return e * inv, es * inv


def _unfold(p, cur):
    return jnp.concatenate([jnp.where(cur, 0.0, p), jnp.where(cur, p, 0.0)], axis=0).astype(BF)


def _attn_fwd(aq, ak, av, sinks):
    S = aq.shape[0]
    N = S // SEQ_CHUNK

    def body(sk_ref, q_ref, kp_ref, kc_ref, vp_ref, vc_ref, o_ref):
        kk, kk_sw = _kv_halves(kp_ref, kc_ref)
        vv, vv_sw = _kv_halves(vp_ref, vc_ref)
        fold = _attn_fold(pl.program_id(0))
        ops = [_group_operands(g, kk, kk_sw, vv, vv_sw) for g in range(2)]
        raw = [[_dot_nt(kx, _stack_pairs(q_ref, g)) for kx in ops[g][:2]] for g in range(2)]
        for g in range(2):
            o = jnp.zeros((PAIRS * 128, 128), F32)
            for which, vx in enumerate(ops[g][2:]):
                pr, _ = _attn_probs(raw[g][which], fold, _sink_row(sk_ref, g, which))
                o = o + _dot_tn(_unfold(pr, fold[0]), vx)
            for p in range(PAIRS):
                o_ref[:, (4 * g + p) * 128:(4 * g + p + 1) * 128] = o[p * 128:(p + 1) * 128].astype(BF)

    prev = lambda i: (jnp.maximum(i - 1, 0), 0)
    return pl.pallas_call(
        body, name="attn_fwd", grid=(N,),
        in_specs=[_SMEM, _rows(128, 1024), pl.BlockSpec((128, 128), prev), _rows(128, 128),
                  pl.BlockSpec((128, 128), prev), _rows(128, 128)],
        out_specs=_rows(128, 1024),
        out_shape=jax.ShapeDtypeStruct((S, 1024), BF),
        compiler_params=_params(("parallel",)),
    )(sinks, aq, ak, ak, av, av)


def _group_norm(y):
    mu = jnp.mean(y, axis=-1, keepdims=True)
    yc = y - mu
    rstd = lax.rsqrt(jnp.mean(yc * yc, axis=-1, keepdims=True) + EPS)
    return yc * rstd, rstd


def _merge_fwd(x, ry, rg, ay, ga, gb, g_ret, w_ret, w_attn, w_out):
    S = x.shape[0]
    tm = 256

    def body(x_ref, ry_ref, rg_ref, ay_ref, ga_ref, gb_ref, gr_ref, wr_ref, wa_ref, wo_ref,
             u_ref, ba_ref, bb_ref, mg_ref, x1_ref):
        for hd in range(RET_HEADS):
            vs = slice(hd * 256, (hd + 1) * 256)
            yn, _ = _group_norm(ry_ref[:, vs])
            rgv = rg_ref[:, vs]
            u_ref[:, vs] = (rgv * _sigmoid(rgv) * (yn * gr_ref[:, vs])).astype(BF)
        ba = _dot(u_ref[...], wr_ref[...])
        bb = _dot(ay_ref[...], wa_ref[...])
        ba_ref[...] = ba
        bb_ref[...] = bb
        merged = (_sigmoid(ga_ref[...]) * ba + _sigmoid(gb_ref[...]) * bb).astype(BF)
        mg_ref[...] = merged
        x1_ref[...] = x_ref[...] + _dot(merged, wo_ref[...])

    outs = [BF, F32, F32, BF, F32]
    return pl.pallas_call(
        body, name="merge_fwd", grid=(S // tm,),
        in_specs=[_rows(tm, 1024)] * 6 + [_fixed((1, 1024)), _VMEM, _VMEM, _VMEM],
        out_specs=[_rows(tm, 1024)] * 5,
        out_shape=[jax.ShapeDtypeStruct((S, 1024), dt) for dt in outs],
        compiler_params=_params(("parallel",)),
    )(x, ry, rg, ay, ga, gb, g_ret, w_ret, w_attn, w_out)


def _ffn_fwd(x1, tgt, g2, g3, w_gate, w_up, w_down):
    S = x1.shape[0]
    tm = 256

    def body(x1_ref, t_ref, g2_ref, g3_ref, wg_ref, wu_ref, wd_ref,
             gate_ref, up_ref, h2_ref, act_ref, dx2_ref, loss_ref, dg3_ref):
        @pl.when(pl.program_id(0) == 0)
        def _():
            loss_ref[...] = jnp.zeros_like(loss_ref)
            dg3_ref[...] = jnp.zeros_like(dg3_ref)

        x1v = x1_ref[...]
        r2 = lax.rsqrt(jnp.mean(x1v * x1v, axis=-1, keepdims=True) + EPS)
        h2 = (x1v * r2 * g2_ref[...]).astype(BF)
        h2_ref[...] = h2
        gate = _dot_nt(h2, wg_ref[...])
        up = _dot_nt(h2, wu_ref[...])
        gate_ref[...] = gate
        up_ref[...] = up
        act = (gate * _sigmoid(gate) * up).astype(BF)
        act_ref[...] = act
        x2 = x1v + _dot(act, wd_ref[...])
        r3 = lax.rsqrt(jnp.mean(x2 * x2, axis=-1, keepdims=True) + EPS)
        xh = x2 * r3
        g3v = g3_ref[...]
        err = xh * g3v - t_ref[...]
        loss_ref[...] += 0.5 * jnp.sum(jnp.mean(err * err, axis=-1, keepdims=True), axis=0, keepdims=True)
        dy = err * (1.0 / D_MODEL)
        dg3_ref[...] += jnp.sum(dy * xh, axis=0, keepdims=True)
        dxh = dy * g3v
        dx2_ref[...] = r3 * (dxh - xh * jnp.mean(dxh * xh, axis=-1, keepdims=True))

    return pl.pallas_call(
        body, name="ffn_fwd", grid=(S // tm,),
        in_specs=[_rows(tm, 1024), _rows(tm, 1024), _fixed((1, 1024)), _fixed((1, 1024)), _VMEM, _VMEM, _VMEM],
        out_specs=[_rows(tm, D_FF), _rows(tm, D_FF), _rows(tm, 1024), _rows(tm, D_FF), _rows(tm, 1024),
                   _fixed((1, LANES)), _fixed((1, 1024))],
        out_shape=[jax.ShapeDtypeStruct((S, D_FF), F32), jax.ShapeDtypeStruct((S, D_FF), F32),
                   jax.ShapeDtypeStruct((S, 1024), BF), jax.ShapeDtypeStruct((S, D_FF), BF),
                   jax.ShapeDtypeStruct((S, 1024), F32),
                   jax.ShapeDtypeStruct((1, LANES), F32), jax.ShapeDtypeStruct((1, 1024), F32)],
        compiler_params=_params(("arbitrary",)),
    )(x1, tgt, g2, g3, w_gate, w_up, w_down)


def _ffn_bwd(dx2, gate, up, x1, g2, w_gate, w_up, w_down):
    S = x1.shape[0]
    tm = 256

    def body(dx2_ref, gate_ref, up_ref, x1_ref, g2_ref, wg_ref, wu_ref, wd_ref,
             dgate_ref, dup_ref, dx1_ref, dg2_ref):
        @pl.when(pl.program_id(0) == 0)
        def _():
            dg2_ref[...] = jnp.zeros_like(dg2_ref)

        dx2v = dx2_ref[...]
        dact = _dot_nt(dx2v.astype(BF), wd_ref[...])
        gate = gate_ref[...]
        sg = _sigmoid(gate)
        dgate = (dact * up_ref[...] * (sg * (1.0 + gate * (1.0 - sg)))).astype(BF)
        dup = (dact * (gate * sg)).astype(BF)
        dgate_ref[...] = dgate
        dup_ref[...] = dup
        dh2 = _dot(dgate, wg_ref[...]) + _dot(dup, wu_ref[...])
        x1v = x1_ref[...]
        r2 = lax.rsqrt(jnp.mean(x1v * x1v, axis=-1, keepdims=True) + EPS)
        xh = x1v * r2
        dg2_ref[...] += jnp.sum(dh2 * xh, axis=0, keepdims=True)
        dxh = dh2 * g2_ref[...]
        dx1_ref[...] = dx2v + r2 * (dxh - xh * jnp.mean(dxh * xh, axis=-1, keepdims=True))

    return pl.pallas_call(
        body, name="ffn_bwd", grid=(S // tm,),
        in_specs=[_rows(tm, 1024), _rows(tm, D_FF), _rows(tm, D_FF), _rows(tm, 1024), _fixed((1, 1024)),
                  _VMEM, _VMEM, _VMEM],
        out_specs=[_rows(tm, D_FF), _rows(tm, D_FF), _rows(tm, 1024), _fixed((1, 1024))],
        out_shape=[jax.ShapeDtypeStruct((S, D_FF), BF), jax.ShapeDtypeStruct((S, D_FF), BF),
                   jax.ShapeDtypeStruct((S, 1024), F32), jax.ShapeDtypeStruct((1, 1024), F32)],
        compiler_params=_params(("arbitrary",)),
    )(dx2, gate, up, x1, g2, w_gate, w_up, w_down)


def _merge_bwd(dx1, ga, gb, ba, bb, rg, ry, g_ret, w_ret, w_attn, w_out):
    S = dx1.shape[0]
    tm = 256

    def body(dx1_ref, ga_ref, gb_ref, ba_ref, bb_ref, rg_ref, ry_ref, gr_ref, wr_ref, wa_ref, wo_ref,
             dgates_ref, drg_ref, dba_ref, dbb_ref, day_ref, dry_ref, dgr_ref, db_ref):
        @pl.when(pl.program_id(0) == 0)
        def _():
            dgr_ref[...] = jnp.zeros_like(dgr_ref)
            db_ref[...] = jnp.zeros_like(db_ref)

        dm = _dot_nt(dx1_ref[...].astype(BF), wo_ref[...])
        sa = _sigmoid(ga_ref[...])
        sb = _sigmoid(gb_ref[...])
        dga = (dm * ba_ref[...] * (sa * (1.0 - sa))).astype(BF)
        dgb = (dm * bb_ref[...] * (sb * (1.0 - sb))).astype(BF)
        dgates_ref[:, :1024] = dga
        dgates_ref[:, 1024:] = dgb
        dba = (dm * sa).astype(BF)
        dbb = (dm * sb).astype(BF)
        dba_ref[...] = dba
        dbb_ref[...] = dbb
        day_ref[...] = _dot_nt(dbb, wa_ref[...]).astype(BF)
        du = _dot_nt(dba, wr_ref[...])
        for hd in range(RET_HEADS):
            vs = slice(hd * 256, (hd + 1) * 256)
            yn, rstd = _group_norm(ry_ref[:, vs])
            rgv = rg_ref[:, vs]
            sr = _sigmoid(rgv)
            silu = rgv * sr
            duv = du[:, vs]
            grv = gr_ref[:, vs]
            drg = (duv * (yn * grv) * (sr * (1.0 + rgv * (1.0 - sr)))).astype(BF)
            drg_ref[:, vs] = drg
            dyg = duv * silu
            dgr_ref[:, vs] += jnp.sum(dyg * yn, axis=0, keepdims=True)
            dyn = dyg * grv
            dry_ref[:, vs] = rstd * (dyn - jnp.mean(dyn, axis=-1, keepdims=True)
                                     - yn * jnp.mean(dyn * yn, axis=-1, keepdims=True))
            db_ref[:, vs] += jnp.sum(drg.astype(F32), axis=0, keepdims=True)
        db_ref[:, 1024:2048] += jnp.sum(dga.astype(F32), axis=0, keepdims=True)
        db_ref[:, 2048:] += jnp.sum(dgb.astype(F32), axis=0, keepdims=True)

    return pl.pallas_call(
        body, name="merge_bwd", grid=(S // tm,),
        in_specs=[_rows(tm, 1024)] * 7 + [_fixed((1, 1024)), _VMEM, _VMEM, _VMEM],
        out_specs=[_rows(tm, 2048), _rows(tm, 1024), _rows(tm, 1024), _rows(tm, 1024), _rows(tm, 1024),
                   _rows(tm, 1024), _fixed((1, 1024)), _fixed((1, 3072))],
        out_shape=[jax.ShapeDtypeStruct((S, 2048), BF), jax.ShapeDtypeStruct((S, 1024), BF),
                   jax.ShapeDtypeStruct((S, 1024), BF), jax.ShapeDtypeStruct((S, 1024), BF),
                   jax.ShapeDtypeStruct((S, 1024), BF), jax.ShapeDtypeStruct((S, 1024), F32),
                   jax.ShapeDtypeStruct((1, 1024), F32), jax.ShapeDtypeStruct((1, 3072), F32)],
        compiler_params=_params(("arbitrary",)),
    )(dx1, ga, gb, ba, bb, rg, ry, g_ret, w_ret, w_attn, w_out)


def _attn_bwd(aq, ak, av, day, sinks, ca, sa, after):
    S = aq.shape[0]
    N = S // SEQ_CHUNK

    def body(sk_ref, q_ref, kp_ref, kc_ref, vp_ref, vc_ref, do_ref, cq_ref, sq_ref, cp_ref, sp_ref, after_ref,
             dq_ref, dk_ref, dv_ref, dsk_ref, carry_k, carry_v):
        n = pl.program_id(0)

        @pl.when(n == 0)
        def _():
            carry_k[...] = jnp.zeros_like(carry_k)
            carry_v[...] = jnp.zeros_like(carry_v)
            dsk_ref[...] = jnp.zeros_like(dsk_ref)

        @pl.when(n < N)
        def _():
            kk, kk_sw = _kv_halves(kp_ref, kc_ref)
            vv, vv_sw = _kv_halves(vp_ref, vc_ref)
            lo = lax.broadcasted_iota(jnp.int32, (256, 128), 1) < 64
            fold = _attn_fold(n)
            cur = fold[0]
            cq, sq = cq_ref[...], sq_ref[...]
            dk_groups, dv_groups = [], []
            ops = [_group_operands(g, kk, kk_sw, vv, vv_sw) for g in range(2)]
            stacked = [(_stack_pairs(q_ref, g), _stack_pairs(do_ref, g)) for g in range(2)]
            raw = [[_dot_nt(kx, stacked[g][0]) for kx in ops[g][:2]] for g in range(2)]
            raw_dp = [[_dot_nt(vx, stacked[g][1]) for vx in ops[g][2:]] for g in range(2)]
            for g in range(2):
                qs, dos = stacked[g]
                dq = jnp.zeros((PAIRS * 128, 128), F32)
                dk_half, dv_half = [], []
                for which, kx in enumerate(ops[g][:2]):
                    pr, ps = _attn_probs(raw[g][which], fold, _sink_row(sk_ref, g, which))
                    dp = _fold(raw_dp[g][which], cur)
                    dr = jnp.sum(pr * dp, axis=0, keepdims=True)
                    ds = _unfold(pr * (dp - dr) * (ATTN_HEAD_DIM ** -0.5), cur)
                    dsk_ref[2 * g + which:2 * g + which + 1, :] -= ps * dr
                    dq = dq + _dot_tn(ds, kx)
                    dk_half.append(_dot(ds, qs))
                    dv_half.append(_dot(_unfold(pr, cur), dos))
                for p in range(PAIRS):
                    sl = slice((4 * g + p) * 128, (4 * g + p + 1) * 128)
                    dq_ref[:, sl] = _rot_attn_bwd(dq[p * 128:(p + 1) * 128], cq, sq).astype(BF)
                tk = jnp.where(lo, dk_half[0], dk_half[1])
                tv = jnp.where(lo, dv_half[0], dv_half[1])
                dk_groups.append(tk + pltpu.roll(tk, 64, 1))
                dv_groups.append(tv + pltpu.roll(tv, 64, 1))
            dk_full = jnp.where(lo, dk_groups[0], dk_groups[1])
            dv_full = jnp.where(lo, dv_groups[0], dv_groups[1])
            dk_ref[...] = _rot_attn_bwd(carry_k[...] + dk_full[:128], cp_ref[...], sp_ref[...]).astype(BF)
            dv_ref[...] = (carry_v[...] + dv_full[:128]).astype(BF)
            carry_k[...] = dk_full[128:]
            carry_v[...] = dv_full[128:]

        @pl.when(n == N)
        def _():
            dk_ref[...] = _rot_attn_bwd(carry_k[...], cp_ref[...], sp_ref[...]).astype(BF)
            dv_ref[...] = carry_v[...].astype(BF)

    cur = lambda i: (jnp.minimum(i, N - 1), 0)
    prev = lambda i: (jnp.maximum(i - 1, 0), 0)
    blk = lambda w, im: pl.BlockSpec((128, w), im)
    return pl.pallas_call(
        body, name="attn_bwd", grid=(N + 1,),
        in_specs=[_SMEM, blk(1024, cur), blk(128, prev), blk(128, cur), blk(128, prev), blk(128, cur),
                  blk(1024, cur), blk(128, cur), blk(128, cur), blk(128, prev), blk(128, prev), _ANY],
        out_specs=[blk(1024, cur), blk(128, prev), blk(128, prev), _fixed((4, PAIRS * 128))],
        out_shape=[jax.ShapeDtypeStruct((S, 1024), BF), jax.ShapeDtypeStruct((S, 128), BF),
                   jax.ShapeDtypeStruct((S, 128), BF), jax.ShapeDtypeStruct((4, PAIRS * 128), F32)],
        scratch_shapes=[pltpu.VMEM((128, 128), F32), pltpu.VMEM((128, 128), F32)],
        compiler_params=_params(("arbitrary",)),
    )(sinks, aq, ak, ak, av, av, day, ca, sa, ca, sa, after)


def _ret_bwd(rq, rk, rv, dry, states, dmat, qd, kd, cd, cr, sr, after):
    S = rq.shape[0]
    N = S // RET_CHUNK
    scale = RET_QK_DIM ** -0.5
    cps = RET_CHUNKS_PER_STEP
    steps = N // cps

    def body(q_ref, k_ref, v_ref, dy_ref, st_ref, dm_ref, qd_ref, kd_ref, cd_ref, cr_ref, sr_ref, after_ref,
             out_ref, dstate):
        @pl.when(pl.program_id(0) == 0)
        def _():
            dstate[...] = jnp.zeros_like(dstate)

        for hd in range(RET_HEADS):
            qs = slice(hd * 128, (hd + 1) * 128)
            vs = slice(hd * 256, (hd + 1) * 256)
            dmv = dm_ref[hd]
            free = []
            for ch in range(cps):
                rs = slice(ch * RET_CHUNK, (ch + 1) * RET_CHUNK)
                crv, srv = cr_ref[rs, :], sr_ref[rs, :]
                q = q_ref[rs, qs]
                kf = k_ref[rs, qs]
                k = kf.astype(BF)
                v = v_ref[rs, vs]
                do = dy_ref[rs, vs]
                dob = do.astype(BF)
                doq = (do * qd_ref[hd]).astype(BF)
                a = (_dot_nt(q, k) * dmv).astype(BF)
                da = (_dot_nt(dob, v) * dmv).astype(BF)
                dq = _dot(da, k) + _dot_nt(doq, st_ref[ch, hd])
                out_ref[rs, qs] = _rot_ret_bwd(dq * scale, crv, srv).astype(BF)
                free.append((rs, crv, srv, v, (kf * kd_ref[hd]).astype(BF),
                             _dot_tn(a, dob), _dot_tn(da, q), _dot_tn(q, doq)))
            ds = dstate[hd]
            for rs, crv, srv, v, kdv, dv_free, dk_free, ds_update in reversed(free):
                dsb = ds.astype(BF)
                dv = dv_free + _dot(kdv, dsb)
                dk = dk_free + _dot_nt(v, dsb) * kd_ref[hd]
                ds = ds * cd_ref[hd] + ds_update
                out_ref[rs, 512 + hd * 128:512 + (hd + 1) * 128] = _rot_ret_bwd(dk, crv, srv).astype(BF)
                out_ref[rs, 1024 + hd * 256:1024 + (hd + 1) * 256] = dv.astype(BF)
            dstate[hd] = ds

    rev = lambda w: pl.BlockSpec((cps * RET_CHUNK, w), lambda i: (steps - 1 - i, 0))
    return pl.pallas_call(
        body, name="ret_bwd", grid=(steps,),
        in_specs=[rev(512), rev(512), rev(1024), rev(1024),
                  pl.BlockSpec((cps, RET_HEADS, 128, 256), lambda i: (steps - 1 - i, 0, 0, 0)),
                  _VMEM, _VMEM, _VMEM, _VMEM, rev(128), rev(128), _ANY],
        out_specs=rev(2048),
        out_shape=jax.ShapeDtypeStruct((S, 2048), BF),
        scratch_shapes=[pltpu.VMEM((RET_HEADS, 128, 256), F32)],
        compiler_params=_params(("arbitrary",)),
    )(rq, rk, rv, dry, states, dmat, qd, kd, cd, cr, sr, after)


def _proj_bwd(d_pieces, x, dx1, g1, w_in_t):
    S = x.shape[0]
    tm = 512
    widths = [d.shape[1] for d in d_pieces]
    assert sum(widths) == D_IN
    nd = len(d_pieces)

    def body(*refs):
        d_refs = refs[:nd]
        x_ref, dx1_ref, g_ref, w_ref, gx_ref, dg_ref, db_ref = refs[nd:]

        @pl.when(pl.program_id(0) == 0)
        def _():
            dg_ref[...] = jnp.zeros_like(dg_ref)
            db_ref[...] = jnp.zeros_like(db_ref)

        dh = jnp.zeros((tm, D_MODEL), F32)
        lo = 0
        for ref, n in zip(d_refs, widths):
            v = ref[...]
            dh = dh + _dot(v, w_ref[lo:lo + n, :])
            db_ref[:, lo:lo + n] += jnp.sum(v.astype(F32), axis=0, keepdims=True)
            lo += n
        xv = x_ref[...]
        r = lax.rsqrt(jnp.mean(xv * xv, axis=-1, keepdims=True) + EPS)
        xh = xv * r
        dg_ref[...] += jnp.sum(dh * xh, axis=0, keepdims=True)
        dxh = dh * g_ref[...]
        gx_ref[...] = dx1_ref[...] + r * (dxh - xh * jnp.mean(dxh * xh, axis=-1, keepdims=True))

    return pl.pallas_call(
        body, name="proj_bwd", grid=(S // tm,),
        in_specs=[_rows(tm, n) for n in widths] + [_rows(tm, 1024), _rows(tm, 1024), _fixed((1, 1024)), _VMEM],
        out_specs=[_rows(tm, 1024), _fixed((1, 1024)), _fixed((1, D_IN))],
        out_shape=[jax.ShapeDtypeStruct((S, 1024), F32), jax.ShapeDtypeStruct((1, 1024), F32),
                   jax.ShapeDtypeStruct((1, D_IN), F32)],
        compiler_params=_params(("arbitrary",)),
    )(*d_pieces, x, dx1, g1, w_in_t)


SHARD_ROWS = (1600, 256, 256, 256, 704, 704, 704)


class Group(NamedTuple):
    name: str
    tensors: tuple
    offs: tuple
    block: int
    tile: int

    @property
    def half(self):
        return self.block // 2

    def off(self, t):
        return self.offs[self.tensors.index(t)]


LATE = Group("late", (0,), (0,), 1600, 800)
EARLY = Group("early", (1, 2, 3, 4, 5, 6), (0, 256, 512, 768, 1472, 2176), 2880, 480)
GROUP_OF = {t: g for g in (LATE, EARLY) for t in g.tensors}


def _wgrad(a_list, b, t, packed, name, row0=0):
    S = b.shape[0]
    widths = [a.shape[1] for a in a_list]
    M = sum(widths)
    ts = 1024
    steps = S // ts
    na = len(a_list)
    grp = GROUP_OF[t]
    r, off = SHARD_ROWS[t], grp.off(t)
    segments = []
    for chip in range(N_CHIPS):
        lo, hi = max(chip * r, row0), min((chip + 1) * r, row0 + M)
        if lo < hi:
            segments.append((lo - row0, hi - lo, chip * grp.block + off + lo - chip * r))
    assert sum(n for _, n, _ in segments) == M
    stage_rows = max(n for _, n, _ in segments)
    n_in = na + 1 + (2 if packed is not None else 0)

    def body(*refs):
        a_refs, b_ref = refs[:na], refs[na]
        g32_ref, gb_ref, acc, stage, sems = refs[n_in:]
        i = pl.program_id(0)

        @pl.when(i == 0)
        def _():
            acc[...] = jnp.zeros_like(acc)

        bv = b_ref[...].astype(BF)
        lo = 0
        for a_ref, n in zip(a_refs, widths):
            acc[lo:lo + n, :] += _dot_tn(a_ref[...].astype(BF), bv)
            lo += n

        @pl.when(i == steps - 1)
        def _():
            for a0, n, p0 in segments:
                c32 = pltpu.make_async_copy(acc.at[a0:a0 + n, :], g32_ref.at[pl.ds(p0, n), :], sems.at[0])
                c32.start()
                stage[0:n, :] = acc[a0:a0 + n, :].astype(BF)
                c16 = pltpu.make_async_copy(stage.at[0:n, :], gb_ref.at[pl.ds(p0, n), :], sems.at[1])
                c16.start()
                c32.wait()
                c16.wait()

    shapes = [jax.ShapeDtypeStruct((N_CHIPS * grp.block, 1024), F32),
              jax.ShapeDtypeStruct((N_CHIPS * grp.block, 1024), BF)]
    operands = list(a_list) + [b]
    in_specs = [_rows(ts, n) for n in widths] + [_rows(ts, 1024)]
    aliases = {}
    if packed is not None:
        operands += list(packed)
        in_specs += [_ANY, _ANY]
        aliases = {na + 1: 0, na + 2: 1}
    return pl.pallas_call(
        body, name=name, grid=(steps,),
        in_specs=in_specs, out_specs=[_ANY, _ANY], out_shape=shapes,
        scratch_shapes=[pltpu.VMEM((M, 1024), F32), pltpu.VMEM((stage_rows, 1024), BF),
                        pltpu.SemaphoreType.DMA((2,))],
        input_output_aliases=aliases,
        compiler_params=_params(("arbitrary",)),
    )(*operands)


def _stage_weight(place, w, name, after=()):
    r = w.shape[0]
    tr = 800 if r == 1600 else r
    nb = r // tr

    def body(p_ref, w_ref, *rest):
        rest[-1][...] = w_ref[...].astype(BF)

    return pl.pallas_call(
        body, name=name,
        grid_spec=pltpu.PrefetchScalarGridSpec(
            num_scalar_prefetch=1, grid=(nb,),
            in_specs=[pl.BlockSpec((tr, 1024), lambda i, p: (i, 0))] + [_ANY] * len(after),
            out_specs=pl.BlockSpec((tr, 1024), lambda i, p: (p[0] * nb + i, 0))),
        out_shape=jax.ShapeDtypeStruct((N_CHIPS * r, 1024), BF),
        compiler_params=_params(("arbitrary",)),
    )(place, w, *after)


def _kept_block(grp, nb):
    return lambda i, p: ((2 * (i // nb) + p[1]) * nb + i % nb, 0)


def _pair_sum(place, g32, got, grp):
    nb = grp.half // grp.tile

    def body(p_ref, g_ref, r_ref, s16_ref):
        s16_ref[...] = (g_ref[...] + r_ref[...].astype(F32)).astype(BF)

    flat = pl.BlockSpec((grp.tile, 1024), lambda i, p: (i, 0))
    return pl.pallas_call(
        body, name="pair_sum_" + grp.name,
        grid_spec=pltpu.PrefetchScalarGridSpec(
            num_scalar_prefetch=1, grid=(N_CHIPS * nb,),
            in_specs=[pl.BlockSpec((grp.tile, 1024), _kept_block(grp, nb)), flat], out_specs=flat),
        out_shape=jax.ShapeDtypeStruct((N_CHIPS * grp.half, 1024), BF),
        compiler_params=_params(("arbitrary",)),
    )(place, g32, got)


def _chip_sum(place, g32, got, others, grp):
    nb = grp.half // grp.tile

    def body(p_ref, g_ref, r_ref, r0_ref, r1_ref, r2_ref, o_ref):
        own = g_ref[...] + r_ref[...].astype(F32)
        o_ref[...] = ((own + r0_ref[...].astype(F32)) + r1_ref[...].astype(F32)) + r2_ref[...].astype(F32)

    blk = lambda f: pl.BlockSpec((grp.tile, 1024), f)
    return pl.pallas_call(
        body, name="chip_sum_" + grp.name,
        grid_spec=pltpu.PrefetchScalarGridSpec(
            num_scalar_prefetch=1, grid=(nb,),
            in_specs=[blk(lambda i, p: ((2 * p[0] + p[1]) * nb + i, 0)), blk(lambda i, p: (p[0] * nb + i, 0)),
                      blk(lambda i, p: (i, 0)), blk(lambda i, p: (nb + i, 0)), blk(lambda i, p: (2 * nb + i, 0))],
            out_specs=blk(lambda i, p: (p[1] * nb + i, 0))),
        out_shape=jax.ShapeDtypeStruct((grp.block, 1024), F32),
        compiler_params=_params(("arbitrary",)),
    )(place, g32, got, others, others, others)


def _adamw(w, g, m, v, name, g_row0=0, after=None):
    r = w.shape[0]
    tr = 320 if r % 320 == 0 and g_row0 % 320 == 0 else (64 if r % 64 == 0 else r)
    assert g_row0 % tr == 0
    g0 = g_row0 // tr
    extra = [] if after is None else [after]

    def body(w_ref, g_ref, m_ref, v_ref, *rest):
        go_ref, d_ref, mo_ref, vo_ref = rest[len(extra):]
        gv = g_ref[...]
        mn = ADAM_B1 * m_ref[...] + (1.0 - ADAM_B1) * gv
        vn = ADAM_B2 * v_ref[...] + (1.0 - ADAM_B2) * (gv * gv)
        m_hat = mn / (1.0 - ADAM_B1 ** ADAM_STEP)
        v_hat = vn / (1.0 - ADAM_B2 ** ADAM_STEP)
        go_ref[...] = gv
        d_ref[...] = -ADAM_LR * (m_hat / (jnp.sqrt(v_hat) + ADAM_EPS) + ADAM_WD * w_ref[...])
        mo_ref[...] = mn
        vo_ref[...] = vn

    return pl.pallas_call(
        body, name=name, grid=(r // tr,),
        in_specs=[_rows(tr, 1024), pl.BlockSpec((tr, 1024), lambda i: (g0 + i, 0)), _rows(tr, 1024), _rows(tr, 1024)]
        + [_ANY] * len(extra),
        out_specs=[_rows(tr, 1024)] * 4,
        out_shape=[jax.ShapeDtypeStruct((r, 1024), F32)] * 4,
        compiler_params=_params(("parallel",)),
    )(w, g, m, v, *extra)


def _place():
    x, y, c = lax.axis_index("x"), lax.axis_index("y"), lax.axis_index("c")
    chips = [(1 - x, y), (x, 1 - y), (1 - x, 1 - y)]
    return x, y, c, chips


_HBM =pl.BlockSpec(memory_space=pltpu.HBM)
_SEM = pl.BlockSpec(memory_space=pltpu.SEMAPHORE)
_DATAFLOW = pltpu.SideEffectType.DATAFLOW_SIDE_EFFECTING


def _exchange(name, arrays, n, plan):
    na = len(arrays)

    def body(*refs):
        pairs = plan(refs[na:2 * na], refs[2 * na], refs[2 * na + 1])
        for send, _ in pairs:
            send.start()
        for send, recv in pairs:
            send.wait_send()
            recv.wait_recv()

    return pl.pallas_call(
        body, name=name, in_specs=[_ANY] * na, out_specs=[_ANY] * na,
        out_shape=[jax.ShapeDtypeStruct(a.shape, a.dtype) for a in arrays],
        scratch_shapes=[pltpu.SemaphoreType.DMA((n,)), pltpu.SemaphoreType.DMA((n,))],
        input_output_aliases={i: i for i in range(na)},
    )(*arrays)


def _exchange_start(name, arrays, n, plan, after=()):
    na = len(arrays)

    def body(*refs):
        for send, _ in plan(refs[:na], refs[na + len(after)], refs[na + len(after) + 1]):
            send.start()
        refs[-1][...] = jnp.zeros_like(refs[-1])

    out = pl.pallas_call(
        body, name=name,
        out_shape=(pltpu.SemaphoreType.DMA((n,)), pltpu.SemaphoreType.DMA((n,)),
                   *[pltpu.HBM(a.shape, a.dtype) for a in arrays], jax.ShapeDtypeStruct((8, LANES), F32)),
        in_specs=[_HBM] * na + [_ANY] * len(after), out_specs=(_SEM, _SEM, *[_HBM] * na, _VMEM),
        input_output_aliases={i: 2 + i for i in range(na)},
        compiler_params=pltpu.CompilerParams(has_side_effects=_DATAFLOW),
    )(*[pltpu.with_memory_space_constraint(a, pltpu.HBM) for a in arrays], *after)
    return (out[0], out[1]), list(out[2:2 + na]), out[-1]


def _exchange_wait(name, sems, in_flight, plan, after=()):
    na = len(in_flight)

    def body(*refs):
        for send, recv in plan(refs[:na], refs[na], refs[na + 1]):
            send.wait_send()
            recv.wait_recv()

    out = pl.pallas_call(
        body, name=name,
        out_shape=tuple(pltpu.HBM(a.shape, a.dtype) for a in in_flight),
        in_specs=[_HBM] * na + [_SEM, _SEM] + [_ANY] * len(after), out_specs=tuple([_HBM] * na),
        input_output_aliases={i: i for i in range(na)},
        compiler_params=pltpu.CompilerParams(has_side_effects=_DATAFLOW),
    )(*in_flight, *sems, *after)
    return list(out)


def _remote(src, dst, send_sems, recv_sems, k, to):
    return pltpu.make_async_remote_copy(src_ref=src, dst_ref=dst, send_sem=send_sems.at[k], recv_sem=recv_sems.at[k],
                                        device_id=to, device_id_type=MESH)


def _plan_gather_direct(tids):
    def plan(refs, ss, rs):
        x, y, c, chips = _place()
        j = 2 * x + y
        pairs = []
        for t, tid in enumerate(tids):
            r = SHARD_ROWS[tid]
            for k, (cx, cy) in enumerate(chips):
                mine = refs[t].at[pl.ds(j * r, r), :]
                theirs = refs[t].at[pl.ds((2 * cx + cy) * r, r), :]
                pairs.append((_remote(mine, mine, ss, rs, 3 * t + k, (cx, cy, c)),
                              _remote(theirs, theirs, ss, rs, 3 * t + k, (x, y, c))))
        return pairs
    return plan


def _half_rows(ref, tid, pj, half):
    h = SHARD_ROWS[tid] // 2
    return ref.at[pl.ds((2 * pj + half) * h, h), :]


def _plan_gather_halves(tid):
    def plan(refs, ss, rs):
        x, y, c, chips = _place()
        mine = _half_rows(refs[0], tid, 2 * x + y, c)
        return [(_remote(mine, mine, ss, rs, k, (cx, cy, c)),
                 _remote(_half_rows(refs[0], tid, 2 * cx + cy, c), _half_rows(refs[0], tid, 2 * cx + cy, c),
                         ss, rs, k, (x, y, c))) for k, (cx, cy) in enumerate(chips)]
    return plan


def _plan_forward_halves(tid):
    def plan(refs, ss, rs):
        x, y, c, chips = _place()
        pairs = []
        for k, (cx, cy) in enumerate(chips):
            got = _half_rows(refs[0], tid, 2 * cx + cy, c)
            missing = _half_rows(refs[0], tid, 2 * cx + cy, 1 - c)
            pairs.append((_remote(got, got, ss, rs, k, (x, y, 1 - c)), _remote(missing, missing, ss, rs, k, (x, y, 1 - c))))
        return pairs
    return plan


def _plan_pair_exchange(grp):
    def plan(refs, ss, rs):
        x, y, c, _ = _place()
        cps = [_remote(refs[0].at[pl.ds(pj * grp.block + (1 - c) * grp.half, grp.half), :],
                       refs[1].at[pl.ds(pj * grp.half, grp.half), :], ss, rs, pj, (x, y, 1 - c))
               for pj in range(N_CHIPS)]
        return [(cp, cp) for cp in cps]
    return plan


def _plan_to_chips(grp):
    def plan(refs, ss, rs):
        x, y, c, chips = _place()
        cps = [_remote(refs[0].at[pl.ds((2 * cx + cy) * grp.half, grp.half), :],
                       refs[1].at[pl.ds(k * grp.half, grp.half), :], ss, rs, k, (cx, cy, c))
               for k, (cx, cy) in enumerate(chips)]
        return [(cp, cp) for cp in cps]
    return plan


def _plan_share_halves(grp):
    def plan(refs, ss, rs):
        x, y, c, _ = _place()
        mine = refs[0].at[pl.ds(c * grp.half, grp.half), :]
        other = refs[0].at[pl.ds((1 - c) * grp.half, grp.half), :]
        return [(_remote(mine, mine, ss, rs, 0, (x, y, 1 - c)), _remote(other, other, ss, rs, 0, (x, y, 1 - c)))]
    return plan


def _empty(rows, dtype):
    return lax.empty((rows, 1024), dtype)


def _all_reduce_small(v, after):
    m_per, n = v.shape
    n_dev = 8

    def body(x_ref, after_ref, out_ref, all_ref, send_sems, recv_sems, local_sem):
        x, y, c, chips = _place()
        me, sibling = (x, y, c), (x, y, 1 - c)

        def rows(px, py, pc):
            return all_ref.at[pl.ds((4 * px + 2 * py + pc) * m_per, m_per), :]

        def copy(k, block, to, src=None):
            return pltpu.make_async_remote_copy(
                src_ref=rows(*block) if src is None else src, dst_ref=rows(*block),
                send_sem=send_sems.at[k], recv_sem=recv_sems.at[k], device_id=to, device_id_type=MESH)

        mine = pltpu.make_async_copy(x_ref, rows(*me), local_sem)
        mine.start()
        first = [copy(0, me, sibling, src=x_ref)]
        first += [copy(1 + k, me, (*chip, c), src=x_ref) for k, chip in enumerate(chips)]
        for cp in first:
            cp.start()
        passed = [copy(4 + k, (*chip, c), sibling) for k, chip in enumerate(chips)]
        for k, chip in enumerate(chips):
            copy(1 + k, (*chip, c), me).wait_recv()
            passed[k].start()
        copy(0, sibling, me).wait_recv()
        for k, chip in enumerate(chips):
            copy(4 + k, (*chip, 1 - c), me).wait_recv()
        for cp in first + passed:
            cp.wait_send()
        mine.wait()
        acc = all_ref[pl.ds(0, m_per), :]
        for d in range(1, n_dev):
            acc = acc + all_ref[pl.ds(d * m_per, m_per), :]
        out_ref[...] = acc

    return pl.pallas_call(
        body, name="all_reduce_small",
        in_specs=[_VMEM, _ANY], out_specs=_VMEM,
        out_shape=jax.ShapeDtypeStruct((m_per, n), F32),
        scratch_shapes=[pltpu.VMEM((n_dev * m_per, n), F32),
                        pltpu.SemaphoreType.DMA((7,)), pltpu.SemaphoreType.DMA((7,)), pltpu.SemaphoreType.DMA],
        compiler_params=pltpu.CompilerParams(has_side_effects=True),
    )(v, after)


def _tables(S, token):
    pos = jnp.arange(S, dtype=jnp.int32).astype(F32) + token[0, 0]

    def cs(half):
        inv_freq = ROPE_THETA ** (-jnp.arange(half, dtype=F32) / half)
        ang = pos[:, None] * inv_freq[None, :]
        return jnp.cos(ang), jnp.sin(ang)

    c64, s64 = cs(64)
    c32, s32 = cs(32)
    cr = jnp.concatenate([c64, c64], axis=1)
    sr = jnp.concatenate([-s64, s64], axis=1)
    ca = jnp.concatenate([c32, c32, c32, c32], axis=1)
    sa = jnp.concatenate([-s32, s32, -s32, s32], axis=1)
    return cr, sr, ca, sa


def _decays():
    H, C = RET_HEADS, RET_CHUNK
    log_gamma = jnp.log1p(-jnp.exp2(-5.0 - jnp.arange(H, dtype=F32)))
    idx = jnp.arange(C, dtype=F32)
    rel = idx[:, None] - idx[None, :]
    dmat = jnp.where(rel[None] >= 0, jnp.exp(log_gamma[:, None, None] * jnp.maximum(rel, 0.0)[None]), 0.0)
    qd = jnp.exp(log_gamma[:, None] * (idx + 1.0))[:, :, None]
    kd = jnp.exp(log_gamma[:, None] * (C - 1.0 - idx))[:, :, None]
    cd = jnp.exp(log_gamma * C)[:, None, None]
    return dmat, qd, kd, cd


def _local_step(place, x, tgt, ln1_g, b_in, ret_norm_g, attn_sinks, ln2_g, lnf_g, w_in, other_weights, tables, decays):
    S = x.shape[0]
    cr, sr, ca, sa = tables
    dmat, qd, kd, cd = decays

    h, rq, rk, rv, rg, aq, ak, av, ga, gb = _proj_fwd(x, ln1_g, w_in, b_in, cr, sr, ca, sa)
    ry, states = _ret_fwd(rq, rk, rv, dmat, qd, kd, cd)
    ay = _attn_fwd(aq, ak, av, attn_sinks)
    w_ret, w_attn, w_out, w_gate, w_up, w_down = other_weights(ay)
    u, ba, bb, merged, x1 = _merge_fwd(x, ry, rg, ay, ga, gb, ret_norm_g, w_ret, w_attn, w_out)
    gate, up, h2, act, dx2, loss, dg3 = _ffn_fwd(x1, tgt, ln2_g, lnf_g, w_gate, w_up, w_down)

    dgate, dup, dx1, dg2 = _ffn_bwd(dx2, gate, up, x1, ln2_g, w_gate, w_up, w_down)
    early = _wgrad([act], dx2, 6, None, "wgrad_down")
    early = _wgrad([dgate], h2, 4, early, "wgrad_gate")
    early = _wgrad([dup], h2, 5, early, "wgrad_up")
    dgates, drg, dba, dbb, day, dry, dgr, db_mid = _merge_bwd(dx1, ga, gb, ba, bb, rg, ry, ret_norm_g,
                                                                w_ret, w_attn, w_out)
    early = _wgrad([merged], dx1, 3, early, "wgrad_out")
    early = _wgrad([u], dba, 1, early, "wgrad_ret")
    e32, e16 = _wgrad([ay], dbb, 2, early, "wgrad_attn")

    grp = EARLY
    sems, flight, token = _exchange_start("pair_start_early", [e16, _empty(N_CHIPS * grp.half, BF)], N_CHIPS,
                                          _plan_pair_exchange(grp))
    daq, dak, dav, dsk = _attn_bwd(aq, ak, av, day, attn_sinks, ca, sa, token)
    _, got = _exchange_wait("pair_wait_early", sems, flight, _plan_pair_exchange(grp), after=(dsk,))
    s16 = _pair_sum(place, e32, got, grp)
    sems, flight, token = _exchange_start("chips_start_early", [s16, _empty(3 * grp.half, BF)], 3, _plan_to_chips(grp))
    d_ret = _ret_bwd(rq, rk, rv, dry, states, dmat, qd, kd, cd, cr, sr, token)
    d_proj = [d_ret, drg, daq, dak, dav, dgates]
    grad_x, dg1, db = _proj_bwd(d_proj, x, dx1, ln1_g, w_in)
    late = _wgrad(d_proj[:2], h, 0, None, "wgrad_in_a")
    late = _wgrad(d_proj[2:], h, 0, late, "wgrad_in_b", row0=d_ret.shape[1] + drg.shape[1])
    _, others = _exchange_wait("chips_wait_early", sems, flight, _plan_to_chips(grp), after=(late[0],))
    early_shard = _chip_sum(place, e32, got, others, grp)

    d_sinks = dsk.reshape(2, 2, PAIRS, 128).sum(-1).transpose(0, 2, 1).reshape(1, ATTN_Q_HEADS)
    small = dict(loss=loss, ln1_g=dg1, b_in=db, ret_norm_g=dgr, attn_sinks=d_sinks,
                 ln2_g=dg2, lnf_g=dg3)
    return grad_x, early_shard, late, small


SMALL_ROWS = 16


def _pack_small(ln1, ret, ln2, lnf, b_in, sinks, loss, name):
    def body(a_ref, b_ref, c_ref, d_ref, bi_ref, sk_ref, ls_ref, o_ref):
        o_ref[...] = jnp.zeros_like(o_ref)
        for row, ref in enumerate((a_ref, b_ref, c_ref, d_ref)):
            o_ref[row:row + 1, :] = ref[...]
        for k in range(6):
            o_ref[4 + k:5 + k, :] = bi_ref[:, k * 1024:(k + 1) * 1024]
        o_ref[10:11, 0:D_IN - 6144] = bi_ref[:, 6144:D_IN]
        o_ref[11:12, 0:ATTN_Q_HEADS] = sk_ref[...]
        o_ref[12:13, 0:LANES] = ls_ref[...]

    return pl.pallas_call(
        body, name=name, in_specs=[_VMEM] * 7, out_specs=_VMEM,
        out_shape=jax.ShapeDtypeStruct((SMALL_ROWS, 1024), F32),
    )(ln1.reshape(1, 1024), ret.reshape(1, 1024), ln2.reshape(1, 1024), lnf.reshape(1, 1024),
      b_in.reshape(1, D_IN), sinks.reshape(1, ATTN_Q_HEADS), loss.reshape(1, LANES))


def _unpack_small(p):
    return dict(ln1_g=p[0:1], ret_norm_g=p[1:2], ln2_g=p[2:3], lnf_g=p[3], b_in=p[4:11].reshape(1, -1)[:, :D_IN],
                attn_sinks=p[11:12, :ATTN_Q_HEADS], loss=p[12, 0])


def kernel(x, ln1_g, w_in, b_in, ret_norm_g, w_ret_out, attn_sinks, w_attn_out, w_out, ln2_g, w_ffn_gate, w_ffn_up, w_ffn_down, lnf_g, loss_target, m_ln1_g, m_w_in, m_b_in, m_ret_norm_g, m_w_ret_out, m_attn_sinks, m_w_attn_out, m_w_out, m_ln2_g, m_w_ffn_gate, m_w_ffn_up, m_w_ffn_down, m_lnf_g, v_ln1_g, v_w_in, v_b_in, v_ret_norm_g, v_w_ret_out, v_attn_sinks, v_w_attn_out, v_w_out, v_ln2_g, v_w_ffn_gate, v_w_ffn_up, v_w_ffn_down, v_lnf_g):
    xi = lax.axis_index("x")
    yi = lax.axis_index("y")
    ci = lax.axis_index("c")
    j = 2 * xi + yi

    place = jnp.stack([j, ci]).astype(jnp.int32)

    names = ["w_in", "w_ret_out", "w_attn_out", "w_out", "w_ffn_gate", "w_ffn_up", "w_ffn_down"]
    transposed = [True, False, False, False, True, True, False]
    view = lambda a, t: a[0].T if t else a[0]
    big_w = [view(a, t) for a, t in zip([w_in, w_ret_out, w_attn_out, w_out, w_ffn_gate, w_ffn_up, w_ffn_down], transposed)]
    big_m = [view(a, t) for a, t in zip([m_w_in, m_w_ret_out, m_w_attn_out, m_w_out, m_w_ffn_gate, m_w_ffn_up, m_w_ffn_down], transposed)]
    big_v = [view(a, t) for a, t in zip([v_w_in, v_w_ret_out, v_w_attn_out, v_w_out, v_w_ffn_gate, v_w_ffn_up, v_w_ffn_down], transposed)]

    staged_in = _stage_weight(place, big_w[0], "stage_" + names[0])
    sems, in_flight, token = _exchange_start("gather_in_start", [staged_in], 3, _plan_gather_halves(0))
    staged = [_stage_weight(place, w, "stage_" + n, after=(token,)) for n, w in zip(names[1:], big_w[1:])]
    tables = _tables(x.shape[1], token)
    decays = _decays()
    w_in_full = _exchange_wait("gather_in_wait", sems, in_flight, _plan_gather_halves(0),
                               after=(*staged, *tables, *decays))[0]
    w_in_full = _exchange("gather_in_forward", [w_in_full], 3, _plan_forward_halves(0))[0]
    rest = list(range(1, 7))
    sems, in_flight, token = _exchange_start("gather_start", staged, 3 * len(rest), _plan_gather_direct(rest),
                                             after=(w_in_full,))
    other_weights = lambda after: _exchange_wait("gather_wait", sems, in_flight, _plan_gather_direct(rest),
                                                 after=(after,))

    grad_x, early_shard, (l32, l16), small = _local_step(
        place, x[0], loss_target[0], ln1_g, b_in + token[0, 0], ret_norm_g, attn_sinks, ln2_g, lnf_g.reshape(1, -1),
        w_in_full, other_weights, tables, decays)
    pg = _all_reduce_small(_pack_small(
        small["ln1_g"], small["ret_norm_g"], small["ln2_g"], small["lnf_g"], small["b_in"],
        small["attn_sinks"], small["loss"], "pack_small_grads"), l32)
    early_shard = _exchange("share_early", [early_shard], 1, _plan_share_halves(EARLY))[0]
    _, got = _exchange("pair_exchange_late", [l16, _empty(N_CHIPS * LATE.half, BF)], N_CHIPS, _plan_pair_exchange(LATE))
    s16 = _pair_sum(place, l32, got, LATE)
    sems, flight, token = _exchange_start("chips_start_late", [s16, _empty(3 * LATE.half, BF)], 3, _plan_to_chips(LATE),
                                          after=(pg,))

    upd = {}

    def update(t, shard, after):
        res = _adamw(big_w[t], shard, big_m[t], big_v[t], "adamw_" + names[t], g_row0=GROUP_OF[t].off(t), after=after)
        upd[names[t]] = tuple((r.T if transposed[t] else r)[None] for r in res)
        return res[0]

    done = [update(t, early_shard, token) for t in EARLY.tensors]

    sm = _unpack_small(pg)
    z = jnp.zeros((1, LANES), F32)
    pw = _pack_small(ln1_g, ret_norm_g, ln2_g, lnf_g, b_in, attn_sinks, z, "pack_small_w")
    pm = _pack_small(m_ln1_g, m_ret_norm_g, m_ln2_g, m_lnf_g, m_b_in, m_attn_sinks, z, "pack_small_m")
    pv = _pack_small(v_ln1_g, v_ret_norm_g, v_ln2_g, v_lnf_g, v_b_in, v_attn_sinks, z, "pack_small_v")
    small_res = _adamw(pw, pg, pm, pv, "adamw_small", after=token)
    sd, smn, svn = (_unpack_small(t) for t in small_res[1:])
    for n in ["ln1_g", "b_in", "ret_norm_g", "attn_sinks", "ln2_g", "lnf_g"]:
        upd[n] = (sm[n], sd[n], smn[n], svn[n])

    _, others = _exchange_wait("chips_wait_late", sems, flight, _plan_to_chips(LATE), after=(*done, small_res[1]))
    late_shard = _exchange("share_late", [_chip_sum(place, l32, got, others, LATE)], 1, _plan_share_halves(LATE))[0]
    update(0, late_shard, None)

    order_w = ["ln1_g", "w_in", "b_in", "ret_norm_g", "w_ret_out", "attn_sinks", "w_attn_out", "w_out",
               "ln2_g", "w_ffn_gate", "w_ffn_up", "w_ffn_down", "lnf_g"]
    outs = [sm["loss"], grad_x[None]]
    for k in range(4):
        outs += [upd[n][k] for n in order_w]
    return tuple(outs)
```

```python
from typing import NamedTuple

import jax
import jax.numpy as jnp
from jax import lax
from jax.experimental import pallas as pl
from jax.experimental.pallas import tpu as pltpu

D_MODEL = 1024
SEQ_CHUNK = 128
RET_CHUNK = 256
RET_CHUNKS_PER_STEP = 2
RET_HEADS = 4
RET_QK_DIM = 128
RET_V_DIM = 256
ATTN_Q_HEADS = 16
ATTN_HEAD_DIM = 64
D_FF = 2816
D_IN = 6400
ROPE_THETA = 10000.0
EPS = 1e-6
N_CHIPS = 4

ADAM_LR = 0.001
ADAM_B1 = 0.9
ADAM_B2 = 0.999
ADAM_EPS = 1e-08
ADAM_WD = 0.01
ADAM_STEP = 10

OFF_RQ, OFF_RK, OFF_RV, OFF_RG = 0, 512, 1024, 2048
OFF_AQ, OFF_AK, OFF_AV, OFF_GA, OFF_GB = 3072, 4096, 4224, 4352, 5376

V7X_VMEM_LIMIT = 56 * 1024 * 1024
LANES = 128

BF = jnp.bfloat16
F32 = jnp.float32
MESH = pl.DeviceIdType.MESH

_VMEM = pl.BlockSpec(memory_space=pltpu.VMEM)
_SMEM = pl.BlockSpec(memory_space=pltpu.SMEM)
_ANY = pl.BlockSpec(memory_space=pl.ANY)


def _dot(a, b):
    return jnp.dot(a, b, preferred_element_type=F32)


def _dot_nt(a, b):
    return lax.dot_general(a, b, (((1,), (1,)), ((), ())), preferred_element_type=F32)


def _dot_tn(a, b):
    return lax.dot_general(a, b, (((0,), (0,)), ((), ())), preferred_element_type=F32)


def _sigmoid(x):
    return 0.5 * jnp.tanh(0.5 * x) + 0.5


def _rows(tm, n):
    return pl.BlockSpec((tm, n), lambda i: (i, 0))


def _fixed(shape):
    nd = len(shape)
    return pl.BlockSpec(shape, lambda i: (0,) * nd)


def _params(sem, **kw):
    return pltpu.CompilerParams(dimension_semantics=sem, vmem_limit_bytes=V7X_VMEM_LIMIT, **kw)


def _rot_ret(p, cr, sr):
    return p * cr + pltpu.roll(p, 64, 1) * sr


def _rot_ret_bwd(d, cr, sr):
    return d * cr - pltpu.roll(d, 64, 1) * sr


def _attn_partner(p):
    lane = lax.broadcasted_iota(jnp.int32, p.shape, 1)
    first = (lane % 64) < 32
    return jnp.where(first, pltpu.roll(p, 96, 1), pltpu.roll(p, 32, 1))


def _rot_attn(p, ca, sa):
    return p * ca + _attn_partner(p) * sa


def _rot_attn_bwd(d, ca, sa):
    return d * ca - _attn_partner(d) * sa


def _proj_fwd(x, g1, w_in, b_in, cr, sr, ca, sa):
    S = x.shape[0]
    tm = 512
    scale = RET_QK_DIM ** -0.5

    def body(x_ref, g_ref, w_ref, b_ref, cr_ref, sr_ref, ca_ref, sa_ref,
             h_ref, rq_ref, rk_ref, rv_ref, rg_ref, aq_ref, ak_ref, av_ref, ga_ref, gb_ref):
        xv = x_ref[...]
        r = lax.rsqrt(jnp.mean(xv * xv, axis=-1, keepdims=True) + EPS)
        h = (xv * r * g_ref[...]).astype(BF)
        h_ref[...] = h

        def piece(lo, n):
            return _dot_nt(h, w_ref[lo:lo + n, :]) + b_ref[:, lo:lo + n]

        crv, srv = cr_ref[...], sr_ref[...]
        pq = piece(OFF_RQ, 512)
        pk = piece(OFF_RK, 512)
        for hd in range(RET_HEADS):
            sl = slice(hd * 128, (hd + 1) * 128)
            rq_ref[:, sl] = (_rot_ret(pq[:, sl], crv, srv) * scale).astype(BF)
            rk_ref[:, sl] = _rot_ret(pk[:, sl], crv, srv)
        rv_ref[...] = piece(OFF_RV, 1024).astype(BF)
        rg_ref[...] = piece(OFF_RG, 1024)
        cav, sav = ca_ref[...], sa_ref[...]
        pa = piece(OFF_AQ, 1024)
        for blk in range(8):
            sl = slice(blk * 128, (blk + 1) * 128)
            aq_ref[:, sl] = _rot_attn(pa[:, sl], cav, sav).astype(BF)
        pkv = piece(OFF_AK, 256)
        ak_ref[...] = _rot_attn(pkv[:, :128], cav, sav).astype(BF)
        av_ref[...] = pkv[:, 128:].astype(BF)
        ga_ref[...] = piece(OFF_GA, 1024)
        gb_ref[...] = piece(OFF_GB, 1024)

    outs = [(D_MODEL, BF), (512, BF), (512, F32), (1024, BF), (1024, F32),
            (1024, BF), (128, BF), (128, BF), (1024, F32), (1024, F32)]
    return pl.pallas_call(
        body, name="proj_fwd", grid=(S // tm,),
        in_specs=[_rows(tm, D_MODEL), _fixed((1, D_MODEL)), _VMEM, _VMEM,
                  _rows(tm, 128), _rows(tm, 128), _rows(tm, 128), _rows(tm, 128)],
        out_specs=[_rows(tm, n) for n, _ in outs],
        out_shape=[jax.ShapeDtypeStruct((S, n), dt) for n, dt in outs],
        compiler_params=_params(("parallel",)),
    )(x, g1, w_in, b_in, cr, sr, ca, sa)


def _ret_fwd(rq, rk, rv, dmat, qd, kd, cd):
    S = rq.shape[0]
    N = S // RET_CHUNK
    cps = RET_CHUNKS_PER_STEP
    rows = cps * RET_CHUNK

    def body(q_ref, k_ref, v_ref, dm_ref, qd_ref, kd_ref, cd_ref, y_ref, st_ref, state):
        @pl.when(pl.program_id(0) == 0)
        def _():
            state[...] = jnp.zeros_like(state)

        for hd in range(RET_HEADS):
            qs = slice(hd * 128, (hd + 1) * 128)
            vs = slice(hd * 256, (hd + 1) * 256)
            free = []
            for ch in range(cps):
                rs = slice(ch * RET_CHUNK, (ch + 1) * RET_CHUNK)
                q = q_ref[rs, qs]
                kf = k_ref[rs, qs]
                v = v_ref[rs, vs]
                a = _dot_nt(q, kf.astype(BF)) * dm_ref[hd]
                kdv = (kf * kd_ref[hd]).astype(BF)
                free.append((rs, q, _dot(a.astype(BF), v), _dot_tn(kdv, v)))
            s = state[hd]
            for ch, (rs, q, inner, update) in enumerate(free):
                sb = s.astype(BF)
                st_ref[ch, hd] = sb
                y_ref[rs, vs] = inner + _dot(q, sb) * qd_ref[hd]
                s = s * cd_ref[hd] + update
            state[hd] = s

    return pl.pallas_call(
        body, name="ret_fwd", grid=(N // cps,),
        in_specs=[_rows(rows, 512), _rows(rows, 512), _rows(rows, 1024), _VMEM, _VMEM, _VMEM, _VMEM],
        out_specs=[_rows(rows, 1024), pl.BlockSpec((cps, RET_HEADS, 128, 256), lambda i: (i, 0, 0, 0))],
        out_shape=[jax.ShapeDtypeStruct((S, 1024), F32),
                   jax.ShapeDtypeStruct((N, RET_HEADS, 128, 256), BF)],
        scratch_shapes=[pltpu.VMEM((RET_HEADS, 128, 256), F32)],
        compiler_params=_params(("arbitrary",)),
    )(rq, rk, rv, dmat, qd, kd, cd)


def _kv_halves(prev_ref, cur_ref):
    kk = jnp.concatenate([prev_ref[...], cur_ref[...]], axis=0).astype(F32)
    return kk, pltpu.roll(kk, 64, 1)


PAIRS = 4


def _attn_fold(n):
    t = lax.broadcasted_iota(jnp.int32, (128, PAIRS * 128), 0)
    qi = lax.broadcasted_iota(jnp.int32, (128, PAIRS * 128), 1) & 127
    cur = t <= qi
    return cur, jnp.logical_or(cur, n > 0)


def _group_operands(g, kk, kk_sw, vv, vv_sw):
    lo = lax.broadcasted_iota(jnp.int32, (256, 128), 1) < 64
    k_a, k_b = (kk, kk_sw) if g == 0 else (kk_sw, kk)
    v_a, v_b = (vv, vv_sw) if g == 0 else (vv_sw, vv)
    return (jnp.where(lo, k_a, 0.0).astype(BF), jnp.where(lo, 0.0, k_b).astype(BF),
            jnp.where(lo, v_a, 0.0).astype(BF), jnp.where(lo, 0.0, v_b).astype(BF))


def _stack_pairs(ref, g):
    return jnp.concatenate([ref[:, (4 * g + p) * 128:(4 * g + p + 1) * 128] for p in range(PAIRS)], axis=0)


def _sink_row(sk_ref, g, which):
    return jnp.concatenate([jnp.full((1, 128), sk_ref[0, 8 * g + 2 * p + which], F32) for p in range(PAIRS)], axis=1)


def _fold(full, cur):
    return jnp.where(cur, full[128:], full[:128])


def _attn_probs(raw, fold, sink):
    cur, valid = fold
    s = _fold(raw, cur) * (ATTN_HEAD_DIM ** -0.5)
    s = jnp.where(valid, s, -1e30)
    m = jnp.maximum(jnp.max(s, axis=0, keepdims=True), sink)
    e = jnp.exp(s - m)
    es = jnp.exp(sink - m)
    inv = 1.0 / (jnp.sum(e, axis=0, keepdims=True) + es)
    return e * inv, es * inv


def _unfold(p, cur):
    return jnp.concatenate([jnp.where(cur, 0.0, p), jnp.where(cur, p, 0.0)], axis=0).astype(BF)


def _attn_fwd(aq, ak, av, sinks):
    S = aq.shape[0]
    N = S // SEQ_CHUNK

    def body(sk_ref, q_ref, kp_ref, kc_ref, vp_ref, vc_ref, o_ref):
        kk, kk_sw = _kv_halves(kp_ref, kc_ref)
        vv, vv_sw = _kv_halves(vp_ref, vc_ref)
        fold = _attn_fold(pl.program_id(0))
        ops = [_group_operands(g, kk, kk_sw, vv, vv_sw) for g in range(2)]
        raw = [[_dot_nt(kx, _stack_pairs(q_ref, g)) for kx in ops[g][:2]] for g in range(2)]
        for g in range(2):
            o = jnp.zeros((PAIRS * 128, 128), F32)
            for which, vx in enumerate(ops[g][2:]):
                pr, _ = _attn_probs(raw[g][which], fold, _sink_row(sk_ref, g, which))
                o = o + _dot_tn(_unfold(pr, fold[0]), vx)
            for p in range(PAIRS):
                o_ref[:, (4 * g + p) * 128:(4 * g + p + 1) * 128] = o[p * 128:(p + 1) * 128].astype(BF)

    prev = lambda i: (jnp.maximum(i - 1, 0), 0)
    return pl.pallas_call(
        body, name="attn_fwd", grid=(N,),
        in_specs=[_SMEM, _rows(128, 1024), pl.BlockSpec((128, 128), prev), _rows(128, 128),
                  pl.BlockSpec((128, 128), prev), _rows(128, 128)],
        out_specs=_rows(128, 1024),
        out_shape=jax.ShapeDtypeStruct((S, 1024), BF),
        compiler_params=_params(("parallel",)),
    )(sinks, aq, ak, ak, av, av)


def _group_norm(y):
    mu = jnp.mean(y, axis=-1, keepdims=True)
    yc = y - mu
    rstd = lax.rsqrt(jnp.mean(yc * yc, axis=-1, keepdims=True) + EPS)
    return yc * rstd, rstd


def _merge_fwd(x, ry, rg, ay, ga, gb, g_ret, w_ret, w_attn, w_out):
    S = x.shape[0]
    tm = 256

    def body(x_ref, ry_ref, rg_ref, ay_ref, ga_ref, gb_ref, gr_ref, wr_ref, wa_ref, wo_ref,
             u_ref, mg_ref, x1_ref):
        bb = _dot(ay_ref[...], wa_ref[...])
        for hd in range(RET_HEADS):
            vs = slice(hd * 256, (hd + 1) * 256)
            yn, _ = _group_norm(ry_ref[:, vs])
            rgv = rg_ref[:, vs]
            u_ref[:, vs] = (rgv * _sigmoid(rgv) * (yn * gr_ref[:, vs])).astype(BF)
        gated_b = _sigmoid(gb_ref[...]) * bb
        sa = _sigmoid(ga_ref[...])
        ba = _dot(u_ref[...], wr_ref[...])
        merged = (sa * ba + gated_b).astype(BF)
        mg_ref[...] = merged
        x1_ref[...] = x_ref[...] + _dot(merged, wo_ref[...])

    outs = [BF, BF, F32]
    return pl.pallas_call(
        body, name="merge_fwd", grid=(S // tm,),
        in_specs=[_rows(tm, 1024)] * 6 + [_fixed((1, 1024)), _VMEM, _VMEM, _VMEM],
        out_specs=[_rows(tm, 1024)] * 3,
        out_shape=[jax.ShapeDtypeStruct((S, 1024), dt) for dt in outs],
        compiler_params=_params(("parallel",)),
    )(x, ry, rg, ay, ga, gb, g_ret, w_ret, w_attn, w_out)


def _ffn_fwd(x1, tgt, g2, g3, w_gate, w_up, w_down):
    S = x1.shape[0]
    tm = 256

    def body(x1_ref, t_ref, g2_ref, g3_ref, wg_ref, wu_ref, wd_ref,
             gate_ref, up_ref, h2_ref, act_ref, dx2_ref, loss_ref, dg3_ref):
        @pl.when(pl.program_id(0) == 0)
        def _():
            loss_ref[...] = jnp.zeros_like(loss_ref)
            dg3_ref[...] = jnp.zeros_like(dg3_ref)

        x1v = x1_ref[...]
        r2 = lax.rsqrt(jnp.mean(x1v * x1v, axis=-1, keepdims=True) + EPS)
        h2 = (x1v * r2 * g2_ref[...]).astype(BF)
        h2_ref[...] = h2
        gate = _dot_nt(h2, wg_ref[...])
        up = _dot_nt(h2, wu_ref[...])
        gate_ref[...] = gate
        up_ref[...] = up
        act = (gate * _sigmoid(gate) * up).astype(BF)
        act_ref[...] = act
        x2 = x1v + _dot(act, wd_ref[...])
        r3 = lax.rsqrt(jnp.mean(x2 * x2, axis=-1, keepdims=True) + EPS)
        xh = x2 * r3
        g3v = g3_ref[...]
        err = xh * g3v - t_ref[...]
        loss_ref[...] += 0.5 * jnp.sum(jnp.mean(err * err, axis=-1, keepdims=True), axis=0, keepdims=True)
        dy = err * (1.0 / D_MODEL)
        dg3_ref[...] += jnp.sum(dy * xh, axis=0, keepdims=True)
        dxh = dy * g3v
        dx2_ref[...] = r3 * (dxh - xh * jnp.mean(dxh * xh, axis=-1, keepdims=True))

    return pl.pallas_call(
        body, name="ffn_fwd", grid=(S // tm,),
        in_specs=[_rows(tm, 1024), _rows(tm, 1024), _fixed((1, 1024)), _fixed((1, 1024)), _VMEM, _VMEM, _VMEM],
        out_specs=[_rows(tm, D_FF), _rows(tm, D_FF), _rows(tm, 1024), _rows(tm, D_FF), _rows(tm, 1024),
                   _fixed((1, LANES)), _fixed((1, 1024))],
        out_shape=[jax.ShapeDtypeStruct((S, D_FF), F32), jax.ShapeDtypeStruct((S, D_FF), F32),
                   jax.ShapeDtypeStruct((S, 1024), BF), jax.ShapeDtypeStruct((S, D_FF), BF),
                   jax.ShapeDtypeStruct((S, 1024), F32),
                   jax.ShapeDtypeStruct((1, LANES), F32), jax.ShapeDtypeStruct((1, 1024), F32)],
        compiler_params=_params(("arbitrary",)),
    )(x1, tgt, g2, g3, w_gate, w_up, w_down)


def _ffn_bwd(dx2, gate, up, x1, g2, w_gate, w_up, w_down):
    S = x1.shape[0]
    tm = 256

    def body(dx2_ref, gate_ref, up_ref, x1_ref, g2_ref, wg_ref, wu_ref, wd_ref,
             dgate_ref, dup_ref, dx1_ref, dg2_ref):
        @pl.when(pl.program_id(0) == 0)
        def _():
            dg2_ref[...] = jnp.zeros_like(dg2_ref)

        dx2v = dx2_ref[...]
        dact = _dot_nt(dx2v.astype(BF), wd_ref[...])
        gate = gate_ref[...]
        sg = _sigmoid(gate)
        dgate = (dact * up_ref[...] * (sg * (1.0 + gate * (1.0 - sg)))).astype(BF)
        dup = (dact * (gate * sg)).astype(BF)
        dgate_ref[...] = dgate
        dup_ref[...] = dup
        dh2 = _dot(dgate, wg_ref[...]) + _dot(dup, wu_ref[...])
        x1v = x1_ref[...]
        r2 = lax.rsqrt(jnp.mean(x1v * x1v, axis=-1, keepdims=True) + EPS)
        xh = x1v * r2
        dg2_ref[...] += jnp.sum(dh2 * xh, axis=0, keepdims=True)
        dxh = dh2 * g2_ref[...]
        dx1_ref[...] = dx2v + r2 * (dxh - xh * jnp.mean(dxh * xh, axis=-1, keepdims=True))

    return pl.pallas_call(
        body, name="ffn_bwd", grid=(S // tm,),
        in_specs=[_rows(tm, 1024), _rows(tm, D_FF), _rows(tm, D_FF), _rows(tm, 1024), _fixed((1, 1024)),
                  _VMEM, _VMEM, _VMEM],
        out_specs=[_rows(tm, D_FF), _rows(tm, D_FF), _rows(tm, 1024), _fixed((1, 1024))],
        out_shape=[jax.ShapeDtypeStruct((S, D_FF), BF), jax.ShapeDtypeStruct((S, D_FF), BF),
                   jax.ShapeDtypeStruct((S, 1024), F32), jax.ShapeDtypeStruct((1, 1024), F32)],
        compiler_params=_params(("arbitrary",)),
    )(dx2, gate, up, x1, g2, w_gate, w_up, w_down)


def _merge_bwd(dx1, ga, gb, u, ay, rg, ry, g_ret, w_ret, w_attn, w_out):
    S = dx1.shape[0]
    tm = 256

    def body(dx1_ref, ga_ref, gb_ref, u_ref, ay_ref, rg_ref, ry_ref, gr_ref, wr_ref, wa_ref, wo_ref,
             dgates_ref, drg_ref, dba_ref, dbb_ref, day_ref, dry_ref, dgr_ref):
        @pl.when(pl.program_id(0) == 0)
        def _():
            dgr_ref[...] = jnp.zeros_like(dgr_ref)

        dm = _dot_nt(dx1_ref[...].astype(BF), wo_ref[...])
        ba = _dot(u_ref[...], wr_ref[...])
        bb = _dot(ay_ref[...], wa_ref[...])
        sa = _sigmoid(ga_ref[...])
        sb = _sigmoid(gb_ref[...])
        dga = (dm * ba * (sa * (1.0 - sa))).astype(BF)
        dgb = (dm * bb * (sb * (1.0 - sb))).astype(BF)
        dgates_ref[:, :1024] = dga
        dgates_ref[:, 1024:] = dgb
        dba = (dm * sa).astype(BF)
        dbb = (dm * sb).astype(BF)
        dba_ref[...] = dba
        dbb_ref[...] = dbb
        day_ref[...] = _dot_nt(dbb, wa_ref[...]).astype(BF)
        du = _dot_nt(dba, wr_ref[...])
        for hd in range(RET_HEADS):
            vs = slice(hd * 256, (hd + 1) * 256)
            yn, rstd = _group_norm(ry_ref[:, vs])
            rgv = rg_ref[:, vs]
            sr = _sigmoid(rgv)
            silu = rgv * sr
            duv = du[:, vs]
            grv = gr_ref[:, vs]
            drg = (duv * (yn * grv) * (sr * (1.0 + rgv * (1.0 - sr)))).astype(BF)
            drg_ref[:, vs] = drg
            dyg = duv * silu
            dgr_ref[:, vs] += jnp.sum(dyg * yn, axis=0, keepdims=True)
            dyn = dyg * grv
            dry_ref[:, vs] = rstd * (dyn - jnp.mean(dyn, axis=-1, keepdims=True)
                                     - yn * jnp.mean(dyn * yn, axis=-1, keepdims=True))

    return pl.pallas_call(
        body, name="merge_bwd", grid=(S // tm,),
        in_specs=[_rows(tm, 1024)] * 7 + [_fixed((1, 1024)), _VMEM, _VMEM, _VMEM],
        out_specs=[_rows(tm, 2048), _rows(tm, 1024), _rows(tm, 1024), _rows(tm, 1024), _rows(tm, 1024),
                   _rows(tm, 1024), _fixed((1, 1024))],
        out_shape=[jax.ShapeDtypeStruct((S, 2048), BF), jax.ShapeDtypeStruct((S, 1024), BF),
                   jax.ShapeDtypeStruct((S, 1024), BF), jax.ShapeDtypeStruct((S, 1024), BF),
                   jax.ShapeDtypeStruct((S, 1024), BF), jax.ShapeDtypeStruct((S, 1024), F32),
                   jax.ShapeDtypeStruct((1, 1024), F32)],
        compiler_params=_params(("arbitrary",)),
    )(dx1, ga, gb, u, ay, rg, ry, g_ret, w_ret, w_attn, w_out)


def _attn_bwd(aq, ak, av, day, sinks, ca, sa, after):
    S = aq.shape[0]
    N = S // SEQ_CHUNK

    def body(sk_ref, q_ref, kp_ref, kc_ref, vp_ref, vc_ref, do_ref, cq_ref, sq_ref, cp_ref, sp_ref, after_ref,
             dq_ref, dk_ref, dv_ref, dsk_ref, carry_k, carry_v):
        n = pl.program_id(0)

        @pl.when(n == 0)
        def _():
            carry_k[...] = jnp.zeros_like(carry_k)
            carry_v[...] = jnp.zeros_like(carry_v)
            dsk_ref[...] = jnp.zeros_like(dsk_ref)

        @pl.when(n < N)
        def _():
            kk, kk_sw = _kv_halves(kp_ref, kc_ref)
            vv, vv_sw = _kv_halves(vp_ref, vc_ref)
            lo = lax.broadcasted_iota(jnp.int32, (256, 128), 1) < 64
            fold = _attn_fold(n)
            cur = fold[0]
            cq, sq = cq_ref[...], sq_ref[...]
            dk_groups, dv_groups = [], []
            ops = [_group_operands(g, kk, kk_sw, vv, vv_sw) for g in range(2)]
            stacked = [(_stack_pairs(q_ref, g), _stack_pairs(do_ref, g)) for g in range(2)]
            raw = [[_dot_nt(kx, stacked[g][0]) for kx in ops[g][:2]] for g in range(2)]
            raw_dp = [[_dot_nt(vx, stacked[g][1]) for vx in ops[g][2:]] for g in range(2)]
            for g in range(2):
                qs, dos = stacked[g]
                dq = jnp.zeros((PAIRS * 128, 128), F32)
                dk_half, dv_half = [], []
                for which, kx in enumerate(ops[g][:2]):
                    pr, ps = _attn_probs(raw[g][which], fold, _sink_row(sk_ref, g, which))
                    dp = _fold(raw_dp[g][which], cur)
                    dr = jnp.sum(pr * dp, axis=0, keepdims=True)
                    ds = _unfold(pr * (dp - dr) * (ATTN_HEAD_DIM ** -0.5), cur)
                    dsk_ref[2 * g + which:2 * g + which + 1, :] -= ps * dr
                    dq = dq + _dot_tn(ds, kx)
                    dk_half.append(_dot(ds, qs))
                    dv_half.append(_dot(_unfold(pr, cur), dos))
                for p in range(PAIRS):
                    sl = slice((4 * g + p) * 128, (4 * g + p + 1) * 128)
                    dq_ref[:, sl] = _rot_attn_bwd(dq[p * 128:(p + 1) * 128], cq, sq).astype(BF)
                tk = jnp.where(lo, dk_half[0], dk_half[1])
                tv = jnp.where(lo, dv_half[0], dv_half[1])
                dk_groups.append(tk + pltpu.roll(tk, 64, 1))
                dv_groups.append(tv + pltpu.roll(tv, 64, 1))
            dk_full = jnp.where(lo, dk_groups[0], dk_groups[1])
            dv_full = jnp.where(lo, dv_groups[0], dv_groups[1])
            dk_ref[...] = _rot_attn_bwd(carry_k[...] + dk_full[:128], cp_ref[...], sp_ref[...]).astype(BF)
            dv_ref[...] = (carry_v[...] + dv_full[:128]).astype(BF)
            carry_k[...] = dk_full[128:]
            carry_v[...] = dv_full[128:]

        @pl.when(n == N)
        def _():
            dk_ref[...] = _rot_attn_bwd(carry_k[...], cp_ref[...], sp_ref[...]).astype(BF)
            dv_ref[...] = carry_v[...].astype(BF)

    cur = lambda i: (jnp.minimum(i, N - 1), 0)
    prev = lambda i: (jnp.maximum(i - 1, 0), 0)
    blk = lambda w, im: pl.BlockSpec((128, w), im)
    return pl.pallas_call(
        body, name="attn_bwd", grid=(N + 1,),
        in_specs=[_SMEM, blk(1024, cur), blk(128, prev), blk(128, cur), blk(128, prev), blk(128, cur),
                  blk(1024, cur), blk(128, cur), blk(128, cur), blk(128, prev), blk(128, prev), _ANY],
        out_specs=[blk(1024, cur), blk(128, prev), blk(128, prev), _fixed((4, PAIRS * 128))],
        out_shape=[jax.ShapeDtypeStruct((S, 1024), BF), jax.ShapeDtypeStruct((S, 128), BF),
                   jax.ShapeDtypeStruct((S, 128), BF), jax.ShapeDtypeStruct((4, PAIRS * 128), F32)],
        scratch_shapes=[pltpu.VMEM((128, 128), F32), pltpu.VMEM((128, 128), F32)],
        compiler_params=_params(("arbitrary",)),
    )(sinks, aq, ak, ak, av, av, day, ca, sa, ca, sa, after)


def _ret_bwd(rq, rk, rv, dry, states, dmat, qd, kd, cd, cr, sr, after):
    S = rq.shape[0]
    N = S // RET_CHUNK
    scale = RET_QK_DIM ** -0.5
    cps = RET_CHUNKS_PER_STEP
    steps = N // cps

    def body(q_ref, k_ref, v_ref, dy_ref, st_ref, dm_ref, qd_ref, kd_ref, cd_ref, cr_ref, sr_ref, after_ref,
             out_ref, dstate):
        @pl.when(pl.program_id(0) == 0)
        def _():
            dstate[...] = jnp.zeros_like(dstate)

        for hd in range(RET_HEADS):
            qs = slice(hd * 128, (hd + 1) * 128)
            vs = slice(hd * 256, (hd + 1) * 256)
            dmv = dm_ref[hd]
            free = []
            for ch in range(cps):
                rs = slice(ch * RET_CHUNK, (ch + 1) * RET_CHUNK)
                crv, srv = cr_ref[rs, :], sr_ref[rs, :]
                q = q_ref[rs, qs]
                kf = k_ref[rs, qs]
                k = kf.astype(BF)
                v = v_ref[rs, vs]
                do = dy_ref[rs, vs]
                dob = do.astype(BF)
                doq = (do * qd_ref[hd]).astype(BF)
                a = (_dot_nt(q, k) * dmv).astype(BF)
                da = (_dot_nt(dob, v) * dmv).astype(BF)
                dq = _dot(da, k) + _dot_nt(doq, st_ref[ch, hd])
                out_ref[rs, qs] = _rot_ret_bwd(dq * scale, crv, srv).astype(BF)
                free.append((rs, crv, srv, v, (kf * kd_ref[hd]).astype(BF),
                             _dot_tn(a, dob), _dot_tn(da, q), _dot_tn(q, doq)))
            ds = dstate[hd]
            for rs, crv, srv, v, kdv, dv_free, dk_free, ds_update in reversed(free):
                dsb = ds.astype(BF)
                dv = dv_free + _dot(kdv, dsb)
                dk = dk_free + _dot_nt(v, dsb) * kd_ref[hd]
                ds = ds * cd_ref[hd] + ds_update
                out_ref[rs, 512 + hd * 128:512 + (hd + 1) * 128] = _rot_ret_bwd(dk, crv, srv).astype(BF)
                out_ref[rs, 1024 + hd * 256:1024 + (hd + 1) * 256] = dv.astype(BF)
            dstate[hd] = ds

    rev = lambda w: pl.BlockSpec((cps * RET_CHUNK, w), lambda i: (steps - 1 - i, 0))
    return pl.pallas_call(
        body, name="ret_bwd", grid=(steps,),
        in_specs=[rev(512), rev(512), rev(1024), rev(1024),
                  pl.BlockSpec((cps, RET_HEADS, 128, 256), lambda i: (steps - 1 - i, 0, 0, 0)),
                  _VMEM, _VMEM, _VMEM, _VMEM, rev(128), rev(128), _ANY],
        out_specs=rev(2048),
        out_shape=jax.ShapeDtypeStruct((S, 2048), BF),
        scratch_shapes=[pltpu.VMEM((RET_HEADS, 128, 256), F32)],
        compiler_params=_params(("arbitrary",)),
    )(rq, rk, rv, dry, states, dmat, qd, kd, cd, cr, sr, after)


def _proj_bwd(d_pieces, x, dx1, g1, w_in_t):
    S = x.shape[0]
    tm = 512
    widths = [d.shape[1] for d in d_pieces]
    assert sum(widths) == D_IN
    nd = len(d_pieces)

    def body(*refs):
        d_refs = refs[:nd]
        x_ref, dx1_ref, g_ref, w_ref, gx_ref, dg_ref, db_ref = refs[nd:]

        @pl.when(pl.program_id(0) == 0)
        def _():
            dg_ref[...] = jnp.zeros_like(dg_ref)
            db_ref[...] = jnp.zeros_like(db_ref)

        dh = jnp.zeros((tm, D_MODEL), F32)
        lo = 0
        for ref, n in zip(d_refs, widths):
            v = ref[...]
            dh = dh + _dot(v, w_ref[lo:lo + n, :])
            db_ref[:, lo:lo + n] += jnp.sum(v.astype(F32), axis=0, keepdims=True)
            lo += n
        xv = x_ref[...]
        r = lax.rsqrt(jnp.mean(xv * xv, axis=-1, keepdims=True) + EPS)
        xh = xv * r
        dg_ref[...] += jnp.sum(dh * xh, axis=0, keepdims=True)
        dxh = dh * g_ref[...]
        gx_ref[...] = dx1_ref[...] + r * (dxh - xh * jnp.mean(dxh * xh, axis=-1, keepdims=True))

    return pl.pallas_call(
        body, name="proj_bwd", grid=(S // tm,),
        in_specs=[_rows(tm, n) for n in widths] + [_rows(tm, 1024), _rows(tm, 1024), _fixed((1, 1024)), _VMEM],
        out_specs=[_rows(tm, 1024), _fixed((1, 1024)), _fixed((1, D_IN))],
        out_shape=[jax.ShapeDtypeStruct((S, 1024), F32), jax.ShapeDtypeStruct((1, 1024), F32),
                   jax.ShapeDtypeStruct((1, D_IN), F32)],
        compiler_params=_params(("arbitrary",)),
    )(*d_pieces, x, dx1, g1, w_in_t)


SHARD_ROWS = (1600, 256, 256, 256, 704, 704, 704)


class Group(NamedTuple):
    name: str
    tensors: tuple
    offs: tuple
    block: int
    tile: int

    @property
    def half(self):
        return self.block // 2

    def off(self, t):
        return self.offs[self.tensors.index(t)]


LATE = Group("late", (0,), (0,), 1600, 800)
EARLY = Group("early", (1, 2, 3, 4, 5, 6), (0, 256, 512, 768, 1472, 2176), 2880, 480)
GROUP_OF = {t: g for g in (LATE, EARLY) for t in g.tensors}


def _wgrad(a_list, b, t, packed, name, row0=0):
    S = b.shape[0]
    widths = [a.shape[1] for a in a_list]
    M = sum(widths)
    ts = 1024
    steps = S // ts
    na = len(a_list)
    grp = GROUP_OF[t]
    r, off = SHARD_ROWS[t], grp.off(t)
    segments = []
    for chip in range(N_CHIPS):
        lo, hi = max(chip * r, row0), min((chip + 1) * r, row0 + M)
        if lo < hi:
            segments.append((lo - row0, hi - lo, chip * grp.block + off + lo - chip * r))
    assert sum(n for _, n, _ in segments) == M
    stage_rows = max(n for _, n, _ in segments)
    n_in = na + 1 + (2 if packed is not None else 0)

    def body(*refs):
        a_refs, b_ref = refs[:na], refs[na]
        g32_ref, gb_ref, acc, stage, sems = refs[n_in:]
        i = pl.program_id(0)

        @pl.when(i == 0)
        def _():
            acc[...] = jnp.zeros_like(acc)

        bv = b_ref[...].astype(BF)
        lo = 0
        for a_ref, n in zip(a_refs, widths):
            acc[lo:lo + n, :] += _dot_tn(a_ref[...].astype(BF), bv)
            lo += n

        @pl.when(i == steps - 1)
        def _():
            for a0, n, p0 in segments:
                c32 = pltpu.make_async_copy(acc.at[a0:a0 + n, :], g32_ref.at[pl.ds(p0, n), :], sems.at[0])
                c32.start()
                stage[0:n, :] = acc[a0:a0 + n, :].astype(BF)
                c16 = pltpu.make_async_copy(stage.at[0:n, :], gb_ref.at[pl.ds(p0, n), :], sems.at[1])
                c16.start()
                c32.wait()
                c16.wait()

    shapes = [jax.ShapeDtypeStruct((N_CHIPS * grp.block, 1024), F32),
              jax.ShapeDtypeStruct((N_CHIPS * grp.block, 1024), BF)]
    operands = list(a_list) + [b]
    in_specs = [_rows(ts, n) for n in widths] + [_rows(ts, 1024)]
    aliases = {}
    if packed is not None:
        operands += list(packed)
        in_specs += [_ANY, _ANY]
        aliases = {na + 1: 0, na + 2: 1}
    return pl.pallas_call(
        body, name=name, grid=(steps,),
        in_specs=in_specs, out_specs=[_ANY, _ANY], out_shape=shapes,
        scratch_shapes=[pltpu.VMEM((M, 1024), F32), pltpu.VMEM((stage_rows, 1024), BF),
                        pltpu.SemaphoreType.DMA((2,))],
        input_output_aliases=aliases,
        compiler_params=_params(("arbitrary",)),
    )(*operands)


def _stage_weight(place, w, name, after=()):
    r = w.shape[0]
    tr = 800 if r == 1600 else r
    nb = r // tr

    def body(p_ref, w_ref, *rest):
        rest[-1][...] = w_ref[...].astype(BF)

    return pl.pallas_call(
        body, name=name,
        grid_spec=pltpu.PrefetchScalarGridSpec(
            num_scalar_prefetch=1, grid=(nb,),
            in_specs=[pl.BlockSpec((tr, 1024), lambda i, p: (i, 0))] + [_ANY] * len(after),
            out_specs=pl.BlockSpec((tr, 1024), lambda i, p: (p[0] * nb + i, 0))),
        out_shape=jax.ShapeDtypeStruct((N_CHIPS * r, 1024), BF),
        compiler_params=_params(("arbitrary",)),
    )(place, w, *after)


def _kept_block(grp, nb):
    return lambda i, p: ((2 * (i // nb) + p[1]) * nb + i % nb, 0)


def _pair_sum(place, g32, got, grp):
    nb = grp.half // grp.tile

    def body(p_ref, g_ref, r_ref, s16_ref):
        s16_ref[...] = (g_ref[...] + r_ref[...].astype(F32)).astype(BF)

    flat = pl.BlockSpec((grp.tile, 1024), lambda i, p: (i, 0))
    return pl.pallas_call(
        body, name="pair_sum_" + grp.name,
        grid_spec=pltpu.PrefetchScalarGridSpec(
            num_scalar_prefetch=1, grid=(N_CHIPS * nb,),
            in_specs=[pl.BlockSpec((grp.tile, 1024), _kept_block(grp, nb)), flat], out_specs=flat),
        out_shape=jax.ShapeDtypeStruct((N_CHIPS * grp.half, 1024), BF),
        compiler_params=_params(("arbitrary",)),
    )(place, g32, got)


def _chip_sum(place, g32, got, others, grp):
    nb = grp.half // grp.tile

    def body(p_ref, g_ref, r_ref, r0_ref, r1_ref, r2_ref, o_ref):
        own = g_ref[...] + r_ref[...].astype(F32)
        o_ref[...] = ((own + r0_ref[...].astype(F32)) + r1_ref[...].astype(F32)) + r2_ref[...].astype(F32)

    blk = lambda f: pl.BlockSpec((grp.tile, 1024), f)
    return pl.pallas_call(
        body, name="chip_sum_" + grp.name,
        grid_spec=pltpu.PrefetchScalarGridSpec(
            num_scalar_prefetch=1, grid=(nb,),
            in_specs=[blk(lambda i, p: ((2 * p[0] + p[1]) * nb + i, 0)), blk(lambda i, p: (p[0] * nb + i, 0)),
                      blk(lambda i, p: (i, 0)), blk(lambda i, p: (nb + i, 0)), blk(lambda i, p: (2 * nb + i, 0))],
            out_specs=blk(lambda i, p: (p[1] * nb + i, 0))),
        out_shape=jax.ShapeDtypeStruct((grp.block, 1024), F32),
        compiler_params=_params(("arbitrary",)),
    )(place, g32, got, others, others, others)


def _adamw(w, g, m, v, name, g_row0=0, after=None):
    r = w.shape[0]
    tr = 320 if r % 320 == 0 and g_row0 % 320 == 0 else (64 if r % 64 == 0 else r)
    assert g_row0 % tr == 0
    g0 = g_row0 // tr
    extra = [] if after is None else [after]

    def body(w_ref, g_ref, m_ref, v_ref, *rest):
        go_ref, d_ref, mo_ref, vo_ref = rest[len(extra):]
        gv = g_ref[...]
        mn = ADAM_B1 * m_ref[...] + (1.0 - ADAM_B1) * gv
        vn = ADAM_B2 * v_ref[...] + (1.0 - ADAM_B2) * (gv * gv)
        m_hat = mn / (1.0 - ADAM_B1 ** ADAM_STEP)
        v_hat = vn / (1.0 - ADAM_B2 ** ADAM_STEP)
        go_ref[...] = gv
        d_ref[...] = -ADAM_LR * (m_hat / (jnp.sqrt(v_hat) + ADAM_EPS) + ADAM_WD * w_ref[...])
        mo_ref[...] = mn
        vo_ref[...] = vn

    return pl.pallas_call(
        body, name=name, grid=(r // tr,),
        in_specs=[_rows(tr, 1024), pl.BlockSpec((tr, 1024), lambda i: (g0 + i, 0)), _rows(tr, 1024), _rows(tr, 1024)]
        + [_ANY] * len(extra),
        out_specs=[_rows(tr, 1024)] * 4,
        out_shape=[jax.ShapeDtypeStruct((r, 1024), F32)] * 4,
        compiler_params=_params(("parallel",)),
    )(w, g, m, v, *extra)


def _place():
    x, y, c = lax.axis_index("x"), lax.axis_index("y"), lax.axis_index("c")
    chips = [(1 - x, y), (x, 1 - y), (1 - x, 1 - y)]
    return x, y, c, chips


_HBM =pl.BlockSpec(memory_space=pltpu.HBM)
_SEM = pl.BlockSpec(memory_space=pltpu.SEMAPHORE)
_DATAFLOW = pltpu.SideEffectType.DATAFLOW_SIDE_EFFECTING


def _exchange(name, arrays, n, plan):
    na = len(arrays)

    def body(*refs):
        pairs = plan(refs[na:2 * na], refs[2 * na], refs[2 * na + 1])
        for send, _ in pairs:
            send.start()
        for send, recv in pairs:
            send.wait_send()
            recv.wait_recv()

    return pl.pallas_call(
        body, name=name, in_specs=[_ANY] * na, out_specs=[_ANY] * na,
        out_shape=[jax.ShapeDtypeStruct(a.shape, a.dtype) for a in arrays],
        scratch_shapes=[pltpu.SemaphoreType.DMA((n,)), pltpu.SemaphoreType.DMA((n,))],
        input_output_aliases={i: i for i in range(na)},
    )(*arrays)


def _exchange_start(name, arrays, n, plan, after=()):
    na = len(arrays)

    def body(*refs):
        for send, _ in plan(refs[:na], refs[na + len(after)], refs[na + len(after) + 1]):
            send.start()
        refs[-1][...] = jnp.zeros_like(refs[-1])

    out = pl.pallas_call(
        body, name=name,
        out_shape=(pltpu.SemaphoreType.DMA((n,)), pltpu.SemaphoreType.DMA((n,)),
                   *[pltpu.HBM(a.shape, a.dtype) for a in arrays], jax.ShapeDtypeStruct((8, LANES), F32)),
        in_specs=[_HBM] * na + [_ANY] * len(after), out_specs=(_SEM, _SEM, *[_HBM] * na, _VMEM),
        input_output_aliases={i: 2 + i for i in range(na)},
        compiler_params=pltpu.CompilerParams(has_side_effects=_DATAFLOW),
    )(*[pltpu.with_memory_space_constraint(a, pltpu.HBM) for a in arrays], *after)
    return (out[0], out[1]), list(out[2:2 + na]), out[-1]


def _exchange_wait(name, sems, in_flight, plan, after=()):
    na = len(in_flight)

    def body(*refs):
        for send, recv in plan(refs[:na], refs[na], refs[na + 1]):
            send.wait_send()
            recv.wait_recv()

    out = pl.pallas_call(
        body, name=name,
        out_shape=tuple(pltpu.HBM(a.shape, a.dtype) for a in in_flight),
        in_specs=[_HBM] * na + [_SEM, _SEM] + [_ANY] * len(after), out_specs=tuple([_HBM] * na),
        input_output_aliases={i: i for i in range(na)},
        compiler_params=pltpu.CompilerParams(has_side_effects=_DATAFLOW),
    )(*in_flight, *sems, *after)
    return list(out)


def _remote(src, dst, send_sems, recv_sems, k, to):
    return pltpu.make_async_remote_copy(src_ref=src, dst_ref=dst, send_sem=send_sems.at[k], recv_sem=recv_sems.at[k],
                                        device_id=to, device_id_type=MESH)


def _plan_gather_direct(tids):
    def plan(refs, ss, rs):
        x, y, c, chips = _place()
        j = 2 * x + y
        pairs = []
        for t, tid in enumerate(tids):
            r = SHARD_ROWS[tid]
            for k, (cx, cy) in enumerate(chips):
                mine = refs[t].at[pl.ds(j * r, r), :]
                theirs = refs[t].at[pl.ds((2 * cx + cy) * r, r), :]
                pairs.append((_remote(mine, mine, ss, rs, 3 * t + k, (cx, cy, c)),
                              _remote(theirs, theirs, ss, rs, 3 * t + k, (x, y, c))))
        return pairs
    return plan


def _half_rows(ref, tid, pj, half):
    h = SHARD_ROWS[tid] // 2
    return ref.at[pl.ds((2 * pj + half) * h, h), :]


def _plan_gather_halves(tid):
    def plan(refs, ss, rs):
        x, y, c, chips = _place()
        mine = _half_rows(refs[0], tid, 2 * x + y, c)
        return [(_remote(mine, mine, ss, rs, k, (cx, cy, c)),
                 _remote(_half_rows(refs[0], tid, 2 * cx + cy, c), _half_rows(refs[0], tid, 2 * cx + cy, c),
                         ss, rs, k, (x, y, c))) for k, (cx, cy) in enumerate(chips)]
    return plan


def _plan_forward_halves(tid):
    def plan(refs, ss, rs):
        x, y, c, chips = _place()
        pairs = []
        for k, (cx, cy) in enumerate(chips):
            got = _half_rows(refs[0], tid, 2 * cx + cy, c)
            missing = _half_rows(refs[0], tid, 2 * cx + cy, 1 - c)
            pairs.append((_remote(got, got, ss, rs, k, (x, y, 1 - c)), _remote(missing, missing, ss, rs, k, (x, y, 1 - c))))
        return pairs
    return plan


def _plan_pair_exchange(grp):
    def plan(refs, ss, rs):
        x, y, c, _ = _place()
        cps = [_remote(refs[0].at[pl.ds(pj * grp.block + (1 - c) * grp.half, grp.half), :],
                       refs[1].at[pl.ds(pj * grp.half, grp.half), :], ss, rs, pj, (x, y, 1 - c))
               for pj in range(N_CHIPS)]
        return [(cp, cp) for cp in cps]
    return plan


def _plan_to_chips(grp):
    def plan(refs, ss, rs):
        x, y, c, chips = _place()
        cps = [_remote(refs[0].at[pl.ds((2 * cx + cy) * grp.half, grp.half), :],
                       refs[1].at[pl.ds(k * grp.half, grp.half), :], ss, rs, k, (cx, cy, c))
               for k, (cx, cy) in enumerate(chips)]
        return [(cp, cp) for cp in cps]
    return plan


def _plan_share_halves(grp):
    def plan(refs, ss, rs):
        x, y, c, _ = _place()
        mine = refs[0].at[pl.ds(c * grp.half, grp.half), :]
        other = refs[0].at[pl.ds((1 - c) * grp.half, grp.half), :]
        return [(_remote(mine, mine, ss, rs, 0, (x, y, 1 - c)), _remote(other, other, ss, rs, 0, (x, y, 1 - c)))]
    return plan


def _empty(rows, dtype):
    return lax.empty((rows, 1024), dtype)


def _all_reduce_small(v, after):
    m_per, n = v.shape
    n_dev = 8

    def body(x_ref, after_ref, out_ref, all_ref, send_sems, recv_sems, local_sem):
        x, y, c, chips = _place()
        me, sibling = (x, y, c), (x, y, 1 - c)

        def rows(px, py, pc):
            return all_ref.at[pl.ds((4 * px + 2 * py + pc) * m_per, m_per), :]

        def copy(k, block, to, src=None):
            return pltpu.make_async_remote_copy(
                src_ref=rows(*block) if src is None else src, dst_ref=rows(*block),
                send_sem=send_sems.at[k], recv_sem=recv_sems.at[k], device_id=to, device_id_type=MESH)

        mine = pltpu.make_async_copy(x_ref, rows(*me), local_sem)
        mine.start()
        first = [copy(0, me, sibling, src=x_ref)]
        first += [copy(1 + k, me, (*chip, c), src=x_ref) for k, chip in enumerate(chips)]
        for cp in first:
            cp.start()
        passed = [copy(4 + k, (*chip, c), sibling) for k, chip in enumerate(chips)]
        for k, chip in enumerate(chips):
            copy(1 + k, (*chip, c), me).wait_recv()
            passed[k].start()
        copy(0, sibling, me).wait_recv()
        for k, chip in enumerate(chips):
            copy(4 + k, (*chip, 1 - c), me).wait_recv()
        for cp in first + passed:
            cp.wait_send()
        mine.wait()
        acc = all_ref[pl.ds(0, m_per), :]
        for d in range(1, n_dev):
            acc = acc + all_ref[pl.ds(d * m_per, m_per), :]
        out_ref[...] = acc

    return pl.pallas_call(
        body, name="all_reduce_small",
        in_specs=[_VMEM, _ANY], out_specs=_VMEM,
        out_shape=jax.ShapeDtypeStruct((m_per, n), F32),
        scratch_shapes=[pltpu.VMEM((n_dev * m_per, n), F32),
                        pltpu.SemaphoreType.DMA((7,)), pltpu.SemaphoreType.DMA((7,)), pltpu.SemaphoreType.DMA],
        compiler_params=pltpu.CompilerParams(has_side_effects=True),
    )(v, after)


def _tables(S, token):
    pos = jnp.arange(S, dtype=jnp.int32).astype(F32) + token[0, 0]

    def cs(half):
        inv_freq = ROPE_THETA ** (-jnp.arange(half, dtype=F32) / half)
        ang = pos[:, None] * inv_freq[None, :]
        return jnp.cos(ang), jnp.sin(ang)

    c64, s64 = cs(64)
    c32, s32 = cs(32)
    cr = jnp.concatenate([c64, c64], axis=1)
    sr = jnp.concatenate([-s64, s64], axis=1)
    ca = jnp.concatenate([c32, c32, c32, c32], axis=1)
    sa = jnp.concatenate([-s32, s32, -s32, s32], axis=1)
    return cr, sr, ca, sa


def _decays():
    H, C = RET_HEADS, RET_CHUNK
    log_gamma = jnp.log1p(-jnp.exp2(-5.0 - jnp.arange(H, dtype=F32)))
    idx = jnp.arange(C, dtype=F32)
    rel = idx[:, None] - idx[None, :]
    dmat = jnp.where(rel[None] >= 0, jnp.exp(log_gamma[:, None, None] * jnp.maximum(rel, 0.0)[None]), 0.0)
    qd = jnp.exp(log_gamma[:, None] * (idx + 1.0))[:, :, None]
    kd = jnp.exp(log_gamma[:, None] * (C - 1.0 - idx))[:, :, None]
    cd = jnp.exp(log_gamma * C)[:, None, None]
    return dmat, qd, kd, cd


def _local_step(place, x, tgt, ln1_g, b_in, ret_norm_g, attn_sinks, ln2_g, lnf_g, w_in, other_weights, tables, decays):
    S = x.shape[0]
    cr, sr, ca, sa = tables
    dmat, qd, kd, cd = decays

    h, rq, rk, rv, rg, aq, ak, av, ga, gb = _proj_fwd(x, ln1_g, w_in, b_in, cr, sr, ca, sa)
    ry, states = _ret_fwd(rq, rk, rv, dmat, qd, kd, cd)
    ay = _attn_fwd(aq, ak, av, attn_sinks)
    w_ret, w_attn, w_out, w_gate, w_up, w_down = other_weights(ay)
    u, merged, x1 = _merge_fwd(x, ry, rg, ay, ga, gb, ret_norm_g, w_ret, w_attn, w_out)
    gate, up, h2, act, dx2, loss, dg3 = _ffn_fwd(x1, tgt, ln2_g, lnf_g, w_gate, w_up, w_down)

    dgate, dup, dx1, dg2 = _ffn_bwd(dx2, gate, up, x1, ln2_g, w_gate, w_up, w_down)
    early = _wgrad([act], dx2, 6, None, "wgrad_down")
    early = _wgrad([dgate], h2, 4, early, "wgrad_gate")
    early = _wgrad([dup], h2, 5, early, "wgrad_up")
    dgates, drg, dba, dbb, day, dry, dgr = _merge_bwd(dx1, ga, gb, u, ay, rg, ry, ret_norm_g,
                                                                w_ret, w_attn, w_out)
    early = _wgrad([merged], dx1, 3, early, "wgrad_out")
    early = _wgrad([u], dba, 1, early, "wgrad_ret")
    e32, e16 = _wgrad([ay], dbb, 2, early, "wgrad_attn")

    grp = EARLY
    sems, flight, token = _exchange_start("pair_start_early", [e16, _empty(N_CHIPS * grp.half, BF)], N_CHIPS,
                                          _plan_pair_exchange(grp))
    daq, dak, dav, dsk = _attn_bwd(aq, ak, av, day, attn_sinks, ca, sa, token)
    _, got = _exchange_wait("pair_wait_early", sems, flight, _plan_pair_exchange(grp), after=(dsk,))
    s16 = _pair_sum(place, e32, got, grp)
    sems, flight, token = _exchange_start("chips_start_early", [s16, _empty(3 * grp.half, BF)], 3, _plan_to_chips(grp))
    d_ret = _ret_bwd(rq, rk, rv, dry, states, dmat, qd, kd, cd, cr, sr, token)
    d_proj = [d_ret, drg, daq, dak, dav, dgates]
    grad_x, dg1, db = _proj_bwd(d_proj, x, dx1, ln1_g, w_in)
    late = _wgrad(d_proj[:2], h, 0, None, "wgrad_in_a")
    late = _wgrad(d_proj[2:], h, 0, late, "wgrad_in_b", row0=d_ret.shape[1] + drg.shape[1])
    _, others = _exchange_wait("chips_wait_early", sems, flight, _plan_to_chips(grp), after=(late[0],))
    early_shard = _chip_sum(place, e32, got, others, grp)

    d_sinks = dsk.reshape(2, 2, PAIRS, 128).sum(-1).transpose(0, 2, 1).reshape(1, ATTN_Q_HEADS)
    small = dict(loss=loss, ln1_g=dg1, b_in=db, ret_norm_g=dgr, attn_sinks=d_sinks,
                 ln2_g=dg2, lnf_g=dg3)
    return grad_x, early_shard, late, small


SMALL_ROWS = 16


def _pack_small(ln1, ret, ln2, lnf, b_in, sinks, loss, name):
    def body(a_ref, b_ref, c_ref, d_ref, bi_ref, sk_ref, ls_ref, o_ref):
        o_ref[...] = jnp.zeros_like(o_ref)
        for row, ref in enumerate((a_ref, b_ref, c_ref, d_ref)):
            o_ref[row:row + 1, :] = ref[...]
        for k in range(6):
            o_ref[4 + k:5 + k, :] = bi_ref[:, k * 1024:(k + 1) * 1024]
        o_ref[10:11, 0:D_IN - 6144] = bi_ref[:, 6144:D_IN]
        o_ref[11:12, 0:ATTN_Q_HEADS] = sk_ref[...]
        o_ref[12:13, 0:LANES] = ls_ref[...]

    return pl.pallas_call(
        body, name=name, in_specs=[_VMEM] * 7, out_specs=_VMEM,
        out_shape=jax.ShapeDtypeStruct((SMALL_ROWS, 1024), F32),
    )(ln1.reshape(1, 1024), ret.reshape(1, 1024), ln2.reshape(1, 1024), lnf.reshape(1, 1024),
      b_in.reshape(1, D_IN), sinks.reshape(1, ATTN_Q_HEADS), loss.reshape(1, LANES))


def _unpack_small(p):
    return dict(ln1_g=p[0:1], ret_norm_g=p[1:2], ln2_g=p[2:3], lnf_g=p[3], b_in=p[4:11].reshape(1, -1)[:, :D_IN],
                attn_sinks=p[11:12, :ATTN_Q_HEADS], loss=p[12, 0])


def kernel(x, ln1_g, w_in, b_in, ret_norm_g, w_ret_out, attn_sinks, w_attn_out, w_out, ln2_g, w_ffn_gate, w_ffn_up, w_ffn_down, lnf_g, loss_target, m_ln1_g, m_w_in, m_b_in, m_ret_norm_g, m_w_ret_out, m_attn_sinks, m_w_attn_out, m_w_out, m_ln2_g, m_w_ffn_gate, m_w_ffn_up, m_w_ffn_down, m_lnf_g, v_ln1_g, v_w_in, v_b_in, v_ret_norm_g, v_w_ret_out, v_attn_sinks, v_w_attn_out, v_w_out, v_ln2_g, v_w_ffn_gate, v_w_ffn_up, v_w_ffn_down, v_lnf_g):
    xi = lax.axis_index("x")
    yi = lax.axis_index("y")
    ci = lax.axis_index("c")
    j = 2 * xi + yi

    place = jnp.stack([j, ci]).astype(jnp.int32)

    names = ["w_in", "w_ret_out", "w_attn_out", "w_out", "w_ffn_gate", "w_ffn_up", "w_ffn_down"]
    transposed = [True, False, False, False, True, True, False]
    view = lambda a, t: a[0].T if t else a[0]
    big_w = [view(a, t) for a, t in zip([w_in, w_ret_out, w_attn_out, w_out, w_ffn_gate, w_ffn_up, w_ffn_down], transposed)]
    big_m = [view(a, t) for a, t in zip([m_w_in, m_w_ret_out, m_w_attn_out, m_w_out, m_w_ffn_gate, m_w_ffn_up, m_w_ffn_down], transposed)]
    big_v = [view(a, t) for a, t in zip([v_w_in, v_w_ret_out, v_w_attn_out, v_w_out, v_w_ffn_gate, v_w_ffn_up, v_w_ffn_down], transposed)]

    staged_in = _stage_weight(place, big_w[0], "stage_" + names[0])
    sems, in_flight, token = _exchange_start("gather_in_start", [staged_in], 3, _plan_gather_halves(0))
    staged = [_stage_weight(place, w, "stage_" + n, after=(token,)) for n, w in zip(names[1:], big_w[1:])]
    tables = _tables(x.shape[1], token)
    decays = _decays()
    w_in_full = _exchange_wait("gather_in_wait", sems, in_flight, _plan_gather_halves(0),
                               after=(*staged, *tables, *decays))[0]
    w_in_full = _exchange("gather_in_forward", [w_in_full], 3, _plan_forward_halves(0))[0]
    rest = list(range(1, 7))
    sems, in_flight, token = _exchange_start("gather_start", staged, 3 * len(rest), _plan_gather_direct(rest),
                                             after=(w_in_full,))
    other_weights = lambda after: _exchange_wait("gather_wait", sems, in_flight, _plan_gather_direct(rest),
                                                 after=(after,))

    grad_x, early_shard, (l32, l16), small = _local_step(
        place, x[0], loss_target[0], ln1_g, b_in + token[0, 0], ret_norm_g, attn_sinks, ln2_g, lnf_g.reshape(1, -1),
        w_in_full, other_weights, tables, decays)
    pg = _all_reduce_small(_pack_small(
        small["ln1_g"], small["ret_norm_g"], small["ln2_g"], small["lnf_g"], small["b_in"],
        small["attn_sinks"], small["loss"], "pack_small_grads"), l32)
    early_shard = _exchange("share_early", [early_shard], 1, _plan_share_halves(EARLY))[0]
    _, got = _exchange("pair_exchange_late", [l16, _empty(N_CHIPS * LATE.half, BF)], N_CHIPS, _plan_pair_exchange(LATE))
    s16 = _pair_sum(place, l32, got, LATE)
    sems, flight, token = _exchange_start("chips_start_late", [s16, _empty(3 * LATE.half, BF)], 3, _plan_to_chips(LATE),
                                          after=(pg,))

    upd = {}

    def update(t, shard, after):
        res = _adamw(big_w[t], shard, big_m[t], big_v[t], "adamw_" + names[t], g_row0=GROUP_OF[t].off(t), after=after)
        upd[names[t]] = tuple((r.T if transposed[t] else r)[None] for r in res)
        return res[0]

    done = [update(t, early_shard, token) for t in EARLY.tensors]

    sm = _unpack_small(pg)
    z = jnp.zeros((1, LANES), F32)
    pw = _pack_small(ln1_g, ret_norm_g, ln2_g, lnf_g, b_in, attn_sinks, z, "pack_small_w")
    pm = _pack_small(m_ln1_g, m_ret_norm_g, m_ln2_g, m_lnf_g, m_b_in, m_attn_sinks, z, "pack_small_m")
    pv = _pack_small(v_ln1_g, v_ret_norm_g, v_ln2_g, v_lnf_g, v_b_in, v_attn_sinks, z, "pack_small_v")
    small_res = _adamw(pw, pg, pm, pv, "adamw_small", after=token)
    sd, smn, svn = (_unpack_small(t) for t in small_res[1:])
    for n in ["ln1_g", "b_in", "ret_norm_g", "attn_sinks", "ln2_g", "lnf_g"]:
        upd[n] = (sm[n], sd[n], smn[n], svn[n])

    _, others = _exchange_wait("chips_wait_late", sems, flight, _plan_to_chips(LATE), after=(*done, small_res[1]))
    late_shard = _exchange("share_late", [_chip_sum(place, l32, got, others, LATE)], 1, _plan_share_halves(LATE))[0]
    update(0, late_shard, None)

    order_w = ["ln1_g", "w_in", "b_in", "ret_norm_g", "w_ret_out", "attn_sinks", "w_attn_out", "w_out",
               "ln2_g", "w_ffn_gate", "w_ffn_up", "w_ffn_down", "lnf_g"]
    outs = [sm["loss"], grad_x[None]]
    for k in range(4):
        outs += [upd[n][k] for n in order_w]
    return tuple(outs)
```

```python
from typing import NamedTuple

import jax
import jax.numpy as jnp
from jax import lax
from jax.experimental import pallas as pl
from jax.experimental.pallas import tpu as pltpu

D_MODEL = 1024
SEQ_CHUNK = 128
RET_CHUNK = 256
RET_CHUNKS_PER_STEP = 2
RET_HEADS = 4
RET_QK_DIM = 128
RET_V_DIM = 256
ATTN_Q_HEADS = 16
ATTN_HEAD_DIM = 64
D_FF = 2816
D_IN = 6400
ROPE_THETA = 10000.0
EPS = 1e-6
N_CHIPS = 4

ADAM_LR = 0.001
ADAM_B1 = 0.9
ADAM_B2 = 0.999
ADAM_EPS = 1e-08
ADAM_WD = 0.01
ADAM_STEP = 10

OFF_RQ, OFF_RK, OFF_RV, OFF_RG = 0, 512, 1024, 2048
OFF_AQ, OFF_AK, OFF_AV, OFF_GA, OFF_GB = 3072, 4096, 4224, 4352, 5376

V7X_VMEM_LIMIT = 56 * 1024 * 1024
LANES = 128

BF = jnp.bfloat16
F32 = jnp.float32
MESH = pl.DeviceIdType.MESH

_VMEM = pl.BlockSpec(memory_space=pltpu.VMEM)
_SMEM = pl.BlockSpec(memory_space=pltpu.SMEM)
_ANY = pl.BlockSpec(memory_space=pl.ANY)


def _dot(a, b):
    return jnp.dot(a, b, preferred_element_type=F32)


def _dot_nt(a, b):
    return lax.dot_general(a, b, (((1,), (1,)), ((), ())), preferred_element_type=F32)


def _dot_tn(a, b):
    return lax.dot_general(a, b, (((0,), (0,)), ((), ())), preferred_element_type=F32)


def _sigmoid(x):
    return 0.5 * jnp.tanh(0.5 * x) + 0.5


def _rows(tm, n):
    return pl.BlockSpec((tm, n), lambda i: (i, 0))


def _fixed(shape):
    nd = len(shape)
    return pl.BlockSpec(shape, lambda i: (0,) * nd)


def _params(sem, **kw):
    return pltpu.CompilerParams(dimension_semantics=sem, vmem_limit_bytes=V7X_VMEM_LIMIT, **kw)


def _rot_ret(p, cr, sr):
    return p * cr + pltpu.roll(p, 64, 1) * sr


def _rot_ret_bwd(d, cr, sr):
    return d * cr - pltpu.roll(d, 64, 1) * sr


def _attn_partner(p):
    lane = lax.broadcasted_iota(jnp.int32, p.shape, 1)
    first = (lane % 64) < 32
    return jnp.where(first, pltpu.roll(p, 96, 1), pltpu.roll(p, 32, 1))


def _rot_attn(p, ca, sa):
    return p * ca + _attn_partner(p) * sa


def _rot_attn_bwd(d, ca, sa):
    return d * ca - _attn_partner(d) * sa


def _proj_fwd(x, g1, w_in, b_in, cr, sr, ca, sa):
    S = x.shape[0]
    tm = 512
    scale = RET_QK_DIM ** -0.5

    def body(x_ref, g_ref, w_ref, b_ref, cr_ref, sr_ref, ca_ref, sa_ref,
             h_ref, rq_ref, rk_ref, rv_ref, rg_ref, aq_ref, ak_ref, av_ref, ga_ref, gb_ref):
        xv = x_ref[...]
        r = lax.rsqrt(jnp.mean(xv * xv, axis=-1, keepdims=True) + EPS)
        h = (xv * r * g_ref[...]).astype(BF)
        h_ref[...] = h

        def piece(lo, n):
            return _dot_nt(h, w_ref[lo:lo + n, :]) + b_ref[:, lo:lo + n]

        crv, srv = cr_ref[...], sr_ref[...]
        pq = piece(OFF_RQ, 512)
        pk = piece(OFF_RK, 512)
        for hd in range(RET_HEADS):
            sl = slice(hd * 128, (hd + 1) * 128)
            rq_ref[:, sl] = (_rot_ret(pq[:, sl], crv, srv) * scale).astype(BF)
            rk_ref[:, sl] = _rot_ret(pk[:, sl], crv, srv)
        rv_ref[...] = piece(OFF_RV, 1024).astype(BF)
        rg_ref[...] = piece(OFF_RG, 1024)
        cav, sav = ca_ref[...], sa_ref[...]
        pa = piece(OFF_AQ, 1024)
        for blk in range(8):
            sl = slice(blk * 128, (blk + 1) * 128)
            aq_ref[:, sl] = _rot_attn(pa[:, sl], cav, sav).astype(BF)
        pkv = piece(OFF_AK, 256)
        ak_ref[...] = _rot_attn(pkv[:, :128], cav, sav).astype(BF)
        av_ref[...] = pkv[:, 128:].astype(BF)
        ga_ref[...] = piece(OFF_GA, 1024)
        gb_ref[...] = piece(OFF_GB, 1024)

    outs = [(D_MODEL, BF), (512, BF), (512, F32), (1024, BF), (1024, F32),
            (1024, BF), (128, BF), (128, BF), (1024, F32), (1024, F32)]
    return pl.pallas_call(
        body, name="proj_fwd", grid=(S // tm,),
        in_specs=[_rows(tm, D_MODEL), _fixed((1, D_MODEL)), _VMEM, _VMEM,
                  _rows(tm, 128), _rows(tm, 128), _rows(tm, 128), _rows(tm, 128)],
        out_specs=[_rows(tm, n) for n, _ in outs],
        out_shape=[jax.ShapeDtypeStruct((S, n), dt) for n, dt in outs],
        compiler_params=_params(("parallel",)),
    )(x, g1, w_in, b_in, cr, sr, ca, sa)


def _ret_fwd(rq, rk, rv, dmat, qd, kd, cd):
    S = rq.shape[0]
    N = S // RET_CHUNK
    cps = RET_CHUNKS_PER_STEP
    rows = cps * RET_CHUNK

    def body(q_ref, k_ref, v_ref, dm_ref, qd_ref, kd_ref, cd_ref, y_ref, st_ref, state):
        @pl.when(pl.program_id(0) == 0)
        def _():
            state[...] = jnp.zeros_like(state)

        for hd in range(RET_HEADS):
            qs = slice(hd * 128, (hd + 1) * 128)
            vs = slice(hd * 256, (hd + 1) * 256)
            free = []
            for ch in range(cps):
                rs = slice(ch * RET_CHUNK, (ch + 1) * RET_CHUNK)
                q = q_ref[rs, qs]
                kf = k_ref[rs, qs]
                v = v_ref[rs, vs]
                a = _dot_nt(q, kf.astype(BF)) * dm_ref[hd]
                kdv = (kf * kd_ref[hd]).astype(BF)
                free.append((rs, q, _dot(a.astype(BF), v), _dot_tn(kdv, v)))
            s = state[hd]
            for ch, (rs, q, inner, update) in enumerate(free):
                sb = s.astype(BF)
                st_ref[ch, hd] = sb
                y_ref[rs, vs] = inner + _dot(q, sb) * qd_ref[hd]
                s = s * cd_ref[hd] + update
            state[hd] = s

    return pl.pallas_call(
        body, name="ret_fwd", grid=(N // cps,),
        in_specs=[_rows(rows, 512), _rows(rows, 512), _rows(rows, 1024), _VMEM, _VMEM, _VMEM, _VMEM],
        out_specs=[_rows(rows, 1024), pl.BlockSpec((cps, RET_HEADS, 128, 256), lambda i: (i, 0, 0, 0))],
        out_shape=[jax.ShapeDtypeStruct((S, 1024), F32),
                   jax.ShapeDtypeStruct((N, RET_HEADS, 128, 256), BF)],
        scratch_shapes=[pltpu.VMEM((RET_HEADS, 128, 256), F32)],
        compiler_params=_params(("arbitrary",)),
    )(rq, rk, rv, dmat, qd, kd, cd)


def _kv_halves(prev_ref, cur_ref):
    kk = jnp.concatenate([prev_ref[...], cur_ref[...]], axis=0).astype(F32)
    return kk, pltpu.roll(kk, 64, 1)


PAIRS = 4


def _attn_fold(n):
    t = lax.broadcasted_iota(jnp.int32, (128, PAIRS * 128), 0)
    qi = lax.broadcasted_iota(jnp.int32, (128, PAIRS * 128), 1) & 127
    cur = t <= qi
    cur_f = jnp.where(cur, 1.0, 0.0)
    bias = jnp.where(jnp.logical_or(cur, n > 0), 0.0, -1e30)
    return cur, cur_f, 1.0 - cur_f, bias


def _group_operands(g, kk, kk_sw, vv, vv_sw):
    lo = lax.broadcasted_iota(jnp.int32, (256, 128), 1) < 64
    k_a, k_b = (kk, kk_sw) if g == 0 else (kk_sw, kk)
    v_a, v_b = (vv, vv_sw) if g == 0 else (vv_sw, vv)
    return (jnp.concatenate([jnp.where(lo, k_a, 0.0), jnp.where(lo, 0.0, k_b)], axis=0).astype(BF),
            jnp.concatenate([jnp.where(lo, v_a, 0.0), jnp.where(lo, 0.0, v_b)], axis=0).astype(BF))


def _stack_pairs(ref, g):
    return jnp.concatenate([ref[:, (4 * g + p) * 128:(4 * g + p + 1) * 128] for p in range(PAIRS)], axis=0)


def _sink_row(sk_ref, g, which):
    return jnp.concatenate([jnp.full((1, 128), sk_ref[0, 8 * g + 2 * p + which], F32) for p in range(PAIRS)], axis=1)


def _fold(full, cur):
    return jnp.where(cur, full[128:], full[:128])


def _attn_probs(raw, fold, sink):
    s = _fold(raw, fold[0]) * (ATTN_HEAD_DIM ** -0.5) + fold[3]
    m = jnp.maximum(jnp.max(s, axis=0, keepdims=True), sink)
    e = jnp.exp(s - m)
    es = jnp.exp(sink - m)
    inv = 1.0 / (jnp.sum(e, axis=0, keepdims=True) + es)
    return e * inv, es * inv


def _unfold(p, fold):
    return jnp.concatenate([p * fold[2], p * fold[1]], axis=0).astype(BF)


def _attn_fwd(aq, ak, av, sinks):
    S = aq.shape[0]
    N = S // SEQ_CHUNK

    def body(sk_ref, q_ref, kp_ref, kc_ref, vp_ref, vc_ref, o_ref):
        kk, kk_sw = _kv_halves(kp_ref, kc_ref)
        vv, vv_sw = _kv_halves(vp_ref, vc_ref)
        fold = _attn_fold(pl.program_id(0))
        ops = [_group_operands(g, kk, kk_sw, vv, vv_sw) for g in range(2)]
        raw = [_dot_nt(ops[g][0], _stack_pairs(q_ref, g)) for g in range(2)]
        for g in range(2):
            probs = [_unfold(_attn_probs(raw[g][w * 256:(w + 1) * 256], fold, _sink_row(sk_ref, g, w))[0], fold)
                     for w in range(2)]
            o = _dot_tn(jnp.concatenate(probs, axis=0), ops[g][1])
            for p in range(PAIRS):
                o_ref[:, (4 * g + p) * 128:(4 * g + p + 1) * 128] = o[p * 128:(p + 1) * 128].astype(BF)

    prev = lambda i: (jnp.maximum(i - 1, 0), 0)
    return pl.pallas_call(
        body, name="attn_fwd", grid=(N,),
        in_specs=[_SMEM, _rows(128, 1024), pl.BlockSpec((128, 128), prev), _rows(128, 128),
                  pl.BlockSpec((128, 128), prev), _rows(128, 128)],
        out_specs=_rows(128, 1024),
        out_shape=jax.ShapeDtypeStruct((S, 1024), BF),
        compiler_params=_params(("parallel",)),
    )(sinks, aq, ak, ak, av, av)


def _group_norm(y):
    mu = jnp.mean(y, axis=-1, keepdims=True)
    yc = y - mu
    rstd = lax.rsqrt(jnp.mean(yc * yc, axis=-1, keepdims=True) + EPS)
    return yc * rstd, rstd


def _merge_fwd(x, ry, rg, ay, ga, gb, g_ret, w_ret, w_attn, w_out):
    S = x.shape[0]
    tm = 256

    def body(x_ref, ry_ref, rg_ref, ay_ref, ga_ref, gb_ref, gr_ref, wr_ref, wa_ref, wo_ref,
             u_ref, mg_ref, x1_ref):
        bb = _dot(ay_ref[...], wa_ref[...])
        for hd in range(RET_HEADS):
            vs = slice(hd * 256, (hd + 1) * 256)
            yn, _ = _group_norm(ry_ref[:, vs])
            rgv = rg_ref[:, vs]
            u_ref[:, vs] = (rgv * _sigmoid(rgv) * (yn * gr_ref[:, vs])).astype(BF)
        gated_b = _sigmoid(gb_ref[...]) * bb
        sa = _sigmoid(ga_ref[...])
        ba = _dot(u_ref[...], wr_ref[...])
        merged = (sa * ba + gated_b).astype(BF)
        mg_ref[...] = merged
        x1_ref[...] = x_ref[...] + _dot(merged, wo_ref[...])

    outs = [BF, BF, F32]
    return pl.pallas_call(
        body, name="merge_fwd", grid=(S // tm,),
        in_specs=[_rows(tm, 1024)] * 6 + [_fixed((1, 1024)), _VMEM, _VMEM, _VMEM],
        out_specs=[_rows(tm, 1024)] * 3,
        out_shape=[jax.ShapeDtypeStruct((S, 1024), dt) for dt in outs],
        compiler_params=_params(("parallel",)),
    )(x, ry, rg, ay, ga, gb, g_ret, w_ret, w_attn, w_out)


def _ffn_fwd(x1, tgt, g2, g3, w_gate, w_up, w_down):
    S = x1.shape[0]
    tm = 256

    def body(x1_ref, t_ref, g2_ref, g3_ref, wg_ref, wu_ref, wd_ref,
             gate_ref, up_ref, h2_ref, act_ref, dx2_ref, loss_ref, dg3_ref):
        @pl.when(pl.program_id(0) == 0)
        def _():
            loss_ref[...] = jnp.zeros_like(loss_ref)
            dg3_ref[...] = jnp.zeros_like(dg3_ref)

        x1v = x1_ref[...]
        r2 = lax.rsqrt(jnp.mean(x1v * x1v, axis=-1, keepdims=True) + EPS)
        h2 = (x1v * r2 * g2_ref[...]).astype(BF)
        h2_ref[...] = h2
        gate = _dot_nt(h2, wg_ref[...])
        up = _dot_nt(h2, wu_ref[...])
        gate_ref[...] = gate
        up_ref[...] = up
        act = (gate * _sigmoid(gate) * up).astype(BF)
        act_ref[...] = act
        x2 = x1v + _dot(act, wd_ref[...])
        r3 = lax.rsqrt(jnp.mean(x2 * x2, axis=-1, keepdims=True) + EPS)
        xh = x2 * r3
        g3v = g3_ref[...]
        err = xh * g3v - t_ref[...]
        loss_ref[...] += 0.5 * jnp.sum(jnp.mean(err * err, axis=-1, keepdims=True), axis=0, keepdims=True)
        dy = err * (1.0 / D_MODEL)
        dg3_ref[...] += jnp.sum(dy * xh, axis=0, keepdims=True)
        dxh = dy * g3v
        dx2_ref[...] = r3 * (dxh - xh * jnp.mean(dxh * xh, axis=-1, keepdims=True))

    return pl.pallas_call(
        body, name="ffn_fwd", grid=(S // tm,),
        in_specs=[_rows(tm, 1024), _rows(tm, 1024), _fixed((1, 1024)), _fixed((1, 1024)), _VMEM, _VMEM, _VMEM],
        out_specs=[_rows(tm, D_FF), _rows(tm, D_FF), _rows(tm, 1024), _rows(tm, D_FF), _rows(tm, 1024),
                   _fixed((1, LANES)), _fixed((1, 1024))],
        out_shape=[jax.ShapeDtypeStruct((S, D_FF), F32), jax.ShapeDtypeStruct((S, D_FF), F32),
                   jax.ShapeDtypeStruct((S, 1024), BF), jax.ShapeDtypeStruct((S, D_FF), BF),
                   jax.ShapeDtypeStruct((S, 1024), F32),
                   jax.ShapeDtypeStruct((1, LANES), F32), jax.ShapeDtypeStruct((1, 1024), F32)],
        compiler_params=_params(("arbitrary",)),
    )(x1, tgt, g2, g3, w_gate, w_up, w_down)


def _ffn_bwd(dx2, gate, up, x1, g2, w_gate, w_up, w_down):
    S = x1.shape[0]
    tm = 256

    def body(dx2_ref, gate_ref, up_ref, x1_ref, g2_ref, wg_ref, wu_ref, wd_ref,
             dgate_ref, dup_ref, dx1_ref, dg2_ref):
        @pl.when(pl.program_id(0) == 0)
        def _():
            dg2_ref[...] = jnp.zeros_like(dg2_ref)

        dx2v = dx2_ref[...]
        dact = _dot_nt(dx2v.astype(BF), wd_ref[...])
        gate = gate_ref[...]
        sg = _sigmoid(gate)
        dgate = (dact * up_ref[...] * (sg * (1.0 + gate * (1.0 - sg)))).astype(BF)
        dup = (dact * (gate * sg)).astype(BF)
        dgate_ref[...] = dgate
        dup_ref[...] = dup
        dh2 = _dot(dgate, wg_ref[...]) + _dot(dup, wu_ref[...])
        x1v = x1_ref[...]
        r2 = lax.rsqrt(jnp.mean(x1v * x1v, axis=-1, keepdims=True) + EPS)
        xh = x1v * r2
        dg2_ref[...] += jnp.sum(dh2 * xh, axis=0, keepdims=True)
        dxh = dh2 * g2_ref[...]
        dx1_ref[...] = dx2v + r2 * (dxh - xh * jnp.mean(dxh * xh, axis=-1, keepdims=True))

    return pl.pallas_call(
        body, name="ffn_bwd", grid=(S // tm,),
        in_specs=[_rows(tm, 1024), _rows(tm, D_FF), _rows(tm, D_FF), _rows(tm, 1024), _fixed((1, 1024)),
                  _VMEM, _VMEM, _VMEM],
        out_specs=[_rows(tm, D_FF), _rows(tm, D_FF), _rows(tm, 1024), _fixed((1, 1024))],
        out_shape=[jax.ShapeDtypeStruct((S, D_FF), BF), jax.ShapeDtypeStruct((S, D_FF), BF),
                   jax.ShapeDtypeStruct((S, 1024), F32), jax.ShapeDtypeStruct((1, 1024), F32)],
        compiler_params=_params(("arbitrary",)),
    )(dx2, gate, up, x1, g2, w_gate, w_up, w_down)


def _merge_bwd(dx1, ga, gb, u, ay, rg, ry, g_ret, w_ret, w_attn, w_out):
    S = dx1.shape[0]
    tm = 256

    def body(dx1_ref, ga_ref, gb_ref, u_ref, ay_ref, rg_ref, ry_ref, gr_ref, wr_ref, wa_ref, wo_ref,
             dgates_ref, drg_ref, dba_ref, dbb_ref, day_ref, dry_ref, dgr_ref):
        @pl.when(pl.program_id(0) == 0)
        def _():
            dgr_ref[...] = jnp.zeros_like(dgr_ref)

        dm = _dot_nt(dx1_ref[...].astype(BF), wo_ref[...])
        ba = _dot(u_ref[...], wr_ref[...])
        bb = _dot(ay_ref[...], wa_ref[...])
        sa = _sigmoid(ga_ref[...])
        sb = _sigmoid(gb_ref[...])
        dga = (dm * ba * (sa * (1.0 - sa))).astype(BF)
        dgb = (dm * bb * (sb * (1.0 - sb))).astype(BF)
        dgates_ref[:, :1024] = dga
        dgates_ref[:, 1024:] = dgb
        dba = (dm * sa).astype(BF)
        dbb = (dm * sb).astype(BF)
        dba_ref[...] = dba
        dbb_ref[...] = dbb
        day_ref[...] = _dot_nt(dbb, wa_ref[...]).astype(BF)
        du = _dot_nt(dba, wr_ref[...])
        for hd in range(RET_HEADS):
            vs = slice(hd * 256, (hd + 1) * 256)
            yn, rstd = _group_norm(ry_ref[:, vs])
            rgv = rg_ref[:, vs]
            sr = _sigmoid(rgv)
            silu = rgv * sr
            duv = du[:, vs]
            grv = gr_ref[:, vs]
            drg = (duv * (yn * grv) * (sr * (1.0 + rgv * (1.0 - sr)))).astype(BF)
            drg_ref[:, vs] = drg
            dyg = duv * silu
            dgr_ref[:, vs] += jnp.sum(dyg * yn, axis=0, keepdims=True)
            dyn = dyg * grv
            dry_ref[:, vs] = rstd * (dyn - jnp.mean(dyn, axis=-1, keepdims=True)
                                     - yn * jnp.mean(dyn * yn, axis=-1, keepdims=True))

    return pl.pallas_call(
        body, name="merge_bwd", grid=(S // tm,),
        in_specs=[_rows(tm, 1024)] * 7 + [_fixed((1, 1024)), _VMEM, _VMEM, _VMEM],
        out_specs=[_rows(tm, 2048), _rows(tm, 1024), _rows(tm, 1024), _rows(tm, 1024), _rows(tm, 1024),
                   _rows(tm, 1024), _fixed((1, 1024))],
        out_shape=[jax.ShapeDtypeStruct((S, 2048), BF), jax.ShapeDtypeStruct((S, 1024), BF),
                   jax.ShapeDtypeStruct((S, 1024), BF), jax.ShapeDtypeStruct((S, 1024), BF),
                   jax.ShapeDtypeStruct((S, 1024), BF), jax.ShapeDtypeStruct((S, 1024), F32),
                   jax.ShapeDtypeStruct((1, 1024), F32)],
        compiler_params=_params(("arbitrary",)),
    )(dx1, ga, gb, u, ay, rg, ry, g_ret, w_ret, w_attn, w_out)


def _attn_bwd(aq, ak, av, day, sinks, ca, sa, after):
    S = aq.shape[0]
    N = S // SEQ_CHUNK

    def body(sk_ref, q_ref, kp_ref, kc_ref, vp_ref, vc_ref, do_ref, cq_ref, sq_ref, cp_ref, sp_ref, after_ref,
             dq_ref, dk_ref, dv_ref, dsk_ref, carry_k, carry_v):
        n = pl.program_id(0)

        @pl.when(n == 0)
        def _():
            carry_k[...] = jnp.zeros_like(carry_k)
            carry_v[...] = jnp.zeros_like(carry_v)
            dsk_ref[...] = jnp.zeros_like(dsk_ref)

        @pl.when(n < N)
        def _():
            kk, kk_sw = _kv_halves(kp_ref, kc_ref)
            vv, vv_sw = _kv_halves(vp_ref, vc_ref)
            lo = lax.broadcasted_iota(jnp.int32, (256, 128), 1) < 64
            fold = _attn_fold(n)
            cur = fold[0]
            cq, sq = cq_ref[...], sq_ref[...]
            dk_groups, dv_groups = [], []
            ops = [_group_operands(g, kk, kk_sw, vv, vv_sw) for g in range(2)]
            stacked = [(_stack_pairs(q_ref, g), _stack_pairs(do_ref, g)) for g in range(2)]
            raw = [_dot_nt(ops[g][0], stacked[g][0]) for g in range(2)]
            raw_dp = [_dot_nt(ops[g][1], stacked[g][1]) for g in range(2)]
            for g in range(2):
                qs, dos = stacked[g]
                ds_parts, p_parts = [], []
                for which in range(2):
                    rows = slice(which * 256, (which + 1) * 256)
                    pr, ps = _attn_probs(raw[g][rows], fold, _sink_row(sk_ref, g, which))
                    dp = _fold(raw_dp[g][rows], cur)
                    dr = jnp.sum(pr * dp, axis=0, keepdims=True)
                    ds_parts.append(_unfold(pr * (dp - dr) * (ATTN_HEAD_DIM ** -0.5), fold))
                    p_parts.append(_unfold(pr, fold))
                    dsk_ref[2 * g + which:2 * g + which + 1, :] -= ps * dr
                ds = jnp.concatenate(ds_parts, axis=0)
                dq = _dot_tn(ds, ops[g][0])
                dk_both = _dot(ds, qs)
                dv_both = _dot(jnp.concatenate(p_parts, axis=0), dos)
                for p in range(PAIRS):
                    sl = slice((4 * g + p) * 128, (4 * g + p + 1) * 128)
                    dq_ref[:, sl] = _rot_attn_bwd(dq[p * 128:(p + 1) * 128], cq, sq).astype(BF)
                tk = jnp.where(lo, dk_both[:256], dk_both[256:])
                tv = jnp.where(lo, dv_both[:256], dv_both[256:])
                dk_groups.append(tk + pltpu.roll(tk, 64, 1))
                dv_groups.append(tv + pltpu.roll(tv, 64, 1))
            dk_full = jnp.where(lo, dk_groups[0], dk_groups[1])
            dv_full = jnp.where(lo, dv_groups[0], dv_groups[1])
            dk_ref[...] = _rot_attn_bwd(carry_k[...] + dk_full[:128], cp_ref[...], sp_ref[...]).astype(BF)
            dv_ref[...] = (carry_v[...] + dv_full[:128]).astype(BF)
            carry_k[...] = dk_full[128:]
            carry_v[...] = dv_full[128:]

        @pl.when(n == N)
        def _():
            dk_ref[...] = _rot_attn_bwd(carry_k[...], cp_ref[...], sp_ref[...]).astype(BF)
            dv_ref[...] = carry_v[...].astype(BF)

    cur = lambda i: (jnp.minimum(i, N - 1), 0)
    prev = lambda i: (jnp.maximum(i - 1, 0), 0)
    blk = lambda w, im: pl.BlockSpec((128, w), im)
    return pl.pallas_call(
        body, name="attn_bwd", grid=(N + 1,),
        in_specs=[_SMEM, blk(1024, cur), blk(128, prev), blk(128, cur), blk(128, prev), blk(128, cur),
                  blk(1024, cur), blk(128, cur), blk(128, cur), blk(128, prev), blk(128, prev), _ANY],
        out_specs=[blk(1024, cur), blk(128, prev), blk(128, prev), _fixed((4, PAIRS * 128))],
        out_shape=[jax.ShapeDtypeStruct((S, 1024), BF), jax.ShapeDtypeStruct((S, 128), BF),
                   jax.ShapeDtypeStruct((S, 128), BF), jax.ShapeDtypeStruct((4, PAIRS * 128), F32)],
        scratch_shapes=[pltpu.VMEM((128, 128), F32), pltpu.VMEM((128, 128), F32)],
        compiler_params=_params(("arbitrary",)),
    )(sinks, aq, ak, ak, av, av, day, ca, sa, ca, sa, after)


def _ret_bwd(rq, rk, rv, dry, states, dmat, qd, kd, cd, cr, sr, after):
    S = rq.shape[0]
    N = S // RET_CHUNK
    scale = RET_QK_DIM ** -0.5
    cps = RET_CHUNKS_PER_STEP
    steps = N // cps

    def body(q_ref, k_ref, v_ref, dy_ref, st_ref, dm_ref, qd_ref, kd_ref, cd_ref, cr_ref, sr_ref, after_ref,
             out_ref, dstate):
        @pl.when(pl.program_id(0) == 0)
        def _():
            dstate[...] = jnp.zeros_like(dstate)

        for hd in range(RET_HEADS):
            qs = slice(hd * 128, (hd + 1) * 128)
            vs = slice(hd * 256, (hd + 1) * 256)
            dmv = dm_ref[hd]
            free = []
            for ch in range(cps):
                rs = slice(ch * RET_CHUNK, (ch + 1) * RET_CHUNK)
                crv, srv = cr_ref[rs, :], sr_ref[rs, :]
                q = q_ref[rs, qs]
                kf = k_ref[rs, qs]
                k = kf.astype(BF)
                v = v_ref[rs, vs]
                do = dy_ref[rs, vs]
                dob = do.astype(BF)
                doq = (do * qd_ref[hd]).astype(BF)
                a = (_dot_nt(q, k) * dmv).astype(BF)
                da = (_dot_nt(dob, v) * dmv).astype(BF)
                dq = _dot(da, k) + _dot_nt(doq, st_ref[ch, hd])
                out_ref[rs, qs] = _rot_ret_bwd(dq * scale, crv, srv).astype(BF)
                free.append((rs, crv, srv, v, (kf * kd_ref[hd]).astype(BF),
                             _dot_tn(a, dob), _dot_tn(da, q), _dot_tn(q, doq)))
            ds = dstate[hd]
            for rs, crv, srv, v, kdv, dv_free, dk_free, ds_update in reversed(free):
                dsb = ds.astype(BF)
                dv = dv_free + _dot(kdv, dsb)
                dk = dk_free + _dot_nt(v, dsb) * kd_ref[hd]
                ds = ds * cd_ref[hd] + ds_update
                out_ref[rs, 512 + hd * 128:512 + (hd + 1) * 128] = _rot_ret_bwd(dk, crv, srv).astype(BF)
                out_ref[rs, 1024 + hd * 256:1024 + (hd + 1) * 256] = dv.astype(BF)
            dstate[hd] = ds

    rev = lambda w: pl.BlockSpec((cps * RET_CHUNK, w), lambda i: (steps - 1 - i, 0))
    return pl.pallas_call(
        body, name="ret_bwd", grid=(steps,),
        in_specs=[rev(512), rev(512), rev(1024), rev(1024),
                  pl.BlockSpec((cps, RET_HEADS, 128, 256), lambda i: (steps - 1 - i, 0, 0, 0)),
                  _VMEM, _VMEM, _VMEM, _VMEM, rev(128), rev(128), _ANY],
        out_specs=rev(2048),
        out_shape=jax.ShapeDtypeStruct((S, 2048), BF),
        scratch_shapes=[pltpu.VMEM((RET_HEADS, 128, 256), F32)],
        compiler_params=_params(("arbitrary",)),
    )(rq, rk, rv, dry, states, dmat, qd, kd, cd, cr, sr, after)


def _proj_bwd(d_pieces, x, dx1, g1, w_in_t):
    S = x.shape[0]
    tm = 512
    widths = [d.shape[1] for d in d_pieces]
    assert sum(widths) == D_IN
    nd = len(d_pieces)

    def body(*refs):
        d_refs = refs[:nd]
        x_ref, dx1_ref, g_ref, w_ref, gx_ref, dg_ref, db_ref = refs[nd:]

        @pl.when(pl.program_id(0) == 0)
        def _():
            dg_ref[...] = jnp.zeros_like(dg_ref)
            db_ref[...] = jnp.zeros_like(db_ref)

        dh = jnp.zeros((tm, D_MODEL), F32)
        lo = 0
        for ref, n in zip(d_refs, widths):
            v = ref[...]
            dh = dh + _dot(v, w_ref[lo:lo + n, :])
            db_ref[:, lo:lo + n] += jnp.sum(v.astype(F32), axis=0, keepdims=True)
            lo += n
        xv = x_ref[...]
        r = lax.rsqrt(jnp.mean(xv * xv, axis=-1, keepdims=True) + EPS)
        xh = xv * r
        dg_ref[...] += jnp.sum(dh * xh, axis=0, keepdims=True)
        dxh = dh * g_ref[...]
        gx_ref[...] = dx1_ref[...] + r * (dxh - xh * jnp.mean(dxh * xh, axis=-1, keepdims=True))

    return pl.pallas_call(
        body, name="proj_bwd", grid=(S // tm,),
        in_specs=[_rows(tm, n) for n in widths] + [_rows(tm, 1024), _rows(tm, 1024), _fixed((1, 1024)), _VMEM],
        out_specs=[_rows(tm, 1024), _fixed((1, 1024)), _fixed((1, D_IN))],
        out_shape=[jax.ShapeDtypeStruct((S, 1024), F32), jax.ShapeDtypeStruct((1, 1024), F32),
                   jax.ShapeDtypeStruct((1, D_IN), F32)],
        compiler_params=_params(("arbitrary",)),
    )(*d_pieces, x, dx1, g1, w_in_t)


SHARD_ROWS = (1600, 256, 256, 256, 704, 704, 704)


class Group(NamedTuple):
    name: str
    tensors: tuple
    offs: tuple
    block: int
    tile: int

    @property
    def half(self):
        return self.block // 2

    def off(self, t):
        return self.offs[self.tensors.index(t)]


LATE = Group("late", (0,), (0,), 1600, 800)
EARLY = Group("early", (1, 2, 3, 4, 5, 6), (0, 256, 512, 768, 1472, 2176), 2880, 480)
GROUP_OF = {t: g for g in (LATE, EARLY) for t in g.tensors}


def _wgrad(a_list, b, t, packed, name, row0=0):
    S = b.shape[0]
    widths = [a.shape[1] for a in a_list]
    M = sum(widths)
    ts = 1024
    steps = S // ts
    na = len(a_list)
    grp = GROUP_OF[t]
    r, off = SHARD_ROWS[t], grp.off(t)
    segments = []
    for chip in range(N_CHIPS):
        lo, hi = max(chip * r, row0), min((chip + 1) * r, row0 + M)
        if lo < hi:
            segments.append((lo - row0, hi - lo, chip * grp.block + off + lo - chip * r))
    assert sum(n for _, n, _ in segments) == M
    stage_rows = max(n for _, n, _ in segments)
    n_in = na + 1 + (2 if packed is not None else 0)

    def body(*refs):
        a_refs, b_ref = refs[:na], refs[na]
        g32_ref, gb_ref, acc, stage, sems = refs[n_in:]
        i = pl.program_id(0)

        @pl.when(i == 0)
        def _():
            acc[...] = jnp.zeros_like(acc)

        bv = b_ref[...].astype(BF)
        lo = 0
        for a_ref, n in zip(a_refs, widths):
            acc[lo:lo + n, :] += _dot_tn(a_ref[...].astype(BF), bv)
            lo += n

        @pl.when(i == steps - 1)
        def _():
            for a0, n, p0 in segments:
                c32 = pltpu.make_async_copy(acc.at[a0:a0 + n, :], g32_ref.at[pl.ds(p0, n), :], sems.at[0])
                c32.start()
                stage[0:n, :] = acc[a0:a0 + n, :].astype(BF)
                c16 = pltpu.make_async_copy(stage.at[0:n, :], gb_ref.at[pl.ds(p0, n), :], sems.at[1])
                c16.start()
                c32.wait()
                c16.wait()

    shapes = [jax.ShapeDtypeStruct((N_CHIPS * grp.block, 1024), F32),
              jax.ShapeDtypeStruct((N_CHIPS * grp.block, 1024), BF)]
    operands = list(a_list) + [b]
    in_specs = [_rows(ts, n) for n in widths] + [_rows(ts, 1024)]
    aliases = {}
    if packed is not None:
        operands += list(packed)
        in_specs += [_ANY, _ANY]
        aliases = {na + 1: 0, na + 2: 1}
    return pl.pallas_call(
        body, name=name, grid=(steps,),
        in_specs=in_specs, out_specs=[_ANY, _ANY], out_shape=shapes,
        scratch_shapes=[pltpu.VMEM((M, 1024), F32), pltpu.VMEM((stage_rows, 1024), BF),
                        pltpu.SemaphoreType.DMA((2,))],
        input_output_aliases=aliases,
        compiler_params=_params(("arbitrary",)),
    )(*operands)


def _stage_weight(place, w, name, after=()):
    r = w.shape[0]
    tr = 800 if r == 1600 else r
    nb = r // tr

    def body(p_ref, w_ref, *rest):
        rest[-1][...] = w_ref[...].astype(BF)

    return pl.pallas_call(
        body, name=name,
        grid_spec=pltpu.PrefetchScalarGridSpec(
            num_scalar_prefetch=1, grid=(nb,),
            in_specs=[pl.BlockSpec((tr, 1024), lambda i, p: (i, 0))] + [_ANY] * len(after),
            out_specs=pl.BlockSpec((tr, 1024), lambda i, p: (p[0] * nb + i, 0))),
        out_shape=jax.ShapeDtypeStruct((N_CHIPS * r, 1024), BF),
        compiler_params=_params(("arbitrary",)),
    )(place, w, *after)


def _kept_block(grp, nb):
    return lambda i, p: ((2 * (i // nb) + p[1]) * nb + i % nb, 0)


def _pair_sum(place, g32, got, grp):
    nb = grp.half // grp.tile

    def body(p_ref, g_ref, r_ref, s16_ref):
        s16_ref[...] = (g_ref[...] + r_ref[...].astype(F32)).astype(BF)

    flat = pl.BlockSpec((grp.tile, 1024), lambda i, p: (i, 0))
    return pl.pallas_call(
        body, name="pair_sum_" + grp.name,
        grid_spec=pltpu.PrefetchScalarGridSpec(
            num_scalar_prefetch=1, grid=(N_CHIPS * nb,),
            in_specs=[pl.BlockSpec((grp.tile, 1024), _kept_block(grp, nb)), flat], out_specs=flat),
        out_shape=jax.ShapeDtypeStruct((N_CHIPS * grp.half, 1024), BF),
        compiler_params=_params(("arbitrary",)),
    )(place, g32, got)


def _chip_sum(place, g32, got, others, grp):
    nb = grp.half // grp.tile

    def body(p_ref, g_ref, r_ref, r0_ref, r1_ref, r2_ref, o_ref):
        own = g_ref[...] + r_ref[...].astype(F32)
        o_ref[...] = ((own + r0_ref[...].astype(F32)) + r1_ref[...].astype(F32)) + r2_ref[...].astype(F32)

    blk = lambda f: pl.BlockSpec((grp.tile, 1024), f)
    return pl.pallas_call(
        body, name="chip_sum_" + grp.name,
        grid_spec=pltpu.PrefetchScalarGridSpec(
            num_scalar_prefetch=1, grid=(nb,),
            in_specs=[blk(lambda i, p: ((2 * p[0] + p[1]) * nb + i, 0)), blk(lambda i, p: (p[0] * nb + i, 0)),
                      blk(lambda i, p: (i, 0)), blk(lambda i, p: (nb + i, 0)), blk(lambda i, p: (2 * nb + i, 0))],
            out_specs=blk(lambda i, p: (p[1] * nb + i, 0))),
        out_shape=jax.ShapeDtypeStruct((grp.block, 1024), F32),
        compiler_params=_params(("arbitrary",)),
    )(place, g32, got, others, others, others)


def _adamw(w, g, m, v, name, g_row0=0, after=None):
    r = w.shape[0]
    tr = 320 if r % 320 == 0 and g_row0 % 320 == 0 else (64 if r % 64 == 0 else r)
    assert g_row0 % tr == 0
    g0 = g_row0 // tr
    extra = [] if after is None else [after]

    def body(w_ref, g_ref, m_ref, v_ref, *rest):
        go_ref, d_ref, mo_ref, vo_ref = rest[len(extra):]
        gv = g_ref[...]
        mn = ADAM_B1 * m_ref[...] + (1.0 - ADAM_B1) * gv
        vn = ADAM_B2 * v_ref[...] + (1.0 - ADAM_B2) * (gv * gv)
        m_hat = mn / (1.0 - ADAM_B1 ** ADAM_STEP)
        v_hat = vn / (1.0 - ADAM_B2 ** ADAM_STEP)
        go_ref[...] = gv
        d_ref[...] = -ADAM_LR * (m_hat / (jnp.sqrt(v_hat) + ADAM_EPS) + ADAM_WD * w_ref[...])
        mo_ref[...] = mn
        vo_ref[...] = vn

    return pl.pallas_call(
        body, name=name, grid=(r // tr,),
        in_specs=[_rows(tr, 1024), pl.BlockSpec((tr, 1024), lambda i: (g0 + i, 0)), _rows(tr, 1024), _rows(tr, 1024)]
        + [_ANY] * len(extra),
        out_specs=[_rows(tr, 1024)] * 4,
        out_shape=[jax.ShapeDtypeStruct((r, 1024), F32)] * 4,
        compiler_params=_params(("parallel",)),
    )(w, g, m, v, *extra)


def _place():
    x, y, c = lax.axis_index("x"), lax.axis_index("y"), lax.axis_index("c")
    chips = [(1 - x, y), (x, 1 - y), (1 - x, 1 - y)]
    return x, y, c, chips


_HBM =pl.BlockSpec(memory_space=pltpu.HBM)
_SEM = pl.BlockSpec(memory_space=pltpu.SEMAPHORE)
_DATAFLOW = pltpu.SideEffectType.DATAFLOW_SIDE_EFFECTING


def _exchange(name, arrays, n, plan):
    na = len(arrays)

    def body(*refs):
        pairs = plan(refs[na:2 * na], refs[2 * na], refs[2 * na + 1])
        for send, _ in pairs:
            send.start()
        for send, recv in pairs:
            send.wait_send()
            recv.wait_recv()

    return pl.pallas_call(
        body, name=name, in_specs=[_ANY] * na, out_specs=[_ANY] * na,
        out_shape=[jax.ShapeDtypeStruct(a.shape, a.dtype) for a in arrays],
        scratch_shapes=[pltpu.SemaphoreType.DMA((n,)), pltpu.SemaphoreType.DMA((n,))],
        input_output_aliases={i: i for i in range(na)},
    )(*arrays)


def _exchange_start(name, arrays, n, plan, after=()):
    na = len(arrays)

    def body(*refs):
        for send, _ in plan(refs[:na], refs[na + len(after)], refs[na + len(after) + 1]):
            send.start()
        refs[-1][...] = jnp.zeros_like(refs[-1])

    out = pl.pallas_call(
        body, name=name,
        out_shape=(pltpu.SemaphoreType.DMA((n,)), pltpu.SemaphoreType.DMA((n,)),
                   *[pltpu.HBM(a.shape, a.dtype) for a in arrays], jax.ShapeDtypeStruct((8, LANES), F32)),
        in_specs=[_HBM] * na + [_ANY] * len(after), out_specs=(_SEM, _SEM, *[_HBM] * na, _VMEM),
        input_output_aliases={i: 2 + i for i in range(na)},
        compiler_params=pltpu.CompilerParams(has_side_effects=_DATAFLOW),
    )(*[pltpu.with_memory_space_constraint(a, pltpu.HBM) for a in arrays], *after)
    return (out[0], out[1]), list(out[2:2 + na]), out[-1]


def _exchange_wait(name, sems, in_flight, plan, after=()):
    na = len(in_flight)

    def body(*refs):
        for send, recv in plan(refs[:na], refs[na], refs[na + 1]):
            send.wait_send()
            recv.wait_recv()

    out = pl.pallas_call(
        body, name=name,
        out_shape=tuple(pltpu.HBM(a.shape, a.dtype) for a in in_flight),
        in_specs=[_HBM] * na + [_SEM, _SEM] + [_ANY] * len(after), out_specs=tuple([_HBM] * na),
        input_output_aliases={i: i for i in range(na)},
        compiler_params=pltpu.CompilerParams(has_side_effects=_DATAFLOW),
    )(*in_flight, *sems, *after)
    return list(out)


def _remote(src, dst, send_sems, recv_sems, k, to):
    return pltpu.make_async_remote_copy(src_ref=src, dst_ref=dst, send_sem=send_sems.at[k], recv_sem=recv_sems.at[k],
                                        device_id=to, device_id_type=MESH)


def _plan_gather_direct(tids):
    def plan(refs, ss, rs):
        x, y, c, chips = _place()
        j = 2 * x + y
        pairs = []
        for t, tid in enumerate(tids):
            r = SHARD_ROWS[tid]
            for k, (cx, cy) in enumerate(chips):
                mine = refs[t].at[pl.ds(j * r, r), :]
                theirs = refs[t].at[pl.ds((2 * cx + cy) * r, r), :]
                pairs.append((_remote(mine, mine, ss, rs, 3 * t + k, (cx, cy, c)),
                              _remote(theirs, theirs, ss, rs, 3 * t + k, (x, y, c))))
        return pairs
    return plan


def _half_rows(ref, tid, pj, half):
    h = SHARD_ROWS[tid] // 2
    return ref.at[pl.ds((2 * pj + half) * h, h), :]


def _plan_gather_halves(tid):
    def plan(refs, ss, rs):
        x, y, c, chips = _place()
        mine = _half_rows(refs[0], tid, 2 * x + y, c)
        return [(_remote(mine, mine, ss, rs, k, (cx, cy, c)),
                 _remote(_half_rows(refs[0], tid, 2 * cx + cy, c), _half_rows(refs[0], tid, 2 * cx + cy, c),
                         ss, rs, k, (x, y, c))) for k, (cx, cy) in enumerate(chips)]
    return plan


def _plan_forward_halves(tid):
    def plan(refs, ss, rs):
        x, y, c, chips = _place()
        pairs = []
        for k, (cx, cy) in enumerate(chips):
            got = _half_rows(refs[0], tid, 2 * cx + cy, c)
            missing = _half_rows(refs[0], tid, 2 * cx + cy, 1 - c)
            pairs.append((_remote(got, got, ss, rs, k, (x, y, 1 - c)), _remote(missing, missing, ss, rs, k, (x, y, 1 - c))))
        return pairs
    return plan


def _plan_pair_exchange(grp):
    def plan(refs, ss, rs):
        x, y, c, _ = _place()
        cps = [_remote(refs[0].at[pl.ds(pj * grp.block + (1 - c) * grp.half, grp.half), :],
                       refs[1].at[pl.ds(pj * grp.half, grp.half), :], ss, rs, pj, (x, y, 1 - c))
               for pj in range(N_CHIPS)]
        return [(cp, cp) for cp in cps]
    return plan


def _plan_to_chips(grp):
    def plan(refs, ss, rs):
        x, y, c, chips = _place()
        cps = [_remote(refs[0].at[pl.ds((2 * cx + cy) * grp.half, grp.half), :],
                       refs[1].at[pl.ds(k * grp.half, grp.half), :], ss, rs, k, (cx, cy, c))
               for k, (cx, cy) in enumerate(chips)]
        return [(cp, cp) for cp in cps]
    return plan


def _plan_share_halves(grp):
    def plan(refs, ss, rs):
        x, y, c, _ = _place()
        mine = refs[0].at[pl.ds(c * grp.half, grp.half), :]
        other = refs[0].at[pl.ds((1 - c) * grp.half, grp.half), :]
        return [(_remote(mine, mine, ss, rs, 0, (x, y, 1 - c)), _remote(other, other, ss, rs, 0, (x, y, 1 - c)))]
    return plan


def _empty(rows, dtype):
    return lax.empty((rows, 1024), dtype)


def _all_reduce_small(v, after):
    m_per, n = v.shape
    n_dev = 8

    def body(x_ref, after_ref, out_ref, all_ref, send_sems, recv_sems, local_sem):
        x, y, c, chips = _place()
        me, sibling = (x, y, c), (x, y, 1 - c)

        def rows(px, py, pc):
            return all_ref.at[pl.ds((4 * px + 2 * py + pc) * m_per, m_per), :]

        def copy(k, block, to, src=None):
            return pltpu.make_async_remote_copy(
                src_ref=rows(*block) if src is None else src, dst_ref=rows(*block),
                send_sem=send_sems.at[k], recv_sem=recv_sems.at[k], device_id=to, device_id_type=MESH)

        mine = pltpu.make_async_copy(x_ref, rows(*me), local_sem)
        mine.start()
        first = [copy(0, me, sibling, src=x_ref)]
        first += [copy(1 + k, me, (*chip, c), src=x_ref) for k, chip in enumerate(chips)]
        for cp in first:
            cp.start()
        passed = [copy(4 + k, (*chip, c), sibling) for k, chip in enumerate(chips)]
        for k, chip in enumerate(chips):
            copy(1 + k, (*chip, c), me).wait_recv()
            passed[k].start()
        copy(0, sibling, me).wait_recv()
        for k, chip in enumerate(chips):
            copy(4 + k, (*chip, 1 - c), me).wait_recv()
        for cp in first + passed:
            cp.wait_send()
        mine.wait()
        acc = all_ref[pl.ds(0, m_per), :]
        for d in range(1, n_dev):
            acc = acc + all_ref[pl.ds(d * m_per, m_per), :]
        out_ref[...] = acc

    return pl.pallas_call(
        body, name="all_reduce_small",
        in_specs=[_VMEM, _ANY], out_specs=_VMEM,
        out_shape=jax.ShapeDtypeStruct((m_per, n), F32),
        scratch_shapes=[pltpu.VMEM((n_dev * m_per, n), F32),
                        pltpu.SemaphoreType.DMA((7,)), pltpu.SemaphoreType.DMA((7,)), pltpu.SemaphoreType.DMA],
        compiler_params=pltpu.CompilerParams(has_side_effects=True),
    )(v, after)


def _tables(S, token):
    pos = jnp.arange(S, dtype=jnp.int32).astype(F32) + token[0, 0]

    def cs(half):
        inv_freq = ROPE_THETA ** (-jnp.arange(half, dtype=F32) / half)
        ang = pos[:, None] * inv_freq[None, :]
        return jnp.cos(ang), jnp.sin(ang)

    c64, s64 = cs(64)
    c32, s32 = cs(32)
    cr = jnp.concatenate([c64, c64], axis=1)
    sr = jnp.concatenate([-s64, s64], axis=1)
    ca = jnp.concatenate([c32, c32, c32, c32], axis=1)
    sa = jnp.concatenate([-s32, s32, -s32, s32], axis=1)
    return cr, sr, ca, sa


def _decays():
    H, C = RET_HEADS, RET_CHUNK
    log_gamma = jnp.log1p(-jnp.exp2(-5.0 - jnp.arange(H, dtype=F32)))
    idx = jnp.arange(C, dtype=F32)
    rel = idx[:, None] - idx[None, :]
    dmat = jnp.where(rel[None] >= 0, jnp.exp(log_gamma[:, None, None] * jnp.maximum(rel, 0.0)[None]), 0.0)
    qd = jnp.exp(log_gamma[:, None] * (idx + 1.0))[:, :, None]
    kd = jnp.exp(log_gamma[:, None] * (C - 1.0 - idx))[:, :, None]
    cd = jnp.exp(log_gamma * C)[:, None, None]
    return dmat, qd, kd, cd


def _local_step(place, x, tgt, ln1_g, b_in, ret_norm_g, attn_sinks, ln2_g, lnf_g, w_in, other_weights, tables, decays):
    S = x.shape[0]
    cr, sr, ca, sa = tables
    dmat, qd, kd, cd = decays

    h, rq, rk, rv, rg, aq, ak, av, ga, gb = _proj_fwd(x, ln1_g, w_in, b_in, cr, sr, ca, sa)
    ry, states = _ret_fwd(rq, rk, rv, dmat, qd, kd, cd)
    ay = _attn_fwd(aq, ak, av, attn_sinks)
    w_ret, w_attn, w_out, w_gate, w_up, w_down = other_weights(ay)
    u, merged, x1 = _merge_fwd(x, ry, rg, ay, ga, gb, ret_norm_g, w_ret, w_attn, w_out)
    gate, up, h2, act, dx2, loss, dg3 = _ffn_fwd(x1, tgt, ln2_g, lnf_g, w_gate, w_up, w_down)

    dgate, dup, dx1, dg2 = _ffn_bwd(dx2, gate, up, x1, ln2_g, w_gate, w_up, w_down)
    early = _wgrad([act], dx2, 6, None, "wgrad_down")
    early = _wgrad([dgate], h2, 4, early, "wgrad_gate")
    early = _wgrad([dup], h2, 5, early, "wgrad_up")
    dgates, drg, dba, dbb, day, dry, dgr = _merge_bwd(dx1, ga, gb, u, ay, rg, ry, ret_norm_g,
                                                                w_ret, w_attn, w_out)
    early = _wgrad([merged], dx1, 3, early, "wgrad_out")
    early = _wgrad([u], dba, 1, early, "wgrad_ret")
    e32, e16 = _wgrad([ay], dbb, 2, early, "wgrad_attn")

    grp = EARLY
    sems, flight, token = _exchange_start("pair_start_early", [e16, _empty(N_CHIPS * grp.half, BF)], N_CHIPS,
                                          _plan_pair_exchange(grp))
    daq, dak, dav, dsk = _attn_bwd(aq, ak, av, day, attn_sinks, ca, sa, token)
    _, got = _exchange_wait("pair_wait_early", sems, flight, _plan_pair_exchange(grp), after=(dsk,))
    s16 = _pair_sum(place, e32, got, grp)
    sems, flight, token = _exchange_start("chips_start_early", [s16, _empty(3 * grp.half, BF)], 3, _plan_to_chips(grp))
    d_ret = _ret_bwd(rq, rk, rv, dry, states, dmat, qd, kd, cd, cr, sr, token)
    d_proj = [d_ret, drg, daq, dak, dav, dgates]
    grad_x, dg1, db = _proj_bwd(d_proj, x, dx1, ln1_g, w_in)
    late = _wgrad(d_proj[:2], h, 0, None, "wgrad_in_a")
    late = _wgrad(d_proj[2:], h, 0, late, "wgrad_in_b", row0=d_ret.shape[1] + drg.shape[1])
    _, others = _exchange_wait("chips_wait_early", sems, flight, _plan_to_chips(grp), after=(late[0],))
    early_shard = _chip_sum(place, e32, got, others, grp)

    d_sinks = dsk.reshape(2, 2, PAIRS, 128).sum(-1).transpose(0, 2, 1).reshape(1, ATTN_Q_HEADS)
    small = dict(loss=loss, ln1_g=dg1, b_in=db, ret_norm_g=dgr, attn_sinks=d_sinks,
                 ln2_g=dg2, lnf_g=dg3)
    return grad_x, early_shard, late, small


SMALL_ROWS = 16


def _pack_small(ln1, ret, ln2, lnf, b_in, sinks, loss, name):
    def body(a_ref, b_ref, c_ref, d_ref, bi_ref, sk_ref, ls_ref, o_ref):
        o_ref[...] = jnp.zeros_like(o_ref)
        for row, ref in enumerate((a_ref, b_ref, c_ref, d_ref)):
            o_ref[row:row + 1, :] = ref[...]
        for k in range(6):
            o_ref[4 + k:5 + k, :] = bi_ref[:, k * 1024:(k + 1) * 1024]
        o_ref[10:11, 0:D_IN - 6144] = bi_ref[:, 6144:D_IN]
        o_ref[11:12, 0:ATTN_Q_HEADS] = sk_ref[...]
        o_ref[12:13, 0:LANES] = ls_ref[...]

    return pl.pallas_call(
        body, name=name, in_specs=[_VMEM] * 7, out_specs=_VMEM,
        out_shape=jax.ShapeDtypeStruct((SMALL_ROWS, 1024), F32),
    )(ln1.reshape(1, 1024), ret.reshape(1, 1024), ln2.reshape(1, 1024), lnf.reshape(1, 1024),
      b_in.reshape(1, D_IN), sinks.reshape(1, ATTN_Q_HEADS), loss.reshape(1, LANES))


def _unpack_small(p):
    return dict(ln1_g=p[0:1], ret_norm_g=p[1:2], ln2_g=p[2:3], lnf_g=p[3], b_in=p[4:11].reshape(1, -1)[:, :D_IN],
                attn_sinks=p[11:12, :ATTN_Q_HEADS], loss=p[12, 0])


def kernel(x, ln1_g, w_in, b_in, ret_norm_g, w_ret_out, attn_sinks, w_attn_out, w_out, ln2_g, w_ffn_gate, w_ffn_up, w_ffn_down, lnf_g, loss_target, m_ln1_g, m_w_in, m_b_in, m_ret_norm_g, m_w_ret_out, m_attn_sinks, m_w_attn_out, m_w_out, m_ln2_g, m_w_ffn_gate, m_w_ffn_up, m_w_ffn_down, m_lnf_g, v_ln1_g, v_w_in, v_b_in, v_ret_norm_g, v_w_ret_out, v_attn_sinks, v_w_attn_out, v_w_out, v_ln2_g, v_w_ffn_gate, v_w_ffn_up, v_w_ffn_down, v_lnf_g):
    xi = lax.axis_index("x")
    yi = lax.axis_index("y")
    ci = lax.axis_index("c")
    j = 2 * xi + yi

    place = jnp.stack([j, ci]).astype(jnp.int32)

    names = ["w_in", "w_ret_out", "w_attn_out", "w_out", "w_ffn_gate", "w_ffn_up", "w_ffn_down"]
    transposed = [True, False, False, False, True, True, False]
    view = lambda a, t: a[0].T if t else a[0]
    big_w = [view(a, t) for a, t in zip([w_in, w_ret_out, w_attn_out, w_out, w_ffn_gate, w_ffn_up, w_ffn_down], transposed)]
    big_m = [view(a, t) for a, t in zip([m_w_in, m_w_ret_out, m_w_attn_out, m_w_out, m_w_ffn_gate, m_w_ffn_up, m_w_ffn_down], transposed)]
    big_v = [view(a, t) for a, t in zip([v_w_in, v_w_ret_out, v_w_attn_out, v_w_out, v_w_ffn_gate, v_w_ffn_up, v_w_ffn_down], transposed)]

    staged_in = _stage_weight(place, big_w[0], "stage_" + names[0])
    sems, in_flight, token = _exchange_start("gather_in_start", [staged_in], 3, _plan_gather_halves(0))
    staged = [_stage_weight(place, w, "stage_" + n, after=(token,)) for n, w in zip(names[1:], big_w[1:])]
    tables = _tables(x.shape[1], token)
    decays = _decays()
    w_in_full = _exchange_wait("gather_in_wait", sems, in_flight, _plan_gather_halves(0),
                               after=(*staged, *tables, *decays))[0]
    w_in_full = _exchange("gather_in_forward", [w_in_full], 3, _plan_forward_halves(0))[0]
    rest = list(range(1, 7))
    sems, in_flight, token = _exchange_start("gather_start", staged, 3 * len(rest), _plan_gather_direct(rest),
                                             after=(w_in_full,))
    other_weights = lambda after: _exchange_wait("gather_wait", sems, in_flight, _plan_gather_direct(rest),
                                                 after=(after,))

    grad_x, early_shard, (l32, l16), small = _local_step(
        place, x[0], loss_target[0], ln1_g, b_in + token[0, 0], ret_norm_g, attn_sinks, ln2_g, lnf_g.reshape(1, -1),
        w_in_full, other_weights, tables, decays)
    pg = _all_reduce_small(_pack_small(
        small["ln1_g"], small["ret_norm_g"], small["ln2_g"], small["lnf_g"], small["b_in"],
        small["attn_sinks"], small["loss"], "pack_small_grads"), l32)
    early_shard = _exchange("share_early", [early_shard], 1, _plan_share_halves(EARLY))[0]
    _, got = _exchange("pair_exchange_late", [l16, _empty(N_CHIPS * LATE.half, BF)], N_CHIPS, _plan_pair_exchange(LATE))
    s16 = _pair_sum(place, l32, got, LATE)
    sems, flight, token = _exchange_start("chips_start_late", [s16, _empty(3 * LATE.half, BF)], 3, _plan_to_chips(LATE),
                                          after=(pg,))

    upd = {}

    def update(t, shard, after):
        res = _adamw(big_w[t], shard, big_m[t], big_v[t], "adamw_" + names[t], g_row0=GROUP_OF[t].off(t), after=after)
        upd[names[t]] = tuple((r.T if transposed[t] else r)[None] for r in res)
        return res[0]

    done = [update(t, early_shard, token) for t in EARLY.tensors]

    sm = _unpack_small(pg)
    z = jnp.zeros((1, LANES), F32)
    pw = _pack_small(ln1_g, ret_norm_g, ln2_g, lnf_g, b_in, attn_sinks, z, "pack_small_w")
    pm = _pack_small(m_ln1_g, m_ret_norm_g, m_ln2_g, m_lnf_g, m_b_in, m_attn_sinks, z, "pack_small_m")
    pv = _pack_small(v_ln1_g, v_ret_norm_g, v_ln2_g, v_lnf_g, v_b_in, v_attn_sinks, z, "pack_small_v")
    small_res = _adamw(pw, pg, pm, pv, "adamw_small", after=token)
    sd, smn, svn = (_unpack_small(t) for t in small_res[1:])
    for n in ["ln1_g", "b_in", "ret_norm_g", "attn_sinks", "ln2_g", "lnf_g"]:
        upd[n] = (sm[n], sd[n], smn[n], svn[n])

    _, others = _exchange_wait("chips_wait_late", sems, flight, _plan_to_chips(LATE), after=(*done, small_res[1]))
    late_shard = _exchange("share_late", [_chip_sum(place, l32, got, others, LATE)], 1, _plan_share_halves(LATE))[0]
    update(0, late_shard, None)

    order_w = ["ln1_g", "w_in", "b_in", "ret_norm_g", "w_ret_out", "attn_sinks", "w_attn_out", "w_out",
               "ln2_g", "w_ffn_gate", "w_ffn_up", "w_ffn_down", "lnf_g"]
    outs = [sm["loss"], grad_x[None]]
    for k in range(4):
        outs += [upd[n][k] for n in order_w]
    return tuple(outs)
```

```python
from typing import NamedTuple

import jax
import jax.numpy as jnp
from jax import lax
from jax.experimental import pallas as pl
from jax.experimental.pallas import tpu as pltpu

D_MODEL = 1024
SEQ_CHUNK = 128
RET_CHUNK = 256
RET_CHUNKS_PER_STEP = 2
RET_HEADS = 4
RET_QK_DIM = 128
RET_V_DIM = 256
ATTN_Q_HEADS = 16
ATTN_HEAD_DIM = 64
D_FF = 2816
D_IN = 6400
ROPE_THETA = 10000.0
EPS = 1e-6
N_CHIPS = 4

ADAM_LR = 0.001
ADAM_B1 = 0.9
ADAM_B2 = 0.999
ADAM_EPS = 1e-08
ADAM_WD = 0.01
ADAM_STEP = 10

OFF_RQ, OFF_RK, OFF_RV, OFF_RG = 0, 512, 1024, 2048
OFF_AQ, OFF_AK, OFF_AV, OFF_GA, OFF_GB = 3072, 4096, 4224, 4352, 5376

V7X_VMEM_LIMIT = 56 * 1024 * 1024
LANES = 128

BF = jnp.bfloat16
F32 = jnp.float32
KEPT = jnp.bfloat16
MESH = pl.DeviceIdType.MESH

_VMEM = pl.BlockSpec(memory_space=pltpu.VMEM)
_SMEM = pl.BlockSpec(memory_space=pltpu.SMEM)
_ANY = pl.BlockSpec(memory_space=pl.ANY)


def _dot(a, b):
    return jnp.dot(a, b, preferred_element_type=F32)


def _dot_nt(a, b):
    return lax.dot_general(a, b, (((1,), (1,)), ((), ())), preferred_element_type=F32)


def _dot_tn(a, b):
    return lax.dot_general(a, b, (((0,), (0,)), ((), ())), preferred_element_type=F32)


def _sigmoid(x):
    return 0.5 * jnp.tanh(0.5 * x) + 0.5


def _rows(tm, n):
    return pl.BlockSpec((tm, n), lambda i: (i, 0))


def _fixed(shape):
    nd = len(shape)
    return pl.BlockSpec(shape, lambda i: (0,) * nd)


def _params(sem, **kw):
    return pltpu.CompilerParams(dimension_semantics=sem, vmem_limit_bytes=V7X_VMEM_LIMIT, **kw)


def _rot_ret(p, cr, sr):
    return p * cr + pltpu.roll(p, 64, 1) * sr


def _rot_ret_bwd(d, cr, sr):
    return d * cr - pltpu.roll(d, 64, 1) * sr


def _attn_partner(p):
    lane = lax.broadcasted_iota(jnp.int32, p.shape, 1)
    first = (lane % 64) < 32
    return jnp.where(first, pltpu.roll(p, 96, 1), pltpu.roll(p, 32, 1))


def _rot_attn(p, ca, sa):
    return p * ca + _attn_partner(p) * sa


def _rot_attn_bwd(d, ca, sa):
    return d * ca - _attn_partner(d) * sa


def _proj_fwd(x, g1, w_in, b_in, cr, sr, ca, sa):
    S = x.shape[0]
    tm = 512
    scale = RET_QK_DIM ** -0.5

    def body(x_ref, g_ref, w_ref, b_ref, cr_ref, sr_ref, ca_ref, sa_ref,
             h_ref, rq_ref, rk_ref, rv_ref, rg_ref, aq_ref, ak_ref, av_ref, ga_ref, gb_ref):
        xv = x_ref[...]
        r = lax.rsqrt(jnp.mean(xv * xv, axis=-1, keepdims=True) + EPS)
        h = (xv * r * g_ref[...]).astype(BF)
        h_ref[...] = h

        def piece(lo, n):
            return _dot_nt(h, w_ref[lo:lo + n, :]) + b_ref[:, lo:lo + n]

        crv, srv = cr_ref[...], sr_ref[...]
        pq = piece(OFF_RQ, 512)
        pk = piece(OFF_RK, 512)
        for hd in range(RET_HEADS):
            sl = slice(hd * 128, (hd + 1) * 128)
            rq_ref[:, sl] = (_rot_ret(pq[:, sl], crv, srv) * scale).astype(BF)
            rk_ref[:, sl] = _rot_ret(pk[:, sl], crv, srv)
        rv_ref[...] = piece(OFF_RV, 1024).astype(BF)
        rg_ref[...] = piece(OFF_RG, 1024).astype(KEPT)
        cav, sav = ca_ref[...], sa_ref[...]
        pa = piece(OFF_AQ, 1024)
        for blk in range(8):
            sl = slice(blk * 128, (blk + 1) * 128)
            aq_ref[:, sl] = _rot_attn(pa[:, sl], cav, sav).astype(BF)
        pkv = piece(OFF_AK, 256)
        ak_ref[...] = _rot_attn(pkv[:, :128], cav, sav).astype(BF)
        av_ref[...] = pkv[:, 128:].astype(BF)
        ga_ref[...] = piece(OFF_GA, 1024).astype(KEPT)
        gb_ref[...] = piece(OFF_GB, 1024).astype(KEPT)

    outs = [(D_MODEL, BF), (512, BF), (512, F32), (1024, BF), (1024, KEPT),
            (1024, BF), (128, BF), (128, BF), (1024, KEPT), (1024, KEPT)]
    return pl.pallas_call(
        body, name="proj_fwd", grid=(S // tm,),
        in_specs=[_rows(tm, D_MODEL), _fixed((1, D_MODEL)), _VMEM, _VMEM,
                  _rows(tm, 128), _rows(tm, 128), _rows(tm, 128), _rows(tm, 128)],
        out_specs=[_rows(tm, n) for n, _ in outs],
        out_shape=[jax.ShapeDtypeStruct((S, n), dt) for n, dt in outs],
        compiler_params=_params(("parallel",)),
    )(x, g1, w_in, b_in, cr, sr, ca, sa)


def _ret_fwd(rq, rk, rv, dmat, qd, kd, cd):
    S = rq.shape[0]
    N = S // RET_CHUNK
    cps = RET_CHUNKS_PER_STEP
    rows = cps * RET_CHUNK

    def body(q_ref, k_ref, v_ref, dm_ref, qd_ref, kd_ref, cd_ref, y_ref, st_ref, state):
        @pl.when(pl.program_id(0) == 0)
        def _():
            state[...] = jnp.zeros_like(state)

        for hd in range(RET_HEADS):
            qs = slice(hd * 128, (hd + 1) * 128)
            vs = slice(hd * 256, (hd + 1) * 256)
            free = []
            for ch in range(cps):
                rs = slice(ch * RET_CHUNK, (ch + 1) * RET_CHUNK)
                q = q_ref[rs, qs]
                kf = k_ref[rs, qs]
                v = v_ref[rs, vs]
                a = _dot_nt(q, kf.astype(BF)) * dm_ref[hd]
                kdv = (kf * kd_ref[hd]).astype(BF)
                free.append((rs, q, _dot(a.astype(BF), v), _dot_tn(kdv, v)))
            s = state[hd]
            for ch, (rs, q, inner, update) in enumerate(free):
                sb = s.astype(BF)
                st_ref[ch, hd] = sb
                y_ref[rs, vs] = (inner + _dot(q, sb) * qd_ref[hd]).astype(KEPT)
                s = s * cd_ref[hd] + update
            state[hd] = s

    return pl.pallas_call(
        body, name="ret_fwd", grid=(N // cps,),
        in_specs=[_rows(rows, 512), _rows(rows, 512), _rows(rows, 1024), _VMEM, _VMEM, _VMEM, _VMEM],
        out_specs=[_rows(rows, 1024), pl.BlockSpec((cps, RET_HEADS, 128, 256), lambda i: (i, 0, 0, 0))],
        out_shape=[jax.ShapeDtypeStruct((S, 1024), KEPT),
                   jax.ShapeDtypeStruct((N, RET_HEADS, 128, 256), BF)],
        scratch_shapes=[pltpu.VMEM((RET_HEADS, 128, 256), F32)],
        compiler_params=_params(("arbitrary",)),
    )(rq, rk, rv, dmat, qd, kd, cd)


def _kv_halves(prev_ref, cur_ref):
    kk = jnp.concatenate([prev_ref[...], cur_ref[...]], axis=0).astype(F32)
    return kk, pltpu.roll(kk, 64, 1)


PAIRS = 4


def _attn_fold(n):
    t = lax.broadcasted_iota(jnp.int32, (128, PAIRS * 128), 0)
    qi = lax.broadcasted_iota(jnp.int32, (128, PAIRS * 128), 1) & 127
    cur = t <= qi
    cur_f = jnp.where(cur, 1.0, 0.0)
    bias = jnp.where(jnp.logical_or(cur, n > 0), 0.0, -1e30)
    return cur, cur_f, 1.0 - cur_f, bias


def _group_operands(g, kk, kk_sw, vv, vv_sw):
    lo = lax.broadcasted_iota(jnp.int32, (256, 128), 1) < 64
    k_a, k_b = (kk, kk_sw) if g == 0 else (kk_sw, kk)
    v_a, v_b = (vv, vv_sw) if g == 0 else (vv_sw, vv)
    return (jnp.concatenate([jnp.where(lo, k_a, 0.0), jnp.where(lo, 0.0, k_b)], axis=0).astype(BF),
            jnp.concatenate([jnp.where(lo, v_a, 0.0), jnp.where(lo, 0.0, v_b)], axis=0).astype(BF))


def _stack_pairs(ref, g):
    return jnp.concatenate([ref[:, (4 * g + p) * 128:(4 * g + p + 1) * 128] for p in range(PAIRS)], axis=0)


def _sink_row(sk_ref, g, which):
    return jnp.concatenate([jnp.full((1, 128), sk_ref[0, 8 * g + 2 * p + which], F32) for p in range(PAIRS)], axis=1)


def _fold(full, cur):
    return jnp.where(cur, full[128:], full[:128])


def _attn_probs(raw, fold, sink):
    s = _fold(raw, fold[0]) * (ATTN_HEAD_DIM ** -0.5) + fold[3]
    m = jnp.maximum(jnp.max(s, axis=0, keepdims=True), sink)
    e = jnp.exp(s - m)
    es = jnp.exp(sink - m)
    inv = 1.0 / (jnp.sum(e, axis=0, keepdims=True) + es)
    return e * inv, es * inv


def _unfold(p, fold):
    return jnp.concatenate([p * fold[2], p * fold[1]], axis=0).astype(BF)


def _attn_fwd(aq, ak, av, sinks):
    S = aq.shape[0]
    N = S // SEQ_CHUNK

    def body(sk_ref, q_ref, kp_ref, kc_ref, vp_ref, vc_ref, o_ref):
        kk, kk_sw = _kv_halves(kp_ref, kc_ref)
        vv, vv_sw = _kv_halves(vp_ref, vc_ref)
        fold = _attn_fold(pl.program_id(0))
        ops = [_group_operands(g, kk, kk_sw, vv, vv_sw) for g in range(2)]
        raw = [_dot_nt(ops[g][0], _stack_pairs(q_ref, g)) for g in range(2)]
        for g in range(2):
            probs = [_unfold(_attn_probs(raw[g][w * 256:(w + 1) * 256], fold, _sink_row(sk_ref, g, w))[0], fold)
                     for w in range(2)]
            o = _dot_tn(jnp.concatenate(probs, axis=0), ops[g][1])
            for p in range(PAIRS):
                o_ref[:, (4 * g + p) * 128:(4 * g + p + 1) * 128] = o[p * 128:(p + 1) * 128].astype(BF)

    prev = lambda i: (jnp.maximum(i - 1, 0), 0)
    return pl.pallas_call(
        body, name="attn_fwd", grid=(N,),
        in_specs=[_SMEM, _rows(128, 1024), pl.BlockSpec((128, 128), prev), _rows(128, 128),
                  pl.BlockSpec((128, 128), prev), _rows(128, 128)],
        out_specs=_rows(128, 1024),
        out_shape=jax.ShapeDtypeStruct((S, 1024), BF),
        compiler_params=_params(("parallel",)),
    )(sinks, aq, ak, ak, av, av)


def _group_norm(y):
    mu = jnp.mean(y, axis=-1, keepdims=True)
    yc = y - mu
    rstd = lax.rsqrt(jnp.mean(yc * yc, axis=-1, keepdims=True) + EPS)
    return yc * rstd, rstd


def _merge_fwd(x, ry, rg, ay, ga, gb, g_ret, w_ret, w_attn, w_out):
    S = x.shape[0]
    tm = 256

    def body(x_ref, ry_ref, rg_ref, ay_ref, ga_ref, gb_ref, gr_ref, wr_ref, wa_ref, wo_ref,
             u_ref, mg_ref, x1_ref):
        bb = _dot(ay_ref[...], wa_ref[...])
        for hd in range(RET_HEADS):
            vs = slice(hd * 256, (hd + 1) * 256)
            yn, _ = _group_norm(ry_ref[:, vs].astype(F32))
            rgv = rg_ref[:, vs].astype(F32)
            u_ref[:, vs] = (rgv * _sigmoid(rgv) * (yn * gr_ref[:, vs])).astype(BF)
        gated_b = _sigmoid(gb_ref[...].astype(F32)) * bb
        sa = _sigmoid(ga_ref[...].astype(F32))
        ba = _dot(u_ref[...], wr_ref[...])
        merged = (sa * ba + gated_b).astype(BF)
        mg_ref[...] = merged
        x1_ref[...] = x_ref[...] + _dot(merged, wo_ref[...])

    outs = [BF, BF, F32]
    return pl.pallas_call(
        body, name="merge_fwd", grid=(S // tm,),
        in_specs=[_rows(tm, 1024)] * 6 + [_fixed((1, 1024)), _VMEM, _VMEM, _VMEM],
        out_specs=[_rows(tm, 1024)] * 3,
        out_shape=[jax.ShapeDtypeStruct((S, 1024), dt) for dt in outs],
        compiler_params=_params(("parallel",)),
    )(x, ry, rg, ay, ga, gb, g_ret, w_ret, w_attn, w_out)


def _ffn_fwd(x1, tgt, g2, g3, w_gate, w_up, w_down):
    S = x1.shape[0]
    tm = 256

    def body(x1_ref, t_ref, g2_ref, g3_ref, wg_ref, wu_ref, wd_ref,
             gate_ref, up_ref, h2_ref, act_ref, dx2_ref, loss_ref, dg3_ref):
        @pl.when(pl.program_id(0) == 0)
        def _():
            loss_ref[...] = jnp.zeros_like(loss_ref)
            dg3_ref[...] = jnp.zeros_like(dg3_ref)

        x1v = x1_ref[...]
        r2 = lax.rsqrt(jnp.mean(x1v * x1v, axis=-1, keepdims=True) + EPS)
        h2 = (x1v * r2 * g2_ref[...]).astype(BF)
        h2_ref[...] = h2
        gate = _dot_nt(h2, wg_ref[...])
        up = _dot_nt(h2, wu_ref[...])
        gate_ref[...] = gate
        up_ref[...] = up
        act = (gate * _sigmoid(gate) * up).astype(BF)
        act_ref[...] = act
        x2 = x1v + _dot(act, wd_ref[...])
        r3 = lax.rsqrt(jnp.mean(x2 * x2, axis=-1, keepdims=True) + EPS)
        xh = x2 * r3
        g3v = g3_ref[...]
        err = xh * g3v - t_ref[...]
        loss_ref[...] += 0.5 * jnp.sum(jnp.mean(err * err, axis=-1, keepdims=True), axis=0, keepdims=True)
        dy = err * (1.0 / D_MODEL)
        dg3_ref[...] += jnp.sum(dy * xh, axis=0, keepdims=True)
        dxh = dy * g3v
        dx2_ref[...] = r3 * (dxh - xh * jnp.mean(dxh * xh, axis=-1, keepdims=True))

    return pl.pallas_call(
        body, name="ffn_fwd", grid=(S // tm,),
        in_specs=[_rows(tm, 1024), _rows(tm, 1024), _fixed((1, 1024)), _fixed((1, 1024)), _VMEM, _VMEM, _VMEM],
        out_specs=[_rows(tm, D_FF), _rows(tm, D_FF), _rows(tm, 1024), _rows(tm, D_FF), _rows(tm, 1024),
                   _fixed((1, LANES)), _fixed((1, 1024))],
        out_shape=[jax.ShapeDtypeStruct((S, D_FF), F32), jax.ShapeDtypeStruct((S, D_FF), F32),
                   jax.ShapeDtypeStruct((S, 1024), BF), jax.ShapeDtypeStruct((S, D_FF), BF),
                   jax.ShapeDtypeStruct((S, 1024), F32),
                   jax.ShapeDtypeStruct((1, LANES), F32), jax.ShapeDtypeStruct((1, 1024), F32)],
        compiler_params=_params(("arbitrary",)),
    )(x1, tgt, g2, g3, w_gate, w_up, w_down)


def _ffn_bwd(dx2, gate, up, x1, g2, w_gate, w_up, w_down):
    S = x1.shape[0]
    tm = 256

    def body(dx2_ref, gate_ref, up_ref, x1_ref, g2_ref, wg_ref, wu_ref, wd_ref,
             dgate_ref, dup_ref, dx1_ref, dg2_ref):
        @pl.when(pl.program_id(0) == 0)
        def _():
            dg2_ref[...] = jnp.zeros_like(dg2_ref)

        dx2v = dx2_ref[...]
        dact = _dot_nt(dx2v.astype(BF), wd_ref[...])
        gate = gate_ref[...]
        sg = _sigmoid(gate)
        dgate = (dact * up_ref[...] * (sg * (1.0 + gate * (1.0 - sg)))).astype(BF)
        dup = (dact * (gate * sg)).astype(BF)
        dgate_ref[...] = dgate
        dup_ref[...] = dup
        dh2 = _dot(dgate, wg_ref[...]) + _dot(dup, wu_ref[...])
        x1v = x1_ref[...]
        r2 = lax.rsqrt(jnp.mean(x1v * x1v, axis=-1, keepdims=True) + EPS)
        xh = x1v * r2
        dg2_ref[...] += jnp.sum(dh2 * xh, axis=0, keepdims=True)
        dxh = dh2 * g2_ref[...]
        dx1_ref[...] = dx2v + r2 * (dxh - xh * jnp.mean(dxh * xh, axis=-1, keepdims=True))

    return pl.pallas_call(
        body, name="ffn_bwd", grid=(S // tm,),
        in_specs=[_rows(tm, 1024), _rows(tm, D_FF), _rows(tm, D_FF), _rows(tm, 1024), _fixed((1, 1024)),
                  _VMEM, _VMEM, _VMEM],
        out_specs=[_rows(tm, D_FF), _rows(tm, D_FF), _rows(tm, 1024), _fixed((1, 1024))],
        out_shape=[jax.ShapeDtypeStruct((S, D_FF), BF), jax.ShapeDtypeStruct((S, D_FF), BF),
                   jax.ShapeDtypeStruct((S, 1024), F32), jax.ShapeDtypeStruct((1, 1024), F32)],
        compiler_params=_params(("arbitrary",)),
    )(dx2, gate, up, x1, g2, w_gate, w_up, w_down)


def _merge_bwd(dx1, ga, gb, u, ay, rg, ry, g_ret, w_ret, w_attn, w_out):
    S = dx1.shape[0]
    tm = 256

    def body(dx1_ref, ga_ref, gb_ref, u_ref, ay_ref, rg_ref, ry_ref, gr_ref, wr_ref, wa_ref, wo_ref,
             dgates_ref, drg_ref, dba_ref, dbb_ref, day_ref, dry_ref, dgr_ref):
        @pl.when(pl.program_id(0) == 0)
        def _():
            dgr_ref[...] = jnp.zeros_like(dgr_ref)

        dm = _dot_nt(dx1_ref[...].astype(BF), wo_ref[...])
        ba = _dot(u_ref[...], wr_ref[...])
        bb = _dot(ay_ref[...], wa_ref[...])
        sa = _sigmoid(ga_ref[...].astype(F32))
        sb = _sigmoid(gb_ref[...].astype(F32))
        dga = (dm * ba * (sa * (1.0 - sa))).astype(BF)
        dgb = (dm * bb * (sb * (1.0 - sb))).astype(BF)
        dgates_ref[:, :1024] = dga
        dgates_ref[:, 1024:] = dgb
        dba = (dm * sa).astype(BF)
        dbb = (dm * sb).astype(BF)
        dba_ref[...] = dba
        dbb_ref[...] = dbb
        day_ref[...] = _dot_nt(dbb, wa_ref[...]).astype(BF)
        du = _dot_nt(dba, wr_ref[...])
        for hd in range(RET_HEADS):
            vs = slice(hd * 256, (hd + 1) * 256)
            yn, rstd = _group_norm(ry_ref[:, vs].astype(F32))
            rgv = rg_ref[:, vs].astype(F32)
            sr = _sigmoid(rgv)
            silu = rgv * sr
            duv = du[:, vs]
            grv = gr_ref[:, vs]
            drg = (duv * (yn * grv) * (sr * (1.0 + rgv * (1.0 - sr)))).astype(BF)
            drg_ref[:, vs] = drg
            dyg = duv * silu
            dgr_ref[:, vs] += jnp.sum(dyg * yn, axis=0, keepdims=True)
            dyn = dyg * grv
            dry_ref[:, vs] = (rstd * (dyn - jnp.mean(dyn, axis=-1, keepdims=True)
                                      - yn * jnp.mean(dyn * yn, axis=-1, keepdims=True))).astype(KEPT)

    return pl.pallas_call(
        body, name="merge_bwd", grid=(S // tm,),
        in_specs=[_rows(tm, 1024)] * 7 + [_fixed((1, 1024)), _VMEM, _VMEM, _VMEM],
        out_specs=[_rows(tm, 2048), _rows(tm, 1024), _rows(tm, 1024), _rows(tm, 1024), _rows(tm, 1024),
                   _rows(tm, 1024), _fixed((1, 1024))],
        out_shape=[jax.ShapeDtypeStruct((S, 2048), BF), jax.ShapeDtypeStruct((S, 1024), BF),
                   jax.ShapeDtypeStruct((S, 1024), BF), jax.ShapeDtypeStruct((S, 1024), BF),
                   jax.ShapeDtypeStruct((S, 1024), BF), jax.ShapeDtypeStruct((S, 1024), KEPT),
                   jax.ShapeDtypeStruct((1, 1024), F32)],
        compiler_params=_params(("arbitrary",)),
    )(dx1, ga, gb, u, ay, rg, ry, g_ret, w_ret, w_attn, w_out)


def _attn_bwd(aq, ak, av, day, sinks, ca, sa, after):
    S = aq.shape[0]
    N = S // SEQ_CHUNK

    def body(sk_ref, q_ref, kp_ref, kc_ref, vp_ref, vc_ref, do_ref, cq_ref, sq_ref, cp_ref, sp_ref, after_ref,
             dq_ref, dk_ref, dv_ref, dsk_ref, carry_k, carry_v):
        n = pl.program_id(0)

        @pl.when(n == 0)
        def _():
            carry_k[...] = jnp.zeros_like(carry_k)
            carry_v[...] = jnp.zeros_like(carry_v)
            dsk_ref[...] = jnp.zeros_like(dsk_ref)

        @pl.when(n < N)
        def _():
            kk, kk_sw = _kv_halves(kp_ref, kc_ref)
            vv, vv_sw = _kv_halves(vp_ref, vc_ref)
            lo = lax.broadcasted_iota(jnp.int32, (256, 128), 1) < 64
            fold = _attn_fold(n)
            cur = fold[0]
            cq, sq = cq_ref[...], sq_ref[...]
            dk_groups, dv_groups = [], []
            ops = [_group_operands(g, kk, kk_sw, vv, vv_sw) for g in range(2)]
            stacked = [(_stack_pairs(q_ref, g), _stack_pairs(do_ref, g)) for g in range(2)]
            raw = [_dot_nt(ops[g][0], stacked[g][0]) for g in range(2)]
            raw_dp = [_dot_nt(ops[g][1], stacked[g][1]) for g in range(2)]
            for g in range(2):
                qs, dos = stacked[g]
                ds_parts, p_parts = [], []
                for which in range(2):
                    rows = slice(which * 256, (which + 1) * 256)
                    pr, ps = _attn_probs(raw[g][rows], fold, _sink_row(sk_ref, g, which))
                    dp = _fold(raw_dp[g][rows], cur)
                    dr = jnp.sum(pr * dp, axis=0, keepdims=True)
                    ds_parts.append(_unfold(pr * (dp - dr) * (ATTN_HEAD_DIM ** -0.5), fold))
                    p_parts.append(_unfold(pr, fold))
                    dsk_ref[2 * g + which:2 * g + which + 1, :] -= ps * dr
                ds = jnp.concatenate(ds_parts, axis=0)
                dq = _dot_tn(ds, ops[g][0])
                dk_both = _dot(ds, qs)
                dv_both = _dot(jnp.concatenate(p_parts, axis=0), dos)
                for p in range(PAIRS):
                    sl = slice((4 * g + p) * 128, (4 * g + p + 1) * 128)
                    dq_ref[:, sl] = _rot_attn_bwd(dq[p * 128:(p + 1) * 128], cq, sq).astype(BF)
                tk = jnp.where(lo, dk_both[:256], dk_both[256:])
                tv = jnp.where(lo, dv_both[:256], dv_both[256:])
                dk_groups.append(tk + pltpu.roll(tk, 64, 1))
                dv_groups.append(tv + pltpu.roll(tv, 64, 1))
            dk_full = jnp.where(lo, dk_groups[0], dk_groups[1])
            dv_full = jnp.where(lo, dv_groups[0], dv_groups[1])
            dk_ref[...] = _rot_attn_bwd(carry_k[...] + dk_full[:128], cp_ref[...], sp_ref[...]).astype(BF)
            dv_ref[...] = (carry_v[...] + dv_full[:128]).astype(BF)
            carry_k[...] = dk_full[128:]
            carry_v[...] = dv_full[128:]

        @pl.when(n == N)
        def _():
            dk_ref[...] = _rot_attn_bwd(carry_k[...], cp_ref[...], sp_ref[...]).astype(BF)
            dv_ref[...] = carry_v[...].astype(BF)

    cur = lambda i: (jnp.minimum(i, N - 1), 0)
    prev = lambda i: (jnp.maximum(i - 1, 0), 0)
    blk = lambda w, im: pl.BlockSpec((128, w), im)
    return pl.pallas_call(
        body, name="attn_bwd", grid=(N + 1,),
        in_specs=[_SMEM, blk(1024, cur), blk(128, prev), blk(128, cur), blk(128, prev), blk(128, cur),
                  blk(1024, cur), blk(128, cur), blk(128, cur), blk(128, prev), blk(128, prev), _ANY],
        out_specs=[blk(1024, cur), blk(128, prev), blk(128, prev), _fixed((4, PAIRS * 128))],
        out_shape=[jax.ShapeDtypeStruct((S, 1024), BF), jax.ShapeDtypeStruct((S, 128), BF),
                   jax.ShapeDtypeStruct((S, 128), BF), jax.ShapeDtypeStruct((4, PAIRS * 128), F32)],
        scratch_shapes=[pltpu.VMEM((128, 128), F32), pltpu.VMEM((128, 128), F32)],
        compiler_params=_params(("arbitrary",)),
    )(sinks, aq, ak, ak, av, av, day, ca, sa, ca, sa, after)


def _ret_bwd(rq, rk, rv, dry, states, dmat, qd, kd, cd, cr, sr, after):
    S = rq.shape[0]
    N = S // RET_CHUNK
    scale = RET_QK_DIM ** -0.5
    cps = RET_CHUNKS_PER_STEP
    steps = N // cps

    def body(q_ref, k_ref, v_ref, dy_ref, st_ref, dm_ref, qd_ref, kd_ref, cd_ref, cr_ref, sr_ref, after_ref,
             out_ref, dstate):
        @pl.when(pl.program_id(0) == 0)
        def _():
            dstate[...] = jnp.zeros_like(dstate)

        for hd in range(RET_HEADS):
            qs = slice(hd * 128, (hd + 1) * 128)
            vs = slice(hd * 256, (hd + 1) * 256)
            dmv = dm_ref[hd]
            free = []
            for ch in range(cps):
                rs = slice(ch * RET_CHUNK, (ch + 1) * RET_CHUNK)
                crv, srv = cr_ref[rs, :], sr_ref[rs, :]
                q = q_ref[rs, qs]
                kf = k_ref[rs, qs]
                k = kf.astype(BF)
                v = v_ref[rs, vs]
                do = dy_ref[rs, vs].astype(F32)
                dob = do.astype(BF)
                doq = (do * qd_ref[hd]).astype(BF)
                a = (_dot_nt(q, k) * dmv).astype(BF)
                da = (_dot_nt(dob, v) * dmv).astype(BF)
                dq = _dot(da, k) + _dot_nt(doq, st_ref[ch, hd])
                out_ref[rs, qs] = _rot_ret_bwd(dq * scale, crv, srv).astype(BF)
                free.append((rs, crv, srv, v, (kf * kd_ref[hd]).astype(BF),
                             _dot_tn(a, dob), _dot_tn(da, q), _dot_tn(q, doq)))
            ds = dstate[hd]
            for rs, crv, srv, v, kdv, dv_free, dk_free, ds_update in reversed(free):
                dsb = ds.astype(BF)
                dv = dv_free + _dot(kdv, dsb)
                dk = dk_free + _dot_nt(v, dsb) * kd_ref[hd]
                ds = ds * cd_ref[hd] + ds_update
                out_ref[rs, 512 + hd * 128:512 + (hd + 1) * 128] = _rot_ret_bwd(dk, crv, srv).astype(BF)
                out_ref[rs, 1024 + hd * 256:1024 + (hd + 1) * 256] = dv.astype(BF)
            dstate[hd] = ds

    rev = lambda w: pl.BlockSpec((cps * RET_CHUNK, w), lambda i: (steps - 1 - i, 0))
    return pl.pallas_call(
        body, name="ret_bwd", grid=(steps,),
        in_specs=[rev(512), rev(512), rev(1024), rev(1024),
                  pl.BlockSpec((cps, RET_HEADS, 128, 256), lambda i: (steps - 1 - i, 0, 0, 0)),
                  _VMEM, _VMEM, _VMEM, _VMEM, rev(128), rev(128), _ANY],
        out_specs=rev(2048),
        out_shape=jax.ShapeDtypeStruct((S, 2048), BF),
        scratch_shapes=[pltpu.VMEM((RET_HEADS, 128, 256), F32)],
        compiler_params=_params(("arbitrary",)),
    )(rq, rk, rv, dry, states, dmat, qd, kd, cd, cr, sr, after)


def _proj_bwd(d_pieces, x, dx1, g1, w_in_t):
    S = x.shape[0]
    tm = 512
    widths = [d.shape[1] for d in d_pieces]
    assert sum(widths) == D_IN
    nd = len(d_pieces)

    def body(*refs):
        d_refs = refs[:nd]
        x_ref, dx1_ref, g_ref, w_ref, gx_ref, dg_ref, db_ref = refs[nd:]

        @pl.when(pl.program_id(0) == 0)
        def _():
            dg_ref[...] = jnp.zeros_like(dg_ref)
            db_ref[...] = jnp.zeros_like(db_ref)

        dh = jnp.zeros((tm, D_MODEL), F32)
        lo = 0
        for ref, n in zip(d_refs, widths):
            v = ref[...]
            dh = dh + _dot(v, w_ref[lo:lo + n, :])
            db_ref[:, lo:lo + n] += jnp.sum(v.astype(F32), axis=0, keepdims=True)
            lo += n
        xv = x_ref[...]
        r = lax.rsqrt(jnp.mean(xv * xv, axis=-1, keepdims=True) + EPS)
        xh = xv * r
        dg_ref[...] += jnp.sum(dh * xh, axis=0, keepdims=True)
        dxh = dh * g_ref[...]
        gx_ref[...] = dx1_ref[...] + r * (dxh - xh * jnp.mean(dxh * xh, axis=-1, keepdims=True))

    return pl.pallas_call(
        body, name="proj_bwd", grid=(S // tm,),
        in_specs=[_rows(tm, n) for n in widths] + [_rows(tm, 1024), _rows(tm, 1024), _fixed((1, 1024)), _VMEM],
        out_specs=[_rows(tm, 1024), _fixed((1, 1024)), _fixed((1, D_IN))],
        out_shape=[jax.ShapeDtypeStruct((S, 1024), F32), jax.ShapeDtypeStruct((1, 1024), F32),
                   jax.ShapeDtypeStruct((1, D_IN), F32)],
        compiler_params=_params(("arbitrary",)),
    )(*d_pieces, x, dx1, g1, w_in_t)


SHARD_ROWS = (1600, 256, 256, 256, 704, 704, 704)


class Group(NamedTuple):
    name: str
    tensors: tuple
    offs: tuple
    block: int
    tile: int

    @property
    def half(self):
        return self.block // 2

    def off(self, t):
        return self.offs[self.tensors.index(t)]


LATE = Group("late", (0,), (0,), 1600, 800)
EARLY = Group("early", (1, 2, 3, 4, 5, 6), (0, 256, 512, 768, 1472, 2176), 2880, 480)
GROUP_OF = {t: g for g in (LATE, EARLY) for t in g.tensors}


def _wgrad(a_list, b, t, packed, name, row0=0):
    S = b.shape[0]
    widths = [a.shape[1] for a in a_list]
    M = sum(widths)
    ts = 1024
    steps = S // ts
    na = len(a_list)
    grp = GROUP_OF[t]
    r, off = SHARD_ROWS[t], grp.off(t)
    segments = []
    for chip in range(N_CHIPS):
        lo, hi = max(chip * r, row0), min((chip + 1) * r, row0 + M)
        if lo < hi:
            segments.append((lo - row0, hi - lo, chip * grp.block + off + lo - chip * r))
    assert sum(n for _, n, _ in segments) == M
    stage_rows = max(n for _, n, _ in segments)
    n_in = na + 1 + (2 if packed is not None else 0)

    def body(*refs):
        a_refs, b_ref = refs[:na], refs[na]
        g32_ref, gb_ref, acc, stage, sems = refs[n_in:]
        i = pl.program_id(0)

        @pl.when(i == 0)
        def _():
            acc[...] = jnp.zeros_like(acc)

        bv = b_ref[...].astype(BF)
        lo = 0
        for a_ref, n in zip(a_refs, widths):
            acc[lo:lo + n, :] += _dot_tn(a_ref[...].astype(BF), bv)
            lo += n

        @pl.when(i == steps - 1)
        def _():
            for a0, n, p0 in segments:
                c32 = pltpu.make_async_copy(acc.at[a0:a0 + n, :], g32_ref.at[pl.ds(p0, n), :], sems.at[0])
                c32.start()
                stage[0:n, :] = acc[a0:a0 + n, :].astype(BF)
                c16 = pltpu.make_async_copy(stage.at[0:n, :], gb_ref.at[pl.ds(p0, n), :], sems.at[1])
                c16.start()
                c32.wait()
                c16.wait()

    shapes = [jax.ShapeDtypeStruct((N_CHIPS * grp.block, 1024), F32),
              jax.ShapeDtypeStruct((N_CHIPS * grp.block, 1024), BF)]
    operands = list(a_list) + [b]
    in_specs = [_rows(ts, n) for n in widths] + [_rows(ts, 1024)]
    aliases = {}
    if packed is not None:
        operands += list(packed)
        in_specs += [_ANY, _ANY]
        aliases = {na + 1: 0, na + 2: 1}
    return pl.pallas_call(
        body, name=name, grid=(steps,),
        in_specs=in_specs, out_specs=[_ANY, _ANY], out_shape=shapes,
        scratch_shapes=[pltpu.VMEM((M, 1024), F32), pltpu.VMEM((stage_rows, 1024), BF),
                        pltpu.SemaphoreType.DMA((2,))],
        input_output_aliases=aliases,
        compiler_params=_params(("arbitrary",)),
    )(*operands)


def _stage_weight(place, w, name, after=()):
    r = w.shape[0]
    tr = 800 if r == 1600 else r
    nb = r // tr

    def body(p_ref, w_ref, *rest):
        rest[-1][...] = w_ref[...].astype(BF)

    return pl.pallas_call(
        body, name=name,
        grid_spec=pltpu.PrefetchScalarGridSpec(
            num_scalar_prefetch=1, grid=(nb,),
            in_specs=[pl.BlockSpec((tr, 1024), lambda i, p: (i, 0))] + [_ANY] * len(after),
            out_specs=pl.BlockSpec((tr, 1024), lambda i, p: (p[0] * nb + i, 0))),
        out_shape=jax.ShapeDtypeStruct((N_CHIPS * r, 1024), BF),
        compiler_params=_params(("arbitrary",)),
    )(place, w, *after)


def _kept_block(grp, nb):
    return lambda i, p: ((2 * (i // nb) + p[1]) * nb + i % nb, 0)


def _pair_sum(place, g32, got, grp):
    nb = grp.half // grp.tile

    def body(p_ref, g_ref, r_ref, s16_ref):
        s16_ref[...] = (g_ref[...] + r_ref[...].astype(F32)).astype(BF)

    flat = pl.BlockSpec((grp.tile, 1024), lambda i, p: (i, 0))
    return pl.pallas_call(
        body, name="pair_sum_" + grp.name,
        grid_spec=pltpu.PrefetchScalarGridSpec(
            num_scalar_prefetch=1, grid=(N_CHIPS * nb,),
            in_specs=[pl.BlockSpec((grp.tile, 1024), _kept_block(grp, nb)), flat], out_specs=flat),
        out_shape=jax.ShapeDtypeStruct((N_CHIPS * grp.half, 1024), BF),
        compiler_params=_params(("arbitrary",)),
    )(place, g32, got)


def _chip_sum(place, g32, got, others, grp):
    nb = grp.half // grp.tile

    def body(p_ref, g_ref, r_ref, r0_ref, r1_ref, r2_ref, o_ref):
        own = g_ref[...] + r_ref[...].astype(F32)
        o_ref[...] = ((own + r0_ref[...].astype(F32)) + r1_ref[...].astype(F32)) + r2_ref[...].astype(F32)

    blk = lambda f: pl.BlockSpec((grp.tile, 1024), f)
    return pl.pallas_call(
        body, name="chip_sum_" + grp.name,
        grid_spec=pltpu.PrefetchScalarGridSpec(
            num_scalar_prefetch=1, grid=(nb,),
            in_specs=[blk(lambda i, p: ((2 * p[0] + p[1]) * nb + i, 0)), blk(lambda i, p: (p[0] * nb + i, 0)),
                      blk(lambda i, p: (i, 0)), blk(lambda i, p: (nb + i, 0)), blk(lambda i, p: (2 * nb + i, 0))],
            out_specs=blk(lambda i, p: (p[1] * nb + i, 0))),
        out_shape=jax.ShapeDtypeStruct((grp.block, 1024), F32),
        compiler_params=_params(("arbitrary",)),
    )(place, g32, got, others, others, others)


def _adamw(w, g, m, v, name, g_row0=0, after=None):
    r = w.shape[0]
    tr = 320 if r % 320 == 0 and g_row0 % 320 == 0 else (64 if r % 64 == 0 else r)
    assert g_row0 % tr == 0
    g0 = g_row0 // tr
    extra = [] if after is None else [after]

    def body(w_ref, g_ref, m_ref, v_ref, *rest):
        go_ref, d_ref, mo_ref, vo_ref = rest[len(extra):]
        gv = g_ref[...]
        mn = ADAM_B1 * m_ref[...] + (1.0 - ADAM_B1) * gv
        vn = ADAM_B2 * v_ref[...] + (1.0 - ADAM_B2) * (gv * gv)
        m_hat = mn / (1.0 - ADAM_B1 ** ADAM_STEP)
        v_hat = vn / (1.0 - ADAM_B2 ** ADAM_STEP)
        go_ref[...] = gv
        d_ref[...] = -ADAM_LR * (m_hat / (jnp.sqrt(v_hat) + ADAM_EPS) + ADAM_WD * w_ref[...])
        mo_ref[...] = mn
        vo_ref[...] = vn

    return pl.pallas_call(
        body, name=name, grid=(r // tr,),
        in_specs=[_rows(tr, 1024), pl.BlockSpec((tr, 1024), lambda i: (g0 + i, 0)), _rows(tr, 1024), _rows(tr, 1024)]
        + [_ANY] * len(extra),
        out_specs=[_rows(tr, 1024)] * 4,
        out_shape=[jax.ShapeDtypeStruct((r, 1024), F32)] * 4,
        compiler_params=_params(("parallel",)),
    )(w, g, m, v, *extra)


def _place():
    x, y, c = lax.axis_index("x"), lax.axis_index("y"), lax.axis_index("c")
    chips = [(1 - x, y), (x, 1 - y), (1 - x, 1 - y)]
    return x, y, c, chips


_HBM =pl.BlockSpec(memory_space=pltpu.HBM)
_SEM = pl.BlockSpec(memory_space=pltpu.SEMAPHORE)
_DATAFLOW = pltpu.SideEffectType.DATAFLOW_SIDE_EFFECTING


def _exchange(name, arrays, n, plan):
    na = len(arrays)

    def body(*refs):
        pairs = plan(refs[na:2 * na], refs[2 * na], refs[2 * na + 1])
        for send, _ in pairs:
            send.start()
        for send, recv in pairs:
            send.wait_send()
            recv.wait_recv()

    return pl.pallas_call(
        body, name=name, in_specs=[_ANY] * na, out_specs=[_ANY] * na,
        out_shape=[jax.ShapeDtypeStruct(a.shape, a.dtype) for a in arrays],
        scratch_shapes=[pltpu.SemaphoreType.DMA((n,)), pltpu.SemaphoreType.DMA((n,))],
        input_output_aliases={i: i for i in range(na)},
    )(*arrays)


def _exchange_start(name, arrays, n, plan, after=()):
    na = len(arrays)

    def body(*refs):
        for send, _ in plan(refs[:na], refs[na + len(after)], refs[na + len(after) + 1]):
            send.start()
        refs[-1][...] = jnp.zeros_like(refs[-1])

    out = pl.pallas_call(
        body, name=name,
        out_shape=(pltpu.SemaphoreType.DMA((n,)), pltpu.SemaphoreType.DMA((n,)),
                   *[pltpu.HBM(a.shape, a.dtype) for a in arrays], jax.ShapeDtypeStruct((8, LANES), F32)),
        in_specs=[_HBM] * na + [_ANY] * len(after), out_specs=(_SEM, _SEM, *[_HBM] * na, _VMEM),
        input_output_aliases={i: 2 + i for i in range(na)},
        compiler_params=pltpu.CompilerParams(has_side_effects=_DATAFLOW),
    )(*[pltpu.with_memory_space_constraint(a, pltpu.HBM) for a in arrays], *after)
    return (out[0], out[1]), list(out[2:2 + na]), out[-1]


def _exchange_wait(name, sems, in_flight, plan, after=()):
    na = len(in_flight)

    def body(*refs):
        for send, recv in plan(refs[:na], refs[na], refs[na + 1]):
            send.wait_send()
            recv.wait_recv()

    out = pl.pallas_call(
        body, name=name,
        out_shape=tuple(pltpu.HBM(a.shape, a.dtype) for a in in_flight),
        in_specs=[_HBM] * na + [_SEM, _SEM] + [_ANY] * len(after), out_specs=tuple([_HBM] * na),
        input_output_aliases={i: i for i in range(na)},
        compiler_params=pltpu.CompilerParams(has_side_effects=_DATAFLOW),
    )(*in_flight, *sems, *after)
    return list(out)


def _remote(src, dst, send_sems, recv_sems, k, to):
    return pltpu.make_async_remote_copy(src_ref=src, dst_ref=dst, send_sem=send_sems.at[k], recv_sem=recv_sems.at[k],
                                        device_id=to, device_id_type=MESH)


def _plan_gather_direct(tids):
    def plan(refs, ss, rs):
        x, y, c, chips = _place()
        j = 2 * x + y
        pairs = []
        for t, tid in enumerate(tids):
            r = SHARD_ROWS[tid]
            for k, (cx, cy) in enumerate(chips):
                mine = refs[t].at[pl.ds(j * r, r), :]
                theirs = refs[t].at[pl.ds((2 * cx + cy) * r, r), :]
                pairs.append((_remote(mine, mine, ss, rs, 3 * t + k, (cx, cy, c)),
                              _remote(theirs, theirs, ss, rs, 3 * t + k, (x, y, c))))
        return pairs
    return plan


def _half_rows(ref, tid, pj, half):
    h = SHARD_ROWS[tid] // 2
    return ref.at[pl.ds((2 * pj + half) * h, h), :]


def _plan_gather_halves(tid):
    def plan(refs, ss, rs):
        x, y, c, chips = _place()
        mine = _half_rows(refs[0], tid, 2 * x + y, c)
        return [(_remote(mine, mine, ss, rs, k, (cx, cy, c)),
                 _remote(_half_rows(refs[0], tid, 2 * cx + cy, c), _half_rows(refs[0], tid, 2 * cx + cy, c),
                         ss, rs, k, (x, y, c))) for k, (cx, cy) in enumerate(chips)]
    return plan


def _plan_forward_halves(tid):
    def plan(refs, ss, rs):
        x, y, c, chips = _place()
        pairs = []
        for k, (cx, cy) in enumerate(chips):
            got = _half_rows(refs[0], tid, 2 * cx + cy, c)
            missing = _half_rows(refs[0], tid, 2 * cx + cy, 1 - c)
            pairs.append((_remote(got, got, ss, rs, k, (x, y, 1 - c)), _remote(missing, missing, ss, rs, k, (x, y, 1 - c))))
        return pairs
    return plan


def _plan_pair_exchange(grp):
    def plan(refs, ss, rs):
        x, y, c, _ = _place()
        cps = [_remote(refs[0].at[pl.ds(pj * grp.block + (1 - c) * grp.half, grp.half), :],
                       refs[1].at[pl.ds(pj * grp.half, grp.half), :], ss, rs, pj, (x, y, 1 - c))
               for pj in range(N_CHIPS)]
        return [(cp, cp) for cp in cps]
    return plan


def _plan_to_chips(grp):
    def plan(refs, ss, rs):
        x, y, c, chips = _place()
        cps = [_remote(refs[0].at[pl.ds((2 * cx + cy) * grp.half, grp.half), :],
                       refs[1].at[pl.ds(k * grp.half, grp.half), :], ss, rs, k, (cx, cy, c))
               for k, (cx, cy) in enumerate(chips)]
        return [(cp, cp) for cp in cps]
    return plan


def _plan_share_halves(grp):
    def plan(refs, ss, rs):
        x, y, c, _ = _place()
        mine = refs[0].at[pl.ds(c * grp.half, grp.half), :]
        other = refs[0].at[pl.ds((1 - c) * grp.half, grp.half), :]
        return [(_remote(mine, mine, ss, rs, 0, (x, y, 1 - c)), _remote(other, other, ss, rs, 0, (x, y, 1 - c)))]
    return plan


def _empty(rows, dtype):
    return lax.empty((rows, 1024), dtype)


def _all_reduce_small(v, after):
    m_per, n = v.shape
    n_dev = 8

    def body(x_ref, after_ref, out_ref, all_ref, send_sems, recv_sems, local_sem):
        x, y, c, chips = _place()
        me, sibling = (x, y, c), (x, y, 1 - c)

        def rows(px, py, pc):
            return all_ref.at[pl.ds((4 * px + 2 * py + pc) * m_per, m_per), :]

        def copy(k, block, to, src=None):
            return pltpu.make_async_remote_copy(
                src_ref=rows(*block) if src is None else src, dst_ref=rows(*block),
                send_sem=send_sems.at[k], recv_sem=recv_sems.at[k], device_id=to, device_id_type=MESH)

        mine = pltpu.make_async_copy(x_ref, rows(*me), local_sem)
        mine.start()
        first = [copy(0, me, sibling, src=x_ref)]
        first += [copy(1 + k, me, (*chip, c), src=x_ref) for k, chip in enumerate(chips)]
        for cp in first:
            cp.start()
        passed = [copy(4 + k, (*chip, c), sibling) for k, chip in enumerate(chips)]
        for k, chip in enumerate(chips):
            copy(1 + k, (*chip, c), me).wait_recv()
            passed[k].start()
        copy(0, sibling, me).wait_recv()
        for k, chip in enumerate(chips):
            copy(4 + k, (*chip, 1 - c), me).wait_recv()
        for cp in first + passed:
            cp.wait_send()
        mine.wait()
        acc = all_ref[pl.ds(0, m_per), :]
        for d in range(1, n_dev):
            acc = acc + all_ref[pl.ds(d * m_per, m_per), :]
        out_ref[...] = acc

    return pl.pallas_call(
        body, name="all_reduce_small",
        in_specs=[_VMEM, _ANY], out_specs=_VMEM,
        out_shape=jax.ShapeDtypeStruct((m_per, n), F32),
        scratch_shapes=[pltpu.VMEM((n_dev * m_per, n), F32),
                        pltpu.SemaphoreType.DMA((7,)), pltpu.SemaphoreType.DMA((7,)), pltpu.SemaphoreType.DMA],
        compiler_params=pltpu.CompilerParams(has_side_effects=True),
    )(v, after)


def _tables(S, token):
    pos = jnp.arange(S, dtype=jnp.int32).astype(F32) + token[0, 0]

    def cs(half):
        inv_freq = ROPE_THETA ** (-jnp.arange(half, dtype=F32) / half)
        ang = pos[:, None] * inv_freq[None, :]
        return jnp.cos(ang), jnp.sin(ang)

    c64, s64 = cs(64)
    c32, s32 = cs(32)
    cr = jnp.concatenate([c64, c64], axis=1)
    sr = jnp.concatenate([-s64, s64], axis=1)
    ca = jnp.concatenate([c32, c32, c32, c32], axis=1)
    sa = jnp.concatenate([-s32, s32, -s32, s32], axis=1)
    return cr, sr, ca, sa


def _decays():
    H, C = RET_HEADS, RET_CHUNK
    log_gamma = jnp.log1p(-jnp.exp2(-5.0 - jnp.arange(H, dtype=F32)))
    idx = jnp.arange(C, dtype=F32)
    rel = idx[:, None] - idx[None, :]
    dmat = jnp.where(rel[None] >= 0, jnp.exp(log_gamma[:, None, None] * jnp.maximum(rel, 0.0)[None]), 0.0)
    qd = jnp.exp(log_gamma[:, None] * (idx + 1.0))[:, :, None]
    kd = jnp.exp(log_gamma[:, None] * (C - 1.0 - idx))[:, :, None]
    cd = jnp.exp(log_gamma * C)[:, None, None]
    return dmat, qd, kd, cd


def _local_step(place, x, tgt, ln1_g, b_in, ret_norm_g, attn_sinks, ln2_g, lnf_g, w_in, other_weights, tables, decays):
    S = x.shape[0]
    cr, sr, ca, sa = tables
    dmat, qd, kd, cd = decays

    h, rq, rk, rv, rg, aq, ak, av, ga, gb = _proj_fwd(x, ln1_g, w_in, b_in, cr, sr, ca, sa)
    ry, states = _ret_fwd(rq, rk, rv, dmat, qd, kd, cd)
    ay = _attn_fwd(aq, ak, av, attn_sinks)
    w_ret, w_attn, w_out, w_gate, w_up, w_down = other_weights(ay)
    u, merged, x1 = _merge_fwd(x, ry, rg, ay, ga, gb, ret_norm_g, w_ret, w_attn, w_out)
    gate, up, h2, act, dx2, loss, dg3 = _ffn_fwd(x1, tgt, ln2_g, lnf_g, w_gate, w_up, w_down)

    dgate, dup, dx1, dg2 = _ffn_bwd(dx2, gate, up, x1, ln2_g, w_gate, w_up, w_down)
    early = _wgrad([act], dx2, 6, None, "wgrad_down")
    early = _wgrad([dgate], h2, 4, early, "wgrad_gate")
    early = _wgrad([dup], h2, 5, early, "wgrad_up")
    dgates, drg, dba, dbb, day, dry, dgr = _merge_bwd(dx1, ga, gb, u, ay, rg, ry, ret_norm_g,
                                                                w_ret, w_attn, w_out)
    early = _wgrad([merged], dx1, 3, early, "wgrad_out")
    early = _wgrad([u], dba, 1, early, "wgrad_ret")
    e32, e16 = _wgrad([ay], dbb, 2, early, "wgrad_attn")

    grp = EARLY
    sems, flight, token = _exchange_start("pair_start_early", [e16, _empty(N_CHIPS * grp.half, BF)], N_CHIPS,
                                          _plan_pair_exchange(grp))
    daq, dak, dav, dsk = _attn_bwd(aq, ak, av, day, attn_sinks, ca, sa, token)
    _, got = _exchange_wait("pair_wait_early", sems, flight, _plan_pair_exchange(grp), after=(dsk,))
    s16 = _pair_sum(place, e32, got, grp)
    sems, flight, token = _exchange_start("chips_start_early", [s16, _empty(3 * grp.half, BF)], 3, _plan_to_chips(grp))
    d_ret = _ret_bwd(rq, rk, rv, dry, states, dmat, qd, kd, cd, cr, sr, token)
    d_proj = [d_ret, drg, daq, dak, dav, dgates]
    grad_x, dg1, db = _proj_bwd(d_proj, x, dx1, ln1_g, w_in)
    late = _wgrad(d_proj[:2], h, 0, None, "wgrad_in_a")
    late = _wgrad(d_proj[2:], h, 0, late, "wgrad_in_b", row0=d_ret.shape[1] + drg.shape[1])
    _, others = _exchange_wait("chips_wait_early", sems, flight, _plan_to_chips(grp), after=(late[0],))
    early_shard = _chip_sum(place, e32, got, others, grp)

    d_sinks = dsk.reshape(2, 2, PAIRS, 128).sum(-1).transpose(0, 2, 1).reshape(1, ATTN_Q_HEADS)
    small = dict(loss=loss, ln1_g=dg1, b_in=db, ret_norm_g=dgr, attn_sinks=d_sinks,
                 ln2_g=dg2, lnf_g=dg3)
    return grad_x, early_shard, late, small


SMALL_ROWS = 16


def _pack_small(ln1, ret, ln2, lnf, b_in, sinks, loss, name):
    def body(a_ref, b_ref, c_ref, d_ref, bi_ref, sk_ref, ls_ref, o_ref):
        o_ref[...] = jnp.zeros_like(o_ref)
        for row, ref in enumerate((a_ref, b_ref, c_ref, d_ref)):
            o_ref[row:row + 1, :] = ref[...]
        for k in range(6):
            o_ref[4 + k:5 + k, :] = bi_ref[:, k * 1024:(k + 1) * 1024]
        o_ref[10:11, 0:D_IN - 6144] = bi_ref[:, 6144:D_IN]
        o_ref[11:12, 0:ATTN_Q_HEADS] = sk_ref[...]
        o_ref[12:13, 0:LANES] = ls_ref[...]

    return pl.pallas_call(
        body, name=name, in_specs=[_VMEM] * 7, out_specs=_VMEM,
        out_shape=jax.ShapeDtypeStruct((SMALL_ROWS, 1024), F32),
    )(ln1.reshape(1, 1024), ret.reshape(1, 1024), ln2.reshape(1, 1024), lnf.reshape(1, 1024),
      b_in.reshape(1, D_IN), sinks.reshape(1, ATTN_Q_HEADS), loss.reshape(1, LANES))


def _unpack_small(p):
    return dict(ln1_g=p[0:1], ret_norm_g=p[1:2], ln2_g=p[2:3], lnf_g=p[3], b_in=p[4:11].reshape(1, -1)[:, :D_IN],
                attn_sinks=p[11:12, :ATTN_Q_HEADS], loss=p[12, 0])


def kernel(x, ln1_g, w_in, b_in, ret_norm_g, w_ret_out, attn_sinks, w_attn_out, w_out, ln2_g, w_ffn_gate, w_ffn_up, w_ffn_down, lnf_g, loss_target, m_ln1_g, m_w_in, m_b_in, m_ret_norm_g, m_w_ret_out, m_attn_sinks, m_w_attn_out, m_w_out, m_ln2_g, m_w_ffn_gate, m_w_ffn_up, m_w_ffn_down, m_lnf_g, v_ln1_g, v_w_in, v_b_in, v_ret_norm_g, v_w_ret_out, v_attn_sinks, v_w_attn_out, v_w_out, v_ln2_g, v_w_ffn_gate, v_w_ffn_up, v_w_ffn_down, v_lnf_g):
    xi = lax.axis_index("x")
    yi = lax.axis_index("y")
    ci = lax.axis_index("c")
    j = 2 * xi + yi

    place = jnp.stack([j, ci]).astype(jnp.int32)

    names = ["w_in", "w_ret_out", "w_attn_out", "w_out", "w_ffn_gate", "w_ffn_up", "w_ffn_down"]
    transposed = [True, False, False, False, True, True, False]
    view = lambda a, t: a[0].T if t else a[0]
    big_w = [view(a, t) for a, t in zip([w_in, w_ret_out, w_attn_out, w_out, w_ffn_gate, w_ffn_up, w_ffn_down], transposed)]
    big_m = [view(a, t) for a, t in zip([m_w_in, m_w_ret_out, m_w_attn_out, m_w_out, m_w_ffn_gate, m_w_ffn_up, m_w_ffn_down], transposed)]
    big_v = [view(a, t) for a, t in zip([v_w_in, v_w_ret_out, v_w_attn_out, v_w_out, v_w_ffn_gate, v_w_ffn_up, v_w_ffn_down], transposed)]

    staged_in = _stage_weight(place, big_w[0], "stage_" + names[0])
    sems, in_flight, token = _exchange_start("gather_in_start", [staged_in], 3, _plan_gather_halves(0))
    staged = [_stage_weight(place, w, "stage_" + n, after=(token,)) for n, w in zip(names[1:], big_w[1:])]
    tables = _tables(x.shape[1], token)
    decays = _decays()
    w_in_full = _exchange_wait("gather_in_wait", sems, in_flight, _plan_gather_halves(0),
                               after=(*staged, *tables, *decays))[0]
    w_in_full = _exchange("gather_in_forward", [w_in_full], 3, _plan_forward_halves(0))[0]
    rest = list(range(1, 7))
    sems, in_flight, token = _exchange_start("gather_start", staged, 3 * len(rest), _plan_gather_direct(rest),
                                             after=(w_in_full,))
    other_weights = lambda after: _exchange_wait("gather_wait", sems, in_flight, _plan_gather_direct(rest),
                                                 after=(after,))

    grad_x, early_shard, (l32, l16), small = _local_step(
        place, x[0], loss_target[0], ln1_g, b_in + token[0, 0], ret_norm_g, attn_sinks, ln2_g, lnf_g.reshape(1, -1),
        w_in_full, other_weights, tables, decays)
    pg = _all_reduce_small(_pack_small(
        small["ln1_g"], small["ret_norm_g"], small["ln2_g"], small["lnf_g"], small["b_in"],
        small["attn_sinks"], small["loss"], "pack_small_grads"), l32)
    early_shard = _exchange("share_early", [early_shard], 1, _plan_share_halves(EARLY))[0]
    _, got = _exchange("pair_exchange_late", [l16, _empty(N_CHIPS * LATE.half, BF)], N_CHIPS, _plan_pair_exchange(LATE))
    s16 = _pair_sum(place, l32, got, LATE)
    sems, flight, token = _exchange_start("chips_start_late", [s16, _empty(3 * LATE.half, BF)], 3, _plan_to_chips(LATE),
                                          after=(pg,))

    upd = {}

    def update(t, shard, after):
        res = _adamw(big_w[t], shard, big_m[t], big_v[t], "adamw_" + names[t], g_row0=GROUP_OF[t].off(t), after=after)
        upd[names[t]] = tuple((r.T if transposed[t] else r)[None] for r in res)
        return res[0]

    done = [update(t, early_shard, token) for t in EARLY.tensors]

    sm = _unpack_small(pg)
    z = jnp.zeros((1, LANES), F32)
    pw = _pack_small(ln1_g, ret_norm_g, ln2_g, lnf_g, b_in, attn_sinks, z, "pack_small_w")
    pm = _pack_small(m_ln1_g, m_ret_norm_g, m_ln2_g, m_lnf_g, m_b_in, m_attn_sinks, z, "pack_small_m")
    pv = _pack_small(v_ln1_g, v_ret_norm_g, v_ln2_g, v_lnf_g, v_b_in, v_attn_sinks, z, "pack_small_v")
    small_res = _adamw(pw, pg, pm, pv, "adamw_small", after=token)
    sd, smn, svn = (_unpack_small(t) for t in small_res[1:])
    for n in ["ln1_g", "b_in", "ret_norm_g", "attn_sinks", "ln2_g", "lnf_g"]:
        upd[n] = (sm[n], sd[n], smn[n], svn[n])

    _, others = _exchange_wait("chips_wait_late", sems, flight, _plan_to_chips(LATE), after=(*done, small_res[1]))
    late_shard = _exchange("share_late", [_chip_sum(place, l32, got, others, LATE)], 1, _plan_share_halves(LATE))[0]
    update(0, late_shard, None)

    order_w = ["ln1_g", "w_in", "b_in", "ret_norm_g", "w_ret_out", "attn_sinks", "w_attn_out", "w_out",
               "ln2_g", "w_ffn_gate", "w_ffn_up", "w_ffn_down", "lnf_g"]
    outs = [sm["loss"], grad_x[None]]
    for k in range(4):
        outs += [upd[n][k] for n in order_w]
    return tuple(outs)
```

```python
from typing import NamedTuple

import jax
import jax.numpy as jnp
from jax import lax
from jax.experimental import pallas as pl
from jax.experimental.pallas import tpu as pltpu

D_MODEL = 1024
SEQ_CHUNK = 128
RET_CHUNK = 256
RET_CHUNKS_PER_STEP = 2
RET_HEADS = 4
RET_QK_DIM = 128
RET_V_DIM = 256
ATTN_Q_HEADS = 16
ATTN_HEAD_DIM = 64
D_FF = 2816
D_IN = 6400
ROPE_THETA = 10000.0
EPS = 1e-6
N_CHIPS = 4

ADAM_LR = 0.001
ADAM_B1 = 0.9
ADAM_B2 = 0.999
ADAM_EPS = 1e-08
ADAM_WD = 0.01
ADAM_STEP = 10

OFF_RQ, OFF_RK, OFF_RV, OFF_RG = 0, 512, 1024, 2048
OFF_AQ, OFF_AK, OFF_AV, OFF_GA, OFF_GB = 3072, 4096, 4224, 4352, 5376

V7X_VMEM_LIMIT = 56 * 1024 * 1024
LANES = 128

BF = jnp.bfloat16
F32 = jnp.float32
KEPT = jnp.bfloat16
MESH = pl.DeviceIdType.MESH

_VMEM = pl.BlockSpec(memory_space=pltpu.VMEM)
_SMEM = pl.BlockSpec(memory_space=pltpu.SMEM)
_ANY = pl.BlockSpec(memory_space=pl.ANY)


def _dot(a, b):
    return jnp.dot(a, b, preferred_element_type=F32)


def _dot_nt(a, b):
    return lax.dot_general(a, b, (((1,), (1,)), ((), ())), preferred_element_type=F32)


def _dot_tn(a, b):
    return lax.dot_general(a, b, (((0,), (0,)), ((), ())), preferred_element_type=F32)


def _sigmoid(x):
    return 0.5 * jnp.tanh(0.5 * x) + 0.5


def _rows(tm, n):
    return pl.BlockSpec((tm, n), lambda i: (i, 0))


def _fixed(shape):
    nd = len(shape)
    return pl.BlockSpec(shape, lambda i: (0,) * nd)


def _params(sem, **kw):
    return pltpu.CompilerParams(dimension_semantics=sem, vmem_limit_bytes=V7X_VMEM_LIMIT, **kw)


def _rot_ret(p, cr, sr):
    return p * cr + pltpu.roll(p, 64, 1) * sr


def _rot_ret_bwd(d, cr, sr):
    return d * cr - pltpu.roll(d, 64, 1) * sr


def _attn_partner(p):
    lane = lax.broadcasted_iota(jnp.int32, p.shape, 1)
    first = (lane % 64) < 32
    return jnp.where(first, pltpu.roll(p, 96, 1), pltpu.roll(p, 32, 1))


def _rot_attn(p, ca, sa):
    return p * ca + _attn_partner(p) * sa


def _rot_attn_bwd(d, ca, sa):
    return d * ca - _attn_partner(d) * sa


def _proj_fwd(x, g1, w_in, b_in, cr, sr, ca, sa):
    S = x.shape[0]
    tm = 512
    scale = RET_QK_DIM ** -0.5

    def body(x_ref, g_ref, w_ref, b_ref, cr_ref, sr_ref, ca_ref, sa_ref,
             h_ref, rq_ref, rk_ref, rv_ref, rg_ref, aq_ref, ak_ref, av_ref, ga_ref, gb_ref):
        xv = x_ref[...]
        r = lax.rsqrt(jnp.mean(xv * xv, axis=-1, keepdims=True) + EPS)
        h = (xv * r * g_ref[...]).astype(BF)
        h_ref[...] = h

        def piece(lo, n):
            return _dot_nt(h, w_ref[lo:lo + n, :]) + b_ref[:, lo:lo + n]

        crv, srv = cr_ref[...], sr_ref[...]
        pq = piece(OFF_RQ, 512)
        pk = piece(OFF_RK, 512)
        for hd in range(RET_HEADS):
            sl = slice(hd * 128, (hd + 1) * 128)
            rq_ref[:, sl] = (_rot_ret(pq[:, sl], crv, srv) * scale).astype(BF)
            rk_ref[:, sl] = _rot_ret(pk[:, sl], crv, srv)
        rv_ref[...] = piece(OFF_RV, 1024).astype(BF)
        rg_ref[...] = piece(OFF_RG, 1024).astype(KEPT)
        cav, sav = ca_ref[...], sa_ref[...]
        pa = piece(OFF_AQ, 1024)
        for blk in range(8):
            sl = slice(blk * 128, (blk + 1) * 128)
            aq_ref[:, sl] = _rot_attn(pa[:, sl], cav, sav).astype(BF)
        pkv = piece(OFF_AK, 256)
        ak_ref[...] = _rot_attn(pkv[:, :128], cav, sav).astype(BF)
        av_ref[...] = pkv[:, 128:].astype(BF)
        ga_ref[...] = piece(OFF_GA, 1024).astype(KEPT)
        gb_ref[...] = piece(OFF_GB, 1024).astype(KEPT)

    outs = [(D_MODEL, BF), (512, BF), (512, F32), (1024, BF), (1024, KEPT),
            (1024, BF), (128, BF), (128, BF), (1024, KEPT), (1024, KEPT)]
    return pl.pallas_call(
        body, name="proj_fwd", grid=(S // tm,),
        in_specs=[_rows(tm, D_MODEL), _fixed((1, D_MODEL)), _VMEM, _VMEM,
                  _rows(tm, 128), _rows(tm, 128), _rows(tm, 128), _rows(tm, 128)],
        out_specs=[_rows(tm, n) for n, _ in outs],
        out_shape=[jax.ShapeDtypeStruct((S, n), dt) for n, dt in outs],
        compiler_params=_params(("parallel",)),
    )(x, g1, w_in, b_in, cr, sr, ca, sa)


def _ret_fwd(rq, rk, rv, dmat, qd, kd, cd):
    S = rq.shape[0]
    N = S // RET_CHUNK
    cps = RET_CHUNKS_PER_STEP
    rows = cps * RET_CHUNK

    def body(q_ref, k_ref, v_ref, dm_ref, qd_ref, kd_ref, cd_ref, y_ref, st_ref, state):
        @pl.when(pl.program_id(0) == 0)
        def _():
            state[...] = jnp.zeros_like(state)

        for hd in range(RET_HEADS):
            qs = slice(hd * 128, (hd + 1) * 128)
            vs = slice(hd * 256, (hd + 1) * 256)
            free = []
            for ch in range(cps):
                rs = slice(ch * RET_CHUNK, (ch + 1) * RET_CHUNK)
                q = q_ref[rs, qs]
                kf = k_ref[rs, qs]
                v = v_ref[rs, vs]
                a = _dot_nt(q, kf.astype(BF)) * dm_ref[hd]
                kdv = (kf * kd_ref[hd]).astype(BF)
                free.append((rs, q, _dot(a.astype(BF), v), _dot_tn(kdv, v)))
            s = state[hd]
            for ch, (rs, q, inner, update) in enumerate(free):
                sb = s.astype(BF)
                st_ref[ch, hd] = sb
                y_ref[rs, vs] = (inner + _dot(q, sb) * qd_ref[hd]).astype(KEPT)
                s = s * cd_ref[hd] + update
            state[hd] = s

    return pl.pallas_call(
        body, name="ret_fwd", grid=(N // cps,),
        in_specs=[_rows(rows, 512), _rows(rows, 512), _rows(rows, 1024), _VMEM, _VMEM, _VMEM, _VMEM],
        out_specs=[_rows(rows, 1024), pl.BlockSpec((cps, RET_HEADS, 128, 256), lambda i: (i, 0, 0, 0))],
        out_shape=[jax.ShapeDtypeStruct((S, 1024), KEPT),
                   jax.ShapeDtypeStruct((N, RET_HEADS, 128, 256), BF)],
        scratch_shapes=[pltpu.VMEM((RET_HEADS, 128, 256), F32)],
        compiler_params=_params(("arbitrary",)),
    )(rq, rk, rv, dmat, qd, kd, cd)


def _kv_halves(prev_ref, cur_ref):
    kk = jnp.concatenate([prev_ref[...], cur_ref[...]], axis=0).astype(F32)
    return kk, pltpu.roll(kk, 64, 1)


PAIRS = 4


def _attn_fold(n):
    t = lax.broadcasted_iota(jnp.int32, (128, PAIRS * 128), 0)
    qi = lax.broadcasted_iota(jnp.int32, (128, PAIRS * 128), 1) & 127
    cur = t <= qi
    cur_f = jnp.where(cur, 1.0, 0.0)
    bias = jnp.where(jnp.logical_or(cur, n > 0), 0.0, -1e30)
    return cur, cur_f, 1.0 - cur_f, bias


def _group_operands(g, kk, kk_sw, vv, vv_sw):
    lo = lax.broadcasted_iota(jnp.int32, (256, 128), 1) < 64
    k_a, k_b = (kk, kk_sw) if g == 0 else (kk_sw, kk)
    v_a, v_b = (vv, vv_sw) if g == 0 else (vv_sw, vv)
    return (jnp.concatenate([jnp.where(lo, k_a, 0.0), jnp.where(lo, 0.0, k_b)], axis=0).astype(BF),
            jnp.concatenate([jnp.where(lo, v_a, 0.0), jnp.where(lo, 0.0, v_b)], axis=0).astype(BF))


def _stack_pairs(ref, g):
    return jnp.concatenate([ref[:, (4 * g + p) * 128:(4 * g + p + 1) * 128] for p in range(PAIRS)], axis=0)


def _sink_row(sk_ref, g, which):
    return jnp.concatenate([jnp.full((1, 128), sk_ref[0, 8 * g + 2 * p + which], F32) for p in range(PAIRS)], axis=1)


def _fold(full, cur):
    return jnp.where(cur, full[128:], full[:128])


def _attn_probs(raw, fold, sink):
    s = _fold(raw, fold[0]) * (ATTN_HEAD_DIM ** -0.5) + fold[3]
    m = jnp.maximum(jnp.max(s, axis=0, keepdims=True), sink)
    e = jnp.exp(s - m)
    es = jnp.exp(sink - m)
    inv = 1.0 / (jnp.sum(e, axis=0, keepdims=True) + es)
    return e * inv, es * inv


def _unfold(p, fold):
    return jnp.concatenate([p * fold[2], p * fold[1]], axis=0).astype(BF)


def _attn_fwd(aq, ak, av, sinks):
    S = aq.shape[0]
    N = S // SEQ_CHUNK

    def body(sk_ref, q_ref, kp_ref, kc_ref, vp_ref, vc_ref, o_ref):
        kk, kk_sw = _kv_halves(kp_ref, kc_ref)
        vv, vv_sw = _kv_halves(vp_ref, vc_ref)
        fold = _attn_fold(pl.program_id(0))
        ops = [_group_operands(g, kk, kk_sw, vv, vv_sw) for g in range(2)]
        raw = [_dot_nt(ops[g][0], _stack_pairs(q_ref, g)) for g in range(2)]
        for g in range(2):
            probs = [_unfold(_attn_probs(raw[g][w * 256:(w + 1) * 256], fold, _sink_row(sk_ref, g, w))[0], fold)
                     for w in range(2)]
            o = _dot_tn(jnp.concatenate(probs, axis=0), ops[g][1])
            for p in range(PAIRS):
                o_ref[:, (4 * g + p) * 128:(4 * g + p + 1) * 128] = o[p * 128:(p + 1) * 128].astype(BF)

    prev = lambda i: (jnp.maximum(i - 1, 0), 0)
    return pl.pallas_call(
        body, name="attn_fwd", grid=(N,),
        in_specs=[_SMEM, _rows(128, 1024), pl.BlockSpec((128, 128), prev), _rows(128, 128),
                  pl.BlockSpec((128, 128), prev), _rows(128, 128)],
        out_specs=_rows(128, 1024),
        out_shape=jax.ShapeDtypeStruct((S, 1024), BF),
        compiler_params=_params(("parallel",)),
    )(sinks, aq, ak, ak, av, av)


def _group_norm(y):
    mu = jnp.mean(y, axis=-1, keepdims=True)
    yc = y - mu
    rstd = lax.rsqrt(jnp.mean(yc * yc, axis=-1, keepdims=True) + EPS)
    return yc * rstd, rstd


def _merge_fwd(x, ry, rg, ay, ga, gb, g_ret, w_ret, w_attn, w_out):
    S = x.shape[0]
    tm = 512

    def body(x_ref, ry_ref, rg_ref, ay_ref, ga_ref, gb_ref, gr_ref, wr_ref, wa_ref, wo_ref,
             u_ref, mg_ref, x1_ref):
        bb = _dot(ay_ref[...], wa_ref[...])
        for hd in range(RET_HEADS):
            vs = slice(hd * 256, (hd + 1) * 256)
            yn, _ = _group_norm(ry_ref[:, vs].astype(F32))
            rgv = rg_ref[:, vs].astype(F32)
            u_ref[:, vs] = (rgv * _sigmoid(rgv) * (yn * gr_ref[:, vs])).astype(BF)
        gated_b = _sigmoid(gb_ref[...].astype(F32)) * bb
        sa = _sigmoid(ga_ref[...].astype(F32))
        ba = _dot(u_ref[...], wr_ref[...])
        merged = (sa * ba + gated_b).astype(BF)
        mg_ref[...] = merged
        x1_ref[...] = x_ref[...] + _dot(merged, wo_ref[...])

    outs = [BF, BF, F32]
    return pl.pallas_call(
        body, name="merge_fwd", grid=(S // tm,),
        in_specs=[_rows(tm, 1024)] * 6 + [_fixed((1, 1024)), _VMEM, _VMEM, _VMEM],
        out_specs=[_rows(tm, 1024)] * 3,
        out_shape=[jax.ShapeDtypeStruct((S, 1024), dt) for dt in outs],
        compiler_params=_params(("parallel",)),
    )(x, ry, rg, ay, ga, gb, g_ret, w_ret, w_attn, w_out)


def _ffn_fwd(x1, tgt, g2, g3, w_gate, w_up, w_down):
    S = x1.shape[0]
    tm = 256

    def body(x1_ref, t_ref, g2_ref, g3_ref, wg_ref, wu_ref, wd_ref,
             gate_ref, up_ref, h2_ref, act_ref, dx2_ref, loss_ref, dg3_ref):
        @pl.when(pl.program_id(0) == 0)
        def _():
            loss_ref[...] = jnp.zeros_like(loss_ref)
            dg3_ref[...] = jnp.zeros_like(dg3_ref)

        x1v = x1_ref[...]
        r2 = lax.rsqrt(jnp.mean(x1v * x1v, axis=-1, keepdims=True) + EPS)
        h2 = (x1v * r2 * g2_ref[...]).astype(BF)
        h2_ref[...] = h2
        gate = _dot_nt(h2, wg_ref[...])
        up = _dot_nt(h2, wu_ref[...])
        gate_ref[...] = gate
        up_ref[...] = up
        act = (gate * _sigmoid(gate) * up).astype(BF)
        act_ref[...] = act
        x2 = x1v + _dot(act, wd_ref[...])
        r3 = lax.rsqrt(jnp.mean(x2 * x2, axis=-1, keepdims=True) + EPS)
        xh = x2 * r3
        g3v = g3_ref[...]
        err = xh * g3v - t_ref[...]
        loss_ref[...] += 0.5 * jnp.sum(jnp.mean(err * err, axis=-1, keepdims=True), axis=0, keepdims=True)
        dy = err * (1.0 / D_MODEL)
        dg3_ref[...] += jnp.sum(dy * xh, axis=0, keepdims=True)
        dxh = dy * g3v
        dx2_ref[...] = r3 * (dxh - xh * jnp.mean(dxh * xh, axis=-1, keepdims=True))

    return pl.pallas_call(
        body, name="ffn_fwd", grid=(S // tm,),
        in_specs=[_rows(tm, 1024), _rows(tm, 1024), _fixed((1, 1024)), _fixed((1, 1024)), _VMEM, _VMEM, _VMEM],
        out_specs=[_rows(tm, D_FF), _rows(tm, D_FF), _rows(tm, 1024), _rows(tm, D_FF), _rows(tm, 1024),
                   _fixed((1, LANES)), _fixed((1, 1024))],
        out_shape=[jax.ShapeDtypeStruct((S, D_FF), F32), jax.ShapeDtypeStruct((S, D_FF), F32),
                   jax.ShapeDtypeStruct((S, 1024), BF), jax.ShapeDtypeStruct((S, D_FF), BF),
                   jax.ShapeDtypeStruct((S, 1024), F32),
                   jax.ShapeDtypeStruct((1, LANES), F32), jax.ShapeDtypeStruct((1, 1024), F32)],
        compiler_params=_params(("arbitrary",)),
    )(x1, tgt, g2, g3, w_gate, w_up, w_down)


def _ffn_bwd(dx2, gate, up, x1, g2, w_gate, w_up, w_down):
    S = x1.shape[0]
    tm = 256

    def body(dx2_ref, gate_ref, up_ref, x1_ref, g2_ref, wg_ref, wu_ref, wd_ref,
             dgate_ref, dup_ref, dx1_ref, dg2_ref):
        @pl.when(pl.program_id(0) == 0)
        def _():
            dg2_ref[...] = jnp.zeros_like(dg2_ref)

        dx2v = dx2_ref[...]
        dact = _dot_nt(dx2v.astype(BF), wd_ref[...])
        gate = gate_ref[...]
        sg = _sigmoid(gate)
        dgate = (dact * up_ref[...] * (sg * (1.0 + gate * (1.0 - sg)))).astype(BF)
        dup = (dact * (gate * sg)).astype(BF)
        dgate_ref[...] = dgate
        dup_ref[...] = dup
        dh2 = _dot(dgate, wg_ref[...]) + _dot(dup, wu_ref[...])
        x1v = x1_ref[...]
        r2 = lax.rsqrt(jnp.mean(x1v * x1v, axis=-1, keepdims=True) + EPS)
        xh = x1v * r2
        dg2_ref[...] += jnp.sum(dh2 * xh, axis=0, keepdims=True)
        dxh = dh2 * g2_ref[...]
        dx1_ref[...] = dx2v + r2 * (dxh - xh * jnp.mean(dxh * xh, axis=-1, keepdims=True))

    return pl.pallas_call(
        body, name="ffn_bwd", grid=(S // tm,),
        in_specs=[_rows(tm, 1024), _rows(tm, D_FF), _rows(tm, D_FF), _rows(tm, 1024), _fixed((1, 1024)),
                  _VMEM, _VMEM, _VMEM],
        out_specs=[_rows(tm, D_FF), _rows(tm, D_FF), _rows(tm, 1024), _fixed((1, 1024))],
        out_shape=[jax.ShapeDtypeStruct((S, D_FF), BF), jax.ShapeDtypeStruct((S, D_FF), BF),
                   jax.ShapeDtypeStruct((S, 1024), F32), jax.ShapeDtypeStruct((1, 1024), F32)],
        compiler_params=_params(("arbitrary",)),
    )(dx2, gate, up, x1, g2, w_gate, w_up, w_down)


def _merge_bwd(dx1, ga, gb, u, ay, rg, ry, g_ret, w_ret, w_attn, w_out):
    S = dx1.shape[0]
    tm = 512

    def body(dx1_ref, ga_ref, gb_ref, u_ref, ay_ref, rg_ref, ry_ref, gr_ref, wr_ref, wa_ref, wo_ref,
             dgates_ref, drg_ref, dba_ref, dbb_ref, day_ref, dry_ref, dgr_ref):
        @pl.when(pl.program_id(0) == 0)
        def _():
            dgr_ref[...] = jnp.zeros_like(dgr_ref)

        dm = _dot_nt(dx1_ref[...].astype(BF), wo_ref[...])
        ba = _dot(u_ref[...], wr_ref[...])
        bb = _dot(ay_ref[...], wa_ref[...])
        sa = _sigmoid(ga_ref[...].astype(F32))
        sb = _sigmoid(gb_ref[...].astype(F32))
        dga = (dm * ba * (sa * (1.0 - sa))).astype(BF)
        dgb = (dm * bb * (sb * (1.0 - sb))).astype(BF)
        dgates_ref[:, :1024] = dga
        dgates_ref[:, 1024:] = dgb
        dba = (dm * sa).astype(BF)
        dbb = (dm * sb).astype(BF)
        dba_ref[...] = dba
        dbb_ref[...] = dbb
        day_ref[...] = _dot_nt(dbb, wa_ref[...]).astype(BF)
        du = _dot_nt(dba, wr_ref[...])
        for hd in range(RET_HEADS):
            vs = slice(hd * 256, (hd + 1) * 256)
            yn, rstd = _group_norm(ry_ref[:, vs].astype(F32))
            rgv = rg_ref[:, vs].astype(F32)
            sr = _sigmoid(rgv)
            silu = rgv * sr
            duv = du[:, vs]
            grv = gr_ref[:, vs]
            drg = (duv * (yn * grv) * (sr * (1.0 + rgv * (1.0 - sr)))).astype(BF)
            drg_ref[:, vs] = drg
            dyg = duv * silu
            dgr_ref[:, vs] += jnp.sum(dyg * yn, axis=0, keepdims=True)
            dyn = dyg * grv
            dry_ref[:, vs] = (rstd * (dyn - jnp.mean(dyn, axis=-1, keepdims=True)
                                      - yn * jnp.mean(dyn * yn, axis=-1, keepdims=True))).astype(KEPT)

    return pl.pallas_call(
        body, name="merge_bwd", grid=(S // tm,),
        in_specs=[_rows(tm, 1024)] * 7 + [_fixed((1, 1024)), _VMEM, _VMEM, _VMEM],
        out_specs=[_rows(tm, 2048), _rows(tm, 1024), _rows(tm, 1024), _rows(tm, 1024), _rows(tm, 1024),
                   _rows(tm, 1024), _fixed((1, 1024))],
        out_shape=[jax.ShapeDtypeStruct((S, 2048), BF), jax.ShapeDtypeStruct((S, 1024), BF),
                   jax.ShapeDtypeStruct((S, 1024), BF), jax.ShapeDtypeStruct((S, 1024), BF),
                   jax.ShapeDtypeStruct((S, 1024), BF), jax.ShapeDtypeStruct((S, 1024), KEPT),
                   jax.ShapeDtypeStruct((1, 1024), F32)],
        compiler_params=_params(("arbitrary",)),
    )(dx1, ga, gb, u, ay, rg, ry, g_ret, w_ret, w_attn, w_out)


def _attn_bwd(aq, ak, av, day, sinks, ca, sa, after):
    S = aq.shape[0]
    N = S // SEQ_CHUNK

    def body(sk_ref, q_ref, kp_ref, kc_ref, vp_ref, vc_ref, do_ref, cq_ref, sq_ref, cp_ref, sp_ref, after_ref,
             dq_ref, dk_ref, dv_ref, dsk_ref, carry_k, carry_v):
        n = pl.program_id(0)

        @pl.when(n == 0)
        def _():
            carry_k[...] = jnp.zeros_like(carry_k)
            carry_v[...] = jnp.zeros_like(carry_v)
            dsk_ref[...] = jnp.zeros_like(dsk_ref)

        @pl.when(n < N)
        def _():
            kk, kk_sw = _kv_halves(kp_ref, kc_ref)
            vv, vv_sw = _kv_halves(vp_ref, vc_ref)
            lo = lax.broadcasted_iota(jnp.int32, (256, 128), 1) < 64
            fold = _attn_fold(n)
            cur = fold[0]
            cq, sq = cq_ref[...], sq_ref[...]
            dk_groups, dv_groups = [], []
            ops = [_group_operands(g, kk, kk_sw, vv, vv_sw) for g in range(2)]
            stacked = [(_stack_pairs(q_ref, g), _stack_pairs(do_ref, g)) for g in range(2)]
            raw = [_dot_nt(ops[g][0], stacked[g][0]) for g in range(2)]
            raw_dp = [_dot_nt(ops[g][1], stacked[g][1]) for g in range(2)]
            for g in range(2):
                qs, dos = stacked[g]
                ds_parts, p_parts = [], []
                for which in range(2):
                    rows = slice(which * 256, (which + 1) * 256)
                    pr, ps = _attn_probs(raw[g][rows], fold, _sink_row(sk_ref, g, which))
                    dp = _fold(raw_dp[g][rows], cur)
                    dr = jnp.sum(pr * dp, axis=0, keepdims=True)
                    ds_parts.append(_unfold(pr * (dp - dr) * (ATTN_HEAD_DIM ** -0.5), fold))
                    p_parts.append(_unfold(pr, fold))
                    dsk_ref[2 * g + which:2 * g + which + 1, :] -= ps * dr
                ds = jnp.concatenate(ds_parts, axis=0)
                dq = _dot_tn(ds, ops[g][0])
                dk_both = _dot(ds, qs)
                dv_both = _dot(jnp.concatenate(p_parts, axis=0), dos)
                for p in range(PAIRS):
                    sl = slice((4 * g + p) * 128, (4 * g + p + 1) * 128)
                    dq_ref[:, sl] = _rot_attn_bwd(dq[p * 128:(p + 1) * 128], cq, sq).astype(BF)
                tk = jnp.where(lo, dk_both[:256], dk_both[256:])
                tv = jnp.where(lo, dv_both[:256], dv_both[256:])
                dk_groups.append(tk + pltpu.roll(tk, 64, 1))
                dv_groups.append(tv + pltpu.roll(tv, 64, 1))
            dk_full = jnp.where(lo, dk_groups[0], dk_groups[1])
            dv_full = jnp.where(lo, dv_groups[0], dv_groups[1])
            dk_ref[...] = _rot_attn_bwd(carry_k[...] + dk_full[:128], cp_ref[...], sp_ref[...]).astype(BF)
            dv_ref[...] = (carry_v[...] + dv_full[:128]).astype(BF)
            carry_k[...] = dk_full[128:]
            carry_v[...] = dv_full[128:]

        @pl.when(n == N)
        def _():
            dk_ref[...] = _rot_attn_bwd(carry_k[...], cp_ref[...], sp_ref[...]).astype(BF)
            dv_ref[...] = carry_v[...].astype(BF)

    cur = lambda i: (jnp.minimum(i, N - 1), 0)
    prev = lambda i: (jnp.maximum(i - 1, 0), 0)
    blk = lambda w, im: pl.BlockSpec((128, w), im)
    return pl.pallas_call(
        body, name="attn_bwd", grid=(N + 1,),
        in_specs=[_SMEM, blk(1024, cur), blk(128, prev), blk(128, cur), blk(128, prev), blk(128, cur),
                  blk(1024, cur), blk(128, cur), blk(128, cur), blk(128, prev), blk(128, prev), _ANY],
        out_specs=[blk(1024, cur), blk(128, prev), blk(128, prev), _fixed((4, PAIRS * 128))],
        out_shape=[jax.ShapeDtypeStruct((S, 1024), BF), jax.ShapeDtypeStruct((S, 128), BF),
                   jax.ShapeDtypeStruct((S, 128), BF), jax.ShapeDtypeStruct((4, PAIRS * 128), F32)],
        scratch_shapes=[pltpu.VMEM((128, 128), F32), pltpu.VMEM((128, 128), F32)],
        compiler_params=_params(("arbitrary",)),
    )(sinks, aq, ak, ak, av, av, day, ca, sa, ca, sa, after)


def _ret_bwd(rq, rk, rv, dry, states, dmat, qd, kd, cd, cr, sr, after):
    S = rq.shape[0]
    N = S // RET_CHUNK
    scale = RET_QK_DIM ** -0.5
    cps = RET_CHUNKS_PER_STEP
    steps = N // cps

    def body(q_ref, k_ref, v_ref, dy_ref, st_ref, dm_ref, qd_ref, kd_ref, cd_ref, cr_ref, sr_ref, after_ref,
             out_ref, dstate):
        @pl.when(pl.program_id(0) == 0)
        def _():
            dstate[...] = jnp.zeros_like(dstate)

        for hd in range(RET_HEADS):
            qs = slice(hd * 128, (hd + 1) * 128)
            vs = slice(hd * 256, (hd + 1) * 256)
            dmv = dm_ref[hd]
            free = []
            for ch in range(cps):
                rs = slice(ch * RET_CHUNK, (ch + 1) * RET_CHUNK)
                crv, srv = cr_ref[rs, :], sr_ref[rs, :]
                q = q_ref[rs, qs]
                kf = k_ref[rs, qs]
                k = kf.astype(BF)
                v = v_ref[rs, vs]
                do = dy_ref[rs, vs].astype(F32)
                dob = do.astype(BF)
                doq = (do * qd_ref[hd]).astype(BF)
                a = (_dot_nt(q, k) * dmv).astype(BF)
                da = (_dot_nt(dob, v) * dmv).astype(BF)
                dq = _dot(da, k) + _dot_nt(doq, st_ref[ch, hd])
                out_ref[rs, qs] = _rot_ret_bwd(dq * scale, crv, srv).astype(BF)
                free.append((rs, crv, srv, v, (kf * kd_ref[hd]).astype(BF),
                             _dot_tn(a, dob), _dot_tn(da, q), _dot_tn(q, doq)))
            ds = dstate[hd]
            for rs, crv, srv, v, kdv, dv_free, dk_free, ds_update in reversed(free):
                dsb = ds.astype(BF)
                dv = dv_free + _dot(kdv, dsb)
                dk = dk_free + _dot_nt(v, dsb) * kd_ref[hd]
                ds = ds * cd_ref[hd] + ds_update
                out_ref[rs, 512 + hd * 128:512 + (hd + 1) * 128] = _rot_ret_bwd(dk, crv, srv).astype(BF)
                out_ref[rs, 1024 + hd * 256:1024 + (hd + 1) * 256] = dv.astype(BF)
            dstate[hd] = ds

    rev = lambda w: pl.BlockSpec((cps * RET_CHUNK, w), lambda i: (steps - 1 - i, 0))
    return pl.pallas_call(
        body, name="ret_bwd", grid=(steps,),
        in_specs=[rev(512), rev(512), rev(1024), rev(1024),
                  pl.BlockSpec((cps, RET_HEADS, 128, 256), lambda i: (steps - 1 - i, 0, 0, 0)),
                  _VMEM, _VMEM, _VMEM, _VMEM, rev(128), rev(128), _ANY],
        out_specs=rev(2048),
        out_shape=jax.ShapeDtypeStruct((S, 2048), BF),
        scratch_shapes=[pltpu.VMEM((RET_HEADS, 128, 256), F32)],
        compiler_params=_params(("arbitrary",)),
    )(rq, rk, rv, dry, states, dmat, qd, kd, cd, cr, sr, after)


def _proj_bwd(d_pieces, x, dx1, g1, w_in_t):
    S = x.shape[0]
    tm = 512
    widths = [d.shape[1] for d in d_pieces]
    assert sum(widths) == D_IN
    nd = len(d_pieces)

    def body(*refs):
        d_refs = refs[:nd]
        x_ref, dx1_ref, g_ref, w_ref, gx_ref, dg_ref, db_ref = refs[nd:]

        @pl.when(pl.program_id(0) == 0)
        def _():
            dg_ref[...] = jnp.zeros_like(dg_ref)
            db_ref[...] = jnp.zeros_like(db_ref)

        dh = jnp.zeros((tm, D_MODEL), F32)
        lo = 0
        for ref, n in zip(d_refs, widths):
            v = ref[...]
            dh = dh + _dot(v, w_ref[lo:lo + n, :])
            db_ref[:, lo:lo + n] += jnp.sum(v.astype(F32), axis=0, keepdims=True)
            lo += n
        xv = x_ref[...]
        r = lax.rsqrt(jnp.mean(xv * xv, axis=-1, keepdims=True) + EPS)
        xh = xv * r
        dg_ref[...] += jnp.sum(dh * xh, axis=0, keepdims=True)
        dxh = dh * g_ref[...]
        gx_ref[...] = dx1_ref[...] + r * (dxh - xh * jnp.mean(dxh * xh, axis=-1, keepdims=True))

    return pl.pallas_call(
        body, name="proj_bwd", grid=(S // tm,),
        in_specs=[_rows(tm, n) for n in widths] + [_rows(tm, 1024), _rows(tm, 1024), _fixed((1, 1024)), _VMEM],
        out_specs=[_rows(tm, 1024), _fixed((1, 1024)), _fixed((1, D_IN))],
        out_shape=[jax.ShapeDtypeStruct((S, 1024), F32), jax.ShapeDtypeStruct((1, 1024), F32),
                   jax.ShapeDtypeStruct((1, D_IN), F32)],
        compiler_params=_params(("arbitrary",)),
    )(*d_pieces, x, dx1, g1, w_in_t)


SHARD_ROWS = (1600, 256, 256, 256, 704, 704, 704)


class Group(NamedTuple):
    name: str
    tensors: tuple
    offs: tuple
    block: int
    tile: int

    @property
    def half(self):
        return self.block // 2

    def off(self, t):
        return self.offs[self.tensors.index(t)]


LATE = Group("late", (0,), (0,), 1600, 800)
EARLY = Group("early", (1, 2, 3, 4, 5, 6), (0, 256, 512, 768, 1472, 2176), 2880, 480)
GROUP_OF = {t: g for g in (LATE, EARLY) for t in g.tensors}


def _wgrad(a_list, b, t, packed, name, row0=0):
    S = b.shape[0]
    widths = [a.shape[1] for a in a_list]
    M = sum(widths)
    ts = 1024
    steps = S // ts
    na = len(a_list)
    grp = GROUP_OF[t]
    r, off = SHARD_ROWS[t], grp.off(t)
    segments = []
    for chip in range(N_CHIPS):
        lo, hi = max(chip * r, row0), min((chip + 1) * r, row0 + M)
        if lo < hi:
            segments.append((lo - row0, hi - lo, chip * grp.block + off + lo - chip * r))
    assert sum(n for _, n, _ in segments) == M
    stage_rows = max(n for _, n, _ in segments)
    n_in = na + 1 + (2 if packed is not None else 0)

    def body(*refs):
        a_refs, b_ref = refs[:na], refs[na]
        g32_ref, gb_ref, acc, stage, sems = refs[n_in:]
        i = pl.program_id(0)

        @pl.when(i == 0)
        def _():
            acc[...] = jnp.zeros_like(acc)

        bv = b_ref[...].astype(BF)
        lo = 0
        for a_ref, n in zip(a_refs, widths):
            acc[lo:lo + n, :] += _dot_tn(a_ref[...].astype(BF), bv)
            lo += n

        @pl.when(i == steps - 1)
        def _():
            for a0, n, p0 in segments:
                c32 = pltpu.make_async_copy(acc.at[a0:a0 + n, :], g32_ref.at[pl.ds(p0, n), :], sems.at[0])
                c32.start()
                stage[0:n, :] = acc[a0:a0 + n, :].astype(BF)
                c16 = pltpu.make_async_copy(stage.at[0:n, :], gb_ref.at[pl.ds(p0, n), :], sems.at[1])
                c16.start()
                c32.wait()
                c16.wait()

    shapes = [jax.ShapeDtypeStruct((N_CHIPS * grp.block, 1024), F32),
              jax.ShapeDtypeStruct((N_CHIPS * grp.block, 1024), BF)]
    operands = list(a_list) + [b]
    in_specs = [_rows(ts, n) for n in widths] + [_rows(ts, 1024)]
    aliases = {}
    if packed is not None:
        operands += list(packed)
        in_specs += [_ANY, _ANY]
        aliases = {na + 1: 0, na + 2: 1}
    return pl.pallas_call(
        body, name=name, grid=(steps,),
        in_specs=in_specs, out_specs=[_ANY, _ANY], out_shape=shapes,
        scratch_shapes=[pltpu.VMEM((M, 1024), F32), pltpu.VMEM((stage_rows, 1024), BF),
                        pltpu.SemaphoreType.DMA((2,))],
        input_output_aliases=aliases,
        compiler_params=_params(("arbitrary",)),
    )(*operands)


def _stage_weight(place, w, name, after=()):
    r = w.shape[0]
    tr = 800 if r == 1600 else r
    nb = r // tr

    def body(p_ref, w_ref, *rest):
        rest[-1][...] = w_ref[...].astype(BF)

    return pl.pallas_call(
        body, name=name,
        grid_spec=pltpu.PrefetchScalarGridSpec(
            num_scalar_prefetch=1, grid=(nb,),
            in_specs=[pl.BlockSpec((tr, 1024), lambda i, p: (i, 0))] + [_ANY] * len(after),
            out_specs=pl.BlockSpec((tr, 1024), lambda i, p: (p[0] * nb + i, 0))),
        out_shape=jax.ShapeDtypeStruct((N_CHIPS * r, 1024), BF),
        compiler_params=_params(("arbitrary",)),
    )(place, w, *after)


def _kept_block(grp, nb):
    return lambda i, p: ((2 * (i // nb) + p[1]) * nb + i % nb, 0)


def _pair_sum(place, g32, got, grp):
    nb = grp.half // grp.tile

    def body(p_ref, g_ref, r_ref, s16_ref):
        s16_ref[...] = (g_ref[...] + r_ref[...].astype(F32)).astype(BF)

    flat = pl.BlockSpec((grp.tile, 1024), lambda i, p: (i, 0))
    return pl.pallas_call(
        body, name="pair_sum_" + grp.name,
        grid_spec=pltpu.PrefetchScalarGridSpec(
            num_scalar_prefetch=1, grid=(N_CHIPS * nb,),
            in_specs=[pl.BlockSpec((grp.tile, 1024), _kept_block(grp, nb)), flat], out_specs=flat),
        out_shape=jax.ShapeDtypeStruct((N_CHIPS * grp.half, 1024), BF),
        compiler_params=_params(("arbitrary",)),
    )(place, g32, got)


def _chip_sum(place, g32, got, others, grp):
    nb = grp.half // grp.tile

    def body(p_ref, g_ref, r_ref, r0_ref, r1_ref, r2_ref, o_ref):
        own = g_ref[...] + r_ref[...].astype(F32)
        o_ref[...] = ((own + r0_ref[...].astype(F32)) + r1_ref[...].astype(F32)) + r2_ref[...].astype(F32)

    blk = lambda f: pl.BlockSpec((grp.tile, 1024), f)
    return pl.pallas_call(
        body, name="chip_sum_" + grp.name,
        grid_spec=pltpu.PrefetchScalarGridSpec(
            num_scalar_prefetch=1, grid=(nb,),
            in_specs=[blk(lambda i, p: ((2 * p[0] + p[1]) * nb + i, 0)), blk(lambda i, p: (p[0] * nb + i, 0)),
                      blk(lambda i, p: (i, 0)), blk(lambda i, p: (nb + i, 0)), blk(lambda i, p: (2 * nb + i, 0))],
            out_specs=blk(lambda i, p: (p[1] * nb + i, 0))),
        out_shape=jax.ShapeDtypeStruct((grp.block, 1024), F32),
        compiler_params=_params(("arbitrary",)),
    )(place, g32, got, others, others, others)


def _adamw(w, g, m, v, name, g_row0=0, after=None):
    r = w.shape[0]
    tr = 320 if r % 320 == 0 and g_row0 % 320 == 0 else (64 if r % 64 == 0 else r)
    assert g_row0 % tr == 0
    g0 = g_row0 // tr
    extra = [] if after is None else [after]

    def body(w_ref, g_ref, m_ref, v_ref, *rest):
        go_ref, d_ref, mo_ref, vo_ref = rest[len(extra):]
        gv = g_ref[...]
        mn = ADAM_B1 * m_ref[...] + (1.0 - ADAM_B1) * gv
        vn = ADAM_B2 * v_ref[...] + (1.0 - ADAM_B2) * (gv * gv)
        m_hat = mn / (1.0 - ADAM_B1 ** ADAM_STEP)
        v_hat = vn / (1.0 - ADAM_B2 ** ADAM_STEP)
        go_ref[...] = gv
        d_ref[...] = -ADAM_LR * (m_hat / (jnp.sqrt(v_hat) + ADAM_EPS) + ADAM_WD * w_ref[...])
        mo_ref[...] = mn
        vo_ref[...] = vn

    return pl.pallas_call(
        body, name=name, grid=(r // tr,),
        in_specs=[_rows(tr, 1024), pl.BlockSpec((tr, 1024), lambda i: (g0 + i, 0)), _rows(tr, 1024), _rows(tr, 1024)]
        + [_ANY] * len(extra),
        out_specs=[_rows(tr, 1024)] * 4,
        out_shape=[jax.ShapeDtypeStruct((r, 1024), F32)] * 4,
        compiler_params=_params(("parallel",)),
    )(w, g, m, v, *extra)


def _place():
    x, y, c = lax.axis_index("x"), lax.axis_index("y"), lax.axis_index("c")
    chips = [(1 - x, y), (x, 1 - y), (1 - x, 1 - y)]
    return x, y, c, chips


_HBM =pl.BlockSpec(memory_space=pltpu.HBM)
_SEM = pl.BlockSpec(memory_space=pltpu.SEMAPHORE)
_DATAFLOW = pltpu.SideEffectType.DATAFLOW_SIDE_EFFECTING


def _exchange(name, arrays, n, plan):
    na = len(arrays)

    def body(*refs):
        pairs = plan(refs[na:2 * na], refs[2 * na], refs[2 * na + 1])
        for send, _ in pairs:
            send.start()
        for send, recv in pairs:
            send.wait_send()
            recv.wait_recv()

    return pl.pallas_call(
        body, name=name, in_specs=[_ANY] * na, out_specs=[_ANY] * na,
        out_shape=[jax.ShapeDtypeStruct(a.shape, a.dtype) for a in arrays],
        scratch_shapes=[pltpu.SemaphoreType.DMA((n,)), pltpu.SemaphoreType.DMA((n,))],
        input_output_aliases={i: i for i in range(na)},
    )(*arrays)


def _exchange_start(name, arrays, n, plan, after=()):
    na = len(arrays)

    def body(*refs):
        for send, _ in plan(refs[:na], refs[na + len(after)], refs[na + len(after) + 1]):
            send.start()
        refs[-1][...] = jnp.zeros_like(refs[-1])

    out = pl.pallas_call(
        body, name=name,
        out_shape=(pltpu.SemaphoreType.DMA((n,)), pltpu.SemaphoreType.DMA((n,)),
                   *[pltpu.HBM(a.shape, a.dtype) for a in arrays], jax.ShapeDtypeStruct((8, LANES), F32)),
        in_specs=[_HBM] * na + [_ANY] * len(after), out_specs=(_SEM, _SEM, *[_HBM] * na, _VMEM),
        input_output_aliases={i: 2 + i for i in range(na)},
        compiler_params=pltpu.CompilerParams(has_side_effects=_DATAFLOW),
    )(*[pltpu.with_memory_space_constraint(a, pltpu.HBM) for a in arrays], *after)
    return (out[0], out[1]), list(out[2:2 + na]), out[-1]


def _exchange_wait(name, sems, in_flight, plan, after=()):
    na = len(in_flight)

    def body(*refs):
        for send, recv in plan(refs[:na], refs[na], refs[na + 1]):
            send.wait_send()
            recv.wait_recv()

    out = pl.pallas_call(
        body, name=name,
        out_shape=tuple(pltpu.HBM(a.shape, a.dtype) for a in in_flight),
        in_specs=[_HBM] * na + [_SEM, _SEM] + [_ANY] * len(after), out_specs=tuple([_HBM] * na),
        input_output_aliases={i: i for i in range(na)},
        compiler_params=pltpu.CompilerParams(has_side_effects=_DATAFLOW),
    )(*in_flight, *sems, *after)
    return list(out)


def _remote(src, dst, send_sems, recv_sems, k, to):
    return pltpu.make_async_remote_copy(src_ref=src, dst_ref=dst, send_sem=send_sems.at[k], recv_sem=recv_sems.at[k],
                                        device_id=to, device_id_type=MESH)


def _plan_gather_direct(tids):
    def plan(refs, ss, rs):
        x, y, c, chips = _place()
        j = 2 * x + y
        pairs = []
        for t, tid in enumerate(tids):
            r = SHARD_ROWS[tid]
            for k, (cx, cy) in enumerate(chips):
                mine = refs[t].at[pl.ds(j * r, r), :]
                theirs = refs[t].at[pl.ds((2 * cx + cy) * r, r), :]
                pairs.append((_remote(mine, mine, ss, rs, 3 * t + k, (cx, cy, c)),
                              _remote(theirs, theirs, ss, rs, 3 * t + k, (x, y, c))))
        return pairs
    return plan


def _half_rows(ref, tid, pj, half):
    h = SHARD_ROWS[tid] // 2
    return ref.at[pl.ds((2 * pj + half) * h, h), :]


def _plan_gather_halves(tid):
    def plan(refs, ss, rs):
        x, y, c, chips = _place()
        mine = _half_rows(refs[0], tid, 2 * x + y, c)
        return [(_remote(mine, mine, ss, rs, k, (cx, cy, c)),
                 _remote(_half_rows(refs[0], tid, 2 * cx + cy, c), _half_rows(refs[0], tid, 2 * cx + cy, c),
                         ss, rs, k, (x, y, c))) for k, (cx, cy) in enumerate(chips)]
    return plan


def _plan_forward_halves(tid):
    def plan(refs, ss, rs):
        x, y, c, chips = _place()
        pairs = []
        for k, (cx, cy) in enumerate(chips):
            got = _half_rows(refs[0], tid, 2 * cx + cy, c)
            missing = _half_rows(refs[0], tid, 2 * cx + cy, 1 - c)
            pairs.append((_remote(got, got, ss, rs, k, (x, y, 1 - c)), _remote(missing, missing, ss, rs, k, (x, y, 1 - c))))
        return pairs
    return plan


def _plan_pair_exchange(grp):
    def plan(refs, ss, rs):
        x, y, c, _ = _place()
        cps = [_remote(refs[0].at[pl.ds(pj * grp.block + (1 - c) * grp.half, grp.half), :],
                       refs[1].at[pl.ds(pj * grp.half, grp.half), :], ss, rs, pj, (x, y, 1 - c))
               for pj in range(N_CHIPS)]
        return [(cp, cp) for cp in cps]
    return plan


def _plan_to_chips(grp):
    def plan(refs, ss, rs):
        x, y, c, chips = _place()
        cps = [_remote(refs[0].at[pl.ds((2 * cx + cy) * grp.half, grp.half), :],
                       refs[1].at[pl.ds(k * grp.half, grp.half), :], ss, rs, k, (cx, cy, c))
               for k, (cx, cy) in enumerate(chips)]
        return [(cp, cp) for cp in cps]
    return plan


def _plan_share_halves(grp):
    def plan(refs, ss, rs):
        x, y, c, _ = _place()
        mine = refs[0].at[pl.ds(c * grp.half, grp.half), :]
        other = refs[0].at[pl.ds((1 - c) * grp.half, grp.half), :]
        return [(_remote(mine, mine, ss, rs, 0, (x, y, 1 - c)), _remote(other, other, ss, rs, 0, (x, y, 1 - c)))]
    return plan


def _empty(rows, dtype):
    return lax.empty((rows, 1024), dtype)


def _all_reduce_small(v, after):
    m_per, n = v.shape
    n_dev = 8

    def body(x_ref, after_ref, out_ref, all_ref, send_sems, recv_sems, local_sem):
        x, y, c, chips = _place()
        me, sibling = (x, y, c), (x, y, 1 - c)

        def rows(px, py, pc):
            return all_ref.at[pl.ds((4 * px + 2 * py + pc) * m_per, m_per), :]

        def copy(k, block, to, src=None):
            return pltpu.make_async_remote_copy(
                src_ref=rows(*block) if src is None else src, dst_ref=rows(*block),
                send_sem=send_sems.at[k], recv_sem=recv_sems.at[k], device_id=to, device_id_type=MESH)

        mine = pltpu.make_async_copy(x_ref, rows(*me), local_sem)
        mine.start()
        first = [copy(0, me, sibling, src=x_ref)]
        first += [copy(1 + k, me, (*chip, c), src=x_ref) for k, chip in enumerate(chips)]
        for cp in first:
            cp.start()
        passed = [copy(4 + k, (*chip, c), sibling) for k, chip in enumerate(chips)]
        for k, chip in enumerate(chips):
            copy(1 + k, (*chip, c), me).wait_recv()
            passed[k].start()
        copy(0, sibling, me).wait_recv()
        for k, chip in enumerate(chips):
            copy(4 + k, (*chip, 1 - c), me).wait_recv()
        for cp in first + passed:
            cp.wait_send()
        mine.wait()
        acc = all_ref[pl.ds(0, m_per), :]
        for d in range(1, n_dev):
            acc = acc + all_ref[pl.ds(d * m_per, m_per), :]
        out_ref[...] = acc

    return pl.pallas_call(
        body, name="all_reduce_small",
        in_specs=[_VMEM, _ANY], out_specs=_VMEM,
        out_shape=jax.ShapeDtypeStruct((m_per, n), F32),
        scratch_shapes=[pltpu.VMEM((n_dev * m_per, n), F32),
                        pltpu.SemaphoreType.DMA((7,)), pltpu.SemaphoreType.DMA((7,)), pltpu.SemaphoreType.DMA],
        compiler_params=pltpu.CompilerParams(has_side_effects=True),
    )(v, after)


def _tables(S, token):
    pos = jnp.arange(S, dtype=jnp.int32).astype(F32) + token[0, 0]

    def cs(half):
        inv_freq = ROPE_THETA ** (-jnp.arange(half, dtype=F32) / half)
        ang = pos[:, None] * inv_freq[None, :]
        return jnp.cos(ang), jnp.sin(ang)

    c64, s64 = cs(64)
    c32, s32 = cs(32)
    cr = jnp.concatenate([c64, c64], axis=1)
    sr = jnp.concatenate([-s64, s64], axis=1)
    ca = jnp.concatenate([c32, c32, c32, c32], axis=1)
    sa = jnp.concatenate([-s32, s32, -s32, s32], axis=1)
    return cr, sr, ca, sa


def _decays():
    H, C = RET_HEADS, RET_CHUNK
    log_gamma = jnp.log1p(-jnp.exp2(-5.0 - jnp.arange(H, dtype=F32)))
    idx = jnp.arange(C, dtype=F32)
    rel = idx[:, None] - idx[None, :]
    dmat = jnp.where(rel[None] >= 0, jnp.exp(log_gamma[:, None, None] * jnp.maximum(rel, 0.0)[None]), 0.0)
    qd = jnp.exp(log_gamma[:, None] * (idx + 1.0))[:, :, None]
    kd = jnp.exp(log_gamma[:, None] * (C - 1.0 - idx))[:, :, None]
    cd = jnp.exp(log_gamma * C)[:, None, None]
    return dmat, qd, kd, cd


def _local_step(place, x, tgt, ln1_g, b_in, ret_norm_g, attn_sinks, ln2_g, lnf_g, w_in, other_weights, tables, decays):
    S = x.shape[0]
    cr, sr, ca, sa = tables
    dmat, qd, kd, cd = decays

    h, rq, rk, rv, rg, aq, ak, av, ga, gb = _proj_fwd(x, ln1_g, w_in, b_in, cr, sr, ca, sa)
    ry, states = _ret_fwd(rq, rk, rv, dmat, qd, kd, cd)
    ay = _attn_fwd(aq, ak, av, attn_sinks)
    w_ret, w_attn, w_out, w_gate, w_up, w_down = other_weights(ay)
    u, merged, x1 = _merge_fwd(x, ry, rg, ay, ga, gb, ret_norm_g, w_ret, w_attn, w_out)
    gate, up, h2, act, dx2, loss, dg3 = _ffn_fwd(x1, tgt, ln2_g, lnf_g, w_gate, w_up, w_down)

    dgate, dup, dx1, dg2 = _ffn_bwd(dx2, gate, up, x1, ln2_g, w_gate, w_up, w_down)
    early = _wgrad([act], dx2, 6, None, "wgrad_down")
    early = _wgrad([dgate], h2, 4, early, "wgrad_gate")
    early = _wgrad([dup], h2, 5, early, "wgrad_up")
    dgates, drg, dba, dbb, day, dry, dgr = _merge_bwd(dx1, ga, gb, u, ay, rg, ry, ret_norm_g,
                                                                w_ret, w_attn, w_out)
    early = _wgrad([merged], dx1, 3, early, "wgrad_out")
    early = _wgrad([u], dba, 1, early, "wgrad_ret")
    e32, e16 = _wgrad([ay], dbb, 2, early, "wgrad_attn")

    grp = EARLY
    sems, flight, token = _exchange_start("pair_start_early", [e16, _empty(N_CHIPS * grp.half, BF)], N_CHIPS,
                                          _plan_pair_exchange(grp))
    daq, dak, dav, dsk = _attn_bwd(aq, ak, av, day, attn_sinks, ca, sa, token)
    _, got = _exchange_wait("pair_wait_early", sems, flight, _plan_pair_exchange(grp), after=(dsk,))
    s16 = _pair_sum(place, e32, got, grp)
    sems, flight, token = _exchange_start("chips_start_early", [s16, _empty(3 * grp.half, BF)], 3, _plan_to_chips(grp))
    d_ret = _ret_bwd(rq, rk, rv, dry, states, dmat, qd, kd, cd, cr, sr, token)
    d_proj = [d_ret, drg, daq, dak, dav, dgates]
    grad_x, dg1, db = _proj_bwd(d_proj, x, dx1, ln1_g, w_in)
    late = _wgrad(d_proj[:2], h, 0, None, "wgrad_in_a")
    late = _wgrad(d_proj[2:], h, 0, late, "wgrad_in_b", row0=d_ret.shape[1] + drg.shape[1])

    def finish_early(after):
        _, others = _exchange_wait("chips_wait_early", sems, flight, _plan_to_chips(grp), after=after)
        return _chip_sum(place, e32, got, others, grp)

    d_sinks = dsk.reshape(2, 2, PAIRS, 128).sum(-1).transpose(0, 2, 1).reshape(1, ATTN_Q_HEADS)
    small = dict(loss=loss, ln1_g=dg1, b_in=db, ret_norm_g=dgr, attn_sinks=d_sinks,
                 ln2_g=dg2, lnf_g=dg3)
    return grad_x, finish_early, late, small


SMALL_ROWS = 16


def _pack_small(ln1, ret, ln2, lnf, b_in, sinks, loss, name):
    def body(a_ref, b_ref, c_ref, d_ref, bi_ref, sk_ref, ls_ref, o_ref):
        o_ref[...] = jnp.zeros_like(o_ref)
        for row, ref in enumerate((a_ref, b_ref, c_ref, d_ref)):
            o_ref[row:row + 1, :] = ref[...]
        for k in range(6):
            o_ref[4 + k:5 + k, :] = bi_ref[:, k * 1024:(k + 1) * 1024]
        o_ref[10:11, 0:D_IN - 6144] = bi_ref[:, 6144:D_IN]
        o_ref[11:12, 0:ATTN_Q_HEADS] = sk_ref[...]
        o_ref[12:13, 0:LANES] = ls_ref[...]

    return pl.pallas_call(
        body, name=name, in_specs=[_VMEM] * 7, out_specs=_VMEM,
        out_shape=jax.ShapeDtypeStruct((SMALL_ROWS, 1024), F32),
    )(ln1.reshape(1, 1024), ret.reshape(1, 1024), ln2.reshape(1, 1024), lnf.reshape(1, 1024),
      b_in.reshape(1, D_IN), sinks.reshape(1, ATTN_Q_HEADS), loss.reshape(1, LANES))


def _unpack_small(p):
    return dict(ln1_g=p[0:1], ret_norm_g=p[1:2], ln2_g=p[2:3], lnf_g=p[3], b_in=p[4:11].reshape(1, -1)[:, :D_IN],
                attn_sinks=p[11:12, :ATTN_Q_HEADS], loss=p[12, 0])


def kernel(x, ln1_g, w_in, b_in, ret_norm_g, w_ret_out, attn_sinks, w_attn_out, w_out, ln2_g, w_ffn_gate, w_ffn_up, w_ffn_down, lnf_g, loss_target, m_ln1_g, m_w_in, m_b_in, m_ret_norm_g, m_w_ret_out, m_attn_sinks, m_w_attn_out, m_w_out, m_ln2_g, m_w_ffn_gate, m_w_ffn_up, m_w_ffn_down, m_lnf_g, v_ln1_g, v_w_in, v_b_in, v_ret_norm_g, v_w_ret_out, v_attn_sinks, v_w_attn_out, v_w_out, v_ln2_g, v_w_ffn_gate, v_w_ffn_up, v_w_ffn_down, v_lnf_g):
    xi = lax.axis_index("x")
    yi = lax.axis_index("y")
    ci = lax.axis_index("c")
    j = 2 * xi + yi

    place = jnp.stack([j, ci]).astype(jnp.int32)

    names = ["w_in", "w_ret_out", "w_attn_out", "w_out", "w_ffn_gate", "w_ffn_up", "w_ffn_down"]
    transposed = [True, False, False, False, True, True, False]
    view = lambda a, t: a[0].T if t else a[0]
    big_w = [view(a, t) for a, t in zip([w_in, w_ret_out, w_attn_out, w_out, w_ffn_gate, w_ffn_up, w_ffn_down], transposed)]
    big_m = [view(a, t) for a, t in zip([m_w_in, m_w_ret_out, m_w_attn_out, m_w_out, m_w_ffn_gate, m_w_ffn_up, m_w_ffn_down], transposed)]
    big_v = [view(a, t) for a, t in zip([v_w_in, v_w_ret_out, v_w_attn_out, v_w_out, v_w_ffn_gate, v_w_ffn_up, v_w_ffn_down], transposed)]

    staged_in = _stage_weight(place, big_w[0], "stage_" + names[0])
    sems, in_flight, token = _exchange_start("gather_in_start", [staged_in], 3, _plan_gather_halves(0))
    staged = [_stage_weight(place, w, "stage_" + n, after=(token,)) for n, w in zip(names[1:], big_w[1:])]
    tables = _tables(x.shape[1], token)
    decays = _decays()
    w_in_full = _exchange_wait("gather_in_wait", sems, in_flight, _plan_gather_halves(0),
                               after=(*staged, *tables, *decays))[0]
    w_in_full = _exchange("gather_in_forward", [w_in_full], 3, _plan_forward_halves(0))[0]
    rest = list(range(1, 7))
    sems, in_flight, token = _exchange_start("gather_start", staged, 3 * len(rest), _plan_gather_direct(rest),
                                             after=(w_in_full,))
    other_weights = lambda after: _exchange_wait("gather_wait", sems, in_flight, _plan_gather_direct(rest),
                                                 after=(after,))

    grad_x, finish_early, (l32, l16), small = _local_step(
        place, x[0], loss_target[0], ln1_g, b_in + token[0, 0], ret_norm_g, attn_sinks, ln2_g, lnf_g.reshape(1, -1),
        w_in_full, other_weights, tables, decays)
    pg = _all_reduce_small(_pack_small(
        small["ln1_g"], small["ret_norm_g"], small["ln2_g"], small["lnf_g"], small["b_in"],
        small["attn_sinks"], small["loss"], "pack_small_grads"), l32)
    early_shard = _exchange("share_early", [finish_early((l32,))], 1, _plan_share_halves(EARLY))[0]
    _, got = _exchange("pair_exchange_late", [l16, _empty(N_CHIPS * LATE.half, BF)], N_CHIPS, _plan_pair_exchange(LATE))
    s16 = _pair_sum(place, l32, got, LATE)
    sems, flight, token = _exchange_start("chips_start_late", [s16, _empty(3 * LATE.half, BF)], 3, _plan_to_chips(LATE),
                                          after=(pg, early_shard))

    upd = {}

    def update(t, shard, after):
        res = _adamw(big_w[t], shard, big_m[t], big_v[t], "adamw_" + names[t], g_row0=GROUP_OF[t].off(t), after=after)
        upd[names[t]] = tuple((r.T if transposed[t] else r)[None] for r in res)
        return res[0]

    done = [update(t, early_shard, token) for t in EARLY.tensors]

    sm = _unpack_small(pg)
    z = jnp.zeros((1, LANES), F32)
    pw = _pack_small(ln1_g, ret_norm_g, ln2_g, lnf_g, b_in, attn_sinks, z, "pack_small_w")
    pm = _pack_small(m_ln1_g, m_ret_norm_g, m_ln2_g, m_lnf_g, m_b_in, m_attn_sinks, z, "pack_small_m")
    pv = _pack_small(v_ln1_g, v_ret_norm_g, v_ln2_g, v_lnf_g, v_b_in, v_attn_sinks, z, "pack_small_v")
    small_res = _adamw(pw, pg, pm, pv, "adamw_small", after=token)
    sd, smn, svn = (_unpack_small(t) for t in small_res[1:])
    for n in ["ln1_g", "b_in", "ret_norm_g", "attn_sinks", "ln2_g", "lnf_g"]:
        upd[n] = (sm[n], sd[n], smn[n], svn[n])

    _, others = _exchange_wait("chips_wait_late", sems, flight, _plan_to_chips(LATE), after=(*done, small_res[1]))
    late_shard = _exchange("share_late", [_chip_sum(place, l32, got, others, LATE)], 1, _plan_share_halves(LATE))[0]
    update(0, late_shard, None)

    order_w = ["ln1_g", "w_in", "b_in", "ret_norm_g", "w_ret_out", "attn_sinks", "w_attn_out", "w_out",
               "ln2_g", "w_ffn_gate", "w_ffn_up", "w_ffn_down", "lnf_g"]
    outs = [sm["loss"], grad_x[None]]
    for k in range(4):
        outs += [upd[n][k] for n in order_w]
    return tuple(outs)
```

```python
from typing import NamedTuple

import jax
import jax.numpy as jnp
from jax import lax
from jax.experimental import pallas as pl
from jax.experimental.pallas import tpu as pltpu

D_MODEL = 1024
SEQ_CHUNK = 128
RET_CHUNK = 256
RET_CHUNKS_PER_STEP = 2
RET_HEADS = 4
RET_QK_DIM = 128
RET_V_DIM = 256
ATTN_Q_HEADS = 16
ATTN_HEAD_DIM = 64
D_FF = 2816
D_IN = 6400
ROPE_THETA = 10000.0
EPS = 1e-6
N_CHIPS = 4

ADAM_LR = 0.001
ADAM_B1 = 0.9
ADAM_B2 = 0.999
ADAM_EPS = 1e-08
ADAM_WD = 0.01
ADAM_STEP = 10

OFF_RQ, OFF_RK, OFF_RV, OFF_RG = 0, 512, 1024, 2048
OFF_AQ, OFF_AK, OFF_AV, OFF_GA, OFF_GB = 3072, 4096, 4224, 4352, 5376

V7X_VMEM_LIMIT = 56 * 1024 * 1024
LANES = 128

BF = jnp.bfloat16
F32 = jnp.float32
KEPT = jnp.bfloat16
MESH = pl.DeviceIdType.MESH

_VMEM = pl.BlockSpec(memory_space=pltpu.VMEM)
_SMEM = pl.BlockSpec(memory_space=pltpu.SMEM)
_ANY = pl.BlockSpec(memory_space=pl.ANY)


def _dot(a, b):
    return jnp.dot(a, b, preferred_element_type=F32)


def _dot_nt(a, b):
    return lax.dot_general(a, b, (((1,), (1,)), ((), ())), preferred_element_type=F32)


def _dot_tn(a, b):
    return lax.dot_general(a, b, (((0,), (0,)), ((), ())), preferred_element_type=F32)


def _sigmoid(x):
    return 0.5 * jnp.tanh(0.5 * x) + 0.5


def _rows(tm, n):
    return pl.BlockSpec((tm, n), lambda i: (i, 0))


def _fixed(shape):
    nd = len(shape)
    return pl.BlockSpec(shape, lambda i: (0,) * nd)


def _params(sem, **kw):
    return pltpu.CompilerParams(dimension_semantics=sem, vmem_limit_bytes=V7X_VMEM_LIMIT, **kw)


def _rot_ret(p, cr, sr):
    return p * cr + pltpu.roll(p, 64, 1) * sr


def _rot_ret_bwd(d, cr, sr):
    return d * cr - pltpu.roll(d, 64, 1) * sr


def _attn_partner(p):
    lane = lax.broadcasted_iota(jnp.int32, p.shape, 1)
    first = (lane % 64) < 32
    return jnp.where(first, pltpu.roll(p, 96, 1), pltpu.roll(p, 32, 1))


def _rot_attn(p, ca, sa):
    return p * ca + _attn_partner(p) * sa


def _rot_attn_bwd(d, ca, sa):
    return d * ca - _attn_partner(d) * sa


def _proj_fwd(x, g1, w_in, b_in, cr, sr, ca, sa):
    S = x.shape[0]
    tm = 512
    scale = RET_QK_DIM ** -0.5

    def body(x_ref, g_ref, w_ref, b_ref, cr_ref, sr_ref, ca_ref, sa_ref,
             h_ref, rq_ref, rk_ref, rv_ref, rg_ref, aq_ref, ak_ref, av_ref, ga_ref, gb_ref):
        xv = x_ref[...]
        r = lax.rsqrt(jnp.mean(xv * xv, axis=-1, keepdims=True) + EPS)
        h = (xv * r * g_ref[...]).astype(BF)
        h_ref[...] = h

        def piece(lo, n):
            return _dot_nt(h, w_ref[lo:lo + n, :]) + b_ref[:, lo:lo + n]

        crv, srv = cr_ref[...], sr_ref[...]
        pq = piece(OFF_RQ, 512)
        pk = piece(OFF_RK, 512)
        for hd in range(RET_HEADS):
            sl = slice(hd * 128, (hd + 1) * 128)
            rq_ref[:, sl] = (_rot_ret(pq[:, sl], crv, srv) * scale).astype(BF)
            rk_ref[:, sl] = _rot_ret(pk[:, sl], crv, srv)
        rv_ref[...] = piece(OFF_RV, 1024).astype(BF)
        rg_ref[...] = piece(OFF_RG, 1024).astype(KEPT)
        cav, sav = ca_ref[...], sa_ref[...]
        pa = piece(OFF_AQ, 1024)
        for blk in range(8):
            sl = slice(blk * 128, (blk + 1) * 128)
            aq_ref[:, sl] = _rot_attn(pa[:, sl], cav, sav).astype(BF)
        pkv = piece(OFF_AK, 256)
        ak_ref[...] = _rot_attn(pkv[:, :128], cav, sav).astype(BF)
        av_ref[...] = pkv[:, 128:].astype(BF)
        ga_ref[...] = piece(OFF_GA, 1024).astype(KEPT)
        gb_ref[...] = piece(OFF_GB, 1024).astype(KEPT)

    outs = [(D_MODEL, BF), (512, BF), (512, F32), (1024, BF), (1024, KEPT),
            (1024, BF), (128, BF), (128, BF), (1024, KEPT), (1024, KEPT)]
    return pl.pallas_call(
        body, name="proj_fwd", grid=(S // tm,),
        in_specs=[_rows(tm, D_MODEL), _fixed((1, D_MODEL)), _VMEM, _VMEM,
                  _rows(tm, 128), _rows(tm, 128), _rows(tm, 128), _rows(tm, 128)],
        out_specs=[_rows(tm, n) for n, _ in outs],
        out_shape=[jax.ShapeDtypeStruct((S, n), dt) for n, dt in outs],
        compiler_params=_params(("parallel",)),
    )(x, g1, w_in, b_in, cr, sr, ca, sa)


def _ret_fwd(rq, rk, rv, dmat, qd, kd, cd):
    S = rq.shape[0]
    N = S // RET_CHUNK
    cps = RET_CHUNKS_PER_STEP
    rows = cps * RET_CHUNK

    def body(q_ref, k_ref, v_ref, dm_ref, qd_ref, kd_ref, cd_ref, y_ref, st_ref, state):
        @pl.when(pl.program_id(0) == 0)
        def _():
            state[...] = jnp.zeros_like(state)

        for hd in range(RET_HEADS):
            qs = slice(hd * 128, (hd + 1) * 128)
            vs = slice(hd * 256, (hd + 1) * 256)
            free = []
            for ch in range(cps):
                rs = slice(ch * RET_CHUNK, (ch + 1) * RET_CHUNK)
                q = q_ref[rs, qs]
                kf = k_ref[rs, qs]
                v = v_ref[rs, vs]
                a = _dot_nt(q, kf.astype(BF)) * dm_ref[hd]
                kdv = (kf * kd_ref[hd]).astype(BF)
                free.append((rs, q, _dot(a.astype(BF), v), _dot_tn(kdv, v)))
            s = state[hd]
            for ch, (rs, q, inner, update) in enumerate(free):
                sb = s.astype(BF)
                st_ref[ch, hd] = sb
                y_ref[rs, vs] = (inner + _dot(q, sb) * qd_ref[hd]).astype(KEPT)
                s = s * cd_ref[hd] + update
            state[hd] = s

    return pl.pallas_call(
        body, name="ret_fwd", grid=(N // cps,),
        in_specs=[_rows(rows, 512), _rows(rows, 512), _rows(rows, 1024), _VMEM, _VMEM, _VMEM, _VMEM],
        out_specs=[_rows(rows, 1024), pl.BlockSpec((cps, RET_HEADS, 128, 256), lambda i: (i, 0, 0, 0))],
        out_shape=[jax.ShapeDtypeStruct((S, 1024), KEPT),
                   jax.ShapeDtypeStruct((N, RET_HEADS, 128, 256), BF)],
        scratch_shapes=[pltpu.VMEM((RET_HEADS, 128, 256), F32)],
        compiler_params=_params(("arbitrary",)),
    )(rq, rk, rv, dmat, qd, kd, cd)


def _kv_halves(prev_ref, cur_ref):
    kk = jnp.concatenate([prev_ref[...], cur_ref[...]], axis=0).astype(F32)
    return kk, pltpu.roll(kk, 64, 1)


PAIRS = 4


def _attn_fold(n):
    t = lax.broadcasted_iota(jnp.int32, (128, PAIRS * 128), 0)
    qi = lax.broadcasted_iota(jnp.int32, (128, PAIRS * 128), 1) & 127
    cur = t <= qi
    cur_f = jnp.where(cur, 1.0, 0.0)
    bias = jnp.where(jnp.logical_or(cur, n > 0), 0.0, -1e30)
    return cur, cur_f, 1.0 - cur_f, bias


def _group_operands(g, kk, kk_sw, vv, vv_sw):
    lo = lax.broadcasted_iota(jnp.int32, (256, 128), 1) < 64
    k_a, k_b = (kk, kk_sw) if g == 0 else (kk_sw, kk)
    v_a, v_b = (vv, vv_sw) if g == 0 else (vv_sw, vv)
    return (jnp.concatenate([jnp.where(lo, k_a, 0.0), jnp.where(lo, 0.0, k_b)], axis=0).astype(BF),
            jnp.concatenate([jnp.where(lo, v_a, 0.0), jnp.where(lo, 0.0, v_b)], axis=0).astype(BF))


def _stack_pairs(ref, g, rows=slice(None)):
    return jnp.concatenate([ref[rows, (4 * g + p) * 128:(4 * g + p + 1) * 128] for p in range(PAIRS)], axis=0)


def _sink_row(sk_ref, g, which):
    return jnp.concatenate([jnp.full((1, 128), sk_ref[0, 8 * g + 2 * p + which], F32) for p in range(PAIRS)], axis=1)


def _fold(full, cur):
    return jnp.where(cur, full[128:], full[:128])


def _attn_probs(raw, fold, sink):
    s = _fold(raw, fold[0]) * (ATTN_HEAD_DIM ** -0.5) + fold[3]
    m = jnp.maximum(jnp.max(s, axis=0, keepdims=True), sink)
    e = jnp.exp(s - m)
    es = jnp.exp(sink - m)
    inv = 1.0 / (jnp.sum(e, axis=0, keepdims=True) + es)
    return e * inv, es * inv


def _unfold(p, fold):
    return jnp.concatenate([p * fold[2], p * fold[1]], axis=0).astype(BF)


def _attn_fwd(aq, ak, av, sinks):
    S = aq.shape[0]
    N = S // SEQ_CHUNK

    def body(sk_ref, q_ref, kp_ref, kc_ref, vp_ref, vc_ref, o_ref):
        kall, kall_sw = _kv_halves(kp_ref, kc_ref)
        vall, vall_sw = _kv_halves(vp_ref, vc_ref)
        blocks = []
        for sb in range(2):
            kv = slice(sb * 128, sb * 128 + 256)
            rows = slice(sb * 128, (sb + 1) * 128)
            ops = [_group_operands(g, kall[kv], kall_sw[kv], vall[kv], vall_sw[kv]) for g in range(2)]
            raw = [_dot_nt(ops[g][0], _stack_pairs(q_ref, g, rows)) for g in range(2)]
            blocks.append((rows, ops, raw, _attn_fold(2 * pl.program_id(0) + sb)))
        for rows, ops, raw, fold in blocks:
            for g in range(2):
                probs = [_unfold(_attn_probs(raw[g][w * 256:(w + 1) * 256], fold, _sink_row(sk_ref, g, w))[0], fold)
                         for w in range(2)]
                o = _dot_tn(jnp.concatenate(probs, axis=0), ops[g][1])
                for p in range(PAIRS):
                    o_ref[rows, (4 * g + p) * 128:(4 * g + p + 1) * 128] = o[p * 128:(p + 1) * 128].astype(BF)

    prev = lambda i: (jnp.maximum(2 * i - 1, 0), 0)
    return pl.pallas_call(
        body, name="attn_fwd", grid=(N // 2,),
        in_specs=[_SMEM, _rows(256, 1024), pl.BlockSpec((128, 128), prev), _rows(256, 128),
                  pl.BlockSpec((128, 128), prev), _rows(256, 128)],
        out_specs=_rows(256, 1024),
        out_shape=jax.ShapeDtypeStruct((S, 1024), BF),
        compiler_params=_params(("parallel",)),
    )(sinks, aq, ak, ak, av, av)


def _group_norm(y):
    mu = jnp.mean(y, axis=-1, keepdims=True)
    yc = y - mu
    rstd = lax.rsqrt(jnp.mean(yc * yc, axis=-1, keepdims=True) + EPS)
    return yc * rstd, rstd


def _merge_fwd(x, ry, rg, ay, ga, gb, g_ret, w_ret, w_attn, w_out):
    S = x.shape[0]
    tm = 512

    def body(x_ref, ry_ref, rg_ref, ay_ref, ga_ref, gb_ref, gr_ref, wr_ref, wa_ref, wo_ref,
             u_ref, mg_ref, x1_ref):
        bb = _dot(ay_ref[...], wa_ref[...])
        for hd in range(RET_HEADS):
            vs = slice(hd * 256, (hd + 1) * 256)
            yn, _ = _group_norm(ry_ref[:, vs].astype(F32))
            rgv = rg_ref[:, vs].astype(F32)
            u_ref[:, vs] = (rgv * _sigmoid(rgv) * (yn * gr_ref[:, vs])).astype(BF)
        gated_b = _sigmoid(gb_ref[...].astype(F32)) * bb
        sa = _sigmoid(ga_ref[...].astype(F32))
        ba = _dot(u_ref[...], wr_ref[...])
        merged = (sa * ba + gated_b).astype(BF)
        mg_ref[...] = merged
        x1_ref[...] = x_ref[...] + _dot(merged, wo_ref[...])

    outs = [BF, BF, F32]
    return pl.pallas_call(
        body, name="merge_fwd", grid=(S // tm,),
        in_specs=[_rows(tm, 1024)] * 6 + [_fixed((1, 1024)), _VMEM, _VMEM, _VMEM],
        out_specs=[_rows(tm, 1024)] * 3,
        out_shape=[jax.ShapeDtypeStruct((S, 1024), dt) for dt in outs],
        compiler_params=_params(("parallel",)),
    )(x, ry, rg, ay, ga, gb, g_ret, w_ret, w_attn, w_out)


def _ffn_fwd(x1, tgt, g2, g3, w_gate, w_up, w_down):
    S = x1.shape[0]
    tm = 256

    def body(x1_ref, t_ref, g2_ref, g3_ref, wg_ref, wu_ref, wd_ref,
             gate_ref, up_ref, h2_ref, act_ref, dx2_ref, loss_ref, dg3_ref):
        @pl.when(pl.program_id(0) == 0)
        def _():
            loss_ref[...] = jnp.zeros_like(loss_ref)
            dg3_ref[...] = jnp.zeros_like(dg3_ref)

        x1v = x1_ref[...]
        r2 = lax.rsqrt(jnp.mean(x1v * x1v, axis=-1, keepdims=True) + EPS)
        h2 = (x1v * r2 * g2_ref[...]).astype(BF)
        h2_ref[...] = h2
        gate = _dot_nt(h2, wg_ref[...])
        up = _dot_nt(h2, wu_ref[...])
        gate_ref[...] = gate
        up_ref[...] = up
        act = (gate * _sigmoid(gate) * up).astype(BF)
        act_ref[...] = act
        x2 = x1v + _dot(act, wd_ref[...])
        r3 = lax.rsqrt(jnp.mean(x2 * x2, axis=-1, keepdims=True) + EPS)
        xh = x2 * r3
        g3v = g3_ref[...]
        err = xh * g3v - t_ref[...]
        loss_ref[...] += 0.5 * jnp.sum(jnp.mean(err * err, axis=-1, keepdims=True), axis=0, keepdims=True)
        dy = err * (1.0 / D_MODEL)
        dg3_ref[...] += jnp.sum(dy * xh, axis=0, keepdims=True)
        dxh = dy * g3v
        dx2_ref[...] = r3 * (dxh - xh * jnp.mean(dxh * xh, axis=-1, keepdims=True))

    return pl.pallas_call(
        body, name="ffn_fwd", grid=(S // tm,),
        in_specs=[_rows(tm, 1024), _rows(tm, 1024), _fixed((1, 1024)), _fixed((1, 1024)), _VMEM, _VMEM, _VMEM],
        out_specs=[_rows(tm, D_FF), _rows(tm, D_FF), _rows(tm, 1024), _rows(tm, D_FF), _rows(tm, 1024),
                   _fixed((1, LANES)), _fixed((1, 1024))],
        out_shape=[jax.ShapeDtypeStruct((S, D_FF), F32), jax.ShapeDtypeStruct((S, D_FF), F32),
                   jax.ShapeDtypeStruct((S, 1024), BF), jax.ShapeDtypeStruct((S, D_FF), BF),
                   jax.ShapeDtypeStruct((S, 1024), F32),
                   jax.ShapeDtypeStruct((1, LANES), F32), jax.ShapeDtypeStruct((1, 1024), F32)],
        compiler_params=_params(("arbitrary",)),
    )(x1, tgt, g2, g3, w_gate, w_up, w_down)


def _ffn_bwd(dx2, gate, up, x1, g2, w_gate, w_up, w_down):
    S = x1.shape[0]
    tm = 256

    def body(dx2_ref, gate_ref, up_ref, x1_ref, g2_ref, wg_ref, wu_ref, wd_ref,
             dgate_ref, dup_ref, dx1_ref, dg2_ref):
        @pl.when(pl.program_id(0) == 0)
        def _():
            dg2_ref[...] = jnp.zeros_like(dg2_ref)

        dx2v = dx2_ref[...]
        dact = _dot_nt(dx2v.astype(BF), wd_ref[...])
        gate = gate_ref[...]
        sg = _sigmoid(gate)
        dgate = (dact * up_ref[...] * (sg * (1.0 + gate * (1.0 - sg)))).astype(BF)
        dup = (dact * (gate * sg)).astype(BF)
        dgate_ref[...] = dgate
        dup_ref[...] = dup
        dh2 = _dot(dgate, wg_ref[...]) + _dot(dup, wu_ref[...])
        x1v = x1_ref[...]
        r2 = lax.rsqrt(jnp.mean(x1v * x1v, axis=-1, keepdims=True) + EPS)
        xh = x1v * r2
        dg2_ref[...] += jnp.sum(dh2 * xh, axis=0, keepdims=True)
        dxh = dh2 * g2_ref[...]
        dx1_ref[...] = dx2v + r2 * (dxh - xh * jnp.mean(dxh * xh, axis=-1, keepdims=True))

    return pl.pallas_call(
        body, name="ffn_bwd", grid=(S // tm,),
        in_specs=[_rows(tm, 1024), _rows(tm, D_FF), _rows(tm, D_FF), _rows(tm, 1024), _fixed((1, 1024)),
                  _VMEM, _VMEM, _VMEM],
        out_specs=[_rows(tm, D_FF), _rows(tm, D_FF), _rows(tm, 1024), _fixed((1, 1024))],
        out_shape=[jax.ShapeDtypeStruct((S, D_FF), BF), jax.ShapeDtypeStruct((S, D_FF), BF),
                   jax.ShapeDtypeStruct((S, 1024), F32), jax.ShapeDtypeStruct((1, 1024), F32)],
        compiler_params=_params(("arbitrary",)),
    )(dx2, gate, up, x1, g2, w_gate, w_up, w_down)


def _merge_bwd(dx1, ga, gb, u, ay, rg, ry, g_ret, w_ret, w_attn, w_out):
    S = dx1.shape[0]
    tm = 512

    def body(dx1_ref, ga_ref, gb_ref, u_ref, ay_ref, rg_ref, ry_ref, gr_ref, wr_ref, wa_ref, wo_ref,
             dgates_ref, drg_ref, dba_ref, dbb_ref, day_ref, dry_ref, dgr_ref):
        @pl.when(pl.program_id(0) == 0)
        def _():
            dgr_ref[...] = jnp.zeros_like(dgr_ref)

        dm = _dot_nt(dx1_ref[...].astype(BF), wo_ref[...])
        ba = _dot(u_ref[...], wr_ref[...])
        bb = _dot(ay_ref[...], wa_ref[...])
        sa = _sigmoid(ga_ref[...].astype(F32))
        sb = _sigmoid(gb_ref[...].astype(F32))
        dga = (dm * ba * (sa * (1.0 - sa))).astype(BF)
        dgb = (dm * bb * (sb * (1.0 - sb))).astype(BF)
        dgates_ref[:, :1024] = dga
        dgates_ref[:, 1024:] = dgb
        dba = (dm * sa).astype(BF)
        dbb = (dm * sb).astype(BF)
        dba_ref[...] = dba
        dbb_ref[...] = dbb
        day_ref[...] = _dot_nt(dbb, wa_ref[...]).astype(BF)
        du = _dot_nt(dba, wr_ref[...])
        for hd in range(RET_HEADS):
            vs = slice(hd * 256, (hd + 1) * 256)
            yn, rstd = _group_norm(ry_ref[:, vs].astype(F32))
            rgv = rg_ref[:, vs].astype(F32)
            sr = _sigmoid(rgv)
            silu = rgv * sr
            duv = du[:, vs]
            grv = gr_ref[:, vs]
            drg = (duv * (yn * grv) * (sr * (1.0 + rgv * (1.0 - sr)))).astype(BF)
            drg_ref[:, vs] = drg
            dyg = duv * silu
            dgr_ref[:, vs] += jnp.sum(dyg * yn, axis=0, keepdims=True)
            dyn = dyg * grv
            dry_ref[:, vs] = (rstd * (dyn - jnp.mean(dyn, axis=-1, keepdims=True)
                                      - yn * jnp.mean(dyn * yn, axis=-1, keepdims=True))).astype(KEPT)

    return pl.pallas_call(
        body, name="merge_bwd", grid=(S // tm,),
        in_specs=[_rows(tm, 1024)] * 7 + [_fixed((1, 1024)), _VMEM, _VMEM, _VMEM],
        out_specs=[_rows(tm, 2048), _rows(tm, 1024), _rows(tm, 1024), _rows(tm, 1024), _rows(tm, 1024),
                   _rows(tm, 1024), _fixed((1, 1024))],
        out_shape=[jax.ShapeDtypeStruct((S, 2048), BF), jax.ShapeDtypeStruct((S, 1024), BF),
                   jax.ShapeDtypeStruct((S, 1024), BF), jax.ShapeDtypeStruct((S, 1024), BF),
                   jax.ShapeDtypeStruct((S, 1024), BF), jax.ShapeDtypeStruct((S, 1024), KEPT),
                   jax.ShapeDtypeStruct((1, 1024), F32)],
        compiler_params=_params(("arbitrary",)),
    )(dx1, ga, gb, u, ay, rg, ry, g_ret, w_ret, w_attn, w_out)


def _attn_bwd(aq, ak, av, day, sinks, ca, sa, after):
    S = aq.shape[0]
    N = S // SEQ_CHUNK
    steps = N // 2

    def body(sk_ref, q_ref, kp_ref, kc_ref, vp_ref, vc_ref, do_ref, cq_ref, sq_ref, cp_ref, sp_ref, after_ref,
             dq_ref, dke_ref, dve_ref, dko_ref, dvo_ref, dsk_ref, carry_k, carry_v):
        i = pl.program_id(0)

        @pl.when(i == 0)
        def _():
            carry_k[...] = jnp.zeros_like(carry_k)
            carry_v[...] = jnp.zeros_like(carry_v)
            dsk_ref[...] = jnp.zeros_like(dsk_ref)

        @pl.when(i < steps)
        def _():
            kall, kall_sw = _kv_halves(kp_ref, kc_ref)
            vall, vall_sw = _kv_halves(vp_ref, vc_ref)
            lo = lax.broadcasted_iota(jnp.int32, (256, 128), 1) < 64
            blocks = []
            for sb in range(2):
                kv = slice(sb * 128, sb * 128 + 256)
                rows = slice(sb * 128, (sb + 1) * 128)
                ops = [_group_operands(g, kall[kv], kall_sw[kv], vall[kv], vall_sw[kv]) for g in range(2)]
                stacked = [(_stack_pairs(q_ref, g, rows), _stack_pairs(do_ref, g, rows)) for g in range(2)]
                raw = [_dot_nt(ops[g][0], stacked[g][0]) for g in range(2)]
                raw_dp = [_dot_nt(ops[g][1], stacked[g][1]) for g in range(2)]
                blocks.append((rows, ops, stacked, raw, raw_dp, _attn_fold(2 * i + sb)))
            grads = []
            for rows, ops, stacked, raw, raw_dp, fold in blocks:
                cur = fold[0]
                cq, sq = cq_ref[rows, :], sq_ref[rows, :]
                dk_groups, dv_groups = [], []
                for g in range(2):
                    qs, dos = stacked[g]
                    ds_parts, p_parts = [], []
                    for which in range(2):
                        half = slice(which * 256, (which + 1) * 256)
                        pr, ps = _attn_probs(raw[g][half], fold, _sink_row(sk_ref, g, which))
                        dp = _fold(raw_dp[g][half], cur)
                        dr = jnp.sum(pr * dp, axis=0, keepdims=True)
                        ds_parts.append(_unfold(pr * (dp - dr) * (ATTN_HEAD_DIM ** -0.5), fold))
                        p_parts.append(_unfold(pr, fold))
                        dsk_ref[2 * g + which:2 * g + which + 1, :] -= ps * dr
                    ds = jnp.concatenate(ds_parts, axis=0)
                    dq = _dot_tn(ds, ops[g][0])
                    dk_both = _dot(ds, qs)
                    dv_both = _dot(jnp.concatenate(p_parts, axis=0), dos)
                    for p in range(PAIRS):
                        sl = slice((4 * g + p) * 128, (4 * g + p + 1) * 128)
                        dq_ref[rows, sl] = _rot_attn_bwd(dq[p * 128:(p + 1) * 128], cq, sq).astype(BF)
                    tk = jnp.where(lo, dk_both[:256], dk_both[256:])
                    tv = jnp.where(lo, dv_both[:256], dv_both[256:])
                    dk_groups.append(tk + pltpu.roll(tk, 64, 1))
                    dv_groups.append(tv + pltpu.roll(tv, 64, 1))
                grads.append((jnp.where(lo, dk_groups[0], dk_groups[1]), jnp.where(lo, dv_groups[0], dv_groups[1])))
            (dk_a, dv_a), (dk_b, dv_b) = grads
            dko_ref[...] = _rot_attn_bwd(carry_k[...] + dk_a[:128], cp_ref[...], sp_ref[...]).astype(BF)
            dvo_ref[...] = (carry_v[...] + dv_a[:128]).astype(BF)
            dke_ref[...] = _rot_attn_bwd(dk_a[128:] + dk_b[:128], cq_ref[0:128, :], sq_ref[0:128, :]).astype(BF)
            dve_ref[...] = (dv_a[128:] + dv_b[:128]).astype(BF)
            carry_k[...] = dk_b[128:]
            carry_v[...] = dv_b[128:]

        @pl.when(i == steps)
        def _():
            dko_ref[...] = _rot_attn_bwd(carry_k[...], cp_ref[...], sp_ref[...]).astype(BF)
            dvo_ref[...] = carry_v[...].astype(BF)

    two = lambda i: (jnp.minimum(i, steps - 1), 0)
    before = lambda i: (jnp.maximum(2 * i - 1, 0), 0)
    odd = lambda i: (jnp.maximum(i - 1, 0), 0)
    blk = lambda r, w, im: pl.BlockSpec((r, w), im)
    half = jax.ShapeDtypeStruct((S // 2, 128), BF)
    dq, dk_even, dv_even, dk_odd, dv_odd, dsk = pl.pallas_call(
        body, name="attn_bwd", grid=(steps + 1,),
        in_specs=[_SMEM, blk(256, 1024, two), blk(128, 128, before), blk(256, 128, two), blk(128, 128, before),
                  blk(256, 128, two), blk(256, 1024, two), blk(256, 128, two), blk(256, 128, two),
                  blk(128, 128, before), blk(128, 128, before), _ANY],
        out_specs=[blk(256, 1024, two), blk(128, 128, two), blk(128, 128, two), blk(128, 128, odd),
                   blk(128, 128, odd), _fixed((4, PAIRS * 128))],
        out_shape=[jax.ShapeDtypeStruct((S, 1024), BF), half, half, half, half,
                   jax.ShapeDtypeStruct((4, PAIRS * 128), F32)],
        scratch_shapes=[pltpu.VMEM((128, 128), F32), pltpu.VMEM((128, 128), F32)],
        compiler_params=_params(("arbitrary",)),
    )(sinks, aq, ak, ak, av, av, day, ca, sa, ca, sa, after)

    def interleave(even, odd_):
        return jnp.stack([even.reshape(steps, 128, 128), odd_.reshape(steps, 128, 128)], axis=1).reshape(S, 128)

    return dq, interleave(dk_even, dk_odd), interleave(dv_even, dv_odd), dsk


def _ret_bwd(rq, rk, rv, dry, states, dmat, qd, kd, cd, cr, sr, after):
    S = rq.shape[0]
    N = S // RET_CHUNK
    scale = RET_QK_DIM ** -0.5
    cps = RET_CHUNKS_PER_STEP
    steps = N // cps

    def body(q_ref, k_ref, v_ref, dy_ref, st_ref, dm_ref, qd_ref, kd_ref, cd_ref, cr_ref, sr_ref, after_ref,
             out_ref, dstate):
        @pl.when(pl.program_id(0) == 0)
        def _():
            dstate[...] = jnp.zeros_like(dstate)

        for hd in range(RET_HEADS):
            qs = slice(hd * 128, (hd + 1) * 128)
            vs = slice(hd * 256, (hd + 1) * 256)
            dmv = dm_ref[hd]
            free = []
            for ch in range(cps):
                rs = slice(ch * RET_CHUNK, (ch + 1) * RET_CHUNK)
                crv, srv = cr_ref[rs, :], sr_ref[rs, :]
                q = q_ref[rs, qs]
                kf = k_ref[rs, qs]
                k = kf.astype(BF)
                v = v_ref[rs, vs]
                do = dy_ref[rs, vs].astype(F32)
                dob = do.astype(BF)
                doq = (do * qd_ref[hd]).astype(BF)
                a = (_dot_nt(q, k) * dmv).astype(BF)
                da = (_dot_nt(dob, v) * dmv).astype(BF)
                dq = _dot(da, k) + _dot_nt(doq, st_ref[ch, hd])
                out_ref[rs, qs] = _rot_ret_bwd(dq * scale, crv, srv).astype(BF)
                free.append((rs, crv, srv, v, (kf * kd_ref[hd]).astype(BF),
                             _dot_tn(a, dob), _dot_tn(da, q), _dot_tn(q, doq)))
            ds = dstate[hd]
            for rs, crv, srv, v, kdv, dv_free, dk_free, ds_update in reversed(free):
                dsb = ds.astype(BF)
                dv = dv_free + _dot(kdv, dsb)
                dk = dk_free + _dot_nt(v, dsb) * kd_ref[hd]
                ds = ds * cd_ref[hd] + ds_update
                out_ref[rs, 512 + hd * 128:512 + (hd + 1) * 128] = _rot_ret_bwd(dk, crv, srv).astype(BF)
                out_ref[rs, 1024 + hd * 256:1024 + (hd + 1) * 256] = dv.astype(BF)
            dstate[hd] = ds

    rev = lambda w: pl.BlockSpec((cps * RET_CHUNK, w), lambda i: (steps - 1 - i, 0))
    return pl.pallas_call(
        body, name="ret_bwd", grid=(steps,),
        in_specs=[rev(512), rev(512), rev(1024), rev(1024),
                  pl.BlockSpec((cps, RET_HEADS, 128, 256), lambda i: (steps - 1 - i, 0, 0, 0)),
                  _VMEM, _VMEM, _VMEM, _VMEM, rev(128), rev(128), _ANY],
        out_specs=rev(2048),
        out_shape=jax.ShapeDtypeStruct((S, 2048), BF),
        scratch_shapes=[pltpu.VMEM((RET_HEADS, 128, 256), F32)],
        compiler_params=_params(("arbitrary",)),
    )(rq, rk, rv, dry, states, dmat, qd, kd, cd, cr, sr, after)


def _proj_bwd(d_pieces, x, dx1, g1, w_in_t):
    S = x.shape[0]
    tm = 512
    widths = [d.shape[1] for d in d_pieces]
    assert sum(widths) == D_IN
    nd = len(d_pieces)

    def body(*refs):
        d_refs = refs[:nd]
        x_ref, dx1_ref, g_ref, w_ref, gx_ref, dg_ref, db_ref = refs[nd:]

        @pl.when(pl.program_id(0) == 0)
        def _():
            dg_ref[...] = jnp.zeros_like(dg_ref)
            db_ref[...] = jnp.zeros_like(db_ref)

        dh = jnp.zeros((tm, D_MODEL), F32)
        lo = 0
        for ref, n in zip(d_refs, widths):
            v = ref[...]
            dh = dh + _dot(v, w_ref[lo:lo + n, :])
            db_ref[:, lo:lo + n] += jnp.sum(v.astype(F32), axis=0, keepdims=True)
            lo += n
        xv = x_ref[...]
        r = lax.rsqrt(jnp.mean(xv * xv, axis=-1, keepdims=True) + EPS)
        xh = xv * r
        dg_ref[...] += jnp.sum(dh * xh, axis=0, keepdims=True)
        dxh = dh * g_ref[...]
        gx_ref[...] = dx1_ref[...] + r * (dxh - xh * jnp.mean(dxh * xh, axis=-1, keepdims=True))

    return pl.pallas_call(
        body, name="proj_bwd", grid=(S // tm,),
        in_specs=[_rows(tm, n) for n in widths] + [_rows(tm, 1024), _rows(tm, 1024), _fixed((1, 1024)), _VMEM],
        out_specs=[_rows(tm, 1024), _fixed((1, 1024)), _fixed((1, D_IN))],
        out_shape=[jax.ShapeDtypeStruct((S, 1024), F32), jax.ShapeDtypeStruct((1, 1024), F32),
                   jax.ShapeDtypeStruct((1, D_IN), F32)],
        compiler_params=_params(("arbitrary",)),
    )(*d_pieces, x, dx1, g1, w_in_t)


SHARD_ROWS = (1600, 256, 256, 256, 704, 704, 704)


class Group(NamedTuple):
    name: str
    tensors: tuple
    offs: tuple
    block: int
    tile: int

    @property
    def half(self):
        return self.block // 2

    def off(self, t):
        return self.offs[self.tensors.index(t)]


LATE = Group("late", (0,), (0,), 1600, 800)
EARLY = Group("early", (1, 2, 3, 4, 5, 6), (0, 256, 512, 768, 1472, 2176), 2880, 480)
GROUP_OF = {t: g for g in (LATE, EARLY) for t in g.tensors}


def _wgrad(a_list, b, t, packed, name, row0=0):
    S = b.shape[0]
    widths = [a.shape[1] for a in a_list]
    M = sum(widths)
    ts = 1024
    steps = S // ts
    na = len(a_list)
    grp = GROUP_OF[t]
    r, off = SHARD_ROWS[t], grp.off(t)
    segments = []
    for chip in range(N_CHIPS):
        lo, hi = max(chip * r, row0), min((chip + 1) * r, row0 + M)
        if lo < hi:
            segments.append((lo - row0, hi - lo, chip * grp.block + off + lo - chip * r))
    assert sum(n for _, n, _ in segments) == M
    stage_rows = max(n for _, n, _ in segments)
    n_in = na + 1 + (2 if packed is not None else 0)

    def body(*refs):
        a_refs, b_ref = refs[:na], refs[na]
        g32_ref, gb_ref, acc, stage, sems = refs[n_in:]
        i = pl.program_id(0)

        @pl.when(i == 0)
        def _():
            acc[...] = jnp.zeros_like(acc)

        bv = b_ref[...].astype(BF)
        lo = 0
        for a_ref, n in zip(a_refs, widths):
            acc[lo:lo + n, :] += _dot_tn(a_ref[...].astype(BF), bv)
            lo += n

        @pl.when(i == steps - 1)
        def _():
            for a0, n, p0 in segments:
                c32 = pltpu.make_async_copy(acc.at[a0:a0 + n, :], g32_ref.at[pl.ds(p0, n), :], sems.at[0])
                c32.start()
                stage[0:n, :] = acc[a0:a0 + n, :].astype(BF)
                c16 = pltpu.make_async_copy(stage.at[0:n, :], gb_ref.at[pl.ds(p0, n), :], sems.at[1])
                c16.start()
                c32.wait()
                c16.wait()

    shapes = [jax.ShapeDtypeStruct((N_CHIPS * grp.block, 1024), F32),
              jax.ShapeDtypeStruct((N_CHIPS * grp.block, 1024), BF)]
    operands = list(a_list) + [b]
    in_specs = [_rows(ts, n) for n in widths] + [_rows(ts, 1024)]
    aliases = {}
    if packed is not None:
        operands += list(packed)
        in_specs += [_ANY, _ANY]
        aliases = {na + 1: 0, na + 2: 1}
    return pl.pallas_call(
        body, name=name, grid=(steps,),
        in_specs=in_specs, out_specs=[_ANY, _ANY], out_shape=shapes,
        scratch_shapes=[pltpu.VMEM((M, 1024), F32), pltpu.VMEM((stage_rows, 1024), BF),
                        pltpu.SemaphoreType.DMA((2,))],
        input_output_aliases=aliases,
        compiler_params=_params(("arbitrary",)),
    )(*operands)


def _stage_weight(place, w, name, after=()):
    r = w.shape[0]
    tr = 800 if r == 1600 else r
    nb = r // tr

    def body(p_ref, w_ref, *rest):
        rest[-1][...] = w_ref[...].astype(BF)

    return pl.pallas_call(
        body, name=name,
        grid_spec=pltpu.PrefetchScalarGridSpec(
            num_scalar_prefetch=1, grid=(nb,),
            in_specs=[pl.BlockSpec((tr, 1024), lambda i, p: (i, 0))] + [_ANY] * len(after),
            out_specs=pl.BlockSpec((tr, 1024), lambda i, p: (p[0] * nb + i, 0))),
        out_shape=jax.ShapeDtypeStruct((N_CHIPS * r, 1024), BF),
        compiler_params=_params(("arbitrary",)),
    )(place, w, *after)


def _kept_block(grp, nb):
    return lambda i, p: ((2 * (i // nb) + p[1]) * nb + i % nb, 0)


def _pair_sum(place, g32, got, grp):
    nb = grp.half // grp.tile

    def body(p_ref, g_ref, r_ref, s16_ref):
        s16_ref[...] = (g_ref[...] + r_ref[...].astype(F32)).astype(BF)

    flat = pl.BlockSpec((grp.tile, 1024), lambda i, p: (i, 0))
    return pl.pallas_call(
        body, name="pair_sum_" + grp.name,
        grid_spec=pltpu.PrefetchScalarGridSpec(
            num_scalar_prefetch=1, grid=(N_CHIPS * nb,),
            in_specs=[pl.BlockSpec((grp.tile, 1024), _kept_block(grp, nb)), flat], out_specs=flat),
        out_shape=jax.ShapeDtypeStruct((N_CHIPS * grp.half, 1024), BF),
        compiler_params=_params(("arbitrary",)),
    )(place, g32, got)


def _chip_sum(place, g32, got, others, grp):
    nb = grp.half // grp.tile

    def body(p_ref, g_ref, r_ref, r0_ref, r1_ref, r2_ref, o_ref):
        own = g_ref[...] + r_ref[...].astype(F32)
        o_ref[...] = ((own + r0_ref[...].astype(F32)) + r1_ref[...].astype(F32)) + r2_ref[...].astype(F32)

    blk = lambda f: pl.BlockSpec((grp.tile, 1024), f)
    return pl.pallas_call(
        body, name="chip_sum_" + grp.name,
        grid_spec=pltpu.PrefetchScalarGridSpec(
            num_scalar_prefetch=1, grid=(nb,),
            in_specs=[blk(lambda i, p: ((2 * p[0] + p[1]) * nb + i, 0)), blk(lambda i, p: (p[0] * nb + i, 0)),
                      blk(lambda i, p: (i, 0)), blk(lambda i, p: (nb + i, 0)), blk(lambda i, p: (2 * nb + i, 0))],
            out_specs=blk(lambda i, p: (p[1] * nb + i, 0))),
        out_shape=jax.ShapeDtypeStruct((grp.block, 1024), F32),
        compiler_params=_params(("arbitrary",)),
    )(place, g32, got, others, others, others)


def _adamw(w, g, m, v, name, g_row0=0, after=None):
    r = w.shape[0]
    tr = 320 if r % 320 == 0 and g_row0 % 320 == 0 else (64 if r % 64 == 0 else r)
    assert g_row0 % tr == 0
    g0 = g_row0 // tr
    extra = [] if after is None else [after]

    def body(w_ref, g_ref, m_ref, v_ref, *rest):
        go_ref, d_ref, mo_ref, vo_ref = rest[len(extra):]
        gv = g_ref[...]
        mn = ADAM_B1 * m_ref[...] + (1.0 - ADAM_B1) * gv
        vn = ADAM_B2 * v_ref[...] + (1.0 - ADAM_B2) * (gv * gv)
        m_hat = mn / (1.0 - ADAM_B1 ** ADAM_STEP)
        v_hat = vn / (1.0 - ADAM_B2 ** ADAM_STEP)
        go_ref[...] = gv
        d_ref[...] = -ADAM_LR * (m_hat / (jnp.sqrt(v_hat) + ADAM_EPS) + ADAM_WD * w_ref[...])
        mo_ref[...] = mn
        vo_ref[...] = vn

    return pl.pallas_call(
        body, name=name, grid=(r // tr,),
        in_specs=[_rows(tr, 1024), pl.BlockSpec((tr, 1024), lambda i: (g0 + i, 0)), _rows(tr, 1024), _rows(tr, 1024)]
        + [_ANY] * len(extra),
        out_specs=[_rows(tr, 1024)] * 4,
        out_shape=[jax.ShapeDtypeStruct((r, 1024), F32)] * 4,
        compiler_params=_params(("parallel",)),
    )(w, g, m, v, *extra)


def _place():
    x, y, c = lax.axis_index("x"), lax.axis_index("y"), lax.axis_index("c")
    chips = [(1 - x, y), (x, 1 - y), (1 - x, 1 - y)]
    return x, y, c, chips


_HBM =pl.BlockSpec(memory_space=pltpu.HBM)
_SEM = pl.BlockSpec(memory_space=pltpu.SEMAPHORE)
_DATAFLOW = pltpu.SideEffectType.DATAFLOW_SIDE_EFFECTING


def _exchange(name, arrays, n, plan):
    na = len(arrays)

    def body(*refs):
        pairs = plan(refs[na:2 * na], refs[2 * na], refs[2 * na + 1])
        for send, _ in pairs:
            send.start()
        for send, recv in pairs:
            send.wait_send()
            recv.wait_recv()

    return pl.pallas_call(
        body, name=name, in_specs=[_ANY] * na, out_specs=[_ANY] * na,
        out_shape=[jax.ShapeDtypeStruct(a.shape, a.dtype) for a in arrays],
        scratch_shapes=[pltpu.SemaphoreType.DMA((n,)), pltpu.SemaphoreType.DMA((n,))],
        input_output_aliases={i: i for i in range(na)},
    )(*arrays)


def _exchange_start(name, arrays, n, plan, after=()):
    na = len(arrays)

    def body(*refs):
        for send, _ in plan(refs[:na], refs[na + len(after)], refs[na + len(after) + 1]):
            send.start()
        refs[-1][...] = jnp.zeros_like(refs[-1])

    out = pl.pallas_call(
        body, name=name,
        out_shape=(pltpu.SemaphoreType.DMA((n,)), pltpu.SemaphoreType.DMA((n,)),
                   *[pltpu.HBM(a.shape, a.dtype) for a in arrays], jax.ShapeDtypeStruct((8, LANES), F32)),
        in_specs=[_HBM] * na + [_ANY] * len(after), out_specs=(_SEM, _SEM, *[_HBM] * na, _VMEM),
        input_output_aliases={i: 2 + i for i in range(na)},
        compiler_params=pltpu.CompilerParams(has_side_effects=_DATAFLOW),
    )(*[pltpu.with_memory_space_constraint(a, pltpu.HBM) for a in arrays], *after)
    return (out[0], out[1]), list(out[2:2 + na]), out[-1]


def _exchange_wait(name, sems, in_flight, plan, after=()):
    na = len(in_flight)

    def body(*refs):
        for send, recv in plan(refs[:na], refs[na], refs[na + 1]):
            send.wait_send()
            recv.wait_recv()

    out = pl.pallas_call(
        body, name=name,
        out_shape=tuple(pltpu.HBM(a.shape, a.dtype) for a in in_flight),
        in_specs=[_HBM] * na + [_SEM, _SEM] + [_ANY] * len(after), out_specs=tuple([_HBM] * na),
        input_output_aliases={i: i for i in range(na)},
        compiler_params=pltpu.CompilerParams(has_side_effects=_DATAFLOW),
    )(*in_flight, *sems, *after)
    return list(out)


def _remote(src, dst, send_sems, recv_sems, k, to):
    return pltpu.make_async_remote_copy(src_ref=src, dst_ref=dst, send_sem=send_sems.at[k], recv_sem=recv_sems.at[k],
                                        device_id=to, device_id_type=MESH)


def _half_rows(ref, tid, pj, half):
    h = SHARD_ROWS[tid] // 2
    return ref.at[pl.ds((2 * pj + half) * h, h), :]


def _plan_gather_halves(tids):
    def plan(refs, ss, rs):
        x, y, c, chips = _place()
        pairs = []
        for t, tid in enumerate(tids):
            mine = _half_rows(refs[t], tid, 2 * x + y, c)
            for k, (cx, cy) in enumerate(chips):
                theirs = _half_rows(refs[t], tid, 2 * cx + cy, c)
                pairs.append((_remote(mine, mine, ss, rs, 3 * t + k, (cx, cy, c)),
                              _remote(theirs, theirs, ss, rs, 3 * t + k, (x, y, c))))
        return pairs
    return plan


def _plan_forward_halves(tids):
    def plan(refs, ss, rs):
        x, y, c, chips = _place()
        pairs = []
        for t, tid in enumerate(tids):
            for k, (cx, cy) in enumerate(chips):
                got = _half_rows(refs[t], tid, 2 * cx + cy, c)
                missing = _half_rows(refs[t], tid, 2 * cx + cy, 1 - c)
                pairs.append((_remote(got, got, ss, rs, 3 * t + k, (x, y, 1 - c)),
                              _remote(missing, missing, ss, rs, 3 * t + k, (x, y, 1 - c))))
        return pairs
    return plan


def _plan_pair_exchange(grp):
    def plan(refs, ss, rs):
        x, y, c, _ = _place()
        cps = [_remote(refs[0].at[pl.ds(pj * grp.block + (1 - c) * grp.half, grp.half), :],
                       refs[1].at[pl.ds(pj * grp.half, grp.half), :], ss, rs, pj, (x, y, 1 - c))
               for pj in range(N_CHIPS)]
        return [(cp, cp) for cp in cps]
    return plan


def _plan_to_chips(grp):
    def plan(refs, ss, rs):
        x, y, c, chips = _place()
        cps = [_remote(refs[0].at[pl.ds((2 * cx + cy) * grp.half, grp.half), :],
                       refs[1].at[pl.ds(k * grp.half, grp.half), :], ss, rs, k, (cx, cy, c))
               for k, (cx, cy) in enumerate(chips)]
        return [(cp, cp) for cp in cps]
    return plan


def _plan_share_halves(grp):
    def plan(refs, ss, rs):
        x, y, c, _ = _place()
        mine = refs[0].at[pl.ds(c * grp.half, grp.half), :]
        other = refs[0].at[pl.ds((1 - c) * grp.half, grp.half), :]
        return [(_remote(mine, mine, ss, rs, 0, (x, y, 1 - c)), _remote(other, other, ss, rs, 0, (x, y, 1 - c)))]
    return plan


def _empty(rows, dtype):
    return lax.empty((rows, 1024), dtype)


def _all_reduce_small(v, after):
    m_per, n = v.shape
    n_dev = 8

    def body(x_ref, after_ref, out_ref, all_ref, send_sems, recv_sems, local_sem):
        x, y, c, chips = _place()
        me, sibling = (x, y, c), (x, y, 1 - c)

        def rows(px, py, pc):
            return all_ref.at[pl.ds((4 * px + 2 * py + pc) * m_per, m_per), :]

        def copy(k, block, to, src=None):
            return pltpu.make_async_remote_copy(
                src_ref=rows(*block) if src is None else src, dst_ref=rows(*block),
                send_sem=send_sems.at[k], recv_sem=recv_sems.at[k], device_id=to, device_id_type=MESH)

        mine = pltpu.make_async_copy(x_ref, rows(*me), local_sem)
        mine.start()
        first = [copy(0, me, sibling, src=x_ref)]
        first += [copy(1 + k, me, (*chip, c), src=x_ref) for k, chip in enumerate(chips)]
        for cp in first:
            cp.start()
        passed = [copy(4 + k, (*chip, c), sibling) for k, chip in enumerate(chips)]
        for k, chip in enumerate(chips):
            copy(1 + k, (*chip, c), me).wait_recv()
            passed[k].start()
        copy(0, sibling, me).wait_recv()
        for k, chip in enumerate(chips):
            copy(4 + k, (*chip, 1 - c), me).wait_recv()
        for cp in first + passed:
            cp.wait_send()
        mine.wait()
        acc = all_ref[pl.ds(0, m_per), :]
        for d in range(1, n_dev):
            acc = acc + all_ref[pl.ds(d * m_per, m_per), :]
        out_ref[...] = acc

    return pl.pallas_call(
        body, name="all_reduce_small",
        in_specs=[_VMEM, _ANY], out_specs=_VMEM,
        out_shape=jax.ShapeDtypeStruct((m_per, n), F32),
        scratch_shapes=[pltpu.VMEM((n_dev * m_per, n), F32),
                        pltpu.SemaphoreType.DMA((7,)), pltpu.SemaphoreType.DMA((7,)), pltpu.SemaphoreType.DMA],
        compiler_params=pltpu.CompilerParams(has_side_effects=True),
    )(v, after)


def _tables(S, token):
    pos = jnp.arange(S, dtype=jnp.int32).astype(F32) + token[0, 0]

    def cs(half):
        inv_freq = ROPE_THETA ** (-jnp.arange(half, dtype=F32) / half)
        ang = pos[:, None] * inv_freq[None, :]
        return jnp.cos(ang), jnp.sin(ang)

    c64, s64 = cs(64)
    c32, s32 = cs(32)
    cr = jnp.concatenate([c64, c64], axis=1)
    sr = jnp.concatenate([-s64, s64], axis=1)
    ca = jnp.concatenate([c32, c32, c32, c32], axis=1)
    sa = jnp.concatenate([-s32, s32, -s32, s32], axis=1)
    return cr, sr, ca, sa


def _decays():
    H, C = RET_HEADS, RET_CHUNK
    log_gamma = jnp.log1p(-jnp.exp2(-5.0 - jnp.arange(H, dtype=F32)))
    idx = jnp.arange(C, dtype=F32)
    rel = idx[:, None] - idx[None, :]
    dmat = jnp.where(rel[None] >= 0, jnp.exp(log_gamma[:, None, None] * jnp.maximum(rel, 0.0)[None]), 0.0)
    qd = jnp.exp(log_gamma[:, None] * (idx + 1.0))[:, :, None]
    kd = jnp.exp(log_gamma[:, None] * (C - 1.0 - idx))[:, :, None]
    cd = jnp.exp(log_gamma * C)[:, None, None]
    return dmat, qd, kd, cd


def _local_step(place, x, tgt, ln1_g, b_in, ret_norm_g, attn_sinks, ln2_g, lnf_g, w_in, other_weights, tables, decays):
    cr, sr, ca, sa = tables
    dmat, qd, kd, cd = decays
    forward_weights, landed_weights = other_weights

    h, rq, rk, rv, rg, aq, ak, av, ga, gb = _proj_fwd(x, ln1_g, w_in, b_in, cr, sr, ca, sa)
    ry, states = _ret_fwd(rq, rk, rv, dmat, qd, kd, cd)
    token = forward_weights(ry)
    ay = _attn_fwd(aq, ak, av, attn_sinks + token[0, 0])
    w_ret, w_attn, w_out, w_gate, w_up, w_down = landed_weights(ay)
    u, merged, x1 = _merge_fwd(x, ry, rg, ay, ga, gb, ret_norm_g, w_ret, w_attn, w_out)
    gate, up, h2, act, dx2, loss, dg3 = _ffn_fwd(x1, tgt, ln2_g, lnf_g, w_gate, w_up, w_down)

    dgate, dup, dx1, dg2 = _ffn_bwd(dx2, gate, up, x1, ln2_g, w_gate, w_up, w_down)
    early = _wgrad([act], dx2, 6, None, "wgrad_down")
    early = _wgrad([dgate], h2, 4, early, "wgrad_gate")
    early = _wgrad([dup], h2, 5, early, "wgrad_up")
    dgates, drg, dba, dbb, day, dry, dgr = _merge_bwd(dx1, ga, gb, u, ay, rg, ry, ret_norm_g,
                                                                w_ret, w_attn, w_out)
    early = _wgrad([merged], dx1, 3, early, "wgrad_out")
    early = _wgrad([u], dba, 1, early, "wgrad_ret")
    e32, e16 = _wgrad([ay], dbb, 2, early, "wgrad_attn")

    grp = EARLY
    sems, flight, token = _exchange_start("pair_start_early", [e16, _empty(N_CHIPS * grp.half, BF)], N_CHIPS,
                                          _plan_pair_exchange(grp))
    daq, dak, dav, dsk = _attn_bwd(aq, ak, av, day, attn_sinks, ca, sa, token)
    _, got = _exchange_wait("pair_wait_early", sems, flight, _plan_pair_exchange(grp), after=(dsk,))
    s16 = _pair_sum(place, e32, got, grp)
    sems, flight, token = _exchange_start("chips_start_early", [s16, _empty(3 * grp.half, BF)], 3, _plan_to_chips(grp))
    d_ret = _ret_bwd(rq, rk, rv, dry, states, dmat, qd, kd, cd, cr, sr, token)
    d_proj = [d_ret, drg, daq, dak, dav, dgates]
    grad_x, dg1, db = _proj_bwd(d_proj, x, dx1, ln1_g, w_in)
    late = _wgrad(d_proj[:2], h, 0, None, "wgrad_in_a")
    late = _wgrad(d_proj[2:], h, 0, late, "wgrad_in_b", row0=d_ret.shape[1] + drg.shape[1])

    def finish_early(after):
        _, others = _exchange_wait("chips_wait_early", sems, flight, _plan_to_chips(grp), after=after)
        return _chip_sum(place, e32, got, others, grp)

    d_sinks = dsk.reshape(2, 2, PAIRS, 128).sum(-1).transpose(0, 2, 1).reshape(1, ATTN_Q_HEADS)
    small = dict(loss=loss, ln1_g=dg1, b_in=db, ret_norm_g=dgr, attn_sinks=d_sinks,
                 ln2_g=dg2, lnf_g=dg3)
    return grad_x, finish_early, late, small


SMALL_ROWS = 16


def _pack_small(ln1, ret, ln2, lnf, b_in, sinks, loss, name):
    def body(a_ref, b_ref, c_ref, d_ref, bi_ref, sk_ref, ls_ref, o_ref):
        o_ref[...] = jnp.zeros_like(o_ref)
        for row, ref in enumerate((a_ref, b_ref, c_ref, d_ref)):
            o_ref[row:row + 1, :] = ref[...]
        for k in range(6):
            o_ref[4 + k:5 + k, :] = bi_ref[:, k * 1024:(k + 1) * 1024]
        o_ref[10:11, 0:D_IN - 6144] = bi_ref[:, 6144:D_IN]
        o_ref[11:12, 0:ATTN_Q_HEADS] = sk_ref[...]
        o_ref[12:13, 0:LANES] = ls_ref[...]

    return pl.pallas_call(
        body, name=name, in_specs=[_VMEM] * 7, out_specs=_VMEM,
        out_shape=jax.ShapeDtypeStruct((SMALL_ROWS, 1024), F32),
    )(ln1.reshape(1, 1024), ret.reshape(1, 1024), ln2.reshape(1, 1024), lnf.reshape(1, 1024),
      b_in.reshape(1, D_IN), sinks.reshape(1, ATTN_Q_HEADS), loss.reshape(1, LANES))


def _unpack_small(p):
    return dict(ln1_g=p[0:1], ret_norm_g=p[1:2], ln2_g=p[2:3], lnf_g=p[3], b_in=p[4:11].reshape(1, -1)[:, :D_IN],
                attn_sinks=p[11:12, :ATTN_Q_HEADS], loss=p[12, 0])


def kernel(x, ln1_g, w_in, b_in, ret_norm_g, w_ret_out, attn_sinks, w_attn_out, w_out, ln2_g, w_ffn_gate, w_ffn_up, w_ffn_down, lnf_g, loss_target, m_ln1_g, m_w_in, m_b_in, m_ret_norm_g, m_w_ret_out, m_attn_sinks, m_w_attn_out, m_w_out, m_ln2_g, m_w_ffn_gate, m_w_ffn_up, m_w_ffn_down, m_lnf_g, v_ln1_g, v_w_in, v_b_in, v_ret_norm_g, v_w_ret_out, v_attn_sinks, v_w_attn_out, v_w_out, v_ln2_g, v_w_ffn_gate, v_w_ffn_up, v_w_ffn_down, v_lnf_g):
    xi = lax.axis_index("x")
    yi = lax.axis_index("y")
    ci = lax.axis_index("c")
    j = 2 * xi + yi

    place = jnp.stack([j, ci]).astype(jnp.int32)

    names = ["w_in", "w_ret_out", "w_attn_out", "w_out", "w_ffn_gate", "w_ffn_up", "w_ffn_down"]
    transposed = [True, False, False, False, True, True, False]
    view = lambda a, t: a[0].T if t else a[0]
    big_w = [view(a, t) for a, t in zip([w_in, w_ret_out, w_attn_out, w_out, w_ffn_gate, w_ffn_up, w_ffn_down], transposed)]
    big_m = [view(a, t) for a, t in zip([m_w_in, m_w_ret_out, m_w_attn_out, m_w_out, m_w_ffn_gate, m_w_ffn_up, m_w_ffn_down], transposed)]
    big_v = [view(a, t) for a, t in zip([v_w_in, v_w_ret_out, v_w_attn_out, v_w_out, v_w_ffn_gate, v_w_ffn_up, v_w_ffn_down], transposed)]

    staged_in = _stage_weight(place, big_w[0], "stage_" + names[0])
    sems, in_flight, token = _exchange_start("gather_in_start", [staged_in], 3, _plan_gather_halves([0]))
    staged = [_stage_weight(place, w, "stage_" + n, after=(token,)) for n, w in zip(names[1:], big_w[1:])]
    tables = _tables(x.shape[1], token)
    decays = _decays()
    w_in_full = _exchange_wait("gather_in_wait", sems, in_flight, _plan_gather_halves([0]),
                               after=(*staged, *tables, *decays))[0]
    w_in_full = _exchange("gather_in_forward", [w_in_full], 3, _plan_forward_halves([0]))[0]
    rest = list(range(1, 7))
    crossing = _exchange_start("gather_start", staged, 3 * len(rest), _plan_gather_halves(rest), after=(w_in_full,))
    token = crossing[2]
    forwarding = []

    def forward_weights(after):
        arrived = _exchange_wait("gather_wait", crossing[0], crossing[1], _plan_gather_halves(rest), after=(after,))
        forwarding.extend(_exchange_start("forward_start", arrived, 3 * len(rest), _plan_forward_halves(rest)))
        return forwarding[2]

    def landed_weights(after):
        return _exchange_wait("forward_wait", forwarding[0], forwarding[1], _plan_forward_halves(rest), after=(after,))

    other_weights = (forward_weights, landed_weights)

    grad_x, finish_early, (l32, l16), small = _local_step(
        place, x[0], loss_target[0], ln1_g, b_in + token[0, 0], ret_norm_g, attn_sinks, ln2_g, lnf_g.reshape(1, -1),
        w_in_full, other_weights, tables, decays)
    pg = _all_reduce_small(_pack_small(
        small["ln1_g"], small["ret_norm_g"], small["ln2_g"], small["lnf_g"], small["b_in"],
        small["attn_sinks"], small["loss"], "pack_small_grads"), l32)
    early_shard = _exchange("share_early", [finish_early((l32,))], 1, _plan_share_halves(EARLY))[0]
    _, got = _exchange("pair_exchange_late", [l16, _empty(N_CHIPS * LATE.half, BF)], N_CHIPS, _plan_pair_exchange(LATE))
    s16 = _pair_sum(place, l32, got, LATE)
    sems, flight, token = _exchange_start("chips_start_late", [s16, _empty(3 * LATE.half, BF)], 3, _plan_to_chips(LATE),
                                          after=(pg, early_shard))

    upd = {}

    def update(t, shard, after):
        res = _adamw(big_w[t], shard, big_m[t], big_v[t], "adamw_" + names[t], g_row0=GROUP_OF[t].off(t), after=after)
        upd[names[t]] = tuple((r.T if transposed[t] else r)[None] for r in res)
        return res[0]

    done = [update(t, early_shard, token) for t in EARLY.tensors]

    sm = _unpack_small(pg)
    z = jnp.zeros((1, LANES), F32)
    pw = _pack_small(ln1_g, ret_norm_g, ln2_g, lnf_g, b_in, attn_sinks, z, "pack_small_w")
    pm = _pack_small(m_ln1_g, m_ret_norm_g, m_ln2_g, m_lnf_g, m_b_in, m_attn_sinks, z, "pack_small_m")
    pv = _pack_small(v_ln1_g, v_ret_norm_g, v_ln2_g, v_lnf_g, v_b_in, v_attn_sinks, z, "pack_small_v")
    small_res = _adamw(pw, pg, pm, pv, "adamw_small", after=token)
    sd, smn, svn = (_unpack_small(t) for t in small_res[1:])
    for n in ["ln1_g", "b_in", "ret_norm_g", "attn_sinks", "ln2_g", "lnf_g"]:
        upd[n] = (sm[n], sd[n], smn[n], svn[n])

    _, others = _exchange_wait("chips_wait_late", sems, flight, _plan_to_chips(LATE), after=(*done, small_res[1]))
    late_shard = _exchange("share_late", [_chip_sum(place, l32, got, others, LATE)], 1, _plan_share_halves(LATE))[0]
    update(0, late_shard, None)

    order_w = ["ln1_g", "w_in", "b_in", "ret_norm_g", "w_ret_out", "attn_sinks", "w_attn_out", "w_out",
               "ln2_g", "w_ffn_gate", "w_ffn_up", "w_ffn_down", "lnf_g"]
    outs = [sm["loss"], grad_x[None]]
    for k in range(4):
        outs += [upd[n][k] for n in order_w]
    return tuple(outs)
```

```python
from typing import NamedTuple

import jax
import jax.numpy as jnp
from jax import lax
from jax.experimental import pallas as pl
from jax.experimental.pallas import tpu as pltpu

D_MODEL = 1024
SEQ_CHUNK = 128
RET_CHUNK = 256
RET_CHUNKS_PER_STEP = 4
RET_HEADS = 4
RET_QK_DIM = 128
RET_V_DIM = 256
ATTN_Q_HEADS = 16
ATTN_HEAD_DIM = 64
D_FF = 2816
D_IN = 6400
ROPE_THETA = 10000.0
EPS = 1e-6
N_CHIPS = 4

ADAM_LR = 0.001
ADAM_B1 = 0.9
ADAM_B2 = 0.999
ADAM_EPS = 1e-08
ADAM_WD = 0.01
ADAM_STEP = 10

OFF_RQ, OFF_RK, OFF_RV, OFF_RG = 0, 512, 1024, 2048
OFF_AQ, OFF_AK, OFF_AV, OFF_GA, OFF_GB = 3072, 4096, 4224, 4352, 5376

V7X_VMEM_LIMIT = 56 * 1024 * 1024
LANES = 128

BF = jnp.bfloat16
F32 = jnp.float32
KEPT = jnp.bfloat16
MESH = pl.DeviceIdType.MESH

_VMEM = pl.BlockSpec(memory_space=pltpu.VMEM)
_SMEM = pl.BlockSpec(memory_space=pltpu.SMEM)
_ANY = pl.BlockSpec(memory_space=pl.ANY)


def _dot(a, b):
    return jnp.dot(a, b, preferred_element_type=F32)


def _dot_nt(a, b):
    return lax.dot_general(a, b, (((1,), (1,)), ((), ())), preferred_element_type=F32)


def _dot_tn(a, b):
    return lax.dot_general(a, b, (((0,), (0,)), ((), ())), preferred_element_type=F32)


def _sigmoid(x):
    return 0.5 * jnp.tanh(0.5 * x) + 0.5


def _rows(tm, n):
    return pl.BlockSpec((tm, n), lambda i: (i, 0))


def _fixed(shape):
    nd = len(shape)
    return pl.BlockSpec(shape, lambda i: (0,) * nd)


def _params(sem, **kw):
    return pltpu.CompilerParams(dimension_semantics=sem, vmem_limit_bytes=V7X_VMEM_LIMIT, **kw)


def _rot_ret(p, cr, sr):
    return p * cr + pltpu.roll(p, 64, 1) * sr


def _rot_ret_bwd(d, cr, sr):
    return d * cr - pltpu.roll(d, 64, 1) * sr


def _attn_partner(p):
    lane = lax.broadcasted_iota(jnp.int32, p.shape, 1)
    first = (lane % 64) < 32
    return jnp.where(first, pltpu.roll(p, 96, 1), pltpu.roll(p, 32, 1))


def _rot_attn(p, ca, sa):
    return p * ca + _attn_partner(p) * sa


def _rot_attn_bwd(d, ca, sa):
    return d * ca - _attn_partner(d) * sa


def _proj_fwd(x, g1, w_in, b_in, cr, sr, ca, sa):
    S = x.shape[0]
    tm = 512
    scale = RET_QK_DIM ** -0.5

    def body(x_ref, g_ref, w_ref, b_ref, cr_ref, sr_ref, ca_ref, sa_ref,
             h_ref, rq_ref, rk_ref, rv_ref, rg_ref, aq_ref, ak_ref, av_ref, ga_ref, gb_ref):
        xv = x_ref[...]
        r = lax.rsqrt(jnp.mean(xv * xv, axis=-1, keepdims=True) + EPS)
        h = (xv * r * g_ref[...]).astype(BF)
        h_ref[...] = h

        def piece(lo, n):
            return _dot_nt(h, w_ref[lo:lo + n, :]) + b_ref[:, lo:lo + n]

        crv, srv = cr_ref[...], sr_ref[...]
        pq = piece(OFF_RQ, 512)
        pk = piece(OFF_RK, 512)
        for hd in range(RET_HEADS):
            sl = slice(hd * 128, (hd + 1) * 128)
            rq_ref[:, sl] = (_rot_ret(pq[:, sl], crv, srv) * scale).astype(BF)
            rk_ref[:, sl] = _rot_ret(pk[:, sl], crv, srv)
        rv_ref[...] = piece(OFF_RV, 1024).astype(BF)
        rg_ref[...] = piece(OFF_RG, 1024).astype(KEPT)
        cav, sav = ca_ref[...], sa_ref[...]
        pa = piece(OFF_AQ, 1024)
        for blk in range(8):
            sl = slice(blk * 128, (blk + 1) * 128)
            aq_ref[:, sl] = _rot_attn(pa[:, sl], cav, sav).astype(BF)
        pkv = piece(OFF_AK, 256)
        ak_ref[...] = _rot_attn(pkv[:, :128], cav, sav).astype(BF)
        av_ref[...] = pkv[:, 128:].astype(BF)
        ga_ref[...] = piece(OFF_GA, 1024).astype(KEPT)
        gb_ref[...] = piece(OFF_GB, 1024).astype(KEPT)

    outs = [(D_MODEL, BF), (512, BF), (512, F32), (1024, BF), (1024, KEPT),
            (1024, BF), (128, BF), (128, BF), (1024, KEPT), (1024, KEPT)]
    return pl.pallas_call(
        body, name="proj_fwd", grid=(S // tm,),
        in_specs=[_rows(tm, D_MODEL), _fixed((1, D_MODEL)), _VMEM, _VMEM,
                  _rows(tm, 128), _rows(tm, 128), _rows(tm, 128), _rows(tm, 128)],
        out_specs=[_rows(tm, n) for n, _ in outs],
        out_shape=[jax.ShapeDtypeStruct((S, n), dt) for n, dt in outs],
        compiler_params=_params(("parallel",)),
    )(x, g1, w_in, b_in, cr, sr, ca, sa)


def _ret_fwd(rq, rk, rv, dmat, qd, kd, cd):
    S = rq.shape[0]
    N = S // RET_CHUNK
    cps = RET_CHUNKS_PER_STEP
    rows = cps * RET_CHUNK

    def body(q_ref, k_ref, v_ref, dm_ref, qd_ref, kd_ref, cd_ref, y_ref, st_ref, state):
        @pl.when(pl.program_id(0) == 0)
        def _():
            state[...] = jnp.zeros_like(state)

        for hd in range(RET_HEADS):
            qs = slice(hd * 128, (hd + 1) * 128)
            vs = slice(hd * 256, (hd + 1) * 256)
            free = []
            for ch in range(cps):
                rs = slice(ch * RET_CHUNK, (ch + 1) * RET_CHUNK)
                q = q_ref[rs, qs]
                kf = k_ref[rs, qs]
                v = v_ref[rs, vs]
                a = _dot_nt(q, kf.astype(BF)) * dm_ref[hd]
                kdv = (kf * kd_ref[hd]).astype(BF)
                free.append((rs, q, _dot(a.astype(BF), v), _dot_tn(kdv, v)))
            s = state[hd]
            for ch, (rs, q, inner, update) in enumerate(free):
                sb = s.astype(BF)
                st_ref[ch, hd] = sb
                y_ref[rs, vs] = (inner + _dot(q, sb) * qd_ref[hd]).astype(KEPT)
                s = s * cd_ref[hd] + update
            state[hd] = s

    return pl.pallas_call(
        body, name="ret_fwd", grid=(N // cps,),
        in_specs=[_rows(rows, 512), _rows(rows, 512), _rows(rows, 1024), _VMEM, _VMEM, _VMEM, _VMEM],
        out_specs=[_rows(rows, 1024), pl.BlockSpec((cps, RET_HEADS, 128, 256), lambda i: (i, 0, 0, 0))],
        out_shape=[jax.ShapeDtypeStruct((S, 1024), KEPT),
                   jax.ShapeDtypeStruct((N, RET_HEADS, 128, 256), BF)],
        scratch_shapes=[pltpu.VMEM((RET_HEADS, 128, 256), F32)],
        compiler_params=_params(("arbitrary",)),
    )(rq, rk, rv, dmat, qd, kd, cd)


def _kv_halves(prev_ref, cur_ref):
    kk = jnp.concatenate([prev_ref[...], cur_ref[...]], axis=0).astype(F32)
    return kk, pltpu.roll(kk, 64, 1)


PAIRS = 4


def _attn_fold(n):
    t = lax.broadcasted_iota(jnp.int32, (128, PAIRS * 128), 0)
    qi = lax.broadcasted_iota(jnp.int32, (128, PAIRS * 128), 1) & 127
    cur = t <= qi
    cur_f = jnp.where(cur, 1.0, 0.0)
    bias = jnp.where(jnp.logical_or(cur, n > 0), 0.0, -1e30)
    return cur, cur_f, 1.0 - cur_f, bias


def _group_operands(g, kk, kk_sw, vv, vv_sw):
    lo = lax.broadcasted_iota(jnp.int32, (256, 128), 1) < 64
    k_a, k_b = (kk, kk_sw) if g == 0 else (kk_sw, kk)
    v_a, v_b = (vv, vv_sw) if g == 0 else (vv_sw, vv)
    return (jnp.concatenate([jnp.where(lo, k_a, 0.0), jnp.where(lo, 0.0, k_b)], axis=0).astype(BF),
            jnp.concatenate([jnp.where(lo, v_a, 0.0), jnp.where(lo, 0.0, v_b)], axis=0).astype(BF))


def _stack_pairs(ref, g, rows=slice(None)):
    return jnp.concatenate([ref[rows, (4 * g + p) * 128:(4 * g + p + 1) * 128] for p in range(PAIRS)], axis=0)


def _sink_row(sk_ref, g, which):
    return jnp.concatenate([jnp.full((1, 128), sk_ref[0, 8 * g + 2 * p + which], F32) for p in range(PAIRS)], axis=1)


def _fold(full, cur):
    return jnp.where(cur, full[128:], full[:128])


def _attn_probs(raw, fold, sink):
    s = _fold(raw, fold[0]) * (ATTN_HEAD_DIM ** -0.5) + fold[3]
    m = jnp.maximum(jnp.max(s, axis=0, keepdims=True), sink)
    e = jnp.exp(s - m)
    es = jnp.exp(sink - m)
    inv = 1.0 / (jnp.sum(e, axis=0, keepdims=True) + es)
    return e * inv, es * inv


def _unfold(p, fold):
    return jnp.concatenate([p * fold[2], p * fold[1]], axis=0).astype(BF)


def _attn_fwd(aq, ak, av, sinks):
    S = aq.shape[0]
    N = S // SEQ_CHUNK

    def body(sk_ref, q_ref, kp_ref, kc_ref, vp_ref, vc_ref, o_ref):
        kall, kall_sw = _kv_halves(kp_ref, kc_ref)
        vall, vall_sw = _kv_halves(vp_ref, vc_ref)
        blocks = []
        for sb in range(2):
            kv = slice(sb * 128, sb * 128 + 256)
            rows = slice(sb * 128, (sb + 1) * 128)
            ops = [_group_operands(g, kall[kv], kall_sw[kv], vall[kv], vall_sw[kv]) for g in range(2)]
            raw = [_dot_nt(ops[g][0], _stack_pairs(q_ref, g, rows)) for g in range(2)]
            blocks.append((rows, ops, raw, _attn_fold(2 * pl.program_id(0) + sb)))
        for rows, ops, raw, fold in blocks:
            for g in range(2):
                probs = [_unfold(_attn_probs(raw[g][w * 256:(w + 1) * 256], fold, _sink_row(sk_ref, g, w))[0], fold)
                         for w in range(2)]
                o = _dot_tn(jnp.concatenate(probs, axis=0), ops[g][1])
                for p in range(PAIRS):
                    o_ref[rows, (4 * g + p) * 128:(4 * g + p + 1) * 128] = o[p * 128:(p + 1) * 128].astype(BF)

    prev = lambda i: (jnp.maximum(2 * i - 1, 0), 0)
    return pl.pallas_call(
        body, name="attn_fwd", grid=(N // 2,),
        in_specs=[_SMEM, _rows(256, 1024), pl.BlockSpec((128, 128), prev), _rows(256, 128),
                  pl.BlockSpec((128, 128), prev), _rows(256, 128)],
        out_specs=_rows(256, 1024),
        out_shape=jax.ShapeDtypeStruct((S, 1024), BF),
        compiler_params=_params(("parallel",)),
    )(sinks, aq, ak, ak, av, av)


def _group_norm(y):
    mu = jnp.mean(y, axis=-1, keepdims=True)
    yc = y - mu
    rstd = lax.rsqrt(jnp.mean(yc * yc, axis=-1, keepdims=True) + EPS)
    return yc * rstd, rstd


def _merge_fwd(x, ry, rg, ay, ga, gb, g_ret, w_ret, w_attn, w_out):
    S = x.shape[0]
    tm = 512

    def body(x_ref, ry_ref, rg_ref, ay_ref, ga_ref, gb_ref, gr_ref, wr_ref, wa_ref, wo_ref,
             u_ref, mg_ref, x1_ref):
        bb = _dot(ay_ref[...], wa_ref[...])
        for hd in range(RET_HEADS):
            vs = slice(hd * 256, (hd + 1) * 256)
            yn, _ = _group_norm(ry_ref[:, vs].astype(F32))
            rgv = rg_ref[:, vs].astype(F32)
            u_ref[:, vs] = (rgv * _sigmoid(rgv) * (yn * gr_ref[:, vs])).astype(BF)
        gated_b = _sigmoid(gb_ref[...].astype(F32)) * bb
        sa = _sigmoid(ga_ref[...].astype(F32))
        ba = _dot(u_ref[...], wr_ref[...])
        merged = (sa * ba + gated_b).astype(BF)
        mg_ref[...] = merged
        x1_ref[...] = x_ref[...] + _dot(merged, wo_ref[...])

    outs = [BF, BF, F32]
    return pl.pallas_call(
        body, name="merge_fwd", grid=(S // tm,),
        in_specs=[_rows(tm, 1024)] * 6 + [_fixed((1, 1024)), _VMEM, _VMEM, _VMEM],
        out_specs=[_rows(tm, 1024)] * 3,
        out_shape=[jax.ShapeDtypeStruct((S, 1024), dt) for dt in outs],
        compiler_params=_params(("parallel",)),
    )(x, ry, rg, ay, ga, gb, g_ret, w_ret, w_attn, w_out)


def _ffn_fwd(x1, tgt, g2, g3, w_gate, w_up, w_down):
    S = x1.shape[0]
    tm = 256

    def body(x1_ref, t_ref, g2_ref, g3_ref, wg_ref, wu_ref, wd_ref,
             gate_ref, up_ref, h2_ref, act_ref, dx2_ref, loss_ref, dg3_ref):
        @pl.when(pl.program_id(0) == 0)
        def _():
            loss_ref[...] = jnp.zeros_like(loss_ref)
            dg3_ref[...] = jnp.zeros_like(dg3_ref)

        x1v = x1_ref[...]
        r2 = lax.rsqrt(jnp.mean(x1v * x1v, axis=-1, keepdims=True) + EPS)
        h2 = (x1v * r2 * g2_ref[...]).astype(BF)
        h2_ref[...] = h2
        gate = _dot_nt(h2, wg_ref[...])
        up = _dot_nt(h2, wu_ref[...])
        gate_ref[...] = gate
        up_ref[...] = up
        act = (gate * _sigmoid(gate) * up).astype(BF)
        act_ref[...] = act
        x2 = x1v + _dot(act, wd_ref[...])
        r3 = lax.rsqrt(jnp.mean(x2 * x2, axis=-1, keepdims=True) + EPS)
        xh = x2 * r3
        g3v = g3_ref[...]
        err = xh * g3v - t_ref[...]
        loss_ref[...] += 0.5 * jnp.sum(jnp.mean(err * err, axis=-1, keepdims=True), axis=0, keepdims=True)
        dy = err * (1.0 / D_MODEL)
        dg3_ref[...] += jnp.sum(dy * xh, axis=0, keepdims=True)
        dxh = dy * g3v
        dx2_ref[...] = r3 * (dxh - xh * jnp.mean(dxh * xh, axis=-1, keepdims=True))

    return pl.pallas_call(
        body, name="ffn_fwd", grid=(S // tm,),
        in_specs=[_rows(tm, 1024), _rows(tm, 1024), _fixed((1, 1024)), _fixed((1, 1024)), _VMEM, _VMEM, _VMEM],
        out_specs=[_rows(tm, D_FF), _rows(tm, D_FF), _rows(tm, 1024), _rows(tm, D_FF), _rows(tm, 1024),
                   _fixed((1, LANES)), _fixed((1, 1024))],
        out_shape=[jax.ShapeDtypeStruct((S, D_FF), F32), jax.ShapeDtypeStruct((S, D_FF), F32),
                   jax.ShapeDtypeStruct((S, 1024), BF), jax.ShapeDtypeStruct((S, D_FF), BF),
                   jax.ShapeDtypeStruct((S, 1024), F32),
                   jax.ShapeDtypeStruct((1, LANES), F32), jax.ShapeDtypeStruct((1, 1024), F32)],
        compiler_params=_params(("arbitrary",)),
    )(x1, tgt, g2, g3, w_gate, w_up, w_down)


def _ffn_bwd(dx2, gate, up, x1, g2, w_gate, w_up, w_down):
    S = x1.shape[0]
    tm = 256

    def body(dx2_ref, gate_ref, up_ref, x1_ref, g2_ref, wg_ref, wu_ref, wd_ref,
             dgate_ref, dup_ref, dx1_ref, dg2_ref):
        @pl.when(pl.program_id(0) == 0)
        def _():
            dg2_ref[...] = jnp.zeros_like(dg2_ref)

        dx2v = dx2_ref[...]
        dact = _dot_nt(dx2v.astype(BF), wd_ref[...])
        gate = gate_ref[...]
        sg = _sigmoid(gate)
        dgate = (dact * up_ref[...] * (sg * (1.0 + gate * (1.0 - sg)))).astype(BF)
        dup = (dact * (gate * sg)).astype(BF)
        dgate_ref[...] = dgate
        dup_ref[...] = dup
        dh2 = _dot(dgate, wg_ref[...]) + _dot(dup, wu_ref[...])
        x1v = x1_ref[...]
        r2 = lax.rsqrt(jnp.mean(x1v * x1v, axis=-1, keepdims=True) + EPS)
        xh = x1v * r2
        dg2_ref[...] += jnp.sum(dh2 * xh, axis=0, keepdims=True)
        dxh = dh2 * g2_ref[...]
        dx1_ref[...] = dx2v + r2 * (dxh - xh * jnp.mean(dxh * xh, axis=-1, keepdims=True))

    return pl.pallas_call(
        body, name="ffn_bwd", grid=(S // tm,),
        in_specs=[_rows(tm, 1024), _rows(tm, D_FF), _rows(tm, D_FF), _rows(tm, 1024), _fixed((1, 1024)),
                  _VMEM, _VMEM, _VMEM],
        out_specs=[_rows(tm, D_FF), _rows(tm, D_FF), _rows(tm, 1024), _fixed((1, 1024))],
        out_shape=[jax.ShapeDtypeStruct((S, D_FF), BF), jax.ShapeDtypeStruct((S, D_FF), BF),
                   jax.ShapeDtypeStruct((S, 1024), F32), jax.ShapeDtypeStruct((1, 1024), F32)],
        compiler_params=_params(("arbitrary",)),
    )(dx2, gate, up, x1, g2, w_gate, w_up, w_down)


def _merge_bwd(dx1, ga, gb, u, ay, rg, ry, g_ret, w_ret, w_attn, w_out):
    S = dx1.shape[0]
    tm = 512

    def body(dx1_ref, ga_ref, gb_ref, u_ref, ay_ref, rg_ref, ry_ref, gr_ref, wr_ref, wa_ref, wo_ref,
             dgates_ref, drg_ref, dba_ref, dbb_ref, day_ref, dry_ref, dgr_ref):
        @pl.when(pl.program_id(0) == 0)
        def _():
            dgr_ref[...] = jnp.zeros_like(dgr_ref)

        dm = _dot_nt(dx1_ref[...].astype(BF), wo_ref[...])
        ba = _dot(u_ref[...], wr_ref[...])
        bb = _dot(ay_ref[...], wa_ref[...])
        sa = _sigmoid(ga_ref[...].astype(F32))
        sb = _sigmoid(gb_ref[...].astype(F32))
        dga = (dm * ba * (sa * (1.0 - sa))).astype(BF)
        dgb = (dm * bb * (sb * (1.0 - sb))).astype(BF)
        dgates_ref[:, :1024] = dga
        dgates_ref[:, 1024:] = dgb
        dba = (dm * sa).astype(BF)
        dbb = (dm * sb).astype(BF)
        dba_ref[...] = dba
        dbb_ref[...] = dbb
        day_ref[...] = _dot_nt(dbb, wa_ref[...]).astype(BF)
        du = _dot_nt(dba, wr_ref[...])
        for hd in range(RET_HEADS):
            vs = slice(hd * 256, (hd + 1) * 256)
            yn, rstd = _group_norm(ry_ref[:, vs].astype(F32))
            rgv = rg_ref[:, vs].astype(F32)
            sr = _sigmoid(rgv)
            silu = rgv * sr
            duv = du[:, vs]
            grv = gr_ref[:, vs]
            drg = (duv * (yn * grv) * (sr * (1.0 + rgv * (1.0 - sr)))).astype(BF)
            drg_ref[:, vs] = drg
            dyg = duv * silu
            dgr_ref[:, vs] += jnp.sum(dyg * yn, axis=0, keepdims=True)
            dyn = dyg * grv
            dry_ref[:, vs] = (rstd * (dyn - jnp.mean(dyn, axis=-1, keepdims=True)
                                      - yn * jnp.mean(dyn * yn, axis=-1, keepdims=True))).astype(KEPT)

    return pl.pallas_call(
        body, name="merge_bwd", grid=(S // tm,),
        in_specs=[_rows(tm, 1024)] * 7 + [_fixed((1, 1024)), _VMEM, _VMEM, _VMEM],
        out_specs=[_rows(tm, 2048), _rows(tm, 1024), _rows(tm, 1024), _rows(tm, 1024), _rows(tm, 1024),
                   _rows(tm, 1024), _fixed((1, 1024))],
        out_shape=[jax.ShapeDtypeStruct((S, 2048), BF), jax.ShapeDtypeStruct((S, 1024), BF),
                   jax.ShapeDtypeStruct((S, 1024), BF), jax.ShapeDtypeStruct((S, 1024), BF),
                   jax.ShapeDtypeStruct((S, 1024), BF), jax.ShapeDtypeStruct((S, 1024), KEPT),
                   jax.ShapeDtypeStruct((1, 1024), F32)],
        compiler_params=_params(("arbitrary",)),
    )(dx1, ga, gb, u, ay, rg, ry, g_ret, w_ret, w_attn, w_out)


def _attn_bwd(aq, ak, av, day, sinks, ca, sa, after):
    S = aq.shape[0]
    N = S // SEQ_CHUNK
    steps = N // 2

    def body(sk_ref, q_ref, kp_ref, kc_ref, vp_ref, vc_ref, do_ref, cq_ref, sq_ref, cp_ref, sp_ref, after_ref,
             dq_ref, dke_ref, dve_ref, dko_ref, dvo_ref, dsk_ref, carry_k, carry_v):
        i = pl.program_id(0)

        @pl.when(i == 0)
        def _():
            carry_k[...] = jnp.zeros_like(carry_k)
            carry_v[...] = jnp.zeros_like(carry_v)
            dsk_ref[...] = jnp.zeros_like(dsk_ref)

        @pl.when(i < steps)
        def _():
            kall, kall_sw = _kv_halves(kp_ref, kc_ref)
            vall, vall_sw = _kv_halves(vp_ref, vc_ref)
            lo = lax.broadcasted_iota(jnp.int32, (256, 128), 1) < 64
            blocks = []
            for sb in range(2):
                kv = slice(sb * 128, sb * 128 + 256)
                rows = slice(sb * 128, (sb + 1) * 128)
                ops = [_group_operands(g, kall[kv], kall_sw[kv], vall[kv], vall_sw[kv]) for g in range(2)]
                stacked = [(_stack_pairs(q_ref, g, rows), _stack_pairs(do_ref, g, rows)) for g in range(2)]
                raw = [_dot_nt(ops[g][0], stacked[g][0]) for g in range(2)]
                raw_dp = [_dot_nt(ops[g][1], stacked[g][1]) for g in range(2)]
                blocks.append((rows, ops, stacked, raw, raw_dp, _attn_fold(2 * i + sb)))
            grads = []
            for rows, ops, stacked, raw, raw_dp, fold in blocks:
                cur = fold[0]
                cq, sq = cq_ref[rows, :], sq_ref[rows, :]
                dk_groups, dv_groups = [], []
                for g in range(2):
                    qs, dos = stacked[g]
                    ds_parts, p_parts = [], []
                    for which in range(2):
                        half = slice(which * 256, (which + 1) * 256)
                        pr, ps = _attn_probs(raw[g][half], fold, _sink_row(sk_ref, g, which))
                        dp = _fold(raw_dp[g][half], cur)
                        dr = jnp.sum(pr * dp, axis=0, keepdims=True)
                        ds_parts.append(_unfold(pr * (dp - dr) * (ATTN_HEAD_DIM ** -0.5), fold))
                        p_parts.append(_unfold(pr, fold))
                        dsk_ref[2 * g + which:2 * g + which + 1, :] -= ps * dr
                    ds = jnp.concatenate(ds_parts, axis=0)
                    dq = _dot_tn(ds, ops[g][0])
                    dk_both = _dot(ds, qs)
                    dv_both = _dot(jnp.concatenate(p_parts, axis=0), dos)
                    for p in range(PAIRS):
                        sl = slice((4 * g + p) * 128, (4 * g + p + 1) * 128)
                        dq_ref[rows, sl] = _rot_attn_bwd(dq[p * 128:(p + 1) * 128], cq, sq).astype(BF)
                    tk = jnp.where(lo, dk_both[:256], dk_both[256:])
                    tv = jnp.where(lo, dv_both[:256], dv_both[256:])
                    dk_groups.append(tk + pltpu.roll(tk, 64, 1))
                    dv_groups.append(tv + pltpu.roll(tv, 64, 1))
                grads.append((jnp.where(lo, dk_groups[0], dk_groups[1]), jnp.where(lo, dv_groups[0], dv_groups[1])))
            (dk_a, dv_a), (dk_b, dv_b) = grads
            dko_ref[...] = _rot_attn_bwd(carry_k[...] + dk_a[:128], cp_ref[...], sp_ref[...]).astype(BF)
            dvo_ref[...] = (carry_v[...] + dv_a[:128]).astype(BF)
            dke_ref[...] = _rot_attn_bwd(dk_a[128:] + dk_b[:128], cq_ref[0:128, :], sq_ref[0:128, :]).astype(BF)
            dve_ref[...] = (dv_a[128:] + dv_b[:128]).astype(BF)
            carry_k[...] = dk_b[128:]
            carry_v[...] = dv_b[128:]

        @pl.when(i == steps)
        def _():
            dko_ref[...] = _rot_attn_bwd(carry_k[...], cp_ref[...], sp_ref[...]).astype(BF)
            dvo_ref[...] = carry_v[...].astype(BF)

    two = lambda i: (jnp.minimum(i, steps - 1), 0)
    before = lambda i: (jnp.maximum(2 * i - 1, 0), 0)
    odd = lambda i: (jnp.maximum(i - 1, 0), 0)
    blk = lambda r, w, im: pl.BlockSpec((r, w), im)
    half = jax.ShapeDtypeStruct((S // 2, 128), BF)
    dq, dk_even, dv_even, dk_odd, dv_odd, dsk = pl.pallas_call(
        body, name="attn_bwd", grid=(steps + 1,),
        in_specs=[_SMEM, blk(256, 1024, two), blk(128, 128, before), blk(256, 128, two), blk(128, 128, before),
                  blk(256, 128, two), blk(256, 1024, two), blk(256, 128, two), blk(256, 128, two),
                  blk(128, 128, before), blk(128, 128, before), _ANY],
        out_specs=[blk(256, 1024, two), blk(128, 128, two), blk(128, 128, two), blk(128, 128, odd),
                   blk(128, 128, odd), _fixed((4, PAIRS * 128))],
        out_shape=[jax.ShapeDtypeStruct((S, 1024), BF), half, half, half, half,
                   jax.ShapeDtypeStruct((4, PAIRS * 128), F32)],
        scratch_shapes=[pltpu.VMEM((128, 128), F32), pltpu.VMEM((128, 128), F32)],
        compiler_params=_params(("arbitrary",)),
    )(sinks, aq, ak, ak, av, av, day, ca, sa, ca, sa, after)

    def interleave(even, odd_):
        return jnp.stack([even.reshape(steps, 128, 128), odd_.reshape(steps, 128, 128)], axis=1).reshape(S, 128)

    return dq, interleave(dk_even, dk_odd), interleave(dv_even, dv_odd), dsk


def _ret_bwd(rq, rk, rv, dry, states, dmat, qd, kd, cd, cr, sr, after):
    S = rq.shape[0]
    N = S // RET_CHUNK
    scale = RET_QK_DIM ** -0.5
    cps = RET_CHUNKS_PER_STEP
    steps = N // cps

    def body(q_ref, k_ref, v_ref, dy_ref, st_ref, dm_ref, qd_ref, kd_ref, cd_ref, cr_ref, sr_ref, after_ref,
             out_ref, dstate):
        @pl.when(pl.program_id(0) == 0)
        def _():
            dstate[...] = jnp.zeros_like(dstate)

        for hd in range(RET_HEADS):
            qs = slice(hd * 128, (hd + 1) * 128)
            vs = slice(hd * 256, (hd + 1) * 256)
            dmv = dm_ref[hd]
            free = []
            for ch in range(cps):
                rs = slice(ch * RET_CHUNK, (ch + 1) * RET_CHUNK)
                crv, srv = cr_ref[rs, :], sr_ref[rs, :]
                q = q_ref[rs, qs]
                kf = k_ref[rs, qs]
                k = kf.astype(BF)
                v = v_ref[rs, vs]
                do = dy_ref[rs, vs].astype(F32)
                dob = do.astype(BF)
                doq = (do * qd_ref[hd]).astype(BF)
                a = (_dot_nt(q, k) * dmv).astype(BF)
                da = (_dot_nt(dob, v) * dmv).astype(BF)
                dq = _dot(da, k) + _dot_nt(doq, st_ref[ch, hd])
                out_ref[rs, qs] = _rot_ret_bwd(dq * scale, crv, srv).astype(BF)
                free.append((rs, crv, srv, v, (kf * kd_ref[hd]).astype(BF),
                             _dot_tn(a, dob), _dot_tn(da, q), _dot_tn(q, doq)))
            ds = dstate[hd]
            for rs, crv, srv, v, kdv, dv_free, dk_free, ds_update in reversed(free):
                dsb = ds.astype(BF)
                dv = dv_free + _dot(kdv, dsb)
                dk = dk_free + _dot_nt(v, dsb) * kd_ref[hd]
                ds = ds * cd_ref[hd] + ds_update
                out_ref[rs, 512 + hd * 128:512 + (hd + 1) * 128] = _rot_ret_bwd(dk, crv, srv).astype(BF)
                out_ref[rs, 1024 + hd * 256:1024 + (hd + 1) * 256] = dv.astype(BF)
            dstate[hd] = ds

    rev = lambda w: pl.BlockSpec((cps * RET_CHUNK, w), lambda i: (steps - 1 - i, 0))
    return pl.pallas_call(
        body, name="ret_bwd", grid=(steps,),
        in_specs=[rev(512), rev(512), rev(1024), rev(1024),
                  pl.BlockSpec((cps, RET_HEADS, 128, 256), lambda i: (steps - 1 - i, 0, 0, 0)),
                  _VMEM, _VMEM, _VMEM, _VMEM, rev(128), rev(128), _ANY],
        out_specs=rev(2048),
        out_shape=jax.ShapeDtypeStruct((S, 2048), BF),
        scratch_shapes=[pltpu.VMEM((RET_HEADS, 128, 256), F32)],
        compiler_params=_params(("arbitrary",)),
    )(rq, rk, rv, dry, states, dmat, qd, kd, cd, cr, sr, after)


def _proj_bwd(d_pieces, x, dx1, g1, w_in_t):
    S = x.shape[0]
    tm = 512
    widths = [d.shape[1] for d in d_pieces]
    assert sum(widths) == D_IN
    nd = len(d_pieces)

    def body(*refs):
        d_refs = refs[:nd]
        x_ref, dx1_ref, g_ref, w_ref, gx_ref, dg_ref, db_ref = refs[nd:]

        @pl.when(pl.program_id(0) == 0)
        def _():
            dg_ref[...] = jnp.zeros_like(dg_ref)
            db_ref[...] = jnp.zeros_like(db_ref)

        dh = jnp.zeros((tm, D_MODEL), F32)
        lo = 0
        for ref, n in zip(d_refs, widths):
            v = ref[...]
            dh = dh + _dot(v, w_ref[lo:lo + n, :])
            db_ref[:, lo:lo + n] += jnp.sum(v.astype(F32), axis=0, keepdims=True)
            lo += n
        xv = x_ref[...]
        r = lax.rsqrt(jnp.mean(xv * xv, axis=-1, keepdims=True) + EPS)
        xh = xv * r
        dg_ref[...] += jnp.sum(dh * xh, axis=0, keepdims=True)
        dxh = dh * g_ref[...]
        gx_ref[...] = dx1_ref[...] + r * (dxh - xh * jnp.mean(dxh * xh, axis=-1, keepdims=True))

    return pl.pallas_call(
        body, name="proj_bwd", grid=(S // tm,),
        in_specs=[_rows(tm, n) for n in widths] + [_rows(tm, 1024), _rows(tm, 1024), _fixed((1, 1024)), _VMEM],
        out_specs=[_rows(tm, 1024), _fixed((1, 1024)), _fixed((1, D_IN))],
        out_shape=[jax.ShapeDtypeStruct((S, 1024), F32), jax.ShapeDtypeStruct((1, 1024), F32),
                   jax.ShapeDtypeStruct((1, D_IN), F32)],
        compiler_params=_params(("arbitrary",)),
    )(*d_pieces, x, dx1, g1, w_in_t)


SHARD_ROWS = (1600, 256, 256, 256, 704, 704, 704)


class Group(NamedTuple):
    name: str
    tensors: tuple
    offs: tuple
    block: int
    tile: int

    @property
    def half(self):
        return self.block // 2

    def off(self, t):
        return self.offs[self.tensors.index(t)]


LATE = Group("late", (0,), (0,), 1600, 800)
EARLY = Group("early", (1, 2, 3, 4, 5, 6), (0, 256, 512, 768, 1472, 2176), 2880, 480)
GROUP_OF = {t: g for g in (LATE, EARLY) for t in g.tensors}


def _wgrad(a_list, b, t, packed, name, row0=0):
    S = b.shape[0]
    widths = [a.shape[1] for a in a_list]
    M = sum(widths)
    ts = 1024
    steps = S // ts
    na = len(a_list)
    grp = GROUP_OF[t]
    r, off = SHARD_ROWS[t], grp.off(t)
    segments = []
    for chip in range(N_CHIPS):
        lo, hi = max(chip * r, row0), min((chip + 1) * r, row0 + M)
        if lo < hi:
            segments.append((lo - row0, hi - lo, chip * grp.block + off + lo - chip * r))
    assert sum(n for _, n, _ in segments) == M
    stage_rows = max(n for _, n, _ in segments)
    n_in = na + 1 + (2 if packed is not None else 0)

    def body(*refs):
        a_refs, b_ref = refs[:na], refs[na]
        g32_ref, gb_ref, acc, stage, sems = refs[n_in:]
        i = pl.program_id(0)

        @pl.when(i == 0)
        def _():
            acc[...] = jnp.zeros_like(acc)

        bv = b_ref[...].astype(BF)
        lo = 0
        for a_ref, n in zip(a_refs, widths):
            acc[lo:lo + n, :] += _dot_tn(a_ref[...].astype(BF), bv)
            lo += n

        @pl.when(i == steps - 1)
        def _():
            for a0, n, p0 in segments:
                c32 = pltpu.make_async_copy(acc.at[a0:a0 + n, :], g32_ref.at[pl.ds(p0, n), :], sems.at[0])
                c32.start()
                stage[0:n, :] = acc[a0:a0 + n, :].astype(BF)
                c16 = pltpu.make_async_copy(stage.at[0:n, :], gb_ref.at[pl.ds(p0, n), :], sems.at[1])
                c16.start()
                c32.wait()
                c16.wait()

    shapes = [jax.ShapeDtypeStruct((N_CHIPS * grp.block, 1024), F32),
              jax.ShapeDtypeStruct((N_CHIPS * grp.block, 1024), BF)]
    operands = list(a_list) + [b]
    in_specs = [_rows(ts, n) for n in widths] + [_rows(ts, 1024)]
    aliases = {}
    if packed is not None:
        operands += list(packed)
        in_specs += [_ANY, _ANY]
        aliases = {na + 1: 0, na + 2: 1}
    return pl.pallas_call(
        body, name=name, grid=(steps,),
        in_specs=in_specs, out_specs=[_ANY, _ANY], out_shape=shapes,
        scratch_shapes=[pltpu.VMEM((M, 1024), F32), pltpu.VMEM((stage_rows, 1024), BF),
                        pltpu.SemaphoreType.DMA((2,))],
        input_output_aliases=aliases,
        compiler_params=_params(("arbitrary",)),
    )(*operands)


def _stage_weight(place, w, name, after=()):
    r = w.shape[0]
    tr = 800 if r == 1600 else r
    nb = r // tr

    def body(p_ref, w_ref, *rest):
        rest[-1][...] = w_ref[...].astype(BF)

    return pl.pallas_call(
        body, name=name,
        grid_spec=pltpu.PrefetchScalarGridSpec(
            num_scalar_prefetch=1, grid=(nb,),
            in_specs=[pl.BlockSpec((tr, 1024), lambda i, p: (i, 0))] + [_ANY] * len(after),
            out_specs=pl.BlockSpec((tr, 1024), lambda i, p: (p[0] * nb + i, 0))),
        out_shape=jax.ShapeDtypeStruct((N_CHIPS * r, 1024), BF),
        compiler_params=_params(("arbitrary",)),
    )(place, w, *after)


def _kept_block(grp, nb):
    return lambda i, p: ((2 * (i // nb) + p[1]) * nb + i % nb, 0)


def _pair_sum(place, g32, got, grp):
    nb = grp.half // grp.tile

    def body(p_ref, g_ref, r_ref, s16_ref):
        s16_ref[...] = (g_ref[...] + r_ref[...].astype(F32)).astype(BF)

    flat = pl.BlockSpec((grp.tile, 1024), lambda i, p: (i, 0))
    return pl.pallas_call(
        body, name="pair_sum_" + grp.name,
        grid_spec=pltpu.PrefetchScalarGridSpec(
            num_scalar_prefetch=1, grid=(N_CHIPS * nb,),
            in_specs=[pl.BlockSpec((grp.tile, 1024), _kept_block(grp, nb)), flat], out_specs=flat),
        out_shape=jax.ShapeDtypeStruct((N_CHIPS * grp.half, 1024), BF),
        compiler_params=_params(("arbitrary",)),
    )(place, g32, got)


def _chip_sum(place, g32, got, others, grp):
    nb = grp.half // grp.tile

    def body(p_ref, g_ref, r_ref, r0_ref, r1_ref, r2_ref, o_ref):
        own = g_ref[...] + r_ref[...].astype(F32)
        o_ref[...] = ((own + r0_ref[...].astype(F32)) + r1_ref[...].astype(F32)) + r2_ref[...].astype(F32)

    blk = lambda f: pl.BlockSpec((grp.tile, 1024), f)
    return pl.pallas_call(
        body, name="chip_sum_" + grp.name,
        grid_spec=pltpu.PrefetchScalarGridSpec(
            num_scalar_prefetch=1, grid=(nb,),
            in_specs=[blk(lambda i, p: ((2 * p[0] + p[1]) * nb + i, 0)), blk(lambda i, p: (p[0] * nb + i, 0)),
                      blk(lambda i, p: (i, 0)), blk(lambda i, p: (nb + i, 0)), blk(lambda i, p: (2 * nb + i, 0))],
            out_specs=blk(lambda i, p: (p[1] * nb + i, 0))),
        out_shape=jax.ShapeDtypeStruct((grp.block, 1024), F32),
        compiler_params=_params(("arbitrary",)),
    )(place, g32, got, others, others, others)


def _adamw(w, g, m, v, name, g_row0=0, after=None):
    r = w.shape[0]
    tr = 320 if r % 320 == 0 and g_row0 % 320 == 0 else (64 if r % 64 == 0 else r)
    assert g_row0 % tr == 0
    g0 = g_row0 // tr
    extra = [] if after is None else [after]

    def body(w_ref, g_ref, m_ref, v_ref, *rest):
        go_ref, d_ref, mo_ref, vo_ref = rest[len(extra):]
        gv = g_ref[...]
        mn = ADAM_B1 * m_ref[...] + (1.0 - ADAM_B1) * gv
        vn = ADAM_B2 * v_ref[...] + (1.0 - ADAM_B2) * (gv * gv)
        m_hat = mn / (1.0 - ADAM_B1 ** ADAM_STEP)
        v_hat = vn / (1.0 - ADAM_B2 ** ADAM_STEP)
        go_ref[...] = gv
        d_ref[...] = -ADAM_LR * (m_hat / (jnp.sqrt(v_hat) + ADAM_EPS) + ADAM_WD * w_ref[...])
        mo_ref[...] = mn
        vo_ref[...] = vn

    return pl.pallas_call(
        body, name=name, grid=(r // tr,),
        in_specs=[_rows(tr, 1024), pl.BlockSpec((tr, 1024), lambda i: (g0 + i, 0)), _rows(tr, 1024), _rows(tr, 1024)]
        + [_ANY] * len(extra),
        out_specs=[_rows(tr, 1024)] * 4,
        out_shape=[jax.ShapeDtypeStruct((r, 1024), F32)] * 4,
        compiler_params=_params(("parallel",)),
    )(w, g, m, v, *extra)


def _place():
    x, y, c = lax.axis_index("x"), lax.axis_index("y"), lax.axis_index("c")
    chips = [(1 - x, y), (x, 1 - y), (1 - x, 1 - y)]
    return x, y, c, chips


_HBM =pl.BlockSpec(memory_space=pltpu.HBM)
_SEM = pl.BlockSpec(memory_space=pltpu.SEMAPHORE)
_DATAFLOW = pltpu.SideEffectType.DATAFLOW_SIDE_EFFECTING


def _exchange(name, arrays, n, plan):
    na = len(arrays)

    def body(*refs):
        pairs = plan(refs[na:2 * na], refs[2 * na], refs[2 * na + 1])
        for send, _ in pairs:
            send.start()
        for send, recv in pairs:
            send.wait_send()
            recv.wait_recv()

    return pl.pallas_call(
        body, name=name, in_specs=[_ANY] * na, out_specs=[_ANY] * na,
        out_shape=[jax.ShapeDtypeStruct(a.shape, a.dtype) for a in arrays],
        scratch_shapes=[pltpu.SemaphoreType.DMA((n,)), pltpu.SemaphoreType.DMA((n,))],
        input_output_aliases={i: i for i in range(na)},
    )(*arrays)


def _exchange_start(name, arrays, n, plan, after=()):
    na = len(arrays)

    def body(*refs):
        for send, _ in plan(refs[:na], refs[na + len(after)], refs[na + len(after) + 1]):
            send.start()
        refs[-1][...] = jnp.zeros_like(refs[-1])

    out = pl.pallas_call(
        body, name=name,
        out_shape=(pltpu.SemaphoreType.DMA((n,)), pltpu.SemaphoreType.DMA((n,)),
                   *[pltpu.HBM(a.shape, a.dtype) for a in arrays], jax.ShapeDtypeStruct((8, LANES), F32)),
        in_specs=[_HBM] * na + [_ANY] * len(after), out_specs=(_SEM, _SEM, *[_HBM] * na, _VMEM),
        input_output_aliases={i: 2 + i for i in range(na)},
        compiler_params=pltpu.CompilerParams(has_side_effects=_DATAFLOW),
    )(*[pltpu.with_memory_space_constraint(a, pltpu.HBM) for a in arrays], *after)
    return (out[0], out[1]), list(out[2:2 + na]), out[-1]


def _exchange_wait(name, sems, in_flight, plan, after=()):
    na = len(in_flight)

    def body(*refs):
        for send, recv in plan(refs[:na], refs[na], refs[na + 1]):
            send.wait_send()
            recv.wait_recv()

    out = pl.pallas_call(
        body, name=name,
        out_shape=tuple(pltpu.HBM(a.shape, a.dtype) for a in in_flight),
        in_specs=[_HBM] * na + [_SEM, _SEM] + [_ANY] * len(after), out_specs=tuple([_HBM] * na),
        input_output_aliases={i: i for i in range(na)},
        compiler_params=pltpu.CompilerParams(has_side_effects=_DATAFLOW),
    )(*in_flight, *sems, *after)
    return list(out)


def _remote(src, dst, send_sems, recv_sems, k, to):
    return pltpu.make_async_remote_copy(src_ref=src, dst_ref=dst, send_sem=send_sems.at[k], recv_sem=recv_sems.at[k],
                                        device_id=to, device_id_type=MESH)


def _half_rows(ref, tid, pj, half):
    h = SHARD_ROWS[tid] // 2
    return ref.at[pl.ds((2 * pj + half) * h, h), :]


def _plan_gather_halves(tids):
    def plan(refs, ss, rs):
        x, y, c, chips = _place()
        pairs = []
        for t, tid in enumerate(tids):
            mine = _half_rows(refs[t], tid, 2 * x + y, c)
            for k, (cx, cy) in enumerate(chips):
                theirs = _half_rows(refs[t], tid, 2 * cx + cy, c)
                pairs.append((_remote(mine, mine, ss, rs, 3 * t + k, (cx, cy, c)),
                              _remote(theirs, theirs, ss, rs, 3 * t + k, (x, y, c))))
        return pairs
    return plan


def _plan_forward_halves(tids):
    def plan(refs, ss, rs):
        x, y, c, chips = _place()
        pairs = []
        for t, tid in enumerate(tids):
            for k, (cx, cy) in enumerate(chips):
                got = _half_rows(refs[t], tid, 2 * cx + cy, c)
                missing = _half_rows(refs[t], tid, 2 * cx + cy, 1 - c)
                pairs.append((_remote(got, got, ss, rs, 3 * t + k, (x, y, 1 - c)),
                              _remote(missing, missing, ss, rs, 3 * t + k, (x, y, 1 - c))))
        return pairs
    return plan


def _plan_pair_exchange(grp):
    def plan(refs, ss, rs):
        x, y, c, _ = _place()
        cps = [_remote(refs[0].at[pl.ds(pj * grp.block + (1 - c) * grp.half, grp.half), :],
                       refs[1].at[pl.ds(pj * grp.half, grp.half), :], ss, rs, pj, (x, y, 1 - c))
               for pj in range(N_CHIPS)]
        return [(cp, cp) for cp in cps]
    return plan


def _plan_to_chips(grp):
    def plan(refs, ss, rs):
        x, y, c, chips = _place()
        cps = [_remote(refs[0].at[pl.ds((2 * cx + cy) * grp.half, grp.half), :],
                       refs[1].at[pl.ds(k * grp.half, grp.half), :], ss, rs, k, (cx, cy, c))
               for k, (cx, cy) in enumerate(chips)]
        return [(cp, cp) for cp in cps]
    return plan


def _plan_share_halves(grp):
    def plan(refs, ss, rs):
        x, y, c, _ = _place()
        mine = refs[0].at[pl.ds(c * grp.half, grp.half), :]
        other = refs[0].at[pl.ds((1 - c) * grp.half, grp.half), :]
        return [(_remote(mine, mine, ss, rs, 0, (x, y, 1 - c)), _remote(other, other, ss, rs, 0, (x, y, 1 - c)))]
    return plan


def _empty(rows, dtype):
    return lax.empty((rows, 1024), dtype)


def _all_reduce_small(v, after):
    m_per, n = v.shape
    n_dev = 8

    def body(x_ref, after_ref, out_ref, all_ref, send_sems, recv_sems, local_sem):
        x, y, c, chips = _place()
        me, sibling = (x, y, c), (x, y, 1 - c)

        def rows(px, py, pc):
            return all_ref.at[pl.ds((4 * px + 2 * py + pc) * m_per, m_per), :]

        def copy(k, block, to, src=None):
            return pltpu.make_async_remote_copy(
                src_ref=rows(*block) if src is None else src, dst_ref=rows(*block),
                send_sem=send_sems.at[k], recv_sem=recv_sems.at[k], device_id=to, device_id_type=MESH)

        mine = pltpu.make_async_copy(x_ref, rows(*me), local_sem)
        mine.start()
        first = [copy(0, me, sibling, src=x_ref)]
        first += [copy(1 + k, me, (*chip, c), src=x_ref) for k, chip in enumerate(chips)]
        for cp in first:
            cp.start()
        passed = [copy(4 + k, (*chip, c), sibling) for k, chip in enumerate(chips)]
        for k, chip in enumerate(chips):
            copy(1 + k, (*chip, c), me).wait_recv()
            passed[k].start()
        copy(0, sibling, me).wait_recv()
        for k, chip in enumerate(chips):
            copy(4 + k, (*chip, 1 - c), me).wait_recv()
        for cp in first + passed:
            cp.wait_send()
        mine.wait()
        acc = all_ref[pl.ds(0, m_per), :]
        for d in range(1, n_dev):
            acc = acc + all_ref[pl.ds(d * m_per, m_per), :]
        out_ref[...] = acc

    return pl.pallas_call(
        body, name="all_reduce_small",
        in_specs=[_VMEM, _ANY], out_specs=_VMEM,
        out_shape=jax.ShapeDtypeStruct((m_per, n), F32),
        scratch_shapes=[pltpu.VMEM((n_dev * m_per, n), F32),
                        pltpu.SemaphoreType.DMA((7,)), pltpu.SemaphoreType.DMA((7,)), pltpu.SemaphoreType.DMA],
        compiler_params=pltpu.CompilerParams(has_side_effects=True),
    )(v, after)


def _tables(S, token):
    pos = jnp.arange(S, dtype=jnp.int32).astype(F32) + token[0, 0]

    def cs(half):
        inv_freq = ROPE_THETA ** (-jnp.arange(half, dtype=F32) / half)
        ang = pos[:, None] * inv_freq[None, :]
        return jnp.cos(ang), jnp.sin(ang)

    c64, s64 = cs(64)
    c32, s32 = cs(32)
    cr = jnp.concatenate([c64, c64], axis=1)
    sr = jnp.concatenate([-s64, s64], axis=1)
    ca = jnp.concatenate([c32, c32, c32, c32], axis=1)
    sa = jnp.concatenate([-s32, s32, -s32, s32], axis=1)
    return cr, sr, ca, sa


def _decays():
    H, C = RET_HEADS, RET_CHUNK
    log_gamma = jnp.log1p(-jnp.exp2(-5.0 - jnp.arange(H, dtype=F32)))
    idx = jnp.arange(C, dtype=F32)
    rel = idx[:, None] - idx[None, :]
    dmat = jnp.where(rel[None] >= 0, jnp.exp(log_gamma[:, None, None] * jnp.maximum(rel, 0.0)[None]), 0.0)
    qd = jnp.exp(log_gamma[:, None] * (idx + 1.0))[:, :, None]
    kd = jnp.exp(log_gamma[:, None] * (C - 1.0 - idx))[:, :, None]
    cd = jnp.exp(log_gamma * C)[:, None, None]
    return dmat, qd, kd, cd


def _local_step(place, x, tgt, ln1_g, b_in, ret_norm_g, attn_sinks, ln2_g, lnf_g, w_in, other_weights, tables, decays):
    cr, sr, ca, sa = tables
    dmat, qd, kd, cd = decays
    forward_weights, landed_weights = other_weights

    h, rq, rk, rv, rg, aq, ak, av, ga, gb = _proj_fwd(x, ln1_g, w_in, b_in, cr, sr, ca, sa)
    ry, states = _ret_fwd(rq, rk, rv, dmat, qd, kd, cd)
    token = forward_weights(ry)
    ay = _attn_fwd(aq, ak, av, attn_sinks + token[0, 0])
    w_ret, w_attn, w_out, w_gate, w_up, w_down = landed_weights(ay)
    u, merged, x1 = _merge_fwd(x, ry, rg, ay, ga, gb, ret_norm_g, w_ret, w_attn, w_out)
    gate, up, h2, act, dx2, loss, dg3 = _ffn_fwd(x1, tgt, ln2_g, lnf_g, w_gate, w_up, w_down)

    dgate, dup, dx1, dg2 = _ffn_bwd(dx2, gate, up, x1, ln2_g, w_gate, w_up, w_down)
    early = _wgrad([act], dx2, 6, None, "wgrad_down")
    early = _wgrad([dgate], h2, 4, early, "wgrad_gate")
    early = _wgrad([dup], h2, 5, early, "wgrad_up")
    dgates, drg, dba, dbb, day, dry, dgr = _merge_bwd(dx1, ga, gb, u, ay, rg, ry, ret_norm_g,
                                                                w_ret, w_attn, w_out)
    early = _wgrad([merged], dx1, 3, early, "wgrad_out")
    early = _wgrad([u], dba, 1, early, "wgrad_ret")
    e32, e16 = _wgrad([ay], dbb, 2, early, "wgrad_attn")

    grp = EARLY
    sems, flight, token = _exchange_start("pair_start_early", [e16, _empty(N_CHIPS * grp.half, BF)], N_CHIPS,
                                          _plan_pair_exchange(grp))
    daq, dak, dav, dsk = _attn_bwd(aq, ak, av, day, attn_sinks, ca, sa, token)
    _, got = _exchange_wait("pair_wait_early", sems, flight, _plan_pair_exchange(grp), after=(dsk,))
    s16 = _pair_sum(place, e32, got, grp)
    sems, flight, token = _exchange_start("chips_start_early", [s16, _empty(3 * grp.half, BF)], 3, _plan_to_chips(grp))
    d_ret = _ret_bwd(rq, rk, rv, dry, states, dmat, qd, kd, cd, cr, sr, token)
    d_proj = [d_ret, drg, daq, dak, dav, dgates]
    grad_x, dg1, db = _proj_bwd(d_proj, x, dx1, ln1_g, w_in)
    late = _wgrad(d_proj[:2], h, 0, None, "wgrad_in_a")
    late = _wgrad(d_proj[2:], h, 0, late, "wgrad_in_b", row0=d_ret.shape[1] + drg.shape[1])

    def finish_early(after):
        _, others = _exchange_wait("chips_wait_early", sems, flight, _plan_to_chips(grp), after=after)
        return _chip_sum(place, e32, got, others, grp)

    d_sinks = dsk.reshape(2, 2, PAIRS, 128).sum(-1).transpose(0, 2, 1).reshape(1, ATTN_Q_HEADS)
    small = dict(loss=loss, ln1_g=dg1, b_in=db, ret_norm_g=dgr, attn_sinks=d_sinks,
                 ln2_g=dg2, lnf_g=dg3)
    return grad_x, finish_early, late, small


SMALL_ROWS = 16


def _pack_small(ln1, ret, ln2, lnf, b_in, sinks, loss, name):
    def body(a_ref, b_ref, c_ref, d_ref, bi_ref, sk_ref, ls_ref, o_ref):
        o_ref[...] = jnp.zeros_like(o_ref)
        for row, ref in enumerate((a_ref, b_ref, c_ref, d_ref)):
            o_ref[row:row + 1, :] = ref[...]
        for k in range(6):
            o_ref[4 + k:5 + k, :] = bi_ref[:, k * 1024:(k + 1) * 1024]
        o_ref[10:11, 0:D_IN - 6144] = bi_ref[:, 6144:D_IN]
        o_ref[11:12, 0:ATTN_Q_HEADS] = sk_ref[...]
        o_ref[12:13, 0:LANES] = ls_ref[...]

    return pl.pallas_call(
        body, name=name, in_specs=[_VMEM] * 7, out_specs=_VMEM,
        out_shape=jax.ShapeDtypeStruct((SMALL_ROWS, 1024), F32),
    )(ln1.reshape(1, 1024), ret.reshape(1, 1024), ln2.reshape(1, 1024), lnf.reshape(1, 1024),
      b_in.reshape(1, D_IN), sinks.reshape(1, ATTN_Q_HEADS), loss.reshape(1, LANES))


def _unpack_small(p):
    return dict(ln1_g=p[0:1], ret_norm_g=p[1:2], ln2_g=p[2:3], lnf_g=p[3], b_in=p[4:11].reshape(1, -1)[:, :D_IN],
                attn_sinks=p[11:12, :ATTN_Q_HEADS], loss=p[12, 0])


def kernel(x, ln1_g, w_in, b_in, ret_norm_g, w_ret_out, attn_sinks, w_attn_out, w_out, ln2_g, w_ffn_gate, w_ffn_up, w_ffn_down, lnf_g, loss_target, m_ln1_g, m_w_in, m_b_in, m_ret_norm_g, m_w_ret_out, m_attn_sinks, m_w_attn_out, m_w_out, m_ln2_g, m_w_ffn_gate, m_w_ffn_up, m_w_ffn_down, m_lnf_g, v_ln1_g, v_w_in, v_b_in, v_ret_norm_g, v_w_ret_out, v_attn_sinks, v_w_attn_out, v_w_out, v_ln2_g, v_w_ffn_gate, v_w_ffn_up, v_w_ffn_down, v_lnf_g):
    xi = lax.axis_index("x")
    yi = lax.axis_index("y")
    ci = lax.axis_index("c")
    j = 2 * xi + yi

    place = jnp.stack([j, ci]).astype(jnp.int32)

    names = ["w_in", "w_ret_out", "w_attn_out", "w_out", "w_ffn_gate", "w_ffn_up", "w_ffn_down"]
    transposed = [True, False, False, False, True, True, False]
    view = lambda a, t: a[0].T if t else a[0]
    big_w = [view(a, t) for a, t in zip([w_in, w_ret_out, w_attn_out, w_out, w_ffn_gate, w_ffn_up, w_ffn_down], transposed)]
    big_m = [view(a, t) for a, t in zip([m_w_in, m_w_ret_out, m_w_attn_out, m_w_out, m_w_ffn_gate, m_w_ffn_up, m_w_ffn_down], transposed)]
    big_v = [view(a, t) for a, t in zip([v_w_in, v_w_ret_out, v_w_attn_out, v_w_out, v_w_ffn_gate, v_w_ffn_up, v_w_ffn_down], transposed)]

    staged_in = _stage_weight(place, big_w[0], "stage_" + names[0])
    sems, in_flight, token = _exchange_start("gather_in_start", [staged_in], 3, _plan_gather_halves([0]))
    staged = [_stage_weight(place, w, "stage_" + n, after=(token,)) for n, w in zip(names[1:], big_w[1:])]
    tables = _tables(x.shape[1], token)
    decays = _decays()
    w_in_full = _exchange_wait("gather_in_wait", sems, in_flight, _plan_gather_halves([0]),
                               after=(*staged, *tables, *decays))[0]
    w_in_full = _exchange("gather_in_forward", [w_in_full], 3, _plan_forward_halves([0]))[0]
    rest = list(range(1, 7))
    crossing = _exchange_start("gather_start", staged, 3 * len(rest), _plan_gather_halves(rest), after=(w_in_full,))
    token = crossing[2]
    forwarding = []

    def forward_weights(after):
        arrived = _exchange_wait("gather_wait", crossing[0], crossing[1], _plan_gather_halves(rest), after=(after,))
        forwarding.extend(_exchange_start("forward_start", arrived, 3 * len(rest), _plan_forward_halves(rest)))
        return forwarding[2]

    def landed_weights(after):
        return _exchange_wait("forward_wait", forwarding[0], forwarding[1], _plan_forward_halves(rest), after=(after,))

    other_weights = (forward_weights, landed_weights)

    grad_x, finish_early, (l32, l16), small = _local_step(
        place, x[0], loss_target[0], ln1_g, b_in + token[0, 0], ret_norm_g, attn_sinks, ln2_g, lnf_g.reshape(1, -1),
        w_in_full, other_weights, tables, decays)
    pg = _all_reduce_small(_pack_small(
        small["ln1_g"], small["ret_norm_g"], small["ln2_g"], small["lnf_g"], small["b_in"],
        small["attn_sinks"], small["loss"], "pack_small_grads"), l32)
    early_shard = _exchange("share_early", [finish_early((l32,))], 1, _plan_share_halves(EARLY))[0]
    _, got = _exchange("pair_exchange_late", [l16, _empty(N_CHIPS * LATE.half, BF)], N_CHIPS, _plan_pair_exchange(LATE))
    s16 = _pair_sum(place, l32, got, LATE)
    sems, flight, token = _exchange_start("chips_start_late", [s16, _empty(3 * LATE.half, BF)], 3, _plan_to_chips(LATE),
                                          after=(pg, early_shard))

    upd = {}

    def update(t, shard, after):
        res = _adamw(big_w[t], shard, big_m[t], big_v[t], "adamw_" + names[t], g_row0=GROUP_OF[t].off(t), after=after)
        upd[names[t]] = tuple((r.T if transposed[t] else r)[None] for r in res)
        return res[0]

    done = [update(t, early_shard, token) for t in EARLY.tensors]

    sm = _unpack_small(pg)
    z = jnp.zeros((1, LANES), F32)
    pw = _pack_small(ln1_g, ret_norm_g, ln2_g, lnf_g, b_in, attn_sinks, z, "pack_small_w")
    pm = _pack_small(m_ln1_g, m_ret_norm_g, m_ln2_g, m_lnf_g, m_b_in, m_attn_sinks, z, "pack_small_m")
    pv = _pack_small(v_ln1_g, v_ret_norm_g, v_ln2_g, v_lnf_g, v_b_in, v_attn_sinks, z, "pack_small_v")
    small_res = _adamw(pw, pg, pm, pv, "adamw_small", after=token)
    sd, smn, svn = (_unpack_small(t) for t in small_res[1:])
    for n in ["ln1_g", "b_in", "ret_norm_g", "attn_sinks", "ln2_g", "lnf_g"]:
        upd[n] = (sm[n], sd[n], smn[n], svn[n])

    _, others = _exchange_wait("chips_wait_late", sems, flight, _plan_to_chips(LATE), after=(*done, small_res[1]))
    late_shard = _exchange("share_late", [_chip_sum(place, l32, got, others, LATE)], 1, _plan_share_halves(LATE))[0]
    update(0, late_shard, None)

    order_w = ["ln1_g", "w_in", "b_in", "ret_norm_g", "w_ret_out", "attn_sinks", "w_attn_out", "w_out",
               "ln2_g", "w_ffn_gate", "w_ffn_up", "w_ffn_down", "lnf_g"]
    outs = [sm["loss"], grad_x[None]]
    for k in range(4):
        outs += [upd[n][k] for n in order_w]
    return tuple(outs)
```

```python
from typing import NamedTuple

import jax
import jax.numpy as jnp
from jax import lax
from jax.experimental import pallas as pl
from jax.experimental.pallas import tpu as pltpu

D_MODEL = 1024
SEQ_CHUNK = 128
RET_CHUNK = 256
RET_CHUNKS_PER_STEP = 4
RET_HEADS = 4
RET_QK_DIM = 128
RET_V_DIM = 256
ATTN_Q_HEADS = 16
ATTN_HEAD_DIM = 64
D_FF = 2816
D_IN = 6400
ROPE_THETA = 10000.0
EPS = 1e-6
N_CHIPS = 4

ADAM_LR = 0.001
ADAM_B1 = 0.9
ADAM_B2 = 0.999
ADAM_EPS = 1e-08
ADAM_WD = 0.01
ADAM_STEP = 10

OFF_RQ, OFF_RK, OFF_RV, OFF_RG = 0, 512, 1024, 2048
OFF_AQ, OFF_AK, OFF_AV, OFF_GA, OFF_GB = 3072, 4096, 4224, 4352, 5376

V7X_VMEM_LIMIT = 56 * 1024 * 1024
LANES = 128

BF = jnp.bfloat16
F32 = jnp.float32
KEPT = jnp.bfloat16
MESH = pl.DeviceIdType.MESH

_VMEM = pl.BlockSpec(memory_space=pltpu.VMEM)
_SMEM = pl.BlockSpec(memory_space=pltpu.SMEM)
_ANY = pl.BlockSpec(memory_space=pl.ANY)


def _dot(a, b):
    return jnp.dot(a, b, preferred_element_type=F32)


def _dot_nt(a, b):
    return lax.dot_general(a, b, (((1,), (1,)), ((), ())), preferred_element_type=F32)


def _dot_tn(a, b):
    return lax.dot_general(a, b, (((0,), (0,)), ((), ())), preferred_element_type=F32)


def _sigmoid(x):
    return 0.5 * jnp.tanh(0.5 * x) + 0.5


def _rows(tm, n):
    return pl.BlockSpec((tm, n), lambda i: (i, 0))


def _fixed(shape):
    nd = len(shape)
    return pl.BlockSpec(shape, lambda i: (0,) * nd)


def _params(sem, **kw):
    return pltpu.CompilerParams(dimension_semantics=sem, vmem_limit_bytes=V7X_VMEM_LIMIT, **kw)


def _rot_ret(p, cr, sr):
    return p * cr + pltpu.roll(p, 64, 1) * sr


def _rot_ret_bwd(d, cr, sr):
    return d * cr - pltpu.roll(d, 64, 1) * sr


def _attn_partner(p):
    lane = lax.broadcasted_iota(jnp.int32, p.shape, 1)
    first = (lane % 64) < 32
    return jnp.where(first, pltpu.roll(p, 96, 1), pltpu.roll(p, 32, 1))


def _rot_attn(p, ca, sa):
    return p * ca + _attn_partner(p) * sa


def _rot_attn_bwd(d, ca, sa):
    return d * ca - _attn_partner(d) * sa


def _proj_fwd(x, g1, w_in, b_in, cr, sr, ca, sa):
    S = x.shape[0]
    tm = 512
    scale = RET_QK_DIM ** -0.5

    def body(x_ref, g_ref, w_ref, b_ref, cr_ref, sr_ref, ca_ref, sa_ref,
             h_ref, rq_ref, rk_ref, rv_ref, rg_ref, aq_ref, ak_ref, av_ref, ga_ref, gb_ref):
        xv = x_ref[...]
        r = lax.rsqrt(jnp.mean(xv * xv, axis=-1, keepdims=True) + EPS)
        h = (xv * r * g_ref[...]).astype(BF)
        h_ref[...] = h

        def piece(lo, n):
            return _dot_nt(h, w_ref[lo:lo + n, :]) + b_ref[:, lo:lo + n]

        crv, srv = cr_ref[...], sr_ref[...]
        pq = piece(OFF_RQ, 512)
        pk = piece(OFF_RK, 512)
        for hd in range(RET_HEADS):
            sl = slice(hd * 128, (hd + 1) * 128)
            rq_ref[:, sl] = (_rot_ret(pq[:, sl], crv, srv) * scale).astype(BF)
            rk_ref[:, sl] = _rot_ret(pk[:, sl], crv, srv)
        rv_ref[...] = piece(OFF_RV, 1024).astype(BF)
        rg_ref[...] = piece(OFF_RG, 1024).astype(KEPT)
        cav, sav = ca_ref[...], sa_ref[...]
        pa = piece(OFF_AQ, 1024)
        for blk in range(8):
            sl = slice(blk * 128, (blk + 1) * 128)
            aq_ref[:, sl] = _rot_attn(pa[:, sl], cav, sav).astype(BF)
        pkv = piece(OFF_AK, 256)
        ak_ref[...] = _rot_attn(pkv[:, :128], cav, sav).astype(BF)
        av_ref[...] = pkv[:, 128:].astype(BF)
        ga_ref[...] = piece(OFF_GA, 1024).astype(KEPT)
        gb_ref[...] = piece(OFF_GB, 1024).astype(KEPT)

    outs = [(D_MODEL, BF), (512, BF), (512, F32), (1024, BF), (1024, KEPT),
            (1024, BF), (128, BF), (128, BF), (1024, KEPT), (1024, KEPT)]
    return pl.pallas_call(
        body, name="proj_fwd", grid=(S // tm,),
        in_specs=[_rows(tm, D_MODEL), _fixed((1, D_MODEL)), _VMEM, _VMEM,
                  _rows(tm, 128), _rows(tm, 128), _rows(tm, 128), _rows(tm, 128)],
        out_specs=[_rows(tm, n) for n, _ in outs],
        out_shape=[jax.ShapeDtypeStruct((S, n), dt) for n, dt in outs],
        compiler_params=_params(("parallel",)),
    )(x, g1, w_in, b_in, cr, sr, ca, sa)


def _ret_fwd(rq, rk, rv, dmat, qd, kd, cd):
    S = rq.shape[0]
    N = S // RET_CHUNK
    cps = RET_CHUNKS_PER_STEP
    rows = cps * RET_CHUNK

    def body(q_ref, k_ref, v_ref, dm_ref, qd_ref, kd_ref, cd_ref, y_ref, st_ref, state):
        @pl.when(pl.program_id(0) == 0)
        def _():
            state[...] = jnp.zeros_like(state)

        for hd in range(RET_HEADS):
            qs = slice(hd * 128, (hd + 1) * 128)
            vs = slice(hd * 256, (hd + 1) * 256)
            free = []
            for ch in range(cps):
                rs = slice(ch * RET_CHUNK, (ch + 1) * RET_CHUNK)
                q = q_ref[rs, qs]
                kf = k_ref[rs, qs]
                v = v_ref[rs, vs]
                a = _dot_nt(q, kf.astype(BF)) * dm_ref[hd]
                kdv = (kf * kd_ref[hd]).astype(BF)
                free.append((rs, q, _dot(a.astype(BF), v), _dot_tn(kdv, v)))
            s = state[hd]
            for ch, (rs, q, inner, update) in enumerate(free):
                sb = s.astype(BF)
                st_ref[ch, hd] = sb
                y_ref[rs, vs] = (inner + _dot(q, sb) * qd_ref[hd]).astype(KEPT)
                s = s * cd_ref[hd] + update
            state[hd] = s

    return pl.pallas_call(
        body, name="ret_fwd", grid=(N // cps,),
        in_specs=[_rows(rows, 512), _rows(rows, 512), _rows(rows, 1024), _VMEM, _VMEM, _VMEM, _VMEM],
        out_specs=[_rows(rows, 1024), pl.BlockSpec((cps, RET_HEADS, 128, 256), lambda i: (i, 0, 0, 0))],
        out_shape=[jax.ShapeDtypeStruct((S, 1024), KEPT),
                   jax.ShapeDtypeStruct((N, RET_HEADS, 128, 256), BF)],
        scratch_shapes=[pltpu.VMEM((RET_HEADS, 128, 256), F32)],
        compiler_params=_params(("arbitrary",)),
    )(rq, rk, rv, dmat, qd, kd, cd)


def _kv_halves(prev_ref, cur_ref):
    kk = jnp.concatenate([prev_ref[...], cur_ref[...]], axis=0).astype(F32)
    return kk, pltpu.roll(kk, 64, 1)


ATTN_FWD_BLOCKS_PER_STEP = 4
PAIRS = 4


def _attn_fold(n):
    t = lax.broadcasted_iota(jnp.int32, (128, PAIRS * 128), 0)
    qi = lax.broadcasted_iota(jnp.int32, (128, PAIRS * 128), 1) & 127
    cur = t <= qi
    cur_f = jnp.where(cur, 1.0, 0.0)
    bias = jnp.where(jnp.logical_or(cur, n > 0), 0.0, -1e30)
    return cur, cur_f, 1.0 - cur_f, bias


def _group_operands(g, kk, kk_sw, vv, vv_sw):
    lo = lax.broadcasted_iota(jnp.int32, (256, 128), 1) < 64
    k_a, k_b = (kk, kk_sw) if g == 0 else (kk_sw, kk)
    v_a, v_b = (vv, vv_sw) if g == 0 else (vv_sw, vv)
    return (jnp.concatenate([jnp.where(lo, k_a, 0.0), jnp.where(lo, 0.0, k_b)], axis=0).astype(BF),
            jnp.concatenate([jnp.where(lo, v_a, 0.0), jnp.where(lo, 0.0, v_b)], axis=0).astype(BF))


def _stack_pairs(ref, g, rows=slice(None)):
    return jnp.concatenate([ref[rows, (4 * g + p) * 128:(4 * g + p + 1) * 128] for p in range(PAIRS)], axis=0)


def _sink_row(sk_ref, g, which):
    return jnp.concatenate([jnp.full((1, 128), sk_ref[0, 8 * g + 2 * p + which], F32) for p in range(PAIRS)], axis=1)


def _fold(full, cur):
    return jnp.where(cur, full[128:], full[:128])


def _attn_probs(raw, fold, sink):
    s = _fold(raw, fold[0]) * (ATTN_HEAD_DIM ** -0.5) + fold[3]
    m = jnp.maximum(jnp.max(s, axis=0, keepdims=True), sink)
    e = jnp.exp(s - m)
    es = jnp.exp(sink - m)
    inv = 1.0 / (jnp.sum(e, axis=0, keepdims=True) + es)
    return e * inv, es * inv


def _unfold(p, fold):
    return jnp.concatenate([p * fold[2], p * fold[1]], axis=0).astype(BF)


def _attn_fwd(aq, ak, av, sinks):
    S = aq.shape[0]
    N = S // SEQ_CHUNK
    nb = ATTN_FWD_BLOCKS_PER_STEP

    def body(sk_ref, q_ref, kp_ref, kc_ref, vp_ref, vc_ref, o_ref):
        kall, kall_sw = _kv_halves(kp_ref, kc_ref)
        vall, vall_sw = _kv_halves(vp_ref, vc_ref)
        blocks = []
        for sb in range(nb):
            kv = slice(sb * 128, sb * 128 + 256)
            rows = slice(sb * 128, (sb + 1) * 128)
            ops = [_group_operands(g, kall[kv], kall_sw[kv], vall[kv], vall_sw[kv]) for g in range(2)]
            raw = [_dot_nt(ops[g][0], _stack_pairs(q_ref, g, rows)) for g in range(2)]
            blocks.append((rows, ops, raw, _attn_fold(nb * pl.program_id(0) + sb)))
        for rows, ops, raw, fold in blocks:
            for g in range(2):
                probs = [_unfold(_attn_probs(raw[g][w * 256:(w + 1) * 256], fold, _sink_row(sk_ref, g, w))[0], fold)
                         for w in range(2)]
                o = _dot_tn(jnp.concatenate(probs, axis=0), ops[g][1])
                for p in range(PAIRS):
                    o_ref[rows, (4 * g + p) * 128:(4 * g + p + 1) * 128] = o[p * 128:(p + 1) * 128].astype(BF)

    prev = lambda i: (jnp.maximum(nb * i - 1, 0), 0)
    return pl.pallas_call(
        body, name="attn_fwd", grid=(N // nb,),
        in_specs=[_SMEM, _rows(nb * 128, 1024), pl.BlockSpec((128, 128), prev), _rows(nb * 128, 128),
                  pl.BlockSpec((128, 128), prev), _rows(nb * 128, 128)],
        out_specs=_rows(nb * 128, 1024),
        out_shape=jax.ShapeDtypeStruct((S, 1024), BF),
        compiler_params=_params(("parallel",)),
    )(sinks, aq, ak, ak, av, av)


def _group_norm(y):
    mu = jnp.mean(y, axis=-1, keepdims=True)
    yc = y - mu
    rstd = lax.rsqrt(jnp.mean(yc * yc, axis=-1, keepdims=True) + EPS)
    return yc * rstd, rstd


def _merge_fwd(x, ry, rg, ay, ga, gb, g_ret, w_ret, w_attn, w_out):
    S = x.shape[0]
    tm = 512

    def body(x_ref, ry_ref, rg_ref, ay_ref, ga_ref, gb_ref, gr_ref, wr_ref, wa_ref, wo_ref,
             u_ref, mg_ref, x1_ref):
        bb = _dot(ay_ref[...], wa_ref[...])
        for hd in range(RET_HEADS):
            vs = slice(hd * 256, (hd + 1) * 256)
            yn, _ = _group_norm(ry_ref[:, vs].astype(F32))
            rgv = rg_ref[:, vs].astype(F32)
            u_ref[:, vs] = (rgv * _sigmoid(rgv) * (yn * gr_ref[:, vs])).astype(BF)
        gated_b = _sigmoid(gb_ref[...].astype(F32)) * bb
        sa = _sigmoid(ga_ref[...].astype(F32))
        ba = _dot(u_ref[...], wr_ref[...])
        merged = (sa * ba + gated_b).astype(BF)
        mg_ref[...] = merged
        x1_ref[...] = x_ref[...] + _dot(merged, wo_ref[...])

    outs = [BF, BF, F32]
    return pl.pallas_call(
        body, name="merge_fwd", grid=(S // tm,),
        in_specs=[_rows(tm, 1024)] * 6 + [_fixed((1, 1024)), _VMEM, _VMEM, _VMEM],
        out_specs=[_rows(tm, 1024)] * 3,
        out_shape=[jax.ShapeDtypeStruct((S, 1024), dt) for dt in outs],
        compiler_params=_params(("parallel",)),
    )(x, ry, rg, ay, ga, gb, g_ret, w_ret, w_attn, w_out)


def _ffn_fwd(x1, tgt, g2, g3, w_gate, w_up, w_down):
    S = x1.shape[0]
    tm = 256

    def body(x1_ref, t_ref, g2_ref, g3_ref, wg_ref, wu_ref, wd_ref,
             gate_ref, up_ref, h2_ref, act_ref, dx2_ref, loss_ref, dg3_ref):
        @pl.when(pl.program_id(0) == 0)
        def _():
            loss_ref[...] = jnp.zeros_like(loss_ref)
            dg3_ref[...] = jnp.zeros_like(dg3_ref)

        x1v = x1_ref[...]
        r2 = lax.rsqrt(jnp.mean(x1v * x1v, axis=-1, keepdims=True) + EPS)
        h2 = (x1v * r2 * g2_ref[...]).astype(BF)
        h2_ref[...] = h2
        gate = _dot_nt(h2, wg_ref[...])
        up = _dot_nt(h2, wu_ref[...])
        gate_ref[...] = gate
        up_ref[...] = up
        act = (gate * _sigmoid(gate) * up).astype(BF)
        act_ref[...] = act
        x2 = x1v + _dot(act, wd_ref[...])
        r3 = lax.rsqrt(jnp.mean(x2 * x2, axis=-1, keepdims=True) + EPS)
        xh = x2 * r3
        g3v = g3_ref[...]
        err = xh * g3v - t_ref[...]
        loss_ref[...] += 0.5 * jnp.sum(jnp.mean(err * err, axis=-1, keepdims=True), axis=0, keepdims=True)
        dy = err * (1.0 / D_MODEL)
        dg3_ref[...] += jnp.sum(dy * xh, axis=0, keepdims=True)
        dxh = dy * g3v
        dx2_ref[...] = r3 * (dxh - xh * jnp.mean(dxh * xh, axis=-1, keepdims=True))

    return pl.pallas_call(
        body, name="ffn_fwd", grid=(S // tm,),
        in_specs=[_rows(tm, 1024), _rows(tm, 1024), _fixed((1, 1024)), _fixed((1, 1024)), _VMEM, _VMEM, _VMEM],
        out_specs=[_rows(tm, D_FF), _rows(tm, D_FF), _rows(tm, 1024), _rows(tm, D_FF), _rows(tm, 1024),
                   _fixed((1, LANES)), _fixed((1, 1024))],
        out_shape=[jax.ShapeDtypeStruct((S, D_FF), F32), jax.ShapeDtypeStruct((S, D_FF), F32),
                   jax.ShapeDtypeStruct((S, 1024), BF), jax.ShapeDtypeStruct((S, D_FF), BF),
                   jax.ShapeDtypeStruct((S, 1024), F32),
                   jax.ShapeDtypeStruct((1, LANES), F32), jax.ShapeDtypeStruct((1, 1024), F32)],
        compiler_params=_params(("arbitrary",)),
    )(x1, tgt, g2, g3, w_gate, w_up, w_down)


def _ffn_bwd(dx2, gate, up, x1, g2, w_gate, w_up, w_down):
    S = x1.shape[0]
    tm = 256

    def body(dx2_ref, gate_ref, up_ref, x1_ref, g2_ref, wg_ref, wu_ref, wd_ref,
             dgate_ref, dup_ref, dx1_ref, dg2_ref):
        @pl.when(pl.program_id(0) == 0)
        def _():
            dg2_ref[...] = jnp.zeros_like(dg2_ref)

        dx2v = dx2_ref[...]
        dact = _dot_nt(dx2v.astype(BF), wd_ref[...])
        gate = gate_ref[...]
        sg = _sigmoid(gate)
        dgate = (dact * up_ref[...] * (sg * (1.0 + gate * (1.0 - sg)))).astype(BF)
        dup = (dact * (gate * sg)).astype(BF)
        dgate_ref[...] = dgate
        dup_ref[...] = dup
        dh2 = _dot(dgate, wg_ref[...]) + _dot(dup, wu_ref[...])
        x1v = x1_ref[...]
        r2 = lax.rsqrt(jnp.mean(x1v * x1v, axis=-1, keepdims=True) + EPS)
        xh = x1v * r2
        dg2_ref[...] += jnp.sum(dh2 * xh, axis=0, keepdims=True)
        dxh = dh2 * g2_ref[...]
        dx1_ref[...] = dx2v + r2 * (dxh - xh * jnp.mean(dxh * xh, axis=-1, keepdims=True))

    return pl.pallas_call(
        body, name="ffn_bwd", grid=(S // tm,),
        in_specs=[_rows(tm, 1024), _rows(tm, D_FF), _rows(tm, D_FF), _rows(tm, 1024), _fixed((1, 1024)),
                  _VMEM, _VMEM, _VMEM],
        out_specs=[_rows(tm, D_FF), _rows(tm, D_FF), _rows(tm, 1024), _fixed((1, 1024))],
        out_shape=[jax.ShapeDtypeStruct((S, D_FF), BF), jax.ShapeDtypeStruct((S, D_FF), BF),
                   jax.ShapeDtypeStruct((S, 1024), F32), jax.ShapeDtypeStruct((1, 1024), F32)],
        compiler_params=_params(("arbitrary",)),
    )(dx2, gate, up, x1, g2, w_gate, w_up, w_down)


def _merge_bwd(dx1, ga, gb, u, ay, rg, ry, g_ret, w_ret, w_attn, w_out):
    S = dx1.shape[0]
    tm = 512

    def body(dx1_ref, ga_ref, gb_ref, u_ref, ay_ref, rg_ref, ry_ref, gr_ref, wr_ref, wa_ref, wo_ref,
             dgates_ref, drg_ref, dba_ref, dbb_ref, day_ref, dry_ref, dgr_ref):
        @pl.when(pl.program_id(0) == 0)
        def _():
            dgr_ref[...] = jnp.zeros_like(dgr_ref)

        dm = _dot_nt(dx1_ref[...].astype(BF), wo_ref[...])
        ba = _dot(u_ref[...], wr_ref[...])
        bb = _dot(ay_ref[...], wa_ref[...])
        sa = _sigmoid(ga_ref[...].astype(F32))
        sb = _sigmoid(gb_ref[...].astype(F32))
        dga = (dm * ba * (sa * (1.0 - sa))).astype(BF)
        dgb = (dm * bb * (sb * (1.0 - sb))).astype(BF)
        dgates_ref[:, :1024] = dga
        dgates_ref[:, 1024:] = dgb
        dba = (dm * sa).astype(BF)
        dbb = (dm * sb).astype(BF)
        dba_ref[...] = dba
        dbb_ref[...] = dbb
        day_ref[...] = _dot_nt(dbb, wa_ref[...]).astype(BF)
        du = _dot_nt(dba, wr_ref[...])
        for hd in range(RET_HEADS):
            vs = slice(hd * 256, (hd + 1) * 256)
            yn, rstd = _group_norm(ry_ref[:, vs].astype(F32))
            rgv = rg_ref[:, vs].astype(F32)
            sr = _sigmoid(rgv)
            silu = rgv * sr
            duv = du[:, vs]
            grv = gr_ref[:, vs]
            drg = (duv * (yn * grv) * (sr * (1.0 + rgv * (1.0 - sr)))).astype(BF)
            drg_ref[:, vs] = drg
            dyg = duv * silu
            dgr_ref[:, vs] += jnp.sum(dyg * yn, axis=0, keepdims=True)
            dyn = dyg * grv
            dry_ref[:, vs] = (rstd * (dyn - jnp.mean(dyn, axis=-1, keepdims=True)
                                      - yn * jnp.mean(dyn * yn, axis=-1, keepdims=True))).astype(KEPT)

    return pl.pallas_call(
        body, name="merge_bwd", grid=(S // tm,),
        in_specs=[_rows(tm, 1024)] * 7 + [_fixed((1, 1024)), _VMEM, _VMEM, _VMEM],
        out_specs=[_rows(tm, 2048), _rows(tm, 1024), _rows(tm, 1024), _rows(tm, 1024), _rows(tm, 1024),
                   _rows(tm, 1024), _fixed((1, 1024))],
        out_shape=[jax.ShapeDtypeStruct((S, 2048), BF), jax.ShapeDtypeStruct((S, 1024), BF),
                   jax.ShapeDtypeStruct((S, 1024), BF), jax.ShapeDtypeStruct((S, 1024), BF),
                   jax.ShapeDtypeStruct((S, 1024), BF), jax.ShapeDtypeStruct((S, 1024), KEPT),
                   jax.ShapeDtypeStruct((1, 1024), F32)],
        compiler_params=_params(("arbitrary",)),
    )(dx1, ga, gb, u, ay, rg, ry, g_ret, w_ret, w_attn, w_out)


def _attn_bwd(aq, ak, av, day, sinks, ca, sa, after):
    S = aq.shape[0]
    N = S // SEQ_CHUNK
    steps = N // 2

    def body(sk_ref, q_ref, kp_ref, kc_ref, vp_ref, vc_ref, do_ref, cq_ref, sq_ref, cp_ref, sp_ref, after_ref,
             dq_ref, dke_ref, dve_ref, dko_ref, dvo_ref, dsk_ref, carry_k, carry_v):
        i = pl.program_id(0)

        @pl.when(i == 0)
        def _():
            carry_k[...] = jnp.zeros_like(carry_k)
            carry_v[...] = jnp.zeros_like(carry_v)
            dsk_ref[...] = jnp.zeros_like(dsk_ref)

        @pl.when(i < steps)
        def _():
            kall, kall_sw = _kv_halves(kp_ref, kc_ref)
            vall, vall_sw = _kv_halves(vp_ref, vc_ref)
            lo = lax.broadcasted_iota(jnp.int32, (256, 128), 1) < 64
            blocks = []
            for sb in range(2):
                kv = slice(sb * 128, sb * 128 + 256)
                rows = slice(sb * 128, (sb + 1) * 128)
                ops = [_group_operands(g, kall[kv], kall_sw[kv], vall[kv], vall_sw[kv]) for g in range(2)]
                stacked = [(_stack_pairs(q_ref, g, rows), _stack_pairs(do_ref, g, rows)) for g in range(2)]
                raw = [_dot_nt(ops[g][0], stacked[g][0]) for g in range(2)]
                raw_dp = [_dot_nt(ops[g][1], stacked[g][1]) for g in range(2)]
                blocks.append((rows, ops, stacked, raw, raw_dp, _attn_fold(2 * i + sb)))
            grads = []
            d_sink = [jnp.zeros((1, PAIRS * 128), F32) for _ in range(4)]
            for rows, ops, stacked, raw, raw_dp, fold in blocks:
                cur = fold[0]
                cq, sq = cq_ref[rows, :], sq_ref[rows, :]
                dk_groups, dv_groups = [], []
                for g in range(2):
                    qs, dos = stacked[g]
                    ds_parts, p_parts = [], []
                    for which in range(2):
                        half = slice(which * 256, (which + 1) * 256)
                        pr, ps = _attn_probs(raw[g][half], fold, _sink_row(sk_ref, g, which))
                        dp = _fold(raw_dp[g][half], cur)
                        dr = jnp.sum(pr * dp, axis=0, keepdims=True)
                        ds_parts.append(_unfold(pr * (dp - dr) * (ATTN_HEAD_DIM ** -0.5), fold))
                        p_parts.append(_unfold(pr, fold))
                        d_sink[2 * g + which] = d_sink[2 * g + which] - ps * dr
                    ds = jnp.concatenate(ds_parts, axis=0)
                    dq = _dot_tn(ds, ops[g][0])
                    dk_both = _dot(ds, qs)
                    dv_both = _dot(jnp.concatenate(p_parts, axis=0), dos)
                    for p in range(PAIRS):
                        sl = slice((4 * g + p) * 128, (4 * g + p + 1) * 128)
                        dq_ref[rows, sl] = _rot_attn_bwd(dq[p * 128:(p + 1) * 128], cq, sq).astype(BF)
                    tk = jnp.where(lo, dk_both[:256], dk_both[256:])
                    tv = jnp.where(lo, dv_both[:256], dv_both[256:])
                    dk_groups.append(tk + pltpu.roll(tk, 64, 1))
                    dv_groups.append(tv + pltpu.roll(tv, 64, 1))
                grads.append((jnp.where(lo, dk_groups[0], dk_groups[1]), jnp.where(lo, dv_groups[0], dv_groups[1])))
            dsk_ref[...] += jnp.concatenate(d_sink, axis=0)
            (dk_a, dv_a), (dk_b, dv_b) = grads
            dko_ref[...] = _rot_attn_bwd(carry_k[...] + dk_a[:128], cp_ref[...], sp_ref[...]).astype(BF)
            dvo_ref[...] = (carry_v[...] + dv_a[:128]).astype(BF)
            dke_ref[...] = _rot_attn_bwd(dk_a[128:] + dk_b[:128], cq_ref[0:128, :], sq_ref[0:128, :]).astype(BF)
            dve_ref[...] = (dv_a[128:] + dv_b[:128]).astype(BF)
            carry_k[...] = dk_b[128:]
            carry_v[...] = dv_b[128:]

        @pl.when(i == steps)
        def _():
            dko_ref[...] = _rot_attn_bwd(carry_k[...], cp_ref[...], sp_ref[...]).astype(BF)
            dvo_ref[...] = carry_v[...].astype(BF)

    two = lambda i: (jnp.minimum(i, steps - 1), 0)
    before = lambda i: (jnp.maximum(2 * i - 1, 0), 0)
    odd = lambda i: (jnp.maximum(i - 1, 0), 0)
    blk = lambda r, w, im: pl.BlockSpec((r, w), im)
    half = jax.ShapeDtypeStruct((S // 2, 128), BF)
    dq, dk_even, dv_even, dk_odd, dv_odd, dsk = pl.pallas_call(
        body, name="attn_bwd", grid=(steps + 1,),
        in_specs=[_SMEM, blk(256, 1024, two), blk(128, 128, before), blk(256, 128, two), blk(128, 128, before),
                  blk(256, 128, two), blk(256, 1024, two), blk(256, 128, two), blk(256, 128, two),
                  blk(128, 128, before), blk(128, 128, before), _ANY],
        out_specs=[blk(256, 1024, two), blk(128, 128, two), blk(128, 128, two), blk(128, 128, odd),
                   blk(128, 128, odd), _fixed((4, PAIRS * 128))],
        out_shape=[jax.ShapeDtypeStruct((S, 1024), BF), half, half, half, half,
                   jax.ShapeDtypeStruct((4, PAIRS * 128), F32)],
        scratch_shapes=[pltpu.VMEM((128, 128), F32), pltpu.VMEM((128, 128), F32)],
        compiler_params=_params(("arbitrary",)),
    )(sinks, aq, ak, ak, av, av, day, ca, sa, ca, sa, after)

    def interleave(even, odd_):
        return jnp.stack([even.reshape(steps, 128, 128), odd_.reshape(steps, 128, 128)], axis=1).reshape(S, 128)

    return dq, interleave(dk_even, dk_odd), interleave(dv_even, dv_odd), dsk


def _ret_bwd(rq, rk, rv, dry, states, dmat, qd, kd, cd, cr, sr, after):
    S = rq.shape[0]
    N = S // RET_CHUNK
    scale = RET_QK_DIM ** -0.5
    cps = RET_CHUNKS_PER_STEP
    steps = N // cps

    def body(q_ref, k_ref, v_ref, dy_ref, st_ref, dm_ref, qd_ref, kd_ref, cd_ref, cr_ref, sr_ref, after_ref,
             out_ref, dstate):
        @pl.when(pl.program_id(0) == 0)
        def _():
            dstate[...] = jnp.zeros_like(dstate)

        for hd in range(RET_HEADS):
            qs = slice(hd * 128, (hd + 1) * 128)
            vs = slice(hd * 256, (hd + 1) * 256)
            dmv = dm_ref[hd]
            free = []
            for ch in range(cps):
                rs = slice(ch * RET_CHUNK, (ch + 1) * RET_CHUNK)
                crv, srv = cr_ref[rs, :], sr_ref[rs, :]
                q = q_ref[rs, qs]
                kf = k_ref[rs, qs]
                k = kf.astype(BF)
                v = v_ref[rs, vs]
                do = dy_ref[rs, vs].astype(F32)
                dob = do.astype(BF)
                doq = (do * qd_ref[hd]).astype(BF)
                a = (_dot_nt(q, k) * dmv).astype(BF)
                da = (_dot_nt(dob, v) * dmv).astype(BF)
                dq = _dot(da, k) + _dot_nt(doq, st_ref[ch, hd])
                out_ref[rs, qs] = _rot_ret_bwd(dq * scale, crv, srv).astype(BF)
                free.append((rs, crv, srv, v, (kf * kd_ref[hd]).astype(BF),
                             _dot_tn(a, dob), _dot_tn(da, q), _dot_tn(q, doq)))
            ds = dstate[hd]
            for rs, crv, srv, v, kdv, dv_free, dk_free, ds_update in reversed(free):
                dsb = ds.astype(BF)
                dv = dv_free + _dot(kdv, dsb)
                dk = dk_free + _dot_nt(v, dsb) * kd_ref[hd]
                ds = ds * cd_ref[hd] + ds_update
                out_ref[rs, 512 + hd * 128:512 + (hd + 1) * 128] = _rot_ret_bwd(dk, crv, srv).astype(BF)
                out_ref[rs, 1024 + hd * 256:1024 + (hd + 1) * 256] = dv.astype(BF)
            dstate[hd] = ds

    rev = lambda w: pl.BlockSpec((cps * RET_CHUNK, w), lambda i: (steps - 1 - i, 0))
    return pl.pallas_call(
        body, name="ret_bwd", grid=(steps,),
        in_specs=[rev(512), rev(512), rev(1024), rev(1024),
                  pl.BlockSpec((cps, RET_HEADS, 128, 256), lambda i: (steps - 1 - i, 0, 0, 0)),
                  _VMEM, _VMEM, _VMEM, _VMEM, rev(128), rev(128), _ANY],
        out_specs=rev(2048),
        out_shape=jax.ShapeDtypeStruct((S, 2048), BF),
        scratch_shapes=[pltpu.VMEM((RET_HEADS, 128, 256), F32)],
        compiler_params=_params(("arbitrary",)),
    )(rq, rk, rv, dry, states, dmat, qd, kd, cd, cr, sr, after)


def _proj_bwd(d_pieces, x, dx1, g1, w_in_t):
    S = x.shape[0]
    tm = 512
    widths = [d.shape[1] for d in d_pieces]
    assert sum(widths) == D_IN
    nd = len(d_pieces)

    def body(*refs):
        d_refs = refs[:nd]
        x_ref, dx1_ref, g_ref, w_ref, gx_ref, dg_ref, db_ref = refs[nd:]

        @pl.when(pl.program_id(0) == 0)
        def _():
            dg_ref[...] = jnp.zeros_like(dg_ref)
            db_ref[...] = jnp.zeros_like(db_ref)

        dh = jnp.zeros((tm, D_MODEL), F32)
        lo = 0
        for ref, n in zip(d_refs, widths):
            v = ref[...]
            dh = dh + _dot(v, w_ref[lo:lo + n, :])
            db_ref[:, lo:lo + n] += jnp.sum(v.astype(F32), axis=0, keepdims=True)
            lo += n
        xv = x_ref[...]
        r = lax.rsqrt(jnp.mean(xv * xv, axis=-1, keepdims=True) + EPS)
        xh = xv * r
        dg_ref[...] += jnp.sum(dh * xh, axis=0, keepdims=True)
        dxh = dh * g_ref[...]
        gx_ref[...] = dx1_ref[...] + r * (dxh - xh * jnp.mean(dxh * xh, axis=-1, keepdims=True))

    return pl.pallas_call(
        body, name="proj_bwd", grid=(S // tm,),
        in_specs=[_rows(tm, n) for n in widths] + [_rows(tm, 1024), _rows(tm, 1024), _fixed((1, 1024)), _VMEM],
        out_specs=[_rows(tm, 1024), _fixed((1, 1024)), _fixed((1, D_IN))],
        out_shape=[jax.ShapeDtypeStruct((S, 1024), F32), jax.ShapeDtypeStruct((1, 1024), F32),
                   jax.ShapeDtypeStruct((1, D_IN), F32)],
        compiler_params=_params(("arbitrary",)),
    )(*d_pieces, x, dx1, g1, w_in_t)


SHARD_ROWS = (1600, 256, 256, 256, 704, 704, 704)


class Group(NamedTuple):
    name: str
    tensors: tuple
    offs: tuple
    block: int
    tile: int

    @property
    def half(self):
        return self.block // 2

    def off(self, t):
        return self.offs[self.tensors.index(t)]


LATE = Group("late", (0,), (0,), 1600, 800)
EARLY = Group("early", (1, 2, 3, 4, 5, 6), (0, 256, 512, 768, 1472, 2176), 2880, 480)
GROUP_OF = {t: g for g in (LATE, EARLY) for t in g.tensors}


def _wgrad(a_list, b, t, packed, name, row0=0):
    S = b.shape[0]
    widths = [a.shape[1] for a in a_list]
    M = sum(widths)
    ts = 1024
    steps = S // ts
    na = len(a_list)
    grp = GROUP_OF[t]
    r, off = SHARD_ROWS[t], grp.off(t)
    segments = []
    for chip in range(N_CHIPS):
        lo, hi = max(chip * r, row0), min((chip + 1) * r, row0 + M)
        if lo < hi:
            segments.append((lo - row0, hi - lo, chip * grp.block + off + lo - chip * r))
    assert sum(n for _, n, _ in segments) == M
    stage_rows = max(n for _, n, _ in segments)
    n_in = na + 1 + (2 if packed is not None else 0)

    def body(*refs):
        a_refs, b_ref = refs[:na], refs[na]
        g32_ref, gb_ref, acc, stage, sems = refs[n_in:]
        i = pl.program_id(0)

        @pl.when(i == 0)
        def _():
            acc[...] = jnp.zeros_like(acc)

        bv = b_ref[...].astype(BF)
        lo = 0
        for a_ref, n in zip(a_refs, widths):
            acc[lo:lo + n, :] += _dot_tn(a_ref[...].astype(BF), bv)
            lo += n

        @pl.when(i == steps - 1)
        def _():
            for a0, n, p0 in segments:
                c32 = pltpu.make_async_copy(acc.at[a0:a0 + n, :], g32_ref.at[pl.ds(p0, n), :], sems.at[0])
                c32.start()
                stage[0:n, :] = acc[a0:a0 + n, :].astype(BF)
                c16 = pltpu.make_async_copy(stage.at[0:n, :], gb_ref.at[pl.ds(p0, n), :], sems.at[1])
                c16.start()
                c32.wait()
                c16.wait()

    shapes = [jax.ShapeDtypeStruct((N_CHIPS * grp.block, 1024), F32),
              jax.ShapeDtypeStruct((N_CHIPS * grp.block, 1024), BF)]
    operands = list(a_list) + [b]
    in_specs = [_rows(ts, n) for n in widths] + [_rows(ts, 1024)]
    aliases = {}
    if packed is not None:
        operands += list(packed)
        in_specs += [_ANY, _ANY]
        aliases = {na + 1: 0, na + 2: 1}
    return pl.pallas_call(
        body, name=name, grid=(steps,),
        in_specs=in_specs, out_specs=[_ANY, _ANY], out_shape=shapes,
        scratch_shapes=[pltpu.VMEM((M, 1024), F32), pltpu.VMEM((stage_rows, 1024), BF),
                        pltpu.SemaphoreType.DMA((2,))],
        input_output_aliases=aliases,
        compiler_params=_params(("arbitrary",)),
    )(*operands)


def _stage_weight(place, w, name, after=()):
    r = w.shape[0]
    tr = 800 if r == 1600 else r
    nb = r // tr

    def body(p_ref, w_ref, *rest):
        rest[-1][...] = w_ref[...].astype(BF)

    return pl.pallas_call(
        body, name=name,
        grid_spec=pltpu.PrefetchScalarGridSpec(
            num_scalar_prefetch=1, grid=(nb,),
            in_specs=[pl.BlockSpec((tr, 1024), lambda i, p: (i, 0))] + [_ANY] * len(after),
            out_specs=pl.BlockSpec((tr, 1024), lambda i, p: (p[0] * nb + i, 0))),
        out_shape=jax.ShapeDtypeStruct((N_CHIPS * r, 1024), BF),
        compiler_params=_params(("arbitrary",)),
    )(place, w, *after)


def _kept_block(grp, nb):
    return lambda i, p: ((2 * (i // nb) + p[1]) * nb + i % nb, 0)


def _pair_sum(place, g32, got, grp):
    nb = grp.half // grp.tile

    def body(p_ref, g_ref, r_ref, s16_ref):
        s16_ref[...] = (g_ref[...] + r_ref[...].astype(F32)).astype(BF)

    flat = pl.BlockSpec((grp.tile, 1024), lambda i, p: (i, 0))
    return pl.pallas_call(
        body, name="pair_sum_" + grp.name,
        grid_spec=pltpu.PrefetchScalarGridSpec(
            num_scalar_prefetch=1, grid=(N_CHIPS * nb,),
            in_specs=[pl.BlockSpec((grp.tile, 1024), _kept_block(grp, nb)), flat], out_specs=flat),
        out_shape=jax.ShapeDtypeStruct((N_CHIPS * grp.half, 1024), BF),
        compiler_params=_params(("arbitrary",)),
    )(place, g32, got)


def _chip_sum(place, g32, got, others, grp):
    nb = grp.half // grp.tile

    def body(p_ref, g_ref, r_ref, r0_ref, r1_ref, r2_ref, o_ref):
        own = g_ref[...] + r_ref[...].astype(F32)
        o_ref[...] = ((own + r0_ref[...].astype(F32)) + r1_ref[...].astype(F32)) + r2_ref[...].astype(F32)

    blk = lambda f: pl.BlockSpec((grp.tile, 1024), f)
    return pl.pallas_call(
        body, name="chip_sum_" + grp.name,
        grid_spec=pltpu.PrefetchScalarGridSpec(
            num_scalar_prefetch=1, grid=(nb,),
            in_specs=[blk(lambda i, p: ((2 * p[0] + p[1]) * nb + i, 0)), blk(lambda i, p: (p[0] * nb + i, 0)),
                      blk(lambda i, p: (i, 0)), blk(lambda i, p: (nb + i, 0)), blk(lambda i, p: (2 * nb + i, 0))],
            out_specs=blk(lambda i, p: (p[1] * nb + i, 0))),
        out_shape=jax.ShapeDtypeStruct((grp.block, 1024), F32),
        compiler_params=_params(("arbitrary",)),
    )(place, g32, got, others, others, others)


def _adamw(w, g, m, v, name, g_row0=0, after=None):
    r = w.shape[0]
    tr = 320 if r % 320 == 0 and g_row0 % 320 == 0 else (64 if r % 64 == 0 else r)
    assert g_row0 % tr == 0
    g0 = g_row0 // tr
    extra = [] if after is None else [after]

    def body(w_ref, g_ref, m_ref, v_ref, *rest):
        go_ref, d_ref, mo_ref, vo_ref = rest[len(extra):]
        gv = g_ref[...]
        mn = ADAM_B1 * m_ref[...] + (1.0 - ADAM_B1) * gv
        vn = ADAM_B2 * v_ref[...] + (1.0 - ADAM_B2) * (gv * gv)
        m_hat = mn / (1.0 - ADAM_B1 ** ADAM_STEP)
        v_hat = vn / (1.0 - ADAM_B2 ** ADAM_STEP)
        go_ref[...] = gv
        d_ref[...] = -ADAM_LR * (m_hat / (jnp.sqrt(v_hat) + ADAM_EPS) + ADAM_WD * w_ref[...])
        mo_ref[...] = mn
        vo_ref[...] = vn

    return pl.pallas_call(
        body, name=name, grid=(r // tr,),
        in_specs=[_rows(tr, 1024), pl.BlockSpec((tr, 1024), lambda i: (g0 + i, 0)), _rows(tr, 1024), _rows(tr, 1024)]
        + [_ANY] * len(extra),
        out_specs=[_rows(tr, 1024)] * 4,
        out_shape=[jax.ShapeDtypeStruct((r, 1024), F32)] * 4,
        compiler_params=_params(("parallel",)),
    )(w, g, m, v, *extra)


def _place():
    x, y, c = lax.axis_index("x"), lax.axis_index("y"), lax.axis_index("c")
    chips = [(1 - x, y), (x, 1 - y), (1 - x, 1 - y)]
    return x, y, c, chips


_HBM =pl.BlockSpec(memory_space=pltpu.HBM)
_SEM = pl.BlockSpec(memory_space=pltpu.SEMAPHORE)
_DATAFLOW = pltpu.SideEffectType.DATAFLOW_SIDE_EFFECTING


def _exchange(name, arrays, n, plan):
    na = len(arrays)

    def body(*refs):
        pairs = plan(refs[na:2 * na], refs[2 * na], refs[2 * na + 1])
        for send, _ in pairs:
            send.start()
        for send, recv in pairs:
            send.wait_send()
            recv.wait_recv()

    return pl.pallas_call(
        body, name=name, in_specs=[_ANY] * na, out_specs=[_ANY] * na,
        out_shape=[jax.ShapeDtypeStruct(a.shape, a.dtype) for a in arrays],
        scratch_shapes=[pltpu.SemaphoreType.DMA((n,)), pltpu.SemaphoreType.DMA((n,))],
        input_output_aliases={i: i for i in range(na)},
    )(*arrays)


def _exchange_start(name, arrays, n, plan, after=()):
    na = len(arrays)

    def body(*refs):
        for send, _ in plan(refs[:na], refs[na + len(after)], refs[na + len(after) + 1]):
            send.start()
        refs[-1][...] = jnp.zeros_like(refs[-1])

    out = pl.pallas_call(
        body, name=name,
        out_shape=(pltpu.SemaphoreType.DMA((n,)), pltpu.SemaphoreType.DMA((n,)),
                   *[pltpu.HBM(a.shape, a.dtype) for a in arrays], jax.ShapeDtypeStruct((8, LANES), F32)),
        in_specs=[_HBM] * na + [_ANY] * len(after), out_specs=(_SEM, _SEM, *[_HBM] * na, _VMEM),
        input_output_aliases={i: 2 + i for i in range(na)},
        compiler_params=pltpu.CompilerParams(has_side_effects=_DATAFLOW),
    )(*[pltpu.with_memory_space_constraint(a, pltpu.HBM) for a in arrays], *after)
    return (out[0], out[1]), list(out[2:2 + na]), out[-1]


def _exchange_wait(name, sems, in_flight, plan, after=()):
    na = len(in_flight)

    def body(*refs):
        for send, recv in plan(refs[:na], refs[na], refs[na + 1]):
            send.wait_send()
            recv.wait_recv()

    out = pl.pallas_call(
        body, name=name,
        out_shape=tuple(pltpu.HBM(a.shape, a.dtype) for a in in_flight),
        in_specs=[_HBM] * na + [_SEM, _SEM] + [_ANY] * len(after), out_specs=tuple([_HBM] * na),
        input_output_aliases={i: i for i in range(na)},
        compiler_params=pltpu.CompilerParams(has_side_effects=_DATAFLOW),
    )(*in_flight, *sems, *after)
    return list(out)


def _remote(src, dst, send_sems, recv_sems, k, to):
    return pltpu.make_async_remote_copy(src_ref=src, dst_ref=dst, send_sem=send_sems.at[k], recv_sem=recv_sems.at[k],
                                        device_id=to, device_id_type=MESH)


def _half_rows(ref, tid, pj, half):
    h = SHARD_ROWS[tid] // 2
    return ref.at[pl.ds((2 * pj + half) * h, h), :]


def _plan_gather_halves(tids):
    def plan(refs, ss, rs):
        x, y, c, chips = _place()
        pairs = []
        for t, tid in enumerate(tids):
            mine = _half_rows(refs[t], tid, 2 * x + y, c)
            for k, (cx, cy) in enumerate(chips):
                theirs = _half_rows(refs[t], tid, 2 * cx + cy, c)
                pairs.append((_remote(mine, mine, ss, rs, 3 * t + k, (cx, cy, c)),
                              _remote(theirs, theirs, ss, rs, 3 * t + k, (x, y, c))))
        return pairs
    return plan


def _plan_forward_halves(tids):
    def plan(refs, ss, rs):
        x, y, c, chips = _place()
        pairs = []
        for t, tid in enumerate(tids):
            for k, (cx, cy) in enumerate(chips):
                got = _half_rows(refs[t], tid, 2 * cx + cy, c)
                missing = _half_rows(refs[t], tid, 2 * cx + cy, 1 - c)
                pairs.append((_remote(got, got, ss, rs, 3 * t + k, (x, y, 1 - c)),
                              _remote(missing, missing, ss, rs, 3 * t + k, (x, y, 1 - c))))
        return pairs
    return plan


def _plan_pair_exchange(grp):
    def plan(refs, ss, rs):
        x, y, c, _ = _place()
        cps = [_remote(refs[0].at[pl.ds(pj * grp.block + (1 - c) * grp.half, grp.half), :],
                       refs[1].at[pl.ds(pj * grp.half, grp.half), :], ss, rs, pj, (x, y, 1 - c))
               for pj in range(N_CHIPS)]
        return [(cp, cp) for cp in cps]
    return plan


def _plan_to_chips(grp):
    def plan(refs, ss, rs):
        x, y, c, chips = _place()
        cps = [_remote(refs[0].at[pl.ds((2 * cx + cy) * grp.half, grp.half), :],
                       refs[1].at[pl.ds(k * grp.half, grp.half), :], ss, rs, k, (cx, cy, c))
               for k, (cx, cy) in enumerate(chips)]
        return [(cp, cp) for cp in cps]
    return plan


def _plan_share_halves(grp):
    def plan(refs, ss, rs):
        x, y, c, _ = _place()
        mine = refs[0].at[pl.ds(c * grp.half, grp.half), :]
        other = refs[0].at[pl.ds((1 - c) * grp.half, grp.half), :]
        return [(_remote(mine, mine, ss, rs, 0, (x, y, 1 - c)), _remote(other, other, ss, rs, 0, (x, y, 1 - c)))]
    return plan


def _empty(rows, dtype):
    return lax.empty((rows, 1024), dtype)


def _all_reduce_small(v, after):
    m_per, n = v.shape
    n_dev = 8

    def body(x_ref, after_ref, out_ref, all_ref, send_sems, recv_sems, local_sem):
        x, y, c, chips = _place()
        me, sibling = (x, y, c), (x, y, 1 - c)

        def rows(px, py, pc):
            return all_ref.at[pl.ds((4 * px + 2 * py + pc) * m_per, m_per), :]

        def copy(k, block, to, src=None):
            return pltpu.make_async_remote_copy(
                src_ref=rows(*block) if src is None else src, dst_ref=rows(*block),
                send_sem=send_sems.at[k], recv_sem=recv_sems.at[k], device_id=to, device_id_type=MESH)

        mine = pltpu.make_async_copy(x_ref, rows(*me), local_sem)
        mine.start()
        first = [copy(0, me, sibling, src=x_ref)]
        first += [copy(1 + k, me, (*chip, c), src=x_ref) for k, chip in enumerate(chips)]
        for cp in first:
            cp.start()
        passed = [copy(4 + k, (*chip, c), sibling) for k, chip in enumerate(chips)]
        for k, chip in enumerate(chips):
            copy(1 + k, (*chip, c), me).wait_recv()
            passed[k].start()
        copy(0, sibling, me).wait_recv()
        for k, chip in enumerate(chips):
            copy(4 + k, (*chip, 1 - c), me).wait_recv()
        for cp in first + passed:
            cp.wait_send()
        mine.wait()
        acc = all_ref[pl.ds(0, m_per), :]
        for d in range(1, n_dev):
            acc = acc + all_ref[pl.ds(d * m_per, m_per), :]
        out_ref[...] = acc

    return pl.pallas_call(
        body, name="all_reduce_small",
        in_specs=[_VMEM, _ANY], out_specs=_VMEM,
        out_shape=jax.ShapeDtypeStruct((m_per, n), F32),
        scratch_shapes=[pltpu.VMEM((n_dev * m_per, n), F32),
                        pltpu.SemaphoreType.DMA((7,)), pltpu.SemaphoreType.DMA((7,)), pltpu.SemaphoreType.DMA],
        compiler_params=pltpu.CompilerParams(has_side_effects=True),
    )(v, after)


def _tables(S, token):
    pos = jnp.arange(S, dtype=jnp.int32).astype(F32) + token[0, 0]

    def cs(half):
        inv_freq = ROPE_THETA ** (-jnp.arange(half, dtype=F32) / half)
        ang = pos[:, None] * inv_freq[None, :]
        return jnp.cos(ang), jnp.sin(ang)

    c64, s64 = cs(64)
    c32, s32 = cs(32)
    cr = jnp.concatenate([c64, c64], axis=1)
    sr = jnp.concatenate([-s64, s64], axis=1)
    ca = jnp.concatenate([c32, c32, c32, c32], axis=1)
    sa = jnp.concatenate([-s32, s32, -s32, s32], axis=1)
    return cr, sr, ca, sa


def _decays():
    H, C = RET_HEADS, RET_CHUNK
    log_gamma = jnp.log1p(-jnp.exp2(-5.0 - jnp.arange(H, dtype=F32)))
    idx = jnp.arange(C, dtype=F32)
    rel = idx[:, None] - idx[None, :]
    dmat = jnp.where(rel[None] >= 0, jnp.exp(log_gamma[:, None, None] * jnp.maximum(rel, 0.0)[None]), 0.0)
    qd = jnp.exp(log_gamma[:, None] * (idx + 1.0))[:, :, None]
    kd = jnp.exp(log_gamma[:, None] * (C - 1.0 - idx))[:, :, None]
    cd = jnp.exp(log_gamma * C)[:, None, None]
    return dmat, qd, kd, cd


def _local_step(place, x, tgt, ln1_g, b_in, ret_norm_g, attn_sinks, ln2_g, lnf_g, w_in, other_weights, tables, decays):
    cr, sr, ca, sa = tables
    dmat, qd, kd, cd = decays
    forward_weights, landed_weights = other_weights

    h, rq, rk, rv, rg, aq, ak, av, ga, gb = _proj_fwd(x, ln1_g, w_in, b_in, cr, sr, ca, sa)
    ry, states = _ret_fwd(rq, rk, rv, dmat, qd, kd, cd)
    token = forward_weights(ry)
    ay = _attn_fwd(aq, ak, av, attn_sinks + token[0, 0])
    w_ret, w_attn, w_out, w_gate, w_up, w_down = landed_weights(ay)
    u, merged, x1 = _merge_fwd(x, ry, rg, ay, ga, gb, ret_norm_g, w_ret, w_attn, w_out)
    gate, up, h2, act, dx2, loss, dg3 = _ffn_fwd(x1, tgt, ln2_g, lnf_g, w_gate, w_up, w_down)

    dgate, dup, dx1, dg2 = _ffn_bwd(dx2, gate, up, x1, ln2_g, w_gate, w_up, w_down)
    early = _wgrad([act], dx2, 6, None, "wgrad_down")
    early = _wgrad([dgate], h2, 4, early, "wgrad_gate")
    early = _wgrad([dup], h2, 5, early, "wgrad_up")
    dgates, drg, dba, dbb, day, dry, dgr = _merge_bwd(dx1, ga, gb, u, ay, rg, ry, ret_norm_g,
                                                                w_ret, w_attn, w_out)
    early = _wgrad([merged], dx1, 3, early, "wgrad_out")
    early = _wgrad([u], dba, 1, early, "wgrad_ret")
    e32, e16 = _wgrad([ay], dbb, 2, early, "wgrad_attn")

    grp = EARLY
    sems, flight, token = _exchange_start("pair_start_early", [e16, _empty(N_CHIPS * grp.half, BF)], N_CHIPS,
                                          _plan_pair_exchange(grp))
    daq, dak, dav, dsk = _attn_bwd(aq, ak, av, day, attn_sinks, ca, sa, token)
    _, got = _exchange_wait("pair_wait_early", sems, flight, _plan_pair_exchange(grp), after=(dsk,))
    s16 = _pair_sum(place, e32, got, grp)
    sems, flight, token = _exchange_start("chips_start_early", [s16, _empty(3 * grp.half, BF)], 3, _plan_to_chips(grp))
    d_ret = _ret_bwd(rq, rk, rv, dry, states, dmat, qd, kd, cd, cr, sr, token)
    d_proj = [d_ret, drg, daq, dak, dav, dgates]
    grad_x, dg1, db = _proj_bwd(d_proj, x, dx1, ln1_g, w_in)
    late = _wgrad(d_proj[:2], h, 0, None, "wgrad_in_a")
    late = _wgrad(d_proj[2:], h, 0, late, "wgrad_in_b", row0=d_ret.shape[1] + drg.shape[1])

    def finish_early(after):
        _, others = _exchange_wait("chips_wait_early", sems, flight, _plan_to_chips(grp), after=after)
        return _chip_sum(place, e32, got, others, grp)

    d_sinks = dsk.reshape(2, 2, PAIRS, 128).sum(-1).transpose(0, 2, 1).reshape(1, ATTN_Q_HEADS)
    small = dict(loss=loss, ln1_g=dg1, b_in=db, ret_norm_g=dgr, attn_sinks=d_sinks,
                 ln2_g=dg2, lnf_g=dg3)
    return grad_x, finish_early, late, small


SMALL_ROWS = 16


def _pack_small(ln1, ret, ln2, lnf, b_in, sinks, loss, name):
    def body(a_ref, b_ref, c_ref, d_ref, bi_ref, sk_ref, ls_ref, o_ref):
        o_ref[...] = jnp.zeros_like(o_ref)
        for row, ref in enumerate((a_ref, b_ref, c_ref, d_ref)):
            o_ref[row:row + 1, :] = ref[...]
        for k in range(6):
            o_ref[4 + k:5 + k, :] = bi_ref[:, k * 1024:(k + 1) * 1024]
        o_ref[10:11, 0:D_IN - 6144] = bi_ref[:, 6144:D_IN]
        o_ref[11:12, 0:ATTN_Q_HEADS] = sk_ref[...]
        o_ref[12:13, 0:LANES] = ls_ref[...]

    return pl.pallas_call(
        body, name=name, in_specs=[_VMEM] * 7, out_specs=_VMEM,
        out_shape=jax.ShapeDtypeStruct((SMALL_ROWS, 1024), F32),
    )(ln1.reshape(1, 1024), ret.reshape(1, 1024), ln2.reshape(1, 1024), lnf.reshape(1, 1024),
      b_in.reshape(1, D_IN), sinks.reshape(1, ATTN_Q_HEADS), loss.reshape(1, LANES))


def _unpack_small(p):
    return dict(ln1_g=p[0:1], ret_norm_g=p[1:2], ln2_g=p[2:3], lnf_g=p[3], b_in=p[4:11].reshape(1, -1)[:, :D_IN],
                attn_sinks=p[11:12, :ATTN_Q_HEADS], loss=p[12, 0])


def kernel(x, ln1_g, w_in, b_in, ret_norm_g, w_ret_out, attn_sinks, w_attn_out, w_out, ln2_g, w_ffn_gate, w_ffn_up, w_ffn_down, lnf_g, loss_target, m_ln1_g, m_w_in, m_b_in, m_ret_norm_g, m_w_ret_out, m_attn_sinks, m_w_attn_out, m_w_out, m_ln2_g, m_w_ffn_gate, m_w_ffn_up, m_w_ffn_down, m_lnf_g, v_ln1_g, v_w_in, v_b_in, v_ret_norm_g, v_w_ret_out, v_attn_sinks, v_w_attn_out, v_w_out, v_ln2_g, v_w_ffn_gate, v_w_ffn_up, v_w_ffn_down, v_lnf_g):
    xi = lax.axis_index("x")
    yi = lax.axis_index("y")
    ci = lax.axis_index("c")
    j = 2 * xi + yi

    place = jnp.stack([j, ci]).astype(jnp.int32)

    names = ["w_in", "w_ret_out", "w_attn_out", "w_out", "w_ffn_gate", "w_ffn_up", "w_ffn_down"]
    transposed = [True, False, False, False, True, True, False]
    view = lambda a, t: a[0].T if t else a[0]
    big_w = [view(a, t) for a, t in zip([w_in, w_ret_out, w_attn_out, w_out, w_ffn_gate, w_ffn_up, w_ffn_down], transposed)]
    big_m = [view(a, t) for a, t in zip([m_w_in, m_w_ret_out, m_w_attn_out, m_w_out, m_w_ffn_gate, m_w_ffn_up, m_w_ffn_down], transposed)]
    big_v = [view(a, t) for a, t in zip([v_w_in, v_w_ret_out, v_w_attn_out, v_w_out, v_w_ffn_gate, v_w_ffn_up, v_w_ffn_down], transposed)]

    staged_in = _stage_weight(place, big_w[0], "stage_" + names[0])
    sems, in_flight, token = _exchange_start("gather_in_start", [staged_in], 3, _plan_gather_halves([0]))
    staged = [_stage_weight(place, w, "stage_" + n, after=(token,)) for n, w in zip(names[1:], big_w[1:])]
    tables = _tables(x.shape[1], token)
    decays = _decays()
    w_in_full = _exchange_wait("gather_in_wait", sems, in_flight, _plan_gather_halves([0]),
                               after=(*staged, *tables, *decays))[0]
    w_in_full = _exchange("gather_in_forward", [w_in_full], 3, _plan_forward_halves([0]))[0]
    rest = list(range(1, 7))
    crossing = _exchange_start("gather_start", staged, 3 * len(rest), _plan_gather_halves(rest), after=(w_in_full,))
    token = crossing[2]
    forwarding = []

    def forward_weights(after):
        arrived = _exchange_wait("gather_wait", crossing[0], crossing[1], _plan_gather_halves(rest), after=(after,))
        forwarding.extend(_exchange_start("forward_start", arrived, 3 * len(rest), _plan_forward_halves(rest)))
        return forwarding[2]

    def landed_weights(after):
        return _exchange_wait("forward_wait", forwarding[0], forwarding[1], _plan_forward_halves(rest), after=(after,))

    other_weights = (forward_weights, landed_weights)

    grad_x, finish_early, (l32, l16), small = _local_step(
        place, x[0], loss_target[0], ln1_g, b_in + token[0, 0], ret_norm_g, attn_sinks, ln2_g, lnf_g.reshape(1, -1),
        w_in_full, other_weights, tables, decays)
    pg = _all_reduce_small(_pack_small(
        small["ln1_g"], small["ret_norm_g"], small["ln2_g"], small["lnf_g"], small["b_in"],
        small["attn_sinks"], small["loss"], "pack_small_grads"), l32)
    early_shard = _exchange("share_early", [finish_early((l32,))], 1, _plan_share_halves(EARLY))[0]
    _, got = _exchange("pair_exchange_late", [l16, _empty(N_CHIPS * LATE.half, BF)], N_CHIPS, _plan_pair_exchange(LATE))
    s16 = _pair_sum(place, l32, got, LATE)
    sems, flight, token = _exchange_start("chips_start_late", [s16, _empty(3 * LATE.half, BF)], 3, _plan_to_chips(LATE),
                                          after=(pg, early_shard))

    upd = {}

    def update(t, shard, after):
        res = _adamw(big_w[t], shard, big_m[t], big_v[t], "adamw_" + names[t], g_row0=GROUP_OF[t].off(t), after=after)
        upd[names[t]] = tuple((r.T if transposed[t] else r)[None] for r in res)
        return res[0]

    done = [update(t, early_shard, token) for t in EARLY.tensors]

    sm = _unpack_small(pg)
    z = jnp.zeros((1, LANES), F32)
    pw = _pack_small(ln1_g, ret_norm_g, ln2_g, lnf_g, b_in, attn_sinks, z, "pack_small_w")
    pm = _pack_small(m_ln1_g, m_ret_norm_g, m_ln2_g, m_lnf_g, m_b_in, m_attn_sinks, z, "pack_small_m")
    pv = _pack_small(v_ln1_g, v_ret_norm_g, v_ln2_g, v_lnf_g, v_b_in, v_attn_sinks, z, "pack_small_v")
    small_res = _adamw(pw, pg, pm, pv, "adamw_small", after=token)
    sd, smn, svn = (_unpack_small(t) for t in small_res[1:])
    for n in ["ln1_g", "b_in", "ret_norm_g", "attn_sinks", "ln2_g", "lnf_g"]:
        upd[n] = (sm[n], sd[n], smn[n], svn[n])

    _, others = _exchange_wait("chips_wait_late", sems, flight, _plan_to_chips(LATE), after=(*done, small_res[1]))
    late_shard = _exchange("share_late", [_chip_sum(place, l32, got, others, LATE)], 1, _plan_share_halves(LATE))[0]
    update(0, late_shard, None)

    order_w = ["ln1_g", "w_in", "b_in", "ret_norm_g", "w_ret_out", "attn_sinks", "w_attn_out", "w_out",
               "ln2_g", "w_ffn_gate", "w_ffn_up", "w_ffn_down", "lnf_g"]
    outs = [sm["loss"], grad_x[None]]
    for k in range(4):
        outs += [upd[n][k] for n in order_w]
    return tuple(outs)
```

```python
from typing import NamedTuple

import jax
import jax.numpy as jnp
from jax import lax
from jax.experimental import pallas as pl
from jax.experimental.pallas import tpu as pltpu

D_MODEL = 1024
SEQ_CHUNK = 128
RET_CHUNK = 256
RET_CHUNKS_PER_STEP = 4
RET_HEADS = 4
RET_QK_DIM = 128
RET_V_DIM = 256
ATTN_Q_HEADS = 16
ATTN_HEAD_DIM = 64
D_FF = 2816
D_IN = 6400
ROPE_THETA = 10000.0
EPS = 1e-6
N_CHIPS = 4

ADAM_LR = 0.001
ADAM_B1 = 0.9
ADAM_B2 = 0.999
ADAM_EPS = 1e-08
ADAM_WD = 0.01
ADAM_STEP = 10

OFF_RQ, OFF_RK, OFF_RV, OFF_RG = 0, 512, 1024, 2048
OFF_AQ, OFF_AK, OFF_AV, OFF_GA, OFF_GB = 3072, 4096, 4224, 4352, 5376

V7X_VMEM_LIMIT = 56 * 1024 * 1024
LANES = 128

BF = jnp.bfloat16
F32 = jnp.float32
KEPT = jnp.bfloat16
MESH = pl.DeviceIdType.MESH

_VMEM = pl.BlockSpec(memory_space=pltpu.VMEM)
_SMEM = pl.BlockSpec(memory_space=pltpu.SMEM)
_ANY = pl.BlockSpec(memory_space=pl.ANY)


def _dot(a, b):
    return jnp.dot(a, b, preferred_element_type=F32)


def _dot_nt(a, b):
    return lax.dot_general(a, b, (((1,), (1,)), ((), ())), preferred_element_type=F32)


def _dot_tn(a, b):
    return lax.dot_general(a, b, (((0,), (0,)), ((), ())), preferred_element_type=F32)


def _sigmoid(x):
    return 0.5 * jnp.tanh(0.5 * x) + 0.5


def _rows(tm, n):
    return pl.BlockSpec((tm, n), lambda i: (i, 0))


def _fixed(shape):
    nd = len(shape)
    return pl.BlockSpec(shape, lambda i: (0,) * nd)


def _params(sem, **kw):
    return pltpu.CompilerParams(dimension_semantics=sem, vmem_limit_bytes=V7X_VMEM_LIMIT, **kw)


def _rot_ret(p, cr, sr):
    return p * cr + pltpu.roll(p, 64, 1) * sr


def _rot_ret_bwd(d, cr, sr):
    return d * cr - pltpu.roll(d, 64, 1) * sr


def _attn_partner(p):
    lane = lax.broadcasted_iota(jnp.int32, p.shape, 1)
    first = (lane % 64) < 32
    return jnp.where(first, pltpu.roll(p, 96, 1), pltpu.roll(p, 32, 1))


def _rot_attn(p, ca, sa):
    return p * ca + _attn_partner(p) * sa


def _rot_attn_bwd(d, ca, sa):
    return d * ca - _attn_partner(d) * sa


def _proj_fwd(x, g1, w_in, b_in, cr, sr, ca, sa):
    S = x.shape[0]
    tm = 512
    scale = RET_QK_DIM ** -0.5

    def body(x_ref, g_ref, w_ref, b_ref, cr_ref, sr_ref, ca_ref, sa_ref,
             h_ref, rq_ref, rk_ref, rv_ref, rg_ref, aq_ref, ak_ref, av_ref, ga_ref, gb_ref):
        xv = x_ref[...]
        r = lax.rsqrt(jnp.mean(xv * xv, axis=-1, keepdims=True) + EPS)
        h = (xv * r * g_ref[...]).astype(BF)
        h_ref[...] = h

        def piece(lo, n):
            return _dot_nt(h, w_ref[lo:lo + n, :]) + b_ref[:, lo:lo + n]

        crv, srv = cr_ref[...], sr_ref[...]
        pq = piece(OFF_RQ, 512)
        pk = piece(OFF_RK, 512)
        for hd in range(RET_HEADS):
            sl = slice(hd * 128, (hd + 1) * 128)
            rq_ref[:, sl] = (_rot_ret(pq[:, sl], crv, srv) * scale).astype(BF)
            rk_ref[:, sl] = _rot_ret(pk[:, sl], crv, srv)
        rv_ref[...] = piece(OFF_RV, 1024).astype(BF)
        rg_ref[...] = piece(OFF_RG, 1024).astype(KEPT)
        cav, sav = ca_ref[...], sa_ref[...]
        pa = piece(OFF_AQ, 1024)
        for blk in range(8):
            sl = slice(blk * 128, (blk + 1) * 128)
            aq_ref[:, sl] = _rot_attn(pa[:, sl], cav, sav).astype(BF)
        pkv = piece(OFF_AK, 256)
        ak_ref[...] = _rot_attn(pkv[:, :128], cav, sav).astype(BF)
        av_ref[...] = pkv[:, 128:].astype(BF)
        ga_ref[...] = piece(OFF_GA, 1024).astype(KEPT)
        gb_ref[...] = piece(OFF_GB, 1024).astype(KEPT)

    outs = [(D_MODEL, BF), (512, BF), (512, F32), (1024, BF), (1024, KEPT),
            (1024, BF), (128, BF), (128, BF), (1024, KEPT), (1024, KEPT)]
    return pl.pallas_call(
        body, name="proj_fwd", grid=(S // tm,),
        in_specs=[_rows(tm, D_MODEL), _fixed((1, D_MODEL)), _VMEM, _VMEM,
                  _rows(tm, 128), _rows(tm, 128), _rows(tm, 128), _rows(tm, 128)],
        out_specs=[_rows(tm, n) for n, _ in outs],
        out_shape=[jax.ShapeDtypeStruct((S, n), dt) for n, dt in outs],
        compiler_params=_params(("parallel",)),
    )(x, g1, w_in, b_in, cr, sr, ca, sa)


def _ret_fwd(rq, rk, rv, dmat, qd, kd, cd):
    S = rq.shape[0]
    N = S // RET_CHUNK
    cps = RET_CHUNKS_PER_STEP
    rows = cps * RET_CHUNK

    def body(q_ref, k_ref, v_ref, dm_ref, qd_ref, kd_ref, cd_ref, y_ref, st_ref, state):
        @pl.when(pl.program_id(0) == 0)
        def _():
            state[...] = jnp.zeros_like(state)

        for hd in range(RET_HEADS):
            qs = slice(hd * 128, (hd + 1) * 128)
            vs = slice(hd * 256, (hd + 1) * 256)
            free = []
            for ch in range(cps):
                rs = slice(ch * RET_CHUNK, (ch + 1) * RET_CHUNK)
                q = q_ref[rs, qs]
                kf = k_ref[rs, qs]
                v = v_ref[rs, vs]
                a = _dot_nt(q, kf.astype(BF)) * dm_ref[hd]
                kdv = (kf * kd_ref[hd]).astype(BF)
                free.append((rs, q, _dot(a.astype(BF), v), _dot_tn(kdv, v)))
            s = state[hd]
            for ch, (rs, q, inner, update) in enumerate(free):
                sb = s.astype(BF)
                st_ref[ch, hd] = sb
                y_ref[rs, vs] = (inner + _dot(q, sb) * qd_ref[hd]).astype(KEPT)
                s = s * cd_ref[hd] + update
            state[hd] = s

    return pl.pallas_call(
        body, name="ret_fwd", grid=(N // cps,),
        in_specs=[_rows(rows, 512), _rows(rows, 512), _rows(rows, 1024), _VMEM, _VMEM, _VMEM, _VMEM],
        out_specs=[_rows(rows, 1024), pl.BlockSpec((cps, RET_HEADS, 128, 256), lambda i: (i, 0, 0, 0))],
        out_shape=[jax.ShapeDtypeStruct((S, 1024), KEPT),
                   jax.ShapeDtypeStruct((N, RET_HEADS, 128, 256), BF)],
        scratch_shapes=[pltpu.VMEM((RET_HEADS, 128, 256), F32)],
        compiler_params=_params(("arbitrary",)),
    )(rq, rk, rv, dmat, qd, kd, cd)


def _kv_halves(prev_ref, cur_ref):
    kk = jnp.concatenate([prev_ref[...], cur_ref[...]], axis=0).astype(F32)
    return kk, pltpu.roll(kk, 64, 1)


ATTN_FWD_BLOCKS_PER_STEP = 4
PAIRS = 4


def _attn_fold(n):
    t = lax.broadcasted_iota(jnp.int32, (128, PAIRS * 128), 0)
    qi = lax.broadcasted_iota(jnp.int32, (128, PAIRS * 128), 1) & 127
    cur = t <= qi
    cur_f = jnp.where(cur, 1.0, 0.0)
    bias = jnp.where(jnp.logical_or(cur, n > 0), 0.0, -1e30)
    return cur, cur_f, 1.0 - cur_f, bias


def _group_operands(g, kk, kk_sw, vv, vv_sw):
    lo = lax.broadcasted_iota(jnp.int32, (256, 128), 1) < 64
    k_a, k_b = (kk, kk_sw) if g == 0 else (kk_sw, kk)
    v_a, v_b = (vv, vv_sw) if g == 0 else (vv_sw, vv)
    return (jnp.concatenate([jnp.where(lo, k_a, 0.0), jnp.where(lo, 0.0, k_b)], axis=0).astype(BF),
            jnp.concatenate([jnp.where(lo, v_a, 0.0), jnp.where(lo, 0.0, v_b)], axis=0).astype(BF))


def _stack_pairs(ref, g, rows=slice(None)):
    return jnp.concatenate([ref[rows, (4 * g + p) * 128:(4 * g + p + 1) * 128] for p in range(PAIRS)], axis=0)


def _sink_row(sk_ref, g, which):
    return jnp.concatenate([jnp.full((1, 128), sk_ref[0, 8 * g + 2 * p + which], F32) for p in range(PAIRS)], axis=1)


def _fold(full, cur):
    return jnp.where(cur, full[128:], full[:128])


def _attn_probs(raw, fold, sink):
    s = _fold(raw, fold[0]) * (ATTN_HEAD_DIM ** -0.5) + fold[3]
    m = jnp.maximum(jnp.max(s, axis=0, keepdims=True), sink)
    e = jnp.exp(s - m)
    es = jnp.exp(sink - m)
    inv = 1.0 / (jnp.sum(e, axis=0, keepdims=True) + es)
    return e * inv, es * inv


def _unfold(p, fold):
    return jnp.concatenate([p * fold[2], p * fold[1]], axis=0).astype(BF)


def _attn_fwd(aq, ak, av, sinks):
    S = aq.shape[0]
    N = S // SEQ_CHUNK
    nb = ATTN_FWD_BLOCKS_PER_STEP

    def body(sk_ref, q_ref, kp_ref, kc_ref, vp_ref, vc_ref, o_ref):
        kall, kall_sw = _kv_halves(kp_ref, kc_ref)
        vall, vall_sw = _kv_halves(vp_ref, vc_ref)
        blocks = []
        for sb in range(nb):
            kv = slice(sb * 128, sb * 128 + 256)
            rows = slice(sb * 128, (sb + 1) * 128)
            ops = [_group_operands(g, kall[kv], kall_sw[kv], vall[kv], vall_sw[kv]) for g in range(2)]
            raw = [_dot_nt(ops[g][0], _stack_pairs(q_ref, g, rows)) for g in range(2)]
            blocks.append((rows, ops, raw, _attn_fold(nb * pl.program_id(0) + sb)))
        for rows, ops, raw, fold in blocks:
            for g in range(2):
                probs = [_unfold(_attn_probs(raw[g][w * 256:(w + 1) * 256], fold, _sink_row(sk_ref, g, w))[0], fold)
                         for w in range(2)]
                o = _dot_tn(jnp.concatenate(probs, axis=0), ops[g][1])
                for p in range(PAIRS):
                    o_ref[rows, (4 * g + p) * 128:(4 * g + p + 1) * 128] = o[p * 128:(p + 1) * 128].astype(BF)

    prev = lambda i: (jnp.maximum(nb * i - 1, 0), 0)
    return pl.pallas_call(
        body, name="attn_fwd", grid=(N // nb,),
        in_specs=[_SMEM, _rows(nb * 128, 1024), pl.BlockSpec((128, 128), prev), _rows(nb * 128, 128),
                  pl.BlockSpec((128, 128), prev), _rows(nb * 128, 128)],
        out_specs=_rows(nb * 128, 1024),
        out_shape=jax.ShapeDtypeStruct((S, 1024), BF),
        compiler_params=_params(("parallel",)),
    )(sinks, aq, ak, ak, av, av)


def _group_norm(y):
    mu = jnp.mean(y, axis=-1, keepdims=True)
    yc = y - mu
    rstd = lax.rsqrt(jnp.mean(yc * yc, axis=-1, keepdims=True) + EPS)
    return yc * rstd, rstd


def _merge_fwd(x, ry, rg, ay, ga, gb, g_ret, w_ret, w_attn, w_out):
    S = x.shape[0]
    tm = 512

    def body(x_ref, ry_ref, rg_ref, ay_ref, ga_ref, gb_ref, gr_ref, wr_ref, wa_ref, wo_ref,
             u_ref, mg_ref, x1_ref):
        bb = _dot(ay_ref[...], wa_ref[...])
        for hd in range(RET_HEADS):
            vs = slice(hd * 256, (hd + 1) * 256)
            yn, _ = _group_norm(ry_ref[:, vs].astype(F32))
            rgv = rg_ref[:, vs].astype(F32)
            u_ref[:, vs] = (rgv * _sigmoid(rgv) * (yn * gr_ref[:, vs])).astype(BF)
        gated_b = _sigmoid(gb_ref[...].astype(F32)) * bb
        sa = _sigmoid(ga_ref[...].astype(F32))
        ba = _dot(u_ref[...], wr_ref[...])
        merged = (sa * ba + gated_b).astype(BF)
        mg_ref[...] = merged
        x1_ref[...] = x_ref[...] + _dot(merged, wo_ref[...])

    outs = [BF, BF, F32]
    return pl.pallas_call(
        body, name="merge_fwd", grid=(S // tm,),
        in_specs=[_rows(tm, 1024)] * 6 + [_fixed((1, 1024)), _VMEM, _VMEM, _VMEM],
        out_specs=[_rows(tm, 1024)] * 3,
        out_shape=[jax.ShapeDtypeStruct((S, 1024), dt) for dt in outs],
        compiler_params=_params(("parallel",)),
    )(x, ry, rg, ay, ga, gb, g_ret, w_ret, w_attn, w_out)


def _ffn_fwd(x1, tgt, g2, g3, w_gate, w_up, w_down):
    S = x1.shape[0]
    tm = 256

    def body(x1_ref, t_ref, g2_ref, g3_ref, wg_ref, wu_ref, wd_ref,
             gate_ref, up_ref, h2_ref, act_ref, dx2_ref, loss_ref, dg3_ref):
        @pl.when(pl.program_id(0) == 0)
        def _():
            loss_ref[...] = jnp.zeros_like(loss_ref)
            dg3_ref[...] = jnp.zeros_like(dg3_ref)

        x1v = x1_ref[...]
        r2 = lax.rsqrt(jnp.mean(x1v * x1v, axis=-1, keepdims=True) + EPS)
        h2 = (x1v * r2 * g2_ref[...]).astype(BF)
        h2_ref[...] = h2
        gate = _dot_nt(h2, wg_ref[...])
        up = _dot_nt(h2, wu_ref[...])
        gate_ref[...] = gate
        up_ref[...] = up
        act = (gate * _sigmoid(gate) * up).astype(BF)
        act_ref[...] = act
        x2 = x1v + _dot(act, wd_ref[...])
        r3 = lax.rsqrt(jnp.mean(x2 * x2, axis=-1, keepdims=True) + EPS)
        xh = x2 * r3
        g3v = g3_ref[...]
        err = xh * g3v - t_ref[...]
        loss_ref[...] += 0.5 * jnp.sum(jnp.mean(err * err, axis=-1, keepdims=True), axis=0, keepdims=True)
        dy = err * (1.0 / D_MODEL)
        dg3_ref[...] += jnp.sum(dy * xh, axis=0, keepdims=True)
        dxh = dy * g3v
        dx2_ref[...] = r3 * (dxh - xh * jnp.mean(dxh * xh, axis=-1, keepdims=True))

    return pl.pallas_call(
        body, name="ffn_fwd", grid=(S // tm,),
        in_specs=[_rows(tm, 1024), _rows(tm, 1024), _fixed((1, 1024)), _fixed((1, 1024)), _VMEM, _VMEM, _VMEM],
        out_specs=[_rows(tm, D_FF), _rows(tm, D_FF), _rows(tm, 1024), _rows(tm, D_FF), _rows(tm, 1024),
                   _fixed((1, LANES)), _fixed((1, 1024))],
        out_shape=[jax.ShapeDtypeStruct((S, D_FF), F32), jax.ShapeDtypeStruct((S, D_FF), F32),
                   jax.ShapeDtypeStruct((S, 1024), BF), jax.ShapeDtypeStruct((S, D_FF), BF),
                   jax.ShapeDtypeStruct((S, 1024), F32),
                   jax.ShapeDtypeStruct((1, LANES), F32), jax.ShapeDtypeStruct((1, 1024), F32)],
        compiler_params=_params(("arbitrary",)),
    )(x1, tgt, g2, g3, w_gate, w_up, w_down)


def _ffn_bwd(dx2, gate, up, x1, g2, w_gate, w_up, w_down):
    S = x1.shape[0]
    tm = 256

    def body(dx2_ref, gate_ref, up_ref, x1_ref, g2_ref, wg_ref, wu_ref, wd_ref,
             dgate_ref, dup_ref, dx1_ref, dg2_ref):
        @pl.when(pl.program_id(0) == 0)
        def _():
            dg2_ref[...] = jnp.zeros_like(dg2_ref)

        dx2v = dx2_ref[...]
        dact = _dot_nt(dx2v.astype(BF), wd_ref[...])
        gate = gate_ref[...]
        sg = _sigmoid(gate)
        dgate = (dact * up_ref[...] * (sg * (1.0 + gate * (1.0 - sg)))).astype(BF)
        dup = (dact * (gate * sg)).astype(BF)
        dgate_ref[...] = dgate
        dup_ref[...] = dup
        dh2 = _dot(dgate, wg_ref[...]) + _dot(dup, wu_ref[...])
        x1v = x1_ref[...]
        r2 = lax.rsqrt(jnp.mean(x1v * x1v, axis=-1, keepdims=True) + EPS)
        xh = x1v * r2
        dg2_ref[...] += jnp.sum(dh2 * xh, axis=0, keepdims=True)
        dxh = dh2 * g2_ref[...]
        dx1_ref[...] = dx2v + r2 * (dxh - xh * jnp.mean(dxh * xh, axis=-1, keepdims=True))

    return pl.pallas_call(
        body, name="ffn_bwd", grid=(S // tm,),
        in_specs=[_rows(tm, 1024), _rows(tm, D_FF), _rows(tm, D_FF), _rows(tm, 1024), _fixed((1, 1024)),
                  _VMEM, _VMEM, _VMEM],
        out_specs=[_rows(tm, D_FF), _rows(tm, D_FF), _rows(tm, 1024), _fixed((1, 1024))],
        out_shape=[jax.ShapeDtypeStruct((S, D_FF), BF), jax.ShapeDtypeStruct((S, D_FF), BF),
                   jax.ShapeDtypeStruct((S, 1024), F32), jax.ShapeDtypeStruct((1, 1024), F32)],
        compiler_params=_params(("arbitrary",)),
    )(dx2, gate, up, x1, g2, w_gate, w_up, w_down)


def _merge_bwd(dx1, ga, gb, u, ay, rg, ry, g_ret, w_ret, w_attn, w_out):
    S = dx1.shape[0]
    tm = 512

    def body(dx1_ref, ga_ref, gb_ref, u_ref, ay_ref, rg_ref, ry_ref, gr_ref, wr_ref, wa_ref, wo_ref,
             dgates_ref, drg_ref, dba_ref, dbb_ref, day_ref, dry_ref, dgr_ref):
        @pl.when(pl.program_id(0) == 0)
        def _():
            dgr_ref[...] = jnp.zeros_like(dgr_ref)

        dm = _dot_nt(dx1_ref[...].astype(BF), wo_ref[...])
        ba = _dot(u_ref[...], wr_ref[...])
        bb = _dot(ay_ref[...], wa_ref[...])
        sa = _sigmoid(ga_ref[...].astype(F32))
        sb = _sigmoid(gb_ref[...].astype(F32))
        dga = (dm * ba * (sa * (1.0 - sa))).astype(BF)
        dgb = (dm * bb * (sb * (1.0 - sb))).astype(BF)
        dgates_ref[:, :1024] = dga
        dgates_ref[:, 1024:] = dgb
        dba = (dm * sa).astype(BF)
        dbb = (dm * sb).astype(BF)
        dba_ref[...] = dba
        dbb_ref[...] = dbb
        day_ref[...] = _dot_nt(dbb, wa_ref[...]).astype(BF)
        du = _dot_nt(dba, wr_ref[...])
        for hd in range(RET_HEADS):
            vs = slice(hd * 256, (hd + 1) * 256)
            yn, rstd = _group_norm(ry_ref[:, vs].astype(F32))
            rgv = rg_ref[:, vs].astype(F32)
            sr = _sigmoid(rgv)
            silu = rgv * sr
            duv = du[:, vs]
            grv = gr_ref[:, vs]
            drg = (duv * (yn * grv) * (sr * (1.0 + rgv * (1.0 - sr)))).astype(BF)
            drg_ref[:, vs] = drg
            dyg = duv * silu
            dgr_ref[:, vs] += jnp.sum(dyg * yn, axis=0, keepdims=True)
            dyn = dyg * grv
            dry_ref[:, vs] = (rstd * (dyn - jnp.mean(dyn, axis=-1, keepdims=True)
                                      - yn * jnp.mean(dyn * yn, axis=-1, keepdims=True))).astype(KEPT)

    return pl.pallas_call(
        body, name="merge_bwd", grid=(S // tm,),
        in_specs=[_rows(tm, 1024)] * 7 + [_fixed((1, 1024)), _VMEM, _VMEM, _VMEM],
        out_specs=[_rows(tm, 2048), _rows(tm, 1024), _rows(tm, 1024), _rows(tm, 1024), _rows(tm, 1024),
                   _rows(tm, 1024), _fixed((1, 1024))],
        out_shape=[jax.ShapeDtypeStruct((S, 2048), BF), jax.ShapeDtypeStruct((S, 1024), BF),
                   jax.ShapeDtypeStruct((S, 1024), BF), jax.ShapeDtypeStruct((S, 1024), BF),
                   jax.ShapeDtypeStruct((S, 1024), BF), jax.ShapeDtypeStruct((S, 1024), KEPT),
                   jax.ShapeDtypeStruct((1, 1024), F32)],
        compiler_params=_params(("arbitrary",)),
    )(dx1, ga, gb, u, ay, rg, ry, g_ret, w_ret, w_attn, w_out)


def _attn_bwd(aq, ak, av, day, sinks, ca, sa, after):
    S = aq.shape[0]
    N = S // SEQ_CHUNK
    steps = N // 2

    def body(sk_ref, q_ref, kp_ref, kc_ref, vp_ref, vc_ref, do_ref, cq_ref, sq_ref, cp_ref, sp_ref, after_ref,
             dq_ref, dke_ref, dve_ref, dko_ref, dvo_ref, dsk_ref, carry_k, carry_v):
        i = pl.program_id(0)

        @pl.when(i == 0)
        def _():
            carry_k[...] = jnp.zeros_like(carry_k)
            carry_v[...] = jnp.zeros_like(carry_v)
            dsk_ref[...] = jnp.zeros_like(dsk_ref)

        @pl.when(i < steps)
        def _():
            kall, kall_sw = _kv_halves(kp_ref, kc_ref)
            vall, vall_sw = _kv_halves(vp_ref, vc_ref)
            lo = lax.broadcasted_iota(jnp.int32, (256, 128), 1) < 64
            blocks = []
            for sb in range(2):
                kv = slice(sb * 128, sb * 128 + 256)
                rows = slice(sb * 128, (sb + 1) * 128)
                ops = [_group_operands(g, kall[kv], kall_sw[kv], vall[kv], vall_sw[kv]) for g in range(2)]
                stacked = [(_stack_pairs(q_ref, g, rows), _stack_pairs(do_ref, g, rows)) for g in range(2)]
                raw = [_dot_nt(ops[g][0], stacked[g][0]) for g in range(2)]
                raw_dp = [_dot_nt(ops[g][1], stacked[g][1]) for g in range(2)]
                blocks.append((rows, ops, stacked, raw, raw_dp, _attn_fold(2 * i + sb)))
            grads = []
            d_sink = [jnp.zeros((1, PAIRS * 128), F32) for _ in range(4)]
            for rows, ops, stacked, raw, raw_dp, fold in blocks:
                cur = fold[0]
                cq, sq = cq_ref[rows, :], sq_ref[rows, :]
                dk_groups, dv_groups = [], []
                for g in range(2):
                    qs, dos = stacked[g]
                    ds_parts, p_parts = [], []
                    for which in range(2):
                        half = slice(which * 256, (which + 1) * 256)
                        pr, ps = _attn_probs(raw[g][half], fold, _sink_row(sk_ref, g, which))
                        dp = _fold(raw_dp[g][half], cur)
                        dr = jnp.sum(pr * dp, axis=0, keepdims=True)
                        ds_parts.append(_unfold(pr * (dp - dr) * (ATTN_HEAD_DIM ** -0.5), fold))
                        p_parts.append(_unfold(pr, fold))
                        d_sink[2 * g + which] = d_sink[2 * g + which] - ps * dr
                    ds = jnp.concatenate(ds_parts, axis=0)
                    dq = _dot_tn(ds, ops[g][0])
                    dk_both = _dot(ds, qs)
                    dv_both = _dot(jnp.concatenate(p_parts, axis=0), dos)
                    for p in range(PAIRS):
                        sl = slice((4 * g + p) * 128, (4 * g + p + 1) * 128)
                        dq_ref[rows, sl] = _rot_attn_bwd(dq[p * 128:(p + 1) * 128], cq, sq).astype(BF)
                    tk = jnp.where(lo, dk_both[:256], dk_both[256:])
                    tv = jnp.where(lo, dv_both[:256], dv_both[256:])
                    dk_groups.append(tk + pltpu.roll(tk, 64, 1))
                    dv_groups.append(tv + pltpu.roll(tv, 64, 1))
                grads.append((jnp.where(lo, dk_groups[0], dk_groups[1]), jnp.where(lo, dv_groups[0], dv_groups[1])))
            dsk_ref[...] += jnp.concatenate(d_sink, axis=0)
            (dk_a, dv_a), (dk_b, dv_b) = grads
            dko_ref[...] = _rot_attn_bwd(carry_k[...] + dk_a[:128], cp_ref[...], sp_ref[...]).astype(BF)
            dvo_ref[...] = (carry_v[...] + dv_a[:128]).astype(BF)
            dke_ref[...] = _rot_attn_bwd(dk_a[128:] + dk_b[:128], cq_ref[0:128, :], sq_ref[0:128, :]).astype(BF)
            dve_ref[...] = (dv_a[128:] + dv_b[:128]).astype(BF)
            carry_k[...] = dk_b[128:]
            carry_v[...] = dv_b[128:]

        @pl.when(i == steps)
        def _():
            dko_ref[...] = _rot_attn_bwd(carry_k[...], cp_ref[...], sp_ref[...]).astype(BF)
            dvo_ref[...] = carry_v[...].astype(BF)

    two = lambda i: (jnp.minimum(i, steps - 1), 0)
    before = lambda i: (jnp.maximum(2 * i - 1, 0), 0)
    odd = lambda i: (jnp.maximum(i - 1, 0), 0)
    blk = lambda r, w, im: pl.BlockSpec((r, w), im)
    half = jax.ShapeDtypeStruct((S // 2, 128), BF)
    dq, dk_even, dv_even, dk_odd, dv_odd, dsk = pl.pallas_call(
        body, name="attn_bwd", grid=(steps + 1,),
        in_specs=[_SMEM, blk(256, 1024, two), blk(128, 128, before), blk(256, 128, two), blk(128, 128, before),
                  blk(256, 128, two), blk(256, 1024, two), blk(256, 128, two), blk(256, 128, two),
                  blk(128, 128, before), blk(128, 128, before), _ANY],
        out_specs=[blk(256, 1024, two), blk(128, 128, two), blk(128, 128, two), blk(128, 128, odd),
                   blk(128, 128, odd), _fixed((4, PAIRS * 128))],
        out_shape=[jax.ShapeDtypeStruct((S, 1024), BF), half, half, half, half,
                   jax.ShapeDtypeStruct((4, PAIRS * 128), F32)],
        scratch_shapes=[pltpu.VMEM((128, 128), F32), pltpu.VMEM((128, 128), F32)],
        compiler_params=_params(("arbitrary",)),
    )(sinks, aq, ak, ak, av, av, day, ca, sa, ca, sa, after)

    def interleave(even, odd_):
        return jnp.stack([even.reshape(steps, 128, 128), odd_.reshape(steps, 128, 128)], axis=1).reshape(S, 128)

    return dq, interleave(dk_even, dk_odd), interleave(dv_even, dv_odd), dsk


def _ret_bwd(rq, rk, rv, dry, states, dmat, qd, kd, cd, cr, sr, after):
    S = rq.shape[0]
    N = S // RET_CHUNK
    scale = RET_QK_DIM ** -0.5
    cps = RET_CHUNKS_PER_STEP
    steps = N // cps

    def body(q_ref, k_ref, v_ref, dy_ref, st_ref, dm_ref, qd_ref, kd_ref, cd_ref, cr_ref, sr_ref, after_ref,
             out_ref, dstate):
        @pl.when(pl.program_id(0) == 0)
        def _():
            dstate[...] = jnp.zeros_like(dstate)

        for hd in range(RET_HEADS):
            qs = slice(hd * 128, (hd + 1) * 128)
            vs = slice(hd * 256, (hd + 1) * 256)
            dmv = dm_ref[hd]
            free = []
            for ch in range(cps):
                rs = slice(ch * RET_CHUNK, (ch + 1) * RET_CHUNK)
                crv, srv = cr_ref[rs, :], sr_ref[rs, :]
                q = q_ref[rs, qs]
                kf = k_ref[rs, qs]
                k = kf.astype(BF)
                v = v_ref[rs, vs]
                do = dy_ref[rs, vs].astype(F32)
                dob = do.astype(BF)
                doq = (do * qd_ref[hd]).astype(BF)
                a = (_dot_nt(q, k) * dmv).astype(BF)
                da = (_dot_nt(dob, v) * dmv).astype(BF)
                dq = _dot(da, k) + _dot_nt(doq, st_ref[ch, hd])
                out_ref[rs, qs] = _rot_ret_bwd(dq * scale, crv, srv).astype(BF)
                free.append((rs, crv, srv, v, (kf * kd_ref[hd]).astype(BF),
                             _dot_tn(a, dob), _dot_tn(da, q), _dot_tn(q, doq)))
            ds = dstate[hd]
            for rs, crv, srv, v, kdv, dv_free, dk_free, ds_update in reversed(free):
                dsb = ds.astype(BF)
                dv = dv_free + _dot(kdv, dsb)
                dk = dk_free + _dot_nt(v, dsb) * kd_ref[hd]
                ds = ds * cd_ref[hd] + ds_update
                out_ref[rs, 512 + hd * 128:512 + (hd + 1) * 128] = _rot_ret_bwd(dk, crv, srv).astype(BF)
                out_ref[rs, 1024 + hd * 256:1024 + (hd + 1) * 256] = dv.astype(BF)
            dstate[hd] = ds

    rev = lambda w: pl.BlockSpec((cps * RET_CHUNK, w), lambda i: (steps - 1 - i, 0))
    return pl.pallas_call(
        body, name="ret_bwd", grid=(steps,),
        in_specs=[rev(512), rev(512), rev(1024), rev(1024),
                  pl.BlockSpec((cps, RET_HEADS, 128, 256), lambda i: (steps - 1 - i, 0, 0, 0)),
                  _VMEM, _VMEM, _VMEM, _VMEM, rev(128), rev(128), _ANY],
        out_specs=rev(2048),
        out_shape=jax.ShapeDtypeStruct((S, 2048), BF),
        scratch_shapes=[pltpu.VMEM((RET_HEADS, 128, 256), F32)],
        compiler_params=_params(("arbitrary",)),
    )(rq, rk, rv, dry, states, dmat, qd, kd, cd, cr, sr, after)


def _proj_bwd(d_pieces, x, dx1, g1, w_in_t):
    S = x.shape[0]
    tm = 512
    widths = [d.shape[1] for d in d_pieces]
    assert sum(widths) == D_IN
    nd = len(d_pieces)

    def body(*refs):
        d_refs = refs[:nd]
        x_ref, dx1_ref, g_ref, w_ref, gx_ref, dg_ref, db_ref = refs[nd:]

        @pl.when(pl.program_id(0) == 0)
        def _():
            dg_ref[...] = jnp.zeros_like(dg_ref)
            db_ref[...] = jnp.zeros_like(db_ref)

        dh = jnp.zeros((tm, D_MODEL), F32)
        lo = 0
        for ref, n in zip(d_refs, widths):
            v = ref[...]
            dh = dh + _dot(v, w_ref[lo:lo + n, :])
            db_ref[:, lo:lo + n] += jnp.sum(v.astype(F32), axis=0, keepdims=True)
            lo += n
        xv = x_ref[...]
        r = lax.rsqrt(jnp.mean(xv * xv, axis=-1, keepdims=True) + EPS)
        xh = xv * r
        dg_ref[...] += jnp.sum(dh * xh, axis=0, keepdims=True)
        dxh = dh * g_ref[...]
        gx_ref[...] = dx1_ref[...] + r * (dxh - xh * jnp.mean(dxh * xh, axis=-1, keepdims=True))

    return pl.pallas_call(
        body, name="proj_bwd", grid=(S // tm,),
        in_specs=[_rows(tm, n) for n in widths] + [_rows(tm, 1024), _rows(tm, 1024), _fixed((1, 1024)), _VMEM],
        out_specs=[_rows(tm, 1024), _fixed((1, 1024)), _fixed((1, D_IN))],
        out_shape=[jax.ShapeDtypeStruct((S, 1024), F32), jax.ShapeDtypeStruct((1, 1024), F32),
                   jax.ShapeDtypeStruct((1, D_IN), F32)],
        compiler_params=_params(("arbitrary",)),
    )(*d_pieces, x, dx1, g1, w_in_t)


SHARD_ROWS = (1600, 256, 256, 256, 704, 704, 704)


class Group(NamedTuple):
    name: str
    tensors: tuple
    offs: tuple
    block: int
    tile: int

    @property
    def half(self):
        return self.block // 2

    def off(self, t):
        return self.offs[self.tensors.index(t)]


LATE = Group("late", (0,), (0,), 1600, 800)
EARLY = Group("early", (1, 2, 3, 4, 5, 6), (0, 256, 512, 768, 1472, 2176), 2880, 480)
GROUP_OF = {t: g for g in (LATE, EARLY) for t in g.tensors}


def _wgrad(a_list, b, t, packed, name, row0=0):
    S = b.shape[0]
    widths = [a.shape[1] for a in a_list]
    M = sum(widths)
    ts = 1024
    steps = S // ts
    na = len(a_list)
    grp = GROUP_OF[t]
    r, off = SHARD_ROWS[t], grp.off(t)
    segments = []
    for chip in range(N_CHIPS):
        lo, hi = max(chip * r, row0), min((chip + 1) * r, row0 + M)
        if lo < hi:
            segments.append((lo - row0, hi - lo, chip * grp.block + off + lo - chip * r))
    assert sum(n for _, n, _ in segments) == M
    stage_rows = max(n for _, n, _ in segments)
    n_in = na + 1 + (2 if packed is not None else 0)

    def body(*refs):
        a_refs, b_ref = refs[:na], refs[na]
        g32_ref, gb_ref, acc, stage, sems = refs[n_in:]
        i = pl.program_id(0)

        @pl.when(i == 0)
        def _():
            acc[...] = jnp.zeros_like(acc)

        bv = b_ref[...].astype(BF)
        lo = 0
        for a_ref, n in zip(a_refs, widths):
            acc[lo:lo + n, :] += _dot_tn(a_ref[...].astype(BF), bv)
            lo += n

        @pl.when(i == steps - 1)
        def _():
            for a0, n, p0 in segments:
                c32 = pltpu.make_async_copy(acc.at[a0:a0 + n, :], g32_ref.at[pl.ds(p0, n), :], sems.at[0])
                c32.start()
                stage[0:n, :] = acc[a0:a0 + n, :].astype(BF)
                c16 = pltpu.make_async_copy(stage.at[0:n, :], gb_ref.at[pl.ds(p0, n), :], sems.at[1])
                c16.start()
                c32.wait()
                c16.wait()

    shapes = [jax.ShapeDtypeStruct((N_CHIPS * grp.block, 1024), F32),
              jax.ShapeDtypeStruct((N_CHIPS * grp.block, 1024), BF)]
    operands = list(a_list) + [b]
    in_specs = [_rows(ts, n) for n in widths] + [_rows(ts, 1024)]
    aliases = {}
    if packed is not None:
        operands += list(packed)
        in_specs += [_ANY, _ANY]
        aliases = {na + 1: 0, na + 2: 1}
    return pl.pallas_call(
        body, name=name, grid=(steps,),
        in_specs=in_specs, out_specs=[_ANY, _ANY], out_shape=shapes,
        scratch_shapes=[pltpu.VMEM((M, 1024), F32), pltpu.VMEM((stage_rows, 1024), BF),
                        pltpu.SemaphoreType.DMA((2,))],
        input_output_aliases=aliases,
        compiler_params=_params(("arbitrary",)),
    )(*operands)


def _stage_weights(place, ws, name, after=()):
    nw = len(ws)

    def body(p_ref, *refs):
        for w_ref, o_ref in zip(refs[:nw], refs[nw + len(after):]):
            o_ref[...] = w_ref[...].astype(BF)

    return pl.pallas_call(
        body, name=name,
        grid_spec=pltpu.PrefetchScalarGridSpec(
            num_scalar_prefetch=1, grid=(1,),
            in_specs=[pl.BlockSpec(w.shape, lambda i, p: (0, 0)) for w in ws] + [_ANY] * len(after),
            out_specs=[pl.BlockSpec(w.shape, lambda i, p: (p[0], 0)) for w in ws]),
        out_shape=[jax.ShapeDtypeStruct((N_CHIPS * w.shape[0], 1024), BF) for w in ws],
        compiler_params=_params(("arbitrary",)),
    )(place, *ws, *after)


def _kept_block(grp, nb):
    return lambda i, p: ((2 * (i // nb) + p[1]) * nb + i % nb, 0)


def _pair_sum(place, g32, got, grp):
    nb = grp.half // grp.tile

    def body(p_ref, g_ref, r_ref, s16_ref):
        s16_ref[...] = (g_ref[...] + r_ref[...].astype(F32)).astype(BF)

    flat = pl.BlockSpec((grp.tile, 1024), lambda i, p: (i, 0))
    return pl.pallas_call(
        body, name="pair_sum_" + grp.name,
        grid_spec=pltpu.PrefetchScalarGridSpec(
            num_scalar_prefetch=1, grid=(N_CHIPS * nb,),
            in_specs=[pl.BlockSpec((grp.tile, 1024), _kept_block(grp, nb)), flat], out_specs=flat),
        out_shape=jax.ShapeDtypeStruct((N_CHIPS * grp.half, 1024), BF),
        compiler_params=_params(("arbitrary",)),
    )(place, g32, got)


def _chip_sum(place, g32, got, others, grp):
    nb = grp.half // grp.tile

    def body(p_ref, g_ref, r_ref, r0_ref, r1_ref, r2_ref, o_ref):
        own = g_ref[...] + r_ref[...].astype(F32)
        o_ref[...] = ((own + r0_ref[...].astype(F32)) + r1_ref[...].astype(F32)) + r2_ref[...].astype(F32)

    blk = lambda f: pl.BlockSpec((grp.tile, 1024), f)
    return pl.pallas_call(
        body, name="chip_sum_" + grp.name,
        grid_spec=pltpu.PrefetchScalarGridSpec(
            num_scalar_prefetch=1, grid=(nb,),
            in_specs=[blk(lambda i, p: ((2 * p[0] + p[1]) * nb + i, 0)), blk(lambda i, p: (p[0] * nb + i, 0)),
                      blk(lambda i, p: (i, 0)), blk(lambda i, p: (nb + i, 0)), blk(lambda i, p: (2 * nb + i, 0))],
            out_specs=blk(lambda i, p: (p[1] * nb + i, 0))),
        out_shape=jax.ShapeDtypeStruct((grp.block, 1024), F32),
        compiler_params=_params(("arbitrary",)),
    )(place, g32, got, others, others, others)


def _adamw(w, g, m, v, name, g_row0=0, after=None):
    r = w.shape[0]
    tr = 320 if r % 320 == 0 and g_row0 % 320 == 0 else (64 if r % 64 == 0 else r)
    assert g_row0 % tr == 0
    g0 = g_row0 // tr
    extra = [] if after is None else [after]

    def body(w_ref, g_ref, m_ref, v_ref, *rest):
        go_ref, d_ref, mo_ref, vo_ref = rest[len(extra):]
        gv = g_ref[...]
        mn = ADAM_B1 * m_ref[...] + (1.0 - ADAM_B1) * gv
        vn = ADAM_B2 * v_ref[...] + (1.0 - ADAM_B2) * (gv * gv)
        m_hat = mn / (1.0 - ADAM_B1 ** ADAM_STEP)
        v_hat = vn / (1.0 - ADAM_B2 ** ADAM_STEP)
        go_ref[...] = gv
        d_ref[...] = -ADAM_LR * (m_hat / (jnp.sqrt(v_hat) + ADAM_EPS) + ADAM_WD * w_ref[...])
        mo_ref[...] = mn
        vo_ref[...] = vn

    return pl.pallas_call(
        body, name=name, grid=(r // tr,),
        in_specs=[_rows(tr, 1024), pl.BlockSpec((tr, 1024), lambda i: (g0 + i, 0)), _rows(tr, 1024), _rows(tr, 1024)]
        + [_ANY] * len(extra),
        out_specs=[_rows(tr, 1024)] * 4,
        out_shape=[jax.ShapeDtypeStruct((r, 1024), F32)] * 4,
        compiler_params=_params(("parallel",)),
    )(w, g, m, v, *extra)


def _place():
    x, y, c = lax.axis_index("x"), lax.axis_index("y"), lax.axis_index("c")
    chips = [(1 - x, y), (x, 1 - y), (1 - x, 1 - y)]
    return x, y, c, chips


_HBM =pl.BlockSpec(memory_space=pltpu.HBM)
_SEM = pl.BlockSpec(memory_space=pltpu.SEMAPHORE)
_DATAFLOW = pltpu.SideEffectType.DATAFLOW_SIDE_EFFECTING


def _exchange(name, arrays, n, plan):
    na = len(arrays)

    def body(*refs):
        pairs = plan(refs[na:2 * na], refs[2 * na], refs[2 * na + 1])
        for send, _ in pairs:
            send.start()
        for send, recv in pairs:
            send.wait_send()
            recv.wait_recv()

    return pl.pallas_call(
        body, name=name, in_specs=[_ANY] * na, out_specs=[_ANY] * na,
        out_shape=[jax.ShapeDtypeStruct(a.shape, a.dtype) for a in arrays],
        scratch_shapes=[pltpu.SemaphoreType.DMA((n,)), pltpu.SemaphoreType.DMA((n,))],
        input_output_aliases={i: i for i in range(na)},
    )(*arrays)


def _exchange_start(name, arrays, n, plan, after=()):
    na = len(arrays)

    def body(*refs):
        for send, _ in plan(refs[:na], refs[na + len(after)], refs[na + len(after) + 1]):
            send.start()
        refs[-1][...] = jnp.zeros_like(refs[-1])

    out = pl.pallas_call(
        body, name=name,
        out_shape=(pltpu.SemaphoreType.DMA((n,)), pltpu.SemaphoreType.DMA((n,)),
                   *[pltpu.HBM(a.shape, a.dtype) for a in arrays], jax.ShapeDtypeStruct((8, LANES), F32)),
        in_specs=[_HBM] * na + [_ANY] * len(after), out_specs=(_SEM, _SEM, *[_HBM] * na, _VMEM),
        input_output_aliases={i: 2 + i for i in range(na)},
        compiler_params=pltpu.CompilerParams(has_side_effects=_DATAFLOW),
    )(*[pltpu.with_memory_space_constraint(a, pltpu.HBM) for a in arrays], *after)
    return (out[0], out[1]), list(out[2:2 + na]), out[-1]


def _exchange_wait(name, sems, in_flight, plan, after=()):
    na = len(in_flight)

    def body(*refs):
        for send, recv in plan(refs[:na], refs[na], refs[na + 1]):
            send.wait_send()
            recv.wait_recv()

    out = pl.pallas_call(
        body, name=name,
        out_shape=tuple(pltpu.HBM(a.shape, a.dtype) for a in in_flight),
        in_specs=[_HBM] * na + [_SEM, _SEM] + [_ANY] * len(after), out_specs=tuple([_HBM] * na),
        input_output_aliases={i: i for i in range(na)},
        compiler_params=pltpu.CompilerParams(has_side_effects=_DATAFLOW),
    )(*in_flight, *sems, *after)
    return list(out)


def _remote(src, dst, send_sems, recv_sems, k, to):
    return pltpu.make_async_remote_copy(src_ref=src, dst_ref=dst, send_sem=send_sems.at[k], recv_sem=recv_sems.at[k],
                                        device_id=to, device_id_type=MESH)


def _half_rows(ref, tid, pj, half):
    h = SHARD_ROWS[tid] // 2
    return ref.at[pl.ds((2 * pj + half) * h, h), :]


def _plan_gather_halves(tids):
    def plan(refs, ss, rs):
        x, y, c, chips = _place()
        pairs = []
        for t, tid in enumerate(tids):
            mine = _half_rows(refs[t], tid, 2 * x + y, c)
            for k, (cx, cy) in enumerate(chips):
                theirs = _half_rows(refs[t], tid, 2 * cx + cy, c)
                pairs.append((_remote(mine, mine, ss, rs, 3 * t + k, (cx, cy, c)),
                              _remote(theirs, theirs, ss, rs, 3 * t + k, (x, y, c))))
        return pairs
    return plan


def _plan_forward_halves(tids):
    def plan(refs, ss, rs):
        x, y, c, chips = _place()
        pairs = []
        for t, tid in enumerate(tids):
            for k, (cx, cy) in enumerate(chips):
                got = _half_rows(refs[t], tid, 2 * cx + cy, c)
                missing = _half_rows(refs[t], tid, 2 * cx + cy, 1 - c)
                pairs.append((_remote(got, got, ss, rs, 3 * t + k, (x, y, 1 - c)),
                              _remote(missing, missing, ss, rs, 3 * t + k, (x, y, 1 - c))))
        return pairs
    return plan


def _plan_pair_exchange(grp):
    def plan(refs, ss, rs):
        x, y, c, _ = _place()
        cps = [_remote(refs[0].at[pl.ds(pj * grp.block + (1 - c) * grp.half, grp.half), :],
                       refs[1].at[pl.ds(pj * grp.half, grp.half), :], ss, rs, pj, (x, y, 1 - c))
               for pj in range(N_CHIPS)]
        return [(cp, cp) for cp in cps]
    return plan


def _plan_to_chips(grp):
    def plan(refs, ss, rs):
        x, y, c, chips = _place()
        cps = [_remote(refs[0].at[pl.ds((2 * cx + cy) * grp.half, grp.half), :],
                       refs[1].at[pl.ds(k * grp.half, grp.half), :], ss, rs, k, (cx, cy, c))
               for k, (cx, cy) in enumerate(chips)]
        return [(cp, cp) for cp in cps]
    return plan


def _plan_share_halves(grp):
    def plan(refs, ss, rs):
        x, y, c, _ = _place()
        mine = refs[0].at[pl.ds(c * grp.half, grp.half), :]
        other = refs[0].at[pl.ds((1 - c) * grp.half, grp.half), :]
        return [(_remote(mine, mine, ss, rs, 0, (x, y, 1 - c)), _remote(other, other, ss, rs, 0, (x, y, 1 - c)))]
    return plan


def _empty(rows, dtype):
    return lax.empty((rows, 1024), dtype)


def _all_reduce_small(v, after):
    m_per, n = v.shape
    n_dev = 8

    def body(x_ref, after_ref, out_ref, all_ref, send_sems, recv_sems, local_sem):
        x, y, c, chips = _place()
        me, sibling = (x, y, c), (x, y, 1 - c)

        def rows(px, py, pc):
            return all_ref.at[pl.ds((4 * px + 2 * py + pc) * m_per, m_per), :]

        def copy(k, block, to, src=None):
            return pltpu.make_async_remote_copy(
                src_ref=rows(*block) if src is None else src, dst_ref=rows(*block),
                send_sem=send_sems.at[k], recv_sem=recv_sems.at[k], device_id=to, device_id_type=MESH)

        mine = pltpu.make_async_copy(x_ref, rows(*me), local_sem)
        mine.start()
        first = [copy(0, me, sibling, src=x_ref)]
        first += [copy(1 + k, me, (*chip, c), src=x_ref) for k, chip in enumerate(chips)]
        for cp in first:
            cp.start()
        passed = [copy(4 + k, (*chip, c), sibling) for k, chip in enumerate(chips)]
        for k, chip in enumerate(chips):
            copy(1 + k, (*chip, c), me).wait_recv()
            passed[k].start()
        copy(0, sibling, me).wait_recv()
        for k, chip in enumerate(chips):
            copy(4 + k, (*chip, 1 - c), me).wait_recv()
        for cp in first + passed:
            cp.wait_send()
        mine.wait()
        acc = all_ref[pl.ds(0, m_per), :]
        for d in range(1, n_dev):
            acc = acc + all_ref[pl.ds(d * m_per, m_per), :]
        out_ref[...] = acc

    return pl.pallas_call(
        body, name="all_reduce_small",
        in_specs=[_VMEM, _ANY], out_specs=_VMEM,
        out_shape=jax.ShapeDtypeStruct((m_per, n), F32),
        scratch_shapes=[pltpu.VMEM((n_dev * m_per, n), F32),
                        pltpu.SemaphoreType.DMA((7,)), pltpu.SemaphoreType.DMA((7,)), pltpu.SemaphoreType.DMA],
        compiler_params=pltpu.CompilerParams(has_side_effects=True),
    )(v, after)


def _tables(S, token):
    pos = jnp.arange(S, dtype=jnp.int32).astype(F32) + token[0, 0]

    def cs(half):
        inv_freq = ROPE_THETA ** (-jnp.arange(half, dtype=F32) / half)
        ang = pos[:, None] * inv_freq[None, :]
        return jnp.cos(ang), jnp.sin(ang)

    c64, s64 = cs(64)
    c32, s32 = cs(32)
    cr = jnp.concatenate([c64, c64], axis=1)
    sr = jnp.concatenate([-s64, s64], axis=1)
    ca = jnp.concatenate([c32, c32, c32, c32], axis=1)
    sa = jnp.concatenate([-s32, s32, -s32, s32], axis=1)
    return cr, sr, ca, sa


def _decays():
    H, C = RET_HEADS, RET_CHUNK
    log_gamma = jnp.log1p(-jnp.exp2(-5.0 - jnp.arange(H, dtype=F32)))
    idx = jnp.arange(C, dtype=F32)
    rel = idx[:, None] - idx[None, :]
    dmat = jnp.where(rel[None] >= 0, jnp.exp(log_gamma[:, None, None] * jnp.maximum(rel, 0.0)[None]), 0.0)
    qd = jnp.exp(log_gamma[:, None] * (idx + 1.0))[:, :, None]
    kd = jnp.exp(log_gamma[:, None] * (C - 1.0 - idx))[:, :, None]
    cd = jnp.exp(log_gamma * C)[:, None, None]
    return dmat, qd, kd, cd


def _local_step(place, x, tgt, ln1_g, b_in, ret_norm_g, attn_sinks, ln2_g, lnf_g, w_in, other_weights, tables, decays):
    cr, sr, ca, sa = tables
    dmat, qd, kd, cd = decays
    forward_weights, landed_weights = other_weights

    h, rq, rk, rv, rg, aq, ak, av, ga, gb = _proj_fwd(x, ln1_g, w_in, b_in, cr, sr, ca, sa)
    ry, states = _ret_fwd(rq, rk, rv, dmat, qd, kd, cd)
    token = forward_weights(ry)
    ay = _attn_fwd(aq, ak, av, attn_sinks + token[0, 0])
    w_ret, w_attn, w_out, w_gate, w_up, w_down = landed_weights(ay)
    u, merged, x1 = _merge_fwd(x, ry, rg, ay, ga, gb, ret_norm_g, w_ret, w_attn, w_out)
    gate, up, h2, act, dx2, loss, dg3 = _ffn_fwd(x1, tgt, ln2_g, lnf_g, w_gate, w_up, w_down)

    dgate, dup, dx1, dg2 = _ffn_bwd(dx2, gate, up, x1, ln2_g, w_gate, w_up, w_down)
    early = _wgrad([act], dx2, 6, None, "wgrad_down")
    early = _wgrad([dgate], h2, 4, early, "wgrad_gate")
    early = _wgrad([dup], h2, 5, early, "wgrad_up")
    dgates, drg, dba, dbb, day, dry, dgr = _merge_bwd(dx1, ga, gb, u, ay, rg, ry, ret_norm_g,
                                                                w_ret, w_attn, w_out)
    early = _wgrad([merged], dx1, 3, early, "wgrad_out")
    early = _wgrad([u], dba, 1, early, "wgrad_ret")
    e32, e16 = _wgrad([ay], dbb, 2, early, "wgrad_attn")

    grp = EARLY
    sems, flight, token = _exchange_start("pair_start_early", [e16, _empty(N_CHIPS * grp.half, BF)], N_CHIPS,
                                          _plan_pair_exchange(grp))
    daq, dak, dav, dsk = _attn_bwd(aq, ak, av, day, attn_sinks, ca, sa, token)
    _, got = _exchange_wait("pair_wait_early", sems, flight, _plan_pair_exchange(grp), after=(dsk,))
    s16 = _pair_sum(place, e32, got, grp)
    sems, flight, token = _exchange_start("chips_start_early", [s16, _empty(3 * grp.half, BF)], 3, _plan_to_chips(grp))
    d_ret = _ret_bwd(rq, rk, rv, dry, states, dmat, qd, kd, cd, cr, sr, token)
    d_proj = [d_ret, drg, daq, dak, dav, dgates]
    grad_x, dg1, db = _proj_bwd(d_proj, x, dx1, ln1_g, w_in)
    late = _wgrad(d_proj[:2], h, 0, None, "wgrad_in_a")
    late = _wgrad(d_proj[2:], h, 0, late, "wgrad_in_b", row0=d_ret.shape[1] + drg.shape[1])

    def finish_early(after):
        _, others = _exchange_wait("chips_wait_early", sems, flight, _plan_to_chips(grp), after=after)
        return _chip_sum(place, e32, got, others, grp)

    d_sinks = dsk.reshape(2, 2, PAIRS, 128).sum(-1).transpose(0, 2, 1).reshape(1, ATTN_Q_HEADS)
    small = dict(loss=loss, ln1_g=dg1, b_in=db, ret_norm_g=dgr, attn_sinks=d_sinks,
                 ln2_g=dg2, lnf_g=dg3)
    return grad_x, finish_early, late, small


SMALL_ROWS = 16


def _pack_small(ln1, ret, ln2, lnf, b_in, sinks, loss, name):
    def body(a_ref, b_ref, c_ref, d_ref, bi_ref, sk_ref, ls_ref, o_ref):
        o_ref[...] = jnp.zeros_like(o_ref)
        for row, ref in enumerate((a_ref, b_ref, c_ref, d_ref)):
            o_ref[row:row + 1, :] = ref[...]
        for k in range(6):
            o_ref[4 + k:5 + k, :] = bi_ref[:, k * 1024:(k + 1) * 1024]
        o_ref[10:11, 0:D_IN - 6144] = bi_ref[:, 6144:D_IN]
        o_ref[11:12, 0:ATTN_Q_HEADS] = sk_ref[...]
        o_ref[12:13, 0:LANES] = ls_ref[...]

    return pl.pallas_call(
        body, name=name, in_specs=[_VMEM] * 7, out_specs=_VMEM,
        out_shape=jax.ShapeDtypeStruct((SMALL_ROWS, 1024), F32),
    )(ln1.reshape(1, 1024), ret.reshape(1, 1024), ln2.reshape(1, 1024), lnf.reshape(1, 1024),
      b_in.reshape(1, D_IN), sinks.reshape(1, ATTN_Q_HEADS), loss.reshape(1, LANES))


def _unpack_small(p):
    return dict(ln1_g=p[0:1], ret_norm_g=p[1:2], ln2_g=p[2:3], lnf_g=p[3], b_in=p[4:11].reshape(1, -1)[:, :D_IN],
                attn_sinks=p[11:12, :ATTN_Q_HEADS], loss=p[12, 0])


def kernel(x, ln1_g, w_in, b_in, ret_norm_g, w_ret_out, attn_sinks, w_attn_out, w_out, ln2_g, w_ffn_gate, w_ffn_up, w_ffn_down, lnf_g, loss_target, m_ln1_g, m_w_in, m_b_in, m_ret_norm_g, m_w_ret_out, m_attn_sinks, m_w_attn_out, m_w_out, m_ln2_g, m_w_ffn_gate, m_w_ffn_up, m_w_ffn_down, m_lnf_g, v_ln1_g, v_w_in, v_b_in, v_ret_norm_g, v_w_ret_out, v_attn_sinks, v_w_attn_out, v_w_out, v_ln2_g, v_w_ffn_gate, v_w_ffn_up, v_w_ffn_down, v_lnf_g):
    xi = lax.axis_index("x")
    yi = lax.axis_index("y")
    ci = lax.axis_index("c")
    j = 2 * xi + yi

    place = jnp.stack([j, ci]).astype(jnp.int32)

    names = ["w_in", "w_ret_out", "w_attn_out", "w_out", "w_ffn_gate", "w_ffn_up", "w_ffn_down"]
    transposed = [True, False, False, False, True, True, False]
    view = lambda a, t: a[0].T if t else a[0]
    big_w = [view(a, t) for a, t in zip([w_in, w_ret_out, w_attn_out, w_out, w_ffn_gate, w_ffn_up, w_ffn_down], transposed)]
    big_m = [view(a, t) for a, t in zip([m_w_in, m_w_ret_out, m_w_attn_out, m_w_out, m_w_ffn_gate, m_w_ffn_up, m_w_ffn_down], transposed)]
    big_v = [view(a, t) for a, t in zip([v_w_in, v_w_ret_out, v_w_attn_out, v_w_out, v_w_ffn_gate, v_w_ffn_up, v_w_ffn_down], transposed)]

    staged_in = _stage_weights(place, big_w[:1], "stage_w_in")[0]
    sems, in_flight, token = _exchange_start("gather_in_start", [staged_in], 3, _plan_gather_halves([0]))
    staged = _stage_weights(place, big_w[1:], "stage_others", after=(token,))
    tables = _tables(x.shape[1], token)
    decays = _decays()
    w_in_full = _exchange_wait("gather_in_wait", sems, in_flight, _plan_gather_halves([0]),
                               after=(*staged, *tables, *decays))[0]
    w_in_full = _exchange("gather_in_forward", [w_in_full], 3, _plan_forward_halves([0]))[0]
    rest = list(range(1, 7))
    crossing = _exchange_start("gather_start", staged, 3 * len(rest), _plan_gather_halves(rest), after=(w_in_full,))
    token = crossing[2]
    forwarding = []

    def forward_weights(after):
        arrived = _exchange_wait("gather_wait", crossing[0], crossing[1], _plan_gather_halves(rest), after=(after,))
        forwarding.extend(_exchange_start("forward_start", arrived, 3 * len(rest), _plan_forward_halves(rest)))
        return forwarding[2]

    def landed_weights(after):
        return _exchange_wait("forward_wait", forwarding[0], forwarding[1], _plan_forward_halves(rest), after=(after,))

    other_weights = (forward_weights, landed_weights)

    grad_x, finish_early, (l32, l16), small = _local_step(
        place, x[0], loss_target[0], ln1_g, b_in + token[0, 0], ret_norm_g, attn_sinks, ln2_g, lnf_g.reshape(1, -1),
        w_in_full, other_weights, tables, decays)
    pg = _all_reduce_small(_pack_small(
        small["ln1_g"], small["ret_norm_g"], small["ln2_g"], small["lnf_g"], small["b_in"],
        small["attn_sinks"], small["loss"], "pack_small_grads"), l32)
    early_shard = _exchange("share_early", [finish_early((l32,))], 1, _plan_share_halves(EARLY))[0]
    _, got = _exchange("pair_exchange_late", [l16, _empty(N_CHIPS * LATE.half, BF)], N_CHIPS, _plan_pair_exchange(LATE))
    s16 = _pair_sum(place, l32, got, LATE)
    sems, flight, token = _exchange_start("chips_start_late", [s16, _empty(3 * LATE.half, BF)], 3, _plan_to_chips(LATE),
                                          after=(pg, early_shard))

    upd = {}

    def update(t, shard, after):
        res = _adamw(big_w[t], shard, big_m[t], big_v[t], "adamw_" + names[t], g_row0=GROUP_OF[t].off(t), after=after)
        upd[names[t]] = tuple((r.T if transposed[t] else r)[None] for r in res)
        return res[0]

    done = [update(t, early_shard, token) for t in EARLY.tensors]

    sm = _unpack_small(pg)
    z = jnp.zeros((1, LANES), F32)
    pw = _pack_small(ln1_g, ret_norm_g, ln2_g, lnf_g, b_in, attn_sinks, z, "pack_small_w")
    pm = _pack_small(m_ln1_g, m_ret_norm_g, m_ln2_g, m_lnf_g, m_b_in, m_attn_sinks, z, "pack_small_m")
    pv = _pack_small(v_ln1_g, v_ret_norm_g, v_ln2_g, v_lnf_g, v_b_in, v_attn_sinks, z, "pack_small_v")
    small_res = _adamw(pw, pg, pm, pv, "adamw_small", after=token)
    sd, smn, svn = (_unpack_small(t) for t in small_res[1:])
    for n in ["ln1_g", "b_in", "ret_norm_g", "attn_sinks", "ln2_g", "lnf_g"]:
        upd[n] = (sm[n], sd[n], smn[n], svn[n])

    small_outs = [a for n in ("ln1_g", "b_in", "ret_norm_g", "attn_sinks", "ln2_g", "lnf_g") for a in upd[n]]
    _, others = _exchange_wait("chips_wait_late", sems, flight, _plan_to_chips(LATE), after=(*done, *small_outs))
    late_shard = _exchange("share_late", [_chip_sum(place, l32, got, others, LATE)], 1, _plan_share_halves(LATE))[0]
    update(0, late_shard, None)

    order_w = ["ln1_g", "w_in", "b_in", "ret_norm_g", "w_ret_out", "attn_sinks", "w_attn_out", "w_out",
               "ln2_g", "w_ffn_gate", "w_ffn_up", "w_ffn_down", "lnf_g"]
    outs = [sm["loss"], grad_x[None]]
    for k in range(4):
        outs += [upd[n][k] for n in order_w]
    return tuple(outs)
```

```python
from typing import NamedTuple

import jax
import jax.numpy as jnp
from jax import lax
from jax.experimental import pallas as pl
from jax.experimental.pallas import tpu as pltpu

D_MODEL = 1024
SEQ_CHUNK = 128
RET_CHUNK = 256
RET_CHUNKS_PER_STEP = 4
RET_HEADS = 4
RET_QK_DIM = 128
RET_V_DIM = 256
ATTN_Q_HEADS = 16
ATTN_HEAD_DIM = 64
D_FF = 2816
D_IN = 6400
ROPE_THETA = 10000.0
EPS = 1e-6
N_CHIPS = 4

ADAM_LR = 0.001
ADAM_B1 = 0.9
ADAM_B2 = 0.999
ADAM_EPS = 1e-08
ADAM_WD = 0.01
ADAM_STEP = 10

OFF_RQ, OFF_RK, OFF_RV, OFF_RG = 0, 512, 1024, 2048
OFF_AQ, OFF_AK, OFF_AV, OFF_GA, OFF_GB = 3072, 4096, 4224, 4352, 5376

V7X_VMEM_LIMIT = 56 * 1024 * 1024
LANES = 128

BF = jnp.bfloat16
F32 = jnp.float32
KEPT = jnp.bfloat16
MESH = pl.DeviceIdType.MESH

_VMEM = pl.BlockSpec(memory_space=pltpu.VMEM)
_SMEM = pl.BlockSpec(memory_space=pltpu.SMEM)
_ANY = pl.BlockSpec(memory_space=pl.ANY)


def _dot(a, b):
    return jnp.dot(a, b, preferred_element_type=F32)


def _dot_nt(a, b):
    return lax.dot_general(a, b, (((1,), (1,)), ((), ())), preferred_element_type=F32)


def _dot_tn(a, b):
    return lax.dot_general(a, b, (((0,), (0,)), ((), ())), preferred_element_type=F32)


def _sigmoid(x):
    return 0.5 * jnp.tanh(0.5 * x) + 0.5


def _rows(tm, n):
    return pl.BlockSpec((tm, n), lambda i: (i, 0))


def _fixed(shape):
    nd = len(shape)
    return pl.BlockSpec(shape, lambda i: (0,) * nd)


def _params(sem, **kw):
    return pltpu.CompilerParams(dimension_semantics=sem, vmem_limit_bytes=V7X_VMEM_LIMIT, **kw)


def _rot_ret(p, cr, sr):
    return p * cr + pltpu.roll(p, 64, 1) * sr


def _rot_ret_bwd(d, cr, sr):
    return d * cr - pltpu.roll(d, 64, 1) * sr


def _attn_partner(p):
    lane = lax.broadcasted_iota(jnp.int32, p.shape, 1)
    first = (lane % 64) < 32
    return jnp.where(first, pltpu.roll(p, 96, 1), pltpu.roll(p, 32, 1))


def _rot_attn(p, ca, sa):
    return p * ca + _attn_partner(p) * sa


def _rot_attn_bwd(d, ca, sa):
    return d * ca - _attn_partner(d) * sa


def _proj_fwd(x, g1, w_in, b_in, cr, sr, ca, sa):
    S = x.shape[0]
    tm = 512
    scale = RET_QK_DIM ** -0.5

    def body(x_ref, g_ref, w_ref, b_ref, cr_ref, sr_ref, ca_ref, sa_ref,
             h_ref, rq_ref, rk_ref, rv_ref, rg_ref, aq_ref, ak_ref, av_ref, ga_ref, gb_ref):
        xv = x_ref[...]
        r = lax.rsqrt(jnp.mean(xv * xv, axis=-1, keepdims=True) + EPS)
        h = (xv * r * g_ref[...]).astype(BF)
        h_ref[...] = h

        def piece(lo, n):
            return _dot_nt(h, w_ref[lo:lo + n, :]) + b_ref[:, lo:lo + n]

        crv, srv = cr_ref[...], sr_ref[...]
        pq = piece(OFF_RQ, 512)
        pk = piece(OFF_RK, 512)
        for hd in range(RET_HEADS):
            sl = slice(hd * 128, (hd + 1) * 128)
            rq_ref[:, sl] = (_rot_ret(pq[:, sl], crv, srv) * scale).astype(BF)
            rk_ref[:, sl] = _rot_ret(pk[:, sl], crv, srv)
        rv_ref[...] = piece(OFF_RV, 1024).astype(BF)
        rg_ref[...] = piece(OFF_RG, 1024).astype(KEPT)
        cav, sav = ca_ref[...], sa_ref[...]
        pa = piece(OFF_AQ, 1024)
        for blk in range(8):
            sl = slice(blk * 128, (blk + 1) * 128)
            aq_ref[:, sl] = _rot_attn(pa[:, sl], cav, sav).astype(BF)
        pkv = piece(OFF_AK, 256)
        ak_ref[...] = _rot_attn(pkv[:, :128], cav, sav).astype(BF)
        av_ref[...] = pkv[:, 128:].astype(BF)
        ga_ref[...] = piece(OFF_GA, 1024).astype(KEPT)
        gb_ref[...] = piece(OFF_GB, 1024).astype(KEPT)

    outs = [(D_MODEL, BF), (512, BF), (512, F32), (1024, BF), (1024, KEPT),
            (1024, BF), (128, BF), (128, BF), (1024, KEPT), (1024, KEPT)]
    return pl.pallas_call(
        body, name="proj_fwd", grid=(S // tm,),
        in_specs=[_rows(tm, D_MODEL), _fixed((1, D_MODEL)), _VMEM, _VMEM,
                  _rows(tm, 128), _rows(tm, 128), _rows(tm, 128), _rows(tm, 128)],
        out_specs=[_rows(tm, n) for n, _ in outs],
        out_shape=[jax.ShapeDtypeStruct((S, n), dt) for n, dt in outs],
        compiler_params=_params(("parallel",)),
    )(x, g1, w_in, b_in, cr, sr, ca, sa)


def _ret_fwd(rq, rk, rv, dmat, qd, kd, cd):
    S = rq.shape[0]
    N = S // RET_CHUNK
    cps = RET_CHUNKS_PER_STEP
    rows = cps * RET_CHUNK

    def body(q_ref, k_ref, v_ref, dm_ref, qd_ref, kd_ref, cd_ref, y_ref, st_ref, state):
        @pl.when(pl.program_id(0) == 0)
        def _():
            state[...] = jnp.zeros_like(state)

        for hd in range(RET_HEADS):
            qs = slice(hd * 128, (hd + 1) * 128)
            vs = slice(hd * 256, (hd + 1) * 256)
            free = []
            for ch in range(cps):
                rs = slice(ch * RET_CHUNK, (ch + 1) * RET_CHUNK)
                q = q_ref[rs, qs]
                kf = k_ref[rs, qs]
                v = v_ref[rs, vs]
                a = _dot_nt(q, kf.astype(BF)) * dm_ref[hd]
                kdv = (kf * kd_ref[hd]).astype(BF)
                free.append((rs, q, _dot(a.astype(BF), v), _dot_tn(kdv, v)))
            s = state[hd]
            for ch, (rs, q, inner, update) in enumerate(free):
                sb = s.astype(BF)
                st_ref[ch, hd] = sb
                y_ref[rs, vs] = (inner + _dot(q, sb) * qd_ref[hd]).astype(KEPT)
                s = s * cd_ref[hd] + update
            state[hd] = s

    return pl.pallas_call(
        body, name="ret_fwd", grid=(N // cps,),
        in_specs=[_rows(rows, 512), _rows(rows, 512), _rows(rows, 1024), _VMEM, _VMEM, _VMEM, _VMEM],
        out_specs=[_rows(rows, 1024), pl.BlockSpec((cps, RET_HEADS, 128, 256), lambda i: (i, 0, 0, 0))],
        out_shape=[jax.ShapeDtypeStruct((S, 1024), KEPT),
                   jax.ShapeDtypeStruct((N, RET_HEADS, 128, 256), BF)],
        scratch_shapes=[pltpu.VMEM((RET_HEADS, 128, 256), F32)],
        compiler_params=_params(("arbitrary",)),
    )(rq, rk, rv, dmat, qd, kd, cd)


def _kv_halves(prev_ref, cur_ref):
    kk = jnp.concatenate([prev_ref[...], cur_ref[...]], axis=0).astype(F32)
    return kk, pltpu.roll(kk, 64, 1)


ATTN_FWD_BLOCKS_PER_STEP = 4
PAIRS = 4


def _attn_fold(n):
    t = lax.broadcasted_iota(jnp.int32, (128, PAIRS * 128), 0)
    qi = lax.broadcasted_iota(jnp.int32, (128, PAIRS * 128), 1) & 127
    cur = t <= qi
    cur_f = jnp.where(cur, 1.0, 0.0)
    bias = jnp.where(jnp.logical_or(cur, n > 0), 0.0, -1e30)
    return cur, cur_f, 1.0 - cur_f, bias


def _group_operands(g, kk, kk_sw, vv, vv_sw):
    lo = lax.broadcasted_iota(jnp.int32, (256, 128), 1) < 64
    k_a, k_b = (kk, kk_sw) if g == 0 else (kk_sw, kk)
    v_a, v_b = (vv, vv_sw) if g == 0 else (vv_sw, vv)
    return (jnp.concatenate([jnp.where(lo, k_a, 0.0), jnp.where(lo, 0.0, k_b)], axis=0).astype(BF),
            jnp.concatenate([jnp.where(lo, v_a, 0.0), jnp.where(lo, 0.0, v_b)], axis=0).astype(BF))


def _stack_pairs(ref, g, rows=slice(None)):
    return jnp.concatenate([ref[rows, (4 * g + p) * 128:(4 * g + p + 1) * 128] for p in range(PAIRS)], axis=0)


def _sink_row(sk_ref, g, which):
    return jnp.concatenate([jnp.full((1, 128), sk_ref[0, 8 * g + 2 * p + which], F32) for p in range(PAIRS)], axis=1)


def _fold(full, cur):
    return jnp.where(cur, full[128:], full[:128])


def _attn_probs(raw, fold, sink):
    s = _fold(raw, fold[0]) * (ATTN_HEAD_DIM ** -0.5) + fold[3]
    m = jnp.maximum(jnp.max(s, axis=0, keepdims=True), sink)
    e = jnp.exp(s - m)
    es = jnp.exp(sink - m)
    inv = 1.0 / (jnp.sum(e, axis=0, keepdims=True) + es)
    return e * inv, es * inv


def _unfold(p, fold):
    return jnp.concatenate([p * fold[2], p * fold[1]], axis=0).astype(BF)


def _attn_fwd(aq, ak, av, sinks):
    S = aq.shape[0]
    N = S // SEQ_CHUNK
    nb = ATTN_FWD_BLOCKS_PER_STEP

    def body(sk_ref, q_ref, kp_ref, kc_ref, vp_ref, vc_ref, o_ref):
        kall, kall_sw = _kv_halves(kp_ref, kc_ref)
        vall, vall_sw = _kv_halves(vp_ref, vc_ref)
        blocks = []
        for sb in range(nb):
            kv = slice(sb * 128, sb * 128 + 256)
            rows = slice(sb * 128, (sb + 1) * 128)
            ops = [_group_operands(g, kall[kv], kall_sw[kv], vall[kv], vall_sw[kv]) for g in range(2)]
            raw = [_dot_nt(ops[g][0], _stack_pairs(q_ref, g, rows)) for g in range(2)]
            blocks.append((rows, ops, raw, _attn_fold(nb * pl.program_id(0) + sb)))
        for rows, ops, raw, fold in blocks:
            for g in range(2):
                probs = [_unfold(_attn_probs(raw[g][w * 256:(w + 1) * 256], fold, _sink_row(sk_ref, g, w))[0], fold)
                         for w in range(2)]
                o = _dot_tn(jnp.concatenate(probs, axis=0), ops[g][1])
                for p in range(PAIRS):
                    o_ref[rows, (4 * g + p) * 128:(4 * g + p + 1) * 128] = o[p * 128:(p + 1) * 128].astype(BF)

    prev = lambda i: (jnp.maximum(nb * i - 1, 0), 0)
    return pl.pallas_call(
        body, name="attn_fwd", grid=(N // nb,),
        in_specs=[_SMEM, _rows(nb * 128, 1024), pl.BlockSpec((128, 128), prev), _rows(nb * 128, 128),
                  pl.BlockSpec((128, 128), prev), _rows(nb * 128, 128)],
        out_specs=_rows(nb * 128, 1024),
        out_shape=jax.ShapeDtypeStruct((S, 1024), BF),
        compiler_params=_params(("parallel",)),
    )(sinks, aq, ak, ak, av, av)


def _group_norm(y):
    mu = jnp.mean(y, axis=-1, keepdims=True)
    yc = y - mu
    rstd = lax.rsqrt(jnp.mean(yc * yc, axis=-1, keepdims=True) + EPS)
    return yc * rstd, rstd


def _merge_fwd(x, ry, rg, ay, ga, gb, g_ret, w_ret, w_attn, w_out):
    S = x.shape[0]
    tm = 512

    def body(x_ref, ry_ref, rg_ref, ay_ref, ga_ref, gb_ref, gr_ref, wr_ref, wa_ref, wo_ref,
             u_ref, mg_ref, x1_ref):
        bb = _dot(ay_ref[...], wa_ref[...])
        for hd in range(RET_HEADS):
            vs = slice(hd * 256, (hd + 1) * 256)
            yn, _ = _group_norm(ry_ref[:, vs].astype(F32))
            rgv = rg_ref[:, vs].astype(F32)
            u_ref[:, vs] = (rgv * _sigmoid(rgv) * (yn * gr_ref[:, vs])).astype(BF)
        gated_b = _sigmoid(gb_ref[...].astype(F32)) * bb
        sa = _sigmoid(ga_ref[...].astype(F32))
        ba = _dot(u_ref[...], wr_ref[...])
        merged = (sa * ba + gated_b).astype(BF)
        mg_ref[...] = merged
        x1_ref[...] = x_ref[...] + _dot(merged, wo_ref[...])

    outs = [BF, BF, F32]
    return pl.pallas_call(
        body, name="merge_fwd", grid=(S // tm,),
        in_specs=[_rows(tm, 1024)] * 6 + [_fixed((1, 1024)), _VMEM, _VMEM, _VMEM],
        out_specs=[_rows(tm, 1024)] * 3,
        out_shape=[jax.ShapeDtypeStruct((S, 1024), dt) for dt in outs],
        compiler_params=_params(("parallel",)),
    )(x, ry, rg, ay, ga, gb, g_ret, w_ret, w_attn, w_out)


def _ffn_fwd(x1, tgt, g2, g3, w_gate, w_up, w_down):
    S = x1.shape[0]
    tm = 256

    def body(x1_ref, t_ref, g2_ref, g3_ref, wg_ref, wu_ref, wd_ref,
             gate_ref, up_ref, h2_ref, act_ref, dx2_ref, loss_ref, dg3_ref):
        @pl.when(pl.program_id(0) == 0)
        def _():
            loss_ref[...] = jnp.zeros_like(loss_ref)
            dg3_ref[...] = jnp.zeros_like(dg3_ref)

        x1v = x1_ref[...]
        r2 = lax.rsqrt(jnp.mean(x1v * x1v, axis=-1, keepdims=True) + EPS)
        h2 = (x1v * r2 * g2_ref[...]).astype(BF)
        h2_ref[...] = h2
        gate = _dot_nt(h2, wg_ref[...])
        up = _dot_nt(h2, wu_ref[...])
        gate_ref[...] = gate
        up_ref[...] = up
        act = (gate * _sigmoid(gate) * up).astype(BF)
        act_ref[...] = act
        x2 = x1v + _dot(act, wd_ref[...])
        r3 = lax.rsqrt(jnp.mean(x2 * x2, axis=-1, keepdims=True) + EPS)
        xh = x2 * r3
        g3v = g3_ref[...]
        err = xh * g3v - t_ref[...]
        loss_ref[...] += 0.5 * jnp.sum(jnp.mean(err * err, axis=-1, keepdims=True), axis=0, keepdims=True)
        dy = err * (1.0 / D_MODEL)
        dg3_ref[...] += jnp.sum(dy * xh, axis=0, keepdims=True)
        dxh = dy * g3v
        dx2_ref[...] = r3 * (dxh - xh * jnp.mean(dxh * xh, axis=-1, keepdims=True))

    return pl.pallas_call(
        body, name="ffn_fwd", grid=(S // tm,),
        in_specs=[_rows(tm, 1024), _rows(tm, 1024), _fixed((1, 1024)), _fixed((1, 1024)), _VMEM, _VMEM, _VMEM],
        out_specs=[_rows(tm, D_FF), _rows(tm, D_FF), _rows(tm, 1024), _rows(tm, D_FF), _rows(tm, 1024),
                   _fixed((1, LANES)), _fixed((1, 1024))],
        out_shape=[jax.ShapeDtypeStruct((S, D_FF), F32), jax.ShapeDtypeStruct((S, D_FF), F32),
                   jax.ShapeDtypeStruct((S, 1024), BF), jax.ShapeDtypeStruct((S, D_FF), BF),
                   jax.ShapeDtypeStruct((S, 1024), F32),
                   jax.ShapeDtypeStruct((1, LANES), F32), jax.ShapeDtypeStruct((1, 1024), F32)],
        compiler_params=_params(("arbitrary",)),
    )(x1, tgt, g2, g3, w_gate, w_up, w_down)


def _ffn_bwd(dx2, gate, up, x1, g2, w_gate, w_up, w_down):
    S = x1.shape[0]
    tm = 256

    def body(dx2_ref, gate_ref, up_ref, x1_ref, g2_ref, wg_ref, wu_ref, wd_ref,
             dgate_ref, dup_ref, dx1_ref, dg2_ref):
        @pl.when(pl.program_id(0) == 0)
        def _():
            dg2_ref[...] = jnp.zeros_like(dg2_ref)

        dx2v = dx2_ref[...]
        dact = _dot_nt(dx2v.astype(BF), wd_ref[...])
        gate = gate_ref[...]
        sg = _sigmoid(gate)
        dgate = (dact * up_ref[...] * (sg * (1.0 + gate * (1.0 - sg)))).astype(BF)
        dup = (dact * (gate * sg)).astype(BF)
        dgate_ref[...] = dgate
        dup_ref[...] = dup
        dh2 = _dot(dgate, wg_ref[...]) + _dot(dup, wu_ref[...])
        x1v = x1_ref[...]
        r2 = lax.rsqrt(jnp.mean(x1v * x1v, axis=-1, keepdims=True) + EPS)
        xh = x1v * r2
        dg2_ref[...] += jnp.sum(dh2 * xh, axis=0, keepdims=True)
        dxh = dh2 * g2_ref[...]
        dx1_ref[...] = dx2v + r2 * (dxh - xh * jnp.mean(dxh * xh, axis=-1, keepdims=True))

    return pl.pallas_call(
        body, name="ffn_bwd", grid=(S // tm,),
        in_specs=[_rows(tm, 1024), _rows(tm, D_FF), _rows(tm, D_FF), _rows(tm, 1024), _fixed((1, 1024)),
                  _VMEM, _VMEM, _VMEM],
        out_specs=[_rows(tm, D_FF), _rows(tm, D_FF), _rows(tm, 1024), _fixed((1, 1024))],
        out_shape=[jax.ShapeDtypeStruct((S, D_FF), BF), jax.ShapeDtypeStruct((S, D_FF), BF),
                   jax.ShapeDtypeStruct((S, 1024), F32), jax.ShapeDtypeStruct((1, 1024), F32)],
        compiler_params=_params(("arbitrary",)),
    )(dx2, gate, up, x1, g2, w_gate, w_up, w_down)


def _merge_bwd(dx1, ga, gb, u, ay, rg, ry, g_ret, w_ret, w_attn, w_out):
    S = dx1.shape[0]
    tm = 512

    def body(dx1_ref, ga_ref, gb_ref, u_ref, ay_ref, rg_ref, ry_ref, gr_ref, wr_ref, wa_ref, wo_ref,
             dgates_ref, drg_ref, dba_ref, dbb_ref, day_ref, dry_ref, dgr_ref):
        @pl.when(pl.program_id(0) == 0)
        def _():
            dgr_ref[...] = jnp.zeros_like(dgr_ref)

        dm = _dot_nt(dx1_ref[...].astype(BF), wo_ref[...])
        ba = _dot(u_ref[...], wr_ref[...])
        bb = _dot(ay_ref[...], wa_ref[...])
        sa = _sigmoid(ga_ref[...].astype(F32))
        sb = _sigmoid(gb_ref[...].astype(F32))
        dga = (dm * ba * (sa * (1.0 - sa))).astype(BF)
        dgb = (dm * bb * (sb * (1.0 - sb))).astype(BF)
        dgates_ref[:, :1024] = dga
        dgates_ref[:, 1024:] = dgb
        dba = (dm * sa).astype(BF)
        dbb = (dm * sb).astype(BF)
        dba_ref[...] = dba
        dbb_ref[...] = dbb
        day_ref[...] = _dot_nt(dbb, wa_ref[...]).astype(BF)
        du = _dot_nt(dba, wr_ref[...])
        for hd in range(RET_HEADS):
            vs = slice(hd * 256, (hd + 1) * 256)
            yn, rstd = _group_norm(ry_ref[:, vs].astype(F32))
            rgv = rg_ref[:, vs].astype(F32)
            sr = _sigmoid(rgv)
            silu = rgv * sr
            duv = du[:, vs]
            grv = gr_ref[:, vs]
            drg = (duv * (yn * grv) * (sr * (1.0 + rgv * (1.0 - sr)))).astype(BF)
            drg_ref[:, vs] = drg
            dyg = duv * silu
            dgr_ref[:, vs] += jnp.sum(dyg * yn, axis=0, keepdims=True)
            dyn = dyg * grv
            dry_ref[:, vs] = (rstd * (dyn - jnp.mean(dyn, axis=-1, keepdims=True)
                                      - yn * jnp.mean(dyn * yn, axis=-1, keepdims=True))).astype(KEPT)

    return pl.pallas_call(
        body, name="merge_bwd", grid=(S // tm,),
        in_specs=[_rows(tm, 1024)] * 7 + [_fixed((1, 1024)), _VMEM, _VMEM, _VMEM],
        out_specs=[_rows(tm, 2048), _rows(tm, 1024), _rows(tm, 1024), _rows(tm, 1024), _rows(tm, 1024),
                   _rows(tm, 1024), _fixed((1, 1024))],
        out_shape=[jax.ShapeDtypeStruct((S, 2048), BF), jax.ShapeDtypeStruct((S, 1024), BF),
                   jax.ShapeDtypeStruct((S, 1024), BF), jax.ShapeDtypeStruct((S, 1024), BF),
                   jax.ShapeDtypeStruct((S, 1024), BF), jax.ShapeDtypeStruct((S, 1024), KEPT),
                   jax.ShapeDtypeStruct((1, 1024), F32)],
        compiler_params=_params(("arbitrary",)),
    )(dx1, ga, gb, u, ay, rg, ry, g_ret, w_ret, w_attn, w_out)


def _attn_bwd(aq, ak, av, day, sinks, ca, sa, after):
    S = aq.shape[0]
    N = S // SEQ_CHUNK
    steps = N // 2

    def body(sk_ref, q_ref, kp_ref, kc_ref, vp_ref, vc_ref, do_ref, cq_ref, sq_ref, cp_ref, sp_ref, after_ref,
             dq_ref, dke_ref, dve_ref, dko_ref, dvo_ref, dsk_ref, carry_k, carry_v):
        i = pl.program_id(0)

        @pl.when(i == 0)
        def _():
            carry_k[...] = jnp.zeros_like(carry_k)
            carry_v[...] = jnp.zeros_like(carry_v)
            dsk_ref[...] = jnp.zeros_like(dsk_ref)

        @pl.when(i < steps)
        def _():
            kall, kall_sw = _kv_halves(kp_ref, kc_ref)
            vall, vall_sw = _kv_halves(vp_ref, vc_ref)
            lo = lax.broadcasted_iota(jnp.int32, (256, 128), 1) < 64
            blocks = []
            for sb in range(2):
                kv = slice(sb * 128, sb * 128 + 256)
                rows = slice(sb * 128, (sb + 1) * 128)
                ops = [_group_operands(g, kall[kv], kall_sw[kv], vall[kv], vall_sw[kv]) for g in range(2)]
                stacked = [(_stack_pairs(q_ref, g, rows), _stack_pairs(do_ref, g, rows)) for g in range(2)]
                raw = [_dot_nt(ops[g][0], stacked[g][0]) for g in range(2)]
                raw_dp = [_dot_nt(ops[g][1], stacked[g][1]) for g in range(2)]
                blocks.append((rows, ops, stacked, raw, raw_dp, _attn_fold(2 * i + sb)))
            grads = []
            d_sink = [jnp.zeros((1, PAIRS * 128), F32) for _ in range(4)]
            for rows, ops, stacked, raw, raw_dp, fold in blocks:
                cur = fold[0]
                cq, sq = cq_ref[rows, :], sq_ref[rows, :]
                dk_groups, dv_groups = [], []
                for g in range(2):
                    qs, dos = stacked[g]
                    ds_parts, p_parts = [], []
                    for which in range(2):
                        half = slice(which * 256, (which + 1) * 256)
                        pr, ps = _attn_probs(raw[g][half], fold, _sink_row(sk_ref, g, which))
                        dp = _fold(raw_dp[g][half], cur)
                        dr = jnp.sum(pr * dp, axis=0, keepdims=True)
                        ds_parts.append(_unfold(pr * (dp - dr) * (ATTN_HEAD_DIM ** -0.5), fold))
                        p_parts.append(_unfold(pr, fold))
                        d_sink[2 * g + which] = d_sink[2 * g + which] - ps * dr
                    ds = jnp.concatenate(ds_parts, axis=0)
                    dq = _dot_tn(ds, ops[g][0])
                    dk_both = _dot(ds, qs)
                    dv_both = _dot(jnp.concatenate(p_parts, axis=0), dos)
                    for p in range(PAIRS):
                        sl = slice((4 * g + p) * 128, (4 * g + p + 1) * 128)
                        dq_ref[rows, sl] = _rot_attn_bwd(dq[p * 128:(p + 1) * 128], cq, sq).astype(BF)
                    tk = jnp.where(lo, dk_both[:256], dk_both[256:])
                    tv = jnp.where(lo, dv_both[:256], dv_both[256:])
                    dk_groups.append(tk + pltpu.roll(tk, 64, 1))
                    dv_groups.append(tv + pltpu.roll(tv, 64, 1))
                grads.append((jnp.where(lo, dk_groups[0], dk_groups[1]), jnp.where(lo, dv_groups[0], dv_groups[1])))
            dsk_ref[...] += jnp.concatenate(d_sink, axis=0)
            (dk_a, dv_a), (dk_b, dv_b) = grads
            dko_ref[...] = _rot_attn_bwd(carry_k[...] + dk_a[:128], cp_ref[...], sp_ref[...]).astype(BF)
            dvo_ref[...] = (carry_v[...] + dv_a[:128]).astype(BF)
            dke_ref[...] = _rot_attn_bwd(dk_a[128:] + dk_b[:128], cq_ref[0:128, :], sq_ref[0:128, :]).astype(BF)
            dve_ref[...] = (dv_a[128:] + dv_b[:128]).astype(BF)
            carry_k[...] = dk_b[128:]
            carry_v[...] = dv_b[128:]

        @pl.when(i == steps)
        def _():
            dko_ref[...] = _rot_attn_bwd(carry_k[...], cp_ref[...], sp_ref[...]).astype(BF)
            dvo_ref[...] = carry_v[...].astype(BF)

    two = lambda i: (jnp.minimum(i, steps - 1), 0)
    before = lambda i: (jnp.maximum(2 * i - 1, 0), 0)
    odd = lambda i: (jnp.maximum(i - 1, 0), 0)
    blk = lambda r, w, im: pl.BlockSpec((r, w), im)
    half = jax.ShapeDtypeStruct((S // 2, 128), BF)
    dq, dk_even, dv_even, dk_odd, dv_odd, dsk = pl.pallas_call(
        body, name="attn_bwd", grid=(steps + 1,),
        in_specs=[_SMEM, blk(256, 1024, two), blk(128, 128, before), blk(256, 128, two), blk(128, 128, before),
                  blk(256, 128, two), blk(256, 1024, two), blk(256, 128, two), blk(256, 128, two),
                  blk(128, 128, before), blk(128, 128, before), _ANY],
        out_specs=[blk(256, 1024, two), blk(128, 128, two), blk(128, 128, two), blk(128, 128, odd),
                   blk(128, 128, odd), _fixed((4, PAIRS * 128))],
        out_shape=[jax.ShapeDtypeStruct((S, 1024), BF), half, half, half, half,
                   jax.ShapeDtypeStruct((4, PAIRS * 128), F32)],
        scratch_shapes=[pltpu.VMEM((128, 128), F32), pltpu.VMEM((128, 128), F32)],
        compiler_params=_params(("arbitrary",)),
    )(sinks, aq, ak, ak, av, av, day, ca, sa, ca, sa, after)

    def interleave(even, odd_):
        return jnp.stack([even.reshape(steps, 128, 128), odd_.reshape(steps, 128, 128)], axis=1).reshape(S, 128)

    return dq, interleave(dk_even, dk_odd), interleave(dv_even, dv_odd), dsk


def _ret_bwd(rq, rk, rv, dry, states, dmat, qd, kd, cd, cr, sr, after):
    S = rq.shape[0]
    N = S // RET_CHUNK
    scale = RET_QK_DIM ** -0.5
    cps = RET_CHUNKS_PER_STEP
    steps = N // cps

    def body(q_ref, k_ref, v_ref, dy_ref, st_ref, dm_ref, qd_ref, kd_ref, cd_ref, cr_ref, sr_ref, after_ref,
             out_ref, dstate):
        @pl.when(pl.program_id(0) == 0)
        def _():
            dstate[...] = jnp.zeros_like(dstate)

        for hd in range(RET_HEADS):
            qs = slice(hd * 128, (hd + 1) * 128)
            vs = slice(hd * 256, (hd + 1) * 256)
            dmv = dm_ref[hd]
            free = []
            for ch in range(cps):
                rs = slice(ch * RET_CHUNK, (ch + 1) * RET_CHUNK)
                crv, srv = cr_ref[rs, :], sr_ref[rs, :]
                q = q_ref[rs, qs]
                kf = k_ref[rs, qs]
                k = kf.astype(BF)
                v = v_ref[rs, vs]
                do = dy_ref[rs, vs].astype(F32)
                dob = do.astype(BF)
                doq = (do * qd_ref[hd]).astype(BF)
                a = (_dot_nt(q, k) * dmv).astype(BF)
                da = (_dot_nt(dob, v) * dmv).astype(BF)
                dq = _dot(da, k) + _dot_nt(doq, st_ref[ch, hd])
                out_ref[rs, qs] = _rot_ret_bwd(dq * scale, crv, srv).astype(BF)
                free.append((rs, crv, srv, v, (kf * kd_ref[hd]).astype(BF),
                             _dot_tn(a, dob), _dot_tn(da, q), _dot_tn(q, doq)))
            ds = dstate[hd]
            for rs, crv, srv, v, kdv, dv_free, dk_free, ds_update in reversed(free):
                dsb = ds.astype(BF)
                dv = dv_free + _dot(kdv, dsb)
                dk = dk_free + _dot_nt(v, dsb) * kd_ref[hd]
                ds = ds * cd_ref[hd] + ds_update
                out_ref[rs, 512 + hd * 128:512 + (hd + 1) * 128] = _rot_ret_bwd(dk, crv, srv).astype(BF)
                out_ref[rs, 1024 + hd * 256:1024 + (hd + 1) * 256] = dv.astype(BF)
            dstate[hd] = ds

    rev = lambda w: pl.BlockSpec((cps * RET_CHUNK, w), lambda i: (steps - 1 - i, 0))
    return pl.pallas_call(
        body, name="ret_bwd", grid=(steps,),
        in_specs=[rev(512), rev(512), rev(1024), rev(1024),
                  pl.BlockSpec((cps, RET_HEADS, 128, 256), lambda i: (steps - 1 - i, 0, 0, 0)),
                  _VMEM, _VMEM, _VMEM, _VMEM, rev(128), rev(128), _ANY],
        out_specs=rev(2048),
        out_shape=jax.ShapeDtypeStruct((S, 2048), BF),
        scratch_shapes=[pltpu.VMEM((RET_HEADS, 128, 256), F32)],
        compiler_params=_params(("arbitrary",)),
    )(rq, rk, rv, dry, states, dmat, qd, kd, cd, cr, sr, after)


def _proj_bwd(d_pieces, x, dx1, g1, w_in_t):
    S = x.shape[0]
    tm = 512
    widths = [d.shape[1] for d in d_pieces]
    assert sum(widths) == D_IN
    nd = len(d_pieces)

    def body(*refs):
        d_refs = refs[:nd]
        x_ref, dx1_ref, g_ref, w_ref, gx_ref, dg_ref, db_ref = refs[nd:]

        @pl.when(pl.program_id(0) == 0)
        def _():
            dg_ref[...] = jnp.zeros_like(dg_ref)
            db_ref[...] = jnp.zeros_like(db_ref)

        dh = jnp.zeros((tm, D_MODEL), F32)
        lo = 0
        for ref, n in zip(d_refs, widths):
            v = ref[...]
            dh = dh + _dot(v, w_ref[lo:lo + n, :])
            db_ref[:, lo:lo + n] += jnp.sum(v.astype(F32), axis=0, keepdims=True)
            lo += n
        xv = x_ref[...]
        r = lax.rsqrt(jnp.mean(xv * xv, axis=-1, keepdims=True) + EPS)
        xh = xv * r
        dg_ref[...] += jnp.sum(dh * xh, axis=0, keepdims=True)
        dxh = dh * g_ref[...]
        gx_ref[...] = dx1_ref[...] + r * (dxh - xh * jnp.mean(dxh * xh, axis=-1, keepdims=True))

    return pl.pallas_call(
        body, name="proj_bwd", grid=(S // tm,),
        in_specs=[_rows(tm, n) for n in widths] + [_rows(tm, 1024), _rows(tm, 1024), _fixed((1, 1024)), _VMEM],
        out_specs=[_rows(tm, 1024), _fixed((1, 1024)), _fixed((1, D_IN))],
        out_shape=[jax.ShapeDtypeStruct((S, 1024), F32), jax.ShapeDtypeStruct((1, 1024), F32),
                   jax.ShapeDtypeStruct((1, D_IN), F32)],
        compiler_params=_params(("arbitrary",)),
    )(*d_pieces, x, dx1, g1, w_in_t)


SHARD_ROWS = (1600, 256, 256, 256, 704, 704, 704)


class Group(NamedTuple):
    name: str
    tensors: tuple
    offs: tuple
    block: int
    tile: int

    @property
    def half(self):
        return self.block // 2

    def off(self, t):
        return self.offs[self.tensors.index(t)]


LATE = Group("late", (0,), (0,), 1600, 800)
EARLY = Group("early", (4, 5, 6, 1, 2, 3), (0, 704, 1408, 2112, 2368, 2624), 2880, 480)
GROUP_OF = {t: g for g in (LATE, EARLY) for t in g.tensors}


def _wgrad(a_list, b, t, packed, name):
    S = b.shape[0]
    widths = [a.shape[1] for a in a_list]
    M = sum(widths)
    ts = 1024 if M <= 4096 else 512
    steps = S // ts
    na = len(a_list)
    grp = GROUP_OF[t]
    r, off = SHARD_ROWS[t], grp.off(t)
    assert M == N_CHIPS * r
    segments = [(chip * r, r, chip * grp.block + off) for chip in range(N_CHIPS)]
    stage_rows = r
    n_in = na + 1 + (2 if packed is not None else 0)

    def body(*refs):
        a_refs, b_ref = refs[:na], refs[na]
        g32_ref, gb_ref, acc, stage, sems = refs[n_in:]
        i = pl.program_id(0)

        @pl.when(i == 0)
        def _():
            acc[...] = jnp.zeros_like(acc)

        bv = b_ref[...].astype(BF)
        lo = 0
        for a_ref, n in zip(a_refs, widths):
            acc[lo:lo + n, :] += _dot_tn(a_ref[...].astype(BF), bv)
            lo += n

        @pl.when(i == steps - 1)
        def _():
            for a0, n, p0 in segments:
                c32 = pltpu.make_async_copy(acc.at[a0:a0 + n, :], g32_ref.at[pl.ds(p0, n), :], sems.at[0])
                c32.start()
                stage[0:n, :] = acc[a0:a0 + n, :].astype(BF)
                c16 = pltpu.make_async_copy(stage.at[0:n, :], gb_ref.at[pl.ds(p0, n), :], sems.at[1])
                c16.start()
                c32.wait()
                c16.wait()

    shapes = [jax.ShapeDtypeStruct((N_CHIPS * grp.block, 1024), F32),
              jax.ShapeDtypeStruct((N_CHIPS * grp.block, 1024), BF)]
    operands = list(a_list) + [b]
    in_specs = [_rows(ts, n) for n in widths] + [_rows(ts, 1024)]
    aliases = {}
    if packed is not None:
        operands += list(packed)
        in_specs += [_ANY, _ANY]
        aliases = {na + 1: 0, na + 2: 1}
    return pl.pallas_call(
        body, name=name, grid=(steps,),
        in_specs=in_specs, out_specs=[_ANY, _ANY], out_shape=shapes,
        scratch_shapes=[pltpu.VMEM((M, 1024), F32), pltpu.VMEM((stage_rows, 1024), BF),
                        pltpu.SemaphoreType.DMA((2,))],
        input_output_aliases=aliases,
        compiler_params=_params(("arbitrary",)),
    )(*operands)


def _stage_weights(place, ws, name, after=()):
    nw = len(ws)

    def body(p_ref, *refs):
        for w_ref, o_ref in zip(refs[:nw], refs[nw + len(after):]):
            o_ref[...] = w_ref[...].astype(BF)

    return pl.pallas_call(
        body, name=name,
        grid_spec=pltpu.PrefetchScalarGridSpec(
            num_scalar_prefetch=1, grid=(1,),
            in_specs=[pl.BlockSpec(w.shape, lambda i, p: (0, 0)) for w in ws] + [_ANY] * len(after),
            out_specs=[pl.BlockSpec(w.shape, lambda i, p: (p[0], 0)) for w in ws]),
        out_shape=[jax.ShapeDtypeStruct((N_CHIPS * w.shape[0], 1024), BF) for w in ws],
        compiler_params=_params(("arbitrary",)),
    )(place, *ws, *after)


def _kept_block(grp, nb):
    return lambda i, p: ((2 * (i // nb) + p[1]) * nb + i % nb, 0)


def _pair_sum(place, g32, got, grp):
    nb = grp.half // grp.tile

    def body(p_ref, g_ref, r_ref, s16_ref):
        s16_ref[...] = (g_ref[...] + r_ref[...].astype(F32)).astype(BF)

    flat = pl.BlockSpec((grp.tile, 1024), lambda i, p: (i, 0))
    return pl.pallas_call(
        body, name="pair_sum_" + grp.name,
        grid_spec=pltpu.PrefetchScalarGridSpec(
            num_scalar_prefetch=1, grid=(N_CHIPS * nb,),
            in_specs=[pl.BlockSpec((grp.tile, 1024), _kept_block(grp, nb)), flat], out_specs=flat),
        out_shape=jax.ShapeDtypeStruct((N_CHIPS * grp.half, 1024), BF),
        compiler_params=_params(("arbitrary",)),
    )(place, g32, got)


def _chip_sum(place, g32, got, others, grp):
    nb = grp.half // grp.tile

    def body(p_ref, g_ref, r_ref, r0_ref, r1_ref, r2_ref, o_ref):
        own = g_ref[...] + r_ref[...].astype(F32)
        o_ref[...] = ((own + r0_ref[...].astype(F32)) + r1_ref[...].astype(F32)) + r2_ref[...].astype(F32)

    blk = lambda f: pl.BlockSpec((grp.tile, 1024), f)
    return pl.pallas_call(
        body, name="chip_sum_" + grp.name,
        grid_spec=pltpu.PrefetchScalarGridSpec(
            num_scalar_prefetch=1, grid=(nb,),
            in_specs=[blk(lambda i, p: ((2 * p[0] + p[1]) * nb + i, 0)), blk(lambda i, p: (p[0] * nb + i, 0)),
                      blk(lambda i, p: (i, 0)), blk(lambda i, p: (nb + i, 0)), blk(lambda i, p: (2 * nb + i, 0))],
            out_specs=blk(lambda i, p: (p[1] * nb + i, 0))),
        out_shape=jax.ShapeDtypeStruct((grp.block, 1024), F32),
        compiler_params=_params(("arbitrary",)),
    )(place, g32, got, others, others, others)


def _adamw(w, g, m, v, name, g_row0=0, after=None):
    r = w.shape[0]
    tr = next((c for c in (352, 320, 64) if r % c == 0 and g_row0 % c == 0), r)
    assert g_row0 % tr == 0
    g0 = g_row0 // tr
    extra = [] if after is None else [after]

    def body(w_ref, g_ref, m_ref, v_ref, *rest):
        go_ref, d_ref, mo_ref, vo_ref = rest[len(extra):]
        gv = g_ref[...]
        mn = ADAM_B1 * m_ref[...] + (1.0 - ADAM_B1) * gv
        vn = ADAM_B2 * v_ref[...] + (1.0 - ADAM_B2) * (gv * gv)
        m_hat = mn / (1.0 - ADAM_B1 ** ADAM_STEP)
        v_hat = vn / (1.0 - ADAM_B2 ** ADAM_STEP)
        go_ref[...] = gv
        d_ref[...] = -ADAM_LR * (m_hat / (jnp.sqrt(v_hat) + ADAM_EPS) + ADAM_WD * w_ref[...])
        mo_ref[...] = mn
        vo_ref[...] = vn

    return pl.pallas_call(
        body, name=name, grid=(r // tr,),
        in_specs=[_rows(tr, 1024), pl.BlockSpec((tr, 1024), lambda i: (g0 + i, 0)), _rows(tr, 1024), _rows(tr, 1024)]
        + [_ANY] * len(extra),
        out_specs=[_rows(tr, 1024)] * 4,
        out_shape=[jax.ShapeDtypeStruct((r, 1024), F32)] * 4,
        compiler_params=_params(("parallel",)),
    )(w, g, m, v, *extra)


def _place():
    x, y, c = lax.axis_index("x"), lax.axis_index("y"), lax.axis_index("c")
    chips = [(1 - x, y), (x, 1 - y), (1 - x, 1 - y)]
    return x, y, c, chips


_HBM =pl.BlockSpec(memory_space=pltpu.HBM)
_SEM = pl.BlockSpec(memory_space=pltpu.SEMAPHORE)
_DATAFLOW = pltpu.SideEffectType.DATAFLOW_SIDE_EFFECTING


def _exchange(name, arrays, n, plan):
    na = len(arrays)

    def body(*refs):
        pairs = plan(refs[na:2 * na], refs[2 * na], refs[2 * na + 1])
        for send, _ in pairs:
            send.start()
        for send, recv in pairs:
            send.wait_send()
            recv.wait_recv()

    return pl.pallas_call(
        body, name=name, in_specs=[_ANY] * na, out_specs=[_ANY] * na,
        out_shape=[jax.ShapeDtypeStruct(a.shape, a.dtype) for a in arrays],
        scratch_shapes=[pltpu.SemaphoreType.DMA((n,)), pltpu.SemaphoreType.DMA((n,))],
        input_output_aliases={i: i for i in range(na)},
    )(*arrays)


def _exchange_start(name, arrays, n, plan, after=()):
    na = len(arrays)

    def body(*refs):
        for send, _ in plan(refs[:na], refs[na + len(after)], refs[na + len(after) + 1]):
            send.start()
        refs[-1][...] = jnp.zeros_like(refs[-1])

    out = pl.pallas_call(
        body, name=name,
        out_shape=(pltpu.SemaphoreType.DMA((n,)), pltpu.SemaphoreType.DMA((n,)),
                   *[pltpu.HBM(a.shape, a.dtype) for a in arrays], jax.ShapeDtypeStruct((8, LANES), F32)),
        in_specs=[_HBM] * na + [_ANY] * len(after), out_specs=(_SEM, _SEM, *[_HBM] * na, _VMEM),
        input_output_aliases={i: 2 + i for i in range(na)},
        compiler_params=pltpu.CompilerParams(has_side_effects=_DATAFLOW),
    )(*[pltpu.with_memory_space_constraint(a, pltpu.HBM) for a in arrays], *after)
    return (out[0], out[1]), list(out[2:2 + na]), out[-1]


def _exchange_wait(name, sems, in_flight, plan, after=()):
    na = len(in_flight)

    def body(*refs):
        for send, recv in plan(refs[:na], refs[na], refs[na + 1]):
            send.wait_send()
            recv.wait_recv()

    out = pl.pallas_call(
        body, name=name,
        out_shape=tuple(pltpu.HBM(a.shape, a.dtype) for a in in_flight),
        in_specs=[_HBM] * na + [_SEM, _SEM] + [_ANY] * len(after), out_specs=tuple([_HBM] * na),
        input_output_aliases={i: i for i in range(na)},
        compiler_params=pltpu.CompilerParams(has_side_effects=_DATAFLOW),
    )(*in_flight, *sems, *after)
    return list(out)


def _remote(src, dst, send_sems, recv_sems, k, to):
    return pltpu.make_async_remote_copy(src_ref=src, dst_ref=dst, send_sem=send_sems.at[k], recv_sem=recv_sems.at[k],
                                        device_id=to, device_id_type=MESH)


def _half_rows(ref, tid, pj, half):
    h = SHARD_ROWS[tid] // 2
    return ref.at[pl.ds((2 * pj + half) * h, h), :]


def _plan_gather_halves(tids):
    def plan(refs, ss, rs):
        x, y, c, chips = _place()
        pairs = []
        for t, tid in enumerate(tids):
            mine = _half_rows(refs[t], tid, 2 * x + y, c)
            for k, (cx, cy) in enumerate(chips):
                theirs = _half_rows(refs[t], tid, 2 * cx + cy, c)
                pairs.append((_remote(mine, mine, ss, rs, 3 * t + k, (cx, cy, c)),
                              _remote(theirs, theirs, ss, rs, 3 * t + k, (x, y, c))))
        return pairs
    return plan


def _plan_forward_halves(tids):
    def plan(refs, ss, rs):
        x, y, c, chips = _place()
        pairs = []
        for t, tid in enumerate(tids):
            for k, (cx, cy) in enumerate(chips):
                got = _half_rows(refs[t], tid, 2 * cx + cy, c)
                missing = _half_rows(refs[t], tid, 2 * cx + cy, 1 - c)
                pairs.append((_remote(got, got, ss, rs, 3 * t + k, (x, y, 1 - c)),
                              _remote(missing, missing, ss, rs, 3 * t + k, (x, y, 1 - c))))
        return pairs
    return plan


def _plan_pair_exchange(grp):
    def plan(refs, ss, rs):
        x, y, c, _ = _place()
        cps = [_remote(refs[0].at[pl.ds(pj * grp.block + (1 - c) * grp.half, grp.half), :],
                       refs[1].at[pl.ds(pj * grp.half, grp.half), :], ss, rs, pj, (x, y, 1 - c))
               for pj in range(N_CHIPS)]
        return [(cp, cp) for cp in cps]
    return plan


def _plan_to_chips(grp):
    def plan(refs, ss, rs):
        x, y, c, chips = _place()
        cps = [_remote(refs[0].at[pl.ds((2 * cx + cy) * grp.half, grp.half), :],
                       refs[1].at[pl.ds(k * grp.half, grp.half), :], ss, rs, k, (cx, cy, c))
               for k, (cx, cy) in enumerate(chips)]
        return [(cp, cp) for cp in cps]
    return plan


def _plan_share_halves(grp):
    def plan(refs, ss, rs):
        x, y, c, _ = _place()
        mine = refs[0].at[pl.ds(c * grp.half, grp.half), :]
        other = refs[0].at[pl.ds((1 - c) * grp.half, grp.half), :]
        return [(_remote(mine, mine, ss, rs, 0, (x, y, 1 - c)), _remote(other, other, ss, rs, 0, (x, y, 1 - c)))]
    return plan


def _empty(rows, dtype):
    return lax.empty((rows, 1024), dtype)


def _all_reduce_small(v, after):
    m_per, n = v.shape
    n_dev = 8

    def body(x_ref, after_ref, out_ref, all_ref, send_sems, recv_sems, local_sem):
        x, y, c, chips = _place()
        me, sibling = (x, y, c), (x, y, 1 - c)

        def rows(px, py, pc):
            return all_ref.at[pl.ds((4 * px + 2 * py + pc) * m_per, m_per), :]

        def copy(k, block, to, src=None):
            return pltpu.make_async_remote_copy(
                src_ref=rows(*block) if src is None else src, dst_ref=rows(*block),
                send_sem=send_sems.at[k], recv_sem=recv_sems.at[k], device_id=to, device_id_type=MESH)

        mine = pltpu.make_async_copy(x_ref, rows(*me), local_sem)
        mine.start()
        first = [copy(0, me, sibling, src=x_ref)]
        first += [copy(1 + k, me, (*chip, c), src=x_ref) for k, chip in enumerate(chips)]
        for cp in first:
            cp.start()
        passed = [copy(4 + k, (*chip, c), sibling) for k, chip in enumerate(chips)]
        for k, chip in enumerate(chips):
            copy(1 + k, (*chip, c), me).wait_recv()
            passed[k].start()
        copy(0, sibling, me).wait_recv()
        for k, chip in enumerate(chips):
            copy(4 + k, (*chip, 1 - c), me).wait_recv()
        for cp in first + passed:
            cp.wait_send()
        mine.wait()
        acc = all_ref[pl.ds(0, m_per), :]
        for d in range(1, n_dev):
            acc = acc + all_ref[pl.ds(d * m_per, m_per), :]
        out_ref[...] = acc

    return pl.pallas_call(
        body, name="all_reduce_small",
        in_specs=[_VMEM, _ANY], out_specs=_VMEM,
        out_shape=jax.ShapeDtypeStruct((m_per, n), F32),
        scratch_shapes=[pltpu.VMEM((n_dev * m_per, n), F32),
                        pltpu.SemaphoreType.DMA((7,)), pltpu.SemaphoreType.DMA((7,)), pltpu.SemaphoreType.DMA],
        compiler_params=pltpu.CompilerParams(has_side_effects=True),
    )(v, after)


def _tables(S, token):
    pos = jnp.arange(S, dtype=jnp.int32).astype(F32) + token[0, 0]

    def cs(half):
        inv_freq = ROPE_THETA ** (-jnp.arange(half, dtype=F32) / half)
        ang = pos[:, None] * inv_freq[None, :]
        return jnp.cos(ang), jnp.sin(ang)

    c64, s64 = cs(64)
    c32, s32 = cs(32)
    cr = jnp.concatenate([c64, c64], axis=1)
    sr = jnp.concatenate([-s64, s64], axis=1)
    ca = jnp.concatenate([c32, c32, c32, c32], axis=1)
    sa = jnp.concatenate([-s32, s32, -s32, s32], axis=1)
    return cr, sr, ca, sa


def _decays():
    H, C = RET_HEADS, RET_CHUNK
    log_gamma = jnp.log1p(-jnp.exp2(-5.0 - jnp.arange(H, dtype=F32)))
    idx = jnp.arange(C, dtype=F32)
    rel = idx[:, None] - idx[None, :]
    dmat = jnp.where(rel[None] >= 0, jnp.exp(log_gamma[:, None, None] * jnp.maximum(rel, 0.0)[None]), 0.0)
    qd = jnp.exp(log_gamma[:, None] * (idx + 1.0))[:, :, None]
    kd = jnp.exp(log_gamma[:, None] * (C - 1.0 - idx))[:, :, None]
    cd = jnp.exp(log_gamma * C)[:, None, None]
    return dmat, qd, kd, cd


def _local_step(place, x, tgt, ln1_g, b_in, ret_norm_g, attn_sinks, ln2_g, lnf_g, w_in, other_weights, tables, decays):
    cr, sr, ca, sa = tables
    dmat, qd, kd, cd = decays
    forward_weights, landed_weights = other_weights

    h, rq, rk, rv, rg, aq, ak, av, ga, gb = _proj_fwd(x, ln1_g, w_in, b_in, cr, sr, ca, sa)
    ry, states = _ret_fwd(rq, rk, rv, dmat, qd, kd, cd)
    token = forward_weights(ry)
    ay = _attn_fwd(aq, ak, av, attn_sinks + token[0, 0])
    w_ret, w_attn, w_out, w_gate, w_up, w_down = landed_weights(ay)
    u, merged, x1 = _merge_fwd(x, ry, rg, ay, ga, gb, ret_norm_g, w_ret, w_attn, w_out)
    gate, up, h2, act, dx2, loss, dg3 = _ffn_fwd(x1, tgt, ln2_g, lnf_g, w_gate, w_up, w_down)

    dgate, dup, dx1, dg2 = _ffn_bwd(dx2, gate, up, x1, ln2_g, w_gate, w_up, w_down)
    early = _wgrad([act], dx2, 6, None, "wgrad_down")
    early = _wgrad([dgate], h2, 4, early, "wgrad_gate")
    early = _wgrad([dup], h2, 5, early, "wgrad_up")
    dgates, drg, dba, dbb, day, dry, dgr = _merge_bwd(dx1, ga, gb, u, ay, rg, ry, ret_norm_g,
                                                                w_ret, w_attn, w_out)
    early = _wgrad([merged], dx1, 3, early, "wgrad_out")
    early = _wgrad([u], dba, 1, early, "wgrad_ret")
    e32, e16 = _wgrad([ay], dbb, 2, early, "wgrad_attn")

    grp = EARLY
    sems, flight, token = _exchange_start("pair_start_early", [e16, _empty(N_CHIPS * grp.half, BF)], N_CHIPS,
                                          _plan_pair_exchange(grp))
    daq, dak, dav, dsk = _attn_bwd(aq, ak, av, day, attn_sinks, ca, sa, token)
    _, got = _exchange_wait("pair_wait_early", sems, flight, _plan_pair_exchange(grp), after=(dsk,))
    s16 = _pair_sum(place, e32, got, grp)
    sems, flight, token = _exchange_start("chips_start_early", [s16, _empty(3 * grp.half, BF)], 3, _plan_to_chips(grp))
    d_ret = _ret_bwd(rq, rk, rv, dry, states, dmat, qd, kd, cd, cr, sr, token)
    d_proj = [d_ret, drg, daq, dak, dav, dgates]
    grad_x, dg1, db = _proj_bwd(d_proj, x, dx1, ln1_g, w_in)
    late = _wgrad(d_proj, h, 0, None, "wgrad_in")

    def finish_early(after):
        _, others = _exchange_wait("chips_wait_early", sems, flight, _plan_to_chips(grp), after=after)
        return _chip_sum(place, e32, got, others, grp)

    d_sinks = dsk.reshape(2, 2, PAIRS, 128).sum(-1).transpose(0, 2, 1).reshape(1, ATTN_Q_HEADS)
    small = dict(loss=loss, ln1_g=dg1, b_in=db, ret_norm_g=dgr, attn_sinks=d_sinks,
                 ln2_g=dg2, lnf_g=dg3)
    return grad_x, finish_early, late, small


SMALL_ROWS = 16


def _pack_small(ln1, ret, ln2, lnf, b_in, sinks, loss, name):
    def body(a_ref, b_ref, c_ref, d_ref, bi_ref, sk_ref, ls_ref, o_ref):
        o_ref[...] = jnp.zeros_like(o_ref)
        for row, ref in enumerate((a_ref, b_ref, c_ref, d_ref)):
            o_ref[row:row + 1, :] = ref[...]
        for k in range(6):
            o_ref[4 + k:5 + k, :] = bi_ref[:, k * 1024:(k + 1) * 1024]
        o_ref[10:11, 0:D_IN - 6144] = bi_ref[:, 6144:D_IN]
        o_ref[11:12, 0:ATTN_Q_HEADS] = sk_ref[...]
        o_ref[12:13, 0:LANES] = ls_ref[...]

    return pl.pallas_call(
        body, name=name, in_specs=[_VMEM] * 7, out_specs=_VMEM,
        out_shape=jax.ShapeDtypeStruct((SMALL_ROWS, 1024), F32),
    )(ln1.reshape(1, 1024), ret.reshape(1, 1024), ln2.reshape(1, 1024), lnf.reshape(1, 1024),
      b_in.reshape(1, D_IN), sinks.reshape(1, ATTN_Q_HEADS), loss.reshape(1, LANES))


def _unpack_small(p):
    return dict(ln1_g=p[0:1], ret_norm_g=p[1:2], ln2_g=p[2:3], lnf_g=p[3], b_in=p[4:11].reshape(1, -1)[:, :D_IN],
                attn_sinks=p[11:12, :ATTN_Q_HEADS], loss=p[12, 0])


def kernel(x, ln1_g, w_in, b_in, ret_norm_g, w_ret_out, attn_sinks, w_attn_out, w_out, ln2_g, w_ffn_gate, w_ffn_up, w_ffn_down, lnf_g, loss_target, m_ln1_g, m_w_in, m_b_in, m_ret_norm_g, m_w_ret_out, m_attn_sinks, m_w_attn_out, m_w_out, m_ln2_g, m_w_ffn_gate, m_w_ffn_up, m_w_ffn_down, m_lnf_g, v_ln1_g, v_w_in, v_b_in, v_ret_norm_g, v_w_ret_out, v_attn_sinks, v_w_attn_out, v_w_out, v_ln2_g, v_w_ffn_gate, v_w_ffn_up, v_w_ffn_down, v_lnf_g):
    xi = lax.axis_index("x")
    yi = lax.axis_index("y")
    ci = lax.axis_index("c")
    j = 2 * xi + yi

    place = jnp.stack([j, ci]).astype(jnp.int32)

    names = ["w_in", "w_ret_out", "w_attn_out", "w_out", "w_ffn_gate", "w_ffn_up", "w_ffn_down"]
    transposed = [True, False, False, False, True, True, False]
    view = lambda a, t: a[0].T if t else a[0]
    big_w = [view(a, t) for a, t in zip([w_in, w_ret_out, w_attn_out, w_out, w_ffn_gate, w_ffn_up, w_ffn_down], transposed)]
    big_m = [view(a, t) for a, t in zip([m_w_in, m_w_ret_out, m_w_attn_out, m_w_out, m_w_ffn_gate, m_w_ffn_up, m_w_ffn_down], transposed)]
    big_v = [view(a, t) for a, t in zip([v_w_in, v_w_ret_out, v_w_attn_out, v_w_out, v_w_ffn_gate, v_w_ffn_up, v_w_ffn_down], transposed)]

    staged_in = _stage_weights(place, big_w[:1], "stage_w_in")[0]
    sems, in_flight, token = _exchange_start("gather_in_start", [staged_in], 3, _plan_gather_halves([0]))
    staged = _stage_weights(place, big_w[1:], "stage_others", after=(token,))
    tables = _tables(x.shape[1], token)
    decays = _decays()
    w_in_full = _exchange_wait("gather_in_wait", sems, in_flight, _plan_gather_halves([0]),
                               after=(*staged, *tables, *decays))[0]
    w_in_full = _exchange("gather_in_forward", [w_in_full], 3, _plan_forward_halves([0]))[0]
    rest = list(range(1, 7))
    crossing = _exchange_start("gather_start", staged, 3 * len(rest), _plan_gather_halves(rest), after=(w_in_full,))
    token = crossing[2]
    forwarding = []

    def forward_weights(after):
        arrived = _exchange_wait("gather_wait", crossing[0], crossing[1], _plan_gather_halves(rest), after=(after,))
        forwarding.extend(_exchange_start("forward_start", arrived, 3 * len(rest), _plan_forward_halves(rest)))
        return forwarding[2]

    def landed_weights(after):
        return _exchange_wait("forward_wait", forwarding[0], forwarding[1], _plan_forward_halves(rest), after=(after,))

    other_weights = (forward_weights, landed_weights)

    grad_x, finish_early, (l32, l16), small = _local_step(
        place, x[0], loss_target[0], ln1_g, b_in + token[0, 0], ret_norm_g, attn_sinks, ln2_g, lnf_g.reshape(1, -1),
        w_in_full, other_weights, tables, decays)
    pg = _all_reduce_small(_pack_small(
        small["ln1_g"], small["ret_norm_g"], small["ln2_g"], small["lnf_g"], small["b_in"],
        small["attn_sinks"], small["loss"], "pack_small_grads"), l32)
    early_shard = _exchange("share_early", [finish_early((l32,))], 1, _plan_share_halves(EARLY))[0]
    _, got = _exchange("pair_exchange_late", [l16, _empty(N_CHIPS * LATE.half, BF)], N_CHIPS, _plan_pair_exchange(LATE))
    s16 = _pair_sum(place, l32, got, LATE)
    sems, flight, token = _exchange_start("chips_start_late", [s16, _empty(3 * LATE.half, BF)], 3, _plan_to_chips(LATE),
                                          after=(pg, early_shard))

    upd = {}

    def update(t, shard, after):
        res = _adamw(big_w[t], shard, big_m[t], big_v[t], "adamw_" + names[t], g_row0=GROUP_OF[t].off(t), after=after)
        upd[names[t]] = tuple((r.T if transposed[t] else r)[None] for r in res)
        return res[0]

    done = [update(t, early_shard, token) for t in EARLY.tensors]

    sm = _unpack_small(pg)
    z = jnp.zeros((1, LANES), F32)
    pw = _pack_small(ln1_g, ret_norm_g, ln2_g, lnf_g, b_in, attn_sinks, z, "pack_small_w")
    pm = _pack_small(m_ln1_g, m_ret_norm_g, m_ln2_g, m_lnf_g, m_b_in, m_attn_sinks, z, "pack_small_m")
    pv = _pack_small(v_ln1_g, v_ret_norm_g, v_ln2_g, v_lnf_g, v_b_in, v_attn_sinks, z, "pack_small_v")
    small_res = _adamw(pw, pg, pm, pv, "adamw_small", after=token)
    sd, smn, svn = (_unpack_small(t) for t in small_res[1:])
    for n in ["ln1_g", "b_in", "ret_norm_g", "attn_sinks", "ln2_g", "lnf_g"]:
        upd[n] = (sm[n], sd[n], smn[n], svn[n])

    small_outs = [a for n in ("ln1_g", "b_in", "ret_norm_g", "attn_sinks", "ln2_g", "lnf_g") for a in upd[n]]
    _, others = _exchange_wait("chips_wait_late", sems, flight, _plan_to_chips(LATE), after=(*done, *small_outs))
    late_shard = _exchange("share_late", [_chip_sum(place, l32, got, others, LATE)], 1, _plan_share_halves(LATE))[0]
    update(0, late_shard, None)

    order_w = ["ln1_g", "w_in", "b_in", "ret_norm_g", "w_ret_out", "attn_sinks", "w_attn_out", "w_out",
               "ln2_g", "w_ffn_gate", "w_ffn_up", "w_ffn_down", "lnf_g"]
    outs = [sm["loss"], grad_x[None]]
    for k in range(4):
        outs += [upd[n][k] for n in order_w]
    return tuple(outs)
```

```python
from typing import NamedTuple

import jax
import jax.numpy as jnp
from jax import lax
from jax.experimental import pallas as pl
from jax.experimental.pallas import tpu as pltpu

D_MODEL = 1024
SEQ_CHUNK = 128
RET_CHUNK = 256
RET_CHUNKS_PER_STEP = 4
RET_HEADS = 4
RET_QK_DIM = 128
ATTN_Q_HEADS = 16
ATTN_HEAD_DIM = 64
D_FF = 2816
D_IN = 6400
ROPE_THETA = 10000.0
EPS = 1e-6
N_CHIPS = 4

ADAM_LR = 0.001
ADAM_B1 = 0.9
ADAM_B2 = 0.999
ADAM_EPS = 1e-08
ADAM_WD = 0.01
ADAM_STEP = 10

OFF_RQ, OFF_RK, OFF_RV, OFF_RG = 0, 512, 1024, 2048
OFF_AQ, OFF_AK, OFF_AV, OFF_GA, OFF_GB = 3072, 4096, 4224, 4352, 5376

V7X_VMEM_LIMIT = 56 * 1024 * 1024
LANES = 128

BF = jnp.bfloat16
F32 = jnp.float32
KEPT = jnp.bfloat16
MESH = pl.DeviceIdType.MESH

_VMEM = pl.BlockSpec(memory_space=pltpu.VMEM)
_SMEM = pl.BlockSpec(memory_space=pltpu.SMEM)
_ANY = pl.BlockSpec(memory_space=pl.ANY)


def _dot(a, b):
    return jnp.dot(a, b, preferred_element_type=F32)


def _dot_nt(a, b):
    return lax.dot_general(a, b, (((1,), (1,)), ((), ())), preferred_element_type=F32)


def _dot_tn(a, b):
    return lax.dot_general(a, b, (((0,), (0,)), ((), ())), preferred_element_type=F32)


def _sigmoid(x):
    return 0.5 * jnp.tanh(0.5 * x) + 0.5


def _rows(tm, n):
    return pl.BlockSpec((tm, n), lambda i: (i, 0))


def _fixed(shape):
    nd = len(shape)
    return pl.BlockSpec(shape, lambda i: (0,) * nd)


def _params(sem, **kw):
    return pltpu.CompilerParams(dimension_semantics=sem, vmem_limit_bytes=V7X_VMEM_LIMIT, **kw)


def _rot_ret(p, cr, sr):
    return p * cr + pltpu.roll(p, 64, 1) * sr


def _rot_ret_bwd(d, cr, sr):
    return d * cr - pltpu.roll(d, 64, 1) * sr


def _attn_partner(p):
    lane = lax.broadcasted_iota(jnp.int32, p.shape, 1)
    first = (lane % 64) < 32
    return jnp.where(first, pltpu.roll(p, 96, 1), pltpu.roll(p, 32, 1))


def _rot_attn(p, ca, sa):
    return p * ca + _attn_partner(p) * sa


def _rot_attn_bwd(d, ca, sa):
    return d * ca - _attn_partner(d) * sa


def _proj_fwd(x, g1, w_in, b_in, cr, sr, ca, sa):
    S = x.shape[0]
    tm = 512
    scale = RET_QK_DIM ** -0.5

    def body(x_ref, g_ref, w_ref, b_ref, cr_ref, sr_ref, ca_ref, sa_ref,
             h_ref, rq_ref, rk_ref, rv_ref, rg_ref, aq_ref, ak_ref, av_ref, ga_ref, gb_ref):
        xv = x_ref[...]
        r = lax.rsqrt(jnp.mean(xv * xv, axis=-1, keepdims=True) + EPS)
        h = (xv * r * g_ref[...]).astype(BF)
        h_ref[...] = h

        def piece(lo, n):
            return _dot_nt(h, w_ref[lo:lo + n, :]) + b_ref[:, lo:lo + n]

        crv, srv = cr_ref[...], sr_ref[...]
        pq = piece(OFF_RQ, 512)
        pk = piece(OFF_RK, 512)
        for hd in range(RET_HEADS):
            sl = slice(hd * 128, (hd + 1) * 128)
            rq_ref[:, sl] = (_rot_ret(pq[:, sl], crv, srv) * scale).astype(BF)
            rk_ref[:, sl] = _rot_ret(pk[:, sl], crv, srv)
        rv_ref[...] = piece(OFF_RV, 1024).astype(BF)
        rg_ref[...] = piece(OFF_RG, 1024).astype(KEPT)
        cav, sav = ca_ref[...], sa_ref[...]
        pa = piece(OFF_AQ, 1024)
        for blk in range(8):
            sl = slice(blk * 128, (blk + 1) * 128)
            aq_ref[:, sl] = _rot_attn(pa[:, sl], cav, sav).astype(BF)
        pkv = piece(OFF_AK, 256)
        ak_ref[...] = _rot_attn(pkv[:, :128], cav, sav).astype(BF)
        av_ref[...] = pkv[:, 128:].astype(BF)
        ga_ref[...] = piece(OFF_GA, 1024).astype(KEPT)
        gb_ref[...] = piece(OFF_GB, 1024).astype(KEPT)

    outs = [(D_MODEL, BF), (512, BF), (512, F32), (1024, BF), (1024, KEPT),
            (1024, BF), (128, BF), (128, BF), (1024, KEPT), (1024, KEPT)]
    return pl.pallas_call(
        body, name="proj_fwd", grid=(S // tm,),
        in_specs=[_rows(tm, D_MODEL), _fixed((1, D_MODEL)), _VMEM, _VMEM,
                  _rows(tm, 128), _rows(tm, 128), _rows(tm, 128), _rows(tm, 128)],
        out_specs=[_rows(tm, n) for n, _ in outs],
        out_shape=[jax.ShapeDtypeStruct((S, n), dt) for n, dt in outs],
        compiler_params=_params(("parallel",)),
    )(x, g1, w_in, b_in, cr, sr, ca, sa)


def _ret_fwd(rq, rk, rv, dmat, qd, kd, cd):
    S = rq.shape[0]
    N = S // RET_CHUNK
    cps = RET_CHUNKS_PER_STEP
    rows = cps * RET_CHUNK

    def body(q_ref, k_ref, v_ref, dm_ref, qd_ref, kd_ref, cd_ref, y_ref, st_ref, state):
        @pl.when(pl.program_id(0) == 0)
        def _():
            state[...] = jnp.zeros_like(state)

        for hd in range(RET_HEADS):
            qs = slice(hd * 128, (hd + 1) * 128)
            vs = slice(hd * 256, (hd + 1) * 256)
            free = []
            for ch in range(cps):
                rs = slice(ch * RET_CHUNK, (ch + 1) * RET_CHUNK)
                q = q_ref[rs, qs]
                kf = k_ref[rs, qs]
                v = v_ref[rs, vs]
                a = _dot_nt(q, kf.astype(BF)) * dm_ref[hd]
                kdv = (kf * kd_ref[hd]).astype(BF)
                free.append((rs, q, _dot(a.astype(BF), v), _dot_tn(kdv, v)))
            s = state[hd]
            for ch, (rs, q, inner, update) in enumerate(free):
                sb = s.astype(BF)
                st_ref[ch, hd] = sb
                y_ref[rs, vs] = (inner + _dot(q, sb) * qd_ref[hd]).astype(KEPT)
                s = s * cd_ref[hd] + update
            state[hd] = s

    return pl.pallas_call(
        body, name="ret_fwd", grid=(N // cps,),
        in_specs=[_rows(rows, 512), _rows(rows, 512), _rows(rows, 1024), _VMEM, _VMEM, _VMEM, _VMEM],
        out_specs=[_rows(rows, 1024), pl.BlockSpec((cps, RET_HEADS, 128, 256), lambda i: (i, 0, 0, 0))],
        out_shape=[jax.ShapeDtypeStruct((S, 1024), KEPT),
                   jax.ShapeDtypeStruct((N, RET_HEADS, 128, 256), BF)],
        scratch_shapes=[pltpu.VMEM((RET_HEADS, 128, 256), F32)],
        compiler_params=_params(("arbitrary",)),
    )(rq, rk, rv, dmat, qd, kd, cd)


def _kv_halves(prev_ref, cur_ref):
    kk = jnp.concatenate([prev_ref[...], cur_ref[...]], axis=0).astype(F32)
    return kk, pltpu.roll(kk, 64, 1)


ATTN_FWD_BLOCKS_PER_STEP = 4
PAIRS = 4


def _attn_fold(n):
    t = lax.broadcasted_iota(jnp.int32, (128, PAIRS * 128), 0)
    qi = lax.broadcasted_iota(jnp.int32, (128, PAIRS * 128), 1) & 127
    cur = t <= qi
    cur_f = jnp.where(cur, 1.0, 0.0)
    bias = jnp.where(jnp.logical_or(cur, n > 0), 0.0, -1e30)
    return cur, cur_f, 1.0 - cur_f, bias


def _group_operands(g, kk, kk_sw, vv, vv_sw):
    lo = lax.broadcasted_iota(jnp.int32, (256, 128), 1) < 64
    k_a, k_b = (kk, kk_sw) if g == 0 else (kk_sw, kk)
    v_a, v_b = (vv, vv_sw) if g == 0 else (vv_sw, vv)
    return (jnp.concatenate([jnp.where(lo, k_a, 0.0), jnp.where(lo, 0.0, k_b)], axis=0).astype(BF),
            jnp.concatenate([jnp.where(lo, v_a, 0.0), jnp.where(lo, 0.0, v_b)], axis=0).astype(BF))


def _stack_pairs(ref, g, rows=slice(None)):
    return jnp.concatenate([ref[rows, (4 * g + p) * 128:(4 * g + p + 1) * 128] for p in range(PAIRS)], axis=0)


def _sink_row(sk_ref, g, which):
    return jnp.concatenate([jnp.full((1, 128), sk_ref[0, 8 * g + 2 * p + which], F32) for p in range(PAIRS)], axis=1)


def _fold(full, cur):
    return jnp.where(cur, full[128:], full[:128])


def _attn_probs(raw, fold, sink):
    s = _fold(raw, fold[0]) * (ATTN_HEAD_DIM ** -0.5) + fold[3]
    m = jnp.maximum(jnp.max(s, axis=0, keepdims=True), sink)
    e = jnp.exp(s - m)
    es = jnp.exp(sink - m)
    inv = 1.0 / (jnp.sum(e, axis=0, keepdims=True) + es)
    return e * inv, es * inv


def _unfold(p, fold):
    return jnp.concatenate([p * fold[2], p * fold[1]], axis=0).astype(BF)


def _attn_fwd(aq, ak, av, sinks):
    S = aq.shape[0]
    N = S // SEQ_CHUNK
    nb = ATTN_FWD_BLOCKS_PER_STEP

    def body(sk_ref, q_ref, kp_ref, kc_ref, vp_ref, vc_ref, o_ref):
        kall, kall_sw = _kv_halves(kp_ref, kc_ref)
        vall, vall_sw = _kv_halves(vp_ref, vc_ref)
        blocks = []
        for sb in range(nb):
            kv = slice(sb * 128, sb * 128 + 256)
            rows = slice(sb * 128, (sb + 1) * 128)
            ops = [_group_operands(g, kall[kv], kall_sw[kv], vall[kv], vall_sw[kv]) for g in range(2)]
            raw = [_dot_nt(ops[g][0], _stack_pairs(q_ref, g, rows)) for g in range(2)]
            blocks.append((rows, ops, raw, _attn_fold(nb * pl.program_id(0) + sb)))
        for rows, ops, raw, fold in blocks:
            for g in range(2):
                probs = [_unfold(_attn_probs(raw[g][w * 256:(w + 1) * 256], fold, _sink_row(sk_ref, g, w))[0], fold)
                         for w in range(2)]
                o = _dot_tn(jnp.concatenate(probs, axis=0), ops[g][1])
                for p in range(PAIRS):
                    o_ref[rows, (4 * g + p) * 128:(4 * g + p + 1) * 128] = o[p * 128:(p + 1) * 128].astype(BF)

    prev = lambda i: (jnp.maximum(nb * i - 1, 0), 0)
    return pl.pallas_call(
        body, name="attn_fwd", grid=(N // nb,),
        in_specs=[_SMEM, _rows(nb * 128, 1024), pl.BlockSpec((128, 128), prev), _rows(nb * 128, 128),
                  pl.BlockSpec((128, 128), prev), _rows(nb * 128, 128)],
        out_specs=_rows(nb * 128, 1024),
        out_shape=jax.ShapeDtypeStruct((S, 1024), BF),
        compiler_params=_params(("parallel",)),
    )(sinks, aq, ak, ak, av, av)


def _group_norm(y):
    mu = jnp.mean(y, axis=-1, keepdims=True)
    yc = y - mu
    rstd = lax.rsqrt(jnp.mean(yc * yc, axis=-1, keepdims=True) + EPS)
    return yc * rstd, rstd


def _merge_fwd(x, ry, rg, ay, ga, gb, g_ret, w_ret, w_attn, w_out):
    S = x.shape[0]
    tm = 512

    def body(x_ref, ry_ref, rg_ref, ay_ref, ga_ref, gb_ref, gr_ref, wr_ref, wa_ref, wo_ref,
             u_ref, mg_ref, x1_ref):
        bb = _dot(ay_ref[...], wa_ref[...])
        for hd in range(RET_HEADS):
            vs = slice(hd * 256, (hd + 1) * 256)
            yn, _ = _group_norm(ry_ref[:, vs].astype(F32))
            rgv = rg_ref[:, vs].astype(F32)
            u_ref[:, vs] = (rgv * _sigmoid(rgv) * (yn * gr_ref[:, vs])).astype(BF)
        gated_b = _sigmoid(gb_ref[...].astype(F32)) * bb
        sa = _sigmoid(ga_ref[...].astype(F32))
        ba = _dot(u_ref[...], wr_ref[...])
        merged = (sa * ba + gated_b).astype(BF)
        mg_ref[...] = merged
        x1_ref[...] = x_ref[...] + _dot(merged, wo_ref[...])

    outs = [BF, BF, F32]
    return pl.pallas_call(
        body, name="merge_fwd", grid=(S // tm,),
        in_specs=[_rows(tm, 1024)] * 6 + [_fixed((1, 1024)), _VMEM, _VMEM, _VMEM],
        out_specs=[_rows(tm, 1024)] * 3,
        out_shape=[jax.ShapeDtypeStruct((S, 1024), dt) for dt in outs],
        compiler_params=_params(("parallel",)),
    )(x, ry, rg, ay, ga, gb, g_ret, w_ret, w_attn, w_out)


def _ffn_fwd(x1, tgt, g2, g3, w_gate, w_up, w_down):
    S = x1.shape[0]
    tm = 256

    def body(x1_ref, t_ref, g2_ref, g3_ref, wg_ref, wu_ref, wd_ref,
             gate_ref, up_ref, h2_ref, act_ref, dx2_ref, loss_ref, dg3_ref):
        @pl.when(pl.program_id(0) == 0)
        def _():
            loss_ref[...] = jnp.zeros_like(loss_ref)
            dg3_ref[...] = jnp.zeros_like(dg3_ref)

        x1v = x1_ref[...]
        r2 = lax.rsqrt(jnp.mean(x1v * x1v, axis=-1, keepdims=True) + EPS)
        h2 = (x1v * r2 * g2_ref[...]).astype(BF)
        h2_ref[...] = h2
        gate = _dot_nt(h2, wg_ref[...])
        up = _dot_nt(h2, wu_ref[...])
        gate_ref[...] = gate
        up_ref[...] = up
        act = (gate * _sigmoid(gate) * up).astype(BF)
        act_ref[...] = act
        x2 = x1v + _dot(act, wd_ref[...])
        r3 = lax.rsqrt(jnp.mean(x2 * x2, axis=-1, keepdims=True) + EPS)
        xh = x2 * r3
        g3v = g3_ref[...]
        err = xh * g3v - t_ref[...]
        loss_ref[...] += 0.5 * jnp.sum(jnp.mean(err * err, axis=-1, keepdims=True), axis=0, keepdims=True)
        dy = err * (1.0 / D_MODEL)
        dg3_ref[...] += jnp.sum(dy * xh, axis=0, keepdims=True)
        dxh = dy * g3v
        dx2_ref[...] = r3 * (dxh - xh * jnp.mean(dxh * xh, axis=-1, keepdims=True))

    return pl.pallas_call(
        body, name="ffn_fwd", grid=(S // tm,),
        in_specs=[_rows(tm, 1024), _rows(tm, 1024), _fixed((1, 1024)), _fixed((1, 1024)), _VMEM, _VMEM, _VMEM],
        out_specs=[_rows(tm, D_FF), _rows(tm, D_FF), _rows(tm, 1024), _rows(tm, D_FF), _rows(tm, 1024),
                   _fixed((1, LANES)), _fixed((1, 1024))],
        out_shape=[jax.ShapeDtypeStruct((S, D_FF), F32), jax.ShapeDtypeStruct((S, D_FF), F32),
                   jax.ShapeDtypeStruct((S, 1024), BF), jax.ShapeDtypeStruct((S, D_FF), BF),
                   jax.ShapeDtypeStruct((S, 1024), F32),
                   jax.ShapeDtypeStruct((1, LANES), F32), jax.ShapeDtypeStruct((1, 1024), F32)],
        compiler_params=_params(("arbitrary",)),
    )(x1, tgt, g2, g3, w_gate, w_up, w_down)


def _ffn_bwd(dx2, gate, up, x1, g2, w_gate, w_up, w_down):
    S = x1.shape[0]
    tm = 256

    def body(dx2_ref, gate_ref, up_ref, x1_ref, g2_ref, wg_ref, wu_ref, wd_ref,
             dgate_ref, dup_ref, dx1_ref, dg2_ref):
        @pl.when(pl.program_id(0) == 0)
        def _():
            dg2_ref[...] = jnp.zeros_like(dg2_ref)

        dx2v = dx2_ref[...]
        dact = _dot_nt(dx2v.astype(BF), wd_ref[...])
        gate = gate_ref[...]
        sg = _sigmoid(gate)
        dgate = (dact * up_ref[...] * (sg * (1.0 + gate * (1.0 - sg)))).astype(BF)
        dup = (dact * (gate * sg)).astype(BF)
        dgate_ref[...] = dgate
        dup_ref[...] = dup
        dh2 = _dot(dgate, wg_ref[...]) + _dot(dup, wu_ref[...])
        x1v = x1_ref[...]
        r2 = lax.rsqrt(jnp.mean(x1v * x1v, axis=-1, keepdims=True) + EPS)
        xh = x1v * r2
        dg2_ref[...] += jnp.sum(dh2 * xh, axis=0, keepdims=True)
        dxh = dh2 * g2_ref[...]
        dx1_ref[...] = dx2v + r2 * (dxh - xh * jnp.mean(dxh * xh, axis=-1, keepdims=True))

    return pl.pallas_call(
        body, name="ffn_bwd", grid=(S // tm,),
        in_specs=[_rows(tm, 1024), _rows(tm, D_FF), _rows(tm, D_FF), _rows(tm, 1024), _fixed((1, 1024)),
                  _VMEM, _VMEM, _VMEM],
        out_specs=[_rows(tm, D_FF), _rows(tm, D_FF), _rows(tm, 1024), _fixed((1, 1024))],
        out_shape=[jax.ShapeDtypeStruct((S, D_FF), BF), jax.ShapeDtypeStruct((S, D_FF), BF),
                   jax.ShapeDtypeStruct((S, 1024), F32), jax.ShapeDtypeStruct((1, 1024), F32)],
        compiler_params=_params(("arbitrary",)),
    )(dx2, gate, up, x1, g2, w_gate, w_up, w_down)


def _merge_bwd(dx1, ga, gb, u, ay, rg, ry, g_ret, w_ret, w_attn, w_out):
    S = dx1.shape[0]
    tm = 512

    def body(dx1_ref, ga_ref, gb_ref, u_ref, ay_ref, rg_ref, ry_ref, gr_ref, wr_ref, wa_ref, wo_ref,
             dgates_ref, drg_ref, dba_ref, dbb_ref, day_ref, dry_ref, dgr_ref):
        @pl.when(pl.program_id(0) == 0)
        def _():
            dgr_ref[...] = jnp.zeros_like(dgr_ref)

        dm = _dot_nt(dx1_ref[...].astype(BF), wo_ref[...])
        ba = _dot(u_ref[...], wr_ref[...])
        bb = _dot(ay_ref[...], wa_ref[...])
        sa = _sigmoid(ga_ref[...].astype(F32))
        sb = _sigmoid(gb_ref[...].astype(F32))
        dga = (dm * ba * (sa * (1.0 - sa))).astype(BF)
        dgb = (dm * bb * (sb * (1.0 - sb))).astype(BF)
        dgates_ref[:, :1024] = dga
        dgates_ref[:, 1024:] = dgb
        dba = (dm * sa).astype(BF)
        dbb = (dm * sb).astype(BF)
        dba_ref[...] = dba
        dbb_ref[...] = dbb
        day_ref[...] = _dot_nt(dbb, wa_ref[...]).astype(BF)
        du = _dot_nt(dba, wr_ref[...])
        for hd in range(RET_HEADS):
            vs = slice(hd * 256, (hd + 1) * 256)
            yn, rstd = _group_norm(ry_ref[:, vs].astype(F32))
            rgv = rg_ref[:, vs].astype(F32)
            sr = _sigmoid(rgv)
            silu = rgv * sr
            duv = du[:, vs]
            grv = gr_ref[:, vs]
            drg = (duv * (yn * grv) * (sr * (1.0 + rgv * (1.0 - sr)))).astype(BF)
            drg_ref[:, vs] = drg
            dyg = duv * silu
            dgr_ref[:, vs] += jnp.sum(dyg * yn, axis=0, keepdims=True)
            dyn = dyg * grv
            dry_ref[:, vs] = (rstd * (dyn - jnp.mean(dyn, axis=-1, keepdims=True)
                                      - yn * jnp.mean(dyn * yn, axis=-1, keepdims=True))).astype(KEPT)

    return pl.pallas_call(
        body, name="merge_bwd", grid=(S // tm,),
        in_specs=[_rows(tm, 1024)] * 7 + [_fixed((1, 1024)), _VMEM, _VMEM, _VMEM],
        out_specs=[_rows(tm, 2048), _rows(tm, 1024), _rows(tm, 1024), _rows(tm, 1024), _rows(tm, 1024),
                   _rows(tm, 1024), _fixed((1, 1024))],
        out_shape=[jax.ShapeDtypeStruct((S, 2048), BF), jax.ShapeDtypeStruct((S, 1024), BF),
                   jax.ShapeDtypeStruct((S, 1024), BF), jax.ShapeDtypeStruct((S, 1024), BF),
                   jax.ShapeDtypeStruct((S, 1024), BF), jax.ShapeDtypeStruct((S, 1024), KEPT),
                   jax.ShapeDtypeStruct((1, 1024), F32)],
        compiler_params=_params(("arbitrary",)),
    )(dx1, ga, gb, u, ay, rg, ry, g_ret, w_ret, w_attn, w_out)


def _attn_bwd(aq, ak, av, day, sinks, ca, sa, after):
    S = aq.shape[0]
    N = S // SEQ_CHUNK
    steps = N // 2

    def body(sk_ref, q_ref, kp_ref, kc_ref, vp_ref, vc_ref, do_ref, cq_ref, sq_ref, cp_ref, sp_ref, after_ref,
             dq_ref, dke_ref, dve_ref, dko_ref, dvo_ref, dsk_ref, carry_k, carry_v):
        i = pl.program_id(0)

        @pl.when(i == 0)
        def _():
            carry_k[...] = jnp.zeros_like(carry_k)
            carry_v[...] = jnp.zeros_like(carry_v)
            dsk_ref[...] = jnp.zeros_like(dsk_ref)

        @pl.when(i < steps)
        def _():
            kall, kall_sw = _kv_halves(kp_ref, kc_ref)
            vall, vall_sw = _kv_halves(vp_ref, vc_ref)
            lo = lax.broadcasted_iota(jnp.int32, (256, 128), 1) < 64
            blocks = []
            for sb in range(2):
                kv = slice(sb * 128, sb * 128 + 256)
                rows = slice(sb * 128, (sb + 1) * 128)
                ops = [_group_operands(g, kall[kv], kall_sw[kv], vall[kv], vall_sw[kv]) for g in range(2)]
                stacked = [(_stack_pairs(q_ref, g, rows), _stack_pairs(do_ref, g, rows)) for g in range(2)]
                raw = [_dot_nt(ops[g][0], stacked[g][0]) for g in range(2)]
                raw_dp = [_dot_nt(ops[g][1], stacked[g][1]) for g in range(2)]
                blocks.append((rows, ops, stacked, raw, raw_dp, _attn_fold(2 * i + sb)))
            grads = []
            d_sink = [jnp.zeros((1, PAIRS * 128), F32) for _ in range(4)]
            for rows, ops, stacked, raw, raw_dp, fold in blocks:
                cur = fold[0]
                cq, sq = cq_ref[rows, :], sq_ref[rows, :]
                dk_groups, dv_groups = [], []
                for g in range(2):
                    qs, dos = stacked[g]
                    ds_parts, p_parts = [], []
                    for which in range(2):
                        half = slice(which * 256, (which + 1) * 256)
                        pr, ps = _attn_probs(raw[g][half], fold, _sink_row(sk_ref, g, which))
                        dp = _fold(raw_dp[g][half], cur)
                        dr = jnp.sum(pr * dp, axis=0, keepdims=True)
                        ds_parts.append(_unfold(pr * (dp - dr) * (ATTN_HEAD_DIM ** -0.5), fold))
                        p_parts.append(_unfold(pr, fold))
                        d_sink[2 * g + which] = d_sink[2 * g + which] - ps * dr
                    ds = jnp.concatenate(ds_parts, axis=0)
                    dq = _dot_tn(ds, ops[g][0])
                    dk_both = _dot(ds, qs)
                    dv_both = _dot(jnp.concatenate(p_parts, axis=0), dos)
                    for p in range(PAIRS):
                        sl = slice((4 * g + p) * 128, (4 * g + p + 1) * 128)
                        dq_ref[rows, sl] = _rot_attn_bwd(dq[p * 128:(p + 1) * 128], cq, sq).astype(BF)
                    tk = jnp.where(lo, dk_both[:256], dk_both[256:])
                    tv = jnp.where(lo, dv_both[:256], dv_both[256:])
                    dk_groups.append(tk + pltpu.roll(tk, 64, 1))
                    dv_groups.append(tv + pltpu.roll(tv, 64, 1))
                grads.append((jnp.where(lo, dk_groups[0], dk_groups[1]), jnp.where(lo, dv_groups[0], dv_groups[1])))
            dsk_ref[...] += jnp.concatenate(d_sink, axis=0)
            (dk_a, dv_a), (dk_b, dv_b) = grads
            dko_ref[...] = _rot_attn_bwd(carry_k[...] + dk_a[:128], cp_ref[...], sp_ref[...]).astype(BF)
            dvo_ref[...] = (carry_v[...] + dv_a[:128]).astype(BF)
            dke_ref[...] = _rot_attn_bwd(dk_a[128:] + dk_b[:128], cq_ref[0:128, :], sq_ref[0:128, :]).astype(BF)
            dve_ref[...] = (dv_a[128:] + dv_b[:128]).astype(BF)
            carry_k[...] = dk_b[128:]
            carry_v[...] = dv_b[128:]

        @pl.when(i == steps)
        def _():
            dko_ref[...] = _rot_attn_bwd(carry_k[...], cp_ref[...], sp_ref[...]).astype(BF)
            dvo_ref[...] = carry_v[...].astype(BF)

    two = lambda i: (jnp.minimum(i, steps - 1), 0)
    before = lambda i: (jnp.maximum(2 * i - 1, 0), 0)
    odd = lambda i: (jnp.maximum(i - 1, 0), 0)
    blk = lambda r, w, im: pl.BlockSpec((r, w), im)
    half = jax.ShapeDtypeStruct((S // 2, 128), BF)
    dq, dk_even, dv_even, dk_odd, dv_odd, dsk = pl.pallas_call(
        body, name="attn_bwd", grid=(steps + 1,),
        in_specs=[_SMEM, blk(256, 1024, two), blk(128, 128, before), blk(256, 128, two), blk(128, 128, before),
                  blk(256, 128, two), blk(256, 1024, two), blk(256, 128, two), blk(256, 128, two),
                  blk(128, 128, before), blk(128, 128, before), _ANY],
        out_specs=[blk(256, 1024, two), blk(128, 128, two), blk(128, 128, two), blk(128, 128, odd),
                   blk(128, 128, odd), _fixed((4, PAIRS * 128))],
        out_shape=[jax.ShapeDtypeStruct((S, 1024), BF), half, half, half, half,
                   jax.ShapeDtypeStruct((4, PAIRS * 128), F32)],
        scratch_shapes=[pltpu.VMEM((128, 128), F32), pltpu.VMEM((128, 128), F32)],
        compiler_params=_params(("arbitrary",)),
    )(sinks, aq, ak, ak, av, av, day, ca, sa, ca, sa, after)

    def interleave(even, odd_):
        return jnp.stack([even.reshape(steps, 128, 128), odd_.reshape(steps, 128, 128)], axis=1).reshape(S, 128)

    return dq, interleave(dk_even, dk_odd), interleave(dv_even, dv_odd), dsk


def _ret_bwd(rq, rk, rv, dry, states, dmat, qd, kd, cd, cr, sr, after):
    S = rq.shape[0]
    N = S // RET_CHUNK
    scale = RET_QK_DIM ** -0.5
    cps = RET_CHUNKS_PER_STEP
    steps = N // cps

    def body(q_ref, k_ref, v_ref, dy_ref, st_ref, dm_ref, qd_ref, kd_ref, cd_ref, cr_ref, sr_ref, after_ref,
             out_ref, dstate):
        @pl.when(pl.program_id(0) == 0)
        def _():
            dstate[...] = jnp.zeros_like(dstate)

        for hd in range(RET_HEADS):
            qs = slice(hd * 128, (hd + 1) * 128)
            vs = slice(hd * 256, (hd + 1) * 256)
            dmv = dm_ref[hd]
            free = []
            for ch in range(cps):
                rs = slice(ch * RET_CHUNK, (ch + 1) * RET_CHUNK)
                crv, srv = cr_ref[rs, :], sr_ref[rs, :]
                q = q_ref[rs, qs]
                kf = k_ref[rs, qs]
                k = kf.astype(BF)
                v = v_ref[rs, vs]
                do = dy_ref[rs, vs].astype(F32)
                dob = do.astype(BF)
                doq = (do * qd_ref[hd]).astype(BF)
                a = (_dot_nt(q, k) * dmv).astype(BF)
                da = (_dot_nt(dob, v) * dmv).astype(BF)
                dq = _dot(da, k) + _dot_nt(doq, st_ref[ch, hd])
                out_ref[rs, qs] = _rot_ret_bwd(dq * scale, crv, srv).astype(BF)
                free.append((rs, crv, srv, v, (kf * kd_ref[hd]).astype(BF),
                             _dot_tn(a, dob), _dot_tn(da, q), _dot_tn(q, doq)))
            ds = dstate[hd]
            for rs, crv, srv, v, kdv, dv_free, dk_free, ds_update in reversed(free):
                dsb = ds.astype(BF)
                dv = dv_free + _dot(kdv, dsb)
                dk = dk_free + _dot_nt(v, dsb) * kd_ref[hd]
                ds = ds * cd_ref[hd] + ds_update
                out_ref[rs, 512 + hd * 128:512 + (hd + 1) * 128] = _rot_ret_bwd(dk, crv, srv).astype(BF)
                out_ref[rs, 1024 + hd * 256:1024 + (hd + 1) * 256] = dv.astype(BF)
            dstate[hd] = ds

    rev = lambda w: pl.BlockSpec((cps * RET_CHUNK, w), lambda i: (steps - 1 - i, 0))
    return pl.pallas_call(
        body, name="ret_bwd", grid=(steps,),
        in_specs=[rev(512), rev(512), rev(1024), rev(1024),
                  pl.BlockSpec((cps, RET_HEADS, 128, 256), lambda i: (steps - 1 - i, 0, 0, 0)),
                  _VMEM, _VMEM, _VMEM, _VMEM, rev(128), rev(128), _ANY],
        out_specs=rev(2048),
        out_shape=jax.ShapeDtypeStruct((S, 2048), BF),
        scratch_shapes=[pltpu.VMEM((RET_HEADS, 128, 256), F32)],
        compiler_params=_params(("arbitrary",)),
    )(rq, rk, rv, dry, states, dmat, qd, kd, cd, cr, sr, after)


def _proj_bwd(d_pieces, x, dx1, g1, w_in_t):
    S = x.shape[0]
    tm = 512
    widths = [d.shape[1] for d in d_pieces]
    assert sum(widths) == D_IN
    nd = len(d_pieces)

    def body(*refs):
        d_refs = refs[:nd]
        x_ref, dx1_ref, g_ref, w_ref, gx_ref, dg_ref, db_ref = refs[nd:]

        @pl.when(pl.program_id(0) == 0)
        def _():
            dg_ref[...] = jnp.zeros_like(dg_ref)
            db_ref[...] = jnp.zeros_like(db_ref)

        d_all = jnp.concatenate([ref[...] for ref in d_refs], axis=1)
        dh = _dot(d_all, w_ref[...])
        db_ref[...] += jnp.sum(d_all.astype(F32), axis=0, keepdims=True)
        xv = x_ref[...]
        r = lax.rsqrt(jnp.mean(xv * xv, axis=-1, keepdims=True) + EPS)
        xh = xv * r
        dg_ref[...] += jnp.sum(dh * xh, axis=0, keepdims=True)
        dxh = dh * g_ref[...]
        gx_ref[...] = dx1_ref[...] + r * (dxh - xh * jnp.mean(dxh * xh, axis=-1, keepdims=True))

    return pl.pallas_call(
        body, name="proj_bwd", grid=(S // tm,),
        in_specs=[_rows(tm, n) for n in widths] + [_rows(tm, 1024), _rows(tm, 1024), _fixed((1, 1024)), _VMEM],
        out_specs=[_rows(tm, 1024), _fixed((1, 1024)), _fixed((1, D_IN))],
        out_shape=[jax.ShapeDtypeStruct((S, 1024), F32), jax.ShapeDtypeStruct((1, 1024), F32),
                   jax.ShapeDtypeStruct((1, D_IN), F32)],
        compiler_params=_params(("arbitrary",)),
    )(*d_pieces, x, dx1, g1, w_in_t)


SHARD_ROWS = (1600, 256, 256, 256, 704, 704, 704)


class Group(NamedTuple):
    name: str
    tensors: tuple
    offs: tuple
    block: int
    tile: int

    @property
    def half(self):
        return self.block // 2

    def off(self, t):
        return self.offs[self.tensors.index(t)]


LATE = Group("late", (0,), (0,), 1600, 800)
EARLY = Group("early", (4, 5, 6, 1, 2, 3), (0, 704, 1408, 2112, 2368, 2624), 2880, 480)
GROUP_OF = {t: g for g in (LATE, EARLY) for t in g.tensors}


def _wgrad(a_list, b, t, packed, name):
    S = b.shape[0]
    widths = [a.shape[1] for a in a_list]
    M = sum(widths)
    ts = 1024 if M <= 4096 else 512
    steps = S // ts
    na = len(a_list)
    grp = GROUP_OF[t]
    r, off = SHARD_ROWS[t], grp.off(t)
    assert M == N_CHIPS * r
    segments = [(chip * r, r, chip * grp.block + off) for chip in range(N_CHIPS)]
    stage_rows = r
    n_in = na + 1 + (2 if packed is not None else 0)

    def body(*refs):
        a_refs, b_ref = refs[:na], refs[na]
        g32_ref, gb_ref, acc, stage, sems = refs[n_in:]
        i = pl.program_id(0)

        @pl.when(i == 0)
        def _():
            acc[...] = jnp.zeros_like(acc)

        bv = b_ref[...].astype(BF)
        lo = 0
        for a_ref, n in zip(a_refs, widths):
            acc[lo:lo + n, :] += _dot_tn(a_ref[...].astype(BF), bv)
            lo += n

        @pl.when(i == steps - 1)
        def _():
            for a0, n, p0 in segments:
                c32 = pltpu.make_async_copy(acc.at[a0:a0 + n, :], g32_ref.at[pl.ds(p0, n), :], sems.at[0])
                c32.start()
                stage[0:n, :] = acc[a0:a0 + n, :].astype(BF)
                c16 = pltpu.make_async_copy(stage.at[0:n, :], gb_ref.at[pl.ds(p0, n), :], sems.at[1])
                c16.start()
                c32.wait()
                c16.wait()

    shapes = [jax.ShapeDtypeStruct((N_CHIPS * grp.block, 1024), F32),
              jax.ShapeDtypeStruct((N_CHIPS * grp.block, 1024), BF)]
    operands = list(a_list) + [b]
    in_specs = [_rows(ts, n) for n in widths] + [_rows(ts, 1024)]
    aliases = {}
    if packed is not None:
        operands += list(packed)
        in_specs += [_ANY, _ANY]
        aliases = {na + 1: 0, na + 2: 1}
    return pl.pallas_call(
        body, name=name, grid=(steps,),
        in_specs=in_specs, out_specs=[_ANY, _ANY], out_shape=shapes,
        scratch_shapes=[pltpu.VMEM((M, 1024), F32), pltpu.VMEM((stage_rows, 1024), BF),
                        pltpu.SemaphoreType.DMA((2,))],
        input_output_aliases=aliases,
        compiler_params=_params(("arbitrary",)),
    )(*operands)


def _stage_weights(place, ws, name, after=()):
    nw = len(ws)

    def body(p_ref, *refs):
        for w_ref, o_ref in zip(refs[:nw], refs[nw + len(after):]):
            o_ref[...] = w_ref[...].astype(BF)

    return pl.pallas_call(
        body, name=name,
        grid_spec=pltpu.PrefetchScalarGridSpec(
            num_scalar_prefetch=1, grid=(1,),
            in_specs=[pl.BlockSpec(w.shape, lambda i, p: (0, 0)) for w in ws] + [_ANY] * len(after),
            out_specs=[pl.BlockSpec(w.shape, lambda i, p: (p[0], 0)) for w in ws]),
        out_shape=[jax.ShapeDtypeStruct((N_CHIPS * w.shape[0], 1024), BF) for w in ws],
        compiler_params=_params(("arbitrary",)),
    )(place, *ws, *after)


def _kept_block(grp, nb):
    return lambda i, p: ((2 * (i // nb) + p[1]) * nb + i % nb, 0)


def _pair_sum(place, g32, got, grp):
    nb = grp.half // grp.tile

    def body(p_ref, g_ref, r_ref, s16_ref):
        s16_ref[...] = (g_ref[...] + r_ref[...].astype(F32)).astype(BF)

    flat = pl.BlockSpec((grp.tile, 1024), lambda i, p: (i, 0))
    return pl.pallas_call(
        body, name="pair_sum_" + grp.name,
        grid_spec=pltpu.PrefetchScalarGridSpec(
            num_scalar_prefetch=1, grid=(N_CHIPS * nb,),
            in_specs=[pl.BlockSpec((grp.tile, 1024), _kept_block(grp, nb)), flat], out_specs=flat),
        out_shape=jax.ShapeDtypeStruct((N_CHIPS * grp.half, 1024), BF),
        compiler_params=_params(("arbitrary",)),
    )(place, g32, got)


def _chip_sum(place, g32, got, others, grp):
    nb = grp.half // grp.tile

    def body(p_ref, g_ref, r_ref, r0_ref, r1_ref, r2_ref, o_ref):
        own = g_ref[...] + r_ref[...].astype(F32)
        o_ref[...] = ((own + r0_ref[...].astype(F32)) + r1_ref[...].astype(F32)) + r2_ref[...].astype(F32)

    blk = lambda f: pl.BlockSpec((grp.tile, 1024), f)
    return pl.pallas_call(
        body, name="chip_sum_" + grp.name,
        grid_spec=pltpu.PrefetchScalarGridSpec(
            num_scalar_prefetch=1, grid=(nb,),
            in_specs=[blk(lambda i, p: ((2 * p[0] + p[1]) * nb + i, 0)), blk(lambda i, p: (p[0] * nb + i, 0)),
                      blk(lambda i, p: (i, 0)), blk(lambda i, p: (nb + i, 0)), blk(lambda i, p: (2 * nb + i, 0))],
            out_specs=blk(lambda i, p: (p[1] * nb + i, 0))),
        out_shape=jax.ShapeDtypeStruct((grp.block, 1024), F32),
        compiler_params=_params(("arbitrary",)),
    )(place, g32, got, others, others, others)


def _adamw(w, g, m, v, name, g_row0=0, after=None):
    r = w.shape[0]
    tr = next((c for c in (352, 320, 64) if r % c == 0 and g_row0 % c == 0), r)
    assert g_row0 % tr == 0
    g0 = g_row0 // tr
    extra = [] if after is None else [after]

    def body(w_ref, g_ref, m_ref, v_ref, *rest):
        go_ref, d_ref, mo_ref, vo_ref = rest[len(extra):]
        gv = g_ref[...]
        mn = ADAM_B1 * m_ref[...] + (1.0 - ADAM_B1) * gv
        vn = ADAM_B2 * v_ref[...] + (1.0 - ADAM_B2) * (gv * gv)
        m_hat = mn / (1.0 - ADAM_B1 ** ADAM_STEP)
        v_hat = vn / (1.0 - ADAM_B2 ** ADAM_STEP)
        go_ref[...] = gv
        d_ref[...] = -ADAM_LR * (m_hat / (jnp.sqrt(v_hat) + ADAM_EPS) + ADAM_WD * w_ref[...])
        mo_ref[...] = mn
        vo_ref[...] = vn

    return pl.pallas_call(
        body, name=name, grid=(r // tr,),
        in_specs=[_rows(tr, 1024), pl.BlockSpec((tr, 1024), lambda i: (g0 + i, 0)), _rows(tr, 1024), _rows(tr, 1024)]
        + [_ANY] * len(extra),
        out_specs=[_rows(tr, 1024)] * 4,
        out_shape=[jax.ShapeDtypeStruct((r, 1024), F32)] * 4,
        compiler_params=_params(("parallel",)),
    )(w, g, m, v, *extra)


def _place():
    x, y, c = lax.axis_index("x"), lax.axis_index("y"), lax.axis_index("c")
    chips = [(1 - x, y), (x, 1 - y), (1 - x, 1 - y)]
    return x, y, c, chips


_HBM =pl.BlockSpec(memory_space=pltpu.HBM)
_SEM = pl.BlockSpec(memory_space=pltpu.SEMAPHORE)
_DATAFLOW = pltpu.SideEffectType.DATAFLOW_SIDE_EFFECTING


def _exchange(name, arrays, n, plan):
    na = len(arrays)

    def body(*refs):
        pairs = plan(refs[na:2 * na], refs[2 * na], refs[2 * na + 1])
        for send, _ in pairs:
            send.start()
        for send, recv in pairs:
            send.wait_send()
            recv.wait_recv()

    return pl.pallas_call(
        body, name=name, in_specs=[_ANY] * na, out_specs=[_ANY] * na,
        out_shape=[jax.ShapeDtypeStruct(a.shape, a.dtype) for a in arrays],
        scratch_shapes=[pltpu.SemaphoreType.DMA((n,)), pltpu.SemaphoreType.DMA((n,))],
        input_output_aliases={i: i for i in range(na)},
    )(*arrays)


def _exchange_start(name, arrays, n, plan, after=()):
    na = len(arrays)

    def body(*refs):
        for send, _ in plan(refs[:na], refs[na + len(after)], refs[na + len(after) + 1]):
            send.start()
        refs[-1][...] = jnp.zeros_like(refs[-1])

    out = pl.pallas_call(
        body, name=name,
        out_shape=(pltpu.SemaphoreType.DMA((n,)), pltpu.SemaphoreType.DMA((n,)),
                   *[pltpu.HBM(a.shape, a.dtype) for a in arrays], jax.ShapeDtypeStruct((8, LANES), F32)),
        in_specs=[_HBM] * na + [_ANY] * len(after), out_specs=(_SEM, _SEM, *[_HBM] * na, _VMEM),
        input_output_aliases={i: 2 + i for i in range(na)},
        compiler_params=pltpu.CompilerParams(has_side_effects=_DATAFLOW),
    )(*[pltpu.with_memory_space_constraint(a, pltpu.HBM) for a in arrays], *after)
    return (out[0], out[1]), list(out[2:2 + na]), out[-1]


def _exchange_wait(name, sems, in_flight, plan, after=()):
    na = len(in_flight)

    def body(*refs):
        for send, recv in plan(refs[:na], refs[na], refs[na + 1]):
            send.wait_send()
            recv.wait_recv()

    out = pl.pallas_call(
        body, name=name,
        out_shape=tuple(pltpu.HBM(a.shape, a.dtype) for a in in_flight),
        in_specs=[_HBM] * na + [_SEM, _SEM] + [_ANY] * len(after), out_specs=tuple([_HBM] * na),
        input_output_aliases={i: i for i in range(na)},
        compiler_params=pltpu.CompilerParams(has_side_effects=_DATAFLOW),
    )(*in_flight, *sems, *after)
    return list(out)


def _remote(src, dst, send_sems, recv_sems, k, to):
    return pltpu.make_async_remote_copy(src_ref=src, dst_ref=dst, send_sem=send_sems.at[k], recv_sem=recv_sems.at[k],
                                        device_id=to, device_id_type=MESH)


def _half_rows(ref, tid, pj, half):
    h = SHARD_ROWS[tid] // 2
    return ref.at[pl.ds((2 * pj + half) * h, h), :]


def _plan_gather_halves(tids):
    def plan(refs, ss, rs):
        x, y, c, chips = _place()
        pairs = []
        for t, tid in enumerate(tids):
            mine = _half_rows(refs[t], tid, 2 * x + y, c)
            for k, (cx, cy) in enumerate(chips):
                theirs = _half_rows(refs[t], tid, 2 * cx + cy, c)
                pairs.append((_remote(mine, mine, ss, rs, 3 * t + k, (cx, cy, c)),
                              _remote(theirs, theirs, ss, rs, 3 * t + k, (x, y, c))))
        return pairs
    return plan


def _plan_forward_halves(tids):
    def plan(refs, ss, rs):
        x, y, c, chips = _place()
        pairs = []
        for t, tid in enumerate(tids):
            for k, (cx, cy) in enumerate(chips):
                got = _half_rows(refs[t], tid, 2 * cx + cy, c)
                missing = _half_rows(refs[t], tid, 2 * cx + cy, 1 - c)
                pairs.append((_remote(got, got, ss, rs, 3 * t + k, (x, y, 1 - c)),
                              _remote(missing, missing, ss, rs, 3 * t + k, (x, y, 1 - c))))
        return pairs
    return plan


def _plan_pair_exchange(grp):
    def plan(refs, ss, rs):
        x, y, c, _ = _place()
        cps = [_remote(refs[0].at[pl.ds(pj * grp.block + (1 - c) * grp.half, grp.half), :],
                       refs[1].at[pl.ds(pj * grp.half, grp.half), :], ss, rs, pj, (x, y, 1 - c))
               for pj in range(N_CHIPS)]
        return [(cp, cp) for cp in cps]
    return plan


def _plan_to_chips(grp):
    def plan(refs, ss, rs):
        x, y, c, chips = _place()
        cps = [_remote(refs[0].at[pl.ds((2 * cx + cy) * grp.half, grp.half), :],
                       refs[1].at[pl.ds(k * grp.half, grp.half), :], ss, rs, k, (cx, cy, c))
               for k, (cx, cy) in enumerate(chips)]
        return [(cp, cp) for cp in cps]
    return plan


def _plan_share_halves(grp):
    def plan(refs, ss, rs):
        x, y, c, _ = _place()
        mine = refs[0].at[pl.ds(c * grp.half, grp.half), :]
        other = refs[0].at[pl.ds((1 - c) * grp.half, grp.half), :]
        return [(_remote(mine, mine, ss, rs, 0, (x, y, 1 - c)), _remote(other, other, ss, rs, 0, (x, y, 1 - c)))]
    return plan


def _empty(rows, dtype):
    return lax.empty((rows, 1024), dtype)


def _all_reduce_small(v, after):
    m_per, n = v.shape
    n_dev = 8

    def body(x_ref, after_ref, out_ref, all_ref, send_sems, recv_sems, local_sem):
        x, y, c, chips = _place()
        me, sibling = (x, y, c), (x, y, 1 - c)

        def rows(px, py, pc):
            return all_ref.at[pl.ds((4 * px + 2 * py + pc) * m_per, m_per), :]

        def copy(k, block, to, src=None):
            return pltpu.make_async_remote_copy(
                src_ref=rows(*block) if src is None else src, dst_ref=rows(*block),
                send_sem=send_sems.at[k], recv_sem=recv_sems.at[k], device_id=to, device_id_type=MESH)

        mine = pltpu.make_async_copy(x_ref, rows(*me), local_sem)
        mine.start()
        first = [copy(0, me, sibling, src=x_ref)]
        first += [copy(1 + k, me, (*chip, c), src=x_ref) for k, chip in enumerate(chips)]
        for cp in first:
            cp.start()
        passed = [copy(4 + k, (*chip, c), sibling) for k, chip in enumerate(chips)]
        for k, chip in enumerate(chips):
            copy(1 + k, (*chip, c), me).wait_recv()
            passed[k].start()
        copy(0, sibling, me).wait_recv()
        for k, chip in enumerate(chips):
            copy(4 + k, (*chip, 1 - c), me).wait_recv()
        for cp in first + passed:
            cp.wait_send()
        mine.wait()
        acc = all_ref[pl.ds(0, m_per), :]
        for d in range(1, n_dev):
            acc = acc + all_ref[pl.ds(d * m_per, m_per), :]
        out_ref[...] = acc

    return pl.pallas_call(
        body, name="all_reduce_small",
        in_specs=[_VMEM, _ANY], out_specs=_VMEM,
        out_shape=jax.ShapeDtypeStruct((m_per, n), F32),
        scratch_shapes=[pltpu.VMEM((n_dev * m_per, n), F32),
                        pltpu.SemaphoreType.DMA((7,)), pltpu.SemaphoreType.DMA((7,)), pltpu.SemaphoreType.DMA],
        compiler_params=pltpu.CompilerParams(has_side_effects=True),
    )(v, after)


def _tables(S, token):
    pos = jnp.arange(S, dtype=jnp.int32).astype(F32) + token[0, 0]

    def cs(half):
        inv_freq = ROPE_THETA ** (-jnp.arange(half, dtype=F32) / half)
        ang = pos[:, None] * inv_freq[None, :]
        return jnp.cos(ang), jnp.sin(ang)

    c64, s64 = cs(64)
    c32, s32 = cs(32)
    cr = jnp.concatenate([c64, c64], axis=1)
    sr = jnp.concatenate([-s64, s64], axis=1)
    ca = jnp.concatenate([c32, c32, c32, c32], axis=1)
    sa = jnp.concatenate([-s32, s32, -s32, s32], axis=1)
    return cr, sr, ca, sa


def _decays():
    H, C = RET_HEADS, RET_CHUNK
    log_gamma = jnp.log1p(-jnp.exp2(-5.0 - jnp.arange(H, dtype=F32)))
    idx = jnp.arange(C, dtype=F32)
    rel = idx[:, None] - idx[None, :]
    dmat = jnp.where(rel[None] >= 0, jnp.exp(log_gamma[:, None, None] * jnp.maximum(rel, 0.0)[None]), 0.0)
    qd = jnp.exp(log_gamma[:, None] * (idx + 1.0))[:, :, None]
    kd = jnp.exp(log_gamma[:, None] * (C - 1.0 - idx))[:, :, None]
    cd = jnp.exp(log_gamma * C)[:, None, None]
    return dmat, qd, kd, cd


def _local_step(place, x, tgt, ln1_g, b_in, ret_norm_g, attn_sinks, ln2_g, lnf_g, w_in, other_weights, tables, decays):
    cr, sr, ca, sa = tables
    dmat, qd, kd, cd = decays
    forward_weights, landed_weights = other_weights

    h, rq, rk, rv, rg, aq, ak, av, ga, gb = _proj_fwd(x, ln1_g, w_in, b_in, cr, sr, ca, sa)
    ry, states = _ret_fwd(rq, rk, rv, dmat, qd, kd, cd)
    token = forward_weights(ry)
    ay = _attn_fwd(aq, ak, av, attn_sinks + token[0, 0])
    w_ret, w_attn, w_out, w_gate, w_up, w_down = landed_weights(ay)
    u, merged, x1 = _merge_fwd(x, ry, rg, ay, ga, gb, ret_norm_g, w_ret, w_attn, w_out)
    gate, up, h2, act, dx2, loss, dg3 = _ffn_fwd(x1, tgt, ln2_g, lnf_g, w_gate, w_up, w_down)

    dgate, dup, dx1, dg2 = _ffn_bwd(dx2, gate, up, x1, ln2_g, w_gate, w_up, w_down)
    early = _wgrad([act], dx2, 6, None, "wgrad_down")
    early = _wgrad([dgate], h2, 4, early, "wgrad_gate")
    early = _wgrad([dup], h2, 5, early, "wgrad_up")
    dgates, drg, dba, dbb, day, dry, dgr = _merge_bwd(dx1, ga, gb, u, ay, rg, ry, ret_norm_g,
                                                                w_ret, w_attn, w_out)
    early = _wgrad([merged], dx1, 3, early, "wgrad_out")
    early = _wgrad([u], dba, 1, early, "wgrad_ret")
    e32, e16 = _wgrad([ay], dbb, 2, early, "wgrad_attn")

    grp = EARLY
    sems, flight, token = _exchange_start("pair_start_early", [e16, _empty(N_CHIPS * grp.half, BF)], N_CHIPS,
                                          _plan_pair_exchange(grp))
    daq, dak, dav, dsk = _attn_bwd(aq, ak, av, day, attn_sinks, ca, sa, token)
    _, got = _exchange_wait("pair_wait_early", sems, flight, _plan_pair_exchange(grp), after=(dsk,))
    s16 = _pair_sum(place, e32, got, grp)
    sems, flight, token = _exchange_start("chips_start_early", [s16, _empty(3 * grp.half, BF)], 3, _plan_to_chips(grp))
    d_ret = _ret_bwd(rq, rk, rv, dry, states, dmat, qd, kd, cd, cr, sr, token)
    d_proj = [d_ret, drg, daq, dak, dav, dgates]
    grad_x, dg1, db = _proj_bwd(d_proj, x, dx1, ln1_g, w_in)
    late = _wgrad(d_proj, h, 0, None, "wgrad_in")

    def finish_early(after):
        _, others = _exchange_wait("chips_wait_early", sems, flight, _plan_to_chips(grp), after=after)
        return _chip_sum(place, e32, got, others, grp)

    d_sinks = dsk.reshape(2, 2, PAIRS, 128).sum(-1).transpose(0, 2, 1).reshape(1, ATTN_Q_HEADS)
    small = dict(loss=loss, ln1_g=dg1, b_in=db, ret_norm_g=dgr, attn_sinks=d_sinks,
                 ln2_g=dg2, lnf_g=dg3)
    return grad_x, finish_early, late, small


SMALL_ROWS = 16


def _pack_small(ln1, ret, ln2, lnf, b_in, sinks, loss, name):
    def body(a_ref, b_ref, c_ref, d_ref, bi_ref, sk_ref, ls_ref, o_ref):
        o_ref[...] = jnp.zeros_like(o_ref)
        for row, ref in enumerate((a_ref, b_ref, c_ref, d_ref)):
            o_ref[row:row + 1, :] = ref[...]
        for k in range(6):
            o_ref[4 + k:5 + k, :] = bi_ref[:, k * 1024:(k + 1) * 1024]
        o_ref[10:11, 0:D_IN - 6144] = bi_ref[:, 6144:D_IN]
        o_ref[11:12, 0:ATTN_Q_HEADS] = sk_ref[...]
        o_ref[12:13, 0:LANES] = ls_ref[...]

    return pl.pallas_call(
        body, name=name, in_specs=[_VMEM] * 7, out_specs=_VMEM,
        out_shape=jax.ShapeDtypeStruct((SMALL_ROWS, 1024), F32),
    )(ln1.reshape(1, 1024), ret.reshape(1, 1024), ln2.reshape(1, 1024), lnf.reshape(1, 1024),
      b_in.reshape(1, D_IN), sinks.reshape(1, ATTN_Q_HEADS), loss.reshape(1, LANES))


def _unpack_small(p):
    return dict(ln1_g=p[0:1], ret_norm_g=p[1:2], ln2_g=p[2:3], lnf_g=p[3], b_in=p[4:11].reshape(1, -1)[:, :D_IN],
                attn_sinks=p[11:12, :ATTN_Q_HEADS], loss=p[12, 0])


def kernel(x, ln1_g, w_in, b_in, ret_norm_g, w_ret_out, attn_sinks, w_attn_out, w_out, ln2_g, w_ffn_gate, w_ffn_up, w_ffn_down, lnf_g, loss_target, m_ln1_g, m_w_in, m_b_in, m_ret_norm_g, m_w_ret_out, m_attn_sinks, m_w_attn_out, m_w_out, m_ln2_g, m_w_ffn_gate, m_w_ffn_up, m_w_ffn_down, m_lnf_g, v_ln1_g, v_w_in, v_b_in, v_ret_norm_g, v_w_ret_out, v_attn_sinks, v_w_attn_out, v_w_out, v_ln2_g, v_w_ffn_gate, v_w_ffn_up, v_w_ffn_down, v_lnf_g):
    xi = lax.axis_index("x")
    yi = lax.axis_index("y")
    ci = lax.axis_index("c")
    j = 2 * xi + yi

    place = jnp.stack([j, ci]).astype(jnp.int32)

    names = ["w_in", "w_ret_out", "w_attn_out", "w_out", "w_ffn_gate", "w_ffn_up", "w_ffn_down"]
    transposed = [True, False, False, False, True, True, False]
    view = lambda a, t: a[0].T if t else a[0]
    big_w = [view(a, t) for a, t in zip([w_in, w_ret_out, w_attn_out, w_out, w_ffn_gate, w_ffn_up, w_ffn_down], transposed)]
    big_m = [view(a, t) for a, t in zip([m_w_in, m_w_ret_out, m_w_attn_out, m_w_out, m_w_ffn_gate, m_w_ffn_up, m_w_ffn_down], transposed)]
    big_v = [view(a, t) for a, t in zip([v_w_in, v_w_ret_out, v_w_attn_out, v_w_out, v_w_ffn_gate, v_w_ffn_up, v_w_ffn_down], transposed)]

    staged_in = _stage_weights(place, big_w[:1], "stage_w_in")[0]
    sems, in_flight, token = _exchange_start("gather_in_start", [staged_in], 3, _plan_gather_halves([0]))
    staged = _stage_weights(place, big_w[1:], "stage_others", after=(token,))
    tables = _tables(x.shape[1], token)
    decays = _decays()
    w_in_full = _exchange_wait("gather_in_wait", sems, in_flight, _plan_gather_halves([0]),
                               after=(*staged, *tables, *decays))[0]
    w_in_full = _exchange("gather_in_forward", [w_in_full], 3, _plan_forward_halves([0]))[0]
    rest = list(range(1, 7))
    crossing = _exchange_start("gather_start", staged, 3 * len(rest), _plan_gather_halves(rest), after=(w_in_full,))
    token = crossing[2]
    forwarding = []

    def forward_weights(after):
        arrived = _exchange_wait("gather_wait", crossing[0], crossing[1], _plan_gather_halves(rest), after=(after,))
        forwarding.extend(_exchange_start("forward_start", arrived, 3 * len(rest), _plan_forward_halves(rest)))
        return forwarding[2]

    def landed_weights(after):
        return _exchange_wait("forward_wait", forwarding[0], forwarding[1], _plan_forward_halves(rest), after=(after,))

    other_weights = (forward_weights, landed_weights)

    grad_x, finish_early, (l32, l16), small = _local_step(
        place, x[0], loss_target[0], ln1_g, b_in + token[0, 0], ret_norm_g, attn_sinks, ln2_g, lnf_g.reshape(1, -1),
        w_in_full, other_weights, tables, decays)
    pg = _all_reduce_small(_pack_small(
        small["ln1_g"], small["ret_norm_g"], small["ln2_g"], small["lnf_g"], small["b_in"],
        small["attn_sinks"], small["loss"], "pack_small_grads"), l32)
    early_shard = _exchange("share_early", [finish_early((l32,))], 1, _plan_share_halves(EARLY))[0]
    _, got = _exchange("pair_exchange_late", [l16, _empty(N_CHIPS * LATE.half, BF)], N_CHIPS, _plan_pair_exchange(LATE))
    s16 = _pair_sum(place, l32, got, LATE)
    sems, flight, token = _exchange_start("chips_start_late", [s16, _empty(3 * LATE.half, BF)], 3, _plan_to_chips(LATE),
                                          after=(pg, early_shard))

    upd = {}

    def update(t, shard, after):
        res = _adamw(big_w[t], shard, big_m[t], big_v[t], "adamw_" + names[t], g_row0=GROUP_OF[t].off(t), after=after)
        upd[names[t]] = tuple((r.T if transposed[t] else r)[None] for r in res)
        return res[0]

    done = [update(t, early_shard, token) for t in EARLY.tensors]

    sm = _unpack_small(pg)
    z = jnp.zeros((1, LANES), F32)
    pw = _pack_small(ln1_g, ret_norm_g, ln2_g, lnf_g, b_in, attn_sinks, z, "pack_small_w")
    pm = _pack_small(m_ln1_g, m_ret_norm_g, m_ln2_g, m_lnf_g, m_b_in, m_attn_sinks, z, "pack_small_m")
    pv = _pack_small(v_ln1_g, v_ret_norm_g, v_ln2_g, v_lnf_g, v_b_in, v_attn_sinks, z, "pack_small_v")
    small_res = _adamw(pw, pg, pm, pv, "adamw_small", after=token)
    sd, smn, svn = (_unpack_small(t) for t in small_res[1:])
    for n in ["ln1_g", "b_in", "ret_norm_g", "attn_sinks", "ln2_g", "lnf_g"]:
        upd[n] = (sm[n], sd[n], smn[n], svn[n])

    small_outs = [a for n in ("ln1_g", "b_in", "ret_norm_g", "attn_sinks", "ln2_g", "lnf_g") for a in upd[n]]
    _, others = _exchange_wait("chips_wait_late", sems, flight, _plan_to_chips(LATE), after=(*done, *small_outs))
    late_shard = _exchange("share_late", [_chip_sum(place, l32, got, others, LATE)], 1, _plan_share_halves(LATE))[0]
    update(0, late_shard, None)

    order_w = ["ln1_g", "w_in", "b_in", "ret_norm_g", "w_ret_out", "attn_sinks", "w_attn_out", "w_out",
               "ln2_g", "w_ffn_gate", "w_ffn_up", "w_ffn_down", "lnf_g"]
    outs = [sm["loss"], grad_x[None]]
    for k in range(4):
        outs += [upd[n][k] for n in order_w]
    return tuple(outs)
```

```python
from typing import NamedTuple

import jax
import jax.numpy as jnp
from jax import lax
from jax.experimental import pallas as pl
from jax.experimental.pallas import tpu as pltpu

D_MODEL = 1024
SEQ_CHUNK = 128
RET_CHUNK = 256
RET_CHUNKS_PER_STEP = 4
RET_HEADS = 4
RET_QK_DIM = 128
ATTN_Q_HEADS = 16
ATTN_HEAD_DIM = 64
D_FF = 2816
D_IN = 6400
ROPE_THETA = 10000.0
EPS = 1e-6
N_CHIPS = 4

ADAM_LR = 0.001
ADAM_B1 = 0.9
ADAM_B2 = 0.999
ADAM_EPS = 1e-08
ADAM_WD = 0.01
ADAM_STEP = 10

OFF_RQ, OFF_RK, OFF_RV, OFF_RG = 0, 512, 1024, 2048
OFF_AQ, OFF_AK, OFF_AV, OFF_GA, OFF_GB = 3072, 4096, 4224, 4352, 5376

V7X_VMEM_LIMIT = 56 * 1024 * 1024
LANES = 128

BF = jnp.bfloat16
F32 = jnp.float32
KEPT = jnp.bfloat16
MESH = pl.DeviceIdType.MESH

_VMEM = pl.BlockSpec(memory_space=pltpu.VMEM)
_SMEM = pl.BlockSpec(memory_space=pltpu.SMEM)
_ANY = pl.BlockSpec(memory_space=pl.ANY)


def _dot(a, b):
    return jnp.dot(a, b, preferred_element_type=F32)


def _dot_nt(a, b):
    return lax.dot_general(a, b, (((1,), (1,)), ((), ())), preferred_element_type=F32)


def _dot_tn(a, b):
    return lax.dot_general(a, b, (((0,), (0,)), ((), ())), preferred_element_type=F32)


def _sigmoid(x):
    return 0.5 * jnp.tanh(0.5 * x) + 0.5


def _rows(tm, n):
    return pl.BlockSpec((tm, n), lambda i: (i, 0))


def _fixed(shape):
    nd = len(shape)
    return pl.BlockSpec(shape, lambda i: (0,) * nd)


def _params(sem, **kw):
    return pltpu.CompilerParams(dimension_semantics=sem, vmem_limit_bytes=V7X_VMEM_LIMIT, **kw)


def _rot_ret(p, cr, sr):
    return p * cr + pltpu.roll(p, 64, 1) * sr


def _rot_ret_bwd(d, cr, sr):
    return d * cr - pltpu.roll(d, 64, 1) * sr


def _attn_partner(p):
    lane = lax.broadcasted_iota(jnp.int32, p.shape, 1)
    first = (lane % 64) < 32
    return jnp.where(first, pltpu.roll(p, 96, 1), pltpu.roll(p, 32, 1))


def _rot_attn(p, ca, sa):
    return p * ca + _attn_partner(p) * sa


def _rot_attn_bwd(d, ca, sa):
    return d * ca - _attn_partner(d) * sa


def _proj_fwd(x, g1, w_in, b_in, cr, sr, ca, sa):
    S = x.shape[0]
    tm = 512
    scale = RET_QK_DIM ** -0.5

    def body(x_ref, g_ref, w_ref, b_ref, cr_ref, sr_ref, ca_ref, sa_ref,
             h_ref, rq_ref, rk_ref, rv_ref, rg_ref, aq_ref, ak_ref, av_ref, ga_ref, gb_ref):
        xv = x_ref[...]
        r = lax.rsqrt(jnp.mean(xv * xv, axis=-1, keepdims=True) + EPS)
        h = (xv * r * g_ref[...]).astype(BF)
        h_ref[...] = h

        def piece(lo, n):
            return _dot_nt(h, w_ref[lo:lo + n, :]) + b_ref[:, lo:lo + n]

        crv, srv = cr_ref[...], sr_ref[...]
        pq = piece(OFF_RQ, 512)
        pk = piece(OFF_RK, 512)
        for hd in range(RET_HEADS):
            sl = slice(hd * 128, (hd + 1) * 128)
            rq_ref[:, sl] = (_rot_ret(pq[:, sl], crv, srv) * scale).astype(BF)
            rk_ref[:, sl] = _rot_ret(pk[:, sl], crv, srv)
        rv_ref[...] = piece(OFF_RV, 1024).astype(BF)
        rg_ref[...] = piece(OFF_RG, 1024).astype(KEPT)
        cav, sav = ca_ref[...], sa_ref[...]
        pa = piece(OFF_AQ, 1024)
        for blk in range(8):
            sl = slice(blk * 128, (blk + 1) * 128)
            aq_ref[:, sl] = _rot_attn(pa[:, sl], cav, sav).astype(BF)
        pkv = piece(OFF_AK, 256)
        ak_ref[...] = _rot_attn(pkv[:, :128], cav, sav).astype(BF)
        av_ref[...] = pkv[:, 128:].astype(BF)
        ga_ref[...] = piece(OFF_GA, 1024).astype(KEPT)
        gb_ref[...] = piece(OFF_GB, 1024).astype(KEPT)

    outs = [(D_MODEL, BF), (512, BF), (512, F32), (1024, BF), (1024, KEPT),
            (1024, BF), (128, BF), (128, BF), (1024, KEPT), (1024, KEPT)]
    return pl.pallas_call(
        body, name="proj_fwd", grid=(S // tm,),
        in_specs=[_rows(tm, D_MODEL), _fixed((1, D_MODEL)), _VMEM, _VMEM,
                  _rows(tm, 128), _rows(tm, 128), _rows(tm, 128), _rows(tm, 128)],
        out_specs=[_rows(tm, n) for n, _ in outs],
        out_shape=[jax.ShapeDtypeStruct((S, n), dt) for n, dt in outs],
        compiler_params=_params(("parallel",)),
    )(x, g1, w_in, b_in, cr, sr, ca, sa)


def _ret_fwd(rq, rk, rv, dmat, qd, kd, cd):
    S = rq.shape[0]
    N = S // RET_CHUNK
    cps = RET_CHUNKS_PER_STEP
    rows = cps * RET_CHUNK

    def body(q_ref, k_ref, v_ref, dm_ref, qd_ref, kd_ref, cd_ref, y_ref, st_ref, state):
        @pl.when(pl.program_id(0) == 0)
        def _():
            state[...] = jnp.zeros_like(state)

        for hd in range(RET_HEADS):
            qs = slice(hd * 128, (hd + 1) * 128)
            vs = slice(hd * 256, (hd + 1) * 256)
            free = []
            for ch in range(cps):
                rs = slice(ch * RET_CHUNK, (ch + 1) * RET_CHUNK)
                q = q_ref[rs, qs]
                kf = k_ref[rs, qs]
                v = v_ref[rs, vs]
                a = _dot_nt(q, kf.astype(BF)) * dm_ref[hd]
                kdv = (kf * kd_ref[hd]).astype(BF)
                free.append((rs, q, _dot(a.astype(BF), v), _dot_tn(kdv, v)))
            s = state[hd]
            for ch, (rs, q, inner, update) in enumerate(free):
                sb = s.astype(BF)
                st_ref[ch, hd] = sb
                y_ref[rs, vs] = (inner + _dot(q, sb) * qd_ref[hd]).astype(KEPT)
                s = s * cd_ref[hd] + update
            state[hd] = s

    return pl.pallas_call(
        body, name="ret_fwd", grid=(N // cps,),
        in_specs=[_rows(rows, 512), _rows(rows, 512), _rows(rows, 1024), _VMEM, _VMEM, _VMEM, _VMEM],
        out_specs=[_rows(rows, 1024), pl.BlockSpec((cps, RET_HEADS, 128, 256), lambda i: (i, 0, 0, 0))],
        out_shape=[jax.ShapeDtypeStruct((S, 1024), KEPT),
                   jax.ShapeDtypeStruct((N, RET_HEADS, 128, 256), BF)],
        scratch_shapes=[pltpu.VMEM((RET_HEADS, 128, 256), F32)],
        compiler_params=_params(("arbitrary",)),
    )(rq, rk, rv, dmat, qd, kd, cd)


def _kv_halves(prev_ref, cur_ref):
    kk = jnp.concatenate([prev_ref[...], cur_ref[...]], axis=0).astype(F32)
    return kk, pltpu.roll(kk, 64, 1)


ATTN_FWD_BLOCKS_PER_STEP = 4
PAIRS = 4


def _attn_fold(n):
    t = lax.broadcasted_iota(jnp.int32, (128, PAIRS * 128), 0)
    qi = lax.broadcasted_iota(jnp.int32, (128, PAIRS * 128), 1) & 127
    cur = t <= qi
    cur_f = jnp.where(cur, 1.0, 0.0)
    bias = jnp.where(jnp.logical_or(cur, n > 0), 0.0, -1e30)
    return cur, cur_f, 1.0 - cur_f, bias


def _group_operands(g, kk, kk_sw, vv, vv_sw):
    lo = lax.broadcasted_iota(jnp.int32, (256, 128), 1) < 64
    k_a, k_b = (kk, kk_sw) if g == 0 else (kk_sw, kk)
    v_a, v_b = (vv, vv_sw) if g == 0 else (vv_sw, vv)
    return (jnp.concatenate([jnp.where(lo, k_a, 0.0), jnp.where(lo, 0.0, k_b)], axis=0).astype(BF),
            jnp.concatenate([jnp.where(lo, v_a, 0.0), jnp.where(lo, 0.0, v_b)], axis=0).astype(BF))


def _stack_pairs(ref, g, rows=slice(None)):
    return jnp.concatenate([ref[rows, (4 * g + p) * 128:(4 * g + p + 1) * 128] for p in range(PAIRS)], axis=0)


def _sink_row(sk_ref, g, which):
    return jnp.concatenate([jnp.full((1, 128), sk_ref[0, 8 * g + 2 * p + which], F32) for p in range(PAIRS)], axis=1)


def _fold(full, cur):
    return jnp.where(cur, full[128:], full[:128])


def _attn_probs(raw, fold, sink):
    s = _fold(raw, fold[0]) * (ATTN_HEAD_DIM ** -0.5) + fold[3]
    m = jnp.maximum(jnp.max(s, axis=0, keepdims=True), sink)
    e = jnp.exp(s - m)
    es = jnp.exp(sink - m)
    inv = 1.0 / (jnp.sum(e, axis=0, keepdims=True) + es)
    return e * inv, es * inv


def _unfold(p, fold):
    return jnp.concatenate([p * fold[2], p * fold[1]], axis=0).astype(BF)


def _attn_fwd(aq, ak, av, sinks):
    S = aq.shape[0]
    N = S // SEQ_CHUNK
    nb = ATTN_FWD_BLOCKS_PER_STEP

    def body(sk_ref, q_ref, kp_ref, kc_ref, vp_ref, vc_ref, o_ref):
        kall, kall_sw = _kv_halves(kp_ref, kc_ref)
        vall, vall_sw = _kv_halves(vp_ref, vc_ref)
        blocks = []
        for sb in range(nb):
            kv = slice(sb * 128, sb * 128 + 256)
            rows = slice(sb * 128, (sb + 1) * 128)
            ops = [_group_operands(g, kall[kv], kall_sw[kv], vall[kv], vall_sw[kv]) for g in range(2)]
            raw = [_dot_nt(ops[g][0], _stack_pairs(q_ref, g, rows)) for g in range(2)]
            blocks.append((rows, ops, raw, _attn_fold(nb * pl.program_id(0) + sb)))
        for rows, ops, raw, fold in blocks:
            for g in range(2):
                probs = [_unfold(_attn_probs(raw[g][w * 256:(w + 1) * 256], fold, _sink_row(sk_ref, g, w))[0], fold)
                         for w in range(2)]
                o = _dot_tn(jnp.concatenate(probs, axis=0), ops[g][1])
                for p in range(PAIRS):
                    o_ref[rows, (4 * g + p) * 128:(4 * g + p + 1) * 128] = o[p * 128:(p + 1) * 128].astype(BF)

    prev = lambda i: (jnp.maximum(nb * i - 1, 0), 0)
    return pl.pallas_call(
        body, name="attn_fwd", grid=(N // nb,),
        in_specs=[_SMEM, _rows(nb * 128, 1024), pl.BlockSpec((128, 128), prev), _rows(nb * 128, 128),
                  pl.BlockSpec((128, 128), prev), _rows(nb * 128, 128)],
        out_specs=_rows(nb * 128, 1024),
        out_shape=jax.ShapeDtypeStruct((S, 1024), BF),
        compiler_params=_params(("parallel",)),
    )(sinks, aq, ak, ak, av, av)


def _group_norm(y):
    mu = jnp.mean(y, axis=-1, keepdims=True)
    yc = y - mu
    rstd = lax.rsqrt(jnp.mean(yc * yc, axis=-1, keepdims=True) + EPS)
    return yc * rstd, rstd


def _merge_fwd(x, ry, rg, ay, ga, gb, g_ret, w_ret, w_attn, w_out):
    S = x.shape[0]
    tm = 512

    def body(x_ref, ry_ref, rg_ref, ay_ref, ga_ref, gb_ref, gr_ref, wr_ref, wa_ref, wo_ref,
             u_ref, mg_ref, x1_ref):
        bb = _dot(ay_ref[...], wa_ref[...])
        for hd in range(RET_HEADS):
            vs = slice(hd * 256, (hd + 1) * 256)
            yn, _ = _group_norm(ry_ref[:, vs].astype(F32))
            rgv = rg_ref[:, vs].astype(F32)
            u_ref[:, vs] = (rgv * _sigmoid(rgv) * (yn * gr_ref[:, vs])).astype(BF)
        gated_b = _sigmoid(gb_ref[...].astype(F32)) * bb
        sa = _sigmoid(ga_ref[...].astype(F32))
        ba = _dot(u_ref[...], wr_ref[...])
        merged = (sa * ba + gated_b).astype(BF)
        mg_ref[...] = merged
        x1_ref[...] = x_ref[...] + _dot(merged, wo_ref[...])

    outs = [BF, BF, F32]
    return pl.pallas_call(
        body, name="merge_fwd", grid=(S // tm,),
        in_specs=[_rows(tm, 1024)] * 6 + [_fixed((1, 1024)), _VMEM, _VMEM, _VMEM],
        out_specs=[_rows(tm, 1024)] * 3,
        out_shape=[jax.ShapeDtypeStruct((S, 1024), dt) for dt in outs],
        compiler_params=_params(("parallel",)),
    )(x, ry, rg, ay, ga, gb, g_ret, w_ret, w_attn, w_out)


def _ffn_fwd(x1, tgt, g2, g3, w_gate, w_up, w_down):
    S = x1.shape[0]
    tm = 256

    def body(x1_ref, t_ref, g2_ref, g3_ref, wg_ref, wu_ref, wd_ref,
             gate_ref, up_ref, h2_ref, act_ref, dx2_ref, loss_ref, dg3_ref):
        @pl.when(pl.program_id(0) == 0)
        def _():
            loss_ref[...] = jnp.zeros_like(loss_ref)
            dg3_ref[...] = jnp.zeros_like(dg3_ref)

        x1v = x1_ref[...]
        r2 = lax.rsqrt(jnp.mean(x1v * x1v, axis=-1, keepdims=True) + EPS)
        h2 = (x1v * r2 * g2_ref[...]).astype(BF)
        h2_ref[...] = h2
        gate = _dot_nt(h2, wg_ref[...])
        up = _dot_nt(h2, wu_ref[...])
        gate_ref[...] = gate
        up_ref[...] = up
        act = (gate * _sigmoid(gate) * up).astype(BF)
        act_ref[...] = act
        x2 = x1v + _dot(act, wd_ref[...])
        r3 = lax.rsqrt(jnp.mean(x2 * x2, axis=-1, keepdims=True) + EPS)
        xh = x2 * r3
        g3v = g3_ref[...]
        err = xh * g3v - t_ref[...]
        loss_ref[...] += 0.5 * jnp.sum(jnp.mean(err * err, axis=-1, keepdims=True), axis=0, keepdims=True)
        dy = err * (1.0 / D_MODEL)
        dg3_ref[...] += jnp.sum(dy * xh, axis=0, keepdims=True)
        dxh = dy * g3v
        dx2_ref[...] = r3 * (dxh - xh * jnp.mean(dxh * xh, axis=-1, keepdims=True))

    return pl.pallas_call(
        body, name="ffn_fwd", grid=(S // tm,),
        in_specs=[_rows(tm, 1024), _rows(tm, 1024), _fixed((1, 1024)), _fixed((1, 1024)), _VMEM, _VMEM, _VMEM],
        out_specs=[_rows(tm, D_FF), _rows(tm, D_FF), _rows(tm, 1024), _rows(tm, D_FF), _rows(tm, 1024),
                   _fixed((1, LANES)), _fixed((1, 1024))],
        out_shape=[jax.ShapeDtypeStruct((S, D_FF), F32), jax.ShapeDtypeStruct((S, D_FF), F32),
                   jax.ShapeDtypeStruct((S, 1024), BF), jax.ShapeDtypeStruct((S, D_FF), BF),
                   jax.ShapeDtypeStruct((S, 1024), F32),
                   jax.ShapeDtypeStruct((1, LANES), F32), jax.ShapeDtypeStruct((1, 1024), F32)],
        compiler_params=_params(("arbitrary",)),
    )(x1, tgt, g2, g3, w_gate, w_up, w_down)


def _ffn_bwd(dx2, gate, up, x1, g2, w_gate, w_up, w_down):
    S = x1.shape[0]
    tm = 256

    def body(dx2_ref, gate_ref, up_ref, x1_ref, g2_ref, wg_ref, wu_ref, wd_ref,
             dgate_ref, dup_ref, dx1_ref, dg2_ref):
        @pl.when(pl.program_id(0) == 0)
        def _():
            dg2_ref[...] = jnp.zeros_like(dg2_ref)

        dx2v = dx2_ref[...]
        dact = _dot_nt(dx2v.astype(BF), wd_ref[...])
        gate = gate_ref[...]
        sg = _sigmoid(gate)
        dgate = (dact * up_ref[...] * (sg * (1.0 + gate * (1.0 - sg)))).astype(BF)
        dup = (dact * (gate * sg)).astype(BF)
        dgate_ref[...] = dgate
        dup_ref[...] = dup
        dh2 = _dot(dgate, wg_ref[...]) + _dot(dup, wu_ref[...])
        x1v = x1_ref[...]
        r2 = lax.rsqrt(jnp.mean(x1v * x1v, axis=-1, keepdims=True) + EPS)
        xh = x1v * r2
        dg2_ref[...] += jnp.sum(dh2 * xh, axis=0, keepdims=True)
        dxh = dh2 * g2_ref[...]
        dx1_ref[...] = dx2v + r2 * (dxh - xh * jnp.mean(dxh * xh, axis=-1, keepdims=True))

    return pl.pallas_call(
        body, name="ffn_bwd", grid=(S // tm,),
        in_specs=[_rows(tm, 1024), _rows(tm, D_FF), _rows(tm, D_FF), _rows(tm, 1024), _fixed((1, 1024)),
                  _VMEM, _VMEM, _VMEM],
        out_specs=[_rows(tm, D_FF), _rows(tm, D_FF), _rows(tm, 1024), _fixed((1, 1024))],
        out_shape=[jax.ShapeDtypeStruct((S, D_FF), BF), jax.ShapeDtypeStruct((S, D_FF), BF),
                   jax.ShapeDtypeStruct((S, 1024), F32), jax.ShapeDtypeStruct((1, 1024), F32)],
        compiler_params=_params(("arbitrary",)),
    )(dx2, gate, up, x1, g2, w_gate, w_up, w_down)


def _merge_bwd(dx1, ga, gb, u, ay, rg, ry, g_ret, w_ret, w_attn, w_out):
    S = dx1.shape[0]
    tm = 512

    def body(dx1_ref, ga_ref, gb_ref, u_ref, ay_ref, rg_ref, ry_ref, gr_ref, wr_ref, wa_ref, wo_ref,
             dgates_ref, drg_ref, dba_ref, dbb_ref, day_ref, dry_ref, dgr_ref):
        @pl.when(pl.program_id(0) == 0)
        def _():
            dgr_ref[...] = jnp.zeros_like(dgr_ref)

        dm = _dot_nt(dx1_ref[...].astype(BF), wo_ref[...])
        ba = _dot(u_ref[...], wr_ref[...])
        bb = _dot(ay_ref[...], wa_ref[...])
        sa = _sigmoid(ga_ref[...].astype(F32))
        sb = _sigmoid(gb_ref[...].astype(F32))
        dga = (dm * ba * (sa * (1.0 - sa))).astype(BF)
        dgb = (dm * bb * (sb * (1.0 - sb))).astype(BF)
        dgates_ref[:, :1024] = dga
        dgates_ref[:, 1024:] = dgb
        dba = (dm * sa).astype(BF)
        dbb = (dm * sb).astype(BF)
        dba_ref[...] = dba
        dbb_ref[...] = dbb
        day_ref[...] = _dot_nt(dbb, wa_ref[...]).astype(BF)
        du = _dot_nt(dba, wr_ref[...])
        for hd in range(RET_HEADS):
            vs = slice(hd * 256, (hd + 1) * 256)
            yn, rstd = _group_norm(ry_ref[:, vs].astype(F32))
            rgv = rg_ref[:, vs].astype(F32)
            sr = _sigmoid(rgv)
            silu = rgv * sr
            duv = du[:, vs]
            grv = gr_ref[:, vs]
            drg = (duv * (yn * grv) * (sr * (1.0 + rgv * (1.0 - sr)))).astype(BF)
            drg_ref[:, vs] = drg
            dyg = duv * silu
            dgr_ref[:, vs] += jnp.sum(dyg * yn, axis=0, keepdims=True)
            dyn = dyg * grv
            dry_ref[:, vs] = (rstd * (dyn - jnp.mean(dyn, axis=-1, keepdims=True)
                                      - yn * jnp.mean(dyn * yn, axis=-1, keepdims=True))).astype(KEPT)

    return pl.pallas_call(
        body, name="merge_bwd", grid=(S // tm,),
        in_specs=[_rows(tm, 1024)] * 7 + [_fixed((1, 1024)), _VMEM, _VMEM, _VMEM],
        out_specs=[_rows(tm, 2048), _rows(tm, 1024), _rows(tm, 1024), _rows(tm, 1024), _rows(tm, 1024),
                   _rows(tm, 1024), _fixed((1, 1024))],
        out_shape=[jax.ShapeDtypeStruct((S, 2048), BF), jax.ShapeDtypeStruct((S, 1024), BF),
                   jax.ShapeDtypeStruct((S, 1024), BF), jax.ShapeDtypeStruct((S, 1024), BF),
                   jax.ShapeDtypeStruct((S, 1024), BF), jax.ShapeDtypeStruct((S, 1024), KEPT),
                   jax.ShapeDtypeStruct((1, 1024), F32)],
        compiler_params=_params(("arbitrary",)),
    )(dx1, ga, gb, u, ay, rg, ry, g_ret, w_ret, w_attn, w_out)


def _attn_bwd(aq, ak, av, day, sinks, ca, sa, after):
    S = aq.shape[0]
    N = S // SEQ_CHUNK
    steps = N // 2

    def body(sk_ref, q_ref, kp_ref, kc_ref, vp_ref, vc_ref, do_ref, cq_ref, sq_ref, cp_ref, sp_ref, after_ref,
             dq_ref, dke_ref, dve_ref, dko_ref, dvo_ref, dsk_ref, carry_k, carry_v):
        i = pl.program_id(0)

        @pl.when(i == 0)
        def _():
            carry_k[...] = jnp.zeros_like(carry_k)
            carry_v[...] = jnp.zeros_like(carry_v)
            dsk_ref[...] = jnp.zeros_like(dsk_ref)

        @pl.when(i < steps)
        def _():
            kall, kall_sw = _kv_halves(kp_ref, kc_ref)
            vall, vall_sw = _kv_halves(vp_ref, vc_ref)
            lo = lax.broadcasted_iota(jnp.int32, (256, 128), 1) < 64
            blocks = []
            for sb in range(2):
                kv = slice(sb * 128, sb * 128 + 256)
                rows = slice(sb * 128, (sb + 1) * 128)
                ops = [_group_operands(g, kall[kv], kall_sw[kv], vall[kv], vall_sw[kv]) for g in range(2)]
                stacked = [(_stack_pairs(q_ref, g, rows), _stack_pairs(do_ref, g, rows)) for g in range(2)]
                raw = [_dot_nt(ops[g][0], stacked[g][0]) for g in range(2)]
                raw_dp = [_dot_nt(ops[g][1], stacked[g][1]) for g in range(2)]
                blocks.append((rows, ops, stacked, raw, raw_dp, _attn_fold(2 * i + sb)))
            grads = []
            d_sink = [jnp.zeros((1, PAIRS * 128), F32) for _ in range(4)]
            for rows, ops, stacked, raw, raw_dp, fold in blocks:
                cur = fold[0]
                cq, sq = cq_ref[rows, :], sq_ref[rows, :]
                dk_groups, dv_groups = [], []
                for g in range(2):
                    qs, dos = stacked[g]
                    ds_parts, p_parts = [], []
                    for which in range(2):
                        half = slice(which * 256, (which + 1) * 256)
                        pr, ps = _attn_probs(raw[g][half], fold, _sink_row(sk_ref, g, which))
                        dp = _fold(raw_dp[g][half], cur)
                        dr = jnp.sum(pr * dp, axis=0, keepdims=True)
                        ds_parts.append(_unfold(pr * (dp - dr) * (ATTN_HEAD_DIM ** -0.5), fold))
                        p_parts.append(_unfold(pr, fold))
                        d_sink[2 * g + which] = d_sink[2 * g + which] - ps * dr
                    ds = jnp.concatenate(ds_parts, axis=0)
                    dq = _dot_tn(ds, ops[g][0])
                    dk_both = _dot(ds, qs)
                    dv_both = _dot(jnp.concatenate(p_parts, axis=0), dos)
                    for p in range(PAIRS):
                        sl = slice((4 * g + p) * 128, (4 * g + p + 1) * 128)
                        dq_ref[rows, sl] = _rot_attn_bwd(dq[p * 128:(p + 1) * 128], cq, sq).astype(BF)
                    tk = jnp.where(lo, dk_both[:256], dk_both[256:])
                    tv = jnp.where(lo, dv_both[:256], dv_both[256:])
                    dk_groups.append(tk + pltpu.roll(tk, 64, 1))
                    dv_groups.append(tv + pltpu.roll(tv, 64, 1))
                grads.append((jnp.where(lo, dk_groups[0], dk_groups[1]), jnp.where(lo, dv_groups[0], dv_groups[1])))
            dsk_ref[...] += jnp.concatenate(d_sink, axis=0)
            (dk_a, dv_a), (dk_b, dv_b) = grads
            dko_ref[...] = _rot_attn_bwd(carry_k[...] + dk_a[:128], cp_ref[...], sp_ref[...]).astype(BF)
            dvo_ref[...] = (carry_v[...] + dv_a[:128]).astype(BF)
            dke_ref[...] = _rot_attn_bwd(dk_a[128:] + dk_b[:128], cq_ref[0:128, :], sq_ref[0:128, :]).astype(BF)
            dve_ref[...] = (dv_a[128:] + dv_b[:128]).astype(BF)
            carry_k[...] = dk_b[128:]
            carry_v[...] = dv_b[128:]

        @pl.when(i == steps)
        def _():
            dko_ref[...] = _rot_attn_bwd(carry_k[...], cp_ref[...], sp_ref[...]).astype(BF)
            dvo_ref[...] = carry_v[...].astype(BF)

    two = lambda i: (jnp.minimum(i, steps - 1), 0)
    before = lambda i: (jnp.maximum(2 * i - 1, 0), 0)
    odd = lambda i: (jnp.maximum(i - 1, 0), 0)
    blk = lambda r, w, im: pl.BlockSpec((r, w), im)
    half = jax.ShapeDtypeStruct((S // 2, 128), BF)
    dq, dk_even, dv_even, dk_odd, dv_odd, dsk = pl.pallas_call(
        body, name="attn_bwd", grid=(steps + 1,),
        in_specs=[_SMEM, blk(256, 1024, two), blk(128, 128, before), blk(256, 128, two), blk(128, 128, before),
                  blk(256, 128, two), blk(256, 1024, two), blk(256, 128, two), blk(256, 128, two),
                  blk(128, 128, before), blk(128, 128, before), _ANY],
        out_specs=[blk(256, 1024, two), blk(128, 128, two), blk(128, 128, two), blk(128, 128, odd),
                   blk(128, 128, odd), _fixed((4, PAIRS * 128))],
        out_shape=[jax.ShapeDtypeStruct((S, 1024), BF), half, half, half, half,
                   jax.ShapeDtypeStruct((4, PAIRS * 128), F32)],
        scratch_shapes=[pltpu.VMEM((128, 128), F32), pltpu.VMEM((128, 128), F32)],
        compiler_params=_params(("arbitrary",)),
    )(sinks, aq, ak, ak, av, av, day, ca, sa, ca, sa, after)

    def interleave(even, odd_):
        return jnp.stack([even.reshape(steps, 128, 128), odd_.reshape(steps, 128, 128)], axis=1).reshape(S, 128)

    return dq, interleave(dk_even, dk_odd), interleave(dv_even, dv_odd), dsk


def _ret_bwd(rq, rk, rv, dry, states, dmat, qd, kd, cd, cr, sr, after):
    S = rq.shape[0]
    N = S // RET_CHUNK
    scale = RET_QK_DIM ** -0.5
    cps = RET_CHUNKS_PER_STEP
    steps = N // cps

    def body(q_ref, k_ref, v_ref, dy_ref, st_ref, dm_ref, qd_ref, kd_ref, cd_ref, cr_ref, sr_ref, after_ref,
             out_ref, dstate):
        @pl.when(pl.program_id(0) == 0)
        def _():
            dstate[...] = jnp.zeros_like(dstate)

        for hd in range(RET_HEADS):
            qs = slice(hd * 128, (hd + 1) * 128)
            vs = slice(hd * 256, (hd + 1) * 256)
            dmv = dm_ref[hd]
            free = []
            for ch in range(cps):
                rs = slice(ch * RET_CHUNK, (ch + 1) * RET_CHUNK)
                crv, srv = cr_ref[rs, :], sr_ref[rs, :]
                q = q_ref[rs, qs]
                kf = k_ref[rs, qs]
                k = kf.astype(BF)
                v = v_ref[rs, vs]
                do = dy_ref[rs, vs].astype(F32)
                dob = do.astype(BF)
                doq = (do * qd_ref[hd]).astype(BF)
                a = (_dot_nt(q, k) * dmv).astype(BF)
                da = (_dot_nt(dob, v) * dmv).astype(BF)
                dq = _dot(da, k) + _dot_nt(doq, st_ref[ch, hd])
                out_ref[rs, qs] = _rot_ret_bwd(dq * scale, crv, srv).astype(BF)
                free.append((rs, crv, srv, v, (kf * kd_ref[hd]).astype(BF),
                             _dot_tn(a, dob), _dot_tn(da, q), _dot_tn(q, doq)))
            ds = dstate[hd]
            for rs, crv, srv, v, kdv, dv_free, dk_free, ds_update in reversed(free):
                dsb = ds.astype(BF)
                dv = dv_free + _dot(kdv, dsb)
                dk = dk_free + _dot_nt(v, dsb) * kd_ref[hd]
                ds = ds * cd_ref[hd] + ds_update
                out_ref[rs, 512 + hd * 128:512 + (hd + 1) * 128] = _rot_ret_bwd(dk, crv, srv).astype(BF)
                out_ref[rs, 1024 + hd * 256:1024 + (hd + 1) * 256] = dv.astype(BF)
            dstate[hd] = ds

    rev = lambda w: pl.BlockSpec((cps * RET_CHUNK, w), lambda i: (steps - 1 - i, 0))
    return pl.pallas_call(
        body, name="ret_bwd", grid=(steps,),
        in_specs=[rev(512), rev(512), rev(1024), rev(1024),
                  pl.BlockSpec((cps, RET_HEADS, 128, 256), lambda i: (steps - 1 - i, 0, 0, 0)),
                  _VMEM, _VMEM, _VMEM, _VMEM, rev(128), rev(128), _ANY],
        out_specs=rev(2048),
        out_shape=jax.ShapeDtypeStruct((S, 2048), BF),
        scratch_shapes=[pltpu.VMEM((RET_HEADS, 128, 256), F32)],
        compiler_params=_params(("arbitrary",)),
    )(rq, rk, rv, dry, states, dmat, qd, kd, cd, cr, sr, after)


def _proj_bwd(d_pieces, x, dx1, g1, w_in_t):
    S = x.shape[0]
    tm = 512
    widths = [d.shape[1] for d in d_pieces]
    assert sum(widths) == D_IN
    nd = len(d_pieces)

    def body(*refs):
        d_refs = refs[:nd]
        x_ref, dx1_ref, g_ref, w_ref, gx_ref, dg_ref, db_ref = refs[nd:]

        @pl.when(pl.program_id(0) == 0)
        def _():
            dg_ref[...] = jnp.zeros_like(dg_ref)
            db_ref[...] = jnp.zeros_like(db_ref)

        d_all = jnp.concatenate([ref[...] for ref in d_refs], axis=1)
        dh = _dot(d_all, w_ref[...])
        db_ref[...] += jnp.sum(d_all.astype(F32), axis=0, keepdims=True)
        xv = x_ref[...]
        r = lax.rsqrt(jnp.mean(xv * xv, axis=-1, keepdims=True) + EPS)
        xh = xv * r
        dg_ref[...] += jnp.sum(dh * xh, axis=0, keepdims=True)
        dxh = dh * g_ref[...]
        gx_ref[...] = dx1_ref[...] + r * (dxh - xh * jnp.mean(dxh * xh, axis=-1, keepdims=True))

    return pl.pallas_call(
        body, name="proj_bwd", grid=(S // tm,),
        in_specs=[_rows(tm, n) for n in widths] + [_rows(tm, 1024), _rows(tm, 1024), _fixed((1, 1024)), _VMEM],
        out_specs=[_rows(tm, 1024), _fixed((1, 1024)), _fixed((1, D_IN))],
        out_shape=[jax.ShapeDtypeStruct((S, 1024), F32), jax.ShapeDtypeStruct((1, 1024), F32),
                   jax.ShapeDtypeStruct((1, D_IN), F32)],
        compiler_params=_params(("arbitrary",)),
    )(*d_pieces, x, dx1, g1, w_in_t)


SHARD_ROWS = (1600, 256, 256, 256, 704, 704, 704)


class Group(NamedTuple):
    name: str
    tensors: tuple
    offs: tuple
    block: int
    tile: int

    @property
    def half(self):
        return self.block // 2

    def off(self, t):
        return self.offs[self.tensors.index(t)]


LATE = Group("late", (0,), (0,), 1600, 800)
EARLY = Group("early", (4, 5, 6, 1, 2, 3), (0, 704, 1408, 2112, 2368, 2624), 2880, 480)
GROUP_OF = {t: g for g in (LATE, EARLY) for t in g.tensors}


def _wgrad(a_list, b, t, packed, name):
    S = b.shape[0]
    widths = [a.shape[1] for a in a_list]
    M = sum(widths)
    ts = 1024 if M <= 4096 else 512
    steps = S // ts
    na = len(a_list)
    grp = GROUP_OF[t]
    r, off = SHARD_ROWS[t], grp.off(t)
    assert M == N_CHIPS * r
    segments = [(chip * r, r, chip * grp.block + off) for chip in range(N_CHIPS)]
    stage_rows = r
    n_in = na + 1 + (2 if packed is not None else 0)

    def body(*refs):
        a_refs, b_ref = refs[:na], refs[na]
        g32_ref, gb_ref, acc, stage, sems = refs[n_in:]
        i = pl.program_id(0)

        @pl.when(i == 0)
        def _():
            acc[...] = jnp.zeros_like(acc)

        bv = b_ref[...].astype(BF)
        lo = 0
        for a_ref, n in zip(a_refs, widths):
            acc[lo:lo + n, :] += _dot_tn(a_ref[...].astype(BF), bv)
            lo += n

        @pl.when(i == steps - 1)
        def _():
            for a0, n, p0 in segments:
                c32 = pltpu.make_async_copy(acc.at[a0:a0 + n, :], g32_ref.at[pl.ds(p0, n), :], sems.at[0])
                c32.start()
                stage[0:n, :] = acc[a0:a0 + n, :].astype(BF)
                c16 = pltpu.make_async_copy(stage.at[0:n, :], gb_ref.at[pl.ds(p0, n), :], sems.at[1])
                c16.start()
                c32.wait()
                c16.wait()

    shapes = [jax.ShapeDtypeStruct((N_CHIPS * grp.block, 1024), F32),
              jax.ShapeDtypeStruct((N_CHIPS * grp.block, 1024), BF)]
    operands = list(a_list) + [b]
    in_specs = [_rows(ts, n) for n in widths] + [_rows(ts, 1024)]
    aliases = {}
    if packed is not None:
        operands += list(packed)
        in_specs += [_ANY, _ANY]
        aliases = {na + 1: 0, na + 2: 1}
    return pl.pallas_call(
        body, name=name, grid=(steps,),
        in_specs=in_specs, out_specs=[_ANY, _ANY], out_shape=shapes,
        scratch_shapes=[pltpu.VMEM((M, 1024), F32), pltpu.VMEM((stage_rows, 1024), BF),
                        pltpu.SemaphoreType.DMA((2,))],
        input_output_aliases=aliases,
        compiler_params=_params(("arbitrary",)),
    )(*operands)


def _stage_weights(place, ws, name, after=()):
    nw = len(ws)

    def body(p_ref, *refs):
        for w_ref, o_ref in zip(refs[:nw], refs[nw + len(after):]):
            o_ref[...] = w_ref[...].astype(BF)

    return pl.pallas_call(
        body, name=name,
        grid_spec=pltpu.PrefetchScalarGridSpec(
            num_scalar_prefetch=1, grid=(1,),
            in_specs=[pl.BlockSpec(w.shape, lambda i, p: (0, 0)) for w in ws] + [_ANY] * len(after),
            out_specs=[pl.BlockSpec(w.shape, lambda i, p: (p[0], 0)) for w in ws]),
        out_shape=[jax.ShapeDtypeStruct((N_CHIPS * w.shape[0], 1024), BF) for w in ws],
        compiler_params=_params(("arbitrary",)),
    )(place, *ws, *after)


def _kept_block(grp, nb):
    return lambda i, p: ((2 * (i // nb) + p[1]) * nb + i % nb, 0)


def _pair_sum(place, g32, got, grp):
    nb = grp.half // grp.tile

    def body(p_ref, g_ref, r_ref, s16_ref):
        s16_ref[...] = (g_ref[...] + r_ref[...].astype(F32)).astype(BF)

    flat = pl.BlockSpec((grp.tile, 1024), lambda i, p: (i, 0))
    return pl.pallas_call(
        body, name="pair_sum_" + grp.name,
        grid_spec=pltpu.PrefetchScalarGridSpec(
            num_scalar_prefetch=1, grid=(N_CHIPS * nb,),
            in_specs=[pl.BlockSpec((grp.tile, 1024), _kept_block(grp, nb)), flat], out_specs=flat),
        out_shape=jax.ShapeDtypeStruct((N_CHIPS * grp.half, 1024), BF),
        compiler_params=_params(("arbitrary",)),
    )(place, g32, got)


def _chip_sum(place, g32, got, others, grp):
    nb = grp.half // grp.tile
    n_others = others.shape[0] // grp.half
    n_own = 1 if got is None else 2

    def body(p_ref, *refs):
        acc = refs[0][...]
        for r_ref in refs[1:n_own + n_others]:
            acc = acc + r_ref[...].astype(F32)
        refs[-1][...] = acc

    blk = lambda f: pl.BlockSpec((grp.tile, 1024), f)
    slot = lambda k: blk(lambda i, p: (k * nb + i, 0))
    own_specs = [blk(lambda i, p: ((2 * p[0] + p[1]) * nb + i, 0))]
    own_args = [g32]
    if got is not None:
        own_specs.append(blk(lambda i, p: (p[0] * nb + i, 0)))
        own_args.append(got)
    return pl.pallas_call(
        body, name="chip_sum_" + grp.name,
        grid_spec=pltpu.PrefetchScalarGridSpec(
            num_scalar_prefetch=1, grid=(nb,),
            in_specs=own_specs + [slot(k) for k in range(n_others)],
            out_specs=blk(lambda i, p: (p[1] * nb + i, 0))),
        out_shape=jax.ShapeDtypeStruct((grp.block, 1024), F32),
        compiler_params=_params(("arbitrary",)),
    )(place, *own_args, *[others] * n_others)


def _adamw(w, g, m, v, name, g_row0=0, after=None):
    r = w.shape[0]
    tr = next((c for c in (352, 320, 64) if r % c == 0 and g_row0 % c == 0), r)
    assert g_row0 % tr == 0
    g0 = g_row0 // tr
    extra = [] if after is None else [after]

    def body(w_ref, g_ref, m_ref, v_ref, *rest):
        go_ref, d_ref, mo_ref, vo_ref = rest[len(extra):]
        gv = g_ref[...]
        mn = ADAM_B1 * m_ref[...] + (1.0 - ADAM_B1) * gv
        vn = ADAM_B2 * v_ref[...] + (1.0 - ADAM_B2) * (gv * gv)
        m_hat = mn / (1.0 - ADAM_B1 ** ADAM_STEP)
        v_hat = vn / (1.0 - ADAM_B2 ** ADAM_STEP)
        go_ref[...] = gv
        d_ref[...] = -ADAM_LR * (m_hat / (jnp.sqrt(v_hat) + ADAM_EPS) + ADAM_WD * w_ref[...])
        mo_ref[...] = mn
        vo_ref[...] = vn

    return pl.pallas_call(
        body, name=name, grid=(r // tr,),
        in_specs=[_rows(tr, 1024), pl.BlockSpec((tr, 1024), lambda i: (g0 + i, 0)), _rows(tr, 1024), _rows(tr, 1024)]
        + [_ANY] * len(extra),
        out_specs=[_rows(tr, 1024)] * 4,
        out_shape=[jax.ShapeDtypeStruct((r, 1024), F32)] * 4,
        compiler_params=_params(("parallel",)),
    )(w, g, m, v, *extra)


def _place():
    x, y, c = lax.axis_index("x"), lax.axis_index("y"), lax.axis_index("c")
    chips = [(1 - x, y), (x, 1 - y), (1 - x, 1 - y)]
    return x, y, c, chips


_HBM =pl.BlockSpec(memory_space=pltpu.HBM)
_SEM = pl.BlockSpec(memory_space=pltpu.SEMAPHORE)
_DATAFLOW = pltpu.SideEffectType.DATAFLOW_SIDE_EFFECTING


def _exchange(name, arrays, n, plan):
    na = len(arrays)

    def body(*refs):
        pairs = plan(refs[na:2 * na], refs[2 * na], refs[2 * na + 1])
        for send, _ in pairs:
            send.start()
        for send, recv in pairs:
            send.wait_send()
            recv.wait_recv()

    return pl.pallas_call(
        body, name=name, in_specs=[_ANY] * na, out_specs=[_ANY] * na,
        out_shape=[jax.ShapeDtypeStruct(a.shape, a.dtype) for a in arrays],
        scratch_shapes=[pltpu.SemaphoreType.DMA((n,)), pltpu.SemaphoreType.DMA((n,))],
        input_output_aliases={i: i for i in range(na)},
    )(*arrays)


def _exchange_start(name, arrays, n, plan, after=()):
    na = len(arrays)

    def body(*refs):
        for send, _ in plan(refs[:na], refs[na + len(after)], refs[na + len(after) + 1]):
            send.start()
        refs[-1][...] = jnp.zeros_like(refs[-1])

    out = pl.pallas_call(
        body, name=name,
        out_shape=(pltpu.SemaphoreType.DMA((n,)), pltpu.SemaphoreType.DMA((n,)),
                   *[pltpu.HBM(a.shape, a.dtype) for a in arrays], jax.ShapeDtypeStruct((8, LANES), F32)),
        in_specs=[_HBM] * na + [_ANY] * len(after), out_specs=(_SEM, _SEM, *[_HBM] * na, _VMEM),
        input_output_aliases={i: 2 + i for i in range(na)},
        compiler_params=pltpu.CompilerParams(has_side_effects=_DATAFLOW),
    )(*[pltpu.with_memory_space_constraint(a, pltpu.HBM) for a in arrays], *after)
    return (out[0], out[1]), list(out[2:2 + na]), out[-1]


def _exchange_wait(name, sems, in_flight, plan, after=()):
    na = len(in_flight)

    def body(*refs):
        for send, recv in plan(refs[:na], refs[na], refs[na + 1]):
            send.wait_send()
            recv.wait_recv()

    out = pl.pallas_call(
        body, name=name,
        out_shape=tuple(pltpu.HBM(a.shape, a.dtype) for a in in_flight),
        in_specs=[_HBM] * na + [_SEM, _SEM] + [_ANY] * len(after), out_specs=tuple([_HBM] * na),
        input_output_aliases={i: i for i in range(na)},
        compiler_params=pltpu.CompilerParams(has_side_effects=_DATAFLOW),
    )(*in_flight, *sems, *after)
    return list(out)


def _remote(src, dst, send_sems, recv_sems, k, to):
    return pltpu.make_async_remote_copy(src_ref=src, dst_ref=dst, send_sem=send_sems.at[k], recv_sem=recv_sems.at[k],
                                        device_id=to, device_id_type=MESH)


def _half_rows(ref, tid, pj, half):
    h = SHARD_ROWS[tid] // 2
    return ref.at[pl.ds((2 * pj + half) * h, h), :]


def _plan_gather_halves(tids):
    def plan(refs, ss, rs):
        x, y, c, chips = _place()
        pairs = []
        for t, tid in enumerate(tids):
            mine = _half_rows(refs[t], tid, 2 * x + y, c)
            for k, (cx, cy) in enumerate(chips):
                theirs = _half_rows(refs[t], tid, 2 * cx + cy, c)
                pairs.append((_remote(mine, mine, ss, rs, 3 * t + k, (cx, cy, c)),
                              _remote(theirs, theirs, ss, rs, 3 * t + k, (x, y, c))))
        return pairs
    return plan


def _plan_forward_halves(tids):
    def plan(refs, ss, rs):
        x, y, c, chips = _place()
        pairs = []
        for t, tid in enumerate(tids):
            for k, (cx, cy) in enumerate(chips):
                got = _half_rows(refs[t], tid, 2 * cx + cy, c)
                missing = _half_rows(refs[t], tid, 2 * cx + cy, 1 - c)
                pairs.append((_remote(got, got, ss, rs, 3 * t + k, (x, y, 1 - c)),
                              _remote(missing, missing, ss, rs, 3 * t + k, (x, y, 1 - c))))
        return pairs
    return plan


def _plan_pair_exchange(grp):
    def plan(refs, ss, rs):
        x, y, c, _ = _place()
        cps = [_remote(refs[0].at[pl.ds(pj * grp.block + (1 - c) * grp.half, grp.half), :],
                       refs[1].at[pl.ds(pj * grp.half, grp.half), :], ss, rs, pj, (x, y, 1 - c))
               for pj in range(N_CHIPS)]
        return [(cp, cp) for cp in cps]
    return plan


def _plan_to_chips(grp):
    def plan(refs, ss, rs):
        x, y, c, chips = _place()
        cps = [_remote(refs[0].at[pl.ds((2 * cx + cy) * grp.half, grp.half), :],
                       refs[1].at[pl.ds(k * grp.half, grp.half), :], ss, rs, k, (cx, cy, c))
               for k, (cx, cy) in enumerate(chips)]
        return [(cp, cp) for cp in cps]
    return plan


def _plan_to_all(grp):
    def plan(refs, ss, rs):
        x, y, c, chips = _place()
        peers = [(x, y, 1 - c)] + [(cx, cy, pc) for cx, cy in chips for pc in (c, 1 - c)]
        cps = [_remote(refs[0].at[pl.ds((2 * px + py) * grp.block + pc * grp.half, grp.half), :],
                       refs[1].at[pl.ds(k * grp.half, grp.half), :], ss, rs, k, (px, py, pc))
               for k, (px, py, pc) in enumerate(peers)]
        return [(cp, cp) for cp in cps]
    return plan


def _plan_share_halves(grp):
    def plan(refs, ss, rs):
        x, y, c, _ = _place()
        mine = refs[0].at[pl.ds(c * grp.half, grp.half), :]
        other = refs[0].at[pl.ds((1 - c) * grp.half, grp.half), :]
        return [(_remote(mine, mine, ss, rs, 0, (x, y, 1 - c)), _remote(other, other, ss, rs, 0, (x, y, 1 - c)))]
    return plan


def _empty(rows, dtype):
    return lax.empty((rows, 1024), dtype)


def _all_reduce_small(v, after):
    m_per, n = v.shape
    n_dev = 8

    def body(x_ref, after_ref, out_ref, all_ref, send_sems, recv_sems, local_sem):
        x, y, c, chips = _place()
        me, sibling = (x, y, c), (x, y, 1 - c)

        def rows(px, py, pc):
            return all_ref.at[pl.ds((4 * px + 2 * py + pc) * m_per, m_per), :]

        def copy(k, block, to, src=None):
            return pltpu.make_async_remote_copy(
                src_ref=rows(*block) if src is None else src, dst_ref=rows(*block),
                send_sem=send_sems.at[k], recv_sem=recv_sems.at[k], device_id=to, device_id_type=MESH)

        mine = pltpu.make_async_copy(x_ref, rows(*me), local_sem)
        mine.start()
        first = [copy(0, me, sibling, src=x_ref)]
        first += [copy(1 + k, me, (*chip, c), src=x_ref) for k, chip in enumerate(chips)]
        for cp in first:
            cp.start()
        passed = [copy(4 + k, (*chip, c), sibling) for k, chip in enumerate(chips)]
        for k, chip in enumerate(chips):
            copy(1 + k, (*chip, c), me).wait_recv()
            passed[k].start()
        copy(0, sibling, me).wait_recv()
        for k, chip in enumerate(chips):
            copy(4 + k, (*chip, 1 - c), me).wait_recv()
        for cp in first + passed:
            cp.wait_send()
        mine.wait()
        acc = all_ref[pl.ds(0, m_per), :]
        for d in range(1, n_dev):
            acc = acc + all_ref[pl.ds(d * m_per, m_per), :]
        out_ref[...] = acc

    return pl.pallas_call(
        body, name="all_reduce_small",
        in_specs=[_VMEM, _ANY], out_specs=_VMEM,
        out_shape=jax.ShapeDtypeStruct((m_per, n), F32),
        scratch_shapes=[pltpu.VMEM((n_dev * m_per, n), F32),
                        pltpu.SemaphoreType.DMA((7,)), pltpu.SemaphoreType.DMA((7,)), pltpu.SemaphoreType.DMA],
        compiler_params=pltpu.CompilerParams(has_side_effects=True),
    )(v, after)


def _tables(S, token):
    pos = jnp.arange(S, dtype=jnp.int32).astype(F32) + token[0, 0]

    def cs(half):
        inv_freq = ROPE_THETA ** (-jnp.arange(half, dtype=F32) / half)
        ang = pos[:, None] * inv_freq[None, :]
        return jnp.cos(ang), jnp.sin(ang)

    c64, s64 = cs(64)
    c32, s32 = cs(32)
    cr = jnp.concatenate([c64, c64], axis=1)
    sr = jnp.concatenate([-s64, s64], axis=1)
    ca = jnp.concatenate([c32, c32, c32, c32], axis=1)
    sa = jnp.concatenate([-s32, s32, -s32, s32], axis=1)
    return cr, sr, ca, sa


def _decays():
    H, C = RET_HEADS, RET_CHUNK
    log_gamma = jnp.log1p(-jnp.exp2(-5.0 - jnp.arange(H, dtype=F32)))
    idx = jnp.arange(C, dtype=F32)
    rel = idx[:, None] - idx[None, :]
    dmat = jnp.where(rel[None] >= 0, jnp.exp(log_gamma[:, None, None] * jnp.maximum(rel, 0.0)[None]), 0.0)
    qd = jnp.exp(log_gamma[:, None] * (idx + 1.0))[:, :, None]
    kd = jnp.exp(log_gamma[:, None] * (C - 1.0 - idx))[:, :, None]
    cd = jnp.exp(log_gamma * C)[:, None, None]
    return dmat, qd, kd, cd


def _local_step(place, x, tgt, ln1_g, b_in, ret_norm_g, attn_sinks, ln2_g, lnf_g, w_in, other_weights, tables, decays):
    cr, sr, ca, sa = tables
    dmat, qd, kd, cd = decays
    forward_weights, landed_weights = other_weights

    h, rq, rk, rv, rg, aq, ak, av, ga, gb = _proj_fwd(x, ln1_g, w_in, b_in, cr, sr, ca, sa)
    ry, states = _ret_fwd(rq, rk, rv, dmat, qd, kd, cd)
    token = forward_weights(ry)
    ay = _attn_fwd(aq, ak, av, attn_sinks + token[0, 0])
    w_ret, w_attn, w_out, w_gate, w_up, w_down = landed_weights(ay)
    u, merged, x1 = _merge_fwd(x, ry, rg, ay, ga, gb, ret_norm_g, w_ret, w_attn, w_out)
    gate, up, h2, act, dx2, loss, dg3 = _ffn_fwd(x1, tgt, ln2_g, lnf_g, w_gate, w_up, w_down)

    dgate, dup, dx1, dg2 = _ffn_bwd(dx2, gate, up, x1, ln2_g, w_gate, w_up, w_down)
    early = _wgrad([act], dx2, 6, None, "wgrad_down")
    early = _wgrad([dgate], h2, 4, early, "wgrad_gate")
    early = _wgrad([dup], h2, 5, early, "wgrad_up")
    dgates, drg, dba, dbb, day, dry, dgr = _merge_bwd(dx1, ga, gb, u, ay, rg, ry, ret_norm_g,
                                                                w_ret, w_attn, w_out)
    early = _wgrad([merged], dx1, 3, early, "wgrad_out")
    early = _wgrad([u], dba, 1, early, "wgrad_ret")
    e32, e16 = _wgrad([ay], dbb, 2, early, "wgrad_attn")

    grp = EARLY
    sems, flight, token = _exchange_start("to_all_start_early", [e16, _empty(7 * grp.half, BF)], 7, _plan_to_all(grp))
    daq, dak, dav, dsk = _attn_bwd(aq, ak, av, day, attn_sinks, ca, sa, token)
    d_ret = _ret_bwd(rq, rk, rv, dry, states, dmat, qd, kd, cd, cr, sr, token)
    d_proj = [d_ret, drg, daq, dak, dav, dgates]
    grad_x, dg1, db = _proj_bwd(d_proj, x, dx1, ln1_g, w_in)
    late = _wgrad(d_proj, h, 0, None, "wgrad_in")

    def finish_early(after):
        _, others = _exchange_wait("to_all_wait_early", sems, flight, _plan_to_all(grp), after=after)
        return _chip_sum(place, e32, None, others, grp)

    d_sinks = dsk.reshape(2, 2, PAIRS, 128).sum(-1).transpose(0, 2, 1).reshape(1, ATTN_Q_HEADS)
    small = dict(loss=loss, ln1_g=dg1, b_in=db, ret_norm_g=dgr, attn_sinks=d_sinks,
                 ln2_g=dg2, lnf_g=dg3)
    return grad_x, finish_early, late, small


SMALL_ROWS = 16


def _pack_small(ln1, ret, ln2, lnf, b_in, sinks, loss, name):
    def body(a_ref, b_ref, c_ref, d_ref, bi_ref, sk_ref, ls_ref, o_ref):
        o_ref[...] = jnp.zeros_like(o_ref)
        for row, ref in enumerate((a_ref, b_ref, c_ref, d_ref)):
            o_ref[row:row + 1, :] = ref[...]
        for k in range(6):
            o_ref[4 + k:5 + k, :] = bi_ref[:, k * 1024:(k + 1) * 1024]
        o_ref[10:11, 0:D_IN - 6144] = bi_ref[:, 6144:D_IN]
        o_ref[11:12, 0:ATTN_Q_HEADS] = sk_ref[...]
        o_ref[12:13, 0:LANES] = ls_ref[...]

    return pl.pallas_call(
        body, name=name, in_specs=[_VMEM] * 7, out_specs=_VMEM,
        out_shape=jax.ShapeDtypeStruct((SMALL_ROWS, 1024), F32),
    )(ln1.reshape(1, 1024), ret.reshape(1, 1024), ln2.reshape(1, 1024), lnf.reshape(1, 1024),
      b_in.reshape(1, D_IN), sinks.reshape(1, ATTN_Q_HEADS), loss.reshape(1, LANES))


def _unpack_small(p):
    return dict(ln1_g=p[0:1], ret_norm_g=p[1:2], ln2_g=p[2:3], lnf_g=p[3], b_in=p[4:11].reshape(1, -1)[:, :D_IN],
                attn_sinks=p[11:12, :ATTN_Q_HEADS], loss=p[12, 0])


def kernel(x, ln1_g, w_in, b_in, ret_norm_g, w_ret_out, attn_sinks, w_attn_out, w_out, ln2_g, w_ffn_gate, w_ffn_up, w_ffn_down, lnf_g, loss_target, m_ln1_g, m_w_in, m_b_in, m_ret_norm_g, m_w_ret_out, m_attn_sinks, m_w_attn_out, m_w_out, m_ln2_g, m_w_ffn_gate, m_w_ffn_up, m_w_ffn_down, m_lnf_g, v_ln1_g, v_w_in, v_b_in, v_ret_norm_g, v_w_ret_out, v_attn_sinks, v_w_attn_out, v_w_out, v_ln2_g, v_w_ffn_gate, v_w_ffn_up, v_w_ffn_down, v_lnf_g):
    xi = lax.axis_index("x")
    yi = lax.axis_index("y")
    ci = lax.axis_index("c")
    j = 2 * xi + yi

    place = jnp.stack([j, ci]).astype(jnp.int32)

    names = ["w_in", "w_ret_out", "w_attn_out", "w_out", "w_ffn_gate", "w_ffn_up", "w_ffn_down"]
    transposed = [True, False, False, False, True, True, False]
    view = lambda a, t: a[0].T if t else a[0]
    big_w = [view(a, t) for a, t in zip([w_in, w_ret_out, w_attn_out, w_out, w_ffn_gate, w_ffn_up, w_ffn_down], transposed)]
    big_m = [view(a, t) for a, t in zip([m_w_in, m_w_ret_out, m_w_attn_out, m_w_out, m_w_ffn_gate, m_w_ffn_up, m_w_ffn_down], transposed)]
    big_v = [view(a, t) for a, t in zip([v_w_in, v_w_ret_out, v_w_attn_out, v_w_out, v_w_ffn_gate, v_w_ffn_up, v_w_ffn_down], transposed)]

    staged_in = _stage_weights(place, big_w[:1], "stage_w_in")[0]
    sems, in_flight, token = _exchange_start("gather_in_start", [staged_in], 3, _plan_gather_halves([0]))
    staged = _stage_weights(place, big_w[1:], "stage_others", after=(token,))
    tables = _tables(x.shape[1], token)
    decays = _decays()
    w_in_full = _exchange_wait("gather_in_wait", sems, in_flight, _plan_gather_halves([0]),
                               after=(*staged, *tables, *decays))[0]
    w_in_full = _exchange("gather_in_forward", [w_in_full], 3, _plan_forward_halves([0]))[0]
    rest = list(range(1, 7))
    crossing = _exchange_start("gather_start", staged, 3 * len(rest), _plan_gather_halves(rest), after=(w_in_full,))
    token = crossing[2]
    forwarding = []

    def forward_weights(after):
        arrived = _exchange_wait("gather_wait", crossing[0], crossing[1], _plan_gather_halves(rest), after=(after,))
        forwarding.extend(_exchange_start("forward_start", arrived, 3 * len(rest), _plan_forward_halves(rest)))
        return forwarding[2]

    def landed_weights(after):
        return _exchange_wait("forward_wait", forwarding[0], forwarding[1], _plan_forward_halves(rest), after=(after,))

    other_weights = (forward_weights, landed_weights)

    grad_x, finish_early, (l32, l16), small = _local_step(
        place, x[0], loss_target[0], ln1_g, b_in + token[0, 0], ret_norm_g, attn_sinks, ln2_g, lnf_g.reshape(1, -1),
        w_in_full, other_weights, tables, decays)
    pg = _all_reduce_small(_pack_small(
        small["ln1_g"], small["ret_norm_g"], small["ln2_g"], small["lnf_g"], small["b_in"],
        small["attn_sinks"], small["loss"], "pack_small_grads"), l32)
    early_shard = _exchange("share_early", [finish_early((l32,))], 1, _plan_share_halves(EARLY))[0]
    _, got = _exchange("pair_exchange_late", [l16, _empty(N_CHIPS * LATE.half, BF)], N_CHIPS, _plan_pair_exchange(LATE))
    s16 = _pair_sum(place, l32, got, LATE)
    sems, flight, token = _exchange_start("chips_start_late", [s16, _empty(3 * LATE.half, BF)], 3, _plan_to_chips(LATE),
                                          after=(pg, early_shard))

    upd = {}

    def update(t, shard, after):
        res = _adamw(big_w[t], shard, big_m[t], big_v[t], "adamw_" + names[t], g_row0=GROUP_OF[t].off(t), after=after)
        upd[names[t]] = tuple((r.T if transposed[t] else r)[None] for r in res)
        return res[0]

    done = [update(t, early_shard, token) for t in EARLY.tensors]

    sm = _unpack_small(pg)
    z = jnp.zeros((1, LANES), F32)
    pw = _pack_small(ln1_g, ret_norm_g, ln2_g, lnf_g, b_in, attn_sinks, z, "pack_small_w")
    pm = _pack_small(m_ln1_g, m_ret_norm_g, m_ln2_g, m_lnf_g, m_b_in, m_attn_sinks, z, "pack_small_m")
    pv = _pack_small(v_ln1_g, v_ret_norm_g, v_ln2_g, v_lnf_g, v_b_in, v_attn_sinks, z, "pack_small_v")
    small_res = _adamw(pw, pg, pm, pv, "adamw_small", after=token)
    sd, smn, svn = (_unpack_small(t) for t in small_res[1:])
    for n in ["ln1_g", "b_in", "ret_norm_g", "attn_sinks", "ln2_g", "lnf_g"]:
        upd[n] = (sm[n], sd[n], smn[n], svn[n])

    small_outs = [a for n in ("ln1_g", "b_in", "ret_norm_g", "attn_sinks", "ln2_g", "lnf_g") for a in upd[n]]
    _, others = _exchange_wait("chips_wait_late", sems, flight, _plan_to_chips(LATE), after=(*done, *small_outs))
    late_shard = _exchange("share_late", [_chip_sum(place, l32, got, others, LATE)], 1, _plan_share_halves(LATE))[0]
    update(0, late_shard, None)

    order_w = ["ln1_g", "w_in", "b_in", "ret_norm_g", "w_ret_out", "attn_sinks", "w_attn_out", "w_out",
               "ln2_g", "w_ffn_gate", "w_ffn_up", "w_ffn_down", "lnf_g"]
    outs = [sm["loss"], grad_x[None]]
    for k in range(4):
        outs += [upd[n][k] for n in order_w]
    return tuple(outs)
```

```python
from typing import NamedTuple

import jax
import jax.numpy as jnp
from jax import lax
from jax.experimental import pallas as pl
from jax.experimental.pallas import tpu as pltpu

D_MODEL = 1024
SEQ_CHUNK = 128
RET_CHUNK = 256
RET_CHUNKS_PER_STEP = 4
RET_HEADS = 4
RET_QK_DIM = 128
ATTN_Q_HEADS = 16
ATTN_HEAD_DIM = 64
D_FF = 2816
D_IN = 6400
ROPE_THETA = 10000.0
EPS = 1e-6
N_CHIPS = 4

ADAM_LR = 0.001
ADAM_B1 = 0.9
ADAM_B2 = 0.999
ADAM_EPS = 1e-08
ADAM_WD = 0.01
ADAM_STEP = 10

OFF_RQ, OFF_RK, OFF_RV, OFF_RG = 0, 512, 1024, 2048
OFF_AQ, OFF_AK, OFF_AV, OFF_GA, OFF_GB = 3072, 4096, 4224, 4352, 5376

V7X_VMEM_LIMIT = 56 * 1024 * 1024
LANES = 128

BF = jnp.bfloat16
F32 = jnp.float32
KEPT = jnp.bfloat16
MESH = pl.DeviceIdType.MESH

_VMEM = pl.BlockSpec(memory_space=pltpu.VMEM)
_SMEM = pl.BlockSpec(memory_space=pltpu.SMEM)
_ANY = pl.BlockSpec(memory_space=pl.ANY)


def _dot(a, b):
    return jnp.dot(a, b, preferred_element_type=F32)


def _dot_nt(a, b):
    return lax.dot_general(a, b, (((1,), (1,)), ((), ())), preferred_element_type=F32)


def _dot_tn(a, b):
    return lax.dot_general(a, b, (((0,), (0,)), ((), ())), preferred_element_type=F32)


def _sigmoid(x):
    return 0.5 * jnp.tanh(0.5 * x) + 0.5


def _rows(tm, n):
    return pl.BlockSpec((tm, n), lambda i: (i, 0))


def _fixed(shape):
    nd = len(shape)
    return pl.BlockSpec(shape, lambda i: (0,) * nd)


def _params(sem, **kw):
    return pltpu.CompilerParams(dimension_semantics=sem, vmem_limit_bytes=V7X_VMEM_LIMIT, **kw)


def _rot_ret(p, cr, sr):
    return p * cr + pltpu.roll(p, 64, 1) * sr


def _rot_ret_bwd(d, cr, sr):
    return d * cr - pltpu.roll(d, 64, 1) * sr


def _attn_partner(p):
    lane = lax.broadcasted_iota(jnp.int32, p.shape, 1)
    first = (lane % 64) < 32
    return jnp.where(first, pltpu.roll(p, 96, 1), pltpu.roll(p, 32, 1))


def _rot_attn(p, ca, sa):
    return p * ca + _attn_partner(p) * sa


def _rot_attn_bwd(d, ca, sa):
    return d * ca - _attn_partner(d) * sa


def _proj_fwd(x, g1, w_in, b_in, cr, sr, ca, sa):
    S = x.shape[0]
    tm = 512
    scale = RET_QK_DIM ** -0.5

    def body(x_ref, g_ref, w_ref, b_ref, cr_ref, sr_ref, ca_ref, sa_ref,
             h_ref, rq_ref, rk_ref, rv_ref, rg_ref, aq_ref, ak_ref, av_ref, ga_ref, gb_ref):
        xv = x_ref[...]
        r = lax.rsqrt(jnp.mean(xv * xv, axis=-1, keepdims=True) + EPS)
        h = (xv * r * g_ref[...]).astype(BF)
        h_ref[...] = h

        def piece(lo, n):
            return _dot_nt(h, w_ref[lo:lo + n, :]) + b_ref[:, lo:lo + n]

        crv, srv = cr_ref[...], sr_ref[...]
        pq = piece(OFF_RQ, 512)
        pk = piece(OFF_RK, 512)
        for hd in range(RET_HEADS):
            sl = slice(hd * 128, (hd + 1) * 128)
            rq_ref[:, sl] = (_rot_ret(pq[:, sl], crv, srv) * scale).astype(BF)
            rk_ref[:, sl] = _rot_ret(pk[:, sl], crv, srv)
        rv_ref[...] = piece(OFF_RV, 1024).astype(BF)
        rg_ref[...] = piece(OFF_RG, 1024).astype(KEPT)
        cav, sav = ca_ref[...], sa_ref[...]
        pa = piece(OFF_AQ, 1024)
        for blk in range(8):
            sl = slice(blk * 128, (blk + 1) * 128)
            aq_ref[:, sl] = _rot_attn(pa[:, sl], cav, sav).astype(BF)
        pkv = piece(OFF_AK, 256)
        ak_ref[...] = _rot_attn(pkv[:, :128], cav, sav).astype(BF)
        av_ref[...] = pkv[:, 128:].astype(BF)
        ga_ref[...] = piece(OFF_GA, 1024).astype(KEPT)
        gb_ref[...] = piece(OFF_GB, 1024).astype(KEPT)

    outs = [(D_MODEL, BF), (512, BF), (512, F32), (1024, BF), (1024, KEPT),
            (1024, BF), (128, BF), (128, BF), (1024, KEPT), (1024, KEPT)]
    return pl.pallas_call(
        body, name="proj_fwd", grid=(S // tm,),
        in_specs=[_rows(tm, D_MODEL), _fixed((1, D_MODEL)), _VMEM, _VMEM,
                  _rows(tm, 128), _rows(tm, 128), _rows(tm, 128), _rows(tm, 128)],
        out_specs=[_rows(tm, n) for n, _ in outs],
        out_shape=[jax.ShapeDtypeStruct((S, n), dt) for n, dt in outs],
        compiler_params=_params(("parallel",)),
    )(x, g1, w_in, b_in, cr, sr, ca, sa)


def _ret_fwd(rq, rk, rv, dmat, qd, kd, cd):
    S = rq.shape[0]
    N = S // RET_CHUNK
    cps = RET_CHUNKS_PER_STEP
    rows = cps * RET_CHUNK

    def body(q_ref, k_ref, v_ref, dm_ref, qd_ref, kd_ref, cd_ref, y_ref, st_ref, state):
        @pl.when(pl.program_id(0) == 0)
        def _():
            state[...] = jnp.zeros_like(state)

        for hd in range(RET_HEADS):
            qs = slice(hd * 128, (hd + 1) * 128)
            vs = slice(hd * 256, (hd + 1) * 256)
            free = []
            for ch in range(cps):
                rs = slice(ch * RET_CHUNK, (ch + 1) * RET_CHUNK)
                q = q_ref[rs, qs]
                kf = k_ref[rs, qs]
                v = v_ref[rs, vs]
                a = _dot_nt(q, kf.astype(BF)) * dm_ref[hd]
                kdv = (kf * kd_ref[hd]).astype(BF)
                free.append((rs, q, _dot(a.astype(BF), v), _dot_tn(kdv, v)))
            s = state[hd]
            for ch, (rs, q, inner, update) in enumerate(free):
                sb = s.astype(BF)
                st_ref[ch, hd] = sb
                y_ref[rs, vs] = (inner + _dot(q, sb) * qd_ref[hd]).astype(KEPT)
                s = s * cd_ref[hd] + update
            state[hd] = s

    return pl.pallas_call(
        body, name="ret_fwd", grid=(N // cps,),
        in_specs=[_rows(rows, 512), _rows(rows, 512), _rows(rows, 1024), _VMEM, _VMEM, _VMEM, _VMEM],
        out_specs=[_rows(rows, 1024), pl.BlockSpec((cps, RET_HEADS, 128, 256), lambda i: (i, 0, 0, 0))],
        out_shape=[jax.ShapeDtypeStruct((S, 1024), KEPT),
                   jax.ShapeDtypeStruct((N, RET_HEADS, 128, 256), BF)],
        scratch_shapes=[pltpu.VMEM((RET_HEADS, 128, 256), F32)],
        compiler_params=_params(("arbitrary",)),
    )(rq, rk, rv, dmat, qd, kd, cd)


def _kv_halves(prev_ref, cur_ref):
    kk = jnp.concatenate([prev_ref[...], cur_ref[...]], axis=0).astype(F32)
    return kk, pltpu.roll(kk, 64, 1)


ATTN_FWD_BLOCKS_PER_STEP = 4
PAIRS = 4


def _attn_fold(n):
    t = lax.broadcasted_iota(jnp.int32, (128, PAIRS * 128), 0)
    qi = lax.broadcasted_iota(jnp.int32, (128, PAIRS * 128), 1) & 127
    cur = t <= qi
    cur_f = jnp.where(cur, 1.0, 0.0)
    bias = jnp.where(jnp.logical_or(cur, n > 0), 0.0, -1e30)
    return cur, cur_f, 1.0 - cur_f, bias


def _group_operands(g, kk, kk_sw, vv, vv_sw):
    lo = lax.broadcasted_iota(jnp.int32, (256, 128), 1) < 64
    k_a, k_b = (kk, kk_sw) if g == 0 else (kk_sw, kk)
    v_a, v_b = (vv, vv_sw) if g == 0 else (vv_sw, vv)
    return (jnp.concatenate([jnp.where(lo, k_a, 0.0), jnp.where(lo, 0.0, k_b)], axis=0).astype(BF),
            jnp.concatenate([jnp.where(lo, v_a, 0.0), jnp.where(lo, 0.0, v_b)], axis=0).astype(BF))


def _stack_pairs(ref, g, rows=slice(None)):
    return jnp.concatenate([ref[rows, (4 * g + p) * 128:(4 * g + p + 1) * 128] for p in range(PAIRS)], axis=0)


def _sink_row(sk_ref, g, which):
    return jnp.concatenate([jnp.full((1, 128), sk_ref[0, 8 * g + 2 * p + which], F32) for p in range(PAIRS)], axis=1)


def _fold(full, cur):
    return jnp.where(cur, full[128:], full[:128])


def _attn_probs(raw, fold, sink):
    s = _fold(raw, fold[0]) * (ATTN_HEAD_DIM ** -0.5) + fold[3]
    m = jnp.maximum(jnp.max(s, axis=0, keepdims=True), sink)
    e = jnp.exp(s - m)
    es = jnp.exp(sink - m)
    inv = 1.0 / (jnp.sum(e, axis=0, keepdims=True) + es)
    return e * inv, es * inv


def _unfold(p, fold):
    return jnp.concatenate([p * fold[2], p * fold[1]], axis=0).astype(BF)


def _attn_fwd(aq, ak, av, sinks):
    S = aq.shape[0]
    N = S // SEQ_CHUNK
    nb = ATTN_FWD_BLOCKS_PER_STEP

    def body(sk_ref, q_ref, kp_ref, kc_ref, vp_ref, vc_ref, o_ref):
        kall, kall_sw = _kv_halves(kp_ref, kc_ref)
        vall, vall_sw = _kv_halves(vp_ref, vc_ref)
        blocks = []
        for sb in range(nb):
            kv = slice(sb * 128, sb * 128 + 256)
            rows = slice(sb * 128, (sb + 1) * 128)
            ops = [_group_operands(g, kall[kv], kall_sw[kv], vall[kv], vall_sw[kv]) for g in range(2)]
            raw = [_dot_nt(ops[g][0], _stack_pairs(q_ref, g, rows)) for g in range(2)]
            blocks.append((rows, ops, raw, _attn_fold(nb * pl.program_id(0) + sb)))
        for rows, ops, raw, fold in blocks:
            for g in range(2):
                probs = [_unfold(_attn_probs(raw[g][w * 256:(w + 1) * 256], fold, _sink_row(sk_ref, g, w))[0], fold)
                         for w in range(2)]
                o = _dot_tn(jnp.concatenate(probs, axis=0), ops[g][1])
                for p in range(PAIRS):
                    o_ref[rows, (4 * g + p) * 128:(4 * g + p + 1) * 128] = o[p * 128:(p + 1) * 128].astype(BF)

    prev = lambda i: (jnp.maximum(nb * i - 1, 0), 0)
    return pl.pallas_call(
        body, name="attn_fwd", grid=(N // nb,),
        in_specs=[_SMEM, _rows(nb * 128, 1024), pl.BlockSpec((128, 128), prev), _rows(nb * 128, 128),
                  pl.BlockSpec((128, 128), prev), _rows(nb * 128, 128)],
        out_specs=_rows(nb * 128, 1024),
        out_shape=jax.ShapeDtypeStruct((S, 1024), BF),
        compiler_params=_params(("parallel",)),
    )(sinks, aq, ak, ak, av, av)


def _group_norm(y):
    mu = jnp.mean(y, axis=-1, keepdims=True)
    yc = y - mu
    rstd = lax.rsqrt(jnp.mean(yc * yc, axis=-1, keepdims=True) + EPS)
    return yc * rstd, rstd


def _merge_fwd(x, ry, rg, ay, ga, gb, g_ret, w_ret, w_attn, w_out):
    S = x.shape[0]
    tm = 512

    def body(x_ref, ry_ref, rg_ref, ay_ref, ga_ref, gb_ref, gr_ref, wr_ref, wa_ref, wo_ref,
             u_ref, mg_ref, x1_ref):
        bb = _dot(ay_ref[...], wa_ref[...])
        for hd in range(RET_HEADS):
            vs = slice(hd * 256, (hd + 1) * 256)
            yn, _ = _group_norm(ry_ref[:, vs].astype(F32))
            rgv = rg_ref[:, vs].astype(F32)
            u_ref[:, vs] = (rgv * _sigmoid(rgv) * (yn * gr_ref[:, vs])).astype(BF)
        gated_b = _sigmoid(gb_ref[...].astype(F32)) * bb
        sa = _sigmoid(ga_ref[...].astype(F32))
        ba = _dot(u_ref[...], wr_ref[...])
        merged = (sa * ba + gated_b).astype(BF)
        mg_ref[...] = merged
        x1_ref[...] = x_ref[...] + _dot(merged, wo_ref[...])

    outs = [BF, BF, F32]
    return pl.pallas_call(
        body, name="merge_fwd", grid=(S // tm,),
        in_specs=[_rows(tm, 1024)] * 6 + [_fixed((1, 1024)), _VMEM, _VMEM, _VMEM],
        out_specs=[_rows(tm, 1024)] * 3,
        out_shape=[jax.ShapeDtypeStruct((S, 1024), dt) for dt in outs],
        compiler_params=_params(("parallel",)),
    )(x, ry, rg, ay, ga, gb, g_ret, w_ret, w_attn, w_out)


def _ffn_fwd(x1, tgt, g2, g3, w_gate, w_up, w_down):
    S = x1.shape[0]
    tm = 256

    def body(x1_ref, t_ref, g2_ref, g3_ref, wg_ref, wu_ref, wd_ref,
             gate_ref, up_ref, h2_ref, act_ref, dx2_ref, loss_ref, dg3_ref):
        @pl.when(pl.program_id(0) == 0)
        def _():
            loss_ref[...] = jnp.zeros_like(loss_ref)
            dg3_ref[...] = jnp.zeros_like(dg3_ref)

        x1v = x1_ref[...]
        r2 = lax.rsqrt(jnp.mean(x1v * x1v, axis=-1, keepdims=True) + EPS)
        h2 = (x1v * r2 * g2_ref[...]).astype(BF)
        h2_ref[...] = h2
        gate = _dot_nt(h2, wg_ref[...])
        up = _dot_nt(h2, wu_ref[...])
        gate_ref[...] = gate
        up_ref[...] = up
        act = (gate * _sigmoid(gate) * up).astype(BF)
        act_ref[...] = act
        x2 = x1v + _dot(act, wd_ref[...])
        r3 = lax.rsqrt(jnp.mean(x2 * x2, axis=-1, keepdims=True) + EPS)
        xh = x2 * r3
        g3v = g3_ref[...]
        err = xh * g3v - t_ref[...]
        loss_ref[...] += 0.5 * jnp.sum(jnp.mean(err * err, axis=-1, keepdims=True), axis=0, keepdims=True)
        dy = err * (1.0 / D_MODEL)
        dg3_ref[...] += jnp.sum(dy * xh, axis=0, keepdims=True)
        dxh = dy * g3v
        dx2_ref[...] = r3 * (dxh - xh * jnp.mean(dxh * xh, axis=-1, keepdims=True))

    return pl.pallas_call(
        body, name="ffn_fwd", grid=(S // tm,),
        in_specs=[_rows(tm, 1024), _rows(tm, 1024), _fixed((1, 1024)), _fixed((1, 1024)), _VMEM, _VMEM, _VMEM],
        out_specs=[_rows(tm, D_FF), _rows(tm, D_FF), _rows(tm, 1024), _rows(tm, D_FF), _rows(tm, 1024),
                   _fixed((1, LANES)), _fixed((1, 1024))],
        out_shape=[jax.ShapeDtypeStruct((S, D_FF), F32), jax.ShapeDtypeStruct((S, D_FF), F32),
                   jax.ShapeDtypeStruct((S, 1024), BF), jax.ShapeDtypeStruct((S, D_FF), BF),
                   jax.ShapeDtypeStruct((S, 1024), F32),
                   jax.ShapeDtypeStruct((1, LANES), F32), jax.ShapeDtypeStruct((1, 1024), F32)],
        compiler_params=_params(("arbitrary",)),
    )(x1, tgt, g2, g3, w_gate, w_up, w_down)


def _ffn_bwd(dx2, gate, up, x1, g2, w_gate, w_up, w_down):
    S = x1.shape[0]
    tm = 256

    def body(dx2_ref, gate_ref, up_ref, x1_ref, g2_ref, wg_ref, wu_ref, wd_ref,
             dgate_ref, dup_ref, dx1_ref, dg2_ref):
        @pl.when(pl.program_id(0) == 0)
        def _():
            dg2_ref[...] = jnp.zeros_like(dg2_ref)

        dx2v = dx2_ref[...]
        dact = _dot_nt(dx2v.astype(BF), wd_ref[...])
        gate = gate_ref[...]
        sg = _sigmoid(gate)
        dgate = (dact * up_ref[...] * (sg * (1.0 + gate * (1.0 - sg)))).astype(BF)
        dup = (dact * (gate * sg)).astype(BF)
        dgate_ref[...] = dgate
        dup_ref[...] = dup
        dh2 = _dot(dgate, wg_ref[...]) + _dot(dup, wu_ref[...])
        x1v = x1_ref[...]
        r2 = lax.rsqrt(jnp.mean(x1v * x1v, axis=-1, keepdims=True) + EPS)
        xh = x1v * r2
        dg2_ref[...] += jnp.sum(dh2 * xh, axis=0, keepdims=True)
        dxh = dh2 * g2_ref[...]
        dx1_ref[...] = dx2v + r2 * (dxh - xh * jnp.mean(dxh * xh, axis=-1, keepdims=True))

    return pl.pallas_call(
        body, name="ffn_bwd", grid=(S // tm,),
        in_specs=[_rows(tm, 1024), _rows(tm, D_FF), _rows(tm, D_FF), _rows(tm, 1024), _fixed((1, 1024)),
                  _VMEM, _VMEM, _VMEM],
        out_specs=[_rows(tm, D_FF), _rows(tm, D_FF), _rows(tm, 1024), _fixed((1, 1024))],
        out_shape=[jax.ShapeDtypeStruct((S, D_FF), BF), jax.ShapeDtypeStruct((S, D_FF), BF),
                   jax.ShapeDtypeStruct((S, 1024), F32), jax.ShapeDtypeStruct((1, 1024), F32)],
        compiler_params=_params(("arbitrary",)),
    )(dx2, gate, up, x1, g2, w_gate, w_up, w_down)


def _merge_bwd(dx1, ga, gb, u, ay, rg, ry, g_ret, w_ret, w_attn, w_out):
    S = dx1.shape[0]
    tm = 512

    def body(dx1_ref, ga_ref, gb_ref, u_ref, ay_ref, rg_ref, ry_ref, gr_ref, wr_ref, wa_ref, wo_ref,
             dgates_ref, drg_ref, dba_ref, dbb_ref, day_ref, dry_ref, dgr_ref):
        @pl.when(pl.program_id(0) == 0)
        def _():
            dgr_ref[...] = jnp.zeros_like(dgr_ref)

        dm = _dot_nt(dx1_ref[...].astype(BF), wo_ref[...])
        ba = _dot(u_ref[...], wr_ref[...])
        bb = _dot(ay_ref[...], wa_ref[...])
        sa = _sigmoid(ga_ref[...].astype(F32))
        sb = _sigmoid(gb_ref[...].astype(F32))
        dga = (dm * ba * (sa * (1.0 - sa))).astype(BF)
        dgb = (dm * bb * (sb * (1.0 - sb))).astype(BF)
        dgates_ref[:, :1024] = dga
        dgates_ref[:, 1024:] = dgb
        dba = (dm * sa).astype(BF)
        dbb = (dm * sb).astype(BF)
        dba_ref[...] = dba
        dbb_ref[...] = dbb
        day_ref[...] = _dot_nt(dbb, wa_ref[...]).astype(BF)
        du = _dot_nt(dba, wr_ref[...])
        for hd in range(RET_HEADS):
            vs = slice(hd * 256, (hd + 1) * 256)
            yn, rstd = _group_norm(ry_ref[:, vs].astype(F32))
            rgv = rg_ref[:, vs].astype(F32)
            sr = _sigmoid(rgv)
            silu = rgv * sr
            duv = du[:, vs]
            grv = gr_ref[:, vs]
            drg = (duv * (yn * grv) * (sr * (1.0 + rgv * (1.0 - sr)))).astype(BF)
            drg_ref[:, vs] = drg
            dyg = duv * silu
            dgr_ref[:, vs] += jnp.sum(dyg * yn, axis=0, keepdims=True)
            dyn = dyg * grv
            dry_ref[:, vs] = (rstd * (dyn - jnp.mean(dyn, axis=-1, keepdims=True)
                                      - yn * jnp.mean(dyn * yn, axis=-1, keepdims=True))).astype(KEPT)

    return pl.pallas_call(
        body, name="merge_bwd", grid=(S // tm,),
        in_specs=[_rows(tm, 1024)] * 7 + [_fixed((1, 1024)), _VMEM, _VMEM, _VMEM],
        out_specs=[_rows(tm, 2048), _rows(tm, 1024), _rows(tm, 1024), _rows(tm, 1024), _rows(tm, 1024),
                   _rows(tm, 1024), _fixed((1, 1024))],
        out_shape=[jax.ShapeDtypeStruct((S, 2048), BF), jax.ShapeDtypeStruct((S, 1024), BF),
                   jax.ShapeDtypeStruct((S, 1024), BF), jax.ShapeDtypeStruct((S, 1024), BF),
                   jax.ShapeDtypeStruct((S, 1024), BF), jax.ShapeDtypeStruct((S, 1024), KEPT),
                   jax.ShapeDtypeStruct((1, 1024), F32)],
        compiler_params=_params(("arbitrary",)),
    )(dx1, ga, gb, u, ay, rg, ry, g_ret, w_ret, w_attn, w_out)


def _attn_bwd(aq, ak, av, day, sinks, ca, sa, after):
    S = aq.shape[0]
    N = S // SEQ_CHUNK
    steps = N // 2

    def body(sk_ref, q_ref, kp_ref, kc_ref, vp_ref, vc_ref, do_ref, cq_ref, sq_ref, cp_ref, sp_ref, after_ref,
             dq_ref, dke_ref, dve_ref, dko_ref, dvo_ref, dsk_ref, carry_k, carry_v):
        i = pl.program_id(0)

        @pl.when(i == 0)
        def _():
            carry_k[...] = jnp.zeros_like(carry_k)
            carry_v[...] = jnp.zeros_like(carry_v)
            dsk_ref[...] = jnp.zeros_like(dsk_ref)

        @pl.when(i < steps)
        def _():
            kall, kall_sw = _kv_halves(kp_ref, kc_ref)
            vall, vall_sw = _kv_halves(vp_ref, vc_ref)
            lo = lax.broadcasted_iota(jnp.int32, (256, 128), 1) < 64
            blocks = []
            for sb in range(2):
                kv = slice(sb * 128, sb * 128 + 256)
                rows = slice(sb * 128, (sb + 1) * 128)
                ops = [_group_operands(g, kall[kv], kall_sw[kv], vall[kv], vall_sw[kv]) for g in range(2)]
                stacked = [(_stack_pairs(q_ref, g, rows), _stack_pairs(do_ref, g, rows)) for g in range(2)]
                raw = [_dot_nt(ops[g][0], stacked[g][0]) for g in range(2)]
                raw_dp = [_dot_nt(ops[g][1], stacked[g][1]) for g in range(2)]
                blocks.append((rows, ops, stacked, raw, raw_dp, _attn_fold(2 * i + sb)))
            grads = []
            d_sink = [jnp.zeros((1, PAIRS * 128), F32) for _ in range(4)]
            for rows, ops, stacked, raw, raw_dp, fold in blocks:
                cur = fold[0]
                cq, sq = cq_ref[rows, :], sq_ref[rows, :]
                dk_groups, dv_groups = [], []
                for g in range(2):
                    qs, dos = stacked[g]
                    ds_parts, p_parts = [], []
                    for which in range(2):
                        half = slice(which * 256, (which + 1) * 256)
                        pr, ps = _attn_probs(raw[g][half], fold, _sink_row(sk_ref, g, which))
                        dp = _fold(raw_dp[g][half], cur)
                        dr = jnp.sum(pr * dp, axis=0, keepdims=True)
                        ds_parts.append(_unfold(pr * (dp - dr) * (ATTN_HEAD_DIM ** -0.5), fold))
                        p_parts.append(_unfold(pr, fold))
                        d_sink[2 * g + which] = d_sink[2 * g + which] - ps * dr
                    ds = jnp.concatenate(ds_parts, axis=0)
                    dq = _dot_tn(ds, ops[g][0])
                    dk_both = _dot(ds, qs)
                    dv_both = _dot(jnp.concatenate(p_parts, axis=0), dos)
                    for p in range(PAIRS):
                        sl = slice((4 * g + p) * 128, (4 * g + p + 1) * 128)
                        dq_ref[rows, sl] = _rot_attn_bwd(dq[p * 128:(p + 1) * 128], cq, sq).astype(BF)
                    tk = jnp.where(lo, dk_both[:256], dk_both[256:])
                    tv = jnp.where(lo, dv_both[:256], dv_both[256:])
                    dk_groups.append(tk + pltpu.roll(tk, 64, 1))
                    dv_groups.append(tv + pltpu.roll(tv, 64, 1))
                grads.append((jnp.where(lo, dk_groups[0], dk_groups[1]), jnp.where(lo, dv_groups[0], dv_groups[1])))
            dsk_ref[...] += jnp.concatenate(d_sink, axis=0)
            (dk_a, dv_a), (dk_b, dv_b) = grads
            dko_ref[...] = _rot_attn_bwd(carry_k[...] + dk_a[:128], cp_ref[...], sp_ref[...]).astype(BF)
            dvo_ref[...] = (carry_v[...] + dv_a[:128]).astype(BF)
            dke_ref[...] = _rot_attn_bwd(dk_a[128:] + dk_b[:128], cq_ref[0:128, :], sq_ref[0:128, :]).astype(BF)
            dve_ref[...] = (dv_a[128:] + dv_b[:128]).astype(BF)
            carry_k[...] = dk_b[128:]
            carry_v[...] = dv_b[128:]

        @pl.when(i == steps)
        def _():
            dko_ref[...] = _rot_attn_bwd(carry_k[...], cp_ref[...], sp_ref[...]).astype(BF)
            dvo_ref[...] = carry_v[...].astype(BF)

    two = lambda i: (jnp.minimum(i, steps - 1), 0)
    before = lambda i: (jnp.maximum(2 * i - 1, 0), 0)
    odd = lambda i: (jnp.maximum(i - 1, 0), 0)
    blk = lambda r, w, im: pl.BlockSpec((r, w), im)
    half = jax.ShapeDtypeStruct((S // 2, 128), BF)
    dq, dk_even, dv_even, dk_odd, dv_odd, dsk = pl.pallas_call(
        body, name="attn_bwd", grid=(steps + 1,),
        in_specs=[_SMEM, blk(256, 1024, two), blk(128, 128, before), blk(256, 128, two), blk(128, 128, before),
                  blk(256, 128, two), blk(256, 1024, two), blk(256, 128, two), blk(256, 128, two),
                  blk(128, 128, before), blk(128, 128, before), _ANY],
        out_specs=[blk(256, 1024, two), blk(128, 128, two), blk(128, 128, two), blk(128, 128, odd),
                   blk(128, 128, odd), _fixed((4, PAIRS * 128))],
        out_shape=[jax.ShapeDtypeStruct((S, 1024), BF), half, half, half, half,
                   jax.ShapeDtypeStruct((4, PAIRS * 128), F32)],
        scratch_shapes=[pltpu.VMEM((128, 128), F32), pltpu.VMEM((128, 128), F32)],
        compiler_params=_params(("arbitrary",)),
    )(sinks, aq, ak, ak, av, av, day, ca, sa, ca, sa, after)

    def interleave(even, odd_):
        return jnp.stack([even.reshape(steps, 128, 128), odd_.reshape(steps, 128, 128)], axis=1).reshape(S, 128)

    return dq, interleave(dk_even, dk_odd), interleave(dv_even, dv_odd), dsk


def _ret_bwd(rq, rk, rv, dry, states, dmat, qd, kd, cd, cr, sr, after):
    S = rq.shape[0]
    N = S // RET_CHUNK
    scale = RET_QK_DIM ** -0.5
    cps = RET_CHUNKS_PER_STEP
    steps = N // cps

    def body(q_ref, k_ref, v_ref, dy_ref, st_ref, dm_ref, qd_ref, kd_ref, cd_ref, cr_ref, sr_ref, after_ref,
             out_ref, dstate):
        @pl.when(pl.program_id(0) == 0)
        def _():
            dstate[...] = jnp.zeros_like(dstate)

        for hd in range(RET_HEADS):
            qs = slice(hd * 128, (hd + 1) * 128)
            vs = slice(hd * 256, (hd + 1) * 256)
            dmv = dm_ref[hd]
            free = []
            for ch in range(cps):
                rs = slice(ch * RET_CHUNK, (ch + 1) * RET_CHUNK)
                crv, srv = cr_ref[rs, :], sr_ref[rs, :]
                q = q_ref[rs, qs]
                kf = k_ref[rs, qs]
                k = kf.astype(BF)
                v = v_ref[rs, vs]
                do = dy_ref[rs, vs].astype(F32)
                dob = do.astype(BF)
                doq = (do * qd_ref[hd]).astype(BF)
                a = (_dot_nt(q, k) * dmv).astype(BF)
                da = (_dot_nt(dob, v) * dmv).astype(BF)
                dq = _dot(da, k) + _dot_nt(doq, st_ref[ch, hd])
                out_ref[rs, qs] = _rot_ret_bwd(dq * scale, crv, srv).astype(BF)
                free.append((rs, crv, srv, v, (kf * kd_ref[hd]).astype(BF),
                             _dot_tn(a, dob), _dot_tn(da, q), _dot_tn(q, doq)))
            ds = dstate[hd]
            for rs, crv, srv, v, kdv, dv_free, dk_free, ds_update in reversed(free):
                dsb = ds.astype(BF)
                dv = dv_free + _dot(kdv, dsb)
                dk = dk_free + _dot_nt(v, dsb) * kd_ref[hd]
                ds = ds * cd_ref[hd] + ds_update
                out_ref[rs, 512 + hd * 128:512 + (hd + 1) * 128] = _rot_ret_bwd(dk, crv, srv).astype(BF)
                out_ref[rs, 1024 + hd * 256:1024 + (hd + 1) * 256] = dv.astype(BF)
            dstate[hd] = ds

    rev = lambda w: pl.BlockSpec((cps * RET_CHUNK, w), lambda i: (steps - 1 - i, 0))
    return pl.pallas_call(
        body, name="ret_bwd", grid=(steps,),
        in_specs=[rev(512), rev(512), rev(1024), rev(1024),
                  pl.BlockSpec((cps, RET_HEADS, 128, 256), lambda i: (steps - 1 - i, 0, 0, 0)),
                  _VMEM, _VMEM, _VMEM, _VMEM, rev(128), rev(128), _ANY],
        out_specs=rev(2048),
        out_shape=jax.ShapeDtypeStruct((S, 2048), BF),
        scratch_shapes=[pltpu.VMEM((RET_HEADS, 128, 256), F32)],
        compiler_params=_params(("arbitrary",)),
    )(rq, rk, rv, dry, states, dmat, qd, kd, cd, cr, sr, after)


def _proj_bwd(d_pieces, x, dx1, g1, w_in_t):
    S = x.shape[0]
    tm = 512
    widths = [d.shape[1] for d in d_pieces]
    assert sum(widths) == D_IN
    nd = len(d_pieces)

    def body(*refs):
        d_refs = refs[:nd]
        x_ref, dx1_ref, g_ref, w_ref, gx_ref, dg_ref, db_ref = refs[nd:]

        @pl.when(pl.program_id(0) == 0)
        def _():
            dg_ref[...] = jnp.zeros_like(dg_ref)
            db_ref[...] = jnp.zeros_like(db_ref)

        d_all = jnp.concatenate([ref[...] for ref in d_refs], axis=1)
        dh = _dot(d_all, w_ref[...])
        db_ref[...] += jnp.sum(d_all.astype(F32), axis=0, keepdims=True)
        xv = x_ref[...]
        r = lax.rsqrt(jnp.mean(xv * xv, axis=-1, keepdims=True) + EPS)
        xh = xv * r
        dg_ref[...] += jnp.sum(dh * xh, axis=0, keepdims=True)
        dxh = dh * g_ref[...]
        gx_ref[...] = dx1_ref[...] + r * (dxh - xh * jnp.mean(dxh * xh, axis=-1, keepdims=True))

    return pl.pallas_call(
        body, name="proj_bwd", grid=(S // tm,),
        in_specs=[_rows(tm, n) for n in widths] + [_rows(tm, 1024), _rows(tm, 1024), _fixed((1, 1024)), _VMEM],
        out_specs=[_rows(tm, 1024), _fixed((1, 1024)), _fixed((1, D_IN))],
        out_shape=[jax.ShapeDtypeStruct((S, 1024), F32), jax.ShapeDtypeStruct((1, 1024), F32),
                   jax.ShapeDtypeStruct((1, D_IN), F32)],
        compiler_params=_params(("arbitrary",)),
    )(*d_pieces, x, dx1, g1, w_in_t)


SHARD_ROWS = (1600, 256, 256, 256, 704, 704, 704)


class Group(NamedTuple):
    name: str
    tensors: tuple
    offs: tuple
    block: int
    tile: int

    @property
    def half(self):
        return self.block // 2

    def off(self, t):
        return self.offs[self.tensors.index(t)]


LATE = Group("late", (0,), (0,), 1600, 800)
EARLY = Group("early", (4, 5, 6, 1, 2, 3), (0, 704, 1408, 2112, 2368, 2624), 2880, 480)
GROUP_OF = {t: g for g in (LATE, EARLY) for t in g.tensors}


def _wgrad(a_list, b, t, packed, name):
    S = b.shape[0]
    widths = [a.shape[1] for a in a_list]
    M = sum(widths)
    ts = 1024 if M <= 4096 else 512
    steps = S // ts
    na = len(a_list)
    grp = GROUP_OF[t]
    r, off = SHARD_ROWS[t], grp.off(t)
    assert M == N_CHIPS * r
    segments = [(chip * r, r, chip * grp.block + off) for chip in range(N_CHIPS)]
    stage_rows = r
    n_in = na + 1 + (2 if packed is not None else 0)

    def body(*refs):
        a_refs, b_ref = refs[:na], refs[na]
        g32_ref, gb_ref, acc, stage, sems = refs[n_in:]
        i = pl.program_id(0)

        @pl.when(i == 0)
        def _():
            acc[...] = jnp.zeros_like(acc)

        bv = b_ref[...].astype(BF)
        lo = 0
        for a_ref, n in zip(a_refs, widths):
            acc[lo:lo + n, :] += _dot_tn(a_ref[...].astype(BF), bv)
            lo += n

        @pl.when(i == steps - 1)
        def _():
            for a0, n, p0 in segments:
                c32 = pltpu.make_async_copy(acc.at[a0:a0 + n, :], g32_ref.at[pl.ds(p0, n), :], sems.at[0])
                c32.start()
                stage[0:n, :] = acc[a0:a0 + n, :].astype(BF)
                c16 = pltpu.make_async_copy(stage.at[0:n, :], gb_ref.at[pl.ds(p0, n), :], sems.at[1])
                c16.start()
                c32.wait()
                c16.wait()

    shapes = [jax.ShapeDtypeStruct((N_CHIPS * grp.block, 1024), F32),
              jax.ShapeDtypeStruct((N_CHIPS * grp.block, 1024), BF)]
    operands = list(a_list) + [b]
    in_specs = [_rows(ts, n) for n in widths] + [_rows(ts, 1024)]
    aliases = {}
    if packed is not None:
        operands += list(packed)
        in_specs += [_ANY, _ANY]
        aliases = {na + 1: 0, na + 2: 1}
    return pl.pallas_call(
        body, name=name, grid=(steps,),
        in_specs=in_specs, out_specs=[_ANY, _ANY], out_shape=shapes,
        scratch_shapes=[pltpu.VMEM((M, 1024), F32), pltpu.VMEM((stage_rows, 1024), BF),
                        pltpu.SemaphoreType.DMA((2,))],
        input_output_aliases=aliases,
        compiler_params=_params(("arbitrary",)),
    )(*operands)


def _stage_weights(place, ws, name, after=()):
    nw = len(ws)

    def body(p_ref, *refs):
        for w_ref, o_ref in zip(refs[:nw], refs[nw + len(after):]):
            o_ref[...] = w_ref[...].astype(BF)

    return pl.pallas_call(
        body, name=name,
        grid_spec=pltpu.PrefetchScalarGridSpec(
            num_scalar_prefetch=1, grid=(1,),
            in_specs=[pl.BlockSpec(w.shape, lambda i, p: (0, 0)) for w in ws] + [_ANY] * len(after),
            out_specs=[pl.BlockSpec(w.shape, lambda i, p: (p[0], 0)) for w in ws]),
        out_shape=[jax.ShapeDtypeStruct((N_CHIPS * w.shape[0], 1024), BF) for w in ws],
        compiler_params=_params(("arbitrary",)),
    )(place, *ws, *after)


def _kept_block(grp, nb):
    return lambda i, p: ((2 * (i // nb) + p[1]) * nb + i % nb, 0)


def _pair_sum(place, g32, got, grp):
    nb = grp.half // grp.tile

    def body(p_ref, g_ref, r_ref, s16_ref):
        s16_ref[...] = (g_ref[...] + r_ref[...].astype(F32)).astype(BF)

    flat = pl.BlockSpec((grp.tile, 1024), lambda i, p: (i, 0))
    return pl.pallas_call(
        body, name="pair_sum_" + grp.name,
        grid_spec=pltpu.PrefetchScalarGridSpec(
            num_scalar_prefetch=1, grid=(N_CHIPS * nb,),
            in_specs=[pl.BlockSpec((grp.tile, 1024), _kept_block(grp, nb)), flat], out_specs=flat),
        out_shape=jax.ShapeDtypeStruct((N_CHIPS * grp.half, 1024), BF),
        compiler_params=_params(("arbitrary",)),
    )(place, g32, got)


def _chip_sum(place, g32, got, others, grp):
    nb = grp.half // grp.tile
    n_others = others.shape[0] // grp.half
    n_own = 1 if got is None else 2

    def body(p_ref, *refs):
        acc = refs[0][...]
        for r_ref in refs[1:n_own + n_others]:
            acc = acc + r_ref[...].astype(F32)
        refs[-1][...] = acc

    blk = lambda f: pl.BlockSpec((grp.tile, 1024), f)
    slot = lambda k: blk(lambda i, p: (k * nb + i, 0))
    own_specs = [blk(lambda i, p: ((2 * p[0] + p[1]) * nb + i, 0))]
    own_args = [g32]
    if got is not None:
        own_specs.append(blk(lambda i, p: (p[0] * nb + i, 0)))
        own_args.append(got)
    return pl.pallas_call(
        body, name="chip_sum_" + grp.name,
        grid_spec=pltpu.PrefetchScalarGridSpec(
            num_scalar_prefetch=1, grid=(nb,),
            in_specs=own_specs + [slot(k) for k in range(n_others)],
            out_specs=blk(lambda i, p: (p[1] * nb + i, 0))),
        out_shape=jax.ShapeDtypeStruct((grp.block, 1024), F32),
        compiler_params=_params(("arbitrary",)),
    )(place, *own_args, *[others] * n_others)


def _adamw(w, g, m, v, name, g_row0=0, after=None):
    r = w.shape[0]
    tr = next((c for c in (352, 320, 64) if r % c == 0 and g_row0 % c == 0), r)
    assert g_row0 % tr == 0
    g0 = g_row0 // tr
    extra = [] if after is None else [after]

    def body(w_ref, g_ref, m_ref, v_ref, *rest):
        go_ref, d_ref, mo_ref, vo_ref = rest[len(extra):]
        gv = g_ref[...]
        mn = ADAM_B1 * m_ref[...] + (1.0 - ADAM_B1) * gv
        vn = ADAM_B2 * v_ref[...] + (1.0 - ADAM_B2) * (gv * gv)
        m_hat = mn / (1.0 - ADAM_B1 ** ADAM_STEP)
        v_hat = vn / (1.0 - ADAM_B2 ** ADAM_STEP)
        go_ref[...] = gv
        d_ref[...] = -ADAM_LR * (m_hat / (jnp.sqrt(v_hat) + ADAM_EPS) + ADAM_WD * w_ref[...])
        mo_ref[...] = mn
        vo_ref[...] = vn

    return pl.pallas_call(
        body, name=name, grid=(r // tr,),
        in_specs=[_rows(tr, 1024), pl.BlockSpec((tr, 1024), lambda i: (g0 + i, 0)), _rows(tr, 1024), _rows(tr, 1024)]
        + [_ANY] * len(extra),
        out_specs=[_rows(tr, 1024)] * 4,
        out_shape=[jax.ShapeDtypeStruct((r, 1024), F32)] * 4,
        compiler_params=_params(("parallel",)),
    )(w, g, m, v, *extra)


def _place():
    x, y, c = lax.axis_index("x"), lax.axis_index("y"), lax.axis_index("c")
    chips = [(1 - x, y), (x, 1 - y), (1 - x, 1 - y)]
    return x, y, c, chips


_HBM =pl.BlockSpec(memory_space=pltpu.HBM)
_SEM = pl.BlockSpec(memory_space=pltpu.SEMAPHORE)
_DATAFLOW = pltpu.SideEffectType.DATAFLOW_SIDE_EFFECTING


def _exchange(name, arrays, n, plan):
    na = len(arrays)

    def body(*refs):
        pairs = plan(refs[na:2 * na], refs[2 * na], refs[2 * na + 1])
        for send, _ in pairs:
            send.start()
        for send, recv in pairs:
            send.wait_send()
            recv.wait_recv()

    return pl.pallas_call(
        body, name=name, in_specs=[_ANY] * na, out_specs=[_ANY] * na,
        out_shape=[jax.ShapeDtypeStruct(a.shape, a.dtype) for a in arrays],
        scratch_shapes=[pltpu.SemaphoreType.DMA((n,)), pltpu.SemaphoreType.DMA((n,))],
        input_output_aliases={i: i for i in range(na)},
    )(*arrays)


def _exchange_start(name, arrays, n, plan, after=()):
    na = len(arrays)

    def body(*refs):
        for send, _ in plan(refs[:na], refs[na + len(after)], refs[na + len(after) + 1]):
            send.start()
        refs[-1][...] = jnp.zeros_like(refs[-1])

    out = pl.pallas_call(
        body, name=name,
        out_shape=(pltpu.SemaphoreType.DMA((n,)), pltpu.SemaphoreType.DMA((n,)),
                   *[pltpu.HBM(a.shape, a.dtype) for a in arrays], jax.ShapeDtypeStruct((8, LANES), F32)),
        in_specs=[_HBM] * na + [_ANY] * len(after), out_specs=(_SEM, _SEM, *[_HBM] * na, _VMEM),
        input_output_aliases={i: 2 + i for i in range(na)},
        compiler_params=pltpu.CompilerParams(has_side_effects=_DATAFLOW),
    )(*[pltpu.with_memory_space_constraint(a, pltpu.HBM) for a in arrays], *after)
    return (out[0], out[1]), list(out[2:2 + na]), out[-1]


def _exchange_wait(name, sems, in_flight, plan, after=()):
    na = len(in_flight)

    def body(*refs):
        for send, recv in plan(refs[:na], refs[na], refs[na + 1]):
            send.wait_send()
            recv.wait_recv()

    out = pl.pallas_call(
        body, name=name,
        out_shape=tuple(pltpu.HBM(a.shape, a.dtype) for a in in_flight),
        in_specs=[_HBM] * na + [_SEM, _SEM] + [_ANY] * len(after), out_specs=tuple([_HBM] * na),
        input_output_aliases={i: i for i in range(na)},
        compiler_params=pltpu.CompilerParams(has_side_effects=_DATAFLOW),
    )(*in_flight, *sems, *after)
    return list(out)


def _remote(src, dst, send_sems, recv_sems, k, to):
    return pltpu.make_async_remote_copy(src_ref=src, dst_ref=dst, send_sem=send_sems.at[k], recv_sem=recv_sems.at[k],
                                        device_id=to, device_id_type=MESH)


def _half_rows(ref, tid, pj, half):
    h = SHARD_ROWS[tid] // 2
    return ref.at[pl.ds((2 * pj + half) * h, h), :]


def _plan_gather_halves(tids):
    def plan(refs, ss, rs):
        x, y, c, chips = _place()
        pairs = []
        for t, tid in enumerate(tids):
            mine = _half_rows(refs[t], tid, 2 * x + y, c)
            for k, (cx, cy) in enumerate(chips):
                theirs = _half_rows(refs[t], tid, 2 * cx + cy, c)
                pairs.append((_remote(mine, mine, ss, rs, 3 * t + k, (cx, cy, c)),
                              _remote(theirs, theirs, ss, rs, 3 * t + k, (x, y, c))))
        return pairs
    return plan


def _plan_forward_halves(tids):
    def plan(refs, ss, rs):
        x, y, c, chips = _place()
        pairs = []
        for t, tid in enumerate(tids):
            for k, (cx, cy) in enumerate(chips):
                got = _half_rows(refs[t], tid, 2 * cx + cy, c)
                missing = _half_rows(refs[t], tid, 2 * cx + cy, 1 - c)
                pairs.append((_remote(got, got, ss, rs, 3 * t + k, (x, y, 1 - c)),
                              _remote(missing, missing, ss, rs, 3 * t + k, (x, y, 1 - c))))
        return pairs
    return plan


def _plan_pair_exchange(grp):
    def plan(refs, ss, rs):
        x, y, c, _ = _place()
        cps = [_remote(refs[0].at[pl.ds(pj * grp.block + (1 - c) * grp.half, grp.half), :],
                       refs[1].at[pl.ds(pj * grp.half, grp.half), :], ss, rs, pj, (x, y, 1 - c))
               for pj in range(N_CHIPS)]
        return [(cp, cp) for cp in cps]
    return plan


def _plan_to_chips(grp):
    def plan(refs, ss, rs):
        x, y, c, chips = _place()
        cps = [_remote(refs[0].at[pl.ds((2 * cx + cy) * grp.half, grp.half), :],
                       refs[1].at[pl.ds(k * grp.half, grp.half), :], ss, rs, k, (cx, cy, c))
               for k, (cx, cy) in enumerate(chips)]
        return [(cp, cp) for cp in cps]
    return plan


def _plan_to_all(grp):
    def plan(refs, ss, rs):
        x, y, c, chips = _place()
        peers = [(x, y, 1 - c)] + [(cx, cy, pc) for cx, cy in chips for pc in (c, 1 - c)]
        cps = [_remote(refs[0].at[pl.ds((2 * px + py) * grp.block + pc * grp.half, grp.half), :],
                       refs[1].at[pl.ds(k * grp.half, grp.half), :], ss, rs, k, (px, py, pc))
               for k, (px, py, pc) in enumerate(peers)]
        return [(cp, cp) for cp in cps]
    return plan


def _plan_share_halves(grp):
    def plan(refs, ss, rs):
        x, y, c, _ = _place()
        mine = refs[0].at[pl.ds(c * grp.half, grp.half), :]
        other = refs[0].at[pl.ds((1 - c) * grp.half, grp.half), :]
        return [(_remote(mine, mine, ss, rs, 0, (x, y, 1 - c)), _remote(other, other, ss, rs, 0, (x, y, 1 - c)))]
    return plan


def _empty(rows, dtype):
    return lax.empty((rows, 1024), dtype)


def _all_reduce_small(v, after):
    m_per, n = v.shape
    n_dev = 8

    def body(x_ref, after_ref, out_ref, all_ref, send_sems, recv_sems, local_sem):
        x, y, c, chips = _place()
        me, sibling = (x, y, c), (x, y, 1 - c)

        def rows(px, py, pc):
            return all_ref.at[pl.ds((4 * px + 2 * py + pc) * m_per, m_per), :]

        def copy(k, block, to, src=None):
            return pltpu.make_async_remote_copy(
                src_ref=rows(*block) if src is None else src, dst_ref=rows(*block),
                send_sem=send_sems.at[k], recv_sem=recv_sems.at[k], device_id=to, device_id_type=MESH)

        mine = pltpu.make_async_copy(x_ref, rows(*me), local_sem)
        mine.start()
        first = [copy(0, me, sibling, src=x_ref)]
        first += [copy(1 + k, me, (*chip, c), src=x_ref) for k, chip in enumerate(chips)]
        for cp in first:
            cp.start()
        passed = [copy(4 + k, (*chip, c), sibling) for k, chip in enumerate(chips)]
        for k, chip in enumerate(chips):
            copy(1 + k, (*chip, c), me).wait_recv()
            passed[k].start()
        copy(0, sibling, me).wait_recv()
        for k, chip in enumerate(chips):
            copy(4 + k, (*chip, 1 - c), me).wait_recv()
        for cp in first + passed:
            cp.wait_send()
        mine.wait()
        acc = all_ref[pl.ds(0, m_per), :]
        for d in range(1, n_dev):
            acc = acc + all_ref[pl.ds(d * m_per, m_per), :]
        out_ref[...] = acc

    return pl.pallas_call(
        body, name="all_reduce_small",
        in_specs=[_VMEM, _ANY], out_specs=_VMEM,
        out_shape=jax.ShapeDtypeStruct((m_per, n), F32),
        scratch_shapes=[pltpu.VMEM((n_dev * m_per, n), F32),
                        pltpu.SemaphoreType.DMA((7,)), pltpu.SemaphoreType.DMA((7,)), pltpu.SemaphoreType.DMA],
        compiler_params=pltpu.CompilerParams(has_side_effects=True),
    )(v, after)


def _tables(S, token):
    pos = jnp.arange(S, dtype=jnp.int32).astype(F32) + token[0, 0]

    def cs(half):
        inv_freq = ROPE_THETA ** (-jnp.arange(half, dtype=F32) / half)
        ang = pos[:, None] * inv_freq[None, :]
        return jnp.cos(ang), jnp.sin(ang)

    c64, s64 = cs(64)
    c32, s32 = cs(32)
    cr = jnp.concatenate([c64, c64], axis=1)
    sr = jnp.concatenate([-s64, s64], axis=1)
    ca = jnp.concatenate([c32, c32, c32, c32], axis=1)
    sa = jnp.concatenate([-s32, s32, -s32, s32], axis=1)
    return cr, sr, ca, sa


def _decays():
    H, C = RET_HEADS, RET_CHUNK
    log_gamma = jnp.log1p(-jnp.exp2(-5.0 - jnp.arange(H, dtype=F32)))
    idx = jnp.arange(C, dtype=F32)
    rel = idx[:, None] - idx[None, :]
    dmat = jnp.where(rel[None] >= 0, jnp.exp(log_gamma[:, None, None] * jnp.maximum(rel, 0.0)[None]), 0.0)
    qd = jnp.exp(log_gamma[:, None] * (idx + 1.0))[:, :, None]
    kd = jnp.exp(log_gamma[:, None] * (C - 1.0 - idx))[:, :, None]
    cd = jnp.exp(log_gamma * C)[:, None, None]
    return dmat, qd, kd, cd


def _local_step(place, x, tgt, ln1_g, b_in, ret_norm_g, attn_sinks, ln2_g, lnf_g, w_in, other_weights, tables, decays):
    cr, sr, ca, sa = tables
    dmat, qd, kd, cd = decays
    forward_weights, landed_weights = other_weights

    h, rq, rk, rv, rg, aq, ak, av, ga, gb = _proj_fwd(x, ln1_g, w_in, b_in, cr, sr, ca, sa)
    ry, states = _ret_fwd(rq, rk, rv, dmat, qd, kd, cd)
    token = forward_weights(ry)
    ay = _attn_fwd(aq, ak, av, attn_sinks + token[0, 0])
    w_ret, w_attn, w_out, w_gate, w_up, w_down = landed_weights(ay)
    u, merged, x1 = _merge_fwd(x, ry, rg, ay, ga, gb, ret_norm_g, w_ret, w_attn, w_out)
    gate, up, h2, act, dx2, loss, dg3 = _ffn_fwd(x1, tgt, ln2_g, lnf_g, w_gate, w_up, w_down)

    dgate, dup, dx1, dg2 = _ffn_bwd(dx2, gate, up, x1, ln2_g, w_gate, w_up, w_down)
    early = _wgrad([act], dx2, 6, None, "wgrad_down")
    early = _wgrad([dgate], h2, 4, early, "wgrad_gate")
    early = _wgrad([dup], h2, 5, early, "wgrad_up")
    dgates, drg, dba, dbb, day, dry, dgr = _merge_bwd(dx1, ga, gb, u, ay, rg, ry, ret_norm_g,
                                                                w_ret, w_attn, w_out)
    early = _wgrad([merged], dx1, 3, early, "wgrad_out")
    early = _wgrad([u], dba, 1, early, "wgrad_ret")
    e32, e16 = _wgrad([ay], dbb, 2, early, "wgrad_attn")

    sems, flight, token = _exchange_start("to_all_start_early", [e16, _empty(7 * EARLY.half, BF)], 7,
                                          _plan_to_all(EARLY))
    daq, dak, dav, dsk = _attn_bwd(aq, ak, av, day, attn_sinks, ca, sa, token)
    d_ret = _ret_bwd(rq, rk, rv, dry, states, dmat, qd, kd, cd, cr, sr, token)
    d_proj = [d_ret, drg, daq, dak, dav, dgates]
    l32, l16 = _wgrad(d_proj, h, 0, None, "wgrad_in")
    _, others = _exchange_wait("to_all_wait_early", sems, flight, _plan_to_all(EARLY), after=(l32,))
    early_shard = _exchange("share_early", [_chip_sum(place, e32, None, others, EARLY)], 1,
                            _plan_share_halves(EARLY))[0]
    sems, flight, token = _exchange_start("to_all_start_late", [l16, _empty(7 * LATE.half, BF)], 7,
                                          _plan_to_all(LATE), after=(early_shard,))
    grad_x, dg1, db = _proj_bwd(d_proj, x, dx1, ln1_g + token[0, 0], w_in)

    def finish_late(after):
        _, others = _exchange_wait("to_all_wait_late", sems, flight, _plan_to_all(LATE), after=after)
        return _chip_sum(place, l32, None, others, LATE)

    d_sinks = dsk.reshape(2, 2, PAIRS, 128).sum(-1).transpose(0, 2, 1).reshape(1, ATTN_Q_HEADS)
    small = dict(loss=loss, ln1_g=dg1, b_in=db, ret_norm_g=dgr, attn_sinks=d_sinks,
                 ln2_g=dg2, lnf_g=dg3)
    return grad_x, early_shard, token, finish_late, small


SMALL_ROWS = 16


def _pack_small(ln1, ret, ln2, lnf, b_in, sinks, loss, name):
    def body(a_ref, b_ref, c_ref, d_ref, bi_ref, sk_ref, ls_ref, o_ref):
        o_ref[...] = jnp.zeros_like(o_ref)
        for row, ref in enumerate((a_ref, b_ref, c_ref, d_ref)):
            o_ref[row:row + 1, :] = ref[...]
        for k in range(6):
            o_ref[4 + k:5 + k, :] = bi_ref[:, k * 1024:(k + 1) * 1024]
        o_ref[10:11, 0:D_IN - 6144] = bi_ref[:, 6144:D_IN]
        o_ref[11:12, 0:ATTN_Q_HEADS] = sk_ref[...]
        o_ref[12:13, 0:LANES] = ls_ref[...]

    return pl.pallas_call(
        body, name=name, in_specs=[_VMEM] * 7, out_specs=_VMEM,
        out_shape=jax.ShapeDtypeStruct((SMALL_ROWS, 1024), F32),
    )(ln1.reshape(1, 1024), ret.reshape(1, 1024), ln2.reshape(1, 1024), lnf.reshape(1, 1024),
      b_in.reshape(1, D_IN), sinks.reshape(1, ATTN_Q_HEADS), loss.reshape(1, LANES))


def _unpack_small(p):
    return dict(ln1_g=p[0:1], ret_norm_g=p[1:2], ln2_g=p[2:3], lnf_g=p[3], b_in=p[4:11].reshape(1, -1)[:, :D_IN],
                attn_sinks=p[11:12, :ATTN_Q_HEADS], loss=p[12, 0])


def kernel(x, ln1_g, w_in, b_in, ret_norm_g, w_ret_out, attn_sinks, w_attn_out, w_out, ln2_g, w_ffn_gate, w_ffn_up, w_ffn_down, lnf_g, loss_target, m_ln1_g, m_w_in, m_b_in, m_ret_norm_g, m_w_ret_out, m_attn_sinks, m_w_attn_out, m_w_out, m_ln2_g, m_w_ffn_gate, m_w_ffn_up, m_w_ffn_down, m_lnf_g, v_ln1_g, v_w_in, v_b_in, v_ret_norm_g, v_w_ret_out, v_attn_sinks, v_w_attn_out, v_w_out, v_ln2_g, v_w_ffn_gate, v_w_ffn_up, v_w_ffn_down, v_lnf_g):
    xi = lax.axis_index("x")
    yi = lax.axis_index("y")
    ci = lax.axis_index("c")
    j = 2 * xi + yi

    place = jnp.stack([j, ci]).astype(jnp.int32)

    names = ["w_in", "w_ret_out", "w_attn_out", "w_out", "w_ffn_gate", "w_ffn_up", "w_ffn_down"]
    transposed = [True, False, False, False, True, True, False]
    view = lambda a, t: a[0].T if t else a[0]
    big_w = [view(a, t) for a, t in zip([w_in, w_ret_out, w_attn_out, w_out, w_ffn_gate, w_ffn_up, w_ffn_down], transposed)]
    big_m = [view(a, t) for a, t in zip([m_w_in, m_w_ret_out, m_w_attn_out, m_w_out, m_w_ffn_gate, m_w_ffn_up, m_w_ffn_down], transposed)]
    big_v = [view(a, t) for a, t in zip([v_w_in, v_w_ret_out, v_w_attn_out, v_w_out, v_w_ffn_gate, v_w_ffn_up, v_w_ffn_down], transposed)]

    staged_in = _stage_weights(place, big_w[:1], "stage_w_in")[0]
    sems, in_flight, token = _exchange_start("gather_in_start", [staged_in], 3, _plan_gather_halves([0]))
    staged = _stage_weights(place, big_w[1:], "stage_others", after=(token,))
    tables = _tables(x.shape[1], token)
    decays = _decays()
    w_in_full = _exchange_wait("gather_in_wait", sems, in_flight, _plan_gather_halves([0]),
                               after=(*staged, *tables, *decays))[0]
    w_in_full = _exchange("gather_in_forward", [w_in_full], 3, _plan_forward_halves([0]))[0]
    rest = list(range(1, 7))
    crossing = _exchange_start("gather_start", staged, 3 * len(rest), _plan_gather_halves(rest), after=(w_in_full,))
    token = crossing[2]
    forwarding = []

    def forward_weights(after):
        arrived = _exchange_wait("gather_wait", crossing[0], crossing[1], _plan_gather_halves(rest), after=(after,))
        forwarding.extend(_exchange_start("forward_start", arrived, 3 * len(rest), _plan_forward_halves(rest)))
        return forwarding[2]

    def landed_weights(after):
        return _exchange_wait("forward_wait", forwarding[0], forwarding[1], _plan_forward_halves(rest), after=(after,))

    other_weights = (forward_weights, landed_weights)

    grad_x, early_shard, token, finish_late, small = _local_step(
        place, x[0], loss_target[0], ln1_g, b_in + token[0, 0], ret_norm_g, attn_sinks, ln2_g, lnf_g.reshape(1, -1),
        w_in_full, other_weights, tables, decays)

    upd = {}

    def update(t, shard, after):
        res = _adamw(big_w[t], shard, big_m[t], big_v[t], "adamw_" + names[t], g_row0=GROUP_OF[t].off(t), after=after)
        upd[names[t]] = tuple((r.T if transposed[t] else r)[None] for r in res)
        return res[0]

    done = [update(t, early_shard, token) for t in EARLY.tensors]
    late_half = finish_late((*done, grad_x))
    pg = _all_reduce_small(_pack_small(
        small["ln1_g"], small["ret_norm_g"], small["ln2_g"], small["lnf_g"], small["b_in"],
        small["attn_sinks"], small["loss"], "pack_small_grads"), late_half)
    late_shard = _exchange("share_late", [late_half], 1, _plan_share_halves(LATE))[0]
    update(0, late_shard, None)

    sm = _unpack_small(pg)
    z = jnp.zeros((1, LANES), F32)
    pw = _pack_small(ln1_g, ret_norm_g, ln2_g, lnf_g, b_in, attn_sinks, z, "pack_small_w")
    pm = _pack_small(m_ln1_g, m_ret_norm_g, m_ln2_g, m_lnf_g, m_b_in, m_attn_sinks, z, "pack_small_m")
    pv = _pack_small(v_ln1_g, v_ret_norm_g, v_ln2_g, v_lnf_g, v_b_in, v_attn_sinks, z, "pack_small_v")
    small_res = _adamw(pw, pg, pm, pv, "adamw_small", after=token)
    sd, smn, svn = (_unpack_small(t) for t in small_res[1:])
    for n in ["ln1_g", "b_in", "ret_norm_g", "attn_sinks", "ln2_g", "lnf_g"]:
        upd[n] = (sm[n], sd[n], smn[n], svn[n])

    order_w = ["ln1_g", "w_in", "b_in", "ret_norm_g", "w_ret_out", "attn_sinks", "w_attn_out", "w_out",
               "ln2_g", "w_ffn_gate", "w_ffn_up", "w_ffn_down", "lnf_g"]
    outs = [sm["loss"], grad_x[None]]
    for k in range(4):
        outs += [upd[n][k] for n in order_w]
    return tuple(outs)
```

```python
from typing import NamedTuple

import jax
import jax.numpy as jnp
from jax import lax
from jax.experimental import pallas as pl
from jax.experimental.pallas import tpu as pltpu

D_MODEL = 1024
SEQ_CHUNK = 128
RET_CHUNK = 256
RET_CHUNKS_PER_STEP = 4
RET_HEADS = 4
RET_QK_DIM = 128
ATTN_Q_HEADS = 16
ATTN_HEAD_DIM = 64
D_FF = 2816
D_IN = 6400
ROPE_THETA = 10000.0
EPS = 1e-6
N_CHIPS = 4

ADAM_LR = 0.001
ADAM_B1 = 0.9
ADAM_B2 = 0.999
ADAM_EPS = 1e-08
ADAM_WD = 0.01
ADAM_STEP = 10

OFF_RQ, OFF_RK, OFF_RV, OFF_RG = 0, 512, 1024, 2048
OFF_AQ, OFF_AK, OFF_AV, OFF_GA, OFF_GB = 3072, 4096, 4224, 4352, 5376

V7X_VMEM_LIMIT = 56 * 1024 * 1024
LANES = 128

BF = jnp.bfloat16
F32 = jnp.float32
KEPT = jnp.bfloat16
MESH = pl.DeviceIdType.MESH

_VMEM = pl.BlockSpec(memory_space=pltpu.VMEM)
_SMEM = pl.BlockSpec(memory_space=pltpu.SMEM)
_ANY = pl.BlockSpec(memory_space=pl.ANY)


def _dot(a, b):
    return jnp.dot(a, b, preferred_element_type=F32)


def _dot_nt(a, b):
    return lax.dot_general(a, b, (((1,), (1,)), ((), ())), preferred_element_type=F32)


def _dot_tn(a, b):
    return lax.dot_general(a, b, (((0,), (0,)), ((), ())), preferred_element_type=F32)


def _sigmoid(x):
    return 0.5 * jnp.tanh(0.5 * x) + 0.5


def _rows(tm, n):
    return pl.BlockSpec((tm, n), lambda i: (i, 0))


def _fixed(shape):
    nd = len(shape)
    return pl.BlockSpec(shape, lambda i: (0,) * nd)


def _params(sem, **kw):
    return pltpu.CompilerParams(dimension_semantics=sem, vmem_limit_bytes=V7X_VMEM_LIMIT, **kw)


def _rot_ret(p, cr, sr):
    return p * cr + pltpu.roll(p, 64, 1) * sr


def _rot_ret_bwd(d, cr, sr):
    return d * cr - pltpu.roll(d, 64, 1) * sr


def _attn_partner(p):
    lane = lax.broadcasted_iota(jnp.int32, p.shape, 1)
    first = (lane % 64) < 32
    return jnp.where(first, pltpu.roll(p, 96, 1), pltpu.roll(p, 32, 1))


def _rot_attn(p, ca, sa):
    return p * ca + _attn_partner(p) * sa


def _rot_attn_bwd(d, ca, sa):
    return d * ca - _attn_partner(d) * sa


def _proj_fwd(x, g1, w_in, b_in, cr, sr, ca, sa):
    S = x.shape[0]
    tm = 512
    scale = RET_QK_DIM ** -0.5

    def body(x_ref, g_ref, w_ref, b_ref, cr_ref, sr_ref, ca_ref, sa_ref,
             h_ref, rq_ref, rk_ref, rv_ref, rg_ref, aq_ref, ak_ref, av_ref, ga_ref, gb_ref):
        xv = x_ref[...]
        r = lax.rsqrt(jnp.mean(xv * xv, axis=-1, keepdims=True) + EPS)
        h = (xv * r * g_ref[...]).astype(BF)
        h_ref[...] = h

        def piece(lo, n):
            return _dot_nt(h, w_ref[lo:lo + n, :]) + b_ref[:, lo:lo + n]

        crv, srv = cr_ref[...], sr_ref[...]
        pq = piece(OFF_RQ, 512)
        pk = piece(OFF_RK, 512)
        for hd in range(RET_HEADS):
            sl = slice(hd * 128, (hd + 1) * 128)
            rq_ref[:, sl] = (_rot_ret(pq[:, sl], crv, srv) * scale).astype(BF)
            rk_ref[:, sl] = _rot_ret(pk[:, sl], crv, srv)
        rv_ref[...] = piece(OFF_RV, 1024).astype(BF)
        rg_ref[...] = piece(OFF_RG, 1024).astype(KEPT)
        cav, sav = ca_ref[...], sa_ref[...]
        pa = piece(OFF_AQ, 1024)
        for blk in range(8):
            sl = slice(blk * 128, (blk + 1) * 128)
            aq_ref[:, sl] = _rot_attn(pa[:, sl], cav, sav).astype(BF)
        pkv = piece(OFF_AK, 256)
        ak_ref[...] = _rot_attn(pkv[:, :128], cav, sav).astype(BF)
        av_ref[...] = pkv[:, 128:].astype(BF)
        ga_ref[...] = piece(OFF_GA, 1024).astype(KEPT)
        gb_ref[...] = piece(OFF_GB, 1024).astype(KEPT)

    outs = [(D_MODEL, BF), (512, BF), (512, F32), (1024, BF), (1024, KEPT),
            (1024, BF), (128, BF), (128, BF), (1024, KEPT), (1024, KEPT)]
    return pl.pallas_call(
        body, name="proj_fwd", grid=(S // tm,),
        in_specs=[_rows(tm, D_MODEL), _fixed((1, D_MODEL)), _VMEM, _VMEM,
                  _rows(tm, 128), _rows(tm, 128), _rows(tm, 128), _rows(tm, 128)],
        out_specs=[_rows(tm, n) for n, _ in outs],
        out_shape=[jax.ShapeDtypeStruct((S, n), dt) for n, dt in outs],
        compiler_params=_params(("parallel",)),
    )(x, g1, w_in, b_in, cr, sr, ca, sa)


def _ret_fwd(rq, rk, rv, dmat, qd, kd, cd):
    S = rq.shape[0]
    N = S // RET_CHUNK
    cps = RET_CHUNKS_PER_STEP
    rows = cps * RET_CHUNK

    def body(q_ref, k_ref, v_ref, dm_ref, qd_ref, kd_ref, cd_ref, y_ref, st_ref, state):
        @pl.when(pl.program_id(0) == 0)
        def _():
            state[...] = jnp.zeros_like(state)

        for hd in range(RET_HEADS):
            qs = slice(hd * 128, (hd + 1) * 128)
            vs = slice(hd * 256, (hd + 1) * 256)
            free = []
            for ch in range(cps):
                rs = slice(ch * RET_CHUNK, (ch + 1) * RET_CHUNK)
                q = q_ref[rs, qs]
                kf = k_ref[rs, qs]
                v = v_ref[rs, vs]
                a = _dot_nt(q, kf.astype(BF)) * dm_ref[hd]
                kdv = (kf * kd_ref[hd]).astype(BF)
                free.append((rs, q, _dot(a.astype(BF), v), _dot_tn(kdv, v)))
            s = state[hd]
            for ch, (rs, q, inner, update) in enumerate(free):
                sb = s.astype(BF)
                st_ref[ch, hd] = sb
                y_ref[rs, vs] = (inner + _dot(q, sb) * qd_ref[hd]).astype(KEPT)
                s = s * cd_ref[hd] + update
            state[hd] = s

    return pl.pallas_call(
        body, name="ret_fwd", grid=(N // cps,),
        in_specs=[_rows(rows, 512), _rows(rows, 512), _rows(rows, 1024), _VMEM, _VMEM, _VMEM, _VMEM],
        out_specs=[_rows(rows, 1024), pl.BlockSpec((cps, RET_HEADS, 128, 256), lambda i: (i, 0, 0, 0))],
        out_shape=[jax.ShapeDtypeStruct((S, 1024), KEPT),
                   jax.ShapeDtypeStruct((N, RET_HEADS, 128, 256), BF)],
        scratch_shapes=[pltpu.VMEM((RET_HEADS, 128, 256), F32)],
        compiler_params=_params(("arbitrary",)),
    )(rq, rk, rv, dmat, qd, kd, cd)


def _kv_halves(prev_ref, cur_ref):
    kk = jnp.concatenate([prev_ref[...], cur_ref[...]], axis=0).astype(F32)
    return kk, pltpu.roll(kk, 64, 1)


ATTN_FWD_BLOCKS_PER_STEP = 4
PAIRS = 4


def _attn_fold(n):
    t = lax.broadcasted_iota(jnp.int32, (128, PAIRS * 128), 0)
    qi = lax.broadcasted_iota(jnp.int32, (128, PAIRS * 128), 1) & 127
    cur = t <= qi
    cur_f = jnp.where(cur, 1.0, 0.0)
    bias = jnp.where(jnp.logical_or(cur, n > 0), 0.0, -1e30)
    return cur, cur_f, 1.0 - cur_f, bias


def _group_operands(g, kk, kk_sw, vv, vv_sw):
    lo = lax.broadcasted_iota(jnp.int32, (256, 128), 1) < 64
    k_a, k_b = (kk, kk_sw) if g == 0 else (kk_sw, kk)
    v_a, v_b = (vv, vv_sw) if g == 0 else (vv_sw, vv)
    return (jnp.concatenate([jnp.where(lo, k_a, 0.0), jnp.where(lo, 0.0, k_b)], axis=0).astype(BF),
            jnp.concatenate([jnp.where(lo, v_a, 0.0), jnp.where(lo, 0.0, v_b)], axis=0).astype(BF))


def _stack_pairs(ref, g, rows=slice(None)):
    return jnp.concatenate([ref[rows, (4 * g + p) * 128:(4 * g + p + 1) * 128] for p in range(PAIRS)], axis=0)


def _sink_row(sk_ref, g, which):
    return jnp.concatenate([jnp.full((1, 128), sk_ref[0, 8 * g + 2 * p + which], F32) for p in range(PAIRS)], axis=1)


def _fold(full, cur):
    return jnp.where(cur, full[128:], full[:128])


def _attn_probs(raw, fold, sink):
    s = _fold(raw, fold[0]) * (ATTN_HEAD_DIM ** -0.5) + fold[3]
    m = jnp.maximum(jnp.max(s, axis=0, keepdims=True), sink)
    e = jnp.exp(s - m)
    es = jnp.exp(sink - m)
    inv = 1.0 / (jnp.sum(e, axis=0, keepdims=True) + es)
    return e * inv, es * inv


def _unfold(p, fold):
    return jnp.concatenate([p * fold[2], p * fold[1]], axis=0).astype(BF)


def _attn_fwd(aq, ak, av, sinks):
    S = aq.shape[0]
    N = S // SEQ_CHUNK
    nb = ATTN_FWD_BLOCKS_PER_STEP

    def body(sk_ref, q_ref, kp_ref, kc_ref, vp_ref, vc_ref, o_ref):
        kall, kall_sw = _kv_halves(kp_ref, kc_ref)
        vall, vall_sw = _kv_halves(vp_ref, vc_ref)
        blocks = []
        for sb in range(nb):
            kv = slice(sb * 128, sb * 128 + 256)
            rows = slice(sb * 128, (sb + 1) * 128)
            ops = [_group_operands(g, kall[kv], kall_sw[kv], vall[kv], vall_sw[kv]) for g in range(2)]
            raw = [_dot_nt(ops[g][0], _stack_pairs(q_ref, g, rows)) for g in range(2)]
            blocks.append((rows, ops, raw, _attn_fold(nb * pl.program_id(0) + sb)))
        for rows, ops, raw, fold in blocks:
            for g in range(2):
                probs = [_unfold(_attn_probs(raw[g][w * 256:(w + 1) * 256], fold, _sink_row(sk_ref, g, w))[0], fold)
                         for w in range(2)]
                o = _dot_tn(jnp.concatenate(probs, axis=0), ops[g][1])
                for p in range(PAIRS):
                    o_ref[rows, (4 * g + p) * 128:(4 * g + p + 1) * 128] = o[p * 128:(p + 1) * 128].astype(BF)

    prev = lambda i: (jnp.maximum(nb * i - 1, 0), 0)
    return pl.pallas_call(
        body, name="attn_fwd", grid=(N // nb,),
        in_specs=[_SMEM, _rows(nb * 128, 1024), pl.BlockSpec((128, 128), prev), _rows(nb * 128, 128),
                  pl.BlockSpec((128, 128), prev), _rows(nb * 128, 128)],
        out_specs=_rows(nb * 128, 1024),
        out_shape=jax.ShapeDtypeStruct((S, 1024), BF),
        compiler_params=_params(("parallel",)),
    )(sinks, aq, ak, ak, av, av)


def _group_norm(y):
    mu = jnp.mean(y, axis=-1, keepdims=True)
    yc = y - mu
    rstd = lax.rsqrt(jnp.mean(yc * yc, axis=-1, keepdims=True) + EPS)
    return yc * rstd, rstd


def _merge_fwd(x, ry, rg, ay, ga, gb, g_ret, w_ret, w_attn, w_out):
    S = x.shape[0]
    tm = 512

    def body(x_ref, ry_ref, rg_ref, ay_ref, ga_ref, gb_ref, gr_ref, wr_ref, wa_ref, wo_ref,
             u_ref, mg_ref, x1_ref):
        bb = _dot(ay_ref[...], wa_ref[...])
        for hd in range(RET_HEADS):
            vs = slice(hd * 256, (hd + 1) * 256)
            yn, _ = _group_norm(ry_ref[:, vs].astype(F32))
            rgv = rg_ref[:, vs].astype(F32)
            u_ref[:, vs] = (rgv * _sigmoid(rgv) * (yn * gr_ref[:, vs])).astype(BF)
        gated_b = _sigmoid(gb_ref[...].astype(F32)) * bb
        sa = _sigmoid(ga_ref[...].astype(F32))
        ba = _dot(u_ref[...], wr_ref[...])
        merged = (sa * ba + gated_b).astype(BF)
        mg_ref[...] = merged
        x1_ref[...] = x_ref[...] + _dot(merged, wo_ref[...])

    outs = [BF, BF, F32]
    return pl.pallas_call(
        body, name="merge_fwd", grid=(S // tm,),
        in_specs=[_rows(tm, 1024)] * 6 + [_fixed((1, 1024)), _VMEM, _VMEM, _VMEM],
        out_specs=[_rows(tm, 1024)] * 3,
        out_shape=[jax.ShapeDtypeStruct((S, 1024), dt) for dt in outs],
        compiler_params=_params(("parallel",)),
    )(x, ry, rg, ay, ga, gb, g_ret, w_ret, w_attn, w_out)


def _ffn_fwd(x1, tgt, g2, g3, w_gate, w_up, w_down):
    S = x1.shape[0]
    tm = 256

    def body(x1_ref, t_ref, g2_ref, g3_ref, wg_ref, wu_ref, wd_ref,
             gate_ref, up_ref, h2_ref, act_ref, dx2_ref, loss_ref, dg3_ref):
        @pl.when(pl.program_id(0) == 0)
        def _():
            loss_ref[...] = jnp.zeros_like(loss_ref)
            dg3_ref[...] = jnp.zeros_like(dg3_ref)

        x1v = x1_ref[...]
        r2 = lax.rsqrt(jnp.mean(x1v * x1v, axis=-1, keepdims=True) + EPS)
        h2 = (x1v * r2 * g2_ref[...]).astype(BF)
        h2_ref[...] = h2
        gate = _dot_nt(h2, wg_ref[...])
        up = _dot_nt(h2, wu_ref[...])
        gate_ref[...] = gate
        up_ref[...] = up
        act = (gate * _sigmoid(gate) * up).astype(BF)
        act_ref[...] = act
        x2 = x1v + _dot(act, wd_ref[...])
        r3 = lax.rsqrt(jnp.mean(x2 * x2, axis=-1, keepdims=True) + EPS)
        xh = x2 * r3
        g3v = g3_ref[...]
        err = xh * g3v - t_ref[...]
        loss_ref[...] += 0.5 * jnp.sum(jnp.mean(err * err, axis=-1, keepdims=True), axis=0, keepdims=True)
        dy = err * (1.0 / D_MODEL)
        dg3_ref[...] += jnp.sum(dy * xh, axis=0, keepdims=True)
        dxh = dy * g3v
        dx2_ref[...] = r3 * (dxh - xh * jnp.mean(dxh * xh, axis=-1, keepdims=True))

    return pl.pallas_call(
        body, name="ffn_fwd", grid=(S // tm,),
        in_specs=[_rows(tm, 1024), _rows(tm, 1024), _fixed((1, 1024)), _fixed((1, 1024)), _VMEM, _VMEM, _VMEM],
        out_specs=[_rows(tm, D_FF), _rows(tm, D_FF), _rows(tm, 1024), _rows(tm, D_FF), _rows(tm, 1024),
                   _fixed((1, LANES)), _fixed((1, 1024))],
        out_shape=[jax.ShapeDtypeStruct((S, D_FF), F32), jax.ShapeDtypeStruct((S, D_FF), F32),
                   jax.ShapeDtypeStruct((S, 1024), BF), jax.ShapeDtypeStruct((S, D_FF), BF),
                   jax.ShapeDtypeStruct((S, 1024), F32),
                   jax.ShapeDtypeStruct((1, LANES), F32), jax.ShapeDtypeStruct((1, 1024), F32)],
        compiler_params=_params(("arbitrary",)),
    )(x1, tgt, g2, g3, w_gate, w_up, w_down)


def _ffn_bwd(dx2, gate, up, x1, g2, w_gate, w_up, w_down):
    S = x1.shape[0]
    tm = 256

    def body(dx2_ref, gate_ref, up_ref, x1_ref, g2_ref, wg_ref, wu_ref, wd_ref,
             dgate_ref, dup_ref, dx1_ref, dg2_ref):
        @pl.when(pl.program_id(0) == 0)
        def _():
            dg2_ref[...] = jnp.zeros_like(dg2_ref)

        dx2v = dx2_ref[...]
        dact = _dot_nt(dx2v.astype(BF), wd_ref[...])
        gate = gate_ref[...]
        sg = _sigmoid(gate)
        dgate = (dact * up_ref[...] * (sg * (1.0 + gate * (1.0 - sg)))).astype(BF)
        dup = (dact * (gate * sg)).astype(BF)
        dgate_ref[...] = dgate
        dup_ref[...] = dup
        dh2 = _dot(dgate, wg_ref[...]) + _dot(dup, wu_ref[...])
        x1v = x1_ref[...]
        r2 = lax.rsqrt(jnp.mean(x1v * x1v, axis=-1, keepdims=True) + EPS)
        xh = x1v * r2
        dg2_ref[...] += jnp.sum(dh2 * xh, axis=0, keepdims=True)
        dxh = dh2 * g2_ref[...]
        dx1_ref[...] = dx2v + r2 * (dxh - xh * jnp.mean(dxh * xh, axis=-1, keepdims=True))

    return pl.pallas_call(
        body, name="ffn_bwd", grid=(S // tm,),
        in_specs=[_rows(tm, 1024), _rows(tm, D_FF), _rows(tm, D_FF), _rows(tm, 1024), _fixed((1, 1024)),
                  _VMEM, _VMEM, _VMEM],
        out_specs=[_rows(tm, D_FF), _rows(tm, D_FF), _rows(tm, 1024), _fixed((1, 1024))],
        out_shape=[jax.ShapeDtypeStruct((S, D_FF), BF), jax.ShapeDtypeStruct((S, D_FF), BF),
                   jax.ShapeDtypeStruct((S, 1024), F32), jax.ShapeDtypeStruct((1, 1024), F32)],
        compiler_params=_params(("arbitrary",)),
    )(dx2, gate, up, x1, g2, w_gate, w_up, w_down)


def _merge_bwd(dx1, ga, gb, u, ay, rg, ry, g_ret, w_ret, w_attn, w_out):
    S = dx1.shape[0]
    tm = 512

    def body(dx1_ref, ga_ref, gb_ref, u_ref, ay_ref, rg_ref, ry_ref, gr_ref, wr_ref, wa_ref, wo_ref,
             dgates_ref, drg_ref, dba_ref, dbb_ref, day_ref, dry_ref, dgr_ref):
        @pl.when(pl.program_id(0) == 0)
        def _():
            dgr_ref[...] = jnp.zeros_like(dgr_ref)

        dm = _dot_nt(dx1_ref[...].astype(BF), wo_ref[...])
        ba = _dot(u_ref[...], wr_ref[...])
        bb = _dot(ay_ref[...], wa_ref[...])
        sa = _sigmoid(ga_ref[...].astype(F32))
        sb = _sigmoid(gb_ref[...].astype(F32))
        dga = (dm * ba * (sa * (1.0 - sa))).astype(BF)
        dgb = (dm * bb * (sb * (1.0 - sb))).astype(BF)
        dgates_ref[:, :1024] = dga
        dgates_ref[:, 1024:] = dgb
        dba = (dm * sa).astype(BF)
        dbb = (dm * sb).astype(BF)
        dba_ref[...] = dba
        dbb_ref[...] = dbb
        day_ref[...] = _dot_nt(dbb, wa_ref[...]).astype(BF)
        du = _dot_nt(dba, wr_ref[...])
        for hd in range(RET_HEADS):
            vs = slice(hd * 256, (hd + 1) * 256)
            yn, rstd = _group_norm(ry_ref[:, vs].astype(F32))
            rgv = rg_ref[:, vs].astype(F32)
            sr = _sigmoid(rgv)
            silu = rgv * sr
            duv = du[:, vs]
            grv = gr_ref[:, vs]
            drg = (duv * (yn * grv) * (sr * (1.0 + rgv * (1.0 - sr)))).astype(BF)
            drg_ref[:, vs] = drg
            dyg = duv * silu
            dgr_ref[:, vs] += jnp.sum(dyg * yn, axis=0, keepdims=True)
            dyn = dyg * grv
            dry_ref[:, vs] = (rstd * (dyn - jnp.mean(dyn, axis=-1, keepdims=True)
                                      - yn * jnp.mean(dyn * yn, axis=-1, keepdims=True))).astype(KEPT)

    return pl.pallas_call(
        body, name="merge_bwd", grid=(S // tm,),
        in_specs=[_rows(tm, 1024)] * 7 + [_fixed((1, 1024)), _VMEM, _VMEM, _VMEM],
        out_specs=[_rows(tm, 2048), _rows(tm, 1024), _rows(tm, 1024), _rows(tm, 1024), _rows(tm, 1024),
                   _rows(tm, 1024), _fixed((1, 1024))],
        out_shape=[jax.ShapeDtypeStruct((S, 2048), BF), jax.ShapeDtypeStruct((S, 1024), BF),
                   jax.ShapeDtypeStruct((S, 1024), BF), jax.ShapeDtypeStruct((S, 1024), BF),
                   jax.ShapeDtypeStruct((S, 1024), BF), jax.ShapeDtypeStruct((S, 1024), KEPT),
                   jax.ShapeDtypeStruct((1, 1024), F32)],
        compiler_params=_params(("arbitrary",)),
    )(dx1, ga, gb, u, ay, rg, ry, g_ret, w_ret, w_attn, w_out)


def _attn_bwd(aq, ak, av, day, sinks, ca, sa, after):
    S = aq.shape[0]
    N = S // SEQ_CHUNK
    steps = N // 2

    def body(sk_ref, q_ref, kp_ref, kc_ref, vp_ref, vc_ref, do_ref, cq_ref, sq_ref, cp_ref, sp_ref, after_ref,
             dq_ref, dke_ref, dve_ref, dko_ref, dvo_ref, dsk_ref, carry_k, carry_v):
        i = pl.program_id(0)

        @pl.when(i == 0)
        def _():
            carry_k[...] = jnp.zeros_like(carry_k)
            carry_v[...] = jnp.zeros_like(carry_v)
            dsk_ref[...] = jnp.zeros_like(dsk_ref)

        @pl.when(i < steps)
        def _():
            kall, kall_sw = _kv_halves(kp_ref, kc_ref)
            vall, vall_sw = _kv_halves(vp_ref, vc_ref)
            lo = lax.broadcasted_iota(jnp.int32, (256, 128), 1) < 64
            blocks = []
            for sb in range(2):
                kv = slice(sb * 128, sb * 128 + 256)
                rows = slice(sb * 128, (sb + 1) * 128)
                ops = [_group_operands(g, kall[kv], kall_sw[kv], vall[kv], vall_sw[kv]) for g in range(2)]
                stacked = [(_stack_pairs(q_ref, g, rows), _stack_pairs(do_ref, g, rows)) for g in range(2)]
                raw = [_dot_nt(ops[g][0], stacked[g][0]) for g in range(2)]
                raw_dp = [_dot_nt(ops[g][1], stacked[g][1]) for g in range(2)]
                blocks.append((rows, ops, stacked, raw, raw_dp, _attn_fold(2 * i + sb)))
            grads = []
            d_sink = [jnp.zeros((1, PAIRS * 128), F32) for _ in range(4)]
            for rows, ops, stacked, raw, raw_dp, fold in blocks:
                cur = fold[0]
                cq, sq = cq_ref[rows, :], sq_ref[rows, :]
                dk_groups, dv_groups = [], []
                for g in range(2):
                    qs, dos = stacked[g]
                    ds_parts, p_parts = [], []
                    for which in range(2):
                        half = slice(which * 256, (which + 1) * 256)
                        pr, ps = _attn_probs(raw[g][half], fold, _sink_row(sk_ref, g, which))
                        dp = _fold(raw_dp[g][half], cur)
                        dr = jnp.sum(pr * dp, axis=0, keepdims=True)
                        ds_parts.append(_unfold(pr * (dp - dr) * (ATTN_HEAD_DIM ** -0.5), fold))
                        p_parts.append(_unfold(pr, fold))
                        d_sink[2 * g + which] = d_sink[2 * g + which] - ps * dr
                    ds = jnp.concatenate(ds_parts, axis=0)
                    dq = _dot_tn(ds, ops[g][0])
                    dk_both = _dot(ds, qs)
                    dv_both = _dot(jnp.concatenate(p_parts, axis=0), dos)
                    for p in range(PAIRS):
                        sl = slice((4 * g + p) * 128, (4 * g + p + 1) * 128)
                        dq_ref[rows, sl] = _rot_attn_bwd(dq[p * 128:(p + 1) * 128], cq, sq).astype(BF)
                    tk = jnp.where(lo, dk_both[:256], dk_both[256:])
                    tv = jnp.where(lo, dv_both[:256], dv_both[256:])
                    dk_groups.append(tk + pltpu.roll(tk, 64, 1))
                    dv_groups.append(tv + pltpu.roll(tv, 64, 1))
                grads.append((jnp.where(lo, dk_groups[0], dk_groups[1]), jnp.where(lo, dv_groups[0], dv_groups[1])))
            dsk_ref[...] += jnp.concatenate(d_sink, axis=0)
            (dk_a, dv_a), (dk_b, dv_b) = grads
            dko_ref[...] = _rot_attn_bwd(carry_k[...] + dk_a[:128], cp_ref[...], sp_ref[...]).astype(BF)
            dvo_ref[...] = (carry_v[...] + dv_a[:128]).astype(BF)
            dke_ref[...] = _rot_attn_bwd(dk_a[128:] + dk_b[:128], cq_ref[0:128, :], sq_ref[0:128, :]).astype(BF)
            dve_ref[...] = (dv_a[128:] + dv_b[:128]).astype(BF)
            carry_k[...] = dk_b[128:]
            carry_v[...] = dv_b[128:]

        @pl.when(i == steps)
        def _():
            dko_ref[...] = _rot_attn_bwd(carry_k[...], cp_ref[...], sp_ref[...]).astype(BF)
            dvo_ref[...] = carry_v[...].astype(BF)

    two = lambda i: (jnp.minimum(i, steps - 1), 0)
    before = lambda i: (jnp.maximum(2 * i - 1, 0), 0)
    odd = lambda i: (jnp.maximum(i - 1, 0), 0)
    blk = lambda r, w, im: pl.BlockSpec((r, w), im)
    half = jax.ShapeDtypeStruct((S // 2, 128), BF)
    dq, dk_even, dv_even, dk_odd, dv_odd, dsk = pl.pallas_call(
        body, name="attn_bwd", grid=(steps + 1,),
        in_specs=[_SMEM, blk(256, 1024, two), blk(128, 128, before), blk(256, 128, two), blk(128, 128, before),
                  blk(256, 128, two), blk(256, 1024, two), blk(256, 128, two), blk(256, 128, two),
                  blk(128, 128, before), blk(128, 128, before), _ANY],
        out_specs=[blk(256, 1024, two), blk(128, 128, two), blk(128, 128, two), blk(128, 128, odd),
                   blk(128, 128, odd), _fixed((4, PAIRS * 128))],
        out_shape=[jax.ShapeDtypeStruct((S, 1024), BF), half, half, half, half,
                   jax.ShapeDtypeStruct((4, PAIRS * 128), F32)],
        scratch_shapes=[pltpu.VMEM((128, 128), F32), pltpu.VMEM((128, 128), F32)],
        compiler_params=_params(("arbitrary",)),
    )(sinks, aq, ak, ak, av, av, day, ca, sa, ca, sa, after)

    def interleave(even, odd_):
        return jnp.stack([even.reshape(steps, 128, 128), odd_.reshape(steps, 128, 128)], axis=1).reshape(S, 128)

    return dq, interleave(dk_even, dk_odd), interleave(dv_even, dv_odd), dsk


def _ret_bwd(rq, rk, rv, dry, states, dmat, qd, kd, cd, cr, sr, after):
    S = rq.shape[0]
    N = S // RET_CHUNK
    scale = RET_QK_DIM ** -0.5
    cps = RET_CHUNKS_PER_STEP
    steps = N // cps

    def body(q_ref, k_ref, v_ref, dy_ref, st_ref, dm_ref, qd_ref, kd_ref, cd_ref, cr_ref, sr_ref, after_ref,
             out_ref, dstate):
        @pl.when(pl.program_id(0) == 0)
        def _():
            dstate[...] = jnp.zeros_like(dstate)

        for hd in range(RET_HEADS):
            qs = slice(hd * 128, (hd + 1) * 128)
            vs = slice(hd * 256, (hd + 1) * 256)
            dmv = dm_ref[hd]
            free = []
            for ch in range(cps):
                rs = slice(ch * RET_CHUNK, (ch + 1) * RET_CHUNK)
                crv, srv = cr_ref[rs, :], sr_ref[rs, :]
                q = q_ref[rs, qs]
                kf = k_ref[rs, qs]
                k = kf.astype(BF)
                v = v_ref[rs, vs]
                do = dy_ref[rs, vs].astype(F32)
                dob = do.astype(BF)
                doq = (do * qd_ref[hd]).astype(BF)
                a = (_dot_nt(q, k) * dmv).astype(BF)
                da = (_dot_nt(dob, v) * dmv).astype(BF)
                dq = _dot(da, k) + _dot_nt(doq, st_ref[ch, hd])
                out_ref[rs, qs] = _rot_ret_bwd(dq * scale, crv, srv).astype(BF)
                free.append((rs, crv, srv, v, (kf * kd_ref[hd]).astype(BF),
                             _dot_tn(a, dob), _dot_tn(da, q), _dot_tn(q, doq)))
            ds = dstate[hd]
            for rs, crv, srv, v, kdv, dv_free, dk_free, ds_update in reversed(free):
                dsb = ds.astype(BF)
                dv = dv_free + _dot(kdv, dsb)
                dk = dk_free + _dot_nt(v, dsb) * kd_ref[hd]
                ds = ds * cd_ref[hd] + ds_update
                out_ref[rs, 512 + hd * 128:512 + (hd + 1) * 128] = _rot_ret_bwd(dk, crv, srv).astype(BF)
                out_ref[rs, 1024 + hd * 256:1024 + (hd + 1) * 256] = dv.astype(BF)
            dstate[hd] = ds

    rev = lambda w: pl.BlockSpec((cps * RET_CHUNK, w), lambda i: (steps - 1 - i, 0))
    return pl.pallas_call(
        body, name="ret_bwd", grid=(steps,),
        in_specs=[rev(512), rev(512), rev(1024), rev(1024),
                  pl.BlockSpec((cps, RET_HEADS, 128, 256), lambda i: (steps - 1 - i, 0, 0, 0)),
                  _VMEM, _VMEM, _VMEM, _VMEM, rev(128), rev(128), _ANY],
        out_specs=rev(2048),
        out_shape=jax.ShapeDtypeStruct((S, 2048), BF),
        scratch_shapes=[pltpu.VMEM((RET_HEADS, 128, 256), F32)],
        compiler_params=_params(("arbitrary",)),
    )(rq, rk, rv, dry, states, dmat, qd, kd, cd, cr, sr, after)


def _proj_bwd(d_pieces, x, dx1, g1, w_in_t):
    S = x.shape[0]
    tm = 512
    widths = [d.shape[1] for d in d_pieces]
    assert sum(widths) == D_IN
    nd = len(d_pieces)

    def body(*refs):
        d_refs = refs[:nd]
        x_ref, dx1_ref, g_ref, w_ref, gx_ref, dg_ref, db_ref = refs[nd:]

        @pl.when(pl.program_id(0) == 0)
        def _():
            dg_ref[...] = jnp.zeros_like(dg_ref)
            db_ref[...] = jnp.zeros_like(db_ref)

        d_all = jnp.concatenate([ref[...] for ref in d_refs], axis=1)
        dh = _dot(d_all, w_ref[...])
        db_ref[...] += jnp.sum(d_all.astype(F32), axis=0, keepdims=True)
        xv = x_ref[...]
        r = lax.rsqrt(jnp.mean(xv * xv, axis=-1, keepdims=True) + EPS)
        xh = xv * r
        dg_ref[...] += jnp.sum(dh * xh, axis=0, keepdims=True)
        dxh = dh * g_ref[...]
        gx_ref[...] = dx1_ref[...] + r * (dxh - xh * jnp.mean(dxh * xh, axis=-1, keepdims=True))

    return pl.pallas_call(
        body, name="proj_bwd", grid=(S // tm,),
        in_specs=[_rows(tm, n) for n in widths] + [_rows(tm, 1024), _rows(tm, 1024), _fixed((1, 1024)), _VMEM],
        out_specs=[_rows(tm, 1024), _fixed((1, 1024)), _fixed((1, D_IN))],
        out_shape=[jax.ShapeDtypeStruct((S, 1024), F32), jax.ShapeDtypeStruct((1, 1024), F32),
                   jax.ShapeDtypeStruct((1, D_IN), F32)],
        compiler_params=_params(("arbitrary",)),
    )(*d_pieces, x, dx1, g1, w_in_t)


SHARD_ROWS = (1600, 256, 256, 256, 704, 704, 704)


class Group(NamedTuple):
    name: str
    tensors: tuple
    offs: tuple
    block: int
    tile: int

    @property
    def half(self):
        return self.block // 2

    def off(self, t):
        return self.offs[self.tensors.index(t)]


LATE = Group("late", (0,), (0,), 1600, 800)
EARLY = Group("early", (4, 5, 6, 1, 2, 3), (0, 704, 1408, 2112, 2368, 2624), 2880, 480)
GROUP_OF = {t: g for g in (LATE, EARLY) for t in g.tensors}


def _wgrad(a_list, b, t, packed, name):
    S = b.shape[0]
    widths = [a.shape[1] for a in a_list]
    M = sum(widths)
    ts = 1024 if M <= 4096 else 512
    steps = S // ts
    na = len(a_list)
    grp = GROUP_OF[t]
    r, off = SHARD_ROWS[t], grp.off(t)
    assert M == N_CHIPS * r
    segments = [(chip * r, r, chip * grp.block + off) for chip in range(N_CHIPS)]
    stage_rows = r
    n_in = na + 1 + (2 if packed is not None else 0)

    def body(*refs):
        a_refs, b_ref = refs[:na], refs[na]
        g32_ref, gb_ref, acc, stage, sems = refs[n_in:]
        i = pl.program_id(0)

        @pl.when(i == 0)
        def _():
            acc[...] = jnp.zeros_like(acc)

        bv = b_ref[...].astype(BF)
        lo = 0
        for a_ref, n in zip(a_refs, widths):
            acc[lo:lo + n, :] += _dot_tn(a_ref[...].astype(BF), bv)
            lo += n

        @pl.when(i == steps - 1)
        def _():
            for a0, n, p0 in segments:
                c32 = pltpu.make_async_copy(acc.at[a0:a0 + n, :], g32_ref.at[pl.ds(p0, n), :], sems.at[0])
                c32.start()
                stage[0:n, :] = acc[a0:a0 + n, :].astype(BF)
                c16 = pltpu.make_async_copy(stage.at[0:n, :], gb_ref.at[pl.ds(p0, n), :], sems.at[1])
                c16.start()
                c32.wait()
                c16.wait()

    shapes = [jax.ShapeDtypeStruct((N_CHIPS * grp.block, 1024), F32),
              jax.ShapeDtypeStruct((N_CHIPS * grp.block, 1024), BF)]
    operands = list(a_list) + [b]
    in_specs = [_rows(ts, n) for n in widths] + [_rows(ts, 1024)]
    aliases = {}
    if packed is not None:
        operands += list(packed)
        in_specs += [_ANY, _ANY]
        aliases = {na + 1: 0, na + 2: 1}
    return pl.pallas_call(
        body, name=name, grid=(steps,),
        in_specs=in_specs, out_specs=[_ANY, _ANY], out_shape=shapes,
        scratch_shapes=[pltpu.VMEM((M, 1024), F32), pltpu.VMEM((stage_rows, 1024), BF),
                        pltpu.SemaphoreType.DMA((2,))],
        input_output_aliases=aliases,
        compiler_params=_params(("arbitrary",)),
    )(*operands)


def _stage_weights(place, ws, name, after=()):
    nw = len(ws)

    def body(p_ref, *refs):
        for w_ref, o_ref in zip(refs[:nw], refs[nw + len(after):]):
            o_ref[...] = w_ref[...].astype(BF)

    return pl.pallas_call(
        body, name=name,
        grid_spec=pltpu.PrefetchScalarGridSpec(
            num_scalar_prefetch=1, grid=(1,),
            in_specs=[pl.BlockSpec(w.shape, lambda i, p: (0, 0)) for w in ws] + [_ANY] * len(after),
            out_specs=[pl.BlockSpec(w.shape, lambda i, p: (p[0], 0)) for w in ws]),
        out_shape=[jax.ShapeDtypeStruct((N_CHIPS * w.shape[0], 1024), BF) for w in ws],
        compiler_params=_params(("arbitrary",)),
    )(place, *ws, *after)


N_PEERS = 7


def _sum_all(place, g32, others, grp):
    nb = grp.half // grp.tile

    def body(p_ref, g_ref, *refs):
        acc = g_ref[...]
        for r_ref in refs[:N_PEERS]:
            acc = acc + r_ref[...].astype(F32)
        refs[-1][...] = acc

    blk = lambda f: pl.BlockSpec((grp.tile, 1024), f)
    slot = lambda k: blk(lambda i, p: (k * nb + i, 0))
    return pl.pallas_call(
        body, name="sum_all_" + grp.name,
        grid_spec=pltpu.PrefetchScalarGridSpec(
            num_scalar_prefetch=1, grid=(nb,),
            in_specs=[blk(lambda i, p: ((2 * p[0] + p[1]) * nb + i, 0))] + [slot(k) for k in range(N_PEERS)],
            out_specs=blk(lambda i, p: (p[1] * nb + i, 0))),
        out_shape=jax.ShapeDtypeStruct((grp.block, 1024), F32),
        compiler_params=_params(("arbitrary",)),
    )(place, g32, *[others] * N_PEERS)


def _adamw(w, g, m, v, name, g_row0=0, after=None):
    r = w.shape[0]
    tr = next((c for c in (352, 320, 64) if r % c == 0 and g_row0 % c == 0), r)
    assert g_row0 % tr == 0
    g0 = g_row0 // tr
    extra = [] if after is None else [after]

    def body(w_ref, g_ref, m_ref, v_ref, *rest):
        go_ref, d_ref, mo_ref, vo_ref = rest[len(extra):]
        gv = g_ref[...]
        mn = ADAM_B1 * m_ref[...] + (1.0 - ADAM_B1) * gv
        vn = ADAM_B2 * v_ref[...] + (1.0 - ADAM_B2) * (gv * gv)
        m_hat = mn / (1.0 - ADAM_B1 ** ADAM_STEP)
        v_hat = vn / (1.0 - ADAM_B2 ** ADAM_STEP)
        go_ref[...] = gv
        d_ref[...] = -ADAM_LR * (m_hat / (jnp.sqrt(v_hat) + ADAM_EPS) + ADAM_WD * w_ref[...])
        mo_ref[...] = mn
        vo_ref[...] = vn

    return pl.pallas_call(
        body, name=name, grid=(r // tr,),
        in_specs=[_rows(tr, 1024), pl.BlockSpec((tr, 1024), lambda i: (g0 + i, 0)), _rows(tr, 1024), _rows(tr, 1024)]
        + [_ANY] * len(extra),
        out_specs=[_rows(tr, 1024)] * 4,
        out_shape=[jax.ShapeDtypeStruct((r, 1024), F32)] * 4,
        compiler_params=_params(("parallel",)),
    )(w, g, m, v, *extra)


def _place():
    x, y, c = lax.axis_index("x"), lax.axis_index("y"), lax.axis_index("c")
    chips = [(1 - x, y), (x, 1 - y), (1 - x, 1 - y)]
    return x, y, c, chips


_HBM =pl.BlockSpec(memory_space=pltpu.HBM)
_SEM = pl.BlockSpec(memory_space=pltpu.SEMAPHORE)
_DATAFLOW = pltpu.SideEffectType.DATAFLOW_SIDE_EFFECTING


def _exchange(name, arrays, n, plan):
    na = len(arrays)

    def body(*refs):
        pairs = plan(refs[na:2 * na], refs[2 * na], refs[2 * na + 1])
        for send, _ in pairs:
            send.start()
        for send, recv in pairs:
            send.wait_send()
            recv.wait_recv()

    return pl.pallas_call(
        body, name=name, in_specs=[_ANY] * na, out_specs=[_ANY] * na,
        out_shape=[jax.ShapeDtypeStruct(a.shape, a.dtype) for a in arrays],
        scratch_shapes=[pltpu.SemaphoreType.DMA((n,)), pltpu.SemaphoreType.DMA((n,))],
        input_output_aliases={i: i for i in range(na)},
    )(*arrays)


def _exchange_start(name, arrays, n, plan, after=()):
    na = len(arrays)

    def body(*refs):
        for send, _ in plan(refs[:na], refs[na + len(after)], refs[na + len(after) + 1]):
            send.start()
        refs[-1][...] = jnp.zeros_like(refs[-1])

    out = pl.pallas_call(
        body, name=name,
        out_shape=(pltpu.SemaphoreType.DMA((n,)), pltpu.SemaphoreType.DMA((n,)),
                   *[pltpu.HBM(a.shape, a.dtype) for a in arrays], jax.ShapeDtypeStruct((8, LANES), F32)),
        in_specs=[_HBM] * na + [_ANY] * len(after), out_specs=(_SEM, _SEM, *[_HBM] * na, _VMEM),
        input_output_aliases={i: 2 + i for i in range(na)},
        compiler_params=pltpu.CompilerParams(has_side_effects=_DATAFLOW),
    )(*[pltpu.with_memory_space_constraint(a, pltpu.HBM) for a in arrays], *after)
    return (out[0], out[1]), list(out[2:2 + na]), out[-1]


def _exchange_wait(name, sems, in_flight, plan, after=()):
    na = len(in_flight)

    def body(*refs):
        for send, recv in plan(refs[:na], refs[na], refs[na + 1]):
            send.wait_send()
            recv.wait_recv()

    out = pl.pallas_call(
        body, name=name,
        out_shape=tuple(pltpu.HBM(a.shape, a.dtype) for a in in_flight),
        in_specs=[_HBM] * na + [_SEM, _SEM] + [_ANY] * len(after), out_specs=tuple([_HBM] * na),
        input_output_aliases={i: i for i in range(na)},
        compiler_params=pltpu.CompilerParams(has_side_effects=_DATAFLOW),
    )(*in_flight, *sems, *after)
    return list(out)


def _remote(src, dst, send_sems, recv_sems, k, to):
    return pltpu.make_async_remote_copy(src_ref=src, dst_ref=dst, send_sem=send_sems.at[k], recv_sem=recv_sems.at[k],
                                        device_id=to, device_id_type=MESH)


def _half_rows(ref, tid, pj, half):
    h = SHARD_ROWS[tid] // 2
    return ref.at[pl.ds((2 * pj + half) * h, h), :]


def _plan_gather_halves(tids):
    def plan(refs, ss, rs):
        x, y, c, chips = _place()
        pairs = []
        for t, tid in enumerate(tids):
            mine = _half_rows(refs[t], tid, 2 * x + y, c)
            for k, (cx, cy) in enumerate(chips):
                theirs = _half_rows(refs[t], tid, 2 * cx + cy, c)
                pairs.append((_remote(mine, mine, ss, rs, 3 * t + k, (cx, cy, c)),
                              _remote(theirs, theirs, ss, rs, 3 * t + k, (x, y, c))))
        return pairs
    return plan


def _plan_forward_halves(tids):
    def plan(refs, ss, rs):
        x, y, c, chips = _place()
        pairs = []
        for t, tid in enumerate(tids):
            for k, (cx, cy) in enumerate(chips):
                got = _half_rows(refs[t], tid, 2 * cx + cy, c)
                missing = _half_rows(refs[t], tid, 2 * cx + cy, 1 - c)
                pairs.append((_remote(got, got, ss, rs, 3 * t + k, (x, y, 1 - c)),
                              _remote(missing, missing, ss, rs, 3 * t + k, (x, y, 1 - c))))
        return pairs
    return plan


def _plan_to_all(grp):
    def plan(refs, ss, rs):
        x, y, c, chips = _place()
        peers = [(x, y, 1 - c)] + [(cx, cy, pc) for cx, cy in chips for pc in (c, 1 - c)]
        assert len(peers) == N_PEERS
        cps = [_remote(refs[0].at[pl.ds((2 * px + py) * grp.block + pc * grp.half, grp.half), :],
                       refs[1].at[pl.ds(k * grp.half, grp.half), :], ss, rs, k, (px, py, pc))
               for k, (px, py, pc) in enumerate(peers)]
        return [(cp, cp) for cp in cps]
    return plan


def _plan_share_halves(grp, k=0):
    def plan(refs, ss, rs):
        x, y, c, _ = _place()
        mine = refs[0].at[pl.ds(c * grp.half, grp.half), :]
        other = refs[0].at[pl.ds((1 - c) * grp.half, grp.half), :]
        return [(_remote(mine, mine, ss, rs, k, (x, y, 1 - c)), _remote(other, other, ss, rs, k, (x, y, 1 - c)))]
    return plan


def _plan_both(first, n_refs, second):
    def plan(refs, ss, rs):
        return first(refs[:n_refs], ss, rs) + second(refs[n_refs:], ss, rs)
    return plan


def _empty(rows, dtype):
    return lax.empty((rows, 1024), dtype)


def _all_reduce_small(v, after):
    m_per, n = v.shape
    n_dev = 8

    def body(x_ref, after_ref, out_ref, all_ref, send_sems, recv_sems, local_sem):
        x, y, c, chips = _place()
        me, sibling = (x, y, c), (x, y, 1 - c)

        def rows(px, py, pc):
            return all_ref.at[pl.ds((4 * px + 2 * py + pc) * m_per, m_per), :]

        def copy(k, block, to, src=None):
            return pltpu.make_async_remote_copy(
                src_ref=rows(*block) if src is None else src, dst_ref=rows(*block),
                send_sem=send_sems.at[k], recv_sem=recv_sems.at[k], device_id=to, device_id_type=MESH)

        mine = pltpu.make_async_copy(x_ref, rows(*me), local_sem)
        mine.start()
        first = [copy(0, me, sibling, src=x_ref)]
        first += [copy(1 + k, me, (*chip, c), src=x_ref) for k, chip in enumerate(chips)]
        for cp in first:
            cp.start()
        passed = [copy(4 + k, (*chip, c), sibling) for k, chip in enumerate(chips)]
        for k, chip in enumerate(chips):
            copy(1 + k, (*chip, c), me).wait_recv()
            passed[k].start()
        copy(0, sibling, me).wait_recv()
        for k, chip in enumerate(chips):
            copy(4 + k, (*chip, 1 - c), me).wait_recv()
        for cp in first + passed:
            cp.wait_send()
        mine.wait()
        acc = all_ref[pl.ds(0, m_per), :]
        for d in range(1, n_dev):
            acc = acc + all_ref[pl.ds(d * m_per, m_per), :]
        out_ref[...] = acc

    return pl.pallas_call(
        body, name="all_reduce_small",
        in_specs=[_VMEM, _ANY], out_specs=_VMEM,
        out_shape=jax.ShapeDtypeStruct((m_per, n), F32),
        scratch_shapes=[pltpu.VMEM((n_dev * m_per, n), F32),
                        pltpu.SemaphoreType.DMA((7,)), pltpu.SemaphoreType.DMA((7,)), pltpu.SemaphoreType.DMA],
        compiler_params=pltpu.CompilerParams(has_side_effects=True),
    )(v, after)


def _tables(S, token):
    pos = jnp.arange(S, dtype=jnp.int32).astype(F32) + token[0, 0]

    def cs(half):
        inv_freq = ROPE_THETA ** (-jnp.arange(half, dtype=F32) / half)
        ang = pos[:, None] * inv_freq[None, :]
        return jnp.cos(ang), jnp.sin(ang)

    c64, s64 = cs(64)
    c32, s32 = cs(32)
    cr = jnp.concatenate([c64, c64], axis=1)
    sr = jnp.concatenate([-s64, s64], axis=1)
    ca = jnp.concatenate([c32, c32, c32, c32], axis=1)
    sa = jnp.concatenate([-s32, s32, -s32, s32], axis=1)
    return cr, sr, ca, sa


def _decays():
    H, C = RET_HEADS, RET_CHUNK
    log_gamma = jnp.log1p(-jnp.exp2(-5.0 - jnp.arange(H, dtype=F32)))
    idx = jnp.arange(C, dtype=F32)
    rel = idx[:, None] - idx[None, :]
    dmat = jnp.where(rel[None] >= 0, jnp.exp(log_gamma[:, None, None] * jnp.maximum(rel, 0.0)[None]), 0.0)
    qd = jnp.exp(log_gamma[:, None] * (idx + 1.0))[:, :, None]
    kd = jnp.exp(log_gamma[:, None] * (C - 1.0 - idx))[:, :, None]
    cd = jnp.exp(log_gamma * C)[:, None, None]
    return dmat, qd, kd, cd


def _local_step(place, x, tgt, ln1_g, b_in, ret_norm_g, attn_sinks, ln2_g, lnf_g, w_in, other_weights, tables, decays):
    cr, sr, ca, sa = tables
    dmat, qd, kd, cd = decays
    forward_weights, landed_weights = other_weights

    h, rq, rk, rv, rg, aq, ak, av, ga, gb = _proj_fwd(x, ln1_g, w_in, b_in, cr, sr, ca, sa)
    ry, states = _ret_fwd(rq, rk, rv, dmat, qd, kd, cd)
    token = forward_weights(ry)
    ay = _attn_fwd(aq, ak, av, attn_sinks + token[0, 0])
    w_ret, w_attn, w_out, w_gate, w_up, w_down = landed_weights(ay)
    u, merged, x1 = _merge_fwd(x, ry, rg, ay, ga, gb, ret_norm_g, w_ret, w_attn, w_out)
    gate, up, h2, act, dx2, loss, dg3 = _ffn_fwd(x1, tgt, ln2_g, lnf_g, w_gate, w_up, w_down)

    dgate, dup, dx1, dg2 = _ffn_bwd(dx2, gate, up, x1, ln2_g, w_gate, w_up, w_down)
    early = _wgrad([act], dx2, 6, None, "wgrad_down")
    early = _wgrad([dgate], h2, 4, early, "wgrad_gate")
    early = _wgrad([dup], h2, 5, early, "wgrad_up")
    dgates, drg, dba, dbb, day, dry, dgr = _merge_bwd(dx1, ga, gb, u, ay, rg, ry, ret_norm_g,
                                                                w_ret, w_attn, w_out)
    early = _wgrad([merged], dx1, 3, early, "wgrad_out")
    early = _wgrad([u], dba, 1, early, "wgrad_ret")
    e32, e16 = _wgrad([ay], dbb, 2, early, "wgrad_attn")

    sems, flight, token = _exchange_start("to_all_start_early", [e16, _empty(N_PEERS * EARLY.half, BF)], N_PEERS,
                                          _plan_to_all(EARLY))
    daq, dak, dav, dsk = _attn_bwd(aq, ak, av, day, attn_sinks, ca, sa, token)
    d_ret = _ret_bwd(rq, rk, rv, dry, states, dmat, qd, kd, cd, cr, sr, token)
    d_proj = [d_ret, drg, daq, dak, dav, dgates]
    l32, l16 = _wgrad(d_proj, h, 0, None, "wgrad_in")
    _, others = _exchange_wait("to_all_wait_early", sems, flight, _plan_to_all(EARLY), after=(l32,))
    share = _plan_share_halves(EARLY, k=N_PEERS)
    sems, flight, token = _exchange_start(
        "to_all_late_share_early_start", [l16, _empty(N_PEERS * LATE.half, BF), _sum_all(place, e32, others, EARLY)],
        N_PEERS + 1, _plan_both(_plan_to_all(LATE), 2, share))
    grad_x, dg1, db = _proj_bwd(d_proj, x, dx1, ln1_g + token[0, 0], w_in)
    early_shard, = _exchange_wait("share_wait_early", sems, flight[2:], share, after=(grad_x,))

    def finish_late(after):
        _, others = _exchange_wait("to_all_wait_late", sems, flight[:2], _plan_to_all(LATE), after=after)
        return _sum_all(place, l32, others, LATE)

    d_sinks = dsk.reshape(2, 2, PAIRS, 128).sum(-1).transpose(0, 2, 1).reshape(1, ATTN_Q_HEADS)
    small = dict(loss=loss, ln1_g=dg1, b_in=db, ret_norm_g=dgr, attn_sinks=d_sinks,
                 ln2_g=dg2, lnf_g=dg3)
    return grad_x, early_shard, token, finish_late, small


SMALL_ROWS = 16


def _pack_small(ln1, ret, ln2, lnf, b_in, sinks, loss, name):
    def body(a_ref, b_ref, c_ref, d_ref, bi_ref, sk_ref, ls_ref, o_ref):
        o_ref[...] = jnp.zeros_like(o_ref)
        for row, ref in enumerate((a_ref, b_ref, c_ref, d_ref)):
            o_ref[row:row + 1, :] = ref[...]
        for k in range(6):
            o_ref[4 + k:5 + k, :] = bi_ref[:, k * 1024:(k + 1) * 1024]
        o_ref[10:11, 0:D_IN - 6144] = bi_ref[:, 6144:D_IN]
        o_ref[11:12, 0:ATTN_Q_HEADS] = sk_ref[...]
        o_ref[12:13, 0:LANES] = ls_ref[...]

    return pl.pallas_call(
        body, name=name, in_specs=[_VMEM] * 7, out_specs=_VMEM,
        out_shape=jax.ShapeDtypeStruct((SMALL_ROWS, 1024), F32),
    )(ln1.reshape(1, 1024), ret.reshape(1, 1024), ln2.reshape(1, 1024), lnf.reshape(1, 1024),
      b_in.reshape(1, D_IN), sinks.reshape(1, ATTN_Q_HEADS), loss.reshape(1, LANES))


def _unpack_small(p):
    return dict(ln1_g=p[0:1], ret_norm_g=p[1:2], ln2_g=p[2:3], lnf_g=p[3], b_in=p[4:11].reshape(1, -1)[:, :D_IN],
                attn_sinks=p[11:12, :ATTN_Q_HEADS], loss=p[12, 0])


def kernel(x, ln1_g, w_in, b_in, ret_norm_g, w_ret_out, attn_sinks, w_attn_out, w_out, ln2_g, w_ffn_gate, w_ffn_up, w_ffn_down, lnf_g, loss_target, m_ln1_g, m_w_in, m_b_in, m_ret_norm_g, m_w_ret_out, m_attn_sinks, m_w_attn_out, m_w_out, m_ln2_g, m_w_ffn_gate, m_w_ffn_up, m_w_ffn_down, m_lnf_g, v_ln1_g, v_w_in, v_b_in, v_ret_norm_g, v_w_ret_out, v_attn_sinks, v_w_attn_out, v_w_out, v_ln2_g, v_w_ffn_gate, v_w_ffn_up, v_w_ffn_down, v_lnf_g):
    xi = lax.axis_index("x")
    yi = lax.axis_index("y")
    ci = lax.axis_index("c")
    j = 2 * xi + yi

    place = jnp.stack([j, ci]).astype(jnp.int32)

    names = ["w_in", "w_ret_out", "w_attn_out", "w_out", "w_ffn_gate", "w_ffn_up", "w_ffn_down"]
    transposed = [True, False, False, False, True, True, False]
    view = lambda a, t: a[0].T if t else a[0]
    big_w = [view(a, t) for a, t in zip([w_in, w_ret_out, w_attn_out, w_out, w_ffn_gate, w_ffn_up, w_ffn_down], transposed)]
    big_m = [view(a, t) for a, t in zip([m_w_in, m_w_ret_out, m_w_attn_out, m_w_out, m_w_ffn_gate, m_w_ffn_up, m_w_ffn_down], transposed)]
    big_v = [view(a, t) for a, t in zip([v_w_in, v_w_ret_out, v_w_attn_out, v_w_out, v_w_ffn_gate, v_w_ffn_up, v_w_ffn_down], transposed)]

    staged_in = _stage_weights(place, big_w[:1], "stage_w_in")[0]
    sems, in_flight, token = _exchange_start("gather_in_start", [staged_in], 3, _plan_gather_halves([0]))
    staged = _stage_weights(place, big_w[1:], "stage_others", after=(token,))
    tables = _tables(x.shape[1], token)
    decays = _decays()
    w_in_full = _exchange_wait("gather_in_wait", sems, in_flight, _plan_gather_halves([0]),
                               after=(*staged, *tables, *decays))[0]
    w_in_full = _exchange("gather_in_forward", [w_in_full], 3, _plan_forward_halves([0]))[0]
    rest = list(range(1, 7))
    crossing = _exchange_start("gather_start", staged, 3 * len(rest), _plan_gather_halves(rest), after=(w_in_full,))
    token = crossing[2]
    forwarding = []

    def forward_weights(after):
        arrived = _exchange_wait("gather_wait", crossing[0], crossing[1], _plan_gather_halves(rest), after=(after,))
        forwarding.extend(_exchange_start("forward_start", arrived, 3 * len(rest), _plan_forward_halves(rest)))
        return forwarding[2]

    def landed_weights(after):
        return _exchange_wait("forward_wait", forwarding[0], forwarding[1], _plan_forward_halves(rest), after=(after,))

    other_weights = (forward_weights, landed_weights)

    grad_x, early_shard, token, finish_late, small = _local_step(
        place, x[0], loss_target[0], ln1_g, b_in + token[0, 0], ret_norm_g, attn_sinks, ln2_g, lnf_g.reshape(1, -1),
        w_in_full, other_weights, tables, decays)

    upd = {}

    def update(t, shard, after):
        res = _adamw(big_w[t], shard, big_m[t], big_v[t], "adamw_" + names[t], g_row0=GROUP_OF[t].off(t), after=after)
        upd[names[t]] = tuple((r.T if transposed[t] else r)[None] for r in res)
        return res[0]

    done = [update(t, early_shard, token) for t in EARLY.tensors]
    late_half = finish_late((*done, grad_x))
    pg = _all_reduce_small(_pack_small(
        small["ln1_g"], small["ret_norm_g"], small["ln2_g"], small["lnf_g"], small["b_in"],
        small["attn_sinks"], small["loss"], "pack_small_grads"), late_half)
    late_shard = _exchange("share_late", [late_half], 1, _plan_share_halves(LATE))[0]
    update(0, late_shard, None)

    sm = _unpack_small(pg)
    z = jnp.zeros((1, LANES), F32)
    pw = _pack_small(ln1_g, ret_norm_g, ln2_g, lnf_g, b_in, attn_sinks, z, "pack_small_w")
    pm = _pack_small(m_ln1_g, m_ret_norm_g, m_ln2_g, m_lnf_g, m_b_in, m_attn_sinks, z, "pack_small_m")
    pv = _pack_small(v_ln1_g, v_ret_norm_g, v_ln2_g, v_lnf_g, v_b_in, v_attn_sinks, z, "pack_small_v")
    small_res = _adamw(pw, pg, pm, pv, "adamw_small", after=token)
    sd, smn, svn = (_unpack_small(t) for t in small_res[1:])
    for n in ["ln1_g", "b_in", "ret_norm_g", "attn_sinks", "ln2_g", "lnf_g"]:
        upd[n] = (sm[n], sd[n], smn[n], svn[n])

    order_w = ["ln1_g", "w_in", "b_in", "ret_norm_g", "w_ret_out", "attn_sinks", "w_attn_out", "w_out",
               "ln2_g", "w_ffn_gate", "w_ffn_up", "w_ffn_down", "lnf_g"]
    outs = [sm["loss"], grad_x[None]]
    for k in range(4):
        outs += [upd[n][k] for n in order_w]
    return tuple(outs)
```

```python
from typing import NamedTuple

import jax
import jax.numpy as jnp
from jax import lax
from jax.experimental import pallas as pl
from jax.experimental.pallas import tpu as pltpu

D_MODEL = 1024
SEQ_CHUNK = 128
RET_CHUNK = 256
RET_CHUNKS_PER_STEP = 4
RET_HEADS = 4
RET_QK_DIM = 128
ATTN_Q_HEADS = 16
ATTN_HEAD_DIM = 64
D_FF = 2816
D_IN = 6400
ROPE_THETA = 10000.0
EPS = 1e-6
N_CHIPS = 4

ADAM_LR = 0.001
ADAM_B1 = 0.9
ADAM_B2 = 0.999
ADAM_EPS = 1e-08
ADAM_WD = 0.01
ADAM_STEP = 10

OFF_RQ, OFF_RK, OFF_RV, OFF_RG = 0, 512, 1024, 2048
OFF_AQ, OFF_AK, OFF_AV, OFF_GA, OFF_GB = 3072, 4096, 4224, 4352, 5376

V7X_VMEM_LIMIT = 56 * 1024 * 1024
LANES = 128

BF = jnp.bfloat16
F32 = jnp.float32
KEPT = jnp.bfloat16
MESH = pl.DeviceIdType.MESH

_VMEM = pl.BlockSpec(memory_space=pltpu.VMEM)
_SMEM = pl.BlockSpec(memory_space=pltpu.SMEM)
_ANY = pl.BlockSpec(memory_space=pl.ANY)


def _dot(a, b):
    return jnp.dot(a, b, preferred_element_type=F32)


def _dot_nt(a, b):
    return lax.dot_general(a, b, (((1,), (1,)), ((), ())), preferred_element_type=F32)


def _dot_tn(a, b):
    return lax.dot_general(a, b, (((0,), (0,)), ((), ())), preferred_element_type=F32)


def _sigmoid(x):
    return 0.5 * jnp.tanh(0.5 * x) + 0.5


def _rows(tm, n):
    return pl.BlockSpec((tm, n), lambda i: (i, 0))


def _fixed(shape):
    nd = len(shape)
    return pl.BlockSpec(shape, lambda i: (0,) * nd)


def _params(sem, **kw):
    return pltpu.CompilerParams(dimension_semantics=sem, vmem_limit_bytes=V7X_VMEM_LIMIT, **kw)


def _rot_ret(p, cr, sr):
    return p * cr + pltpu.roll(p, 64, 1) * sr


def _rot_ret_bwd(d, cr, sr):
    return d * cr - pltpu.roll(d, 64, 1) * sr


def _attn_partner(p):
    lane = lax.broadcasted_iota(jnp.int32, p.shape, 1)
    first = (lane % 64) < 32
    return jnp.where(first, pltpu.roll(p, 96, 1), pltpu.roll(p, 32, 1))


def _rot_attn(p, ca, sa):
    return p * ca + _attn_partner(p) * sa


def _rot_attn_bwd(d, ca, sa):
    return d * ca - _attn_partner(d) * sa


def _proj_fwd(x, g1, w_in, b_in, cr, sr, ca, sa):
    S = x.shape[0]
    tm = 512
    scale = RET_QK_DIM ** -0.5

    def body(x_ref, g_ref, w_ref, b_ref, cr_ref, sr_ref, ca_ref, sa_ref,
             h_ref, rq_ref, rk_ref, rv_ref, rg_ref, aq_ref, ak_ref, av_ref, ga_ref, gb_ref):
        xv = x_ref[...]
        r = lax.rsqrt(jnp.mean(xv * xv, axis=-1, keepdims=True) + EPS)
        h = (xv * r * g_ref[...]).astype(BF)
        h_ref[...] = h

        def piece(lo, n):
            return _dot_nt(h, w_ref[lo:lo + n, :]) + b_ref[:, lo:lo + n]

        crv, srv = cr_ref[...], sr_ref[...]
        pq = piece(OFF_RQ, 512)
        pk = piece(OFF_RK, 512)
        for hd in range(RET_HEADS):
            sl = slice(hd * 128, (hd + 1) * 128)
            rq_ref[:, sl] = (_rot_ret(pq[:, sl], crv, srv) * scale).astype(BF)
            rk_ref[:, sl] = _rot_ret(pk[:, sl], crv, srv)
        rv_ref[...] = piece(OFF_RV, 1024).astype(BF)
        rg_ref[...] = piece(OFF_RG, 1024).astype(KEPT)
        cav, sav = ca_ref[...], sa_ref[...]
        pa = piece(OFF_AQ, 1024)
        for blk in range(8):
            sl = slice(blk * 128, (blk + 1) * 128)
            aq_ref[:, sl] = _rot_attn(pa[:, sl], cav, sav).astype(BF)
        pkv = piece(OFF_AK, 256)
        ak_ref[...] = _rot_attn(pkv[:, :128], cav, sav).astype(BF)
        av_ref[...] = pkv[:, 128:].astype(BF)
        ga_ref[...] = piece(OFF_GA, 1024).astype(KEPT)
        gb_ref[...] = piece(OFF_GB, 1024).astype(KEPT)

    outs = [(D_MODEL, BF), (512, BF), (512, F32), (1024, BF), (1024, KEPT),
            (1024, BF), (128, BF), (128, BF), (1024, KEPT), (1024, KEPT)]
    return pl.pallas_call(
        body, name="proj_fwd", grid=(S // tm,),
        in_specs=[_rows(tm, D_MODEL), _fixed((1, D_MODEL)), _VMEM, _VMEM,
                  _rows(tm, 128), _rows(tm, 128), _rows(tm, 128), _rows(tm, 128)],
        out_specs=[_rows(tm, n) for n, _ in outs],
        out_shape=[jax.ShapeDtypeStruct((S, n), dt) for n, dt in outs],
        compiler_params=_params(("parallel",)),
    )(x, g1, w_in, b_in, cr, sr, ca, sa)


def _ret_fwd(rq, rk, rv, dmat, qd, kd, cd):
    S = rq.shape[0]
    N = S // RET_CHUNK
    cps = RET_CHUNKS_PER_STEP
    rows = cps * RET_CHUNK

    def body(q_ref, k_ref, v_ref, dm_ref, qd_ref, kd_ref, cd_ref, y_ref, st_ref, state):
        @pl.when(pl.program_id(0) == 0)
        def _():
            state[...] = jnp.zeros_like(state)

        for hd in range(RET_HEADS):
            qs = slice(hd * 128, (hd + 1) * 128)
            vs = slice(hd * 256, (hd + 1) * 256)
            free = []
            for ch in range(cps):
                rs = slice(ch * RET_CHUNK, (ch + 1) * RET_CHUNK)
                q = q_ref[rs, qs]
                kf = k_ref[rs, qs]
                v = v_ref[rs, vs]
                a = _dot_nt(q, kf.astype(BF)) * dm_ref[hd]
                kdv = (kf * kd_ref[hd]).astype(BF)
                free.append((rs, q, _dot(a.astype(BF), v), _dot_tn(kdv, v)))
            s = state[hd]
            for ch, (rs, q, inner, update) in enumerate(free):
                sb = s.astype(BF)
                st_ref[ch, hd] = sb
                y_ref[rs, vs] = (inner + _dot(q, sb) * qd_ref[hd]).astype(KEPT)
                s = s * cd_ref[hd] + update
            state[hd] = s

    return pl.pallas_call(
        body, name="ret_fwd", grid=(N // cps,),
        in_specs=[_rows(rows, 512), _rows(rows, 512), _rows(rows, 1024), _VMEM, _VMEM, _VMEM, _VMEM],
        out_specs=[_rows(rows, 1024), pl.BlockSpec((cps, RET_HEADS, 128, 256), lambda i: (i, 0, 0, 0))],
        out_shape=[jax.ShapeDtypeStruct((S, 1024), KEPT),
                   jax.ShapeDtypeStruct((N, RET_HEADS, 128, 256), BF)],
        scratch_shapes=[pltpu.VMEM((RET_HEADS, 128, 256), F32)],
        compiler_params=_params(("arbitrary",)),
    )(rq, rk, rv, dmat, qd, kd, cd)


def _kv_halves(prev_ref, cur_ref):
    kk = jnp.concatenate([prev_ref[...], cur_ref[...]], axis=0).astype(F32)
    return kk, pltpu.roll(kk, 64, 1)


ATTN_FWD_BLOCKS_PER_STEP = 4
PAIRS = 4


def _attn_fold(n):
    t = lax.broadcasted_iota(jnp.int32, (128, PAIRS * 128), 0)
    qi = lax.broadcasted_iota(jnp.int32, (128, PAIRS * 128), 1) & 127
    cur = t <= qi
    cur_f = jnp.where(cur, 1.0, 0.0)
    bias = jnp.where(jnp.logical_or(cur, n > 0), 0.0, -1e30)
    return cur, cur_f, 1.0 - cur_f, bias


def _group_operands(g, kk, kk_sw, vv, vv_sw):
    lo = lax.broadcasted_iota(jnp.int32, (256, 128), 1) < 64
    k_a, k_b = (kk, kk_sw) if g == 0 else (kk_sw, kk)
    v_a, v_b = (vv, vv_sw) if g == 0 else (vv_sw, vv)
    return (jnp.concatenate([jnp.where(lo, k_a, 0.0), jnp.where(lo, 0.0, k_b)], axis=0).astype(BF),
            jnp.concatenate([jnp.where(lo, v_a, 0.0), jnp.where(lo, 0.0, v_b)], axis=0).astype(BF))


def _stack_pairs(ref, g, rows=slice(None)):
    return jnp.concatenate([ref[rows, (4 * g + p) * 128:(4 * g + p + 1) * 128] for p in range(PAIRS)], axis=0)


def _sink_row(sk_ref, g, which):
    return jnp.concatenate([jnp.full((1, 128), sk_ref[0, 8 * g + 2 * p + which], F32) for p in range(PAIRS)], axis=1)


def _fold(full, cur):
    return jnp.where(cur, full[128:], full[:128])


def _attn_probs(raw, fold, sink):
    s = _fold(raw, fold[0]) * (ATTN_HEAD_DIM ** -0.5) + fold[3]
    m = jnp.maximum(jnp.max(s, axis=0, keepdims=True), sink)
    e = jnp.exp(s - m)
    es = jnp.exp(sink - m)
    inv = 1.0 / (jnp.sum(e, axis=0, keepdims=True) + es)
    return e * inv, es * inv


def _unfold(p, fold):
    return jnp.concatenate([p * fold[2], p * fold[1]], axis=0).astype(BF)


def _attn_fwd(aq, ak, av, sinks):
    S = aq.shape[0]
    N = S // SEQ_CHUNK
    nb = ATTN_FWD_BLOCKS_PER_STEP

    def body(sk_ref, q_ref, kp_ref, kc_ref, vp_ref, vc_ref, o_ref):
        kall, kall_sw = _kv_halves(kp_ref, kc_ref)
        vall, vall_sw = _kv_halves(vp_ref, vc_ref)
        blocks = []
        for sb in range(nb):
            kv = slice(sb * 128, sb * 128 + 256)
            rows = slice(sb * 128, (sb + 1) * 128)
            ops = [_group_operands(g, kall[kv], kall_sw[kv], vall[kv], vall_sw[kv]) for g in range(2)]
            raw = [_dot_nt(ops[g][0], _stack_pairs(q_ref, g, rows)) for g in range(2)]
            blocks.append((rows, ops, raw, _attn_fold(nb * pl.program_id(0) + sb)))
        for rows, ops, raw, fold in blocks:
            for g in range(2):
                probs = [_unfold(_attn_probs(raw[g][w * 256:(w + 1) * 256], fold, _sink_row(sk_ref, g, w))[0], fold)
                         for w in range(2)]
                o = _dot_tn(jnp.concatenate(probs, axis=0), ops[g][1])
                for p in range(PAIRS):
                    o_ref[rows, (4 * g + p) * 128:(4 * g + p + 1) * 128] = o[p * 128:(p + 1) * 128].astype(BF)

    prev = lambda i: (jnp.maximum(nb * i - 1, 0), 0)
    return pl.pallas_call(
        body, name="attn_fwd", grid=(N // nb,),
        in_specs=[_SMEM, _rows(nb * 128, 1024), pl.BlockSpec((128, 128), prev), _rows(nb * 128, 128),
                  pl.BlockSpec((128, 128), prev), _rows(nb * 128, 128)],
        out_specs=_rows(nb * 128, 1024),
        out_shape=jax.ShapeDtypeStruct((S, 1024), BF),
        compiler_params=_params(("parallel",)),
    )(sinks, aq, ak, ak, av, av)


def _group_norm(y):
    mu = jnp.mean(y, axis=-1, keepdims=True)
    yc = y - mu
    rstd = lax.rsqrt(jnp.mean(yc * yc, axis=-1, keepdims=True) + EPS)
    return yc * rstd, rstd


def _merge_fwd(x, ry, rg, ay, ga, gb, g_ret, w_ret, w_attn, w_out):
    S = x.shape[0]
    tm = 512

    def body(x_ref, ry_ref, rg_ref, ay_ref, ga_ref, gb_ref, gr_ref, wr_ref, wa_ref, wo_ref,
             u_ref, mg_ref, x1_ref):
        bb = _dot(ay_ref[...], wa_ref[...])
        for hd in range(RET_HEADS):
            vs = slice(hd * 256, (hd + 1) * 256)
            yn, _ = _group_norm(ry_ref[:, vs].astype(F32))
            rgv = rg_ref[:, vs].astype(F32)
            u_ref[:, vs] = (rgv * _sigmoid(rgv) * (yn * gr_ref[:, vs])).astype(BF)
        gated_b = _sigmoid(gb_ref[...].astype(F32)) * bb
        sa = _sigmoid(ga_ref[...].astype(F32))
        ba = _dot(u_ref[...], wr_ref[...])
        merged = (sa * ba + gated_b).astype(BF)
        mg_ref[...] = merged
        x1_ref[...] = x_ref[...] + _dot(merged, wo_ref[...])

    outs = [BF, BF, F32]
    return pl.pallas_call(
        body, name="merge_fwd", grid=(S // tm,),
        in_specs=[_rows(tm, 1024)] * 6 + [_fixed((1, 1024)), _VMEM, _VMEM, _VMEM],
        out_specs=[_rows(tm, 1024)] * 3,
        out_shape=[jax.ShapeDtypeStruct((S, 1024), dt) for dt in outs],
        compiler_params=_params(("parallel",)),
    )(x, ry, rg, ay, ga, gb, g_ret, w_ret, w_attn, w_out)


def _ffn_fwd(x1, tgt, g2, g3, w_gate, w_up, w_down):
    S = x1.shape[0]
    tm = 256

    def body(x1_ref, t_ref, g2_ref, g3_ref, wg_ref, wu_ref, wd_ref,
             gate_ref, up_ref, h2_ref, act_ref, dx2_ref, loss_ref, dg3_ref):
        @pl.when(pl.program_id(0) == 0)
        def _():
            loss_ref[...] = jnp.zeros_like(loss_ref)
            dg3_ref[...] = jnp.zeros_like(dg3_ref)

        x1v = x1_ref[...]
        r2 = lax.rsqrt(jnp.mean(x1v * x1v, axis=-1, keepdims=True) + EPS)
        h2 = (x1v * r2 * g2_ref[...]).astype(BF)
        h2_ref[...] = h2
        gate = _dot_nt(h2, wg_ref[...])
        up = _dot_nt(h2, wu_ref[...])
        gate_ref[...] = gate
        up_ref[...] = up
        act = (gate * _sigmoid(gate) * up).astype(BF)
        act_ref[...] = act
        x2 = x1v + _dot(act, wd_ref[...])
        r3 = lax.rsqrt(jnp.mean(x2 * x2, axis=-1, keepdims=True) + EPS)
        xh = x2 * r3
        g3v = g3_ref[...]
        err = xh * g3v - t_ref[...]
        loss_ref[...] += 0.5 * jnp.sum(jnp.mean(err * err, axis=-1, keepdims=True), axis=0, keepdims=True)
        dy = err * (1.0 / D_MODEL)
        dg3_ref[...] += jnp.sum(dy * xh, axis=0, keepdims=True)
        dxh = dy * g3v
        dx2_ref[...] = r3 * (dxh - xh * jnp.mean(dxh * xh, axis=-1, keepdims=True))

    return pl.pallas_call(
        body, name="ffn_fwd", grid=(S // tm,),
        in_specs=[_rows(tm, 1024), _rows(tm, 1024), _fixed((1, 1024)), _fixed((1, 1024)), _VMEM, _VMEM, _VMEM],
        out_specs=[_rows(tm, D_FF), _rows(tm, D_FF), _rows(tm, 1024), _rows(tm, D_FF), _rows(tm, 1024),
                   _fixed((1, LANES)), _fixed((1, 1024))],
        out_shape=[jax.ShapeDtypeStruct((S, D_FF), F32), jax.ShapeDtypeStruct((S, D_FF), F32),
                   jax.ShapeDtypeStruct((S, 1024), BF), jax.ShapeDtypeStruct((S, D_FF), BF),
                   jax.ShapeDtypeStruct((S, 1024), F32),
                   jax.ShapeDtypeStruct((1, LANES), F32), jax.ShapeDtypeStruct((1, 1024), F32)],
        compiler_params=_params(("arbitrary",)),
    )(x1, tgt, g2, g3, w_gate, w_up, w_down)


def _ffn_bwd(dx2, gate, up, x1, g2, w_gate, w_up, w_down):
    S = x1.shape[0]
    tm = 256

    def body(dx2_ref, gate_ref, up_ref, x1_ref, g2_ref, wg_ref, wu_ref, wd_ref,
             dgate_ref, dup_ref, dx1_ref, dg2_ref):
        @pl.when(pl.program_id(0) == 0)
        def _():
            dg2_ref[...] = jnp.zeros_like(dg2_ref)

        dx2v = dx2_ref[...]
        dact = _dot_nt(dx2v.astype(BF), wd_ref[...])
        gate = gate_ref[...]
        sg = _sigmoid(gate)
        dgate = (dact * up_ref[...] * (sg * (1.0 + gate * (1.0 - sg)))).astype(BF)
        dup = (dact * (gate * sg)).astype(BF)
        dgate_ref[...] = dgate
        dup_ref[...] = dup
        dh2 = _dot(dgate, wg_ref[...]) + _dot(dup, wu_ref[...])
        x1v = x1_ref[...]
        r2 = lax.rsqrt(jnp.mean(x1v * x1v, axis=-1, keepdims=True) + EPS)
        xh = x1v * r2
        dg2_ref[...] += jnp.sum(dh2 * xh, axis=0, keepdims=True)
        dxh = dh2 * g2_ref[...]
        dx1_ref[...] = dx2v + r2 * (dxh - xh * jnp.mean(dxh * xh, axis=-1, keepdims=True))

    return pl.pallas_call(
        body, name="ffn_bwd", grid=(S // tm,),
        in_specs=[_rows(tm, 1024), _rows(tm, D_FF), _rows(tm, D_FF), _rows(tm, 1024), _fixed((1, 1024)),
                  _VMEM, _VMEM, _VMEM],
        out_specs=[_rows(tm, D_FF), _rows(tm, D_FF), _rows(tm, 1024), _fixed((1, 1024))],
        out_shape=[jax.ShapeDtypeStruct((S, D_FF), BF), jax.ShapeDtypeStruct((S, D_FF), BF),
                   jax.ShapeDtypeStruct((S, 1024), F32), jax.ShapeDtypeStruct((1, 1024), F32)],
        compiler_params=_params(("arbitrary",)),
    )(dx2, gate, up, x1, g2, w_gate, w_up, w_down)


def _merge_bwd(dx1, ga, gb, u, ay, rg, ry, g_ret, w_ret, w_attn, w_out):
    S = dx1.shape[0]
    tm = 512

    def body(dx1_ref, ga_ref, gb_ref, u_ref, ay_ref, rg_ref, ry_ref, gr_ref, wr_ref, wa_ref, wo_ref,
             dgates_ref, drg_ref, dba_ref, dbb_ref, day_ref, dry_ref, dgr_ref):
        @pl.when(pl.program_id(0) == 0)
        def _():
            dgr_ref[...] = jnp.zeros_like(dgr_ref)

        dm = _dot_nt(dx1_ref[...].astype(BF), wo_ref[...])
        ba = _dot(u_ref[...], wr_ref[...])
        bb = _dot(ay_ref[...], wa_ref[...])
        sa = _sigmoid(ga_ref[...].astype(F32))
        sb = _sigmoid(gb_ref[...].astype(F32))
        dga = (dm * ba * (sa * (1.0 - sa))).astype(BF)
        dgb = (dm * bb * (sb * (1.0 - sb))).astype(BF)
        dgates_ref[:, :1024] = dga
        dgates_ref[:, 1024:] = dgb
        dba = (dm * sa).astype(BF)
        dbb = (dm * sb).astype(BF)
        dba_ref[...] = dba
        dbb_ref[...] = dbb
        day_ref[...] = _dot_nt(dbb, wa_ref[...]).astype(BF)
        du = _dot_nt(dba, wr_ref[...])
        for hd in range(RET_HEADS):
            vs = slice(hd * 256, (hd + 1) * 256)
            yn, rstd = _group_norm(ry_ref[:, vs].astype(F32))
            rgv = rg_ref[:, vs].astype(F32)
            sr = _sigmoid(rgv)
            silu = rgv * sr
            duv = du[:, vs]
            grv = gr_ref[:, vs]
            drg = (duv * (yn * grv) * (sr * (1.0 + rgv * (1.0 - sr)))).astype(BF)
            drg_ref[:, vs] = drg
            dyg = duv * silu
            dgr_ref[:, vs] += jnp.sum(dyg * yn, axis=0, keepdims=True)
            dyn = dyg * grv
            dry_ref[:, vs] = (rstd * (dyn - jnp.mean(dyn, axis=-1, keepdims=True)
                                      - yn * jnp.mean(dyn * yn, axis=-1, keepdims=True))).astype(KEPT)

    return pl.pallas_call(
        body, name="merge_bwd", grid=(S // tm,),
        in_specs=[_rows(tm, 1024)] * 7 + [_fixed((1, 1024)), _VMEM, _VMEM, _VMEM],
        out_specs=[_rows(tm, 2048), _rows(tm, 1024), _rows(tm, 1024), _rows(tm, 1024), _rows(tm, 1024),
                   _rows(tm, 1024), _fixed((1, 1024))],
        out_shape=[jax.ShapeDtypeStruct((S, 2048), BF), jax.ShapeDtypeStruct((S, 1024), BF),
                   jax.ShapeDtypeStruct((S, 1024), BF), jax.ShapeDtypeStruct((S, 1024), BF),
                   jax.ShapeDtypeStruct((S, 1024), BF), jax.ShapeDtypeStruct((S, 1024), KEPT),
                   jax.ShapeDtypeStruct((1, 1024), F32)],
        compiler_params=_params(("arbitrary",)),
    )(dx1, ga, gb, u, ay, rg, ry, g_ret, w_ret, w_attn, w_out)


def _attn_bwd(aq, ak, av, day, sinks, ca, sa, after):
    S = aq.shape[0]
    N = S // SEQ_CHUNK
    steps = N // 2

    def body(sk_ref, q_ref, kp_ref, kc_ref, vp_ref, vc_ref, do_ref, cq_ref, sq_ref, cp_ref, sp_ref, after_ref,
             dq_ref, dke_ref, dve_ref, dko_ref, dvo_ref, dsk_ref, carry_k, carry_v):
        i = pl.program_id(0)

        @pl.when(i == 0)
        def _():
            carry_k[...] = jnp.zeros_like(carry_k)
            carry_v[...] = jnp.zeros_like(carry_v)
            dsk_ref[...] = jnp.zeros_like(dsk_ref)

        @pl.when(i < steps)
        def _():
            kall, kall_sw = _kv_halves(kp_ref, kc_ref)
            vall, vall_sw = _kv_halves(vp_ref, vc_ref)
            lo = lax.broadcasted_iota(jnp.int32, (256, 128), 1) < 64
            blocks = []
            for sb in range(2):
                kv = slice(sb * 128, sb * 128 + 256)
                rows = slice(sb * 128, (sb + 1) * 128)
                ops = [_group_operands(g, kall[kv], kall_sw[kv], vall[kv], vall_sw[kv]) for g in range(2)]
                stacked = [(_stack_pairs(q_ref, g, rows), _stack_pairs(do_ref, g, rows)) for g in range(2)]
                raw = [_dot_nt(ops[g][0], stacked[g][0]) for g in range(2)]
                raw_dp = [_dot_nt(ops[g][1], stacked[g][1]) for g in range(2)]
                blocks.append((rows, ops, stacked, raw, raw_dp, _attn_fold(2 * i + sb)))
            grads = []
            d_sink = [jnp.zeros((1, PAIRS * 128), F32) for _ in range(4)]
            for rows, ops, stacked, raw, raw_dp, fold in blocks:
                cur = fold[0]
                cq, sq = cq_ref[rows, :], sq_ref[rows, :]
                dk_groups, dv_groups = [], []
                for g in range(2):
                    qs, dos = stacked[g]
                    ds_parts, p_parts = [], []
                    for which in range(2):
                        half = slice(which * 256, (which + 1) * 256)
                        pr, ps = _attn_probs(raw[g][half], fold, _sink_row(sk_ref, g, which))
                        dp = _fold(raw_dp[g][half], cur)
                        dr = jnp.sum(pr * dp, axis=0, keepdims=True)
                        ds_parts.append(_unfold(pr * (dp - dr) * (ATTN_HEAD_DIM ** -0.5), fold))
                        p_parts.append(_unfold(pr, fold))
                        d_sink[2 * g + which] = d_sink[2 * g + which] - ps * dr
                    ds = jnp.concatenate(ds_parts, axis=0)
                    dq = _dot_tn(ds, ops[g][0])
                    dk_both = _dot(ds, qs)
                    dv_both = _dot(jnp.concatenate(p_parts, axis=0), dos)
                    for p in range(PAIRS):
                        sl = slice((4 * g + p) * 128, (4 * g + p + 1) * 128)
                        dq_ref[rows, sl] = _rot_attn_bwd(dq[p * 128:(p + 1) * 128], cq, sq).astype(BF)
                    tk = jnp.where(lo, dk_both[:256], dk_both[256:])
                    tv = jnp.where(lo, dv_both[:256], dv_both[256:])
                    dk_groups.append(tk + pltpu.roll(tk, 64, 1))
                    dv_groups.append(tv + pltpu.roll(tv, 64, 1))
                grads.append((jnp.where(lo, dk_groups[0], dk_groups[1]), jnp.where(lo, dv_groups[0], dv_groups[1])))
            dsk_ref[...] += jnp.concatenate(d_sink, axis=0)
            (dk_a, dv_a), (dk_b, dv_b) = grads
            dko_ref[...] = _rot_attn_bwd(carry_k[...] + dk_a[:128], cp_ref[...], sp_ref[...]).astype(BF)
            dvo_ref[...] = (carry_v[...] + dv_a[:128]).astype(BF)
            dke_ref[...] = _rot_attn_bwd(dk_a[128:] + dk_b[:128], cq_ref[0:128, :], sq_ref[0:128, :]).astype(BF)
            dve_ref[...] = (dv_a[128:] + dv_b[:128]).astype(BF)
            carry_k[...] = dk_b[128:]
            carry_v[...] = dv_b[128:]

        @pl.when(i == steps)
        def _():
            dko_ref[...] = _rot_attn_bwd(carry_k[...], cp_ref[...], sp_ref[...]).astype(BF)
            dvo_ref[...] = carry_v[...].astype(BF)

    two = lambda i: (jnp.minimum(i, steps - 1), 0)
    before = lambda i: (jnp.maximum(2 * i - 1, 0), 0)
    odd = lambda i: (jnp.maximum(i - 1, 0), 0)
    blk = lambda r, w, im: pl.BlockSpec((r, w), im)
    half = jax.ShapeDtypeStruct((S // 2, 128), BF)
    dq, dk_even, dv_even, dk_odd, dv_odd, dsk = pl.pallas_call(
        body, name="attn_bwd", grid=(steps + 1,),
        in_specs=[_SMEM, blk(256, 1024, two), blk(128, 128, before), blk(256, 128, two), blk(128, 128, before),
                  blk(256, 128, two), blk(256, 1024, two), blk(256, 128, two), blk(256, 128, two),
                  blk(128, 128, before), blk(128, 128, before), _ANY],
        out_specs=[blk(256, 1024, two), blk(128, 128, two), blk(128, 128, two), blk(128, 128, odd),
                   blk(128, 128, odd), _fixed((4, PAIRS * 128))],
        out_shape=[jax.ShapeDtypeStruct((S, 1024), BF), half, half, half, half,
                   jax.ShapeDtypeStruct((4, PAIRS * 128), F32)],
        scratch_shapes=[pltpu.VMEM((128, 128), F32), pltpu.VMEM((128, 128), F32)],
        compiler_params=_params(("arbitrary",)),
    )(sinks, aq, ak, ak, av, av, day, ca, sa, ca, sa, after)

    def interleave(even, odd_):
        return jnp.stack([even.reshape(steps, 128, 128), odd_.reshape(steps, 128, 128)], axis=1).reshape(S, 128)

    return dq, interleave(dk_even, dk_odd), interleave(dv_even, dv_odd), dsk


def _ret_bwd(rq, rk, rv, dry, states, dmat, qd, kd, cd, cr, sr, after):
    S = rq.shape[0]
    N = S // RET_CHUNK
    scale = RET_QK_DIM ** -0.5
    cps = RET_CHUNKS_PER_STEP
    steps = N // cps

    def body(q_ref, k_ref, v_ref, dy_ref, st_ref, dm_ref, qd_ref, kd_ref, cd_ref, cr_ref, sr_ref, after_ref,
             out_ref, dstate):
        @pl.when(pl.program_id(0) == 0)
        def _():
            dstate[...] = jnp.zeros_like(dstate)

        for hd in range(RET_HEADS):
            qs = slice(hd * 128, (hd + 1) * 128)
            vs = slice(hd * 256, (hd + 1) * 256)
            dmv = dm_ref[hd]
            free = []
            for ch in range(cps):
                rs = slice(ch * RET_CHUNK, (ch + 1) * RET_CHUNK)
                crv, srv = cr_ref[rs, :], sr_ref[rs, :]
                q = q_ref[rs, qs]
                kf = k_ref[rs, qs]
                k = kf.astype(BF)
                v = v_ref[rs, vs]
                do = dy_ref[rs, vs].astype(F32)
                dob = do.astype(BF)
                doq = (do * qd_ref[hd]).astype(BF)
                a = (_dot_nt(q, k) * dmv).astype(BF)
                da = (_dot_nt(dob, v) * dmv).astype(BF)
                dq = _dot(da, k) + _dot_nt(doq, st_ref[ch, hd])
                out_ref[rs, qs] = _rot_ret_bwd(dq * scale, crv, srv).astype(BF)
                free.append((rs, crv, srv, v, (kf * kd_ref[hd]).astype(BF),
                             _dot_tn(a, dob), _dot_tn(da, q), _dot_tn(q, doq)))
            ds = dstate[hd]
            for rs, crv, srv, v, kdv, dv_free, dk_free, ds_update in reversed(free):
                dsb = ds.astype(BF)
                dv = dv_free + _dot(kdv, dsb)
                dk = dk_free + _dot_nt(v, dsb) * kd_ref[hd]
                ds = ds * cd_ref[hd] + ds_update
                out_ref[rs, 512 + hd * 128:512 + (hd + 1) * 128] = _rot_ret_bwd(dk, crv, srv).astype(BF)
                out_ref[rs, 1024 + hd * 256:1024 + (hd + 1) * 256] = dv.astype(BF)
            dstate[hd] = ds

    rev = lambda w: pl.BlockSpec((cps * RET_CHUNK, w), lambda i: (steps - 1 - i, 0))
    return pl.pallas_call(
        body, name="ret_bwd", grid=(steps,),
        in_specs=[rev(512), rev(512), rev(1024), rev(1024),
                  pl.BlockSpec((cps, RET_HEADS, 128, 256), lambda i: (steps - 1 - i, 0, 0, 0)),
                  _VMEM, _VMEM, _VMEM, _VMEM, rev(128), rev(128), _ANY],
        out_specs=rev(2048),
        out_shape=jax.ShapeDtypeStruct((S, 2048), BF),
        scratch_shapes=[pltpu.VMEM((RET_HEADS, 128, 256), F32)],
        compiler_params=_params(("arbitrary",)),
    )(rq, rk, rv, dry, states, dmat, qd, kd, cd, cr, sr, after)


def _proj_bwd(d_pieces, x, dx1, g1, w_in_t):
    S = x.shape[0]
    tm = 512
    widths = [d.shape[1] for d in d_pieces]
    assert sum(widths) == D_IN
    nd = len(d_pieces)

    def body(*refs):
        d_refs = refs[:nd]
        x_ref, dx1_ref, g_ref, w_ref, gx_ref, dg_ref, db_ref = refs[nd:]

        @pl.when(pl.program_id(0) == 0)
        def _():
            dg_ref[...] = jnp.zeros_like(dg_ref)
            db_ref[...] = jnp.zeros_like(db_ref)

        d_all = jnp.concatenate([ref[...] for ref in d_refs], axis=1)
        dh = _dot(d_all, w_ref[...])
        db_ref[...] += jnp.sum(d_all.astype(F32), axis=0, keepdims=True)
        xv = x_ref[...]
        r = lax.rsqrt(jnp.mean(xv * xv, axis=-1, keepdims=True) + EPS)
        xh = xv * r
        dg_ref[...] += jnp.sum(dh * xh, axis=0, keepdims=True)
        dxh = dh * g_ref[...]
        gx_ref[...] = dx1_ref[...] + r * (dxh - xh * jnp.mean(dxh * xh, axis=-1, keepdims=True))

    return pl.pallas_call(
        body, name="proj_bwd", grid=(S // tm,),
        in_specs=[_rows(tm, n) for n in widths] + [_rows(tm, 1024), _rows(tm, 1024), _fixed((1, 1024)), _VMEM],
        out_specs=[_rows(tm, 1024), _fixed((1, 1024)), _fixed((1, D_IN))],
        out_shape=[jax.ShapeDtypeStruct((S, 1024), F32), jax.ShapeDtypeStruct((1, 1024), F32),
                   jax.ShapeDtypeStruct((1, D_IN), F32)],
        compiler_params=_params(("arbitrary",)),
    )(*d_pieces, x, dx1, g1, w_in_t)


SHARD_ROWS = (1600, 256, 256, 256, 704, 704, 704)


class Group(NamedTuple):
    name: str
    tensors: tuple
    offs: tuple
    block: int
    tile: int

    @property
    def half(self):
        return self.block // 2

    def off(self, t):
        return self.offs[self.tensors.index(t)]


LATE = Group("late", (0,), (0,), 1600, 800)
EARLY = Group("early", (4, 5, 6, 1, 2, 3), (0, 704, 1408, 2112, 2368, 2624), 2880, 480)
GROUP_OF = {t: g for g in (LATE, EARLY) for t in g.tensors}


def _wgrad(a_list, b, t, packed, name):
    S = b.shape[0]
    widths = [a.shape[1] for a in a_list]
    M = sum(widths)
    ts = 1024 if M <= 4096 else 512
    steps = S // ts
    na = len(a_list)
    grp = GROUP_OF[t]
    r, off = SHARD_ROWS[t], grp.off(t)
    assert M == N_CHIPS * r
    segments = [(chip * r, r, chip * grp.block + off) for chip in range(N_CHIPS)]
    stage_rows = r
    n_in = na + 1 + (2 if packed is not None else 0)

    def body(*refs):
        a_refs, b_ref = refs[:na], refs[na]
        g32_ref, gb_ref, acc, stage, sems = refs[n_in:]
        i = pl.program_id(0)

        @pl.when(i == 0)
        def _():
            acc[...] = jnp.zeros_like(acc)

        bv = b_ref[...].astype(BF)
        lo = 0
        for a_ref, n in zip(a_refs, widths):
            acc[lo:lo + n, :] += _dot_tn(a_ref[...].astype(BF), bv)
            lo += n

        @pl.when(i == steps - 1)
        def _():
            for a0, n, p0 in segments:
                c32 = pltpu.make_async_copy(acc.at[a0:a0 + n, :], g32_ref.at[pl.ds(p0, n), :], sems.at[0])
                c32.start()
                stage[0:n, :] = acc[a0:a0 + n, :].astype(BF)
                c16 = pltpu.make_async_copy(stage.at[0:n, :], gb_ref.at[pl.ds(p0, n), :], sems.at[1])
                c16.start()
                c32.wait()
                c16.wait()

    shapes = [jax.ShapeDtypeStruct((N_CHIPS * grp.block, 1024), F32),
              jax.ShapeDtypeStruct((N_CHIPS * grp.block, 1024), BF)]
    operands = list(a_list) + [b]
    in_specs = [_rows(ts, n) for n in widths] + [_rows(ts, 1024)]
    aliases = {}
    if packed is not None:
        operands += list(packed)
        in_specs += [_ANY, _ANY]
        aliases = {na + 1: 0, na + 2: 1}
    return pl.pallas_call(
        body, name=name, grid=(steps,),
        in_specs=in_specs, out_specs=[_ANY, _ANY], out_shape=shapes,
        scratch_shapes=[pltpu.VMEM((M, 1024), F32), pltpu.VMEM((stage_rows, 1024), BF),
                        pltpu.SemaphoreType.DMA((2,))],
        input_output_aliases=aliases,
        compiler_params=_params(("arbitrary",)),
    )(*operands)


def _stage_weights(place, ws, name, after=()):
    nw = len(ws)

    def body(p_ref, *refs):
        for w_ref, o_ref in zip(refs[:nw], refs[nw + len(after):]):
            o_ref[...] = w_ref[...].astype(BF)

    return pl.pallas_call(
        body, name=name,
        grid_spec=pltpu.PrefetchScalarGridSpec(
            num_scalar_prefetch=1, grid=(1,),
            in_specs=[pl.BlockSpec(w.shape, lambda i, p: (0, 0)) for w in ws] + [_ANY] * len(after),
            out_specs=[pl.BlockSpec(w.shape, lambda i, p: (p[0], 0)) for w in ws]),
        out_shape=[jax.ShapeDtypeStruct((N_CHIPS * w.shape[0], 1024), BF) for w in ws],
        compiler_params=_params(("arbitrary",)),
    )(place, *ws, *after)


N_PEERS = 7


def _sum_all(place, g32, others, grp):
    nb = grp.half // grp.tile

    def body(p_ref, g_ref, *refs):
        acc = g_ref[...]
        for r_ref in refs[:N_PEERS]:
            acc = acc + r_ref[...].astype(F32)
        refs[-1][...] = acc

    blk = lambda f: pl.BlockSpec((grp.tile, 1024), f)
    slot = lambda k: blk(lambda i, p: (k * nb + i, 0))
    return pl.pallas_call(
        body, name="sum_all_" + grp.name,
        grid_spec=pltpu.PrefetchScalarGridSpec(
            num_scalar_prefetch=1, grid=(nb,),
            in_specs=[blk(lambda i, p: ((2 * p[0] + p[1]) * nb + i, 0))] + [slot(k) for k in range(N_PEERS)],
            out_specs=blk(lambda i, p: (p[1] * nb + i, 0))),
        out_shape=jax.ShapeDtypeStruct((grp.block, 1024), F32),
        compiler_params=_params(("arbitrary",)),
    )(place, g32, *[others] * N_PEERS)


def _adamw_math(w, gv, m, v):
    mn = ADAM_B1 * m + (1.0 - ADAM_B1) * gv
    vn = ADAM_B2 * v + (1.0 - ADAM_B2) * (gv * gv)
    m_hat = mn / (1.0 - ADAM_B1 ** ADAM_STEP)
    v_hat = vn / (1.0 - ADAM_B2 ** ADAM_STEP)
    return -ADAM_LR * (m_hat / (jnp.sqrt(v_hat) + ADAM_EPS) + ADAM_WD * w), mn, vn


def _adamw(w, g, m, v, name, g_row0=0, after=None):
    r = w.shape[0]
    tr = next((c for c in (352, 320, 64) if r % c == 0 and g_row0 % c == 0), r)
    assert g_row0 % tr == 0
    g0 = g_row0 // tr
    extra = [] if after is None else [after]

    def body(w_ref, g_ref, m_ref, v_ref, *rest):
        go_ref, d_ref, mo_ref, vo_ref = rest[len(extra):]
        gv = g_ref[...]
        go_ref[...] = gv
        d_ref[...], mo_ref[...], vo_ref[...] = _adamw_math(w_ref[...], gv, m_ref[...], v_ref[...])

    return pl.pallas_call(
        body, name=name, grid=(r // tr,),
        in_specs=[_rows(tr, 1024), pl.BlockSpec((tr, 1024), lambda i: (g0 + i, 0)), _rows(tr, 1024), _rows(tr, 1024)]
        + [_ANY] * len(extra),
        out_specs=[_rows(tr, 1024)] * 4,
        out_shape=[jax.ShapeDtypeStruct((r, 1024), F32)] * 4,
        compiler_params=_params(("parallel",)),
    )(w, g, m, v, *extra)


def _place():
    x, y, c = lax.axis_index("x"), lax.axis_index("y"), lax.axis_index("c")
    chips = [(1 - x, y), (x, 1 - y), (1 - x, 1 - y)]
    return x, y, c, chips


_HBM =pl.BlockSpec(memory_space=pltpu.HBM)
_SEM = pl.BlockSpec(memory_space=pltpu.SEMAPHORE)
_DATAFLOW = pltpu.SideEffectType.DATAFLOW_SIDE_EFFECTING


def _exchange(name, arrays, n, plan):
    na = len(arrays)

    def body(*refs):
        pairs = plan(refs[na:2 * na], refs[2 * na], refs[2 * na + 1])
        for send, _ in pairs:
            send.start()
        for send, recv in pairs:
            send.wait_send()
            recv.wait_recv()

    return pl.pallas_call(
        body, name=name, in_specs=[_ANY] * na, out_specs=[_ANY] * na,
        out_shape=[jax.ShapeDtypeStruct(a.shape, a.dtype) for a in arrays],
        scratch_shapes=[pltpu.SemaphoreType.DMA((n,)), pltpu.SemaphoreType.DMA((n,))],
        input_output_aliases={i: i for i in range(na)},
    )(*arrays)


def _exchange_start(name, arrays, n, plan, after=()):
    na = len(arrays)

    def body(*refs):
        for send, _ in plan(refs[:na], refs[na + len(after)], refs[na + len(after) + 1]):
            send.start()
        refs[-1][...] = jnp.zeros_like(refs[-1])

    out = pl.pallas_call(
        body, name=name,
        out_shape=(pltpu.SemaphoreType.DMA((n,)), pltpu.SemaphoreType.DMA((n,)),
                   *[pltpu.HBM(a.shape, a.dtype) for a in arrays], jax.ShapeDtypeStruct((8, LANES), F32)),
        in_specs=[_HBM] * na + [_ANY] * len(after), out_specs=(_SEM, _SEM, *[_HBM] * na, _VMEM),
        input_output_aliases={i: 2 + i for i in range(na)},
        compiler_params=pltpu.CompilerParams(has_side_effects=_DATAFLOW),
    )(*[pltpu.with_memory_space_constraint(a, pltpu.HBM) for a in arrays], *after)
    return (out[0], out[1]), list(out[2:2 + na]), out[-1]


def _exchange_wait(name, sems, in_flight, plan, after=()):
    na = len(in_flight)

    def body(*refs):
        for send, recv in plan(refs[:na], refs[na], refs[na + 1]):
            send.wait_send()
            recv.wait_recv()

    out = pl.pallas_call(
        body, name=name,
        out_shape=tuple(pltpu.HBM(a.shape, a.dtype) for a in in_flight),
        in_specs=[_HBM] * na + [_SEM, _SEM] + [_ANY] * len(after), out_specs=tuple([_HBM] * na),
        input_output_aliases={i: i for i in range(na)},
        compiler_params=pltpu.CompilerParams(has_side_effects=_DATAFLOW),
    )(*in_flight, *sems, *after)
    return list(out)


def _remote(src, dst, send_sems, recv_sems, k, to):
    return pltpu.make_async_remote_copy(src_ref=src, dst_ref=dst, send_sem=send_sems.at[k], recv_sem=recv_sems.at[k],
                                        device_id=to, device_id_type=MESH)


def _half_rows(ref, tid, pj, half):
    h = SHARD_ROWS[tid] // 2
    return ref.at[pl.ds((2 * pj + half) * h, h), :]


def _plan_gather_halves(tids):
    def plan(refs, ss, rs):
        x, y, c, chips = _place()
        pairs = []
        for t, tid in enumerate(tids):
            mine = _half_rows(refs[t], tid, 2 * x + y, c)
            for k, (cx, cy) in enumerate(chips):
                theirs = _half_rows(refs[t], tid, 2 * cx + cy, c)
                pairs.append((_remote(mine, mine, ss, rs, 3 * t + k, (cx, cy, c)),
                              _remote(theirs, theirs, ss, rs, 3 * t + k, (x, y, c))))
        return pairs
    return plan


def _plan_forward_halves(tids):
    def plan(refs, ss, rs):
        x, y, c, chips = _place()
        pairs = []
        for t, tid in enumerate(tids):
            for k, (cx, cy) in enumerate(chips):
                got = _half_rows(refs[t], tid, 2 * cx + cy, c)
                missing = _half_rows(refs[t], tid, 2 * cx + cy, 1 - c)
                pairs.append((_remote(got, got, ss, rs, 3 * t + k, (x, y, 1 - c)),
                              _remote(missing, missing, ss, rs, 3 * t + k, (x, y, 1 - c))))
        return pairs
    return plan


def _plan_to_all(grp):
    def plan(refs, ss, rs):
        x, y, c, chips = _place()
        peers = [(x, y, 1 - c)] + [(cx, cy, pc) for cx, cy in chips for pc in (c, 1 - c)]
        assert len(peers) == N_PEERS
        cps = [_remote(refs[0].at[pl.ds((2 * px + py) * grp.block + pc * grp.half, grp.half), :],
                       refs[1].at[pl.ds(k * grp.half, grp.half), :], ss, rs, k, (px, py, pc))
               for k, (px, py, pc) in enumerate(peers)]
        return [(cp, cp) for cp in cps]
    return plan


def _plan_share_halves(grp, k=0):
    def plan(refs, ss, rs):
        x, y, c, _ = _place()
        mine = refs[0].at[pl.ds(c * grp.half, grp.half), :]
        other = refs[0].at[pl.ds((1 - c) * grp.half, grp.half), :]
        return [(_remote(mine, mine, ss, rs, k, (x, y, 1 - c)), _remote(other, other, ss, rs, k, (x, y, 1 - c)))]
    return plan


def _plan_both(first, n_refs, second):
    def plan(refs, ss, rs):
        return first(refs[:n_refs], ss, rs) + second(refs[n_refs:], ss, rs)
    return plan


def _empty(rows, dtype):
    return lax.empty((rows, 1024), dtype)


def _all_reduce_small(v, after):
    m_per, n = v.shape
    n_dev = 8

    def body(x_ref, after_ref, out_ref, all_ref, send_sems, recv_sems, local_sem):
        x, y, c, chips = _place()
        me, sibling = (x, y, c), (x, y, 1 - c)

        def rows(px, py, pc):
            return all_ref.at[pl.ds((4 * px + 2 * py + pc) * m_per, m_per), :]

        def copy(k, block, to, src=None):
            return pltpu.make_async_remote_copy(
                src_ref=rows(*block) if src is None else src, dst_ref=rows(*block),
                send_sem=send_sems.at[k], recv_sem=recv_sems.at[k], device_id=to, device_id_type=MESH)

        mine = pltpu.make_async_copy(x_ref, rows(*me), local_sem)
        mine.start()
        first = [copy(0, me, sibling, src=x_ref)]
        first += [copy(1 + k, me, (*chip, c), src=x_ref) for k, chip in enumerate(chips)]
        for cp in first:
            cp.start()
        passed = [copy(4 + k, (*chip, c), sibling) for k, chip in enumerate(chips)]
        for k, chip in enumerate(chips):
            copy(1 + k, (*chip, c), me).wait_recv()
            passed[k].start()
        copy(0, sibling, me).wait_recv()
        for k, chip in enumerate(chips):
            copy(4 + k, (*chip, 1 - c), me).wait_recv()
        for cp in first + passed:
            cp.wait_send()
        mine.wait()
        acc = all_ref[pl.ds(0, m_per), :]
        for d in range(1, n_dev):
            acc = acc + all_ref[pl.ds(d * m_per, m_per), :]
        out_ref[...] = acc

    return pl.pallas_call(
        body, name="all_reduce_small",
        in_specs=[_VMEM, _ANY], out_specs=_VMEM,
        out_shape=jax.ShapeDtypeStruct((m_per, n), F32),
        scratch_shapes=[pltpu.VMEM((n_dev * m_per, n), F32),
                        pltpu.SemaphoreType.DMA((7,)), pltpu.SemaphoreType.DMA((7,)), pltpu.SemaphoreType.DMA],
        compiler_params=pltpu.CompilerParams(has_side_effects=True),
    )(v, after)


def _tables(S, token):
    pos = jnp.arange(S, dtype=jnp.int32).astype(F32) + token[0, 0]

    def cs(half):
        inv_freq = ROPE_THETA ** (-jnp.arange(half, dtype=F32) / half)
        ang = pos[:, None] * inv_freq[None, :]
        return jnp.cos(ang), jnp.sin(ang)

    c64, s64 = cs(64)
    c32, s32 = cs(32)
    cr = jnp.concatenate([c64, c64], axis=1)
    sr = jnp.concatenate([-s64, s64], axis=1)
    ca = jnp.concatenate([c32, c32, c32, c32], axis=1)
    sa = jnp.concatenate([-s32, s32, -s32, s32], axis=1)
    return cr, sr, ca, sa


def _decays():
    H, C = RET_HEADS, RET_CHUNK
    log_gamma = jnp.log1p(-jnp.exp2(-5.0 - jnp.arange(H, dtype=F32)))
    idx = jnp.arange(C, dtype=F32)
    rel = idx[:, None] - idx[None, :]
    dmat = jnp.where(rel[None] >= 0, jnp.exp(log_gamma[:, None, None] * jnp.maximum(rel, 0.0)[None]), 0.0)
    qd = jnp.exp(log_gamma[:, None] * (idx + 1.0))[:, :, None]
    kd = jnp.exp(log_gamma[:, None] * (C - 1.0 - idx))[:, :, None]
    cd = jnp.exp(log_gamma * C)[:, None, None]
    return dmat, qd, kd, cd


def _local_step(place, x, tgt, ln1_g, b_in, ret_norm_g, attn_sinks, ln2_g, lnf_g, w_in, other_weights, tables, decays):
    cr, sr, ca, sa = tables
    dmat, qd, kd, cd = decays
    forward_weights, landed_weights = other_weights

    h, rq, rk, rv, rg, aq, ak, av, ga, gb = _proj_fwd(x, ln1_g, w_in, b_in, cr, sr, ca, sa)
    ry, states = _ret_fwd(rq, rk, rv, dmat, qd, kd, cd)
    token = forward_weights(ry)
    ay = _attn_fwd(aq, ak, av, attn_sinks + token[0, 0])
    w_ret, w_attn, w_out, w_gate, w_up, w_down = landed_weights(ay)
    u, merged, x1 = _merge_fwd(x, ry, rg, ay, ga, gb, ret_norm_g, w_ret, w_attn, w_out)
    gate, up, h2, act, dx2, loss, dg3 = _ffn_fwd(x1, tgt, ln2_g, lnf_g, w_gate, w_up, w_down)

    dgate, dup, dx1, dg2 = _ffn_bwd(dx2, gate, up, x1, ln2_g, w_gate, w_up, w_down)
    early = _wgrad([act], dx2, 6, None, "wgrad_down")
    early = _wgrad([dgate], h2, 4, early, "wgrad_gate")
    early = _wgrad([dup], h2, 5, early, "wgrad_up")
    dgates, drg, dba, dbb, day, dry, dgr = _merge_bwd(dx1, ga, gb, u, ay, rg, ry, ret_norm_g,
                                                                w_ret, w_attn, w_out)
    early = _wgrad([merged], dx1, 3, early, "wgrad_out")
    early = _wgrad([u], dba, 1, early, "wgrad_ret")
    e32, e16 = _wgrad([ay], dbb, 2, early, "wgrad_attn")

    sems, flight, token = _exchange_start("to_all_start_early", [e16, _empty(N_PEERS * EARLY.half, BF)], N_PEERS,
                                          _plan_to_all(EARLY))
    daq, dak, dav, dsk = _attn_bwd(aq, ak, av, day, attn_sinks, ca, sa, token)
    d_ret = _ret_bwd(rq, rk, rv, dry, states, dmat, qd, kd, cd, cr, sr, token)
    d_proj = [d_ret, drg, daq, dak, dav, dgates]
    l32, l16 = _wgrad(d_proj, h, 0, None, "wgrad_in")
    _, others = _exchange_wait("to_all_wait_early", sems, flight, _plan_to_all(EARLY), after=(l32,))
    share = _plan_share_halves(EARLY, k=N_PEERS)
    sems, flight, token = _exchange_start(
        "to_all_late_share_early_start", [l16, _empty(N_PEERS * LATE.half, BF), _sum_all(place, e32, others, EARLY)],
        N_PEERS + 1, _plan_both(_plan_to_all(LATE), 2, share))
    grad_x, dg1, db = _proj_bwd(d_proj, x, dx1, ln1_g + token[0, 0], w_in)
    early_shard, = _exchange_wait("share_wait_early", sems, flight[2:], share, after=(grad_x,))

    def finish_late(after):
        _, others = _exchange_wait("to_all_wait_late", sems, flight[:2], _plan_to_all(LATE), after=after)
        return _sum_all(place, l32, others, LATE)

    d_sinks = dsk.reshape(2, 2, PAIRS, 128).sum(-1).transpose(0, 2, 1).reshape(1, ATTN_Q_HEADS)
    small = dict(loss=loss, ln1_g=dg1, b_in=db, ret_norm_g=dgr, attn_sinks=d_sinks,
                 ln2_g=dg2, lnf_g=dg3)
    return grad_x, early_shard, token, finish_late, small


SMALL_ROWS = 16


def _pack_small(ln1, ret, ln2, lnf, b_in, sinks, loss):
    def body(a_ref, b_ref, c_ref, d_ref, bi_ref, sk_ref, ls_ref, o_ref):
        o_ref[...] = jnp.zeros_like(o_ref)
        for row, ref in enumerate((a_ref, b_ref, c_ref, d_ref)):
            o_ref[row:row + 1, :] = ref[...]
        for k in range(6):
            o_ref[4 + k:5 + k, :] = bi_ref[:, k * 1024:(k + 1) * 1024]
        o_ref[10:11, 0:D_IN - 6144] = bi_ref[:, 6144:D_IN]
        o_ref[11:12, 0:ATTN_Q_HEADS] = sk_ref[...]
        o_ref[12:13, 0:LANES] = ls_ref[...]

    return pl.pallas_call(
        body, name="pack_small_grads", in_specs=[_VMEM] * 7, out_specs=_VMEM,
        out_shape=jax.ShapeDtypeStruct((SMALL_ROWS, 1024), F32),
    )(ln1.reshape(1, 1024), ret.reshape(1, 1024), ln2.reshape(1, 1024), lnf.reshape(1, 1024),
      b_in.reshape(1, D_IN), sinks.reshape(1, ATTN_Q_HEADS), loss.reshape(1, LANES))


def _adamw_small(pg, ws, ms, vs):
    n_par = len(ws)
    widths = [w.size for w in ws]
    first_row = [0, 1, 2, 3, 4, 11]

    def body(pg_ref, *refs):
        w_refs, m_refs, v_refs = refs[:n_par], refs[n_par:2 * n_par], refs[2 * n_par:3 * n_par]
        out_refs = refs[3 * n_par:]
        for i, width in enumerate(widths):
            for k, c0 in enumerate(range(0, width, 1024)):
                cols = slice(c0, min(c0 + 1024, width))
                gv = pg_ref[first_row[i] + k:first_row[i] + k + 1, 0:cols.stop - c0]
                go_ref, d_ref, mo_ref, vo_ref = out_refs[4 * i:4 * i + 4]
                go_ref[:, cols] = gv
                d_ref[:, cols], mo_ref[:, cols], vo_ref[:, cols] = _adamw_math(
                    w_refs[i][:, cols], gv, m_refs[i][:, cols], v_refs[i][:, cols])

    flat = lambda a: a.reshape(1, a.size)
    res = pl.pallas_call(
        body, name="adamw_small", in_specs=[_VMEM] * (1 + 3 * n_par), out_specs=[_VMEM] * (4 * n_par),
        out_shape=[jax.ShapeDtypeStruct((1, width), F32) for width in widths for _ in range(4)],
    )(pg, *map(flat, ws), *map(flat, ms), *map(flat, vs))
    return [tuple(r.reshape(w.shape) for r in res[4 * i:4 * i + 4]) for i, w in enumerate(ws)]


def kernel(x, ln1_g, w_in, b_in, ret_norm_g, w_ret_out, attn_sinks, w_attn_out, w_out, ln2_g, w_ffn_gate, w_ffn_up, w_ffn_down, lnf_g, loss_target, m_ln1_g, m_w_in, m_b_in, m_ret_norm_g, m_w_ret_out, m_attn_sinks, m_w_attn_out, m_w_out, m_ln2_g, m_w_ffn_gate, m_w_ffn_up, m_w_ffn_down, m_lnf_g, v_ln1_g, v_w_in, v_b_in, v_ret_norm_g, v_w_ret_out, v_attn_sinks, v_w_attn_out, v_w_out, v_ln2_g, v_w_ffn_gate, v_w_ffn_up, v_w_ffn_down, v_lnf_g):
    xi = lax.axis_index("x")
    yi = lax.axis_index("y")
    ci = lax.axis_index("c")
    j = 2 * xi + yi

    place = jnp.stack([j, ci]).astype(jnp.int32)

    names = ["w_in", "w_ret_out", "w_attn_out", "w_out", "w_ffn_gate", "w_ffn_up", "w_ffn_down"]
    transposed = [True, False, False, False, True, True, False]
    view = lambda a, t: a[0].T if t else a[0]
    big_w = [view(a, t) for a, t in zip([w_in, w_ret_out, w_attn_out, w_out, w_ffn_gate, w_ffn_up, w_ffn_down], transposed)]
    big_m = [view(a, t) for a, t in zip([m_w_in, m_w_ret_out, m_w_attn_out, m_w_out, m_w_ffn_gate, m_w_ffn_up, m_w_ffn_down], transposed)]
    big_v = [view(a, t) for a, t in zip([v_w_in, v_w_ret_out, v_w_attn_out, v_w_out, v_w_ffn_gate, v_w_ffn_up, v_w_ffn_down], transposed)]

    staged_in = _stage_weights(place, big_w[:1], "stage_w_in")[0]
    sems, in_flight, token = _exchange_start("gather_in_start", [staged_in], 3, _plan_gather_halves([0]))
    staged = _stage_weights(place, big_w[1:], "stage_others", after=(token,))
    tables = _tables(x.shape[1], token)
    decays = _decays()
    w_in_full = _exchange_wait("gather_in_wait", sems, in_flight, _plan_gather_halves([0]),
                               after=(*staged, *tables, *decays))[0]
    w_in_full = _exchange("gather_in_forward", [w_in_full], 3, _plan_forward_halves([0]))[0]
    rest = list(range(1, 7))
    crossing = _exchange_start("gather_start", staged, 3 * len(rest), _plan_gather_halves(rest), after=(w_in_full,))
    token = crossing[2]
    forwarding = []

    def forward_weights(after):
        arrived = _exchange_wait("gather_wait", crossing[0], crossing[1], _plan_gather_halves(rest), after=(after,))
        forwarding.extend(_exchange_start("forward_start", arrived, 3 * len(rest), _plan_forward_halves(rest)))
        return forwarding[2]

    def landed_weights(after):
        return _exchange_wait("forward_wait", forwarding[0], forwarding[1], _plan_forward_halves(rest), after=(after,))

    other_weights = (forward_weights, landed_weights)

    grad_x, early_shard, token, finish_late, small = _local_step(
        place, x[0], loss_target[0], ln1_g, b_in + token[0, 0], ret_norm_g, attn_sinks, ln2_g, lnf_g.reshape(1, -1),
        w_in_full, other_weights, tables, decays)

    upd = {}

    def update(t, shard, after):
        res = _adamw(big_w[t], shard, big_m[t], big_v[t], "adamw_" + names[t], g_row0=GROUP_OF[t].off(t), after=after)
        upd[names[t]] = tuple((r.T if transposed[t] else r)[None] for r in res)
        return res[0]

    done = [update(t, early_shard, token) for t in EARLY.tensors]
    late_half = finish_late((*done, grad_x))
    pg = _all_reduce_small(_pack_small(
        small["ln1_g"], small["ret_norm_g"], small["ln2_g"], small["lnf_g"], small["b_in"],
        small["attn_sinks"], small["loss"]), late_half)
    late_shard = _exchange("share_late", [late_half], 1, _plan_share_halves(LATE))[0]
    update(0, late_shard, None)

    small_names = ["ln1_g", "ret_norm_g", "ln2_g", "lnf_g", "b_in", "attn_sinks"]
    upd.update(zip(small_names, _adamw_small(
        pg, [ln1_g, ret_norm_g, ln2_g, lnf_g, b_in, attn_sinks],
        [m_ln1_g, m_ret_norm_g, m_ln2_g, m_lnf_g, m_b_in, m_attn_sinks],
        [v_ln1_g, v_ret_norm_g, v_ln2_g, v_lnf_g, v_b_in, v_attn_sinks])))

    order_w = ["ln1_g", "w_in", "b_in", "ret_norm_g", "w_ret_out", "attn_sinks", "w_attn_out", "w_out",
               "ln2_g", "w_ffn_gate", "w_ffn_up", "w_ffn_down", "lnf_g"]
    outs = [pg[12, 0], grad_x[None]]
    for k in range(4):
        outs += [upd[n][k] for n in order_w]
    return tuple(outs)
```

```python
from typing import NamedTuple

import jax
import jax.numpy as jnp
from jax import lax
from jax.experimental import pallas as pl
from jax.experimental.pallas import tpu as pltpu

D_MODEL = 1024
SEQ_CHUNK = 128
RET_CHUNK = 256
RET_CHUNKS_PER_STEP = 4
RET_HEADS = 4
RET_QK_DIM = 128
ATTN_Q_HEADS = 16
ATTN_HEAD_DIM = 64
D_FF = 2816
D_IN = 6400
ROPE_THETA = 10000.0
EPS = 1e-6
N_CHIPS = 4

ADAM_LR = 0.001
ADAM_B1 = 0.9
ADAM_B2 = 0.999
ADAM_EPS = 1e-08
ADAM_WD = 0.01
ADAM_STEP = 10

OFF_RQ, OFF_RK, OFF_RV, OFF_RG = 0, 512, 1024, 2048
OFF_AQ, OFF_AK, OFF_AV, OFF_GA, OFF_GB = 3072, 4096, 4224, 4352, 5376

V7X_VMEM_LIMIT = 56 * 1024 * 1024
LANES = 128

BF = jnp.bfloat16
F32 = jnp.float32
KEPT = jnp.bfloat16
MESH = pl.DeviceIdType.MESH

_VMEM = pl.BlockSpec(memory_space=pltpu.VMEM)
_SMEM = pl.BlockSpec(memory_space=pltpu.SMEM)
_ANY = pl.BlockSpec(memory_space=pl.ANY)


def _dot(a, b):
    return jnp.dot(a, b, preferred_element_type=F32)


def _dot_nt(a, b):
    return lax.dot_general(a, b, (((1,), (1,)), ((), ())), preferred_element_type=F32)


def _dot_tn(a, b):
    return lax.dot_general(a, b, (((0,), (0,)), ((), ())), preferred_element_type=F32)


def _sigmoid(x):
    return 0.5 * jnp.tanh(0.5 * x) + 0.5


def _rows(tm, n):
    return pl.BlockSpec((tm, n), lambda i: (i, 0))


def _fixed(shape):
    nd = len(shape)
    return pl.BlockSpec(shape, lambda i: (0,) * nd)


def _params(sem, **kw):
    return pltpu.CompilerParams(dimension_semantics=sem, vmem_limit_bytes=V7X_VMEM_LIMIT, **kw)


def _rot_ret(p, cr, sr):
    return p * cr + pltpu.roll(p, 64, 1) * sr


def _rot_ret_bwd(d, cr, sr):
    return d * cr - pltpu.roll(d, 64, 1) * sr


def _attn_partner(p):
    lane = lax.broadcasted_iota(jnp.int32, p.shape, 1)
    first = (lane % 64) < 32
    return jnp.where(first, pltpu.roll(p, 96, 1), pltpu.roll(p, 32, 1))


def _rot_attn(p, ca, sa):
    return p * ca + _attn_partner(p) * sa


def _rot_attn_bwd(d, ca, sa):
    return d * ca - _attn_partner(d) * sa


def _proj_fwd(x, g1, w_in, b_in, cr, sr, ca, sa):
    S = x.shape[0]
    tm = 512
    scale = RET_QK_DIM ** -0.5

    def body(x_ref, g_ref, w_ref, b_ref, cr_ref, sr_ref, ca_ref, sa_ref,
             h_ref, rq_ref, rk_ref, rv_ref, rg_ref, aq_ref, ak_ref, av_ref, ga_ref, gb_ref):
        xv = x_ref[...]
        r = lax.rsqrt(jnp.mean(xv * xv, axis=-1, keepdims=True) + EPS)
        h = (xv * r * g_ref[...]).astype(BF)
        h_ref[...] = h

        def piece(lo, n):
            return _dot_nt(h, w_ref[lo:lo + n, :]) + b_ref[:, lo:lo + n]

        crv, srv = cr_ref[...], sr_ref[...]
        pq = piece(OFF_RQ, 512)
        pk = piece(OFF_RK, 512)
        for hd in range(RET_HEADS):
            sl = slice(hd * 128, (hd + 1) * 128)
            rq_ref[:, sl] = (_rot_ret(pq[:, sl], crv, srv) * scale).astype(BF)
            rk_ref[:, sl] = _rot_ret(pk[:, sl], crv, srv)
        rv_ref[...] = piece(OFF_RV, 1024).astype(BF)
        rg_ref[...] = piece(OFF_RG, 1024).astype(KEPT)
        cav, sav = ca_ref[...], sa_ref[...]
        pa = piece(OFF_AQ, 1024)
        for blk in range(8):
            sl = slice(blk * 128, (blk + 1) * 128)
            aq_ref[:, sl] = _rot_attn(pa[:, sl], cav, sav).astype(BF)
        pkv = piece(OFF_AK, 256)
        ak_ref[...] = _rot_attn(pkv[:, :128], cav, sav).astype(BF)
        av_ref[...] = pkv[:, 128:].astype(BF)
        ga_ref[...] = piece(OFF_GA, 1024).astype(KEPT)
        gb_ref[...] = piece(OFF_GB, 1024).astype(KEPT)

    outs = [(D_MODEL, BF), (512, BF), (512, F32), (1024, BF), (1024, KEPT),
            (1024, BF), (128, BF), (128, BF), (1024, KEPT), (1024, KEPT)]
    return pl.pallas_call(
        body, name="proj_fwd", grid=(S // tm,),
        in_specs=[_rows(tm, D_MODEL), _fixed((1, D_MODEL)), _VMEM, _VMEM,
                  _rows(tm, 128), _rows(tm, 128), _rows(tm, 128), _rows(tm, 128)],
        out_specs=[_rows(tm, n) for n, _ in outs],
        out_shape=[jax.ShapeDtypeStruct((S, n), dt) for n, dt in outs],
        compiler_params=_params(("parallel",)),
    )(x, g1, w_in, b_in, cr, sr, ca, sa)


def _ret_fwd(rq, rk, rv, dmat, qd, kd, cd):
    S = rq.shape[0]
    N = S // RET_CHUNK
    cps = RET_CHUNKS_PER_STEP
    rows = cps * RET_CHUNK

    def body(q_ref, k_ref, v_ref, dm_ref, qd_ref, kd_ref, cd_ref, y_ref, st_ref, state):
        @pl.when(pl.program_id(0) == 0)
        def _():
            state[...] = jnp.zeros_like(state)

        for hd in range(RET_HEADS):
            qs = slice(hd * 128, (hd + 1) * 128)
            vs = slice(hd * 256, (hd + 1) * 256)
            free = []
            for ch in range(cps):
                rs = slice(ch * RET_CHUNK, (ch + 1) * RET_CHUNK)
                q = q_ref[rs, qs]
                kf = k_ref[rs, qs]
                v = v_ref[rs, vs]
                a = _dot_nt(q, kf.astype(BF)) * dm_ref[hd]
                kdv = (kf * kd_ref[hd]).astype(BF)
                free.append((rs, q, _dot(a.astype(BF), v), _dot_tn(kdv, v)))
            s = state[hd]
            for ch, (rs, q, inner, update) in enumerate(free):
                sb = s.astype(BF)
                st_ref[ch, hd] = sb
                y_ref[rs, vs] = (inner + _dot(q, sb) * qd_ref[hd]).astype(KEPT)
                s = s * cd_ref[hd] + update
            state[hd] = s

    return pl.pallas_call(
        body, name="ret_fwd", grid=(N // cps,),
        in_specs=[_rows(rows, 512), _rows(rows, 512), _rows(rows, 1024), _VMEM, _VMEM, _VMEM, _VMEM],
        out_specs=[_rows(rows, 1024), pl.BlockSpec((cps, RET_HEADS, 128, 256), lambda i: (i, 0, 0, 0))],
        out_shape=[jax.ShapeDtypeStruct((S, 1024), KEPT),
                   jax.ShapeDtypeStruct((N, RET_HEADS, 128, 256), BF)],
        scratch_shapes=[pltpu.VMEM((RET_HEADS, 128, 256), F32)],
        compiler_params=_params(("arbitrary",)),
    )(rq, rk, rv, dmat, qd, kd, cd)


def _kv_halves(prev_ref, cur_ref):
    kk = jnp.concatenate([prev_ref[...], cur_ref[...]], axis=0).astype(F32)
    return kk, pltpu.roll(kk, 64, 1)


ATTN_FWD_BLOCKS_PER_STEP = 4
PAIRS = 4


def _attn_fold(n):
    t = lax.broadcasted_iota(jnp.int32, (128, PAIRS * 128), 0)
    qi = lax.broadcasted_iota(jnp.int32, (128, PAIRS * 128), 1) & 127
    cur = t <= qi
    cur_f = jnp.where(cur, 1.0, 0.0)
    bias = jnp.where(jnp.logical_or(cur, n > 0), 0.0, -1e30)
    return cur, cur_f, 1.0 - cur_f, bias


def _group_operands(g, kk, kk_sw, vv, vv_sw):
    lo = lax.broadcasted_iota(jnp.int32, (256, 128), 1) < 64
    k_a, k_b = (kk, kk_sw) if g == 0 else (kk_sw, kk)
    v_a, v_b = (vv, vv_sw) if g == 0 else (vv_sw, vv)
    return (jnp.concatenate([jnp.where(lo, k_a, 0.0), jnp.where(lo, 0.0, k_b)], axis=0).astype(BF),
            jnp.concatenate([jnp.where(lo, v_a, 0.0), jnp.where(lo, 0.0, v_b)], axis=0).astype(BF))


def _stack_pairs(ref, g, rows=slice(None)):
    return jnp.concatenate([ref[rows, (4 * g + p) * 128:(4 * g + p + 1) * 128] for p in range(PAIRS)], axis=0)


def _sink_row(sk_ref, g, which):
    return jnp.concatenate([jnp.full((1, 128), sk_ref[0, 8 * g + 2 * p + which], F32) for p in range(PAIRS)], axis=1)


def _fold(full, cur):
    return jnp.where(cur, full[128:], full[:128])


def _attn_probs(raw, fold, sink):
    s = _fold(raw, fold[0]) * (ATTN_HEAD_DIM ** -0.5) + fold[3]
    m = jnp.maximum(jnp.max(s, axis=0, keepdims=True), sink)
    e = jnp.exp(s - m)
    es = jnp.exp(sink - m)
    inv = 1.0 / (jnp.sum(e, axis=0, keepdims=True) + es)
    return e * inv, es * inv


def _unfold(p, fold):
    return jnp.concatenate([p * fold[2], p * fold[1]], axis=0).astype(BF)


def _attn_fwd(aq, ak, av, sinks):
    S = aq.shape[0]
    N = S // SEQ_CHUNK
    nb = ATTN_FWD_BLOCKS_PER_STEP

    def body(sk_ref, q_ref, kp_ref, kc_ref, vp_ref, vc_ref, o_ref):
        kall, kall_sw = _kv_halves(kp_ref, kc_ref)
        vall, vall_sw = _kv_halves(vp_ref, vc_ref)
        blocks = []
        for sb in range(nb):
            kv = slice(sb * 128, sb * 128 + 256)
            rows = slice(sb * 128, (sb + 1) * 128)
            ops = [_group_operands(g, kall[kv], kall_sw[kv], vall[kv], vall_sw[kv]) for g in range(2)]
            raw = [_dot_nt(ops[g][0], _stack_pairs(q_ref, g, rows)) for g in range(2)]
            blocks.append((rows, ops, raw, _attn_fold(nb * pl.program_id(0) + sb)))
        for rows, ops, raw, fold in blocks:
            for g in range(2):
                probs = [_unfold(_attn_probs(raw[g][w * 256:(w + 1) * 256], fold, _sink_row(sk_ref, g, w))[0], fold)
                         for w in range(2)]
                o = _dot_tn(jnp.concatenate(probs, axis=0), ops[g][1])
                for p in range(PAIRS):
                    o_ref[rows, (4 * g + p) * 128:(4 * g + p + 1) * 128] = o[p * 128:(p + 1) * 128].astype(BF)

    prev = lambda i: (jnp.maximum(nb * i - 1, 0), 0)
    return pl.pallas_call(
        body, name="attn_fwd", grid=(N // nb,),
        in_specs=[_SMEM, _rows(nb * 128, 1024), pl.BlockSpec((128, 128), prev), _rows(nb * 128, 128),
                  pl.BlockSpec((128, 128), prev), _rows(nb * 128, 128)],
        out_specs=_rows(nb * 128, 1024),
        out_shape=jax.ShapeDtypeStruct((S, 1024), BF),
        compiler_params=_params(("parallel",)),
    )(sinks, aq, ak, ak, av, av)


def _group_norm(y):
    mu = jnp.mean(y, axis=-1, keepdims=True)
    yc = y - mu
    rstd = lax.rsqrt(jnp.mean(yc * yc, axis=-1, keepdims=True) + EPS)
    return yc * rstd, rstd


def _merge_fwd(x, ry, rg, ay, ga, gb, g_ret, w_ret, w_attn, w_out):
    S = x.shape[0]
    tm = 512

    def body(x_ref, ry_ref, rg_ref, ay_ref, ga_ref, gb_ref, gr_ref, wr_ref, wa_ref, wo_ref,
             u_ref, mg_ref, x1_ref):
        bb = _dot(ay_ref[...], wa_ref[...])
        for hd in range(RET_HEADS):
            vs = slice(hd * 256, (hd + 1) * 256)
            yn, _ = _group_norm(ry_ref[:, vs].astype(F32))
            rgv = rg_ref[:, vs].astype(F32)
            u_ref[:, vs] = (rgv * _sigmoid(rgv) * (yn * gr_ref[:, vs])).astype(BF)
        gated_b = _sigmoid(gb_ref[...].astype(F32)) * bb
        sa = _sigmoid(ga_ref[...].astype(F32))
        ba = _dot(u_ref[...], wr_ref[...])
        merged = (sa * ba + gated_b).astype(BF)
        mg_ref[...] = merged
        x1_ref[...] = x_ref[...] + _dot(merged, wo_ref[...])

    outs = [BF, BF, F32]
    return pl.pallas_call(
        body, name="merge_fwd", grid=(S // tm,),
        in_specs=[_rows(tm, 1024)] * 6 + [_fixed((1, 1024)), _VMEM, _VMEM, _VMEM],
        out_specs=[_rows(tm, 1024)] * 3,
        out_shape=[jax.ShapeDtypeStruct((S, 1024), dt) for dt in outs],
        compiler_params=_params(("parallel",)),
    )(x, ry, rg, ay, ga, gb, g_ret, w_ret, w_attn, w_out)


def _ffn_fwd(x1, tgt, g2, g3, w_gate, w_up, w_down):
    S = x1.shape[0]
    tm = 256

    def body(x1_ref, t_ref, g2_ref, g3_ref, wg_ref, wu_ref, wd_ref,
             gate_ref, up_ref, h2_ref, act_ref, dx2_ref, loss_ref, dg3_ref):
        @pl.when(pl.program_id(0) == 0)
        def _():
            loss_ref[...] = jnp.zeros_like(loss_ref)
            dg3_ref[...] = jnp.zeros_like(dg3_ref)

        x1v = x1_ref[...]
        r2 = lax.rsqrt(jnp.mean(x1v * x1v, axis=-1, keepdims=True) + EPS)
        h2 = (x1v * r2 * g2_ref[...]).astype(BF)
        h2_ref[...] = h2
        gate = _dot_nt(h2, wg_ref[...])
        up = _dot_nt(h2, wu_ref[...])
        gate_ref[...] = gate
        up_ref[...] = up
        act = (gate * _sigmoid(gate) * up).astype(BF)
        act_ref[...] = act
        x2 = x1v + _dot(act, wd_ref[...])
        r3 = lax.rsqrt(jnp.mean(x2 * x2, axis=-1, keepdims=True) + EPS)
        xh = x2 * r3
        g3v = g3_ref[...]
        err = xh * g3v - t_ref[...]
        loss_ref[...] += 0.5 * jnp.sum(jnp.mean(err * err, axis=-1, keepdims=True), axis=0, keepdims=True)
        dy = err * (1.0 / D_MODEL)
        dg3_ref[...] += jnp.sum(dy * xh, axis=0, keepdims=True)
        dxh = dy * g3v
        dx2_ref[...] = r3 * (dxh - xh * jnp.mean(dxh * xh, axis=-1, keepdims=True))

    return pl.pallas_call(
        body, name="ffn_fwd", grid=(S // tm,),
        in_specs=[_rows(tm, 1024), _rows(tm, 1024), _fixed((1, 1024)), _fixed((1, 1024)), _VMEM, _VMEM, _VMEM],
        out_specs=[_rows(tm, D_FF), _rows(tm, D_FF), _rows(tm, 1024), _rows(tm, D_FF), _rows(tm, 1024),
                   _fixed((1, LANES)), _fixed((1, 1024))],
        out_shape=[jax.ShapeDtypeStruct((S, D_FF), F32), jax.ShapeDtypeStruct((S, D_FF), F32),
                   jax.ShapeDtypeStruct((S, 1024), BF), jax.ShapeDtypeStruct((S, D_FF), BF),
                   jax.ShapeDtypeStruct((S, 1024), F32),
                   jax.ShapeDtypeStruct((1, LANES), F32), jax.ShapeDtypeStruct((1, 1024), F32)],
        compiler_params=_params(("arbitrary",)),
    )(x1, tgt, g2, g3, w_gate, w_up, w_down)


def _ffn_bwd(dx2, gate, up, x1, g2, w_gate, w_up, w_down):
    S = x1.shape[0]
    tm = 256

    def body(dx2_ref, gate_ref, up_ref, x1_ref, g2_ref, wg_ref, wu_ref, wd_ref,
             dgate_ref, dup_ref, dx1_ref, dg2_ref):
        @pl.when(pl.program_id(0) == 0)
        def _():
            dg2_ref[...] = jnp.zeros_like(dg2_ref)

        dx2v = dx2_ref[...]
        dact = _dot_nt(dx2v.astype(BF), wd_ref[...])
        gate = gate_ref[...]
        sg = _sigmoid(gate)
        dgate = (dact * up_ref[...] * (sg * (1.0 + gate * (1.0 - sg)))).astype(BF)
        dup = (dact * (gate * sg)).astype(BF)
        dgate_ref[...] = dgate
        dup_ref[...] = dup
        dh2 = _dot(dgate, wg_ref[...]) + _dot(dup, wu_ref[...])
        x1v = x1_ref[...]
        r2 = lax.rsqrt(jnp.mean(x1v * x1v, axis=-1, keepdims=True) + EPS)
        xh = x1v * r2
        dg2_ref[...] += jnp.sum(dh2 * xh, axis=0, keepdims=True)
        dxh = dh2 * g2_ref[...]
        dx1_ref[...] = dx2v + r2 * (dxh - xh * jnp.mean(dxh * xh, axis=-1, keepdims=True))

    return pl.pallas_call(
        body, name="ffn_bwd", grid=(S // tm,),
        in_specs=[_rows(tm, 1024), _rows(tm, D_FF), _rows(tm, D_FF), _rows(tm, 1024), _fixed((1, 1024)),
                  _VMEM, _VMEM, _VMEM],
        out_specs=[_rows(tm, D_FF), _rows(tm, D_FF), _rows(tm, 1024), _fixed((1, 1024))],
        out_shape=[jax.ShapeDtypeStruct((S, D_FF), BF), jax.ShapeDtypeStruct((S, D_FF), BF),
                   jax.ShapeDtypeStruct((S, 1024), F32), jax.ShapeDtypeStruct((1, 1024), F32)],
        compiler_params=_params(("arbitrary",)),
    )(dx2, gate, up, x1, g2, w_gate, w_up, w_down)


def _merge_bwd(dx1, ga, gb, u, ay, rg, ry, g_ret, w_ret, w_attn, w_out):
    S = dx1.shape[0]
    tm = 512

    def body(dx1_ref, ga_ref, gb_ref, u_ref, ay_ref, rg_ref, ry_ref, gr_ref, wr_ref, wa_ref, wo_ref,
             dgates_ref, drg_ref, dba_ref, dbb_ref, day_ref, dry_ref, dgr_ref):
        @pl.when(pl.program_id(0) == 0)
        def _():
            dgr_ref[...] = jnp.zeros_like(dgr_ref)

        dm = _dot_nt(dx1_ref[...].astype(BF), wo_ref[...])
        ba = _dot(u_ref[...], wr_ref[...])
        bb = _dot(ay_ref[...], wa_ref[...])
        sa = _sigmoid(ga_ref[...].astype(F32))
        sb = _sigmoid(gb_ref[...].astype(F32))
        dga = (dm * ba * (sa * (1.0 - sa))).astype(BF)
        dgb = (dm * bb * (sb * (1.0 - sb))).astype(BF)
        dgates_ref[:, :1024] = dga
        dgates_ref[:, 1024:] = dgb
        dba = (dm * sa).astype(BF)
        dbb = (dm * sb).astype(BF)
        dba_ref[...] = dba
        dbb_ref[...] = dbb
        day_ref[...] = _dot_nt(dbb, wa_ref[...]).astype(BF)
        du = _dot_nt(dba, wr_ref[...])
        for hd in range(RET_HEADS):
            vs = slice(hd * 256, (hd + 1) * 256)
            yn, rstd = _group_norm(ry_ref[:, vs].astype(F32))
            rgv = rg_ref[:, vs].astype(F32)
            sr = _sigmoid(rgv)
            silu = rgv * sr
            duv = du[:, vs]
            grv = gr_ref[:, vs]
            drg = (duv * (yn * grv) * (sr * (1.0 + rgv * (1.0 - sr)))).astype(BF)
            drg_ref[:, vs] = drg
            dyg = duv * silu
            dgr_ref[:, vs] += jnp.sum(dyg * yn, axis=0, keepdims=True)
            dyn = dyg * grv
            dry_ref[:, vs] = (rstd * (dyn - jnp.mean(dyn, axis=-1, keepdims=True)
                                      - yn * jnp.mean(dyn * yn, axis=-1, keepdims=True))).astype(KEPT)

    return pl.pallas_call(
        body, name="merge_bwd", grid=(S // tm,),
        in_specs=[_rows(tm, 1024)] * 7 + [_fixed((1, 1024)), _VMEM, _VMEM, _VMEM],
        out_specs=[_rows(tm, 2048), _rows(tm, 1024), _rows(tm, 1024), _rows(tm, 1024), _rows(tm, 1024),
                   _rows(tm, 1024), _fixed((1, 1024))],
        out_shape=[jax.ShapeDtypeStruct((S, 2048), BF), jax.ShapeDtypeStruct((S, 1024), BF),
                   jax.ShapeDtypeStruct((S, 1024), BF), jax.ShapeDtypeStruct((S, 1024), BF),
                   jax.ShapeDtypeStruct((S, 1024), BF), jax.ShapeDtypeStruct((S, 1024), KEPT),
                   jax.ShapeDtypeStruct((1, 1024), F32)],
        compiler_params=_params(("arbitrary",)),
    )(dx1, ga, gb, u, ay, rg, ry, g_ret, w_ret, w_attn, w_out)


def _attn_bwd(aq, ak, av, day, sinks, ca, sa, after):
    S = aq.shape[0]
    N = S // SEQ_CHUNK
    steps = N // 2

    def body(sk_ref, q_ref, kp_ref, kc_ref, vp_ref, vc_ref, do_ref, cq_ref, sq_ref, cp_ref, sp_ref, after_ref,
             dq_ref, dke_ref, dve_ref, dko_ref, dvo_ref, dsk_ref, carry_k, carry_v):
        i = pl.program_id(0)

        @pl.when(i == 0)
        def _():
            carry_k[...] = jnp.zeros_like(carry_k)
            carry_v[...] = jnp.zeros_like(carry_v)
            dsk_ref[...] = jnp.zeros_like(dsk_ref)

        @pl.when(i < steps)
        def _():
            kall, kall_sw = _kv_halves(kp_ref, kc_ref)
            vall, vall_sw = _kv_halves(vp_ref, vc_ref)
            lo = lax.broadcasted_iota(jnp.int32, (256, 128), 1) < 64
            blocks = []
            for sb in range(2):
                kv = slice(sb * 128, sb * 128 + 256)
                rows = slice(sb * 128, (sb + 1) * 128)
                ops = [_group_operands(g, kall[kv], kall_sw[kv], vall[kv], vall_sw[kv]) for g in range(2)]
                stacked = [(_stack_pairs(q_ref, g, rows), _stack_pairs(do_ref, g, rows)) for g in range(2)]
                raw = [_dot_nt(ops[g][0], stacked[g][0]) for g in range(2)]
                raw_dp = [_dot_nt(ops[g][1], stacked[g][1]) for g in range(2)]
                blocks.append((rows, ops, stacked, raw, raw_dp, _attn_fold(2 * i + sb)))
            grads = []
            d_sink = [jnp.zeros((1, PAIRS * 128), F32) for _ in range(4)]
            for rows, ops, stacked, raw, raw_dp, fold in blocks:
                cur = fold[0]
                cq, sq = cq_ref[rows, :], sq_ref[rows, :]
                dk_groups, dv_groups = [], []
                for g in range(2):
                    qs, dos = stacked[g]
                    ds_parts, p_parts = [], []
                    for which in range(2):
                        half = slice(which * 256, (which + 1) * 256)
                        pr, ps = _attn_probs(raw[g][half], fold, _sink_row(sk_ref, g, which))
                        dp = _fold(raw_dp[g][half], cur)
                        dr = jnp.sum(pr * dp, axis=0, keepdims=True)
                        ds_parts.append(_unfold(pr * (dp - dr) * (ATTN_HEAD_DIM ** -0.5), fold))
                        p_parts.append(_unfold(pr, fold))
                        d_sink[2 * g + which] = d_sink[2 * g + which] - ps * dr
                    ds = jnp.concatenate(ds_parts, axis=0)
                    dq = _dot_tn(ds, ops[g][0])
                    dk_both = _dot(ds, qs)
                    dv_both = _dot(jnp.concatenate(p_parts, axis=0), dos)
                    for p in range(PAIRS):
                        sl = slice((4 * g + p) * 128, (4 * g + p + 1) * 128)
                        dq_ref[rows, sl] = _rot_attn_bwd(dq[p * 128:(p + 1) * 128], cq, sq).astype(BF)
                    tk = jnp.where(lo, dk_both[:256], dk_both[256:])
                    tv = jnp.where(lo, dv_both[:256], dv_both[256:])
                    dk_groups.append(tk + pltpu.roll(tk, 64, 1))
                    dv_groups.append(tv + pltpu.roll(tv, 64, 1))
                grads.append((jnp.where(lo, dk_groups[0], dk_groups[1]), jnp.where(lo, dv_groups[0], dv_groups[1])))
            dsk_ref[...] += jnp.concatenate(d_sink, axis=0)
            (dk_a, dv_a), (dk_b, dv_b) = grads
            dko_ref[...] = _rot_attn_bwd(carry_k[...] + dk_a[:128], cp_ref[...], sp_ref[...]).astype(BF)
            dvo_ref[...] = (carry_v[...] + dv_a[:128]).astype(BF)
            dke_ref[...] = _rot_attn_bwd(dk_a[128:] + dk_b[:128], cq_ref[0:128, :], sq_ref[0:128, :]).astype(BF)
            dve_ref[...] = (dv_a[128:] + dv_b[:128]).astype(BF)
            carry_k[...] = dk_b[128:]
            carry_v[...] = dv_b[128:]

        @pl.when(i == steps)
        def _():
            dko_ref[...] = _rot_attn_bwd(carry_k[...], cp_ref[...], sp_ref[...]).astype(BF)
            dvo_ref[...] = carry_v[...].astype(BF)

    two = lambda i: (jnp.minimum(i, steps - 1), 0)
    before = lambda i: (jnp.maximum(2 * i - 1, 0), 0)
    odd = lambda i: (jnp.maximum(i - 1, 0), 0)
    blk = lambda r, w, im: pl.BlockSpec((r, w), im)
    half = jax.ShapeDtypeStruct((S // 2, 128), BF)
    dq, dk_even, dv_even, dk_odd, dv_odd, dsk = pl.pallas_call(
        body, name="attn_bwd", grid=(steps + 1,),
        in_specs=[_SMEM, blk(256, 1024, two), blk(128, 128, before), blk(256, 128, two), blk(128, 128, before),
                  blk(256, 128, two), blk(256, 1024, two), blk(256, 128, two), blk(256, 128, two),
                  blk(128, 128, before), blk(128, 128, before), _ANY],
        out_specs=[blk(256, 1024, two), blk(128, 128, two), blk(128, 128, two), blk(128, 128, odd),
                   blk(128, 128, odd), _fixed((4, PAIRS * 128))],
        out_shape=[jax.ShapeDtypeStruct((S, 1024), BF), half, half, half, half,
                   jax.ShapeDtypeStruct((4, PAIRS * 128), F32)],
        scratch_shapes=[pltpu.VMEM((128, 128), F32), pltpu.VMEM((128, 128), F32)],
        compiler_params=_params(("arbitrary",)),
    )(sinks, aq, ak, ak, av, av, day, ca, sa, ca, sa, after)

    def interleave(even, odd_):
        return jnp.stack([even.reshape(steps, 128, 128), odd_.reshape(steps, 128, 128)], axis=1).reshape(S, 128)

    return dq, interleave(dk_even, dk_odd), interleave(dv_even, dv_odd), dsk


def _ret_bwd(rq, rk, rv, dry, states, dmat, qd, kd, cd, cr, sr, after):
    S = rq.shape[0]
    N = S // RET_CHUNK
    scale = RET_QK_DIM ** -0.5
    cps = RET_CHUNKS_PER_STEP
    steps = N // cps

    def body(q_ref, k_ref, v_ref, dy_ref, st_ref, dm_ref, qd_ref, kd_ref, cd_ref, cr_ref, sr_ref, after_ref,
             out_ref, dstate):
        @pl.when(pl.program_id(0) == 0)
        def _():
            dstate[...] = jnp.zeros_like(dstate)

        for hd in range(RET_HEADS):
            qs = slice(hd * 128, (hd + 1) * 128)
            vs = slice(hd * 256, (hd + 1) * 256)
            dmv = dm_ref[hd]
            free = []
            for ch in range(cps):
                rs = slice(ch * RET_CHUNK, (ch + 1) * RET_CHUNK)
                crv, srv = cr_ref[rs, :], sr_ref[rs, :]
                q = q_ref[rs, qs]
                kf = k_ref[rs, qs]
                k = kf.astype(BF)
                v = v_ref[rs, vs]
                do = dy_ref[rs, vs].astype(F32)
                dob = do.astype(BF)
                doq = (do * qd_ref[hd]).astype(BF)
                a = (_dot_nt(q, k) * dmv).astype(BF)
                da = (_dot_nt(dob, v) * dmv).astype(BF)
                dq = _dot(da, k) + _dot_nt(doq, st_ref[ch, hd])
                out_ref[rs, qs] = _rot_ret_bwd(dq * scale, crv, srv).astype(BF)
                free.append((rs, crv, srv, v, (kf * kd_ref[hd]).astype(BF),
                             _dot_tn(a, dob), _dot_tn(da, q), _dot_tn(q, doq)))
            ds = dstate[hd]
            for rs, crv, srv, v, kdv, dv_free, dk_free, ds_update in reversed(free):
                dsb = ds.astype(BF)
                dv = dv_free + _dot(kdv, dsb)
                dk = dk_free + _dot_nt(v, dsb) * kd_ref[hd]
                ds = ds * cd_ref[hd] + ds_update
                out_ref[rs, 512 + hd * 128:512 + (hd + 1) * 128] = _rot_ret_bwd(dk, crv, srv).astype(BF)
                out_ref[rs, 1024 + hd * 256:1024 + (hd + 1) * 256] = dv.astype(BF)
            dstate[hd] = ds

    rev = lambda w: pl.BlockSpec((cps * RET_CHUNK, w), lambda i: (steps - 1 - i, 0))
    return pl.pallas_call(
        body, name="ret_bwd", grid=(steps,),
        in_specs=[rev(512), rev(512), rev(1024), rev(1024),
                  pl.BlockSpec((cps, RET_HEADS, 128, 256), lambda i: (steps - 1 - i, 0, 0, 0)),
                  _VMEM, _VMEM, _VMEM, _VMEM, rev(128), rev(128), _ANY],
        out_specs=rev(2048),
        out_shape=jax.ShapeDtypeStruct((S, 2048), BF),
        scratch_shapes=[pltpu.VMEM((RET_HEADS, 128, 256), F32)],
        compiler_params=_params(("arbitrary",)),
    )(rq, rk, rv, dry, states, dmat, qd, kd, cd, cr, sr, after)


def _proj_bwd(d_pieces, x, dx1, g1, w_in_t):
    S = x.shape[0]
    tm = 512
    widths = [d.shape[1] for d in d_pieces]
    assert sum(widths) == D_IN
    nd = len(d_pieces)

    def body(*refs):
        d_refs = refs[:nd]
        x_ref, dx1_ref, g_ref, w_ref, gx_ref, dg_ref, db_ref = refs[nd:]

        @pl.when(pl.program_id(0) == 0)
        def _():
            dg_ref[...] = jnp.zeros_like(dg_ref)
            db_ref[...] = jnp.zeros_like(db_ref)

        d_all = jnp.concatenate([ref[...] for ref in d_refs], axis=1)
        dh = _dot(d_all, w_ref[...])
        db_ref[...] += jnp.sum(d_all.astype(F32), axis=0, keepdims=True)
        xv = x_ref[...]
        r = lax.rsqrt(jnp.mean(xv * xv, axis=-1, keepdims=True) + EPS)
        xh = xv * r
        dg_ref[...] += jnp.sum(dh * xh, axis=0, keepdims=True)
        dxh = dh * g_ref[...]
        gx_ref[...] = dx1_ref[...] + r * (dxh - xh * jnp.mean(dxh * xh, axis=-1, keepdims=True))

    return pl.pallas_call(
        body, name="proj_bwd", grid=(S // tm,),
        in_specs=[_rows(tm, n) for n in widths] + [_rows(tm, 1024), _rows(tm, 1024), _fixed((1, 1024)), _VMEM],
        out_specs=[_rows(tm, 1024), _fixed((1, 1024)), _fixed((1, D_IN))],
        out_shape=[jax.ShapeDtypeStruct((S, 1024), F32), jax.ShapeDtypeStruct((1, 1024), F32),
                   jax.ShapeDtypeStruct((1, D_IN), F32)],
        compiler_params=_params(("arbitrary",)),
    )(*d_pieces, x, dx1, g1, w_in_t)


SHARD_ROWS = (1600, 256, 256, 256, 704, 704, 704)


class Group(NamedTuple):
    name: str
    tensors: tuple
    offs: tuple
    block: int
    tile: int

    @property
    def half(self):
        return self.block // 2

    def off(self, t):
        return self.offs[self.tensors.index(t)]


LATE = Group("late", (0,), (0,), 1600, 800)
EARLY = Group("early", (4, 5, 6, 1, 2, 3), (0, 704, 1408, 2112, 2368, 2624), 2880, 480)
GROUP_OF = {t: g for g in (LATE, EARLY) for t in g.tensors}


def _wgrad(a_list, b, t, packed, name):
    S = b.shape[0]
    widths = [a.shape[1] for a in a_list]
    M = sum(widths)
    ts = 1024 if M <= 4096 else 512
    steps = S // ts
    na = len(a_list)
    grp = GROUP_OF[t]
    r, off = SHARD_ROWS[t], grp.off(t)
    assert M == N_CHIPS * r
    segments = [(chip * r, r, chip * grp.block + off) for chip in range(N_CHIPS)]
    stage_rows = r
    n_in = na + 1 + (2 if packed is not None else 0)

    def body(*refs):
        a_refs, b_ref = refs[:na], refs[na]
        g32_ref, gb_ref, acc, stage, sems = refs[n_in:]
        i = pl.program_id(0)

        @pl.when(i == 0)
        def _():
            acc[...] = jnp.zeros_like(acc)

        bv = b_ref[...].astype(BF)
        lo = 0
        for a_ref, n in zip(a_refs, widths):
            acc[lo:lo + n, :] += _dot_tn(a_ref[...].astype(BF), bv)
            lo += n

        @pl.when(i == steps - 1)
        def _():
            for a0, n, p0 in segments:
                c32 = pltpu.make_async_copy(acc.at[a0:a0 + n, :], g32_ref.at[pl.ds(p0, n), :], sems.at[0])
                c32.start()
                stage[0:n, :] = acc[a0:a0 + n, :].astype(BF)
                c16 = pltpu.make_async_copy(stage.at[0:n, :], gb_ref.at[pl.ds(p0, n), :], sems.at[1])
                c16.start()
                c32.wait()
                c16.wait()

    shapes = [jax.ShapeDtypeStruct((N_CHIPS * grp.block, 1024), F32),
              jax.ShapeDtypeStruct((N_CHIPS * grp.block, 1024), BF)]
    operands = list(a_list) + [b]
    in_specs = [_rows(ts, n) for n in widths] + [_rows(ts, 1024)]
    aliases = {}
    if packed is not None:
        operands += list(packed)
        in_specs += [_ANY, _ANY]
        aliases = {na + 1: 0, na + 2: 1}
    return pl.pallas_call(
        body, name=name, grid=(steps,),
        in_specs=in_specs, out_specs=[_ANY, _ANY], out_shape=shapes,
        scratch_shapes=[pltpu.VMEM((M, 1024), F32), pltpu.VMEM((stage_rows, 1024), BF),
                        pltpu.SemaphoreType.DMA((2,))],
        input_output_aliases=aliases,
        compiler_params=_params(("arbitrary",)),
    )(*operands)


def _stage_weights(place, ws, name, after=()):
    nw = len(ws)

    def body(p_ref, *refs):
        for w_ref, o_ref in zip(refs[:nw], refs[nw + len(after):]):
            o_ref[...] = w_ref[...].astype(BF)

    return pl.pallas_call(
        body, name=name,
        grid_spec=pltpu.PrefetchScalarGridSpec(
            num_scalar_prefetch=1, grid=(1,),
            in_specs=[pl.BlockSpec(w.shape, lambda i, p: (0, 0)) for w in ws] + [_ANY] * len(after),
            out_specs=[pl.BlockSpec(w.shape, lambda i, p: (p[0], 0)) for w in ws]),
        out_shape=[jax.ShapeDtypeStruct((N_CHIPS * w.shape[0], 1024), BF) for w in ws],
        compiler_params=_params(("arbitrary",)),
    )(place, *ws, *after)


N_PEERS = 7


def _sum_all(place, g32, others, grp):
    nb = grp.half // grp.tile

    def body(p_ref, g_ref, *refs):
        acc = g_ref[...]
        for r_ref in refs[:N_PEERS]:
            acc = acc + r_ref[...].astype(F32)
        refs[-1][...] = acc

    blk = lambda f: pl.BlockSpec((grp.tile, 1024), f)
    slot = lambda k: blk(lambda i, p: (k * nb + i, 0))
    return pl.pallas_call(
        body, name="sum_all_" + grp.name,
        grid_spec=pltpu.PrefetchScalarGridSpec(
            num_scalar_prefetch=1, grid=(nb,),
            in_specs=[blk(lambda i, p: ((2 * p[0] + p[1]) * nb + i, 0))] + [slot(k) for k in range(N_PEERS)],
            out_specs=blk(lambda i, p: (p[1] * nb + i, 0))),
        out_shape=jax.ShapeDtypeStruct((grp.block, 1024), F32),
        compiler_params=_params(("arbitrary",)),
    )(place, g32, *[others] * N_PEERS)


def _adamw_math(w, gv, m, v):
    mn = ADAM_B1 * m + (1.0 - ADAM_B1) * gv
    vn = ADAM_B2 * v + (1.0 - ADAM_B2) * (gv * gv)
    m_hat = mn / (1.0 - ADAM_B1 ** ADAM_STEP)
    v_hat = vn / (1.0 - ADAM_B2 ** ADAM_STEP)
    return -ADAM_LR * (m_hat / (jnp.sqrt(v_hat) + ADAM_EPS) + ADAM_WD * w), mn, vn


def _adamw(w, g, m, v, name, g_row0=0, after=None):
    r = w.shape[0]
    tr = next((c for c in (352, 320, 64) if r % c == 0 and g_row0 % c == 0), r)
    assert g_row0 % tr == 0
    g0 = g_row0 // tr
    extra = [] if after is None else [after]

    def body(w_ref, g_ref, m_ref, v_ref, *rest):
        go_ref, d_ref, mo_ref, vo_ref = rest[len(extra):]
        gv = g_ref[...]
        go_ref[...] = gv
        d_ref[...], mo_ref[...], vo_ref[...] = _adamw_math(w_ref[...], gv, m_ref[...], v_ref[...])

    return pl.pallas_call(
        body, name=name, grid=(r // tr,),
        in_specs=[_rows(tr, 1024), pl.BlockSpec((tr, 1024), lambda i: (g0 + i, 0)), _rows(tr, 1024), _rows(tr, 1024)]
        + [_ANY] * len(extra),
        out_specs=[_rows(tr, 1024)] * 4,
        out_shape=[jax.ShapeDtypeStruct((r, 1024), F32)] * 4,
        compiler_params=_params(("parallel",)),
    )(w, g, m, v, *extra)


def _place():
    x, y, c = lax.axis_index("x"), lax.axis_index("y"), lax.axis_index("c")
    chips = [(1 - x, y), (x, 1 - y), (1 - x, 1 - y)]
    return x, y, c, chips


_HBM =pl.BlockSpec(memory_space=pltpu.HBM)
_SEM = pl.BlockSpec(memory_space=pltpu.SEMAPHORE)
_DATAFLOW = pltpu.SideEffectType.DATAFLOW_SIDE_EFFECTING


def _exchange(name, arrays, n, plan):
    na = len(arrays)

    def body(*refs):
        pairs = plan(refs[na:2 * na], refs[2 * na], refs[2 * na + 1])
        for send, _ in pairs:
            send.start()
        for send, recv in pairs:
            send.wait_send()
            recv.wait_recv()

    return pl.pallas_call(
        body, name=name, in_specs=[_ANY] * na, out_specs=[_ANY] * na,
        out_shape=[jax.ShapeDtypeStruct(a.shape, a.dtype) for a in arrays],
        scratch_shapes=[pltpu.SemaphoreType.DMA((n,)), pltpu.SemaphoreType.DMA((n,))],
        input_output_aliases={i: i for i in range(na)},
    )(*arrays)


def _exchange_start(name, arrays, n, plan, after=()):
    na = len(arrays)

    def body(*refs):
        for send, _ in plan(refs[:na], refs[na + len(after)], refs[na + len(after) + 1]):
            send.start()
        refs[-1][...] = jnp.zeros_like(refs[-1])

    out = pl.pallas_call(
        body, name=name,
        out_shape=(pltpu.SemaphoreType.DMA((n,)), pltpu.SemaphoreType.DMA((n,)),
                   *[pltpu.HBM(a.shape, a.dtype) for a in arrays], jax.ShapeDtypeStruct((8, LANES), F32)),
        in_specs=[_HBM] * na + [_ANY] * len(after), out_specs=(_SEM, _SEM, *[_HBM] * na, _VMEM),
        input_output_aliases={i: 2 + i for i in range(na)},
        compiler_params=pltpu.CompilerParams(has_side_effects=_DATAFLOW),
    )(*[pltpu.with_memory_space_constraint(a, pltpu.HBM) for a in arrays], *after)
    return (out[0], out[1]), list(out[2:2 + na]), out[-1]


def _exchange_wait(name, sems, in_flight, plan, after=()):
    na = len(in_flight)

    def body(*refs):
        for send, recv in plan(refs[:na], refs[na], refs[na + 1]):
            send.wait_send()
            recv.wait_recv()

    out = pl.pallas_call(
        body, name=name,
        out_shape=tuple(pltpu.HBM(a.shape, a.dtype) for a in in_flight),
        in_specs=[_HBM] * na + [_SEM, _SEM] + [_ANY] * len(after), out_specs=tuple([_HBM] * na),
        input_output_aliases={i: i for i in range(na)},
        compiler_params=pltpu.CompilerParams(has_side_effects=_DATAFLOW),
    )(*in_flight, *sems, *after)
    return list(out)


def _remote(src, dst, send_sems, recv_sems, k, to):
    return pltpu.make_async_remote_copy(src_ref=src, dst_ref=dst, send_sem=send_sems.at[k], recv_sem=recv_sems.at[k],
                                        device_id=to, device_id_type=MESH)


def _half_rows(ref, tid, pj, half):
    h = SHARD_ROWS[tid] // 2
    return ref.at[pl.ds((2 * pj + half) * h, h), :]


def _plan_gather_halves(tids):
    def plan(refs, ss, rs):
        x, y, c, chips = _place()
        pairs = []
        for t, tid in enumerate(tids):
            mine = _half_rows(refs[t], tid, 2 * x + y, c)
            for k, (cx, cy) in enumerate(chips):
                theirs = _half_rows(refs[t], tid, 2 * cx + cy, c)
                pairs.append((_remote(mine, mine, ss, rs, 3 * t + k, (cx, cy, c)),
                              _remote(theirs, theirs, ss, rs, 3 * t + k, (x, y, c))))
        return pairs
    return plan


def _plan_forward_halves(tids):
    def plan(refs, ss, rs):
        x, y, c, chips = _place()
        pairs = []
        for t, tid in enumerate(tids):
            for k, (cx, cy) in enumerate(chips):
                got = _half_rows(refs[t], tid, 2 * cx + cy, c)
                missing = _half_rows(refs[t], tid, 2 * cx + cy, 1 - c)
                pairs.append((_remote(got, got, ss, rs, 3 * t + k, (x, y, 1 - c)),
                              _remote(missing, missing, ss, rs, 3 * t + k, (x, y, 1 - c))))
        return pairs
    return plan


def _plan_to_all(grp):
    def plan(refs, ss, rs):
        x, y, c, chips = _place()
        peers = [(x, y, 1 - c)] + [(cx, cy, pc) for cx, cy in chips for pc in (c, 1 - c)]
        assert len(peers) == N_PEERS
        cps = [_remote(refs[0].at[pl.ds((2 * px + py) * grp.block + pc * grp.half, grp.half), :],
                       refs[1].at[pl.ds(k * grp.half, grp.half), :], ss, rs, k, (px, py, pc))
               for k, (px, py, pc) in enumerate(peers)]
        return [(cp, cp) for cp in cps]
    return plan


def _plan_share_halves(grp, k=0):
    def plan(refs, ss, rs):
        x, y, c, _ = _place()
        mine = refs[0].at[pl.ds(c * grp.half, grp.half), :]
        other = refs[0].at[pl.ds((1 - c) * grp.half, grp.half), :]
        return [(_remote(mine, mine, ss, rs, k, (x, y, 1 - c)), _remote(other, other, ss, rs, k, (x, y, 1 - c)))]
    return plan


def _plan_both(first, n_refs, second):
    def plan(refs, ss, rs):
        return first(refs[:n_refs], ss, rs) + second(refs[n_refs:], ss, rs)
    return plan


def _empty(rows, dtype):
    return lax.empty((rows, 1024), dtype)


def _all_reduce_small(v, shard, grp):
    m_per, n = v.shape
    n_dev = 8

    def body(x_ref, _, out_ref, shard_ref, all_ref, send_sems, recv_sems, local_sem):
        x, y, c, chips = _place()
        (share, shared), = _plan_share_halves(grp, k=N_PEERS)([shard_ref], send_sems, recv_sems)
        me, sibling = (x, y, c), (x, y, 1 - c)

        def rows(px, py, pc):
            return all_ref.at[pl.ds((4 * px + 2 * py + pc) * m_per, m_per), :]

        def copy(k, block, to, src=None):
            return pltpu.make_async_remote_copy(
                src_ref=rows(*block) if src is None else src, dst_ref=rows(*block),
                send_sem=send_sems.at[k], recv_sem=recv_sems.at[k], device_id=to, device_id_type=MESH)

        mine = pltpu.make_async_copy(x_ref, rows(*me), local_sem)
        mine.start()
        first = [copy(0, me, sibling, src=x_ref)]
        first += [copy(1 + k, me, (*chip, c), src=x_ref) for k, chip in enumerate(chips)]
        for cp in first:
            cp.start()
        share.start()
        passed = [copy(4 + k, (*chip, c), sibling) for k, chip in enumerate(chips)]
        for k, chip in enumerate(chips):
            copy(1 + k, (*chip, c), me).wait_recv()
            passed[k].start()
        copy(0, sibling, me).wait_recv()
        for k, chip in enumerate(chips):
            copy(4 + k, (*chip, 1 - c), me).wait_recv()
        for cp in first + passed:
            cp.wait_send()
        mine.wait()
        acc = all_ref[pl.ds(0, m_per), :]
        for d in range(1, n_dev):
            acc = acc + all_ref[pl.ds(d * m_per, m_per), :]
        out_ref[...] = acc
        share.wait_send()
        shared.wait_recv()

    return pl.pallas_call(
        body, name="all_reduce_small_share_" + grp.name,
        in_specs=[_VMEM, _ANY], out_specs=[_VMEM, _ANY],
        out_shape=[jax.ShapeDtypeStruct((m_per, n), F32), jax.ShapeDtypeStruct(shard.shape, shard.dtype)],
        scratch_shapes=[pltpu.VMEM((n_dev * m_per, n), F32), pltpu.SemaphoreType.DMA((N_PEERS + 1,)),
                        pltpu.SemaphoreType.DMA((N_PEERS + 1,)), pltpu.SemaphoreType.DMA],
        input_output_aliases={1: 1},
        compiler_params=pltpu.CompilerParams(has_side_effects=True),
    )(v, shard)


def _tables(S, token):
    pos = jnp.arange(S, dtype=jnp.int32).astype(F32) + token[0, 0]

    def cs(half):
        inv_freq = ROPE_THETA ** (-jnp.arange(half, dtype=F32) / half)
        ang = pos[:, None] * inv_freq[None, :]
        return jnp.cos(ang), jnp.sin(ang)

    c64, s64 = cs(64)
    c32, s32 = cs(32)
    cr = jnp.concatenate([c64, c64], axis=1)
    sr = jnp.concatenate([-s64, s64], axis=1)
    ca = jnp.concatenate([c32, c32, c32, c32], axis=1)
    sa = jnp.concatenate([-s32, s32, -s32, s32], axis=1)
    return cr, sr, ca, sa


def _decays():
    H, C = RET_HEADS, RET_CHUNK
    log_gamma = jnp.log1p(-jnp.exp2(-5.0 - jnp.arange(H, dtype=F32)))
    idx = jnp.arange(C, dtype=F32)
    rel = idx[:, None] - idx[None, :]
    dmat = jnp.where(rel[None] >= 0, jnp.exp(log_gamma[:, None, None] * jnp.maximum(rel, 0.0)[None]), 0.0)
    qd = jnp.exp(log_gamma[:, None] * (idx + 1.0))[:, :, None]
    kd = jnp.exp(log_gamma[:, None] * (C - 1.0 - idx))[:, :, None]
    cd = jnp.exp(log_gamma * C)[:, None, None]
    return dmat, qd, kd, cd


def _local_step(place, x, tgt, ln1_g, b_in, ret_norm_g, attn_sinks, ln2_g, lnf_g, w_in, other_weights, tables, decays):
    cr, sr, ca, sa = tables
    dmat, qd, kd, cd = decays
    forward_weights, landed_weights = other_weights

    h, rq, rk, rv, rg, aq, ak, av, ga, gb = _proj_fwd(x, ln1_g, w_in, b_in, cr, sr, ca, sa)
    ry, states = _ret_fwd(rq, rk, rv, dmat, qd, kd, cd)
    token = forward_weights(ry)
    ay = _attn_fwd(aq, ak, av, attn_sinks + token[0, 0])
    w_ret, w_attn, w_out, w_gate, w_up, w_down = landed_weights(ay)
    u, merged, x1 = _merge_fwd(x, ry, rg, ay, ga, gb, ret_norm_g, w_ret, w_attn, w_out)
    gate, up, h2, act, dx2, loss, dg3 = _ffn_fwd(x1, tgt, ln2_g, lnf_g, w_gate, w_up, w_down)

    dgate, dup, dx1, dg2 = _ffn_bwd(dx2, gate, up, x1, ln2_g, w_gate, w_up, w_down)
    early = _wgrad([act], dx2, 6, None, "wgrad_down")
    early = _wgrad([dgate], h2, 4, early, "wgrad_gate")
    early = _wgrad([dup], h2, 5, early, "wgrad_up")
    dgates, drg, dba, dbb, day, dry, dgr = _merge_bwd(dx1, ga, gb, u, ay, rg, ry, ret_norm_g,
                                                                w_ret, w_attn, w_out)
    early = _wgrad([merged], dx1, 3, early, "wgrad_out")
    early = _wgrad([u], dba, 1, early, "wgrad_ret")
    e32, e16 = _wgrad([ay], dbb, 2, early, "wgrad_attn")

    sems, flight, token = _exchange_start("to_all_start_early", [e16, _empty(N_PEERS * EARLY.half, BF)], N_PEERS,
                                          _plan_to_all(EARLY))
    daq, dak, dav, dsk = _attn_bwd(aq, ak, av, day, attn_sinks, ca, sa, token)
    d_ret = _ret_bwd(rq, rk, rv, dry, states, dmat, qd, kd, cd, cr, sr, token)
    d_proj = [d_ret, drg, daq, dak, dav, dgates]
    l32, l16 = _wgrad(d_proj, h, 0, None, "wgrad_in")
    _, others = _exchange_wait("to_all_wait_early", sems, flight, _plan_to_all(EARLY), after=(l32,))
    share = _plan_share_halves(EARLY, k=N_PEERS)
    sems, flight, token = _exchange_start(
        "to_all_late_share_early_start", [l16, _empty(N_PEERS * LATE.half, BF), _sum_all(place, e32, others, EARLY)],
        N_PEERS + 1, _plan_both(_plan_to_all(LATE), 2, share))
    grad_x, dg1, db = _proj_bwd(d_proj, x, dx1, ln1_g + token[0, 0], w_in)
    early_shard, = _exchange_wait("share_wait_early", sems, flight[2:], share, after=(grad_x,))

    def finish_late(after):
        _, others = _exchange_wait("to_all_wait_late", sems, flight[:2], _plan_to_all(LATE), after=after)
        return _sum_all(place, l32, others, LATE)

    d_sinks = dsk.reshape(2, 2, PAIRS, 128).sum(-1).transpose(0, 2, 1).reshape(1, ATTN_Q_HEADS)
    small = dict(loss=loss, ln1_g=dg1, b_in=db, ret_norm_g=dgr, attn_sinks=d_sinks,
                 ln2_g=dg2, lnf_g=dg3)
    return grad_x, early_shard, token, finish_late, small


SMALL_ROWS = 16


def _pack_small(ln1, ret, ln2, lnf, b_in, sinks, loss):
    def body(a_ref, b_ref, c_ref, d_ref, bi_ref, sk_ref, ls_ref, o_ref):
        o_ref[...] = jnp.zeros_like(o_ref)
        for row, ref in enumerate((a_ref, b_ref, c_ref, d_ref)):
            o_ref[row:row + 1, :] = ref[...]
        for k in range(6):
            o_ref[4 + k:5 + k, :] = bi_ref[:, k * 1024:(k + 1) * 1024]
        o_ref[10:11, 0:D_IN - 6144] = bi_ref[:, 6144:D_IN]
        o_ref[11:12, 0:ATTN_Q_HEADS] = sk_ref[...]
        o_ref[12:13, 0:LANES] = ls_ref[...]

    return pl.pallas_call(
        body, name="pack_small_grads", in_specs=[_VMEM] * 7, out_specs=_VMEM,
        out_shape=jax.ShapeDtypeStruct((SMALL_ROWS, 1024), F32),
    )(ln1.reshape(1, 1024), ret.reshape(1, 1024), ln2.reshape(1, 1024), lnf.reshape(1, 1024),
      b_in.reshape(1, D_IN), sinks.reshape(1, ATTN_Q_HEADS), loss.reshape(1, LANES))


def _adamw_small(pg, ws, ms, vs):
    n_par = len(ws)
    widths = [w.size for w in ws]
    first_row = [0, 1, 2, 3, 4, 11]

    def body(pg_ref, *refs):
        w_refs, m_refs, v_refs = refs[:n_par], refs[n_par:2 * n_par], refs[2 * n_par:3 * n_par]
        out_refs = refs[3 * n_par:]
        for i, width in enumerate(widths):
            for k, c0 in enumerate(range(0, width, 1024)):
                cols = slice(c0, min(c0 + 1024, width))
                gv = pg_ref[first_row[i] + k:first_row[i] + k + 1, 0:cols.stop - c0]
                go_ref, d_ref, mo_ref, vo_ref = out_refs[4 * i:4 * i + 4]
                go_ref[:, cols] = gv
                d_ref[:, cols], mo_ref[:, cols], vo_ref[:, cols] = _adamw_math(
                    w_refs[i][:, cols], gv, m_refs[i][:, cols], v_refs[i][:, cols])

    flat = lambda a: a.reshape(1, a.size)
    res = pl.pallas_call(
        body, name="adamw_small", in_specs=[_VMEM] * (1 + 3 * n_par), out_specs=[_VMEM] * (4 * n_par),
        out_shape=[jax.ShapeDtypeStruct((1, width), F32) for width in widths for _ in range(4)],
    )(pg, *map(flat, ws), *map(flat, ms), *map(flat, vs))
    return [tuple(r.reshape(w.shape) for r in res[4 * i:4 * i + 4]) for i, w in enumerate(ws)]


def kernel(x, ln1_g, w_in, b_in, ret_norm_g, w_ret_out, attn_sinks, w_attn_out, w_out, ln2_g, w_ffn_gate, w_ffn_up, w_ffn_down, lnf_g, loss_target, m_ln1_g, m_w_in, m_b_in, m_ret_norm_g, m_w_ret_out, m_attn_sinks, m_w_attn_out, m_w_out, m_ln2_g, m_w_ffn_gate, m_w_ffn_up, m_w_ffn_down, m_lnf_g, v_ln1_g, v_w_in, v_b_in, v_ret_norm_g, v_w_ret_out, v_attn_sinks, v_w_attn_out, v_w_out, v_ln2_g, v_w_ffn_gate, v_w_ffn_up, v_w_ffn_down, v_lnf_g):
    xi = lax.axis_index("x")
    yi = lax.axis_index("y")
    ci = lax.axis_index("c")
    j = 2 * xi + yi

    place = jnp.stack([j, ci]).astype(jnp.int32)

    names = ["w_in", "w_ret_out", "w_attn_out", "w_out", "w_ffn_gate", "w_ffn_up", "w_ffn_down"]
    transposed = [True, False, False, False, True, True, False]
    view = lambda a, t: a[0].T if t else a[0]
    big_w = [view(a, t) for a, t in zip([w_in, w_ret_out, w_attn_out, w_out, w_ffn_gate, w_ffn_up, w_ffn_down], transposed)]
    big_m = [view(a, t) for a, t in zip([m_w_in, m_w_ret_out, m_w_attn_out, m_w_out, m_w_ffn_gate, m_w_ffn_up, m_w_ffn_down], transposed)]
    big_v = [view(a, t) for a, t in zip([v_w_in, v_w_ret_out, v_w_attn_out, v_w_out, v_w_ffn_gate, v_w_ffn_up, v_w_ffn_down], transposed)]

    staged_in = _stage_weights(place, big_w[:1], "stage_w_in")[0]
    sems, in_flight, token = _exchange_start("gather_in_start", [staged_in], 3, _plan_gather_halves([0]))
    staged = _stage_weights(place, big_w[1:], "stage_others", after=(token,))
    tables = _tables(x.shape[1], token)
    decays = _decays()
    w_in_full = _exchange_wait("gather_in_wait", sems, in_flight, _plan_gather_halves([0]),
                               after=(*staged, *tables, *decays))[0]
    w_in_full = _exchange("gather_in_forward", [w_in_full], 3, _plan_forward_halves([0]))[0]
    rest = list(range(1, 7))
    crossing = _exchange_start("gather_start", staged, 3 * len(rest), _plan_gather_halves(rest), after=(w_in_full,))
    token = crossing[2]
    forwarding = []

    def forward_weights(after):
        arrived = _exchange_wait("gather_wait", crossing[0], crossing[1], _plan_gather_halves(rest), after=(after,))
        forwarding.extend(_exchange_start("forward_start", arrived, 3 * len(rest), _plan_forward_halves(rest)))
        return forwarding[2]

    def landed_weights(after):
        return _exchange_wait("forward_wait", forwarding[0], forwarding[1], _plan_forward_halves(rest), after=(after,))

    other_weights = (forward_weights, landed_weights)

    grad_x, early_shard, token, finish_late, small = _local_step(
        place, x[0], loss_target[0], ln1_g, b_in + token[0, 0], ret_norm_g, attn_sinks, ln2_g, lnf_g.reshape(1, -1),
        w_in_full, other_weights, tables, decays)

    upd = {}

    def update(t, shard, after):
        res = _adamw(big_w[t], shard, big_m[t], big_v[t], "adamw_" + names[t], g_row0=GROUP_OF[t].off(t), after=after)
        upd[names[t]] = tuple((r.T if transposed[t] else r)[None] for r in res)
        return res[0]

    done = [update(t, early_shard, token) for t in EARLY.tensors]
    late_half = finish_late((*done, grad_x))
    pg, late_shard = _all_reduce_small(_pack_small(
        small["ln1_g"], small["ret_norm_g"], small["ln2_g"], small["lnf_g"], small["b_in"],
        small["attn_sinks"], small["loss"]), late_half, LATE)
    update(0, late_shard, None)

    small_names = ["ln1_g", "ret_norm_g", "ln2_g", "lnf_g", "b_in", "attn_sinks"]
    upd.update(zip(small_names, _adamw_small(
        pg, [ln1_g, ret_norm_g, ln2_g, lnf_g, b_in, attn_sinks],
        [m_ln1_g, m_ret_norm_g, m_ln2_g, m_lnf_g, m_b_in, m_attn_sinks],
        [v_ln1_g, v_ret_norm_g, v_ln2_g, v_lnf_g, v_b_in, v_attn_sinks])))

    order_w = ["ln1_g", "w_in", "b_in", "ret_norm_g", "w_ret_out", "attn_sinks", "w_attn_out", "w_out",
               "ln2_g", "w_ffn_gate", "w_ffn_up", "w_ffn_down", "lnf_g"]
    outs = [pg[12, 0], grad_x[None]]
    for k in range(4):
        outs += [upd[n][k] for n in order_w]
    return tuple(outs)
```

```python
from typing import NamedTuple

import jax
import jax.numpy as jnp
from jax import lax
from jax.experimental import pallas as pl
from jax.experimental.pallas import tpu as pltpu

D_MODEL = 1024
SEQ_CHUNK = 128
RET_CHUNK = 256
RET_CHUNKS_PER_STEP = 4
RET_HEADS = 4
RET_QK_DIM = 128
ATTN_Q_HEADS = 16
ATTN_HEAD_DIM = 64
D_FF = 2816
D_IN = 6400
ROPE_THETA = 10000.0
EPS = 1e-6
N_CHIPS = 4

ADAM_LR = 0.001
ADAM_B1 = 0.9
ADAM_B2 = 0.999
ADAM_EPS = 1e-08
ADAM_WD = 0.01
ADAM_STEP = 10

OFF_RQ, OFF_RK, OFF_RV, OFF_RG = 0, 512, 1024, 2048
OFF_AQ, OFF_AK, OFF_AV, OFF_GA, OFF_GB = 3072, 4096, 4224, 4352, 5376

V7X_VMEM_LIMIT = 56 * 1024 * 1024
LANES = 128

BF = jnp.bfloat16
F32 = jnp.float32
KEPT = jnp.bfloat16
MESH = pl.DeviceIdType.MESH

_VMEM = pl.BlockSpec(memory_space=pltpu.VMEM)
_SMEM = pl.BlockSpec(memory_space=pltpu.SMEM)
_ANY = pl.BlockSpec(memory_space=pl.ANY)


def _dot(a, b):
    return jnp.dot(a, b, preferred_element_type=F32)


def _dot_nt(a, b):
    return lax.dot_general(a, b, (((1,), (1,)), ((), ())), preferred_element_type=F32)


def _dot_tn(a, b):
    return lax.dot_general(a, b, (((0,), (0,)), ((), ())), preferred_element_type=F32)


def _sigmoid(x):
    return 0.5 * jnp.tanh(0.5 * x) + 0.5


def _rows(tm, n):
    return pl.BlockSpec((tm, n), lambda i: (i, 0))


def _fixed(shape):
    nd = len(shape)
    return pl.BlockSpec(shape, lambda i: (0,) * nd)


def _params(sem, **kw):
    return pltpu.CompilerParams(dimension_semantics=sem, vmem_limit_bytes=V7X_VMEM_LIMIT, **kw)


def _rot_ret(p, cr, sr):
    return p * cr + pltpu.roll(p, 64, 1) * sr


def _rot_ret_bwd(d, cr, sr):
    return d * cr - pltpu.roll(d, 64, 1) * sr


def _attn_partner(p):
    lane = lax.broadcasted_iota(jnp.int32, p.shape, 1)
    first = (lane % 64) < 32
    return jnp.where(first, pltpu.roll(p, 96, 1), pltpu.roll(p, 32, 1))


def _rot_attn(p, ca, sa):
    return p * ca + _attn_partner(p) * sa


def _rot_attn_bwd(d, ca, sa):
    return d * ca - _attn_partner(d) * sa


def _proj_fwd(x, g1, w_in, b_in, cr, sr, ca, sa, after):
    S = x.shape[0]
    tm = 512
    scale = RET_QK_DIM ** -0.5

    def body(x_ref, g_ref, w_ref, b_ref, cr_ref, sr_ref, ca_ref, sa_ref, after_ref,
             h_ref, rq_ref, rk_ref, rv_ref, rg_ref, aq_ref, ak_ref, av_ref, ga_ref, gb_ref):
        xv = x_ref[...]
        r = lax.rsqrt(jnp.mean(xv * xv, axis=-1, keepdims=True) + EPS)
        h = (xv * r * g_ref[...]).astype(BF)
        h_ref[...] = h

        def piece(lo, n):
            return _dot_nt(h, w_ref[lo:lo + n, :]) + b_ref[:, lo:lo + n]

        crv, srv = cr_ref[...], sr_ref[...]
        pq = piece(OFF_RQ, 512)
        pk = piece(OFF_RK, 512)
        for hd in range(RET_HEADS):
            sl = slice(hd * 128, (hd + 1) * 128)
            rq_ref[:, sl] = (_rot_ret(pq[:, sl], crv, srv) * scale).astype(BF)
            rk_ref[:, sl] = _rot_ret(pk[:, sl], crv, srv)
        rv_ref[...] = piece(OFF_RV, 1024).astype(BF)
        rg_ref[...] = piece(OFF_RG, 1024).astype(KEPT)
        cav, sav = ca_ref[...], sa_ref[...]
        pa = piece(OFF_AQ, 1024)
        for blk in range(8):
            sl = slice(blk * 128, (blk + 1) * 128)
            aq_ref[:, sl] = _rot_attn(pa[:, sl], cav, sav).astype(BF)
        pkv = piece(OFF_AK, 256)
        ak_ref[...] = _rot_attn(pkv[:, :128], cav, sav).astype(BF)
        av_ref[...] = pkv[:, 128:].astype(BF)
        ga_ref[...] = piece(OFF_GA, 1024).astype(KEPT)
        gb_ref[...] = piece(OFF_GB, 1024).astype(KEPT)

    outs = [(D_MODEL, BF), (512, BF), (512, F32), (1024, BF), (1024, KEPT),
            (1024, BF), (128, BF), (128, BF), (1024, KEPT), (1024, KEPT)]
    return pl.pallas_call(
        body, name="proj_fwd", grid=(S // tm,),
        in_specs=[_rows(tm, D_MODEL), _fixed((1, D_MODEL)), _VMEM, _VMEM,
                  _rows(tm, 128), _rows(tm, 128), _rows(tm, 128), _rows(tm, 128), _ANY],
        out_specs=[_rows(tm, n) for n, _ in outs],
        out_shape=[jax.ShapeDtypeStruct((S, n), dt) for n, dt in outs],
        compiler_params=_params(("parallel",)),
    )(x, g1, w_in, b_in, cr, sr, ca, sa, after)


def _ret_fwd(rq, rk, rv, dmat, qd, kd, cd):
    S = rq.shape[0]
    N = S // RET_CHUNK
    cps = RET_CHUNKS_PER_STEP
    rows = cps * RET_CHUNK

    def body(q_ref, k_ref, v_ref, dm_ref, qd_ref, kd_ref, cd_ref, y_ref, st_ref, state):
        @pl.when(pl.program_id(0) == 0)
        def _():
            state[...] = jnp.zeros_like(state)

        for hd in range(RET_HEADS):
            qs = slice(hd * 128, (hd + 1) * 128)
            vs = slice(hd * 256, (hd + 1) * 256)
            free = []
            for ch in range(cps):
                rs = slice(ch * RET_CHUNK, (ch + 1) * RET_CHUNK)
                q = q_ref[rs, qs]
                kf = k_ref[rs, qs]
                v = v_ref[rs, vs]
                a = _dot_nt(q, kf.astype(BF)) * dm_ref[hd]
                kdv = (kf * kd_ref[hd]).astype(BF)
                free.append((rs, q, _dot(a.astype(BF), v), _dot_tn(kdv, v)))
            s = state[hd]
            for ch, (rs, q, inner, update) in enumerate(free):
                sb = s.astype(BF)
                st_ref[ch, hd] = sb
                y_ref[rs, vs] = (inner + _dot(q, sb) * qd_ref[hd]).astype(KEPT)
                s = s * cd_ref[hd] + update
            state[hd] = s

    return pl.pallas_call(
        body, name="ret_fwd", grid=(N // cps,),
        in_specs=[_rows(rows, 512), _rows(rows, 512), _rows(rows, 1024), _VMEM, _VMEM, _VMEM, _VMEM],
        out_specs=[_rows(rows, 1024), pl.BlockSpec((cps, RET_HEADS, 128, 256), lambda i: (i, 0, 0, 0))],
        out_shape=[jax.ShapeDtypeStruct((S, 1024), KEPT),
                   jax.ShapeDtypeStruct((N, RET_HEADS, 128, 256), BF)],
        scratch_shapes=[pltpu.VMEM((RET_HEADS, 128, 256), F32)],
        compiler_params=_params(("arbitrary",)),
    )(rq, rk, rv, dmat, qd, kd, cd)


def _kv_halves(prev_ref, cur_ref):
    kk = jnp.concatenate([prev_ref[...], cur_ref[...]], axis=0).astype(F32)
    return kk, pltpu.roll(kk, 64, 1)


ATTN_FWD_BLOCKS_PER_STEP = 4
PAIRS = 4


def _attn_fold(n):
    t = lax.broadcasted_iota(jnp.int32, (128, PAIRS * 128), 0)
    qi = lax.broadcasted_iota(jnp.int32, (128, PAIRS * 128), 1) & 127
    cur = t <= qi
    cur_f = jnp.where(cur, 1.0, 0.0)
    bias = jnp.where(jnp.logical_or(cur, n > 0), 0.0, -1e30)
    return cur, cur_f, 1.0 - cur_f, bias


def _group_operands(g, kk, kk_sw, vv, vv_sw):
    lo = lax.broadcasted_iota(jnp.int32, (256, 128), 1) < 64
    k_a, k_b = (kk, kk_sw) if g == 0 else (kk_sw, kk)
    v_a, v_b = (vv, vv_sw) if g == 0 else (vv_sw, vv)
    return (jnp.concatenate([jnp.where(lo, k_a, 0.0), jnp.where(lo, 0.0, k_b)], axis=0).astype(BF),
            jnp.concatenate([jnp.where(lo, v_a, 0.0), jnp.where(lo, 0.0, v_b)], axis=0).astype(BF))


def _stack_pairs(ref, g, rows=slice(None)):
    return jnp.concatenate([ref[rows, (4 * g + p) * 128:(4 * g + p + 1) * 128] for p in range(PAIRS)], axis=0)


def _sink_row(sk_ref, g, which):
    return jnp.concatenate([jnp.full((1, 128), sk_ref[0, 8 * g + 2 * p + which], F32) for p in range(PAIRS)], axis=1)


def _fold(full, cur):
    return jnp.where(cur, full[128:], full[:128])


def _attn_probs(raw, fold, sink):
    s = _fold(raw, fold[0]) * (ATTN_HEAD_DIM ** -0.5) + fold[3]
    m = jnp.maximum(jnp.max(s, axis=0, keepdims=True), sink)
    e = jnp.exp(s - m)
    es = jnp.exp(sink - m)
    inv = 1.0 / (jnp.sum(e, axis=0, keepdims=True) + es)
    return e * inv, es * inv


def _unfold(p, fold):
    return jnp.concatenate([p * fold[2], p * fold[1]], axis=0).astype(BF)


def _attn_fwd(aq, ak, av, sinks, after):
    S = aq.shape[0]
    N = S // SEQ_CHUNK
    nb = ATTN_FWD_BLOCKS_PER_STEP

    def body(sk_ref, q_ref, kp_ref, kc_ref, vp_ref, vc_ref, after_ref, o_ref):
        kall, kall_sw = _kv_halves(kp_ref, kc_ref)
        vall, vall_sw = _kv_halves(vp_ref, vc_ref)
        blocks = []
        for sb in range(nb):
            kv = slice(sb * 128, sb * 128 + 256)
            rows = slice(sb * 128, (sb + 1) * 128)
            ops = [_group_operands(g, kall[kv], kall_sw[kv], vall[kv], vall_sw[kv]) for g in range(2)]
            raw = [_dot_nt(ops[g][0], _stack_pairs(q_ref, g, rows)) for g in range(2)]
            blocks.append((rows, ops, raw, _attn_fold(nb * pl.program_id(0) + sb)))
        for rows, ops, raw, fold in blocks:
            for g in range(2):
                probs = [_unfold(_attn_probs(raw[g][w * 256:(w + 1) * 256], fold, _sink_row(sk_ref, g, w))[0], fold)
                         for w in range(2)]
                o = _dot_tn(jnp.concatenate(probs, axis=0), ops[g][1])
                for p in range(PAIRS):
                    o_ref[rows, (4 * g + p) * 128:(4 * g + p + 1) * 128] = o[p * 128:(p + 1) * 128].astype(BF)

    prev = lambda i: (jnp.maximum(nb * i - 1, 0), 0)
    return pl.pallas_call(
        body, name="attn_fwd", grid=(N // nb,),
        in_specs=[_SMEM, _rows(nb * 128, 1024), pl.BlockSpec((128, 128), prev), _rows(nb * 128, 128),
                  pl.BlockSpec((128, 128), prev), _rows(nb * 128, 128), _ANY],
        out_specs=_rows(nb * 128, 1024),
        out_shape=jax.ShapeDtypeStruct((S, 1024), BF),
        compiler_params=_params(("parallel",)),
    )(sinks, aq, ak, ak, av, av, after)


def _group_norm(y):
    mu = jnp.mean(y, axis=-1, keepdims=True)
    yc = y - mu
    rstd = lax.rsqrt(jnp.mean(yc * yc, axis=-1, keepdims=True) + EPS)
    return yc * rstd, rstd


def _merge_fwd(x, ry, rg, ay, ga, gb, g_ret, w_ret, w_attn, w_out):
    S = x.shape[0]
    tm = 512

    def body(x_ref, ry_ref, rg_ref, ay_ref, ga_ref, gb_ref, gr_ref, wr_ref, wa_ref, wo_ref,
             u_ref, mg_ref, x1_ref):
        bb = _dot(ay_ref[...], wa_ref[...])
        for hd in range(RET_HEADS):
            vs = slice(hd * 256, (hd + 1) * 256)
            yn, _ = _group_norm(ry_ref[:, vs].astype(F32))
            rgv = rg_ref[:, vs].astype(F32)
            u_ref[:, vs] = (rgv * _sigmoid(rgv) * (yn * gr_ref[:, vs])).astype(BF)
        gated_b = _sigmoid(gb_ref[...].astype(F32)) * bb
        sa = _sigmoid(ga_ref[...].astype(F32))
        ba = _dot(u_ref[...], wr_ref[...])
        merged = (sa * ba + gated_b).astype(BF)
        mg_ref[...] = merged
        x1_ref[...] = x_ref[...] + _dot(merged, wo_ref[...])

    outs = [BF, BF, F32]
    return pl.pallas_call(
        body, name="merge_fwd", grid=(S // tm,),
        in_specs=[_rows(tm, 1024)] * 6 + [_fixed((1, 1024)), _VMEM, _VMEM, _VMEM],
        out_specs=[_rows(tm, 1024)] * 3,
        out_shape=[jax.ShapeDtypeStruct((S, 1024), dt) for dt in outs],
        compiler_params=_params(("parallel",)),
    )(x, ry, rg, ay, ga, gb, g_ret, w_ret, w_attn, w_out)


def _ffn_fwd(x1, tgt, g2, g3, w_gate, w_up, w_down):
    S = x1.shape[0]
    tm = 256

    def body(x1_ref, t_ref, g2_ref, g3_ref, wg_ref, wu_ref, wd_ref,
             gate_ref, up_ref, h2_ref, act_ref, dx2_ref, loss_ref, dg3_ref):
        @pl.when(pl.program_id(0) == 0)
        def _():
            loss_ref[...] = jnp.zeros_like(loss_ref)
            dg3_ref[...] = jnp.zeros_like(dg3_ref)

        x1v = x1_ref[...]
        r2 = lax.rsqrt(jnp.mean(x1v * x1v, axis=-1, keepdims=True) + EPS)
        h2 = (x1v * r2 * g2_ref[...]).astype(BF)
        h2_ref[...] = h2
        gate = _dot_nt(h2, wg_ref[...])
        up = _dot_nt(h2, wu_ref[...])
        gate_ref[...] = gate
        up_ref[...] = up
        act = (gate * _sigmoid(gate) * up).astype(BF)
        act_ref[...] = act
        x2 = x1v + _dot(act, wd_ref[...])
        r3 = lax.rsqrt(jnp.mean(x2 * x2, axis=-1, keepdims=True) + EPS)
        xh = x2 * r3
        g3v = g3_ref[...]
        err = xh * g3v - t_ref[...]
        loss_ref[...] += 0.5 * jnp.sum(jnp.mean(err * err, axis=-1, keepdims=True), axis=0, keepdims=True)
        dy = err * (1.0 / D_MODEL)
        dg3_ref[...] += jnp.sum(dy * xh, axis=0, keepdims=True)
        dxh = dy * g3v
        dx2_ref[...] = r3 * (dxh - xh * jnp.mean(dxh * xh, axis=-1, keepdims=True))

    return pl.pallas_call(
        body, name="ffn_fwd", grid=(S // tm,),
        in_specs=[_rows(tm, 1024), _rows(tm, 1024), _fixed((1, 1024)), _fixed((1, 1024)), _VMEM, _VMEM, _VMEM],
        out_specs=[_rows(tm, D_FF), _rows(tm, D_FF), _rows(tm, 1024), _rows(tm, D_FF), _rows(tm, 1024),
                   _fixed((1, LANES)), _fixed((1, 1024))],
        out_shape=[jax.ShapeDtypeStruct((S, D_FF), F32), jax.ShapeDtypeStruct((S, D_FF), F32),
                   jax.ShapeDtypeStruct((S, 1024), BF), jax.ShapeDtypeStruct((S, D_FF), BF),
                   jax.ShapeDtypeStruct((S, 1024), F32),
                   jax.ShapeDtypeStruct((1, LANES), F32), jax.ShapeDtypeStruct((1, 1024), F32)],
        compiler_params=_params(("arbitrary",)),
    )(x1, tgt, g2, g3, w_gate, w_up, w_down)


def _ffn_bwd(dx2, gate, up, x1, g2, w_gate, w_up, w_down):
    S = x1.shape[0]
    tm = 256

    def body(dx2_ref, gate_ref, up_ref, x1_ref, g2_ref, wg_ref, wu_ref, wd_ref,
             dgate_ref, dup_ref, dx1_ref, dg2_ref):
        @pl.when(pl.program_id(0) == 0)
        def _():
            dg2_ref[...] = jnp.zeros_like(dg2_ref)

        dx2v = dx2_ref[...]
        dact = _dot_nt(dx2v.astype(BF), wd_ref[...])
        gate = gate_ref[...]
        sg = _sigmoid(gate)
        dgate = (dact * up_ref[...] * (sg * (1.0 + gate * (1.0 - sg)))).astype(BF)
        dup = (dact * (gate * sg)).astype(BF)
        dgate_ref[...] = dgate
        dup_ref[...] = dup
        dh2 = _dot(dgate, wg_ref[...]) + _dot(dup, wu_ref[...])
        x1v = x1_ref[...]
        r2 = lax.rsqrt(jnp.mean(x1v * x1v, axis=-1, keepdims=True) + EPS)
        xh = x1v * r2
        dg2_ref[...] += jnp.sum(dh2 * xh, axis=0, keepdims=True)
        dxh = dh2 * g2_ref[...]
        dx1_ref[...] = dx2v + r2 * (dxh - xh * jnp.mean(dxh * xh, axis=-1, keepdims=True))

    return pl.pallas_call(
        body, name="ffn_bwd", grid=(S // tm,),
        in_specs=[_rows(tm, 1024), _rows(tm, D_FF), _rows(tm, D_FF), _rows(tm, 1024), _fixed((1, 1024)),
                  _VMEM, _VMEM, _VMEM],
        out_specs=[_rows(tm, D_FF), _rows(tm, D_FF), _rows(tm, 1024), _fixed((1, 1024))],
        out_shape=[jax.ShapeDtypeStruct((S, D_FF), BF), jax.ShapeDtypeStruct((S, D_FF), BF),
                   jax.ShapeDtypeStruct((S, 1024), F32), jax.ShapeDtypeStruct((1, 1024), F32)],
        compiler_params=_params(("arbitrary",)),
    )(dx2, gate, up, x1, g2, w_gate, w_up, w_down)


def _merge_bwd(dx1, ga, gb, u, ay, rg, ry, g_ret, w_ret, w_attn, w_out):
    S = dx1.shape[0]
    tm = 512

    def body(dx1_ref, ga_ref, gb_ref, u_ref, ay_ref, rg_ref, ry_ref, gr_ref, wr_ref, wa_ref, wo_ref,
             dgates_ref, drg_ref, dba_ref, dbb_ref, day_ref, dry_ref, dgr_ref):
        @pl.when(pl.program_id(0) == 0)
        def _():
            dgr_ref[...] = jnp.zeros_like(dgr_ref)

        dm = _dot_nt(dx1_ref[...].astype(BF), wo_ref[...])
        ba = _dot(u_ref[...], wr_ref[...])
        bb = _dot(ay_ref[...], wa_ref[...])
        sa = _sigmoid(ga_ref[...].astype(F32))
        sb = _sigmoid(gb_ref[...].astype(F32))
        dga = (dm * ba * (sa * (1.0 - sa))).astype(BF)
        dgb = (dm * bb * (sb * (1.0 - sb))).astype(BF)
        dgates_ref[:, :1024] = dga
        dgates_ref[:, 1024:] = dgb
        dba = (dm * sa).astype(BF)
        dbb = (dm * sb).astype(BF)
        dba_ref[...] = dba
        dbb_ref[...] = dbb
        day_ref[...] = _dot_nt(dbb, wa_ref[...]).astype(BF)
        du = _dot_nt(dba, wr_ref[...])
        for hd in range(RET_HEADS):
            vs = slice(hd * 256, (hd + 1) * 256)
            yn, rstd = _group_norm(ry_ref[:, vs].astype(F32))
            rgv = rg_ref[:, vs].astype(F32)
            sr = _sigmoid(rgv)
            silu = rgv * sr
            duv = du[:, vs]
            grv = gr_ref[:, vs]
            drg = (duv * (yn * grv) * (sr * (1.0 + rgv * (1.0 - sr)))).astype(BF)
            drg_ref[:, vs] = drg
            dyg = duv * silu
            dgr_ref[:, vs] += jnp.sum(dyg * yn, axis=0, keepdims=True)
            dyn = dyg * grv
            dry_ref[:, vs] = (rstd * (dyn - jnp.mean(dyn, axis=-1, keepdims=True)
                                      - yn * jnp.mean(dyn * yn, axis=-1, keepdims=True))).astype(KEPT)

    return pl.pallas_call(
        body, name="merge_bwd", grid=(S // tm,),
        in_specs=[_rows(tm, 1024)] * 7 + [_fixed((1, 1024)), _VMEM, _VMEM, _VMEM],
        out_specs=[_rows(tm, 2048), _rows(tm, 1024), _rows(tm, 1024), _rows(tm, 1024), _rows(tm, 1024),
                   _rows(tm, 1024), _fixed((1, 1024))],
        out_shape=[jax.ShapeDtypeStruct((S, 2048), BF), jax.ShapeDtypeStruct((S, 1024), BF),
                   jax.ShapeDtypeStruct((S, 1024), BF), jax.ShapeDtypeStruct((S, 1024), BF),
                   jax.ShapeDtypeStruct((S, 1024), BF), jax.ShapeDtypeStruct((S, 1024), KEPT),
                   jax.ShapeDtypeStruct((1, 1024), F32)],
        compiler_params=_params(("arbitrary",)),
    )(dx1, ga, gb, u, ay, rg, ry, g_ret, w_ret, w_attn, w_out)


def _attn_bwd(aq, ak, av, day, sinks, ca, sa, after):
    S = aq.shape[0]
    N = S // SEQ_CHUNK
    steps = N // 2

    def body(sk_ref, q_ref, kp_ref, kc_ref, vp_ref, vc_ref, do_ref, cq_ref, sq_ref, cp_ref, sp_ref, after_ref,
             dq_ref, dke_ref, dve_ref, dko_ref, dvo_ref, dsk_ref, carry_k, carry_v):
        i = pl.program_id(0)

        @pl.when(i == 0)
        def _():
            carry_k[...] = jnp.zeros_like(carry_k)
            carry_v[...] = jnp.zeros_like(carry_v)
            dsk_ref[...] = jnp.zeros_like(dsk_ref)

        @pl.when(i < steps)
        def _():
            kall, kall_sw = _kv_halves(kp_ref, kc_ref)
            vall, vall_sw = _kv_halves(vp_ref, vc_ref)
            lo = lax.broadcasted_iota(jnp.int32, (256, 128), 1) < 64
            blocks = []
            for sb in range(2):
                kv = slice(sb * 128, sb * 128 + 256)
                rows = slice(sb * 128, (sb + 1) * 128)
                ops = [_group_operands(g, kall[kv], kall_sw[kv], vall[kv], vall_sw[kv]) for g in range(2)]
                stacked = [(_stack_pairs(q_ref, g, rows), _stack_pairs(do_ref, g, rows)) for g in range(2)]
                raw = [_dot_nt(ops[g][0], stacked[g][0]) for g in range(2)]
                raw_dp = [_dot_nt(ops[g][1], stacked[g][1]) for g in range(2)]
                blocks.append((rows, ops, stacked, raw, raw_dp, _attn_fold(2 * i + sb)))
            grads = []
            d_sink = [jnp.zeros((1, PAIRS * 128), F32) for _ in range(4)]
            for rows, ops, stacked, raw, raw_dp, fold in blocks:
                cur = fold[0]
                cq, sq = cq_ref[rows, :], sq_ref[rows, :]
                dk_groups, dv_groups = [], []
                for g in range(2):
                    qs, dos = stacked[g]
                    ds_parts, p_parts = [], []
                    for which in range(2):
                        half = slice(which * 256, (which + 1) * 256)
                        pr, ps = _attn_probs(raw[g][half], fold, _sink_row(sk_ref, g, which))
                        dp = _fold(raw_dp[g][half], cur)
                        dr = jnp.sum(pr * dp, axis=0, keepdims=True)
                        ds_parts.append(_unfold(pr * (dp - dr) * (ATTN_HEAD_DIM ** -0.5), fold))
                        p_parts.append(_unfold(pr, fold))
                        d_sink[2 * g + which] = d_sink[2 * g + which] - ps * dr
                    ds = jnp.concatenate(ds_parts, axis=0)
                    dq = _dot_tn(ds, ops[g][0])
                    dk_both = _dot(ds, qs)
                    dv_both = _dot(jnp.concatenate(p_parts, axis=0), dos)
                    for p in range(PAIRS):
                        sl = slice((4 * g + p) * 128, (4 * g + p + 1) * 128)
                        dq_ref[rows, sl] = _rot_attn_bwd(dq[p * 128:(p + 1) * 128], cq, sq).astype(BF)
                    tk = jnp.where(lo, dk_both[:256], dk_both[256:])
                    tv = jnp.where(lo, dv_both[:256], dv_both[256:])
                    dk_groups.append(tk + pltpu.roll(tk, 64, 1))
                    dv_groups.append(tv + pltpu.roll(tv, 64, 1))
                grads.append((jnp.where(lo, dk_groups[0], dk_groups[1]), jnp.where(lo, dv_groups[0], dv_groups[1])))
            dsk_ref[...] += jnp.concatenate(d_sink, axis=0)
            (dk_a, dv_a), (dk_b, dv_b) = grads
            dko_ref[...] = _rot_attn_bwd(carry_k[...] + dk_a[:128], cp_ref[...], sp_ref[...]).astype(BF)
            dvo_ref[...] = (carry_v[...] + dv_a[:128]).astype(BF)
            dke_ref[...] = _rot_attn_bwd(dk_a[128:] + dk_b[:128], cq_ref[0:128, :], sq_ref[0:128, :]).astype(BF)
            dve_ref[...] = (dv_a[128:] + dv_b[:128]).astype(BF)
            carry_k[...] = dk_b[128:]
            carry_v[...] = dv_b[128:]

        @pl.when(i == steps)
        def _():
            dko_ref[...] = _rot_attn_bwd(carry_k[...], cp_ref[...], sp_ref[...]).astype(BF)
            dvo_ref[...] = carry_v[...].astype(BF)

    two = lambda i: (jnp.minimum(i, steps - 1), 0)
    before = lambda i: (jnp.maximum(2 * i - 1, 0), 0)
    odd = lambda i: (jnp.maximum(i - 1, 0), 0)
    blk = lambda r, w, im: pl.BlockSpec((r, w), im)
    half = jax.ShapeDtypeStruct((S // 2, 128), BF)
    dq, dk_even, dv_even, dk_odd, dv_odd, dsk = pl.pallas_call(
        body, name="attn_bwd", grid=(steps + 1,),
        in_specs=[_SMEM, blk(256, 1024, two), blk(128, 128, before), blk(256, 128, two), blk(128, 128, before),
                  blk(256, 128, two), blk(256, 1024, two), blk(256, 128, two), blk(256, 128, two),
                  blk(128, 128, before), blk(128, 128, before), _ANY],
        out_specs=[blk(256, 1024, two), blk(128, 128, two), blk(128, 128, two), blk(128, 128, odd),
                   blk(128, 128, odd), _fixed((4, PAIRS * 128))],
        out_shape=[jax.ShapeDtypeStruct((S, 1024), BF), half, half, half, half,
                   jax.ShapeDtypeStruct((4, PAIRS * 128), F32)],
        scratch_shapes=[pltpu.VMEM((128, 128), F32), pltpu.VMEM((128, 128), F32)],
        compiler_params=_params(("arbitrary",)),
    )(sinks, aq, ak, ak, av, av, day, ca, sa, ca, sa, after)

    def interleave(even, odd_):
        return jnp.stack([even.reshape(steps, 128, 128), odd_.reshape(steps, 128, 128)], axis=1).reshape(S, 128)

    return dq, interleave(dk_even, dk_odd), interleave(dv_even, dv_odd), dsk


def _ret_bwd(rq, rk, rv, dry, states, dmat, qd, kd, cd, cr, sr, after):
    S = rq.shape[0]
    N = S // RET_CHUNK
    scale = RET_QK_DIM ** -0.5
    cps = RET_CHUNKS_PER_STEP
    steps = N // cps

    def body(q_ref, k_ref, v_ref, dy_ref, st_ref, dm_ref, qd_ref, kd_ref, cd_ref, cr_ref, sr_ref, after_ref,
             out_ref, dstate):
        @pl.when(pl.program_id(0) == 0)
        def _():
            dstate[...] = jnp.zeros_like(dstate)

        for hd in range(RET_HEADS):
            qs = slice(hd * 128, (hd + 1) * 128)
            vs = slice(hd * 256, (hd + 1) * 256)
            dmv = dm_ref[hd]
            free = []
            for ch in range(cps):
                rs = slice(ch * RET_CHUNK, (ch + 1) * RET_CHUNK)
                crv, srv = cr_ref[rs, :], sr_ref[rs, :]
                q = q_ref[rs, qs]
                kf = k_ref[rs, qs]
                k = kf.astype(BF)
                v = v_ref[rs, vs]
                do = dy_ref[rs, vs].astype(F32)
                dob = do.astype(BF)
                doq = (do * qd_ref[hd]).astype(BF)
                a = (_dot_nt(q, k) * dmv).astype(BF)
                da = (_dot_nt(dob, v) * dmv).astype(BF)
                dq = _dot(da, k) + _dot_nt(doq, st_ref[ch, hd])
                out_ref[rs, qs] = _rot_ret_bwd(dq * scale, crv, srv).astype(BF)
                free.append((rs, crv, srv, v, (kf * kd_ref[hd]).astype(BF),
                             _dot_tn(a, dob), _dot_tn(da, q), _dot_tn(q, doq)))
            ds = dstate[hd]
            for rs, crv, srv, v, kdv, dv_free, dk_free, ds_update in reversed(free):
                dsb = ds.astype(BF)
                dv = dv_free + _dot(kdv, dsb)
                dk = dk_free + _dot_nt(v, dsb) * kd_ref[hd]
                ds = ds * cd_ref[hd] + ds_update
                out_ref[rs, 512 + hd * 128:512 + (hd + 1) * 128] = _rot_ret_bwd(dk, crv, srv).astype(BF)
                out_ref[rs, 1024 + hd * 256:1024 + (hd + 1) * 256] = dv.astype(BF)
            dstate[hd] = ds

    rev = lambda w: pl.BlockSpec((cps * RET_CHUNK, w), lambda i: (steps - 1 - i, 0))
    return pl.pallas_call(
        body, name="ret_bwd", grid=(steps,),
        in_specs=[rev(512), rev(512), rev(1024), rev(1024),
                  pl.BlockSpec((cps, RET_HEADS, 128, 256), lambda i: (steps - 1 - i, 0, 0, 0)),
                  _VMEM, _VMEM, _VMEM, _VMEM, rev(128), rev(128), _ANY],
        out_specs=rev(2048),
        out_shape=jax.ShapeDtypeStruct((S, 2048), BF),
        scratch_shapes=[pltpu.VMEM((RET_HEADS, 128, 256), F32)],
        compiler_params=_params(("arbitrary",)),
    )(rq, rk, rv, dry, states, dmat, qd, kd, cd, cr, sr, after)


def _proj_bwd(d_pieces, x, dx1, g1, w_in_t, after):
    S = x.shape[0]
    tm = 512
    widths = [d.shape[1] for d in d_pieces]
    assert sum(widths) == D_IN
    nd = len(d_pieces)

    def body(*refs):
        d_refs = refs[:nd]
        x_ref, dx1_ref, g_ref, w_ref, after_ref, gx_ref, dg_ref, db_ref = refs[nd:]

        @pl.when(pl.program_id(0) == 0)
        def _():
            dg_ref[...] = jnp.zeros_like(dg_ref)
            db_ref[...] = jnp.zeros_like(db_ref)

        d_all = jnp.concatenate([ref[...] for ref in d_refs], axis=1)
        dh = _dot(d_all, w_ref[...])
        db_ref[...] += jnp.sum(d_all.astype(F32), axis=0, keepdims=True)
        xv = x_ref[...]
        r = lax.rsqrt(jnp.mean(xv * xv, axis=-1, keepdims=True) + EPS)
        xh = xv * r
        dg_ref[...] += jnp.sum(dh * xh, axis=0, keepdims=True)
        dxh = dh * g_ref[...]
        gx_ref[...] = dx1_ref[...] + r * (dxh - xh * jnp.mean(dxh * xh, axis=-1, keepdims=True))

    return pl.pallas_call(
        body, name="proj_bwd", grid=(S // tm,),
        in_specs=[_rows(tm, n) for n in widths] + [_rows(tm, 1024), _rows(tm, 1024), _fixed((1, 1024)), _VMEM, _ANY],
        out_specs=[_rows(tm, 1024), _fixed((1, 1024)), _fixed((1, D_IN))],
        out_shape=[jax.ShapeDtypeStruct((S, 1024), F32), jax.ShapeDtypeStruct((1, 1024), F32),
                   jax.ShapeDtypeStruct((1, D_IN), F32)],
        compiler_params=_params(("arbitrary",)),
    )(*d_pieces, x, dx1, g1, w_in_t, after)


SHARD_ROWS = (1600, 256, 256, 256, 704, 704, 704)


class Group(NamedTuple):
    name: str
    tensors: tuple
    offs: tuple
    block: int
    tile: int

    @property
    def half(self):
        return self.block // 2

    def off(self, t):
        return self.offs[self.tensors.index(t)]


LATE = Group("late", (0,), (0,), 1600, 800)
EARLY = Group("early", (4, 5, 6, 1, 2, 3), (0, 704, 1408, 2112, 2368, 2624), 2880, 480)
GROUP_OF = {t: g for g in (LATE, EARLY) for t in g.tensors}


def _wgrad(a_list, b, t, packed, name):
    S = b.shape[0]
    widths = [a.shape[1] for a in a_list]
    M = sum(widths)
    ts = 1024 if M <= 4096 else 512
    steps = S // ts
    na = len(a_list)
    grp = GROUP_OF[t]
    r, off = SHARD_ROWS[t], grp.off(t)
    assert M == N_CHIPS * r
    segments = [(chip * r, r, chip * grp.block + off) for chip in range(N_CHIPS)]
    stage_rows = r
    n_in = na + 1 + (2 if packed is not None else 0)

    def body(*refs):
        a_refs, b_ref = refs[:na], refs[na]
        g32_ref, gb_ref, acc, stage, sems = refs[n_in:]
        i = pl.program_id(0)

        @pl.when(i == 0)
        def _():
            acc[...] = jnp.zeros_like(acc)

        bv = b_ref[...].astype(BF)
        lo = 0
        for a_ref, n in zip(a_refs, widths):
            acc[lo:lo + n, :] += _dot_tn(a_ref[...].astype(BF), bv)
            lo += n

        @pl.when(i == steps - 1)
        def _():
            for a0, n, p0 in segments:
                c32 = pltpu.make_async_copy(acc.at[a0:a0 + n, :], g32_ref.at[pl.ds(p0, n), :], sems.at[0])
                c32.start()
                stage[0:n, :] = acc[a0:a0 + n, :].astype(BF)
                c16 = pltpu.make_async_copy(stage.at[0:n, :], gb_ref.at[pl.ds(p0, n), :], sems.at[1])
                c16.start()
                c32.wait()
                c16.wait()

    shapes = [jax.ShapeDtypeStruct((N_CHIPS * grp.block, 1024), F32),
              jax.ShapeDtypeStruct((N_CHIPS * grp.block, 1024), BF)]
    operands = list(a_list) + [b]
    in_specs = [_rows(ts, n) for n in widths] + [_rows(ts, 1024)]
    aliases = {}
    if packed is not None:
        operands += list(packed)
        in_specs += [_ANY, _ANY]
        aliases = {na + 1: 0, na + 2: 1}
    return pl.pallas_call(
        body, name=name, grid=(steps,),
        in_specs=in_specs, out_specs=[_ANY, _ANY], out_shape=shapes,
        scratch_shapes=[pltpu.VMEM((M, 1024), F32), pltpu.VMEM((stage_rows, 1024), BF),
                        pltpu.SemaphoreType.DMA((2,))],
        input_output_aliases=aliases,
        compiler_params=_params(("arbitrary",)),
    )(*operands)


def _stage_weights(place, ws, name, after=()):
    nw = len(ws)

    def body(p_ref, *refs):
        for w_ref, o_ref in zip(refs[:nw], refs[nw + len(after):]):
            o_ref[...] = w_ref[...].astype(BF)

    return pl.pallas_call(
        body, name=name,
        grid_spec=pltpu.PrefetchScalarGridSpec(
            num_scalar_prefetch=1, grid=(1,),
            in_specs=[pl.BlockSpec(w.shape, lambda i, p: (0, 0)) for w in ws] + [_ANY] * len(after),
            out_specs=[pl.BlockSpec(w.shape, lambda i, p: (p[0], 0)) for w in ws]),
        out_shape=[jax.ShapeDtypeStruct((N_CHIPS * w.shape[0], 1024), BF) for w in ws],
        compiler_params=_params(("arbitrary",)),
    )(place, *ws, *after)


N_PEERS = 7


def _sum_all(place, g32, others, grp):
    nb = grp.half // grp.tile

    def body(p_ref, g_ref, *refs):
        acc = g_ref[...]
        for r_ref in refs[:N_PEERS]:
            acc = acc + r_ref[...].astype(F32)
        refs[-1][...] = acc

    blk = lambda f: pl.BlockSpec((grp.tile, 1024), f)
    slot = lambda k: blk(lambda i, p: (k * nb + i, 0))
    return pl.pallas_call(
        body, name="sum_all_" + grp.name,
        grid_spec=pltpu.PrefetchScalarGridSpec(
            num_scalar_prefetch=1, grid=(nb,),
            in_specs=[blk(lambda i, p: ((2 * p[0] + p[1]) * nb + i, 0))] + [slot(k) for k in range(N_PEERS)],
            out_specs=blk(lambda i, p: (p[1] * nb + i, 0))),
        out_shape=jax.ShapeDtypeStruct((grp.block, 1024), F32),
        compiler_params=_params(("arbitrary",)),
    )(place, g32, *[others] * N_PEERS)


def _adamw_math(w, gv, m, v):
    mn = ADAM_B1 * m + (1.0 - ADAM_B1) * gv
    vn = ADAM_B2 * v + (1.0 - ADAM_B2) * (gv * gv)
    m_hat = mn / (1.0 - ADAM_B1 ** ADAM_STEP)
    v_hat = vn / (1.0 - ADAM_B2 ** ADAM_STEP)
    return -ADAM_LR * (m_hat / (jnp.sqrt(v_hat) + ADAM_EPS) + ADAM_WD * w), mn, vn


def _adamw(w, g, m, v, name, g_row0=0, after=None):
    r = w.shape[0]
    tr = next((c for c in (352, 320, 64) if r % c == 0 and g_row0 % c == 0), r)
    assert g_row0 % tr == 0
    g0 = g_row0 // tr
    extra = [] if after is None else [after]

    def body(w_ref, g_ref, m_ref, v_ref, *rest):
        go_ref, d_ref, mo_ref, vo_ref = rest[len(extra):]
        gv = g_ref[...]
        go_ref[...] = gv
        d_ref[...], mo_ref[...], vo_ref[...] = _adamw_math(w_ref[...], gv, m_ref[...], v_ref[...])

    return pl.pallas_call(
        body, name=name, grid=(r // tr,),
        in_specs=[_rows(tr, 1024), pl.BlockSpec((tr, 1024), lambda i: (g0 + i, 0)), _rows(tr, 1024), _rows(tr, 1024)]
        + [_ANY] * len(extra),
        out_specs=[_rows(tr, 1024)] * 4,
        out_shape=[jax.ShapeDtypeStruct((r, 1024), F32)] * 4,
        compiler_params=_params(("parallel",)),
    )(w, g, m, v, *extra)


def _place():
    x, y, c = lax.axis_index("x"), lax.axis_index("y"), lax.axis_index("c")
    chips = [(1 - x, y), (x, 1 - y), (1 - x, 1 - y)]
    return x, y, c, chips


_HBM =pl.BlockSpec(memory_space=pltpu.HBM)
_SEM = pl.BlockSpec(memory_space=pltpu.SEMAPHORE)
_DATAFLOW = pltpu.SideEffectType.DATAFLOW_SIDE_EFFECTING


def _exchange(name, arrays, n, plan):
    na = len(arrays)

    def body(*refs):
        pairs = plan(refs[na:2 * na], refs[2 * na], refs[2 * na + 1])
        for send, _ in pairs:
            send.start()
        for send, recv in pairs:
            send.wait_send()
            recv.wait_recv()

    return pl.pallas_call(
        body, name=name, in_specs=[_ANY] * na, out_specs=[_ANY] * na,
        out_shape=[jax.ShapeDtypeStruct(a.shape, a.dtype) for a in arrays],
        scratch_shapes=[pltpu.SemaphoreType.DMA((n,)), pltpu.SemaphoreType.DMA((n,))],
        input_output_aliases={i: i for i in range(na)},
    )(*arrays)


def _exchange_start(name, arrays, n, plan, after=()):
    na = len(arrays)

    def body(*refs):
        for send, _ in plan(refs[:na], refs[na + len(after)], refs[na + len(after) + 1]):
            send.start()
        refs[-1][...] = jnp.zeros_like(refs[-1])

    out = pl.pallas_call(
        body, name=name,
        out_shape=(pltpu.SemaphoreType.DMA((n,)), pltpu.SemaphoreType.DMA((n,)),
                   *[pltpu.HBM(a.shape, a.dtype) for a in arrays], jax.ShapeDtypeStruct((8, LANES), F32)),
        in_specs=[_HBM] * na + [_ANY] * len(after), out_specs=(_SEM, _SEM, *[_HBM] * na, _VMEM),
        input_output_aliases={i: 2 + i for i in range(na)},
        compiler_params=pltpu.CompilerParams(has_side_effects=_DATAFLOW),
    )(*[pltpu.with_memory_space_constraint(a, pltpu.HBM) for a in arrays], *after)
    return (out[0], out[1]), list(out[2:2 + na]), out[-1]


def _exchange_wait(name, sems, in_flight, plan, after=()):
    na = len(in_flight)

    def body(*refs):
        for send, recv in plan(refs[:na], refs[na], refs[na + 1]):
            send.wait_send()
            recv.wait_recv()

    out = pl.pallas_call(
        body, name=name,
        out_shape=tuple(pltpu.HBM(a.shape, a.dtype) for a in in_flight),
        in_specs=[_HBM] * na + [_SEM, _SEM] + [_ANY] * len(after), out_specs=tuple([_HBM] * na),
        input_output_aliases={i: i for i in range(na)},
        compiler_params=pltpu.CompilerParams(has_side_effects=_DATAFLOW),
    )(*in_flight, *sems, *after)
    return list(out)


def _remote(src, dst, send_sems, recv_sems, k, to):
    return pltpu.make_async_remote_copy(src_ref=src, dst_ref=dst, send_sem=send_sems.at[k], recv_sem=recv_sems.at[k],
                                        device_id=to, device_id_type=MESH)


def _half_rows(ref, tid, pj, half):
    h = SHARD_ROWS[tid] // 2
    return ref.at[pl.ds((2 * pj + half) * h, h), :]


def _plan_gather_halves(tids):
    def plan(refs, ss, rs):
        x, y, c, chips = _place()
        pairs = []
        for t, tid in enumerate(tids):
            mine = _half_rows(refs[t], tid, 2 * x + y, c)
            for k, (cx, cy) in enumerate(chips):
                theirs = _half_rows(refs[t], tid, 2 * cx + cy, c)
                pairs.append((_remote(mine, mine, ss, rs, 3 * t + k, (cx, cy, c)),
                              _remote(theirs, theirs, ss, rs, 3 * t + k, (x, y, c))))
        return pairs
    return plan


def _plan_forward_halves(tids):
    def plan(refs, ss, rs):
        x, y, c, chips = _place()
        pairs = []
        for t, tid in enumerate(tids):
            for k, (cx, cy) in enumerate(chips):
                got = _half_rows(refs[t], tid, 2 * cx + cy, c)
                missing = _half_rows(refs[t], tid, 2 * cx + cy, 1 - c)
                pairs.append((_remote(got, got, ss, rs, 3 * t + k, (x, y, 1 - c)),
                              _remote(missing, missing, ss, rs, 3 * t + k, (x, y, 1 - c))))
        return pairs
    return plan


def _plan_to_all(grp):
    def plan(refs, ss, rs):
        x, y, c, chips = _place()
        peers = [(x, y, 1 - c)] + [(cx, cy, pc) for cx, cy in chips for pc in (c, 1 - c)]
        assert len(peers) == N_PEERS
        cps = [_remote(refs[0].at[pl.ds((2 * px + py) * grp.block + pc * grp.half, grp.half), :],
                       refs[1].at[pl.ds(k * grp.half, grp.half), :], ss, rs, k, (px, py, pc))
               for k, (px, py, pc) in enumerate(peers)]
        return [(cp, cp) for cp in cps]
    return plan


def _plan_share_halves(grp, k=0):
    def plan(refs, ss, rs):
        x, y, c, _ = _place()
        mine = refs[0].at[pl.ds(c * grp.half, grp.half), :]
        other = refs[0].at[pl.ds((1 - c) * grp.half, grp.half), :]
        return [(_remote(mine, mine, ss, rs, k, (x, y, 1 - c)), _remote(other, other, ss, rs, k, (x, y, 1 - c)))]
    return plan


def _plan_both(first, n_refs, second):
    def plan(refs, ss, rs):
        return first(refs[:n_refs], ss, rs) + second(refs[n_refs:], ss, rs)
    return plan


def _empty(rows, dtype):
    return lax.empty((rows, 1024), dtype)


def _all_reduce_small(v, shard, grp):
    m_per, n = v.shape
    n_dev = 8

    def body(x_ref, _, out_ref, shard_ref, all_ref, send_sems, recv_sems, local_sem):
        x, y, c, chips = _place()
        (share, shared), = _plan_share_halves(grp, k=N_PEERS)([shard_ref], send_sems, recv_sems)
        me, sibling = (x, y, c), (x, y, 1 - c)

        def rows(px, py, pc):
            return all_ref.at[pl.ds((4 * px + 2 * py + pc) * m_per, m_per), :]

        def copy(k, block, to, src=None):
            return pltpu.make_async_remote_copy(
                src_ref=rows(*block) if src is None else src, dst_ref=rows(*block),
                send_sem=send_sems.at[k], recv_sem=recv_sems.at[k], device_id=to, device_id_type=MESH)

        mine = pltpu.make_async_copy(x_ref, rows(*me), local_sem)
        mine.start()
        first = [copy(0, me, sibling, src=x_ref)]
        first += [copy(1 + k, me, (*chip, c), src=x_ref) for k, chip in enumerate(chips)]
        for cp in first:
            cp.start()
        share.start()
        passed = [copy(4 + k, (*chip, c), sibling) for k, chip in enumerate(chips)]
        for k, chip in enumerate(chips):
            copy(1 + k, (*chip, c), me).wait_recv()
            passed[k].start()
        copy(0, sibling, me).wait_recv()
        for k, chip in enumerate(chips):
            copy(4 + k, (*chip, 1 - c), me).wait_recv()
        for cp in first + passed:
            cp.wait_send()
        mine.wait()
        acc = all_ref[pl.ds(0, m_per), :]
        for d in range(1, n_dev):
            acc = acc + all_ref[pl.ds(d * m_per, m_per), :]
        out_ref[...] = acc
        share.wait_send()
        shared.wait_recv()

    return pl.pallas_call(
        body, name="all_reduce_small_share_" + grp.name,
        in_specs=[_VMEM, _ANY], out_specs=[_VMEM, _ANY],
        out_shape=[jax.ShapeDtypeStruct((m_per, n), F32), jax.ShapeDtypeStruct(shard.shape, shard.dtype)],
        scratch_shapes=[pltpu.VMEM((n_dev * m_per, n), F32), pltpu.SemaphoreType.DMA((N_PEERS + 1,)),
                        pltpu.SemaphoreType.DMA((N_PEERS + 1,)), pltpu.SemaphoreType.DMA],
        input_output_aliases={1: 1},
        compiler_params=pltpu.CompilerParams(has_side_effects=True),
    )(v, shard)


def _tables(S, token):
    pos = jnp.arange(S, dtype=jnp.int32).astype(F32) + token[0, 0]

    def cs(half):
        inv_freq = ROPE_THETA ** (-jnp.arange(half, dtype=F32) / half)
        ang = pos[:, None] * inv_freq[None, :]
        return jnp.cos(ang), jnp.sin(ang)

    c64, s64 = cs(64)
    c32, s32 = cs(32)
    cr = jnp.concatenate([c64, c64], axis=1)
    sr = jnp.concatenate([-s64, s64], axis=1)
    ca = jnp.concatenate([c32, c32, c32, c32], axis=1)
    sa = jnp.concatenate([-s32, s32, -s32, s32], axis=1)
    return cr, sr, ca, sa


def _decays():
    H, C = RET_HEADS, RET_CHUNK
    log_gamma = jnp.log1p(-jnp.exp2(-5.0 - jnp.arange(H, dtype=F32)))
    idx = jnp.arange(C, dtype=F32)
    rel = idx[:, None] - idx[None, :]
    dmat = jnp.where(rel[None] >= 0, jnp.exp(log_gamma[:, None, None] * jnp.maximum(rel, 0.0)[None]), 0.0)
    qd = jnp.exp(log_gamma[:, None] * (idx + 1.0))[:, :, None]
    kd = jnp.exp(log_gamma[:, None] * (C - 1.0 - idx))[:, :, None]
    cd = jnp.exp(log_gamma * C)[:, None, None]
    return dmat, qd, kd, cd


def _local_step(place, x, tgt, ln1_g, b_in, ret_norm_g, attn_sinks, ln2_g, lnf_g, w_in, other_weights, tables, decays,
                after):
    cr, sr, ca, sa = tables
    dmat, qd, kd, cd = decays
    forward_weights, landed_weights = other_weights

    h, rq, rk, rv, rg, aq, ak, av, ga, gb = _proj_fwd(x, ln1_g, w_in, b_in, cr, sr, ca, sa, after)
    ry, states = _ret_fwd(rq, rk, rv, dmat, qd, kd, cd)
    token = forward_weights(ry)
    ay = _attn_fwd(aq, ak, av, attn_sinks, token)
    w_ret, w_attn, w_out, w_gate, w_up, w_down = landed_weights(ay)
    u, merged, x1 = _merge_fwd(x, ry, rg, ay, ga, gb, ret_norm_g, w_ret, w_attn, w_out)
    gate, up, h2, act, dx2, loss, dg3 = _ffn_fwd(x1, tgt, ln2_g, lnf_g, w_gate, w_up, w_down)

    dgate, dup, dx1, dg2 = _ffn_bwd(dx2, gate, up, x1, ln2_g, w_gate, w_up, w_down)
    early = _wgrad([act], dx2, 6, None, "wgrad_down")
    early = _wgrad([dgate], h2, 4, early, "wgrad_gate")
    early = _wgrad([dup], h2, 5, early, "wgrad_up")
    dgates, drg, dba, dbb, day, dry, dgr = _merge_bwd(dx1, ga, gb, u, ay, rg, ry, ret_norm_g,
                                                                w_ret, w_attn, w_out)
    early = _wgrad([merged], dx1, 3, early, "wgrad_out")
    early = _wgrad([u], dba, 1, early, "wgrad_ret")
    e32, e16 = _wgrad([ay], dbb, 2, early, "wgrad_attn")

    sems, flight, token = _exchange_start("to_all_start_early", [e16, _empty(N_PEERS * EARLY.half, BF)], N_PEERS,
                                          _plan_to_all(EARLY))
    daq, dak, dav, dsk = _attn_bwd(aq, ak, av, day, attn_sinks, ca, sa, token)
    d_ret = _ret_bwd(rq, rk, rv, dry, states, dmat, qd, kd, cd, cr, sr, token)
    d_proj = [d_ret, drg, daq, dak, dav, dgates]
    l32, l16 = _wgrad(d_proj, h, 0, None, "wgrad_in")
    _, others = _exchange_wait("to_all_wait_early", sems, flight, _plan_to_all(EARLY), after=(l32,))
    share = _plan_share_halves(EARLY, k=N_PEERS)
    sems, flight, token = _exchange_start(
        "to_all_late_share_early_start", [l16, _empty(N_PEERS * LATE.half, BF), _sum_all(place, e32, others, EARLY)],
        N_PEERS + 1, _plan_both(_plan_to_all(LATE), 2, share))
    grad_x, dg1, db = _proj_bwd(d_proj, x, dx1, ln1_g, w_in, token)
    early_shard, = _exchange_wait("share_wait_early", sems, flight[2:], share, after=(grad_x,))

    def finish_late(after):
        _, others = _exchange_wait("to_all_wait_late", sems, flight[:2], _plan_to_all(LATE), after=after)
        return _sum_all(place, l32, others, LATE)

    d_sinks = dsk.reshape(2, 2, PAIRS, 128).sum(-1).transpose(0, 2, 1).reshape(1, ATTN_Q_HEADS)
    small = dict(loss=loss, ln1_g=dg1, b_in=db, ret_norm_g=dgr, attn_sinks=d_sinks,
                 ln2_g=dg2, lnf_g=dg3)
    return grad_x, early_shard, token, finish_late, small


SMALL_ROWS = 16


def _pack_small(ln1, ret, ln2, lnf, b_in, sinks, loss):
    def body(a_ref, b_ref, c_ref, d_ref, bi_ref, sk_ref, ls_ref, o_ref):
        o_ref[...] = jnp.zeros_like(o_ref)
        for row, ref in enumerate((a_ref, b_ref, c_ref, d_ref)):
            o_ref[row:row + 1, :] = ref[...]
        for k in range(6):
            o_ref[4 + k:5 + k, :] = bi_ref[:, k * 1024:(k + 1) * 1024]
        o_ref[10:11, 0:D_IN - 6144] = bi_ref[:, 6144:D_IN]
        o_ref[11:12, 0:ATTN_Q_HEADS] = sk_ref[...]
        o_ref[12:13, 0:LANES] = ls_ref[...]

    return pl.pallas_call(
        body, name="pack_small_grads", in_specs=[_VMEM] * 7, out_specs=_VMEM,
        out_shape=jax.ShapeDtypeStruct((SMALL_ROWS, 1024), F32),
    )(ln1.reshape(1, 1024), ret.reshape(1, 1024), ln2.reshape(1, 1024), lnf.reshape(1, 1024),
      b_in.reshape(1, D_IN), sinks.reshape(1, ATTN_Q_HEADS), loss.reshape(1, LANES))


def _adamw_small(pg, ws, ms, vs):
    n_par = len(ws)
    widths = [w.size for w in ws]
    first_row = [0, 1, 2, 3, 4, 11]

    def body(pg_ref, *refs):
        w_refs, m_refs, v_refs = refs[:n_par], refs[n_par:2 * n_par], refs[2 * n_par:3 * n_par]
        out_refs = refs[3 * n_par:]
        for i, width in enumerate(widths):
            for k, c0 in enumerate(range(0, width, 1024)):
                cols = slice(c0, min(c0 + 1024, width))
                gv = pg_ref[first_row[i] + k:first_row[i] + k + 1, 0:cols.stop - c0]
                go_ref, d_ref, mo_ref, vo_ref = out_refs[4 * i:4 * i + 4]
                go_ref[:, cols] = gv
                d_ref[:, cols], mo_ref[:, cols], vo_ref[:, cols] = _adamw_math(
                    w_refs[i][:, cols], gv, m_refs[i][:, cols], v_refs[i][:, cols])

    flat = lambda a: a.reshape(1, a.size)
    res = pl.pallas_call(
        body, name="adamw_small", in_specs=[_VMEM] * (1 + 3 * n_par), out_specs=[_VMEM] * (4 * n_par),
        out_shape=[jax.ShapeDtypeStruct((1, width), F32) for width in widths for _ in range(4)],
    )(pg, *map(flat, ws), *map(flat, ms), *map(flat, vs))
    return [tuple(r.reshape(w.shape) for r in res[4 * i:4 * i + 4]) for i, w in enumerate(ws)]


def kernel(x, ln1_g, w_in, b_in, ret_norm_g, w_ret_out, attn_sinks, w_attn_out, w_out, ln2_g, w_ffn_gate, w_ffn_up, w_ffn_down, lnf_g, loss_target, m_ln1_g, m_w_in, m_b_in, m_ret_norm_g, m_w_ret_out, m_attn_sinks, m_w_attn_out, m_w_out, m_ln2_g, m_w_ffn_gate, m_w_ffn_up, m_w_ffn_down, m_lnf_g, v_ln1_g, v_w_in, v_b_in, v_ret_norm_g, v_w_ret_out, v_attn_sinks, v_w_attn_out, v_w_out, v_ln2_g, v_w_ffn_gate, v_w_ffn_up, v_w_ffn_down, v_lnf_g):
    xi = lax.axis_index("x")
    yi = lax.axis_index("y")
    ci = lax.axis_index("c")
    j = 2 * xi + yi

    place = jnp.stack([j, ci]).astype(jnp.int32)

    names = ["w_in", "w_ret_out", "w_attn_out", "w_out", "w_ffn_gate", "w_ffn_up", "w_ffn_down"]
    transposed = [True, False, False, False, True, True, False]
    view = lambda a, t: a[0].T if t else a[0]
    big_w = [view(a, t) for a, t in zip([w_in, w_ret_out, w_attn_out, w_out, w_ffn_gate, w_ffn_up, w_ffn_down], transposed)]
    big_m = [view(a, t) for a, t in zip([m_w_in, m_w_ret_out, m_w_attn_out, m_w_out, m_w_ffn_gate, m_w_ffn_up, m_w_ffn_down], transposed)]
    big_v = [view(a, t) for a, t in zip([v_w_in, v_w_ret_out, v_w_attn_out, v_w_out, v_w_ffn_gate, v_w_ffn_up, v_w_ffn_down], transposed)]

    staged_in = _stage_weights(place, big_w[:1], "stage_w_in")[0]
    sems, in_flight, token = _exchange_start("gather_in_start", [staged_in], 3, _plan_gather_halves([0]))
    staged = _stage_weights(place, big_w[1:], "stage_others", after=(token,))
    tables = _tables(x.shape[1], token)
    decays = _decays()
    w_in_full = _exchange_wait("gather_in_wait", sems, in_flight, _plan_gather_halves([0]),
                               after=(*staged, *tables, *decays))[0]
    w_in_full = _exchange("gather_in_forward", [w_in_full], 3, _plan_forward_halves([0]))[0]
    rest = list(range(1, 7))
    crossing = _exchange_start("gather_start", staged, 3 * len(rest), _plan_gather_halves(rest), after=(w_in_full,))
    token = crossing[2]
    forwarding = []

    def forward_weights(after):
        arrived = _exchange_wait("gather_wait", crossing[0], crossing[1], _plan_gather_halves(rest), after=(after,))
        forwarding.extend(_exchange_start("forward_start", arrived, 3 * len(rest), _plan_forward_halves(rest)))
        return forwarding[2]

    def landed_weights(after):
        return _exchange_wait("forward_wait", forwarding[0], forwarding[1], _plan_forward_halves(rest), after=(after,))

    other_weights = (forward_weights, landed_weights)

    grad_x, early_shard, token, finish_late, small = _local_step(
        place, x[0], loss_target[0], ln1_g, b_in, ret_norm_g, attn_sinks, ln2_g, lnf_g.reshape(1, -1),
        w_in_full, other_weights, tables, decays, token)

    upd = {}

    def update(t, shard, after):
        res = _adamw(big_w[t], shard, big_m[t], big_v[t], "adamw_" + names[t], g_row0=GROUP_OF[t].off(t), after=after)
        upd[names[t]] = tuple((r.T if transposed[t] else r)[None] for r in res)
        return res[0]

    done = [update(t, early_shard, token) for t in EARLY.tensors]
    late_half = finish_late((*done, grad_x))
    pg, late_shard = _all_reduce_small(_pack_small(
        small["ln1_g"], small["ret_norm_g"], small["ln2_g"], small["lnf_g"], small["b_in"],
        small["attn_sinks"], small["loss"]), late_half, LATE)
    update(0, late_shard, None)

    small_names = ["ln1_g", "ret_norm_g", "ln2_g", "lnf_g", "b_in", "attn_sinks"]
    upd.update(zip(small_names, _adamw_small(
        pg, [ln1_g, ret_norm_g, ln2_g, lnf_g, b_in, attn_sinks],
        [m_ln1_g, m_ret_norm_g, m_ln2_g, m_lnf_g, m_b_in, m_attn_sinks],
        [v_ln1_g, v_ret_norm_g, v_ln2_g, v_lnf_g, v_b_in, v_attn_sinks])))

    order_w = ["ln1_g", "w_in", "b_in", "ret_norm_g", "w_ret_out", "attn_sinks", "w_attn_out", "w_out",
               "ln2_g", "w_ffn_gate", "w_ffn_up", "w_ffn_down", "lnf_g"]
    outs = [pg[12, 0], grad_x[None]]
    for k in range(4):
        outs += [upd[n][k] for n in order_w]
    return tuple(outs)
```

```python
from typing import NamedTuple

import jax
import jax.numpy as jnp
from jax import lax
from jax.experimental import pallas as pl
from jax.experimental.pallas import tpu as pltpu

D_MODEL = 1024
SEQ_CHUNK = 128
RET_CHUNK = 256
RET_CHUNKS_PER_STEP = 4
RET_HEADS = 4
RET_QK_DIM = 128
ATTN_Q_HEADS = 16
ATTN_HEAD_DIM = 64
D_FF = 2816
D_IN = 6400
ROPE_THETA = 10000.0
EPS = 1e-6
N_CHIPS = 4

ADAM_LR = 0.001
ADAM_B1 = 0.9
ADAM_B2 = 0.999
ADAM_EPS = 1e-08
ADAM_WD = 0.01
ADAM_STEP = 10

OFF_RQ, OFF_RK, OFF_RV, OFF_RG = 0, 512, 1024, 2048
OFF_AQ, OFF_AK, OFF_AV, OFF_GA, OFF_GB = 3072, 4096, 4224, 4352, 5376

V7X_VMEM_LIMIT = 56 * 1024 * 1024
LANES = 128

BF = jnp.bfloat16
F32 = jnp.float32
KEPT = jnp.bfloat16
MESH = pl.DeviceIdType.MESH

_VMEM = pl.BlockSpec(memory_space=pltpu.VMEM)
_SMEM = pl.BlockSpec(memory_space=pltpu.SMEM)
_ANY = pl.BlockSpec(memory_space=pl.ANY)


def _dot(a, b):
    return jnp.dot(a, b, preferred_element_type=F32)


def _dot_nt(a, b):
    return lax.dot_general(a, b, (((1,), (1,)), ((), ())), preferred_element_type=F32)


def _dot_tn(a, b):
    return lax.dot_general(a, b, (((0,), (0,)), ((), ())), preferred_element_type=F32)


def _sigmoid(x):
    return 0.5 * jnp.tanh(0.5 * x) + 0.5


def _rows(tm, n):
    return pl.BlockSpec((tm, n), lambda i: (i, 0))


def _fixed(shape):
    nd = len(shape)
    return pl.BlockSpec(shape, lambda i: (0,) * nd)


def _params(sem, **kw):
    return pltpu.CompilerParams(dimension_semantics=sem, vmem_limit_bytes=V7X_VMEM_LIMIT, **kw)


def _rot_ret(p, cr, sr):
    return p * cr + pltpu.roll(p, 64, 1) * sr


def _rot_ret_bwd(d, cr, sr):
    return d * cr - pltpu.roll(d, 64, 1) * sr


def _attn_partner(p):
    lane = lax.broadcasted_iota(jnp.int32, p.shape, 1)
    first = (lane % 64) < 32
    return jnp.where(first, pltpu.roll(p, 96, 1), pltpu.roll(p, 32, 1))


def _rot_attn(p, ca, sa):
    return p * ca + _attn_partner(p) * sa


def _rot_attn_bwd(d, ca, sa):
    return d * ca - _attn_partner(d) * sa


def _proj_fwd(x, g1, w_in, b_in, cr, sr, ca, sa, after):
    S = x.shape[0]
    tm = 512
    scale = RET_QK_DIM ** -0.5

    def body(x_ref, g_ref, w_ref, b_ref, cr_ref, sr_ref, ca_ref, sa_ref, after_ref,
             h_ref, rq_ref, rk_ref, rv_ref, rg_ref, aq_ref, ak_ref, av_ref, ga_ref, gb_ref):
        xv = x_ref[...]
        r = lax.rsqrt(jnp.mean(xv * xv, axis=-1, keepdims=True) + EPS)
        h = (xv * r * g_ref[...]).astype(BF)
        h_ref[...] = h

        def piece(lo, n):
            return _dot_nt(h, w_ref[lo:lo + n, :]) + b_ref[:, lo:lo + n]

        crv, srv = cr_ref[...], sr_ref[...]
        pq = piece(OFF_RQ, 512)
        pk = piece(OFF_RK, 512)
        for hd in range(RET_HEADS):
            sl = slice(hd * 128, (hd + 1) * 128)
            rq_ref[:, sl] = (_rot_ret(pq[:, sl], crv, srv) * scale).astype(BF)
            rk_ref[:, sl] = _rot_ret(pk[:, sl], crv, srv)
        rv_ref[...] = piece(OFF_RV, 1024).astype(BF)
        rg_ref[...] = piece(OFF_RG, 1024).astype(KEPT)
        cav, sav = ca_ref[...], sa_ref[...]
        pa = piece(OFF_AQ, 1024)
        for blk in range(8):
            sl = slice(blk * 128, (blk + 1) * 128)
            aq_ref[:, sl] = _rot_attn(pa[:, sl], cav, sav).astype(BF)
        pkv = piece(OFF_AK, 256)
        ak_ref[...] = _rot_attn(pkv[:, :128], cav, sav).astype(BF)
        av_ref[...] = pkv[:, 128:].astype(BF)
        ga_ref[...] = piece(OFF_GA, 1024).astype(KEPT)
        gb_ref[...] = piece(OFF_GB, 1024).astype(KEPT)

    outs = [(D_MODEL, BF), (512, BF), (512, F32), (1024, BF), (1024, KEPT),
            (1024, BF), (128, BF), (128, BF), (1024, KEPT), (1024, KEPT)]
    return pl.pallas_call(
        body, name="proj_fwd", grid=(S // tm,),
        in_specs=[_rows(tm, D_MODEL), _fixed((1, D_MODEL)), _VMEM, _VMEM,
                  _rows(tm, 128), _rows(tm, 128), _rows(tm, 128), _rows(tm, 128), _ANY],
        out_specs=[_rows(tm, n) for n, _ in outs],
        out_shape=[jax.ShapeDtypeStruct((S, n), dt) for n, dt in outs],
        compiler_params=_params(("parallel",)),
    )(x, g1, w_in, b_in, cr, sr, ca, sa, after)


def _ret_fwd(rq, rk, rv, dmat, qd, kd, cd):
    S = rq.shape[0]
    N = S // RET_CHUNK
    cps = RET_CHUNKS_PER_STEP
    rows = cps * RET_CHUNK

    def body(q_ref, k_ref, v_ref, dm_ref, qd_ref, kd_ref, cd_ref, y_ref, st_ref, state):
        @pl.when(pl.program_id(0) == 0)
        def _():
            state[...] = jnp.zeros_like(state)

        for hd in range(RET_HEADS):
            qs = slice(hd * 128, (hd + 1) * 128)
            vs = slice(hd * 256, (hd + 1) * 256)
            free = []
            for ch in range(cps):
                rs = slice(ch * RET_CHUNK, (ch + 1) * RET_CHUNK)
                q = q_ref[rs, qs]
                kf = k_ref[rs, qs]
                v = v_ref[rs, vs]
                a = _dot_nt(q, kf.astype(BF)) * dm_ref[hd]
                kdv = (kf * kd_ref[hd]).astype(BF)
                free.append((rs, q, _dot(a.astype(BF), v), _dot_tn(kdv, v)))
            s = state[hd]
            for ch, (rs, q, inner, update) in enumerate(free):
                sb = s.astype(BF)
                st_ref[ch, hd] = sb
                y_ref[rs, vs] = (inner + _dot(q, sb) * qd_ref[hd]).astype(KEPT)
                s = s * cd_ref[hd] + update
            state[hd] = s

    return pl.pallas_call(
        body, name="ret_fwd", grid=(N // cps,),
        in_specs=[_rows(rows, 512), _rows(rows, 512), _rows(rows, 1024), _VMEM, _VMEM, _VMEM, _VMEM],
        out_specs=[_rows(rows, 1024), pl.BlockSpec((cps, RET_HEADS, 128, 256), lambda i: (i, 0, 0, 0))],
        out_shape=[jax.ShapeDtypeStruct((S, 1024), KEPT),
                   jax.ShapeDtypeStruct((N, RET_HEADS, 128, 256), BF)],
        scratch_shapes=[pltpu.VMEM((RET_HEADS, 128, 256), F32)],
        compiler_params=_params(("arbitrary",)),
    )(rq, rk, rv, dmat, qd, kd, cd)


def _kv_halves(prev_ref, cur_ref):
    kk = jnp.concatenate([prev_ref[...], cur_ref[...]], axis=0).astype(F32)
    return kk, pltpu.roll(kk, 64, 1)


ATTN_FWD_BLOCKS_PER_STEP = 4
PAIRS = 4


def _attn_fold(n):
    t = lax.broadcasted_iota(jnp.int32, (128, PAIRS * 128), 0)
    qi = lax.broadcasted_iota(jnp.int32, (128, PAIRS * 128), 1) & 127
    cur = t <= qi
    cur_f = jnp.where(cur, 1.0, 0.0)
    bias = jnp.where(jnp.logical_or(cur, n > 0), 0.0, -1e30)
    return cur, cur_f, 1.0 - cur_f, bias


def _group_operands(g, kk, kk_sw, vv, vv_sw):
    lo = lax.broadcasted_iota(jnp.int32, (256, 128), 1) < 64
    k_a, k_b = (kk, kk_sw) if g == 0 else (kk_sw, kk)
    v_a, v_b = (vv, vv_sw) if g == 0 else (vv_sw, vv)
    return (jnp.concatenate([jnp.where(lo, k_a, 0.0), jnp.where(lo, 0.0, k_b)], axis=0).astype(BF),
            jnp.concatenate([jnp.where(lo, v_a, 0.0), jnp.where(lo, 0.0, v_b)], axis=0).astype(BF))


def _stack_pairs(ref, g, rows=slice(None)):
    return jnp.concatenate([ref[rows, (4 * g + p) * 128:(4 * g + p + 1) * 128] for p in range(PAIRS)], axis=0)


def _sink_row(sk_ref, g, which):
    return jnp.concatenate([jnp.full((1, 128), sk_ref[0, 8 * g + 2 * p + which], F32) for p in range(PAIRS)], axis=1)


def _fold(full, cur):
    return jnp.where(cur, full[128:], full[:128])


def _attn_probs(raw, fold, sink):
    s = _fold(raw, fold[0]) * (ATTN_HEAD_DIM ** -0.5) + fold[3]
    m = jnp.maximum(jnp.max(s, axis=0, keepdims=True), sink)
    e = jnp.exp(s - m)
    es = jnp.exp(sink - m)
    inv = 1.0 / (jnp.sum(e, axis=0, keepdims=True) + es)
    return e * inv, es * inv


def _unfold(p, fold):
    return jnp.concatenate([p * fold[2], p * fold[1]], axis=0).astype(BF)


def _attn_fwd(aq, ak, av, sinks, after):
    S = aq.shape[0]
    N = S // SEQ_CHUNK
    nb = ATTN_FWD_BLOCKS_PER_STEP

    def body(sk_ref, q_ref, kp_ref, kc_ref, vp_ref, vc_ref, after_ref, o_ref):
        kall, kall_sw = _kv_halves(kp_ref, kc_ref)
        vall, vall_sw = _kv_halves(vp_ref, vc_ref)
        blocks = []
        for sb in range(nb):
            kv = slice(sb * 128, sb * 128 + 256)
            rows = slice(sb * 128, (sb + 1) * 128)
            ops = [_group_operands(g, kall[kv], kall_sw[kv], vall[kv], vall_sw[kv]) for g in range(2)]
            raw = [_dot_nt(ops[g][0], _stack_pairs(q_ref, g, rows)) for g in range(2)]
            blocks.append((rows, ops, raw, _attn_fold(nb * pl.program_id(0) + sb)))
        for rows, ops, raw, fold in blocks:
            for g in range(2):
                probs = [_unfold(_attn_probs(raw[g][w * 256:(w + 1) * 256], fold, _sink_row(sk_ref, g, w))[0], fold)
                         for w in range(2)]
                o = _dot_tn(jnp.concatenate(probs, axis=0), ops[g][1])
                for p in range(PAIRS):
                    o_ref[rows, (4 * g + p) * 128:(4 * g + p + 1) * 128] = o[p * 128:(p + 1) * 128].astype(BF)

    prev = lambda i: (jnp.maximum(nb * i - 1, 0), 0)
    return pl.pallas_call(
        body, name="attn_fwd", grid=(N // nb,),
        in_specs=[_SMEM, _rows(nb * 128, 1024), pl.BlockSpec((128, 128), prev), _rows(nb * 128, 128),
                  pl.BlockSpec((128, 128), prev), _rows(nb * 128, 128), _ANY],
        out_specs=_rows(nb * 128, 1024),
        out_shape=jax.ShapeDtypeStruct((S, 1024), BF),
        compiler_params=_params(("parallel",)),
    )(sinks, aq, ak, ak, av, av, after)


def _group_norm(y):
    mu = jnp.mean(y, axis=-1, keepdims=True)
    yc = y - mu
    rstd = lax.rsqrt(jnp.mean(yc * yc, axis=-1, keepdims=True) + EPS)
    return yc * rstd, rstd


def _merge_fwd(x, ry, rg, ay, ga, gb, g_ret, w_ret, w_attn, w_out):
    S = x.shape[0]
    tm = 512

    def body(x_ref, ry_ref, rg_ref, ay_ref, ga_ref, gb_ref, gr_ref, wr_ref, wa_ref, wo_ref,
             u_ref, mg_ref, x1_ref):
        bb = _dot(ay_ref[...], wa_ref[...])
        for hd in range(RET_HEADS):
            vs = slice(hd * 256, (hd + 1) * 256)
            yn, _ = _group_norm(ry_ref[:, vs].astype(F32))
            rgv = rg_ref[:, vs].astype(F32)
            u_ref[:, vs] = (rgv * _sigmoid(rgv) * (yn * gr_ref[:, vs])).astype(BF)
        gated_b = _sigmoid(gb_ref[...].astype(F32)) * bb
        sa = _sigmoid(ga_ref[...].astype(F32))
        ba = _dot(u_ref[...], wr_ref[...])
        merged = (sa * ba + gated_b).astype(BF)
        mg_ref[...] = merged
        x1_ref[...] = x_ref[...] + _dot(merged, wo_ref[...])

    outs = [BF, BF, F32]
    return pl.pallas_call(
        body, name="merge_fwd", grid=(S // tm,),
        in_specs=[_rows(tm, 1024)] * 6 + [_fixed((1, 1024)), _VMEM, _VMEM, _VMEM],
        out_specs=[_rows(tm, 1024)] * 3,
        out_shape=[jax.ShapeDtypeStruct((S, 1024), dt) for dt in outs],
        compiler_params=_params(("parallel",)),
    )(x, ry, rg, ay, ga, gb, g_ret, w_ret, w_attn, w_out)


def _ffn_fwd(x1, tgt, g2, g3, w_gate, w_up, w_down):
    S = x1.shape[0]
    tm = 256

    def body(x1_ref, t_ref, g2_ref, g3_ref, wg_ref, wu_ref, wd_ref,
             gate_ref, up_ref, h2_ref, act_ref, dx2_ref, loss_ref, dg3_ref):
        @pl.when(pl.program_id(0) == 0)
        def _():
            loss_ref[...] = jnp.zeros_like(loss_ref)
            dg3_ref[...] = jnp.zeros_like(dg3_ref)

        x1v = x1_ref[...]
        r2 = lax.rsqrt(jnp.mean(x1v * x1v, axis=-1, keepdims=True) + EPS)
        h2 = (x1v * r2 * g2_ref[...]).astype(BF)
        h2_ref[...] = h2
        gate = _dot_nt(h2, wg_ref[...])
        up = _dot_nt(h2, wu_ref[...])
        gate_ref[...] = gate
        up_ref[...] = up
        act = (gate * _sigmoid(gate) * up).astype(BF)
        act_ref[...] = act
        x2 = x1v + _dot(act, wd_ref[...])
        r3 = lax.rsqrt(jnp.mean(x2 * x2, axis=-1, keepdims=True) + EPS)
        xh = x2 * r3
        g3v = g3_ref[...]
        err = xh * g3v - t_ref[...]
        loss_ref[...] += 0.5 * jnp.sum(jnp.mean(err * err, axis=-1, keepdims=True), axis=0, keepdims=True)
        dy = err * (1.0 / D_MODEL)
        dg3_ref[...] += jnp.sum(dy * xh, axis=0, keepdims=True)
        dxh = dy * g3v
        dx2_ref[...] = r3 * (dxh - xh * jnp.mean(dxh * xh, axis=-1, keepdims=True))

    return pl.pallas_call(
        body, name="ffn_fwd", grid=(S // tm,),
        in_specs=[_rows(tm, 1024), _rows(tm, 1024), _fixed((1, 1024)), _fixed((1, 1024)), _VMEM, _VMEM, _VMEM],
        out_specs=[_rows(tm, D_FF), _rows(tm, D_FF), _rows(tm, 1024), _rows(tm, D_FF), _rows(tm, 1024),
                   _fixed((1, LANES)), _fixed((1, 1024))],
        out_shape=[jax.ShapeDtypeStruct((S, D_FF), F32), jax.ShapeDtypeStruct((S, D_FF), F32),
                   jax.ShapeDtypeStruct((S, 1024), BF), jax.ShapeDtypeStruct((S, D_FF), BF),
                   jax.ShapeDtypeStruct((S, 1024), F32),
                   jax.ShapeDtypeStruct((1, LANES), F32), jax.ShapeDtypeStruct((1, 1024), F32)],
        compiler_params=_params(("arbitrary",)),
    )(x1, tgt, g2, g3, w_gate, w_up, w_down)


def _ffn_bwd(dx2, gate, up, x1, g2, w_gate, w_up, w_down):
    S = x1.shape[0]
    tm = 256

    def body(dx2_ref, gate_ref, up_ref, x1_ref, g2_ref, wg_ref, wu_ref, wd_ref,
             dgate_ref, dup_ref, dx1_ref, dg2_ref):
        @pl.when(pl.program_id(0) == 0)
        def _():
            dg2_ref[...] = jnp.zeros_like(dg2_ref)

        dx2v = dx2_ref[...]
        dact = _dot_nt(dx2v.astype(BF), wd_ref[...])
        gate = gate_ref[...]
        sg = _sigmoid(gate)
        dgate = (dact * up_ref[...] * (sg * (1.0 + gate * (1.0 - sg)))).astype(BF)
        dup = (dact * (gate * sg)).astype(BF)
        dgate_ref[...] = dgate
        dup_ref[...] = dup
        dh2 = _dot(dgate, wg_ref[...]) + _dot(dup, wu_ref[...])
        x1v = x1_ref[...]
        r2 = lax.rsqrt(jnp.mean(x1v * x1v, axis=-1, keepdims=True) + EPS)
        xh = x1v * r2
        dg2_ref[...] += jnp.sum(dh2 * xh, axis=0, keepdims=True)
        dxh = dh2 * g2_ref[...]
        dx1_ref[...] = dx2v + r2 * (dxh - xh * jnp.mean(dxh * xh, axis=-1, keepdims=True))

    return pl.pallas_call(
        body, name="ffn_bwd", grid=(S // tm,),
        in_specs=[_rows(tm, 1024), _rows(tm, D_FF), _rows(tm, D_FF), _rows(tm, 1024), _fixed((1, 1024)),
                  _VMEM, _VMEM, _VMEM],
        out_specs=[_rows(tm, D_FF), _rows(tm, D_FF), _rows(tm, 1024), _fixed((1, 1024))],
        out_shape=[jax.ShapeDtypeStruct((S, D_FF), BF), jax.ShapeDtypeStruct((S, D_FF), BF),
                   jax.ShapeDtypeStruct((S, 1024), F32), jax.ShapeDtypeStruct((1, 1024), F32)],
        compiler_params=_params(("arbitrary",)),
    )(dx2, gate, up, x1, g2, w_gate, w_up, w_down)


def _merge_bwd(dx1, ga, gb, u, ay, rg, ry, g_ret, w_ret, w_attn, w_out):
    S = dx1.shape[0]
    tm = 512

    def body(dx1_ref, ga_ref, gb_ref, u_ref, ay_ref, rg_ref, ry_ref, gr_ref, wr_ref, wa_ref, wo_ref,
             dgates_ref, drg_ref, dba_ref, dbb_ref, day_ref, dry_ref, dgr_ref):
        @pl.when(pl.program_id(0) == 0)
        def _():
            dgr_ref[...] = jnp.zeros_like(dgr_ref)

        dm = _dot_nt(dx1_ref[...].astype(BF), wo_ref[...])
        ba = _dot(u_ref[...], wr_ref[...])
        bb = _dot(ay_ref[...], wa_ref[...])
        sa = _sigmoid(ga_ref[...].astype(F32))
        sb = _sigmoid(gb_ref[...].astype(F32))
        dga = (dm * ba * (sa * (1.0 - sa))).astype(BF)
        dgb = (dm * bb * (sb * (1.0 - sb))).astype(BF)
        dgates_ref[:, :1024] = dga
        dgates_ref[:, 1024:] = dgb
        dba = (dm * sa).astype(BF)
        dbb = (dm * sb).astype(BF)
        dba_ref[...] = dba
        dbb_ref[...] = dbb
        day_ref[...] = _dot_nt(dbb, wa_ref[...]).astype(BF)
        du = _dot_nt(dba, wr_ref[...])
        for hd in range(RET_HEADS):
            vs = slice(hd * 256, (hd + 1) * 256)
            yn, rstd = _group_norm(ry_ref[:, vs].astype(F32))
            rgv = rg_ref[:, vs].astype(F32)
            sr = _sigmoid(rgv)
            silu = rgv * sr
            duv = du[:, vs]
            grv = gr_ref[:, vs]
            drg = (duv * (yn * grv) * (sr * (1.0 + rgv * (1.0 - sr)))).astype(BF)
            drg_ref[:, vs] = drg
            dyg = duv * silu
            dgr_ref[:, vs] += jnp.sum(dyg * yn, axis=0, keepdims=True)
            dyn = dyg * grv
            dry_ref[:, vs] = (rstd * (dyn - jnp.mean(dyn, axis=-1, keepdims=True)
                                      - yn * jnp.mean(dyn * yn, axis=-1, keepdims=True))).astype(KEPT)

    return pl.pallas_call(
        body, name="merge_bwd", grid=(S // tm,),
        in_specs=[_rows(tm, 1024)] * 7 + [_fixed((1, 1024)), _VMEM, _VMEM, _VMEM],
        out_specs=[_rows(tm, 2048), _rows(tm, 1024), _rows(tm, 1024), _rows(tm, 1024), _rows(tm, 1024),
                   _rows(tm, 1024), _fixed((1, 1024))],
        out_shape=[jax.ShapeDtypeStruct((S, 2048), BF), jax.ShapeDtypeStruct((S, 1024), BF),
                   jax.ShapeDtypeStruct((S, 1024), BF), jax.ShapeDtypeStruct((S, 1024), BF),
                   jax.ShapeDtypeStruct((S, 1024), BF), jax.ShapeDtypeStruct((S, 1024), KEPT),
                   jax.ShapeDtypeStruct((1, 1024), F32)],
        compiler_params=_params(("arbitrary",)),
    )(dx1, ga, gb, u, ay, rg, ry, g_ret, w_ret, w_attn, w_out)


def _attn_bwd(aq, ak, av, day, sinks, ca, sa, after):
    S = aq.shape[0]
    N = S // SEQ_CHUNK
    steps = N // 2

    def body(sk_ref, q_ref, kp_ref, kc_ref, vp_ref, vc_ref, do_ref, cq_ref, sq_ref, cp_ref, sp_ref, after_ref,
             dq_ref, dke_ref, dve_ref, dko_ref, dvo_ref, dsk_ref, carry_k, carry_v):
        i = pl.program_id(0)

        @pl.when(i == 0)
        def _():
            carry_k[...] = jnp.zeros_like(carry_k)
            carry_v[...] = jnp.zeros_like(carry_v)
            dsk_ref[...] = jnp.zeros_like(dsk_ref)

        @pl.when(i < steps)
        def _():
            kall, kall_sw = _kv_halves(kp_ref, kc_ref)
            vall, vall_sw = _kv_halves(vp_ref, vc_ref)
            lo = lax.broadcasted_iota(jnp.int32, (256, 128), 1) < 64
            blocks = []
            for sb in range(2):
                kv = slice(sb * 128, sb * 128 + 256)
                rows = slice(sb * 128, (sb + 1) * 128)
                ops = [_group_operands(g, kall[kv], kall_sw[kv], vall[kv], vall_sw[kv]) for g in range(2)]
                stacked = [(_stack_pairs(q_ref, g, rows), _stack_pairs(do_ref, g, rows)) for g in range(2)]
                raw = [_dot_nt(ops[g][0], stacked[g][0]) for g in range(2)]
                raw_dp = [_dot_nt(ops[g][1], stacked[g][1]) for g in range(2)]
                blocks.append((rows, ops, stacked, raw, raw_dp, _attn_fold(2 * i + sb)))
            grads = []
            d_sink = [jnp.zeros((1, PAIRS * 128), F32) for _ in range(4)]
            for rows, ops, stacked, raw, raw_dp, fold in blocks:
                cur = fold[0]
                cq, sq = cq_ref[rows, :], sq_ref[rows, :]
                dk_groups, dv_groups = [], []
                for g in range(2):
                    qs, dos = stacked[g]
                    ds_parts, p_parts = [], []
                    for which in range(2):
                        half = slice(which * 256, (which + 1) * 256)
                        pr, ps = _attn_probs(raw[g][half], fold, _sink_row(sk_ref, g, which))
                        dp = _fold(raw_dp[g][half], cur)
                        dr = jnp.sum(pr * dp, axis=0, keepdims=True)
                        ds_parts.append(_unfold(pr * (dp - dr) * (ATTN_HEAD_DIM ** -0.5), fold))
                        p_parts.append(_unfold(pr, fold))
                        d_sink[2 * g + which] = d_sink[2 * g + which] - ps * dr
                    ds = jnp.concatenate(ds_parts, axis=0)
                    dq = _dot_tn(ds, ops[g][0])
                    dk_both = _dot(ds, qs)
                    dv_both = _dot(jnp.concatenate(p_parts, axis=0), dos)
                    for p in range(PAIRS):
                        sl = slice((4 * g + p) * 128, (4 * g + p + 1) * 128)
                        dq_ref[rows, sl] = _rot_attn_bwd(dq[p * 128:(p + 1) * 128], cq, sq).astype(BF)
                    tk = jnp.where(lo, dk_both[:256], dk_both[256:])
                    tv = jnp.where(lo, dv_both[:256], dv_both[256:])
                    dk_groups.append(tk + pltpu.roll(tk, 64, 1))
                    dv_groups.append(tv + pltpu.roll(tv, 64, 1))
                grads.append((jnp.where(lo, dk_groups[0], dk_groups[1]), jnp.where(lo, dv_groups[0], dv_groups[1])))
            dsk_ref[...] += jnp.concatenate(d_sink, axis=0)
            (dk_a, dv_a), (dk_b, dv_b) = grads
            dko_ref[...] = _rot_attn_bwd(carry_k[...] + dk_a[:128], cp_ref[...], sp_ref[...]).astype(BF)
            dvo_ref[...] = (carry_v[...] + dv_a[:128]).astype(BF)
            dke_ref[...] = _rot_attn_bwd(dk_a[128:] + dk_b[:128], cq_ref[0:128, :], sq_ref[0:128, :]).astype(BF)
            dve_ref[...] = (dv_a[128:] + dv_b[:128]).astype(BF)
            carry_k[...] = dk_b[128:]
            carry_v[...] = dv_b[128:]

        @pl.when(i == steps)
        def _():
            dko_ref[...] = _rot_attn_bwd(carry_k[...], cp_ref[...], sp_ref[...]).astype(BF)
            dvo_ref[...] = carry_v[...].astype(BF)

    two = lambda i: (jnp.minimum(i, steps - 1), 0)
    before = lambda i: (jnp.maximum(2 * i - 1, 0), 0)
    odd = lambda i: (jnp.maximum(i - 1, 0), 0)
    blk = lambda r, w, im: pl.BlockSpec((r, w), im)
    half = jax.ShapeDtypeStruct((S // 2, 128), BF)
    dq, dk_even, dv_even, dk_odd, dv_odd, dsk = pl.pallas_call(
        body, name="attn_bwd", grid=(steps + 1,),
        in_specs=[_SMEM, blk(256, 1024, two), blk(128, 128, before), blk(256, 128, two), blk(128, 128, before),
                  blk(256, 128, two), blk(256, 1024, two), blk(256, 128, two), blk(256, 128, two),
                  blk(128, 128, before), blk(128, 128, before), _ANY],
        out_specs=[blk(256, 1024, two), blk(128, 128, two), blk(128, 128, two), blk(128, 128, odd),
                   blk(128, 128, odd), _fixed((4, PAIRS * 128))],
        out_shape=[jax.ShapeDtypeStruct((S, 1024), BF), half, half, half, half,
                   jax.ShapeDtypeStruct((4, PAIRS * 128), F32)],
        scratch_shapes=[pltpu.VMEM((128, 128), F32), pltpu.VMEM((128, 128), F32)],
        compiler_params=_params(("arbitrary",)),
    )(sinks, aq, ak, ak, av, av, day, ca, sa, ca, sa, after)

    def interleave(even, odd_):
        return jnp.stack([even.reshape(steps, 128, 128), odd_.reshape(steps, 128, 128)], axis=1).reshape(S, 128)

    return dq, interleave(dk_even, dk_odd), interleave(dv_even, dv_odd), dsk


def _ret_bwd(rq, rk, rv, dry, states, dmat, qd, kd, cd, cr, sr, after):
    S = rq.shape[0]
    N = S // RET_CHUNK
    scale = RET_QK_DIM ** -0.5
    cps = RET_CHUNKS_PER_STEP
    steps = N // cps

    def body(q_ref, k_ref, v_ref, dy_ref, st_ref, dm_ref, qd_ref, kd_ref, cd_ref, cr_ref, sr_ref, after_ref,
             out_ref, dstate):
        @pl.when(pl.program_id(0) == 0)
        def _():
            dstate[...] = jnp.zeros_like(dstate)

        for hd in range(RET_HEADS):
            qs = slice(hd * 128, (hd + 1) * 128)
            vs = slice(hd * 256, (hd + 1) * 256)
            dmv = dm_ref[hd]
            free = []
            for ch in range(cps):
                rs = slice(ch * RET_CHUNK, (ch + 1) * RET_CHUNK)
                crv, srv = cr_ref[rs, :], sr_ref[rs, :]
                q = q_ref[rs, qs]
                kf = k_ref[rs, qs]
                k = kf.astype(BF)
                v = v_ref[rs, vs]
                do = dy_ref[rs, vs].astype(F32)
                dob = do.astype(BF)
                doq = (do * qd_ref[hd]).astype(BF)
                a = (_dot_nt(q, k) * dmv).astype(BF)
                da = (_dot_nt(dob, v) * dmv).astype(BF)
                dq = _dot(da, k) + _dot_nt(doq, st_ref[ch, hd])
                out_ref[rs, qs] = _rot_ret_bwd(dq * scale, crv, srv).astype(BF)
                free.append((rs, crv, srv, v, (kf * kd_ref[hd]).astype(BF),
                             _dot_tn(a, dob), _dot_tn(da, q), _dot_tn(q, doq)))
            ds = dstate[hd]
            for rs, crv, srv, v, kdv, dv_free, dk_free, ds_update in reversed(free):
                dsb = ds.astype(BF)
                dv = dv_free + _dot(kdv, dsb)
                dk = dk_free + _dot_nt(v, dsb) * kd_ref[hd]
                ds = ds * cd_ref[hd] + ds_update
                out_ref[rs, 512 + hd * 128:512 + (hd + 1) * 128] = _rot_ret_bwd(dk, crv, srv).astype(BF)
                out_ref[rs, 1024 + hd * 256:1024 + (hd + 1) * 256] = dv.astype(BF)
            dstate[hd] = ds

    rev = lambda w: pl.BlockSpec((cps * RET_CHUNK, w), lambda i: (steps - 1 - i, 0))
    return pl.pallas_call(
        body, name="ret_bwd", grid=(steps,),
        in_specs=[rev(512), rev(512), rev(1024), rev(1024),
                  pl.BlockSpec((cps, RET_HEADS, 128, 256), lambda i: (steps - 1 - i, 0, 0, 0)),
                  _VMEM, _VMEM, _VMEM, _VMEM, rev(128), rev(128), _ANY],
        out_specs=rev(2048),
        out_shape=jax.ShapeDtypeStruct((S, 2048), BF),
        scratch_shapes=[pltpu.VMEM((RET_HEADS, 128, 256), F32)],
        compiler_params=_params(("arbitrary",)),
    )(rq, rk, rv, dry, states, dmat, qd, kd, cd, cr, sr, after)


def _proj_bwd(d_pieces, x, dx1, g1, w_in_t, after):
    S = x.shape[0]
    tm = 512
    widths = [d.shape[1] for d in d_pieces]
    assert sum(widths) == D_IN
    nd = len(d_pieces)

    def body(*refs):
        d_refs = refs[:nd]
        x_ref, dx1_ref, g_ref, w_ref, after_ref, gx_ref, dg_ref, db_ref = refs[nd:]

        @pl.when(pl.program_id(0) == 0)
        def _():
            dg_ref[...] = jnp.zeros_like(dg_ref)
            db_ref[...] = jnp.zeros_like(db_ref)

        d_all = jnp.concatenate([ref[...] for ref in d_refs], axis=1)
        dh = _dot(d_all, w_ref[...])
        db_ref[...] += jnp.sum(d_all.astype(F32), axis=0, keepdims=True)
        xv = x_ref[...]
        r = lax.rsqrt(jnp.mean(xv * xv, axis=-1, keepdims=True) + EPS)
        xh = xv * r
        dg_ref[...] += jnp.sum(dh * xh, axis=0, keepdims=True)
        dxh = dh * g_ref[...]
        gx_ref[...] = dx1_ref[...] + r * (dxh - xh * jnp.mean(dxh * xh, axis=-1, keepdims=True))

    return pl.pallas_call(
        body, name="proj_bwd", grid=(S // tm,),
        in_specs=[_rows(tm, n) for n in widths] + [_rows(tm, 1024), _rows(tm, 1024), _fixed((1, 1024)), _VMEM, _ANY],
        out_specs=[_rows(tm, 1024), _fixed((1, 1024)), _fixed((1, D_IN))],
        out_shape=[jax.ShapeDtypeStruct((S, 1024), F32), jax.ShapeDtypeStruct((1, 1024), F32),
                   jax.ShapeDtypeStruct((1, D_IN), F32)],
        compiler_params=_params(("arbitrary",)),
    )(*d_pieces, x, dx1, g1, w_in_t, after)


SHARD_ROWS = (1600, 256, 256, 256, 704, 704, 704)


class Group(NamedTuple):
    name: str
    tensors: tuple
    offs: tuple
    block: int
    tile: int

    @property
    def half(self):
        return self.block // 2

    def off(self, t):
        return self.offs[self.tensors.index(t)]


LATE = Group("late", (0,), (0,), 1600, 800)
EARLY = Group("early", (4, 5, 6, 1, 2, 3), (0, 704, 1408, 2112, 2368, 2624), 2880, 480)
GROUP_OF = {t: g for g in (LATE, EARLY) for t in g.tensors}


def _wgrad(a_list, b, t, packed, name):
    S = b.shape[0]
    widths = [a.shape[1] for a in a_list]
    M = sum(widths)
    ts = 1024 if M <= 4096 else 512
    steps = S // ts
    na = len(a_list)
    grp = GROUP_OF[t]
    r, off = SHARD_ROWS[t], grp.off(t)
    assert M == N_CHIPS * r
    segments = [(chip * r, r, chip * grp.block + off) for chip in range(N_CHIPS)]
    stage_rows = r
    n_in = na + 1 + (2 if packed is not None else 0)

    def body(*refs):
        a_refs, b_ref = refs[:na], refs[na]
        g32_ref, gb_ref, acc, stage, sems = refs[n_in:]
        i = pl.program_id(0)

        @pl.when(i == 0)
        def _():
            acc[...] = jnp.zeros_like(acc)

        bv = b_ref[...].astype(BF)
        lo = 0
        for a_ref, n in zip(a_refs, widths):
            acc[lo:lo + n, :] += _dot_tn(a_ref[...].astype(BF), bv)
            lo += n

        @pl.when(i == steps - 1)
        def _():
            for a0, n, p0 in segments:
                c32 = pltpu.make_async_copy(acc.at[a0:a0 + n, :], g32_ref.at[pl.ds(p0, n), :], sems.at[0])
                c32.start()
                stage[0:n, :] = acc[a0:a0 + n, :].astype(BF)
                c16 = pltpu.make_async_copy(stage.at[0:n, :], gb_ref.at[pl.ds(p0, n), :], sems.at[1])
                c16.start()
                c32.wait()
                c16.wait()

    shapes = [jax.ShapeDtypeStruct((N_CHIPS * grp.block, 1024), F32),
              jax.ShapeDtypeStruct((N_CHIPS * grp.block, 1024), BF)]
    operands = list(a_list) + [b]
    in_specs = [_rows(ts, n) for n in widths] + [_rows(ts, 1024)]
    aliases = {}
    if packed is not None:
        operands += list(packed)
        in_specs += [_ANY, _ANY]
        aliases = {na + 1: 0, na + 2: 1}
    return pl.pallas_call(
        body, name=name, grid=(steps,),
        in_specs=in_specs, out_specs=[_ANY, _ANY], out_shape=shapes,
        scratch_shapes=[pltpu.VMEM((M, 1024), F32), pltpu.VMEM((stage_rows, 1024), BF),
                        pltpu.SemaphoreType.DMA((2,))],
        input_output_aliases=aliases,
        compiler_params=_params(("arbitrary",)),
    )(*operands)


def _stage_weights(place, ws, name, after=()):
    nw = len(ws)

    def body(p_ref, *refs):
        for w_ref, o_ref in zip(refs[:nw], refs[nw + len(after):]):
            o_ref[...] = w_ref[...].astype(BF)

    return pl.pallas_call(
        body, name=name,
        grid_spec=pltpu.PrefetchScalarGridSpec(
            num_scalar_prefetch=1, grid=(1,),
            in_specs=[pl.BlockSpec(w.shape, lambda i, p: (0, 0)) for w in ws] + [_ANY] * len(after),
            out_specs=[pl.BlockSpec(w.shape, lambda i, p: (p[0], 0)) for w in ws]),
        out_shape=[jax.ShapeDtypeStruct((N_CHIPS * w.shape[0], 1024), BF) for w in ws],
        compiler_params=_params(("arbitrary",)),
    )(place, *ws, *after)


N_PEERS = 7


def _sum_all(place, g32, others, grp):
    nb = grp.half // grp.tile

    def body(p_ref, g_ref, *refs):
        acc = g_ref[...]
        for r_ref in refs[:N_PEERS]:
            acc = acc + r_ref[...].astype(F32)
        refs[-1][...] = acc

    blk = lambda f: pl.BlockSpec((grp.tile, 1024), f)
    slot = lambda k: blk(lambda i, p: (k * nb + i, 0))
    return pl.pallas_call(
        body, name="sum_all_" + grp.name,
        grid_spec=pltpu.PrefetchScalarGridSpec(
            num_scalar_prefetch=1, grid=(nb,),
            in_specs=[blk(lambda i, p: ((2 * p[0] + p[1]) * nb + i, 0))] + [slot(k) for k in range(N_PEERS)],
            out_specs=blk(lambda i, p: (p[1] * nb + i, 0))),
        out_shape=jax.ShapeDtypeStruct((grp.block, 1024), F32),
        compiler_params=_params(("arbitrary",)),
    )(place, g32, *[others] * N_PEERS)


def _adamw_math(w, gv, m, v):
    mn = ADAM_B1 * m + (1.0 - ADAM_B1) * gv
    vn = ADAM_B2 * v + (1.0 - ADAM_B2) * (gv * gv)
    m_hat = mn / (1.0 - ADAM_B1 ** ADAM_STEP)
    v_hat = vn / (1.0 - ADAM_B2 ** ADAM_STEP)
    return -ADAM_LR * (m_hat / (jnp.sqrt(v_hat) + ADAM_EPS) + ADAM_WD * w), mn, vn


def _adamw(w, g, m, v, name, g_row0=0, after=None):
    r = w.shape[0]
    tr = next((c for c in (176, 160, 64) if r % c == 0 and g_row0 % c == 0), r)
    assert g_row0 % tr == 0
    g0 = g_row0 // tr
    extra = [] if after is None else [after]

    def body(w_ref, g_ref, m_ref, v_ref, *rest):
        go_ref, d_ref, mo_ref, vo_ref = rest[len(extra):]
        gv = g_ref[...]
        go_ref[...] = gv
        d_ref[...], mo_ref[...], vo_ref[...] = _adamw_math(w_ref[...], gv, m_ref[...], v_ref[...])

    return pl.pallas_call(
        body, name=name, grid=(r // tr,),
        in_specs=[_rows(tr, 1024), pl.BlockSpec((tr, 1024), lambda i: (g0 + i, 0)), _rows(tr, 1024), _rows(tr, 1024)]
        + [_ANY] * len(extra),
        out_specs=[_rows(tr, 1024)] * 4,
        out_shape=[jax.ShapeDtypeStruct((r, 1024), F32)] * 4,
        compiler_params=_params(("parallel",)),
    )(w, g, m, v, *extra)


def _place():
    x, y, c = lax.axis_index("x"), lax.axis_index("y"), lax.axis_index("c")
    chips = [(1 - x, y), (x, 1 - y), (1 - x, 1 - y)]
    return x, y, c, chips


_HBM =pl.BlockSpec(memory_space=pltpu.HBM)
_SEM = pl.BlockSpec(memory_space=pltpu.SEMAPHORE)
_DATAFLOW = pltpu.SideEffectType.DATAFLOW_SIDE_EFFECTING


def _exchange(name, arrays, n, plan):
    na = len(arrays)

    def body(*refs):
        pairs = plan(refs[na:2 * na], refs[2 * na], refs[2 * na + 1])
        for send, _ in pairs:
            send.start()
        for send, recv in pairs:
            send.wait_send()
            recv.wait_recv()

    return pl.pallas_call(
        body, name=name, in_specs=[_ANY] * na, out_specs=[_ANY] * na,
        out_shape=[jax.ShapeDtypeStruct(a.shape, a.dtype) for a in arrays],
        scratch_shapes=[pltpu.SemaphoreType.DMA((n,)), pltpu.SemaphoreType.DMA((n,))],
        input_output_aliases={i: i for i in range(na)},
    )(*arrays)


def _exchange_start(name, arrays, n, plan, after=()):
    na = len(arrays)

    def body(*refs):
        for send, _ in plan(refs[:na], refs[na + len(after)], refs[na + len(after) + 1]):
            send.start()
        refs[-1][...] = jnp.zeros_like(refs[-1])

    out = pl.pallas_call(
        body, name=name,
        out_shape=(pltpu.SemaphoreType.DMA((n,)), pltpu.SemaphoreType.DMA((n,)),
                   *[pltpu.HBM(a.shape, a.dtype) for a in arrays], jax.ShapeDtypeStruct((8, LANES), F32)),
        in_specs=[_HBM] * na + [_ANY] * len(after), out_specs=(_SEM, _SEM, *[_HBM] * na, _VMEM),
        input_output_aliases={i: 2 + i for i in range(na)},
        compiler_params=pltpu.CompilerParams(has_side_effects=_DATAFLOW),
    )(*[pltpu.with_memory_space_constraint(a, pltpu.HBM) for a in arrays], *after)
    return (out[0], out[1]), list(out[2:2 + na]), out[-1]


def _exchange_wait(name, sems, in_flight, plan, after=()):
    na = len(in_flight)

    def body(*refs):
        for send, recv in plan(refs[:na], refs[na], refs[na + 1]):
            send.wait_send()
            recv.wait_recv()

    out = pl.pallas_call(
        body, name=name,
        out_shape=tuple(pltpu.HBM(a.shape, a.dtype) for a in in_flight),
        in_specs=[_HBM] * na + [_SEM, _SEM] + [_ANY] * len(after), out_specs=tuple([_HBM] * na),
        input_output_aliases={i: i for i in range(na)},
        compiler_params=pltpu.CompilerParams(has_side_effects=_DATAFLOW),
    )(*in_flight, *sems, *after)
    return list(out)


def _remote(src, dst, send_sems, recv_sems, k, to):
    return pltpu.make_async_remote_copy(src_ref=src, dst_ref=dst, send_sem=send_sems.at[k], recv_sem=recv_sems.at[k],
                                        device_id=to, device_id_type=MESH)


def _half_rows(ref, tid, pj, half):
    h = SHARD_ROWS[tid] // 2
    return ref.at[pl.ds((2 * pj + half) * h, h), :]


def _plan_gather_halves(tids):
    def plan(refs, ss, rs):
        x, y, c, chips = _place()
        pairs = []
        for t, tid in enumerate(tids):
            mine = _half_rows(refs[t], tid, 2 * x + y, c)
            for k, (cx, cy) in enumerate(chips):
                theirs = _half_rows(refs[t], tid, 2 * cx + cy, c)
                pairs.append((_remote(mine, mine, ss, rs, 3 * t + k, (cx, cy, c)),
                              _remote(theirs, theirs, ss, rs, 3 * t + k, (x, y, c))))
        return pairs
    return plan


def _plan_forward_halves(tids):
    def plan(refs, ss, rs):
        x, y, c, chips = _place()
        pairs = []
        for t, tid in enumerate(tids):
            for k, (cx, cy) in enumerate(chips):
                got = _half_rows(refs[t], tid, 2 * cx + cy, c)
                missing = _half_rows(refs[t], tid, 2 * cx + cy, 1 - c)
                pairs.append((_remote(got, got, ss, rs, 3 * t + k, (x, y, 1 - c)),
                              _remote(missing, missing, ss, rs, 3 * t + k, (x, y, 1 - c))))
        return pairs
    return plan


def _plan_to_all(grp):
    def plan(refs, ss, rs):
        x, y, c, chips = _place()
        peers = [(x, y, 1 - c)] + [(cx, cy, pc) for cx, cy in chips for pc in (c, 1 - c)]
        assert len(peers) == N_PEERS
        cps = [_remote(refs[0].at[pl.ds((2 * px + py) * grp.block + pc * grp.half, grp.half), :],
                       refs[1].at[pl.ds(k * grp.half, grp.half), :], ss, rs, k, (px, py, pc))
               for k, (px, py, pc) in enumerate(peers)]
        return [(cp, cp) for cp in cps]
    return plan


def _plan_share_halves(grp, k=0):
    def plan(refs, ss, rs):
        x, y, c, _ = _place()
        mine = refs[0].at[pl.ds(c * grp.half, grp.half), :]
        other = refs[0].at[pl.ds((1 - c) * grp.half, grp.half), :]
        return [(_remote(mine, mine, ss, rs, k, (x, y, 1 - c)), _remote(other, other, ss, rs, k, (x, y, 1 - c)))]
    return plan


def _plan_both(first, n_refs, second):
    def plan(refs, ss, rs):
        return first(refs[:n_refs], ss, rs) + second(refs[n_refs:], ss, rs)
    return plan


def _empty(rows, dtype):
    return lax.empty((rows, 1024), dtype)


def _all_reduce_small(v, shard, grp):
    m_per, n = v.shape
    n_dev = 8

    def body(x_ref, _, out_ref, shard_ref, all_ref, send_sems, recv_sems, local_sem):
        x, y, c, chips = _place()
        (share, shared), = _plan_share_halves(grp, k=N_PEERS)([shard_ref], send_sems, recv_sems)
        me, sibling = (x, y, c), (x, y, 1 - c)

        def rows(px, py, pc):
            return all_ref.at[pl.ds((4 * px + 2 * py + pc) * m_per, m_per), :]

        def copy(k, block, to, src=None):
            return pltpu.make_async_remote_copy(
                src_ref=rows(*block) if src is None else src, dst_ref=rows(*block),
                send_sem=send_sems.at[k], recv_sem=recv_sems.at[k], device_id=to, device_id_type=MESH)

        mine = pltpu.make_async_copy(x_ref, rows(*me), local_sem)
        mine.start()
        first = [copy(0, me, sibling, src=x_ref)]
        first += [copy(1 + k, me, (*chip, c), src=x_ref) for k, chip in enumerate(chips)]
        for cp in first:
            cp.start()
        share.start()
        passed = [copy(4 + k, (*chip, c), sibling) for k, chip in enumerate(chips)]
        for k, chip in enumerate(chips):
            copy(1 + k, (*chip, c), me).wait_recv()
            passed[k].start()
        copy(0, sibling, me).wait_recv()
        for k, chip in enumerate(chips):
            copy(4 + k, (*chip, 1 - c), me).wait_recv()
        for cp in first + passed:
            cp.wait_send()
        mine.wait()
        acc = all_ref[pl.ds(0, m_per), :]
        for d in range(1, n_dev):
            acc = acc + all_ref[pl.ds(d * m_per, m_per), :]
        out_ref[...] = acc
        share.wait_send()
        shared.wait_recv()

    return pl.pallas_call(
        body, name="all_reduce_small_share_" + grp.name,
        in_specs=[_VMEM, _ANY], out_specs=[_VMEM, _ANY],
        out_shape=[jax.ShapeDtypeStruct((m_per, n), F32), jax.ShapeDtypeStruct(shard.shape, shard.dtype)],
        scratch_shapes=[pltpu.VMEM((n_dev * m_per, n), F32), pltpu.SemaphoreType.DMA((N_PEERS + 1,)),
                        pltpu.SemaphoreType.DMA((N_PEERS + 1,)), pltpu.SemaphoreType.DMA],
        input_output_aliases={1: 1},
        compiler_params=pltpu.CompilerParams(has_side_effects=True),
    )(v, shard)


def _tables(S, token):
    pos = jnp.arange(S, dtype=jnp.int32).astype(F32) + token[0, 0]

    def cs(half):
        inv_freq = ROPE_THETA ** (-jnp.arange(half, dtype=F32) / half)
        ang = pos[:, None] * inv_freq[None, :]
        return jnp.cos(ang), jnp.sin(ang)

    c64, s64 = cs(64)
    c32, s32 = cs(32)
    cr = jnp.concatenate([c64, c64], axis=1)
    sr = jnp.concatenate([-s64, s64], axis=1)
    ca = jnp.concatenate([c32, c32, c32, c32], axis=1)
    sa = jnp.concatenate([-s32, s32, -s32, s32], axis=1)
    return cr, sr, ca, sa


def _decays():
    H, C = RET_HEADS, RET_CHUNK
    log_gamma = jnp.log1p(-jnp.exp2(-5.0 - jnp.arange(H, dtype=F32)))
    idx = jnp.arange(C, dtype=F32)
    rel = idx[:, None] - idx[None, :]
    dmat = jnp.where(rel[None] >= 0, jnp.exp(log_gamma[:, None, None] * jnp.maximum(rel, 0.0)[None]), 0.0)
    qd = jnp.exp(log_gamma[:, None] * (idx + 1.0))[:, :, None]
    kd = jnp.exp(log_gamma[:, None] * (C - 1.0 - idx))[:, :, None]
    cd = jnp.exp(log_gamma * C)[:, None, None]
    return dmat, qd, kd, cd


def _local_step(place, x, tgt, ln1_g, b_in, ret_norm_g, attn_sinks, ln2_g, lnf_g, w_in, other_weights, tables, decays,
                after):
    cr, sr, ca, sa = tables
    dmat, qd, kd, cd = decays
    forward_weights, landed_weights = other_weights

    h, rq, rk, rv, rg, aq, ak, av, ga, gb = _proj_fwd(x, ln1_g, w_in, b_in, cr, sr, ca, sa, after)
    ry, states = _ret_fwd(rq, rk, rv, dmat, qd, kd, cd)
    token = forward_weights(ry)
    ay = _attn_fwd(aq, ak, av, attn_sinks, token)
    w_ret, w_attn, w_out, w_gate, w_up, w_down = landed_weights(ay)
    u, merged, x1 = _merge_fwd(x, ry, rg, ay, ga, gb, ret_norm_g, w_ret, w_attn, w_out)
    gate, up, h2, act, dx2, loss, dg3 = _ffn_fwd(x1, tgt, ln2_g, lnf_g, w_gate, w_up, w_down)

    dgate, dup, dx1, dg2 = _ffn_bwd(dx2, gate, up, x1, ln2_g, w_gate, w_up, w_down)
    early = _wgrad([act], dx2, 6, None, "wgrad_down")
    early = _wgrad([dgate], h2, 4, early, "wgrad_gate")
    early = _wgrad([dup], h2, 5, early, "wgrad_up")
    dgates, drg, dba, dbb, day, dry, dgr = _merge_bwd(dx1, ga, gb, u, ay, rg, ry, ret_norm_g,
                                                                w_ret, w_attn, w_out)
    early = _wgrad([merged], dx1, 3, early, "wgrad_out")
    early = _wgrad([u], dba, 1, early, "wgrad_ret")
    e32, e16 = _wgrad([ay], dbb, 2, early, "wgrad_attn")

    sems, flight, token = _exchange_start("to_all_start_early", [e16, _empty(N_PEERS * EARLY.half, BF)], N_PEERS,
                                          _plan_to_all(EARLY))
    daq, dak, dav, dsk = _attn_bwd(aq, ak, av, day, attn_sinks, ca, sa, token)
    d_ret = _ret_bwd(rq, rk, rv, dry, states, dmat, qd, kd, cd, cr, sr, token)
    d_proj = [d_ret, drg, daq, dak, dav, dgates]
    l32, l16 = _wgrad(d_proj, h, 0, None, "wgrad_in")
    _, others = _exchange_wait("to_all_wait_early", sems, flight, _plan_to_all(EARLY), after=(l32,))
    share = _plan_share_halves(EARLY, k=N_PEERS)
    sems, flight, token = _exchange_start(
        "to_all_late_share_early_start", [l16, _empty(N_PEERS * LATE.half, BF), _sum_all(place, e32, others, EARLY)],
        N_PEERS + 1, _plan_both(_plan_to_all(LATE), 2, share))
    grad_x, dg1, db = _proj_bwd(d_proj, x, dx1, ln1_g, w_in, token)
    early_shard, = _exchange_wait("share_wait_early", sems, flight[2:], share, after=(grad_x,))

    def finish_late(after):
        _, others = _exchange_wait("to_all_wait_late", sems, flight[:2], _plan_to_all(LATE), after=after)
        return _sum_all(place, l32, others, LATE)

    d_sinks = dsk.reshape(2, 2, PAIRS, 128).sum(-1).transpose(0, 2, 1).reshape(1, ATTN_Q_HEADS)
    small = dict(loss=loss, ln1_g=dg1, b_in=db, ret_norm_g=dgr, attn_sinks=d_sinks,
                 ln2_g=dg2, lnf_g=dg3)
    return grad_x, early_shard, token, finish_late, small


SMALL_ROWS = 16


def _pack_small(ln1, ret, ln2, lnf, b_in, sinks, loss):
    def body(a_ref, b_ref, c_ref, d_ref, bi_ref, sk_ref, ls_ref, o_ref):
        o_ref[...] = jnp.zeros_like(o_ref)
        for row, ref in enumerate((a_ref, b_ref, c_ref, d_ref)):
            o_ref[row:row + 1, :] = ref[...]
        for k in range(6):
            o_ref[4 + k:5 + k, :] = bi_ref[:, k * 1024:(k + 1) * 1024]
        o_ref[10:11, 0:D_IN - 6144] = bi_ref[:, 6144:D_IN]
        o_ref[11:12, 0:ATTN_Q_HEADS] = sk_ref[...]
        o_ref[12:13, 0:LANES] = ls_ref[...]

    return pl.pallas_call(
        body, name="pack_small_grads", in_specs=[_VMEM] * 7, out_specs=_VMEM,
        out_shape=jax.ShapeDtypeStruct((SMALL_ROWS, 1024), F32),
    )(ln1.reshape(1, 1024), ret.reshape(1, 1024), ln2.reshape(1, 1024), lnf.reshape(1, 1024),
      b_in.reshape(1, D_IN), sinks.reshape(1, ATTN_Q_HEADS), loss.reshape(1, LANES))


def _adamw_small(pg, ws, ms, vs):
    n_par = len(ws)
    widths = [w.size for w in ws]
    first_row = [0, 1, 2, 3, 4, 11]

    def body(pg_ref, *refs):
        w_refs, m_refs, v_refs = refs[:n_par], refs[n_par:2 * n_par], refs[2 * n_par:3 * n_par]
        out_refs = refs[3 * n_par:]
        for i, width in enumerate(widths):
            for k, c0 in enumerate(range(0, width, 1024)):
                cols = slice(c0, min(c0 + 1024, width))
                gv = pg_ref[first_row[i] + k:first_row[i] + k + 1, 0:cols.stop - c0]
                go_ref, d_ref, mo_ref, vo_ref = out_refs[4 * i:4 * i + 4]
                go_ref[:, cols] = gv
                d_ref[:, cols], mo_ref[:, cols], vo_ref[:, cols] = _adamw_math(
                    w_refs[i][:, cols], gv, m_refs[i][:, cols], v_refs[i][:, cols])

    flat = lambda a: a.reshape(1, a.size)
    res = pl.pallas_call(
        body, name="adamw_small", in_specs=[_VMEM] * (1 + 3 * n_par), out_specs=[_VMEM] * (4 * n_par),
        out_shape=[jax.ShapeDtypeStruct((1, width), F32) for width in widths for _ in range(4)],
    )(pg, *map(flat, ws), *map(flat, ms), *map(flat, vs))
    return [tuple(r.reshape(w.shape) for r in res[4 * i:4 * i + 4]) for i, w in enumerate(ws)]


def kernel(x, ln1_g, w_in, b_in, ret_norm_g, w_ret_out, attn_sinks, w_attn_out, w_out, ln2_g, w_ffn_gate, w_ffn_up, w_ffn_down, lnf_g, loss_target, m_ln1_g, m_w_in, m_b_in, m_ret_norm_g, m_w_ret_out, m_attn_sinks, m_w_attn_out, m_w_out, m_ln2_g, m_w_ffn_gate, m_w_ffn_up, m_w_ffn_down, m_lnf_g, v_ln1_g, v_w_in, v_b_in, v_ret_norm_g, v_w_ret_out, v_attn_sinks, v_w_attn_out, v_w_out, v_ln2_g, v_w_ffn_gate, v_w_ffn_up, v_w_ffn_down, v_lnf_g):
    xi = lax.axis_index("x")
    yi = lax.axis_index("y")
    ci = lax.axis_index("c")
    j = 2 * xi + yi

    place = jnp.stack([j, ci]).astype(jnp.int32)

    names = ["w_in", "w_ret_out", "w_attn_out", "w_out", "w_ffn_gate", "w_ffn_up", "w_ffn_down"]
    transposed = [True, False, False, False, True, True, False]
    view = lambda a, t: a[0].T if t else a[0]
    big_w = [view(a, t) for a, t in zip([w_in, w_ret_out, w_attn_out, w_out, w_ffn_gate, w_ffn_up, w_ffn_down], transposed)]
    big_m = [view(a, t) for a, t in zip([m_w_in, m_w_ret_out, m_w_attn_out, m_w_out, m_w_ffn_gate, m_w_ffn_up, m_w_ffn_down], transposed)]
    big_v = [view(a, t) for a, t in zip([v_w_in, v_w_ret_out, v_w_attn_out, v_w_out, v_w_ffn_gate, v_w_ffn_up, v_w_ffn_down], transposed)]

    staged_in = _stage_weights(place, big_w[:1], "stage_w_in")[0]
    sems, in_flight, token = _exchange_start("gather_in_start", [staged_in], 3, _plan_gather_halves([0]))
    staged = _stage_weights(place, big_w[1:], "stage_others", after=(token,))
    tables = _tables(x.shape[1], token)
    decays = _decays()
    w_in_full = _exchange_wait("gather_in_wait", sems, in_flight, _plan_gather_halves([0]),
                               after=(*staged, *tables, *decays))[0]
    w_in_full = _exchange("gather_in_forward", [w_in_full], 3, _plan_forward_halves([0]))[0]
    rest = list(range(1, 7))
    crossing = _exchange_start("gather_start", staged, 3 * len(rest), _plan_gather_halves(rest), after=(w_in_full,))
    token = crossing[2]
    forwarding = []

    def forward_weights(after):
        arrived = _exchange_wait("gather_wait", crossing[0], crossing[1], _plan_gather_halves(rest), after=(after,))
        forwarding.extend(_exchange_start("forward_start", arrived, 3 * len(rest), _plan_forward_halves(rest)))
        return forwarding[2]

    def landed_weights(after):
        return _exchange_wait("forward_wait", forwarding[0], forwarding[1], _plan_forward_halves(rest), after=(after,))

    other_weights = (forward_weights, landed_weights)

    grad_x, early_shard, token, finish_late, small = _local_step(
        place, x[0], loss_target[0], ln1_g, b_in, ret_norm_g, attn_sinks, ln2_g, lnf_g.reshape(1, -1),
        w_in_full, other_weights, tables, decays, token)

    upd = {}

    def update(t, shard, after):
        res = _adamw(big_w[t], shard, big_m[t], big_v[t], "adamw_" + names[t], g_row0=GROUP_OF[t].off(t), after=after)
        upd[names[t]] = tuple((r.T if transposed[t] else r)[None] for r in res)
        return res[0]

    done = [update(t, early_shard, token) for t in EARLY.tensors]
    late_half = finish_late((*done, grad_x))
    pg, late_shard = _all_reduce_small(_pack_small(
        small["ln1_g"], small["ret_norm_g"], small["ln2_g"], small["lnf_g"], small["b_in"],
        small["attn_sinks"], small["loss"]), late_half, LATE)
    update(0, late_shard, None)

    small_names = ["ln1_g", "ret_norm_g", "ln2_g", "lnf_g", "b_in", "attn_sinks"]
    upd.update(zip(small_names, _adamw_small(
        pg, [ln1_g, ret_norm_g, ln2_g, lnf_g, b_in, attn_sinks],
        [m_ln1_g, m_ret_norm_g, m_ln2_g, m_lnf_g, m_b_in, m_attn_sinks],
        [v_ln1_g, v_ret_norm_g, v_ln2_g, v_lnf_g, v_b_in, v_attn_sinks])))

    order_w = ["ln1_g", "w_in", "b_in", "ret_norm_g", "w_ret_out", "attn_sinks", "w_attn_out", "w_out",
               "ln2_g", "w_ffn_gate", "w_ffn_up", "w_ffn_down", "lnf_g"]
    outs = [pg[12, 0], grad_x[None]]
    for k in range(4):
        outs += [upd[n][k] for n in order_w]
    return tuple(outs)
```

```python
from typing import NamedTuple

import jax
import jax.numpy as jnp
from jax import lax
from jax.experimental import pallas as pl
from jax.experimental.pallas import tpu as pltpu

D_MODEL = 1024
SEQ_CHUNK = 128
RET_CHUNK = 256
RET_CHUNKS_PER_STEP = 4
RET_HEADS = 4
RET_QK_DIM = 128
ATTN_Q_HEADS = 16
ATTN_HEAD_DIM = 64
D_FF = 2816
D_IN = 6400
ROPE_THETA = 10000.0
EPS = 1e-6
N_CHIPS = 4

ADAM_LR = 0.001
ADAM_B1 = 0.9
ADAM_B2 = 0.999
ADAM_EPS = 1e-08
ADAM_WD = 0.01
ADAM_STEP = 10

OFF_RQ, OFF_RK, OFF_RV, OFF_RG = 0, 512, 1024, 2048
OFF_AQ, OFF_AK, OFF_AV, OFF_GA, OFF_GB = 3072, 4096, 4224, 4352, 5376

V7X_VMEM_LIMIT = 56 * 1024 * 1024
LANES = 128

BF = jnp.bfloat16
F32 = jnp.float32
KEPT = jnp.bfloat16
MESH = pl.DeviceIdType.MESH

_VMEM = pl.BlockSpec(memory_space=pltpu.VMEM)
_SMEM = pl.BlockSpec(memory_space=pltpu.SMEM)
_ANY = pl.BlockSpec(memory_space=pl.ANY)


def _dot(a, b):
    return jnp.dot(a, b, preferred_element_type=F32)


def _dot_nt(a, b):
    return lax.dot_general(a, b, (((1,), (1,)), ((), ())), preferred_element_type=F32)


def _dot_tn(a, b):
    return lax.dot_general(a, b, (((0,), (0,)), ((), ())), preferred_element_type=F32)


def _sigmoid(x):
    return 0.5 * jnp.tanh(0.5 * x) + 0.5


def _rows(tm, n):
    return pl.BlockSpec((tm, n), lambda i: (i, 0))


def _fixed(shape):
    nd = len(shape)
    return pl.BlockSpec(shape, lambda i: (0,) * nd)


def _row_pieces(pieces, tile):
    operands, specs, widths, arity = [], [], [], []
    for piece in pieces:
        parts = piece if isinstance(piece, tuple) else (piece,)
        operands += parts
        specs += [_rows(tile // len(parts), parts[0].shape[1])] * len(parts)
        widths.append(parts[0].shape[1])
        arity.append(len(parts))

    def load(refs):
        vals, k = [], 0
        for n_parts in arity:
            if n_parts == 1:
                vals.append(refs[k][...])
            else:
                vals.append(jnp.concatenate([ref[b * 128:(b + 1) * 128, :] for b in range(tile // 256)
                                             for ref in refs[k:k + 2]], axis=0))
            k += n_parts
        return vals

    return operands, specs, widths, load


def _params(sem, **kw):
    return pltpu.CompilerParams(dimension_semantics=sem, vmem_limit_bytes=V7X_VMEM_LIMIT, **kw)


def _rot_ret(p, cr, sr):
    return p * cr + pltpu.roll(p, 64, 1) * sr


def _rot_ret_bwd(d, cr, sr):
    return d * cr - pltpu.roll(d, 64, 1) * sr


def _attn_partner(p):
    lane = lax.broadcasted_iota(jnp.int32, p.shape, 1)
    first = (lane % 64) < 32
    return jnp.where(first, pltpu.roll(p, 96, 1), pltpu.roll(p, 32, 1))


def _rot_attn(p, ca, sa):
    return p * ca + _attn_partner(p) * sa


def _rot_attn_bwd(d, ca, sa):
    return d * ca - _attn_partner(d) * sa


def _proj_fwd(x, g1, w_in, b_in, cr, sr, ca, sa, after):
    S = x.shape[0]
    tm = 512
    scale = RET_QK_DIM ** -0.5

    def body(x_ref, g_ref, w_ref, b_ref, cr_ref, sr_ref, ca_ref, sa_ref, after_ref,
             h_ref, rq_ref, rk_ref, rv_ref, rg_ref, aq_ref, ak_ref, av_ref, ga_ref, gb_ref):
        xv = x_ref[...]
        r = lax.rsqrt(jnp.mean(xv * xv, axis=-1, keepdims=True) + EPS)
        h = (xv * r * g_ref[...]).astype(BF)
        h_ref[...] = h

        def piece(lo, n):
            return _dot_nt(h, w_ref[lo:lo + n, :]) + b_ref[:, lo:lo + n]

        crv, srv = cr_ref[...], sr_ref[...]
        pq = piece(OFF_RQ, 512)
        pk = piece(OFF_RK, 512)
        for hd in range(RET_HEADS):
            sl = slice(hd * 128, (hd + 1) * 128)
            rq_ref[:, sl] = (_rot_ret(pq[:, sl], crv, srv) * scale).astype(BF)
            rk_ref[:, sl] = _rot_ret(pk[:, sl], crv, srv)
        rv_ref[...] = piece(OFF_RV, 1024).astype(BF)
        rg_ref[...] = piece(OFF_RG, 1024).astype(KEPT)
        cav, sav = ca_ref[...], sa_ref[...]
        pa = piece(OFF_AQ, 1024)
        for blk in range(8):
            sl = slice(blk * 128, (blk + 1) * 128)
            aq_ref[:, sl] = _rot_attn(pa[:, sl], cav, sav).astype(BF)
        pkv = piece(OFF_AK, 256)
        ak_ref[...] = _rot_attn(pkv[:, :128], cav, sav).astype(BF)
        av_ref[...] = pkv[:, 128:].astype(BF)
        ga_ref[...] = piece(OFF_GA, 1024).astype(KEPT)
        gb_ref[...] = piece(OFF_GB, 1024).astype(KEPT)

    outs = [(D_MODEL, BF), (512, BF), (512, F32), (1024, BF), (1024, KEPT),
            (1024, BF), (128, BF), (128, BF), (1024, KEPT), (1024, KEPT)]
    return pl.pallas_call(
        body, name="proj_fwd", grid=(S // tm,),
        in_specs=[_rows(tm, D_MODEL), _fixed((1, D_MODEL)), _VMEM, _VMEM,
                  _rows(tm, 128), _rows(tm, 128), _rows(tm, 128), _rows(tm, 128), _ANY],
        out_specs=[_rows(tm, n) for n, _ in outs],
        out_shape=[jax.ShapeDtypeStruct((S, n), dt) for n, dt in outs],
        compiler_params=_params(("parallel",)),
    )(x, g1, w_in, b_in, cr, sr, ca, sa, after)


def _ret_fwd(rq, rk, rv, dmat, qd, kd, cd):
    S = rq.shape[0]
    N = S // RET_CHUNK
    cps = RET_CHUNKS_PER_STEP
    rows = cps * RET_CHUNK

    def body(q_ref, k_ref, v_ref, dm_ref, qd_ref, kd_ref, cd_ref, y_ref, st_ref, state):
        @pl.when(pl.program_id(0) == 0)
        def _():
            state[...] = jnp.zeros_like(state)

        for hd in range(RET_HEADS):
            qs = slice(hd * 128, (hd + 1) * 128)
            vs = slice(hd * 256, (hd + 1) * 256)
            free = []
            for ch in range(cps):
                rs = slice(ch * RET_CHUNK, (ch + 1) * RET_CHUNK)
                q = q_ref[rs, qs]
                kf = k_ref[rs, qs]
                v = v_ref[rs, vs]
                a = _dot_nt(q, kf.astype(BF)) * dm_ref[hd]
                kdv = (kf * kd_ref[hd]).astype(BF)
                free.append((rs, q, _dot(a.astype(BF), v), _dot_tn(kdv, v)))
            s = state[hd]
            for ch, (rs, q, inner, update) in enumerate(free):
                sb = s.astype(BF)
                st_ref[ch, hd] = sb
                y_ref[rs, vs] = (inner + _dot(q, sb) * qd_ref[hd]).astype(KEPT)
                s = s * cd_ref[hd] + update
            state[hd] = s

    return pl.pallas_call(
        body, name="ret_fwd", grid=(N // cps,),
        in_specs=[_rows(rows, 512), _rows(rows, 512), _rows(rows, 1024), _VMEM, _VMEM, _VMEM, _VMEM],
        out_specs=[_rows(rows, 1024), pl.BlockSpec((cps, RET_HEADS, 128, 256), lambda i: (i, 0, 0, 0))],
        out_shape=[jax.ShapeDtypeStruct((S, 1024), KEPT),
                   jax.ShapeDtypeStruct((N, RET_HEADS, 128, 256), BF)],
        scratch_shapes=[pltpu.VMEM((RET_HEADS, 128, 256), F32)],
        compiler_params=_params(("arbitrary",)),
    )(rq, rk, rv, dmat, qd, kd, cd)


def _kv_halves(prev_ref, cur_ref):
    kk = jnp.concatenate([prev_ref[...], cur_ref[...]], axis=0).astype(F32)
    return kk, pltpu.roll(kk, 64, 1)


ATTN_FWD_BLOCKS_PER_STEP = 4
PAIRS = 4


def _attn_fold(n):
    t = lax.broadcasted_iota(jnp.int32, (128, PAIRS * 128), 0)
    qi = lax.broadcasted_iota(jnp.int32, (128, PAIRS * 128), 1) & 127
    cur = t <= qi
    cur_f = jnp.where(cur, 1.0, 0.0)
    bias = jnp.where(jnp.logical_or(cur, n > 0), 0.0, -1e30)
    return cur, cur_f, 1.0 - cur_f, bias


def _group_operands(g, kk, kk_sw, vv, vv_sw):
    lo = lax.broadcasted_iota(jnp.int32, (256, 128), 1) < 64
    k_a, k_b = (kk, kk_sw) if g == 0 else (kk_sw, kk)
    v_a, v_b = (vv, vv_sw) if g == 0 else (vv_sw, vv)
    return (jnp.concatenate([jnp.where(lo, k_a, 0.0), jnp.where(lo, 0.0, k_b)], axis=0).astype(BF),
            jnp.concatenate([jnp.where(lo, v_a, 0.0), jnp.where(lo, 0.0, v_b)], axis=0).astype(BF))


def _stack_pairs(ref, g, rows=slice(None)):
    return jnp.concatenate([ref[rows, (4 * g + p) * 128:(4 * g + p + 1) * 128] for p in range(PAIRS)], axis=0)


def _sink_row(sk_ref, g, which):
    return jnp.concatenate([jnp.full((1, 128), sk_ref[0, 8 * g + 2 * p + which], F32) for p in range(PAIRS)], axis=1)


def _fold(full, cur):
    return jnp.where(cur, full[128:], full[:128])


def _attn_probs(raw, fold, sink):
    s = _fold(raw, fold[0]) * (ATTN_HEAD_DIM ** -0.5) + fold[3]
    m = jnp.maximum(jnp.max(s, axis=0, keepdims=True), sink)
    e = jnp.exp(s - m)
    es = jnp.exp(sink - m)
    inv = 1.0 / (jnp.sum(e, axis=0, keepdims=True) + es)
    return e * inv, es * inv


def _unfold(p, fold):
    return jnp.concatenate([p * fold[2], p * fold[1]], axis=0).astype(BF)


def _attn_fwd(aq, ak, av, sinks, after):
    S = aq.shape[0]
    N = S // SEQ_CHUNK
    nb = ATTN_FWD_BLOCKS_PER_STEP

    def body(sk_ref, q_ref, kp_ref, kc_ref, vp_ref, vc_ref, after_ref, o_ref):
        kall, kall_sw = _kv_halves(kp_ref, kc_ref)
        vall, vall_sw = _kv_halves(vp_ref, vc_ref)
        blocks = []
        for sb in range(nb):
            kv = slice(sb * 128, sb * 128 + 256)
            rows = slice(sb * 128, (sb + 1) * 128)
            ops = [_group_operands(g, kall[kv], kall_sw[kv], vall[kv], vall_sw[kv]) for g in range(2)]
            raw = [_dot_nt(ops[g][0], _stack_pairs(q_ref, g, rows)) for g in range(2)]
            blocks.append((rows, ops, raw, _attn_fold(nb * pl.program_id(0) + sb)))
        for rows, ops, raw, fold in blocks:
            for g in range(2):
                probs = [_unfold(_attn_probs(raw[g][w * 256:(w + 1) * 256], fold, _sink_row(sk_ref, g, w))[0], fold)
                         for w in range(2)]
                o = _dot_tn(jnp.concatenate(probs, axis=0), ops[g][1])
                for p in range(PAIRS):
                    o_ref[rows, (4 * g + p) * 128:(4 * g + p + 1) * 128] = o[p * 128:(p + 1) * 128].astype(BF)

    prev = lambda i: (jnp.maximum(nb * i - 1, 0), 0)
    return pl.pallas_call(
        body, name="attn_fwd", grid=(N // nb,),
        in_specs=[_SMEM, _rows(nb * 128, 1024), pl.BlockSpec((128, 128), prev), _rows(nb * 128, 128),
                  pl.BlockSpec((128, 128), prev), _rows(nb * 128, 128), _ANY],
        out_specs=_rows(nb * 128, 1024),
        out_shape=jax.ShapeDtypeStruct((S, 1024), BF),
        compiler_params=_params(("parallel",)),
    )(sinks, aq, ak, ak, av, av, after)


def _group_norm(y):
    mu = jnp.mean(y, axis=-1, keepdims=True)
    yc = y - mu
    rstd = lax.rsqrt(jnp.mean(yc * yc, axis=-1, keepdims=True) + EPS)
    return yc * rstd, rstd


def _merge_fwd(x, ry, rg, ay, ga, gb, g_ret, w_ret, w_attn, w_out):
    S = x.shape[0]
    tm = 512

    def body(x_ref, ry_ref, rg_ref, ay_ref, ga_ref, gb_ref, gr_ref, wr_ref, wa_ref, wo_ref,
             u_ref, mg_ref, x1_ref):
        bb = _dot(ay_ref[...], wa_ref[...])
        for hd in range(RET_HEADS):
            vs = slice(hd * 256, (hd + 1) * 256)
            yn, _ = _group_norm(ry_ref[:, vs].astype(F32))
            rgv = rg_ref[:, vs].astype(F32)
            u_ref[:, vs] = (rgv * _sigmoid(rgv) * (yn * gr_ref[:, vs])).astype(BF)
        gated_b = _sigmoid(gb_ref[...].astype(F32)) * bb
        sa = _sigmoid(ga_ref[...].astype(F32))
        ba = _dot(u_ref[...], wr_ref[...])
        merged = (sa * ba + gated_b).astype(BF)
        mg_ref[...] = merged
        x1_ref[...] = x_ref[...] + _dot(merged, wo_ref[...])

    outs = [BF, BF, F32]
    return pl.pallas_call(
        body, name="merge_fwd", grid=(S // tm,),
        in_specs=[_rows(tm, 1024)] * 6 + [_fixed((1, 1024)), _VMEM, _VMEM, _VMEM],
        out_specs=[_rows(tm, 1024)] * 3,
        out_shape=[jax.ShapeDtypeStruct((S, 1024), dt) for dt in outs],
        compiler_params=_params(("parallel",)),
    )(x, ry, rg, ay, ga, gb, g_ret, w_ret, w_attn, w_out)


def _ffn_fwd(x1, tgt, g2, g3, w_gate, w_up, w_down):
    S = x1.shape[0]
    tm = 256

    def body(x1_ref, t_ref, g2_ref, g3_ref, wg_ref, wu_ref, wd_ref,
             gate_ref, up_ref, h2_ref, act_ref, dx2_ref, loss_ref, dg3_ref):
        @pl.when(pl.program_id(0) == 0)
        def _():
            loss_ref[...] = jnp.zeros_like(loss_ref)
            dg3_ref[...] = jnp.zeros_like(dg3_ref)

        x1v = x1_ref[...]
        r2 = lax.rsqrt(jnp.mean(x1v * x1v, axis=-1, keepdims=True) + EPS)
        h2 = (x1v * r2 * g2_ref[...]).astype(BF)
        h2_ref[...] = h2
        gate = _dot_nt(h2, wg_ref[...])
        up = _dot_nt(h2, wu_ref[...])
        gate_ref[...] = gate
        up_ref[...] = up
        act = (gate * _sigmoid(gate) * up).astype(BF)
        act_ref[...] = act
        x2 = x1v + _dot(act, wd_ref[...])
        r3 = lax.rsqrt(jnp.mean(x2 * x2, axis=-1, keepdims=True) + EPS)
        xh = x2 * r3
        g3v = g3_ref[...]
        err = xh * g3v - t_ref[...]
        loss_ref[...] += 0.5 * jnp.sum(jnp.mean(err * err, axis=-1, keepdims=True), axis=0, keepdims=True)
        dy = err * (1.0 / D_MODEL)
        dg3_ref[...] += jnp.sum(dy * xh, axis=0, keepdims=True)
        dxh = dy * g3v
        dx2_ref[...] = r3 * (dxh - xh * jnp.mean(dxh * xh, axis=-1, keepdims=True))

    return pl.pallas_call(
        body, name="ffn_fwd", grid=(S // tm,),
        in_specs=[_rows(tm, 1024), _rows(tm, 1024), _fixed((1, 1024)), _fixed((1, 1024)), _VMEM, _VMEM, _VMEM],
        out_specs=[_rows(tm, D_FF), _rows(tm, D_FF), _rows(tm, 1024), _rows(tm, D_FF), _rows(tm, 1024),
                   _fixed((1, LANES)), _fixed((1, 1024))],
        out_shape=[jax.ShapeDtypeStruct((S, D_FF), F32), jax.ShapeDtypeStruct((S, D_FF), F32),
                   jax.ShapeDtypeStruct((S, 1024), BF), jax.ShapeDtypeStruct((S, D_FF), BF),
                   jax.ShapeDtypeStruct((S, 1024), F32),
                   jax.ShapeDtypeStruct((1, LANES), F32), jax.ShapeDtypeStruct((1, 1024), F32)],
        compiler_params=_params(("arbitrary",)),
    )(x1, tgt, g2, g3, w_gate, w_up, w_down)


def _ffn_bwd(dx2, gate, up, x1, g2, w_gate, w_up, w_down):
    S = x1.shape[0]
    tm = 256

    def body(dx2_ref, gate_ref, up_ref, x1_ref, g2_ref, wg_ref, wu_ref, wd_ref,
             dgate_ref, dup_ref, dx1_ref, dg2_ref):
        @pl.when(pl.program_id(0) == 0)
        def _():
            dg2_ref[...] = jnp.zeros_like(dg2_ref)

        dx2v = dx2_ref[...]
        dact = _dot_nt(dx2v.astype(BF), wd_ref[...])
        gate = gate_ref[...]
        sg = _sigmoid(gate)
        dgate = (dact * up_ref[...] * (sg * (1.0 + gate * (1.0 - sg)))).astype(BF)
        dup = (dact * (gate * sg)).astype(BF)
        dgate_ref[...] = dgate
        dup_ref[...] = dup
        dh2 = _dot(dgate, wg_ref[...]) + _dot(dup, wu_ref[...])
        x1v = x1_ref[...]
        r2 = lax.rsqrt(jnp.mean(x1v * x1v, axis=-1, keepdims=True) + EPS)
        xh = x1v * r2
        dg2_ref[...] += jnp.sum(dh2 * xh, axis=0, keepdims=True)
        dxh = dh2 * g2_ref[...]
        dx1_ref[...] = dx2v + r2 * (dxh - xh * jnp.mean(dxh * xh, axis=-1, keepdims=True))

    return pl.pallas_call(
        body, name="ffn_bwd", grid=(S // tm,),
        in_specs=[_rows(tm, 1024), _rows(tm, D_FF), _rows(tm, D_FF), _rows(tm, 1024), _fixed((1, 1024)),
                  _VMEM, _VMEM, _VMEM],
        out_specs=[_rows(tm, D_FF), _rows(tm, D_FF), _rows(tm, 1024), _fixed((1, 1024))],
        out_shape=[jax.ShapeDtypeStruct((S, D_FF), BF), jax.ShapeDtypeStruct((S, D_FF), BF),
                   jax.ShapeDtypeStruct((S, 1024), F32), jax.ShapeDtypeStruct((1, 1024), F32)],
        compiler_params=_params(("arbitrary",)),
    )(dx2, gate, up, x1, g2, w_gate, w_up, w_down)


def _merge_bwd(dx1, ga, gb, u, ay, rg, ry, g_ret, w_ret, w_attn, w_out):
    S = dx1.shape[0]
    tm = 512

    def body(dx1_ref, ga_ref, gb_ref, u_ref, ay_ref, rg_ref, ry_ref, gr_ref, wr_ref, wa_ref, wo_ref,
             dgates_ref, drg_ref, dba_ref, dbb_ref, day_ref, dry_ref, dgr_ref):
        @pl.when(pl.program_id(0) == 0)
        def _():
            dgr_ref[...] = jnp.zeros_like(dgr_ref)

        dm = _dot_nt(dx1_ref[...].astype(BF), wo_ref[...])
        ba = _dot(u_ref[...], wr_ref[...])
        bb = _dot(ay_ref[...], wa_ref[...])
        sa = _sigmoid(ga_ref[...].astype(F32))
        sb = _sigmoid(gb_ref[...].astype(F32))
        dga = (dm * ba * (sa * (1.0 - sa))).astype(BF)
        dgb = (dm * bb * (sb * (1.0 - sb))).astype(BF)
        dgates_ref[:, :1024] = dga
        dgates_ref[:, 1024:] = dgb
        dba = (dm * sa).astype(BF)
        dbb = (dm * sb).astype(BF)
        dba_ref[...] = dba
        dbb_ref[...] = dbb
        day_ref[...] = _dot_nt(dbb, wa_ref[...]).astype(BF)
        du = _dot_nt(dba, wr_ref[...])
        for hd in range(RET_HEADS):
            vs = slice(hd * 256, (hd + 1) * 256)
            yn, rstd = _group_norm(ry_ref[:, vs].astype(F32))
            rgv = rg_ref[:, vs].astype(F32)
            sr = _sigmoid(rgv)
            silu = rgv * sr
            duv = du[:, vs]
            grv = gr_ref[:, vs]
            drg = (duv * (yn * grv) * (sr * (1.0 + rgv * (1.0 - sr)))).astype(BF)
            drg_ref[:, vs] = drg
            dyg = duv * silu
            dgr_ref[:, vs] += jnp.sum(dyg * yn, axis=0, keepdims=True)
            dyn = dyg * grv
            dry_ref[:, vs] = (rstd * (dyn - jnp.mean(dyn, axis=-1, keepdims=True)
                                      - yn * jnp.mean(dyn * yn, axis=-1, keepdims=True))).astype(KEPT)

    return pl.pallas_call(
        body, name="merge_bwd", grid=(S // tm,),
        in_specs=[_rows(tm, 1024)] * 7 + [_fixed((1, 1024)), _VMEM, _VMEM, _VMEM],
        out_specs=[_rows(tm, 2048), _rows(tm, 1024), _rows(tm, 1024), _rows(tm, 1024), _rows(tm, 1024),
                   _rows(tm, 1024), _fixed((1, 1024))],
        out_shape=[jax.ShapeDtypeStruct((S, 2048), BF), jax.ShapeDtypeStruct((S, 1024), BF),
                   jax.ShapeDtypeStruct((S, 1024), BF), jax.ShapeDtypeStruct((S, 1024), BF),
                   jax.ShapeDtypeStruct((S, 1024), BF), jax.ShapeDtypeStruct((S, 1024), KEPT),
                   jax.ShapeDtypeStruct((1, 1024), F32)],
        compiler_params=_params(("arbitrary",)),
    )(dx1, ga, gb, u, ay, rg, ry, g_ret, w_ret, w_attn, w_out)


def _attn_bwd(aq, ak, av, day, sinks, ca, sa, after):
    S = aq.shape[0]
    N = S // SEQ_CHUNK
    steps = N // 2

    def body(sk_ref, q_ref, kp_ref, kc_ref, vp_ref, vc_ref, do_ref, cq_ref, sq_ref, cp_ref, sp_ref, after_ref,
             dq_ref, dke_ref, dve_ref, dko_ref, dvo_ref, dsk_ref, carry_k, carry_v):
        i = pl.program_id(0)

        @pl.when(i == 0)
        def _():
            carry_k[...] = jnp.zeros_like(carry_k)
            carry_v[...] = jnp.zeros_like(carry_v)
            dsk_ref[...] = jnp.zeros_like(dsk_ref)

        @pl.when(i < steps)
        def _():
            kall, kall_sw = _kv_halves(kp_ref, kc_ref)
            vall, vall_sw = _kv_halves(vp_ref, vc_ref)
            lo = lax.broadcasted_iota(jnp.int32, (256, 128), 1) < 64
            blocks = []
            for sb in range(2):
                kv = slice(sb * 128, sb * 128 + 256)
                rows = slice(sb * 128, (sb + 1) * 128)
                ops = [_group_operands(g, kall[kv], kall_sw[kv], vall[kv], vall_sw[kv]) for g in range(2)]
                stacked = [(_stack_pairs(q_ref, g, rows), _stack_pairs(do_ref, g, rows)) for g in range(2)]
                raw = [_dot_nt(ops[g][0], stacked[g][0]) for g in range(2)]
                raw_dp = [_dot_nt(ops[g][1], stacked[g][1]) for g in range(2)]
                blocks.append((rows, ops, stacked, raw, raw_dp, _attn_fold(2 * i + sb)))
            grads = []
            d_sink = [jnp.zeros((1, PAIRS * 128), F32) for _ in range(4)]
            for rows, ops, stacked, raw, raw_dp, fold in blocks:
                cur = fold[0]
                cq, sq = cq_ref[rows, :], sq_ref[rows, :]
                dk_groups, dv_groups = [], []
                for g in range(2):
                    qs, dos = stacked[g]
                    ds_parts, p_parts = [], []
                    for which in range(2):
                        half = slice(which * 256, (which + 1) * 256)
                        pr, ps = _attn_probs(raw[g][half], fold, _sink_row(sk_ref, g, which))
                        dp = _fold(raw_dp[g][half], cur)
                        dr = jnp.sum(pr * dp, axis=0, keepdims=True)
                        ds_parts.append(_unfold(pr * (dp - dr) * (ATTN_HEAD_DIM ** -0.5), fold))
                        p_parts.append(_unfold(pr, fold))
                        d_sink[2 * g + which] = d_sink[2 * g + which] - ps * dr
                    ds = jnp.concatenate(ds_parts, axis=0)
                    dq = _dot_tn(ds, ops[g][0])
                    dk_both = _dot(ds, qs)
                    dv_both = _dot(jnp.concatenate(p_parts, axis=0), dos)
                    for p in range(PAIRS):
                        sl = slice((4 * g + p) * 128, (4 * g + p + 1) * 128)
                        dq_ref[rows, sl] = _rot_attn_bwd(dq[p * 128:(p + 1) * 128], cq, sq).astype(BF)
                    tk = jnp.where(lo, dk_both[:256], dk_both[256:])
                    tv = jnp.where(lo, dv_both[:256], dv_both[256:])
                    dk_groups.append(tk + pltpu.roll(tk, 64, 1))
                    dv_groups.append(tv + pltpu.roll(tv, 64, 1))
                grads.append((jnp.where(lo, dk_groups[0], dk_groups[1]), jnp.where(lo, dv_groups[0], dv_groups[1])))
            dsk_ref[...] += jnp.concatenate(d_sink, axis=0)
            (dk_a, dv_a), (dk_b, dv_b) = grads
            dko_ref[...] = _rot_attn_bwd(carry_k[...] + dk_a[:128], cp_ref[...], sp_ref[...]).astype(BF)
            dvo_ref[...] = (carry_v[...] + dv_a[:128]).astype(BF)
            dke_ref[...] = _rot_attn_bwd(dk_a[128:] + dk_b[:128], cq_ref[0:128, :], sq_ref[0:128, :]).astype(BF)
            dve_ref[...] = (dv_a[128:] + dv_b[:128]).astype(BF)
            carry_k[...] = dk_b[128:]
            carry_v[...] = dv_b[128:]

        @pl.when(i == steps)
        def _():
            dko_ref[...] = _rot_attn_bwd(carry_k[...], cp_ref[...], sp_ref[...]).astype(BF)
            dvo_ref[...] = carry_v[...].astype(BF)

    two = lambda i: (jnp.minimum(i, steps - 1), 0)
    before = lambda i: (jnp.maximum(2 * i - 1, 0), 0)
    odd = lambda i: (jnp.maximum(i - 1, 0), 0)
    blk = lambda r, w, im: pl.BlockSpec((r, w), im)
    half = jax.ShapeDtypeStruct((S // 2, 128), BF)
    dq, dk_even, dv_even, dk_odd, dv_odd, dsk = pl.pallas_call(
        body, name="attn_bwd", grid=(steps + 1,),
        in_specs=[_SMEM, blk(256, 1024, two), blk(128, 128, before), blk(256, 128, two), blk(128, 128, before),
                  blk(256, 128, two), blk(256, 1024, two), blk(256, 128, two), blk(256, 128, two),
                  blk(128, 128, before), blk(128, 128, before), _ANY],
        out_specs=[blk(256, 1024, two), blk(128, 128, two), blk(128, 128, two), blk(128, 128, odd),
                   blk(128, 128, odd), _fixed((4, PAIRS * 128))],
        out_shape=[jax.ShapeDtypeStruct((S, 1024), BF), half, half, half, half,
                   jax.ShapeDtypeStruct((4, PAIRS * 128), F32)],
        scratch_shapes=[pltpu.VMEM((128, 128), F32), pltpu.VMEM((128, 128), F32)],
        compiler_params=_params(("arbitrary",)),
    )(sinks, aq, ak, ak, av, av, day, ca, sa, ca, sa, after)

    return dq, (dk_even, dk_odd), (dv_even, dv_odd), dsk


def _ret_bwd(rq, rk, rv, dry, states, dmat, qd, kd, cd, cr, sr, after):
    S = rq.shape[0]
    N = S // RET_CHUNK
    scale = RET_QK_DIM ** -0.5
    cps = RET_CHUNKS_PER_STEP
    steps = N // cps

    def body(q_ref, k_ref, v_ref, dy_ref, st_ref, dm_ref, qd_ref, kd_ref, cd_ref, cr_ref, sr_ref, after_ref,
             out_ref, dstate):
        @pl.when(pl.program_id(0) == 0)
        def _():
            dstate[...] = jnp.zeros_like(dstate)

        for hd in range(RET_HEADS):
            qs = slice(hd * 128, (hd + 1) * 128)
            vs = slice(hd * 256, (hd + 1) * 256)
            dmv = dm_ref[hd]
            free = []
            for ch in range(cps):
                rs = slice(ch * RET_CHUNK, (ch + 1) * RET_CHUNK)
                crv, srv = cr_ref[rs, :], sr_ref[rs, :]
                q = q_ref[rs, qs]
                kf = k_ref[rs, qs]
                k = kf.astype(BF)
                v = v_ref[rs, vs]
                do = dy_ref[rs, vs].astype(F32)
                dob = do.astype(BF)
                doq = (do * qd_ref[hd]).astype(BF)
                a = (_dot_nt(q, k) * dmv).astype(BF)
                da = (_dot_nt(dob, v) * dmv).astype(BF)
                dq = _dot(da, k) + _dot_nt(doq, st_ref[ch, hd])
                out_ref[rs, qs] = _rot_ret_bwd(dq * scale, crv, srv).astype(BF)
                free.append((rs, crv, srv, v, (kf * kd_ref[hd]).astype(BF),
                             _dot_tn(a, dob), _dot_tn(da, q), _dot_tn(q, doq)))
            ds = dstate[hd]
            for rs, crv, srv, v, kdv, dv_free, dk_free, ds_update in reversed(free):
                dsb = ds.astype(BF)
                dv = dv_free + _dot(kdv, dsb)
                dk = dk_free + _dot_nt(v, dsb) * kd_ref[hd]
                ds = ds * cd_ref[hd] + ds_update
                out_ref[rs, 512 + hd * 128:512 + (hd + 1) * 128] = _rot_ret_bwd(dk, crv, srv).astype(BF)
                out_ref[rs, 1024 + hd * 256:1024 + (hd + 1) * 256] = dv.astype(BF)
            dstate[hd] = ds

    rev = lambda w: pl.BlockSpec((cps * RET_CHUNK, w), lambda i: (steps - 1 - i, 0))
    return pl.pallas_call(
        body, name="ret_bwd", grid=(steps,),
        in_specs=[rev(512), rev(512), rev(1024), rev(1024),
                  pl.BlockSpec((cps, RET_HEADS, 128, 256), lambda i: (steps - 1 - i, 0, 0, 0)),
                  _VMEM, _VMEM, _VMEM, _VMEM, rev(128), rev(128), _ANY],
        out_specs=rev(2048),
        out_shape=jax.ShapeDtypeStruct((S, 2048), BF),
        scratch_shapes=[pltpu.VMEM((RET_HEADS, 128, 256), F32)],
        compiler_params=_params(("arbitrary",)),
    )(rq, rk, rv, dry, states, dmat, qd, kd, cd, cr, sr, after)


def _proj_bwd(d_pieces, x, dx1, g1, w_in_t, after):
    S = x.shape[0]
    tm = 512
    d_operands, d_specs, widths, load = _row_pieces(d_pieces, tm)
    assert sum(widths) == D_IN
    nd = len(d_operands)

    def body(*refs):
        x_ref, dx1_ref, g_ref, w_ref, after_ref, gx_ref, dg_ref, db_ref = refs[nd:]

        @pl.when(pl.program_id(0) == 0)
        def _():
            dg_ref[...] = jnp.zeros_like(dg_ref)
            db_ref[...] = jnp.zeros_like(db_ref)

        d_all = jnp.concatenate(load(refs[:nd]), axis=1)
        dh = _dot(d_all, w_ref[...])
        db_ref[...] += jnp.sum(d_all.astype(F32), axis=0, keepdims=True)
        xv = x_ref[...]
        r = lax.rsqrt(jnp.mean(xv * xv, axis=-1, keepdims=True) + EPS)
        xh = xv * r
        dg_ref[...] += jnp.sum(dh * xh, axis=0, keepdims=True)
        dxh = dh * g_ref[...]
        gx_ref[...] = dx1_ref[...] + r * (dxh - xh * jnp.mean(dxh * xh, axis=-1, keepdims=True))

    return pl.pallas_call(
        body, name="proj_bwd", grid=(S // tm,),
        in_specs=d_specs + [_rows(tm, 1024), _rows(tm, 1024), _fixed((1, 1024)), _VMEM, _ANY],
        out_specs=[_rows(tm, 1024), _fixed((1, 1024)), _fixed((1, D_IN))],
        out_shape=[jax.ShapeDtypeStruct((S, 1024), F32), jax.ShapeDtypeStruct((1, 1024), F32),
                   jax.ShapeDtypeStruct((1, D_IN), F32)],
        compiler_params=_params(("arbitrary",)),
    )(*d_operands, x, dx1, g1, w_in_t, after)


SHARD_ROWS = (1600, 256, 256, 256, 704, 704, 704)


class Group(NamedTuple):
    name: str
    tensors: tuple
    offs: tuple
    block: int
    tile: int

    @property
    def half(self):
        return self.block // 2

    def off(self, t):
        return self.offs[self.tensors.index(t)]


LATE = Group("late", (0,), (0,), 1600, 800)
EARLY = Group("early", (4, 5, 6, 1, 2, 3), (0, 704, 1408, 2112, 2368, 2624), 2880, 480)
GROUP_OF = {t: g for g in (LATE, EARLY) for t in g.tensors}


def _wgrad(a_list, b, t, packed, name):
    S = b.shape[0]
    M = sum((a[0] if isinstance(a, tuple) else a).shape[1] for a in a_list)
    ts = 1024 if M <= 4096 else 512
    steps = S // ts
    a_operands, a_specs, widths, load = _row_pieces(a_list, ts)
    na = len(a_operands)
    grp = GROUP_OF[t]
    r, off = SHARD_ROWS[t], grp.off(t)
    assert M == N_CHIPS * r
    segments = [(chip * r, r, chip * grp.block + off) for chip in range(N_CHIPS)]
    stage_rows = r
    n_in = na + 1 + (2 if packed is not None else 0)

    def body(*refs):
        a_refs, b_ref = refs[:na], refs[na]
        g32_ref, gb_ref, acc, stage, sems = refs[n_in:]
        i = pl.program_id(0)

        @pl.when(i == 0)
        def _():
            acc[...] = jnp.zeros_like(acc)

        bv = b_ref[...].astype(BF)
        lo = 0
        for a, n in zip(load(a_refs), widths):
            acc[lo:lo + n, :] += _dot_tn(a.astype(BF), bv)
            lo += n

        @pl.when(i == steps - 1)
        def _():
            for a0, n, p0 in segments:
                c32 = pltpu.make_async_copy(acc.at[a0:a0 + n, :], g32_ref.at[pl.ds(p0, n), :], sems.at[0])
                c32.start()
                stage[0:n, :] = acc[a0:a0 + n, :].astype(BF)
                c16 = pltpu.make_async_copy(stage.at[0:n, :], gb_ref.at[pl.ds(p0, n), :], sems.at[1])
                c16.start()
                c32.wait()
                c16.wait()

    shapes = [jax.ShapeDtypeStruct((N_CHIPS * grp.block, 1024), F32),
              jax.ShapeDtypeStruct((N_CHIPS * grp.block, 1024), BF)]
    operands = a_operands + [b]
    in_specs = a_specs + [_rows(ts, 1024)]
    aliases = {}
    if packed is not None:
        operands += list(packed)
        in_specs += [_ANY, _ANY]
        aliases = {na + 1: 0, na + 2: 1}
    return pl.pallas_call(
        body, name=name, grid=(steps,),
        in_specs=in_specs, out_specs=[_ANY, _ANY], out_shape=shapes,
        scratch_shapes=[pltpu.VMEM((M, 1024), F32), pltpu.VMEM((stage_rows, 1024), BF),
                        pltpu.SemaphoreType.DMA((2,))],
        input_output_aliases=aliases,
        compiler_params=_params(("arbitrary",)),
    )(*operands)


def _stage_weights(place, ws, name, after=()):
    nw = len(ws)

    def body(p_ref, *refs):
        for w_ref, o_ref in zip(refs[:nw], refs[nw + len(after):]):
            o_ref[...] = w_ref[...].astype(BF)

    return pl.pallas_call(
        body, name=name,
        grid_spec=pltpu.PrefetchScalarGridSpec(
            num_scalar_prefetch=1, grid=(1,),
            in_specs=[pl.BlockSpec(w.shape, lambda i, p: (0, 0)) for w in ws] + [_ANY] * len(after),
            out_specs=[pl.BlockSpec(w.shape, lambda i, p: (p[0], 0)) for w in ws]),
        out_shape=[jax.ShapeDtypeStruct((N_CHIPS * w.shape[0], 1024), BF) for w in ws],
        compiler_params=_params(("arbitrary",)),
    )(place, *ws, *after)


N_PEERS = 7


def _sum_all(place, g32, others, grp):
    nb = grp.half // grp.tile

    def body(p_ref, g_ref, *refs):
        acc = g_ref[...]
        for r_ref in refs[:N_PEERS]:
            acc = acc + r_ref[...].astype(F32)
        refs[-1][...] = acc

    blk = lambda f: pl.BlockSpec((grp.tile, 1024), f)
    slot = lambda k: blk(lambda i, p: (k * nb + i, 0))
    return pl.pallas_call(
        body, name="sum_all_" + grp.name,
        grid_spec=pltpu.PrefetchScalarGridSpec(
            num_scalar_prefetch=1, grid=(nb,),
            in_specs=[blk(lambda i, p: ((2 * p[0] + p[1]) * nb + i, 0))] + [slot(k) for k in range(N_PEERS)],
            out_specs=blk(lambda i, p: (p[1] * nb + i, 0))),
        out_shape=jax.ShapeDtypeStruct((grp.block, 1024), F32),
        compiler_params=_params(("arbitrary",)),
    )(place, g32, *[others] * N_PEERS)


def _adamw_math(w, gv, m, v):
    mn = ADAM_B1 * m + (1.0 - ADAM_B1) * gv
    vn = ADAM_B2 * v + (1.0 - ADAM_B2) * (gv * gv)
    m_hat = mn / (1.0 - ADAM_B1 ** ADAM_STEP)
    v_hat = vn / (1.0 - ADAM_B2 ** ADAM_STEP)
    return -ADAM_LR * (m_hat / (jnp.sqrt(v_hat) + ADAM_EPS) + ADAM_WD * w), mn, vn


def _adamw(w, g, m, v, name, g_row0=0, after=None):
    r = w.shape[0]
    tr = next((c for c in (352, 320, 64) if r % c == 0 and g_row0 % c == 0), r)
    assert g_row0 % tr == 0
    g0 = g_row0 // tr
    extra = [] if after is None else [after]

    def body(w_ref, g_ref, m_ref, v_ref, *rest):
        go_ref, d_ref, mo_ref, vo_ref = rest[len(extra):]
        gv = g_ref[...]
        go_ref[...] = gv
        d_ref[...], mo_ref[...], vo_ref[...] = _adamw_math(w_ref[...], gv, m_ref[...], v_ref[...])

    return pl.pallas_call(
        body, name=name, grid=(r // tr,),
        in_specs=[_rows(tr, 1024), pl.BlockSpec((tr, 1024), lambda i: (g0 + i, 0)), _rows(tr, 1024), _rows(tr, 1024)]
        + [_ANY] * len(extra),
        out_specs=[_rows(tr, 1024)] * 4,
        out_shape=[jax.ShapeDtypeStruct((r, 1024), F32)] * 4,
        compiler_params=_params(("parallel",)),
    )(w, g, m, v, *extra)


def _place():
    x, y, c = lax.axis_index("x"), lax.axis_index("y"), lax.axis_index("c")
    chips = [(1 - x, y), (x, 1 - y), (1 - x, 1 - y)]
    return x, y, c, chips


_HBM =pl.BlockSpec(memory_space=pltpu.HBM)
_SEM = pl.BlockSpec(memory_space=pltpu.SEMAPHORE)
_DATAFLOW = pltpu.SideEffectType.DATAFLOW_SIDE_EFFECTING


def _exchange(name, arrays, n, plan):
    na = len(arrays)

    def body(*refs):
        pairs = plan(refs[na:2 * na], refs[2 * na], refs[2 * na + 1])
        for send, _ in pairs:
            send.start()
        for send, recv in pairs:
            send.wait_send()
            recv.wait_recv()

    return pl.pallas_call(
        body, name=name, in_specs=[_ANY] * na, out_specs=[_ANY] * na,
        out_shape=[jax.ShapeDtypeStruct(a.shape, a.dtype) for a in arrays],
        scratch_shapes=[pltpu.SemaphoreType.DMA((n,)), pltpu.SemaphoreType.DMA((n,))],
        input_output_aliases={i: i for i in range(na)},
    )(*arrays)


def _exchange_start(name, arrays, n, plan, after=()):
    na = len(arrays)

    def body(*refs):
        for send, _ in plan(refs[:na], refs[na + len(after)], refs[na + len(after) + 1]):
            send.start()
        refs[-1][...] = jnp.zeros_like(refs[-1])

    out = pl.pallas_call(
        body, name=name,
        out_shape=(pltpu.SemaphoreType.DMA((n,)), pltpu.SemaphoreType.DMA((n,)),
                   *[pltpu.HBM(a.shape, a.dtype) for a in arrays], jax.ShapeDtypeStruct((8, LANES), F32)),
        in_specs=[_HBM] * na + [_ANY] * len(after), out_specs=(_SEM, _SEM, *[_HBM] * na, _VMEM),
        input_output_aliases={i: 2 + i for i in range(na)},
        compiler_params=pltpu.CompilerParams(has_side_effects=_DATAFLOW),
    )(*[pltpu.with_memory_space_constraint(a, pltpu.HBM) for a in arrays], *after)
    return (out[0], out[1]), list(out[2:2 + na]), out[-1]


def _exchange_wait(name, sems, in_flight, plan, after=()):
    na = len(in_flight)

    def body(*refs):
        for send, recv in plan(refs[:na], refs[na], refs[na + 1]):
            send.wait_send()
            recv.wait_recv()

    out = pl.pallas_call(
        body, name=name,
        out_shape=tuple(pltpu.HBM(a.shape, a.dtype) for a in in_flight),
        in_specs=[_HBM] * na + [_SEM, _SEM] + [_ANY] * len(after), out_specs=tuple([_HBM] * na),
        input_output_aliases={i: i for i in range(na)},
        compiler_params=pltpu.CompilerParams(has_side_effects=_DATAFLOW),
    )(*in_flight, *sems, *after)
    return list(out)


def _remote(src, dst, send_sems, recv_sems, k, to):
    return pltpu.make_async_remote_copy(src_ref=src, dst_ref=dst, send_sem=send_sems.at[k], recv_sem=recv_sems.at[k],
                                        device_id=to, device_id_type=MESH)


def _half_rows(ref, tid, pj, half):
    h = SHARD_ROWS[tid] // 2
    return ref.at[pl.ds((2 * pj + half) * h, h), :]


def _plan_gather_halves(tids):
    def plan(refs, ss, rs):
        x, y, c, chips = _place()
        pairs = []
        for t, tid in enumerate(tids):
            mine = _half_rows(refs[t], tid, 2 * x + y, c)
            for k, (cx, cy) in enumerate(chips):
                theirs = _half_rows(refs[t], tid, 2 * cx + cy, c)
                pairs.append((_remote(mine, mine, ss, rs, 3 * t + k, (cx, cy, c)),
                              _remote(theirs, theirs, ss, rs, 3 * t + k, (x, y, c))))
        return pairs
    return plan


def _plan_forward_halves(tids):
    def plan(refs, ss, rs):
        x, y, c, chips = _place()
        pairs = []
        for t, tid in enumerate(tids):
            for k, (cx, cy) in enumerate(chips):
                got = _half_rows(refs[t], tid, 2 * cx + cy, c)
                missing = _half_rows(refs[t], tid, 2 * cx + cy, 1 - c)
                pairs.append((_remote(got, got, ss, rs, 3 * t + k, (x, y, 1 - c)),
                              _remote(missing, missing, ss, rs, 3 * t + k, (x, y, 1 - c))))
        return pairs
    return plan


def _plan_to_all(grp):
    def plan(refs, ss, rs):
        x, y, c, chips = _place()
        peers = [(x, y, 1 - c)] + [(cx, cy, pc) for cx, cy in chips for pc in (c, 1 - c)]
        assert len(peers) == N_PEERS
        cps = [_remote(refs[0].at[pl.ds((2 * px + py) * grp.block + pc * grp.half, grp.half), :],
                       refs[1].at[pl.ds(k * grp.half, grp.half), :], ss, rs, k, (px, py, pc))
               for k, (px, py, pc) in enumerate(peers)]
        return [(cp, cp) for cp in cps]
    return plan


def _plan_share_halves(grp, k=0):
    def plan(refs, ss, rs):
        x, y, c, _ = _place()
        mine = refs[0].at[pl.ds(c * grp.half, grp.half), :]
        other = refs[0].at[pl.ds((1 - c) * grp.half, grp.half), :]
        return [(_remote(mine, mine, ss, rs, k, (x, y, 1 - c)), _remote(other, other, ss, rs, k, (x, y, 1 - c)))]
    return plan


def _plan_both(first, n_refs, second):
    def plan(refs, ss, rs):
        return first(refs[:n_refs], ss, rs) + second(refs[n_refs:], ss, rs)
    return plan


def _empty(rows, dtype):
    return lax.empty((rows, 1024), dtype)


def _all_reduce_small(v, shard, grp):
    m_per, n = v.shape
    n_dev = 8

    def body(x_ref, _, out_ref, shard_ref, all_ref, send_sems, recv_sems, local_sem):
        x, y, c, chips = _place()
        (share, shared), = _plan_share_halves(grp, k=N_PEERS)([shard_ref], send_sems, recv_sems)
        me, sibling = (x, y, c), (x, y, 1 - c)

        def rows(px, py, pc):
            return all_ref.at[pl.ds((4 * px + 2 * py + pc) * m_per, m_per), :]

        def copy(k, block, to, src=None):
            return pltpu.make_async_remote_copy(
                src_ref=rows(*block) if src is None else src, dst_ref=rows(*block),
                send_sem=send_sems.at[k], recv_sem=recv_sems.at[k], device_id=to, device_id_type=MESH)

        mine = pltpu.make_async_copy(x_ref, rows(*me), local_sem)
        mine.start()
        first = [copy(0, me, sibling, src=x_ref)]
        first += [copy(1 + k, me, (*chip, c), src=x_ref) for k, chip in enumerate(chips)]
        for cp in first:
            cp.start()
        share.start()
        passed = [copy(4 + k, (*chip, c), sibling) for k, chip in enumerate(chips)]
        for k, chip in enumerate(chips):
            copy(1 + k, (*chip, c), me).wait_recv()
            passed[k].start()
        copy(0, sibling, me).wait_recv()
        for k, chip in enumerate(chips):
            copy(4 + k, (*chip, 1 - c), me).wait_recv()
        for cp in first + passed:
            cp.wait_send()
        mine.wait()
        acc = all_ref[pl.ds(0, m_per), :]
        for d in range(1, n_dev):
            acc = acc + all_ref[pl.ds(d * m_per, m_per), :]
        out_ref[...] = acc
        share.wait_send()
        shared.wait_recv()

    return pl.pallas_call(
        body, name="all_reduce_small_share_" + grp.name,
        in_specs=[_VMEM, _ANY], out_specs=[_VMEM, _ANY],
        out_shape=[jax.ShapeDtypeStruct((m_per, n), F32), jax.ShapeDtypeStruct(shard.shape, shard.dtype)],
        scratch_shapes=[pltpu.VMEM((n_dev * m_per, n), F32), pltpu.SemaphoreType.DMA((N_PEERS + 1,)),
                        pltpu.SemaphoreType.DMA((N_PEERS + 1,)), pltpu.SemaphoreType.DMA],
        input_output_aliases={1: 1},
        compiler_params=pltpu.CompilerParams(has_side_effects=True),
    )(v, shard)


def _tables(S, token):
    pos = jnp.arange(S, dtype=jnp.int32).astype(F32) + token[0, 0]

    def cs(half):
        inv_freq = ROPE_THETA ** (-jnp.arange(half, dtype=F32) / half)
        ang = pos[:, None] * inv_freq[None, :]
        return jnp.cos(ang), jnp.sin(ang)

    c64, s64 = cs(64)
    c32, s32 = cs(32)
    cr = jnp.concatenate([c64, c64], axis=1)
    sr = jnp.concatenate([-s64, s64], axis=1)
    ca = jnp.concatenate([c32, c32, c32, c32], axis=1)
    sa = jnp.concatenate([-s32, s32, -s32, s32], axis=1)
    return cr, sr, ca, sa


def _decays():
    H, C = RET_HEADS, RET_CHUNK
    log_gamma = jnp.log1p(-jnp.exp2(-5.0 - jnp.arange(H, dtype=F32)))
    idx = jnp.arange(C, dtype=F32)
    rel = idx[:, None] - idx[None, :]
    dmat = jnp.where(rel[None] >= 0, jnp.exp(log_gamma[:, None, None] * jnp.maximum(rel, 0.0)[None]), 0.0)
    qd = jnp.exp(log_gamma[:, None] * (idx + 1.0))[:, :, None]
    kd = jnp.exp(log_gamma[:, None] * (C - 1.0 - idx))[:, :, None]
    cd = jnp.exp(log_gamma * C)[:, None, None]
    return dmat, qd, kd, cd


def _local_step(place, x, tgt, ln1_g, b_in, ret_norm_g, attn_sinks, ln2_g, lnf_g, w_in, other_weights, tables, decays,
                after):
    cr, sr, ca, sa = tables
    dmat, qd, kd, cd = decays
    forward_weights, landed_weights = other_weights

    h, rq, rk, rv, rg, aq, ak, av, ga, gb = _proj_fwd(x, ln1_g, w_in, b_in, cr, sr, ca, sa, after)
    ry, states = _ret_fwd(rq, rk, rv, dmat, qd, kd, cd)
    token = forward_weights(ry)
    ay = _attn_fwd(aq, ak, av, attn_sinks, token)
    w_ret, w_attn, w_out, w_gate, w_up, w_down = landed_weights(ay)
    u, merged, x1 = _merge_fwd(x, ry, rg, ay, ga, gb, ret_norm_g, w_ret, w_attn, w_out)
    gate, up, h2, act, dx2, loss, dg3 = _ffn_fwd(x1, tgt, ln2_g, lnf_g, w_gate, w_up, w_down)

    dgate, dup, dx1, dg2 = _ffn_bwd(dx2, gate, up, x1, ln2_g, w_gate, w_up, w_down)
    early = _wgrad([act], dx2, 6, None, "wgrad_down")
    early = _wgrad([dgate], h2, 4, early, "wgrad_gate")
    early = _wgrad([dup], h2, 5, early, "wgrad_up")
    dgates, drg, dba, dbb, day, dry, dgr = _merge_bwd(dx1, ga, gb, u, ay, rg, ry, ret_norm_g,
                                                                w_ret, w_attn, w_out)
    early = _wgrad([merged], dx1, 3, early, "wgrad_out")
    early = _wgrad([u], dba, 1, early, "wgrad_ret")
    e32, e16 = _wgrad([ay], dbb, 2, early, "wgrad_attn")

    sems, flight, token = _exchange_start("to_all_start_early", [e16, _empty(N_PEERS * EARLY.half, BF)], N_PEERS,
                                          _plan_to_all(EARLY))
    daq, dak, dav, dsk = _attn_bwd(aq, ak, av, day, attn_sinks, ca, sa, token)
    d_ret = _ret_bwd(rq, rk, rv, dry, states, dmat, qd, kd, cd, cr, sr, token)
    d_proj = [d_ret, drg, daq, dak, dav, dgates]
    l32, l16 = _wgrad(d_proj, h, 0, None, "wgrad_in")
    _, others = _exchange_wait("to_all_wait_early", sems, flight, _plan_to_all(EARLY), after=(l32,))
    share = _plan_share_halves(EARLY, k=N_PEERS)
    sems, flight, token = _exchange_start(
        "to_all_late_share_early_start", [l16, _empty(N_PEERS * LATE.half, BF), _sum_all(place, e32, others, EARLY)],
        N_PEERS + 1, _plan_both(_plan_to_all(LATE), 2, share))
    grad_x, dg1, db = _proj_bwd(d_proj, x, dx1, ln1_g, w_in, token)
    early_shard, = _exchange_wait("share_wait_early", sems, flight[2:], share, after=(grad_x,))

    def finish_late(after):
        _, others = _exchange_wait("to_all_wait_late", sems, flight[:2], _plan_to_all(LATE), after=after)
        return _sum_all(place, l32, others, LATE)

    d_sinks = dsk.reshape(2, 2, PAIRS, 128).sum(-1).transpose(0, 2, 1).reshape(1, ATTN_Q_HEADS)
    small = dict(loss=loss, ln1_g=dg1, b_in=db, ret_norm_g=dgr, attn_sinks=d_sinks,
                 ln2_g=dg2, lnf_g=dg3)
    return grad_x, early_shard, token, finish_late, small


SMALL_ROWS = 16


def _pack_small(ln1, ret, ln2, lnf, b_in, sinks, loss):
    def body(a_ref, b_ref, c_ref, d_ref, bi_ref, sk_ref, ls_ref, o_ref):
        o_ref[...] = jnp.zeros_like(o_ref)
        for row, ref in enumerate((a_ref, b_ref, c_ref, d_ref)):
            o_ref[row:row + 1, :] = ref[...]
        for k in range(6):
            o_ref[4 + k:5 + k, :] = bi_ref[:, k * 1024:(k + 1) * 1024]
        o_ref[10:11, 0:D_IN - 6144] = bi_ref[:, 6144:D_IN]
        o_ref[11:12, 0:ATTN_Q_HEADS] = sk_ref[...]
        o_ref[12:13, 0:LANES] = ls_ref[...]

    return pl.pallas_call(
        body, name="pack_small_grads", in_specs=[_VMEM] * 7, out_specs=_VMEM,
        out_shape=jax.ShapeDtypeStruct((SMALL_ROWS, 1024), F32),
    )(ln1.reshape(1, 1024), ret.reshape(1, 1024), ln2.reshape(1, 1024), lnf.reshape(1, 1024),
      b_in.reshape(1, D_IN), sinks.reshape(1, ATTN_Q_HEADS), loss.reshape(1, LANES))


def _adamw_small(pg, ws, ms, vs):
    n_par = len(ws)
    widths = [w.size for w in ws]
    first_row = [0, 1, 2, 3, 4, 11]

    def body(pg_ref, *refs):
        w_refs, m_refs, v_refs = refs[:n_par], refs[n_par:2 * n_par], refs[2 * n_par:3 * n_par]
        out_refs = refs[3 * n_par:]
        for i, width in enumerate(widths):
            for k, c0 in enumerate(range(0, width, 1024)):
                cols = slice(c0, min(c0 + 1024, width))
                gv = pg_ref[first_row[i] + k:first_row[i] + k + 1, 0:cols.stop - c0]
                go_ref, d_ref, mo_ref, vo_ref = out_refs[4 * i:4 * i + 4]
                go_ref[:, cols] = gv
                d_ref[:, cols], mo_ref[:, cols], vo_ref[:, cols] = _adamw_math(
                    w_refs[i][:, cols], gv, m_refs[i][:, cols], v_refs[i][:, cols])

    flat = lambda a: a.reshape(1, a.size)
    res = pl.pallas_call(
        body, name="adamw_small", in_specs=[_VMEM] * (1 + 3 * n_par), out_specs=[_VMEM] * (4 * n_par),
        out_shape=[jax.ShapeDtypeStruct((1, width), F32) for width in widths for _ in range(4)],
    )(pg, *map(flat, ws), *map(flat, ms), *map(flat, vs))
    return [tuple(r.reshape(w.shape) for r in res[4 * i:4 * i + 4]) for i, w in enumerate(ws)]


def kernel(x, ln1_g, w_in, b_in, ret_norm_g, w_ret_out, attn_sinks, w_attn_out, w_out, ln2_g, w_ffn_gate, w_ffn_up, w_ffn_down, lnf_g, loss_target, m_ln1_g, m_w_in, m_b_in, m_ret_norm_g, m_w_ret_out, m_attn_sinks, m_w_attn_out, m_w_out, m_ln2_g, m_w_ffn_gate, m_w_ffn_up, m_w_ffn_down, m_lnf_g, v_ln1_g, v_w_in, v_b_in, v_ret_norm_g, v_w_ret_out, v_attn_sinks, v_w_attn_out, v_w_out, v_ln2_g, v_w_ffn_gate, v_w_ffn_up, v_w_ffn_down, v_lnf_g):
    xi = lax.axis_index("x")
    yi = lax.axis_index("y")
    ci = lax.axis_index("c")
    j = 2 * xi + yi

    place = jnp.stack([j, ci]).astype(jnp.int32)

    names = ["w_in", "w_ret_out", "w_attn_out", "w_out", "w_ffn_gate", "w_ffn_up", "w_ffn_down"]
    transposed = [True, False, False, False, True, True, False]
    view = lambda a, t: a[0].T if t else a[0]
    big_w = [view(a, t) for a, t in zip([w_in, w_ret_out, w_attn_out, w_out, w_ffn_gate, w_ffn_up, w_ffn_down], transposed)]
    big_m = [view(a, t) for a, t in zip([m_w_in, m_w_ret_out, m_w_attn_out, m_w_out, m_w_ffn_gate, m_w_ffn_up, m_w_ffn_down], transposed)]
    big_v = [view(a, t) for a, t in zip([v_w_in, v_w_ret_out, v_w_attn_out, v_w_out, v_w_ffn_gate, v_w_ffn_up, v_w_ffn_down], transposed)]

    staged_in = _stage_weights(place, big_w[:1], "stage_w_in")[0]
    sems, in_flight, token = _exchange_start("gather_in_start", [staged_in], 3, _plan_gather_halves([0]))
    staged = _stage_weights(place, big_w[1:], "stage_others", after=(token,))
    tables = _tables(x.shape[1], token)
    decays = _decays()
    w_in_full = _exchange_wait("gather_in_wait", sems, in_flight, _plan_gather_halves([0]),
                               after=(*staged, *tables, *decays))[0]
    w_in_full = _exchange("gather_in_forward", [w_in_full], 3, _plan_forward_halves([0]))[0]
    rest = list(range(1, 7))
    crossing = _exchange_start("gather_start", staged, 3 * len(rest), _plan_gather_halves(rest), after=(w_in_full,))
    token = crossing[2]
    forwarding = []

    def forward_weights(after):
        arrived = _exchange_wait("gather_wait", crossing[0], crossing[1], _plan_gather_halves(rest), after=(after,))
        forwarding.extend(_exchange_start("forward_start", arrived, 3 * len(rest), _plan_forward_halves(rest)))
        return forwarding[2]

    def landed_weights(after):
        return _exchange_wait("forward_wait", forwarding[0], forwarding[1], _plan_forward_halves(rest), after=(after,))

    other_weights = (forward_weights, landed_weights)

    grad_x, early_shard, token, finish_late, small = _local_step(
        place, x[0], loss_target[0], ln1_g, b_in, ret_norm_g, attn_sinks, ln2_g, lnf_g.reshape(1, -1),
        w_in_full, other_weights, tables, decays, token)

    upd = {}

    def update(t, shard, after):
        res = _adamw(big_w[t], shard, big_m[t], big_v[t], "adamw_" + names[t], g_row0=GROUP_OF[t].off(t), after=after)
        upd[names[t]] = tuple((r.T if transposed[t] else r)[None] for r in res)
        return res[0]

    done = [update(t, early_shard, token) for t in EARLY.tensors]
    late_half = finish_late((*done, grad_x))
    pg, late_shard = _all_reduce_small(_pack_small(
        small["ln1_g"], small["ret_norm_g"], small["ln2_g"], small["lnf_g"], small["b_in"],
        small["attn_sinks"], small["loss"]), late_half, LATE)
    update(0, late_shard, None)

    small_names = ["ln1_g", "ret_norm_g", "ln2_g", "lnf_g", "b_in", "attn_sinks"]
    upd.update(zip(small_names, _adamw_small(
        pg, [ln1_g, ret_norm_g, ln2_g, lnf_g, b_in, attn_sinks],
        [m_ln1_g, m_ret_norm_g, m_ln2_g, m_lnf_g, m_b_in, m_attn_sinks],
        [v_ln1_g, v_ret_norm_g, v_ln2_g, v_lnf_g, v_b_in, v_attn_sinks])))

    order_w = ["ln1_g", "w_in", "b_in", "ret_norm_g", "w_ret_out", "attn_sinks", "w_attn_out", "w_out",
               "ln2_g", "w_ffn_gate", "w_ffn_up", "w_ffn_down", "lnf_g"]
    outs = [pg[12, 0], grad_x[None]]
    for k in range(4):
        outs += [upd[n][k] for n in order_w]
    return tuple(outs)
```

```python
from typing import NamedTuple

import jax
import jax.numpy as jnp
from jax import lax
from jax.experimental import pallas as pl
from jax.experimental.pallas import tpu as pltpu

D_MODEL = 1024
SEQ_CHUNK = 128
RET_CHUNK = 256
RET_CHUNKS_PER_STEP = 4
RET_HEADS = 4
RET_QK_DIM = 128
ATTN_Q_HEADS = 16
ATTN_HEAD_DIM = 64
D_FF = 2816
D_IN = 6400
ROPE_THETA = 10000.0
EPS = 1e-6
N_CHIPS = 4

ADAM_LR = 0.001
ADAM_B1 = 0.9
ADAM_B2 = 0.999
ADAM_EPS = 1e-08
ADAM_WD = 0.01
ADAM_STEP = 10

OFF_RQ, OFF_RK, OFF_RV, OFF_RG = 0, 512, 1024, 2048
OFF_AQ, OFF_AK, OFF_AV, OFF_GA, OFF_GB = 3072, 4096, 4224, 4352, 5376

V7X_VMEM_LIMIT = 56 * 1024 * 1024
LANES = 128

BF = jnp.bfloat16
F32 = jnp.float32
KEPT = jnp.bfloat16
MESH = pl.DeviceIdType.MESH

_VMEM = pl.BlockSpec(memory_space=pltpu.VMEM)
_SMEM = pl.BlockSpec(memory_space=pltpu.SMEM)
_ANY = pl.BlockSpec(memory_space=pl.ANY)


def _dot(a, b):
    return jnp.dot(a, b, preferred_element_type=F32)


def _dot_nt(a, b):
    return lax.dot_general(a, b, (((1,), (1,)), ((), ())), preferred_element_type=F32)


def _dot_tn(a, b):
    return lax.dot_general(a, b, (((0,), (0,)), ((), ())), preferred_element_type=F32)


def _sigmoid(x):
    return 0.5 * jnp.tanh(0.5 * x) + 0.5


def _rows(tm, n):
    return pl.BlockSpec((tm, n), lambda i: (i, 0))


def _fixed(shape):
    nd = len(shape)
    return pl.BlockSpec(shape, lambda i: (0,) * nd)


def _row_pieces(pieces, tile):
    operands, specs, widths, arity = [], [], [], []
    for piece in pieces:
        parts = piece if isinstance(piece, tuple) else (piece,)
        operands += parts
        specs += [_rows(tile // len(parts), parts[0].shape[1])] * len(parts)
        widths.append(parts[0].shape[1])
        arity.append(len(parts))

    def load(refs):
        vals, k = [], 0
        for n_parts in arity:
            if n_parts == 1:
                vals.append(refs[k][...])
            else:
                vals.append(jnp.concatenate([ref[b * 128:(b + 1) * 128, :] for b in range(tile // 256)
                                             for ref in refs[k:k + 2]], axis=0))
            k += n_parts
        return vals

    return operands, specs, widths, load


def _params(sem, **kw):
    return pltpu.CompilerParams(dimension_semantics=sem, vmem_limit_bytes=V7X_VMEM_LIMIT, **kw)


def _rot_ret(p, cr, sr):
    return p * cr + pltpu.roll(p, 64, 1) * sr


def _rot_ret_bwd(d, cr, sr):
    return d * cr - pltpu.roll(d, 64, 1) * sr


def _attn_partner(p):
    lane = lax.broadcasted_iota(jnp.int32, p.shape, 1)
    first = (lane % 64) < 32
    return jnp.where(first, pltpu.roll(p, 96, 1), pltpu.roll(p, 32, 1))


def _rot_attn(p, ca, sa):
    return p * ca + _attn_partner(p) * sa


def _rot_attn_bwd(d, ca, sa):
    return d * ca - _attn_partner(d) * sa


def _proj_fwd(x, g1, w_in, b_in, cr, sr, ca, sa, after):
    S = x.shape[0]
    tm = 512
    scale = RET_QK_DIM ** -0.5

    def body(x_ref, g_ref, w_ref, b_ref, cr_ref, sr_ref, ca_ref, sa_ref, after_ref,
             h_ref, rq_ref, rk_ref, rv_ref, rg_ref, aq_ref, ak_ref, av_ref, ga_ref, gb_ref):
        xv = x_ref[...]
        r = lax.rsqrt(jnp.mean(xv * xv, axis=-1, keepdims=True) + EPS)
        h = (xv * r * g_ref[...]).astype(BF)
        h_ref[...] = h

        def piece(lo, n):
            return _dot_nt(h, w_ref[lo:lo + n, :]) + b_ref[:, lo:lo + n]

        crv, srv = cr_ref[...], sr_ref[...]
        pq = piece(OFF_RQ, 512)
        pk = piece(OFF_RK, 512)
        for hd in range(RET_HEADS):
            sl = slice(hd * 128, (hd + 1) * 128)
            rq_ref[:, sl] = (_rot_ret(pq[:, sl], crv, srv) * scale).astype(BF)
            rk_ref[:, sl] = _rot_ret(pk[:, sl], crv, srv)
        rv_ref[...] = piece(OFF_RV, 1024).astype(BF)
        rg_ref[...] = piece(OFF_RG, 1024).astype(KEPT)
        cav, sav = ca_ref[...], sa_ref[...]
        pa = piece(OFF_AQ, 1024)
        for blk in range(8):
            sl = slice(blk * 128, (blk + 1) * 128)
            aq_ref[:, sl] = _rot_attn(pa[:, sl], cav, sav).astype(BF)
        pkv = piece(OFF_AK, 256)
        ak_ref[...] = _rot_attn(pkv[:, :128], cav, sav).astype(BF)
        av_ref[...] = pkv[:, 128:].astype(BF)
        ga_ref[...] = piece(OFF_GA, 1024).astype(KEPT)
        gb_ref[...] = piece(OFF_GB, 1024).astype(KEPT)

    outs = [(D_MODEL, BF), (512, BF), (512, F32), (1024, BF), (1024, KEPT),
            (1024, BF), (128, BF), (128, BF), (1024, KEPT), (1024, KEPT)]
    return pl.pallas_call(
        body, name="proj_fwd", grid=(S // tm,),
        in_specs=[_rows(tm, D_MODEL), _fixed((1, D_MODEL)), _VMEM, _VMEM,
                  _rows(tm, 128), _rows(tm, 128), _rows(tm, 128), _rows(tm, 128), _ANY],
        out_specs=[_rows(tm, n) for n, _ in outs],
        out_shape=[jax.ShapeDtypeStruct((S, n), dt) for n, dt in outs],
        compiler_params=_params(("parallel",)),
    )(x, g1, w_in, b_in, cr, sr, ca, sa, after)


def _ret_fwd(rq, rk, rv, dmat, qd, kd, cd):
    S = rq.shape[0]
    N = S // RET_CHUNK
    cps = RET_CHUNKS_PER_STEP
    rows = cps * RET_CHUNK

    def body(q_ref, k_ref, v_ref, dm_ref, qd_ref, kd_ref, cd_ref, y_ref, st_ref, state):
        @pl.when(pl.program_id(0) == 0)
        def _():
            state[...] = jnp.zeros_like(state)

        for hd in range(RET_HEADS):
            qs = slice(hd * 128, (hd + 1) * 128)
            vs = slice(hd * 256, (hd + 1) * 256)
            free = []
            for ch in range(cps):
                rs = slice(ch * RET_CHUNK, (ch + 1) * RET_CHUNK)
                q = q_ref[rs, qs]
                kf = k_ref[rs, qs]
                v = v_ref[rs, vs]
                a = _dot_nt(q, kf.astype(BF)) * dm_ref[hd]
                kdv = (kf * kd_ref[hd]).astype(BF)
                free.append((rs, q, _dot(a.astype(BF), v), _dot_tn(kdv, v)))
            s = state[hd]
            for ch, (rs, q, inner, update) in enumerate(free):
                sb = s.astype(BF)
                st_ref[ch, hd] = sb
                y_ref[rs, vs] = (inner + _dot(q, sb) * qd_ref[hd]).astype(KEPT)
                s = s * cd_ref[hd] + update
            state[hd] = s

    return pl.pallas_call(
        body, name="ret_fwd", grid=(N // cps,),
        in_specs=[_rows(rows, 512), _rows(rows, 512), _rows(rows, 1024), _VMEM, _VMEM, _VMEM, _VMEM],
        out_specs=[_rows(rows, 1024), pl.BlockSpec((cps, RET_HEADS, 128, 256), lambda i: (i, 0, 0, 0))],
        out_shape=[jax.ShapeDtypeStruct((S, 1024), KEPT),
                   jax.ShapeDtypeStruct((N, RET_HEADS, 128, 256), BF)],
        scratch_shapes=[pltpu.VMEM((RET_HEADS, 128, 256), F32)],
        compiler_params=_params(("arbitrary",)),
    )(rq, rk, rv, dmat, qd, kd, cd)


def _kv_halves(prev_ref, cur_ref):
    kk = jnp.concatenate([prev_ref[...], cur_ref[...]], axis=0).astype(F32)
    return kk, pltpu.roll(kk, 64, 1)


ATTN_FWD_BLOCKS_PER_STEP = 4
PAIRS = 4


def _attn_fold(n):
    t = lax.broadcasted_iota(jnp.int32, (128, PAIRS * 128), 0)
    qi = lax.broadcasted_iota(jnp.int32, (128, PAIRS * 128), 1) & 127
    cur = t <= qi
    cur_f = jnp.where(cur, 1.0, 0.0)
    bias = jnp.where(jnp.logical_or(cur, n > 0), 0.0, -1e30)
    return cur, cur_f, 1.0 - cur_f, bias


def _group_operands(g, kk, kk_sw, vv, vv_sw):
    lo = lax.broadcasted_iota(jnp.int32, (256, 128), 1) < 64
    k_a, k_b = (kk, kk_sw) if g == 0 else (kk_sw, kk)
    v_a, v_b = (vv, vv_sw) if g == 0 else (vv_sw, vv)
    return (jnp.concatenate([jnp.where(lo, k_a, 0.0), jnp.where(lo, 0.0, k_b)], axis=0).astype(BF),
            jnp.concatenate([jnp.where(lo, v_a, 0.0), jnp.where(lo, 0.0, v_b)], axis=0).astype(BF))


def _stack_pairs(ref, g, rows=slice(None)):
    return jnp.concatenate([ref[rows, (4 * g + p) * 128:(4 * g + p + 1) * 128] for p in range(PAIRS)], axis=0)


def _sink_row(sk_ref, g, which):
    return jnp.concatenate([jnp.full((1, 128), sk_ref[0, 8 * g + 2 * p + which], F32) for p in range(PAIRS)], axis=1)


def _fold(full, cur):
    return jnp.where(cur, full[128:], full[:128])


def _attn_probs(raw, fold, sink):
    s = _fold(raw, fold[0]) * (ATTN_HEAD_DIM ** -0.5) + fold[3]
    m = jnp.maximum(jnp.max(s, axis=0, keepdims=True), sink)
    e = jnp.exp(s - m)
    es = jnp.exp(sink - m)
    inv = 1.0 / (jnp.sum(e, axis=0, keepdims=True) + es)
    return e * inv, es * inv


def _unfold(p, fold):
    return jnp.concatenate([p * fold[2], p * fold[1]], axis=0).astype(BF)


def _attn_fwd(aq, ak, av, sinks, after):
    S = aq.shape[0]
    N = S // SEQ_CHUNK
    nb = ATTN_FWD_BLOCKS_PER_STEP

    def body(sk_ref, q_ref, kp_ref, kc_ref, vp_ref, vc_ref, after_ref, o_ref):
        kall, kall_sw = _kv_halves(kp_ref, kc_ref)
        vall, vall_sw = _kv_halves(vp_ref, vc_ref)
        blocks = []
        for sb in range(nb):
            kv = slice(sb * 128, sb * 128 + 256)
            rows = slice(sb * 128, (sb + 1) * 128)
            ops = [_group_operands(g, kall[kv], kall_sw[kv], vall[kv], vall_sw[kv]) for g in range(2)]
            raw = [_dot_nt(ops[g][0], _stack_pairs(q_ref, g, rows)) for g in range(2)]
            blocks.append((rows, ops, raw, _attn_fold(nb * pl.program_id(0) + sb)))
        for rows, ops, raw, fold in blocks:
            for g in range(2):
                probs = [_unfold(_attn_probs(raw[g][w * 256:(w + 1) * 256], fold, _sink_row(sk_ref, g, w))[0], fold)
                         for w in range(2)]
                o = _dot_tn(jnp.concatenate(probs, axis=0), ops[g][1])
                for p in range(PAIRS):
                    o_ref[rows, (4 * g + p) * 128:(4 * g + p + 1) * 128] = o[p * 128:(p + 1) * 128].astype(BF)

    prev = lambda i: (jnp.maximum(nb * i - 1, 0), 0)
    return pl.pallas_call(
        body, name="attn_fwd", grid=(N // nb,),
        in_specs=[_SMEM, _rows(nb * 128, 1024), pl.BlockSpec((128, 128), prev), _rows(nb * 128, 128),
                  pl.BlockSpec((128, 128), prev), _rows(nb * 128, 128), _ANY],
        out_specs=_rows(nb * 128, 1024),
        out_shape=jax.ShapeDtypeStruct((S, 1024), BF),
        compiler_params=_params(("parallel",)),
    )(sinks, aq, ak, ak, av, av, after)


def _group_norm(y):
    mu = jnp.mean(y, axis=-1, keepdims=True)
    yc = y - mu
    rstd = lax.rsqrt(jnp.mean(yc * yc, axis=-1, keepdims=True) + EPS)
    return yc * rstd, rstd


def _merge_fwd(x, ry, rg, ay, ga, gb, g_ret, w_ret, w_attn, w_out):
    S = x.shape[0]
    tm = 512

    def body(x_ref, ry_ref, rg_ref, ay_ref, ga_ref, gb_ref, gr_ref, wr_ref, wa_ref, wo_ref,
             u_ref, mg_ref, x1_ref):
        bb = _dot(ay_ref[...], wa_ref[...])
        for hd in range(RET_HEADS):
            vs = slice(hd * 256, (hd + 1) * 256)
            yn, _ = _group_norm(ry_ref[:, vs].astype(F32))
            rgv = rg_ref[:, vs].astype(F32)
            u_ref[:, vs] = (rgv * _sigmoid(rgv) * (yn * gr_ref[:, vs])).astype(BF)
        gated_b = _sigmoid(gb_ref[...].astype(F32)) * bb
        sa = _sigmoid(ga_ref[...].astype(F32))
        ba = _dot(u_ref[...], wr_ref[...])
        merged = (sa * ba + gated_b).astype(BF)
        mg_ref[...] = merged
        x1_ref[...] = x_ref[...] + _dot(merged, wo_ref[...])

    outs = [BF, BF, F32]
    return pl.pallas_call(
        body, name="merge_fwd", grid=(S // tm,),
        in_specs=[_rows(tm, 1024)] * 6 + [_fixed((1, 1024)), _VMEM, _VMEM, _VMEM],
        out_specs=[_rows(tm, 1024)] * 3,
        out_shape=[jax.ShapeDtypeStruct((S, 1024), dt) for dt in outs],
        compiler_params=_params(("parallel",)),
    )(x, ry, rg, ay, ga, gb, g_ret, w_ret, w_attn, w_out)


def _ffn_fwd(x1, tgt, g2, g3, w_gate, w_up, w_down):
    S = x1.shape[0]
    tm = 256

    def body(x1_ref, t_ref, g2_ref, g3_ref, wg_ref, wu_ref, wd_ref,
             gate_ref, up_ref, h2_ref, act_ref, dx2_ref, loss_ref, dg3_ref):
        @pl.when(pl.program_id(0) == 0)
        def _():
            loss_ref[...] = jnp.zeros_like(loss_ref)
            dg3_ref[...] = jnp.zeros_like(dg3_ref)

        x1v = x1_ref[...]
        r2 = lax.rsqrt(jnp.mean(x1v * x1v, axis=-1, keepdims=True) + EPS)
        h2 = (x1v * r2 * g2_ref[...]).astype(BF)
        h2_ref[...] = h2
        gate = _dot_nt(h2, wg_ref[...])
        up = _dot_nt(h2, wu_ref[...])
        gate_ref[...] = gate
        up_ref[...] = up
        act = (gate * _sigmoid(gate) * up).astype(BF)
        act_ref[...] = act
        x2 = x1v + _dot(act, wd_ref[...])
        r3 = lax.rsqrt(jnp.mean(x2 * x2, axis=-1, keepdims=True) + EPS)
        xh = x2 * r3
        g3v = g3_ref[...]
        err = xh * g3v - t_ref[...]
        loss_ref[...] += 0.5 * jnp.sum(jnp.mean(err * err, axis=-1, keepdims=True), axis=0, keepdims=True)
        dy = err * (1.0 / D_MODEL)
        dg3_ref[...] += jnp.sum(dy * xh, axis=0, keepdims=True)
        dxh = dy * g3v
        dx2_ref[...] = r3 * (dxh - xh * jnp.mean(dxh * xh, axis=-1, keepdims=True))

    return pl.pallas_call(
        body, name="ffn_fwd", grid=(S // tm,),
        in_specs=[_rows(tm, 1024), _rows(tm, 1024), _fixed((1, 1024)), _fixed((1, 1024)), _VMEM, _VMEM, _VMEM],
        out_specs=[_rows(tm, D_FF), _rows(tm, D_FF), _rows(tm, 1024), _rows(tm, D_FF), _rows(tm, 1024),
                   _fixed((1, LANES)), _fixed((1, 1024))],
        out_shape=[jax.ShapeDtypeStruct((S, D_FF), F32), jax.ShapeDtypeStruct((S, D_FF), F32),
                   jax.ShapeDtypeStruct((S, 1024), BF), jax.ShapeDtypeStruct((S, D_FF), BF),
                   jax.ShapeDtypeStruct((S, 1024), F32),
                   jax.ShapeDtypeStruct((1, LANES), F32), jax.ShapeDtypeStruct((1, 1024), F32)],
        compiler_params=_params(("arbitrary",)),
    )(x1, tgt, g2, g3, w_gate, w_up, w_down)


def _ffn_bwd(dx2, gate, up, x1, g2, w_gate, w_up, w_down):
    S = x1.shape[0]
    tm = 256

    def body(dx2_ref, gate_ref, up_ref, x1_ref, g2_ref, wg_ref, wu_ref, wd_ref,
             dgate_ref, dup_ref, dx1_ref, dg2_ref):
        @pl.when(pl.program_id(0) == 0)
        def _():
            dg2_ref[...] = jnp.zeros_like(dg2_ref)

        dx2v = dx2_ref[...]
        dact = _dot_nt(dx2v.astype(BF), wd_ref[...])
        gate = gate_ref[...]
        sg = _sigmoid(gate)
        dgate = (dact * up_ref[...] * (sg * (1.0 + gate * (1.0 - sg)))).astype(BF)
        dup = (dact * (gate * sg)).astype(BF)
        dgate_ref[...] = dgate
        dup_ref[...] = dup
        dh2 = _dot(dgate, wg_ref[...]) + _dot(dup, wu_ref[...])
        x1v = x1_ref[...]
        r2 = lax.rsqrt(jnp.mean(x1v * x1v, axis=-1, keepdims=True) + EPS)
        xh = x1v * r2
        dg2_ref[...] += jnp.sum(dh2 * xh, axis=0, keepdims=True)
        dxh = dh2 * g2_ref[...]
        dx1_ref[...] = dx2v + r2 * (dxh - xh * jnp.mean(dxh * xh, axis=-1, keepdims=True))

    return pl.pallas_call(
        body, name="ffn_bwd", grid=(S // tm,),
        in_specs=[_rows(tm, 1024), _rows(tm, D_FF), _rows(tm, D_FF), _rows(tm, 1024), _fixed((1, 1024)),
                  _VMEM, _VMEM, _VMEM],
        out_specs=[_rows(tm, D_FF), _rows(tm, D_FF), _rows(tm, 1024), _fixed((1, 1024))],
        out_shape=[jax.ShapeDtypeStruct((S, D_FF), BF), jax.ShapeDtypeStruct((S, D_FF), BF),
                   jax.ShapeDtypeStruct((S, 1024), F32), jax.ShapeDtypeStruct((1, 1024), F32)],
        compiler_params=_params(("arbitrary",)),
    )(dx2, gate, up, x1, g2, w_gate, w_up, w_down)


def _merge_bwd(dx1, ga, gb, u, ay, rg, ry, g_ret, w_ret, w_attn, w_out):
    S = dx1.shape[0]
    tm = 512

    def body(dx1_ref, ga_ref, gb_ref, u_ref, ay_ref, rg_ref, ry_ref, gr_ref, wr_ref, wa_ref, wo_ref,
             dgates_ref, drg_ref, dba_ref, dbb_ref, day_ref, dry_ref, dgr_ref):
        @pl.when(pl.program_id(0) == 0)
        def _():
            dgr_ref[...] = jnp.zeros_like(dgr_ref)

        dm = _dot_nt(dx1_ref[...].astype(BF), wo_ref[...])
        ba = _dot(u_ref[...], wr_ref[...])
        bb = _dot(ay_ref[...], wa_ref[...])
        sa = _sigmoid(ga_ref[...].astype(F32))
        sb = _sigmoid(gb_ref[...].astype(F32))
        dga = (dm * ba * (sa * (1.0 - sa))).astype(BF)
        dgb = (dm * bb * (sb * (1.0 - sb))).astype(BF)
        dgates_ref[:, :1024] = dga
        dgates_ref[:, 1024:] = dgb
        dba = (dm * sa).astype(BF)
        dbb = (dm * sb).astype(BF)
        dba_ref[...] = dba
        dbb_ref[...] = dbb
        day_ref[...] = _dot_nt(dbb, wa_ref[...]).astype(BF)
        du = _dot_nt(dba, wr_ref[...])
        for hd in range(RET_HEADS):
            vs = slice(hd * 256, (hd + 1) * 256)
            yn, rstd = _group_norm(ry_ref[:, vs].astype(F32))
            rgv = rg_ref[:, vs].astype(F32)
            sr = _sigmoid(rgv)
            silu = rgv * sr
            duv = du[:, vs]
            grv = gr_ref[:, vs]
            drg = (duv * (yn * grv) * (sr * (1.0 + rgv * (1.0 - sr)))).astype(BF)
            drg_ref[:, vs] = drg
            dyg = duv * silu
            dgr_ref[:, vs] += jnp.sum(dyg * yn, axis=0, keepdims=True)
            dyn = dyg * grv
            dry_ref[:, vs] = (rstd * (dyn - jnp.mean(dyn, axis=-1, keepdims=True)
                                      - yn * jnp.mean(dyn * yn, axis=-1, keepdims=True))).astype(KEPT)

    return pl.pallas_call(
        body, name="merge_bwd", grid=(S // tm,),
        in_specs=[_rows(tm, 1024)] * 7 + [_fixed((1, 1024)), _VMEM, _VMEM, _VMEM],
        out_specs=[_rows(tm, 2048), _rows(tm, 1024), _rows(tm, 1024), _rows(tm, 1024), _rows(tm, 1024),
                   _rows(tm, 1024), _fixed((1, 1024))],
        out_shape=[jax.ShapeDtypeStruct((S, 2048), BF), jax.ShapeDtypeStruct((S, 1024), BF),
                   jax.ShapeDtypeStruct((S, 1024), BF), jax.ShapeDtypeStruct((S, 1024), BF),
                   jax.ShapeDtypeStruct((S, 1024), BF), jax.ShapeDtypeStruct((S, 1024), KEPT),
                   jax.ShapeDtypeStruct((1, 1024), F32)],
        compiler_params=_params(("arbitrary",)),
    )(dx1, ga, gb, u, ay, rg, ry, g_ret, w_ret, w_attn, w_out)


def _attn_bwd(aq, ak, av, day, sinks, ca, sa, after):
    S = aq.shape[0]
    N = S // SEQ_CHUNK
    steps = N // 2

    def body(sk_ref, q_ref, kp_ref, kc_ref, vp_ref, vc_ref, do_ref, cq_ref, sq_ref, cp_ref, sp_ref, after_ref,
             dq_ref, dke_ref, dve_ref, dko_ref, dvo_ref, dsk_ref, carry_k, carry_v):
        i = pl.program_id(0)

        @pl.when(i == 0)
        def _():
            carry_k[...] = jnp.zeros_like(carry_k)
            carry_v[...] = jnp.zeros_like(carry_v)
            dsk_ref[...] = jnp.zeros_like(dsk_ref)

        @pl.when(i < steps)
        def _():
            kall, kall_sw = _kv_halves(kp_ref, kc_ref)
            vall, vall_sw = _kv_halves(vp_ref, vc_ref)
            lo = lax.broadcasted_iota(jnp.int32, (256, 128), 1) < 64
            blocks = []
            for sb in range(2):
                kv = slice(sb * 128, sb * 128 + 256)
                rows = slice(sb * 128, (sb + 1) * 128)
                ops = [_group_operands(g, kall[kv], kall_sw[kv], vall[kv], vall_sw[kv]) for g in range(2)]
                stacked = [(_stack_pairs(q_ref, g, rows), _stack_pairs(do_ref, g, rows)) for g in range(2)]
                raw = [_dot_nt(ops[g][0], stacked[g][0]) for g in range(2)]
                raw_dp = [_dot_nt(ops[g][1], stacked[g][1]) for g in range(2)]
                blocks.append((rows, ops, stacked, raw, raw_dp, _attn_fold(2 * i + sb)))
            grads = []
            d_sink = [jnp.zeros((1, PAIRS * 128), F32) for _ in range(4)]
            for rows, ops, stacked, raw, raw_dp, fold in blocks:
                cur = fold[0]
                cq, sq = cq_ref[rows, :], sq_ref[rows, :]
                dk_groups, dv_groups = [], []
                for g in range(2):
                    qs, dos = stacked[g]
                    ds_parts, p_parts = [], []
                    for which in range(2):
                        half = slice(which * 256, (which + 1) * 256)
                        pr, ps = _attn_probs(raw[g][half], fold, _sink_row(sk_ref, g, which))
                        dp = _fold(raw_dp[g][half], cur)
                        dr = jnp.sum(pr * dp, axis=0, keepdims=True)
                        ds_parts.append(_unfold(pr * (dp - dr) * (ATTN_HEAD_DIM ** -0.5), fold))
                        p_parts.append(_unfold(pr, fold))
                        d_sink[2 * g + which] = d_sink[2 * g + which] - ps * dr
                    ds = jnp.concatenate(ds_parts, axis=0)
                    dq = _dot_tn(ds, ops[g][0])
                    dk_both = _dot(ds, qs)
                    dv_both = _dot(jnp.concatenate(p_parts, axis=0), dos)
                    for p in range(PAIRS):
                        sl = slice((4 * g + p) * 128, (4 * g + p + 1) * 128)
                        dq_ref[rows, sl] = _rot_attn_bwd(dq[p * 128:(p + 1) * 128], cq, sq).astype(BF)
                    tk = jnp.where(lo, dk_both[:256], dk_both[256:])
                    tv = jnp.where(lo, dv_both[:256], dv_both[256:])
                    dk_groups.append(tk + pltpu.roll(tk, 64, 1))
                    dv_groups.append(tv + pltpu.roll(tv, 64, 1))
                grads.append((jnp.where(lo, dk_groups[0], dk_groups[1]), jnp.where(lo, dv_groups[0], dv_groups[1])))
            dsk_ref[...] += jnp.concatenate(d_sink, axis=0)
            (dk_a, dv_a), (dk_b, dv_b) = grads
            dko_ref[...] = _rot_attn_bwd(carry_k[...] + dk_a[:128], cp_ref[...], sp_ref[...]).astype(BF)
            dvo_ref[...] = (carry_v[...] + dv_a[:128]).astype(BF)
            dke_ref[...] = _rot_attn_bwd(dk_a[128:] + dk_b[:128], cq_ref[0:128, :], sq_ref[0:128, :]).astype(BF)
            dve_ref[...] = (dv_a[128:] + dv_b[:128]).astype(BF)
            carry_k[...] = dk_b[128:]
            carry_v[...] = dv_b[128:]

        @pl.when(i == steps)
        def _():
            dko_ref[...] = _rot_attn_bwd(carry_k[...], cp_ref[...], sp_ref[...]).astype(BF)
            dvo_ref[...] = carry_v[...].astype(BF)

    two = lambda i: (jnp.minimum(i, steps - 1), 0)
    before = lambda i: (jnp.maximum(2 * i - 1, 0), 0)
    odd = lambda i: (jnp.maximum(i - 1, 0), 0)
    blk = lambda r, w, im: pl.BlockSpec((r, w), im)
    half = jax.ShapeDtypeStruct((S // 2, 128), BF)
    dq, dk_even, dv_even, dk_odd, dv_odd, dsk = pl.pallas_call(
        body, name="attn_bwd", grid=(steps + 1,),
        in_specs=[_SMEM, blk(256, 1024, two), blk(128, 128, before), blk(256, 128, two), blk(128, 128, before),
                  blk(256, 128, two), blk(256, 1024, two), blk(256, 128, two), blk(256, 128, two),
                  blk(128, 128, before), blk(128, 128, before), _ANY],
        out_specs=[blk(256, 1024, two), blk(128, 128, two), blk(128, 128, two), blk(128, 128, odd),
                   blk(128, 128, odd), _fixed((4, PAIRS * 128))],
        out_shape=[jax.ShapeDtypeStruct((S, 1024), BF), half, half, half, half,
                   jax.ShapeDtypeStruct((4, PAIRS * 128), F32)],
        scratch_shapes=[pltpu.VMEM((128, 128), F32), pltpu.VMEM((128, 128), F32)],
        compiler_params=_params(("arbitrary",)),
    )(sinks, aq, ak, ak, av, av, day, ca, sa, ca, sa, after)

    return dq, (dk_even, dk_odd), (dv_even, dv_odd), dsk


def _ret_bwd(rq, rk, rv, dry, states, dmat, qd, kd, cd, cr, sr, after):
    S = rq.shape[0]
    N = S // RET_CHUNK
    scale = RET_QK_DIM ** -0.5
    cps = RET_CHUNKS_PER_STEP
    steps = N // cps

    def body(q_ref, k_ref, v_ref, dy_ref, st_ref, dm_ref, qd_ref, kd_ref, cd_ref, cr_ref, sr_ref, after_ref,
             out_ref, dstate):
        @pl.when(pl.program_id(0) == 0)
        def _():
            dstate[...] = jnp.zeros_like(dstate)

        for hd in range(RET_HEADS):
            qs = slice(hd * 128, (hd + 1) * 128)
            vs = slice(hd * 256, (hd + 1) * 256)
            dmv = dm_ref[hd]
            free = []
            for ch in range(cps):
                rs = slice(ch * RET_CHUNK, (ch + 1) * RET_CHUNK)
                crv, srv = cr_ref[rs, :], sr_ref[rs, :]
                q = q_ref[rs, qs]
                kf = k_ref[rs, qs]
                k = kf.astype(BF)
                v = v_ref[rs, vs]
                do = dy_ref[rs, vs].astype(F32)
                dob = do.astype(BF)
                doq = (do * qd_ref[hd]).astype(BF)
                a = (_dot_nt(q, k) * dmv).astype(BF)
                da = (_dot_nt(dob, v) * dmv).astype(BF)
                dq = _dot(da, k) + _dot_nt(doq, st_ref[ch, hd])
                out_ref[rs, qs] = _rot_ret_bwd(dq * scale, crv, srv).astype(BF)
                free.append((rs, crv, srv, v, (kf * kd_ref[hd]).astype(BF),
                             _dot_tn(a, dob), _dot_tn(da, q), _dot_tn(q, doq)))
            ds = dstate[hd]
            for rs, crv, srv, v, kdv, dv_free, dk_free, ds_update in reversed(free):
                dsb = ds.astype(BF)
                dv = dv_free + _dot(kdv, dsb)
                dk = dk_free + _dot_nt(v, dsb) * kd_ref[hd]
                ds = ds * cd_ref[hd] + ds_update
                out_ref[rs, 512 + hd * 128:512 + (hd + 1) * 128] = _rot_ret_bwd(dk, crv, srv).astype(BF)
                out_ref[rs, 1024 + hd * 256:1024 + (hd + 1) * 256] = dv.astype(BF)
            dstate[hd] = ds

    rev = lambda w: pl.BlockSpec((cps * RET_CHUNK, w), lambda i: (steps - 1 - i, 0))
    return pl.pallas_call(
        body, name="ret_bwd", grid=(steps,),
        in_specs=[rev(512), rev(512), rev(1024), rev(1024),
                  pl.BlockSpec((cps, RET_HEADS, 128, 256), lambda i: (steps - 1 - i, 0, 0, 0)),
                  _VMEM, _VMEM, _VMEM, _VMEM, rev(128), rev(128), _ANY],
        out_specs=rev(2048),
        out_shape=jax.ShapeDtypeStruct((S, 2048), BF),
        scratch_shapes=[pltpu.VMEM((RET_HEADS, 128, 256), F32)],
        compiler_params=_params(("arbitrary",)),
    )(rq, rk, rv, dry, states, dmat, qd, kd, cd, cr, sr, after)


def _proj_bwd(d_pieces, x, dx1, g1, w_in_t, after):
    S = x.shape[0]
    tm = 512
    d_operands, d_specs, widths, load = _row_pieces(d_pieces, tm)
    assert sum(widths) == D_IN
    nd = len(d_operands)

    def body(*refs):
        x_ref, dx1_ref, g_ref, w_ref, after_ref, gx_ref, dg_ref, db_ref = refs[nd:]

        @pl.when(pl.program_id(0) == 0)
        def _():
            dg_ref[...] = jnp.zeros_like(dg_ref)
            db_ref[...] = jnp.zeros_like(db_ref)

        d_all = jnp.concatenate(load(refs[:nd]), axis=1)
        dh = _dot(d_all, w_ref[...])
        db_ref[...] += jnp.sum(d_all.astype(F32), axis=0, keepdims=True)
        xv = x_ref[...]
        r = lax.rsqrt(jnp.mean(xv * xv, axis=-1, keepdims=True) + EPS)
        xh = xv * r
        dg_ref[...] += jnp.sum(dh * xh, axis=0, keepdims=True)
        dxh = dh * g_ref[...]
        gx_ref[...] = dx1_ref[...] + r * (dxh - xh * jnp.mean(dxh * xh, axis=-1, keepdims=True))

    return pl.pallas_call(
        body, name="proj_bwd", grid=(S // tm,),
        in_specs=d_specs + [_rows(tm, 1024), _rows(tm, 1024), _fixed((1, 1024)), _VMEM, _ANY],
        out_specs=[_rows(tm, 1024), _fixed((1, 1024)), _fixed((1, D_IN))],
        out_shape=[jax.ShapeDtypeStruct((S, 1024), F32), jax.ShapeDtypeStruct((1, 1024), F32),
                   jax.ShapeDtypeStruct((1, D_IN), F32)],
        compiler_params=_params(("arbitrary",)),
    )(*d_operands, x, dx1, g1, w_in_t, after)


SHARD_ROWS = (1600, 256, 256, 256, 704, 704, 704)


class Group(NamedTuple):
    name: str
    tensors: tuple
    offs: tuple
    block: int
    tile: int

    @property
    def half(self):
        return self.block // 2

    def off(self, t):
        return self.offs[self.tensors.index(t)]


LATE = Group("late", (0,), (0,), 1600, 400)
EARLY = Group("early", (4, 5, 6, 1, 2, 3), (0, 704, 1408, 2112, 2368, 2624), 2880, 480)
GROUP_OF = {t: g for g in (LATE, EARLY) for t in g.tensors}


def _wgrad(a_list, b, t, packed, name):
    S = b.shape[0]
    M = sum((a[0] if isinstance(a, tuple) else a).shape[1] for a in a_list)
    ts = 1024 if M <= 4096 else 512
    steps = S // ts
    a_operands, a_specs, widths, load = _row_pieces(a_list, ts)
    na = len(a_operands)
    grp = GROUP_OF[t]
    r, off = SHARD_ROWS[t], grp.off(t)
    assert M == N_CHIPS * r
    segments = [(chip * r, r, chip * grp.block + off) for chip in range(N_CHIPS)]
    stage_rows = r
    n_in = na + 1 + (2 if packed is not None else 0)

    def body(*refs):
        a_refs, b_ref = refs[:na], refs[na]
        g32_ref, gb_ref, acc, stage, sems = refs[n_in:]
        i = pl.program_id(0)

        @pl.when(i == 0)
        def _():
            acc[...] = jnp.zeros_like(acc)

        bv = b_ref[...].astype(BF)
        lo = 0
        for a, n in zip(load(a_refs), widths):
            acc[lo:lo + n, :] += _dot_tn(a.astype(BF), bv)
            lo += n

        @pl.when(i == steps - 1)
        def _():
            for a0, n, p0 in segments:
                c32 = pltpu.make_async_copy(acc.at[a0:a0 + n, :], g32_ref.at[pl.ds(p0, n), :], sems.at[0])
                c32.start()
                stage[0:n, :] = acc[a0:a0 + n, :].astype(BF)
                c16 = pltpu.make_async_copy(stage.at[0:n, :], gb_ref.at[pl.ds(p0, n), :], sems.at[1])
                c16.start()
                c32.wait()
                c16.wait()

    shapes = [jax.ShapeDtypeStruct((N_CHIPS * grp.block, 1024), F32),
              jax.ShapeDtypeStruct((N_CHIPS * grp.block, 1024), BF)]
    operands = a_operands + [b]
    in_specs = a_specs + [_rows(ts, 1024)]
    aliases = {}
    if packed is not None:
        operands += list(packed)
        in_specs += [_ANY, _ANY]
        aliases = {na + 1: 0, na + 2: 1}
    return pl.pallas_call(
        body, name=name, grid=(steps,),
        in_specs=in_specs, out_specs=[_ANY, _ANY], out_shape=shapes,
        scratch_shapes=[pltpu.VMEM((M, 1024), F32), pltpu.VMEM((stage_rows, 1024), BF),
                        pltpu.SemaphoreType.DMA((2,))],
        input_output_aliases=aliases,
        compiler_params=_params(("arbitrary",)),
    )(*operands)


def _stage_weights(place, ws, name, after=()):
    nw = len(ws)

    def body(p_ref, *refs):
        for w_ref, o_ref in zip(refs[:nw], refs[nw + len(after):]):
            o_ref[...] = w_ref[...].astype(BF)

    return pl.pallas_call(
        body, name=name,
        grid_spec=pltpu.PrefetchScalarGridSpec(
            num_scalar_prefetch=1, grid=(1,),
            in_specs=[pl.BlockSpec(w.shape, lambda i, p: (0, 0)) for w in ws] + [_ANY] * len(after),
            out_specs=[pl.BlockSpec(w.shape, lambda i, p: (p[0], 0)) for w in ws]),
        out_shape=[jax.ShapeDtypeStruct((N_CHIPS * w.shape[0], 1024), BF) for w in ws],
        compiler_params=_params(("arbitrary",)),
    )(place, *ws, *after)


N_PEERS = 7


def _sum_all(place, g32, others, grp):
    nb = grp.half // grp.tile

    def body(p_ref, g_ref, *refs):
        acc = g_ref[...]
        for r_ref in refs[:N_PEERS]:
            acc = acc + r_ref[...].astype(F32)
        refs[-1][...] = acc

    blk = lambda f: pl.BlockSpec((grp.tile, 1024), f)
    slot = lambda k: blk(lambda i, p: (k * nb + i, 0))
    return pl.pallas_call(
        body, name="sum_all_" + grp.name,
        grid_spec=pltpu.PrefetchScalarGridSpec(
            num_scalar_prefetch=1, grid=(nb,),
            in_specs=[blk(lambda i, p: ((2 * p[0] + p[1]) * nb + i, 0))] + [slot(k) for k in range(N_PEERS)],
            out_specs=blk(lambda i, p: (p[1] * nb + i, 0))),
        out_shape=jax.ShapeDtypeStruct((grp.block, 1024), F32),
        compiler_params=_params(("arbitrary",)),
    )(place, g32, *[others] * N_PEERS)


def _adamw_math(w, gv, m, v):
    mn = ADAM_B1 * m + (1.0 - ADAM_B1) * gv
    vn = ADAM_B2 * v + (1.0 - ADAM_B2) * (gv * gv)
    m_hat = mn / (1.0 - ADAM_B1 ** ADAM_STEP)
    v_hat = vn / (1.0 - ADAM_B2 ** ADAM_STEP)
    return -ADAM_LR * (m_hat / (jnp.sqrt(v_hat) + ADAM_EPS) + ADAM_WD * w), mn, vn


def _adamw(w, g, m, v, name, g_row0=0, after=None):
    r = w.shape[0]
    tr = next((c for c in (352, 320, 64) if r % c == 0 and g_row0 % c == 0), r)
    assert g_row0 % tr == 0
    g0 = g_row0 // tr
    extra = [] if after is None else [after]

    def body(w_ref, g_ref, m_ref, v_ref, *rest):
        go_ref, d_ref, mo_ref, vo_ref = rest[len(extra):]
        gv = g_ref[...]
        go_ref[...] = gv
        d_ref[...], mo_ref[...], vo_ref[...] = _adamw_math(w_ref[...], gv, m_ref[...], v_ref[...])

    return pl.pallas_call(
        body, name=name, grid=(r // tr,),
        in_specs=[_rows(tr, 1024), pl.BlockSpec((tr, 1024), lambda i: (g0 + i, 0)), _rows(tr, 1024), _rows(tr, 1024)]
        + [_ANY] * len(extra),
        out_specs=[_rows(tr, 1024)] * 4,
        out_shape=[jax.ShapeDtypeStruct((r, 1024), F32)] * 4,
        compiler_params=_params(("parallel",)),
    )(w, g, m, v, *extra)


def _place():
    x, y, c = lax.axis_index("x"), lax.axis_index("y"), lax.axis_index("c")
    chips = [(1 - x, y), (x, 1 - y), (1 - x, 1 - y)]
    return x, y, c, chips


_HBM =pl.BlockSpec(memory_space=pltpu.HBM)
_SEM = pl.BlockSpec(memory_space=pltpu.SEMAPHORE)
_DATAFLOW = pltpu.SideEffectType.DATAFLOW_SIDE_EFFECTING


def _exchange(name, arrays, n, plan):
    na = len(arrays)

    def body(*refs):
        pairs = plan(refs[na:2 * na], refs[2 * na], refs[2 * na + 1])
        for send, _ in pairs:
            send.start()
        for send, recv in pairs:
            send.wait_send()
            recv.wait_recv()

    return pl.pallas_call(
        body, name=name, in_specs=[_ANY] * na, out_specs=[_ANY] * na,
        out_shape=[jax.ShapeDtypeStruct(a.shape, a.dtype) for a in arrays],
        scratch_shapes=[pltpu.SemaphoreType.DMA((n,)), pltpu.SemaphoreType.DMA((n,))],
        input_output_aliases={i: i for i in range(na)},
    )(*arrays)


def _exchange_start(name, arrays, n, plan, after=()):
    na = len(arrays)

    def body(*refs):
        for send, _ in plan(refs[:na], refs[na + len(after)], refs[na + len(after) + 1]):
            send.start()
        refs[-1][...] = jnp.zeros_like(refs[-1])

    out = pl.pallas_call(
        body, name=name,
        out_shape=(pltpu.SemaphoreType.DMA((n,)), pltpu.SemaphoreType.DMA((n,)),
                   *[pltpu.HBM(a.shape, a.dtype) for a in arrays], jax.ShapeDtypeStruct((8, LANES), F32)),
        in_specs=[_HBM] * na + [_ANY] * len(after), out_specs=(_SEM, _SEM, *[_HBM] * na, _VMEM),
        input_output_aliases={i: 2 + i for i in range(na)},
        compiler_params=pltpu.CompilerParams(has_side_effects=_DATAFLOW),
    )(*[pltpu.with_memory_space_constraint(a, pltpu.HBM) for a in arrays], *after)
    return (out[0], out[1]), list(out[2:2 + na]), out[-1]


def _exchange_wait(name, sems, in_flight, plan, after=()):
    na = len(in_flight)

    def body(*refs):
        for send, recv in plan(refs[:na], refs[na], refs[na + 1]):
            send.wait_send()
            recv.wait_recv()

    out = pl.pallas_call(
        body, name=name,
        out_shape=tuple(pltpu.HBM(a.shape, a.dtype) for a in in_flight),
        in_specs=[_HBM] * na + [_SEM, _SEM] + [_ANY] * len(after), out_specs=tuple([_HBM] * na),
        input_output_aliases={i: i for i in range(na)},
        compiler_params=pltpu.CompilerParams(has_side_effects=_DATAFLOW),
    )(*in_flight, *sems, *after)
    return list(out)


def _remote(src, dst, send_sems, recv_sems, k, to):
    return pltpu.make_async_remote_copy(src_ref=src, dst_ref=dst, send_sem=send_sems.at[k], recv_sem=recv_sems.at[k],
                                        device_id=to, device_id_type=MESH)


def _half_rows(ref, tid, pj, half):
    h = SHARD_ROWS[tid] // 2
    return ref.at[pl.ds((2 * pj + half) * h, h), :]


def _plan_gather_halves(tids):
    def plan(refs, ss, rs):
        x, y, c, chips = _place()
        pairs = []
        for t, tid in enumerate(tids):
            mine = _half_rows(refs[t], tid, 2 * x + y, c)
            for k, (cx, cy) in enumerate(chips):
                theirs = _half_rows(refs[t], tid, 2 * cx + cy, c)
                pairs.append((_remote(mine, mine, ss, rs, 3 * t + k, (cx, cy, c)),
                              _remote(theirs, theirs, ss, rs, 3 * t + k, (x, y, c))))
        return pairs
    return plan


def _plan_forward_halves(tids):
    def plan(refs, ss, rs):
        x, y, c, chips = _place()
        pairs = []
        for t, tid in enumerate(tids):
            for k, (cx, cy) in enumerate(chips):
                got = _half_rows(refs[t], tid, 2 * cx + cy, c)
                missing = _half_rows(refs[t], tid, 2 * cx + cy, 1 - c)
                pairs.append((_remote(got, got, ss, rs, 3 * t + k, (x, y, 1 - c)),
                              _remote(missing, missing, ss, rs, 3 * t + k, (x, y, 1 - c))))
        return pairs
    return plan


def _plan_to_all(grp):
    def plan(refs, ss, rs):
        x, y, c, chips = _place()
        peers = [(x, y, 1 - c)] + [(cx, cy, pc) for cx, cy in chips for pc in (c, 1 - c)]
        assert len(peers) == N_PEERS
        cps = [_remote(refs[0].at[pl.ds((2 * px + py) * grp.block + pc * grp.half, grp.half), :],
                       refs[1].at[pl.ds(k * grp.half, grp.half), :], ss, rs, k, (px, py, pc))
               for k, (px, py, pc) in enumerate(peers)]
        return [(cp, cp) for cp in cps]
    return plan


def _plan_share_halves(grp, k=0):
    def plan(refs, ss, rs):
        x, y, c, _ = _place()
        mine = refs[0].at[pl.ds(c * grp.half, grp.half), :]
        other = refs[0].at[pl.ds((1 - c) * grp.half, grp.half), :]
        return [(_remote(mine, mine, ss, rs, k, (x, y, 1 - c)), _remote(other, other, ss, rs, k, (x, y, 1 - c)))]
    return plan


def _plan_both(first, n_refs, second):
    def plan(refs, ss, rs):
        return first(refs[:n_refs], ss, rs) + second(refs[n_refs:], ss, rs)
    return plan


def _empty(rows, dtype):
    return lax.empty((rows, 1024), dtype)


def _all_reduce_small(v, shard, grp):
    m_per, n = v.shape
    n_dev = 8

    def body(x_ref, _, out_ref, shard_ref, all_ref, send_sems, recv_sems, local_sem):
        x, y, c, chips = _place()
        (share, shared), = _plan_share_halves(grp, k=N_PEERS)([shard_ref], send_sems, recv_sems)
        me, sibling = (x, y, c), (x, y, 1 - c)

        def rows(px, py, pc):
            return all_ref.at[pl.ds((4 * px + 2 * py + pc) * m_per, m_per), :]

        def copy(k, block, to, src=None):
            return pltpu.make_async_remote_copy(
                src_ref=rows(*block) if src is None else src, dst_ref=rows(*block),
                send_sem=send_sems.at[k], recv_sem=recv_sems.at[k], device_id=to, device_id_type=MESH)

        mine = pltpu.make_async_copy(x_ref, rows(*me), local_sem)
        mine.start()
        first = [copy(0, me, sibling, src=x_ref)]
        first += [copy(1 + k, me, (*chip, c), src=x_ref) for k, chip in enumerate(chips)]
        for cp in first:
            cp.start()
        share.start()
        passed = [copy(4 + k, (*chip, c), sibling) for k, chip in enumerate(chips)]
        for k, chip in enumerate(chips):
            copy(1 + k, (*chip, c), me).wait_recv()
            passed[k].start()
        copy(0, sibling, me).wait_recv()
        for k, chip in enumerate(chips):
            copy(4 + k, (*chip, 1 - c), me).wait_recv()
        for cp in first + passed:
            cp.wait_send()
        mine.wait()
        acc = all_ref[pl.ds(0, m_per), :]
        for d in range(1, n_dev):
            acc = acc + all_ref[pl.ds(d * m_per, m_per), :]
        out_ref[...] = acc
        share.wait_send()
        shared.wait_recv()

    return pl.pallas_call(
        body, name="all_reduce_small_share_" + grp.name,
        in_specs=[_VMEM, _ANY], out_specs=[_VMEM, _ANY],
        out_shape=[jax.ShapeDtypeStruct((m_per, n), F32), jax.ShapeDtypeStruct(shard.shape, shard.dtype)],
        scratch_shapes=[pltpu.VMEM((n_dev * m_per, n), F32), pltpu.SemaphoreType.DMA((N_PEERS + 1,)),
                        pltpu.SemaphoreType.DMA((N_PEERS + 1,)), pltpu.SemaphoreType.DMA],
        input_output_aliases={1: 1},
        compiler_params=pltpu.CompilerParams(has_side_effects=True),
    )(v, shard)


def _tables(S, token):
    pos = jnp.arange(S, dtype=jnp.int32).astype(F32) + token[0, 0]

    def cs(half):
        inv_freq = ROPE_THETA ** (-jnp.arange(half, dtype=F32) / half)
        ang = pos[:, None] * inv_freq[None, :]
        return jnp.cos(ang), jnp.sin(ang)

    c64, s64 = cs(64)
    c32, s32 = cs(32)
    cr = jnp.concatenate([c64, c64], axis=1)
    sr = jnp.concatenate([-s64, s64], axis=1)
    ca = jnp.concatenate([c32, c32, c32, c32], axis=1)
    sa = jnp.concatenate([-s32, s32, -s32, s32], axis=1)
    return cr, sr, ca, sa


def _decays():
    H, C = RET_HEADS, RET_CHUNK
    log_gamma = jnp.log1p(-jnp.exp2(-5.0 - jnp.arange(H, dtype=F32)))
    idx = jnp.arange(C, dtype=F32)
    rel = idx[:, None] - idx[None, :]
    dmat = jnp.where(rel[None] >= 0, jnp.exp(log_gamma[:, None, None] * jnp.maximum(rel, 0.0)[None]), 0.0)
    qd = jnp.exp(log_gamma[:, None] * (idx + 1.0))[:, :, None]
    kd = jnp.exp(log_gamma[:, None] * (C - 1.0 - idx))[:, :, None]
    cd = jnp.exp(log_gamma * C)[:, None, None]
    return dmat, qd, kd, cd


def _local_step(place, x, tgt, ln1_g, b_in, ret_norm_g, attn_sinks, ln2_g, lnf_g, w_in, other_weights, tables, decays,
                after):
    cr, sr, ca, sa = tables
    dmat, qd, kd, cd = decays
    forward_weights, landed_weights = other_weights

    h, rq, rk, rv, rg, aq, ak, av, ga, gb = _proj_fwd(x, ln1_g, w_in, b_in, cr, sr, ca, sa, after)
    ry, states = _ret_fwd(rq, rk, rv, dmat, qd, kd, cd)
    token = forward_weights(ry)
    ay = _attn_fwd(aq, ak, av, attn_sinks, token)
    w_ret, w_attn, w_out, w_gate, w_up, w_down = landed_weights(ay)
    u, merged, x1 = _merge_fwd(x, ry, rg, ay, ga, gb, ret_norm_g, w_ret, w_attn, w_out)
    gate, up, h2, act, dx2, loss, dg3 = _ffn_fwd(x1, tgt, ln2_g, lnf_g, w_gate, w_up, w_down)

    dgate, dup, dx1, dg2 = _ffn_bwd(dx2, gate, up, x1, ln2_g, w_gate, w_up, w_down)
    early = _wgrad([act], dx2, 6, None, "wgrad_down")
    early = _wgrad([dgate], h2, 4, early, "wgrad_gate")
    early = _wgrad([dup], h2, 5, early, "wgrad_up")
    dgates, drg, dba, dbb, day, dry, dgr = _merge_bwd(dx1, ga, gb, u, ay, rg, ry, ret_norm_g,
                                                                w_ret, w_attn, w_out)
    early = _wgrad([merged], dx1, 3, early, "wgrad_out")
    early = _wgrad([u], dba, 1, early, "wgrad_ret")
    e32, e16 = _wgrad([ay], dbb, 2, early, "wgrad_attn")

    sems, flight, token = _exchange_start("to_all_start_early", [e16, _empty(N_PEERS * EARLY.half, BF)], N_PEERS,
                                          _plan_to_all(EARLY))
    daq, dak, dav, dsk = _attn_bwd(aq, ak, av, day, attn_sinks, ca, sa, token)
    d_ret = _ret_bwd(rq, rk, rv, dry, states, dmat, qd, kd, cd, cr, sr, token)
    d_proj = [d_ret, drg, daq, dak, dav, dgates]
    l32, l16 = _wgrad(d_proj, h, 0, None, "wgrad_in")
    _, others = _exchange_wait("to_all_wait_early", sems, flight, _plan_to_all(EARLY), after=(l32,))
    share = _plan_share_halves(EARLY, k=N_PEERS)
    sems, flight, token = _exchange_start(
        "to_all_late_share_early_start", [l16, _empty(N_PEERS * LATE.half, BF), _sum_all(place, e32, others, EARLY)],
        N_PEERS + 1, _plan_both(_plan_to_all(LATE), 2, share))
    grad_x, dg1, db = _proj_bwd(d_proj, x, dx1, ln1_g, w_in, token)
    early_shard, = _exchange_wait("share_wait_early", sems, flight[2:], share, after=(grad_x,))

    def finish_late(after):
        _, others = _exchange_wait("to_all_wait_late", sems, flight[:2], _plan_to_all(LATE), after=after)
        return _sum_all(place, l32, others, LATE)

    d_sinks = dsk.reshape(2, 2, PAIRS, 128).sum(-1).transpose(0, 2, 1).reshape(1, ATTN_Q_HEADS)
    small = dict(loss=loss, ln1_g=dg1, b_in=db, ret_norm_g=dgr, attn_sinks=d_sinks,
                 ln2_g=dg2, lnf_g=dg3)
    return grad_x, early_shard, token, finish_late, small


SMALL_ROWS = 16


def _pack_small(ln1, ret, ln2, lnf, b_in, sinks, loss):
    def body(a_ref, b_ref, c_ref, d_ref, bi_ref, sk_ref, ls_ref, o_ref):
        o_ref[...] = jnp.zeros_like(o_ref)
        for row, ref in enumerate((a_ref, b_ref, c_ref, d_ref)):
            o_ref[row:row + 1, :] = ref[...]
        for k in range(6):
            o_ref[4 + k:5 + k, :] = bi_ref[:, k * 1024:(k + 1) * 1024]
        o_ref[10:11, 0:D_IN - 6144] = bi_ref[:, 6144:D_IN]
        o_ref[11:12, 0:ATTN_Q_HEADS] = sk_ref[...]
        o_ref[12:13, 0:LANES] = ls_ref[...]

    return pl.pallas_call(
        body, name="pack_small_grads", in_specs=[_VMEM] * 7, out_specs=_VMEM,
        out_shape=jax.ShapeDtypeStruct((SMALL_ROWS, 1024), F32),
    )(ln1.reshape(1, 1024), ret.reshape(1, 1024), ln2.reshape(1, 1024), lnf.reshape(1, 1024),
      b_in.reshape(1, D_IN), sinks.reshape(1, ATTN_Q_HEADS), loss.reshape(1, LANES))


def _adamw_small(pg, ws, ms, vs):
    n_par = len(ws)
    widths = [w.size for w in ws]
    first_row = [0, 1, 2, 3, 4, 11]

    def body(pg_ref, *refs):
        w_refs, m_refs, v_refs = refs[:n_par], refs[n_par:2 * n_par], refs[2 * n_par:3 * n_par]
        out_refs = refs[3 * n_par:]
        for i, width in enumerate(widths):
            for k, c0 in enumerate(range(0, width, 1024)):
                cols = slice(c0, min(c0 + 1024, width))
                gv = pg_ref[first_row[i] + k:first_row[i] + k + 1, 0:cols.stop - c0]
                go_ref, d_ref, mo_ref, vo_ref = out_refs[4 * i:4 * i + 4]
                go_ref[:, cols] = gv
                d_ref[:, cols], mo_ref[:, cols], vo_ref[:, cols] = _adamw_math(
                    w_refs[i][:, cols], gv, m_refs[i][:, cols], v_refs[i][:, cols])

    flat = lambda a: a.reshape(1, a.size)
    res = pl.pallas_call(
        body, name="adamw_small", in_specs=[_VMEM] * (1 + 3 * n_par), out_specs=[_VMEM] * (4 * n_par),
        out_shape=[jax.ShapeDtypeStruct((1, width), F32) for width in widths for _ in range(4)],
    )(pg, *map(flat, ws), *map(flat, ms), *map(flat, vs))
    return [tuple(r.reshape(w.shape) for r in res[4 * i:4 * i + 4]) for i, w in enumerate(ws)]


def kernel(x, ln1_g, w_in, b_in, ret_norm_g, w_ret_out, attn_sinks, w_attn_out, w_out, ln2_g, w_ffn_gate, w_ffn_up, w_ffn_down, lnf_g, loss_target, m_ln1_g, m_w_in, m_b_in, m_ret_norm_g, m_w_ret_out, m_attn_sinks, m_w_attn_out, m_w_out, m_ln2_g, m_w_ffn_gate, m_w_ffn_up, m_w_ffn_down, m_lnf_g, v_ln1_g, v_w_in, v_b_in, v_ret_norm_g, v_w_ret_out, v_attn_sinks, v_w_attn_out, v_w_out, v_ln2_g, v_w_ffn_gate, v_w_ffn_up, v_w_ffn_down, v_lnf_g):
    xi = lax.axis_index("x")
    yi = lax.axis_index("y")
    ci = lax.axis_index("c")
    j = 2 * xi + yi

    place = jnp.stack([j, ci]).astype(jnp.int32)

    names = ["w_in", "w_ret_out", "w_attn_out", "w_out", "w_ffn_gate", "w_ffn_up", "w_ffn_down"]
    transposed = [True, False, False, False, True, True, False]
    view = lambda a, t: a[0].T if t else a[0]
    big_w = [view(a, t) for a, t in zip([w_in, w_ret_out, w_attn_out, w_out, w_ffn_gate, w_ffn_up, w_ffn_down], transposed)]
    big_m = [view(a, t) for a, t in zip([m_w_in, m_w_ret_out, m_w_attn_out, m_w_out, m_w_ffn_gate, m_w_ffn_up, m_w_ffn_down], transposed)]
    big_v = [view(a, t) for a, t in zip([v_w_in, v_w_ret_out, v_w_attn_out, v_w_out, v_w_ffn_gate, v_w_ffn_up, v_w_ffn_down], transposed)]

    staged_in = _stage_weights(place, big_w[:1], "stage_w_in")[0]
    sems, in_flight, token = _exchange_start("gather_in_start", [staged_in], 3, _plan_gather_halves([0]))
    staged = _stage_weights(place, big_w[1:], "stage_others", after=(token,))
    tables = _tables(x.shape[1], token)
    decays = _decays()
    w_in_full = _exchange_wait("gather_in_wait", sems, in_flight, _plan_gather_halves([0]),
                               after=(*staged, *tables, *decays))[0]
    w_in_full = _exchange("gather_in_forward", [w_in_full], 3, _plan_forward_halves([0]))[0]
    rest = list(range(1, 7))
    crossing = _exchange_start("gather_start", staged, 3 * len(rest), _plan_gather_halves(rest), after=(w_in_full,))
    token = crossing[2]
    forwarding = []

    def forward_weights(after):
        arrived = _exchange_wait("gather_wait", crossing[0], crossing[1], _plan_gather_halves(rest), after=(after,))
        forwarding.extend(_exchange_start("forward_start", arrived, 3 * len(rest), _plan_forward_halves(rest)))
        return forwarding[2]

    def landed_weights(after):
        return _exchange_wait("forward_wait", forwarding[0], forwarding[1], _plan_forward_halves(rest), after=(after,))

    other_weights = (forward_weights, landed_weights)

    grad_x, early_shard, token, finish_late, small = _local_step(
        place, x[0], loss_target[0], ln1_g, b_in, ret_norm_g, attn_sinks, ln2_g, lnf_g.reshape(1, -1),
        w_in_full, other_weights, tables, decays, token)

    upd = {}

    def update(t, shard, after):
        res = _adamw(big_w[t], shard, big_m[t], big_v[t], "adamw_" + names[t], g_row0=GROUP_OF[t].off(t), after=after)
        upd[names[t]] = tuple((r.T if transposed[t] else r)[None] for r in res)
        return res[0]

    done = [update(t, early_shard, token) for t in EARLY.tensors]
    late_half = finish_late((*done, grad_x))
    pg, late_shard = _all_reduce_small(_pack_small(
        small["ln1_g"], small["ret_norm_g"], small["ln2_g"], small["lnf_g"], small["b_in"],
        small["attn_sinks"], small["loss"]), late_half, LATE)
    update(0, late_shard, None)

    small_names = ["ln1_g", "ret_norm_g", "ln2_g", "lnf_g", "b_in", "attn_sinks"]
    upd.update(zip(small_names, _adamw_small(
        pg, [ln1_g, ret_norm_g, ln2_g, lnf_g, b_in, attn_sinks],
        [m_ln1_g, m_ret_norm_g, m_ln2_g, m_lnf_g, m_b_in, m_attn_sinks],
        [v_ln1_g, v_ret_norm_g, v_ln2_g, v_lnf_g, v_b_in, v_attn_sinks])))

    order_w = ["ln1_g", "w_in", "b_in", "ret_norm_g", "w_ret_out", "attn_sinks", "w_attn_out", "w_out",
               "ln2_g", "w_ffn_gate", "w_ffn_up", "w_ffn_down", "lnf_g"]
    outs = [pg[12, 0], grad_x[None]]
    for k in range(4):
        outs += [upd[n][k] for n in order_w]
    return tuple(outs)
```

```python
from typing import NamedTuple

import jax
import jax.numpy as jnp
from jax import lax
from jax.experimental import pallas as pl
from jax.experimental.pallas import tpu as pltpu

D_MODEL = 1024
SEQ_CHUNK = 128
RET_CHUNK = 256
RET_CHUNKS_PER_STEP = 4
RET_HEADS = 4
RET_QK_DIM = 128
ATTN_Q_HEADS = 16
ATTN_HEAD_DIM = 64
D_FF = 2816
D_IN = 6400
ROPE_THETA = 10000.0
EPS = 1e-6
N_CHIPS = 4

ADAM_LR = 0.001
ADAM_B1 = 0.9
ADAM_B2 = 0.999
ADAM_EPS = 1e-08
ADAM_WD = 0.01
ADAM_STEP = 10

OFF_RQ, OFF_RK, OFF_RV, OFF_RG = 0, 512, 1024, 2048
OFF_AQ, OFF_AK, OFF_AV, OFF_GA, OFF_GB = 3072, 4096, 4224, 4352, 5376

V7X_VMEM_LIMIT = 56 * 1024 * 1024
LANES = 128

BF = jnp.bfloat16
F32 = jnp.float32
KEPT = jnp.bfloat16
MESH = pl.DeviceIdType.MESH

_VMEM = pl.BlockSpec(memory_space=pltpu.VMEM)
_SMEM = pl.BlockSpec(memory_space=pltpu.SMEM)
_ANY = pl.BlockSpec(memory_space=pl.ANY)


def _dot(a, b):
    return jnp.dot(a, b, preferred_element_type=F32)


def _dot_nt(a, b):
    return lax.dot_general(a, b, (((1,), (1,)), ((), ())), preferred_element_type=F32)


def _dot_tn(a, b):
    return lax.dot_general(a, b, (((0,), (0,)), ((), ())), preferred_element_type=F32)


def _sigmoid(x):
    return 0.5 * jnp.tanh(0.5 * x) + 0.5


def _rows(tm, n):
    return pl.BlockSpec((tm, n), lambda i: (i, 0))


def _fixed(shape):
    nd = len(shape)
    return pl.BlockSpec(shape, lambda i: (0,) * nd)


def _row_pieces(pieces, tile):
    operands, specs, widths, arity = [], [], [], []
    for piece in pieces:
        parts = piece if isinstance(piece, tuple) else (piece,)
        operands += parts
        specs += [_rows(tile // len(parts), parts[0].shape[1])] * len(parts)
        widths.append(parts[0].shape[1])
        arity.append(len(parts))

    def load(refs):
        vals, k = [], 0
        for n_parts in arity:
            if n_parts == 1:
                vals.append(refs[k][...])
            else:
                vals.append(jnp.concatenate([ref[b * 128:(b + 1) * 128, :] for b in range(tile // 256)
                                             for ref in refs[k:k + 2]], axis=0))
            k += n_parts
        return vals

    return operands, specs, widths, load


def _params(sem, **kw):
    return pltpu.CompilerParams(dimension_semantics=sem, vmem_limit_bytes=V7X_VMEM_LIMIT, **kw)


def _rot_ret(p, cr, sr):
    return p * cr + pltpu.roll(p, 64, 1) * sr


def _rot_ret_bwd(d, cr, sr):
    return d * cr - pltpu.roll(d, 64, 1) * sr


def _attn_partner(p):
    lane = lax.broadcasted_iota(jnp.int32, p.shape, 1)
    first = (lane % 64) < 32
    return jnp.where(first, pltpu.roll(p, 96, 1), pltpu.roll(p, 32, 1))


def _rot_attn(p, ca, sa):
    return p * ca + _attn_partner(p) * sa


def _rot_attn_bwd(d, ca, sa):
    return d * ca - _attn_partner(d) * sa


def _proj_fwd(x, g1, w_in, b_in, cr, sr, ca, sa, after):
    S = x.shape[0]
    tm = 512
    scale = RET_QK_DIM ** -0.5

    def body(x_ref, g_ref, w_ref, b_ref, cr_ref, sr_ref, ca_ref, sa_ref, after_ref,
             h_ref, rq_ref, rk_ref, rv_ref, rg_ref, aq_ref, ak_ref, av_ref, ga_ref, gb_ref):
        xv = x_ref[...]
        r = lax.rsqrt(jnp.mean(xv * xv, axis=-1, keepdims=True) + EPS)
        h = (xv * r * g_ref[...]).astype(BF)
        h_ref[...] = h

        def piece(lo, n):
            return _dot_nt(h, w_ref[lo:lo + n, :]) + b_ref[:, lo:lo + n]

        crv, srv = cr_ref[...], sr_ref[...]
        pq = piece(OFF_RQ, 512)
        pk = piece(OFF_RK, 512)
        for hd in range(RET_HEADS):
            sl = slice(hd * 128, (hd + 1) * 128)
            rq_ref[:, sl] = (_rot_ret(pq[:, sl], crv, srv) * scale).astype(BF)
            rk_ref[:, sl] = _rot_ret(pk[:, sl], crv, srv)
        rv_ref[...] = piece(OFF_RV, 1024).astype(BF)
        rg_ref[...] = piece(OFF_RG, 1024).astype(KEPT)
        cav, sav = ca_ref[...], sa_ref[...]
        pa = piece(OFF_AQ, 1024)
        for blk in range(8):
            sl = slice(blk * 128, (blk + 1) * 128)
            aq_ref[:, sl] = _rot_attn(pa[:, sl], cav, sav).astype(BF)
        pkv = piece(OFF_AK, 256)
        ak_ref[...] = _rot_attn(pkv[:, :128], cav, sav).astype(BF)
        av_ref[...] = pkv[:, 128:].astype(BF)
        ga_ref[...] = piece(OFF_GA, 1024).astype(KEPT)
        gb_ref[...] = piece(OFF_GB, 1024).astype(KEPT)

    outs = [(D_MODEL, BF), (512, BF), (512, F32), (1024, BF), (1024, KEPT),
            (1024, BF), (128, BF), (128, BF), (1024, KEPT), (1024, KEPT)]
    return pl.pallas_call(
        body, name="proj_fwd", grid=(S // tm,),
        in_specs=[_rows(tm, D_MODEL), _fixed((1, D_MODEL)), _VMEM, _VMEM,
                  _rows(tm, 128), _rows(tm, 128), _rows(tm, 128), _rows(tm, 128), _ANY],
        out_specs=[_rows(tm, n) for n, _ in outs],
        out_shape=[jax.ShapeDtypeStruct((S, n), dt) for n, dt in outs],
        compiler_params=_params(("parallel",)),
    )(x, g1, w_in, b_in, cr, sr, ca, sa, after)


def _ret_fwd(rq, rk, rv, dmat, qd, kd, cd):
    S = rq.shape[0]
    N = S // RET_CHUNK
    cps = RET_CHUNKS_PER_STEP
    rows = cps * RET_CHUNK

    def body(q_ref, k_ref, v_ref, dm_ref, qd_ref, kd_ref, cd_ref, y_ref, st_ref, state):
        @pl.when(pl.program_id(0) == 0)
        def _():
            state[...] = jnp.zeros_like(state)

        for hd in range(RET_HEADS):
            qs = slice(hd * 128, (hd + 1) * 128)
            vs = slice(hd * 256, (hd + 1) * 256)
            free = []
            for ch in range(cps):
                rs = slice(ch * RET_CHUNK, (ch + 1) * RET_CHUNK)
                q = q_ref[rs, qs]
                kf = k_ref[rs, qs]
                v = v_ref[rs, vs]
                a = _dot_nt(q, kf.astype(BF)) * dm_ref[hd]
                kdv = (kf * kd_ref[hd]).astype(BF)
                free.append((rs, q, _dot(a.astype(BF), v), _dot_tn(kdv, v)))
            s = state[hd]
            for ch, (rs, q, inner, update) in enumerate(free):
                sb = s.astype(BF)
                st_ref[ch, hd] = sb
                y_ref[rs, vs] = (inner + _dot(q, sb) * qd_ref[hd]).astype(KEPT)
                s = s * cd_ref[hd] + update
            state[hd] = s

    return pl.pallas_call(
        body, name="ret_fwd", grid=(N // cps,),
        in_specs=[_rows(rows, 512), _rows(rows, 512), _rows(rows, 1024), _VMEM, _VMEM, _VMEM, _VMEM],
        out_specs=[_rows(rows, 1024), pl.BlockSpec((cps, RET_HEADS, 128, 256), lambda i: (i, 0, 0, 0))],
        out_shape=[jax.ShapeDtypeStruct((S, 1024), KEPT),
                   jax.ShapeDtypeStruct((N, RET_HEADS, 128, 256), BF)],
        scratch_shapes=[pltpu.VMEM((RET_HEADS, 128, 256), F32)],
        compiler_params=_params(("arbitrary",)),
    )(rq, rk, rv, dmat, qd, kd, cd)


def _kv_halves(prev_ref, cur_ref):
    kk = jnp.concatenate([prev_ref[...], cur_ref[...]], axis=0).astype(F32)
    return kk, pltpu.roll(kk, 64, 1)


ATTN_FWD_BLOCKS_PER_STEP = 4
PAIRS = 4


def _attn_fold(n):
    t = lax.broadcasted_iota(jnp.int32, (128, PAIRS * 128), 0)
    qi = lax.broadcasted_iota(jnp.int32, (128, PAIRS * 128), 1) & 127
    cur = t <= qi
    cur_f = jnp.where(cur, 1.0, 0.0)
    bias = jnp.where(jnp.logical_or(cur, n > 0), 0.0, -1e30)
    return cur, cur_f, 1.0 - cur_f, bias


def _group_operands(g, kk, kk_sw, vv, vv_sw):
    lo = lax.broadcasted_iota(jnp.int32, (256, 128), 1) < 64
    k_a, k_b = (kk, kk_sw) if g == 0 else (kk_sw, kk)
    v_a, v_b = (vv, vv_sw) if g == 0 else (vv_sw, vv)
    return (jnp.concatenate([jnp.where(lo, k_a, 0.0), jnp.where(lo, 0.0, k_b)], axis=0).astype(BF),
            jnp.concatenate([jnp.where(lo, v_a, 0.0), jnp.where(lo, 0.0, v_b)], axis=0).astype(BF))


def _stack_pairs(ref, g, rows=slice(None)):
    return jnp.concatenate([ref[rows, (4 * g + p) * 128:(4 * g + p + 1) * 128] for p in range(PAIRS)], axis=0)


def _sink_row(sk_ref, g, which):
    return jnp.concatenate([jnp.full((1, 128), sk_ref[0, 8 * g + 2 * p + which], F32) for p in range(PAIRS)], axis=1)


def _fold(full, cur):
    return jnp.where(cur, full[128:], full[:128])


def _attn_probs(raw, fold, sink):
    s = _fold(raw, fold[0]) * (ATTN_HEAD_DIM ** -0.5) + fold[3]
    m = jnp.maximum(jnp.max(s, axis=0, keepdims=True), sink)
    e = jnp.exp(s - m)
    es = jnp.exp(sink - m)
    inv = 1.0 / (jnp.sum(e, axis=0, keepdims=True) + es)
    return e * inv, es * inv


def _unfold(p, fold):
    return jnp.concatenate([p * fold[2], p * fold[1]], axis=0).astype(BF)


def _attn_fwd(aq, ak, av, sinks, after):
    S = aq.shape[0]
    N = S // SEQ_CHUNK
    nb = ATTN_FWD_BLOCKS_PER_STEP

    def body(sk_ref, q_ref, kp_ref, kc_ref, vp_ref, vc_ref, after_ref, o_ref):
        kall, kall_sw = _kv_halves(kp_ref, kc_ref)
        vall, vall_sw = _kv_halves(vp_ref, vc_ref)
        blocks = []
        for sb in range(nb):
            kv = slice(sb * 128, sb * 128 + 256)
            rows = slice(sb * 128, (sb + 1) * 128)
            ops = [_group_operands(g, kall[kv], kall_sw[kv], vall[kv], vall_sw[kv]) for g in range(2)]
            raw = [_dot_nt(ops[g][0], _stack_pairs(q_ref, g, rows)) for g in range(2)]
            blocks.append((rows, ops, raw, _attn_fold(nb * pl.program_id(0) + sb)))
        for rows, ops, raw, fold in blocks:
            for g in range(2):
                probs = [_unfold(_attn_probs(raw[g][w * 256:(w + 1) * 256], fold, _sink_row(sk_ref, g, w))[0], fold)
                         for w in range(2)]
                o = _dot_tn(jnp.concatenate(probs, axis=0), ops[g][1])
                for p in range(PAIRS):
                    o_ref[rows, (4 * g + p) * 128:(4 * g + p + 1) * 128] = o[p * 128:(p + 1) * 128].astype(BF)

    prev = lambda i: (jnp.maximum(nb * i - 1, 0), 0)
    return pl.pallas_call(
        body, name="attn_fwd", grid=(N // nb,),
        in_specs=[_SMEM, _rows(nb * 128, 1024), pl.BlockSpec((128, 128), prev), _rows(nb * 128, 128),
                  pl.BlockSpec((128, 128), prev), _rows(nb * 128, 128), _ANY],
        out_specs=_rows(nb * 128, 1024),
        out_shape=jax.ShapeDtypeStruct((S, 1024), BF),
        compiler_params=_params(("parallel",)),
    )(sinks, aq, ak, ak, av, av, after)


def _group_norm(y):
    mu = jnp.mean(y, axis=-1, keepdims=True)
    yc = y - mu
    rstd = lax.rsqrt(jnp.mean(yc * yc, axis=-1, keepdims=True) + EPS)
    return yc * rstd, rstd


def _merge_fwd(x, ry, rg, ay, ga, gb, g_ret, w_ret, w_attn, w_out):
    S = x.shape[0]
    tm = 512

    def body(x_ref, ry_ref, rg_ref, ay_ref, ga_ref, gb_ref, gr_ref, wr_ref, wa_ref, wo_ref,
             u_ref, mg_ref, x1_ref):
        bb = _dot(ay_ref[...], wa_ref[...])
        for hd in range(RET_HEADS):
            vs = slice(hd * 256, (hd + 1) * 256)
            yn, _ = _group_norm(ry_ref[:, vs].astype(F32))
            rgv = rg_ref[:, vs].astype(F32)
            u_ref[:, vs] = (rgv * _sigmoid(rgv) * (yn * gr_ref[:, vs])).astype(BF)
        gated_b = _sigmoid(gb_ref[...].astype(F32)) * bb
        sa = _sigmoid(ga_ref[...].astype(F32))
        ba = _dot(u_ref[...], wr_ref[...])
        merged = (sa * ba + gated_b).astype(BF)
        mg_ref[...] = merged
        x1_ref[...] = x_ref[...] + _dot(merged, wo_ref[...])

    outs = [BF, BF, F32]
    return pl.pallas_call(
        body, name="merge_fwd", grid=(S // tm,),
        in_specs=[_rows(tm, 1024)] * 6 + [_fixed((1, 1024)), _VMEM, _VMEM, _VMEM],
        out_specs=[_rows(tm, 1024)] * 3,
        out_shape=[jax.ShapeDtypeStruct((S, 1024), dt) for dt in outs],
        compiler_params=_params(("parallel",)),
    )(x, ry, rg, ay, ga, gb, g_ret, w_ret, w_attn, w_out)


def _ffn_fwd(x1, tgt, g2, g3, w_gate, w_up, w_down):
    S = x1.shape[0]
    tm = 256

    def body(x1_ref, t_ref, g2_ref, g3_ref, wg_ref, wu_ref, wd_ref,
             gate_ref, up_ref, h2_ref, act_ref, dx2_ref, loss_ref, dg3_ref):
        @pl.when(pl.program_id(0) == 0)
        def _():
            loss_ref[...] = jnp.zeros_like(loss_ref)
            dg3_ref[...] = jnp.zeros_like(dg3_ref)

        x1v = x1_ref[...]
        r2 = lax.rsqrt(jnp.mean(x1v * x1v, axis=-1, keepdims=True) + EPS)
        h2 = (x1v * r2 * g2_ref[...]).astype(BF)
        h2_ref[...] = h2
        gate = _dot_nt(h2, wg_ref[...])
        up = _dot_nt(h2, wu_ref[...])
        gate_ref[...] = gate
        up_ref[...] = up
        act = (gate * _sigmoid(gate) * up).astype(BF)
        act_ref[...] = act
        x2 = x1v + _dot(act, wd_ref[...])
        r3 = lax.rsqrt(jnp.mean(x2 * x2, axis=-1, keepdims=True) + EPS)
        xh = x2 * r3
        g3v = g3_ref[...]
        err = xh * g3v - t_ref[...]
        loss_ref[...] += 0.5 * jnp.sum(jnp.mean(err * err, axis=-1, keepdims=True), axis=0, keepdims=True)
        dy = err * (1.0 / D_MODEL)
        dg3_ref[...] += jnp.sum(dy * xh, axis=0, keepdims=True)
        dxh = dy * g3v
        dx2_ref[...] = r3 * (dxh - xh * jnp.mean(dxh * xh, axis=-1, keepdims=True))

    return pl.pallas_call(
        body, name="ffn_fwd", grid=(S // tm,),
        in_specs=[_rows(tm, 1024), _rows(tm, 1024), _fixed((1, 1024)), _fixed((1, 1024)), _VMEM, _VMEM, _VMEM],
        out_specs=[_rows(tm, D_FF), _rows(tm, D_FF), _rows(tm, 1024), _rows(tm, D_FF), _rows(tm, 1024),
                   _fixed((1, LANES)), _fixed((1, 1024))],
        out_shape=[jax.ShapeDtypeStruct((S, D_FF), F32), jax.ShapeDtypeStruct((S, D_FF), F32),
                   jax.ShapeDtypeStruct((S, 1024), BF), jax.ShapeDtypeStruct((S, D_FF), BF),
                   jax.ShapeDtypeStruct((S, 1024), F32),
                   jax.ShapeDtypeStruct((1, LANES), F32), jax.ShapeDtypeStruct((1, 1024), F32)],
        compiler_params=_params(("arbitrary",)),
    )(x1, tgt, g2, g3, w_gate, w_up, w_down)


def _ffn_bwd(dx2, gate, up, x1, g2, w_gate, w_up, w_down):
    S = x1.shape[0]
    tm = 256

    def body(dx2_ref, gate_ref, up_ref, x1_ref, g2_ref, wg_ref, wu_ref, wd_ref,
             dgate_ref, dup_ref, dx1_ref, dg2_ref):
        @pl.when(pl.program_id(0) == 0)
        def _():
            dg2_ref[...] = jnp.zeros_like(dg2_ref)

        dx2v = dx2_ref[...]
        dact = _dot_nt(dx2v.astype(BF), wd_ref[...])
        gate = gate_ref[...]
        sg = _sigmoid(gate)
        dgate = (dact * up_ref[...] * (sg * (1.0 + gate * (1.0 - sg)))).astype(BF)
        dup = (dact * (gate * sg)).astype(BF)
        dgate_ref[...] = dgate
        dup_ref[...] = dup
        dh2 = _dot(dgate, wg_ref[...]) + _dot(dup, wu_ref[...])
        x1v = x1_ref[...]
        r2 = lax.rsqrt(jnp.mean(x1v * x1v, axis=-1, keepdims=True) + EPS)
        xh = x1v * r2
        dg2_ref[...] += jnp.sum(dh2 * xh, axis=0, keepdims=True)
        dxh = dh2 * g2_ref[...]
        dx1_ref[...] = dx2v + r2 * (dxh - xh * jnp.mean(dxh * xh, axis=-1, keepdims=True))

    return pl.pallas_call(
        body, name="ffn_bwd", grid=(S // tm,),
        in_specs=[_rows(tm, 1024), _rows(tm, D_FF), _rows(tm, D_FF), _rows(tm, 1024), _fixed((1, 1024)),
                  _VMEM, _VMEM, _VMEM],
        out_specs=[_rows(tm, D_FF), _rows(tm, D_FF), _rows(tm, 1024), _fixed((1, 1024))],
        out_shape=[jax.ShapeDtypeStruct((S, D_FF), BF), jax.ShapeDtypeStruct((S, D_FF), BF),
                   jax.ShapeDtypeStruct((S, 1024), F32), jax.ShapeDtypeStruct((1, 1024), F32)],
        compiler_params=_params(("arbitrary",)),
    )(dx2, gate, up, x1, g2, w_gate, w_up, w_down)


def _merge_bwd(dx1, ga, gb, u, ay, rg, ry, g_ret, w_ret, w_attn, w_out):
    S = dx1.shape[0]
    tm = 512

    def body(dx1_ref, ga_ref, gb_ref, u_ref, ay_ref, rg_ref, ry_ref, gr_ref, wr_ref, wa_ref, wo_ref,
             dgates_ref, drg_ref, dba_ref, dbb_ref, day_ref, dry_ref, dgr_ref):
        @pl.when(pl.program_id(0) == 0)
        def _():
            dgr_ref[...] = jnp.zeros_like(dgr_ref)

        dm = _dot_nt(dx1_ref[...].astype(BF), wo_ref[...])
        ba = _dot(u_ref[...], wr_ref[...])
        bb = _dot(ay_ref[...], wa_ref[...])
        sa = _sigmoid(ga_ref[...].astype(F32))
        sb = _sigmoid(gb_ref[...].astype(F32))
        dga = (dm * ba * (sa * (1.0 - sa))).astype(BF)
        dgb = (dm * bb * (sb * (1.0 - sb))).astype(BF)
        dgates_ref[:, :1024] = dga
        dgates_ref[:, 1024:] = dgb
        dba = (dm * sa).astype(BF)
        dbb = (dm * sb).astype(BF)
        dba_ref[...] = dba
        dbb_ref[...] = dbb
        day_ref[...] = _dot_nt(dbb, wa_ref[...]).astype(BF)
        du = _dot_nt(dba, wr_ref[...])
        for hd in range(RET_HEADS):
            vs = slice(hd * 256, (hd + 1) * 256)
            yn, rstd = _group_norm(ry_ref[:, vs].astype(F32))
            rgv = rg_ref[:, vs].astype(F32)
            sr = _sigmoid(rgv)
            silu = rgv * sr
            duv = du[:, vs]
            grv = gr_ref[:, vs]
            drg = (duv * (yn * grv) * (sr * (1.0 + rgv * (1.0 - sr)))).astype(BF)
            drg_ref[:, vs] = drg
            dyg = duv * silu
            dgr_ref[:, vs] += jnp.sum(dyg * yn, axis=0, keepdims=True)
            dyn = dyg * grv
            dry_ref[:, vs] = (rstd * (dyn - jnp.mean(dyn, axis=-1, keepdims=True)
                                      - yn * jnp.mean(dyn * yn, axis=-1, keepdims=True))).astype(KEPT)

    return pl.pallas_call(
        body, name="merge_bwd", grid=(S // tm,),
        in_specs=[_rows(tm, 1024)] * 7 + [_fixed((1, 1024)), _VMEM, _VMEM, _VMEM],
        out_specs=[_rows(tm, 2048), _rows(tm, 1024), _rows(tm, 1024), _rows(tm, 1024), _rows(tm, 1024),
                   _rows(tm, 1024), _fixed((1, 1024))],
        out_shape=[jax.ShapeDtypeStruct((S, 2048), BF), jax.ShapeDtypeStruct((S, 1024), BF),
                   jax.ShapeDtypeStruct((S, 1024), BF), jax.ShapeDtypeStruct((S, 1024), BF),
                   jax.ShapeDtypeStruct((S, 1024), BF), jax.ShapeDtypeStruct((S, 1024), KEPT),
                   jax.ShapeDtypeStruct((1, 1024), F32)],
        compiler_params=_params(("arbitrary",)),
    )(dx1, ga, gb, u, ay, rg, ry, g_ret, w_ret, w_attn, w_out)


def _attn_bwd(aq, ak, av, day, sinks, ca, sa, after):
    S = aq.shape[0]
    N = S // SEQ_CHUNK
    steps = N // 2

    def body(sk_ref, q_ref, kp_ref, kc_ref, vp_ref, vc_ref, do_ref, cq_ref, sq_ref, cp_ref, sp_ref, after_ref,
             dq_ref, dke_ref, dve_ref, dko_ref, dvo_ref, dsk_ref, carry_k, carry_v):
        i = pl.program_id(0)

        @pl.when(i == 0)
        def _():
            carry_k[...] = jnp.zeros_like(carry_k)
            carry_v[...] = jnp.zeros_like(carry_v)
            dsk_ref[...] = jnp.zeros_like(dsk_ref)

        @pl.when(i < steps)
        def _():
            kall, kall_sw = _kv_halves(kp_ref, kc_ref)
            vall, vall_sw = _kv_halves(vp_ref, vc_ref)
            lo = lax.broadcasted_iota(jnp.int32, (256, 128), 1) < 64
            blocks = []
            for sb in range(2):
                kv = slice(sb * 128, sb * 128 + 256)
                rows = slice(sb * 128, (sb + 1) * 128)
                ops = [_group_operands(g, kall[kv], kall_sw[kv], vall[kv], vall_sw[kv]) for g in range(2)]
                stacked = [(_stack_pairs(q_ref, g, rows), _stack_pairs(do_ref, g, rows)) for g in range(2)]
                raw = [_dot_nt(ops[g][0], stacked[g][0]) for g in range(2)]
                raw_dp = [_dot_nt(ops[g][1], stacked[g][1]) for g in range(2)]
                blocks.append((rows, ops, stacked, raw, raw_dp, _attn_fold(2 * i + sb)))
            grads = []
            d_sink = [jnp.zeros((1, PAIRS * 128), F32) for _ in range(4)]
            for rows, ops, stacked, raw, raw_dp, fold in blocks:
                cur = fold[0]
                cq, sq = cq_ref[rows, :], sq_ref[rows, :]
                dk_groups, dv_groups = [], []
                for g in range(2):
                    qs, dos = stacked[g]
                    ds_parts, p_parts = [], []
                    for which in range(2):
                        half = slice(which * 256, (which + 1) * 256)
                        pr, ps = _attn_probs(raw[g][half], fold, _sink_row(sk_ref, g, which))
                        dp = _fold(raw_dp[g][half], cur)
                        dr = jnp.sum(pr * dp, axis=0, keepdims=True)
                        ds_parts.append(_unfold(pr * (dp - dr) * (ATTN_HEAD_DIM ** -0.5), fold))
                        p_parts.append(_unfold(pr, fold))
                        d_sink[2 * g + which] = d_sink[2 * g + which] - ps * dr
                    ds = jnp.concatenate(ds_parts, axis=0)
                    dq = _dot_tn(ds, ops[g][0])
                    dk_both = _dot(ds, qs)
                    dv_both = _dot(jnp.concatenate(p_parts, axis=0), dos)
                    for p in range(PAIRS):
                        sl = slice((4 * g + p) * 128, (4 * g + p + 1) * 128)
                        dq_ref[rows, sl] = _rot_attn_bwd(dq[p * 128:(p + 1) * 128], cq, sq).astype(BF)
                    tk = jnp.where(lo, dk_both[:256], dk_both[256:])
                    tv = jnp.where(lo, dv_both[:256], dv_both[256:])
                    dk_groups.append(tk + pltpu.roll(tk, 64, 1))
                    dv_groups.append(tv + pltpu.roll(tv, 64, 1))
                grads.append((jnp.where(lo, dk_groups[0], dk_groups[1]), jnp.where(lo, dv_groups[0], dv_groups[1])))
            dsk_ref[...] += jnp.concatenate(d_sink, axis=0)
            (dk_a, dv_a), (dk_b, dv_b) = grads
            dko_ref[...] = _rot_attn_bwd(carry_k[...] + dk_a[:128], cp_ref[...], sp_ref[...]).astype(BF)
            dvo_ref[...] = (carry_v[...] + dv_a[:128]).astype(BF)
            dke_ref[...] = _rot_attn_bwd(dk_a[128:] + dk_b[:128], cq_ref[0:128, :], sq_ref[0:128, :]).astype(BF)
            dve_ref[...] = (dv_a[128:] + dv_b[:128]).astype(BF)
            carry_k[...] = dk_b[128:]
            carry_v[...] = dv_b[128:]

        @pl.when(i == steps)
        def _():
            dko_ref[...] = _rot_attn_bwd(carry_k[...], cp_ref[...], sp_ref[...]).astype(BF)
            dvo_ref[...] = carry_v[...].astype(BF)

    two = lambda i: (jnp.minimum(i, steps - 1), 0)
    before = lambda i: (jnp.maximum(2 * i - 1, 0), 0)
    odd = lambda i: (jnp.maximum(i - 1, 0), 0)
    blk = lambda r, w, im: pl.BlockSpec((r, w), im)
    half = jax.ShapeDtypeStruct((S // 2, 128), BF)
    dq, dk_even, dv_even, dk_odd, dv_odd, dsk = pl.pallas_call(
        body, name="attn_bwd", grid=(steps + 1,),
        in_specs=[_SMEM, blk(256, 1024, two), blk(128, 128, before), blk(256, 128, two), blk(128, 128, before),
                  blk(256, 128, two), blk(256, 1024, two), blk(256, 128, two), blk(256, 128, two),
                  blk(128, 128, before), blk(128, 128, before), _ANY],
        out_specs=[blk(256, 1024, two), blk(128, 128, two), blk(128, 128, two), blk(128, 128, odd),
                   blk(128, 128, odd), _fixed((4, PAIRS * 128))],
        out_shape=[jax.ShapeDtypeStruct((S, 1024), BF), half, half, half, half,
                   jax.ShapeDtypeStruct((4, PAIRS * 128), F32)],
        scratch_shapes=[pltpu.VMEM((128, 128), F32), pltpu.VMEM((128, 128), F32)],
        compiler_params=_params(("arbitrary",)),
    )(sinks, aq, ak, ak, av, av, day, ca, sa, ca, sa, after)

    return dq, (dk_even, dk_odd), (dv_even, dv_odd), dsk


def _ret_bwd(rq, rk, rv, dry, states, dmat, qd, kd, cd, cr, sr, after):
    S = rq.shape[0]
    N = S // RET_CHUNK
    scale = RET_QK_DIM ** -0.5
    cps = RET_CHUNKS_PER_STEP
    steps = N // cps

    def body(q_ref, k_ref, v_ref, dy_ref, st_ref, dm_ref, qd_ref, kd_ref, cd_ref, cr_ref, sr_ref, after_ref,
             out_ref, dstate):
        @pl.when(pl.program_id(0) == 0)
        def _():
            dstate[...] = jnp.zeros_like(dstate)

        for hd in range(RET_HEADS):
            qs = slice(hd * 128, (hd + 1) * 128)
            vs = slice(hd * 256, (hd + 1) * 256)
            dmv = dm_ref[hd]
            free = []
            for ch in range(cps):
                rs = slice(ch * RET_CHUNK, (ch + 1) * RET_CHUNK)
                crv, srv = cr_ref[rs, :], sr_ref[rs, :]
                q = q_ref[rs, qs]
                kf = k_ref[rs, qs]
                k = kf.astype(BF)
                v = v_ref[rs, vs]
                do = dy_ref[rs, vs].astype(F32)
                dob = do.astype(BF)
                doq = (do * qd_ref[hd]).astype(BF)
                a = (_dot_nt(q, k) * dmv).astype(BF)
                da = (_dot_nt(dob, v) * dmv).astype(BF)
                dq = _dot(da, k) + _dot_nt(doq, st_ref[ch, hd])
                out_ref[rs, qs] = _rot_ret_bwd(dq * scale, crv, srv).astype(BF)
                free.append((rs, crv, srv, v, (kf * kd_ref[hd]).astype(BF),
                             _dot_tn(a, dob), _dot_tn(da, q), _dot_tn(q, doq)))
            ds = dstate[hd]
            for rs, crv, srv, v, kdv, dv_free, dk_free, ds_update in reversed(free):
                dsb = ds.astype(BF)
                dv = dv_free + _dot(kdv, dsb)
                dk = dk_free + _dot_nt(v, dsb) * kd_ref[hd]
                ds = ds * cd_ref[hd] + ds_update
                out_ref[rs, 512 + hd * 128:512 + (hd + 1) * 128] = _rot_ret_bwd(dk, crv, srv).astype(BF)
                out_ref[rs, 1024 + hd * 256:1024 + (hd + 1) * 256] = dv.astype(BF)
            dstate[hd] = ds

    rev = lambda w: pl.BlockSpec((cps * RET_CHUNK, w), lambda i: (steps - 1 - i, 0))
    return pl.pallas_call(
        body, name="ret_bwd", grid=(steps,),
        in_specs=[rev(512), rev(512), rev(1024), rev(1024),
                  pl.BlockSpec((cps, RET_HEADS, 128, 256), lambda i: (steps - 1 - i, 0, 0, 0)),
                  _VMEM, _VMEM, _VMEM, _VMEM, rev(128), rev(128), _ANY],
        out_specs=rev(2048),
        out_shape=jax.ShapeDtypeStruct((S, 2048), BF),
        scratch_shapes=[pltpu.VMEM((RET_HEADS, 128, 256), F32)],
        compiler_params=_params(("arbitrary",)),
    )(rq, rk, rv, dry, states, dmat, qd, kd, cd, cr, sr, after)


def _proj_bwd(d_pieces, x, dx1, g1, w_in_t, after):
    S = x.shape[0]
    tm = 512
    d_operands, d_specs, widths, load = _row_pieces(d_pieces, tm)
    assert sum(widths) == D_IN
    nd = len(d_operands)

    def body(*refs):
        x_ref, dx1_ref, g_ref, w_ref, after_ref, gx_ref, dg_ref, db_ref = refs[nd:]

        @pl.when(pl.program_id(0) == 0)
        def _():
            dg_ref[...] = jnp.zeros_like(dg_ref)
            db_ref[...] = jnp.zeros_like(db_ref)

        d_all = jnp.concatenate(load(refs[:nd]), axis=1)
        dh = _dot(d_all, w_ref[...])
        db_ref[...] += jnp.sum(d_all.astype(F32), axis=0, keepdims=True)
        xv = x_ref[...]
        r = lax.rsqrt(jnp.mean(xv * xv, axis=-1, keepdims=True) + EPS)
        xh = xv * r
        dg_ref[...] += jnp.sum(dh * xh, axis=0, keepdims=True)
        dxh = dh * g_ref[...]
        gx_ref[...] = dx1_ref[...] + r * (dxh - xh * jnp.mean(dxh * xh, axis=-1, keepdims=True))

    return pl.pallas_call(
        body, name="proj_bwd", grid=(S // tm,),
        in_specs=d_specs + [_rows(tm, 1024), _rows(tm, 1024), _fixed((1, 1024)), _VMEM, _ANY],
        out_specs=[_rows(tm, 1024), _fixed((1, 1024)), _fixed((1, D_IN))],
        out_shape=[jax.ShapeDtypeStruct((S, 1024), F32), jax.ShapeDtypeStruct((1, 1024), F32),
                   jax.ShapeDtypeStruct((1, D_IN), F32)],
        compiler_params=_params(("arbitrary",)),
    )(*d_operands, x, dx1, g1, w_in_t, after)


SHARD_ROWS = (1600, 256, 256, 256, 704, 704, 704)


class Group(NamedTuple):
    name: str
    tensors: tuple
    offs: tuple
    block: int
    tile: int

    @property
    def half(self):
        return self.block // 2

    def off(self, t):
        return self.offs[self.tensors.index(t)]


LATE = Group("late", (0,), (0,), 1600, 400)
EARLY = Group("early", (4, 5, 6, 1, 2, 3), (0, 704, 1408, 2112, 2368, 2624), 2880, 480)
GROUP_OF = {t: g for g in (LATE, EARLY) for t in g.tensors}


def _wgrad(a_list, b, t, packed, name):
    S = b.shape[0]
    M = sum((a[0] if isinstance(a, tuple) else a).shape[1] for a in a_list)
    ts = 1024 if M <= 4096 else 512
    steps = S // ts
    a_operands, a_specs, widths, load = _row_pieces(a_list, ts)
    na = len(a_operands)
    grp = GROUP_OF[t]
    r, off = SHARD_ROWS[t], grp.off(t)
    assert M == N_CHIPS * r
    segments = [(chip * r, r, chip * grp.block + off) for chip in range(N_CHIPS)]
    stage_rows = r
    n_in = na + 1 + (2 if packed is not None else 0)

    def body(*refs):
        a_refs, b_ref = refs[:na], refs[na]
        g32_ref, gb_ref, acc, stage, sems = refs[n_in:]
        i = pl.program_id(0)

        @pl.when(i == 0)
        def _():
            acc[...] = jnp.zeros_like(acc)

        bv = b_ref[...].astype(BF)
        lo = 0
        for a, n in zip(load(a_refs), widths):
            acc[lo:lo + n, :] += _dot_tn(a.astype(BF), bv)
            lo += n

        @pl.when(i == steps - 1)
        def _():
            for a0, n, p0 in segments:
                c32 = pltpu.make_async_copy(acc.at[a0:a0 + n, :], g32_ref.at[pl.ds(p0, n), :], sems.at[0])
                c32.start()
                stage[0:n, :] = acc[a0:a0 + n, :].astype(BF)
                c16 = pltpu.make_async_copy(stage.at[0:n, :], gb_ref.at[pl.ds(p0, n), :], sems.at[1])
                c16.start()
                c32.wait()
                c16.wait()

    shapes = [jax.ShapeDtypeStruct((N_CHIPS * grp.block, 1024), F32),
              jax.ShapeDtypeStruct((N_CHIPS * grp.block, 1024), BF)]
    operands = a_operands + [b]
    in_specs = a_specs + [_rows(ts, 1024)]
    aliases = {}
    if packed is not None:
        operands += list(packed)
        in_specs += [_ANY, _ANY]
        aliases = {na + 1: 0, na + 2: 1}
    return pl.pallas_call(
        body, name=name, grid=(steps,),
        in_specs=in_specs, out_specs=[_ANY, _ANY], out_shape=shapes,
        scratch_shapes=[pltpu.VMEM((M, 1024), F32), pltpu.VMEM((stage_rows, 1024), BF),
                        pltpu.SemaphoreType.DMA((2,))],
        input_output_aliases=aliases,
        compiler_params=_params(("arbitrary",)),
    )(*operands)


def _stage_weights(place, ws, name, after=()):
    nw = len(ws)

    def body(p_ref, *refs):
        for w_ref, o_ref in zip(refs[:nw], refs[nw + len(after):]):
            o_ref[...] = w_ref[...].astype(BF)

    return pl.pallas_call(
        body, name=name,
        grid_spec=pltpu.PrefetchScalarGridSpec(
            num_scalar_prefetch=1, grid=(1,),
            in_specs=[pl.BlockSpec(w.shape, lambda i, p: (0, 0)) for w in ws] + [_ANY] * len(after),
            out_specs=[pl.BlockSpec(w.shape, lambda i, p: (p[0], 0)) for w in ws]),
        out_shape=[jax.ShapeDtypeStruct((N_CHIPS * w.shape[0], 1024), BF) for w in ws],
        compiler_params=_params(("arbitrary",)),
    )(place, *ws, *after)


N_PEERS = 7


def _sum_all(place, g32, others, grp):
    nb = grp.half // grp.tile

    def body(p_ref, g_ref, *refs):
        acc = g_ref[...]
        for r_ref in refs[:N_PEERS]:
            acc = acc + r_ref[...].astype(F32)
        refs[-1][...] = acc

    blk = lambda f: pl.BlockSpec((grp.tile, 1024), f)
    slot = lambda k: blk(lambda i, p: (k * nb + i, 0))
    return pl.pallas_call(
        body, name="sum_all_" + grp.name,
        grid_spec=pltpu.PrefetchScalarGridSpec(
            num_scalar_prefetch=1, grid=(nb,),
            in_specs=[blk(lambda i, p: ((2 * p[0] + p[1]) * nb + i, 0))] + [slot(k) for k in range(N_PEERS)],
            out_specs=blk(lambda i, p: (p[1] * nb + i, 0))),
        out_shape=jax.ShapeDtypeStruct((grp.block, 1024), F32),
        compiler_params=_params(("arbitrary",)),
    )(place, g32, *[others] * N_PEERS)


def _adamw_math(w, gv, m, v):
    mn = ADAM_B1 * m + (1.0 - ADAM_B1) * gv
    vn = ADAM_B2 * v + (1.0 - ADAM_B2) * (gv * gv)
    m_hat = mn / (1.0 - ADAM_B1 ** ADAM_STEP)
    v_hat = vn / (1.0 - ADAM_B2 ** ADAM_STEP)
    return -ADAM_LR * (m_hat / (jnp.sqrt(v_hat) + ADAM_EPS) + ADAM_WD * w), mn, vn


def _adamw(w, g, m, v, name, g_row0=0, after=None):
    r = w.shape[0]
    tr = next((c for c in (352, 320, 64) if r % c == 0 and g_row0 % c == 0), r)
    assert g_row0 % tr == 0
    g0 = g_row0 // tr
    extra = [] if after is None else [after]

    def body(w_ref, g_ref, m_ref, v_ref, *rest):
        go_ref, d_ref, mo_ref, vo_ref = rest[len(extra):]
        gv = g_ref[...]
        go_ref[...] = gv
        d_ref[...], mo_ref[...], vo_ref[...] = _adamw_math(w_ref[...], gv, m_ref[...], v_ref[...])

    return pl.pallas_call(
        body, name=name, grid=(r // tr,),
        in_specs=[_rows(tr, 1024), pl.BlockSpec((tr, 1024), lambda i: (g0 + i, 0)), _rows(tr, 1024), _rows(tr, 1024)]
        + [_ANY] * len(extra),
        out_specs=[_rows(tr, 1024)] * 4,
        out_shape=[jax.ShapeDtypeStruct((r, 1024), F32)] * 4,
        compiler_params=_params(("parallel",)),
    )(w, g, m, v, *extra)


def _place():
    x, y, c = lax.axis_index("x"), lax.axis_index("y"), lax.axis_index("c")
    chips = [(1 - x, y), (x, 1 - y), (1 - x, 1 - y)]
    return x, y, c, chips


_HBM =pl.BlockSpec(memory_space=pltpu.HBM)
_SEM = pl.BlockSpec(memory_space=pltpu.SEMAPHORE)
_DATAFLOW = pltpu.SideEffectType.DATAFLOW_SIDE_EFFECTING


def _exchange(name, arrays, n, plan):
    na = len(arrays)

    def body(*refs):
        pairs = plan(refs[na:2 * na], refs[2 * na], refs[2 * na + 1])
        for send, _ in pairs:
            send.start()
        for send, recv in pairs:
            send.wait_send()
            recv.wait_recv()

    return pl.pallas_call(
        body, name=name, in_specs=[_ANY] * na, out_specs=[_ANY] * na,
        out_shape=[jax.ShapeDtypeStruct(a.shape, a.dtype) for a in arrays],
        scratch_shapes=[pltpu.SemaphoreType.DMA((n,)), pltpu.SemaphoreType.DMA((n,))],
        input_output_aliases={i: i for i in range(na)},
    )(*arrays)


def _exchange_start(name, arrays, n, plan, after=()):
    na = len(arrays)

    def body(*refs):
        for send, _ in plan(refs[:na], refs[na + len(after)], refs[na + len(after) + 1]):
            send.start()
        refs[-1][...] = jnp.zeros_like(refs[-1])

    out = pl.pallas_call(
        body, name=name,
        out_shape=(pltpu.SemaphoreType.DMA((n,)), pltpu.SemaphoreType.DMA((n,)),
                   *[pltpu.HBM(a.shape, a.dtype) for a in arrays], jax.ShapeDtypeStruct((8, LANES), F32)),
        in_specs=[_HBM] * na + [_ANY] * len(after), out_specs=(_SEM, _SEM, *[_HBM] * na, _VMEM),
        input_output_aliases={i: 2 + i for i in range(na)},
        compiler_params=pltpu.CompilerParams(has_side_effects=_DATAFLOW),
    )(*[pltpu.with_memory_space_constraint(a, pltpu.HBM) for a in arrays], *after)
    return (out[0], out[1]), list(out[2:2 + na]), out[-1]


def _exchange_wait(name, sems, in_flight, plan, after=()):
    na = len(in_flight)

    def body(*refs):
        for send, recv in plan(refs[:na], refs[na], refs[na + 1]):
            send.wait_send()
            recv.wait_recv()

    out = pl.pallas_call(
        body, name=name,
        out_shape=tuple(pltpu.HBM(a.shape, a.dtype) for a in in_flight),
        in_specs=[_HBM] * na + [_SEM, _SEM] + [_ANY] * len(after), out_specs=tuple([_HBM] * na),
        input_output_aliases={i: i for i in range(na)},
        compiler_params=pltpu.CompilerParams(has_side_effects=_DATAFLOW),
    )(*in_flight, *sems, *after)
    return list(out)


def _remote(src, dst, send_sems, recv_sems, k, to):
    return pltpu.make_async_remote_copy(src_ref=src, dst_ref=dst, send_sem=send_sems.at[k], recv_sem=recv_sems.at[k],
                                        device_id=to, device_id_type=MESH)


def _half_rows(ref, tid, pj, half):
    h = SHARD_ROWS[tid] // 2
    return ref.at[pl.ds((2 * pj + half) * h, h), :]


def _plan_gather_halves(tids):
    def plan(refs, ss, rs):
        x, y, c, chips = _place()
        pairs = []
        for t, tid in enumerate(tids):
            mine = _half_rows(refs[t], tid, 2 * x + y, c)
            for k, (cx, cy) in enumerate(chips):
                theirs = _half_rows(refs[t], tid, 2 * cx + cy, c)
                pairs.append((_remote(mine, mine, ss, rs, 3 * t + k, (cx, cy, c)),
                              _remote(theirs, theirs, ss, rs, 3 * t + k, (x, y, c))))
        return pairs
    return plan


def _plan_forward_halves(tids):
    def plan(refs, ss, rs):
        x, y, c, chips = _place()
        pairs = []
        for t, tid in enumerate(tids):
            for k, (cx, cy) in enumerate(chips):
                got = _half_rows(refs[t], tid, 2 * cx + cy, c)
                missing = _half_rows(refs[t], tid, 2 * cx + cy, 1 - c)
                pairs.append((_remote(got, got, ss, rs, 3 * t + k, (x, y, 1 - c)),
                              _remote(missing, missing, ss, rs, 3 * t + k, (x, y, 1 - c))))
        return pairs
    return plan


def _plan_to_all(grp):
    def plan(refs, ss, rs):
        x, y, c, chips = _place()
        peers = [(x, y, 1 - c)] + [(cx, cy, pc) for cx, cy in chips for pc in (c, 1 - c)]
        assert len(peers) == N_PEERS
        cps = [_remote(refs[0].at[pl.ds((2 * px + py) * grp.block + pc * grp.half, grp.half), :],
                       refs[1].at[pl.ds(k * grp.half, grp.half), :], ss, rs, k, (px, py, pc))
               for k, (px, py, pc) in enumerate(peers)]
        return [(cp, cp) for cp in cps]
    return plan


def _plan_share_halves(grp, k=0):
    def plan(refs, ss, rs):
        x, y, c, _ = _place()
        mine = refs[0].at[pl.ds(c * grp.half, grp.half), :]
        other = refs[0].at[pl.ds((1 - c) * grp.half, grp.half), :]
        return [(_remote(mine, mine, ss, rs, k, (x, y, 1 - c)), _remote(other, other, ss, rs, k, (x, y, 1 - c)))]
    return plan


def _plan_both(first, n_refs, second):
    def plan(refs, ss, rs):
        return first(refs[:n_refs], ss, rs) + second(refs[n_refs:], ss, rs)
    return plan


def _empty(rows, dtype):
    return lax.empty((rows, 1024), dtype)


def _all_reduce_small(v, shard, grp):
    m_per, n = v.shape
    n_dev = 8

    def body(x_ref, _, out_ref, shard_ref, all_ref, send_sems, recv_sems, local_sem):
        x, y, c, chips = _place()
        (share, shared), = _plan_share_halves(grp, k=N_PEERS)([shard_ref], send_sems, recv_sems)
        me, sibling = (x, y, c), (x, y, 1 - c)

        def rows(px, py, pc):
            return all_ref.at[pl.ds((4 * px + 2 * py + pc) * m_per, m_per), :]

        def copy(k, block, to, src=None):
            return pltpu.make_async_remote_copy(
                src_ref=rows(*block) if src is None else src, dst_ref=rows(*block),
                send_sem=send_sems.at[k], recv_sem=recv_sems.at[k], device_id=to, device_id_type=MESH)

        mine = pltpu.make_async_copy(x_ref, rows(*me), local_sem)
        mine.start()
        first = [copy(0, me, sibling, src=x_ref)]
        first += [copy(1 + k, me, (*chip, c), src=x_ref) for k, chip in enumerate(chips)]
        for cp in first:
            cp.start()
        share.start()
        passed = [copy(4 + k, (*chip, c), sibling) for k, chip in enumerate(chips)]
        for k, chip in enumerate(chips):
            copy(1 + k, (*chip, c), me).wait_recv()
            passed[k].start()
        copy(0, sibling, me).wait_recv()
        for k, chip in enumerate(chips):
            copy(4 + k, (*chip, 1 - c), me).wait_recv()
        for cp in first + passed:
            cp.wait_send()
        mine.wait()
        acc = all_ref[pl.ds(0, m_per), :]
        for d in range(1, n_dev):
            acc = acc + all_ref[pl.ds(d * m_per, m_per), :]
        out_ref[...] = acc
        share.wait_send()
        shared.wait_recv()

    return pl.pallas_call(
        body, name="all_reduce_small_share_" + grp.name,
        in_specs=[_VMEM, _ANY], out_specs=[_VMEM, _ANY],
        out_shape=[jax.ShapeDtypeStruct((m_per, n), F32), jax.ShapeDtypeStruct(shard.shape, shard.dtype)],
        scratch_shapes=[pltpu.VMEM((n_dev * m_per, n), F32), pltpu.SemaphoreType.DMA((N_PEERS + 1,)),
                        pltpu.SemaphoreType.DMA((N_PEERS + 1,)), pltpu.SemaphoreType.DMA],
        input_output_aliases={1: 1},
        compiler_params=pltpu.CompilerParams(has_side_effects=True),
    )(v, shard)


def _tables(S, token):
    pos = jnp.arange(S, dtype=jnp.int32).astype(F32) + token[0, 0]

    def cs(half):
        inv_freq = ROPE_THETA ** (-jnp.arange(half, dtype=F32) / half)
        ang = pos[:, None] * inv_freq[None, :]
        return jnp.cos(ang), jnp.sin(ang)

    c64, s64 = cs(64)
    c32, s32 = cs(32)
    cr = jnp.concatenate([c64, c64], axis=1)
    sr = jnp.concatenate([-s64, s64], axis=1)
    ca = jnp.concatenate([c32, c32, c32, c32], axis=1)
    sa = jnp.concatenate([-s32, s32, -s32, s32], axis=1)
    return cr, sr, ca, sa


def _decays():
    H, C = RET_HEADS, RET_CHUNK
    log_gamma = jnp.log1p(-jnp.exp2(-5.0 - jnp.arange(H, dtype=F32)))
    idx = jnp.arange(C, dtype=F32)
    rel = idx[:, None] - idx[None, :]
    dmat = jnp.where(rel[None] >= 0, jnp.exp(log_gamma[:, None, None] * jnp.maximum(rel, 0.0)[None]), 0.0)
    qd = jnp.exp(log_gamma[:, None] * (idx + 1.0))[:, :, None]
    kd = jnp.exp(log_gamma[:, None] * (C - 1.0 - idx))[:, :, None]
    cd = jnp.exp(log_gamma * C)[:, None, None]
    return dmat, qd, kd, cd


def _local_step(place, x, tgt, ln1_g, b_in, ret_norm_g, attn_sinks, ln2_g, lnf_g, w_in, other_weights, tables, decays,
                after):
    cr, sr, ca, sa = tables
    dmat, qd, kd, cd = decays
    forward_weights, landed_weights = other_weights

    h, rq, rk, rv, rg, aq, ak, av, ga, gb = _proj_fwd(x, ln1_g, w_in, b_in, cr, sr, ca, sa, after)
    ry, states = _ret_fwd(rq, rk, rv, dmat, qd, kd, cd)
    token = forward_weights(ry)
    ay = _attn_fwd(aq, ak, av, attn_sinks, token)
    w_ret, w_attn, w_out, w_gate, w_up, w_down = landed_weights(ay)
    u, merged, x1 = _merge_fwd(x, ry, rg, ay, ga, gb, ret_norm_g, w_ret, w_attn, w_out)
    gate, up, h2, act, dx2, loss, dg3 = _ffn_fwd(x1, tgt, ln2_g, lnf_g, w_gate, w_up, w_down)

    dgate, dup, dx1, dg2 = _ffn_bwd(dx2, gate, up, x1, ln2_g, w_gate, w_up, w_down)
    early = _wgrad([act], dx2, 6, None, "wgrad_down")
    early = _wgrad([dgate], h2, 4, early, "wgrad_gate")
    early = _wgrad([dup], h2, 5, early, "wgrad_up")
    dgates, drg, dba, dbb, day, dry, dgr = _merge_bwd(dx1, ga, gb, u, ay, rg, ry, ret_norm_g,
                                                                w_ret, w_attn, w_out)
    early = _wgrad([merged], dx1, 3, early, "wgrad_out")
    early = _wgrad([u], dba, 1, early, "wgrad_ret")
    e32, e16 = _wgrad([ay], dbb, 2, early, "wgrad_attn")

    sems, flight, token = _exchange_start("to_all_start_early", [e16, _empty(N_PEERS * EARLY.half, BF)], N_PEERS,
                                          _plan_to_all(EARLY))
    daq, dak, dav, dsk = _attn_bwd(aq, ak, av, day, attn_sinks, ca, sa, token)
    d_ret = _ret_bwd(rq, rk, rv, dry, states, dmat, qd, kd, cd, cr, sr, token)
    d_proj = [d_ret, drg, daq, dak, dav, dgates]
    l32, l16 = _wgrad(d_proj, h, 0, None, "wgrad_in")
    _, others = _exchange_wait("to_all_wait_early", sems, flight, _plan_to_all(EARLY), after=(l32,))
    share = _plan_share_halves(EARLY, k=N_PEERS)
    sems, flight, token = _exchange_start(
        "to_all_late_share_early_start", [l16, _empty(N_PEERS * LATE.half, BF), _sum_all(place, e32, others, EARLY)],
        N_PEERS + 1, _plan_both(_plan_to_all(LATE), 2, share))
    grad_x, dg1, db = _proj_bwd(d_proj, x, dx1, ln1_g, w_in, token)
    early_shard, = _exchange_wait("share_wait_early", sems, flight[2:], share, after=(grad_x,))

    def finish_late(after):
        _, others = _exchange_wait("to_all_wait_late", sems, flight[:2], _plan_to_all(LATE), after=after)
        return _sum_all(place, l32, others, LATE)

    d_sinks = dsk.reshape(2, 2, PAIRS, 128).sum(-1).transpose(0, 2, 1).reshape(1, ATTN_Q_HEADS)
    small = dict(loss=loss, ln1_g=dg1, b_in=db, ret_norm_g=dgr, attn_sinks=d_sinks,
                 ln2_g=dg2, lnf_g=dg3)
    return grad_x, early_shard, token, finish_late, small


SMALL_ROWS = 16


def _pack_small(ln1, ret, ln2, lnf, b_in, sinks, loss):
    def body(a_ref, b_ref, c_ref, d_ref, bi_ref, sk_ref, ls_ref, o_ref):
        o_ref[...] = jnp.zeros_like(o_ref)
        for row, ref in enumerate((a_ref, b_ref, c_ref, d_ref)):
            o_ref[row:row + 1, :] = ref[...]
        for k in range(6):
            o_ref[4 + k:5 + k, :] = bi_ref[:, k * 1024:(k + 1) * 1024]
        o_ref[10:11, 0:D_IN - 6144] = bi_ref[:, 6144:D_IN]
        o_ref[11:12, 0:ATTN_Q_HEADS] = sk_ref[...]
        o_ref[12:13, 0:LANES] = ls_ref[...]

    return pl.pallas_call(
        body, name="pack_small_grads", in_specs=[_VMEM] * 7, out_specs=_VMEM,
        out_shape=jax.ShapeDtypeStruct((SMALL_ROWS, 1024), F32),
    )(ln1.reshape(1, 1024), ret.reshape(1, 1024), ln2.reshape(1, 1024), lnf.reshape(1, 1024),
      b_in.reshape(1, D_IN), sinks.reshape(1, ATTN_Q_HEADS), loss.reshape(1, LANES))


def _adamw_whole(ws, g, ms, vs, rows0, name, after):
    n = len(ws)
    r = ws[0].shape[0]
    assert all(w.shape == (r, 1024) for w in ws)

    def body(*refs):
        w_refs, g_ref, m_refs, v_refs = refs[:n], refs[n], refs[n + 1:2 * n + 1], refs[2 * n + 1:3 * n + 1]
        out_refs, g_rows, sems = refs[3 * n + 2:7 * n + 2], refs[-2], refs[-1]
        copies = [pltpu.make_async_copy(g_ref.at[pl.ds(rows0[k], r), :], g_rows.at[k], sems.at[k]) for k in range(n)]
        for cp in copies:
            cp.start()
        for k in range(n):
            copies[k].wait()
            gv = g_rows[k]
            go_ref, d_ref, mo_ref, vo_ref = out_refs[4 * k:4 * k + 4]
            go_ref[...] = gv
            d_ref[...], mo_ref[...], vo_ref[...] = _adamw_math(w_refs[k][...], gv, m_refs[k][...], v_refs[k][...])

    res = pl.pallas_call(
        body, name=name, in_specs=[_VMEM] * n + [_ANY] + [_VMEM] * (2 * n) + [_ANY], out_specs=[_VMEM] * (4 * n),
        out_shape=[jax.ShapeDtypeStruct((r, 1024), F32)] * (4 * n),
        scratch_shapes=[pltpu.VMEM((n, r, 1024), F32), pltpu.SemaphoreType.DMA((n,))],
        compiler_params=pltpu.CompilerParams(vmem_limit_bytes=V7X_VMEM_LIMIT),
    )(*ws, g, *ms, *vs, after)
    return [tuple(res[4 * k:4 * k + 4]) for k in range(n)]


def _adamw_small(pg, ws, ms, vs):
    n_par = len(ws)
    widths = [w.size for w in ws]
    first_row = [0, 1, 2, 3, 4, 11]

    def body(pg_ref, *refs):
        w_refs, m_refs, v_refs = refs[:n_par], refs[n_par:2 * n_par], refs[2 * n_par:3 * n_par]
        out_refs = refs[3 * n_par:]
        for i, width in enumerate(widths):
            for k, c0 in enumerate(range(0, width, 1024)):
                cols = slice(c0, min(c0 + 1024, width))
                gv = pg_ref[first_row[i] + k:first_row[i] + k + 1, 0:cols.stop - c0]
                go_ref, d_ref, mo_ref, vo_ref = out_refs[4 * i:4 * i + 4]
                go_ref[:, cols] = gv
                d_ref[:, cols], mo_ref[:, cols], vo_ref[:, cols] = _adamw_math(
                    w_refs[i][:, cols], gv, m_refs[i][:, cols], v_refs[i][:, cols])

    flat = lambda a: a.reshape(1, a.size)
    res = pl.pallas_call(
        body, name="adamw_small", in_specs=[_VMEM] * (1 + 3 * n_par), out_specs=[_VMEM] * (4 * n_par),
        out_shape=[jax.ShapeDtypeStruct((1, width), F32) for width in widths for _ in range(4)],
    )(pg, *map(flat, ws), *map(flat, ms), *map(flat, vs))
    return [tuple(r.reshape(w.shape) for r in res[4 * i:4 * i + 4]) for i, w in enumerate(ws)]


def kernel(x, ln1_g, w_in, b_in, ret_norm_g, w_ret_out, attn_sinks, w_attn_out, w_out, ln2_g, w_ffn_gate, w_ffn_up, w_ffn_down, lnf_g, loss_target, m_ln1_g, m_w_in, m_b_in, m_ret_norm_g, m_w_ret_out, m_attn_sinks, m_w_attn_out, m_w_out, m_ln2_g, m_w_ffn_gate, m_w_ffn_up, m_w_ffn_down, m_lnf_g, v_ln1_g, v_w_in, v_b_in, v_ret_norm_g, v_w_ret_out, v_attn_sinks, v_w_attn_out, v_w_out, v_ln2_g, v_w_ffn_gate, v_w_ffn_up, v_w_ffn_down, v_lnf_g):
    xi = lax.axis_index("x")
    yi = lax.axis_index("y")
    ci = lax.axis_index("c")
    j = 2 * xi + yi

    place = jnp.stack([j, ci]).astype(jnp.int32)

    names = ["w_in", "w_ret_out", "w_attn_out", "w_out", "w_ffn_gate", "w_ffn_up", "w_ffn_down"]
    transposed = [True, False, False, False, True, True, False]
    view = lambda a, t: a[0].T if t else a[0]
    big_w = [view(a, t) for a, t in zip([w_in, w_ret_out, w_attn_out, w_out, w_ffn_gate, w_ffn_up, w_ffn_down], transposed)]
    big_m = [view(a, t) for a, t in zip([m_w_in, m_w_ret_out, m_w_attn_out, m_w_out, m_w_ffn_gate, m_w_ffn_up, m_w_ffn_down], transposed)]
    big_v = [view(a, t) for a, t in zip([v_w_in, v_w_ret_out, v_w_attn_out, v_w_out, v_w_ffn_gate, v_w_ffn_up, v_w_ffn_down], transposed)]

    staged_in = _stage_weights(place, big_w[:1], "stage_w_in")[0]
    sems, in_flight, token = _exchange_start("gather_in_start", [staged_in], 3, _plan_gather_halves([0]))
    staged = _stage_weights(place, big_w[1:], "stage_others", after=(token,))
    tables = _tables(x.shape[1], token)
    decays = _decays()
    w_in_full = _exchange_wait("gather_in_wait", sems, in_flight, _plan_gather_halves([0]),
                               after=(*staged, *tables, *decays))[0]
    w_in_full = _exchange("gather_in_forward", [w_in_full], 3, _plan_forward_halves([0]))[0]
    rest = list(range(1, 7))
    crossing = _exchange_start("gather_start", staged, 3 * len(rest), _plan_gather_halves(rest), after=(w_in_full,))
    token = crossing[2]
    forwarding = []

    def forward_weights(after):
        arrived = _exchange_wait("gather_wait", crossing[0], crossing[1], _plan_gather_halves(rest), after=(after,))
        forwarding.extend(_exchange_start("forward_start", arrived, 3 * len(rest), _plan_forward_halves(rest)))
        return forwarding[2]

    def landed_weights(after):
        return _exchange_wait("forward_wait", forwarding[0], forwarding[1], _plan_forward_halves(rest), after=(after,))

    other_weights = (forward_weights, landed_weights)

    grad_x, early_shard, token, finish_late, small = _local_step(
        place, x[0], loss_target[0], ln1_g, b_in, ret_norm_g, attn_sinks, ln2_g, lnf_g.reshape(1, -1),
        w_in_full, other_weights, tables, decays, token)

    upd = {}

    def update(t, shard, after):
        res = _adamw(big_w[t], shard, big_m[t], big_v[t], "adamw_" + names[t], g_row0=GROUP_OF[t].off(t), after=after)
        upd[names[t]] = tuple((r.T if transposed[t] else r)[None] for r in res)
        return res[0]

    short = [t for t in EARLY.tensors if SHARD_ROWS[t] == min(SHARD_ROWS)]
    done = [update(t, early_shard, token) for t in EARLY.tensors if t not in short]
    for t, res in zip(short, _adamw_whole([big_w[t] for t in short], early_shard, [big_m[t] for t in short],
                                          [big_v[t] for t in short], [EARLY.off(t) for t in short],
                                          "adamw_short", token)):
        assert not transposed[t]
        upd[names[t]] = tuple(r[None] for r in res)
        done.append(res[0])
    late_half = finish_late((*done, grad_x))
    pg, late_shard = _all_reduce_small(_pack_small(
        small["ln1_g"], small["ret_norm_g"], small["ln2_g"], small["lnf_g"], small["b_in"],
        small["attn_sinks"], small["loss"]), late_half, LATE)
    update(0, late_shard, None)

    small_names = ["ln1_g", "ret_norm_g", "ln2_g", "lnf_g", "b_in", "attn_sinks"]
    upd.update(zip(small_names, _adamw_small(
        pg, [ln1_g, ret_norm_g, ln2_g, lnf_g, b_in, attn_sinks],
        [m_ln1_g, m_ret_norm_g, m_ln2_g, m_lnf_g, m_b_in, m_attn_sinks],
        [v_ln1_g, v_ret_norm_g, v_ln2_g, v_lnf_g, v_b_in, v_attn_sinks])))

    order_w = ["ln1_g", "w_in", "b_in", "ret_norm_g", "w_ret_out", "attn_sinks", "w_attn_out", "w_out",
               "ln2_g", "w_ffn_gate", "w_ffn_up", "w_ffn_down", "lnf_g"]
    outs = [pg[12, 0], grad_x[None]]
    for k in range(4):
        outs += [upd[n][k] for n in order_w]
    return tuple(outs)
```

```python
from typing import NamedTuple

import jax
import jax.numpy as jnp
from jax import lax
from jax.experimental import pallas as pl
from jax.experimental.pallas import tpu as pltpu

D_MODEL = 1024
SEQ_CHUNK = 128
RET_CHUNK = 256
RET_CHUNKS_PER_STEP = 4
RET_HEADS = 4
RET_QK_DIM = 128
ATTN_Q_HEADS = 16
ATTN_HEAD_DIM = 64
D_FF = 2816
D_IN = 6400
ROPE_THETA = 10000.0
EPS = 1e-6
N_CHIPS = 4

ADAM_LR = 0.001
ADAM_B1 = 0.9
ADAM_B2 = 0.999
ADAM_EPS = 1e-08
ADAM_WD = 0.01
ADAM_STEP = 10

OFF_RQ, OFF_RK, OFF_RV, OFF_RG = 0, 512, 1024, 2048
OFF_AQ, OFF_AK, OFF_AV, OFF_GA, OFF_GB = 3072, 4096, 4224, 4352, 5376

V7X_VMEM_LIMIT = 56 * 1024 * 1024
LANES = 128

BF = jnp.bfloat16
F32 = jnp.float32
KEPT = jnp.bfloat16
MESH = pl.DeviceIdType.MESH

_VMEM = pl.BlockSpec(memory_space=pltpu.VMEM)
_SMEM = pl.BlockSpec(memory_space=pltpu.SMEM)
_ANY = pl.BlockSpec(memory_space=pl.ANY)


def _dot(a, b):
    return jnp.dot(a, b, preferred_element_type=F32)


def _dot_nt(a, b):
    return lax.dot_general(a, b, (((1,), (1,)), ((), ())), preferred_element_type=F32)


def _dot_tn(a, b):
    return lax.dot_general(a, b, (((0,), (0,)), ((), ())), preferred_element_type=F32)


def _sigmoid(x):
    return 0.5 * jnp.tanh(0.5 * x) + 0.5


def _rows(tm, n):
    return pl.BlockSpec((tm, n), lambda i: (i, 0))


def _fixed(shape):
    nd = len(shape)
    return pl.BlockSpec(shape, lambda i: (0,) * nd)


def _row_pieces(pieces, tile):
    operands, specs, widths, arity = [], [], [], []
    for piece in pieces:
        parts = piece if isinstance(piece, tuple) else (piece,)
        operands += parts
        specs += [_rows(tile // len(parts), parts[0].shape[1])] * len(parts)
        widths.append(parts[0].shape[1])
        arity.append(len(parts))

    def load(refs):
        vals, k = [], 0
        for n_parts in arity:
            if n_parts == 1:
                vals.append(refs[k][...])
            else:
                vals.append(jnp.concatenate([ref[b * 128:(b + 1) * 128, :] for b in range(tile // 256)
                                             for ref in refs[k:k + 2]], axis=0))
            k += n_parts
        return vals

    return operands, specs, widths, load


def _params(sem, **kw):
    return pltpu.CompilerParams(dimension_semantics=sem, vmem_limit_bytes=V7X_VMEM_LIMIT, **kw)


def _rot_ret(p, cr, sr):
    return p * cr + pltpu.roll(p, 64, 1) * sr


def _rot_ret_bwd(d, cr, sr):
    return d * cr - pltpu.roll(d, 64, 1) * sr


def _attn_partner(p):
    lane = lax.broadcasted_iota(jnp.int32, p.shape, 1)
    first = (lane % 64) < 32
    return jnp.where(first, pltpu.roll(p, 96, 1), pltpu.roll(p, 32, 1))


def _rot_attn(p, ca, sa):
    return p * ca + _attn_partner(p) * sa


def _rot_attn_bwd(d, ca, sa):
    return d * ca - _attn_partner(d) * sa


def _proj_fwd(x, g1, w_in, b_in, cr, sr, ca, sa, after):
    S = x.shape[0]
    tm = 512
    scale = RET_QK_DIM ** -0.5

    def body(x_ref, g_ref, w_ref, b_ref, cr_ref, sr_ref, ca_ref, sa_ref, after_ref,
             h_ref, rq_ref, rk_ref, rv_ref, rg_ref, aq_ref, ak_ref, av_ref, ga_ref, gb_ref):
        xv = x_ref[...]
        r = lax.rsqrt(jnp.mean(xv * xv, axis=-1, keepdims=True) + EPS)
        h = (xv * r * g_ref[...]).astype(BF)
        h_ref[...] = h

        def piece(lo, n):
            return _dot_nt(h, w_ref[lo:lo + n, :]) + b_ref[:, lo:lo + n]

        crv, srv = cr_ref[...], sr_ref[...]
        pq = piece(OFF_RQ, 512)
        pk = piece(OFF_RK, 512)
        for hd in range(RET_HEADS):
            sl = slice(hd * 128, (hd + 1) * 128)
            rq_ref[:, sl] = (_rot_ret(pq[:, sl], crv, srv) * scale).astype(BF)
            rk_ref[:, sl] = _rot_ret(pk[:, sl], crv, srv)
        rv_ref[...] = piece(OFF_RV, 1024).astype(BF)
        rg_ref[...] = piece(OFF_RG, 1024).astype(KEPT)
        cav, sav = ca_ref[...], sa_ref[...]
        pa = piece(OFF_AQ, 1024)
        for blk in range(8):
            sl = slice(blk * 128, (blk + 1) * 128)
            aq_ref[:, sl] = _rot_attn(pa[:, sl], cav, sav).astype(BF)
        pkv = piece(OFF_AK, 256)
        ak_ref[...] = _rot_attn(pkv[:, :128], cav, sav).astype(BF)
        av_ref[...] = pkv[:, 128:].astype(BF)
        ga_ref[...] = piece(OFF_GA, 1024).astype(KEPT)
        gb_ref[...] = piece(OFF_GB, 1024).astype(KEPT)

    outs = [(D_MODEL, BF), (512, BF), (512, F32), (1024, BF), (1024, KEPT),
            (1024, BF), (128, BF), (128, BF), (1024, KEPT), (1024, KEPT)]
    return pl.pallas_call(
        body, name="proj_fwd", grid=(S // tm,),
        in_specs=[_rows(tm, D_MODEL), _fixed((1, D_MODEL)), _VMEM, _VMEM,
                  _rows(tm, 128), _rows(tm, 128), _rows(tm, 128), _rows(tm, 128), _ANY],
        out_specs=[_rows(tm, n) for n, _ in outs],
        out_shape=[jax.ShapeDtypeStruct((S, n), dt) for n, dt in outs],
        compiler_params=_params(("parallel",)),
    )(x, g1, w_in, b_in, cr, sr, ca, sa, after)


def _ret_fwd(rq, rk, rv, dmat, qd, kd, cd):
    S = rq.shape[0]
    N = S // RET_CHUNK
    cps = RET_CHUNKS_PER_STEP
    rows = cps * RET_CHUNK

    def body(q_ref, k_ref, v_ref, dm_ref, qd_ref, kd_ref, cd_ref, y_ref, st_ref, state):
        @pl.when(pl.program_id(0) == 0)
        def _():
            state[...] = jnp.zeros_like(state)

        for hd in range(RET_HEADS):
            qs = slice(hd * 128, (hd + 1) * 128)
            vs = slice(hd * 256, (hd + 1) * 256)
            free = []
            for ch in range(cps):
                rs = slice(ch * RET_CHUNK, (ch + 1) * RET_CHUNK)
                q = q_ref[rs, qs]
                kf = k_ref[rs, qs]
                v = v_ref[rs, vs]
                a = _dot_nt(q, kf.astype(BF)) * dm_ref[hd]
                kdv = (kf * kd_ref[hd]).astype(BF)
                free.append((rs, q, _dot(a.astype(BF), v), _dot_tn(kdv, v)))
            s = state[hd]
            for ch, (rs, q, inner, update) in enumerate(free):
                sb = s.astype(BF)
                st_ref[ch, hd] = sb
                y_ref[rs, vs] = (inner + _dot(q, sb) * qd_ref[hd]).astype(KEPT)
                s = s * cd_ref[hd] + update
            state[hd] = s

    return pl.pallas_call(
        body, name="ret_fwd", grid=(N // cps,),
        in_specs=[_rows(rows, 512), _rows(rows, 512), _rows(rows, 1024), _VMEM, _VMEM, _VMEM, _VMEM],
        out_specs=[_rows(rows, 1024), pl.BlockSpec((cps, RET_HEADS, 128, 256), lambda i: (i, 0, 0, 0))],
        out_shape=[jax.ShapeDtypeStruct((S, 1024), KEPT),
                   jax.ShapeDtypeStruct((N, RET_HEADS, 128, 256), BF)],
        scratch_shapes=[pltpu.VMEM((RET_HEADS, 128, 256), F32)],
        compiler_params=_params(("arbitrary",)),
    )(rq, rk, rv, dmat, qd, kd, cd)


def _kv_halves(prev_ref, cur_ref):
    kk = jnp.concatenate([prev_ref[...], cur_ref[...]], axis=0).astype(F32)
    return kk, pltpu.roll(kk, 64, 1)


ATTN_FWD_BLOCKS_PER_STEP = 4
PAIRS = 4


def _attn_fold(n):
    t = lax.broadcasted_iota(jnp.int32, (128, PAIRS * 128), 0)
    qi = lax.broadcasted_iota(jnp.int32, (128, PAIRS * 128), 1) & 127
    cur = t <= qi
    cur_f = jnp.where(cur, 1.0, 0.0)
    bias = jnp.where(jnp.logical_or(cur, n > 0), 0.0, -1e30)
    return cur, cur_f, 1.0 - cur_f, bias


def _group_operands(g, kk, kk_sw, vv, vv_sw):
    lo = lax.broadcasted_iota(jnp.int32, (256, 128), 1) < 64
    k_a, k_b = (kk, kk_sw) if g == 0 else (kk_sw, kk)
    v_a, v_b = (vv, vv_sw) if g == 0 else (vv_sw, vv)
    return (jnp.concatenate([jnp.where(lo, k_a, 0.0), jnp.where(lo, 0.0, k_b)], axis=0).astype(BF),
            jnp.concatenate([jnp.where(lo, v_a, 0.0), jnp.where(lo, 0.0, v_b)], axis=0).astype(BF))


def _stack_pairs(ref, g, rows=slice(None)):
    return jnp.concatenate([ref[rows, (4 * g + p) * 128:(4 * g + p + 1) * 128] for p in range(PAIRS)], axis=0)


def _sink_row(sk_ref, g, which):
    return jnp.concatenate([jnp.full((1, 128), sk_ref[0, 8 * g + 2 * p + which], F32) for p in range(PAIRS)], axis=1)


def _fold(full, cur):
    return jnp.where(cur, full[128:], full[:128])


def _attn_probs(raw, fold, sink):
    s = _fold(raw, fold[0]) * (ATTN_HEAD_DIM ** -0.5) + fold[3]
    m = jnp.maximum(jnp.max(s, axis=0, keepdims=True), sink)
    e = jnp.exp(s - m)
    es = jnp.exp(sink - m)
    inv = 1.0 / (jnp.sum(e, axis=0, keepdims=True) + es)
    return e * inv, es * inv


def _unfold(p, fold):
    return jnp.concatenate([p * fold[2], p * fold[1]], axis=0).astype(BF)


def _attn_fwd(aq, ak, av, sinks, after):
    S = aq.shape[0]
    N = S // SEQ_CHUNK
    nb = ATTN_FWD_BLOCKS_PER_STEP

    def body(sk_ref, q_ref, kp_ref, kc_ref, vp_ref, vc_ref, after_ref, o_ref):
        kall, kall_sw = _kv_halves(kp_ref, kc_ref)
        vall, vall_sw = _kv_halves(vp_ref, vc_ref)
        blocks = []
        for sb in range(nb):
            kv = slice(sb * 128, sb * 128 + 256)
            rows = slice(sb * 128, (sb + 1) * 128)
            ops = [_group_operands(g, kall[kv], kall_sw[kv], vall[kv], vall_sw[kv]) for g in range(2)]
            raw = [_dot_nt(ops[g][0], _stack_pairs(q_ref, g, rows)) for g in range(2)]
            blocks.append((rows, ops, raw, _attn_fold(nb * pl.program_id(0) + sb)))
        for rows, ops, raw, fold in blocks:
            for g in range(2):
                probs = [_unfold(_attn_probs(raw[g][w * 256:(w + 1) * 256], fold, _sink_row(sk_ref, g, w))[0], fold)
                         for w in range(2)]
                o = _dot_tn(jnp.concatenate(probs, axis=0), ops[g][1])
                for p in range(PAIRS):
                    o_ref[rows, (4 * g + p) * 128:(4 * g + p + 1) * 128] = o[p * 128:(p + 1) * 128].astype(BF)

    prev = lambda i: (jnp.maximum(nb * i - 1, 0), 0)
    return pl.pallas_call(
        body, name="attn_fwd", grid=(N // nb,),
        in_specs=[_SMEM, _rows(nb * 128, 1024), pl.BlockSpec((128, 128), prev), _rows(nb * 128, 128),
                  pl.BlockSpec((128, 128), prev), _rows(nb * 128, 128), _ANY],
        out_specs=_rows(nb * 128, 1024),
        out_shape=jax.ShapeDtypeStruct((S, 1024), BF),
        compiler_params=_params(("parallel",)),
    )(sinks, aq, ak, ak, av, av, after)


def _group_norm(y):
    mu = jnp.mean(y, axis=-1, keepdims=True)
    yc = y - mu
    rstd = lax.rsqrt(jnp.mean(yc * yc, axis=-1, keepdims=True) + EPS)
    return yc * rstd, rstd


def _merge_fwd(x, ry, rg, ay, ga, gb, g_ret, w_ret, w_attn, w_out):
    S = x.shape[0]
    tm = 512

    def body(x_ref, ry_ref, rg_ref, ay_ref, ga_ref, gb_ref, gr_ref, wr_ref, wa_ref, wo_ref,
             u_ref, mg_ref, x1_ref):
        bb = _dot(ay_ref[...], wa_ref[...])
        for hd in range(RET_HEADS):
            vs = slice(hd * 256, (hd + 1) * 256)
            yn, _ = _group_norm(ry_ref[:, vs].astype(F32))
            rgv = rg_ref[:, vs].astype(F32)
            u_ref[:, vs] = (rgv * _sigmoid(rgv) * (yn * gr_ref[:, vs])).astype(BF)
        gated_b = _sigmoid(gb_ref[...].astype(F32)) * bb
        sa = _sigmoid(ga_ref[...].astype(F32))
        ba = _dot(u_ref[...], wr_ref[...])
        merged = (sa * ba + gated_b).astype(BF)
        mg_ref[...] = merged
        x1_ref[...] = x_ref[...] + _dot(merged, wo_ref[...])

    outs = [BF, BF, F32]
    return pl.pallas_call(
        body, name="merge_fwd", grid=(S // tm,),
        in_specs=[_rows(tm, 1024)] * 6 + [_fixed((1, 1024)), _VMEM, _VMEM, _VMEM],
        out_specs=[_rows(tm, 1024)] * 3,
        out_shape=[jax.ShapeDtypeStruct((S, 1024), dt) for dt in outs],
        compiler_params=_params(("parallel",)),
    )(x, ry, rg, ay, ga, gb, g_ret, w_ret, w_attn, w_out)


def _ffn_fwd(x1, tgt, g2, g3, w_gate, w_up, w_down):
    S = x1.shape[0]
    tm = 256

    def body(x1_ref, t_ref, g2_ref, g3_ref, wg_ref, wu_ref, wd_ref,
             gate_ref, up_ref, h2_ref, act_ref, dx2_ref, loss_ref, dg3_ref):
        @pl.when(pl.program_id(0) == 0)
        def _():
            loss_ref[...] = jnp.zeros_like(loss_ref)
            dg3_ref[...] = jnp.zeros_like(dg3_ref)

        x1v = x1_ref[...]
        r2 = lax.rsqrt(jnp.mean(x1v * x1v, axis=-1, keepdims=True) + EPS)
        h2 = (x1v * r2 * g2_ref[...]).astype(BF)
        h2_ref[...] = h2
        gate = _dot_nt(h2, wg_ref[...])
        up = _dot_nt(h2, wu_ref[...])
        gate_ref[...] = gate
        up_ref[...] = up
        act = (gate * _sigmoid(gate) * up).astype(BF)
        act_ref[...] = act
        x2 = x1v + _dot(act, wd_ref[...])
        r3 = lax.rsqrt(jnp.mean(x2 * x2, axis=-1, keepdims=True) + EPS)
        xh = x2 * r3
        g3v = g3_ref[...]
        err = xh * g3v - t_ref[...]
        loss_ref[...] += 0.5 * jnp.sum(jnp.mean(err * err, axis=-1, keepdims=True), axis=0, keepdims=True)
        dy = err * (1.0 / D_MODEL)
        dg3_ref[...] += jnp.sum(dy * xh, axis=0, keepdims=True)
        dxh = dy * g3v
        dx2_ref[...] = r3 * (dxh - xh * jnp.mean(dxh * xh, axis=-1, keepdims=True))

    return pl.pallas_call(
        body, name="ffn_fwd", grid=(S // tm,),
        in_specs=[_rows(tm, 1024), _rows(tm, 1024), _fixed((1, 1024)), _fixed((1, 1024)), _VMEM, _VMEM, _VMEM],
        out_specs=[_rows(tm, D_FF), _rows(tm, D_FF), _rows(tm, 1024), _rows(tm, D_FF), _rows(tm, 1024),
                   _fixed((1, LANES)), _fixed((1, 1024))],
        out_shape=[jax.ShapeDtypeStruct((S, D_FF), F32), jax.ShapeDtypeStruct((S, D_FF), F32),
                   jax.ShapeDtypeStruct((S, 1024), BF), jax.ShapeDtypeStruct((S, D_FF), BF),
                   jax.ShapeDtypeStruct((S, 1024), F32),
                   jax.ShapeDtypeStruct((1, LANES), F32), jax.ShapeDtypeStruct((1, 1024), F32)],
        compiler_params=_params(("arbitrary",)),
    )(x1, tgt, g2, g3, w_gate, w_up, w_down)


def _ffn_bwd(dx2, gate, up, x1, g2, w_gate, w_up, w_down):
    S = x1.shape[0]
    tm = 256

    def body(dx2_ref, gate_ref, up_ref, x1_ref, g2_ref, wg_ref, wu_ref, wd_ref,
             dgate_ref, dup_ref, dx1_ref, dg2_ref):
        @pl.when(pl.program_id(0) == 0)
        def _():
            dg2_ref[...] = jnp.zeros_like(dg2_ref)

        dx2v = dx2_ref[...]
        dact = _dot_nt(dx2v.astype(BF), wd_ref[...])
        gate = gate_ref[...]
        sg = _sigmoid(gate)
        dgate = (dact * up_ref[...] * (sg * (1.0 + gate * (1.0 - sg)))).astype(BF)
        dup = (dact * (gate * sg)).astype(BF)
        dgate_ref[...] = dgate
        dup_ref[...] = dup
        dh2 = _dot(dgate, wg_ref[...]) + _dot(dup, wu_ref[...])
        x1v = x1_ref[...]
        r2 = lax.rsqrt(jnp.mean(x1v * x1v, axis=-1, keepdims=True) + EPS)
        xh = x1v * r2
        dg2_ref[...] += jnp.sum(dh2 * xh, axis=0, keepdims=True)
        dxh = dh2 * g2_ref[...]
        dx1_ref[...] = dx2v + r2 * (dxh - xh * jnp.mean(dxh * xh, axis=-1, keepdims=True))

    return pl.pallas_call(
        body, name="ffn_bwd", grid=(S // tm,),
        in_specs=[_rows(tm, 1024), _rows(tm, D_FF), _rows(tm, D_FF), _rows(tm, 1024), _fixed((1, 1024)),
                  _VMEM, _VMEM, _VMEM],
        out_specs=[_rows(tm, D_FF), _rows(tm, D_FF), _rows(tm, 1024), _fixed((1, 1024))],
        out_shape=[jax.ShapeDtypeStruct((S, D_FF), BF), jax.ShapeDtypeStruct((S, D_FF), BF),
                   jax.ShapeDtypeStruct((S, 1024), F32), jax.ShapeDtypeStruct((1, 1024), F32)],
        compiler_params=_params(("arbitrary",)),
    )(dx2, gate, up, x1, g2, w_gate, w_up, w_down)


def _merge_bwd(dx1, ga, gb, u, ay, rg, ry, g_ret, w_ret, w_attn, w_out):
    S = dx1.shape[0]
    tm = 512

    def body(dx1_ref, ga_ref, gb_ref, u_ref, ay_ref, rg_ref, ry_ref, gr_ref, wr_ref, wa_ref, wo_ref,
             dgates_ref, drg_ref, dba_ref, dbb_ref, day_ref, dry_ref, dgr_ref):
        @pl.when(pl.program_id(0) == 0)
        def _():
            dgr_ref[...] = jnp.zeros_like(dgr_ref)

        dm = _dot_nt(dx1_ref[...].astype(BF), wo_ref[...])
        ba = _dot(u_ref[...], wr_ref[...])
        bb = _dot(ay_ref[...], wa_ref[...])
        sa = _sigmoid(ga_ref[...].astype(F32))
        sb = _sigmoid(gb_ref[...].astype(F32))
        dga = (dm * ba * (sa * (1.0 - sa))).astype(BF)
        dgb = (dm * bb * (sb * (1.0 - sb))).astype(BF)
        dgates_ref[:, :1024] = dga
        dgates_ref[:, 1024:] = dgb
        dba = (dm * sa).astype(BF)
        dbb = (dm * sb).astype(BF)
        dba_ref[...] = dba
        dbb_ref[...] = dbb
        day_ref[...] = _dot_nt(dbb, wa_ref[...]).astype(BF)
        du = _dot_nt(dba, wr_ref[...])
        for hd in range(RET_HEADS):
            vs = slice(hd * 256, (hd + 1) * 256)
            yn, rstd = _group_norm(ry_ref[:, vs].astype(F32))
            rgv = rg_ref[:, vs].astype(F32)
            sr = _sigmoid(rgv)
            silu = rgv * sr
            duv = du[:, vs]
            grv = gr_ref[:, vs]
            drg = (duv * (yn * grv) * (sr * (1.0 + rgv * (1.0 - sr)))).astype(BF)
            drg_ref[:, vs] = drg
            dyg = duv * silu
            dgr_ref[:, vs] += jnp.sum(dyg * yn, axis=0, keepdims=True)
            dyn = dyg * grv
            dry_ref[:, vs] = (rstd * (dyn - jnp.mean(dyn, axis=-1, keepdims=True)
                                      - yn * jnp.mean(dyn * yn, axis=-1, keepdims=True))).astype(KEPT)

    return pl.pallas_call(
        body, name="merge_bwd", grid=(S // tm,),
        in_specs=[_rows(tm, 1024)] * 7 + [_fixed((1, 1024)), _VMEM, _VMEM, _VMEM],
        out_specs=[_rows(tm, 2048), _rows(tm, 1024), _rows(tm, 1024), _rows(tm, 1024), _rows(tm, 1024),
                   _rows(tm, 1024), _fixed((1, 1024))],
        out_shape=[jax.ShapeDtypeStruct((S, 2048), BF), jax.ShapeDtypeStruct((S, 1024), BF),
                   jax.ShapeDtypeStruct((S, 1024), BF), jax.ShapeDtypeStruct((S, 1024), BF),
                   jax.ShapeDtypeStruct((S, 1024), BF), jax.ShapeDtypeStruct((S, 1024), KEPT),
                   jax.ShapeDtypeStruct((1, 1024), F32)],
        compiler_params=_params(("arbitrary",)),
    )(dx1, ga, gb, u, ay, rg, ry, g_ret, w_ret, w_attn, w_out)


def _attn_bwd(aq, ak, av, day, sinks, ca, sa, after):
    S = aq.shape[0]
    N = S // SEQ_CHUNK
    steps = N // 2

    def body(sk_ref, q_ref, kp_ref, kc_ref, vp_ref, vc_ref, do_ref, cq_ref, sq_ref, cp_ref, sp_ref, after_ref,
             dq_ref, dke_ref, dve_ref, dko_ref, dvo_ref, dsk_ref, carry_k, carry_v):
        i = pl.program_id(0)

        @pl.when(i == 0)
        def _():
            carry_k[...] = jnp.zeros_like(carry_k)
            carry_v[...] = jnp.zeros_like(carry_v)
            dsk_ref[...] = jnp.zeros_like(dsk_ref)

        @pl.when(i < steps)
        def _():
            kall, kall_sw = _kv_halves(kp_ref, kc_ref)
            vall, vall_sw = _kv_halves(vp_ref, vc_ref)
            lo = lax.broadcasted_iota(jnp.int32, (256, 128), 1) < 64
            blocks = []
            for sb in range(2):
                kv = slice(sb * 128, sb * 128 + 256)
                rows = slice(sb * 128, (sb + 1) * 128)
                ops = [_group_operands(g, kall[kv], kall_sw[kv], vall[kv], vall_sw[kv]) for g in range(2)]
                stacked = [(_stack_pairs(q_ref, g, rows), _stack_pairs(do_ref, g, rows)) for g in range(2)]
                raw = [_dot_nt(ops[g][0], stacked[g][0]) for g in range(2)]
                raw_dp = [_dot_nt(ops[g][1], stacked[g][1]) for g in range(2)]
                blocks.append((rows, ops, stacked, raw, raw_dp, _attn_fold(2 * i + sb)))
            grads = []
            d_sink = [jnp.zeros((1, PAIRS * 128), F32) for _ in range(4)]
            for rows, ops, stacked, raw, raw_dp, fold in blocks:
                cur = fold[0]
                cq, sq = cq_ref[rows, :], sq_ref[rows, :]
                dk_groups, dv_groups = [], []
                for g in range(2):
                    qs, dos = stacked[g]
                    ds_parts, p_parts = [], []
                    for which in range(2):
                        half = slice(which * 256, (which + 1) * 256)
                        pr, ps = _attn_probs(raw[g][half], fold, _sink_row(sk_ref, g, which))
                        dp = _fold(raw_dp[g][half], cur)
                        dr = jnp.sum(pr * dp, axis=0, keepdims=True)
                        ds_parts.append(_unfold(pr * (dp - dr) * (ATTN_HEAD_DIM ** -0.5), fold))
                        p_parts.append(_unfold(pr, fold))
                        d_sink[2 * g + which] = d_sink[2 * g + which] - ps * dr
                    ds = jnp.concatenate(ds_parts, axis=0)
                    dq = _dot_tn(ds, ops[g][0])
                    dk_both = _dot(ds, qs)
                    dv_both = _dot(jnp.concatenate(p_parts, axis=0), dos)
                    for p in range(PAIRS):
                        sl = slice((4 * g + p) * 128, (4 * g + p + 1) * 128)
                        dq_ref[rows, sl] = _rot_attn_bwd(dq[p * 128:(p + 1) * 128], cq, sq).astype(BF)
                    tk = jnp.where(lo, dk_both[:256], dk_both[256:])
                    tv = jnp.where(lo, dv_both[:256], dv_both[256:])
                    dk_groups.append(tk + pltpu.roll(tk, 64, 1))
                    dv_groups.append(tv + pltpu.roll(tv, 64, 1))
                grads.append((jnp.where(lo, dk_groups[0], dk_groups[1]), jnp.where(lo, dv_groups[0], dv_groups[1])))
            dsk_ref[...] += jnp.concatenate(d_sink, axis=0)
            (dk_a, dv_a), (dk_b, dv_b) = grads
            dko_ref[...] = _rot_attn_bwd(carry_k[...] + dk_a[:128], cp_ref[...], sp_ref[...]).astype(BF)
            dvo_ref[...] = (carry_v[...] + dv_a[:128]).astype(BF)
            dke_ref[...] = _rot_attn_bwd(dk_a[128:] + dk_b[:128], cq_ref[0:128, :], sq_ref[0:128, :]).astype(BF)
            dve_ref[...] = (dv_a[128:] + dv_b[:128]).astype(BF)
            carry_k[...] = dk_b[128:]
            carry_v[...] = dv_b[128:]

        @pl.when(i == steps)
        def _():
            dko_ref[...] = _rot_attn_bwd(carry_k[...], cp_ref[...], sp_ref[...]).astype(BF)
            dvo_ref[...] = carry_v[...].astype(BF)

    two = lambda i: (jnp.minimum(i, steps - 1), 0)
    before = lambda i: (jnp.maximum(2 * i - 1, 0), 0)
    odd = lambda i: (jnp.maximum(i - 1, 0), 0)
    blk = lambda r, w, im: pl.BlockSpec((r, w), im)
    half = jax.ShapeDtypeStruct((S // 2, 128), BF)
    dq, dk_even, dv_even, dk_odd, dv_odd, dsk = pl.pallas_call(
        body, name="attn_bwd", grid=(steps + 1,),
        in_specs=[_SMEM, blk(256, 1024, two), blk(128, 128, before), blk(256, 128, two), blk(128, 128, before),
                  blk(256, 128, two), blk(256, 1024, two), blk(256, 128, two), blk(256, 128, two),
                  blk(128, 128, before), blk(128, 128, before), _ANY],
        out_specs=[blk(256, 1024, two), blk(128, 128, two), blk(128, 128, two), blk(128, 128, odd),
                   blk(128, 128, odd), _fixed((4, PAIRS * 128))],
        out_shape=[jax.ShapeDtypeStruct((S, 1024), BF), half, half, half, half,
                   jax.ShapeDtypeStruct((4, PAIRS * 128), F32)],
        scratch_shapes=[pltpu.VMEM((128, 128), F32), pltpu.VMEM((128, 128), F32)],
        compiler_params=_params(("arbitrary",)),
    )(sinks, aq, ak, ak, av, av, day, ca, sa, ca, sa, after)

    return dq, (dk_even, dk_odd), (dv_even, dv_odd), dsk


def _ret_bwd(rq, rk, rv, dry, states, dmat, qd, kd, cd, cr, sr, after):
    S = rq.shape[0]
    N = S // RET_CHUNK
    scale = RET_QK_DIM ** -0.5
    cps = RET_CHUNKS_PER_STEP
    steps = N // cps

    def body(q_ref, k_ref, v_ref, dy_ref, st_ref, dm_ref, qd_ref, kd_ref, cd_ref, cr_ref, sr_ref, after_ref,
             out_ref, dstate):
        @pl.when(pl.program_id(0) == 0)
        def _():
            dstate[...] = jnp.zeros_like(dstate)

        for hd in range(RET_HEADS):
            qs = slice(hd * 128, (hd + 1) * 128)
            vs = slice(hd * 256, (hd + 1) * 256)
            dmv = dm_ref[hd]
            free = []
            for ch in range(cps):
                rs = slice(ch * RET_CHUNK, (ch + 1) * RET_CHUNK)
                crv, srv = cr_ref[rs, :], sr_ref[rs, :]
                q = q_ref[rs, qs]
                kf = k_ref[rs, qs]
                k = kf.astype(BF)
                v = v_ref[rs, vs]
                do = dy_ref[rs, vs].astype(F32)
                dob = do.astype(BF)
                doq = (do * qd_ref[hd]).astype(BF)
                a = (_dot_nt(q, k) * dmv).astype(BF)
                da = (_dot_nt(dob, v) * dmv).astype(BF)
                dq = _dot(da, k) + _dot_nt(doq, st_ref[ch, hd])
                out_ref[rs, qs] = _rot_ret_bwd(dq * scale, crv, srv).astype(BF)
                free.append((rs, crv, srv, v, (kf * kd_ref[hd]).astype(BF),
                             _dot_tn(a, dob), _dot_tn(da, q), _dot_tn(q, doq)))
            ds = dstate[hd]
            for rs, crv, srv, v, kdv, dv_free, dk_free, ds_update in reversed(free):
                dsb = ds.astype(BF)
                dv = dv_free + _dot(kdv, dsb)
                dk = dk_free + _dot_nt(v, dsb) * kd_ref[hd]
                ds = ds * cd_ref[hd] + ds_update
                out_ref[rs, 512 + hd * 128:512 + (hd + 1) * 128] = _rot_ret_bwd(dk, crv, srv).astype(BF)
                out_ref[rs, 1024 + hd * 256:1024 + (hd + 1) * 256] = dv.astype(BF)
            dstate[hd] = ds

    rev = lambda w: pl.BlockSpec((cps * RET_CHUNK, w), lambda i: (steps - 1 - i, 0))
    return pl.pallas_call(
        body, name="ret_bwd", grid=(steps,),
        in_specs=[rev(512), rev(512), rev(1024), rev(1024),
                  pl.BlockSpec((cps, RET_HEADS, 128, 256), lambda i: (steps - 1 - i, 0, 0, 0)),
                  _VMEM, _VMEM, _VMEM, _VMEM, rev(128), rev(128), _ANY],
        out_specs=rev(2048),
        out_shape=jax.ShapeDtypeStruct((S, 2048), BF),
        scratch_shapes=[pltpu.VMEM((RET_HEADS, 128, 256), F32)],
        compiler_params=_params(("arbitrary",)),
    )(rq, rk, rv, dry, states, dmat, qd, kd, cd, cr, sr, after)


def _proj_bwd(d_pieces, x, dx1, g1, w_in_t, after):
    S = x.shape[0]
    tm = 512
    d_operands, d_specs, widths, load = _row_pieces(d_pieces, tm)
    assert sum(widths) == D_IN
    nd = len(d_operands)

    def body(*refs):
        x_ref, dx1_ref, g_ref, w_ref, after_ref, gx_ref, dg_ref, db_ref = refs[nd:]

        @pl.when(pl.program_id(0) == 0)
        def _():
            dg_ref[...] = jnp.zeros_like(dg_ref)
            db_ref[...] = jnp.zeros_like(db_ref)

        d_all = jnp.concatenate(load(refs[:nd]), axis=1)
        dh = _dot(d_all, w_ref[...])
        db_ref[...] += jnp.sum(d_all.astype(F32), axis=0, keepdims=True)
        xv = x_ref[...]
        r = lax.rsqrt(jnp.mean(xv * xv, axis=-1, keepdims=True) + EPS)
        xh = xv * r
        dg_ref[...] += jnp.sum(dh * xh, axis=0, keepdims=True)
        dxh = dh * g_ref[...]
        gx_ref[...] = dx1_ref[...] + r * (dxh - xh * jnp.mean(dxh * xh, axis=-1, keepdims=True))

    return pl.pallas_call(
        body, name="proj_bwd", grid=(S // tm,),
        in_specs=d_specs + [_rows(tm, 1024), _rows(tm, 1024), _fixed((1, 1024)), _VMEM, _ANY],
        out_specs=[_rows(tm, 1024), _fixed((1, 1024)), _fixed((1, D_IN))],
        out_shape=[jax.ShapeDtypeStruct((S, 1024), F32), jax.ShapeDtypeStruct((1, 1024), F32),
                   jax.ShapeDtypeStruct((1, D_IN), F32)],
        compiler_params=_params(("arbitrary",)),
    )(*d_operands, x, dx1, g1, w_in_t, after)


SHARD_ROWS = (1600, 256, 256, 256, 704, 704, 704)


class Group(NamedTuple):
    name: str
    tensors: tuple
    offs: tuple
    block: int
    tile: int

    @property
    def half(self):
        return self.block // 2

    def off(self, t):
        return self.offs[self.tensors.index(t)]


LATE = Group("late", (0,), (0,), 1600, 400)
EARLY = Group("early", (4, 5, 6, 1, 2, 3), (0, 704, 1408, 2112, 2368, 2624), 2880, 720)
GROUP_OF = {t: g for g in (LATE, EARLY) for t in g.tensors}


def _wgrad(a_list, b, t, packed, name):
    S = b.shape[0]
    M = sum((a[0] if isinstance(a, tuple) else a).shape[1] for a in a_list)
    ts = 1024 if M <= 4096 else 512
    steps = S // ts
    a_operands, a_specs, widths, load = _row_pieces(a_list, ts)
    na = len(a_operands)
    grp = GROUP_OF[t]
    r, off = SHARD_ROWS[t], grp.off(t)
    assert M == N_CHIPS * r
    segments = [(chip * r, r, chip * grp.block + off) for chip in range(N_CHIPS)]
    stage_rows = r
    n_in = na + 1 + (2 if packed is not None else 0)

    def body(*refs):
        a_refs, b_ref = refs[:na], refs[na]
        g32_ref, gb_ref, acc, stage, sems = refs[n_in:]
        i = pl.program_id(0)

        @pl.when(i == 0)
        def _():
            acc[...] = jnp.zeros_like(acc)

        bv = b_ref[...].astype(BF)
        lo = 0
        for a, n in zip(load(a_refs), widths):
            acc[lo:lo + n, :] += _dot_tn(a.astype(BF), bv)
            lo += n

        @pl.when(i == steps - 1)
        def _():
            for a0, n, p0 in segments:
                c32 = pltpu.make_async_copy(acc.at[a0:a0 + n, :], g32_ref.at[pl.ds(p0, n), :], sems.at[0])
                c32.start()
                stage[0:n, :] = acc[a0:a0 + n, :].astype(BF)
                c16 = pltpu.make_async_copy(stage.at[0:n, :], gb_ref.at[pl.ds(p0, n), :], sems.at[1])
                c16.start()
                c32.wait()
                c16.wait()

    shapes = [jax.ShapeDtypeStruct((N_CHIPS * grp.block, 1024), F32),
              jax.ShapeDtypeStruct((N_CHIPS * grp.block, 1024), BF)]
    operands = a_operands + [b]
    in_specs = a_specs + [_rows(ts, 1024)]
    aliases = {}
    if packed is not None:
        operands += list(packed)
        in_specs += [_ANY, _ANY]
        aliases = {na + 1: 0, na + 2: 1}
    return pl.pallas_call(
        body, name=name, grid=(steps,),
        in_specs=in_specs, out_specs=[_ANY, _ANY], out_shape=shapes,
        scratch_shapes=[pltpu.VMEM((M, 1024), F32), pltpu.VMEM((stage_rows, 1024), BF),
                        pltpu.SemaphoreType.DMA((2,))],
        input_output_aliases=aliases,
        compiler_params=_params(("arbitrary",)),
    )(*operands)


def _stage_weights(place, ws, name, after=()):
    nw = len(ws)

    def body(p_ref, *refs):
        for w_ref, o_ref in zip(refs[:nw], refs[nw + len(after):]):
            o_ref[...] = w_ref[...].astype(BF)

    return pl.pallas_call(
        body, name=name,
        grid_spec=pltpu.PrefetchScalarGridSpec(
            num_scalar_prefetch=1, grid=(1,),
            in_specs=[pl.BlockSpec(w.shape, lambda i, p: (0, 0)) for w in ws] + [_ANY] * len(after),
            out_specs=[pl.BlockSpec(w.shape, lambda i, p: (p[0], 0)) for w in ws]),
        out_shape=[jax.ShapeDtypeStruct((N_CHIPS * w.shape[0], 1024), BF) for w in ws],
        compiler_params=_params(("arbitrary",)),
    )(place, *ws, *after)


N_PEERS = 7


def _sum_all(place, g32, others, grp):
    nb = grp.half // grp.tile

    def body(p_ref, g_ref, *refs):
        acc = g_ref[...]
        for r_ref in refs[:N_PEERS]:
            acc = acc + r_ref[...].astype(F32)
        refs[-1][...] = acc

    blk = lambda f: pl.BlockSpec((grp.tile, 1024), f)
    slot = lambda k: blk(lambda i, p: (k * nb + i, 0))
    return pl.pallas_call(
        body, name="sum_all_" + grp.name,
        grid_spec=pltpu.PrefetchScalarGridSpec(
            num_scalar_prefetch=1, grid=(nb,),
            in_specs=[blk(lambda i, p: ((2 * p[0] + p[1]) * nb + i, 0))] + [slot(k) for k in range(N_PEERS)],
            out_specs=blk(lambda i, p: (p[1] * nb + i, 0))),
        out_shape=jax.ShapeDtypeStruct((grp.block, 1024), F32),
        compiler_params=_params(("arbitrary",)),
    )(place, g32, *[others] * N_PEERS)


def _adamw_math(w, gv, m, v):
    mn = ADAM_B1 * m + (1.0 - ADAM_B1) * gv
    vn = ADAM_B2 * v + (1.0 - ADAM_B2) * (gv * gv)
    m_hat = mn / (1.0 - ADAM_B1 ** ADAM_STEP)
    v_hat = vn / (1.0 - ADAM_B2 ** ADAM_STEP)
    return -ADAM_LR * (m_hat / (jnp.sqrt(v_hat) + ADAM_EPS) + ADAM_WD * w), mn, vn


def _adamw(w, g, m, v, name, g_row0=0, after=None):
    r = w.shape[0]
    tr = next((c for c in (352, 320, 64) if r % c == 0 and g_row0 % c == 0), r)
    assert g_row0 % tr == 0
    g0 = g_row0 // tr
    extra = [] if after is None else [after]

    def body(w_ref, g_ref, m_ref, v_ref, *rest):
        go_ref, d_ref, mo_ref, vo_ref = rest[len(extra):]
        gv = g_ref[...]
        go_ref[...] = gv
        d_ref[...], mo_ref[...], vo_ref[...] = _adamw_math(w_ref[...], gv, m_ref[...], v_ref[...])

    return pl.pallas_call(
        body, name=name, grid=(r // tr,),
        in_specs=[_rows(tr, 1024), pl.BlockSpec((tr, 1024), lambda i: (g0 + i, 0)), _rows(tr, 1024), _rows(tr, 1024)]
        + [_ANY] * len(extra),
        out_specs=[_rows(tr, 1024)] * 4,
        out_shape=[jax.ShapeDtypeStruct((r, 1024), F32)] * 4,
        compiler_params=_params(("parallel",)),
    )(w, g, m, v, *extra)


def _place():
    x, y, c = lax.axis_index("x"), lax.axis_index("y"), lax.axis_index("c")
    chips = [(1 - x, y), (x, 1 - y), (1 - x, 1 - y)]
    return x, y, c, chips


_HBM =pl.BlockSpec(memory_space=pltpu.HBM)
_SEM = pl.BlockSpec(memory_space=pltpu.SEMAPHORE)
_DATAFLOW = pltpu.SideEffectType.DATAFLOW_SIDE_EFFECTING


def _exchange(name, arrays, n, plan):
    na = len(arrays)

    def body(*refs):
        pairs = plan(refs[na:2 * na], refs[2 * na], refs[2 * na + 1])
        for send, _ in pairs:
            send.start()
        for send, recv in pairs:
            send.wait_send()
            recv.wait_recv()

    return pl.pallas_call(
        body, name=name, in_specs=[_ANY] * na, out_specs=[_ANY] * na,
        out_shape=[jax.ShapeDtypeStruct(a.shape, a.dtype) for a in arrays],
        scratch_shapes=[pltpu.SemaphoreType.DMA((n,)), pltpu.SemaphoreType.DMA((n,))],
        input_output_aliases={i: i for i in range(na)},
    )(*arrays)


def _exchange_start(name, arrays, n, plan, after=()):
    na = len(arrays)

    def body(*refs):
        for send, _ in plan(refs[:na], refs[na + len(after)], refs[na + len(after) + 1]):
            send.start()
        refs[-1][...] = jnp.zeros_like(refs[-1])

    out = pl.pallas_call(
        body, name=name,
        out_shape=(pltpu.SemaphoreType.DMA((n,)), pltpu.SemaphoreType.DMA((n,)),
                   *[pltpu.HBM(a.shape, a.dtype) for a in arrays], jax.ShapeDtypeStruct((8, LANES), F32)),
        in_specs=[_HBM] * na + [_ANY] * len(after), out_specs=(_SEM, _SEM, *[_HBM] * na, _VMEM),
        input_output_aliases={i: 2 + i for i in range(na)},
        compiler_params=pltpu.CompilerParams(has_side_effects=_DATAFLOW),
    )(*[pltpu.with_memory_space_constraint(a, pltpu.HBM) for a in arrays], *after)
    return (out[0], out[1]), list(out[2:2 + na]), out[-1]


def _exchange_wait(name, sems, in_flight, plan, after=()):
    na = len(in_flight)

    def body(*refs):
        for send, recv in plan(refs[:na], refs[na], refs[na + 1]):
            send.wait_send()
            recv.wait_recv()

    out = pl.pallas_call(
        body, name=name,
        out_shape=tuple(pltpu.HBM(a.shape, a.dtype) for a in in_flight),
        in_specs=[_HBM] * na + [_SEM, _SEM] + [_ANY] * len(after), out_specs=tuple([_HBM] * na),
        input_output_aliases={i: i for i in range(na)},
        compiler_params=pltpu.CompilerParams(has_side_effects=_DATAFLOW),
    )(*in_flight, *sems, *after)
    return list(out)


def _remote(src, dst, send_sems, recv_sems, k, to):
    return pltpu.make_async_remote_copy(src_ref=src, dst_ref=dst, send_sem=send_sems.at[k], recv_sem=recv_sems.at[k],
                                        device_id=to, device_id_type=MESH)


def _half_rows(ref, tid, pj, half):
    h = SHARD_ROWS[tid] // 2
    return ref.at[pl.ds((2 * pj + half) * h, h), :]


def _plan_gather_halves(tids):
    def plan(refs, ss, rs):
        x, y, c, chips = _place()
        pairs = []
        for t, tid in enumerate(tids):
            mine = _half_rows(refs[t], tid, 2 * x + y, c)
            for k, (cx, cy) in enumerate(chips):
                theirs = _half_rows(refs[t], tid, 2 * cx + cy, c)
                pairs.append((_remote(mine, mine, ss, rs, 3 * t + k, (cx, cy, c)),
                              _remote(theirs, theirs, ss, rs, 3 * t + k, (x, y, c))))
        return pairs
    return plan


def _plan_forward_halves(tids):
    def plan(refs, ss, rs):
        x, y, c, chips = _place()
        pairs = []
        for t, tid in enumerate(tids):
            for k, (cx, cy) in enumerate(chips):
                got = _half_rows(refs[t], tid, 2 * cx + cy, c)
                missing = _half_rows(refs[t], tid, 2 * cx + cy, 1 - c)
                pairs.append((_remote(got, got, ss, rs, 3 * t + k, (x, y, 1 - c)),
                              _remote(missing, missing, ss, rs, 3 * t + k, (x, y, 1 - c))))
        return pairs
    return plan


def _plan_to_all(grp):
    def plan(refs, ss, rs):
        x, y, c, chips = _place()
        peers = [(x, y, 1 - c)] + [(cx, cy, pc) for cx, cy in chips for pc in (c, 1 - c)]
        assert len(peers) == N_PEERS
        cps = [_remote(refs[0].at[pl.ds((2 * px + py) * grp.block + pc * grp.half, grp.half), :],
                       refs[1].at[pl.ds(k * grp.half, grp.half), :], ss, rs, k, (px, py, pc))
               for k, (px, py, pc) in enumerate(peers)]
        return [(cp, cp) for cp in cps]
    return plan


def _plan_share_halves(grp, k=0):
    def plan(refs, ss, rs):
        x, y, c, _ = _place()
        mine = refs[0].at[pl.ds(c * grp.half, grp.half), :]
        other = refs[0].at[pl.ds((1 - c) * grp.half, grp.half), :]
        return [(_remote(mine, mine, ss, rs, k, (x, y, 1 - c)), _remote(other, other, ss, rs, k, (x, y, 1 - c)))]
    return plan


def _plan_both(first, n_refs, second):
    def plan(refs, ss, rs):
        return first(refs[:n_refs], ss, rs) + second(refs[n_refs:], ss, rs)
    return plan


def _empty(rows, dtype):
    return lax.empty((rows, 1024), dtype)


def _all_reduce_small(v, shard, grp):
    m_per, n = v.shape
    n_dev = 8

    def body(x_ref, _, out_ref, shard_ref, all_ref, send_sems, recv_sems, local_sem):
        x, y, c, chips = _place()
        (share, shared), = _plan_share_halves(grp, k=N_PEERS)([shard_ref], send_sems, recv_sems)
        me, sibling = (x, y, c), (x, y, 1 - c)

        def rows(px, py, pc):
            return all_ref.at[pl.ds((4 * px + 2 * py + pc) * m_per, m_per), :]

        def copy(k, block, to, src=None):
            return pltpu.make_async_remote_copy(
                src_ref=rows(*block) if src is None else src, dst_ref=rows(*block),
                send_sem=send_sems.at[k], recv_sem=recv_sems.at[k], device_id=to, device_id_type=MESH)

        mine = pltpu.make_async_copy(x_ref, rows(*me), local_sem)
        mine.start()
        first = [copy(0, me, sibling, src=x_ref)]
        first += [copy(1 + k, me, (*chip, c), src=x_ref) for k, chip in enumerate(chips)]
        for cp in first:
            cp.start()
        share.start()
        passed = [copy(4 + k, (*chip, c), sibling) for k, chip in enumerate(chips)]
        for k, chip in enumerate(chips):
            copy(1 + k, (*chip, c), me).wait_recv()
            passed[k].start()
        copy(0, sibling, me).wait_recv()
        for k, chip in enumerate(chips):
            copy(4 + k, (*chip, 1 - c), me).wait_recv()
        for cp in first + passed:
            cp.wait_send()
        mine.wait()
        acc = all_ref[pl.ds(0, m_per), :]
        for d in range(1, n_dev):
            acc = acc + all_ref[pl.ds(d * m_per, m_per), :]
        out_ref[...] = acc
        share.wait_send()
        shared.wait_recv()

    return pl.pallas_call(
        body, name="all_reduce_small_share_" + grp.name,
        in_specs=[_VMEM, _ANY], out_specs=[_VMEM, _ANY],
        out_shape=[jax.ShapeDtypeStruct((m_per, n), F32), jax.ShapeDtypeStruct(shard.shape, shard.dtype)],
        scratch_shapes=[pltpu.VMEM((n_dev * m_per, n), F32), pltpu.SemaphoreType.DMA((N_PEERS + 1,)),
                        pltpu.SemaphoreType.DMA((N_PEERS + 1,)), pltpu.SemaphoreType.DMA],
        input_output_aliases={1: 1},
        compiler_params=pltpu.CompilerParams(has_side_effects=True),
    )(v, shard)


def _tables(S, token):
    pos = jnp.arange(S, dtype=jnp.int32).astype(F32) + token[0, 0]

    def cs(half):
        inv_freq = ROPE_THETA ** (-jnp.arange(half, dtype=F32) / half)
        ang = pos[:, None] * inv_freq[None, :]
        return jnp.cos(ang), jnp.sin(ang)

    c64, s64 = cs(64)
    c32, s32 = cs(32)
    cr = jnp.concatenate([c64, c64], axis=1)
    sr = jnp.concatenate([-s64, s64], axis=1)
    ca = jnp.concatenate([c32, c32, c32, c32], axis=1)
    sa = jnp.concatenate([-s32, s32, -s32, s32], axis=1)
    return cr, sr, ca, sa


def _decays():
    H, C = RET_HEADS, RET_CHUNK
    log_gamma = jnp.log1p(-jnp.exp2(-5.0 - jnp.arange(H, dtype=F32)))
    idx = jnp.arange(C, dtype=F32)
    rel = idx[:, None] - idx[None, :]
    dmat = jnp.where(rel[None] >= 0, jnp.exp(log_gamma[:, None, None] * jnp.maximum(rel, 0.0)[None]), 0.0)
    qd = jnp.exp(log_gamma[:, None] * (idx + 1.0))[:, :, None]
    kd = jnp.exp(log_gamma[:, None] * (C - 1.0 - idx))[:, :, None]
    cd = jnp.exp(log_gamma * C)[:, None, None]
    return dmat, qd, kd, cd


def _local_step(place, x, tgt, ln1_g, b_in, ret_norm_g, attn_sinks, ln2_g, lnf_g, w_in, other_weights, tables, decays,
                after):
    cr, sr, ca, sa = tables
    dmat, qd, kd, cd = decays
    forward_weights, landed_weights = other_weights

    h, rq, rk, rv, rg, aq, ak, av, ga, gb = _proj_fwd(x, ln1_g, w_in, b_in, cr, sr, ca, sa, after)
    ry, states = _ret_fwd(rq, rk, rv, dmat, qd, kd, cd)
    token = forward_weights(ry)
    ay = _attn_fwd(aq, ak, av, attn_sinks, token)
    w_ret, w_attn, w_out, w_gate, w_up, w_down = landed_weights(ay)
    u, merged, x1 = _merge_fwd(x, ry, rg, ay, ga, gb, ret_norm_g, w_ret, w_attn, w_out)
    gate, up, h2, act, dx2, loss, dg3 = _ffn_fwd(x1, tgt, ln2_g, lnf_g, w_gate, w_up, w_down)

    dgate, dup, dx1, dg2 = _ffn_bwd(dx2, gate, up, x1, ln2_g, w_gate, w_up, w_down)
    early = _wgrad([act], dx2, 6, None, "wgrad_down")
    early = _wgrad([dgate], h2, 4, early, "wgrad_gate")
    early = _wgrad([dup], h2, 5, early, "wgrad_up")
    dgates, drg, dba, dbb, day, dry, dgr = _merge_bwd(dx1, ga, gb, u, ay, rg, ry, ret_norm_g,
                                                                w_ret, w_attn, w_out)
    early = _wgrad([merged], dx1, 3, early, "wgrad_out")
    early = _wgrad([u], dba, 1, early, "wgrad_ret")
    e32, e16 = _wgrad([ay], dbb, 2, early, "wgrad_attn")

    sems, flight, token = _exchange_start("to_all_start_early", [e16, _empty(N_PEERS * EARLY.half, BF)], N_PEERS,
                                          _plan_to_all(EARLY))
    daq, dak, dav, dsk = _attn_bwd(aq, ak, av, day, attn_sinks, ca, sa, token)
    d_ret = _ret_bwd(rq, rk, rv, dry, states, dmat, qd, kd, cd, cr, sr, token)
    d_proj = [d_ret, drg, daq, dak, dav, dgates]
    l32, l16 = _wgrad(d_proj, h, 0, None, "wgrad_in")
    _, others = _exchange_wait("to_all_wait_early", sems, flight, _plan_to_all(EARLY), after=(l32,))
    share = _plan_share_halves(EARLY, k=N_PEERS)
    sems, flight, token = _exchange_start(
        "to_all_late_share_early_start", [l16, _empty(N_PEERS * LATE.half, BF), _sum_all(place, e32, others, EARLY)],
        N_PEERS + 1, _plan_both(_plan_to_all(LATE), 2, share))
    grad_x, dg1, db = _proj_bwd(d_proj, x, dx1, ln1_g, w_in, token)
    early_shard, = _exchange_wait("share_wait_early", sems, flight[2:], share, after=(grad_x,))

    def finish_late(after):
        _, others = _exchange_wait("to_all_wait_late", sems, flight[:2], _plan_to_all(LATE), after=after)
        return _sum_all(place, l32, others, LATE)

    d_sinks = dsk.reshape(2, 2, PAIRS, 128).sum(-1).transpose(0, 2, 1).reshape(1, ATTN_Q_HEADS)
    small = dict(loss=loss, ln1_g=dg1, b_in=db, ret_norm_g=dgr, attn_sinks=d_sinks,
                 ln2_g=dg2, lnf_g=dg3)
    return grad_x, early_shard, token, finish_late, small


SMALL_ROWS = 16


def _pack_small(ln1, ret, ln2, lnf, b_in, sinks, loss):
    def body(a_ref, b_ref, c_ref, d_ref, bi_ref, sk_ref, ls_ref, o_ref):
        o_ref[...] = jnp.zeros_like(o_ref)
        for row, ref in enumerate((a_ref, b_ref, c_ref, d_ref)):
            o_ref[row:row + 1, :] = ref[...]
        for k in range(6):
            o_ref[4 + k:5 + k, :] = bi_ref[:, k * 1024:(k + 1) * 1024]
        o_ref[10:11, 0:D_IN - 6144] = bi_ref[:, 6144:D_IN]
        o_ref[11:12, 0:ATTN_Q_HEADS] = sk_ref[...]
        o_ref[12:13, 0:LANES] = ls_ref[...]

    return pl.pallas_call(
        body, name="pack_small_grads", in_specs=[_VMEM] * 7, out_specs=_VMEM,
        out_shape=jax.ShapeDtypeStruct((SMALL_ROWS, 1024), F32),
    )(ln1.reshape(1, 1024), ret.reshape(1, 1024), ln2.reshape(1, 1024), lnf.reshape(1, 1024),
      b_in.reshape(1, D_IN), sinks.reshape(1, ATTN_Q_HEADS), loss.reshape(1, LANES))


def _adamw_whole(ws, g, ms, vs, rows0, name, after):
    n = len(ws)
    r = ws[0].shape[0]
    assert all(w.shape == (r, 1024) for w in ws)

    def body(*refs):
        w_refs, g_ref, m_refs, v_refs = refs[:n], refs[n], refs[n + 1:2 * n + 1], refs[2 * n + 1:3 * n + 1]
        out_refs, g_rows, sems = refs[3 * n + 2:7 * n + 2], refs[-2], refs[-1]
        copies = [pltpu.make_async_copy(g_ref.at[pl.ds(rows0[k], r), :], g_rows.at[k], sems.at[k]) for k in range(n)]
        for cp in copies:
            cp.start()
        for k in range(n):
            copies[k].wait()
            gv = g_rows[k]
            go_ref, d_ref, mo_ref, vo_ref = out_refs[4 * k:4 * k + 4]
            go_ref[...] = gv
            d_ref[...], mo_ref[...], vo_ref[...] = _adamw_math(w_refs[k][...], gv, m_refs[k][...], v_refs[k][...])

    res = pl.pallas_call(
        body, name=name, in_specs=[_VMEM] * n + [_ANY] + [_VMEM] * (2 * n) + [_ANY], out_specs=[_VMEM] * (4 * n),
        out_shape=[jax.ShapeDtypeStruct((r, 1024), F32)] * (4 * n),
        scratch_shapes=[pltpu.VMEM((n, r, 1024), F32), pltpu.SemaphoreType.DMA((n,))],
        compiler_params=pltpu.CompilerParams(vmem_limit_bytes=V7X_VMEM_LIMIT),
    )(*ws, g, *ms, *vs, after)
    return [tuple(res[4 * k:4 * k + 4]) for k in range(n)]


def _adamw_small(pg, ws, ms, vs):
    n_par = len(ws)
    widths = [w.size for w in ws]
    first_row = [0, 1, 2, 3, 4, 11]

    def body(pg_ref, *refs):
        w_refs, m_refs, v_refs = refs[:n_par], refs[n_par:2 * n_par], refs[2 * n_par:3 * n_par]
        out_refs = refs[3 * n_par:]
        for i, width in enumerate(widths):
            for k, c0 in enumerate(range(0, width, 1024)):
                cols = slice(c0, min(c0 + 1024, width))
                gv = pg_ref[first_row[i] + k:first_row[i] + k + 1, 0:cols.stop - c0]
                go_ref, d_ref, mo_ref, vo_ref = out_refs[4 * i:4 * i + 4]
                go_ref[:, cols] = gv
                d_ref[:, cols], mo_ref[:, cols], vo_ref[:, cols] = _adamw_math(
                    w_refs[i][:, cols], gv, m_refs[i][:, cols], v_refs[i][:, cols])

    flat = lambda a: a.reshape(1, a.size)
    res = pl.pallas_call(
        body, name="adamw_small", in_specs=[_VMEM] * (1 + 3 * n_par), out_specs=[_VMEM] * (4 * n_par),
        out_shape=[jax.ShapeDtypeStruct((1, width), F32) for width in widths for _ in range(4)],
    )(pg, *map(flat, ws), *map(flat, ms), *map(flat, vs))
    return [tuple(r.reshape(w.shape) for r in res[4 * i:4 * i + 4]) for i, w in enumerate(ws)]


def kernel(x, ln1_g, w_in, b_in, ret_norm_g, w_ret_out, attn_sinks, w_attn_out, w_out, ln2_g, w_ffn_gate, w_ffn_up, w_ffn_down, lnf_g, loss_target, m_ln1_g, m_w_in, m_b_in, m_ret_norm_g, m_w_ret_out, m_attn_sinks, m_w_attn_out, m_w_out, m_ln2_g, m_w_ffn_gate, m_w_ffn_up, m_w_ffn_down, m_lnf_g, v_ln1_g, v_w_in, v_b_in, v_ret_norm_g, v_w_ret_out, v_attn_sinks, v_w_attn_out, v_w_out, v_ln2_g, v_w_ffn_gate, v_w_ffn_up, v_w_ffn_down, v_lnf_g):
    xi = lax.axis_index("x")
    yi = lax.axis_index("y")
    ci = lax.axis_index("c")
    j = 2 * xi + yi

    place = jnp.stack([j, ci]).astype(jnp.int32)

    names = ["w_in", "w_ret_out", "w_attn_out", "w_out", "w_ffn_gate", "w_ffn_up", "w_ffn_down"]
    transposed = [True, False, False, False, True, True, False]
    view = lambda a, t: a[0].T if t else a[0]
    big_w = [view(a, t) for a, t in zip([w_in, w_ret_out, w_attn_out, w_out, w_ffn_gate, w_ffn_up, w_ffn_down], transposed)]
    big_m = [view(a, t) for a, t in zip([m_w_in, m_w_ret_out, m_w_attn_out, m_w_out, m_w_ffn_gate, m_w_ffn_up, m_w_ffn_down], transposed)]
    big_v = [view(a, t) for a, t in zip([v_w_in, v_w_ret_out, v_w_attn_out, v_w_out, v_w_ffn_gate, v_w_ffn_up, v_w_ffn_down], transposed)]

    staged_in = _stage_weights(place, big_w[:1], "stage_w_in")[0]
    sems, in_flight, token = _exchange_start("gather_in_start", [staged_in], 3, _plan_gather_halves([0]))
    staged = _stage_weights(place, big_w[1:], "stage_others", after=(token,))
    tables = _tables(x.shape[1], token)
    decays = _decays()
    w_in_full = _exchange_wait("gather_in_wait", sems, in_flight, _plan_gather_halves([0]),
                               after=(*staged, *tables, *decays))[0]
    w_in_full = _exchange("gather_in_forward", [w_in_full], 3, _plan_forward_halves([0]))[0]
    rest = list(range(1, 7))
    crossing = _exchange_start("gather_start", staged, 3 * len(rest), _plan_gather_halves(rest), after=(w_in_full,))
    token = crossing[2]
    forwarding = []

    def forward_weights(after):
        arrived = _exchange_wait("gather_wait", crossing[0], crossing[1], _plan_gather_halves(rest), after=(after,))
        forwarding.extend(_exchange_start("forward_start", arrived, 3 * len(rest), _plan_forward_halves(rest)))
        return forwarding[2]

    def landed_weights(after):
        return _exchange_wait("forward_wait", forwarding[0], forwarding[1], _plan_forward_halves(rest), after=(after,))

    other_weights = (forward_weights, landed_weights)

    grad_x, early_shard, token, finish_late, small = _local_step(
        place, x[0], loss_target[0], ln1_g, b_in, ret_norm_g, attn_sinks, ln2_g, lnf_g.reshape(1, -1),
        w_in_full, other_weights, tables, decays, token)

    upd = {}

    def update(t, shard, after):
        res = _adamw(big_w[t], shard, big_m[t], big_v[t], "adamw_" + names[t], g_row0=GROUP_OF[t].off(t), after=after)
        upd[names[t]] = tuple((r.T if transposed[t] else r)[None] for r in res)
        return res[0]

    short = [t for t in EARLY.tensors if SHARD_ROWS[t] == min(SHARD_ROWS)]
    done = [update(t, early_shard, token) for t in EARLY.tensors if t not in short]
    for t, res in zip(short, _adamw_whole([big_w[t] for t in short], early_shard, [big_m[t] for t in short],
                                          [big_v[t] for t in short], [EARLY.off(t) for t in short],
                                          "adamw_short", token)):
        assert not transposed[t]
        upd[names[t]] = tuple(r[None] for r in res)
        done.append(res[0])
    late_half = finish_late((*done, grad_x))
    pg, late_shard = _all_reduce_small(_pack_small(
        small["ln1_g"], small["ret_norm_g"], small["ln2_g"], small["lnf_g"], small["b_in"],
        small["attn_sinks"], small["loss"]), late_half, LATE)
    update(0, late_shard, None)

    small_names = ["ln1_g", "ret_norm_g", "ln2_g", "lnf_g", "b_in", "attn_sinks"]
    upd.update(zip(small_names, _adamw_small(
        pg, [ln1_g, ret_norm_g, ln2_g, lnf_g, b_in, attn_sinks],
        [m_ln1_g, m_ret_norm_g, m_ln2_g, m_lnf_g, m_b_in, m_attn_sinks],
        [v_ln1_g, v_ret_norm_g, v_ln2_g, v_lnf_g, v_b_in, v_attn_sinks])))

    order_w = ["ln1_g", "w_in", "b_in", "ret_norm_g", "w_ret_out", "attn_sinks", "w_attn_out", "w_out",
               "ln2_g", "w_ffn_gate", "w_ffn_up", "w_ffn_down", "lnf_g"]
    outs = [pg[12, 0], grad_x[None]]
    for k in range(4):
        outs += [upd[n][k] for n in order_w]
    return tuple(outs)
```
